```python
import math
import jax, jax.numpy as jnp
from jax import lax
import numpy as np

D_MODEL = 1024
BATCH = 8
SEQ = 4096
DEPTH = 1

N_MEM = 256
MEM_HEADS = 4
MEM_HEAD_DIM = D_MODEL // MEM_HEADS
HEAD_DIM = 64
SWA_HEADS = D_MODEL // HEAD_DIM
SWA_KV_HEADS = 4
WINDOW = 128
BLOCK = 128
REL_BUCKETS = 32
REL_MAX_DIST = 128
CONV_WIDTH = D_MODEL
CONV_K = 3
D_FF = 2816
EPS = 1e-6
NEG_INF = -1e30
POS_PAD = 1 << 30

Q_WIDTH = SWA_HEADS * HEAD_DIM
KV_WIDTH = SWA_KV_HEADS * HEAD_DIM
IN_WIDTHS = (Q_WIDTH, KV_WIDTH, KV_WIDTH, CONV_WIDTH, CONV_WIDTH, CONV_WIDTH, D_MODEL, D_MODEL)
IN_WIDTH = Q_WIDTH + 2 * KV_WIDTH + 3 * CONV_WIDTH + 2 * D_MODEL

kernel_name = "hybrid_gated_swa_shortconv_macaron"


def rms_norm(x, g):
    xf = x.astype(jnp.float32)
    y = xf * lax.rsqrt(jnp.mean(xf * xf, axis=-1, keepdims=True) + EPS)
    return (y * g.astype(jnp.float32)).astype(x.dtype)


def swiglu_ffn(h, w_gu, w_down):
    gate, up = jnp.split(h @ w_gu, 2, axis=-1)
    return (jax.nn.silu(gate) * up) @ w_down


def split_cols(t, widths):
    outs, start = [], 0
    for w in widths:
        outs.append(t[..., start:start + w])
        start += w
    return outs


def t5_causal_bucket(rel):
    n = jnp.maximum(rel, 0)
    max_exact = REL_BUCKETS // 2
    nf = jnp.maximum(n, 1).astype(jnp.float32)
    large = max_exact + (jnp.log(nf / max_exact) / math.log(REL_MAX_DIST / max_exact)
                         * (REL_BUCKETS - max_exact)).astype(jnp.int32)
    large = jnp.minimum(large, REL_BUCKETS - 1)
    return jnp.where(n < max_exact, n, large)


def with_prev_block(t, fill):
    b, s = t.shape[0], t.shape[1]
    nb = s // BLOCK
    tb = t.reshape((b, nb, BLOCK) + t.shape[2:])
    pad = jnp.full_like(tb[:, :1], fill)
    prev = jnp.concatenate([pad, tb[:, :-1]], axis=1)
    return jnp.concatenate([prev, tb], axis=2)


def sliding_window_gqa(q, k, v, positions, rel_bias, sinks):
    b, s, _ = q.shape
    nb = s // BLOCK
    grp = SWA_HEADS // SWA_KV_HEADS
    qb = q.reshape(b, nb, BLOCK, SWA_KV_HEADS, grp, HEAD_DIM)
    kb = with_prev_block(k.reshape(b, s, SWA_KV_HEADS, HEAD_DIM), 0)
    vb = with_prev_block(v.reshape(b, s, SWA_KV_HEADS, HEAD_DIM), 0)
    pq = positions.reshape(b, nb, BLOCK)
    pk = with_prev_block(positions, POS_PAD)
    rel = pq[:, :, :, None] - pk[:, :, None, :]
    visible = (rel >= 0) & (rel < WINDOW)
    bias = jnp.moveaxis(rel_bias[t5_causal_bucket(rel)], -1, 2)
    bias = bias.reshape(b, nb, SWA_KV_HEADS, grp, BLOCK, 2 * BLOCK).astype(jnp.float32)
    logits = jnp.einsum('bnqhgd,bnkhd->bnhgqk', qb, kb).astype(jnp.float32) * (HEAD_DIM ** -0.5)
    logits = jnp.where(visible[:, :, None, None], logits + bias, NEG_INF)
    sink = jnp.broadcast_to(sinks.astype(jnp.float32).reshape(1, 1, SWA_KV_HEADS, grp, 1, 1),
                            logits.shape[:-1] + (1,))
    probs = jax.nn.softmax(jnp.concatenate([logits, sink], axis=-1), axis=-1)[..., :-1]
    out = jnp.einsum('bnhgqk,bnkhd->bnqhgd', probs.astype(vb.dtype), vb)
    return out.reshape(b, s, Q_WIDTH)


def short_conv(u, w_conv):
    s = u.shape[1]
    up = jnp.pad(u, ((0, 0), (CONV_K - 1, 0), (0, 0)))
    return sum(w_conv[j] * up[:, j:j + s] for j in range(CONV_K))


def memory_cross_attention(h, mem_h, w_q, w_kv, w_o):
    b, s, _ = h.shape
    m = mem_h.shape[1]
    q = (h @ w_q).reshape(b, s, MEM_HEADS, MEM_HEAD_DIM)
    k, v = jnp.split(mem_h @ w_kv, 2, axis=-1)
    k = k.reshape(b, m, MEM_HEADS, MEM_HEAD_DIM)
    v = v.reshape(b, m, MEM_HEADS, MEM_HEAD_DIM)
    logits = jnp.einsum('bshd,bmhd->bhsm', q, k).astype(jnp.float32) * (MEM_HEAD_DIM ** -0.5)
    probs = jax.nn.softmax(logits, axis=-1)
    o = jnp.einsum('bhsm,bmhd->bshd', probs.astype(v.dtype), v).reshape(b, s, D_MODEL)
    return o @ w_o


def _fwd_setup_inputs(seed: int = 0) -> dict:
    key = jax.random.key(seed)
    ks = jax.random.split(key, 24)
    f32 = jnp.float32

    def dense(k, shape, fan_in):
        return jax.random.normal(k, shape, f32) * (fan_in ** -0.5)

    def gain(k, shape):
        return 1.0 + 0.05 * jax.random.normal(k, shape, f32)

    L = DEPTH
    x = jax.random.normal(ks[0], (BATCH, SEQ, D_MODEL), f32)
    mem = jax.random.normal(ks[1], (BATCH, N_MEM, D_MODEL), f32)
    offsets = jax.random.randint(ks[2], (BATCH, 1), 0, 1024, dtype=jnp.int32)
    positions = offsets + jnp.arange(SEQ, dtype=jnp.int32)[None, :]
    return {
        "x": x,
        "mem": mem,
        "positions": positions,
        "rel_bias": 0.5 * jax.random.normal(ks[3], (REL_BUCKETS, SWA_HEADS), f32),
        "ffn1_norm": gain(ks[4], (L, D_MODEL)),
        "ffn1_w_gu": dense(ks[5], (L, D_MODEL, 2 * D_FF), D_MODEL),
        "ffn1_w_down": dense(ks[6], (L, D_FF, D_MODEL), D_FF),
        "mix_norm": gain(ks[7], (L, D_MODEL)),
        "w_in": dense(ks[8], (L, D_MODEL, IN_WIDTH), D_MODEL),
        "sinks": 0.5 * jax.random.normal(ks[9], (L, SWA_HEADS), f32),
        "conv_w": dense(ks[10], (L, CONV_K, CONV_WIDTH), CONV_K),
        "w_out": dense(ks[11], (L, D_MODEL, D_MODEL), D_MODEL),
        "xattn_norm": gain(ks[12], (L, D_MODEL)),
        "mem_norm": gain(ks[13], (L, D_MODEL)),
        "xattn_wq": dense(ks[14], (L, D_MODEL, D_MODEL), D_MODEL),
        "xattn_wkv": dense(ks[15], (L, D_MODEL, 2 * D_MODEL), D_MODEL),
        "xattn_wo": dense(ks[16], (L, D_MODEL, D_MODEL), D_MODEL),
        "ffn2_norm": gain(ks[17], (L, D_MODEL)),
        "ffn2_w_gu": dense(ks[18], (L, D_MODEL, 2 * D_FF), D_MODEL),
        "ffn2_w_down": dense(ks[19], (L, D_FF, D_MODEL), D_FF),
        "final_norm": gain(ks[20], (D_MODEL,)),
    }


def _fwd_reference(x, mem, positions, rel_bias, ffn1_norm, ffn1_w_gu, ffn1_w_down, mix_norm, w_in,
              sinks, conv_w, w_out, xattn_norm, mem_norm, xattn_wq, xattn_wkv, xattn_wo,
              ffn2_norm, ffn2_w_gu, ffn2_w_down, final_norm):
    for l in range(DEPTH):
        x = x + 0.5 * swiglu_ffn(rms_norm(x, ffn1_norm[l]), ffn1_w_gu[l], ffn1_w_down[l])
        h = rms_norm(x, mix_norm[l])
        q, k, v, c_pre, b_post, u, g_attn, g_conv = split_cols(h @ w_in[l], IN_WIDTHS)
        attn = sliding_window_gqa(q, k, v, positions, rel_bias, sinks[l])
        conv = b_post * short_conv(c_pre * u, conv_w[l])
        merged = jax.nn.sigmoid(g_attn) * attn + jax.nn.sigmoid(g_conv) * conv
        x = x + merged @ w_out[l]
        x = x + memory_cross_attention(rms_norm(x, xattn_norm[l]), rms_norm(mem, mem_norm[l]),
                                       xattn_wq[l], xattn_wkv[l], xattn_wo[l])
        x = x + 0.5 * swiglu_ffn(rms_norm(x, ffn2_norm[l]), ffn2_w_gu[l], ffn2_w_down[l])
    return rms_norm(x, final_norm)


import jax as _jax
import jax.numpy as _jnp

TWIN_FORMAT = 'train_step'
FWD_PARAMS = ['x', 'mem', 'positions', 'rel_bias', 'ffn1_norm', 'ffn1_w_gu', 'ffn1_w_down', 'mix_norm', 'w_in', 'sinks', 'conv_w', 'w_out', 'xattn_norm', 'mem_norm', 'xattn_wq', 'xattn_wkv', 'xattn_wo', 'ffn2_norm', 'ffn2_w_gu', 'ffn2_w_down', 'final_norm']
TWIN_WEIGHTS = ['rel_bias', 'ffn1_norm', 'ffn1_w_gu', 'ffn1_w_down', 'mix_norm', 'w_in', 'sinks', 'conv_w', 'w_out', 'xattn_norm', 'mem_norm', 'xattn_wq', 'xattn_wkv', 'xattn_wo', 'ffn2_norm', 'ffn2_w_gu', 'ffn2_w_down', 'final_norm']
TWIN_DIFF_INPUT = 'x'
TWIN_INPUTS = ['x', 'mem', 'positions', 'rel_bias', 'ffn1_norm', 'ffn1_w_gu', 'ffn1_w_down', 'mix_norm', 'w_in', 'sinks', 'conv_w', 'w_out', 'xattn_norm', 'mem_norm', 'xattn_wq', 'xattn_wkv', 'xattn_wo', 'ffn2_norm', 'ffn2_w_gu', 'ffn2_w_down', 'final_norm', 'loss_target', 'm_rel_bias', 'm_ffn1_norm', 'm_ffn1_w_gu', 'm_ffn1_w_down', 'm_mix_norm', 'm_w_in', 'm_sinks', 'm_conv_w', 'm_w_out', 'm_xattn_norm', 'm_mem_norm', 'm_xattn_wq', 'm_xattn_wkv', 'm_xattn_wo', 'm_ffn2_norm', 'm_ffn2_w_gu', 'm_ffn2_w_down', 'm_final_norm', 'v_rel_bias', 'v_ffn1_norm', 'v_ffn1_w_gu', 'v_ffn1_w_down', 'v_mix_norm', 'v_w_in', 'v_sinks', 'v_conv_w', 'v_w_out', 'v_xattn_norm', 'v_mem_norm', 'v_xattn_wq', 'v_xattn_wkv', 'v_xattn_wo', 'v_ffn2_norm', 'v_ffn2_w_gu', 'v_ffn2_w_down', 'v_final_norm']
TWIN_OUTPUTS = ['loss', 'grad_x', 'grad_rel_bias', 'grad_ffn1_norm', 'grad_ffn1_w_gu', 'grad_ffn1_w_down', 'grad_mix_norm', 'grad_w_in', 'grad_sinks', 'grad_conv_w', 'grad_w_out', 'grad_xattn_norm', 'grad_mem_norm', 'grad_xattn_wq', 'grad_xattn_wkv', 'grad_xattn_wo', 'grad_ffn2_norm', 'grad_ffn2_w_gu', 'grad_ffn2_w_down', 'grad_final_norm', 'delta_rel_bias', 'delta_ffn1_norm', 'delta_ffn1_w_gu', 'delta_ffn1_w_down', 'delta_mix_norm', 'delta_w_in', 'delta_sinks', 'delta_conv_w', 'delta_w_out', 'delta_xattn_norm', 'delta_mem_norm', 'delta_xattn_wq', 'delta_xattn_wkv', 'delta_xattn_wo', 'delta_ffn2_norm', 'delta_ffn2_w_gu', 'delta_ffn2_w_down', 'delta_final_norm', 'new_m_rel_bias', 'new_m_ffn1_norm', 'new_m_ffn1_w_gu', 'new_m_ffn1_w_down', 'new_m_mix_norm', 'new_m_w_in', 'new_m_sinks', 'new_m_conv_w', 'new_m_w_out', 'new_m_xattn_norm', 'new_m_mem_norm', 'new_m_xattn_wq', 'new_m_xattn_wkv', 'new_m_xattn_wo', 'new_m_ffn2_norm', 'new_m_ffn2_w_gu', 'new_m_ffn2_w_down', 'new_m_final_norm', 'new_v_rel_bias', 'new_v_ffn1_norm', 'new_v_ffn1_w_gu', 'new_v_ffn1_w_down', 'new_v_mix_norm', 'new_v_w_in', 'new_v_sinks', 'new_v_conv_w', 'new_v_w_out', 'new_v_xattn_norm', 'new_v_mem_norm', 'new_v_xattn_wq', 'new_v_xattn_wkv', 'new_v_xattn_wo', 'new_v_ffn2_norm', 'new_v_ffn2_w_gu', 'new_v_ffn2_w_down', 'new_v_final_norm']
TWIN_LEAF_KINDS = {'loss': 'loss', 'grad_x': 'grad_x', 'grad_rel_bias': 'grad_w', 'grad_ffn1_norm': 'grad_w', 'grad_ffn1_w_gu': 'grad_w', 'grad_ffn1_w_down': 'grad_w', 'grad_mix_norm': 'grad_w', 'grad_w_in': 'grad_w', 'grad_sinks': 'grad_w', 'grad_conv_w': 'grad_w', 'grad_w_out': 'grad_w', 'grad_xattn_norm': 'grad_w', 'grad_mem_norm': 'grad_w', 'grad_xattn_wq': 'grad_w', 'grad_xattn_wkv': 'grad_w', 'grad_xattn_wo': 'grad_w', 'grad_ffn2_norm': 'grad_w', 'grad_ffn2_w_gu': 'grad_w', 'grad_ffn2_w_down': 'grad_w', 'grad_final_norm': 'grad_w', 'delta_rel_bias': 'delta_w', 'delta_ffn1_norm': 'delta_w', 'delta_ffn1_w_gu': 'delta_w', 'delta_ffn1_w_down': 'delta_w', 'delta_mix_norm': 'delta_w', 'delta_w_in': 'delta_w', 'delta_sinks': 'delta_w', 'delta_conv_w': 'delta_w', 'delta_w_out': 'delta_w', 'delta_xattn_norm': 'delta_w', 'delta_mem_norm': 'delta_w', 'delta_xattn_wq': 'delta_w', 'delta_xattn_wkv': 'delta_w', 'delta_xattn_wo': 'delta_w', 'delta_ffn2_norm': 'delta_w', 'delta_ffn2_w_gu': 'delta_w', 'delta_ffn2_w_down': 'delta_w', 'delta_final_norm': 'delta_w', 'new_m_rel_bias': 'new_m', 'new_m_ffn1_norm': 'new_m', 'new_m_ffn1_w_gu': 'new_m', 'new_m_ffn1_w_down': 'new_m', 'new_m_mix_norm': 'new_m', 'new_m_w_in': 'new_m', 'new_m_sinks': 'new_m', 'new_m_conv_w': 'new_m', 'new_m_w_out': 'new_m', 'new_m_xattn_norm': 'new_m', 'new_m_mem_norm': 'new_m', 'new_m_xattn_wq': 'new_m', 'new_m_xattn_wkv': 'new_m', 'new_m_xattn_wo': 'new_m', 'new_m_ffn2_norm': 'new_m', 'new_m_ffn2_w_gu': 'new_m', 'new_m_ffn2_w_down': 'new_m', 'new_m_final_norm': 'new_m', 'new_v_rel_bias': 'new_v', 'new_v_ffn1_norm': 'new_v', 'new_v_ffn1_w_gu': 'new_v', 'new_v_ffn1_w_down': 'new_v', 'new_v_mix_norm': 'new_v', 'new_v_w_in': 'new_v', 'new_v_sinks': 'new_v', 'new_v_conv_w': 'new_v', 'new_v_w_out': 'new_v', 'new_v_xattn_norm': 'new_v', 'new_v_mem_norm': 'new_v', 'new_v_xattn_wq': 'new_v', 'new_v_xattn_wkv': 'new_v', 'new_v_xattn_wo': 'new_v', 'new_v_ffn2_norm': 'new_v', 'new_v_ffn2_w_gu': 'new_v', 'new_v_ffn2_w_down': 'new_v', 'new_v_final_norm': 'new_v'}


def _forward(args):
    return _fwd_reference(*[args[k] for k in FWD_PARAMS])


def _output_shape():
    def fwd():
        inp = _fwd_setup_inputs(0)
        return _fwd_reference(*[inp[k] for k in FWD_PARAMS])
    out = _jax.eval_shape(fwd)
    return out.shape, out.dtype

N_MICROBATCH = 1
ADAM_LR = 0.001
ADAM_B1 = 0.9
ADAM_B2 = 0.999
ADAM_EPS = 1e-08
ADAM_WD = 0.01
ADAM_STEP = 10
PER_EXAMPLE_BATCH_AXIS = {'x': 0, 'mem': 0, 'positions': 0, 'loss_target': 0}
SHARED_INPUTS = []
_WEIGHT_DTYPES = {'rel_bias': _jnp.float32, 'ffn1_norm': _jnp.float32, 'ffn1_w_gu': _jnp.float32, 'ffn1_w_down': _jnp.float32, 'mix_norm': _jnp.float32, 'w_in': _jnp.float32, 'sinks': _jnp.float32, 'conv_w': _jnp.float32, 'w_out': _jnp.float32, 'xattn_norm': _jnp.float32, 'mem_norm': _jnp.float32, 'xattn_wq': _jnp.float32, 'xattn_wkv': _jnp.float32, 'xattn_wo': _jnp.float32, 'ffn2_norm': _jnp.float32, 'ffn2_w_gu': _jnp.float32, 'ffn2_w_down': _jnp.float32, 'final_norm': _jnp.float32}
MOMENT_SCALE = {'rel_bias': 2.098216e-02, 'ffn1_norm': 9.030706e-02, 'ffn1_w_gu': 3.887136e-02, 'ffn1_w_down': 6.355669e-02, 'mix_norm': 1.554989e-01, 'w_in': 5.925884e-02, 'sinks': 1.375365e-02, 'conv_w': 8.758607e-02, 'w_out': 8.627617e-02, 'xattn_norm': 1.721445e-02, 'mem_norm': 2.458842e-02, 'xattn_wq': 1.668125e-02, 'xattn_wkv': 1.679547e-02, 'xattn_wo': 1.697147e-02, 'ffn2_norm': 6.317956e-02, 'ffn2_w_gu': 2.707644e-02, 'ffn2_w_down': 4.433700e-02, 'final_norm': 3.211567e+01}


def _to_microbatches(a, axis):
    t = _jnp.moveaxis(a, axis, 0)
    t = t.reshape((N_MICROBATCH, t.shape[0] // N_MICROBATCH) + t.shape[1:])
    return _jnp.moveaxis(t, 1, axis + 1)


def setup_inputs(seed: int = 0) -> dict:
    inp = _fwd_setup_inputs(seed)
    key = _jax.random.fold_in(_jax.random.key(seed), 7919)
    shape, _ = _output_shape()
    out = dict(inp)
    out["loss_target"] = _jax.random.normal(_jax.random.fold_in(key, 0), shape, _jnp.float32)
    for i, name in enumerate(TWIN_WEIGHTS):
        w = inp[name].astype(_jnp.float32)
        if MOMENT_SCALE is None:
            s = _jnp.sqrt(_jnp.mean(_jnp.square(w)) + 1e-30)
        else:
            s = MOMENT_SCALE[name]
        km, kv = _jax.random.split(_jax.random.fold_in(key, i + 1))
        out[name] = w
        out["m_" + name] = s * _jax.random.normal(km, w.shape, _jnp.float32)
        out["v_" + name] = (s * s) * _jax.random.uniform(kv, w.shape, _jnp.float32, 0.5, 1.5)
    if N_MICROBATCH > 1:
        for name, axis in PER_EXAMPLE_BATCH_AXIS.items():
            out[name] = _to_microbatches(out[name], axis)
    return {'x': out['x'], 'mem': out['mem'], 'positions': out['positions'], 'rel_bias': out['rel_bias'], 'ffn1_norm': out['ffn1_norm'], 'ffn1_w_gu': out['ffn1_w_gu'], 'ffn1_w_down': out['ffn1_w_down'], 'mix_norm': out['mix_norm'], 'w_in': out['w_in'], 'sinks': out['sinks'], 'conv_w': out['conv_w'], 'w_out': out['w_out'], 'xattn_norm': out['xattn_norm'], 'mem_norm': out['mem_norm'], 'xattn_wq': out['xattn_wq'], 'xattn_wkv': out['xattn_wkv'], 'xattn_wo': out['xattn_wo'], 'ffn2_norm': out['ffn2_norm'], 'ffn2_w_gu': out['ffn2_w_gu'], 'ffn2_w_down': out['ffn2_w_down'], 'final_norm': out['final_norm'], 'loss_target': out['loss_target'], 'm_rel_bias': out['m_rel_bias'], 'm_ffn1_norm': out['m_ffn1_norm'], 'm_ffn1_w_gu': out['m_ffn1_w_gu'], 'm_ffn1_w_down': out['m_ffn1_w_down'], 'm_mix_norm': out['m_mix_norm'], 'm_w_in': out['m_w_in'], 'm_sinks': out['m_sinks'], 'm_conv_w': out['m_conv_w'], 'm_w_out': out['m_w_out'], 'm_xattn_norm': out['m_xattn_norm'], 'm_mem_norm': out['m_mem_norm'], 'm_xattn_wq': out['m_xattn_wq'], 'm_xattn_wkv': out['m_xattn_wkv'], 'm_xattn_wo': out['m_xattn_wo'], 'm_ffn2_norm': out['m_ffn2_norm'], 'm_ffn2_w_gu': out['m_ffn2_w_gu'], 'm_ffn2_w_down': out['m_ffn2_w_down'], 'm_final_norm': out['m_final_norm'], 'v_rel_bias': out['v_rel_bias'], 'v_ffn1_norm': out['v_ffn1_norm'], 'v_ffn1_w_gu': out['v_ffn1_w_gu'], 'v_ffn1_w_down': out['v_ffn1_w_down'], 'v_mix_norm': out['v_mix_norm'], 'v_w_in': out['v_w_in'], 'v_sinks': out['v_sinks'], 'v_conv_w': out['v_conv_w'], 'v_w_out': out['v_w_out'], 'v_xattn_norm': out['v_xattn_norm'], 'v_mem_norm': out['v_mem_norm'], 'v_xattn_wq': out['v_xattn_wq'], 'v_xattn_wkv': out['v_xattn_wkv'], 'v_xattn_wo': out['v_xattn_wo'], 'v_ffn2_norm': out['v_ffn2_norm'], 'v_ffn2_w_gu': out['v_ffn2_w_gu'], 'v_ffn2_w_down': out['v_ffn2_w_down'], 'v_final_norm': out['v_final_norm']}


def _loss(weights, diff, rest, loss_target):
    with _jax.named_scope("forward"):
        args = {**rest, TWIN_DIFF_INPUT: diff, **{k: w.astype(_WEIGHT_DTYPES[k]) for k, w in weights.items()}}
        y = _forward(args)
    with _jax.named_scope("loss_head"):
        err = _jnp.square(y.astype(_jnp.float32) - loss_target)
        return 0.5 * _jnp.sum(_jnp.mean(err, axis=-1)) if err.ndim else 0.5 * err


def _adamw(w, g, m, v):
    m = ADAM_B1 * m + (1.0 - ADAM_B1) * g
    v = ADAM_B2 * v + (1.0 - ADAM_B2) * _jnp.square(g)
    m_hat = m / (1.0 - ADAM_B1 ** ADAM_STEP)
    v_hat = v / (1.0 - ADAM_B2 ** ADAM_STEP)
    delta = -ADAM_LR * (m_hat / (_jnp.sqrt(v_hat) + ADAM_EPS) + ADAM_WD * w)
    return delta, m, v


def reference(x, mem, positions, rel_bias, ffn1_norm, ffn1_w_gu, ffn1_w_down, mix_norm, w_in, sinks, conv_w, w_out, xattn_norm, mem_norm, xattn_wq, xattn_wkv, xattn_wo, ffn2_norm, ffn2_w_gu, ffn2_w_down, final_norm, loss_target, m_rel_bias, m_ffn1_norm, m_ffn1_w_gu, m_ffn1_w_down, m_mix_norm, m_w_in, m_sinks, m_conv_w, m_w_out, m_xattn_norm, m_mem_norm, m_xattn_wq, m_xattn_wkv, m_xattn_wo, m_ffn2_norm, m_ffn2_w_gu, m_ffn2_w_down, m_final_norm, v_rel_bias, v_ffn1_norm, v_ffn1_w_gu, v_ffn1_w_down, v_mix_norm, v_w_in, v_sinks, v_conv_w, v_w_out, v_xattn_norm, v_mem_norm, v_xattn_wq, v_xattn_wkv, v_xattn_wo, v_ffn2_norm, v_ffn2_w_gu, v_ffn2_w_down, v_final_norm):
    given = dict(x=x, mem=mem, positions=positions, rel_bias=rel_bias, ffn1_norm=ffn1_norm, ffn1_w_gu=ffn1_w_gu, ffn1_w_down=ffn1_w_down, mix_norm=mix_norm, w_in=w_in, sinks=sinks, conv_w=conv_w, w_out=w_out, xattn_norm=xattn_norm, mem_norm=mem_norm, xattn_wq=xattn_wq, xattn_wkv=xattn_wkv, xattn_wo=xattn_wo, ffn2_norm=ffn2_norm, ffn2_w_gu=ffn2_w_gu, ffn2_w_down=ffn2_w_down, final_norm=final_norm, loss_target=loss_target, m_rel_bias=m_rel_bias, m_ffn1_norm=m_ffn1_norm, m_ffn1_w_gu=m_ffn1_w_gu, m_ffn1_w_down=m_ffn1_w_down, m_mix_norm=m_mix_norm, m_w_in=m_w_in, m_sinks=m_sinks, m_conv_w=m_conv_w, m_w_out=m_w_out, m_xattn_norm=m_xattn_norm, m_mem_norm=m_mem_norm, m_xattn_wq=m_xattn_wq, m_xattn_wkv=m_xattn_wkv, m_xattn_wo=m_xattn_wo, m_ffn2_norm=m_ffn2_norm, m_ffn2_w_gu=m_ffn2_w_gu, m_ffn2_w_down=m_ffn2_w_down, m_final_norm=m_final_norm, v_rel_bias=v_rel_bias, v_ffn1_norm=v_ffn1_norm, v_ffn1_w_gu=v_ffn1_w_gu, v_ffn1_w_down=v_ffn1_w_down, v_mix_norm=v_mix_norm, v_w_in=v_w_in, v_sinks=v_sinks, v_conv_w=v_conv_w, v_w_out=v_w_out, v_xattn_norm=v_xattn_norm, v_mem_norm=v_mem_norm, v_xattn_wq=v_xattn_wq, v_xattn_wkv=v_xattn_wkv, v_xattn_wo=v_xattn_wo, v_ffn2_norm=v_ffn2_norm, v_ffn2_w_gu=v_ffn2_w_gu, v_ffn2_w_down=v_ffn2_w_down, v_final_norm=v_final_norm)
    weights = {n: given[n] for n in TWIN_WEIGHTS}
    shared = {n: given[n] for n in SHARED_INPUTS}
    per_example = {n: given[n] for n in ['x', 'mem', 'positions']}
    grad_fn = _jax.value_and_grad(_loss, argnums=(0, 1))

    def one_microbatch(ex, loss_target):
        ex = dict(ex)
        diff = ex.pop(TWIN_DIFF_INPUT)
        return grad_fn(weights, diff, {**shared, **ex}, loss_target)

    if N_MICROBATCH == 1:
        loss, (grad_w, grad_x) = one_microbatch(per_example, given["loss_target"])
    else:
        def body(carry, xs):
            loss_sum, grad_sum = carry
            l_k, (gw_k, gx_k) = one_microbatch(xs[0], xs[1])
            with _jax.named_scope("update"):
                return (loss_sum + l_k, _jax.tree.map(_jnp.add, grad_sum, gw_k)), gx_k

        init = (_jnp.zeros((), _jnp.float32), _jax.tree.map(_jnp.zeros_like, weights))
        (loss, grad_w), grad_x = _jax.lax.scan(body, init, (per_example, given["loss_target"]))
    with _jax.named_scope("update"):
        delta_w, new_m, new_v = {}, {}, {}
        for n in TWIN_WEIGHTS:
            delta_w[n], new_m[n], new_v[n] = _adamw(weights[n], grad_w[n], given["m_" + n], given["v_" + n])
    return (loss, grad_x, *[grad_w[n] for n in TWIN_WEIGHTS], *[delta_w[n] for n in TWIN_WEIGHTS],
            *[new_m[n] for n in TWIN_WEIGHTS], *[new_v[n] for n in TWIN_WEIGHTS])
```

```python
import functools
import math

import jax
import jax.numpy as jnp
from jax import lax
from jax.experimental import pallas as pl
from jax.experimental.pallas import tpu as pltpu

BF = jnp.bfloat16
F32 = jnp.float32
I32 = jnp.int32
S = jax.ShapeDtypeStruct

EPS = 1e-6
NEG = -1e30
POS_PAD = 1 << 30
WINDOW = 128
BLOCK = 128
HEAD_DIM = 64
SWA_HEADS = 16
SWA_KV_HEADS = 4
SWA_GROUP = SWA_HEADS // SWA_KV_HEADS
MEM_HEADS = 4
REL_BUCKETS = 32
REL_MAX_DIST = 128
ADAM_LR = 0.001
ADAM_B1 = 0.9
ADAM_B2 = 0.999
ADAM_EPS = 1e-08
ADAM_WD = 0.01
ADAM_STEP = 10

V7X_VMEM_LIMIT_BYTES = 56 * 1024 * 1024
MESH = pl.DeviceIdType.MESH
ANY = pl.BlockSpec(memory_space=pl.ANY)
VMEM_SPEC = pl.BlockSpec(memory_space=pltpu.VMEM)
SMEM_SPEC = pl.BlockSpec(memory_space=pltpu.SMEM)


def _params(sem=None):
    return pltpu.CompilerParams(dimension_semantics=sem, vmem_limit_bytes=V7X_VMEM_LIMIT_BYTES)


def _resident(shape):
    nd = len(shape)
    return pl.BlockSpec(shape, lambda *_: (0,) * nd, pipeline_mode=pl.Buffered(1))


def _acc_spec(shape):
    nd = len(shape)
    return pl.BlockSpec(shape, lambda *_: (0,) * nd)


def _nn(a, b):
    return jnp.dot(a, b, preferred_element_type=F32)


def _nt(a, b):
    return lax.dot_general(a, b, (((1,), (1,)), ((), ())), preferred_element_type=F32)


def _tn(a, b):
    return lax.dot_general(a, b, (((0,), (0,)), ((), ())), preferred_element_type=F32)


def _sigmoid(v):
    return 1.0 / (1.0 + jnp.exp(-v))


def _rms(x):
    r = lax.rsqrt(jnp.mean(x * x, axis=-1, keepdims=True) + EPS)
    return x * r, r


def _rms_bwd(dh, n, r, g):
    dn = dh * g
    dx = r * (dn - n * jnp.mean(dn * n, axis=-1, keepdims=True))
    return dx, jnp.sum(dh * n, axis=0, keepdims=True)


def _ffn_fwd(x, gn, wgu, wd, name):
    t, d = x.shape
    f = wd.shape[0]
    tm, fc = 256, 1408

    def body(x_ref, gn_ref, wgu_ref, wd_ref, xo_ref, g_ref, u_ref):
        xv = x_ref[...]
        n, _ = _rms(xv)
        h = (n * gn_ref[...]).astype(BF)
        acc = jnp.zeros((tm, d), F32)
        for c0 in range(0, f, fc):
            g = _nn(h, wgu_ref[:, c0:c0 + fc])
            u = _nn(h, wgu_ref[:, f + c0:f + c0 + fc])
            g_ref[:, c0:c0 + fc] = g.astype(BF)
            u_ref[:, c0:c0 + fc] = u.astype(BF)
            a = (g * _sigmoid(g)) * u
            acc = acc + _nn(a.astype(BF), wd_ref[c0:c0 + fc, :])
        xo_ref[...] = xv + 0.5 * acc

    return pl.pallas_call(
        body, name=name, grid=(t // tm,),
        out_shape=(S((t, d), F32), S((t, f), BF), S((t, f), BF)),
        in_specs=[pl.BlockSpec((tm, d), lambda i: (i, 0)), _resident((1, d)), _resident(wgu.shape), _resident(wd.shape)],
        out_specs=(pl.BlockSpec((tm, d), lambda i: (i, 0)), pl.BlockSpec((tm, f), lambda i: (i, 0)),
                   pl.BlockSpec((tm, f), lambda i: (i, 0))),
        compiler_params=_params(("parallel",)),
    )(x, gn, wgu, wd)


def _mix_proj(x, gn, w_in):
    t, d = x.shape
    tm = 256
    nqkv = 1536
    ne = w_in.shape[1] - nqkv

    def body(x_ref, gn_ref, w_ref, qkv_ref, e_ref):
        n, _ = _rms(x_ref[...])
        h = (n * gn_ref[...]).astype(BF)
        qkv_ref[...] = _nn(h, w_ref[:, 0:nqkv]).astype(BF)
        for c0 in range(0, ne, 1024):
            e_ref[:, c0:c0 + 1024] = _nn(h, w_ref[:, nqkv + c0:nqkv + c0 + 1024])

    return pl.pallas_call(
        body, name="mix_proj", grid=(t // tm,),
        out_shape=(S((t, nqkv), BF), S((t, ne), F32)),
        in_specs=[pl.BlockSpec((tm, d), lambda i: (i, 0)), _resident((1, d)), _resident(w_in.shape)],
        out_specs=(pl.BlockSpec((tm, nqkv), lambda i: (i, 0)), pl.BlockSpec((tm, ne), lambda i: (i, 0))),
        compiler_params=_params(("parallel",)),
    )(x, gn, w_in)


def _t5_bucket(rel):
    n = jnp.maximum(rel, 0)
    max_exact = REL_BUCKETS // 2
    nf = jnp.maximum(n, 1).astype(F32)
    large = max_exact + (jnp.log(nf / max_exact) / math.log(REL_MAX_DIST / max_exact)
                         * (REL_BUCKETS - max_exact)).astype(I32)
    large = jnp.minimum(large, REL_BUCKETS - 1)
    return jnp.where(n < max_exact, n, large)


def _block_rel():
    i = lax.broadcasted_iota(I32, (BLOCK, 2 * BLOCK), 0)
    j = lax.broadcasted_iota(I32, (BLOCK, 2 * BLOCK), 1)
    return i + BLOCK - j


def _bias_build(rel_bias):
    def body(rb_ref, o_ref):
        bucket = _t5_bucket(_block_rel())
        for h in range(SWA_HEADS):
            acc = jnp.zeros((BLOCK, 2 * BLOCK), F32)
            for b in range(REL_BUCKETS):
                acc = jnp.where(bucket == b, rb_ref[b, h], acc)
            o_ref[h] = acc

    return pl.pallas_call(
        body, name="bias_build", out_shape=S((SWA_HEADS, BLOCK, 2 * BLOCK), F32),
        in_specs=[SMEM_SPEC], out_specs=VMEM_SPEC,
    )(rel_bias)


def _swa_visible(b, pq_ref, pkp_ref, pkc_ref):
    pk = jnp.concatenate([pkp_ref[...], pkc_ref[...]], axis=1)
    col = lax.broadcasted_iota(I32, (1, 2 * BLOCK), 1)
    pk = jnp.where(jnp.logical_and(b == 0, col < BLOCK), POS_PAD, pk)
    rel = pq_ref[...] - pk
    return jnp.logical_and(rel >= 0, rel < WINDOW)


def _swa_probs(qh, kh, vis, bias, sink):
    s = _nt(qh, kh) * (HEAD_DIM ** -0.5)
    s = jnp.where(vis, s + bias, NEG)
    m = jnp.maximum(jnp.max(s, axis=-1, keepdims=True), sink)
    p = jnp.exp(s - m)
    ps = jnp.exp(sink - m)
    inv = 1.0 / (jnp.sum(p, axis=-1, keepdims=True) + ps)
    return p * inv, ps * inv


def _swa_fwd(qkv, pos_col, pos_row, bias_t, sinks):
    t = qkv.shape[0]
    nb = t // BLOCK
    qw = SWA_HEADS * HEAD_DIM
    kw = SWA_KV_HEADS * HEAD_DIM

    def body(q_ref, kp_ref, kc_ref, vp_ref, vc_ref, pq_ref, pkp_ref, pkc_ref, bias_ref, sink_ref, o_ref):
        b = pl.program_id(0)
        vis = _swa_visible(b, pq_ref, pkp_ref, pkc_ref)
        k2 = jnp.concatenate([kp_ref[...], kc_ref[...]], axis=0)
        v2 = jnp.concatenate([vp_ref[...], vc_ref[...]], axis=0)
        for hk in range(SWA_KV_HEADS):
            kh = k2[:, hk * HEAD_DIM:(hk + 1) * HEAD_DIM]
            vh = v2[:, hk * HEAD_DIM:(hk + 1) * HEAD_DIM]
            for g in range(SWA_GROUP):
                h = hk * SWA_GROUP + g
                qh = q_ref[:, h * HEAD_DIM:(h + 1) * HEAD_DIM]
                pn, _ = _swa_probs(qh, kh, vis, bias_ref[h], sink_ref[0, h])
                o_ref[:, h * HEAD_DIM:(h + 1) * HEAD_DIM] = _nn(pn.astype(BF), vh)

    prev = lambda b: jnp.maximum(b - 1, 0)
    return pl.pallas_call(
        body, name="swa_fwd", grid=(nb,), out_shape=S((t, qw), F32),
        in_specs=[
            pl.BlockSpec((BLOCK, qw), lambda b: (b, 0)),
            pl.BlockSpec((BLOCK, kw), lambda b: (prev(b), qw // kw)),
            pl.BlockSpec((BLOCK, kw), lambda b: (b, qw // kw)),
            pl.BlockSpec((BLOCK, kw), lambda b: (prev(b), qw // kw + 1)),
            pl.BlockSpec((BLOCK, kw), lambda b: (b, qw // kw + 1)),
            pl.BlockSpec((BLOCK, 1), lambda b: (b, 0)),
            pl.BlockSpec((1, BLOCK), lambda b: (0, prev(b))),
            pl.BlockSpec((1, BLOCK), lambda b: (0, b)),
            _resident(bias_t.shape),
            SMEM_SPEC,
        ],
        out_specs=pl.BlockSpec((BLOCK, qw), lambda b: (b, 0)),
        compiler_params=_params(("parallel",)),
    )(qkv, qkv, qkv, qkv, qkv, pos_col, pos_row, pos_row, bias_t, sinks)


def _conv_taps(z, zh, first):
    tm = z.shape[0]
    zh = jnp.where(first, 0.0, zh)
    row = lax.broadcasted_iota(I32, (tm, 1), 0)
    z1 = jnp.where(row == 0, zh[7:8, :], pltpu.roll(z, 1, 0))
    z2 = jnp.where(row == 0, zh[6:7, :], jnp.where(row == 1, zh[7:8, :], pltpu.roll(z, 2, 0)))
    return z1, z2


def _mix_out_fwd(e, attn, conv_w, w_out, x):
    t, d = x.shape
    tm = 256
    hb = tm // 8

    def body(c_ref, b_ref, u_ref, ga_ref, gc_ref, ch_ref, uh_ref, attn_ref, cw_ref, w_ref, x_ref, xo_ref, mg_ref):
        i = pl.program_id(0)
        z = c_ref[...] * u_ref[...]
        z1, z2 = _conv_taps(z, ch_ref[...] * uh_ref[...], i == 0)
        s = cw_ref[0:1, :] * z2 + cw_ref[1:2, :] * z1 + cw_ref[2:3, :] * z
        conv = b_ref[...] * s
        merged = (_sigmoid(ga_ref[...]) * attn_ref[...] + _sigmoid(gc_ref[...]) * conv).astype(BF)
        mg_ref[...] = merged
        xo_ref[...] = x_ref[...] + _nn(merged, w_ref[...])

    ecol = lambda cb: pl.BlockSpec((tm, d), lambda i: (i, cb))
    halo = lambda cb: pl.BlockSpec((8, d), lambda i: (jnp.maximum(i * hb - 1, 0), cb))
    row = pl.BlockSpec((tm, d), lambda i: (i, 0))
    return pl.pallas_call(
        body, name="mix_out_fwd", grid=(t // tm,),
        out_shape=(S((t, d), F32), S((t, d), BF)),
        in_specs=[ecol(0), ecol(1), ecol(2), ecol(3), ecol(4), halo(0), halo(2), row,
                  _resident(conv_w.shape), _resident(w_out.shape), row],
        out_specs=(row, row),
        compiler_params=_params(("parallel",)),
    )(e, e, e, e, e, e, e, attn, conv_w, w_out, x)


def _mem_kv(mem, gm, wkv):
    m, d = mem.shape

    def body(mem_ref, gm_ref, w_ref, mh_ref, kv_ref):
        n, _ = _rms(mem_ref[...])
        mh = (n * gm_ref[...]).astype(BF)
        mh_ref[...] = mh
        kv_ref[...] = _nn(mh, w_ref[...]).astype(BF)

    return pl.pallas_call(
        body, name="mem_kv", out_shape=(S((m, d), BF), S((m, wkv.shape[1]), BF)),
        compiler_params=_params(),
    )(mem, gm, wkv)


def _xattn_probs(qh, kh):
    s = _nt(qh, kh) * (kh.shape[1] ** -0.5)
    p = jnp.exp(s - jnp.max(s, axis=-1, keepdims=True))
    return p * (1.0 / jnp.sum(p, axis=-1, keepdims=True))


def _xattn_fwd(x, gn, wq, kv, wo):
    t, d = x.shape
    tm = 256
    hd = d // MEM_HEADS

    def body(x_ref, gn_ref, wq_ref, kv_ref, wo_ref, xo_ref, q_ref, o_ref):
        xv = x_ref[...]
        n, _ = _rms(xv)
        q = _nn((n * gn_ref[...]).astype(BF), wq_ref[...]).astype(BF)
        q_ref[...] = q
        outs = []
        for hh in range(MEM_HEADS):
            p = _xattn_probs(q[:, hh * hd:(hh + 1) * hd], kv_ref[:, hh * hd:(hh + 1) * hd])
            outs.append(_nn(p.astype(BF), kv_ref[:, d + hh * hd:d + (hh + 1) * hd]))
        o = jnp.concatenate(outs, axis=1).astype(BF)
        o_ref[...] = o
        xo_ref[...] = xv + _nn(o, wo_ref[...])

    row = pl.BlockSpec((tm, d), lambda i: (i, 0))
    return pl.pallas_call(
        body, name="xattn_fwd", grid=(t // tm,),
        out_shape=(S((t, d), F32), S((t, d), BF), S((t, d), BF)),
        in_specs=[row, _resident((1, d)), _resident(wq.shape), _resident(kv.shape), _resident(wo.shape)],
        out_specs=(row, row, row),
        compiler_params=_params(("parallel",)),
    )(x, gn, wq, kv, wo)


def _loss_bwd(x, gf, target):
    t, d = x.shape
    tm = 512

    def body(x_ref, gf_ref, t_ref, dx_ref, dg_ref, loss_ref):
        @pl.when(pl.program_id(0) == 0)
        def _():
            dg_ref[...] = jnp.zeros_like(dg_ref)
            loss_ref[...] = jnp.zeros_like(loss_ref)
        n, r = _rms(x_ref[...])
        g = gf_ref[...]
        err = n * g - t_ref[...]
        loss_ref[...] += 0.5 * jnp.sum(jnp.sum(err * err, axis=-1, keepdims=True) / d, axis=0, keepdims=True)
        dx, dg = _rms_bwd(err / d, n, r, g)
        dx_ref[...] = dx
        dg_ref[...] += dg

    row = pl.BlockSpec((tm, d), lambda i: (i, 0))
    return pl.pallas_call(
        body, name="loss_bwd", grid=(t // tm,),
        out_shape=(S((t, d), F32), S((1, d), F32), S((1, 128), F32)),
        in_specs=[row, _resident((1, d)), row],
        out_specs=(row, _acc_spec((1, d)), _acc_spec((1, 128))),
        compiler_params=_params(("arbitrary",)),
    )(x, gf, target)


def _ffn_bwd(dxo, x, gn, g, u, wgu, wd, name):
    t, d = x.shape
    f = wd.shape[0]
    tm, fc = 256, 1408

    def body(dxo_ref, x_ref, gn_ref, g_ref, u_ref, wgu_ref, wd_ref, dx_ref, dgn_ref, dgu_ref, a_ref, h_ref, dyh_ref):
        @pl.when(pl.program_id(0) == 0)
        def _():
            dgn_ref[...] = jnp.zeros_like(dgn_ref)
        dxov = dxo_ref[...]
        dyh = (0.5 * dxov).astype(BF)
        dyh_ref[...] = dyh
        n, r = _rms(x_ref[...])
        gnv = gn_ref[...]
        h_ref[...] = (n * gnv).astype(BF)
        dh = jnp.zeros((tm, d), F32)
        for c0 in range(0, f, fc):
            gv = g_ref[:, c0:c0 + fc].astype(F32)
            uv = u_ref[:, c0:c0 + fc].astype(F32)
            da = _nt(dyh, wd_ref[c0:c0 + fc, :])
            sg = _sigmoid(gv)
            silu = gv * sg
            a_ref[:, c0:c0 + fc] = (silu * uv).astype(BF)
            dg = (da * uv * (sg * (1.0 + gv * (1.0 - sg)))).astype(BF)
            du = (da * silu).astype(BF)
            dgu_ref[:, c0:c0 + fc] = dg
            dgu_ref[:, f + c0:f + c0 + fc] = du
            dh = dh + _nt(dg, wgu_ref[:, c0:c0 + fc]) + _nt(du, wgu_ref[:, f + c0:f + c0 + fc])
        dx, dgn = _rms_bwd(dh, n, r, gnv)
        dx_ref[...] = dxov + dx
        dgn_ref[...] += dgn

    row = pl.BlockSpec((tm, d), lambda i: (i, 0))
    frow = pl.BlockSpec((tm, f), lambda i: (i, 0))
    return pl.pallas_call(
        body, name=name, grid=(t // tm,),
        out_shape=(S((t, d), F32), S((1, d), F32), S((t, 2 * f), BF), S((t, f), BF), S((t, d), BF), S((t, d), BF)),
        in_specs=[row, row, _resident((1, d)), frow, frow, _resident(wgu.shape), _resident(wd.shape)],
        out_specs=(row, _acc_spec((1, d)), pl.BlockSpec((tm, 2 * f), lambda i: (i, 0)), frow, row, row),
        compiler_params=_params(("arbitrary",)),
    )(dxo, x, gn, g, u, wgu, wd)


def _dw(a, b, tn, name, into=None, col_block=0, ncols=None):
    t, ka = a.shape
    nb = b.shape[1]
    tt = 1024
    nt = t // tt
    ncols = nb if ncols is None else ncols

    def body(*refs):
        a_ref, b_ref = refs[0], refs[1]
        o_ref, acc_ref = refs[-2], refs[-1]
        k = pl.program_id(1)

        @pl.when(k == 0)
        def _():
            acc_ref[...] = jnp.zeros_like(acc_ref)
        acc_ref[...] += _tn(a_ref[...], b_ref[...])

        @pl.when(k == nt - 1)
        def _():
            o_ref[...] = acc_ref[...].astype(BF)

    in_specs = [pl.BlockSpec((tt, ka), lambda j, k: (k, 0)), pl.BlockSpec((tt, tn), lambda j, k: (k, j))]
    args = [a, b]
    aliases = {}
    if into is not None:
        in_specs.append(ANY)
        args.append(into)
        aliases = {2: 0}
    return pl.pallas_call(
        body, name=name, grid=(nb // tn, nt), out_shape=S((ka, ncols), BF),
        in_specs=in_specs, out_specs=pl.BlockSpec((ka, tn), lambda j, k: (0, j + col_block)),
        scratch_shapes=[pltpu.VMEM((ka, tn), F32)], input_output_aliases=aliases,
        compiler_params=_params(("parallel", "arbitrary")),
    )(*args)


def _xattn_bwd(dxo, x, gn, q, o_unused, kv, wq, wo):
    t, d = x.shape
    tm = 256
    hd = d // MEM_HEADS
    nkv = kv.shape[0]

    def body(dxo_ref, x_ref, gn_ref, q_ref, kv_ref, wq_ref, wo_ref, dx_ref, dgn_ref, dkv_ref, dxh_ref, h_ref, dq_ref):
        @pl.when(pl.program_id(0) == 0)
        def _():
            dgn_ref[...] = jnp.zeros_like(dgn_ref)
            dkv_ref[...] = jnp.zeros_like(dkv_ref)
        dxov = dxo_ref[...]
        dxh = dxov.astype(BF)
        dxh_ref[...] = dxh
        do = _nt(dxh, wo_ref[...]).astype(BF)
        dqs = []
        for hh in range(MEM_HEADS):
            lo, hi = hh * hd, (hh + 1) * hd
            qh = q_ref[:, lo:hi]
            kh = kv_ref[:, lo:hi]
            vh = kv_ref[:, d + lo:d + hi]
            doh = do[:, lo:hi]
            p = _xattn_probs(qh, kh)
            dp = _nt(doh, vh)
            ds = (p * (dp - jnp.sum(p * dp, axis=-1, keepdims=True)) * (hd ** -0.5)).astype(BF)
            dqs.append(_nn(ds, kh))
            dkv_ref[:, lo:hi] += _tn(ds, qh)
            dkv_ref[:, d + lo:d + hi] += _tn(p.astype(BF), doh)
        dq = jnp.concatenate(dqs, axis=1).astype(BF)
        dq_ref[...] = dq
        n, r = _rms(x_ref[...])
        gnv = gn_ref[...]
        h_ref[...] = (n * gnv).astype(BF)
        dx, dgn = _rms_bwd(_nt(dq, wq_ref[...]), n, r, gnv)
        dx_ref[...] = dxov + dx
        dgn_ref[...] += dgn

    row = pl.BlockSpec((tm, d), lambda i: (i, 0))
    return pl.pallas_call(
        body, name="xattn_bwd", grid=(t // tm,),
        out_shape=(S((t, d), F32), S((1, d), F32), S((nkv, 2 * d), F32), S((t, d), BF), S((t, d), BF), S((t, d), BF)),
        in_specs=[row, row, _resident((1, d)), row, _resident(kv.shape), _resident(wq.shape), _resident(wo.shape)],
        out_specs=(row, _acc_spec((1, d)), _acc_spec((nkv, 2 * d)), row, row, row),
        compiler_params=_params(("arbitrary",)),
    )(dxo, x, gn, q, kv, wq, wo)


def _mem_bwd(dkv, mh, mem, gm, wkv):
    m, d = mem.shape

    def body(dkv_ref, mh_ref, mem_ref, gm_ref, w_ref, dw_ref, dgm_ref):
        dkvb = dkv_ref[...].astype(BF)
        dw_ref[...] = _tn(mh_ref[...], dkvb).astype(BF)
        dmh = _nt(dkvb, w_ref[...])
        n, _ = _rms(mem_ref[...])
        dgm_ref[...] = jnp.sum(dmh * n, axis=0, keepdims=True)

    return pl.pallas_call(
        body, name="mem_bwd", out_shape=(S(wkv.shape, BF), S((1, d), F32)),
        compiler_params=_params(),
    )(dkv, mh, mem, gm, wkv)


def _mix_out_bwd(dxo, e, attn, conv_w, w_out):
    t, d = attn.shape
    tm = 256
    hb = tm // 8
    nt = t // tm
    last8 = t // 8 - 1

    def body(dxo_ref, dxn_ref, c_ref, b_ref, u_ref, ga_ref, gc_ref, ch_ref, uh_ref, bn_ref, gcn_ref,
             attn_ref, cw_ref, w_ref, dattn_ref, de_ref, dcw_ref, dxh_ref):
        i = pl.program_id(0)

        @pl.when(i == 0)
        def _():
            dcw_ref[...] = jnp.zeros_like(dcw_ref)
        dxh = dxo_ref[...].astype(BF)
        dxh_ref[...] = dxh
        w = w_ref[...]
        dm = _nt(dxh, w)
        dmn = _nt(dxn_ref[...].astype(BF), w)
        cv, bv, uv = c_ref[...], b_ref[...], u_ref[...]
        sga = _sigmoid(ga_ref[...])
        sgc = _sigmoid(gc_ref[...])
        z = cv * uv
        z1, z2 = _conv_taps(z, ch_ref[...] * uh_ref[...], i == 0)
        w0, w1, w2 = cw_ref[0:1, :], cw_ref[1:2, :], cw_ref[2:3, :]
        s = w0 * z2 + w1 * z1 + w2 * z
        av = attn_ref[...]
        dattn_ref[...] = (dm * sga).astype(BF)
        dconv = dm * sgc
        ds = dconv * bv
        dsn = jnp.where(i == nt - 1, 0.0, dmn * _sigmoid(gcn_ref[...]) * bn_ref[...])
        row = lax.broadcasted_iota(I32, (tm, 1), 0)
        dsp1 = jnp.where(row == tm - 1, dsn[0:1, :], pltpu.roll(ds, tm - 1, 0))
        dsp2 = jnp.where(row == tm - 2, dsn[0:1, :], jnp.where(row == tm - 1, dsn[1:2, :], pltpu.roll(ds, tm - 2, 0)))
        dz = w2 * ds + w1 * dsp1 + w0 * dsp2
        de_ref[:, 0:d] = (dz * uv).astype(BF)
        de_ref[:, d:2 * d] = (dconv * s).astype(BF)
        de_ref[:, 2 * d:3 * d] = (dz * cv).astype(BF)
        de_ref[:, 3 * d:4 * d] = (dm * av * sga * (1.0 - sga)).astype(BF)
        de_ref[:, 4 * d:5 * d] = (dm * (bv * s) * sgc * (1.0 - sgc)).astype(BF)
        dcw_ref[0:1, :] += jnp.sum(ds * z2, axis=0, keepdims=True)
        dcw_ref[1:2, :] += jnp.sum(ds * z1, axis=0, keepdims=True)
        dcw_ref[2:3, :] += jnp.sum(ds * z, axis=0, keepdims=True)

    ecol = lambda cb: pl.BlockSpec((tm, d), lambda i: (i, cb))
    prev = lambda cb: pl.BlockSpec((8, d), lambda i: (jnp.maximum(i * hb - 1, 0), cb))
    nxt = lambda cb: pl.BlockSpec((8, d), lambda i: (jnp.minimum((i + 1) * hb, last8), cb))
    row = pl.BlockSpec((tm, d), lambda i: (i, 0))
    return pl.pallas_call(
        body, name="mix_out_bwd", grid=(nt,),
        out_shape=(S((t, d), BF), S((t, 5 * d), BF), S((8, d), F32), S((t, d), BF)),
        in_specs=[row, nxt(0), ecol(0), ecol(1), ecol(2), ecol(3), ecol(4), prev(0), prev(2), nxt(1), nxt(4),
                  row, _resident(conv_w.shape), _resident(w_out.shape)],
        out_specs=(row, pl.BlockSpec((tm, 5 * d), lambda i: (i, 0)), _acc_spec((8, d)), row),
        compiler_params=_params(("arbitrary",)),
    )(dxo, dxo, e, e, e, e, e, e, e, e, e, attn, conv_w, w_out)


def _swa_bwd(qkv, dattn, pos_col, pos_row, bias_t, sinks):
    t = qkv.shape[0]
    nb = t // BLOCK
    qw = SWA_HEADS * HEAD_DIM
    kw = SWA_KV_HEADS * HEAD_DIM

    def body(q_ref, kp_ref, kc_ref, vp_ref, vc_ref, do_ref, pq_ref, pkp_ref, pkc_ref, bias_ref, sink_ref,
             dq_ref, dkv_ref, gb_ref, dsk_ref, carry_ref):
        b = pl.program_id(0)

        @pl.when(b == 0)
        def _():
            gb_ref[...] = jnp.zeros_like(gb_ref)
            dsk_ref[...] = jnp.zeros_like(dsk_ref)
            carry_ref[...] = jnp.zeros_like(carry_ref)

        @pl.when(b < nb)
        def _():
            vis = _swa_visible(b, pq_ref, pkp_ref, pkc_ref)
            k2 = jnp.concatenate([kp_ref[...], kc_ref[...]], axis=0)
            v2 = jnp.concatenate([vp_ref[...], vc_ref[...]], axis=0)
            for hk in range(SWA_KV_HEADS):
                lo, hi = hk * HEAD_DIM, (hk + 1) * HEAD_DIM
                kh = k2[:, lo:hi]
                vh = v2[:, lo:hi]
                dk = jnp.zeros((2 * BLOCK, HEAD_DIM), F32)
                dv = jnp.zeros((2 * BLOCK, HEAD_DIM), F32)
                for g in range(SWA_GROUP):
                    h = hk * SWA_GROUP + g
                    hs = slice(h * HEAD_DIM, (h + 1) * HEAD_DIM)
                    qh = q_ref[:, hs]
                    doh = do_ref[:, hs]
                    pn, psn = _swa_probs(qh, kh, vis, bias_ref[h], sink_ref[0, h])
                    dp = _nt(doh, vh)
                    delta = jnp.sum(pn * dp, axis=-1, keepdims=True)
                    ds = pn * (dp - delta)
                    gb_ref[h] += ds
                    dsk_ref[:, h:h + 1] += -psn * delta
                    dsb = (ds * (HEAD_DIM ** -0.5)).astype(BF)
                    dq_ref[:, hs] = _nn(dsb, kh).astype(BF)
                    dk = dk + _tn(dsb, qh)
                    dv = dv + _tn(pn.astype(BF), doh)
                dkv_ref[:, lo:hi] = (carry_ref[:, lo:hi] + dk[0:BLOCK]).astype(BF)
                dkv_ref[:, kw + lo:kw + hi] = (carry_ref[:, kw + lo:kw + hi] + dv[0:BLOCK]).astype(BF)
                carry_ref[:, lo:hi] = dk[BLOCK:2 * BLOCK]
                carry_ref[:, kw + lo:kw + hi] = dv[BLOCK:2 * BLOCK]

        @pl.when(b == nb)
        def _():
            dkv_ref[...] = carry_ref[...].astype(BF)

    cur = lambda b: jnp.minimum(b, nb - 1)
    prev = lambda b: jnp.maximum(cur(b) - 1, 0)
    return pl.pallas_call(
        body, name="swa_bwd", grid=(nb + 1,),
        out_shape=(S((t, qw), BF), S((t, 2 * kw), BF), S((SWA_HEADS, BLOCK, 2 * BLOCK), F32), S((BLOCK, SWA_HEADS), F32)),
        in_specs=[
            pl.BlockSpec((BLOCK, qw), lambda b: (cur(b), 0)),
            pl.BlockSpec((BLOCK, kw), lambda b: (prev(b), qw // kw)),
            pl.BlockSpec((BLOCK, kw), lambda b: (cur(b), qw // kw)),
            pl.BlockSpec((BLOCK, kw), lambda b: (prev(b), qw // kw + 1)),
            pl.BlockSpec((BLOCK, kw), lambda b: (cur(b), qw // kw + 1)),
            pl.BlockSpec((BLOCK, qw), lambda b: (cur(b), 0)),
            pl.BlockSpec((BLOCK, 1), lambda b: (cur(b), 0)),
            pl.BlockSpec((1, BLOCK), lambda b: (0, prev(b))),
            pl.BlockSpec((1, BLOCK), lambda b: (0, cur(b))),
            _resident(bias_t.shape),
            SMEM_SPEC,
        ],
        out_specs=(
            pl.BlockSpec((BLOCK, qw), lambda b: (cur(b), 0)),
            pl.BlockSpec((BLOCK, 2 * kw), lambda b: (jnp.maximum(b - 1, 0), 0)),
            _acc_spec((SWA_HEADS, BLOCK, 2 * BLOCK)),
            _acc_spec((BLOCK, SWA_HEADS)),
        ),
        scratch_shapes=[pltpu.VMEM((BLOCK, 2 * kw), F32)],
        compiler_params=_params(("arbitrary",)),
    )(qkv, qkv, qkv, qkv, qkv, dattn, pos_col, pos_row, pos_row, bias_t, sinks)


def _bias_reduce(gb, dsk):
    def body(gb_ref, dsk_ref, drb_ref, dsink_ref):
        bucket = _t5_bucket(_block_rel())
        for b in range(REL_BUCKETS):
            mask = bucket == b
            for h in range(SWA_HEADS):
                drb_ref[b, h] = jnp.sum(jnp.where(mask, gb_ref[h], 0.0))
        for h in range(SWA_HEADS):
            dsink_ref[0, h] = jnp.sum(dsk_ref[:, h:h + 1])

    return pl.pallas_call(
        body, name="bias_reduce", out_shape=(S((REL_BUCKETS, SWA_HEADS), F32), S((1, SWA_HEADS), F32)),
        in_specs=[VMEM_SPEC, VMEM_SPEC], out_specs=(SMEM_SPEC, SMEM_SPEC),
    )(gb, dsk)


def _mix_in_bwd(dq, dkv, de, x, gn, w_in, dxo):
    t, d = x.shape
    tm = 256
    nq, nkv, ne = dq.shape[1], dkv.shape[1], de.shape[1]

    def body(dq_ref, dkv_ref, de_ref, x_ref, gn_ref, w_ref, dxo_ref, dx_ref, dgn_ref, h_ref):
        @pl.when(pl.program_id(0) == 0)
        def _():
            dgn_ref[...] = jnp.zeros_like(dgn_ref)
        dh = _nt(dq_ref[...], w_ref[:, 0:nq]) + _nt(dkv_ref[...], w_ref[:, nq:nq + nkv])
        for c0 in range(0, ne, 1024):
            dh = dh + _nt(de_ref[:, c0:c0 + 1024], w_ref[:, nq + nkv + c0:nq + nkv + c0 + 1024])
        n, r = _rms(x_ref[...])
        gnv = gn_ref[...]
        h_ref[...] = (n * gnv).astype(BF)
        dx, dgn = _rms_bwd(dh, n, r, gnv)
        dx_ref[...] = dxo_ref[...] + dx
        dgn_ref[...] += dgn

    row = pl.BlockSpec((tm, d), lambda i: (i, 0))
    wide = lambda w: pl.BlockSpec((tm, w), lambda i: (i, 0))
    return pl.pallas_call(
        body, name="mix_in_bwd", grid=(t // tm,),
        out_shape=(S((t, d), F32), S((1, d), F32), S((t, d), BF)),
        in_specs=[wide(nq), wide(nkv), wide(ne), row, _resident((1, d)), _resident(w_in.shape), row],
        out_specs=(row, _acc_spec((1, d)), row),
        compiler_params=_params(("arbitrary",)),
    )(dq, dkv, de, x, gn, w_in, dxo)


def _row_tile(rows):
    for tr in (256, 352, 128, 64, 16, 8):
        if rows % tr == 0:
            return tr
    return rows


def _add_bf16(a, b, name):
    rows, cols = a.shape
    tr = _row_tile(rows)

    def body(a_ref, b_ref, o_ref):
        o_ref[...] = (a_ref[...].astype(F32) + b_ref[...].astype(F32)).astype(BF)

    blk = pl.BlockSpec((tr, cols), lambda i: (i, 0))
    return pl.pallas_call(body, name=name, grid=(rows // tr,), out_shape=S((rows, cols), BF),
                          in_specs=[blk, blk], out_specs=blk, compiler_params=_params(("parallel",)))(a, b)


def _sum4(parts, name):
    _, rows, cols = parts.shape
    tr = _row_tile(rows)

    def body(p_ref, o_ref):
        acc = p_ref[3].astype(F32)
        for k in range(3):
            acc = acc + p_ref[k].astype(F32)
        o_ref[...] = acc

    return pl.pallas_call(
        body, name=name, grid=(rows // tr,), out_shape=S((rows, cols), F32),
        in_specs=[pl.BlockSpec((4, tr, cols), lambda i: (0, i, 0))], out_specs=pl.BlockSpec((tr, cols), lambda i: (i, 0)),
        compiler_params=_params(("parallel",)))(parts)


def _adamw(w, g, m, v, name):
    rows, cols = w.shape
    tr = _row_tile(rows)

    def body(w_ref, g_ref, m_ref, v_ref, d_ref, mo_ref, vo_ref):
        gv = g_ref[...]
        mn = ADAM_B1 * m_ref[...] + (1.0 - ADAM_B1) * gv
        vn = ADAM_B2 * v_ref[...] + (1.0 - ADAM_B2) * (gv * gv)
        m_hat = mn / (1.0 - ADAM_B1 ** ADAM_STEP)
        v_hat = vn / (1.0 - ADAM_B2 ** ADAM_STEP)
        d_ref[...] = -ADAM_LR * (m_hat / (jnp.sqrt(v_hat) + ADAM_EPS) + ADAM_WD * w_ref[...])
        mo_ref[...] = mn
        vo_ref[...] = vn

    blk = pl.BlockSpec((tr, cols), lambda i: (i, 0))
    shp = S((rows, cols), F32)
    return pl.pallas_call(body, name=name, grid=(rows // tr,), out_shape=(shp, shp, shp),
                          in_specs=[blk] * 4, out_specs=(blk,) * 3, compiler_params=_params(("parallel",)))(w, g, m, v)


def _place():
    x, y, c = lax.axis_index("x"), lax.axis_index("y"), lax.axis_index("c")
    return x, y, c


OTHER_CHIPS = ((1, 0), (0, 1), (1, 1))


def _flip(v, f):
    return 1 - v if f else v


def _remote(src, dst, ssem, rsem, dev):
    return pltpu.make_async_remote_copy(src_ref=src, dst_ref=dst, send_sem=ssem, recv_sem=rsem,
                                        device_id=dev, device_id_type=MESH)


def _all_gather_weights(shards, kinds):
    n = len(shards)
    out_shape = tuple(
        S((w.shape[0], 4 * w.shape[1]), BF) if k == "col" else S((4,) + w.shape, BF) for w, k in zip(shards, kinds))

    def body(*refs):
        ins, outs = refs[:n], refs[n:2 * n]
        ssem, rsem, fsem, frsem, lsem = refs[2 * n:]
        x, y, c = _place()
        s_me = 2 * x + y

        def window(i, s, half):
            rows, cols = shards[i].shape
            rh = rows // 2
            r0 = pl.multiple_of(half * rh, 16)
            if kinds[i] == "col":
                return outs[i].at[pl.ds(r0, rh), pl.ds(pl.multiple_of(s * cols, 128), cols)]
            return outs[i].at[s, pl.ds(r0, rh), :]

        pending = []
        for i in range(n):
            rows, cols = shards[i].shape
            rh = rows // 2
            whole = (outs[i].at[:, pl.ds(pl.multiple_of(s_me * cols, 128), cols)] if kinds[i] == "col"
                     else outs[i].at[s_me])
            cp = pltpu.make_async_copy(ins[i], whole, lsem.at[i])
            cp.start()
            pending.append(cp.wait)
            src = ins[i].at[pl.ds(pl.multiple_of(c * rh, 16), rh), :]
            for j, (fx, fy) in enumerate(OTHER_CHIPS):
                cp = _remote(src, window(i, s_me, c), ssem.at[i, j], rsem.at[i, j], (_flip(x, fx), _flip(y, fy), c))
                cp.start()
                pending.append(cp.wait_send)
        for i in range(n):
            for j, (fx, fy) in enumerate(OTHER_CHIPS):
                px, py = _flip(x, fx), _flip(y, fy)
                landed = window(i, 2 * px + py, c)
                _remote(landed, landed, ssem.at[i, j], rsem.at[i, j], (px, py, c)).wait_recv()
                cp = _remote(landed, landed, fsem.at[i, j], frsem.at[i, j], (x, y, 1 - c))
                cp.start()
                pending.append(cp.wait_send)
        for i in range(n):
            for j, (fx, fy) in enumerate(OTHER_CHIPS):
                passed = window(i, 2 * _flip(x, fx) + _flip(y, fy), 1 - c)
                _remote(passed, passed, fsem.at[i, j], frsem.at[i, j], (x, y, 1 - c)).wait_recv()
        for wait in pending:
            wait()

    sems = [pltpu.SemaphoreType.DMA((n, 3)) for _ in range(4)] + [pltpu.SemaphoreType.DMA((n,))]
    return pl.pallas_call(
        body, name="all_gather_weights", out_shape=out_shape,
        in_specs=[ANY] * n, out_specs=tuple(ANY for _ in range(n)), scratch_shapes=sems,
    )(*shards)


def _half_view(ref, kind, half, rh):
    r0 = pl.multiple_of(half * rh, 16)
    if kind == "col":
        return ref.at[pl.ds(r0, rh), :]
    return ref.at[:, pl.ds(r0, rh), :]


def _reduce_siblings(grads, kinds):
    n = len(grads)

    def half_shape(gr, k):
        return S((gr.shape[0] // 2, gr.shape[1]), BF) if k == "col" else S((4, gr.shape[1] // 2, gr.shape[2]), BF)

    halves = tuple(half_shape(gr, k) for gr, k in zip(grads, kinds))

    def body(*refs):
        ins, mine, sib = refs[:n], refs[n:2 * n], refs[2 * n:3 * n]
        ssem, rsem, lsem = refs[3 * n:]
        x, y, c = _place()
        pending = []
        for i in range(n):
            rh = (grads[i].shape[0] if kinds[i] == "col" else grads[i].shape[1]) // 2
            cp = pltpu.make_async_copy(_half_view(ins[i], kinds[i], c, rh), mine[i], lsem.at[i])
            cp.start()
            pending.append(cp.wait)
            cp = _remote(_half_view(ins[i], kinds[i], 1 - c, rh), sib[i], ssem.at[i], rsem.at[i], (x, y, 1 - c))
            cp.start()
            pending.append(cp.wait)
        for wait in pending:
            wait()

    sems = [pltpu.SemaphoreType.DMA((n,)) for _ in range(3)]
    return pl.pallas_call(
        body, name="reduce_siblings", out_shape=halves + halves,
        in_specs=[ANY] * n, out_specs=tuple(ANY for _ in range(2 * n)), scratch_shapes=sems,
    )(*grads)


def _reduce_chips(parts, kinds):
    n = len(parts)

    def slab_shape(p, k):
        return S((4, p.shape[0], p.shape[1] // 4), BF) if k == "col" else S(p.shape, BF)

    out_shape = tuple(slab_shape(p, k) for p, k in zip(parts, kinds))

    def body(*refs):
        ins, outs = refs[:n], refs[n:2 * n]
        ssem, rsem, lsem = refs[2 * n:]
        x, y, c = _place()
        s_me = 2 * x + y

        def slab(i, s):
            if kinds[i] == "col":
                cols = parts[i].shape[1] // 4
                return ins[i].at[:, pl.ds(pl.multiple_of(s * cols, 128), cols)]
            return ins[i].at[s]

        pending = []
        for i in range(n):
            cp = pltpu.make_async_copy(slab(i, s_me), outs[i].at[3], lsem.at[i])
            cp.start()
            pending.append(cp.wait)
            for j, (fx, fy) in enumerate(OTHER_CHIPS):
                px, py = _flip(x, fx), _flip(y, fy)
                cp = _remote(slab(i, 2 * px + py), outs[i].at[j], ssem.at[i, j], rsem.at[i, j], (px, py, c))
                cp.start()
                pending.append(cp.wait)
        for wait in pending:
            wait()

    sems = [pltpu.SemaphoreType.DMA((n, 3)), pltpu.SemaphoreType.DMA((n, 3)), pltpu.SemaphoreType.DMA((n,))]
    return pl.pallas_call(
        body, name="reduce_chips", out_shape=out_shape,
        in_specs=[ANY] * n, out_specs=tuple(ANY for _ in range(n)), scratch_shapes=sems,
    )(*parts)


def _share_halves(halves):
    n = len(halves)
    out_shape = tuple(S((2 * h.shape[0], h.shape[1]), F32) for h in halves)

    def body(*refs):
        ins, outs = refs[:n], refs[n:2 * n]
        ssem, rsem, lsem = refs[2 * n:]
        x, y, c = _place()
        pending = []
        for i in range(n):
            rh = halves[i].shape[0]
            dst = outs[i].at[pl.ds(pl.multiple_of(c * rh, 8), rh), :]
            cp = pltpu.make_async_copy(ins[i], dst, lsem.at[i])
            cp.start()
            pending.append(cp.wait)
            cp = _remote(ins[i], dst, ssem.at[i], rsem.at[i], (x, y, 1 - c))
            cp.start()
            pending.append(cp.wait)
        for wait in pending:
            wait()

    sems = [pltpu.SemaphoreType.DMA((n,)) for _ in range(3)]
    return pl.pallas_call(
        body, name="share_halves", out_shape=out_shape,
        in_specs=[ANY] * n, out_specs=tuple(ANY for _ in range(n)), scratch_shapes=sems,
    )(*halves)


def _all_reduce_small(buf, name):
    shape = buf.shape

    def body(in_ref, out_ref, slots, ssem, rsem):
        x, y, c = _place()
        me = 4 * x + 2 * y + c
        slots[me] = in_ref[...]
        sends = []
        for r in range(1, 8):
            fx, fy, fc = (r >> 2) & 1, (r >> 1) & 1, r & 1
            cp = _remote(in_ref, slots.at[me], ssem.at[r - 1], rsem.at[r - 1], (_flip(x, fx), _flip(y, fy), _flip(c, fc)))
            cp.start()
            sends.append(cp)
        for r in range(1, 8):
            fx, fy, fc = (r >> 2) & 1, (r >> 1) & 1, r & 1
            px, py, pc = _flip(x, fx), _flip(y, fy), _flip(c, fc)
            _remote(in_ref, slots.at[4 * px + 2 * py + pc], ssem.at[r - 1], rsem.at[r - 1], (px, py, pc)).wait_recv()
        for cp in sends:
            cp.wait_send()
        acc = slots[0]
        for k in range(1, 8):
            acc = acc + slots[k]
        out_ref[...] = acc

    return pl.pallas_call(
        body, name=name, out_shape=S(shape, F32), in_specs=[VMEM_SPEC], out_specs=VMEM_SPEC,
        scratch_shapes=[pltpu.VMEM((8,) + shape, F32), pltpu.SemaphoreType.DMA((7,)), pltpu.SemaphoreType.DMA((7,))],
    )(buf)


BIG = ("ffn1_w_gu", "ffn1_w_down", "w_in", "w_out", "xattn_wq", "xattn_wkv", "xattn_wo", "ffn2_w_gu", "ffn2_w_down")
KIND = {"ffn1_w_gu": "col", "ffn1_w_down": "row", "w_in": "col", "w_out": "row", "xattn_wq": "row",
        "xattn_wkv": "col", "xattn_wo": "row", "ffn2_w_gu": "col", "ffn2_w_down": "row"}
WEIGHTS = ("rel_bias", "ffn1_norm", "ffn1_w_gu", "ffn1_w_down", "mix_norm", "w_in", "sinks", "conv_w", "w_out",
           "xattn_norm", "mem_norm", "xattn_wq", "xattn_wkv", "xattn_wo", "ffn2_norm", "ffn2_w_gu", "ffn2_w_down",
           "final_norm")
SMALL_ROWS = 16
GAIN_ROW = {"ffn1_norm": 0, "mix_norm": 1, "xattn_norm": 2, "mem_norm": 3, "ffn2_norm": 4, "final_norm": 5}
CONV_ROW, SINK_ROW, BIAS_ROW, LOSS_ROW = 6, 9, 10, 11


def _local_step(x, mem, pos, target, w, gains, rel_bias, sinks, conv_w):
    t, d = x.shape
    pos_col = pos.reshape(t, 1)
    pos_row = pos.reshape(1, t)
    bias_t = _bias_build(rel_bias)

    x1, g1, u1 = _ffn_fwd(x, gains["ffn1_norm"], w["ffn1_w_gu"], w["ffn1_w_down"], "ffn1_fwd")
    qkv, e = _mix_proj(x1, gains["mix_norm"], w["w_in"])
    attn = _swa_fwd(qkv, pos_col, pos_row, bias_t, sinks)
    x2, merged = _mix_out_fwd(e, attn, conv_w, w["w_out"], x1)
    mh, kv = _mem_kv(mem, gains["mem_norm"], w["xattn_wkv"])
    x3, qx, o = _xattn_fwd(x2, gains["xattn_norm"], w["xattn_wq"], kv, w["xattn_wo"])
    x4, g2, u2 = _ffn_fwd(x3, gains["ffn2_norm"], w["ffn2_w_gu"], w["ffn2_w_down"], "ffn2_fwd")
    dx4, d_final, loss = _loss_bwd(x4, gains["final_norm"], target)

    grads = {}
    dx3, d_ffn2, dgu2, a2, h4, dyh4 = _ffn_bwd(dx4, x3, gains["ffn2_norm"], g2, u2, w["ffn2_w_gu"], w["ffn2_w_down"],
                                               "ffn2_bwd")
    grads["ffn2_w_gu"] = _dw(h4, dgu2, 1408, "dw_ffn2_gu")
    grads["ffn2_w_down"] = _dw(a2, dyh4, 512, "dw_ffn2_down")
    dx2, d_xattn, dkv, dxh3, h3, dqx = _xattn_bwd(dx3, x2, gains["xattn_norm"], qx, o, kv, w["xattn_wq"], w["xattn_wo"])
    grads["xattn_wo"] = _dw(o, dxh3, 1024, "dw_wo")
    grads["xattn_wq"] = _dw(h3, dqx, 1024, "dw_wq")
    grads["xattn_wkv"], d_mem = _mem_bwd(dkv, mh, mem, gains["mem_norm"], w["xattn_wkv"])
    dattn, de, dcw, dxh2 = _mix_out_bwd(dx2, e, attn, conv_w, w["w_out"])
    grads["w_out"] = _dw(merged, dxh2, 1024, "dw_wout")
    dq, dkvs, gb, dsk = _swa_bwd(qkv, dattn, pos_col, pos_row, bias_t, sinks)
    d_rel_bias, d_sinks = _bias_reduce(gb, dsk)
    dx1, d_mix, h2 = _mix_in_bwd(dq, dkvs, de, x1, gains["mix_norm"], w["w_in"], dx2)
    n_in = w["w_in"].shape[1]
    gw = _dw(h2, dq, 512, "dw_win_q", ncols=n_in)
    gw = _dw(h2, dkvs, 512, "dw_win_kv", into=gw, col_block=dq.shape[1] // 512, ncols=n_in)
    grads["w_in"] = _dw(h2, de, 512, "dw_win_e", into=gw, col_block=(dq.shape[1] + dkvs.shape[1]) // 512, ncols=n_in)
    dx0, d_ffn1, dgu1, a1, h1, dyh1 = _ffn_bwd(dx1, x, gains["ffn1_norm"], g1, u1, w["ffn1_w_gu"], w["ffn1_w_down"],
                                               "ffn1_bwd")
    grads["ffn1_w_gu"] = _dw(h1, dgu1, 1408, "dw_ffn1_gu")
    grads["ffn1_w_down"] = _dw(a1, dyh1, 512, "dw_ffn1_down")

    pad = lambda v: jnp.pad(v.reshape(1, -1), ((0, 0), (0, d - v.size)))
    small = jnp.concatenate([
        d_ffn1, d_mix, d_xattn, d_mem, d_ffn2, d_final, dcw[0:3], pad(d_sinks), pad(d_rel_bias), pad(loss[0, 0:1]),
        jnp.zeros((SMALL_ROWS - LOSS_ROW - 1, d), F32)], axis=0)
    return dx0, grads, small


def kernel(x, mem, positions, rel_bias, ffn1_norm, ffn1_w_gu, ffn1_w_down, mix_norm, w_in, sinks, conv_w, w_out, xattn_norm, mem_norm, xattn_wq, xattn_wkv, xattn_wo, ffn2_norm, ffn2_w_gu, ffn2_w_down, final_norm, loss_target, m_rel_bias, m_ffn1_norm, m_ffn1_w_gu, m_ffn1_w_down, m_mix_norm, m_w_in, m_sinks, m_conv_w, m_w_out, m_xattn_norm, m_mem_norm, m_xattn_wq, m_xattn_wkv, m_xattn_wo, m_ffn2_norm, m_ffn2_w_gu, m_ffn2_w_down, m_final_norm, v_rel_bias, v_ffn1_norm, v_ffn1_w_gu, v_ffn1_w_down, v_mix_norm, v_w_in, v_sinks, v_conv_w, v_w_out, v_xattn_norm, v_mem_norm, v_xattn_wq, v_xattn_wkv, v_xattn_wo, v_ffn2_norm, v_ffn2_w_gu, v_ffn2_w_down, v_final_norm):
    args = dict(locals())
    wts = {k: args[k] for k in WEIGHTS}
    mom = {k: args["m_" + k] for k in WEIGHTS}
    var = {k: args["v_" + k] for k in WEIGHTS}
    d = x.shape[-1]
    s_me = 2 * lax.axis_index("x") + lax.axis_index("y")
    kinds = [KIND[k] for k in BIG]

    shards = [wts[k][0].astype(BF) for k in BIG]
    gathered = _all_gather_weights(shards, kinds)
    whole = {k: (gw if KIND[k] == "col" else gw.reshape(-1, gw.shape[-1])) for k, gw in zip(BIG, gathered)}
    cw_cols = conv_w.shape[-1]
    placed = lax.dynamic_update_slice(jnp.zeros((SMALL_ROWS, d), F32), 0.5 * conv_w[0], (0, s_me * cw_cols))
    conv_whole = _all_reduce_small(placed, "gather_conv_w")[0:3]

    gains = {k: wts[k].reshape(1, d) for k in GAIN_ROW}
    dx0, grads, small = _local_step(x[0], mem[0], positions[0], loss_target[0], whole, gains, rel_bias, sinks,
                                    conv_whole)

    gnat = [grads[k] if KIND[k] == "col" else grads[k].reshape(4, -1, grads[k].shape[-1]) for k in BIG]
    halves = _reduce_siblings(gnat, kinds)
    n = len(BIG)
    parts = []
    for i, k in enumerate(BIG):
        mine, sib = halves[i], halves[n + i]
        if KIND[k] == "row":
            mine, sib = mine.reshape(-1, mine.shape[-1]), sib.reshape(-1, sib.shape[-1])
        p = _add_bf16(mine, sib, "pair_sum_" + k)
        parts.append(p if KIND[k] == "col" else p.reshape(4, -1, p.shape[-1]))
    slabs = _reduce_chips(parts, kinds)
    reduced = [_sum4(sl, "chip_sum_" + k) for sl, k in zip(slabs, BIG)]
    shard_grads = dict(zip(BIG, _share_halves(reduced)))

    small_sum = _all_reduce_small(small, "reduce_small")
    loss = small_sum[LOSS_ROW, 0]

    out_g, out_d, out_m, out_v = {}, {}, {}, {}
    for k in BIG:
        g2d = shard_grads[k]
        dl, mn, vn = _adamw(wts[k][0], g2d, mom[k][0], var[k][0], "adamw_" + k)
        out_g[k], out_d[k], out_m[k], out_v[k] = g2d[None], dl[None], mn[None], vn[None]

    conv_g = lax.dynamic_slice(small_sum, (CONV_ROW, s_me * cw_cols), (3, cw_cols))

    def pack(src, conv_block):
        rows = [None] * SMALL_ROWS
        for k, r in GAIN_ROW.items():
            rows[r] = src[k].reshape(1, d)
        for j in range(3):
            rows[CONV_ROW + j] = jnp.pad(conv_block[j:j + 1], ((0, 0), (0, d - cw_cols)))
        rows[SINK_ROW] = jnp.pad(src["sinks"].reshape(1, -1), ((0, 0), (0, d - src["sinks"].size)))
        rows[BIAS_ROW] = jnp.pad(src["rel_bias"].reshape(1, -1), ((0, 0), (0, d - src["rel_bias"].size)))
        return jnp.concatenate([r if r is not None else jnp.zeros((1, d), F32) for r in rows], axis=0)

    small_g = {k: small_sum[r] for k, r in GAIN_ROW.items()}
    small_g["sinks"] = small_sum[SINK_ROW, 0:sinks.size]
    small_g["rel_bias"] = small_sum[BIAS_ROW, 0:rel_bias.size]
    gp = pack(small_g, conv_g)
    dl, mn, vn = _adamw(pack(wts, conv_w[0]), gp, pack(mom, m_conv_w[0]), pack(var, v_conv_w[0]), "adamw_small")

    def unpack(buf, k):
        if k in GAIN_ROW:
            return buf[GAIN_ROW[k]].reshape(wts[k].shape)
        if k == "conv_w":
            return buf[CONV_ROW:CONV_ROW + 3, 0:cw_cols][None]
        if k == "sinks":
            return buf[SINK_ROW, 0:sinks.size].reshape(sinks.shape)
        return buf[BIAS_ROW, 0:rel_bias.size].reshape(rel_bias.shape)

    for k in WEIGHTS:
        if k not in KIND:
            out_g[k], out_d[k], out_m[k], out_v[k] = unpack(gp, k), unpack(dl, k), unpack(mn, k), unpack(vn, k)

    return (loss, dx0[None], *[out_g[k] for k in WEIGHTS], *[out_d[k] for k in WEIGHTS],
            *[out_m[k] for k in WEIGHTS], *[out_v[k] for k in WEIGHTS])
```

```python
import functools
import math

import jax
import jax.numpy as jnp
from jax import lax
from jax.experimental import pallas as pl
from jax.experimental.pallas import tpu as pltpu

BF = jnp.bfloat16
F32 = jnp.float32
I32 = jnp.int32
S = jax.ShapeDtypeStruct

EPS = 1e-6
NEG = -1e30
POS_PAD = 1 << 30
WINDOW = 128
BLOCK = 128
HEAD_DIM = 64
SWA_HEADS = 16
SWA_KV_HEADS = 4
SWA_GROUP = SWA_HEADS // SWA_KV_HEADS
MEM_HEADS = 4
REL_BUCKETS = 32
REL_MAX_DIST = 128
ADAM_LR = 0.001
ADAM_B1 = 0.9
ADAM_B2 = 0.999
ADAM_EPS = 1e-08
ADAM_WD = 0.01
ADAM_STEP = 10

V7X_VMEM_LIMIT_BYTES = 56 * 1024 * 1024
MESH = pl.DeviceIdType.MESH
ANY = pl.BlockSpec(memory_space=pl.ANY)
VMEM_SPEC = pl.BlockSpec(memory_space=pltpu.VMEM)
SMEM_SPEC = pl.BlockSpec(memory_space=pltpu.SMEM)


def _params(sem=None):
    return pltpu.CompilerParams(dimension_semantics=sem, vmem_limit_bytes=V7X_VMEM_LIMIT_BYTES)


def _resident(shape):
    nd = len(shape)
    return pl.BlockSpec(shape, lambda *_: (0,) * nd, pipeline_mode=pl.Buffered(1))


def _acc_spec(shape):
    nd = len(shape)
    return pl.BlockSpec(shape, lambda *_: (0,) * nd)


def _nn(a, b):
    return jnp.dot(a, b, preferred_element_type=F32)


def _nt(a, b):
    return lax.dot_general(a, b, (((1,), (1,)), ((), ())), preferred_element_type=F32)


def _tn(a, b):
    return lax.dot_general(a, b, (((0,), (0,)), ((), ())), preferred_element_type=F32)


def _sigmoid(v):
    return 1.0 / (1.0 + jnp.exp(-v))


def _rms(x):
    r = lax.rsqrt(jnp.mean(x * x, axis=-1, keepdims=True) + EPS)
    return x * r, r


def _rms_bwd(dh, n, r, g):
    dn = dh * g
    dx = r * (dn - n * jnp.mean(dn * n, axis=-1, keepdims=True))
    return dx, jnp.sum(dh * n, axis=0, keepdims=True)


def _ffn_fwd(x, gn, wgu, wd, name):
    t, d = x.shape
    f = wd.shape[0]
    tm, fc = 256, 1408

    def body(x_ref, gn_ref, wgu_ref, wd_ref, xo_ref, g_ref, u_ref):
        xv = x_ref[...]
        n, _ = _rms(xv)
        h = (n * gn_ref[...]).astype(BF)
        acc = jnp.zeros((tm, d), F32)
        for c0 in range(0, f, fc):
            g = _nn(h, wgu_ref[:, c0:c0 + fc])
            u = _nn(h, wgu_ref[:, f + c0:f + c0 + fc])
            g_ref[:, c0:c0 + fc] = g.astype(BF)
            u_ref[:, c0:c0 + fc] = u.astype(BF)
            a = (g * _sigmoid(g)) * u
            acc = acc + _nn(a.astype(BF), wd_ref[c0:c0 + fc, :])
        xo_ref[...] = xv + 0.5 * acc

    return pl.pallas_call(
        body, name=name, grid=(t // tm,),
        out_shape=(S((t, d), F32), S((t, f), BF), S((t, f), BF)),
        in_specs=[pl.BlockSpec((tm, d), lambda i: (i, 0)), _resident((1, d)), _resident(wgu.shape), _resident(wd.shape)],
        out_specs=(pl.BlockSpec((tm, d), lambda i: (i, 0)), pl.BlockSpec((tm, f), lambda i: (i, 0)),
                   pl.BlockSpec((tm, f), lambda i: (i, 0))),
        compiler_params=_params(("parallel",)),
    )(x, gn, wgu, wd)


def _mix_proj(x, gn, w_in):
    t, d = x.shape
    tm = 256
    nqkv = 1536
    ne = w_in.shape[1] - nqkv

    def body(x_ref, gn_ref, w_ref, qkv_ref, e_ref):
        n, _ = _rms(x_ref[...])
        h = (n * gn_ref[...]).astype(BF)
        qkv_ref[...] = _nn(h, w_ref[:, 0:nqkv]).astype(BF)
        for c0 in range(0, ne, 1024):
            e_ref[:, c0:c0 + 1024] = _nn(h, w_ref[:, nqkv + c0:nqkv + c0 + 1024])

    return pl.pallas_call(
        body, name="mix_proj", grid=(t // tm,),
        out_shape=(S((t, nqkv), BF), S((t, ne), F32)),
        in_specs=[pl.BlockSpec((tm, d), lambda i: (i, 0)), _resident((1, d)), _resident(w_in.shape)],
        out_specs=(pl.BlockSpec((tm, nqkv), lambda i: (i, 0)), pl.BlockSpec((tm, ne), lambda i: (i, 0))),
        compiler_params=_params(("parallel",)),
    )(x, gn, w_in)


def _t5_bucket(rel):
    n = jnp.maximum(rel, 0)
    max_exact = REL_BUCKETS // 2
    nf = jnp.maximum(n, 1).astype(F32)
    large = max_exact + (jnp.log(nf / max_exact) / math.log(REL_MAX_DIST / max_exact)
                         * (REL_BUCKETS - max_exact)).astype(I32)
    large = jnp.minimum(large, REL_BUCKETS - 1)
    return jnp.where(n < max_exact, n, large)


def _block_rel():
    i = lax.broadcasted_iota(I32, (BLOCK, 2 * BLOCK), 0)
    j = lax.broadcasted_iota(I32, (BLOCK, 2 * BLOCK), 1)
    return i + BLOCK - j


def _bias_build(rel_bias):
    def body(rb_ref, o_ref):
        bucket = _t5_bucket(_block_rel())
        for h in range(SWA_HEADS):
            acc = jnp.zeros((BLOCK, 2 * BLOCK), F32)
            for b in range(REL_BUCKETS):
                acc = jnp.where(bucket == b, rb_ref[b, h], acc)
            o_ref[h] = acc

    return pl.pallas_call(
        body, name="bias_build", out_shape=S((SWA_HEADS, BLOCK, 2 * BLOCK), F32),
        in_specs=[SMEM_SPEC], out_specs=VMEM_SPEC,
    )(rel_bias)


def _swa_visible(b, pq_ref, pkp_ref, pkc_ref):
    pk = jnp.concatenate([pkp_ref[...], pkc_ref[...]], axis=1)
    col = lax.broadcasted_iota(I32, (1, 2 * BLOCK), 1)
    pk = jnp.where(jnp.logical_and(b == 0, col < BLOCK), POS_PAD, pk)
    rel = pq_ref[...] - pk
    return jnp.logical_and(rel >= 0, rel < WINDOW)


def _swa_probs(qh, kh, vis, bias, sink):
    s = _nt(qh, kh) * (HEAD_DIM ** -0.5)
    s = jnp.where(vis, s + bias, NEG)
    m = jnp.maximum(jnp.max(s, axis=-1, keepdims=True), sink)
    p = jnp.exp(s - m)
    ps = jnp.exp(sink - m)
    inv = 1.0 / (jnp.sum(p, axis=-1, keepdims=True) + ps)
    return p * inv, ps * inv


def _swa_fwd(qkv, pos_col, pos_row, bias_t, sinks):
    t = qkv.shape[0]
    nb = t // BLOCK
    qw = SWA_HEADS * HEAD_DIM
    kw = SWA_KV_HEADS * HEAD_DIM

    def body(q_ref, kp_ref, kc_ref, vp_ref, vc_ref, pq_ref, pkp_ref, pkc_ref, bias_ref, sink_ref, o_ref):
        b = pl.program_id(0)
        vis = _swa_visible(b, pq_ref, pkp_ref, pkc_ref)
        k2 = jnp.concatenate([kp_ref[...], kc_ref[...]], axis=0)
        v2 = jnp.concatenate([vp_ref[...], vc_ref[...]], axis=0)
        for hk in range(SWA_KV_HEADS):
            kh = k2[:, hk * HEAD_DIM:(hk + 1) * HEAD_DIM]
            vh = v2[:, hk * HEAD_DIM:(hk + 1) * HEAD_DIM]
            for g in range(SWA_GROUP):
                h = hk * SWA_GROUP + g
                qh = q_ref[:, h * HEAD_DIM:(h + 1) * HEAD_DIM]
                pn, _ = _swa_probs(qh, kh, vis, bias_ref[h], sink_ref[0, h])
                o_ref[:, h * HEAD_DIM:(h + 1) * HEAD_DIM] = _nn(pn.astype(BF), vh)

    prev = lambda b: jnp.maximum(b - 1, 0)
    return pl.pallas_call(
        body, name="swa_fwd", grid=(nb,), out_shape=S((t, qw), F32),
        in_specs=[
            pl.BlockSpec((BLOCK, qw), lambda b: (b, 0)),
            pl.BlockSpec((BLOCK, kw), lambda b: (prev(b), qw // kw)),
            pl.BlockSpec((BLOCK, kw), lambda b: (b, qw // kw)),
            pl.BlockSpec((BLOCK, kw), lambda b: (prev(b), qw // kw + 1)),
            pl.BlockSpec((BLOCK, kw), lambda b: (b, qw // kw + 1)),
            pl.BlockSpec((BLOCK, 1), lambda b: (b, 0)),
            pl.BlockSpec((1, BLOCK), lambda b: (0, prev(b))),
            pl.BlockSpec((1, BLOCK), lambda b: (0, b)),
            _resident(bias_t.shape),
            SMEM_SPEC,
        ],
        out_specs=pl.BlockSpec((BLOCK, qw), lambda b: (b, 0)),
        compiler_params=_params(("parallel",)),
    )(qkv, qkv, qkv, qkv, qkv, pos_col, pos_row, pos_row, bias_t, sinks)


def _conv_taps(z, zh, first):
    tm = z.shape[0]
    zh = jnp.where(first, 0.0, zh)
    row = lax.broadcasted_iota(I32, (tm, 1), 0)
    z1 = jnp.where(row == 0, zh[7:8, :], pltpu.roll(z, 1, 0))
    z2 = jnp.where(row == 0, zh[6:7, :], jnp.where(row == 1, zh[7:8, :], pltpu.roll(z, 2, 0)))
    return z1, z2


def _mix_out_fwd(e, attn, conv_w, w_out, x):
    t, d = x.shape
    tm = 256
    hb = tm // 8

    def body(c_ref, b_ref, u_ref, ga_ref, gc_ref, ch_ref, uh_ref, attn_ref, cw_ref, w_ref, x_ref, xo_ref, mg_ref):
        i = pl.program_id(0)
        z = c_ref[...] * u_ref[...]
        z1, z2 = _conv_taps(z, ch_ref[...] * uh_ref[...], i == 0)
        s = cw_ref[0:1, :] * z2 + cw_ref[1:2, :] * z1 + cw_ref[2:3, :] * z
        conv = b_ref[...] * s
        merged = (_sigmoid(ga_ref[...]) * attn_ref[...] + _sigmoid(gc_ref[...]) * conv).astype(BF)
        mg_ref[...] = merged
        xo_ref[...] = x_ref[...] + _nn(merged, w_ref[...])

    ecol = lambda cb: pl.BlockSpec((tm, d), lambda i: (i, cb))
    halo = lambda cb: pl.BlockSpec((8, d), lambda i: (jnp.maximum(i * hb - 1, 0), cb))
    row = pl.BlockSpec((tm, d), lambda i: (i, 0))
    return pl.pallas_call(
        body, name="mix_out_fwd", grid=(t // tm,),
        out_shape=(S((t, d), F32), S((t, d), BF)),
        in_specs=[ecol(0), ecol(1), ecol(2), ecol(3), ecol(4), halo(0), halo(2), row,
                  _resident(conv_w.shape), _resident(w_out.shape), row],
        out_specs=(row, row),
        compiler_params=_params(("parallel",)),
    )(e, e, e, e, e, e, e, attn, conv_w, w_out, x)


def _mem_kv(mem, gm, wkv):
    m, d = mem.shape

    def body(mem_ref, gm_ref, w_ref, mh_ref, kv_ref):
        n, _ = _rms(mem_ref[...])
        mh = (n * gm_ref[...]).astype(BF)
        mh_ref[...] = mh
        kv_ref[...] = _nn(mh, w_ref[...]).astype(BF)

    return pl.pallas_call(
        body, name="mem_kv", out_shape=(S((m, d), BF), S((m, wkv.shape[1]), BF)),
        compiler_params=_params(),
    )(mem, gm, wkv)


def _xattn_probs(qh, kh):
    s = _nt(qh, kh) * (kh.shape[1] ** -0.5)
    p = jnp.exp(s - jnp.max(s, axis=-1, keepdims=True))
    return p * (1.0 / jnp.sum(p, axis=-1, keepdims=True))


def _xattn_fwd(x, gn, wq, kv, wo):
    t, d = x.shape
    tm = 256
    hd = d // MEM_HEADS

    def body(x_ref, gn_ref, wq_ref, kv_ref, wo_ref, xo_ref, q_ref, o_ref):
        xv = x_ref[...]
        n, _ = _rms(xv)
        q = _nn((n * gn_ref[...]).astype(BF), wq_ref[...]).astype(BF)
        q_ref[...] = q
        outs = []
        for hh in range(MEM_HEADS):
            p = _xattn_probs(q[:, hh * hd:(hh + 1) * hd], kv_ref[:, hh * hd:(hh + 1) * hd])
            outs.append(_nn(p.astype(BF), kv_ref[:, d + hh * hd:d + (hh + 1) * hd]))
        o = jnp.concatenate(outs, axis=1).astype(BF)
        o_ref[...] = o
        xo_ref[...] = xv + _nn(o, wo_ref[...])

    row = pl.BlockSpec((tm, d), lambda i: (i, 0))
    return pl.pallas_call(
        body, name="xattn_fwd", grid=(t // tm,),
        out_shape=(S((t, d), F32), S((t, d), BF), S((t, d), BF)),
        in_specs=[row, _resident((1, d)), _resident(wq.shape), _resident(kv.shape), _resident(wo.shape)],
        out_specs=(row, row, row),
        compiler_params=_params(("parallel",)),
    )(x, gn, wq, kv, wo)


def _loss_bwd(x, gf, target):
    t, d = x.shape
    tm = 512

    def body(x_ref, gf_ref, t_ref, dx_ref, dg_ref, loss_ref):
        @pl.when(pl.program_id(0) == 0)
        def _():
            dg_ref[...] = jnp.zeros_like(dg_ref)
            loss_ref[...] = jnp.zeros_like(loss_ref)
        n, r = _rms(x_ref[...])
        g = gf_ref[...]
        err = n * g - t_ref[...]
        loss_ref[...] += 0.5 * jnp.sum(jnp.sum(err * err, axis=-1, keepdims=True) / d, axis=0, keepdims=True)
        dx, dg = _rms_bwd(err / d, n, r, g)
        dx_ref[...] = dx
        dg_ref[...] += dg

    row = pl.BlockSpec((tm, d), lambda i: (i, 0))
    return pl.pallas_call(
        body, name="loss_bwd", grid=(t // tm,),
        out_shape=(S((t, d), F32), S((1, d), F32), S((1, 128), F32)),
        in_specs=[row, _resident((1, d)), row],
        out_specs=(row, _acc_spec((1, d)), _acc_spec((1, 128))),
        compiler_params=_params(("arbitrary",)),
    )(x, gf, target)


def _ffn_bwd(dxo, x, gn, g, u, wgu, wd, name):
    t, d = x.shape
    f = wd.shape[0]
    tm, fc = 256, 1408

    def body(dxo_ref, x_ref, gn_ref, g_ref, u_ref, wgu_ref, wd_ref, dx_ref, dgn_ref, dgu_ref, a_ref, h_ref, dyh_ref):
        @pl.when(pl.program_id(0) == 0)
        def _():
            dgn_ref[...] = jnp.zeros_like(dgn_ref)
        dxov = dxo_ref[...]
        dyh = (0.5 * dxov).astype(BF)
        dyh_ref[...] = dyh
        n, r = _rms(x_ref[...])
        gnv = gn_ref[...]
        h_ref[...] = (n * gnv).astype(BF)
        dh = jnp.zeros((tm, d), F32)
        for c0 in range(0, f, fc):
            gv = g_ref[:, c0:c0 + fc].astype(F32)
            uv = u_ref[:, c0:c0 + fc].astype(F32)
            da = _nt(dyh, wd_ref[c0:c0 + fc, :])
            sg = _sigmoid(gv)
            silu = gv * sg
            a_ref[:, c0:c0 + fc] = (silu * uv).astype(BF)
            dg = (da * uv * (sg * (1.0 + gv * (1.0 - sg)))).astype(BF)
            du = (da * silu).astype(BF)
            dgu_ref[:, c0:c0 + fc] = dg
            dgu_ref[:, f + c0:f + c0 + fc] = du
            dh = dh + _nt(dg, wgu_ref[:, c0:c0 + fc]) + _nt(du, wgu_ref[:, f + c0:f + c0 + fc])
        dx, dgn = _rms_bwd(dh, n, r, gnv)
        dx_ref[...] = dxov + dx
        dgn_ref[...] += dgn

    row = pl.BlockSpec((tm, d), lambda i: (i, 0))
    frow = pl.BlockSpec((tm, f), lambda i: (i, 0))
    return pl.pallas_call(
        body, name=name, grid=(t // tm,),
        out_shape=(S((t, d), F32), S((1, d), F32), S((t, 2 * f), BF), S((t, f), BF), S((t, d), BF), S((t, d), BF)),
        in_specs=[row, row, _resident((1, d)), frow, frow, _resident(wgu.shape), _resident(wd.shape)],
        out_specs=(row, _acc_spec((1, d)), pl.BlockSpec((tm, 2 * f), lambda i: (i, 0)), frow, row, row),
        compiler_params=_params(("arbitrary",)),
    )(dxo, x, gn, g, u, wgu, wd)


def _dw(a, b, tn, name, into=None, col_block=0, ncols=None):
    t, ka = a.shape
    nb = b.shape[1]
    tt = 1024
    nt = t // tt
    ncols = nb if ncols is None else ncols

    def body(*refs):
        a_ref, b_ref = refs[0], refs[1]
        o_ref, acc_ref = refs[-2], refs[-1]
        k = pl.program_id(1)

        @pl.when(k == 0)
        def _():
            acc_ref[...] = jnp.zeros_like(acc_ref)
        acc_ref[...] += _tn(a_ref[...], b_ref[...])

        @pl.when(k == nt - 1)
        def _():
            o_ref[...] = acc_ref[...].astype(BF)

    in_specs = [pl.BlockSpec((tt, ka), lambda j, k: (k, 0)), pl.BlockSpec((tt, tn), lambda j, k: (k, j))]
    args = [a, b]
    aliases = {}
    if into is not None:
        in_specs.append(ANY)
        args.append(into)
        aliases = {2: 0}
    return pl.pallas_call(
        body, name=name, grid=(nb // tn, nt), out_shape=S((ka, ncols), BF),
        in_specs=in_specs, out_specs=pl.BlockSpec((ka, tn), lambda j, k: (0, j + col_block)),
        scratch_shapes=[pltpu.VMEM((ka, tn), F32)], input_output_aliases=aliases,
        compiler_params=_params(("parallel", "arbitrary")),
    )(*args)


def _xattn_bwd(dxo, x, gn, q, o_unused, kv, wq, wo):
    t, d = x.shape
    tm = 256
    hd = d // MEM_HEADS
    nkv = kv.shape[0]

    def body(dxo_ref, x_ref, gn_ref, q_ref, kv_ref, wq_ref, wo_ref, dx_ref, dgn_ref, dkv_ref, dxh_ref, h_ref, dq_ref):
        @pl.when(pl.program_id(0) == 0)
        def _():
            dgn_ref[...] = jnp.zeros_like(dgn_ref)
            dkv_ref[...] = jnp.zeros_like(dkv_ref)
        dxov = dxo_ref[...]
        dxh = dxov.astype(BF)
        dxh_ref[...] = dxh
        do = _nt(dxh, wo_ref[...]).astype(BF)
        dqs = []
        for hh in range(MEM_HEADS):
            lo, hi = hh * hd, (hh + 1) * hd
            qh = q_ref[:, lo:hi]
            kh = kv_ref[:, lo:hi]
            vh = kv_ref[:, d + lo:d + hi]
            doh = do[:, lo:hi]
            p = _xattn_probs(qh, kh)
            dp = _nt(doh, vh)
            ds = (p * (dp - jnp.sum(p * dp, axis=-1, keepdims=True)) * (hd ** -0.5)).astype(BF)
            dqs.append(_nn(ds, kh))
            dkv_ref[:, lo:hi] += _tn(ds, qh)
            dkv_ref[:, d + lo:d + hi] += _tn(p.astype(BF), doh)
        dq = jnp.concatenate(dqs, axis=1).astype(BF)
        dq_ref[...] = dq
        n, r = _rms(x_ref[...])
        gnv = gn_ref[...]
        h_ref[...] = (n * gnv).astype(BF)
        dx, dgn = _rms_bwd(_nt(dq, wq_ref[...]), n, r, gnv)
        dx_ref[...] = dxov + dx
        dgn_ref[...] += dgn

    row = pl.BlockSpec((tm, d), lambda i: (i, 0))
    return pl.pallas_call(
        body, name="xattn_bwd", grid=(t // tm,),
        out_shape=(S((t, d), F32), S((1, d), F32), S((nkv, 2 * d), F32), S((t, d), BF), S((t, d), BF), S((t, d), BF)),
        in_specs=[row, row, _resident((1, d)), row, _resident(kv.shape), _resident(wq.shape), _resident(wo.shape)],
        out_specs=(row, _acc_spec((1, d)), _acc_spec((nkv, 2 * d)), row, row, row),
        compiler_params=_params(("arbitrary",)),
    )(dxo, x, gn, q, kv, wq, wo)


def _mem_bwd(dkv, mh, mem, gm, wkv):
    m, d = mem.shape

    def body(dkv_ref, mh_ref, mem_ref, gm_ref, w_ref, dw_ref, dgm_ref):
        dkvb = dkv_ref[...].astype(BF)
        dw_ref[...] = _tn(mh_ref[...], dkvb).astype(BF)
        dmh = _nt(dkvb, w_ref[...])
        n, _ = _rms(mem_ref[...])
        dgm_ref[...] = jnp.sum(dmh * n, axis=0, keepdims=True)

    return pl.pallas_call(
        body, name="mem_bwd", out_shape=(S(wkv.shape, BF), S((1, d), F32)),
        compiler_params=_params(),
    )(dkv, mh, mem, gm, wkv)


def _mix_out_bwd(dxo, e, attn, conv_w, w_out):
    t, d = attn.shape
    tm = 256
    hb = tm // 8
    nt = t // tm
    last8 = t // 8 - 1

    def body(dxo_ref, dxn_ref, c_ref, b_ref, u_ref, ga_ref, gc_ref, ch_ref, uh_ref, bn_ref, gcn_ref,
             attn_ref, cw_ref, w_ref, dattn_ref, de_ref, dcw_ref, dxh_ref):
        i = pl.program_id(0)

        @pl.when(i == 0)
        def _():
            dcw_ref[...] = jnp.zeros_like(dcw_ref)
        dxh = dxo_ref[...].astype(BF)
        dxh_ref[...] = dxh
        w = w_ref[...]
        dm = _nt(dxh, w)
        dmn = _nt(dxn_ref[...].astype(BF), w)
        cv, bv, uv = c_ref[...], b_ref[...], u_ref[...]
        sga = _sigmoid(ga_ref[...])
        sgc = _sigmoid(gc_ref[...])
        z = cv * uv
        z1, z2 = _conv_taps(z, ch_ref[...] * uh_ref[...], i == 0)
        w0, w1, w2 = cw_ref[0:1, :], cw_ref[1:2, :], cw_ref[2:3, :]
        s = w0 * z2 + w1 * z1 + w2 * z
        av = attn_ref[...]
        dattn_ref[...] = (dm * sga).astype(BF)
        dconv = dm * sgc
        ds = dconv * bv
        dsn = jnp.where(i == nt - 1, 0.0, dmn * _sigmoid(gcn_ref[...]) * bn_ref[...])
        row = lax.broadcasted_iota(I32, (tm, 1), 0)
        dsp1 = jnp.where(row == tm - 1, dsn[0:1, :], pltpu.roll(ds, tm - 1, 0))
        dsp2 = jnp.where(row == tm - 2, dsn[0:1, :], jnp.where(row == tm - 1, dsn[1:2, :], pltpu.roll(ds, tm - 2, 0)))
        dz = w2 * ds + w1 * dsp1 + w0 * dsp2
        de_ref[:, 0:d] = (dz * uv).astype(BF)
        de_ref[:, d:2 * d] = (dconv * s).astype(BF)
        de_ref[:, 2 * d:3 * d] = (dz * cv).astype(BF)
        de_ref[:, 3 * d:4 * d] = (dm * av * sga * (1.0 - sga)).astype(BF)
        de_ref[:, 4 * d:5 * d] = (dm * (bv * s) * sgc * (1.0 - sgc)).astype(BF)
        dcw_ref[0:1, :] += jnp.sum(ds * z2, axis=0, keepdims=True)
        dcw_ref[1:2, :] += jnp.sum(ds * z1, axis=0, keepdims=True)
        dcw_ref[2:3, :] += jnp.sum(ds * z, axis=0, keepdims=True)

    ecol = lambda cb: pl.BlockSpec((tm, d), lambda i: (i, cb))
    prev = lambda cb: pl.BlockSpec((8, d), lambda i: (jnp.maximum(i * hb - 1, 0), cb))
    nxt = lambda cb: pl.BlockSpec((8, d), lambda i: (jnp.minimum((i + 1) * hb, last8), cb))
    row = pl.BlockSpec((tm, d), lambda i: (i, 0))
    return pl.pallas_call(
        body, name="mix_out_bwd", grid=(nt,),
        out_shape=(S((t, d), BF), S((t, 5 * d), BF), S((8, d), F32), S((t, d), BF)),
        in_specs=[row, nxt(0), ecol(0), ecol(1), ecol(2), ecol(3), ecol(4), prev(0), prev(2), nxt(1), nxt(4),
                  row, _resident(conv_w.shape), _resident(w_out.shape)],
        out_specs=(row, pl.BlockSpec((tm, 5 * d), lambda i: (i, 0)), _acc_spec((8, d)), row),
        compiler_params=_params(("arbitrary",)),
    )(dxo, dxo, e, e, e, e, e, e, e, e, e, attn, conv_w, w_out)


def _swa_bwd(qkv, dattn, pos_col, pos_row, bias_t, sinks):
    t = qkv.shape[0]
    nb = t // BLOCK
    qw = SWA_HEADS * HEAD_DIM
    kw = SWA_KV_HEADS * HEAD_DIM

    def body(q_ref, kp_ref, kc_ref, vp_ref, vc_ref, do_ref, pq_ref, pkp_ref, pkc_ref, bias_ref, sink_ref,
             dq_ref, dkv_ref, gb_ref, dsk_ref, carry_ref):
        b = pl.program_id(0)

        @pl.when(b == 0)
        def _():
            gb_ref[...] = jnp.zeros_like(gb_ref)
            dsk_ref[...] = jnp.zeros_like(dsk_ref)
            carry_ref[...] = jnp.zeros_like(carry_ref)

        @pl.when(b < nb)
        def _():
            vis = _swa_visible(b, pq_ref, pkp_ref, pkc_ref)
            k2 = jnp.concatenate([kp_ref[...], kc_ref[...]], axis=0)
            v2 = jnp.concatenate([vp_ref[...], vc_ref[...]], axis=0)
            for hk in range(SWA_KV_HEADS):
                lo, hi = hk * HEAD_DIM, (hk + 1) * HEAD_DIM
                kh = k2[:, lo:hi]
                vh = v2[:, lo:hi]
                dk = jnp.zeros((2 * BLOCK, HEAD_DIM), F32)
                dv = jnp.zeros((2 * BLOCK, HEAD_DIM), F32)
                for g in range(SWA_GROUP):
                    h = hk * SWA_GROUP + g
                    hs = slice(h * HEAD_DIM, (h + 1) * HEAD_DIM)
                    qh = q_ref[:, hs]
                    doh = do_ref[:, hs]
                    pn, psn = _swa_probs(qh, kh, vis, bias_ref[h], sink_ref[0, h])
                    dp = _nt(doh, vh)
                    delta = jnp.sum(pn * dp, axis=-1, keepdims=True)
                    ds = pn * (dp - delta)
                    gb_ref[h] += ds
                    dsk_ref[:, h:h + 1] += -psn * delta
                    dsb = (ds * (HEAD_DIM ** -0.5)).astype(BF)
                    dq_ref[:, hs] = _nn(dsb, kh).astype(BF)
                    dk = dk + _tn(dsb, qh)
                    dv = dv + _tn(pn.astype(BF), doh)
                dkv_ref[:, lo:hi] = (carry_ref[:, lo:hi] + dk[0:BLOCK]).astype(BF)
                dkv_ref[:, kw + lo:kw + hi] = (carry_ref[:, kw + lo:kw + hi] + dv[0:BLOCK]).astype(BF)
                carry_ref[:, lo:hi] = dk[BLOCK:2 * BLOCK]
                carry_ref[:, kw + lo:kw + hi] = dv[BLOCK:2 * BLOCK]

        @pl.when(b == nb)
        def _():
            dkv_ref[...] = carry_ref[...].astype(BF)

    cur = lambda b: jnp.minimum(b, nb - 1)
    prev = lambda b: jnp.maximum(cur(b) - 1, 0)
    return pl.pallas_call(
        body, name="swa_bwd", grid=(nb + 1,),
        out_shape=(S((t, qw), BF), S((t, 2 * kw), BF), S((SWA_HEADS, BLOCK, 2 * BLOCK), F32), S((BLOCK, SWA_HEADS), F32)),
        in_specs=[
            pl.BlockSpec((BLOCK, qw), lambda b: (cur(b), 0)),
            pl.BlockSpec((BLOCK, kw), lambda b: (prev(b), qw // kw)),
            pl.BlockSpec((BLOCK, kw), lambda b: (cur(b), qw // kw)),
            pl.BlockSpec((BLOCK, kw), lambda b: (prev(b), qw // kw + 1)),
            pl.BlockSpec((BLOCK, kw), lambda b: (cur(b), qw // kw + 1)),
            pl.BlockSpec((BLOCK, qw), lambda b: (cur(b), 0)),
            pl.BlockSpec((BLOCK, 1), lambda b: (cur(b), 0)),
            pl.BlockSpec((1, BLOCK), lambda b: (0, prev(b))),
            pl.BlockSpec((1, BLOCK), lambda b: (0, cur(b))),
            _resident(bias_t.shape),
            SMEM_SPEC,
        ],
        out_specs=(
            pl.BlockSpec((BLOCK, qw), lambda b: (cur(b), 0)),
            pl.BlockSpec((BLOCK, 2 * kw), lambda b: (jnp.maximum(b - 1, 0), 0)),
            _acc_spec((SWA_HEADS, BLOCK, 2 * BLOCK)),
            _acc_spec((BLOCK, SWA_HEADS)),
        ),
        scratch_shapes=[pltpu.VMEM((BLOCK, 2 * kw), F32)],
        compiler_params=_params(("arbitrary",)),
    )(qkv, qkv, qkv, qkv, qkv, dattn, pos_col, pos_row, pos_row, bias_t, sinks)


def _bias_reduce(gb, dsk):
    def body(gb_ref, dsk_ref, drb_ref, dsink_ref):
        bucket = _t5_bucket(_block_rel())
        for b in range(REL_BUCKETS):
            mask = bucket == b
            for h in range(SWA_HEADS):
                drb_ref[b, h] = jnp.sum(jnp.where(mask, gb_ref[h], 0.0))
        for h in range(SWA_HEADS):
            dsink_ref[0, h] = jnp.sum(dsk_ref[:, h:h + 1])

    return pl.pallas_call(
        body, name="bias_reduce", out_shape=(S((REL_BUCKETS, SWA_HEADS), F32), S((1, SWA_HEADS), F32)),
        in_specs=[VMEM_SPEC, VMEM_SPEC], out_specs=(SMEM_SPEC, SMEM_SPEC),
    )(gb, dsk)


def _mix_in_bwd(dq, dkv, de, x, gn, w_in, dxo):
    t, d = x.shape
    tm = 256
    nq, nkv, ne = dq.shape[1], dkv.shape[1], de.shape[1]

    def body(dq_ref, dkv_ref, de_ref, x_ref, gn_ref, w_ref, dxo_ref, dx_ref, dgn_ref, h_ref):
        @pl.when(pl.program_id(0) == 0)
        def _():
            dgn_ref[...] = jnp.zeros_like(dgn_ref)
        dh = _nt(dq_ref[...], w_ref[:, 0:nq]) + _nt(dkv_ref[...], w_ref[:, nq:nq + nkv])
        for c0 in range(0, ne, 1024):
            dh = dh + _nt(de_ref[:, c0:c0 + 1024], w_ref[:, nq + nkv + c0:nq + nkv + c0 + 1024])
        n, r = _rms(x_ref[...])
        gnv = gn_ref[...]
        h_ref[...] = (n * gnv).astype(BF)
        dx, dgn = _rms_bwd(dh, n, r, gnv)
        dx_ref[...] = dxo_ref[...] + dx
        dgn_ref[...] += dgn

    row = pl.BlockSpec((tm, d), lambda i: (i, 0))
    wide = lambda w: pl.BlockSpec((tm, w), lambda i: (i, 0))
    return pl.pallas_call(
        body, name="mix_in_bwd", grid=(t // tm,),
        out_shape=(S((t, d), F32), S((1, d), F32), S((t, d), BF)),
        in_specs=[wide(nq), wide(nkv), wide(ne), row, _resident((1, d)), _resident(w_in.shape), row],
        out_specs=(row, _acc_spec((1, d)), row),
        compiler_params=_params(("arbitrary",)),
    )(dq, dkv, de, x, gn, w_in, dxo)


def _row_tile(rows):
    for tr in (256, 352, 128, 64, 16, 8):
        if rows % tr == 0:
            return tr
    return rows


def _add_bf16(a, b, name):
    rows, cols = a.shape
    tr = _row_tile(rows)

    def body(a_ref, b_ref, o_ref):
        o_ref[...] = (a_ref[...].astype(F32) + b_ref[...].astype(F32)).astype(BF)

    blk = pl.BlockSpec((tr, cols), lambda i: (i, 0))
    return pl.pallas_call(body, name=name, grid=(rows // tr,), out_shape=S((rows, cols), BF),
                          in_specs=[blk, blk], out_specs=blk, compiler_params=_params(("parallel",)))(a, b)


def _sum4(parts, name):
    _, rows, cols = parts.shape
    tr = _row_tile(rows)

    def body(p_ref, o_ref):
        acc = p_ref[3].astype(F32)
        for k in range(3):
            acc = acc + p_ref[k].astype(F32)
        o_ref[...] = acc

    return pl.pallas_call(
        body, name=name, grid=(rows // tr,), out_shape=S((rows, cols), F32),
        in_specs=[pl.BlockSpec((4, tr, cols), lambda i: (0, i, 0))], out_specs=pl.BlockSpec((tr, cols), lambda i: (i, 0)),
        compiler_params=_params(("parallel",)))(parts)


def _adamw(w, g, m, v, name):
    rows, cols = w.shape
    tr = _row_tile(rows)

    def body(w_ref, g_ref, m_ref, v_ref, d_ref, mo_ref, vo_ref):
        gv = g_ref[...]
        mn = ADAM_B1 * m_ref[...] + (1.0 - ADAM_B1) * gv
        vn = ADAM_B2 * v_ref[...] + (1.0 - ADAM_B2) * (gv * gv)
        m_hat = mn / (1.0 - ADAM_B1 ** ADAM_STEP)
        v_hat = vn / (1.0 - ADAM_B2 ** ADAM_STEP)
        d_ref[...] = -ADAM_LR * (m_hat / (jnp.sqrt(v_hat) + ADAM_EPS) + ADAM_WD * w_ref[...])
        mo_ref[...] = mn
        vo_ref[...] = vn

    blk = pl.BlockSpec((tr, cols), lambda i: (i, 0))
    shp = S((rows, cols), F32)
    return pl.pallas_call(body, name=name, grid=(rows // tr,), out_shape=(shp, shp, shp),
                          in_specs=[blk] * 4, out_specs=(blk,) * 3, compiler_params=_params(("parallel",)))(w, g, m, v)


def _place():
    x, y, c = lax.axis_index("x"), lax.axis_index("y"), lax.axis_index("c")
    return x, y, c


OTHER_CHIPS = ((1, 0), (0, 1), (1, 1))


def _flip(v, f):
    return 1 - v if f else v


def _remote(src, dst, ssem, rsem, dev):
    return pltpu.make_async_remote_copy(src_ref=src, dst_ref=dst, send_sem=ssem, recv_sem=rsem,
                                        device_id=dev, device_id_type=MESH)


class _Stager:
    def __init__(self, buf, lsem, ssem):
        self.buf, self.lsem, self.ssem = buf, lsem, ssem
        self.slots = buf.shape[0]
        self.count = 0
        self.inflight = [None] * self.slots

    def send(self, src, dst, rsem, dev):
        s = self.count % self.slots
        self.count += 1
        if self.inflight[s] is not None:
            self.inflight[s].wait_send()
        stage = self.buf.at[(s,) + tuple(pl.ds(0, n) for n in src.shape)]
        load = pltpu.make_async_copy(src, stage, self.lsem.at[s])
        load.start()
        load.wait()
        cp = _remote(stage, dst, self.ssem.at[s], rsem, dev)
        cp.start()
        self.inflight[s] = cp

    def drain(self):
        for cp in self.inflight:
            if cp is not None:
                cp.wait_send()


STAGE_SLOTS = 3
STAGE_ROWS = 176


def _chunk_rows(rows):
    for rc in (STAGE_ROWS, 128, 64, 32, 16):
        if rows % rc == 0:
            return rc
    raise ValueError(rows)


def _stage_scratch(cols, dtype):
    return [pltpu.VMEM((STAGE_SLOTS, STAGE_ROWS, cols), dtype), pltpu.SemaphoreType.DMA((STAGE_SLOTS,)),
            pltpu.SemaphoreType.DMA((STAGE_SLOTS,))]


def _all_gather_weights(shards, kinds):
    n = len(shards)
    out_shape = tuple(
        S((w.shape[0], 4 * w.shape[1]), BF) if k == "col" else S((4,) + w.shape, BF) for w, k in zip(shards, kinds))

    def body(*refs):
        ins, outs = refs[:n], refs[n:2 * n]
        ssem, rsem, frsem, lsem, buf, blsem, bssem = refs[2 * n:]
        x, y, c = _place()
        s_me = 2 * x + y
        sibling = (x, y, 1 - c)
        stager = _Stager(buf, blsem, bssem)

        def window(i, s, half, r0=0, nrows=None):
            rows, cols = shards[i].shape
            rh = rows // 2
            nrows = rh if nrows is None else nrows
            start = pl.multiple_of(half * rh + r0, 16)
            if kinds[i] == "col":
                return outs[i].at[pl.ds(start, nrows), pl.ds(pl.multiple_of(s * cols, 128), cols)]
            return outs[i].at[s, pl.ds(start, nrows), :]

        pending = []
        for i in range(n):
            rows, cols = shards[i].shape
            rh = rows // 2
            whole = (outs[i].at[:, pl.ds(pl.multiple_of(s_me * cols, 128), cols)] if kinds[i] == "col"
                     else outs[i].at[s_me])
            cp = pltpu.make_async_copy(ins[i], whole, lsem.at[i])
            cp.start()
            pending.append(cp.wait)
            src = ins[i].at[pl.ds(pl.multiple_of(c * rh, 16), rh), :]
            for j, (fx, fy) in enumerate(OTHER_CHIPS):
                cp = _remote(src, window(i, s_me, c), ssem.at[i, j], rsem.at[i, j], (_flip(x, fx), _flip(y, fy), c))
                cp.start()
                pending.append(cp.wait_send)
        for i in range(n):
            rh = shards[i].shape[0] // 2
            rc = _chunk_rows(rh)
            for j, (fx, fy) in enumerate(OTHER_CHIPS):
                px, py = _flip(x, fx), _flip(y, fy)
                landed = window(i, 2 * px + py, c)
                _remote(landed, landed, ssem.at[i, j], rsem.at[i, j], (px, py, c)).wait_recv()
                for r0 in range(0, rh, rc):
                    part = window(i, 2 * px + py, c, r0, rc)
                    stager.send(part, part, frsem.at[i, j], sibling)
        for i in range(n):
            for j, (fx, fy) in enumerate(OTHER_CHIPS):
                passed = window(i, 2 * _flip(x, fx) + _flip(y, fy), 1 - c)
                _remote(passed, passed, ssem.at[i, j], frsem.at[i, j], sibling).wait_recv()
        stager.drain()
        for wait in pending:
            wait()

    max_cols = max(w.shape[1] for w in shards)
    sems = [pltpu.SemaphoreType.DMA((n, 3)) for _ in range(3)] + [pltpu.SemaphoreType.DMA((n,))]
    return pl.pallas_call(
        body, name="all_gather_weights", out_shape=out_shape,
        in_specs=[ANY] * n, out_specs=tuple(ANY for _ in range(n)), scratch_shapes=sems + _stage_scratch(max_cols, BF),
        compiler_params=_params(),
    )(*shards)


def _reduce_siblings(grads, kinds):
    n = len(grads)

    def half_shape(gr, k):
        return S((gr.shape[0] // 2, gr.shape[1]), BF) if k == "col" else S((4, gr.shape[1] // 2, gr.shape[2]), BF)

    halves = tuple(half_shape(gr, k) for gr, k in zip(grads, kinds))

    def body(*refs):
        ins, mine, sib = refs[:n], refs[n:2 * n], refs[2 * n:3 * n]
        rsem, lsem, buf, blsem, bssem = refs[3 * n:]
        x, y, c = _place()
        sibling = (x, y, 1 - c)
        stager = _Stager(buf, blsem, bssem)
        pending = []
        for i in range(n):
            col = kinds[i] == "col"
            rh = (grads[i].shape[0] if col else grads[i].shape[1]) // 2
            rc = _chunk_rows(rh)
            keep = ins[i].at[pl.ds(pl.multiple_of(c * rh, 16), rh), :] if col else ins[i].at[:, pl.ds(pl.multiple_of(c * rh, 16), rh), :]
            cp = pltpu.make_async_copy(keep, mine[i], lsem.at[i])
            cp.start()
            pending.append(cp.wait)
            for r0 in range(0, rh, rc):
                start = pl.multiple_of((1 - c) * rh + r0, 16)
                if col:
                    stager.send(ins[i].at[pl.ds(start, rc), :], sib[i].at[pl.ds(r0, rc), :], rsem.at[i], sibling)
                else:
                    for s in range(4):
                        stager.send(ins[i].at[s, pl.ds(start, rc), :], sib[i].at[s, pl.ds(r0, rc), :], rsem.at[i], sibling)
        for i in range(n):
            _remote(sib[i], sib[i], lsem.at[i], rsem.at[i], sibling).wait_recv()
        stager.drain()
        for wait in pending:
            wait()

    max_cols = max(gr.shape[-1] for gr in grads)
    sems = [pltpu.SemaphoreType.DMA((n,)), pltpu.SemaphoreType.DMA((n,))]
    return pl.pallas_call(
        body, name="reduce_siblings", out_shape=halves + halves,
        in_specs=[ANY] * n, out_specs=tuple(ANY for _ in range(2 * n)), scratch_shapes=sems + _stage_scratch(max_cols, BF),
        compiler_params=_params(),
    )(*grads)


def _reduce_chips(parts, kinds):
    n = len(parts)

    def slab_shape(p, k):
        return S((4, p.shape[0], p.shape[1] // 4), BF) if k == "col" else S(p.shape, BF)

    out_shape = tuple(slab_shape(p, k) for p, k in zip(parts, kinds))

    def body(*refs):
        ins, outs = refs[:n], refs[n:2 * n]
        ssem, rsem, lsem = refs[2 * n:]
        x, y, c = _place()
        s_me = 2 * x + y

        def slab(i, s):
            if kinds[i] == "col":
                cols = parts[i].shape[1] // 4
                return ins[i].at[:, pl.ds(pl.multiple_of(s * cols, 128), cols)]
            return ins[i].at[s]

        pending = []
        for i in range(n):
            cp = pltpu.make_async_copy(slab(i, s_me), outs[i].at[3], lsem.at[i])
            cp.start()
            pending.append(cp.wait)
            for j, (fx, fy) in enumerate(OTHER_CHIPS):
                px, py = _flip(x, fx), _flip(y, fy)
                cp = _remote(slab(i, 2 * px + py), outs[i].at[j], ssem.at[i, j], rsem.at[i, j], (px, py, c))
                cp.start()
                pending.append(cp.wait)
        for wait in pending:
            wait()

    sems = [pltpu.SemaphoreType.DMA((n, 3)), pltpu.SemaphoreType.DMA((n, 3)), pltpu.SemaphoreType.DMA((n,))]
    return pl.pallas_call(
        body, name="reduce_chips", out_shape=out_shape,
        in_specs=[ANY] * n, out_specs=tuple(ANY for _ in range(n)), scratch_shapes=sems,
    )(*parts)


def _share_halves(halves):
    n = len(halves)
    out_shape = tuple(S((2 * h.shape[0], h.shape[1]), F32) for h in halves)

    def body(*refs):
        ins, outs = refs[:n], refs[n:2 * n]
        rsem, lsem, buf, blsem, bssem = refs[2 * n:]
        x, y, c = _place()
        sibling = (x, y, 1 - c)
        stager = _Stager(buf, blsem, bssem)
        pending = []
        for i in range(n):
            rh = halves[i].shape[0]
            rc = _chunk_rows(rh)
            cp = pltpu.make_async_copy(ins[i], outs[i].at[pl.ds(pl.multiple_of(c * rh, 16), rh), :], lsem.at[i])
            cp.start()
            pending.append(cp.wait)
            for r0 in range(0, rh, rc):
                dst = outs[i].at[pl.ds(pl.multiple_of(c * rh + r0, 16), rc), :]
                stager.send(ins[i].at[pl.ds(r0, rc), :], dst, rsem.at[i], sibling)
        for i in range(n):
            rh = halves[i].shape[0]
            got = outs[i].at[pl.ds(pl.multiple_of((1 - c) * rh, 16), rh), :]
            _remote(got, got, lsem.at[i], rsem.at[i], sibling).wait_recv()
        stager.drain()
        for wait in pending:
            wait()

    max_cols = max(h.shape[1] for h in halves)
    sems = [pltpu.SemaphoreType.DMA((n,)), pltpu.SemaphoreType.DMA((n,))]
    return pl.pallas_call(
        body, name="share_halves", out_shape=out_shape,
        in_specs=[ANY] * n, out_specs=tuple(ANY for _ in range(n)), scratch_shapes=sems + _stage_scratch(max_cols, F32),
        compiler_params=_params(),
    )(*halves)


def _all_reduce_small(buf, name):
    shape = buf.shape

    def body(in_ref, out_ref, slots, ssem, rsem):
        x, y, c = _place()
        me = 4 * x + 2 * y + c
        slots[me] = in_ref[...]
        sends = []
        for r in range(1, 8):
            fx, fy, fc = (r >> 2) & 1, (r >> 1) & 1, r & 1
            cp = _remote(in_ref, slots.at[me], ssem.at[r - 1], rsem.at[r - 1], (_flip(x, fx), _flip(y, fy), _flip(c, fc)))
            cp.start()
            sends.append(cp)
        for r in range(1, 8):
            fx, fy, fc = (r >> 2) & 1, (r >> 1) & 1, r & 1
            px, py, pc = _flip(x, fx), _flip(y, fy), _flip(c, fc)
            _remote(in_ref, slots.at[4 * px + 2 * py + pc], ssem.at[r - 1], rsem.at[r - 1], (px, py, pc)).wait_recv()
        for cp in sends:
            cp.wait_send()
        acc = slots[0]
        for k in range(1, 8):
            acc = acc + slots[k]
        out_ref[...] = acc

    return pl.pallas_call(
        body, name=name, out_shape=S(shape, F32), in_specs=[VMEM_SPEC], out_specs=VMEM_SPEC,
        scratch_shapes=[pltpu.VMEM((8,) + shape, F32), pltpu.SemaphoreType.DMA((7,)), pltpu.SemaphoreType.DMA((7,))],
    )(buf)


BIG = ("ffn1_w_gu", "ffn1_w_down", "w_in", "w_out", "xattn_wq", "xattn_wkv", "xattn_wo", "ffn2_w_gu", "ffn2_w_down")
KIND = {"ffn1_w_gu": "col", "ffn1_w_down": "row", "w_in": "col", "w_out": "row", "xattn_wq": "row",
        "xattn_wkv": "col", "xattn_wo": "row", "ffn2_w_gu": "col", "ffn2_w_down": "row"}
WEIGHTS = ("rel_bias", "ffn1_norm", "ffn1_w_gu", "ffn1_w_down", "mix_norm", "w_in", "sinks", "conv_w", "w_out",
           "xattn_norm", "mem_norm", "xattn_wq", "xattn_wkv", "xattn_wo", "ffn2_norm", "ffn2_w_gu", "ffn2_w_down",
           "final_norm")
SMALL_ROWS = 16
GAIN_ROW = {"ffn1_norm": 0, "mix_norm": 1, "xattn_norm": 2, "mem_norm": 3, "ffn2_norm": 4, "final_norm": 5}
CONV_ROW, SINK_ROW, BIAS_ROW, LOSS_ROW = 6, 9, 10, 11


def _local_step(x, mem, pos, target, w, gains, rel_bias, sinks, conv_w):
    t, d = x.shape
    pos_col = pos.reshape(t, 1)
    pos_row = pos.reshape(1, t)
    bias_t = _bias_build(rel_bias)

    x1, g1, u1 = _ffn_fwd(x, gains["ffn1_norm"], w["ffn1_w_gu"], w["ffn1_w_down"], "ffn1_fwd")
    qkv, e = _mix_proj(x1, gains["mix_norm"], w["w_in"])
    attn = _swa_fwd(qkv, pos_col, pos_row, bias_t, sinks)
    x2, merged = _mix_out_fwd(e, attn, conv_w, w["w_out"], x1)
    mh, kv = _mem_kv(mem, gains["mem_norm"], w["xattn_wkv"])
    x3, qx, o = _xattn_fwd(x2, gains["xattn_norm"], w["xattn_wq"], kv, w["xattn_wo"])
    x4, g2, u2 = _ffn_fwd(x3, gains["ffn2_norm"], w["ffn2_w_gu"], w["ffn2_w_down"], "ffn2_fwd")
    dx4, d_final, loss = _loss_bwd(x4, gains["final_norm"], target)

    grads = {}
    dx3, d_ffn2, dgu2, a2, h4, dyh4 = _ffn_bwd(dx4, x3, gains["ffn2_norm"], g2, u2, w["ffn2_w_gu"], w["ffn2_w_down"],
                                               "ffn2_bwd")
    grads["ffn2_w_gu"] = _dw(h4, dgu2, 1408, "dw_ffn2_gu")
    grads["ffn2_w_down"] = _dw(a2, dyh4, 512, "dw_ffn2_down")
    dx2, d_xattn, dkv, dxh3, h3, dqx = _xattn_bwd(dx3, x2, gains["xattn_norm"], qx, o, kv, w["xattn_wq"], w["xattn_wo"])
    grads["xattn_wo"] = _dw(o, dxh3, 1024, "dw_wo")
    grads["xattn_wq"] = _dw(h3, dqx, 1024, "dw_wq")
    grads["xattn_wkv"], d_mem = _mem_bwd(dkv, mh, mem, gains["mem_norm"], w["xattn_wkv"])
    dattn, de, dcw, dxh2 = _mix_out_bwd(dx2, e, attn, conv_w, w["w_out"])
    grads["w_out"] = _dw(merged, dxh2, 1024, "dw_wout")
    dq, dkvs, gb, dsk = _swa_bwd(qkv, dattn, pos_col, pos_row, bias_t, sinks)
    d_rel_bias, d_sinks = _bias_reduce(gb, dsk)
    dx1, d_mix, h2 = _mix_in_bwd(dq, dkvs, de, x1, gains["mix_norm"], w["w_in"], dx2)
    n_in = w["w_in"].shape[1]
    gw = _dw(h2, dq, 512, "dw_win_q", ncols=n_in)
    gw = _dw(h2, dkvs, 512, "dw_win_kv", into=gw, col_block=dq.shape[1] // 512, ncols=n_in)
    grads["w_in"] = _dw(h2, de, 512, "dw_win_e", into=gw, col_block=(dq.shape[1] + dkvs.shape[1]) // 512, ncols=n_in)
    dx0, d_ffn1, dgu1, a1, h1, dyh1 = _ffn_bwd(dx1, x, gains["ffn1_norm"], g1, u1, w["ffn1_w_gu"], w["ffn1_w_down"],
                                               "ffn1_bwd")
    grads["ffn1_w_gu"] = _dw(h1, dgu1, 1408, "dw_ffn1_gu")
    grads["ffn1_w_down"] = _dw(a1, dyh1, 512, "dw_ffn1_down")

    pad = lambda v: jnp.pad(v.reshape(1, -1), ((0, 0), (0, d - v.size)))
    small = jnp.concatenate([
        d_ffn1, d_mix, d_xattn, d_mem, d_ffn2, d_final, dcw[0:3], pad(d_sinks), pad(d_rel_bias), pad(loss[0, 0:1]),
        jnp.zeros((SMALL_ROWS - LOSS_ROW - 1, d), F32)], axis=0)
    return dx0, grads, small


def kernel(x, mem, positions, rel_bias, ffn1_norm, ffn1_w_gu, ffn1_w_down, mix_norm, w_in, sinks, conv_w, w_out, xattn_norm, mem_norm, xattn_wq, xattn_wkv, xattn_wo, ffn2_norm, ffn2_w_gu, ffn2_w_down, final_norm, loss_target, m_rel_bias, m_ffn1_norm, m_ffn1_w_gu, m_ffn1_w_down, m_mix_norm, m_w_in, m_sinks, m_conv_w, m_w_out, m_xattn_norm, m_mem_norm, m_xattn_wq, m_xattn_wkv, m_xattn_wo, m_ffn2_norm, m_ffn2_w_gu, m_ffn2_w_down, m_final_norm, v_rel_bias, v_ffn1_norm, v_ffn1_w_gu, v_ffn1_w_down, v_mix_norm, v_w_in, v_sinks, v_conv_w, v_w_out, v_xattn_norm, v_mem_norm, v_xattn_wq, v_xattn_wkv, v_xattn_wo, v_ffn2_norm, v_ffn2_w_gu, v_ffn2_w_down, v_final_norm):
    args = dict(locals())
    wts = {k: args[k] for k in WEIGHTS}
    mom = {k: args["m_" + k] for k in WEIGHTS}
    var = {k: args["v_" + k] for k in WEIGHTS}
    d = x.shape[-1]
    s_me = 2 * lax.axis_index("x") + lax.axis_index("y")
    kinds = [KIND[k] for k in BIG]

    shards = [wts[k][0].astype(BF) for k in BIG]
    gathered = _all_gather_weights(shards, kinds)
    whole = {k: (gw if KIND[k] == "col" else gw.reshape(-1, gw.shape[-1])) for k, gw in zip(BIG, gathered)}
    cw_cols = conv_w.shape[-1]
    placed = lax.dynamic_update_slice(jnp.zeros((SMALL_ROWS, d), F32), 0.5 * conv_w[0], (0, s_me * cw_cols))
    conv_whole = _all_reduce_small(placed, "gather_conv_w")[0:3]

    gains = {k: wts[k].reshape(1, d) for k in GAIN_ROW}
    dx0, grads, small = _local_step(x[0], mem[0], positions[0], loss_target[0], whole, gains, rel_bias, sinks,
                                    conv_whole)

    gnat = [grads[k] if KIND[k] == "col" else grads[k].reshape(4, -1, grads[k].shape[-1]) for k in BIG]
    halves = _reduce_siblings(gnat, kinds)
    n = len(BIG)
    parts = []
    for i, k in enumerate(BIG):
        mine, sib = halves[i], halves[n + i]
        if KIND[k] == "row":
            mine, sib = mine.reshape(-1, mine.shape[-1]), sib.reshape(-1, sib.shape[-1])
        p = _add_bf16(mine, sib, "pair_sum_" + k)
        parts.append(p if KIND[k] == "col" else p.reshape(4, -1, p.shape[-1]))
    slabs = _reduce_chips(parts, kinds)
    reduced = [_sum4(sl, "chip_sum_" + k) for sl, k in zip(slabs, BIG)]
    shard_grads = dict(zip(BIG, _share_halves(reduced)))

    small_sum = _all_reduce_small(small, "reduce_small")
    loss = small_sum[LOSS_ROW, 0]

    out_g, out_d, out_m, out_v = {}, {}, {}, {}
    for k in BIG:
        g2d = shard_grads[k]
        dl, mn, vn = _adamw(wts[k][0], g2d, mom[k][0], var[k][0], "adamw_" + k)
        out_g[k], out_d[k], out_m[k], out_v[k] = g2d[None], dl[None], mn[None], vn[None]

    conv_g = lax.dynamic_slice(small_sum, (CONV_ROW, s_me * cw_cols), (3, cw_cols))

    def pack(src, conv_block):
        rows = [None] * SMALL_ROWS
        for k, r in GAIN_ROW.items():
            rows[r] = src[k].reshape(1, d)
        for j in range(3):
            rows[CONV_ROW + j] = jnp.pad(conv_block[j:j + 1], ((0, 0), (0, d - cw_cols)))
        rows[SINK_ROW] = jnp.pad(src["sinks"].reshape(1, -1), ((0, 0), (0, d - src["sinks"].size)))
        rows[BIAS_ROW] = jnp.pad(src["rel_bias"].reshape(1, -1), ((0, 0), (0, d - src["rel_bias"].size)))
        return jnp.concatenate([r if r is not None else jnp.zeros((1, d), F32) for r in rows], axis=0)

    small_g = {k: small_sum[r] for k, r in GAIN_ROW.items()}
    small_g["sinks"] = small_sum[SINK_ROW, 0:sinks.size]
    small_g["rel_bias"] = small_sum[BIAS_ROW, 0:rel_bias.size]
    gp = pack(small_g, conv_g)
    dl, mn, vn = _adamw(pack(wts, conv_w[0]), gp, pack(mom, m_conv_w[0]), pack(var, v_conv_w[0]), "adamw_small")

    def unpack(buf, k):
        if k in GAIN_ROW:
            return buf[GAIN_ROW[k]].reshape(wts[k].shape)
        if k == "conv_w":
            return buf[CONV_ROW:CONV_ROW + 3, 0:cw_cols][None]
        if k == "sinks":
            return buf[SINK_ROW, 0:sinks.size].reshape(sinks.shape)
        return buf[BIAS_ROW, 0:rel_bias.size].reshape(rel_bias.shape)

    for k in WEIGHTS:
        if k not in KIND:
            out_g[k], out_d[k], out_m[k], out_v[k] = unpack(gp, k), unpack(dl, k), unpack(mn, k), unpack(vn, k)

    return (loss, dx0[None], *[out_g[k] for k in WEIGHTS], *[out_d[k] for k in WEIGHTS],
            *[out_m[k] for k in WEIGHTS], *[out_v[k] for k in WEIGHTS])
```

```python
import functools
import math

import jax
import jax.numpy as jnp
from jax import lax
from jax.experimental import pallas as pl
from jax.experimental.pallas import tpu as pltpu

BF = jnp.bfloat16
F32 = jnp.float32
I32 = jnp.int32
S = jax.ShapeDtypeStruct

EPS = 1e-6
NEG = -1e30
POS_PAD = 1 << 30
WINDOW = 128
BLOCK = 128
HEAD_DIM = 64
SWA_HEADS = 16
SWA_KV_HEADS = 4
SWA_GROUP = SWA_HEADS // SWA_KV_HEADS
MEM_HEADS = 4
REL_BUCKETS = 32
REL_MAX_DIST = 128
ADAM_LR = 0.001
ADAM_B1 = 0.9
ADAM_B2 = 0.999
ADAM_EPS = 1e-08
ADAM_WD = 0.01
ADAM_STEP = 10

V7X_VMEM_LIMIT_BYTES = 56 * 1024 * 1024
MESH = pl.DeviceIdType.MESH
ANY = pl.BlockSpec(memory_space=pl.ANY)
VMEM_SPEC = pl.BlockSpec(memory_space=pltpu.VMEM)
SMEM_SPEC = pl.BlockSpec(memory_space=pltpu.SMEM)


def _params(sem=None):
    return pltpu.CompilerParams(dimension_semantics=sem, vmem_limit_bytes=V7X_VMEM_LIMIT_BYTES)


def _resident(shape):
    nd = len(shape)
    return pl.BlockSpec(shape, lambda *_: (0,) * nd, pipeline_mode=pl.Buffered(1))


def _acc_spec(shape):
    nd = len(shape)
    return pl.BlockSpec(shape, lambda *_: (0,) * nd)


def _nn(a, b):
    return jnp.dot(a, b, preferred_element_type=F32)


def _nt(a, b):
    return lax.dot_general(a, b, (((1,), (1,)), ((), ())), preferred_element_type=F32)


def _tn(a, b):
    return lax.dot_general(a, b, (((0,), (0,)), ((), ())), preferred_element_type=F32)


def _sigmoid(v):
    return 1.0 / (1.0 + jnp.exp(-v))


def _rms(x):
    r = lax.rsqrt(jnp.mean(x * x, axis=-1, keepdims=True) + EPS)
    return x * r, r


def _rms_bwd(dh, n, r, g):
    dn = dh * g
    dx = r * (dn - n * jnp.mean(dn * n, axis=-1, keepdims=True))
    return dx, jnp.sum(dh * n, axis=0, keepdims=True)


def _ffn_fwd(x, gn, wgu, wd, name):
    t, d = x.shape
    f = wd.shape[0]
    tm, fc = 256, 1408

    def body(x_ref, gn_ref, wgu_ref, wd_ref, xo_ref, g_ref, u_ref):
        xv = x_ref[...]
        n, _ = _rms(xv)
        h = (n * gn_ref[...]).astype(BF)
        acc = jnp.zeros((tm, d), F32)
        for c0 in range(0, f, fc):
            g = _nn(h, wgu_ref[:, c0:c0 + fc])
            u = _nn(h, wgu_ref[:, f + c0:f + c0 + fc])
            g_ref[:, c0:c0 + fc] = g.astype(BF)
            u_ref[:, c0:c0 + fc] = u.astype(BF)
            a = (g * _sigmoid(g)) * u
            acc = acc + _nn(a.astype(BF), wd_ref[c0:c0 + fc, :])
        xo_ref[...] = xv + 0.5 * acc

    return pl.pallas_call(
        body, name=name, grid=(t // tm,),
        out_shape=(S((t, d), F32), S((t, f), BF), S((t, f), BF)),
        in_specs=[pl.BlockSpec((tm, d), lambda i: (i, 0)), _resident((1, d)), _resident(wgu.shape), _resident(wd.shape)],
        out_specs=(pl.BlockSpec((tm, d), lambda i: (i, 0)), pl.BlockSpec((tm, f), lambda i: (i, 0)),
                   pl.BlockSpec((tm, f), lambda i: (i, 0))),
        compiler_params=_params(("parallel",)),
    )(x, gn, wgu, wd)


def _mix_proj(x, gn, w_in):
    t, d = x.shape
    tm = 256
    nqkv = 1536
    ne = w_in.shape[1] - nqkv

    def body(x_ref, gn_ref, w_ref, qkv_ref, e_ref):
        n, _ = _rms(x_ref[...])
        h = (n * gn_ref[...]).astype(BF)
        qkv_ref[...] = _nn(h, w_ref[:, 0:nqkv]).astype(BF)
        for c0 in range(0, ne, 1024):
            e_ref[:, c0:c0 + 1024] = _nn(h, w_ref[:, nqkv + c0:nqkv + c0 + 1024])

    return pl.pallas_call(
        body, name="mix_proj", grid=(t // tm,),
        out_shape=(S((t, nqkv), BF), S((t, ne), F32)),
        in_specs=[pl.BlockSpec((tm, d), lambda i: (i, 0)), _resident((1, d)), _resident(w_in.shape)],
        out_specs=(pl.BlockSpec((tm, nqkv), lambda i: (i, 0)), pl.BlockSpec((tm, ne), lambda i: (i, 0))),
        compiler_params=_params(("parallel",)),
    )(x, gn, w_in)


def _t5_bucket(rel):
    n = jnp.maximum(rel, 0)
    max_exact = REL_BUCKETS // 2
    nf = jnp.maximum(n, 1).astype(F32)
    large = max_exact + (jnp.log(nf / max_exact) / math.log(REL_MAX_DIST / max_exact)
                         * (REL_BUCKETS - max_exact)).astype(I32)
    large = jnp.minimum(large, REL_BUCKETS - 1)
    return jnp.where(n < max_exact, n, large)


def _block_rel():
    i = lax.broadcasted_iota(I32, (BLOCK, 2 * BLOCK), 0)
    j = lax.broadcasted_iota(I32, (BLOCK, 2 * BLOCK), 1)
    return i + BLOCK - j


def _bias_build(rel_bias):
    def body(rb_ref, o_ref):
        bucket = _t5_bucket(_block_rel())
        for h in range(SWA_HEADS):
            acc = jnp.zeros((BLOCK, 2 * BLOCK), F32)
            for b in range(REL_BUCKETS):
                acc = jnp.where(bucket == b, rb_ref[b, h], acc)
            o_ref[h] = acc

    return pl.pallas_call(
        body, name="bias_build", out_shape=S((SWA_HEADS, BLOCK, 2 * BLOCK), F32),
        in_specs=[SMEM_SPEC], out_specs=VMEM_SPEC,
    )(rel_bias)


def _swa_visible(b, pq_ref, pkp_ref, pkc_ref):
    pk = jnp.concatenate([pkp_ref[...], pkc_ref[...]], axis=1)
    col = lax.broadcasted_iota(I32, (1, 2 * BLOCK), 1)
    pk = jnp.where(jnp.logical_and(b == 0, col < BLOCK), POS_PAD, pk)
    rel = pq_ref[...] - pk
    return jnp.logical_and(rel >= 0, rel < WINDOW)


def _swa_probs(qh, kh, vis, bias, sink):
    s = _nt(qh, kh) * (HEAD_DIM ** -0.5)
    s = jnp.where(vis, s + bias, NEG)
    m = jnp.maximum(jnp.max(s, axis=-1, keepdims=True), sink)
    p = jnp.exp(s - m)
    ps = jnp.exp(sink - m)
    inv = 1.0 / (jnp.sum(p, axis=-1, keepdims=True) + ps)
    return p * inv, ps * inv


def _swa_fwd(qkv, pos_col, pos_row, bias_t, sinks):
    t = qkv.shape[0]
    nb = t // BLOCK
    qw = SWA_HEADS * HEAD_DIM
    kw = SWA_KV_HEADS * HEAD_DIM

    def body(q_ref, kp_ref, kc_ref, vp_ref, vc_ref, pq_ref, pkp_ref, pkc_ref, bias_ref, sink_ref, o_ref):
        b = pl.program_id(0)
        vis = _swa_visible(b, pq_ref, pkp_ref, pkc_ref)
        k2 = jnp.concatenate([kp_ref[...], kc_ref[...]], axis=0)
        v2 = jnp.concatenate([vp_ref[...], vc_ref[...]], axis=0)
        for hk in range(SWA_KV_HEADS):
            kh = k2[:, hk * HEAD_DIM:(hk + 1) * HEAD_DIM]
            vh = v2[:, hk * HEAD_DIM:(hk + 1) * HEAD_DIM]
            for g in range(SWA_GROUP):
                h = hk * SWA_GROUP + g
                qh = q_ref[:, h * HEAD_DIM:(h + 1) * HEAD_DIM]
                pn, _ = _swa_probs(qh, kh, vis, bias_ref[h], sink_ref[0, h])
                o_ref[:, h * HEAD_DIM:(h + 1) * HEAD_DIM] = _nn(pn.astype(BF), vh)

    prev = lambda b: jnp.maximum(b - 1, 0)
    return pl.pallas_call(
        body, name="swa_fwd", grid=(nb,), out_shape=S((t, qw), F32),
        in_specs=[
            pl.BlockSpec((BLOCK, qw), lambda b: (b, 0)),
            pl.BlockSpec((BLOCK, kw), lambda b: (prev(b), qw // kw)),
            pl.BlockSpec((BLOCK, kw), lambda b: (b, qw // kw)),
            pl.BlockSpec((BLOCK, kw), lambda b: (prev(b), qw // kw + 1)),
            pl.BlockSpec((BLOCK, kw), lambda b: (b, qw // kw + 1)),
            pl.BlockSpec((BLOCK, 1), lambda b: (b, 0)),
            pl.BlockSpec((1, BLOCK), lambda b: (0, prev(b))),
            pl.BlockSpec((1, BLOCK), lambda b: (0, b)),
            _resident(bias_t.shape),
            SMEM_SPEC,
        ],
        out_specs=pl.BlockSpec((BLOCK, qw), lambda b: (b, 0)),
        compiler_params=_params(("parallel",)),
    )(qkv, qkv, qkv, qkv, qkv, pos_col, pos_row, pos_row, bias_t, sinks)


def _conv_taps(z, zh, first):
    tm = z.shape[0]
    zh = jnp.where(first, 0.0, zh)
    row = lax.broadcasted_iota(I32, (tm, 1), 0)
    z1 = jnp.where(row == 0, zh[7:8, :], pltpu.roll(z, 1, 0))
    z2 = jnp.where(row == 0, zh[6:7, :], jnp.where(row == 1, zh[7:8, :], pltpu.roll(z, 2, 0)))
    return z1, z2


def _mix_out_fwd(e, attn, conv_w, w_out, x):
    t, d = x.shape
    tm = 256
    hb = tm // 8

    def body(c_ref, b_ref, u_ref, ga_ref, gc_ref, ch_ref, uh_ref, attn_ref, cw_ref, w_ref, x_ref, xo_ref, mg_ref):
        i = pl.program_id(0)
        z = c_ref[...] * u_ref[...]
        z1, z2 = _conv_taps(z, ch_ref[...] * uh_ref[...], i == 0)
        s = cw_ref[0:1, :] * z2 + cw_ref[1:2, :] * z1 + cw_ref[2:3, :] * z
        conv = b_ref[...] * s
        merged = (_sigmoid(ga_ref[...]) * attn_ref[...] + _sigmoid(gc_ref[...]) * conv).astype(BF)
        mg_ref[...] = merged
        xo_ref[...] = x_ref[...] + _nn(merged, w_ref[...])

    ecol = lambda cb: pl.BlockSpec((tm, d), lambda i: (i, cb))
    halo = lambda cb: pl.BlockSpec((8, d), lambda i: (jnp.maximum(i * hb - 1, 0), cb))
    row = pl.BlockSpec((tm, d), lambda i: (i, 0))
    return pl.pallas_call(
        body, name="mix_out_fwd", grid=(t // tm,),
        out_shape=(S((t, d), F32), S((t, d), BF)),
        in_specs=[ecol(0), ecol(1), ecol(2), ecol(3), ecol(4), halo(0), halo(2), row,
                  _resident(conv_w.shape), _resident(w_out.shape), row],
        out_specs=(row, row),
        compiler_params=_params(("parallel",)),
    )(e, e, e, e, e, e, e, attn, conv_w, w_out, x)


def _mem_kv(mem, gm, wkv):
    m, d = mem.shape

    def body(mem_ref, gm_ref, w_ref, mh_ref, kv_ref):
        n, _ = _rms(mem_ref[...])
        mh = (n * gm_ref[...]).astype(BF)
        mh_ref[...] = mh
        kv_ref[...] = _nn(mh, w_ref[...]).astype(BF)

    return pl.pallas_call(
        body, name="mem_kv", out_shape=(S((m, d), BF), S((m, wkv.shape[1]), BF)),
        compiler_params=_params(),
    )(mem, gm, wkv)


def _xattn_probs(qh, kh):
    s = _nt(qh, kh) * (kh.shape[1] ** -0.5)
    p = jnp.exp(s - jnp.max(s, axis=-1, keepdims=True))
    return p * (1.0 / jnp.sum(p, axis=-1, keepdims=True))


def _xattn_fwd(x, gn, wq, kv, wo):
    t, d = x.shape
    tm = 256
    hd = d // MEM_HEADS

    def body(x_ref, gn_ref, wq_ref, kv_ref, wo_ref, xo_ref, q_ref, o_ref):
        xv = x_ref[...]
        n, _ = _rms(xv)
        q = _nn((n * gn_ref[...]).astype(BF), wq_ref[...]).astype(BF)
        q_ref[...] = q
        outs = []
        for hh in range(MEM_HEADS):
            p = _xattn_probs(q[:, hh * hd:(hh + 1) * hd], kv_ref[:, hh * hd:(hh + 1) * hd])
            outs.append(_nn(p.astype(BF), kv_ref[:, d + hh * hd:d + (hh + 1) * hd]))
        o = jnp.concatenate(outs, axis=1).astype(BF)
        o_ref[...] = o
        xo_ref[...] = xv + _nn(o, wo_ref[...])

    row = pl.BlockSpec((tm, d), lambda i: (i, 0))
    return pl.pallas_call(
        body, name="xattn_fwd", grid=(t // tm,),
        out_shape=(S((t, d), F32), S((t, d), BF), S((t, d), BF)),
        in_specs=[row, _resident((1, d)), _resident(wq.shape), _resident(kv.shape), _resident(wo.shape)],
        out_specs=(row, row, row),
        compiler_params=_params(("parallel",)),
    )(x, gn, wq, kv, wo)


def _loss_bwd(x, gf, target):
    t, d = x.shape
    tm = 512

    def body(x_ref, gf_ref, t_ref, dx_ref, dg_ref, loss_ref):
        @pl.when(pl.program_id(0) == 0)
        def _():
            dg_ref[...] = jnp.zeros_like(dg_ref)
            loss_ref[...] = jnp.zeros_like(loss_ref)
        n, r = _rms(x_ref[...])
        g = gf_ref[...]
        err = n * g - t_ref[...]
        loss_ref[...] += 0.5 * jnp.sum(jnp.sum(err * err, axis=-1, keepdims=True) / d, axis=0, keepdims=True)
        dx, dg = _rms_bwd(err / d, n, r, g)
        dx_ref[...] = dx
        dg_ref[...] += dg

    row = pl.BlockSpec((tm, d), lambda i: (i, 0))
    return pl.pallas_call(
        body, name="loss_bwd", grid=(t // tm,),
        out_shape=(S((t, d), F32), S((1, d), F32), S((1, 128), F32)),
        in_specs=[row, _resident((1, d)), row],
        out_specs=(row, _acc_spec((1, d)), _acc_spec((1, 128))),
        compiler_params=_params(("arbitrary",)),
    )(x, gf, target)


def _ffn_bwd(dxo, x, gn, g, u, wgu, wd, name):
    t, d = x.shape
    f = wd.shape[0]
    tm, fc = 256, 1408

    def body(dxo_ref, x_ref, gn_ref, g_ref, u_ref, wgu_ref, wd_ref, dx_ref, dgn_ref, dgu_ref, a_ref, h_ref, dyh_ref):
        @pl.when(pl.program_id(0) == 0)
        def _():
            dgn_ref[...] = jnp.zeros_like(dgn_ref)
        dxov = dxo_ref[...]
        dyh = (0.5 * dxov).astype(BF)
        dyh_ref[...] = dyh
        n, r = _rms(x_ref[...])
        gnv = gn_ref[...]
        h_ref[...] = (n * gnv).astype(BF)
        dh = jnp.zeros((tm, d), F32)
        for c0 in range(0, f, fc):
            gv = g_ref[:, c0:c0 + fc].astype(F32)
            uv = u_ref[:, c0:c0 + fc].astype(F32)
            da = _nt(dyh, wd_ref[c0:c0 + fc, :])
            sg = _sigmoid(gv)
            silu = gv * sg
            a_ref[:, c0:c0 + fc] = (silu * uv).astype(BF)
            dg = (da * uv * (sg * (1.0 + gv * (1.0 - sg)))).astype(BF)
            du = (da * silu).astype(BF)
            dgu_ref[:, c0:c0 + fc] = dg
            dgu_ref[:, f + c0:f + c0 + fc] = du
            dh = dh + _nt(dg, wgu_ref[:, c0:c0 + fc]) + _nt(du, wgu_ref[:, f + c0:f + c0 + fc])
        dx, dgn = _rms_bwd(dh, n, r, gnv)
        dx_ref[...] = dxov + dx
        dgn_ref[...] += dgn

    row = pl.BlockSpec((tm, d), lambda i: (i, 0))
    frow = pl.BlockSpec((tm, f), lambda i: (i, 0))
    return pl.pallas_call(
        body, name=name, grid=(t // tm,),
        out_shape=(S((t, d), F32), S((1, d), F32), S((t, 2 * f), BF), S((t, f), BF), S((t, d), BF), S((t, d), BF)),
        in_specs=[row, row, _resident((1, d)), frow, frow, _resident(wgu.shape), _resident(wd.shape)],
        out_specs=(row, _acc_spec((1, d)), pl.BlockSpec((tm, 2 * f), lambda i: (i, 0)), frow, row, row),
        compiler_params=_params(("arbitrary",)),
    )(dxo, x, gn, g, u, wgu, wd)


def _dw(a, b, tn, name, into=None, col_block=0, ncols=None):
    t, ka = a.shape
    nb = b.shape[1]
    tt = 1024
    nt = t // tt
    ncols = nb if ncols is None else ncols

    def body(*refs):
        a_ref, b_ref = refs[0], refs[1]
        o_ref, acc_ref = refs[-2], refs[-1]
        k = pl.program_id(1)

        @pl.when(k == 0)
        def _():
            acc_ref[...] = jnp.zeros_like(acc_ref)
        acc_ref[...] += _tn(a_ref[...], b_ref[...])

        @pl.when(k == nt - 1)
        def _():
            o_ref[...] = acc_ref[...].astype(BF)

    in_specs = [pl.BlockSpec((tt, ka), lambda j, k: (k, 0)), pl.BlockSpec((tt, tn), lambda j, k: (k, j))]
    args = [a, b]
    aliases = {}
    if into is not None:
        in_specs.append(ANY)
        args.append(into)
        aliases = {2: 0}
    return pl.pallas_call(
        body, name=name, grid=(nb // tn, nt), out_shape=S((ka, ncols), BF),
        in_specs=in_specs, out_specs=pl.BlockSpec((ka, tn), lambda j, k: (0, j + col_block)),
        scratch_shapes=[pltpu.VMEM((ka, tn), F32)], input_output_aliases=aliases,
        compiler_params=_params(("parallel", "arbitrary")),
    )(*args)


def _xattn_bwd(dxo, x, gn, q, o_unused, kv, wq, wo):
    t, d = x.shape
    tm = 256
    hd = d // MEM_HEADS
    nkv = kv.shape[0]

    def body(dxo_ref, x_ref, gn_ref, q_ref, kv_ref, wq_ref, wo_ref, dx_ref, dgn_ref, dkv_ref, dxh_ref, h_ref, dq_ref):
        @pl.when(pl.program_id(0) == 0)
        def _():
            dgn_ref[...] = jnp.zeros_like(dgn_ref)
            dkv_ref[...] = jnp.zeros_like(dkv_ref)
        dxov = dxo_ref[...]
        dxh = dxov.astype(BF)
        dxh_ref[...] = dxh
        do = _nt(dxh, wo_ref[...]).astype(BF)
        dqs = []
        for hh in range(MEM_HEADS):
            lo, hi = hh * hd, (hh + 1) * hd
            qh = q_ref[:, lo:hi]
            kh = kv_ref[:, lo:hi]
            vh = kv_ref[:, d + lo:d + hi]
            doh = do[:, lo:hi]
            p = _xattn_probs(qh, kh)
            dp = _nt(doh, vh)
            ds = (p * (dp - jnp.sum(p * dp, axis=-1, keepdims=True)) * (hd ** -0.5)).astype(BF)
            dqs.append(_nn(ds, kh))
            dkv_ref[:, lo:hi] += _tn(ds, qh)
            dkv_ref[:, d + lo:d + hi] += _tn(p.astype(BF), doh)
        dq = jnp.concatenate(dqs, axis=1).astype(BF)
        dq_ref[...] = dq
        n, r = _rms(x_ref[...])
        gnv = gn_ref[...]
        h_ref[...] = (n * gnv).astype(BF)
        dx, dgn = _rms_bwd(_nt(dq, wq_ref[...]), n, r, gnv)
        dx_ref[...] = dxov + dx
        dgn_ref[...] += dgn

    row = pl.BlockSpec((tm, d), lambda i: (i, 0))
    return pl.pallas_call(
        body, name="xattn_bwd", grid=(t // tm,),
        out_shape=(S((t, d), F32), S((1, d), F32), S((nkv, 2 * d), F32), S((t, d), BF), S((t, d), BF), S((t, d), BF)),
        in_specs=[row, row, _resident((1, d)), row, _resident(kv.shape), _resident(wq.shape), _resident(wo.shape)],
        out_specs=(row, _acc_spec((1, d)), _acc_spec((nkv, 2 * d)), row, row, row),
        compiler_params=_params(("arbitrary",)),
    )(dxo, x, gn, q, kv, wq, wo)


def _mem_bwd(dkv, mh, mem, gm, wkv):
    m, d = mem.shape

    def body(dkv_ref, mh_ref, mem_ref, gm_ref, w_ref, dw_ref, dgm_ref):
        dkvb = dkv_ref[...].astype(BF)
        dw_ref[...] = _tn(mh_ref[...], dkvb).astype(BF)
        dmh = _nt(dkvb, w_ref[...])
        n, _ = _rms(mem_ref[...])
        dgm_ref[...] = jnp.sum(dmh * n, axis=0, keepdims=True)

    return pl.pallas_call(
        body, name="mem_bwd", out_shape=(S(wkv.shape, BF), S((1, d), F32)),
        compiler_params=_params(),
    )(dkv, mh, mem, gm, wkv)


def _mix_out_bwd(dxo, e, attn, conv_w, w_out):
    t, d = attn.shape
    tm = 256
    hb = tm // 8
    nt = t // tm
    last8 = t // 8 - 1

    def body(dxo_ref, dxn_ref, c_ref, b_ref, u_ref, ga_ref, gc_ref, ch_ref, uh_ref, bn_ref, gcn_ref,
             attn_ref, cw_ref, w_ref, dattn_ref, de_ref, dcw_ref, dxh_ref):
        i = pl.program_id(0)

        @pl.when(i == 0)
        def _():
            dcw_ref[...] = jnp.zeros_like(dcw_ref)
        dxh = dxo_ref[...].astype(BF)
        dxh_ref[...] = dxh
        w = w_ref[...]
        dm = _nt(dxh, w)
        dmn = _nt(dxn_ref[...].astype(BF), w)
        cv, bv, uv = c_ref[...], b_ref[...], u_ref[...]
        sga = _sigmoid(ga_ref[...])
        sgc = _sigmoid(gc_ref[...])
        z = cv * uv
        z1, z2 = _conv_taps(z, ch_ref[...] * uh_ref[...], i == 0)
        w0, w1, w2 = cw_ref[0:1, :], cw_ref[1:2, :], cw_ref[2:3, :]
        s = w0 * z2 + w1 * z1 + w2 * z
        av = attn_ref[...]
        dattn_ref[...] = (dm * sga).astype(BF)
        dconv = dm * sgc
        ds = dconv * bv
        dsn = jnp.where(i == nt - 1, 0.0, dmn * _sigmoid(gcn_ref[...]) * bn_ref[...])
        row = lax.broadcasted_iota(I32, (tm, 1), 0)
        dsp1 = jnp.where(row == tm - 1, dsn[0:1, :], pltpu.roll(ds, tm - 1, 0))
        dsp2 = jnp.where(row == tm - 2, dsn[0:1, :], jnp.where(row == tm - 1, dsn[1:2, :], pltpu.roll(ds, tm - 2, 0)))
        dz = w2 * ds + w1 * dsp1 + w0 * dsp2
        de_ref[:, 0:d] = (dz * uv).astype(BF)
        de_ref[:, d:2 * d] = (dconv * s).astype(BF)
        de_ref[:, 2 * d:3 * d] = (dz * cv).astype(BF)
        de_ref[:, 3 * d:4 * d] = (dm * av * sga * (1.0 - sga)).astype(BF)
        de_ref[:, 4 * d:5 * d] = (dm * (bv * s) * sgc * (1.0 - sgc)).astype(BF)
        dcw_ref[0:1, :] += jnp.sum(ds * z2, axis=0, keepdims=True)
        dcw_ref[1:2, :] += jnp.sum(ds * z1, axis=0, keepdims=True)
        dcw_ref[2:3, :] += jnp.sum(ds * z, axis=0, keepdims=True)

    ecol = lambda cb: pl.BlockSpec((tm, d), lambda i: (i, cb))
    prev = lambda cb: pl.BlockSpec((8, d), lambda i: (jnp.maximum(i * hb - 1, 0), cb))
    nxt = lambda cb: pl.BlockSpec((8, d), lambda i: (jnp.minimum((i + 1) * hb, last8), cb))
    row = pl.BlockSpec((tm, d), lambda i: (i, 0))
    return pl.pallas_call(
        body, name="mix_out_bwd", grid=(nt,),
        out_shape=(S((t, d), BF), S((t, 5 * d), BF), S((8, d), F32), S((t, d), BF)),
        in_specs=[row, nxt(0), ecol(0), ecol(1), ecol(2), ecol(3), ecol(4), prev(0), prev(2), nxt(1), nxt(4),
                  row, _resident(conv_w.shape), _resident(w_out.shape)],
        out_specs=(row, pl.BlockSpec((tm, 5 * d), lambda i: (i, 0)), _acc_spec((8, d)), row),
        compiler_params=_params(("arbitrary",)),
    )(dxo, dxo, e, e, e, e, e, e, e, e, e, attn, conv_w, w_out)


def _swa_bwd(qkv, dattn, pos_col, pos_row, bias_t, sinks):
    t = qkv.shape[0]
    nb = t // BLOCK
    qw = SWA_HEADS * HEAD_DIM
    kw = SWA_KV_HEADS * HEAD_DIM

    def body(q_ref, kp_ref, kc_ref, vp_ref, vc_ref, do_ref, pq_ref, pkp_ref, pkc_ref, bias_ref, sink_ref,
             dq_ref, dkv_ref, gb_ref, dsk_ref, carry_ref):
        b = pl.program_id(0)

        @pl.when(b == 0)
        def _():
            gb_ref[...] = jnp.zeros_like(gb_ref)
            dsk_ref[...] = jnp.zeros_like(dsk_ref)
            carry_ref[...] = jnp.zeros_like(carry_ref)

        @pl.when(b < nb)
        def _():
            vis = _swa_visible(b, pq_ref, pkp_ref, pkc_ref)
            k2 = jnp.concatenate([kp_ref[...], kc_ref[...]], axis=0)
            v2 = jnp.concatenate([vp_ref[...], vc_ref[...]], axis=0)
            for hk in range(SWA_KV_HEADS):
                lo, hi = hk * HEAD_DIM, (hk + 1) * HEAD_DIM
                kh = k2[:, lo:hi]
                vh = v2[:, lo:hi]
                dk = jnp.zeros((2 * BLOCK, HEAD_DIM), F32)
                dv = jnp.zeros((2 * BLOCK, HEAD_DIM), F32)
                for g in range(SWA_GROUP):
                    h = hk * SWA_GROUP + g
                    hs = slice(h * HEAD_DIM, (h + 1) * HEAD_DIM)
                    qh = q_ref[:, hs]
                    doh = do_ref[:, hs]
                    pn, psn = _swa_probs(qh, kh, vis, bias_ref[h], sink_ref[0, h])
                    dp = _nt(doh, vh)
                    delta = jnp.sum(pn * dp, axis=-1, keepdims=True)
                    ds = pn * (dp - delta)
                    gb_ref[h] += ds
                    dsk_ref[:, h:h + 1] += -psn * delta
                    dsb = (ds * (HEAD_DIM ** -0.5)).astype(BF)
                    dq_ref[:, hs] = _nn(dsb, kh).astype(BF)
                    dk = dk + _tn(dsb, qh)
                    dv = dv + _tn(pn.astype(BF), doh)
                dkv_ref[:, lo:hi] = (carry_ref[:, lo:hi] + dk[0:BLOCK]).astype(BF)
                dkv_ref[:, kw + lo:kw + hi] = (carry_ref[:, kw + lo:kw + hi] + dv[0:BLOCK]).astype(BF)
                carry_ref[:, lo:hi] = dk[BLOCK:2 * BLOCK]
                carry_ref[:, kw + lo:kw + hi] = dv[BLOCK:2 * BLOCK]

        @pl.when(b == nb)
        def _():
            dkv_ref[...] = carry_ref[...].astype(BF)

    cur = lambda b: jnp.minimum(b, nb - 1)
    prev = lambda b: jnp.maximum(cur(b) - 1, 0)
    return pl.pallas_call(
        body, name="swa_bwd", grid=(nb + 1,),
        out_shape=(S((t, qw), BF), S((t, 2 * kw), BF), S((SWA_HEADS, BLOCK, 2 * BLOCK), F32), S((BLOCK, SWA_HEADS), F32)),
        in_specs=[
            pl.BlockSpec((BLOCK, qw), lambda b: (cur(b), 0)),
            pl.BlockSpec((BLOCK, kw), lambda b: (prev(b), qw // kw)),
            pl.BlockSpec((BLOCK, kw), lambda b: (cur(b), qw // kw)),
            pl.BlockSpec((BLOCK, kw), lambda b: (prev(b), qw // kw + 1)),
            pl.BlockSpec((BLOCK, kw), lambda b: (cur(b), qw // kw + 1)),
            pl.BlockSpec((BLOCK, qw), lambda b: (cur(b), 0)),
            pl.BlockSpec((BLOCK, 1), lambda b: (cur(b), 0)),
            pl.BlockSpec((1, BLOCK), lambda b: (0, prev(b))),
            pl.BlockSpec((1, BLOCK), lambda b: (0, cur(b))),
            _resident(bias_t.shape),
            SMEM_SPEC,
        ],
        out_specs=(
            pl.BlockSpec((BLOCK, qw), lambda b: (cur(b), 0)),
            pl.BlockSpec((BLOCK, 2 * kw), lambda b: (jnp.maximum(b - 1, 0), 0)),
            _acc_spec((SWA_HEADS, BLOCK, 2 * BLOCK)),
            _acc_spec((BLOCK, SWA_HEADS)),
        ),
        scratch_shapes=[pltpu.VMEM((BLOCK, 2 * kw), F32)],
        compiler_params=_params(("arbitrary",)),
    )(qkv, qkv, qkv, qkv, qkv, dattn, pos_col, pos_row, pos_row, bias_t, sinks)


def _bias_reduce(gb, dsk):
    def body(gb_ref, dsk_ref, drb_ref, dsink_ref):
        bucket = _t5_bucket(_block_rel())
        for b in range(REL_BUCKETS):
            mask = bucket == b
            for h in range(SWA_HEADS):
                drb_ref[b, h] = jnp.sum(jnp.where(mask, gb_ref[h], 0.0))
        for h in range(SWA_HEADS):
            dsink_ref[0, h] = jnp.sum(dsk_ref[:, h:h + 1])

    return pl.pallas_call(
        body, name="bias_reduce", out_shape=(S((REL_BUCKETS, SWA_HEADS), F32), S((1, SWA_HEADS), F32)),
        in_specs=[VMEM_SPEC, VMEM_SPEC], out_specs=(SMEM_SPEC, SMEM_SPEC),
    )(gb, dsk)


def _mix_in_bwd(dq, dkv, de, x, gn, w_in, dxo):
    t, d = x.shape
    tm = 256
    nq, nkv, ne = dq.shape[1], dkv.shape[1], de.shape[1]

    def body(dq_ref, dkv_ref, de_ref, x_ref, gn_ref, w_ref, dxo_ref, dx_ref, dgn_ref, h_ref):
        @pl.when(pl.program_id(0) == 0)
        def _():
            dgn_ref[...] = jnp.zeros_like(dgn_ref)
        dh = _nt(dq_ref[...], w_ref[:, 0:nq]) + _nt(dkv_ref[...], w_ref[:, nq:nq + nkv])
        for c0 in range(0, ne, 1024):
            dh = dh + _nt(de_ref[:, c0:c0 + 1024], w_ref[:, nq + nkv + c0:nq + nkv + c0 + 1024])
        n, r = _rms(x_ref[...])
        gnv = gn_ref[...]
        h_ref[...] = (n * gnv).astype(BF)
        dx, dgn = _rms_bwd(dh, n, r, gnv)
        dx_ref[...] = dxo_ref[...] + dx
        dgn_ref[...] += dgn

    row = pl.BlockSpec((tm, d), lambda i: (i, 0))
    wide = lambda w: pl.BlockSpec((tm, w), lambda i: (i, 0))
    return pl.pallas_call(
        body, name="mix_in_bwd", grid=(t // tm,),
        out_shape=(S((t, d), F32), S((1, d), F32), S((t, d), BF)),
        in_specs=[wide(nq), wide(nkv), wide(ne), row, _resident((1, d)), _resident(w_in.shape), row],
        out_specs=(row, _acc_spec((1, d)), row),
        compiler_params=_params(("arbitrary",)),
    )(dq, dkv, de, x, gn, w_in, dxo)


def _row_tile(rows):
    for tr in (256, 352, 128, 64, 16, 8):
        if rows % tr == 0:
            return tr
    return rows


def _add_bf16(a, b, name):
    rows, cols = a.shape
    tr = _row_tile(rows)

    def body(a_ref, b_ref, o_ref):
        o_ref[...] = (a_ref[...].astype(F32) + b_ref[...].astype(F32)).astype(BF)

    blk = pl.BlockSpec((tr, cols), lambda i: (i, 0))
    return pl.pallas_call(body, name=name, grid=(rows // tr,), out_shape=S((rows, cols), BF),
                          in_specs=[blk, blk], out_specs=blk, compiler_params=_params(("parallel",)))(a, b)


def _sum4(parts, name):
    _, rows, cols = parts.shape
    tr = _row_tile(rows)

    def body(p_ref, o_ref):
        acc = p_ref[3].astype(F32)
        for k in range(3):
            acc = acc + p_ref[k].astype(F32)
        o_ref[...] = acc

    return pl.pallas_call(
        body, name=name, grid=(rows // tr,), out_shape=S((rows, cols), F32),
        in_specs=[pl.BlockSpec((4, tr, cols), lambda i: (0, i, 0))], out_specs=pl.BlockSpec((tr, cols), lambda i: (i, 0)),
        compiler_params=_params(("parallel",)))(parts)


def _adamw(w, g, m, v, name):
    rows, cols = w.shape
    tr = _row_tile(rows)

    def body(w_ref, g_ref, m_ref, v_ref, d_ref, mo_ref, vo_ref):
        gv = g_ref[...]
        mn = ADAM_B1 * m_ref[...] + (1.0 - ADAM_B1) * gv
        vn = ADAM_B2 * v_ref[...] + (1.0 - ADAM_B2) * (gv * gv)
        m_hat = mn / (1.0 - ADAM_B1 ** ADAM_STEP)
        v_hat = vn / (1.0 - ADAM_B2 ** ADAM_STEP)
        d_ref[...] = -ADAM_LR * (m_hat / (jnp.sqrt(v_hat) + ADAM_EPS) + ADAM_WD * w_ref[...])
        mo_ref[...] = mn
        vo_ref[...] = vn

    blk = pl.BlockSpec((tr, cols), lambda i: (i, 0))
    shp = S((rows, cols), F32)
    return pl.pallas_call(body, name=name, grid=(rows // tr,), out_shape=(shp, shp, shp),
                          in_specs=[blk] * 4, out_specs=(blk,) * 3, compiler_params=_params(("parallel",)))(w, g, m, v)


def _place():
    x, y, c = lax.axis_index("x"), lax.axis_index("y"), lax.axis_index("c")
    return x, y, c


OTHER_CHIPS = ((1, 0), (0, 1), (1, 1))


def _flip(v, f):
    return 1 - v if f else v


def _remote(src, dst, ssem, rsem, dev):
    return pltpu.make_async_remote_copy(src_ref=src, dst_ref=dst, send_sem=ssem, recv_sem=rsem,
                                        device_id=dev, device_id_type=MESH)


class _Stager:
    def __init__(self, buf, lsem, ssem):
        self.buf, self.lsem, self.ssem = buf, lsem, ssem
        self.slots = buf.shape[0]
        self.count = 0
        self.inflight = [None] * self.slots

    def _load(self, src):
        s = self.count % self.slots
        self.count += 1
        if self.inflight[s] is not None:
            self.inflight[s]()
        stage = self.buf.at[(s,) + tuple(pl.ds(0, n) for n in src.shape)]
        load = pltpu.make_async_copy(src, stage, self.lsem.at[s])
        load.start()
        load.wait()
        return s, stage

    def send(self, src, dst, rsem, dev):
        s, stage = self._load(src)
        cp = _remote(stage, dst, self.ssem.at[s], rsem, dev)
        cp.start()
        self.inflight[s] = cp.wait_send

    def copy(self, src, dst):
        s, stage = self._load(src)
        cp = pltpu.make_async_copy(stage, dst, self.ssem.at[s])
        cp.start()
        self.inflight[s] = cp.wait

    def drain(self):
        for wait in self.inflight:
            if wait is not None:
                wait()


STAGE_SLOTS = 3
STAGE_ROWS = 176


def _chunk_rows(rows):
    for rc in (STAGE_ROWS, 128, 64, 32, 16):
        if rows % rc == 0:
            return rc
    raise ValueError(rows)


def _stage_scratch(cols, dtype):
    return [pltpu.VMEM((STAGE_SLOTS, STAGE_ROWS, cols), dtype), pltpu.SemaphoreType.DMA((STAGE_SLOTS,)),
            pltpu.SemaphoreType.DMA((STAGE_SLOTS,))]


def _all_gather_weights(shards, kinds):
    n = len(shards)
    out_shape = tuple(
        S((w.shape[0], 4 * w.shape[1]), BF) if k == "col" else S((4,) + w.shape, BF) for w, k in zip(shards, kinds))

    def body(*refs):
        ins, outs = refs[:n], refs[n:2 * n]
        ssem, rsem, frsem, buf, blsem, bssem = refs[2 * n:]
        x, y, c = _place()
        s_me = 2 * x + y
        sibling = (x, y, 1 - c)
        stager = _Stager(buf, blsem, bssem)

        def window(i, s, half, r0=0, nrows=None):
            rows, cols = shards[i].shape
            rh = rows // 2
            nrows = rh if nrows is None else nrows
            start = pl.multiple_of(half * rh + r0, 16)
            if kinds[i] == "col":
                return outs[i].at[pl.ds(start, nrows), pl.ds(pl.multiple_of(s * cols, 128), cols)]
            return outs[i].at[s, pl.ds(start, nrows), :]

        pending = []
        for i in range(n):
            rh = shards[i].shape[0] // 2
            src = ins[i].at[pl.ds(pl.multiple_of(c * rh, 16), rh), :]
            for j, (fx, fy) in enumerate(OTHER_CHIPS):
                cp = _remote(src, window(i, s_me, c), ssem.at[i, j], rsem.at[i, j], (_flip(x, fx), _flip(y, fy), c))
                cp.start()
                pending.append(cp.wait_send)
        for i in range(n):
            rows = shards[i].shape[0]
            rc = _chunk_rows(rows // 2)
            for r0 in range(0, rows, rc):
                stager.copy(ins[i].at[pl.ds(r0, rc), :], window(i, s_me, 0, r0, rc))
        for i in range(n):
            rh = shards[i].shape[0] // 2
            rc = _chunk_rows(rh)
            for j, (fx, fy) in enumerate(OTHER_CHIPS):
                px, py = _flip(x, fx), _flip(y, fy)
                landed = window(i, 2 * px + py, c)
                _remote(landed, landed, ssem.at[i, j], rsem.at[i, j], (px, py, c)).wait_recv()
                for r0 in range(0, rh, rc):
                    part = window(i, 2 * px + py, c, r0, rc)
                    stager.send(part, part, frsem.at[i, j], sibling)
        for i in range(n):
            for j, (fx, fy) in enumerate(OTHER_CHIPS):
                passed = window(i, 2 * _flip(x, fx) + _flip(y, fy), 1 - c)
                _remote(passed, passed, ssem.at[i, j], frsem.at[i, j], sibling).wait_recv()
        stager.drain()
        for wait in pending:
            wait()

    max_cols = max(w.shape[1] for w in shards)
    sems = [pltpu.SemaphoreType.DMA((n, 3)) for _ in range(3)]
    return pl.pallas_call(
        body, name="all_gather_weights", out_shape=out_shape,
        in_specs=[ANY] * n, out_specs=tuple(ANY for _ in range(n)), scratch_shapes=sems + _stage_scratch(max_cols, BF),
        compiler_params=_params(),
    )(*shards)


def _reduce_siblings(grads, kinds):
    n = len(grads)

    def half_shape(gr, k):
        return S((gr.shape[0] // 2, gr.shape[1]), BF) if k == "col" else S((4, gr.shape[1] // 2, gr.shape[2]), BF)

    halves = tuple(half_shape(gr, k) for gr, k in zip(grads, kinds))

    def body(*refs):
        ins, mine, sib = refs[:n], refs[n:2 * n], refs[2 * n:3 * n]
        rsem, lsem, buf, blsem, bssem = refs[3 * n:]
        x, y, c = _place()
        sibling = (x, y, 1 - c)
        stager = _Stager(buf, blsem, bssem)
        pending = []
        for i in range(n):
            col = kinds[i] == "col"
            rh = (grads[i].shape[0] if col else grads[i].shape[1]) // 2
            rc = _chunk_rows(rh)
            for r0 in range(0, rh, rc):
                give = pl.multiple_of((1 - c) * rh + r0, 16)
                keep = pl.multiple_of(c * rh + r0, 16)
                if col:
                    stager.send(ins[i].at[pl.ds(give, rc), :], sib[i].at[pl.ds(r0, rc), :], rsem.at[i], sibling)
                    stager.copy(ins[i].at[pl.ds(keep, rc), :], mine[i].at[pl.ds(r0, rc), :])
                else:
                    for s in range(4):
                        stager.send(ins[i].at[s, pl.ds(give, rc), :], sib[i].at[s, pl.ds(r0, rc), :], rsem.at[i], sibling)
                        stager.copy(ins[i].at[s, pl.ds(keep, rc), :], mine[i].at[s, pl.ds(r0, rc), :])
        for i in range(n):
            _remote(sib[i], sib[i], lsem.at[i], rsem.at[i], sibling).wait_recv()
        stager.drain()
        for wait in pending:
            wait()

    max_cols = max(gr.shape[-1] for gr in grads)
    sems = [pltpu.SemaphoreType.DMA((n,)), pltpu.SemaphoreType.DMA((n,))]
    return pl.pallas_call(
        body, name="reduce_siblings", out_shape=halves + halves,
        in_specs=[ANY] * n, out_specs=tuple(ANY for _ in range(2 * n)), scratch_shapes=sems + _stage_scratch(max_cols, BF),
        compiler_params=_params(),
    )(*grads)


def _reduce_chips(parts, kinds):
    n = len(parts)

    def slab_shape(p, k):
        return S((4, p.shape[0], p.shape[1] // 4), BF) if k == "col" else S(p.shape, BF)

    out_shape = tuple(slab_shape(p, k) for p, k in zip(parts, kinds))

    def body(*refs):
        ins, outs = refs[:n], refs[n:2 * n]
        ssem, rsem, buf, blsem, bssem = refs[2 * n:]
        x, y, c = _place()
        s_me = 2 * x + y
        stager = _Stager(buf, blsem, bssem)

        def slab(i, s, r0=0, nrows=None):
            rows = out_shape[i].shape[1]
            nrows = rows if nrows is None else nrows
            if kinds[i] == "col":
                cols = parts[i].shape[1] // 4
                return ins[i].at[pl.ds(r0, nrows), pl.ds(pl.multiple_of(s * cols, 128), cols)]
            return ins[i].at[s, pl.ds(r0, nrows), :]

        pending = []
        for i in range(n):
            for j, (fx, fy) in enumerate(OTHER_CHIPS):
                px, py = _flip(x, fx), _flip(y, fy)
                cp = _remote(slab(i, 2 * px + py), outs[i].at[j], ssem.at[i, j], rsem.at[i, j], (px, py, c))
                cp.start()
                pending.append(cp.wait)
        for i in range(n):
            rows = out_shape[i].shape[1]
            rc = _chunk_rows(rows)
            for r0 in range(0, rows, rc):
                stager.copy(slab(i, s_me, r0, rc), outs[i].at[3, pl.ds(r0, rc), :])
        stager.drain()
        for wait in pending:
            wait()

    max_cols = max(o.shape[2] for o in out_shape)
    sems = [pltpu.SemaphoreType.DMA((n, 3)), pltpu.SemaphoreType.DMA((n, 3))]
    return pl.pallas_call(
        body, name="reduce_chips", out_shape=out_shape,
        in_specs=[ANY] * n, out_specs=tuple(ANY for _ in range(n)), scratch_shapes=sems + _stage_scratch(max_cols, BF),
        compiler_params=_params(),
    )(*parts)


def _share_halves(halves):
    n = len(halves)
    out_shape = tuple(S((2 * h.shape[0], h.shape[1]), F32) for h in halves)

    def body(*refs):
        ins, outs = refs[:n], refs[n:2 * n]
        rsem, lsem, buf, blsem, bssem = refs[2 * n:]
        x, y, c = _place()
        sibling = (x, y, 1 - c)
        stager = _Stager(buf, blsem, bssem)
        pending = []
        for i in range(n):
            rh = halves[i].shape[0]
            rc = _chunk_rows(rh)
            for r0 in range(0, rh, rc):
                dst = outs[i].at[pl.ds(pl.multiple_of(c * rh + r0, 16), rc), :]
                stager.send(ins[i].at[pl.ds(r0, rc), :], dst, rsem.at[i], sibling)
                stager.copy(ins[i].at[pl.ds(r0, rc), :], dst)
        for i in range(n):
            rh = halves[i].shape[0]
            got = outs[i].at[pl.ds(pl.multiple_of((1 - c) * rh, 16), rh), :]
            _remote(got, got, lsem.at[i], rsem.at[i], sibling).wait_recv()
        stager.drain()
        for wait in pending:
            wait()

    max_cols = max(h.shape[1] for h in halves)
    sems = [pltpu.SemaphoreType.DMA((n,)), pltpu.SemaphoreType.DMA((n,))]
    return pl.pallas_call(
        body, name="share_halves", out_shape=out_shape,
        in_specs=[ANY] * n, out_specs=tuple(ANY for _ in range(n)), scratch_shapes=sems + _stage_scratch(max_cols, F32),
        compiler_params=_params(),
    )(*halves)


def _all_reduce_small(buf, name):
    shape = buf.shape

    def body(in_ref, out_ref, slots, ssem, rsem):
        x, y, c = _place()
        me = 4 * x + 2 * y + c
        slots[me] = in_ref[...]
        sends = []
        for r in range(1, 8):
            fx, fy, fc = (r >> 2) & 1, (r >> 1) & 1, r & 1
            cp = _remote(in_ref, slots.at[me], ssem.at[r - 1], rsem.at[r - 1], (_flip(x, fx), _flip(y, fy), _flip(c, fc)))
            cp.start()
            sends.append(cp)
        for r in range(1, 8):
            fx, fy, fc = (r >> 2) & 1, (r >> 1) & 1, r & 1
            px, py, pc = _flip(x, fx), _flip(y, fy), _flip(c, fc)
            _remote(in_ref, slots.at[4 * px + 2 * py + pc], ssem.at[r - 1], rsem.at[r - 1], (px, py, pc)).wait_recv()
        for cp in sends:
            cp.wait_send()
        acc = slots[0]
        for k in range(1, 8):
            acc = acc + slots[k]
        out_ref[...] = acc

    return pl.pallas_call(
        body, name=name, out_shape=S(shape, F32), in_specs=[VMEM_SPEC], out_specs=VMEM_SPEC,
        scratch_shapes=[pltpu.VMEM((8,) + shape, F32), pltpu.SemaphoreType.DMA((7,)), pltpu.SemaphoreType.DMA((7,))],
    )(buf)


BIG = ("ffn1_w_gu", "ffn1_w_down", "w_in", "w_out", "xattn_wq", "xattn_wkv", "xattn_wo", "ffn2_w_gu", "ffn2_w_down")
KIND = {"ffn1_w_gu": "col", "ffn1_w_down": "row", "w_in": "col", "w_out": "row", "xattn_wq": "row",
        "xattn_wkv": "col", "xattn_wo": "row", "ffn2_w_gu": "col", "ffn2_w_down": "row"}
WEIGHTS = ("rel_bias", "ffn1_norm", "ffn1_w_gu", "ffn1_w_down", "mix_norm", "w_in", "sinks", "conv_w", "w_out",
           "xattn_norm", "mem_norm", "xattn_wq", "xattn_wkv", "xattn_wo", "ffn2_norm", "ffn2_w_gu", "ffn2_w_down",
           "final_norm")
SMALL_ROWS = 16
GAIN_ROW = {"ffn1_norm": 0, "mix_norm": 1, "xattn_norm": 2, "mem_norm": 3, "ffn2_norm": 4, "final_norm": 5}
CONV_ROW, SINK_ROW, BIAS_ROW, LOSS_ROW = 6, 9, 10, 11


def _local_step(x, mem, pos, target, w, gains, rel_bias, sinks, conv_w):
    t, d = x.shape
    pos_col = pos.reshape(t, 1)
    pos_row = pos.reshape(1, t)
    bias_t = _bias_build(rel_bias)

    x1, g1, u1 = _ffn_fwd(x, gains["ffn1_norm"], w["ffn1_w_gu"], w["ffn1_w_down"], "ffn1_fwd")
    qkv, e = _mix_proj(x1, gains["mix_norm"], w["w_in"])
    attn = _swa_fwd(qkv, pos_col, pos_row, bias_t, sinks)
    x2, merged = _mix_out_fwd(e, attn, conv_w, w["w_out"], x1)
    mh, kv = _mem_kv(mem, gains["mem_norm"], w["xattn_wkv"])
    x3, qx, o = _xattn_fwd(x2, gains["xattn_norm"], w["xattn_wq"], kv, w["xattn_wo"])
    x4, g2, u2 = _ffn_fwd(x3, gains["ffn2_norm"], w["ffn2_w_gu"], w["ffn2_w_down"], "ffn2_fwd")
    dx4, d_final, loss = _loss_bwd(x4, gains["final_norm"], target)

    grads = {}
    dx3, d_ffn2, dgu2, a2, h4, dyh4 = _ffn_bwd(dx4, x3, gains["ffn2_norm"], g2, u2, w["ffn2_w_gu"], w["ffn2_w_down"],
                                               "ffn2_bwd")
    grads["ffn2_w_gu"] = _dw(h4, dgu2, 1408, "dw_ffn2_gu")
    grads["ffn2_w_down"] = _dw(a2, dyh4, 512, "dw_ffn2_down")
    dx2, d_xattn, dkv, dxh3, h3, dqx = _xattn_bwd(dx3, x2, gains["xattn_norm"], qx, o, kv, w["xattn_wq"], w["xattn_wo"])
    grads["xattn_wo"] = _dw(o, dxh3, 1024, "dw_wo")
    grads["xattn_wq"] = _dw(h3, dqx, 1024, "dw_wq")
    grads["xattn_wkv"], d_mem = _mem_bwd(dkv, mh, mem, gains["mem_norm"], w["xattn_wkv"])
    dattn, de, dcw, dxh2 = _mix_out_bwd(dx2, e, attn, conv_w, w["w_out"])
    grads["w_out"] = _dw(merged, dxh2, 1024, "dw_wout")
    dq, dkvs, gb, dsk = _swa_bwd(qkv, dattn, pos_col, pos_row, bias_t, sinks)
    d_rel_bias, d_sinks = _bias_reduce(gb, dsk)
    dx1, d_mix, h2 = _mix_in_bwd(dq, dkvs, de, x1, gains["mix_norm"], w["w_in"], dx2)
    n_in = w["w_in"].shape[1]
    gw = _dw(h2, dq, 512, "dw_win_q", ncols=n_in)
    gw = _dw(h2, dkvs, 512, "dw_win_kv", into=gw, col_block=dq.shape[1] // 512, ncols=n_in)
    grads["w_in"] = _dw(h2, de, 512, "dw_win_e", into=gw, col_block=(dq.shape[1] + dkvs.shape[1]) // 512, ncols=n_in)
    dx0, d_ffn1, dgu1, a1, h1, dyh1 = _ffn_bwd(dx1, x, gains["ffn1_norm"], g1, u1, w["ffn1_w_gu"], w["ffn1_w_down"],
                                               "ffn1_bwd")
    grads["ffn1_w_gu"] = _dw(h1, dgu1, 1408, "dw_ffn1_gu")
    grads["ffn1_w_down"] = _dw(a1, dyh1, 512, "dw_ffn1_down")

    pad = lambda v: jnp.pad(v.reshape(1, -1), ((0, 0), (0, d - v.size)))
    small = jnp.concatenate([
        d_ffn1, d_mix, d_xattn, d_mem, d_ffn2, d_final, dcw[0:3], pad(d_sinks), pad(d_rel_bias), pad(loss[0, 0:1]),
        jnp.zeros((SMALL_ROWS - LOSS_ROW - 1, d), F32)], axis=0)
    return dx0, grads, small


def kernel(x, mem, positions, rel_bias, ffn1_norm, ffn1_w_gu, ffn1_w_down, mix_norm, w_in, sinks, conv_w, w_out, xattn_norm, mem_norm, xattn_wq, xattn_wkv, xattn_wo, ffn2_norm, ffn2_w_gu, ffn2_w_down, final_norm, loss_target, m_rel_bias, m_ffn1_norm, m_ffn1_w_gu, m_ffn1_w_down, m_mix_norm, m_w_in, m_sinks, m_conv_w, m_w_out, m_xattn_norm, m_mem_norm, m_xattn_wq, m_xattn_wkv, m_xattn_wo, m_ffn2_norm, m_ffn2_w_gu, m_ffn2_w_down, m_final_norm, v_rel_bias, v_ffn1_norm, v_ffn1_w_gu, v_ffn1_w_down, v_mix_norm, v_w_in, v_sinks, v_conv_w, v_w_out, v_xattn_norm, v_mem_norm, v_xattn_wq, v_xattn_wkv, v_xattn_wo, v_ffn2_norm, v_ffn2_w_gu, v_ffn2_w_down, v_final_norm):
    args = dict(locals())
    wts = {k: args[k] for k in WEIGHTS}
    mom = {k: args["m_" + k] for k in WEIGHTS}
    var = {k: args["v_" + k] for k in WEIGHTS}
    d = x.shape[-1]
    s_me = 2 * lax.axis_index("x") + lax.axis_index("y")
    kinds = [KIND[k] for k in BIG]

    shards = [wts[k][0].astype(BF) for k in BIG]
    gathered = _all_gather_weights(shards, kinds)
    whole = {k: (gw if KIND[k] == "col" else gw.reshape(-1, gw.shape[-1])) for k, gw in zip(BIG, gathered)}
    cw_cols = conv_w.shape[-1]
    placed = lax.dynamic_update_slice(jnp.zeros((SMALL_ROWS, d), F32), 0.5 * conv_w[0], (0, s_me * cw_cols))
    conv_whole = _all_reduce_small(placed, "gather_conv_w")[0:3]

    gains = {k: wts[k].reshape(1, d) for k in GAIN_ROW}
    dx0, grads, small = _local_step(x[0], mem[0], positions[0], loss_target[0], whole, gains, rel_bias, sinks,
                                    conv_whole)

    gnat = [grads[k] if KIND[k] == "col" else grads[k].reshape(4, -1, grads[k].shape[-1]) for k in BIG]
    halves = _reduce_siblings(gnat, kinds)
    n = len(BIG)
    parts = []
    for i, k in enumerate(BIG):
        mine, sib = halves[i], halves[n + i]
        if KIND[k] == "row":
            mine, sib = mine.reshape(-1, mine.shape[-1]), sib.reshape(-1, sib.shape[-1])
        p = _add_bf16(mine, sib, "pair_sum_" + k)
        parts.append(p if KIND[k] == "col" else p.reshape(4, -1, p.shape[-1]))
    slabs = _reduce_chips(parts, kinds)
    reduced = [_sum4(sl, "chip_sum_" + k) for sl, k in zip(slabs, BIG)]
    shard_grads = dict(zip(BIG, _share_halves(reduced)))

    small_sum = _all_reduce_small(small, "reduce_small")
    loss = small_sum[LOSS_ROW, 0]

    out_g, out_d, out_m, out_v = {}, {}, {}, {}
    for k in BIG:
        g2d = shard_grads[k]
        dl, mn, vn = _adamw(wts[k][0], g2d, mom[k][0], var[k][0], "adamw_" + k)
        out_g[k], out_d[k], out_m[k], out_v[k] = g2d[None], dl[None], mn[None], vn[None]

    conv_g = lax.dynamic_slice(small_sum, (CONV_ROW, s_me * cw_cols), (3, cw_cols))

    def pack(src, conv_block):
        rows = [None] * SMALL_ROWS
        for k, r in GAIN_ROW.items():
            rows[r] = src[k].reshape(1, d)
        for j in range(3):
            rows[CONV_ROW + j] = jnp.pad(conv_block[j:j + 1], ((0, 0), (0, d - cw_cols)))
        rows[SINK_ROW] = jnp.pad(src["sinks"].reshape(1, -1), ((0, 0), (0, d - src["sinks"].size)))
        rows[BIAS_ROW] = jnp.pad(src["rel_bias"].reshape(1, -1), ((0, 0), (0, d - src["rel_bias"].size)))
        return jnp.concatenate([r if r is not None else jnp.zeros((1, d), F32) for r in rows], axis=0)

    small_g = {k: small_sum[r] for k, r in GAIN_ROW.items()}
    small_g["sinks"] = small_sum[SINK_ROW, 0:sinks.size]
    small_g["rel_bias"] = small_sum[BIAS_ROW, 0:rel_bias.size]
    gp = pack(small_g, conv_g)
    dl, mn, vn = _adamw(pack(wts, conv_w[0]), gp, pack(mom, m_conv_w[0]), pack(var, v_conv_w[0]), "adamw_small")

    def unpack(buf, k):
        if k in GAIN_ROW:
            return buf[GAIN_ROW[k]].reshape(wts[k].shape)
        if k == "conv_w":
            return buf[CONV_ROW:CONV_ROW + 3, 0:cw_cols][None]
        if k == "sinks":
            return buf[SINK_ROW, 0:sinks.size].reshape(sinks.shape)
        return buf[BIAS_ROW, 0:rel_bias.size].reshape(rel_bias.shape)

    for k in WEIGHTS:
        if k not in KIND:
            out_g[k], out_d[k], out_m[k], out_v[k] = unpack(gp, k), unpack(dl, k), unpack(mn, k), unpack(vn, k)

    return (loss, dx0[None], *[out_g[k] for k in WEIGHTS], *[out_d[k] for k in WEIGHTS],
            *[out_m[k] for k in WEIGHTS], *[out_v[k] for k in WEIGHTS])
```

```python
import functools
import math

import jax
import jax.numpy as jnp
from jax import lax
from jax.experimental import pallas as pl
from jax.experimental.pallas import tpu as pltpu

BF = jnp.bfloat16
F32 = jnp.float32
I32 = jnp.int32
S = jax.ShapeDtypeStruct

EPS = 1e-6
NEG = -1e30
POS_PAD = 1 << 30
WINDOW = 128
BLOCK = 128
HEAD_DIM = 64
SWA_HEADS = 16
SWA_KV_HEADS = 4
SWA_GROUP = SWA_HEADS // SWA_KV_HEADS
MEM_HEADS = 4
REL_BUCKETS = 32
REL_MAX_DIST = 128
ADAM_LR = 0.001
ADAM_B1 = 0.9
ADAM_B2 = 0.999
ADAM_EPS = 1e-08
ADAM_WD = 0.01
ADAM_STEP = 10

V7X_VMEM_LIMIT_BYTES = 56 * 1024 * 1024
MESH = pl.DeviceIdType.MESH
ANY = pl.BlockSpec(memory_space=pl.ANY)
VMEM_SPEC = pl.BlockSpec(memory_space=pltpu.VMEM)
SMEM_SPEC = pl.BlockSpec(memory_space=pltpu.SMEM)


def _params(sem=None):
    return pltpu.CompilerParams(dimension_semantics=sem, vmem_limit_bytes=V7X_VMEM_LIMIT_BYTES)


def _resident(shape):
    nd = len(shape)
    return pl.BlockSpec(shape, lambda *_: (0,) * nd, pipeline_mode=pl.Buffered(1))


def _acc_spec(shape):
    nd = len(shape)
    return pl.BlockSpec(shape, lambda *_: (0,) * nd)


def _call(body, *, name, grid, out_shape, in_specs, out_specs, args, sem, scratch_shapes=(), hook=None):
    if hook is None:
        return pl.pallas_call(body, name=name, grid=grid, out_shape=out_shape, in_specs=in_specs, out_specs=out_specs,
                              scratch_shapes=list(scratch_shapes), compiler_params=_params(sem))(*args)
    n_in, n_out, n_scr = len(in_specs), len(out_shape), len(scratch_shapes)
    h_in, h_out = len(hook.ins), len(hook.out_shape)
    last = grid[0] - 1

    def hosted(*refs):
        k_in, x_in = refs[:n_in], refs[n_in:n_in + h_in]
        o0 = n_in + h_in
        k_out, x_out = refs[o0:o0 + n_out], refs[o0 + n_out:o0 + n_out + h_out]
        s0 = o0 + n_out + h_out
        k_scr, x_scr = refs[s0:s0 + n_scr], refs[s0 + n_scr:]

        @pl.when(pl.program_id(0) == 0)
        def _():
            hook.start(x_in, x_out, x_scr)
        body(*k_in, *k_out, *k_scr)

        @pl.when(pl.program_id(0) == last)
        def _():
            hook.finish(x_in, x_out, x_scr)

    res = pl.pallas_call(
        hosted, name=name, grid=grid, out_shape=tuple(out_shape) + tuple(hook.out_shape),
        in_specs=list(in_specs) + [ANY] * h_in, out_specs=tuple(out_specs) + (ANY,) * h_out,
        scratch_shapes=list(scratch_shapes) + list(hook.scratch), compiler_params=_params(("arbitrary",) * len(grid)),
    )(*args, *hook.ins)
    hook.results = res[n_out:]
    return res[:n_out]


def _nn(a, b):
    return jnp.dot(a, b, preferred_element_type=F32)


def _nt(a, b):
    return lax.dot_general(a, b, (((1,), (1,)), ((), ())), preferred_element_type=F32)


def _tn(a, b):
    return lax.dot_general(a, b, (((0,), (0,)), ((), ())), preferred_element_type=F32)


def _sigmoid(v):
    return 1.0 / (1.0 + jnp.exp(-v))


def _rms(x):
    r = lax.rsqrt(jnp.mean(x * x, axis=-1, keepdims=True) + EPS)
    return x * r, r


def _rms_bwd(dh, n, r, g):
    dn = dh * g
    dx = r * (dn - n * jnp.mean(dn * n, axis=-1, keepdims=True))
    return dx, jnp.sum(dh * n, axis=0, keepdims=True)


def _ffn_fwd(x, gn, wgu, wd, name, hook=None):
    t, d = x.shape
    f = wd.shape[0]
    tm, fc = 256, 1408

    def body(x_ref, gn_ref, wgu_ref, wd_ref, xo_ref, g_ref, u_ref):
        xv = x_ref[...]
        n, _ = _rms(xv)
        h = (n * gn_ref[...]).astype(BF)
        acc = jnp.zeros((tm, d), F32)
        for c0 in range(0, f, fc):
            g = _nn(h, wgu_ref[:, c0:c0 + fc])
            u = _nn(h, wgu_ref[:, f + c0:f + c0 + fc])
            g_ref[:, c0:c0 + fc] = g.astype(BF)
            u_ref[:, c0:c0 + fc] = u.astype(BF)
            a = (g * _sigmoid(g)) * u
            acc = acc + _nn(a.astype(BF), wd_ref[c0:c0 + fc, :])
        xo_ref[...] = xv + 0.5 * acc

    return _call(
        body, name=name, grid=(t // tm,),
        out_shape=(S((t, d), F32), S((t, f), BF), S((t, f), BF)),
        in_specs=[pl.BlockSpec((tm, d), lambda i: (i, 0)), _resident((1, d)), _resident(wgu.shape), _resident(wd.shape)],
        out_specs=(pl.BlockSpec((tm, d), lambda i: (i, 0)), pl.BlockSpec((tm, f), lambda i: (i, 0)),
                   pl.BlockSpec((tm, f), lambda i: (i, 0))),
        sem=("parallel",), args=(x, gn, wgu, wd), hook=hook)


def _mix_proj(x, gn, w_in, hook=None):
    t, d = x.shape
    tm = 256
    nqkv = 1536
    ne = w_in.shape[1] - nqkv

    def body(x_ref, gn_ref, w_ref, qkv_ref, e_ref):
        n, _ = _rms(x_ref[...])
        h = (n * gn_ref[...]).astype(BF)
        qkv_ref[...] = _nn(h, w_ref[:, 0:nqkv]).astype(BF)
        for c0 in range(0, ne, 1024):
            e_ref[:, c0:c0 + 1024] = _nn(h, w_ref[:, nqkv + c0:nqkv + c0 + 1024])

    return _call(
        body, name="mix_proj", grid=(t // tm,),
        out_shape=(S((t, nqkv), BF), S((t, ne), F32)),
        in_specs=[pl.BlockSpec((tm, d), lambda i: (i, 0)), _resident((1, d)), _resident(w_in.shape)],
        out_specs=(pl.BlockSpec((tm, nqkv), lambda i: (i, 0)), pl.BlockSpec((tm, ne), lambda i: (i, 0))),
        sem=("parallel",), args=(x, gn, w_in), hook=hook)


def _t5_bucket(rel):
    n = jnp.maximum(rel, 0)
    max_exact = REL_BUCKETS // 2
    nf = jnp.maximum(n, 1).astype(F32)
    large = max_exact + (jnp.log(nf / max_exact) / math.log(REL_MAX_DIST / max_exact)
                         * (REL_BUCKETS - max_exact)).astype(I32)
    large = jnp.minimum(large, REL_BUCKETS - 1)
    return jnp.where(n < max_exact, n, large)


def _block_rel():
    i = lax.broadcasted_iota(I32, (BLOCK, 2 * BLOCK), 0)
    j = lax.broadcasted_iota(I32, (BLOCK, 2 * BLOCK), 1)
    return i + BLOCK - j


def _bias_build(rel_bias):
    def body(rb_ref, o_ref):
        bucket = _t5_bucket(_block_rel())
        for h in range(SWA_HEADS):
            acc = jnp.zeros((BLOCK, 2 * BLOCK), F32)
            for b in range(REL_BUCKETS):
                acc = jnp.where(bucket == b, rb_ref[b, h], acc)
            o_ref[h] = acc

    return pl.pallas_call(
        body, name="bias_build", out_shape=S((SWA_HEADS, BLOCK, 2 * BLOCK), F32),
        in_specs=[SMEM_SPEC], out_specs=VMEM_SPEC,
    )(rel_bias)


GROUP_ROWS = SWA_GROUP * BLOCK


def _swa_visible(b, pq_ref, pkp_ref, pkc_ref):
    pk = jnp.concatenate([pkp_ref[...], pkc_ref[...]], axis=1)
    col = lax.broadcasted_iota(I32, (1, 2 * BLOCK), 1)
    pk = jnp.where(jnp.logical_and(b == 0, col < BLOCK), POS_PAD, pk)
    rel = jnp.concatenate([pq_ref[...]] * SWA_GROUP, axis=0) - pk
    return jnp.logical_and(rel >= 0, rel < WINDOW)


def _group_heads(ref, hk):
    h0 = hk * SWA_GROUP
    return jnp.concatenate([ref[:, (h0 + g) * HEAD_DIM:(h0 + g + 1) * HEAD_DIM] for g in range(SWA_GROUP)], axis=0)


def _group_sinks(sink_ref, hk):
    row = lax.broadcasted_iota(I32, (GROUP_ROWS, 1), 0)
    col = jnp.zeros((GROUP_ROWS, 1), F32) + sink_ref[0, hk * SWA_GROUP]
    for g in range(1, SWA_GROUP):
        col = jnp.where(row >= g * BLOCK, sink_ref[0, hk * SWA_GROUP + g], col)
    return col


def _swa_probs(qg, kh, vis, bias, sink):
    s = _nt(qg, kh) * (HEAD_DIM ** -0.5)
    s = jnp.where(vis, s + bias, NEG)
    m = jnp.maximum(jnp.max(s, axis=-1, keepdims=True), sink)
    p = jnp.exp(s - m)
    ps = jnp.exp(sink - m)
    inv = 1.0 / (jnp.sum(p, axis=-1, keepdims=True) + ps)
    return p * inv, ps * inv


def _swa_fwd(qkv, pos_col, pos_row, bias_t, sinks, hook=None):
    t = qkv.shape[0]
    nb = t // BLOCK
    qw = SWA_HEADS * HEAD_DIM
    kw = SWA_KV_HEADS * HEAD_DIM

    def body(q_ref, kp_ref, kc_ref, vp_ref, vc_ref, pq_ref, pkp_ref, pkc_ref, bias_ref, sink_ref, o_ref):
        b = pl.program_id(0)
        vis = _swa_visible(b, pq_ref, pkp_ref, pkc_ref)
        k2 = jnp.concatenate([kp_ref[...], kc_ref[...]], axis=0)
        v2 = jnp.concatenate([vp_ref[...], vc_ref[...]], axis=0)
        for hk in range(SWA_KV_HEADS):
            kh = k2[:, hk * HEAD_DIM:(hk + 1) * HEAD_DIM]
            vh = v2[:, hk * HEAD_DIM:(hk + 1) * HEAD_DIM]
            pn, _ = _swa_probs(_group_heads(q_ref, hk), kh, vis, bias_ref[hk], _group_sinks(sink_ref, hk))
            o = _nn(pn.astype(BF), vh)
            for g in range(SWA_GROUP):
                h = hk * SWA_GROUP + g
                o_ref[:, h * HEAD_DIM:(h + 1) * HEAD_DIM] = o[g * BLOCK:(g + 1) * BLOCK]

    prev = lambda b: jnp.maximum(b - 1, 0)
    return _call(
        body, name="swa_fwd", grid=(nb,), out_shape=(S((t, qw), F32),),
        in_specs=[
            pl.BlockSpec((BLOCK, qw), lambda b: (b, 0)),
            pl.BlockSpec((BLOCK, kw), lambda b: (prev(b), qw // kw)),
            pl.BlockSpec((BLOCK, kw), lambda b: (b, qw // kw)),
            pl.BlockSpec((BLOCK, kw), lambda b: (prev(b), qw // kw + 1)),
            pl.BlockSpec((BLOCK, kw), lambda b: (b, qw // kw + 1)),
            pl.BlockSpec((BLOCK, 1), lambda b: (b, 0)),
            pl.BlockSpec((1, BLOCK), lambda b: (0, prev(b))),
            pl.BlockSpec((1, BLOCK), lambda b: (0, b)),
            _resident(bias_t.shape),
            SMEM_SPEC,
        ],
        out_specs=(pl.BlockSpec((BLOCK, qw), lambda b: (b, 0)),),
        sem=("parallel",), args=(qkv, qkv, qkv, qkv, qkv, pos_col, pos_row, pos_row, bias_t, sinks), hook=hook)[0]


def _conv_taps(z, zh, first):
    tm = z.shape[0]
    zh = jnp.where(first, 0.0, zh)
    row = lax.broadcasted_iota(I32, (tm, 1), 0)
    z1 = jnp.where(row == 0, zh[7:8, :], pltpu.roll(z, 1, 0))
    z2 = jnp.where(row == 0, zh[6:7, :], jnp.where(row == 1, zh[7:8, :], pltpu.roll(z, 2, 0)))
    return z1, z2


def _mix_out_fwd(e, attn, conv_w, w_out, x, hook=None):
    t, d = x.shape
    tm = 256
    hb = tm // 8

    def body(c_ref, b_ref, u_ref, ga_ref, gc_ref, ch_ref, uh_ref, attn_ref, cw_ref, w_ref, x_ref, xo_ref, mg_ref):
        i = pl.program_id(0)
        z = c_ref[...] * u_ref[...]
        z1, z2 = _conv_taps(z, ch_ref[...] * uh_ref[...], i == 0)
        s = cw_ref[0:1, :] * z2 + cw_ref[1:2, :] * z1 + cw_ref[2:3, :] * z
        conv = b_ref[...] * s
        merged = (_sigmoid(ga_ref[...]) * attn_ref[...] + _sigmoid(gc_ref[...]) * conv).astype(BF)
        mg_ref[...] = merged
        xo_ref[...] = x_ref[...] + _nn(merged, w_ref[...])

    ecol = lambda cb: pl.BlockSpec((tm, d), lambda i: (i, cb))
    halo = lambda cb: pl.BlockSpec((8, d), lambda i: (jnp.maximum(i * hb - 1, 0), cb))
    row = pl.BlockSpec((tm, d), lambda i: (i, 0))
    return _call(
        body, name="mix_out_fwd", grid=(t // tm,),
        out_shape=(S((t, d), F32), S((t, d), BF)),
        in_specs=[ecol(0), ecol(1), ecol(2), ecol(3), ecol(4), halo(0), halo(2), row,
                  _resident(conv_w.shape), _resident(w_out.shape), row],
        out_specs=(row, row),
        sem=("parallel",), args=(e, e, e, e, e, e, e, attn, conv_w, w_out, x), hook=hook)


def _mem_kv(mem, gm, wkv):
    m, d = mem.shape

    def body(mem_ref, gm_ref, w_ref, mh_ref, kv_ref):
        n, _ = _rms(mem_ref[...])
        mh = (n * gm_ref[...]).astype(BF)
        mh_ref[...] = mh
        kv_ref[...] = _nn(mh, w_ref[...]).astype(BF)

    return pl.pallas_call(
        body, name="mem_kv", out_shape=(S((m, d), BF), S((m, wkv.shape[1]), BF)),
        compiler_params=_params(),
    )(mem, gm, wkv)


def _xattn_probs(qh, kh):
    s = _nt(qh, kh) * (kh.shape[1] ** -0.5)
    p = jnp.exp(s - jnp.max(s, axis=-1, keepdims=True))
    return p * (1.0 / jnp.sum(p, axis=-1, keepdims=True))


def _xattn_fwd(x, gn, wq, kv, wo):
    t, d = x.shape
    tm = 256
    hd = d // MEM_HEADS

    def body(x_ref, gn_ref, wq_ref, kv_ref, wo_ref, xo_ref, q_ref, o_ref):
        xv = x_ref[...]
        n, _ = _rms(xv)
        q = _nn((n * gn_ref[...]).astype(BF), wq_ref[...]).astype(BF)
        q_ref[...] = q
        outs = []
        for hh in range(MEM_HEADS):
            p = _xattn_probs(q[:, hh * hd:(hh + 1) * hd], kv_ref[:, hh * hd:(hh + 1) * hd])
            outs.append(_nn(p.astype(BF), kv_ref[:, d + hh * hd:d + (hh + 1) * hd]))
        o = jnp.concatenate(outs, axis=1).astype(BF)
        o_ref[...] = o
        xo_ref[...] = xv + _nn(o, wo_ref[...])

    row = pl.BlockSpec((tm, d), lambda i: (i, 0))
    return pl.pallas_call(
        body, name="xattn_fwd", grid=(t // tm,),
        out_shape=(S((t, d), F32), S((t, d), BF), S((t, d), BF)),
        in_specs=[row, _resident((1, d)), _resident(wq.shape), _resident(kv.shape), _resident(wo.shape)],
        out_specs=(row, row, row),
        compiler_params=_params(("parallel",)),
    )(x, gn, wq, kv, wo)


def _loss_bwd(x, gf, target):
    t, d = x.shape
    tm = 512

    def body(x_ref, gf_ref, t_ref, dx_ref, dg_ref, loss_ref):
        @pl.when(pl.program_id(0) == 0)
        def _():
            dg_ref[...] = jnp.zeros_like(dg_ref)
            loss_ref[...] = jnp.zeros_like(loss_ref)
        n, r = _rms(x_ref[...])
        g = gf_ref[...]
        err = n * g - t_ref[...]
        loss_ref[...] += 0.5 * jnp.sum(jnp.sum(err * err, axis=-1, keepdims=True) / d, axis=0, keepdims=True)
        dx, dg = _rms_bwd(err / d, n, r, g)
        dx_ref[...] = dx
        dg_ref[...] += dg

    row = pl.BlockSpec((tm, d), lambda i: (i, 0))
    return pl.pallas_call(
        body, name="loss_bwd", grid=(t // tm,),
        out_shape=(S((t, d), F32), S((1, d), F32), S((1, 128), F32)),
        in_specs=[row, _resident((1, d)), row],
        out_specs=(row, _acc_spec((1, d)), _acc_spec((1, 128))),
        compiler_params=_params(("arbitrary",)),
    )(x, gf, target)


def _ffn_bwd(dxo, x, gn, g, u, wgu, wd, name, hook=None):
    t, d = x.shape
    f = wd.shape[0]
    tm, fc = 256, 1408

    def body(dxo_ref, x_ref, gn_ref, g_ref, u_ref, wgu_ref, wd_ref, dx_ref, dgn_ref, dgu_ref, a_ref, h_ref, dyh_ref):
        @pl.when(pl.program_id(0) == 0)
        def _():
            dgn_ref[...] = jnp.zeros_like(dgn_ref)
        dxov = dxo_ref[...]
        dyh = (0.5 * dxov).astype(BF)
        dyh_ref[...] = dyh
        n, r = _rms(x_ref[...])
        gnv = gn_ref[...]
        h_ref[...] = (n * gnv).astype(BF)
        dh = jnp.zeros((tm, d), F32)
        for c0 in range(0, f, fc):
            gv = g_ref[:, c0:c0 + fc].astype(F32)
            uv = u_ref[:, c0:c0 + fc].astype(F32)
            da = _nt(dyh, wd_ref[c0:c0 + fc, :])
            sg = _sigmoid(gv)
            silu = gv * sg
            a_ref[:, c0:c0 + fc] = (silu * uv).astype(BF)
            dg = (da * uv * (sg * (1.0 + gv * (1.0 - sg)))).astype(BF)
            du = (da * silu).astype(BF)
            dgu_ref[:, c0:c0 + fc] = dg
            dgu_ref[:, f + c0:f + c0 + fc] = du
            dh = dh + _nt(dg, wgu_ref[:, c0:c0 + fc]) + _nt(du, wgu_ref[:, f + c0:f + c0 + fc])
        dx, dgn = _rms_bwd(dh, n, r, gnv)
        dx_ref[...] = dxov + dx
        dgn_ref[...] += dgn

    row = pl.BlockSpec((tm, d), lambda i: (i, 0))
    frow = pl.BlockSpec((tm, f), lambda i: (i, 0))
    return _call(
        body, name=name, grid=(t // tm,),
        out_shape=(S((t, d), F32), S((1, d), F32), S((t, 2 * f), BF), S((t, f), BF), S((t, d), BF), S((t, d), BF)),
        in_specs=[row, row, _resident((1, d)), frow, frow, _resident(wgu.shape), _resident(wd.shape)],
        out_specs=(row, _acc_spec((1, d)), pl.BlockSpec((tm, 2 * f), lambda i: (i, 0)), frow, row, row),
        sem=("arbitrary",), args=(dxo, x, gn, g, u, wgu, wd), hook=hook)


def _dw(a, b, tn, name, into=None, col_block=0, ncols=None):
    t, ka = a.shape
    nb = b.shape[1]
    tt = 1024
    nt = t // tt
    ncols = nb if ncols is None else ncols

    def body(*refs):
        a_ref, b_ref = refs[0], refs[1]
        o_ref, acc_ref = refs[-2], refs[-1]
        k = pl.program_id(1)

        @pl.when(k == 0)
        def _():
            acc_ref[...] = jnp.zeros_like(acc_ref)
        acc_ref[...] += _tn(a_ref[...], b_ref[...])

        @pl.when(k == nt - 1)
        def _():
            o_ref[...] = acc_ref[...].astype(BF)

    in_specs = [pl.BlockSpec((tt, ka), lambda j, k: (k, 0)), pl.BlockSpec((tt, tn), lambda j, k: (k, j))]
    args = [a, b]
    aliases = {}
    if into is not None:
        in_specs.append(ANY)
        args.append(into)
        aliases = {2: 0}
    return pl.pallas_call(
        body, name=name, grid=(nb // tn, nt), out_shape=S((ka, ncols), BF),
        in_specs=in_specs, out_specs=pl.BlockSpec((ka, tn), lambda j, k: (0, j + col_block)),
        scratch_shapes=[pltpu.VMEM((ka, tn), F32)], input_output_aliases=aliases,
        compiler_params=_params(("parallel", "arbitrary")),
    )(*args)


def _xattn_bwd(dxo, x, gn, q, kv, wq, wo, hook=None):
    t, d = x.shape
    tm = 256
    hd = d // MEM_HEADS
    nkv = kv.shape[0]

    def body(dxo_ref, x_ref, gn_ref, q_ref, kv_ref, wq_ref, wo_ref, dx_ref, dgn_ref, dkv_ref, dxh_ref, h_ref, dq_ref):
        @pl.when(pl.program_id(0) == 0)
        def _():
            dgn_ref[...] = jnp.zeros_like(dgn_ref)
            dkv_ref[...] = jnp.zeros_like(dkv_ref)
        dxov = dxo_ref[...]
        dxh = dxov.astype(BF)
        dxh_ref[...] = dxh
        do = _nt(dxh, wo_ref[...]).astype(BF)
        dqs = []
        for hh in range(MEM_HEADS):
            lo, hi = hh * hd, (hh + 1) * hd
            qh = q_ref[:, lo:hi]
            kh = kv_ref[:, lo:hi]
            vh = kv_ref[:, d + lo:d + hi]
            doh = do[:, lo:hi]
            p = _xattn_probs(qh, kh)
            dp = _nt(doh, vh)
            ds = (p * (dp - jnp.sum(p * dp, axis=-1, keepdims=True)) * (hd ** -0.5)).astype(BF)
            dqs.append(_nn(ds, kh))
            dkv_ref[:, lo:hi] += _tn(ds, qh)
            dkv_ref[:, d + lo:d + hi] += _tn(p.astype(BF), doh)
        dq = jnp.concatenate(dqs, axis=1).astype(BF)
        dq_ref[...] = dq
        n, r = _rms(x_ref[...])
        gnv = gn_ref[...]
        h_ref[...] = (n * gnv).astype(BF)
        dx, dgn = _rms_bwd(_nt(dq, wq_ref[...]), n, r, gnv)
        dx_ref[...] = dxov + dx
        dgn_ref[...] += dgn

    row = pl.BlockSpec((tm, d), lambda i: (i, 0))
    return _call(
        body, name="xattn_bwd", grid=(t // tm,),
        out_shape=(S((t, d), F32), S((1, d), F32), S((nkv, 2 * d), F32), S((t, d), BF), S((t, d), BF), S((t, d), BF)),
        in_specs=[row, row, _resident((1, d)), row, _resident(kv.shape), _resident(wq.shape), _resident(wo.shape)],
        out_specs=(row, _acc_spec((1, d)), _acc_spec((nkv, 2 * d)), row, row, row),
        sem=("arbitrary",), args=(dxo, x, gn, q, kv, wq, wo), hook=hook)


def _mem_bwd(dkv, mh, mem, gm, wkv):
    m, d = mem.shape

    def body(dkv_ref, mh_ref, mem_ref, gm_ref, w_ref, dw_ref, dgm_ref):
        dkvb = dkv_ref[...].astype(BF)
        dw_ref[...] = _tn(mh_ref[...], dkvb).astype(BF)
        dmh = _nt(dkvb, w_ref[...])
        n, _ = _rms(mem_ref[...])
        dgm_ref[...] = jnp.sum(dmh * n, axis=0, keepdims=True)

    return pl.pallas_call(
        body, name="mem_bwd", out_shape=(S(wkv.shape, BF), S((1, d), F32)),
        compiler_params=_params(),
    )(dkv, mh, mem, gm, wkv)


def _mix_out_bwd(dxo, e, attn, conv_w, w_out, hook=None):
    t, d = attn.shape
    tm = 256
    hb = tm // 8
    nt = t // tm
    last8 = t // 8 - 1

    def body(dxo_ref, dxn_ref, c_ref, b_ref, u_ref, ga_ref, gc_ref, ch_ref, uh_ref, bn_ref, gcn_ref,
             attn_ref, cw_ref, w_ref, dattn_ref, de_ref, dcw_ref, dxh_ref):
        i = pl.program_id(0)

        @pl.when(i == 0)
        def _():
            dcw_ref[...] = jnp.zeros_like(dcw_ref)
        dxh = dxo_ref[...].astype(BF)
        dxh_ref[...] = dxh
        w = w_ref[...]
        dm = _nt(dxh, w)
        dmn = _nt(dxn_ref[...].astype(BF), w)
        cv, bv, uv = c_ref[...], b_ref[...], u_ref[...]
        sga = _sigmoid(ga_ref[...])
        sgc = _sigmoid(gc_ref[...])
        z = cv * uv
        z1, z2 = _conv_taps(z, ch_ref[...] * uh_ref[...], i == 0)
        w0, w1, w2 = cw_ref[0:1, :], cw_ref[1:2, :], cw_ref[2:3, :]
        s = w0 * z2 + w1 * z1 + w2 * z
        av = attn_ref[...]
        dattn_ref[...] = (dm * sga).astype(BF)
        dconv = dm * sgc
        ds = dconv * bv
        dsn = jnp.where(i == nt - 1, 0.0, dmn * _sigmoid(gcn_ref[...]) * bn_ref[...])
        row = lax.broadcasted_iota(I32, (tm, 1), 0)
        dsp1 = jnp.where(row == tm - 1, dsn[0:1, :], pltpu.roll(ds, tm - 1, 0))
        dsp2 = jnp.where(row == tm - 2, dsn[0:1, :], jnp.where(row == tm - 1, dsn[1:2, :], pltpu.roll(ds, tm - 2, 0)))
        dz = w2 * ds + w1 * dsp1 + w0 * dsp2
        de_ref[:, 0:d] = (dz * uv).astype(BF)
        de_ref[:, d:2 * d] = (dconv * s).astype(BF)
        de_ref[:, 2 * d:3 * d] = (dz * cv).astype(BF)
        de_ref[:, 3 * d:4 * d] = (dm * av * sga * (1.0 - sga)).astype(BF)
        de_ref[:, 4 * d:5 * d] = (dm * (bv * s) * sgc * (1.0 - sgc)).astype(BF)
        dcw_ref[0:1, :] += jnp.sum(ds * z2, axis=0, keepdims=True)
        dcw_ref[1:2, :] += jnp.sum(ds * z1, axis=0, keepdims=True)
        dcw_ref[2:3, :] += jnp.sum(ds * z, axis=0, keepdims=True)

    ecol = lambda cb: pl.BlockSpec((tm, d), lambda i: (i, cb))
    prev = lambda cb: pl.BlockSpec((8, d), lambda i: (jnp.maximum(i * hb - 1, 0), cb))
    nxt = lambda cb: pl.BlockSpec((8, d), lambda i: (jnp.minimum((i + 1) * hb, last8), cb))
    row = pl.BlockSpec((tm, d), lambda i: (i, 0))
    return _call(
        body, name="mix_out_bwd", grid=(nt,),
        out_shape=(S((t, d), BF), S((t, 5 * d), BF), S((8, d), F32), S((t, d), BF)),
        in_specs=[row, nxt(0), ecol(0), ecol(1), ecol(2), ecol(3), ecol(4), prev(0), prev(2), nxt(1), nxt(4),
                  row, _resident(conv_w.shape), _resident(w_out.shape)],
        out_specs=(row, pl.BlockSpec((tm, 5 * d), lambda i: (i, 0)), _acc_spec((8, d)), row),
        sem=("arbitrary",), args=(dxo, dxo, e, e, e, e, e, e, e, e, e, attn, conv_w, w_out), hook=hook)


def _swa_bwd(qkv, dattn, pos_col, pos_row, bias_t, sinks, hook=None):
    t = qkv.shape[0]
    nb = t // BLOCK
    qw = SWA_HEADS * HEAD_DIM
    kw = SWA_KV_HEADS * HEAD_DIM

    def body(q_ref, kp_ref, kc_ref, vp_ref, vc_ref, do_ref, pq_ref, pkp_ref, pkc_ref, bias_ref, sink_ref,
             dq_ref, dkv_ref, gb_ref, dsk_ref, carry_ref):
        b = pl.program_id(0)

        @pl.when(b == 0)
        def _():
            gb_ref[...] = jnp.zeros_like(gb_ref)
            dsk_ref[...] = jnp.zeros_like(dsk_ref)
            carry_ref[...] = jnp.zeros_like(carry_ref)

        @pl.when(b < nb)
        def _():
            vis = _swa_visible(b, pq_ref, pkp_ref, pkc_ref)
            k2 = jnp.concatenate([kp_ref[...], kc_ref[...]], axis=0)
            v2 = jnp.concatenate([vp_ref[...], vc_ref[...]], axis=0)
            for hk in range(SWA_KV_HEADS):
                lo, hi = hk * HEAD_DIM, (hk + 1) * HEAD_DIM
                kh = k2[:, lo:hi]
                vh = v2[:, lo:hi]
                qg = _group_heads(q_ref, hk)
                dog = _group_heads(do_ref, hk)
                pn, psn = _swa_probs(qg, kh, vis, bias_ref[hk], _group_sinks(sink_ref, hk))
                dp = _nt(dog, vh)
                delta = jnp.sum(pn * dp, axis=-1, keepdims=True)
                ds = pn * (dp - delta)
                gb_ref[hk] += ds
                dsk_ref[hk] += -psn * delta
                dsb = (ds * (HEAD_DIM ** -0.5)).astype(BF)
                dqg = _nn(dsb, kh).astype(BF)
                for g in range(SWA_GROUP):
                    h = hk * SWA_GROUP + g
                    dq_ref[:, h * HEAD_DIM:(h + 1) * HEAD_DIM] = dqg[g * BLOCK:(g + 1) * BLOCK]
                dk = _tn(dsb, qg)
                dv = _tn(pn.astype(BF), dog)
                dkv_ref[:, lo:hi] = (carry_ref[:, lo:hi] + dk[0:BLOCK]).astype(BF)
                dkv_ref[:, kw + lo:kw + hi] = (carry_ref[:, kw + lo:kw + hi] + dv[0:BLOCK]).astype(BF)
                carry_ref[:, lo:hi] = dk[BLOCK:2 * BLOCK]
                carry_ref[:, kw + lo:kw + hi] = dv[BLOCK:2 * BLOCK]

        @pl.when(b == nb)
        def _():
            dkv_ref[...] = carry_ref[...].astype(BF)

    cur = lambda b: jnp.minimum(b, nb - 1)
    prev = lambda b: jnp.maximum(cur(b) - 1, 0)
    return _call(
        body, name="swa_bwd", grid=(nb + 1,),
        out_shape=(S((t, qw), BF), S((t, 2 * kw), BF), S((SWA_KV_HEADS, GROUP_ROWS, 2 * BLOCK), F32),
                   S((SWA_KV_HEADS, GROUP_ROWS, 1), F32)),
        in_specs=[
            pl.BlockSpec((BLOCK, qw), lambda b: (cur(b), 0)),
            pl.BlockSpec((BLOCK, kw), lambda b: (prev(b), qw // kw)),
            pl.BlockSpec((BLOCK, kw), lambda b: (cur(b), qw // kw)),
            pl.BlockSpec((BLOCK, kw), lambda b: (prev(b), qw // kw + 1)),
            pl.BlockSpec((BLOCK, kw), lambda b: (cur(b), qw // kw + 1)),
            pl.BlockSpec((BLOCK, qw), lambda b: (cur(b), 0)),
            pl.BlockSpec((BLOCK, 1), lambda b: (cur(b), 0)),
            pl.BlockSpec((1, BLOCK), lambda b: (0, prev(b))),
            pl.BlockSpec((1, BLOCK), lambda b: (0, cur(b))),
            _resident(bias_t.shape),
            SMEM_SPEC,
        ],
        out_specs=(
            pl.BlockSpec((BLOCK, qw), lambda b: (cur(b), 0)),
            pl.BlockSpec((BLOCK, 2 * kw), lambda b: (jnp.maximum(b - 1, 0), 0)),
            _acc_spec((SWA_KV_HEADS, GROUP_ROWS, 2 * BLOCK)),
            _acc_spec((SWA_KV_HEADS, GROUP_ROWS, 1)),
        ),
        scratch_shapes=[pltpu.VMEM((BLOCK, 2 * kw), F32)],
        sem=("arbitrary",), args=(qkv, qkv, qkv, qkv, qkv, dattn, pos_col, pos_row, pos_row, bias_t, sinks), hook=hook)


def _bias_reduce(gb, dsk):
    def body(gb_ref, dsk_ref, drb_ref, dsink_ref):
        bucket = _t5_bucket(_block_rel())
        for b in range(REL_BUCKETS):
            mask = bucket == b
            for h in range(SWA_HEADS):
                drb_ref[b, h] = jnp.sum(jnp.where(mask, gb_ref[h], 0.0))
        for h in range(SWA_HEADS):
            dsink_ref[0, h] = jnp.sum(dsk_ref[h])

    return pl.pallas_call(
        body, name="bias_reduce", out_shape=(S((REL_BUCKETS, SWA_HEADS), F32), S((1, SWA_HEADS), F32)),
        in_specs=[VMEM_SPEC, VMEM_SPEC], out_specs=(SMEM_SPEC, SMEM_SPEC),
    )(gb, dsk)


def _mix_in_bwd(dq, dkv, de, x, gn, w_in, dxo):
    t, d = x.shape
    tm = 256
    nq, nkv, ne = dq.shape[1], dkv.shape[1], de.shape[1]

    def body(dq_ref, dkv_ref, de_ref, x_ref, gn_ref, w_ref, dxo_ref, dx_ref, dgn_ref, h_ref):
        @pl.when(pl.program_id(0) == 0)
        def _():
            dgn_ref[...] = jnp.zeros_like(dgn_ref)
        dh = _nt(dq_ref[...], w_ref[:, 0:nq]) + _nt(dkv_ref[...], w_ref[:, nq:nq + nkv])
        for c0 in range(0, ne, 1024):
            dh = dh + _nt(de_ref[:, c0:c0 + 1024], w_ref[:, nq + nkv + c0:nq + nkv + c0 + 1024])
        n, r = _rms(x_ref[...])
        gnv = gn_ref[...]
        h_ref[...] = (n * gnv).astype(BF)
        dx, dgn = _rms_bwd(dh, n, r, gnv)
        dx_ref[...] = dxo_ref[...] + dx
        dgn_ref[...] += dgn

    row = pl.BlockSpec((tm, d), lambda i: (i, 0))
    wide = lambda w: pl.BlockSpec((tm, w), lambda i: (i, 0))
    return pl.pallas_call(
        body, name="mix_in_bwd", grid=(t // tm,),
        out_shape=(S((t, d), F32), S((1, d), F32), S((t, d), BF)),
        in_specs=[wide(nq), wide(nkv), wide(ne), row, _resident((1, d)), _resident(w_in.shape), row],
        out_specs=(row, _acc_spec((1, d)), row),
        compiler_params=_params(("arbitrary",)),
    )(dq, dkv, de, x, gn, w_in, dxo)


def _row_tile(rows):
    for tr in (256, 352, 128, 64, 16, 8):
        if rows % tr == 0:
            return tr
    return rows


def _add_bf16(a, b, name):
    rows, cols = a.shape
    tr = _row_tile(rows)

    def body(a_ref, b_ref, o_ref):
        o_ref[...] = (a_ref[...].astype(F32) + b_ref[...].astype(F32)).astype(BF)

    blk = pl.BlockSpec((tr, cols), lambda i: (i, 0))
    return pl.pallas_call(body, name=name, grid=(rows // tr,), out_shape=S((rows, cols), BF),
                          in_specs=[blk, blk], out_specs=blk, compiler_params=_params(("parallel",)))(a, b)


def _sum4(parts, name):
    _, rows, cols = parts.shape
    tr = _row_tile(rows)

    def body(p_ref, o_ref):
        acc = p_ref[3].astype(F32)
        for k in range(3):
            acc = acc + p_ref[k].astype(F32)
        o_ref[...] = acc

    return pl.pallas_call(
        body, name=name, grid=(rows // tr,), out_shape=S((rows, cols), F32),
        in_specs=[pl.BlockSpec((4, tr, cols), lambda i: (0, i, 0))], out_specs=pl.BlockSpec((tr, cols), lambda i: (i, 0)),
        compiler_params=_params(("parallel",)))(parts)


def _adamw(w, g, m, v, name):
    rows, cols = w.shape
    tr = _row_tile(rows)

    def body(w_ref, g_ref, m_ref, v_ref, d_ref, mo_ref, vo_ref):
        gv = g_ref[...]
        mn = ADAM_B1 * m_ref[...] + (1.0 - ADAM_B1) * gv
        vn = ADAM_B2 * v_ref[...] + (1.0 - ADAM_B2) * (gv * gv)
        m_hat = mn / (1.0 - ADAM_B1 ** ADAM_STEP)
        v_hat = vn / (1.0 - ADAM_B2 ** ADAM_STEP)
        d_ref[...] = -ADAM_LR * (m_hat / (jnp.sqrt(v_hat) + ADAM_EPS) + ADAM_WD * w_ref[...])
        mo_ref[...] = mn
        vo_ref[...] = vn

    blk = pl.BlockSpec((tr, cols), lambda i: (i, 0))
    shp = S((rows, cols), F32)
    return pl.pallas_call(body, name=name, grid=(rows // tr,), out_shape=(shp, shp, shp),
                          in_specs=[blk] * 4, out_specs=(blk,) * 3, compiler_params=_params(("parallel",)))(w, g, m, v)


def _place():
    x, y, c = lax.axis_index("x"), lax.axis_index("y"), lax.axis_index("c")
    return x, y, c


OTHER_CHIPS = ((1, 0), (0, 1), (1, 1))


def _flip(v, f):
    return 1 - v if f else v


def _remote(src, dst, ssem, rsem, dev):
    return pltpu.make_async_remote_copy(src_ref=src, dst_ref=dst, send_sem=ssem, recv_sem=rsem,
                                        device_id=dev, device_id_type=MESH)


class _Stager:
    def __init__(self, buf, lsem, ssem):
        self.buf, self.lsem, self.ssem = buf, lsem, ssem
        self.slots = buf.shape[0]
        self.count = 0
        self.inflight = [None] * self.slots

    def _load(self, src):
        s = self.count % self.slots
        self.count += 1
        if self.inflight[s] is not None:
            self.inflight[s]()
        stage = self.buf.at[(s,) + tuple(pl.ds(0, n) for n in src.shape)]
        load = pltpu.make_async_copy(src, stage, self.lsem.at[s])
        load.start()
        load.wait()
        return s, stage

    def send(self, src, dst, rsem, dev):
        s, stage = self._load(src)
        cp = _remote(stage, dst, self.ssem.at[s], rsem, dev)
        cp.start()
        self.inflight[s] = cp.wait_send

    def copy(self, src, dst):
        s, stage = self._load(src)
        cp = pltpu.make_async_copy(stage, dst, self.ssem.at[s])
        cp.start()
        self.inflight[s] = cp.wait

    def drain(self):
        for wait in self.inflight:
            if wait is not None:
                wait()


STAGE_SLOTS = 3
STAGE_ROWS = 176


def _chunk_rows(rows):
    for rc in (STAGE_ROWS, 128, 64, 32, 16):
        if rows % rc == 0:
            return rc
    raise ValueError(rows)


def _stage_scratch(cols, dtype):
    return [pltpu.VMEM((STAGE_SLOTS, STAGE_ROWS, cols), dtype), pltpu.SemaphoreType.DMA((STAGE_SLOTS,)),
            pltpu.SemaphoreType.DMA((STAGE_SLOTS,))]


def _exchange(hook, name):
    n_in, n_out = len(hook.ins), len(hook.out_shape)

    def body(*refs):
        ins, outs, scr = refs[:n_in], refs[n_in:n_in + n_out], refs[n_in + n_out:]
        hook.start(ins, outs, scr)
        hook.finish(ins, outs, scr)

    return pl.pallas_call(
        body, name=name, out_shape=tuple(hook.out_shape), in_specs=[ANY] * n_in, out_specs=(ANY,) * n_out,
        scratch_shapes=list(hook.scratch), compiler_params=_params(),
    )(*hook.ins)


class _GatherHook:
    def __init__(self, shards, kinds):
        self.ins, self.kinds, n = list(shards), list(kinds), len(shards)
        self.out_shape = tuple(
            S((w.shape[0], 4 * w.shape[1]), BF) if k == "col" else S((4,) + w.shape, BF) for w, k in zip(shards, kinds))
        self.scratch = ([pltpu.SemaphoreType.DMA((n, 3)) for _ in range(3)]
                        + _stage_scratch(max(w.shape[1] for w in shards), BF))

    def _window(self, outs, i, s, half, r0=0, nrows=None):
        rows, cols = self.ins[i].shape
        rh = rows // 2
        nrows = rh if nrows is None else nrows
        start = pl.multiple_of(half * rh + r0, 16)
        if self.kinds[i] == "col":
            return outs[i].at[pl.ds(start, nrows), pl.ds(pl.multiple_of(s * cols, 128), cols)]
        return outs[i].at[s, pl.ds(start, nrows), :]

    def _sends(self, ins, outs, scr):
        ssem, rsem = scr[0], scr[1]
        x, y, c = _place()
        for i in range(len(ins)):
            rh = self.ins[i].shape[0] // 2
            src = ins[i].at[pl.ds(pl.multiple_of(c * rh, 16), rh), :]
            for j, (fx, fy) in enumerate(OTHER_CHIPS):
                yield _remote(src, self._window(outs, i, 2 * x + y, c), ssem.at[i, j], rsem.at[i, j],
                              (_flip(x, fx), _flip(y, fy), c))

    def start(self, ins, outs, scr):
        x, y, c = _place()
        for cp in self._sends(ins, outs, scr):
            cp.start()
        stager = _Stager(*scr[3:])
        for i in range(len(ins)):
            rows = self.ins[i].shape[0]
            rc = _chunk_rows(rows // 2)
            for r0 in range(0, rows, rc):
                stager.copy(ins[i].at[pl.ds(r0, rc), :], self._window(outs, i, 2 * x + y, 0, r0, rc))
        stager.drain()

    def finish(self, ins, outs, scr):
        ssem, rsem, frsem = scr[:3]
        x, y, c = _place()
        sibling = (x, y, 1 - c)
        stager = _Stager(*scr[3:])
        for i in range(len(ins)):
            rh = self.ins[i].shape[0] // 2
            rc = _chunk_rows(rh)
            for j, (fx, fy) in enumerate(OTHER_CHIPS):
                px, py = _flip(x, fx), _flip(y, fy)
                landed = self._window(outs, i, 2 * px + py, c)
                _remote(landed, landed, ssem.at[i, j], rsem.at[i, j], (px, py, c)).wait_recv()
                for r0 in range(0, rh, rc):
                    part = self._window(outs, i, 2 * px + py, c, r0, rc)
                    stager.send(part, part, frsem.at[i, j], sibling)
        for i in range(len(ins)):
            for j, (fx, fy) in enumerate(OTHER_CHIPS):
                passed = self._window(outs, i, 2 * _flip(x, fx) + _flip(y, fy), 1 - c)
                _remote(passed, passed, ssem.at[i, j], frsem.at[i, j], sibling).wait_recv()
        stager.drain()
        for cp in self._sends(ins, outs, scr):
            cp.wait_send()


def _reduce_siblings(grads, kinds, name):
    n = len(grads)

    def half_shape(gr, k):
        return S((gr.shape[0] // 2, gr.shape[1]), BF) if k == "col" else S((4, gr.shape[1] // 2, gr.shape[2]), BF)

    halves = tuple(half_shape(gr, k) for gr, k in zip(grads, kinds))

    def body(*refs):
        ins, mine, sib = refs[:n], refs[n:2 * n], refs[2 * n:3 * n]
        rsem, lsem, buf, blsem, bssem = refs[3 * n:]
        x, y, c = _place()
        sibling = (x, y, 1 - c)
        stager = _Stager(buf, blsem, bssem)
        pending = []
        for i in range(n):
            col = kinds[i] == "col"
            rh = (grads[i].shape[0] if col else grads[i].shape[1]) // 2
            rc = _chunk_rows(rh)
            for r0 in range(0, rh, rc):
                give = pl.multiple_of((1 - c) * rh + r0, 16)
                keep = pl.multiple_of(c * rh + r0, 16)
                if col:
                    stager.send(ins[i].at[pl.ds(give, rc), :], sib[i].at[pl.ds(r0, rc), :], rsem.at[i], sibling)
                    stager.copy(ins[i].at[pl.ds(keep, rc), :], mine[i].at[pl.ds(r0, rc), :])
                else:
                    for s in range(4):
                        stager.send(ins[i].at[s, pl.ds(give, rc), :], sib[i].at[s, pl.ds(r0, rc), :], rsem.at[i], sibling)
                        stager.copy(ins[i].at[s, pl.ds(keep, rc), :], mine[i].at[s, pl.ds(r0, rc), :])
        for i in range(n):
            _remote(sib[i], sib[i], lsem.at[i], rsem.at[i], sibling).wait_recv()
        stager.drain()
        for wait in pending:
            wait()

    max_cols = max(gr.shape[-1] for gr in grads)
    sems = [pltpu.SemaphoreType.DMA((n,)), pltpu.SemaphoreType.DMA((n,))]
    return pl.pallas_call(
        body, name=name, out_shape=halves + halves,
        in_specs=[ANY] * n, out_specs=tuple(ANY for _ in range(2 * n)), scratch_shapes=sems + _stage_scratch(max_cols, BF),
        compiler_params=_params(),
    )(*grads)


class _ChipsHook:
    def __init__(self, parts, kinds):
        self.ins, self.kinds, n = list(parts), list(kinds), len(parts)
        self.out_shape = tuple(
            S((4, p.shape[0], p.shape[1] // 4), BF) if k == "col" else S(p.shape, BF) for p, k in zip(parts, kinds))
        self.scratch = ([pltpu.SemaphoreType.DMA((n, 3)), pltpu.SemaphoreType.DMA((n, 3))]
                        + _stage_scratch(max(o.shape[2] for o in self.out_shape), BF))

    def _slab(self, ins, i, s, r0=0, nrows=None):
        _, rows, cols = self.out_shape[i].shape
        nrows = rows if nrows is None else nrows
        if self.kinds[i] == "col":
            return ins[i].at[pl.ds(r0, nrows), pl.ds(pl.multiple_of(s * cols, 128), cols)]
        return ins[i].at[s, pl.ds(r0, nrows), :]

    def _sends(self, ins, outs, scr):
        ssem, rsem = scr[0], scr[1]
        x, y, c = _place()
        for i in range(len(ins)):
            for j, (fx, fy) in enumerate(OTHER_CHIPS):
                px, py = _flip(x, fx), _flip(y, fy)
                yield _remote(self._slab(ins, i, 2 * px + py), outs[i].at[j], ssem.at[i, j], rsem.at[i, j], (px, py, c))

    def start(self, ins, outs, scr):
        x, y, c = _place()
        for cp in self._sends(ins, outs, scr):
            cp.start()
        stager = _Stager(*scr[2:])
        for i in range(len(ins)):
            rows = self.out_shape[i].shape[1]
            rc = _chunk_rows(rows)
            for r0 in range(0, rows, rc):
                stager.copy(self._slab(ins, i, 2 * x + y, r0, rc), outs[i].at[3, pl.ds(r0, rc), :])
        stager.drain()

    def finish(self, ins, outs, scr):
        for cp in self._sends(ins, outs, scr):
            cp.wait()


def _share_halves(halves):
    n = len(halves)
    out_shape = tuple(S((2 * h.shape[0], h.shape[1]), F32) for h in halves)

    def body(*refs):
        ins, outs = refs[:n], refs[n:2 * n]
        rsem, lsem, buf, blsem, bssem = refs[2 * n:]
        x, y, c = _place()
        sibling = (x, y, 1 - c)
        stager = _Stager(buf, blsem, bssem)
        pending = []
        for i in range(n):
            rh = halves[i].shape[0]
            rc = _chunk_rows(rh)
            for r0 in range(0, rh, rc):
                dst = outs[i].at[pl.ds(pl.multiple_of(c * rh + r0, 16), rc), :]
                stager.send(ins[i].at[pl.ds(r0, rc), :], dst, rsem.at[i], sibling)
                stager.copy(ins[i].at[pl.ds(r0, rc), :], dst)
        for i in range(n):
            rh = halves[i].shape[0]
            got = outs[i].at[pl.ds(pl.multiple_of((1 - c) * rh, 16), rh), :]
            _remote(got, got, lsem.at[i], rsem.at[i], sibling).wait_recv()
        stager.drain()
        for wait in pending:
            wait()

    max_cols = max(h.shape[1] for h in halves)
    sems = [pltpu.SemaphoreType.DMA((n,)), pltpu.SemaphoreType.DMA((n,))]
    return pl.pallas_call(
        body, name="share_halves", out_shape=out_shape,
        in_specs=[ANY] * n, out_specs=tuple(ANY for _ in range(n)), scratch_shapes=sems + _stage_scratch(max_cols, F32),
        compiler_params=_params(),
    )(*halves)


def _all_reduce_small(buf, name):
    shape = buf.shape

    def body(in_ref, out_ref, slots, ssem, rsem):
        x, y, c = _place()
        me = 4 * x + 2 * y + c
        slots[me] = in_ref[...]
        sends = []
        for r in range(1, 8):
            fx, fy, fc = (r >> 2) & 1, (r >> 1) & 1, r & 1
            cp = _remote(in_ref, slots.at[me], ssem.at[r - 1], rsem.at[r - 1], (_flip(x, fx), _flip(y, fy), _flip(c, fc)))
            cp.start()
            sends.append(cp)
        for r in range(1, 8):
            fx, fy, fc = (r >> 2) & 1, (r >> 1) & 1, r & 1
            px, py, pc = _flip(x, fx), _flip(y, fy), _flip(c, fc)
            _remote(in_ref, slots.at[4 * px + 2 * py + pc], ssem.at[r - 1], rsem.at[r - 1], (px, py, pc)).wait_recv()
        for cp in sends:
            cp.wait_send()
        acc = slots[0]
        for k in range(1, 8):
            acc = acc + slots[k]
        out_ref[...] = acc

    return pl.pallas_call(
        body, name=name, out_shape=S(shape, F32), in_specs=[VMEM_SPEC], out_specs=VMEM_SPEC,
        scratch_shapes=[pltpu.VMEM((8,) + shape, F32), pltpu.SemaphoreType.DMA((7,)), pltpu.SemaphoreType.DMA((7,))],
    )(buf)


BIG = ("ffn1_w_gu", "ffn1_w_down", "w_in", "w_out", "xattn_wq", "xattn_wkv", "xattn_wo", "ffn2_w_gu", "ffn2_w_down")
KIND = {"ffn1_w_gu": "col", "ffn1_w_down": "row", "w_in": "col", "w_out": "row", "xattn_wq": "row",
        "xattn_wkv": "col", "xattn_wo": "row", "ffn2_w_gu": "col", "ffn2_w_down": "row"}
WEIGHTS = ("rel_bias", "ffn1_norm", "ffn1_w_gu", "ffn1_w_down", "mix_norm", "w_in", "sinks", "conv_w", "w_out",
           "xattn_norm", "mem_norm", "xattn_wq", "xattn_wkv", "xattn_wo", "ffn2_norm", "ffn2_w_gu", "ffn2_w_down",
           "final_norm")
SMALL_ROWS = 16
GAIN_ROW = {"ffn1_norm": 0, "mix_norm": 1, "xattn_norm": 2, "mem_norm": 3, "ffn2_norm": 4, "final_norm": 5}
CONV_ROW, SINK_ROW, BIAS_ROW, LOSS_ROW = 6, 9, 10, 11


def _rows_block(rows, d):
    buf = jnp.zeros((SMALL_ROWS, d), F32)
    for r, v in rows.items():
        buf = lax.dynamic_update_slice(buf, v.reshape(1, -1).astype(F32), (r, 0))
    return buf


def _local_step(x, mem, pos, target, w, gains, rel_bias, sinks, conv_w, shards=None):
    t, d = x.shape
    dist = shards is not None
    w = dict(w)
    grads, slabs = {}, {}
    pos_col = pos.reshape(t, 1)
    pos_row = pos.reshape(1, t)
    bias_t = _bias_build(rel_bias).reshape(SWA_KV_HEADS, GROUP_ROWS, 2 * BLOCK)

    def gather(names):
        return _GatherHook([shards[k] for k in names], [KIND[k] for k in names]) if dist else None

    def gathered(names, hook):
        if dist:
            for k, gw in zip(names, hook.results):
                w[k] = gw if KIND[k] == "col" else gw.reshape(-1, gw.shape[-1])

    def pair_sums(names):
        if not dist:
            return None
        kinds = [KIND[k] for k in names]
        gnat = [grads[k] if KIND[k] == "col" else grads[k].reshape(4, -1, grads[k].shape[-1]) for k in names]
        halves = _reduce_siblings(gnat, kinds, "reduce_siblings_" + names[0])
        parts = {}
        for i, k in enumerate(names):
            mine, sib = halves[i], halves[len(names) + i]
            if KIND[k] == "row":
                mine, sib = mine.reshape(-1, mine.shape[-1]), sib.reshape(-1, sib.shape[-1])
            p = _add_bf16(mine, sib, "pair_sum_" + k)
            parts[k] = p if KIND[k] == "col" else p.reshape(4, -1, p.shape[-1])
        return parts

    def chips(names, parts):
        return _ChipsHook([parts[k] for k in names], [KIND[k] for k in names]) if dist else None

    def reduced(names, hook):
        if dist:
            slabs.update(zip(names, hook.results))

    names = ("w_in", "w_out")
    hook = gather(names)
    x1, g1, u1 = _ffn_fwd(x, gains["ffn1_norm"], w["ffn1_w_gu"], w["ffn1_w_down"], "ffn1_fwd", hook)
    gathered(names, hook)
    names = ("xattn_wq", "xattn_wkv", "xattn_wo")
    hook = gather(names)
    qkv, e = _mix_proj(x1, gains["mix_norm"], w["w_in"], hook)
    gathered(names, hook)
    names = ("ffn2_w_gu",)
    hook = gather(names)
    attn = _swa_fwd(qkv, pos_col, pos_row, bias_t, sinks, hook)
    gathered(names, hook)
    names = ("ffn2_w_down",)
    hook = gather(names)
    x2, merged = _mix_out_fwd(e, attn, conv_w, w["w_out"], x1, hook)
    gathered(names, hook)
    mh, kv = _mem_kv(mem, gains["mem_norm"], w["xattn_wkv"])
    x3, qx, o = _xattn_fwd(x2, gains["xattn_norm"], w["xattn_wq"], kv, w["xattn_wo"])
    x4, g2, u2 = _ffn_fwd(x3, gains["ffn2_norm"], w["ffn2_w_gu"], w["ffn2_w_down"], "ffn2_fwd")
    dx4, d_final, loss = _loss_bwd(x4, gains["final_norm"], target)

    dx3, d_ffn2, dgu2, a2, h4, dyh4 = _ffn_bwd(dx4, x3, gains["ffn2_norm"], g2, u2, w["ffn2_w_gu"], w["ffn2_w_down"],
                                               "ffn2_bwd")
    grads["ffn2_w_gu"] = _dw(h4, dgu2, 1408, "dw_ffn2_gu")
    grads["ffn2_w_down"] = _dw(a2, dyh4, 512, "dw_ffn2_down")
    parts = pair_sums(("ffn2_w_gu", "ffn2_w_down"))
    hook = chips(("ffn2_w_gu",), parts)
    dx2, d_xattn, dkv, dxh3, h3, dqx = _xattn_bwd(dx3, x2, gains["xattn_norm"], qx, kv, w["xattn_wq"], w["xattn_wo"], hook)
    reduced(("ffn2_w_gu",), hook)
    grads["xattn_wo"] = _dw(o, dxh3, 1024, "dw_wo")
    grads["xattn_wq"] = _dw(h3, dqx, 1024, "dw_wq")
    grads["xattn_wkv"], d_mem = _mem_bwd(dkv, mh, mem, gains["mem_norm"], w["xattn_wkv"])
    hook = chips(("ffn2_w_down",), parts)
    dattn, de, dcw, dxh2 = _mix_out_bwd(dx2, e, attn, conv_w, w["w_out"], hook)
    reduced(("ffn2_w_down",), hook)
    grads["w_out"] = _dw(merged, dxh2, 1024, "dw_wout")
    names = ("xattn_wo", "xattn_wq", "xattn_wkv", "w_out")
    hook = chips(names, pair_sums(names))
    dq, dkvs, gb, dsk = _swa_bwd(qkv, dattn, pos_col, pos_row, bias_t, sinks, hook)
    reduced(names, hook)
    d_rel_bias, d_sinks = _bias_reduce(gb.reshape(SWA_HEADS, BLOCK, 2 * BLOCK), dsk.reshape(SWA_HEADS, BLOCK, 1))
    dx1, d_mix, h2 = _mix_in_bwd(dq, dkvs, de, x1, gains["mix_norm"], w["w_in"], dx2)
    n_in = w["w_in"].shape[1]
    gw = _dw(h2, dq, 512, "dw_win_q", ncols=n_in)
    gw = _dw(h2, dkvs, 512, "dw_win_kv", into=gw, col_block=dq.shape[1] // 512, ncols=n_in)
    grads["w_in"] = _dw(h2, de, 512, "dw_win_e", into=gw, col_block=(dq.shape[1] + dkvs.shape[1]) // 512, ncols=n_in)
    names = ("w_in",)
    hook = chips(names, pair_sums(names))
    dx0, d_ffn1, dgu1, a1, h1, dyh1 = _ffn_bwd(dx1, x, gains["ffn1_norm"], g1, u1, w["ffn1_w_gu"], w["ffn1_w_down"],
                                               "ffn1_bwd", hook)
    reduced(names, hook)
    grads["ffn1_w_gu"] = _dw(h1, dgu1, 1408, "dw_ffn1_gu")
    grads["ffn1_w_down"] = _dw(a1, dyh1, 512, "dw_ffn1_down")
    names = ("ffn1_w_gu", "ffn1_w_down")
    hook = chips(names, pair_sums(names))
    if dist:
        hook.results = _exchange(hook, "reduce_chips_ffn1")
    reduced(names, hook)

    rows = {0: d_ffn1, 1: d_mix, 2: d_xattn, 3: d_mem, 4: d_ffn2, 5: d_final, SINK_ROW: d_sinks, BIAS_ROW: d_rel_bias,
            LOSS_ROW: loss[0, 0:1]}
    rows.update({CONV_ROW + j: dcw[j] for j in range(3)})
    return dx0, (slabs if dist else grads), _rows_block(rows, d)


def kernel(x, mem, positions, rel_bias, ffn1_norm, ffn1_w_gu, ffn1_w_down, mix_norm, w_in, sinks, conv_w, w_out, xattn_norm, mem_norm, xattn_wq, xattn_wkv, xattn_wo, ffn2_norm, ffn2_w_gu, ffn2_w_down, final_norm, loss_target, m_rel_bias, m_ffn1_norm, m_ffn1_w_gu, m_ffn1_w_down, m_mix_norm, m_w_in, m_sinks, m_conv_w, m_w_out, m_xattn_norm, m_mem_norm, m_xattn_wq, m_xattn_wkv, m_xattn_wo, m_ffn2_norm, m_ffn2_w_gu, m_ffn2_w_down, m_final_norm, v_rel_bias, v_ffn1_norm, v_ffn1_w_gu, v_ffn1_w_down, v_mix_norm, v_w_in, v_sinks, v_conv_w, v_w_out, v_xattn_norm, v_mem_norm, v_xattn_wq, v_xattn_wkv, v_xattn_wo, v_ffn2_norm, v_ffn2_w_gu, v_ffn2_w_down, v_final_norm):
    args = dict(locals())
    wts = {k: args[k] for k in WEIGHTS}
    mom = {k: args["m_" + k] for k in WEIGHTS}
    var = {k: args["v_" + k] for k in WEIGHTS}
    d = x.shape[-1]
    s_me = 2 * lax.axis_index("x") + lax.axis_index("y")

    shards = {k: wts[k][0].astype(BF) for k in BIG}
    first = ("ffn1_w_gu", "ffn1_w_down")
    gathered = _exchange(_GatherHook([shards[k] for k in first], [KIND[k] for k in first]), "gather_ffn1")
    whole = {k: (gw if KIND[k] == "col" else gw.reshape(-1, gw.shape[-1])) for k, gw in zip(first, gathered)}
    cw_cols = conv_w.shape[-1]
    placed = lax.dynamic_update_slice(jnp.zeros((SMALL_ROWS, d), F32), 0.5 * conv_w[0], (0, s_me * cw_cols))
    conv_whole = _all_reduce_small(placed, "gather_conv_w")[0:3]

    gains = {k: wts[k].reshape(1, d) for k in GAIN_ROW}
    dx0, slabs, small = _local_step(x[0], mem[0], positions[0], loss_target[0], whole, gains, rel_bias, sinks,
                                    conv_whole, shards)

    reduced = [_sum4(slabs[k], "chip_sum_" + k) for k in BIG]
    shard_grads = dict(zip(BIG, _share_halves(reduced)))

    small_sum = _all_reduce_small(small, "reduce_small")
    loss = small_sum[LOSS_ROW, 0]

    out_g, out_d, out_m, out_v = {}, {}, {}, {}
    for k in BIG:
        g2d = shard_grads[k]
        dl, mn, vn = _adamw(wts[k][0], g2d, mom[k][0], var[k][0], "adamw_" + k)
        out_g[k], out_d[k], out_m[k], out_v[k] = g2d[None], dl[None], mn[None], vn[None]

    conv_g = lax.dynamic_slice(small_sum, (CONV_ROW, s_me * cw_cols), (3, cw_cols))

    def pack(src, conv_block):
        rows = {r: src[k] for k, r in GAIN_ROW.items()}
        rows.update({CONV_ROW + j: conv_block[j] for j in range(3)})
        rows[SINK_ROW], rows[BIAS_ROW] = src["sinks"], src["rel_bias"]
        return _rows_block(rows, d)

    small_g = {k: small_sum[r] for k, r in GAIN_ROW.items()}
    small_g["sinks"] = small_sum[SINK_ROW, 0:sinks.size]
    small_g["rel_bias"] = small_sum[BIAS_ROW, 0:rel_bias.size]
    gp = pack(small_g, conv_g)
    dl, mn, vn = _adamw(pack(wts, conv_w[0]), gp, pack(mom, m_conv_w[0]), pack(var, v_conv_w[0]), "adamw_small")

    def unpack(buf, k):
        if k in GAIN_ROW:
            return buf[GAIN_ROW[k]].reshape(wts[k].shape)
        if k == "conv_w":
            return buf[CONV_ROW:CONV_ROW + 3, 0:cw_cols][None]
        if k == "sinks":
            return buf[SINK_ROW, 0:sinks.size].reshape(sinks.shape)
        return buf[BIAS_ROW, 0:rel_bias.size].reshape(rel_bias.shape)

    for k in WEIGHTS:
        if k not in KIND:
            out_g[k], out_d[k], out_m[k], out_v[k] = unpack(gp, k), unpack(dl, k), unpack(mn, k), unpack(vn, k)

    return (loss, dx0[None], *[out_g[k] for k in WEIGHTS], *[out_d[k] for k in WEIGHTS],
            *[out_m[k] for k in WEIGHTS], *[out_v[k] for k in WEIGHTS])
```

```python
import functools
import math

import jax
import jax.numpy as jnp
from jax import lax
from jax.experimental import pallas as pl
from jax.experimental.pallas import tpu as pltpu

BF = jnp.bfloat16
F32 = jnp.float32
I32 = jnp.int32
S = jax.ShapeDtypeStruct

EPS = 1e-6
NEG = -1e30
POS_PAD = 1 << 30
WINDOW = 128
BLOCK = 128
HEAD_DIM = 64
SWA_HEADS = 16
SWA_KV_HEADS = 4
SWA_GROUP = SWA_HEADS // SWA_KV_HEADS
MEM_HEADS = 4
REL_BUCKETS = 32
REL_MAX_DIST = 128
ADAM_LR = 0.001
ADAM_B1 = 0.9
ADAM_B2 = 0.999
ADAM_EPS = 1e-08
ADAM_WD = 0.01
ADAM_STEP = 10

V7X_VMEM_LIMIT_BYTES = 56 * 1024 * 1024
MESH = pl.DeviceIdType.MESH
ANY = pl.BlockSpec(memory_space=pl.ANY)
VMEM_SPEC = pl.BlockSpec(memory_space=pltpu.VMEM)
SMEM_SPEC = pl.BlockSpec(memory_space=pltpu.SMEM)


def _params(sem=None):
    return pltpu.CompilerParams(dimension_semantics=sem, vmem_limit_bytes=V7X_VMEM_LIMIT_BYTES)


def _resident(shape):
    nd = len(shape)
    return pl.BlockSpec(shape, lambda *_: (0,) * nd, pipeline_mode=pl.Buffered(1))


def _acc_spec(shape):
    nd = len(shape)
    return pl.BlockSpec(shape, lambda *_: (0,) * nd)


def _call(body, *, name, grid, out_shape, in_specs, out_specs, args, sem, scratch_shapes=(), hook=None):
    if hook is None:
        return pl.pallas_call(body, name=name, grid=grid, out_shape=out_shape, in_specs=in_specs, out_specs=out_specs,
                              scratch_shapes=list(scratch_shapes), compiler_params=_params(sem))(*args)
    n_in, n_out, n_scr = len(in_specs), len(out_shape), len(scratch_shapes)
    h_in, h_out = len(hook.ins), len(hook.out_shape)
    last = grid[0] - 1

    def hosted(*refs):
        k_in, x_in = refs[:n_in], refs[n_in:n_in + h_in]
        o0 = n_in + h_in
        k_out, x_out = refs[o0:o0 + n_out], refs[o0 + n_out:o0 + n_out + h_out]
        s0 = o0 + n_out + h_out
        k_scr, x_scr = refs[s0:s0 + n_scr], refs[s0 + n_scr:]

        @pl.when(pl.program_id(0) == 0)
        def _():
            hook.start(x_in, x_out, x_scr)
        body(*k_in, *k_out, *k_scr)

        @pl.when(pl.program_id(0) == last)
        def _():
            hook.finish(x_in, x_out, x_scr)

    res = pl.pallas_call(
        hosted, name=name, grid=grid, out_shape=tuple(out_shape) + tuple(hook.out_shape),
        in_specs=list(in_specs) + [ANY] * h_in, out_specs=tuple(out_specs) + (ANY,) * h_out,
        scratch_shapes=list(scratch_shapes) + list(hook.scratch), compiler_params=_params(("arbitrary",) * len(grid)),
    )(*args, *hook.ins)
    hook.results = res[n_out:]
    return res[:n_out]


def _nn(a, b):
    return jnp.dot(a, b, preferred_element_type=F32)


def _nt(a, b):
    return lax.dot_general(a, b, (((1,), (1,)), ((), ())), preferred_element_type=F32)


def _tn(a, b):
    return lax.dot_general(a, b, (((0,), (0,)), ((), ())), preferred_element_type=F32)


def _sigmoid(v):
    return 1.0 / (1.0 + jnp.exp(-v))


def _rms(x):
    r = lax.rsqrt(jnp.mean(x * x, axis=-1, keepdims=True) + EPS)
    return x * r, r


def _rms_bwd(dh, n, r, g):
    dn = dh * g
    dx = r * (dn - n * jnp.mean(dn * n, axis=-1, keepdims=True))
    return dx, jnp.sum(dh * n, axis=0, keepdims=True)


def _ffn_fwd(x, gn, wgu, wd, name, hook=None):
    t, d = x.shape
    f = wd.shape[0]
    tm, fc = 256, 1408

    def body(x_ref, gn_ref, wgu_ref, wd_ref, xo_ref, g_ref, u_ref):
        xv = x_ref[...]
        n, _ = _rms(xv)
        h = (n * gn_ref[...]).astype(BF)
        acc = jnp.zeros((tm, d), F32)
        for c0 in range(0, f, fc):
            g = _nn(h, wgu_ref[:, c0:c0 + fc])
            u = _nn(h, wgu_ref[:, f + c0:f + c0 + fc])
            g_ref[:, c0:c0 + fc] = g.astype(BF)
            u_ref[:, c0:c0 + fc] = u.astype(BF)
            a = (g * _sigmoid(g)) * u
            acc = acc + _nn(a.astype(BF), wd_ref[c0:c0 + fc, :])
        xo_ref[...] = xv + 0.5 * acc

    return _call(
        body, name=name, grid=(t // tm,),
        out_shape=(S((t, d), F32), S((t, f), BF), S((t, f), BF)),
        in_specs=[pl.BlockSpec((tm, d), lambda i: (i, 0)), _resident((1, d)), _resident(wgu.shape), _resident(wd.shape)],
        out_specs=(pl.BlockSpec((tm, d), lambda i: (i, 0)), pl.BlockSpec((tm, f), lambda i: (i, 0)),
                   pl.BlockSpec((tm, f), lambda i: (i, 0))),
        sem=("parallel",), args=(x, gn, wgu, wd), hook=hook)


def _mix_proj(x, gn, w_in, hook=None):
    t, d = x.shape
    tm = 256
    nqkv = 1536
    ne = w_in.shape[1] - nqkv

    def body(x_ref, gn_ref, w_ref, qkv_ref, e_ref):
        n, _ = _rms(x_ref[...])
        h = (n * gn_ref[...]).astype(BF)
        qkv_ref[...] = _nn(h, w_ref[:, 0:nqkv]).astype(BF)
        for c0 in range(0, ne, 1024):
            e_ref[:, c0:c0 + 1024] = _nn(h, w_ref[:, nqkv + c0:nqkv + c0 + 1024])

    return _call(
        body, name="mix_proj", grid=(t // tm,),
        out_shape=(S((t, nqkv), BF), S((t, ne), F32)),
        in_specs=[pl.BlockSpec((tm, d), lambda i: (i, 0)), _resident((1, d)), _resident(w_in.shape)],
        out_specs=(pl.BlockSpec((tm, nqkv), lambda i: (i, 0)), pl.BlockSpec((tm, ne), lambda i: (i, 0))),
        sem=("parallel",), args=(x, gn, w_in), hook=hook)


def _t5_bucket(rel):
    n = jnp.maximum(rel, 0)
    max_exact = REL_BUCKETS // 2
    nf = jnp.maximum(n, 1).astype(F32)
    large = max_exact + (jnp.log(nf / max_exact) / math.log(REL_MAX_DIST / max_exact)
                         * (REL_BUCKETS - max_exact)).astype(I32)
    large = jnp.minimum(large, REL_BUCKETS - 1)
    return jnp.where(n < max_exact, n, large)


def _block_rel():
    i = lax.broadcasted_iota(I32, (BLOCK, 2 * BLOCK), 0)
    j = lax.broadcasted_iota(I32, (BLOCK, 2 * BLOCK), 1)
    return i + BLOCK - j


def _bias_build(rel_bias):
    def body(rb_ref, o_ref):
        bucket = _t5_bucket(_block_rel())
        for h in range(SWA_HEADS):
            acc = jnp.zeros((BLOCK, 2 * BLOCK), F32)
            for b in range(REL_BUCKETS):
                acc = jnp.where(bucket == b, rb_ref[b, h], acc)
            o_ref[h] = acc

    return pl.pallas_call(
        body, name="bias_build", out_shape=S((SWA_HEADS, BLOCK, 2 * BLOCK), F32),
        in_specs=[SMEM_SPEC], out_specs=VMEM_SPEC,
    )(rel_bias)


GROUP_ROWS = SWA_GROUP * BLOCK


def _swa_visible(b, pq_ref, pkp_ref, pkc_ref):
    pk = jnp.concatenate([pkp_ref[...], pkc_ref[...]], axis=1)
    col = lax.broadcasted_iota(I32, (1, 2 * BLOCK), 1)
    pk = jnp.where(jnp.logical_and(b == 0, col < BLOCK), POS_PAD, pk)
    rel = jnp.concatenate([pq_ref[...]] * SWA_GROUP, axis=0) - pk
    return jnp.logical_and(rel >= 0, rel < WINDOW)


def _group_heads(ref, hk):
    h0 = hk * SWA_GROUP
    return jnp.concatenate([ref[:, (h0 + g) * HEAD_DIM:(h0 + g + 1) * HEAD_DIM] for g in range(SWA_GROUP)], axis=0)


def _group_sinks(sink_ref, hk):
    row = lax.broadcasted_iota(I32, (GROUP_ROWS, 1), 0)
    col = jnp.zeros((GROUP_ROWS, 1), F32) + sink_ref[0, hk * SWA_GROUP]
    for g in range(1, SWA_GROUP):
        col = jnp.where(row >= g * BLOCK, sink_ref[0, hk * SWA_GROUP + g], col)
    return col


def _swa_probs(qg, kh, vis, bias, sink):
    s = _nt(qg, kh) * (HEAD_DIM ** -0.5)
    s = jnp.where(vis, s + bias, NEG)
    m = jnp.maximum(jnp.max(s, axis=-1, keepdims=True), sink)
    p = jnp.exp(s - m)
    ps = jnp.exp(sink - m)
    inv = 1.0 / (jnp.sum(p, axis=-1, keepdims=True) + ps)
    return p * inv, ps * inv


def _swa_fwd(qkv, pos_col, pos_row, bias_t, sinks, hook=None):
    t = qkv.shape[0]
    nb = t // BLOCK
    qw = SWA_HEADS * HEAD_DIM
    kw = SWA_KV_HEADS * HEAD_DIM

    def body(q_ref, kp_ref, kc_ref, vp_ref, vc_ref, pq_ref, pkp_ref, pkc_ref, bias_ref, sink_ref, o_ref):
        b = pl.program_id(0)
        vis = _swa_visible(b, pq_ref, pkp_ref, pkc_ref)
        k2 = jnp.concatenate([kp_ref[...], kc_ref[...]], axis=0)
        v2 = jnp.concatenate([vp_ref[...], vc_ref[...]], axis=0)
        for hk in range(SWA_KV_HEADS):
            kh = k2[:, hk * HEAD_DIM:(hk + 1) * HEAD_DIM]
            vh = v2[:, hk * HEAD_DIM:(hk + 1) * HEAD_DIM]
            pn, _ = _swa_probs(_group_heads(q_ref, hk), kh, vis, bias_ref[hk], _group_sinks(sink_ref, hk))
            o = _nn(pn.astype(BF), vh)
            for g in range(SWA_GROUP):
                h = hk * SWA_GROUP + g
                o_ref[:, h * HEAD_DIM:(h + 1) * HEAD_DIM] = o[g * BLOCK:(g + 1) * BLOCK]

    prev = lambda b: jnp.maximum(b - 1, 0)
    return _call(
        body, name="swa_fwd", grid=(nb,), out_shape=(S((t, qw), F32),),
        in_specs=[
            pl.BlockSpec((BLOCK, qw), lambda b: (b, 0)),
            pl.BlockSpec((BLOCK, kw), lambda b: (prev(b), qw // kw)),
            pl.BlockSpec((BLOCK, kw), lambda b: (b, qw // kw)),
            pl.BlockSpec((BLOCK, kw), lambda b: (prev(b), qw // kw + 1)),
            pl.BlockSpec((BLOCK, kw), lambda b: (b, qw // kw + 1)),
            pl.BlockSpec((BLOCK, 1), lambda b: (b, 0)),
            pl.BlockSpec((1, BLOCK), lambda b: (0, prev(b))),
            pl.BlockSpec((1, BLOCK), lambda b: (0, b)),
            _resident(bias_t.shape),
            SMEM_SPEC,
        ],
        out_specs=(pl.BlockSpec((BLOCK, qw), lambda b: (b, 0)),),
        sem=("parallel",), args=(qkv, qkv, qkv, qkv, qkv, pos_col, pos_row, pos_row, bias_t, sinks), hook=hook)[0]


def _conv_taps(z, zh, first):
    tm = z.shape[0]
    zh = jnp.where(first, 0.0, zh)
    row = lax.broadcasted_iota(I32, (tm, 1), 0)
    z1 = jnp.where(row == 0, zh[7:8, :], pltpu.roll(z, 1, 0))
    z2 = jnp.where(row == 0, zh[6:7, :], jnp.where(row == 1, zh[7:8, :], pltpu.roll(z, 2, 0)))
    return z1, z2


def _mix_out_fwd(e, attn, conv_w, w_out, x, hook=None):
    t, d = x.shape
    tm = 256
    hb = tm // 8

    def body(c_ref, b_ref, u_ref, ga_ref, gc_ref, ch_ref, uh_ref, attn_ref, cw_ref, w_ref, x_ref, xo_ref, mg_ref):
        i = pl.program_id(0)
        z = c_ref[...] * u_ref[...]
        z1, z2 = _conv_taps(z, ch_ref[...] * uh_ref[...], i == 0)
        s = cw_ref[0:1, :] * z2 + cw_ref[1:2, :] * z1 + cw_ref[2:3, :] * z
        conv = b_ref[...] * s
        merged = (_sigmoid(ga_ref[...]) * attn_ref[...] + _sigmoid(gc_ref[...]) * conv).astype(BF)
        mg_ref[...] = merged
        xo_ref[...] = x_ref[...] + _nn(merged, w_ref[...])

    ecol = lambda cb: pl.BlockSpec((tm, d), lambda i: (i, cb))
    halo = lambda cb: pl.BlockSpec((8, d), lambda i: (jnp.maximum(i * hb - 1, 0), cb))
    row = pl.BlockSpec((tm, d), lambda i: (i, 0))
    return _call(
        body, name="mix_out_fwd", grid=(t // tm,),
        out_shape=(S((t, d), F32), S((t, d), BF)),
        in_specs=[ecol(0), ecol(1), ecol(2), ecol(3), ecol(4), halo(0), halo(2), row,
                  _resident(conv_w.shape), _resident(w_out.shape), row],
        out_specs=(row, row),
        sem=("parallel",), args=(e, e, e, e, e, e, e, attn, conv_w, w_out, x), hook=hook)


def _mem_kv(mem, gm, wkv):
    m, d = mem.shape

    def body(mem_ref, gm_ref, w_ref, mh_ref, kv_ref):
        n, _ = _rms(mem_ref[...])
        mh = (n * gm_ref[...]).astype(BF)
        mh_ref[...] = mh
        kv_ref[...] = _nn(mh, w_ref[...]).astype(BF)

    return pl.pallas_call(
        body, name="mem_kv", out_shape=(S((m, d), BF), S((m, wkv.shape[1]), BF)),
        compiler_params=_params(),
    )(mem, gm, wkv)


def _xattn_probs(qh, kh):
    s = _nt(qh, kh) * (kh.shape[1] ** -0.5)
    p = jnp.exp(s - jnp.max(s, axis=-1, keepdims=True))
    return p * (1.0 / jnp.sum(p, axis=-1, keepdims=True))


def _xattn_fwd(x, gn, wq, kv, wo):
    t, d = x.shape
    tm = 256
    hd = d // MEM_HEADS

    def body(x_ref, gn_ref, wq_ref, kv_ref, wo_ref, xo_ref, q_ref, o_ref):
        xv = x_ref[...]
        n, _ = _rms(xv)
        q = _nn((n * gn_ref[...]).astype(BF), wq_ref[...]).astype(BF)
        q_ref[...] = q
        outs = []
        for hh in range(MEM_HEADS):
            p = _xattn_probs(q[:, hh * hd:(hh + 1) * hd], kv_ref[:, hh * hd:(hh + 1) * hd])
            outs.append(_nn(p.astype(BF), kv_ref[:, d + hh * hd:d + (hh + 1) * hd]))
        o = jnp.concatenate(outs, axis=1).astype(BF)
        o_ref[...] = o
        xo_ref[...] = xv + _nn(o, wo_ref[...])

    row = pl.BlockSpec((tm, d), lambda i: (i, 0))
    return pl.pallas_call(
        body, name="xattn_fwd", grid=(t // tm,),
        out_shape=(S((t, d), F32), S((t, d), BF), S((t, d), BF)),
        in_specs=[row, _resident((1, d)), _resident(wq.shape), _resident(kv.shape), _resident(wo.shape)],
        out_specs=(row, row, row),
        compiler_params=_params(("parallel",)),
    )(x, gn, wq, kv, wo)


def _loss_bwd(x, gf, target):
    t, d = x.shape
    tm = 512

    def body(x_ref, gf_ref, t_ref, dx_ref, dg_ref, loss_ref):
        @pl.when(pl.program_id(0) == 0)
        def _():
            dg_ref[...] = jnp.zeros_like(dg_ref)
            loss_ref[...] = jnp.zeros_like(loss_ref)
        n, r = _rms(x_ref[...])
        g = gf_ref[...]
        err = n * g - t_ref[...]
        loss_ref[...] += 0.5 * jnp.sum(jnp.sum(err * err, axis=-1, keepdims=True) / d, axis=0, keepdims=True)
        dx, dg = _rms_bwd(err / d, n, r, g)
        dx_ref[...] = dx
        dg_ref[...] += dg

    row = pl.BlockSpec((tm, d), lambda i: (i, 0))
    return pl.pallas_call(
        body, name="loss_bwd", grid=(t // tm,),
        out_shape=(S((t, d), F32), S((1, d), F32), S((1, 128), F32)),
        in_specs=[row, _resident((1, d)), row],
        out_specs=(row, _acc_spec((1, d)), _acc_spec((1, 128))),
        compiler_params=_params(("arbitrary",)),
    )(x, gf, target)


def _ffn_bwd(dxo, x, gn, g, u, wgu, wd, name, hook=None):
    t, d = x.shape
    f = wd.shape[0]
    tm, fc = 256, 1408

    def body(dxo_ref, x_ref, gn_ref, g_ref, u_ref, wgu_ref, wd_ref, dx_ref, dgn_ref, dgu_ref, a_ref, h_ref, dyh_ref):
        @pl.when(pl.program_id(0) == 0)
        def _():
            dgn_ref[...] = jnp.zeros_like(dgn_ref)
        dxov = dxo_ref[...]
        dyh = (0.5 * dxov).astype(BF)
        dyh_ref[...] = dyh
        n, r = _rms(x_ref[...])
        gnv = gn_ref[...]
        h_ref[...] = (n * gnv).astype(BF)
        dh = jnp.zeros((tm, d), F32)
        for c0 in range(0, f, fc):
            gv = g_ref[:, c0:c0 + fc].astype(F32)
            uv = u_ref[:, c0:c0 + fc].astype(F32)
            da = _nt(dyh, wd_ref[c0:c0 + fc, :])
            sg = _sigmoid(gv)
            silu = gv * sg
            a_ref[:, c0:c0 + fc] = (silu * uv).astype(BF)
            dg = (da * uv * (sg * (1.0 + gv * (1.0 - sg)))).astype(BF)
            du = (da * silu).astype(BF)
            dgu_ref[:, c0:c0 + fc] = dg
            dgu_ref[:, f + c0:f + c0 + fc] = du
            dh = dh + _nt(dg, wgu_ref[:, c0:c0 + fc]) + _nt(du, wgu_ref[:, f + c0:f + c0 + fc])
        dx, dgn = _rms_bwd(dh, n, r, gnv)
        dx_ref[...] = dxov + dx
        dgn_ref[...] += dgn

    row = pl.BlockSpec((tm, d), lambda i: (i, 0))
    frow = pl.BlockSpec((tm, f), lambda i: (i, 0))
    return _call(
        body, name=name, grid=(t // tm,),
        out_shape=(S((t, d), F32), S((1, d), F32), S((t, 2 * f), BF), S((t, f), BF), S((t, d), BF), S((t, d), BF)),
        in_specs=[row, row, _resident((1, d)), frow, frow, _resident(wgu.shape), _resident(wd.shape)],
        out_specs=(row, _acc_spec((1, d)), pl.BlockSpec((tm, 2 * f), lambda i: (i, 0)), frow, row, row),
        sem=("arbitrary",), args=(dxo, x, gn, g, u, wgu, wd), hook=hook)


def _dw(a, b, tn, name, into=None, col_block=0, ncols=None):
    t, ka = a.shape
    nb = b.shape[1]
    tt = 1024
    nt = t // tt
    ncols = nb if ncols is None else ncols

    def body(*refs):
        a_ref, b_ref = refs[0], refs[1]
        o_ref, acc_ref = refs[-2], refs[-1]
        k = pl.program_id(1)

        @pl.when(k == 0)
        def _():
            acc_ref[...] = jnp.zeros_like(acc_ref)
        acc_ref[...] += _tn(a_ref[...], b_ref[...])

        @pl.when(k == nt - 1)
        def _():
            o_ref[...] = acc_ref[...].astype(BF)

    in_specs = [pl.BlockSpec((tt, ka), lambda j, k: (k, 0)), pl.BlockSpec((tt, tn), lambda j, k: (k, j))]
    args = [a, b]
    aliases = {}
    if into is not None:
        in_specs.append(ANY)
        args.append(into)
        aliases = {2: 0}
    return pl.pallas_call(
        body, name=name, grid=(nb // tn, nt), out_shape=S((ka, ncols), BF),
        in_specs=in_specs, out_specs=pl.BlockSpec((ka, tn), lambda j, k: (0, j + col_block)),
        scratch_shapes=[pltpu.VMEM((ka, tn), F32)], input_output_aliases=aliases,
        compiler_params=_params(("parallel", "arbitrary")),
    )(*args)


def _xattn_bwd(dxo, x, gn, q, kv, wq, wo, hook=None):
    t, d = x.shape
    tm = 256
    hd = d // MEM_HEADS
    nkv = kv.shape[0]

    def body(dxo_ref, x_ref, gn_ref, q_ref, kv_ref, wq_ref, wo_ref, dx_ref, dgn_ref, dkv_ref, dxh_ref, h_ref, dq_ref):
        @pl.when(pl.program_id(0) == 0)
        def _():
            dgn_ref[...] = jnp.zeros_like(dgn_ref)
            dkv_ref[...] = jnp.zeros_like(dkv_ref)
        dxov = dxo_ref[...]
        dxh = dxov.astype(BF)
        dxh_ref[...] = dxh
        do = _nt(dxh, wo_ref[...]).astype(BF)
        dqs = []
        for hh in range(MEM_HEADS):
            lo, hi = hh * hd, (hh + 1) * hd
            qh = q_ref[:, lo:hi]
            kh = kv_ref[:, lo:hi]
            vh = kv_ref[:, d + lo:d + hi]
            doh = do[:, lo:hi]
            p = _xattn_probs(qh, kh)
            dp = _nt(doh, vh)
            ds = (p * (dp - jnp.sum(p * dp, axis=-1, keepdims=True)) * (hd ** -0.5)).astype(BF)
            dqs.append(_nn(ds, kh))
            dkv_ref[:, lo:hi] += _tn(ds, qh)
            dkv_ref[:, d + lo:d + hi] += _tn(p.astype(BF), doh)
        dq = jnp.concatenate(dqs, axis=1).astype(BF)
        dq_ref[...] = dq
        n, r = _rms(x_ref[...])
        gnv = gn_ref[...]
        h_ref[...] = (n * gnv).astype(BF)
        dx, dgn = _rms_bwd(_nt(dq, wq_ref[...]), n, r, gnv)
        dx_ref[...] = dxov + dx
        dgn_ref[...] += dgn

    row = pl.BlockSpec((tm, d), lambda i: (i, 0))
    return _call(
        body, name="xattn_bwd", grid=(t // tm,),
        out_shape=(S((t, d), F32), S((1, d), F32), S((nkv, 2 * d), F32), S((t, d), BF), S((t, d), BF), S((t, d), BF)),
        in_specs=[row, row, _resident((1, d)), row, _resident(kv.shape), _resident(wq.shape), _resident(wo.shape)],
        out_specs=(row, _acc_spec((1, d)), _acc_spec((nkv, 2 * d)), row, row, row),
        sem=("arbitrary",), args=(dxo, x, gn, q, kv, wq, wo), hook=hook)


def _mem_bwd(dkv, mh, mem, gm, wkv):
    m, d = mem.shape

    def body(dkv_ref, mh_ref, mem_ref, gm_ref, w_ref, dw_ref, dgm_ref):
        dkvb = dkv_ref[...].astype(BF)
        dw_ref[...] = _tn(mh_ref[...], dkvb).astype(BF)
        dmh = _nt(dkvb, w_ref[...])
        n, _ = _rms(mem_ref[...])
        dgm_ref[...] = jnp.sum(dmh * n, axis=0, keepdims=True)

    return pl.pallas_call(
        body, name="mem_bwd", out_shape=(S(wkv.shape, BF), S((1, d), F32)),
        compiler_params=_params(),
    )(dkv, mh, mem, gm, wkv)


def _mix_out_bwd(dxo, e, attn, conv_w, w_out, hook=None):
    t, d = attn.shape
    tm = 256
    hb = tm // 8
    nt = t // tm
    last8 = t // 8 - 1

    def body(dxo_ref, dxn_ref, c_ref, b_ref, u_ref, ga_ref, gc_ref, ch_ref, uh_ref, bn_ref, gcn_ref,
             attn_ref, cw_ref, w_ref, dattn_ref, de_ref, dcw_ref, dxh_ref):
        i = pl.program_id(0)

        @pl.when(i == 0)
        def _():
            dcw_ref[...] = jnp.zeros_like(dcw_ref)
        dxh = dxo_ref[...].astype(BF)
        dxh_ref[...] = dxh
        w = w_ref[...]
        dm = _nt(dxh, w)
        dmn = _nt(dxn_ref[...].astype(BF), w)
        cv, bv, uv = c_ref[...], b_ref[...], u_ref[...]
        sga = _sigmoid(ga_ref[...])
        sgc = _sigmoid(gc_ref[...])
        z = cv * uv
        z1, z2 = _conv_taps(z, ch_ref[...] * uh_ref[...], i == 0)
        w0, w1, w2 = cw_ref[0:1, :], cw_ref[1:2, :], cw_ref[2:3, :]
        s = w0 * z2 + w1 * z1 + w2 * z
        av = attn_ref[...]
        dattn_ref[...] = (dm * sga).astype(BF)
        dconv = dm * sgc
        ds = dconv * bv
        dsn = jnp.where(i == nt - 1, 0.0, dmn * _sigmoid(gcn_ref[...]) * bn_ref[...])
        row = lax.broadcasted_iota(I32, (tm, 1), 0)
        dsp1 = jnp.where(row == tm - 1, dsn[0:1, :], pltpu.roll(ds, tm - 1, 0))
        dsp2 = jnp.where(row == tm - 2, dsn[0:1, :], jnp.where(row == tm - 1, dsn[1:2, :], pltpu.roll(ds, tm - 2, 0)))
        dz = w2 * ds + w1 * dsp1 + w0 * dsp2
        de_ref[:, 0:d] = (dz * uv).astype(BF)
        de_ref[:, d:2 * d] = (dconv * s).astype(BF)
        de_ref[:, 2 * d:3 * d] = (dz * cv).astype(BF)
        de_ref[:, 3 * d:4 * d] = (dm * av * sga * (1.0 - sga)).astype(BF)
        de_ref[:, 4 * d:5 * d] = (dm * (bv * s) * sgc * (1.0 - sgc)).astype(BF)
        dcw_ref[0:1, :] += jnp.sum(ds * z2, axis=0, keepdims=True)
        dcw_ref[1:2, :] += jnp.sum(ds * z1, axis=0, keepdims=True)
        dcw_ref[2:3, :] += jnp.sum(ds * z, axis=0, keepdims=True)

    ecol = lambda cb: pl.BlockSpec((tm, d), lambda i: (i, cb))
    prev = lambda cb: pl.BlockSpec((8, d), lambda i: (jnp.maximum(i * hb - 1, 0), cb))
    nxt = lambda cb: pl.BlockSpec((8, d), lambda i: (jnp.minimum((i + 1) * hb, last8), cb))
    row = pl.BlockSpec((tm, d), lambda i: (i, 0))
    return _call(
        body, name="mix_out_bwd", grid=(nt,),
        out_shape=(S((t, d), BF), S((t, 5 * d), BF), S((8, d), F32), S((t, d), BF)),
        in_specs=[row, nxt(0), ecol(0), ecol(1), ecol(2), ecol(3), ecol(4), prev(0), prev(2), nxt(1), nxt(4),
                  row, _resident(conv_w.shape), _resident(w_out.shape)],
        out_specs=(row, pl.BlockSpec((tm, 5 * d), lambda i: (i, 0)), _acc_spec((8, d)), row),
        sem=("arbitrary",), args=(dxo, dxo, e, e, e, e, e, e, e, e, e, attn, conv_w, w_out), hook=hook)


def _swa_bwd(qkv, dattn, pos_col, pos_row, bias_t, sinks, hook=None):
    t = qkv.shape[0]
    nb = t // BLOCK
    qw = SWA_HEADS * HEAD_DIM
    kw = SWA_KV_HEADS * HEAD_DIM

    def body(q_ref, kp_ref, kc_ref, vp_ref, vc_ref, do_ref, pq_ref, pkp_ref, pkc_ref, bias_ref, sink_ref,
             dq_ref, dkv_ref, gb_ref, dsk_ref, carry_ref):
        b = pl.program_id(0)

        @pl.when(b == 0)
        def _():
            gb_ref[...] = jnp.zeros_like(gb_ref)
            dsk_ref[...] = jnp.zeros_like(dsk_ref)
            carry_ref[...] = jnp.zeros_like(carry_ref)

        @pl.when(b < nb)
        def _():
            vis = _swa_visible(b, pq_ref, pkp_ref, pkc_ref)
            k2 = jnp.concatenate([kp_ref[...], kc_ref[...]], axis=0)
            v2 = jnp.concatenate([vp_ref[...], vc_ref[...]], axis=0)
            for hk in range(SWA_KV_HEADS):
                lo, hi = hk * HEAD_DIM, (hk + 1) * HEAD_DIM
                kh = k2[:, lo:hi]
                vh = v2[:, lo:hi]
                qg = _group_heads(q_ref, hk)
                dog = _group_heads(do_ref, hk)
                pn, psn = _swa_probs(qg, kh, vis, bias_ref[hk], _group_sinks(sink_ref, hk))
                dp = _nt(dog, vh)
                delta = jnp.sum(pn * dp, axis=-1, keepdims=True)
                ds = pn * (dp - delta)
                gb_ref[hk] += ds
                dsk_ref[hk] += -psn * delta
                dsb = (ds * (HEAD_DIM ** -0.5)).astype(BF)
                dqg = _nn(dsb, kh).astype(BF)
                for g in range(SWA_GROUP):
                    h = hk * SWA_GROUP + g
                    dq_ref[:, h * HEAD_DIM:(h + 1) * HEAD_DIM] = dqg[g * BLOCK:(g + 1) * BLOCK]
                dk = _tn(dsb, qg)
                dv = _tn(pn.astype(BF), dog)
                dkv_ref[:, lo:hi] = (carry_ref[:, lo:hi] + dk[0:BLOCK]).astype(BF)
                dkv_ref[:, kw + lo:kw + hi] = (carry_ref[:, kw + lo:kw + hi] + dv[0:BLOCK]).astype(BF)
                carry_ref[:, lo:hi] = dk[BLOCK:2 * BLOCK]
                carry_ref[:, kw + lo:kw + hi] = dv[BLOCK:2 * BLOCK]

        @pl.when(b == nb)
        def _():
            dkv_ref[...] = carry_ref[...].astype(BF)

    cur = lambda b: jnp.minimum(b, nb - 1)
    prev = lambda b: jnp.maximum(cur(b) - 1, 0)
    return _call(
        body, name="swa_bwd", grid=(nb + 1,),
        out_shape=(S((t, qw), BF), S((t, 2 * kw), BF), S((SWA_KV_HEADS, GROUP_ROWS, 2 * BLOCK), F32),
                   S((SWA_KV_HEADS, GROUP_ROWS, 1), F32)),
        in_specs=[
            pl.BlockSpec((BLOCK, qw), lambda b: (cur(b), 0)),
            pl.BlockSpec((BLOCK, kw), lambda b: (prev(b), qw // kw)),
            pl.BlockSpec((BLOCK, kw), lambda b: (cur(b), qw // kw)),
            pl.BlockSpec((BLOCK, kw), lambda b: (prev(b), qw // kw + 1)),
            pl.BlockSpec((BLOCK, kw), lambda b: (cur(b), qw // kw + 1)),
            pl.BlockSpec((BLOCK, qw), lambda b: (cur(b), 0)),
            pl.BlockSpec((BLOCK, 1), lambda b: (cur(b), 0)),
            pl.BlockSpec((1, BLOCK), lambda b: (0, prev(b))),
            pl.BlockSpec((1, BLOCK), lambda b: (0, cur(b))),
            _resident(bias_t.shape),
            SMEM_SPEC,
        ],
        out_specs=(
            pl.BlockSpec((BLOCK, qw), lambda b: (cur(b), 0)),
            pl.BlockSpec((BLOCK, 2 * kw), lambda b: (jnp.maximum(b - 1, 0), 0)),
            _acc_spec((SWA_KV_HEADS, GROUP_ROWS, 2 * BLOCK)),
            _acc_spec((SWA_KV_HEADS, GROUP_ROWS, 1)),
        ),
        scratch_shapes=[pltpu.VMEM((BLOCK, 2 * kw), F32)],
        sem=("arbitrary",), args=(qkv, qkv, qkv, qkv, qkv, dattn, pos_col, pos_row, pos_row, bias_t, sinks), hook=hook)


def _bias_reduce(gb, dsk):
    def body(gb_ref, dsk_ref, drb_ref, dsink_ref):
        bucket = _t5_bucket(_block_rel())
        for b in range(REL_BUCKETS):
            mask = bucket == b
            for h in range(SWA_HEADS):
                drb_ref[b, h] = jnp.sum(jnp.where(mask, gb_ref[h], 0.0))
        for h in range(SWA_HEADS):
            dsink_ref[0, h] = jnp.sum(dsk_ref[h])

    return pl.pallas_call(
        body, name="bias_reduce", out_shape=(S((REL_BUCKETS, SWA_HEADS), F32), S((1, SWA_HEADS), F32)),
        in_specs=[VMEM_SPEC, VMEM_SPEC], out_specs=(SMEM_SPEC, SMEM_SPEC),
    )(gb, dsk)


def _mix_in_bwd(dq, dkv, de, x, gn, w_in, dxo):
    t, d = x.shape
    tm = 256
    nq, nkv, ne = dq.shape[1], dkv.shape[1], de.shape[1]

    def body(dq_ref, dkv_ref, de_ref, x_ref, gn_ref, w_ref, dxo_ref, dx_ref, dgn_ref, h_ref):
        @pl.when(pl.program_id(0) == 0)
        def _():
            dgn_ref[...] = jnp.zeros_like(dgn_ref)
        dh = _nt(dq_ref[...], w_ref[:, 0:nq]) + _nt(dkv_ref[...], w_ref[:, nq:nq + nkv])
        for c0 in range(0, ne, 1024):
            dh = dh + _nt(de_ref[:, c0:c0 + 1024], w_ref[:, nq + nkv + c0:nq + nkv + c0 + 1024])
        n, r = _rms(x_ref[...])
        gnv = gn_ref[...]
        h_ref[...] = (n * gnv).astype(BF)
        dx, dgn = _rms_bwd(dh, n, r, gnv)
        dx_ref[...] = dxo_ref[...] + dx
        dgn_ref[...] += dgn

    row = pl.BlockSpec((tm, d), lambda i: (i, 0))
    wide = lambda w: pl.BlockSpec((tm, w), lambda i: (i, 0))
    return pl.pallas_call(
        body, name="mix_in_bwd", grid=(t // tm,),
        out_shape=(S((t, d), F32), S((1, d), F32), S((t, d), BF)),
        in_specs=[wide(nq), wide(nkv), wide(ne), row, _resident((1, d)), _resident(w_in.shape), row],
        out_specs=(row, _acc_spec((1, d)), row),
        compiler_params=_params(("arbitrary",)),
    )(dq, dkv, de, x, gn, w_in, dxo)


def _row_tile(rows):
    for tr in (256, 352, 128, 64, 16, 8):
        if rows % tr == 0:
            return tr
    return rows


def _add_bf16(a, b, name):
    rows, cols = a.shape
    tr = _row_tile(rows)

    def body(a_ref, b_ref, o_ref):
        o_ref[...] = (a_ref[...].astype(F32) + b_ref[...].astype(F32)).astype(BF)

    blk = pl.BlockSpec((tr, cols), lambda i: (i, 0))
    return pl.pallas_call(body, name=name, grid=(rows // tr,), out_shape=S((rows, cols), BF),
                          in_specs=[blk, blk], out_specs=blk, compiler_params=_params(("parallel",)))(a, b)


def _adamw(w, g, m, v, name):
    rows, cols = w.shape
    tr = _row_tile(rows)

    def body(w_ref, g_ref, m_ref, v_ref, d_ref, mo_ref, vo_ref):
        gv = g_ref[...]
        mn = ADAM_B1 * m_ref[...] + (1.0 - ADAM_B1) * gv
        vn = ADAM_B2 * v_ref[...] + (1.0 - ADAM_B2) * (gv * gv)
        m_hat = mn / (1.0 - ADAM_B1 ** ADAM_STEP)
        v_hat = vn / (1.0 - ADAM_B2 ** ADAM_STEP)
        d_ref[...] = -ADAM_LR * (m_hat / (jnp.sqrt(v_hat) + ADAM_EPS) + ADAM_WD * w_ref[...])
        mo_ref[...] = mn
        vo_ref[...] = vn

    blk = pl.BlockSpec((tr, cols), lambda i: (i, 0))
    shp = S((rows, cols), F32)
    return pl.pallas_call(body, name=name, grid=(rows // tr,), out_shape=(shp, shp, shp),
                          in_specs=[blk] * 4, out_specs=(blk,) * 3, compiler_params=_params(("parallel",)))(w, g, m, v)


def _place():
    x, y, c = lax.axis_index("x"), lax.axis_index("y"), lax.axis_index("c")
    return x, y, c


OTHER_CHIPS = ((1, 0), (0, 1), (1, 1))


def _flip(v, f):
    return 1 - v if f else v


def _remote(src, dst, ssem, rsem, dev):
    return pltpu.make_async_remote_copy(src_ref=src, dst_ref=dst, send_sem=ssem, recv_sem=rsem,
                                        device_id=dev, device_id_type=MESH)


class _Stager:
    def __init__(self, buf, lsem, ssem):
        self.buf, self.lsem, self.ssem = buf, lsem, ssem
        self.slots = buf.shape[0]
        self.count = 0
        self.inflight = [None] * self.slots

    def _load(self, src):
        s = self.count % self.slots
        self.count += 1
        if self.inflight[s] is not None:
            self.inflight[s]()
        stage = self.buf.at[(s,) + tuple(pl.ds(0, n) for n in src.shape)]
        load = pltpu.make_async_copy(src, stage, self.lsem.at[s])
        load.start()
        load.wait()
        return s, stage

    def send(self, src, dst, rsem, dev):
        s, stage = self._load(src)
        cp = _remote(stage, dst, self.ssem.at[s], rsem, dev)
        cp.start()
        self.inflight[s] = cp.wait_send

    def copy(self, src, dst):
        s, stage = self._load(src)
        cp = pltpu.make_async_copy(stage, dst, self.ssem.at[s])
        cp.start()
        self.inflight[s] = cp.wait

    def drain(self):
        for wait in self.inflight:
            if wait is not None:
                wait()


STAGE_SLOTS = 3
STAGE_ROWS = 176


def _chunk_rows(rows):
    for rc in (STAGE_ROWS, 128, 64, 32, 16):
        if rows % rc == 0:
            return rc
    raise ValueError(rows)


def _stage_scratch(cols, dtype):
    return [pltpu.VMEM((STAGE_SLOTS, STAGE_ROWS, cols), dtype), pltpu.SemaphoreType.DMA((STAGE_SLOTS,)),
            pltpu.SemaphoreType.DMA((STAGE_SLOTS,))]


def _exchange(hook, name):
    n_in, n_out = len(hook.ins), len(hook.out_shape)

    def body(*refs):
        ins, outs, scr = refs[:n_in], refs[n_in:n_in + n_out], refs[n_in + n_out:]
        hook.start(ins, outs, scr)
        hook.finish(ins, outs, scr)

    return pl.pallas_call(
        body, name=name, out_shape=tuple(hook.out_shape), in_specs=[ANY] * n_in, out_specs=(ANY,) * n_out,
        scratch_shapes=list(hook.scratch), compiler_params=_params(),
    )(*hook.ins)


class _GatherHook:
    def __init__(self, shards, kinds):
        self.ins, self.kinds, n = list(shards), list(kinds), len(shards)
        self.out_shape = tuple(
            S((w.shape[0], 4 * w.shape[1]), BF) if k == "col" else S((4,) + w.shape, BF) for w, k in zip(shards, kinds))
        dma = pltpu.SemaphoreType.DMA
        self.scratch = ([dma((n, 3)) for _ in range(5)] + [dma((n,)), dma((n,))]
                        + [pltpu.VMEM((3, w.shape[0] // 2, w.shape[1]), BF) for w in shards]
                        + [pltpu.VMEM(w.shape, BF) for w in shards])

    def _window(self, outs, i, s, half):
        rows, cols = self.ins[i].shape
        rh = rows // 2
        start = pl.multiple_of(half * rh, 16)
        if self.kinds[i] == "col":
            return outs[i].at[pl.ds(start, rh), pl.ds(pl.multiple_of(s * cols, 128), cols)]
        return outs[i].at[s, pl.ds(start, rh), :]

    def _copies(self, ins, outs, scr):
        n = len(ins)
        ssem, rsem, fssem, frsem, ksem, lsem, osem = scr[:7]
        land, own = scr[7:7 + n], scr[7 + n:7 + 2 * n]
        x, y, c = _place()
        sibling = (x, y, 1 - c)
        loads, stores, sends, forwards, keeps, passed = [], [], [], [], [], []
        for i in range(n):
            rows, cols = self.ins[i].shape
            rh = rows // 2
            mine = (outs[i].at[:, pl.ds(pl.multiple_of((2 * x + y) * cols, 128), cols)] if self.kinds[i] == "col"
                    else outs[i].at[2 * x + y])
            loads.append(functools.partial(pltpu.make_async_copy, ins[i], own[i], lsem.at[i]))
            stores.append(functools.partial(pltpu.make_async_copy, own[i], mine, osem.at[i]))
            src = ins[i].at[pl.ds(pl.multiple_of(c * rh, 16), rh), :]
            for j, (fx, fy) in enumerate(OTHER_CHIPS):
                px, py = _flip(x, fx), _flip(y, fy)
                sends.append(functools.partial(_remote, src, land[i].at[j], ssem.at[i, j], rsem.at[i, j], (px, py, c)))
                here = self._window(outs, i, 2 * px + py, c)
                forwards.append(functools.partial(_remote, land[i].at[j], here, fssem.at[i, j], frsem.at[i, j], sibling))
                keeps.append(functools.partial(pltpu.make_async_copy, land[i].at[j], here, ksem.at[i, j]))
                there = self._window(outs, i, 2 * px + py, 1 - c)
                passed.append(functools.partial(_remote, there, there, fssem.at[i, j], frsem.at[i, j], sibling))
        return loads, stores, sends, forwards, keeps, passed

    def start(self, ins, outs, scr):
        loads, _, sends, _, _, _ = self._copies(ins, outs, scr)
        for make in sends + loads:
            make().start()

    def finish(self, ins, outs, scr):
        loads, stores, sends, forwards, keeps, passed = self._copies(ins, outs, scr)
        for load, store in zip(loads, stores):
            load().wait()
            store().start()
        for send, forward, keep in zip(sends, forwards, keeps):
            send().wait_recv()
            forward().start()
            keep().start()
        for make in passed:
            make().wait_recv()
        for make in sends + forwards:
            make().wait_send()
        for make in keeps + stores:
            make().wait()


def _reduce_siblings(grads, kinds, name):
    n = len(grads)

    def half_shape(gr, k):
        return S((gr.shape[0] // 2, gr.shape[1]), BF) if k == "col" else S((4, gr.shape[1] // 2, gr.shape[2]), BF)

    halves = tuple(half_shape(gr, k) for gr, k in zip(grads, kinds))

    def body(*refs):
        ins, mine, sib = refs[:n], refs[n:2 * n], refs[2 * n:3 * n]
        rsem, lsem, buf, blsem, bssem = refs[3 * n:]
        x, y, c = _place()
        sibling = (x, y, 1 - c)
        stager = _Stager(buf, blsem, bssem)
        pending = []
        for i in range(n):
            col = kinds[i] == "col"
            rh = (grads[i].shape[0] if col else grads[i].shape[1]) // 2
            rc = _chunk_rows(rh)
            for r0 in range(0, rh, rc):
                give = pl.multiple_of((1 - c) * rh + r0, 16)
                keep = pl.multiple_of(c * rh + r0, 16)
                if col:
                    stager.send(ins[i].at[pl.ds(give, rc), :], sib[i].at[pl.ds(r0, rc), :], rsem.at[i], sibling)
                    stager.copy(ins[i].at[pl.ds(keep, rc), :], mine[i].at[pl.ds(r0, rc), :])
                else:
                    for s in range(4):
                        stager.send(ins[i].at[s, pl.ds(give, rc), :], sib[i].at[s, pl.ds(r0, rc), :], rsem.at[i], sibling)
                        stager.copy(ins[i].at[s, pl.ds(keep, rc), :], mine[i].at[s, pl.ds(r0, rc), :])
        for i in range(n):
            _remote(sib[i], sib[i], lsem.at[i], rsem.at[i], sibling).wait_recv()
        stager.drain()
        for wait in pending:
            wait()

    max_cols = max(gr.shape[-1] for gr in grads)
    sems = [pltpu.SemaphoreType.DMA((n,)), pltpu.SemaphoreType.DMA((n,))]
    return pl.pallas_call(
        body, name=name, out_shape=halves + halves,
        in_specs=[ANY] * n, out_specs=tuple(ANY for _ in range(2 * n)), scratch_shapes=sems + _stage_scratch(max_cols, BF),
        compiler_params=_params(),
    )(*grads)


class _ChipsHook:
    def __init__(self, parts, kinds):
        self.ins, self.kinds, n = list(parts), list(kinds), len(parts)
        self.out_shape = tuple(
            S((4, p.shape[0], p.shape[1] // 4), BF) if k == "col" else S(p.shape, BF) for p, k in zip(parts, kinds))
        dma = pltpu.SemaphoreType.DMA
        self.scratch = ([dma((n, 3)), dma((n, 3)), dma((n,)), dma((n,))]
                        + [pltpu.VMEM(o.shape[1:], BF) for o in self.out_shape])

    def _slab(self, ins, i, s):
        _, rows, cols = self.out_shape[i].shape
        if self.kinds[i] == "col":
            return ins[i].at[:, pl.ds(pl.multiple_of(s * cols, 128), cols)]
        return ins[i].at[s]

    def _copies(self, ins, outs, scr):
        ssem, rsem, lsem, osem = scr[:4]
        own = scr[4:]
        x, y, c = _place()
        loads, stores, sends = [], [], []
        for i in range(len(ins)):
            loads.append(functools.partial(pltpu.make_async_copy, self._slab(ins, i, 2 * x + y), own[i], lsem.at[i]))
            stores.append(functools.partial(pltpu.make_async_copy, own[i], outs[i].at[3], osem.at[i]))
            for j, (fx, fy) in enumerate(OTHER_CHIPS):
                px, py = _flip(x, fx), _flip(y, fy)
                sends.append(functools.partial(_remote, self._slab(ins, i, 2 * px + py), outs[i].at[j], ssem.at[i, j],
                                               rsem.at[i, j], (px, py, c)))
        return loads, stores, sends

    def start(self, ins, outs, scr):
        loads, _, sends = self._copies(ins, outs, scr)
        for make in sends + loads:
            make().start()

    def finish(self, ins, outs, scr):
        loads, stores, sends = self._copies(ins, outs, scr)
        for load, store in zip(loads, stores):
            load().wait()
            store().start()
        for make in sends + stores:
            make().wait()


def _sum_share(slabs, name):
    _, rh, cols = slabs.shape
    tr = _row_tile(rh)
    nt = rh // tr

    def body(p_ref, o_ref, stage, lsem, ssem, rsem):
        i = pl.program_id(0)
        x, y, c = _place()
        sibling = (x, y, 1 - c)
        acc = p_ref[3].astype(F32)
        for k in range(3):
            acc = acc + p_ref[k].astype(F32)
        stage[i] = acc

        def copies(k):
            dst = o_ref.at[pl.ds(pl.multiple_of(c * rh + k * tr, 8), tr), :]
            return (pltpu.make_async_copy(stage.at[k], dst, lsem.at[k]),
                    _remote(stage.at[k], dst, ssem.at[k], rsem, sibling))

        for cp in copies(i):
            cp.start()

        @pl.when(i == nt - 1)
        def _():
            for k in range(nt):
                local, remote = copies(k)
                local.wait()
                remote.wait_send()
            got = o_ref.at[pl.ds(pl.multiple_of((1 - c) * rh, 8), rh), :]
            _remote(got, got, ssem.at[0], rsem, sibling).wait_recv()

    return pl.pallas_call(
        body, name=name, grid=(nt,), out_shape=S((2 * rh, cols), F32),
        in_specs=[pl.BlockSpec((4, tr, cols), lambda i: (0, i, 0))], out_specs=ANY,
        scratch_shapes=[pltpu.VMEM((nt, tr, cols), F32), pltpu.SemaphoreType.DMA((nt,)), pltpu.SemaphoreType.DMA((nt,)),
                        pltpu.SemaphoreType.DMA],
        compiler_params=_params(("arbitrary",)))(slabs)


def _all_reduce_small(buf, name):
    shape = buf.shape

    def body(in_ref, out_ref, slots, ssem, rsem):
        x, y, c = _place()
        me = 4 * x + 2 * y + c
        slots[me] = in_ref[...]
        sends = []
        for r in range(1, 8):
            fx, fy, fc = (r >> 2) & 1, (r >> 1) & 1, r & 1
            cp = _remote(in_ref, slots.at[me], ssem.at[r - 1], rsem.at[r - 1], (_flip(x, fx), _flip(y, fy), _flip(c, fc)))
            cp.start()
            sends.append(cp)
        for r in range(1, 8):
            fx, fy, fc = (r >> 2) & 1, (r >> 1) & 1, r & 1
            px, py, pc = _flip(x, fx), _flip(y, fy), _flip(c, fc)
            _remote(in_ref, slots.at[4 * px + 2 * py + pc], ssem.at[r - 1], rsem.at[r - 1], (px, py, pc)).wait_recv()
        for cp in sends:
            cp.wait_send()
        acc = slots[0]
        for k in range(1, 8):
            acc = acc + slots[k]
        out_ref[...] = acc

    return pl.pallas_call(
        body, name=name, out_shape=S(shape, F32), in_specs=[VMEM_SPEC], out_specs=VMEM_SPEC,
        scratch_shapes=[pltpu.VMEM((8,) + shape, F32), pltpu.SemaphoreType.DMA((7,)), pltpu.SemaphoreType.DMA((7,))],
    )(buf)


BIG = ("ffn1_w_gu", "ffn1_w_down", "w_in", "w_out", "xattn_wq", "xattn_wkv", "xattn_wo", "ffn2_w_gu", "ffn2_w_down")
KIND = {"ffn1_w_gu": "col", "ffn1_w_down": "row", "w_in": "col", "w_out": "row", "xattn_wq": "row",
        "xattn_wkv": "col", "xattn_wo": "row", "ffn2_w_gu": "col", "ffn2_w_down": "row"}
WEIGHTS = ("rel_bias", "ffn1_norm", "ffn1_w_gu", "ffn1_w_down", "mix_norm", "w_in", "sinks", "conv_w", "w_out",
           "xattn_norm", "mem_norm", "xattn_wq", "xattn_wkv", "xattn_wo", "ffn2_norm", "ffn2_w_gu", "ffn2_w_down",
           "final_norm")
SMALL_ROWS = 16
GAIN_ROW = {"ffn1_norm": 0, "mix_norm": 1, "xattn_norm": 2, "mem_norm": 3, "ffn2_norm": 4, "final_norm": 5}
CONV_ROW, SINK_ROW, BIAS_ROW, LOSS_ROW = 6, 9, 10, 11


def _rows_block(rows, d):
    buf = jnp.zeros((SMALL_ROWS, d), F32)
    for r, v in rows.items():
        buf = lax.dynamic_update_slice(buf, v.reshape(1, -1).astype(F32), (r, 0))
    return buf


def _local_step(x, mem, pos, target, w, gains, rel_bias, sinks, conv_w, shards=None):
    t, d = x.shape
    dist = shards is not None
    w = dict(w)
    grads, slabs = {}, {}
    pos_col = pos.reshape(t, 1)
    pos_row = pos.reshape(1, t)
    bias_t = _bias_build(rel_bias).reshape(SWA_KV_HEADS, GROUP_ROWS, 2 * BLOCK)

    def gather(names):
        return _GatherHook([shards[k] for k in names], [KIND[k] for k in names]) if dist else None

    def gathered(names, hook):
        if dist:
            for k, gw in zip(names, hook.results):
                w[k] = gw if KIND[k] == "col" else gw.reshape(-1, gw.shape[-1])

    def pair_sums(names):
        if not dist:
            return None
        kinds = [KIND[k] for k in names]
        gnat = [grads[k] if KIND[k] == "col" else grads[k].reshape(4, -1, grads[k].shape[-1]) for k in names]
        halves = _reduce_siblings(gnat, kinds, "reduce_siblings_" + names[0])
        parts = {}
        for i, k in enumerate(names):
            mine, sib = halves[i], halves[len(names) + i]
            if KIND[k] == "row":
                mine, sib = mine.reshape(-1, mine.shape[-1]), sib.reshape(-1, sib.shape[-1])
            p = _add_bf16(mine, sib, "pair_sum_" + k)
            parts[k] = p if KIND[k] == "col" else p.reshape(4, -1, p.shape[-1])
        return parts

    def chips(names, parts):
        return _ChipsHook([parts[k] for k in names], [KIND[k] for k in names]) if dist else None

    def reduced(names, hook):
        if dist:
            slabs.update(zip(names, hook.results))

    names = ("w_in", "w_out")
    hook = gather(names)
    x1, g1, u1 = _ffn_fwd(x, gains["ffn1_norm"], w["ffn1_w_gu"], w["ffn1_w_down"], "ffn1_fwd", hook)
    gathered(names, hook)
    names = ("xattn_wq", "xattn_wkv", "xattn_wo")
    hook = gather(names)
    qkv, e = _mix_proj(x1, gains["mix_norm"], w["w_in"], hook)
    gathered(names, hook)
    names = ("ffn2_w_gu",)
    hook = gather(names)
    attn = _swa_fwd(qkv, pos_col, pos_row, bias_t, sinks, hook)
    gathered(names, hook)
    names = ("ffn2_w_down",)
    hook = gather(names)
    x2, merged = _mix_out_fwd(e, attn, conv_w, w["w_out"], x1, hook)
    gathered(names, hook)
    mh, kv = _mem_kv(mem, gains["mem_norm"], w["xattn_wkv"])
    x3, qx, o = _xattn_fwd(x2, gains["xattn_norm"], w["xattn_wq"], kv, w["xattn_wo"])
    x4, g2, u2 = _ffn_fwd(x3, gains["ffn2_norm"], w["ffn2_w_gu"], w["ffn2_w_down"], "ffn2_fwd")
    dx4, d_final, loss = _loss_bwd(x4, gains["final_norm"], target)

    dx3, d_ffn2, dgu2, a2, h4, dyh4 = _ffn_bwd(dx4, x3, gains["ffn2_norm"], g2, u2, w["ffn2_w_gu"], w["ffn2_w_down"],
                                               "ffn2_bwd")
    grads["ffn2_w_gu"] = _dw(h4, dgu2, 1408, "dw_ffn2_gu")
    grads["ffn2_w_down"] = _dw(a2, dyh4, 512, "dw_ffn2_down")
    parts = pair_sums(("ffn2_w_gu", "ffn2_w_down"))
    hook = chips(("ffn2_w_gu",), parts)
    dx2, d_xattn, dkv, dxh3, h3, dqx = _xattn_bwd(dx3, x2, gains["xattn_norm"], qx, kv, w["xattn_wq"], w["xattn_wo"], hook)
    reduced(("ffn2_w_gu",), hook)
    grads["xattn_wo"] = _dw(o, dxh3, 1024, "dw_wo")
    grads["xattn_wq"] = _dw(h3, dqx, 1024, "dw_wq")
    grads["xattn_wkv"], d_mem = _mem_bwd(dkv, mh, mem, gains["mem_norm"], w["xattn_wkv"])
    hook = chips(("ffn2_w_down",), parts)
    dattn, de, dcw, dxh2 = _mix_out_bwd(dx2, e, attn, conv_w, w["w_out"], hook)
    reduced(("ffn2_w_down",), hook)
    grads["w_out"] = _dw(merged, dxh2, 1024, "dw_wout")
    names = ("xattn_wo", "xattn_wq", "xattn_wkv", "w_out")
    hook = chips(names, pair_sums(names))
    dq, dkvs, gb, dsk = _swa_bwd(qkv, dattn, pos_col, pos_row, bias_t, sinks, hook)
    reduced(names, hook)
    d_rel_bias, d_sinks = _bias_reduce(gb.reshape(SWA_HEADS, BLOCK, 2 * BLOCK), dsk.reshape(SWA_HEADS, BLOCK, 1))
    dx1, d_mix, h2 = _mix_in_bwd(dq, dkvs, de, x1, gains["mix_norm"], w["w_in"], dx2)
    n_in = w["w_in"].shape[1]
    gw = _dw(h2, dq, 512, "dw_win_q", ncols=n_in)
    gw = _dw(h2, dkvs, 512, "dw_win_kv", into=gw, col_block=dq.shape[1] // 512, ncols=n_in)
    grads["w_in"] = _dw(h2, de, 512, "dw_win_e", into=gw, col_block=(dq.shape[1] + dkvs.shape[1]) // 512, ncols=n_in)
    names = ("w_in",)
    hook = chips(names, pair_sums(names))
    dx0, d_ffn1, dgu1, a1, h1, dyh1 = _ffn_bwd(dx1, x, gains["ffn1_norm"], g1, u1, w["ffn1_w_gu"], w["ffn1_w_down"],
                                               "ffn1_bwd", hook)
    reduced(names, hook)
    grads["ffn1_w_gu"] = _dw(h1, dgu1, 1408, "dw_ffn1_gu")
    grads["ffn1_w_down"] = _dw(a1, dyh1, 512, "dw_ffn1_down")
    names = ("ffn1_w_gu", "ffn1_w_down")
    hook = chips(names, pair_sums(names))
    if dist:
        hook.results = _exchange(hook, "reduce_chips_ffn1")
    reduced(names, hook)

    rows = {0: d_ffn1, 1: d_mix, 2: d_xattn, 3: d_mem, 4: d_ffn2, 5: d_final, SINK_ROW: d_sinks, BIAS_ROW: d_rel_bias,
            LOSS_ROW: loss[0, 0:1]}
    rows.update({CONV_ROW + j: dcw[j] for j in range(3)})
    return dx0, (slabs if dist else grads), _rows_block(rows, d)


def kernel(x, mem, positions, rel_bias, ffn1_norm, ffn1_w_gu, ffn1_w_down, mix_norm, w_in, sinks, conv_w, w_out, xattn_norm, mem_norm, xattn_wq, xattn_wkv, xattn_wo, ffn2_norm, ffn2_w_gu, ffn2_w_down, final_norm, loss_target, m_rel_bias, m_ffn1_norm, m_ffn1_w_gu, m_ffn1_w_down, m_mix_norm, m_w_in, m_sinks, m_conv_w, m_w_out, m_xattn_norm, m_mem_norm, m_xattn_wq, m_xattn_wkv, m_xattn_wo, m_ffn2_norm, m_ffn2_w_gu, m_ffn2_w_down, m_final_norm, v_rel_bias, v_ffn1_norm, v_ffn1_w_gu, v_ffn1_w_down, v_mix_norm, v_w_in, v_sinks, v_conv_w, v_w_out, v_xattn_norm, v_mem_norm, v_xattn_wq, v_xattn_wkv, v_xattn_wo, v_ffn2_norm, v_ffn2_w_gu, v_ffn2_w_down, v_final_norm):
    args = dict(locals())
    wts = {k: args[k] for k in WEIGHTS}
    mom = {k: args["m_" + k] for k in WEIGHTS}
    var = {k: args["v_" + k] for k in WEIGHTS}
    d = x.shape[-1]
    s_me = 2 * lax.axis_index("x") + lax.axis_index("y")

    shards = {k: wts[k][0].astype(BF) for k in BIG}
    first = ("ffn1_w_gu", "ffn1_w_down")
    gathered = _exchange(_GatherHook([shards[k] for k in first], [KIND[k] for k in first]), "gather_ffn1")
    whole = {k: (gw if KIND[k] == "col" else gw.reshape(-1, gw.shape[-1])) for k, gw in zip(first, gathered)}
    cw_cols = conv_w.shape[-1]
    placed = lax.dynamic_update_slice(jnp.zeros((SMALL_ROWS, d), F32), 0.5 * conv_w[0], (0, s_me * cw_cols))
    conv_whole = _all_reduce_small(placed, "gather_conv_w")[0:3]

    gains = {k: wts[k].reshape(1, d) for k in GAIN_ROW}
    dx0, slabs, small = _local_step(x[0], mem[0], positions[0], loss_target[0], whole, gains, rel_bias, sinks,
                                    conv_whole, shards)

    shard_grads = {k: _sum_share(slabs[k], "sum_share_" + k) for k in BIG}

    small_sum = _all_reduce_small(small, "reduce_small")
    loss = small_sum[LOSS_ROW, 0]

    out_g, out_d, out_m, out_v = {}, {}, {}, {}
    for k in BIG:
        g2d = shard_grads[k]
        dl, mn, vn = _adamw(wts[k][0], g2d, mom[k][0], var[k][0], "adamw_" + k)
        out_g[k], out_d[k], out_m[k], out_v[k] = g2d[None], dl[None], mn[None], vn[None]

    conv_g = lax.dynamic_slice(small_sum, (CONV_ROW, s_me * cw_cols), (3, cw_cols))

    def pack(src, conv_block):
        rows = {r: src[k] for k, r in GAIN_ROW.items()}
        rows.update({CONV_ROW + j: conv_block[j] for j in range(3)})
        rows[SINK_ROW], rows[BIAS_ROW] = src["sinks"], src["rel_bias"]
        return _rows_block(rows, d)

    small_g = {k: small_sum[r] for k, r in GAIN_ROW.items()}
    small_g["sinks"] = small_sum[SINK_ROW, 0:sinks.size]
    small_g["rel_bias"] = small_sum[BIAS_ROW, 0:rel_bias.size]
    gp = pack(small_g, conv_g)
    dl, mn, vn = _adamw(pack(wts, conv_w[0]), gp, pack(mom, m_conv_w[0]), pack(var, v_conv_w[0]), "adamw_small")

    def unpack(buf, k):
        if k in GAIN_ROW:
            return buf[GAIN_ROW[k]].reshape(wts[k].shape)
        if k == "conv_w":
            return buf[CONV_ROW:CONV_ROW + 3, 0:cw_cols][None]
        if k == "sinks":
            return buf[SINK_ROW, 0:sinks.size].reshape(sinks.shape)
        return buf[BIAS_ROW, 0:rel_bias.size].reshape(rel_bias.shape)

    for k in WEIGHTS:
        if k not in KIND:
            out_g[k], out_d[k], out_m[k], out_v[k] = unpack(gp, k), unpack(dl, k), unpack(mn, k), unpack(vn, k)

    return (loss, dx0[None], *[out_g[k] for k in WEIGHTS], *[out_d[k] for k in WEIGHTS],
            *[out_m[k] for k in WEIGHTS], *[out_v[k] for k in WEIGHTS])
```

```python
import functools
import math

import jax
import jax.numpy as jnp
from jax import lax
from jax.experimental import pallas as pl
from jax.experimental.pallas import tpu as pltpu

BF = jnp.bfloat16
F32 = jnp.float32
I32 = jnp.int32
S = jax.ShapeDtypeStruct

EPS = 1e-6
NEG = -1e30
POS_PAD = 1 << 30
WINDOW = 128
BLOCK = 128
HEAD_DIM = 64
SWA_HEADS = 16
SWA_KV_HEADS = 4
SWA_GROUP = SWA_HEADS // SWA_KV_HEADS
MEM_HEADS = 4
REL_BUCKETS = 32
REL_MAX_DIST = 128
ADAM_LR = 0.001
ADAM_B1 = 0.9
ADAM_B2 = 0.999
ADAM_EPS = 1e-08
ADAM_WD = 0.01
ADAM_STEP = 10

V7X_VMEM_LIMIT_BYTES = 56 * 1024 * 1024
MESH = pl.DeviceIdType.MESH
ANY = pl.BlockSpec(memory_space=pl.ANY)
VMEM_SPEC = pl.BlockSpec(memory_space=pltpu.VMEM)
SMEM_SPEC = pl.BlockSpec(memory_space=pltpu.SMEM)


def _params(sem=None):
    return pltpu.CompilerParams(dimension_semantics=sem, vmem_limit_bytes=V7X_VMEM_LIMIT_BYTES)


def _resident(shape):
    nd = len(shape)
    return pl.BlockSpec(shape, lambda *_: (0,) * nd, pipeline_mode=pl.Buffered(1))


def _acc_spec(shape):
    nd = len(shape)
    return pl.BlockSpec(shape, lambda *_: (0,) * nd)


def _call(body, *, name, grid, out_shape, in_specs, out_specs, args, sem, scratch_shapes=(), aliases=None, hook=None):
    aliases = aliases or {}
    if hook is None:
        return pl.pallas_call(body, name=name, grid=grid, out_shape=out_shape, in_specs=in_specs, out_specs=out_specs,
                              scratch_shapes=list(scratch_shapes), input_output_aliases=aliases,
                              compiler_params=_params(sem))(*args)
    n_in, n_out, n_scr = len(in_specs), len(out_shape), len(scratch_shapes)
    h_in, h_out = len(hook.ins), len(hook.out_shape)

    def at_step(pick):
        conds = [pl.program_id(ax) == pick(size) for ax, size in enumerate(grid)]
        return functools.reduce(jnp.logical_and, conds)

    def hosted(*refs):
        k_in, x_in = refs[:n_in], refs[n_in:n_in + h_in]
        o0 = n_in + h_in
        k_out, x_out = refs[o0:o0 + n_out], refs[o0 + n_out:o0 + n_out + h_out]
        s0 = o0 + n_out + h_out
        k_scr, x_scr = refs[s0:s0 + n_scr], refs[s0 + n_scr:]

        @pl.when(at_step(lambda size: 0))
        def _():
            hook.start(x_in, x_out, x_scr)
        body(*k_in, *k_out, *k_scr)

        @pl.when(at_step(lambda size: size - 1))
        def _():
            hook.finish(x_in, x_out, x_scr)

    res = pl.pallas_call(
        hosted, name=name, grid=grid, out_shape=tuple(out_shape) + tuple(hook.out_shape),
        in_specs=list(in_specs) + [ANY] * h_in, out_specs=tuple(out_specs) + (ANY,) * h_out,
        scratch_shapes=list(scratch_shapes) + list(hook.scratch), input_output_aliases=aliases,
        compiler_params=_params(("arbitrary",) * len(grid)),
    )(*args, *hook.ins)
    hook.results = res[n_out:]
    return res[:n_out]


def _nn(a, b):
    return jnp.dot(a, b, preferred_element_type=F32)


def _nt(a, b):
    return lax.dot_general(a, b, (((1,), (1,)), ((), ())), preferred_element_type=F32)


def _tn(a, b):
    return lax.dot_general(a, b, (((0,), (0,)), ((), ())), preferred_element_type=F32)


def _sigmoid(v):
    return 1.0 / (1.0 + jnp.exp(-v))


def _rms(x):
    r = lax.rsqrt(jnp.mean(x * x, axis=-1, keepdims=True) + EPS)
    return x * r, r


def _rms_bwd(dh, n, r, g):
    dn = dh * g
    dx = r * (dn - n * jnp.mean(dn * n, axis=-1, keepdims=True))
    return dx, jnp.sum(dh * n, axis=0, keepdims=True)


def _ffn_fwd(x, gn, wgu, wd, name, hook=None):
    t, d = x.shape
    f = wd.shape[0]
    tm, fc = 256, 1408

    def body(x_ref, gn_ref, wgu_ref, wd_ref, xo_ref, g_ref, u_ref):
        xv = x_ref[...]
        n, _ = _rms(xv)
        h = (n * gn_ref[...]).astype(BF)
        acc = jnp.zeros((tm, d), F32)
        for c0 in range(0, f, fc):
            g = _nn(h, wgu_ref[:, c0:c0 + fc])
            u = _nn(h, wgu_ref[:, f + c0:f + c0 + fc])
            g_ref[:, c0:c0 + fc] = g.astype(BF)
            u_ref[:, c0:c0 + fc] = u.astype(BF)
            a = (g * _sigmoid(g)) * u
            acc = acc + _nn(a.astype(BF), wd_ref[c0:c0 + fc, :])
        xo_ref[...] = xv + 0.5 * acc

    return _call(
        body, name=name, grid=(t // tm,),
        out_shape=(S((t, d), F32), S((t, f), BF), S((t, f), BF)),
        in_specs=[pl.BlockSpec((tm, d), lambda i: (i, 0)), _resident((1, d)), _resident(wgu.shape), _resident(wd.shape)],
        out_specs=(pl.BlockSpec((tm, d), lambda i: (i, 0)), pl.BlockSpec((tm, f), lambda i: (i, 0)),
                   pl.BlockSpec((tm, f), lambda i: (i, 0))),
        sem=("parallel",), args=(x, gn, wgu, wd), hook=hook)


def _mix_proj(x, gn, w_in, hook=None):
    t, d = x.shape
    tm = 256
    nqkv = 1536
    ne = w_in.shape[1] - nqkv

    def body(x_ref, gn_ref, w_ref, qkv_ref, e_ref):
        n, _ = _rms(x_ref[...])
        h = (n * gn_ref[...]).astype(BF)
        qkv_ref[...] = _nn(h, w_ref[:, 0:nqkv]).astype(BF)
        for c0 in range(0, ne, 1024):
            e_ref[:, c0:c0 + 1024] = _nn(h, w_ref[:, nqkv + c0:nqkv + c0 + 1024])

    return _call(
        body, name="mix_proj", grid=(t // tm,),
        out_shape=(S((t, nqkv), BF), S((t, ne), F32)),
        in_specs=[pl.BlockSpec((tm, d), lambda i: (i, 0)), _resident((1, d)), _resident(w_in.shape)],
        out_specs=(pl.BlockSpec((tm, nqkv), lambda i: (i, 0)), pl.BlockSpec((tm, ne), lambda i: (i, 0))),
        sem=("parallel",), args=(x, gn, w_in), hook=hook)


def _t5_bucket(rel):
    n = jnp.maximum(rel, 0)
    max_exact = REL_BUCKETS // 2
    nf = jnp.maximum(n, 1).astype(F32)
    large = max_exact + (jnp.log(nf / max_exact) / math.log(REL_MAX_DIST / max_exact)
                         * (REL_BUCKETS - max_exact)).astype(I32)
    large = jnp.minimum(large, REL_BUCKETS - 1)
    return jnp.where(n < max_exact, n, large)


def _block_rel():
    i = lax.broadcasted_iota(I32, (BLOCK, 2 * BLOCK), 0)
    j = lax.broadcasted_iota(I32, (BLOCK, 2 * BLOCK), 1)
    return i + BLOCK - j


def _bias_build(rel_bias):
    def body(rb_ref, o_ref):
        bucket = _t5_bucket(_block_rel())
        for h in range(SWA_HEADS):
            acc = jnp.zeros((BLOCK, 2 * BLOCK), F32)
            for b in range(REL_BUCKETS):
                acc = jnp.where(bucket == b, rb_ref[b, h], acc)
            o_ref[h] = acc

    return pl.pallas_call(
        body, name="bias_build", out_shape=S((SWA_HEADS, BLOCK, 2 * BLOCK), F32),
        in_specs=[SMEM_SPEC], out_specs=VMEM_SPEC,
    )(rel_bias)


GROUP_ROWS = SWA_GROUP * BLOCK


def _swa_visible(b, pq_ref, pkp_ref, pkc_ref):
    pk = jnp.concatenate([pkp_ref[...], pkc_ref[...]], axis=1)
    col = lax.broadcasted_iota(I32, (1, 2 * BLOCK), 1)
    pk = jnp.where(jnp.logical_and(b == 0, col < BLOCK), POS_PAD, pk)
    rel = jnp.concatenate([pq_ref[...]] * SWA_GROUP, axis=0) - pk
    return jnp.logical_and(rel >= 0, rel < WINDOW)


def _group_heads(ref, hk):
    h0 = hk * SWA_GROUP
    return jnp.concatenate([ref[:, (h0 + g) * HEAD_DIM:(h0 + g + 1) * HEAD_DIM] for g in range(SWA_GROUP)], axis=0)


def _group_sinks(sink_ref, hk):
    row = lax.broadcasted_iota(I32, (GROUP_ROWS, 1), 0)
    col = jnp.zeros((GROUP_ROWS, 1), F32) + sink_ref[0, hk * SWA_GROUP]
    for g in range(1, SWA_GROUP):
        col = jnp.where(row >= g * BLOCK, sink_ref[0, hk * SWA_GROUP + g], col)
    return col


def _swa_probs(qg, kh, vis, bias, sink):
    s = _nt(qg, kh) * (HEAD_DIM ** -0.5)
    s = jnp.where(vis, s + bias, NEG)
    m = jnp.maximum(jnp.max(s, axis=-1, keepdims=True), sink)
    p = jnp.exp(s - m)
    ps = jnp.exp(sink - m)
    inv = 1.0 / (jnp.sum(p, axis=-1, keepdims=True) + ps)
    return p * inv, ps * inv


def _swa_fwd(qkv, pos_col, pos_row, bias_t, sinks, hook=None):
    t = qkv.shape[0]
    nb = t // BLOCK
    qw = SWA_HEADS * HEAD_DIM
    kw = SWA_KV_HEADS * HEAD_DIM

    def body(q_ref, kp_ref, kc_ref, vp_ref, vc_ref, pq_ref, pkp_ref, pkc_ref, bias_ref, sink_ref, o_ref):
        b = pl.program_id(0)
        vis = _swa_visible(b, pq_ref, pkp_ref, pkc_ref)
        k2 = jnp.concatenate([kp_ref[...], kc_ref[...]], axis=0)
        v2 = jnp.concatenate([vp_ref[...], vc_ref[...]], axis=0)
        for hk in range(SWA_KV_HEADS):
            kh = k2[:, hk * HEAD_DIM:(hk + 1) * HEAD_DIM]
            vh = v2[:, hk * HEAD_DIM:(hk + 1) * HEAD_DIM]
            pn, _ = _swa_probs(_group_heads(q_ref, hk), kh, vis, bias_ref[hk], _group_sinks(sink_ref, hk))
            o = _nn(pn.astype(BF), vh)
            for g in range(SWA_GROUP):
                h = hk * SWA_GROUP + g
                o_ref[:, h * HEAD_DIM:(h + 1) * HEAD_DIM] = o[g * BLOCK:(g + 1) * BLOCK]

    prev = lambda b: jnp.maximum(b - 1, 0)
    return _call(
        body, name="swa_fwd", grid=(nb,), out_shape=(S((t, qw), F32),),
        in_specs=[
            pl.BlockSpec((BLOCK, qw), lambda b: (b, 0)),
            pl.BlockSpec((BLOCK, kw), lambda b: (prev(b), qw // kw)),
            pl.BlockSpec((BLOCK, kw), lambda b: (b, qw // kw)),
            pl.BlockSpec((BLOCK, kw), lambda b: (prev(b), qw // kw + 1)),
            pl.BlockSpec((BLOCK, kw), lambda b: (b, qw // kw + 1)),
            pl.BlockSpec((BLOCK, 1), lambda b: (b, 0)),
            pl.BlockSpec((1, BLOCK), lambda b: (0, prev(b))),
            pl.BlockSpec((1, BLOCK), lambda b: (0, b)),
            _resident(bias_t.shape),
            SMEM_SPEC,
        ],
        out_specs=(pl.BlockSpec((BLOCK, qw), lambda b: (b, 0)),),
        sem=("parallel",), args=(qkv, qkv, qkv, qkv, qkv, pos_col, pos_row, pos_row, bias_t, sinks), hook=hook)[0]


def _conv_taps(z, zh, first):
    tm = z.shape[0]
    zh = jnp.where(first, 0.0, zh)
    row = lax.broadcasted_iota(I32, (tm, 1), 0)
    z1 = jnp.where(row == 0, zh[7:8, :], pltpu.roll(z, 1, 0))
    z2 = jnp.where(row == 0, zh[6:7, :], jnp.where(row == 1, zh[7:8, :], pltpu.roll(z, 2, 0)))
    return z1, z2


def _mix_out_fwd(e, attn, conv_w, w_out, x, hook=None):
    t, d = x.shape
    tm = 256
    hb = tm // 8

    def body(c_ref, b_ref, u_ref, ga_ref, gc_ref, ch_ref, uh_ref, attn_ref, cw_ref, w_ref, x_ref, xo_ref, mg_ref):
        i = pl.program_id(0)
        z = c_ref[...] * u_ref[...]
        z1, z2 = _conv_taps(z, ch_ref[...] * uh_ref[...], i == 0)
        s = cw_ref[0:1, :] * z2 + cw_ref[1:2, :] * z1 + cw_ref[2:3, :] * z
        conv = b_ref[...] * s
        merged = (_sigmoid(ga_ref[...]) * attn_ref[...] + _sigmoid(gc_ref[...]) * conv).astype(BF)
        mg_ref[...] = merged
        xo_ref[...] = x_ref[...] + _nn(merged, w_ref[...])

    ecol = lambda cb: pl.BlockSpec((tm, d), lambda i: (i, cb))
    halo = lambda cb: pl.BlockSpec((8, d), lambda i: (jnp.maximum(i * hb - 1, 0), cb))
    row = pl.BlockSpec((tm, d), lambda i: (i, 0))
    return _call(
        body, name="mix_out_fwd", grid=(t // tm,),
        out_shape=(S((t, d), F32), S((t, d), BF)),
        in_specs=[ecol(0), ecol(1), ecol(2), ecol(3), ecol(4), halo(0), halo(2), row,
                  _resident(conv_w.shape), _resident(w_out.shape), row],
        out_specs=(row, row),
        sem=("parallel",), args=(e, e, e, e, e, e, e, attn, conv_w, w_out, x), hook=hook)


def _mem_kv(mem, gm, wkv):
    m, d = mem.shape

    def body(mem_ref, gm_ref, w_ref, mh_ref, kv_ref):
        n, _ = _rms(mem_ref[...])
        mh = (n * gm_ref[...]).astype(BF)
        mh_ref[...] = mh
        kv_ref[...] = _nn(mh, w_ref[...]).astype(BF)

    return pl.pallas_call(
        body, name="mem_kv", out_shape=(S((m, d), BF), S((m, wkv.shape[1]), BF)),
        compiler_params=_params(),
    )(mem, gm, wkv)


def _xattn_probs(qh, kh):
    s = _nt(qh, kh) * (kh.shape[1] ** -0.5)
    p = jnp.exp(s - jnp.max(s, axis=-1, keepdims=True))
    return p * (1.0 / jnp.sum(p, axis=-1, keepdims=True))


def _xattn_fwd(x, gn, wq, kv, wo):
    t, d = x.shape
    tm = 256
    hd = d // MEM_HEADS

    def body(x_ref, gn_ref, wq_ref, kv_ref, wo_ref, xo_ref, q_ref, o_ref):
        xv = x_ref[...]
        n, _ = _rms(xv)
        q = _nn((n * gn_ref[...]).astype(BF), wq_ref[...]).astype(BF)
        q_ref[...] = q
        outs = []
        for hh in range(MEM_HEADS):
            p = _xattn_probs(q[:, hh * hd:(hh + 1) * hd], kv_ref[:, hh * hd:(hh + 1) * hd])
            outs.append(_nn(p.astype(BF), kv_ref[:, d + hh * hd:d + (hh + 1) * hd]))
        o = jnp.concatenate(outs, axis=1).astype(BF)
        o_ref[...] = o
        xo_ref[...] = xv + _nn(o, wo_ref[...])

    row = pl.BlockSpec((tm, d), lambda i: (i, 0))
    return pl.pallas_call(
        body, name="xattn_fwd", grid=(t // tm,),
        out_shape=(S((t, d), F32), S((t, d), BF), S((t, d), BF)),
        in_specs=[row, _resident((1, d)), _resident(wq.shape), _resident(kv.shape), _resident(wo.shape)],
        out_specs=(row, row, row),
        compiler_params=_params(("parallel",)),
    )(x, gn, wq, kv, wo)


def _loss_bwd(x, gf, target):
    t, d = x.shape
    tm = 512

    def body(x_ref, gf_ref, t_ref, dx_ref, dg_ref, loss_ref):
        @pl.when(pl.program_id(0) == 0)
        def _():
            dg_ref[...] = jnp.zeros_like(dg_ref)
            loss_ref[...] = jnp.zeros_like(loss_ref)
        n, r = _rms(x_ref[...])
        g = gf_ref[...]
        err = n * g - t_ref[...]
        loss_ref[...] += 0.5 * jnp.sum(jnp.sum(err * err, axis=-1, keepdims=True) / d, axis=0, keepdims=True)
        dx, dg = _rms_bwd(err / d, n, r, g)
        dx_ref[...] = dx
        dg_ref[...] += dg

    row = pl.BlockSpec((tm, d), lambda i: (i, 0))
    return pl.pallas_call(
        body, name="loss_bwd", grid=(t // tm,),
        out_shape=(S((t, d), F32), S((1, d), F32), S((1, 128), F32)),
        in_specs=[row, _resident((1, d)), row],
        out_specs=(row, _acc_spec((1, d)), _acc_spec((1, 128))),
        compiler_params=_params(("arbitrary",)),
    )(x, gf, target)


def _ffn_bwd(dxo, x, gn, g, u, wgu, wd, name, hook=None):
    t, d = x.shape
    f = wd.shape[0]
    tm, fc = 256, 1408

    def body(dxo_ref, x_ref, gn_ref, g_ref, u_ref, wgu_ref, wd_ref, dx_ref, dgn_ref, dgu_ref, a_ref, h_ref, dyh_ref):
        @pl.when(pl.program_id(0) == 0)
        def _():
            dgn_ref[...] = jnp.zeros_like(dgn_ref)
        dxov = dxo_ref[...]
        dyh = (0.5 * dxov).astype(BF)
        dyh_ref[...] = dyh
        n, r = _rms(x_ref[...])
        gnv = gn_ref[...]
        h_ref[...] = (n * gnv).astype(BF)
        dh = jnp.zeros((tm, d), F32)
        for c0 in range(0, f, fc):
            gv = g_ref[:, c0:c0 + fc].astype(F32)
            uv = u_ref[:, c0:c0 + fc].astype(F32)
            da = _nt(dyh, wd_ref[c0:c0 + fc, :])
            sg = _sigmoid(gv)
            silu = gv * sg
            a_ref[:, c0:c0 + fc] = (silu * uv).astype(BF)
            dg = (da * uv * (sg * (1.0 + gv * (1.0 - sg)))).astype(BF)
            du = (da * silu).astype(BF)
            dgu_ref[:, c0:c0 + fc] = dg
            dgu_ref[:, f + c0:f + c0 + fc] = du
            dh = dh + _nt(dg, wgu_ref[:, c0:c0 + fc]) + _nt(du, wgu_ref[:, f + c0:f + c0 + fc])
        dx, dgn = _rms_bwd(dh, n, r, gnv)
        dx_ref[...] = dxov + dx
        dgn_ref[...] += dgn

    row = pl.BlockSpec((tm, d), lambda i: (i, 0))
    frow = pl.BlockSpec((tm, f), lambda i: (i, 0))
    return _call(
        body, name=name, grid=(t // tm,),
        out_shape=(S((t, d), F32), S((1, d), F32), S((t, 2 * f), BF), S((t, f), BF), S((t, d), BF), S((t, d), BF)),
        in_specs=[row, row, _resident((1, d)), frow, frow, _resident(wgu.shape), _resident(wd.shape)],
        out_specs=(row, _acc_spec((1, d)), pl.BlockSpec((tm, 2 * f), lambda i: (i, 0)), frow, row, row),
        sem=("arbitrary",), args=(dxo, x, gn, g, u, wgu, wd), hook=hook)


def _dw(a, b, tn, name, into=None, col_block=0, ncols=None):
    t, ka = a.shape
    nb = b.shape[1]
    tt = 1024
    nt = t // tt
    ncols = nb if ncols is None else ncols

    def body(*refs):
        a_ref, b_ref = refs[0], refs[1]
        o_ref, acc_ref = refs[-2], refs[-1]
        k = pl.program_id(1)

        @pl.when(k == 0)
        def _():
            acc_ref[...] = jnp.zeros_like(acc_ref)
        acc_ref[...] += _tn(a_ref[...], b_ref[...])

        @pl.when(k == nt - 1)
        def _():
            o_ref[...] = acc_ref[...].astype(BF)

    in_specs = [pl.BlockSpec((tt, ka), lambda j, k: (k, 0)), pl.BlockSpec((tt, tn), lambda j, k: (k, j))]
    args = [a, b]
    aliases = {}
    if into is not None:
        in_specs.append(ANY)
        args.append(into)
        aliases = {2: 0}
    return pl.pallas_call(
        body, name=name, grid=(nb // tn, nt), out_shape=S((ka, ncols), BF),
        in_specs=in_specs, out_specs=pl.BlockSpec((ka, tn), lambda j, k: (0, j + col_block)),
        scratch_shapes=[pltpu.VMEM((ka, tn), F32)], input_output_aliases=aliases,
        compiler_params=_params(("parallel", "arbitrary")),
    )(*args)


def _dw_pair(a, b, tn, name, kind, into=None, col_block=0, ncols=None, hook=None):
    t, ka = a.shape
    nb = b.shape[1]
    tt = 1024
    nt, nj = t // tt, nb // tn
    ncols = nb if ncols is None else ncols
    col = kind == "col"
    rh = ka // 2 if col else ka // 8
    tile = (rh, tn) if col else (4, rh, tn)
    half = (rh, ncols) if col else (4, rh, ncols)
    lead = (slice(None),) * (len(tile) - 1)

    def body(*refs):
        a_ref, b_ref = refs[0], refs[1]
        mine_ref, sib_ref, acc_ref, stage, ssem, rsem = refs[-6:]
        j, k = pl.program_id(0), pl.program_id(1)
        x, y, c = _place()
        sibling = (x, y, 1 - c)

        @pl.when(k == 0)
        def _():
            acc_ref[...] = jnp.zeros_like(acc_ref)
        acc_ref[...] += _tn(a_ref[...], b_ref[...])

        def send(slot, jj):
            dst = sib_ref.at[lead + (pl.ds(pl.multiple_of((jj + col_block) * tn, 128), tn),)]
            return _remote(stage.at[slot], dst, ssem.at[slot], rsem, sibling)

        def rows(s, whose):
            return acc_ref[pl.ds(pl.multiple_of(s * 2 * rh + whose * rh, 16), rh), :].astype(BF)

        @pl.when(k == nt - 1)
        def _():
            slot = j % 2

            @pl.when(j >= 2)
            def _():
                send(slot, j).wait_send()
            if col:
                mine_ref[...] = rows(0, c)
                stage[slot] = rows(0, 1 - c)
            else:
                for s in range(4):
                    mine_ref[s] = rows(s, c)
                    stage[slot, s] = rows(s, 1 - c)
            send(slot, j).start()

        @pl.when(jnp.logical_and(j == nj - 1, k == nt - 1))
        def _():
            for jj in range(max(nj - 2, 0), nj):
                send(jj % 2, jj).wait_send()
            got = sib_ref.at[lead + (pl.ds(col_block * tn, nb),)]
            _remote(got, got, ssem.at[0], rsem, sibling).wait_recv()

    in_specs = [pl.BlockSpec((tt, ka), lambda j, k: (k, 0)), pl.BlockSpec((tt, tn), lambda j, k: (k, j))]
    args = [a, b]
    aliases = {}
    if into is not None:
        in_specs += [ANY, ANY]
        args += list(into)
        aliases = {2: 0, 3: 1}
    mine_spec = pl.BlockSpec(tile, (lambda j, k: (0, j + col_block)) if col else (lambda j, k: (0, 0, j + col_block)))
    return _call(
        body, name=name, grid=(nj, nt), out_shape=(S(half, BF), S(half, BF)),
        in_specs=in_specs, out_specs=(mine_spec, ANY),
        scratch_shapes=[pltpu.VMEM((ka, tn), F32), pltpu.VMEM((2,) + tile, BF), pltpu.SemaphoreType.DMA((2,)),
                        pltpu.SemaphoreType.DMA],
        aliases=aliases, sem=("arbitrary", "arbitrary"), args=args, hook=hook)


def _xattn_bwd(dxo, x, gn, q, kv, wq, wo, hook=None):
    t, d = x.shape
    tm = 256
    hd = d // MEM_HEADS
    nkv = kv.shape[0]

    def body(dxo_ref, x_ref, gn_ref, q_ref, kv_ref, wq_ref, wo_ref, dx_ref, dgn_ref, dkv_ref, dxh_ref, h_ref, dq_ref):
        @pl.when(pl.program_id(0) == 0)
        def _():
            dgn_ref[...] = jnp.zeros_like(dgn_ref)
            dkv_ref[...] = jnp.zeros_like(dkv_ref)
        dxov = dxo_ref[...]
        dxh = dxov.astype(BF)
        dxh_ref[...] = dxh
        do = _nt(dxh, wo_ref[...]).astype(BF)
        dqs = []
        for hh in range(MEM_HEADS):
            lo, hi = hh * hd, (hh + 1) * hd
            qh = q_ref[:, lo:hi]
            kh = kv_ref[:, lo:hi]
            vh = kv_ref[:, d + lo:d + hi]
            doh = do[:, lo:hi]
            p = _xattn_probs(qh, kh)
            dp = _nt(doh, vh)
            ds = (p * (dp - jnp.sum(p * dp, axis=-1, keepdims=True)) * (hd ** -0.5)).astype(BF)
            dqs.append(_nn(ds, kh))
            dkv_ref[:, lo:hi] += _tn(ds, qh)
            dkv_ref[:, d + lo:d + hi] += _tn(p.astype(BF), doh)
        dq = jnp.concatenate(dqs, axis=1).astype(BF)
        dq_ref[...] = dq
        n, r = _rms(x_ref[...])
        gnv = gn_ref[...]
        h_ref[...] = (n * gnv).astype(BF)
        dx, dgn = _rms_bwd(_nt(dq, wq_ref[...]), n, r, gnv)
        dx_ref[...] = dxov + dx
        dgn_ref[...] += dgn

    row = pl.BlockSpec((tm, d), lambda i: (i, 0))
    return _call(
        body, name="xattn_bwd", grid=(t // tm,),
        out_shape=(S((t, d), F32), S((1, d), F32), S((nkv, 2 * d), F32), S((t, d), BF), S((t, d), BF), S((t, d), BF)),
        in_specs=[row, row, _resident((1, d)), row, _resident(kv.shape), _resident(wq.shape), _resident(wo.shape)],
        out_specs=(row, _acc_spec((1, d)), _acc_spec((nkv, 2 * d)), row, row, row),
        sem=("arbitrary",), args=(dxo, x, gn, q, kv, wq, wo), hook=hook)


def _mem_bwd(dkv, mh, mem, gm, wkv, pair=False):
    m, d = mem.shape
    rows, cols = wkv.shape
    rh = rows // 2

    def body(dkv_ref, mh_ref, mem_ref, gm_ref, w_ref, *outs):
        dkvb = dkv_ref[...].astype(BF)
        dw = _tn(mh_ref[...], dkvb).astype(BF)
        dmh = _nt(dkvb, w_ref[...])
        n, _ = _rms(mem_ref[...])
        if not pair:
            dw_ref, dgm_ref = outs
            dw_ref[...] = dw
        else:
            mine_ref, sib_ref, dgm_ref, whole, ssem, rsem = outs
            x, y, c = _place()
            whole[...] = dw
            cp = _remote(whole.at[pl.ds(pl.multiple_of((1 - c) * rh, 16), rh), :], sib_ref, ssem, rsem, (x, y, 1 - c))
            cp.start()
            mine_ref[...] = whole[pl.ds(pl.multiple_of(c * rh, 16), rh), :]
            cp.wait()
        dgm_ref[...] = jnp.sum(dmh * n, axis=0, keepdims=True)

    if not pair:
        return pl.pallas_call(body, name="mem_bwd", out_shape=(S(wkv.shape, BF), S((1, d), F32)),
                              compiler_params=_params())(dkv, mh, mem, gm, wkv)
    mine, sib, dgm = pl.pallas_call(
        body, name="mem_bwd", out_shape=(S((rh, cols), BF), S((rh, cols), BF), S((1, d), F32)),
        in_specs=[VMEM_SPEC] * 5, out_specs=(VMEM_SPEC, ANY, VMEM_SPEC),
        scratch_shapes=[pltpu.VMEM((rows, cols), BF), pltpu.SemaphoreType.DMA, pltpu.SemaphoreType.DMA],
        compiler_params=_params())(dkv, mh, mem, gm, wkv)
    return (mine, sib), dgm


def _mix_out_bwd(dxo, e, attn, conv_w, w_out, hook=None):
    t, d = attn.shape
    tm = 256
    hb = tm // 8
    nt = t // tm
    last8 = t // 8 - 1

    def body(dxo_ref, dxn_ref, c_ref, b_ref, u_ref, ga_ref, gc_ref, ch_ref, uh_ref, bn_ref, gcn_ref,
             attn_ref, cw_ref, w_ref, dattn_ref, de_ref, dcw_ref, dxh_ref):
        i = pl.program_id(0)

        @pl.when(i == 0)
        def _():
            dcw_ref[...] = jnp.zeros_like(dcw_ref)
        dxh = dxo_ref[...].astype(BF)
        dxh_ref[...] = dxh
        w = w_ref[...]
        dm = _nt(dxh, w)
        dmn = _nt(dxn_ref[...].astype(BF), w)
        cv, bv, uv = c_ref[...], b_ref[...], u_ref[...]
        sga = _sigmoid(ga_ref[...])
        sgc = _sigmoid(gc_ref[...])
        z = cv * uv
        z1, z2 = _conv_taps(z, ch_ref[...] * uh_ref[...], i == 0)
        w0, w1, w2 = cw_ref[0:1, :], cw_ref[1:2, :], cw_ref[2:3, :]
        s = w0 * z2 + w1 * z1 + w2 * z
        av = attn_ref[...]
        dattn_ref[...] = (dm * sga).astype(BF)
        dconv = dm * sgc
        ds = dconv * bv
        dsn = jnp.where(i == nt - 1, 0.0, dmn * _sigmoid(gcn_ref[...]) * bn_ref[...])
        row = lax.broadcasted_iota(I32, (tm, 1), 0)
        dsp1 = jnp.where(row == tm - 1, dsn[0:1, :], pltpu.roll(ds, tm - 1, 0))
        dsp2 = jnp.where(row == tm - 2, dsn[0:1, :], jnp.where(row == tm - 1, dsn[1:2, :], pltpu.roll(ds, tm - 2, 0)))
        dz = w2 * ds + w1 * dsp1 + w0 * dsp2
        de_ref[:, 0:d] = (dz * uv).astype(BF)
        de_ref[:, d:2 * d] = (dconv * s).astype(BF)
        de_ref[:, 2 * d:3 * d] = (dz * cv).astype(BF)
        de_ref[:, 3 * d:4 * d] = (dm * av * sga * (1.0 - sga)).astype(BF)
        de_ref[:, 4 * d:5 * d] = (dm * (bv * s) * sgc * (1.0 - sgc)).astype(BF)
        dcw_ref[0:1, :] += jnp.sum(ds * z2, axis=0, keepdims=True)
        dcw_ref[1:2, :] += jnp.sum(ds * z1, axis=0, keepdims=True)
        dcw_ref[2:3, :] += jnp.sum(ds * z, axis=0, keepdims=True)

    ecol = lambda cb: pl.BlockSpec((tm, d), lambda i: (i, cb))
    prev = lambda cb: pl.BlockSpec((8, d), lambda i: (jnp.maximum(i * hb - 1, 0), cb))
    nxt = lambda cb: pl.BlockSpec((8, d), lambda i: (jnp.minimum((i + 1) * hb, last8), cb))
    row = pl.BlockSpec((tm, d), lambda i: (i, 0))
    return _call(
        body, name="mix_out_bwd", grid=(nt,),
        out_shape=(S((t, d), BF), S((t, 5 * d), BF), S((8, d), F32), S((t, d), BF)),
        in_specs=[row, nxt(0), ecol(0), ecol(1), ecol(2), ecol(3), ecol(4), prev(0), prev(2), nxt(1), nxt(4),
                  row, _resident(conv_w.shape), _resident(w_out.shape)],
        out_specs=(row, pl.BlockSpec((tm, 5 * d), lambda i: (i, 0)), _acc_spec((8, d)), row),
        sem=("arbitrary",), args=(dxo, dxo, e, e, e, e, e, e, e, e, e, attn, conv_w, w_out), hook=hook)


def _swa_bwd(qkv, dattn, pos_col, pos_row, bias_t, sinks, hook=None):
    t = qkv.shape[0]
    nb = t // BLOCK
    qw = SWA_HEADS * HEAD_DIM
    kw = SWA_KV_HEADS * HEAD_DIM

    def body(q_ref, kp_ref, kc_ref, vp_ref, vc_ref, do_ref, pq_ref, pkp_ref, pkc_ref, bias_ref, sink_ref,
             dq_ref, dkv_ref, gb_ref, dsk_ref, carry_ref):
        b = pl.program_id(0)

        @pl.when(b == 0)
        def _():
            gb_ref[...] = jnp.zeros_like(gb_ref)
            dsk_ref[...] = jnp.zeros_like(dsk_ref)
            carry_ref[...] = jnp.zeros_like(carry_ref)

        @pl.when(b < nb)
        def _():
            vis = _swa_visible(b, pq_ref, pkp_ref, pkc_ref)
            k2 = jnp.concatenate([kp_ref[...], kc_ref[...]], axis=0)
            v2 = jnp.concatenate([vp_ref[...], vc_ref[...]], axis=0)
            for hk in range(SWA_KV_HEADS):
                lo, hi = hk * HEAD_DIM, (hk + 1) * HEAD_DIM
                kh = k2[:, lo:hi]
                vh = v2[:, lo:hi]
                qg = _group_heads(q_ref, hk)
                dog = _group_heads(do_ref, hk)
                pn, psn = _swa_probs(qg, kh, vis, bias_ref[hk], _group_sinks(sink_ref, hk))
                dp = _nt(dog, vh)
                delta = jnp.sum(pn * dp, axis=-1, keepdims=True)
                ds = pn * (dp - delta)
                gb_ref[hk] += ds
                dsk_ref[hk] += -psn * delta
                dsb = (ds * (HEAD_DIM ** -0.5)).astype(BF)
                dqg = _nn(dsb, kh).astype(BF)
                for g in range(SWA_GROUP):
                    h = hk * SWA_GROUP + g
                    dq_ref[:, h * HEAD_DIM:(h + 1) * HEAD_DIM] = dqg[g * BLOCK:(g + 1) * BLOCK]
                dk = _tn(dsb, qg)
                dv = _tn(pn.astype(BF), dog)
                dkv_ref[:, lo:hi] = (carry_ref[:, lo:hi] + dk[0:BLOCK]).astype(BF)
                dkv_ref[:, kw + lo:kw + hi] = (carry_ref[:, kw + lo:kw + hi] + dv[0:BLOCK]).astype(BF)
                carry_ref[:, lo:hi] = dk[BLOCK:2 * BLOCK]
                carry_ref[:, kw + lo:kw + hi] = dv[BLOCK:2 * BLOCK]

        @pl.when(b == nb)
        def _():
            dkv_ref[...] = carry_ref[...].astype(BF)

    cur = lambda b: jnp.minimum(b, nb - 1)
    prev = lambda b: jnp.maximum(cur(b) - 1, 0)
    return _call(
        body, name="swa_bwd", grid=(nb + 1,),
        out_shape=(S((t, qw), BF), S((t, 2 * kw), BF), S((SWA_KV_HEADS, GROUP_ROWS, 2 * BLOCK), F32),
                   S((SWA_KV_HEADS, GROUP_ROWS, 1), F32)),
        in_specs=[
            pl.BlockSpec((BLOCK, qw), lambda b: (cur(b), 0)),
            pl.BlockSpec((BLOCK, kw), lambda b: (prev(b), qw // kw)),
            pl.BlockSpec((BLOCK, kw), lambda b: (cur(b), qw // kw)),
            pl.BlockSpec((BLOCK, kw), lambda b: (prev(b), qw // kw + 1)),
            pl.BlockSpec((BLOCK, kw), lambda b: (cur(b), qw // kw + 1)),
            pl.BlockSpec((BLOCK, qw), lambda b: (cur(b), 0)),
            pl.BlockSpec((BLOCK, 1), lambda b: (cur(b), 0)),
            pl.BlockSpec((1, BLOCK), lambda b: (0, prev(b))),
            pl.BlockSpec((1, BLOCK), lambda b: (0, cur(b))),
            _resident(bias_t.shape),
            SMEM_SPEC,
        ],
        out_specs=(
            pl.BlockSpec((BLOCK, qw), lambda b: (cur(b), 0)),
            pl.BlockSpec((BLOCK, 2 * kw), lambda b: (jnp.maximum(b - 1, 0), 0)),
            _acc_spec((SWA_KV_HEADS, GROUP_ROWS, 2 * BLOCK)),
            _acc_spec((SWA_KV_HEADS, GROUP_ROWS, 1)),
        ),
        scratch_shapes=[pltpu.VMEM((BLOCK, 2 * kw), F32)],
        sem=("arbitrary",), args=(qkv, qkv, qkv, qkv, qkv, dattn, pos_col, pos_row, pos_row, bias_t, sinks), hook=hook)


def _bias_reduce(gb, dsk):
    def body(gb_ref, dsk_ref, drb_ref, dsink_ref):
        bucket = _t5_bucket(_block_rel())
        for b in range(REL_BUCKETS):
            mask = bucket == b
            for h in range(SWA_HEADS):
                drb_ref[b, h] = jnp.sum(jnp.where(mask, gb_ref[h], 0.0))
        for h in range(SWA_HEADS):
            dsink_ref[0, h] = jnp.sum(dsk_ref[h])

    return pl.pallas_call(
        body, name="bias_reduce", out_shape=(S((REL_BUCKETS, SWA_HEADS), F32), S((1, SWA_HEADS), F32)),
        in_specs=[VMEM_SPEC, VMEM_SPEC], out_specs=(SMEM_SPEC, SMEM_SPEC),
    )(gb, dsk)


def _mix_in_bwd(dq, dkv, de, x, gn, w_in, dxo):
    t, d = x.shape
    tm = 256
    nq, nkv, ne = dq.shape[1], dkv.shape[1], de.shape[1]

    def body(dq_ref, dkv_ref, de_ref, x_ref, gn_ref, w_ref, dxo_ref, dx_ref, dgn_ref, h_ref):
        @pl.when(pl.program_id(0) == 0)
        def _():
            dgn_ref[...] = jnp.zeros_like(dgn_ref)
        dh = _nt(dq_ref[...], w_ref[:, 0:nq]) + _nt(dkv_ref[...], w_ref[:, nq:nq + nkv])
        for c0 in range(0, ne, 1024):
            dh = dh + _nt(de_ref[:, c0:c0 + 1024], w_ref[:, nq + nkv + c0:nq + nkv + c0 + 1024])
        n, r = _rms(x_ref[...])
        gnv = gn_ref[...]
        h_ref[...] = (n * gnv).astype(BF)
        dx, dgn = _rms_bwd(dh, n, r, gnv)
        dx_ref[...] = dxo_ref[...] + dx
        dgn_ref[...] += dgn

    row = pl.BlockSpec((tm, d), lambda i: (i, 0))
    wide = lambda w: pl.BlockSpec((tm, w), lambda i: (i, 0))
    return pl.pallas_call(
        body, name="mix_in_bwd", grid=(t // tm,),
        out_shape=(S((t, d), F32), S((1, d), F32), S((t, d), BF)),
        in_specs=[wide(nq), wide(nkv), wide(ne), row, _resident((1, d)), _resident(w_in.shape), row],
        out_specs=(row, _acc_spec((1, d)), row),
        compiler_params=_params(("arbitrary",)),
    )(dq, dkv, de, x, gn, w_in, dxo)


def _row_tile(rows):
    for tr in (256, 352, 128, 64, 16, 8):
        if rows % tr == 0:
            return tr
    return rows


def _add_bf16(a, b, name):
    rows, cols = a.shape
    tr = _row_tile(rows)

    def body(a_ref, b_ref, o_ref):
        o_ref[...] = (a_ref[...].astype(F32) + b_ref[...].astype(F32)).astype(BF)

    blk = pl.BlockSpec((tr, cols), lambda i: (i, 0))
    return pl.pallas_call(body, name=name, grid=(rows // tr,), out_shape=S((rows, cols), BF),
                          in_specs=[blk, blk], out_specs=blk, compiler_params=_params(("parallel",)))(a, b)


def _adamw(w, g, m, v, name):
    rows, cols = w.shape
    tr = _row_tile(rows)

    def body(w_ref, g_ref, m_ref, v_ref, d_ref, mo_ref, vo_ref):
        gv = g_ref[...]
        mn = ADAM_B1 * m_ref[...] + (1.0 - ADAM_B1) * gv
        vn = ADAM_B2 * v_ref[...] + (1.0 - ADAM_B2) * (gv * gv)
        m_hat = mn / (1.0 - ADAM_B1 ** ADAM_STEP)
        v_hat = vn / (1.0 - ADAM_B2 ** ADAM_STEP)
        d_ref[...] = -ADAM_LR * (m_hat / (jnp.sqrt(v_hat) + ADAM_EPS) + ADAM_WD * w_ref[...])
        mo_ref[...] = mn
        vo_ref[...] = vn

    blk = pl.BlockSpec((tr, cols), lambda i: (i, 0))
    shp = S((rows, cols), F32)
    return pl.pallas_call(body, name=name, grid=(rows // tr,), out_shape=(shp, shp, shp),
                          in_specs=[blk] * 4, out_specs=(blk,) * 3, compiler_params=_params(("parallel",)))(w, g, m, v)


def _place():
    x, y, c = lax.axis_index("x"), lax.axis_index("y"), lax.axis_index("c")
    return x, y, c


OTHER_CHIPS = ((1, 0), (0, 1), (1, 1))


def _flip(v, f):
    return 1 - v if f else v


def _remote(src, dst, ssem, rsem, dev):
    return pltpu.make_async_remote_copy(src_ref=src, dst_ref=dst, send_sem=ssem, recv_sem=rsem,
                                        device_id=dev, device_id_type=MESH)


def _exchange(hook, name):
    n_in, n_out = len(hook.ins), len(hook.out_shape)

    def body(*refs):
        ins, outs, scr = refs[:n_in], refs[n_in:n_in + n_out], refs[n_in + n_out:]
        hook.start(ins, outs, scr)
        hook.finish(ins, outs, scr)

    return pl.pallas_call(
        body, name=name, out_shape=tuple(hook.out_shape), in_specs=[ANY] * n_in, out_specs=(ANY,) * n_out,
        scratch_shapes=list(hook.scratch), compiler_params=_params(),
    )(*hook.ins)


class _GatherHook:
    def __init__(self, shards, kinds):
        self.ins, self.kinds, n = list(shards), list(kinds), len(shards)
        self.out_shape = tuple(
            S((w.shape[0], 4 * w.shape[1]), BF) if k == "col" else S((4,) + w.shape, BF) for w, k in zip(shards, kinds))
        dma = pltpu.SemaphoreType.DMA
        self.scratch = ([dma((n, 3)) for _ in range(5)] + [dma((n,)), dma((n,))]
                        + [pltpu.VMEM((3, w.shape[0] // 2, w.shape[1]), BF) for w in shards]
                        + [pltpu.VMEM(w.shape, BF) for w in shards])

    def _window(self, outs, i, s, half):
        rows, cols = self.ins[i].shape
        rh = rows // 2
        start = pl.multiple_of(half * rh, 16)
        if self.kinds[i] == "col":
            return outs[i].at[pl.ds(start, rh), pl.ds(pl.multiple_of(s * cols, 128), cols)]
        return outs[i].at[s, pl.ds(start, rh), :]

    def _copies(self, ins, outs, scr):
        n = len(ins)
        ssem, rsem, fssem, frsem, ksem, lsem, osem = scr[:7]
        land, own = scr[7:7 + n], scr[7 + n:7 + 2 * n]
        x, y, c = _place()
        sibling = (x, y, 1 - c)
        loads, stores, sends, forwards, keeps, passed = [], [], [], [], [], []
        for i in range(n):
            rows, cols = self.ins[i].shape
            rh = rows // 2
            mine = (outs[i].at[:, pl.ds(pl.multiple_of((2 * x + y) * cols, 128), cols)] if self.kinds[i] == "col"
                    else outs[i].at[2 * x + y])
            loads.append(functools.partial(pltpu.make_async_copy, ins[i], own[i], lsem.at[i]))
            stores.append(functools.partial(pltpu.make_async_copy, own[i], mine, osem.at[i]))
            src = ins[i].at[pl.ds(pl.multiple_of(c * rh, 16), rh), :]
            for j, (fx, fy) in enumerate(OTHER_CHIPS):
                px, py = _flip(x, fx), _flip(y, fy)
                sends.append(functools.partial(_remote, src, land[i].at[j], ssem.at[i, j], rsem.at[i, j], (px, py, c)))
                here = self._window(outs, i, 2 * px + py, c)
                forwards.append(functools.partial(_remote, land[i].at[j], here, fssem.at[i, j], frsem.at[i, j], sibling))
                keeps.append(functools.partial(pltpu.make_async_copy, land[i].at[j], here, ksem.at[i, j]))
                there = self._window(outs, i, 2 * px + py, 1 - c)
                passed.append(functools.partial(_remote, there, there, fssem.at[i, j], frsem.at[i, j], sibling))
        return loads, stores, sends, forwards, keeps, passed

    def start(self, ins, outs, scr):
        loads, _, sends, _, _, _ = self._copies(ins, outs, scr)
        for make in sends + loads:
            make().start()

    def finish(self, ins, outs, scr):
        loads, stores, sends, forwards, keeps, passed = self._copies(ins, outs, scr)
        for load, store in zip(loads, stores):
            load().wait()
            store().start()
        for send, forward, keep in zip(sends, forwards, keeps):
            send().wait_recv()
            forward().start()
            keep().start()
        for make in passed:
            make().wait_recv()
        for make in sends + forwards:
            make().wait_send()
        for make in keeps + stores:
            make().wait()


class _ChipsHook:
    def __init__(self, parts, kinds):
        self.ins, self.kinds, n = list(parts), list(kinds), len(parts)
        self.out_shape = tuple(
            S((4, p.shape[0], p.shape[1] // 4), BF) if k == "col" else S(p.shape, BF) for p, k in zip(parts, kinds))
        dma = pltpu.SemaphoreType.DMA
        self.scratch = ([dma((n, 3)), dma((n, 3)), dma((n,)), dma((n,))]
                        + [pltpu.VMEM(o.shape[1:], BF) for o in self.out_shape])

    def _slab(self, ins, i, s):
        _, rows, cols = self.out_shape[i].shape
        if self.kinds[i] == "col":
            return ins[i].at[:, pl.ds(pl.multiple_of(s * cols, 128), cols)]
        return ins[i].at[s]

    def _copies(self, ins, outs, scr):
        ssem, rsem, lsem, osem = scr[:4]
        own = scr[4:]
        x, y, c = _place()
        loads, stores, sends = [], [], []
        for i in range(len(ins)):
            loads.append(functools.partial(pltpu.make_async_copy, self._slab(ins, i, 2 * x + y), own[i], lsem.at[i]))
            stores.append(functools.partial(pltpu.make_async_copy, own[i], outs[i].at[3], osem.at[i]))
            for j, (fx, fy) in enumerate(OTHER_CHIPS):
                px, py = _flip(x, fx), _flip(y, fy)
                sends.append(functools.partial(_remote, self._slab(ins, i, 2 * px + py), outs[i].at[j], ssem.at[i, j],
                                               rsem.at[i, j], (px, py, c)))
        return loads, stores, sends

    def start(self, ins, outs, scr):
        loads, _, sends = self._copies(ins, outs, scr)
        for make in sends + loads:
            make().start()

    def finish(self, ins, outs, scr):
        loads, stores, sends = self._copies(ins, outs, scr)
        for load, store in zip(loads, stores):
            load().wait()
            store().start()
        for make in sends + stores:
            make().wait()


def _sum_share(slabs, name):
    _, rh, cols = slabs.shape
    tr = _row_tile(rh)
    nt = rh // tr

    def body(p_ref, o_ref, stage, lsem, ssem, rsem):
        i = pl.program_id(0)
        x, y, c = _place()
        sibling = (x, y, 1 - c)
        acc = p_ref[3].astype(F32)
        for k in range(3):
            acc = acc + p_ref[k].astype(F32)
        stage[i] = acc

        def copies(k):
            dst = o_ref.at[pl.ds(pl.multiple_of(c * rh + k * tr, 8), tr), :]
            return (pltpu.make_async_copy(stage.at[k], dst, lsem.at[k]),
                    _remote(stage.at[k], dst, ssem.at[k], rsem, sibling))

        for cp in copies(i):
            cp.start()

        @pl.when(i == nt - 1)
        def _():
            for k in range(nt):
                local, remote = copies(k)
                local.wait()
                remote.wait_send()
            got = o_ref.at[pl.ds(pl.multiple_of((1 - c) * rh, 8), rh), :]
            _remote(got, got, ssem.at[0], rsem, sibling).wait_recv()

    return pl.pallas_call(
        body, name=name, grid=(nt,), out_shape=S((2 * rh, cols), F32),
        in_specs=[pl.BlockSpec((4, tr, cols), lambda i: (0, i, 0))], out_specs=ANY,
        scratch_shapes=[pltpu.VMEM((nt, tr, cols), F32), pltpu.SemaphoreType.DMA((nt,)), pltpu.SemaphoreType.DMA((nt,)),
                        pltpu.SemaphoreType.DMA],
        compiler_params=_params(("arbitrary",)))(slabs)


def _all_reduce_small(buf, name):
    shape = buf.shape

    def body(in_ref, out_ref, slots, ssem, rsem):
        x, y, c = _place()
        me = 4 * x + 2 * y + c
        slots[me] = in_ref[...]
        sends = []
        for r in range(1, 8):
            fx, fy, fc = (r >> 2) & 1, (r >> 1) & 1, r & 1
            cp = _remote(in_ref, slots.at[me], ssem.at[r - 1], rsem.at[r - 1], (_flip(x, fx), _flip(y, fy), _flip(c, fc)))
            cp.start()
            sends.append(cp)
        for r in range(1, 8):
            fx, fy, fc = (r >> 2) & 1, (r >> 1) & 1, r & 1
            px, py, pc = _flip(x, fx), _flip(y, fy), _flip(c, fc)
            _remote(in_ref, slots.at[4 * px + 2 * py + pc], ssem.at[r - 1], rsem.at[r - 1], (px, py, pc)).wait_recv()
        for cp in sends:
            cp.wait_send()
        acc = slots[0]
        for k in range(1, 8):
            acc = acc + slots[k]
        out_ref[...] = acc

    return pl.pallas_call(
        body, name=name, out_shape=S(shape, F32), in_specs=[VMEM_SPEC], out_specs=VMEM_SPEC,
        scratch_shapes=[pltpu.VMEM((8,) + shape, F32), pltpu.SemaphoreType.DMA((7,)), pltpu.SemaphoreType.DMA((7,))],
    )(buf)


BIG = ("ffn1_w_gu", "ffn1_w_down", "w_in", "w_out", "xattn_wq", "xattn_wkv", "xattn_wo", "ffn2_w_gu", "ffn2_w_down")
KIND = {"ffn1_w_gu": "col", "ffn1_w_down": "row", "w_in": "col", "w_out": "row", "xattn_wq": "row",
        "xattn_wkv": "col", "xattn_wo": "row", "ffn2_w_gu": "col", "ffn2_w_down": "row"}
WEIGHTS = ("rel_bias", "ffn1_norm", "ffn1_w_gu", "ffn1_w_down", "mix_norm", "w_in", "sinks", "conv_w", "w_out",
           "xattn_norm", "mem_norm", "xattn_wq", "xattn_wkv", "xattn_wo", "ffn2_norm", "ffn2_w_gu", "ffn2_w_down",
           "final_norm")
SMALL_ROWS = 16
GAIN_ROW = {"ffn1_norm": 0, "mix_norm": 1, "xattn_norm": 2, "mem_norm": 3, "ffn2_norm": 4, "final_norm": 5}
CONV_ROW, SINK_ROW, BIAS_ROW, LOSS_ROW = 6, 9, 10, 11


def _rows_block(rows, d):
    buf = jnp.zeros((SMALL_ROWS, d), F32)
    for r, v in rows.items():
        buf = lax.dynamic_update_slice(buf, v.reshape(1, -1).astype(F32), (r, 0))
    return buf


def _local_step(x, mem, pos, target, w, gains, rel_bias, sinks, conv_w, shards=None):
    t, d = x.shape
    dist = shards is not None
    w = dict(w)
    grads, slabs = {}, {}
    pos_col = pos.reshape(t, 1)
    pos_row = pos.reshape(1, t)
    bias_t = _bias_build(rel_bias).reshape(SWA_KV_HEADS, GROUP_ROWS, 2 * BLOCK)

    def gather(names):
        return _GatherHook([shards[k] for k in names], [KIND[k] for k in names]) if dist else None

    def gathered(names, hook):
        if dist:
            for k, gw in zip(names, hook.results):
                w[k] = gw if KIND[k] == "col" else gw.reshape(-1, gw.shape[-1])

    def dw(k, a, b, tn, name, **kw):
        return _dw_pair(a, b, tn, name, KIND[k], **kw) if dist else _dw(a, b, tn, name, **kw)

    def pair_sums(names):
        if not dist:
            return None
        parts = {}
        for k in names:
            mine, sib = grads[k]
            if KIND[k] == "row":
                mine, sib = mine.reshape(-1, mine.shape[-1]), sib.reshape(-1, sib.shape[-1])
            p = _add_bf16(mine, sib, "pair_sum_" + k)
            parts[k] = p if KIND[k] == "col" else p.reshape(4, -1, p.shape[-1])
        return parts

    def chips(names, parts):
        return _ChipsHook([parts[k] for k in names], [KIND[k] for k in names]) if dist else None

    def reduced(names, hook):
        if dist:
            slabs.update(zip(names, hook.results))

    names = ("w_in", "w_out")
    hook = gather(names)
    x1, g1, u1 = _ffn_fwd(x, gains["ffn1_norm"], w["ffn1_w_gu"], w["ffn1_w_down"], "ffn1_fwd", hook)
    gathered(names, hook)
    names = ("xattn_wq", "xattn_wkv", "xattn_wo")
    hook = gather(names)
    qkv, e = _mix_proj(x1, gains["mix_norm"], w["w_in"], hook)
    gathered(names, hook)
    names = ("ffn2_w_gu",)
    hook = gather(names)
    attn = _swa_fwd(qkv, pos_col, pos_row, bias_t, sinks, hook)
    gathered(names, hook)
    names = ("ffn2_w_down",)
    hook = gather(names)
    x2, merged = _mix_out_fwd(e, attn, conv_w, w["w_out"], x1, hook)
    gathered(names, hook)
    mh, kv = _mem_kv(mem, gains["mem_norm"], w["xattn_wkv"])
    x3, qx, o = _xattn_fwd(x2, gains["xattn_norm"], w["xattn_wq"], kv, w["xattn_wo"])
    x4, g2, u2 = _ffn_fwd(x3, gains["ffn2_norm"], w["ffn2_w_gu"], w["ffn2_w_down"], "ffn2_fwd")
    dx4, d_final, loss = _loss_bwd(x4, gains["final_norm"], target)

    dx3, d_ffn2, dgu2, a2, h4, dyh4 = _ffn_bwd(dx4, x3, gains["ffn2_norm"], g2, u2, w["ffn2_w_gu"], w["ffn2_w_down"],
                                               "ffn2_bwd")
    grads["ffn2_w_gu"] = dw("ffn2_w_gu", h4, dgu2, 1408, "dw_ffn2_gu")
    grads["ffn2_w_down"] = dw("ffn2_w_down", a2, dyh4, 512, "dw_ffn2_down")
    parts = pair_sums(("ffn2_w_gu", "ffn2_w_down"))
    hook = chips(("ffn2_w_gu",), parts)
    dx2, d_xattn, dkv, dxh3, h3, dqx = _xattn_bwd(dx3, x2, gains["xattn_norm"], qx, kv, w["xattn_wq"], w["xattn_wo"], hook)
    reduced(("ffn2_w_gu",), hook)
    grads["xattn_wo"] = dw("xattn_wo", o, dxh3, 1024, "dw_wo")
    grads["xattn_wq"] = dw("xattn_wq", h3, dqx, 1024, "dw_wq")
    grads["xattn_wkv"], d_mem = _mem_bwd(dkv, mh, mem, gains["mem_norm"], w["xattn_wkv"], dist)
    hook = chips(("ffn2_w_down",), parts)
    dattn, de, dcw, dxh2 = _mix_out_bwd(dx2, e, attn, conv_w, w["w_out"], hook)
    reduced(("ffn2_w_down",), hook)
    grads["w_out"] = dw("w_out", merged, dxh2, 1024, "dw_wout")
    names = ("xattn_wo", "xattn_wq", "xattn_wkv", "w_out")
    hook = chips(names, pair_sums(names))
    dq, dkvs, gb, dsk = _swa_bwd(qkv, dattn, pos_col, pos_row, bias_t, sinks, hook)
    reduced(names, hook)
    d_rel_bias, d_sinks = _bias_reduce(gb.reshape(SWA_HEADS, BLOCK, 2 * BLOCK), dsk.reshape(SWA_HEADS, BLOCK, 1))
    dx1, d_mix, h2 = _mix_in_bwd(dq, dkvs, de, x1, gains["mix_norm"], w["w_in"], dx2)
    n_in = w["w_in"].shape[1]
    gw = dw("w_in", h2, dq, 512, "dw_win_q", ncols=n_in)
    gw = dw("w_in", h2, dkvs, 512, "dw_win_kv", into=gw, col_block=dq.shape[1] // 512, ncols=n_in)
    grads["w_in"] = dw("w_in", h2, de, 512, "dw_win_e", into=gw, col_block=(dq.shape[1] + dkvs.shape[1]) // 512,
                       ncols=n_in)
    names = ("w_in",)
    hook = chips(names, pair_sums(names))
    dx0, d_ffn1, dgu1, a1, h1, dyh1 = _ffn_bwd(dx1, x, gains["ffn1_norm"], g1, u1, w["ffn1_w_gu"], w["ffn1_w_down"],
                                               "ffn1_bwd", hook)
    reduced(names, hook)
    grads["ffn1_w_down"] = dw("ffn1_w_down", a1, dyh1, 512, "dw_ffn1_down")
    names = ("ffn1_w_down",)
    hook = chips(names, pair_sums(names))
    grads["ffn1_w_gu"] = dw("ffn1_w_gu", h1, dgu1, 1408, "dw_ffn1_gu", **({"hook": hook} if dist else {}))
    reduced(names, hook)
    names = ("ffn1_w_gu",)
    hook = chips(names, pair_sums(names))
    if dist:
        hook.results = _exchange(hook, "reduce_chips_ffn1_gu")
    reduced(names, hook)

    rows = {0: d_ffn1, 1: d_mix, 2: d_xattn, 3: d_mem, 4: d_ffn2, 5: d_final, SINK_ROW: d_sinks, BIAS_ROW: d_rel_bias,
            LOSS_ROW: loss[0, 0:1]}
    rows.update({CONV_ROW + j: dcw[j] for j in range(3)})
    return dx0, (slabs if dist else grads), _rows_block(rows, d)


def kernel(x, mem, positions, rel_bias, ffn1_norm, ffn1_w_gu, ffn1_w_down, mix_norm, w_in, sinks, conv_w, w_out, xattn_norm, mem_norm, xattn_wq, xattn_wkv, xattn_wo, ffn2_norm, ffn2_w_gu, ffn2_w_down, final_norm, loss_target, m_rel_bias, m_ffn1_norm, m_ffn1_w_gu, m_ffn1_w_down, m_mix_norm, m_w_in, m_sinks, m_conv_w, m_w_out, m_xattn_norm, m_mem_norm, m_xattn_wq, m_xattn_wkv, m_xattn_wo, m_ffn2_norm, m_ffn2_w_gu, m_ffn2_w_down, m_final_norm, v_rel_bias, v_ffn1_norm, v_ffn1_w_gu, v_ffn1_w_down, v_mix_norm, v_w_in, v_sinks, v_conv_w, v_w_out, v_xattn_norm, v_mem_norm, v_xattn_wq, v_xattn_wkv, v_xattn_wo, v_ffn2_norm, v_ffn2_w_gu, v_ffn2_w_down, v_final_norm):
    args = dict(locals())
    wts = {k: args[k] for k in WEIGHTS}
    mom = {k: args["m_" + k] for k in WEIGHTS}
    var = {k: args["v_" + k] for k in WEIGHTS}
    d = x.shape[-1]
    s_me = 2 * lax.axis_index("x") + lax.axis_index("y")

    shards = {k: wts[k][0].astype(BF) for k in BIG}
    first = ("ffn1_w_gu", "ffn1_w_down")
    gathered = _exchange(_GatherHook([shards[k] for k in first], [KIND[k] for k in first]), "gather_ffn1")
    whole = {k: (gw if KIND[k] == "col" else gw.reshape(-1, gw.shape[-1])) for k, gw in zip(first, gathered)}
    cw_cols = conv_w.shape[-1]
    placed = lax.dynamic_update_slice(jnp.zeros((SMALL_ROWS, d), F32), 0.5 * conv_w[0], (0, s_me * cw_cols))
    conv_whole = _all_reduce_small(placed, "gather_conv_w")[0:3]

    gains = {k: wts[k].reshape(1, d) for k in GAIN_ROW}
    dx0, slabs, small = _local_step(x[0], mem[0], positions[0], loss_target[0], whole, gains, rel_bias, sinks,
                                    conv_whole, shards)

    shard_grads = {k: _sum_share(slabs[k], "sum_share_" + k) for k in BIG}

    small_sum = _all_reduce_small(small, "reduce_small")
    loss = small_sum[LOSS_ROW, 0]

    out_g, out_d, out_m, out_v = {}, {}, {}, {}
    for k in BIG:
        g2d = shard_grads[k]
        dl, mn, vn = _adamw(wts[k][0], g2d, mom[k][0], var[k][0], "adamw_" + k)
        out_g[k], out_d[k], out_m[k], out_v[k] = g2d[None], dl[None], mn[None], vn[None]

    conv_g = lax.dynamic_slice(small_sum, (CONV_ROW, s_me * cw_cols), (3, cw_cols))

    def pack(src, conv_block):
        rows = {r: src[k] for k, r in GAIN_ROW.items()}
        rows.update({CONV_ROW + j: conv_block[j] for j in range(3)})
        rows[SINK_ROW], rows[BIAS_ROW] = src["sinks"], src["rel_bias"]
        return _rows_block(rows, d)

    small_g = {k: small_sum[r] for k, r in GAIN_ROW.items()}
    small_g["sinks"] = small_sum[SINK_ROW, 0:sinks.size]
    small_g["rel_bias"] = small_sum[BIAS_ROW, 0:rel_bias.size]
    gp = pack(small_g, conv_g)
    dl, mn, vn = _adamw(pack(wts, conv_w[0]), gp, pack(mom, m_conv_w[0]), pack(var, v_conv_w[0]), "adamw_small")

    def unpack(buf, k):
        if k in GAIN_ROW:
            return buf[GAIN_ROW[k]].reshape(wts[k].shape)
        if k == "conv_w":
            return buf[CONV_ROW:CONV_ROW + 3, 0:cw_cols][None]
        if k == "sinks":
            return buf[SINK_ROW, 0:sinks.size].reshape(sinks.shape)
        return buf[BIAS_ROW, 0:rel_bias.size].reshape(rel_bias.shape)

    for k in WEIGHTS:
        if k not in KIND:
            out_g[k], out_d[k], out_m[k], out_v[k] = unpack(gp, k), unpack(dl, k), unpack(mn, k), unpack(vn, k)

    return (loss, dx0[None], *[out_g[k] for k in WEIGHTS], *[out_d[k] for k in WEIGHTS],
            *[out_m[k] for k in WEIGHTS], *[out_v[k] for k in WEIGHTS])
```

```python
import functools
import math

import jax
import jax.numpy as jnp
from jax import lax
from jax.experimental import pallas as pl
from jax.experimental.pallas import tpu as pltpu

BF = jnp.bfloat16
F32 = jnp.float32
I32 = jnp.int32
S = jax.ShapeDtypeStruct

EPS = 1e-6
NEG = -1e30
POS_PAD = 1 << 30
WINDOW = 128
BLOCK = 128
HEAD_DIM = 64
SWA_HEADS = 16
SWA_KV_HEADS = 4
SWA_GROUP = SWA_HEADS // SWA_KV_HEADS
MEM_HEADS = 4
REL_BUCKETS = 32
REL_MAX_DIST = 128
ADAM_LR = 0.001
ADAM_B1 = 0.9
ADAM_B2 = 0.999
ADAM_EPS = 1e-08
ADAM_WD = 0.01
ADAM_STEP = 10

V7X_VMEM_LIMIT_BYTES = 56 * 1024 * 1024
MESH = pl.DeviceIdType.MESH
ANY = pl.BlockSpec(memory_space=pl.ANY)
VMEM_SPEC = pl.BlockSpec(memory_space=pltpu.VMEM)
SMEM_SPEC = pl.BlockSpec(memory_space=pltpu.SMEM)


def _params(sem=None):
    return pltpu.CompilerParams(dimension_semantics=sem, vmem_limit_bytes=V7X_VMEM_LIMIT_BYTES)


def _resident(shape):
    nd = len(shape)
    return pl.BlockSpec(shape, lambda *_: (0,) * nd, pipeline_mode=pl.Buffered(1))


def _acc_spec(shape):
    nd = len(shape)
    return pl.BlockSpec(shape, lambda *_: (0,) * nd)


def _call(body, *, name, grid, out_shape, in_specs, out_specs, args, sem, scratch_shapes=(), aliases=None, hook=None):
    aliases = aliases or {}
    if hook is None:
        return pl.pallas_call(body, name=name, grid=grid, out_shape=out_shape, in_specs=in_specs, out_specs=out_specs,
                              scratch_shapes=list(scratch_shapes), input_output_aliases=aliases,
                              compiler_params=_params(sem))(*args)
    n_in, n_out, n_scr = len(in_specs), len(out_shape), len(scratch_shapes)
    h_in, h_out = len(hook.ins), len(hook.out_shape)

    def at_step(pick):
        conds = [pl.program_id(ax) == pick(size) for ax, size in enumerate(grid)]
        return functools.reduce(jnp.logical_and, conds)

    def hosted(*refs):
        k_in, x_in = refs[:n_in], refs[n_in:n_in + h_in]
        o0 = n_in + h_in
        k_out, x_out = refs[o0:o0 + n_out], refs[o0 + n_out:o0 + n_out + h_out]
        s0 = o0 + n_out + h_out
        k_scr, x_scr = refs[s0:s0 + n_scr], refs[s0 + n_scr:]

        @pl.when(at_step(lambda size: 0))
        def _():
            hook.start(x_in, x_out, x_scr)
        body(*k_in, *k_out, *k_scr)

        @pl.when(at_step(lambda size: size - 1))
        def _():
            hook.finish(x_in, x_out, x_scr)

    res = pl.pallas_call(
        hosted, name=name, grid=grid, out_shape=tuple(out_shape) + tuple(hook.out_shape),
        in_specs=list(in_specs) + [ANY] * h_in, out_specs=tuple(out_specs) + (ANY,) * h_out,
        scratch_shapes=list(scratch_shapes) + list(hook.scratch), input_output_aliases=aliases,
        compiler_params=_params(("arbitrary",) * len(grid)),
    )(*args, *hook.ins)
    hook.results = res[n_out:]
    return res[:n_out]


def _nn(a, b):
    return jnp.dot(a, b, preferred_element_type=F32)


def _nt(a, b):
    return lax.dot_general(a, b, (((1,), (1,)), ((), ())), preferred_element_type=F32)


def _tn(a, b):
    return lax.dot_general(a, b, (((0,), (0,)), ((), ())), preferred_element_type=F32)


def _sigmoid(v):
    return 1.0 / (1.0 + jnp.exp(-v))


def _rms(x):
    r = lax.rsqrt(jnp.mean(x * x, axis=-1, keepdims=True) + EPS)
    return x * r, r


def _rms_bwd(dh, n, r, g):
    dn = dh * g
    dx = r * (dn - n * jnp.mean(dn * n, axis=-1, keepdims=True))
    return dx, jnp.sum(dh * n, axis=0, keepdims=True)


def _ffn_fwd(x, gn, wgu, wd, name, hook=None):
    t, d = x.shape
    f = wd.shape[0]
    tm, fc = 256, 1408

    def body(x_ref, gn_ref, wgu_ref, wd_ref, xo_ref, g_ref, u_ref):
        xv = x_ref[...]
        n, _ = _rms(xv)
        h = (n * gn_ref[...]).astype(BF)
        acc = jnp.zeros((tm, d), F32)
        for c0 in range(0, f, fc):
            g = _nn(h, wgu_ref[:, c0:c0 + fc])
            u = _nn(h, wgu_ref[:, f + c0:f + c0 + fc])
            g_ref[:, c0:c0 + fc] = g.astype(BF)
            u_ref[:, c0:c0 + fc] = u.astype(BF)
            a = (g * _sigmoid(g)) * u
            acc = acc + _nn(a.astype(BF), wd_ref[c0:c0 + fc, :])
        xo_ref[...] = xv + 0.5 * acc

    return _call(
        body, name=name, grid=(t // tm,),
        out_shape=(S((t, d), F32), S((t, f), BF), S((t, f), BF)),
        in_specs=[pl.BlockSpec((tm, d), lambda i: (i, 0)), _resident((1, d)), _resident(wgu.shape), _resident(wd.shape)],
        out_specs=(pl.BlockSpec((tm, d), lambda i: (i, 0)), pl.BlockSpec((tm, f), lambda i: (i, 0)),
                   pl.BlockSpec((tm, f), lambda i: (i, 0))),
        sem=("parallel",), args=(x, gn, wgu, wd), hook=hook)


def _mix_proj(x, gn, w_in, hook=None):
    t, d = x.shape
    tm = 256
    nqkv = 1536
    ne = w_in.shape[1] - nqkv

    def body(x_ref, gn_ref, w_ref, qkv_ref, e_ref):
        n, _ = _rms(x_ref[...])
        h = (n * gn_ref[...]).astype(BF)
        qkv_ref[...] = _nn(h, w_ref[:, 0:nqkv]).astype(BF)
        for c0 in range(0, ne, 1024):
            e_ref[:, c0:c0 + 1024] = _nn(h, w_ref[:, nqkv + c0:nqkv + c0 + 1024]).astype(BF)

    return _call(
        body, name="mix_proj", grid=(t // tm,),
        out_shape=(S((t, nqkv), BF), S((t, ne), BF)),
        in_specs=[pl.BlockSpec((tm, d), lambda i: (i, 0)), _resident((1, d)), _resident(w_in.shape)],
        out_specs=(pl.BlockSpec((tm, nqkv), lambda i: (i, 0)), pl.BlockSpec((tm, ne), lambda i: (i, 0))),
        sem=("parallel",), args=(x, gn, w_in), hook=hook)


def _t5_bucket(rel):
    n = jnp.maximum(rel, 0)
    max_exact = REL_BUCKETS // 2
    nf = jnp.maximum(n, 1).astype(F32)
    large = max_exact + (jnp.log(nf / max_exact) / math.log(REL_MAX_DIST / max_exact)
                         * (REL_BUCKETS - max_exact)).astype(I32)
    large = jnp.minimum(large, REL_BUCKETS - 1)
    return jnp.where(n < max_exact, n, large)


def _block_rel():
    i = lax.broadcasted_iota(I32, (BLOCK, 2 * BLOCK), 0)
    j = lax.broadcasted_iota(I32, (BLOCK, 2 * BLOCK), 1)
    return i + BLOCK - j


def _bias_build(rel_bias):
    def body(rb_ref, o_ref):
        bucket = _t5_bucket(_block_rel())
        for h in range(SWA_HEADS):
            acc = jnp.zeros((BLOCK, 2 * BLOCK), F32)
            for b in range(REL_BUCKETS):
                acc = jnp.where(bucket == b, rb_ref[b, h], acc)
            o_ref[h] = acc

    return pl.pallas_call(
        body, name="bias_build", out_shape=S((SWA_HEADS, BLOCK, 2 * BLOCK), F32),
        in_specs=[SMEM_SPEC], out_specs=VMEM_SPEC,
    )(rel_bias)


GROUP_ROWS = SWA_GROUP * BLOCK


def _swa_visible(b, pq_ref, pkp_ref, pkc_ref):
    pk = jnp.concatenate([pkp_ref[...], pkc_ref[...]], axis=1)
    col = lax.broadcasted_iota(I32, (1, 2 * BLOCK), 1)
    pk = jnp.where(jnp.logical_and(b == 0, col < BLOCK), POS_PAD, pk)
    rel = jnp.concatenate([pq_ref[...]] * SWA_GROUP, axis=0) - pk
    return jnp.logical_and(rel >= 0, rel < WINDOW)


def _group_heads(ref, hk):
    h0 = hk * SWA_GROUP
    return jnp.concatenate([ref[:, (h0 + g) * HEAD_DIM:(h0 + g + 1) * HEAD_DIM] for g in range(SWA_GROUP)], axis=0)


def _group_sinks(sink_ref, hk):
    row = lax.broadcasted_iota(I32, (GROUP_ROWS, 1), 0)
    col = jnp.zeros((GROUP_ROWS, 1), F32) + sink_ref[0, hk * SWA_GROUP]
    for g in range(1, SWA_GROUP):
        col = jnp.where(row >= g * BLOCK, sink_ref[0, hk * SWA_GROUP + g], col)
    return col


def _swa_probs(qg, kh, vis, bias, sink):
    s = _nt(qg, kh) * (HEAD_DIM ** -0.5)
    s = jnp.where(vis, s + bias, NEG)
    m = jnp.maximum(jnp.max(s, axis=-1, keepdims=True), sink)
    p = jnp.exp(s - m)
    ps = jnp.exp(sink - m)
    inv = 1.0 / (jnp.sum(p, axis=-1, keepdims=True) + ps)
    return p * inv, ps * inv


def _swa_fwd(qkv, pos_col, pos_row, bias_t, sinks, hook=None):
    t = qkv.shape[0]
    nb = t // BLOCK
    qw = SWA_HEADS * HEAD_DIM
    kw = SWA_KV_HEADS * HEAD_DIM

    def body(q_ref, kp_ref, kc_ref, vp_ref, vc_ref, pq_ref, pkp_ref, pkc_ref, bias_ref, sink_ref, o_ref):
        b = pl.program_id(0)
        vis = _swa_visible(b, pq_ref, pkp_ref, pkc_ref)
        k2 = jnp.concatenate([kp_ref[...], kc_ref[...]], axis=0)
        v2 = jnp.concatenate([vp_ref[...], vc_ref[...]], axis=0)
        for hk in range(SWA_KV_HEADS):
            kh = k2[:, hk * HEAD_DIM:(hk + 1) * HEAD_DIM]
            vh = v2[:, hk * HEAD_DIM:(hk + 1) * HEAD_DIM]
            pn, _ = _swa_probs(_group_heads(q_ref, hk), kh, vis, bias_ref[hk], _group_sinks(sink_ref, hk))
            o = _nn(pn.astype(BF), vh)
            for g in range(SWA_GROUP):
                h = hk * SWA_GROUP + g
                o_ref[:, h * HEAD_DIM:(h + 1) * HEAD_DIM] = o[g * BLOCK:(g + 1) * BLOCK]

    prev = lambda b: jnp.maximum(b - 1, 0)
    return _call(
        body, name="swa_fwd", grid=(nb,), out_shape=(S((t, qw), F32),),
        in_specs=[
            pl.BlockSpec((BLOCK, qw), lambda b: (b, 0)),
            pl.BlockSpec((BLOCK, kw), lambda b: (prev(b), qw // kw)),
            pl.BlockSpec((BLOCK, kw), lambda b: (b, qw // kw)),
            pl.BlockSpec((BLOCK, kw), lambda b: (prev(b), qw // kw + 1)),
            pl.BlockSpec((BLOCK, kw), lambda b: (b, qw // kw + 1)),
            pl.BlockSpec((BLOCK, 1), lambda b: (b, 0)),
            pl.BlockSpec((1, BLOCK), lambda b: (0, prev(b))),
            pl.BlockSpec((1, BLOCK), lambda b: (0, b)),
            _resident(bias_t.shape),
            SMEM_SPEC,
        ],
        out_specs=(pl.BlockSpec((BLOCK, qw), lambda b: (b, 0)),),
        sem=("parallel",), args=(qkv, qkv, qkv, qkv, qkv, pos_col, pos_row, pos_row, bias_t, sinks), hook=hook)[0]


HALO = 16


def _conv_taps(z, zh, first):
    tm = z.shape[0]
    zh = jnp.where(first, 0.0, zh)
    row = lax.broadcasted_iota(I32, (tm, 1), 0)
    z1 = jnp.where(row == 0, zh[HALO - 1:HALO, :], pltpu.roll(z, 1, 0))
    z2 = jnp.where(row == 0, zh[HALO - 2:HALO - 1, :], jnp.where(row == 1, zh[HALO - 1:HALO, :], pltpu.roll(z, 2, 0)))
    return z1, z2


def _mix_out_fwd(e, attn, conv_w, w_out, x, hook=None):
    t, d = x.shape
    tm = 256
    hb = tm // HALO
    f32 = lambda ref: ref[...].astype(F32)

    def body(c_ref, b_ref, u_ref, ga_ref, gc_ref, ch_ref, uh_ref, attn_ref, cw_ref, w_ref, x_ref, xo_ref, mg_ref):
        i = pl.program_id(0)
        z = f32(c_ref) * f32(u_ref)
        z1, z2 = _conv_taps(z, f32(ch_ref) * f32(uh_ref), i == 0)
        s = cw_ref[0:1, :] * z2 + cw_ref[1:2, :] * z1 + cw_ref[2:3, :] * z
        conv = f32(b_ref) * s
        merged = (_sigmoid(f32(ga_ref)) * attn_ref[...] + _sigmoid(f32(gc_ref)) * conv).astype(BF)
        mg_ref[...] = merged
        xo_ref[...] = x_ref[...] + _nn(merged, w_ref[...])

    ecol = lambda cb: pl.BlockSpec((tm, d), lambda i: (i, cb))
    halo = lambda cb: pl.BlockSpec((HALO, d), lambda i: (jnp.maximum(i * hb - 1, 0), cb))
    row = pl.BlockSpec((tm, d), lambda i: (i, 0))
    return _call(
        body, name="mix_out_fwd", grid=(t // tm,),
        out_shape=(S((t, d), F32), S((t, d), BF)),
        in_specs=[ecol(0), ecol(1), ecol(2), ecol(3), ecol(4), halo(0), halo(2), row,
                  _resident(conv_w.shape), _resident(w_out.shape), row],
        out_specs=(row, row),
        sem=("parallel",), args=(e, e, e, e, e, e, e, attn, conv_w, w_out, x), hook=hook)


def _mem_kv(mem, gm, wkv):
    m, d = mem.shape

    def body(mem_ref, gm_ref, w_ref, mh_ref, kv_ref):
        n, _ = _rms(mem_ref[...])
        mh = (n * gm_ref[...]).astype(BF)
        mh_ref[...] = mh
        kv_ref[...] = _nn(mh, w_ref[...]).astype(BF)

    return pl.pallas_call(
        body, name="mem_kv", out_shape=(S((m, d), BF), S((m, wkv.shape[1]), BF)),
        compiler_params=_params(),
    )(mem, gm, wkv)


def _xattn_probs(qh, kh):
    s = _nt(qh, kh) * (kh.shape[1] ** -0.5)
    p = jnp.exp(s - jnp.max(s, axis=-1, keepdims=True))
    return p * (1.0 / jnp.sum(p, axis=-1, keepdims=True))


def _xattn_fwd(x, gn, wq, kv, wo):
    t, d = x.shape
    tm = 256
    hd = d // MEM_HEADS

    def body(x_ref, gn_ref, wq_ref, kv_ref, wo_ref, xo_ref, q_ref, o_ref):
        xv = x_ref[...]
        n, _ = _rms(xv)
        q = _nn((n * gn_ref[...]).astype(BF), wq_ref[...]).astype(BF)
        q_ref[...] = q
        outs = []
        for hh in range(MEM_HEADS):
            p = _xattn_probs(q[:, hh * hd:(hh + 1) * hd], kv_ref[:, hh * hd:(hh + 1) * hd])
            outs.append(_nn(p.astype(BF), kv_ref[:, d + hh * hd:d + (hh + 1) * hd]))
        o = jnp.concatenate(outs, axis=1).astype(BF)
        o_ref[...] = o
        xo_ref[...] = xv + _nn(o, wo_ref[...])

    row = pl.BlockSpec((tm, d), lambda i: (i, 0))
    return pl.pallas_call(
        body, name="xattn_fwd", grid=(t // tm,),
        out_shape=(S((t, d), F32), S((t, d), BF), S((t, d), BF)),
        in_specs=[row, _resident((1, d)), _resident(wq.shape), _resident(kv.shape), _resident(wo.shape)],
        out_specs=(row, row, row),
        compiler_params=_params(("parallel",)),
    )(x, gn, wq, kv, wo)


def _loss_bwd(x, gf, target):
    t, d = x.shape
    tm = 512

    def body(x_ref, gf_ref, t_ref, dx_ref, dg_ref, loss_ref):
        @pl.when(pl.program_id(0) == 0)
        def _():
            dg_ref[...] = jnp.zeros_like(dg_ref)
            loss_ref[...] = jnp.zeros_like(loss_ref)
        n, r = _rms(x_ref[...])
        g = gf_ref[...]
        err = n * g - t_ref[...]
        loss_ref[...] += 0.5 * jnp.sum(jnp.sum(err * err, axis=-1, keepdims=True) / d, axis=0, keepdims=True)
        dx, dg = _rms_bwd(err / d, n, r, g)
        dx_ref[...] = dx
        dg_ref[...] += dg

    row = pl.BlockSpec((tm, d), lambda i: (i, 0))
    return pl.pallas_call(
        body, name="loss_bwd", grid=(t // tm,),
        out_shape=(S((t, d), F32), S((1, d), F32), S((1, 128), F32)),
        in_specs=[row, _resident((1, d)), row],
        out_specs=(row, _acc_spec((1, d)), _acc_spec((1, 128))),
        compiler_params=_params(("arbitrary",)),
    )(x, gf, target)


def _ffn_bwd(dxo, x, gn, g, u, wgu, wd, name, hook=None):
    t, d = x.shape
    f = wd.shape[0]
    tm, fc = 256, 1408

    def body(dxo_ref, x_ref, gn_ref, g_ref, u_ref, wgu_ref, wd_ref, dx_ref, dgn_ref, dgu_ref, a_ref, h_ref, dyh_ref):
        @pl.when(pl.program_id(0) == 0)
        def _():
            dgn_ref[...] = jnp.zeros_like(dgn_ref)
        dxov = dxo_ref[...]
        dyh = (0.5 * dxov).astype(BF)
        dyh_ref[...] = dyh
        n, r = _rms(x_ref[...])
        gnv = gn_ref[...]
        h_ref[...] = (n * gnv).astype(BF)
        dh = jnp.zeros((tm, d), F32)
        for c0 in range(0, f, fc):
            gv = g_ref[:, c0:c0 + fc].astype(F32)
            uv = u_ref[:, c0:c0 + fc].astype(F32)
            da = _nt(dyh, wd_ref[c0:c0 + fc, :])
            sg = _sigmoid(gv)
            silu = gv * sg
            a_ref[:, c0:c0 + fc] = (silu * uv).astype(BF)
            dg = (da * uv * (sg * (1.0 + gv * (1.0 - sg)))).astype(BF)
            du = (da * silu).astype(BF)
            dgu_ref[:, c0:c0 + fc] = dg
            dgu_ref[:, f + c0:f + c0 + fc] = du
            dh = dh + _nt(dg, wgu_ref[:, c0:c0 + fc]) + _nt(du, wgu_ref[:, f + c0:f + c0 + fc])
        dx, dgn = _rms_bwd(dh, n, r, gnv)
        dx_ref[...] = dxov + dx
        dgn_ref[...] += dgn

    row = pl.BlockSpec((tm, d), lambda i: (i, 0))
    frow = pl.BlockSpec((tm, f), lambda i: (i, 0))
    return _call(
        body, name=name, grid=(t // tm,),
        out_shape=(S((t, d), F32), S((1, d), F32), S((t, 2 * f), BF), S((t, f), BF), S((t, d), BF), S((t, d), BF)),
        in_specs=[row, row, _resident((1, d)), frow, frow, _resident(wgu.shape), _resident(wd.shape)],
        out_specs=(row, _acc_spec((1, d)), pl.BlockSpec((tm, 2 * f), lambda i: (i, 0)), frow, row, row),
        sem=("arbitrary",), args=(dxo, x, gn, g, u, wgu, wd), hook=hook)


def _dw(a, b, tn, name, into=None, col_block=0, ncols=None):
    t, ka = a.shape
    nb = b.shape[1]
    tt = 1024
    nt = t // tt
    ncols = nb if ncols is None else ncols

    def body(*refs):
        a_ref, b_ref = refs[0], refs[1]
        o_ref, acc_ref = refs[-2], refs[-1]
        k = pl.program_id(1)

        @pl.when(k == 0)
        def _():
            acc_ref[...] = jnp.zeros_like(acc_ref)
        acc_ref[...] += _tn(a_ref[...], b_ref[...])

        @pl.when(k == nt - 1)
        def _():
            o_ref[...] = acc_ref[...].astype(BF)

    in_specs = [pl.BlockSpec((tt, ka), lambda j, k: (k, 0)), pl.BlockSpec((tt, tn), lambda j, k: (k, j))]
    args = [a, b]
    aliases = {}
    if into is not None:
        in_specs.append(ANY)
        args.append(into)
        aliases = {2: 0}
    return pl.pallas_call(
        body, name=name, grid=(nb // tn, nt), out_shape=S((ka, ncols), BF),
        in_specs=in_specs, out_specs=pl.BlockSpec((ka, tn), lambda j, k: (0, j + col_block)),
        scratch_shapes=[pltpu.VMEM((ka, tn), F32)], input_output_aliases=aliases,
        compiler_params=_params(("parallel", "arbitrary")),
    )(*args)


def _dw_pair(a, b, tn, name, kind, into=None, col_block=0, ncols=None, hook=None):
    t, ka = a.shape
    nb = b.shape[1]
    tt = 2048
    nt, nj = t // tt, nb // tn
    ncols = nb if ncols is None else ncols
    col = kind == "col"
    rh = ka // 2 if col else ka // 8
    tile = (rh, tn) if col else (4, rh, tn)
    half = (rh, ncols) if col else (4, rh, ncols)
    lead = (slice(None),) * (len(tile) - 1)

    def body(*refs):
        a_ref, b_ref = refs[0], refs[1]
        mine_ref, sib_ref, acc_ref, stage, ssem, rsem = refs[-6:]
        j, k = pl.program_id(0), pl.program_id(1)
        x, y, c = _place()
        sibling = (x, y, 1 - c)

        @pl.when(k == 0)
        def _():
            acc_ref[...] = jnp.zeros_like(acc_ref)
        acc_ref[...] += _tn(a_ref[...], b_ref[...])

        def send(slot, jj):
            dst = sib_ref.at[lead + (pl.ds(pl.multiple_of((jj + col_block) * tn, 128), tn),)]
            return _remote(stage.at[slot], dst, ssem.at[slot], rsem, sibling)

        def rows(s, whose):
            return acc_ref[pl.ds(pl.multiple_of(s * 2 * rh + whose * rh, 16), rh), :].astype(BF)

        @pl.when(k == nt - 1)
        def _():
            slot = j % 2

            @pl.when(j >= 2)
            def _():
                send(slot, j).wait_send()
            if col:
                mine_ref[...] = rows(0, c)
                stage[slot] = rows(0, 1 - c)
            else:
                for s in range(4):
                    mine_ref[s] = rows(s, c)
                    stage[slot, s] = rows(s, 1 - c)
            send(slot, j).start()

        @pl.when(jnp.logical_and(j == nj - 1, k == nt - 1))
        def _():
            for jj in range(max(nj - 2, 0), nj):
                send(jj % 2, jj).wait_send()
            got = sib_ref.at[lead + (pl.ds(col_block * tn, nb),)]
            _remote(got, got, ssem.at[0], rsem, sibling).wait_recv()

    in_specs = [pl.BlockSpec((tt, ka), lambda j, k: (k, 0)), pl.BlockSpec((tt, tn), lambda j, k: (k, j))]
    args = [a, b]
    aliases = {}
    if into is not None:
        in_specs += [ANY, ANY]
        args += list(into)
        aliases = {2: 0, 3: 1}
    mine_spec = pl.BlockSpec(tile, (lambda j, k: (0, j + col_block)) if col else (lambda j, k: (0, 0, j + col_block)))
    return _call(
        body, name=name, grid=(nj, nt), out_shape=(S(half, BF), S(half, BF)),
        in_specs=in_specs, out_specs=(mine_spec, ANY),
        scratch_shapes=[pltpu.VMEM((ka, tn), F32), pltpu.VMEM((2,) + tile, BF), pltpu.SemaphoreType.DMA((2,)),
                        pltpu.SemaphoreType.DMA],
        aliases=aliases, sem=("arbitrary", "arbitrary"), args=args, hook=hook)


def _xattn_bwd(dxo, x, gn, q, kv, wq, wo, hook=None):
    t, d = x.shape
    tm = 256
    hd = d // MEM_HEADS
    nkv = kv.shape[0]

    def body(dxo_ref, x_ref, gn_ref, q_ref, kv_ref, wq_ref, wo_ref, dx_ref, dgn_ref, dkv_ref, dxh_ref, h_ref, dq_ref):
        @pl.when(pl.program_id(0) == 0)
        def _():
            dgn_ref[...] = jnp.zeros_like(dgn_ref)
            dkv_ref[...] = jnp.zeros_like(dkv_ref)
        dxov = dxo_ref[...]
        dxh = dxov.astype(BF)
        dxh_ref[...] = dxh
        do = _nt(dxh, wo_ref[...]).astype(BF)
        dqs = []
        for hh in range(MEM_HEADS):
            lo, hi = hh * hd, (hh + 1) * hd
            qh = q_ref[:, lo:hi]
            kh = kv_ref[:, lo:hi]
            vh = kv_ref[:, d + lo:d + hi]
            doh = do[:, lo:hi]
            p = _xattn_probs(qh, kh)
            dp = _nt(doh, vh)
            ds = (p * (dp - jnp.sum(p * dp, axis=-1, keepdims=True)) * (hd ** -0.5)).astype(BF)
            dqs.append(_nn(ds, kh))
            dkv_ref[:, lo:hi] += _tn(ds, qh)
            dkv_ref[:, d + lo:d + hi] += _tn(p.astype(BF), doh)
        dq = jnp.concatenate(dqs, axis=1).astype(BF)
        dq_ref[...] = dq
        n, r = _rms(x_ref[...])
        gnv = gn_ref[...]
        h_ref[...] = (n * gnv).astype(BF)
        dx, dgn = _rms_bwd(_nt(dq, wq_ref[...]), n, r, gnv)
        dx_ref[...] = dxov + dx
        dgn_ref[...] += dgn

    row = pl.BlockSpec((tm, d), lambda i: (i, 0))
    return _call(
        body, name="xattn_bwd", grid=(t // tm,),
        out_shape=(S((t, d), F32), S((1, d), F32), S((nkv, 2 * d), F32), S((t, d), BF), S((t, d), BF), S((t, d), BF)),
        in_specs=[row, row, _resident((1, d)), row, _resident(kv.shape), _resident(wq.shape), _resident(wo.shape)],
        out_specs=(row, _acc_spec((1, d)), _acc_spec((nkv, 2 * d)), row, row, row),
        sem=("arbitrary",), args=(dxo, x, gn, q, kv, wq, wo), hook=hook)


def _mem_bwd(dkv, mh, mem, gm, wkv, pair=False):
    m, d = mem.shape
    rows, cols = wkv.shape
    rh = rows // 2

    def body(dkv_ref, mh_ref, mem_ref, gm_ref, w_ref, *outs):
        dkvb = dkv_ref[...].astype(BF)
        dw = _tn(mh_ref[...], dkvb).astype(BF)
        dmh = _nt(dkvb, w_ref[...])
        n, _ = _rms(mem_ref[...])
        if not pair:
            dw_ref, dgm_ref = outs
            dw_ref[...] = dw
        else:
            mine_ref, sib_ref, dgm_ref, whole, ssem, rsem = outs
            x, y, c = _place()
            whole[...] = dw
            cp = _remote(whole.at[pl.ds(pl.multiple_of((1 - c) * rh, 16), rh), :], sib_ref, ssem, rsem, (x, y, 1 - c))
            cp.start()
            mine_ref[...] = whole[pl.ds(pl.multiple_of(c * rh, 16), rh), :]
            cp.wait()
        dgm_ref[...] = jnp.sum(dmh * n, axis=0, keepdims=True)

    if not pair:
        return pl.pallas_call(body, name="mem_bwd", out_shape=(S(wkv.shape, BF), S((1, d), F32)),
                              compiler_params=_params())(dkv, mh, mem, gm, wkv)
    mine, sib, dgm = pl.pallas_call(
        body, name="mem_bwd", out_shape=(S((rh, cols), BF), S((rh, cols), BF), S((1, d), F32)),
        in_specs=[VMEM_SPEC] * 5, out_specs=(VMEM_SPEC, ANY, VMEM_SPEC),
        scratch_shapes=[pltpu.VMEM((rows, cols), BF), pltpu.SemaphoreType.DMA, pltpu.SemaphoreType.DMA],
        compiler_params=_params())(dkv, mh, mem, gm, wkv)
    return (mine, sib), dgm


def _mix_out_bwd(dxo, e, attn, conv_w, w_out, hook=None):
    t, d = attn.shape
    tm = 256
    nt = t // tm
    f32 = lambda ref: ref[...].astype(F32)

    def body(dxo_ref, dxn_ref, c_ref, b_ref, u_ref, ga_ref, gc_ref, ch_ref, uh_ref, bn_ref, gcn_ref,
             attn_ref, cw_ref, w_ref, dattn_ref, de_ref, dcw_ref, dxh_ref):
        i = pl.program_id(0)

        @pl.when(i == 0)
        def _():
            dcw_ref[...] = jnp.zeros_like(dcw_ref)
        dxh = dxo_ref[...].astype(BF)
        dxh_ref[...] = dxh
        w = w_ref[...]
        dm = _nt(dxh, w)
        dmn = _nt(dxn_ref[...].astype(BF), w)
        cv, bv, uv = f32(c_ref), f32(b_ref), f32(u_ref)
        sga = _sigmoid(f32(ga_ref))
        sgc = _sigmoid(f32(gc_ref))
        z = cv * uv
        z1, z2 = _conv_taps(z, f32(ch_ref) * f32(uh_ref), i == 0)
        w0, w1, w2 = cw_ref[0:1, :], cw_ref[1:2, :], cw_ref[2:3, :]
        s = w0 * z2 + w1 * z1 + w2 * z
        av = attn_ref[...]
        dattn_ref[...] = (dm * sga).astype(BF)
        dconv = dm * sgc
        ds = dconv * bv
        dsn = jnp.where(i == nt - 1, 0.0, dmn * _sigmoid(gcn_ref[0:8, :].astype(F32)) * bn_ref[0:8, :].astype(F32))
        row = lax.broadcasted_iota(I32, (tm, 1), 0)
        dsp1 = jnp.where(row == tm - 1, dsn[0:1, :], pltpu.roll(ds, tm - 1, 0))
        dsp2 = jnp.where(row == tm - 2, dsn[0:1, :], jnp.where(row == tm - 1, dsn[1:2, :], pltpu.roll(ds, tm - 2, 0)))
        dz = w2 * ds + w1 * dsp1 + w0 * dsp2
        de_ref[:, 0:d] = (dz * uv).astype(BF)
        de_ref[:, d:2 * d] = (dconv * s).astype(BF)
        de_ref[:, 2 * d:3 * d] = (dz * cv).astype(BF)
        de_ref[:, 3 * d:4 * d] = (dm * av * sga * (1.0 - sga)).astype(BF)
        de_ref[:, 4 * d:5 * d] = (dm * (bv * s) * sgc * (1.0 - sgc)).astype(BF)
        dcw_ref[0:1, :] += jnp.sum(ds * z2, axis=0, keepdims=True)
        dcw_ref[1:2, :] += jnp.sum(ds * z1, axis=0, keepdims=True)
        dcw_ref[2:3, :] += jnp.sum(ds * z, axis=0, keepdims=True)

    ecol = lambda cb: pl.BlockSpec((tm, d), lambda i: (i, cb))
    prev = lambda cb: pl.BlockSpec((HALO, d), lambda i: (jnp.maximum(i * (tm // HALO) - 1, 0), cb))
    nxt = lambda rows, cb: pl.BlockSpec((rows, d), lambda i: (jnp.minimum((i + 1) * (tm // rows), t // rows - 1), cb))
    row = pl.BlockSpec((tm, d), lambda i: (i, 0))
    return _call(
        body, name="mix_out_bwd", grid=(nt,),
        out_shape=(S((t, d), BF), S((t, 5 * d), BF), S((8, d), F32), S((t, d), BF)),
        in_specs=[row, nxt(8, 0), ecol(0), ecol(1), ecol(2), ecol(3), ecol(4), prev(0), prev(2), nxt(HALO, 1), nxt(HALO, 4),
                  row, _resident(conv_w.shape), _resident(w_out.shape)],
        out_specs=(row, pl.BlockSpec((tm, 5 * d), lambda i: (i, 0)), _acc_spec((8, d)), row),
        sem=("arbitrary",), args=(dxo, dxo, e, e, e, e, e, e, e, e, e, attn, conv_w, w_out), hook=hook)


def _swa_bwd(qkv, dattn, pos_col, pos_row, bias_t, sinks, hook=None):
    t = qkv.shape[0]
    nb = t // BLOCK
    qw = SWA_HEADS * HEAD_DIM
    kw = SWA_KV_HEADS * HEAD_DIM

    def body(q_ref, kp_ref, kc_ref, vp_ref, vc_ref, do_ref, pq_ref, pkp_ref, pkc_ref, bias_ref, sink_ref,
             dq_ref, dkv_ref, gb_ref, dsk_ref, carry_ref):
        b = pl.program_id(0)

        @pl.when(b == 0)
        def _():
            gb_ref[...] = jnp.zeros_like(gb_ref)
            dsk_ref[...] = jnp.zeros_like(dsk_ref)
            carry_ref[...] = jnp.zeros_like(carry_ref)

        @pl.when(b < nb)
        def _():
            vis = _swa_visible(b, pq_ref, pkp_ref, pkc_ref)
            k2 = jnp.concatenate([kp_ref[...], kc_ref[...]], axis=0)
            v2 = jnp.concatenate([vp_ref[...], vc_ref[...]], axis=0)
            for hk in range(SWA_KV_HEADS):
                lo, hi = hk * HEAD_DIM, (hk + 1) * HEAD_DIM
                kh = k2[:, lo:hi]
                vh = v2[:, lo:hi]
                qg = _group_heads(q_ref, hk)
                dog = _group_heads(do_ref, hk)
                pn, psn = _swa_probs(qg, kh, vis, bias_ref[hk], _group_sinks(sink_ref, hk))
                dp = _nt(dog, vh)
                delta = jnp.sum(pn * dp, axis=-1, keepdims=True)
                ds = pn * (dp - delta)
                gb_ref[hk] += ds
                dsk_ref[hk] += -psn * delta
                dsb = (ds * (HEAD_DIM ** -0.5)).astype(BF)
                dqg = _nn(dsb, kh).astype(BF)
                for g in range(SWA_GROUP):
                    h = hk * SWA_GROUP + g
                    dq_ref[:, h * HEAD_DIM:(h + 1) * HEAD_DIM] = dqg[g * BLOCK:(g + 1) * BLOCK]
                dk = _tn(dsb, qg)
                dv = _tn(pn.astype(BF), dog)
                dkv_ref[:, lo:hi] = (carry_ref[:, lo:hi] + dk[0:BLOCK]).astype(BF)
                dkv_ref[:, kw + lo:kw + hi] = (carry_ref[:, kw + lo:kw + hi] + dv[0:BLOCK]).astype(BF)
                carry_ref[:, lo:hi] = dk[BLOCK:2 * BLOCK]
                carry_ref[:, kw + lo:kw + hi] = dv[BLOCK:2 * BLOCK]

        @pl.when(b == nb)
        def _():
            dkv_ref[...] = carry_ref[...].astype(BF)

    cur = lambda b: jnp.minimum(b, nb - 1)
    prev = lambda b: jnp.maximum(cur(b) - 1, 0)
    return _call(
        body, name="swa_bwd", grid=(nb + 1,),
        out_shape=(S((t, qw), BF), S((t, 2 * kw), BF), S((SWA_KV_HEADS, GROUP_ROWS, 2 * BLOCK), F32),
                   S((SWA_KV_HEADS, GROUP_ROWS, 1), F32)),
        in_specs=[
            pl.BlockSpec((BLOCK, qw), lambda b: (cur(b), 0)),
            pl.BlockSpec((BLOCK, kw), lambda b: (prev(b), qw // kw)),
            pl.BlockSpec((BLOCK, kw), lambda b: (cur(b), qw // kw)),
            pl.BlockSpec((BLOCK, kw), lambda b: (prev(b), qw // kw + 1)),
            pl.BlockSpec((BLOCK, kw), lambda b: (cur(b), qw // kw + 1)),
            pl.BlockSpec((BLOCK, qw), lambda b: (cur(b), 0)),
            pl.BlockSpec((BLOCK, 1), lambda b: (cur(b), 0)),
            pl.BlockSpec((1, BLOCK), lambda b: (0, prev(b))),
            pl.BlockSpec((1, BLOCK), lambda b: (0, cur(b))),
            _resident(bias_t.shape),
            SMEM_SPEC,
        ],
        out_specs=(
            pl.BlockSpec((BLOCK, qw), lambda b: (cur(b), 0)),
            pl.BlockSpec((BLOCK, 2 * kw), lambda b: (jnp.maximum(b - 1, 0), 0)),
            _acc_spec((SWA_KV_HEADS, GROUP_ROWS, 2 * BLOCK)),
            _acc_spec((SWA_KV_HEADS, GROUP_ROWS, 1)),
        ),
        scratch_shapes=[pltpu.VMEM((BLOCK, 2 * kw), F32)],
        sem=("arbitrary",), args=(qkv, qkv, qkv, qkv, qkv, dattn, pos_col, pos_row, pos_row, bias_t, sinks), hook=hook)


def _bias_reduce(gb, dsk):
    def body(gb_ref, dsk_ref, drb_ref, dsink_ref):
        bucket = _t5_bucket(_block_rel())
        for b in range(REL_BUCKETS):
            mask = bucket == b
            for h in range(SWA_HEADS):
                drb_ref[b, h] = jnp.sum(jnp.where(mask, gb_ref[h], 0.0))
        for h in range(SWA_HEADS):
            dsink_ref[0, h] = jnp.sum(dsk_ref[h])

    return pl.pallas_call(
        body, name="bias_reduce", out_shape=(S((REL_BUCKETS, SWA_HEADS), F32), S((1, SWA_HEADS), F32)),
        in_specs=[VMEM_SPEC, VMEM_SPEC], out_specs=(SMEM_SPEC, SMEM_SPEC),
    )(gb, dsk)


def _mix_in_bwd(dq, dkv, de, x, gn, w_in, dxo):
    t, d = x.shape
    tm = 256
    nq, nkv, ne = dq.shape[1], dkv.shape[1], de.shape[1]

    def body(dq_ref, dkv_ref, de_ref, x_ref, gn_ref, w_ref, dxo_ref, dx_ref, dgn_ref, h_ref):
        @pl.when(pl.program_id(0) == 0)
        def _():
            dgn_ref[...] = jnp.zeros_like(dgn_ref)
        dh = _nt(dq_ref[...], w_ref[:, 0:nq]) + _nt(dkv_ref[...], w_ref[:, nq:nq + nkv])
        for c0 in range(0, ne, 1024):
            dh = dh + _nt(de_ref[:, c0:c0 + 1024], w_ref[:, nq + nkv + c0:nq + nkv + c0 + 1024])
        n, r = _rms(x_ref[...])
        gnv = gn_ref[...]
        h_ref[...] = (n * gnv).astype(BF)
        dx, dgn = _rms_bwd(dh, n, r, gnv)
        dx_ref[...] = dxo_ref[...] + dx
        dgn_ref[...] += dgn

    row = pl.BlockSpec((tm, d), lambda i: (i, 0))
    wide = lambda w: pl.BlockSpec((tm, w), lambda i: (i, 0))
    return pl.pallas_call(
        body, name="mix_in_bwd", grid=(t // tm,),
        out_shape=(S((t, d), F32), S((1, d), F32), S((t, d), BF)),
        in_specs=[wide(nq), wide(nkv), wide(ne), row, _resident((1, d)), _resident(w_in.shape), row],
        out_specs=(row, _acc_spec((1, d)), row),
        compiler_params=_params(("arbitrary",)),
    )(dq, dkv, de, x, gn, w_in, dxo)


def _row_tile(rows):
    for tr in (256, 352, 128, 64, 16, 8):
        if rows % tr == 0:
            return tr
    return rows


def _add_bf16(a, b, name):
    rows, cols = a.shape
    tr = _row_tile(rows)

    def body(a_ref, b_ref, o_ref):
        o_ref[...] = (a_ref[...].astype(F32) + b_ref[...].astype(F32)).astype(BF)

    blk = pl.BlockSpec((tr, cols), lambda i: (i, 0))
    return pl.pallas_call(body, name=name, grid=(rows // tr,), out_shape=S((rows, cols), BF),
                          in_specs=[blk, blk], out_specs=blk, compiler_params=_params(("parallel",)))(a, b)


def _adamw(w, g, m, v, name):
    rows, cols = w.shape
    tr = _row_tile(rows)

    def body(w_ref, g_ref, m_ref, v_ref, d_ref, mo_ref, vo_ref):
        gv = g_ref[...]
        mn = ADAM_B1 * m_ref[...] + (1.0 - ADAM_B1) * gv
        vn = ADAM_B2 * v_ref[...] + (1.0 - ADAM_B2) * (gv * gv)
        m_hat = mn / (1.0 - ADAM_B1 ** ADAM_STEP)
        v_hat = vn / (1.0 - ADAM_B2 ** ADAM_STEP)
        d_ref[...] = -ADAM_LR * (m_hat / (jnp.sqrt(v_hat) + ADAM_EPS) + ADAM_WD * w_ref[...])
        mo_ref[...] = mn
        vo_ref[...] = vn

    blk = pl.BlockSpec((tr, cols), lambda i: (i, 0))
    shp = S((rows, cols), F32)
    return pl.pallas_call(body, name=name, grid=(rows // tr,), out_shape=(shp, shp, shp),
                          in_specs=[blk] * 4, out_specs=(blk,) * 3, compiler_params=_params(("parallel",)))(w, g, m, v)


def _place():
    x, y, c = lax.axis_index("x"), lax.axis_index("y"), lax.axis_index("c")
    return x, y, c


OTHER_CHIPS = ((1, 0), (0, 1), (1, 1))


def _flip(v, f):
    return 1 - v if f else v


def _remote(src, dst, ssem, rsem, dev):
    return pltpu.make_async_remote_copy(src_ref=src, dst_ref=dst, send_sem=ssem, recv_sem=rsem,
                                        device_id=dev, device_id_type=MESH)


def _exchange(hook, name):
    n_in, n_out = len(hook.ins), len(hook.out_shape)

    def body(*refs):
        ins, outs, scr = refs[:n_in], refs[n_in:n_in + n_out], refs[n_in + n_out:]
        hook.start(ins, outs, scr)
        hook.finish(ins, outs, scr)

    return pl.pallas_call(
        body, name=name, out_shape=tuple(hook.out_shape), in_specs=[ANY] * n_in, out_specs=(ANY,) * n_out,
        scratch_shapes=list(hook.scratch), compiler_params=_params(),
    )(*hook.ins)


class _GatherHook:
    def __init__(self, shards, kinds):
        self.ins, self.kinds, n = list(shards), list(kinds), len(shards)
        self.out_shape = tuple(
            S((w.shape[0], 4 * w.shape[1]), BF) if k == "col" else S((4,) + w.shape, BF) for w, k in zip(shards, kinds))
        dma = pltpu.SemaphoreType.DMA
        self.scratch = ([dma((n, 3)) for _ in range(5)] + [dma((n,)), dma((n,))]
                        + [pltpu.VMEM((3, w.shape[0] // 2, w.shape[1]), BF) for w in shards]
                        + [pltpu.VMEM(w.shape, BF) for w in shards])

    def _window(self, outs, i, s, half):
        rows, cols = self.ins[i].shape
        rh = rows // 2
        start = pl.multiple_of(half * rh, 16)
        if self.kinds[i] == "col":
            return outs[i].at[pl.ds(start, rh), pl.ds(pl.multiple_of(s * cols, 128), cols)]
        return outs[i].at[s, pl.ds(start, rh), :]

    def _copies(self, ins, outs, scr):
        n = len(ins)
        ssem, rsem, fssem, frsem, ksem, lsem, osem = scr[:7]
        land, own = scr[7:7 + n], scr[7 + n:7 + 2 * n]
        x, y, c = _place()
        sibling = (x, y, 1 - c)
        loads, stores, sends, forwards, keeps, passed = [], [], [], [], [], []
        for i in range(n):
            rows, cols = self.ins[i].shape
            rh = rows // 2
            mine = (outs[i].at[:, pl.ds(pl.multiple_of((2 * x + y) * cols, 128), cols)] if self.kinds[i] == "col"
                    else outs[i].at[2 * x + y])
            loads.append(functools.partial(pltpu.make_async_copy, ins[i], own[i], lsem.at[i]))
            stores.append(functools.partial(pltpu.make_async_copy, own[i], mine, osem.at[i]))
            src = ins[i].at[pl.ds(pl.multiple_of(c * rh, 16), rh), :]
            for j, (fx, fy) in enumerate(OTHER_CHIPS):
                px, py = _flip(x, fx), _flip(y, fy)
                sends.append(functools.partial(_remote, src, land[i].at[j], ssem.at[i, j], rsem.at[i, j], (px, py, c)))
                here = self._window(outs, i, 2 * px + py, c)
                forwards.append(functools.partial(_remote, land[i].at[j], here, fssem.at[i, j], frsem.at[i, j], sibling))
                keeps.append(functools.partial(pltpu.make_async_copy, land[i].at[j], here, ksem.at[i, j]))
                there = self._window(outs, i, 2 * px + py, 1 - c)
                passed.append(functools.partial(_remote, there, there, fssem.at[i, j], frsem.at[i, j], sibling))
        return loads, stores, sends, forwards, keeps, passed

    def start(self, ins, outs, scr):
        loads, _, sends, _, _, _ = self._copies(ins, outs, scr)
        for make in sends + loads:
            make().start()

    def finish(self, ins, outs, scr):
        loads, stores, sends, forwards, keeps, passed = self._copies(ins, outs, scr)
        for load, store in zip(loads, stores):
            load().wait()
            store().start()
        for send, forward, keep in zip(sends, forwards, keeps):
            send().wait_recv()
            forward().start()
            keep().start()
        for make in passed:
            make().wait_recv()
        for make in sends + forwards:
            make().wait_send()
        for make in keeps + stores:
            make().wait()


class _ChipsHook:
    def __init__(self, parts, kinds):
        self.ins, self.kinds, n = list(parts), list(kinds), len(parts)
        self.out_shape = tuple(
            S((4, p.shape[0], p.shape[1] // 4), BF) if k == "col" else S(p.shape, BF) for p, k in zip(parts, kinds))
        dma = pltpu.SemaphoreType.DMA
        self.scratch = ([dma((n, 3)), dma((n, 3)), dma((n,)), dma((n,))]
                        + [pltpu.VMEM(o.shape[1:], BF) for o in self.out_shape])

    def _slab(self, ins, i, s):
        _, rows, cols = self.out_shape[i].shape
        if self.kinds[i] == "col":
            return ins[i].at[:, pl.ds(pl.multiple_of(s * cols, 128), cols)]
        return ins[i].at[s]

    def _copies(self, ins, outs, scr):
        ssem, rsem, lsem, osem = scr[:4]
        own = scr[4:]
        x, y, c = _place()
        loads, stores, sends = [], [], []
        for i in range(len(ins)):
            loads.append(functools.partial(pltpu.make_async_copy, self._slab(ins, i, 2 * x + y), own[i], lsem.at[i]))
            stores.append(functools.partial(pltpu.make_async_copy, own[i], outs[i].at[3], osem.at[i]))
            for j, (fx, fy) in enumerate(OTHER_CHIPS):
                px, py = _flip(x, fx), _flip(y, fy)
                sends.append(functools.partial(_remote, self._slab(ins, i, 2 * px + py), outs[i].at[j], ssem.at[i, j],
                                               rsem.at[i, j], (px, py, c)))
        return loads, stores, sends

    def start(self, ins, outs, scr):
        loads, _, sends = self._copies(ins, outs, scr)
        for make in sends + loads:
            make().start()

    def finish(self, ins, outs, scr):
        loads, stores, sends = self._copies(ins, outs, scr)
        for load, store in zip(loads, stores):
            load().wait()
            store().start()
        for make in sends + stores:
            make().wait()


def _sum_share(slabs, name):
    _, rh, cols = slabs.shape
    tr = _row_tile(rh)
    nt = rh // tr

    def body(p_ref, o_ref, stage, lsem, ssem, rsem):
        i = pl.program_id(0)
        x, y, c = _place()
        sibling = (x, y, 1 - c)
        acc = p_ref[3].astype(F32)
        for k in range(3):
            acc = acc + p_ref[k].astype(F32)
        stage[i] = acc

        def copies(k):
            dst = o_ref.at[pl.ds(pl.multiple_of(c * rh + k * tr, 8), tr), :]
            return (pltpu.make_async_copy(stage.at[k], dst, lsem.at[k]),
                    _remote(stage.at[k], dst, ssem.at[k], rsem, sibling))

        for cp in copies(i):
            cp.start()

        @pl.when(i == nt - 1)
        def _():
            for k in range(nt):
                local, remote = copies(k)
                local.wait()
                remote.wait_send()
            got = o_ref.at[pl.ds(pl.multiple_of((1 - c) * rh, 8), rh), :]
            _remote(got, got, ssem.at[0], rsem, sibling).wait_recv()

    return pl.pallas_call(
        body, name=name, grid=(nt,), out_shape=S((2 * rh, cols), F32),
        in_specs=[pl.BlockSpec((4, tr, cols), lambda i: (0, i, 0))], out_specs=ANY,
        scratch_shapes=[pltpu.VMEM((nt, tr, cols), F32), pltpu.SemaphoreType.DMA((nt,)), pltpu.SemaphoreType.DMA((nt,)),
                        pltpu.SemaphoreType.DMA],
        compiler_params=_params(("arbitrary",)))(slabs)


def _all_reduce_small(buf, name):
    shape = buf.shape

    def body(in_ref, out_ref, slots, ssem, rsem):
        x, y, c = _place()
        me = 4 * x + 2 * y + c
        slots[me] = in_ref[...]
        sends = []
        for r in range(1, 8):
            fx, fy, fc = (r >> 2) & 1, (r >> 1) & 1, r & 1
            cp = _remote(in_ref, slots.at[me], ssem.at[r - 1], rsem.at[r - 1], (_flip(x, fx), _flip(y, fy), _flip(c, fc)))
            cp.start()
            sends.append(cp)
        for r in range(1, 8):
            fx, fy, fc = (r >> 2) & 1, (r >> 1) & 1, r & 1
            px, py, pc = _flip(x, fx), _flip(y, fy), _flip(c, fc)
            _remote(in_ref, slots.at[4 * px + 2 * py + pc], ssem.at[r - 1], rsem.at[r - 1], (px, py, pc)).wait_recv()
        for cp in sends:
            cp.wait_send()
        acc = slots[0]
        for k in range(1, 8):
            acc = acc + slots[k]
        out_ref[...] = acc

    return pl.pallas_call(
        body, name=name, out_shape=S(shape, F32), in_specs=[VMEM_SPEC], out_specs=VMEM_SPEC,
        scratch_shapes=[pltpu.VMEM((8,) + shape, F32), pltpu.SemaphoreType.DMA((7,)), pltpu.SemaphoreType.DMA((7,))],
    )(buf)


BIG = ("ffn1_w_gu", "ffn1_w_down", "w_in", "w_out", "xattn_wq", "xattn_wkv", "xattn_wo", "ffn2_w_gu", "ffn2_w_down")
KIND = {"ffn1_w_gu": "col", "ffn1_w_down": "row", "w_in": "col", "w_out": "row", "xattn_wq": "row",
        "xattn_wkv": "col", "xattn_wo": "row", "ffn2_w_gu": "col", "ffn2_w_down": "row"}
WEIGHTS = ("rel_bias", "ffn1_norm", "ffn1_w_gu", "ffn1_w_down", "mix_norm", "w_in", "sinks", "conv_w", "w_out",
           "xattn_norm", "mem_norm", "xattn_wq", "xattn_wkv", "xattn_wo", "ffn2_norm", "ffn2_w_gu", "ffn2_w_down",
           "final_norm")
SMALL_ROWS = 16
GAIN_ROW = {"ffn1_norm": 0, "mix_norm": 1, "xattn_norm": 2, "mem_norm": 3, "ffn2_norm": 4, "final_norm": 5}
CONV_ROW, SINK_ROW, BIAS_ROW, LOSS_ROW = 6, 9, 10, 11


def _rows_block(rows, d):
    buf = jnp.zeros((SMALL_ROWS, d), F32)
    for r, v in rows.items():
        buf = lax.dynamic_update_slice(buf, v.reshape(1, -1).astype(F32), (r, 0))
    return buf


def _local_step(x, mem, pos, target, w, gains, rel_bias, sinks, conv_w, shards=None):
    t, d = x.shape
    dist = shards is not None
    w = dict(w)
    grads, slabs = {}, {}
    pos_col = pos.reshape(t, 1)
    pos_row = pos.reshape(1, t)
    bias_t = _bias_build(rel_bias).reshape(SWA_KV_HEADS, GROUP_ROWS, 2 * BLOCK)

    def gather(names):
        return _GatherHook([shards[k] for k in names], [KIND[k] for k in names]) if dist else None

    def gathered(names, hook):
        if dist:
            for k, gw in zip(names, hook.results):
                w[k] = gw if KIND[k] == "col" else gw.reshape(-1, gw.shape[-1])

    def dw(k, a, b, tn, name, **kw):
        return _dw_pair(a, b, tn, name, KIND[k], **kw) if dist else _dw(a, b, tn, name, **kw)

    def pair_sums(names):
        if not dist:
            return None
        parts = {}
        for k in names:
            mine, sib = grads[k]
            if KIND[k] == "row":
                mine, sib = mine.reshape(-1, mine.shape[-1]), sib.reshape(-1, sib.shape[-1])
            p = _add_bf16(mine, sib, "pair_sum_" + k)
            parts[k] = p if KIND[k] == "col" else p.reshape(4, -1, p.shape[-1])
        return parts

    def chips(names, parts):
        return _ChipsHook([parts[k] for k in names], [KIND[k] for k in names]) if dist else None

    def reduced(names, hook):
        if dist:
            slabs.update(zip(names, hook.results))

    names = ("w_in", "w_out")
    hook = gather(names)
    x1, g1, u1 = _ffn_fwd(x, gains["ffn1_norm"], w["ffn1_w_gu"], w["ffn1_w_down"], "ffn1_fwd", hook)
    gathered(names, hook)
    names = ("xattn_wq", "xattn_wkv", "xattn_wo")
    hook = gather(names)
    qkv, e = _mix_proj(x1, gains["mix_norm"], w["w_in"], hook)
    gathered(names, hook)
    names = ("ffn2_w_gu",)
    hook = gather(names)
    attn = _swa_fwd(qkv, pos_col, pos_row, bias_t, sinks, hook)
    gathered(names, hook)
    names = ("ffn2_w_down",)
    hook = gather(names)
    x2, merged = _mix_out_fwd(e, attn, conv_w, w["w_out"], x1, hook)
    gathered(names, hook)
    mh, kv = _mem_kv(mem, gains["mem_norm"], w["xattn_wkv"])
    x3, qx, o = _xattn_fwd(x2, gains["xattn_norm"], w["xattn_wq"], kv, w["xattn_wo"])
    x4, g2, u2 = _ffn_fwd(x3, gains["ffn2_norm"], w["ffn2_w_gu"], w["ffn2_w_down"], "ffn2_fwd")
    dx4, d_final, loss = _loss_bwd(x4, gains["final_norm"], target)

    dx3, d_ffn2, dgu2, a2, h4, dyh4 = _ffn_bwd(dx4, x3, gains["ffn2_norm"], g2, u2, w["ffn2_w_gu"], w["ffn2_w_down"],
                                               "ffn2_bwd")
    grads["ffn2_w_gu"] = dw("ffn2_w_gu", h4, dgu2, 1408, "dw_ffn2_gu")
    grads["ffn2_w_down"] = dw("ffn2_w_down", a2, dyh4, 512, "dw_ffn2_down")
    parts = pair_sums(("ffn2_w_gu", "ffn2_w_down"))
    hook = chips(("ffn2_w_gu",), parts)
    dx2, d_xattn, dkv, dxh3, h3, dqx = _xattn_bwd(dx3, x2, gains["xattn_norm"], qx, kv, w["xattn_wq"], w["xattn_wo"], hook)
    reduced(("ffn2_w_gu",), hook)
    grads["xattn_wo"] = dw("xattn_wo", o, dxh3, 1024, "dw_wo")
    grads["xattn_wq"] = dw("xattn_wq", h3, dqx, 1024, "dw_wq")
    grads["xattn_wkv"], d_mem = _mem_bwd(dkv, mh, mem, gains["mem_norm"], w["xattn_wkv"], dist)
    hook = chips(("ffn2_w_down",), parts)
    dattn, de, dcw, dxh2 = _mix_out_bwd(dx2, e, attn, conv_w, w["w_out"], hook)
    reduced(("ffn2_w_down",), hook)
    grads["w_out"] = dw("w_out", merged, dxh2, 1024, "dw_wout")
    names = ("xattn_wo", "xattn_wq", "xattn_wkv", "w_out")
    hook = chips(names, pair_sums(names))
    dq, dkvs, gb, dsk = _swa_bwd(qkv, dattn, pos_col, pos_row, bias_t, sinks, hook)
    reduced(names, hook)
    d_rel_bias, d_sinks = _bias_reduce(gb.reshape(SWA_HEADS, BLOCK, 2 * BLOCK), dsk.reshape(SWA_HEADS, BLOCK, 1))
    dx1, d_mix, h2 = _mix_in_bwd(dq, dkvs, de, x1, gains["mix_norm"], w["w_in"], dx2)
    n_in = w["w_in"].shape[1]
    gw = dw("w_in", h2, dq, 512, "dw_win_q", ncols=n_in)
    gw = dw("w_in", h2, dkvs, 512, "dw_win_kv", into=gw, col_block=dq.shape[1] // 512, ncols=n_in)
    grads["w_in"] = dw("w_in", h2, de, 512, "dw_win_e", into=gw, col_block=(dq.shape[1] + dkvs.shape[1]) // 512,
                       ncols=n_in)
    names = ("w_in",)
    hook = chips(names, pair_sums(names))
    dx0, d_ffn1, dgu1, a1, h1, dyh1 = _ffn_bwd(dx1, x, gains["ffn1_norm"], g1, u1, w["ffn1_w_gu"], w["ffn1_w_down"],
                                               "ffn1_bwd", hook)
    reduced(names, hook)
    grads["ffn1_w_down"] = dw("ffn1_w_down", a1, dyh1, 512, "dw_ffn1_down")
    names = ("ffn1_w_down",)
    hook = chips(names, pair_sums(names))
    grads["ffn1_w_gu"] = dw("ffn1_w_gu", h1, dgu1, 1408, "dw_ffn1_gu", **({"hook": hook} if dist else {}))
    reduced(names, hook)
    names = ("ffn1_w_gu",)
    hook = chips(names, pair_sums(names))
    if dist:
        hook.results = _exchange(hook, "reduce_chips_ffn1_gu")
    reduced(names, hook)

    rows = {0: d_ffn1, 1: d_mix, 2: d_xattn, 3: d_mem, 4: d_ffn2, 5: d_final, SINK_ROW: d_sinks, BIAS_ROW: d_rel_bias,
            LOSS_ROW: loss[0, 0:1]}
    rows.update({CONV_ROW + j: dcw[j] for j in range(3)})
    return dx0, (slabs if dist else grads), _rows_block(rows, d)


def kernel(x, mem, positions, rel_bias, ffn1_norm, ffn1_w_gu, ffn1_w_down, mix_norm, w_in, sinks, conv_w, w_out, xattn_norm, mem_norm, xattn_wq, xattn_wkv, xattn_wo, ffn2_norm, ffn2_w_gu, ffn2_w_down, final_norm, loss_target, m_rel_bias, m_ffn1_norm, m_ffn1_w_gu, m_ffn1_w_down, m_mix_norm, m_w_in, m_sinks, m_conv_w, m_w_out, m_xattn_norm, m_mem_norm, m_xattn_wq, m_xattn_wkv, m_xattn_wo, m_ffn2_norm, m_ffn2_w_gu, m_ffn2_w_down, m_final_norm, v_rel_bias, v_ffn1_norm, v_ffn1_w_gu, v_ffn1_w_down, v_mix_norm, v_w_in, v_sinks, v_conv_w, v_w_out, v_xattn_norm, v_mem_norm, v_xattn_wq, v_xattn_wkv, v_xattn_wo, v_ffn2_norm, v_ffn2_w_gu, v_ffn2_w_down, v_final_norm):
    args = dict(locals())
    wts = {k: args[k] for k in WEIGHTS}
    mom = {k: args["m_" + k] for k in WEIGHTS}
    var = {k: args["v_" + k] for k in WEIGHTS}
    d = x.shape[-1]
    s_me = 2 * lax.axis_index("x") + lax.axis_index("y")

    shards = {k: wts[k][0].astype(BF) for k in BIG}
    first = ("ffn1_w_gu", "ffn1_w_down")
    gathered = _exchange(_GatherHook([shards[k] for k in first], [KIND[k] for k in first]), "gather_ffn1")
    whole = {k: (gw if KIND[k] == "col" else gw.reshape(-1, gw.shape[-1])) for k, gw in zip(first, gathered)}
    cw_cols = conv_w.shape[-1]
    placed = lax.dynamic_update_slice(jnp.zeros((SMALL_ROWS, d), F32), 0.5 * conv_w[0], (0, s_me * cw_cols))
    conv_whole = _all_reduce_small(placed, "gather_conv_w")[0:3]

    gains = {k: wts[k].reshape(1, d) for k in GAIN_ROW}
    dx0, slabs, small = _local_step(x[0], mem[0], positions[0], loss_target[0], whole, gains, rel_bias, sinks,
                                    conv_whole, shards)

    shard_grads = {k: _sum_share(slabs[k], "sum_share_" + k) for k in BIG}

    small_sum = _all_reduce_small(small, "reduce_small")
    loss = small_sum[LOSS_ROW, 0]

    out_g, out_d, out_m, out_v = {}, {}, {}, {}
    for k in BIG:
        g2d = shard_grads[k]
        dl, mn, vn = _adamw(wts[k][0], g2d, mom[k][0], var[k][0], "adamw_" + k)
        out_g[k], out_d[k], out_m[k], out_v[k] = g2d[None], dl[None], mn[None], vn[None]

    conv_g = lax.dynamic_slice(small_sum, (CONV_ROW, s_me * cw_cols), (3, cw_cols))

    def pack(src, conv_block):
        rows = {r: src[k] for k, r in GAIN_ROW.items()}
        rows.update({CONV_ROW + j: conv_block[j] for j in range(3)})
        rows[SINK_ROW], rows[BIAS_ROW] = src["sinks"], src["rel_bias"]
        return _rows_block(rows, d)

    small_g = {k: small_sum[r] for k, r in GAIN_ROW.items()}
    small_g["sinks"] = small_sum[SINK_ROW, 0:sinks.size]
    small_g["rel_bias"] = small_sum[BIAS_ROW, 0:rel_bias.size]
    gp = pack(small_g, conv_g)
    dl, mn, vn = _adamw(pack(wts, conv_w[0]), gp, pack(mom, m_conv_w[0]), pack(var, v_conv_w[0]), "adamw_small")

    def unpack(buf, k):
        if k in GAIN_ROW:
            return buf[GAIN_ROW[k]].reshape(wts[k].shape)
        if k == "conv_w":
            return buf[CONV_ROW:CONV_ROW + 3, 0:cw_cols][None]
        if k == "sinks":
            return buf[SINK_ROW, 0:sinks.size].reshape(sinks.shape)
        return buf[BIAS_ROW, 0:rel_bias.size].reshape(rel_bias.shape)

    for k in WEIGHTS:
        if k not in KIND:
            out_g[k], out_d[k], out_m[k], out_v[k] = unpack(gp, k), unpack(dl, k), unpack(mn, k), unpack(vn, k)

    return (loss, dx0[None], *[out_g[k] for k in WEIGHTS], *[out_d[k] for k in WEIGHTS],
            *[out_m[k] for k in WEIGHTS], *[out_v[k] for k in WEIGHTS])
```

```python
import functools
import math

import jax
import jax.numpy as jnp
from jax import lax
from jax.experimental import pallas as pl
from jax.experimental.pallas import tpu as pltpu

BF = jnp.bfloat16
F32 = jnp.float32
I32 = jnp.int32
S = jax.ShapeDtypeStruct

EPS = 1e-6
NEG = -1e30
POS_PAD = 1 << 30
WINDOW = 128
BLOCK = 128
HEAD_DIM = 64
SWA_HEADS = 16
SWA_KV_HEADS = 4
SWA_GROUP = SWA_HEADS // SWA_KV_HEADS
MEM_HEADS = 4
REL_BUCKETS = 32
REL_MAX_DIST = 128
ADAM_LR = 0.001
ADAM_B1 = 0.9
ADAM_B2 = 0.999
ADAM_EPS = 1e-08
ADAM_WD = 0.01
ADAM_STEP = 10

V7X_VMEM_LIMIT_BYTES = 56 * 1024 * 1024
MESH = pl.DeviceIdType.MESH
ANY = pl.BlockSpec(memory_space=pl.ANY)
VMEM_SPEC = pl.BlockSpec(memory_space=pltpu.VMEM)
SMEM_SPEC = pl.BlockSpec(memory_space=pltpu.SMEM)


def _params(sem=None):
    return pltpu.CompilerParams(dimension_semantics=sem, vmem_limit_bytes=V7X_VMEM_LIMIT_BYTES)


def _resident(shape):
    nd = len(shape)
    return pl.BlockSpec(shape, lambda *_: (0,) * nd, pipeline_mode=pl.Buffered(1))


def _acc_spec(shape):
    nd = len(shape)
    return pl.BlockSpec(shape, lambda *_: (0,) * nd)


def _call(body, *, name, grid, out_shape, in_specs, out_specs, args, sem, scratch_shapes=(), aliases=None, hook=None):
    aliases = aliases or {}
    if hook is None:
        return pl.pallas_call(body, name=name, grid=grid, out_shape=out_shape, in_specs=in_specs, out_specs=out_specs,
                              scratch_shapes=list(scratch_shapes), input_output_aliases=aliases,
                              compiler_params=_params(sem))(*args)
    n_in, n_out, n_scr = len(in_specs), len(out_shape), len(scratch_shapes)
    h_in, h_out = len(hook.ins), len(hook.out_shape)

    def at_step(pick):
        conds = [pl.program_id(ax) == pick(size) for ax, size in enumerate(grid)]
        return functools.reduce(jnp.logical_and, conds)

    def hosted(*refs):
        k_in, x_in = refs[:n_in], refs[n_in:n_in + h_in]
        o0 = n_in + h_in
        k_out, x_out = refs[o0:o0 + n_out], refs[o0 + n_out:o0 + n_out + h_out]
        s0 = o0 + n_out + h_out
        k_scr, x_scr = refs[s0:s0 + n_scr], refs[s0 + n_scr:]

        @pl.when(at_step(lambda size: 0))
        def _():
            hook.start(x_in, x_out, x_scr)
        body(*k_in, *k_out, *k_scr)

        @pl.when(at_step(lambda size: size - 1))
        def _():
            hook.finish(x_in, x_out, x_scr)

    res = pl.pallas_call(
        hosted, name=name, grid=grid, out_shape=tuple(out_shape) + tuple(hook.out_shape),
        in_specs=list(in_specs) + [ANY] * h_in, out_specs=tuple(out_specs) + (ANY,) * h_out,
        scratch_shapes=list(scratch_shapes) + list(hook.scratch), input_output_aliases=aliases,
        compiler_params=_params(("arbitrary",) * len(grid)),
    )(*args, *hook.ins)
    hook.results = res[n_out:]
    return res[:n_out]


def _nn(a, b):
    return jnp.dot(a, b, preferred_element_type=F32)


def _nt(a, b):
    return lax.dot_general(a, b, (((1,), (1,)), ((), ())), preferred_element_type=F32)


def _tn(a, b):
    return lax.dot_general(a, b, (((0,), (0,)), ((), ())), preferred_element_type=F32)


def _sigmoid(v):
    return 1.0 / (1.0 + jnp.exp(-v))


def _rms(x):
    r = lax.rsqrt(jnp.mean(x * x, axis=-1, keepdims=True) + EPS)
    return x * r, r


def _rms_bwd(dh, n, r, g):
    dn = dh * g
    dx = r * (dn - n * jnp.mean(dn * n, axis=-1, keepdims=True))
    return dx, jnp.sum(dh * n, axis=0, keepdims=True)


def _ffn_fwd(x, gn, wgu, wd, name, hook=None):
    t, d = x.shape
    f = wd.shape[0]
    tm, fc = 256, 1408

    def body(x_ref, gn_ref, wgu_ref, wd_ref, xo_ref, g_ref, u_ref):
        xv = x_ref[...]
        n, _ = _rms(xv)
        h = (n * gn_ref[...]).astype(BF)
        acc = jnp.zeros((tm, d), F32)
        for c0 in range(0, f, fc):
            g = _nn(h, wgu_ref[:, c0:c0 + fc])
            u = _nn(h, wgu_ref[:, f + c0:f + c0 + fc])
            g_ref[:, c0:c0 + fc] = g.astype(BF)
            u_ref[:, c0:c0 + fc] = u.astype(BF)
            a = (g * _sigmoid(g)) * u
            acc = acc + _nn(a.astype(BF), wd_ref[c0:c0 + fc, :])
        xo_ref[...] = xv + 0.5 * acc

    return _call(
        body, name=name, grid=(t // tm,),
        out_shape=(S((t, d), F32), S((t, f), BF), S((t, f), BF)),
        in_specs=[pl.BlockSpec((tm, d), lambda i: (i, 0)), _resident((1, d)), _resident(wgu.shape), _resident(wd.shape)],
        out_specs=(pl.BlockSpec((tm, d), lambda i: (i, 0)), pl.BlockSpec((tm, f), lambda i: (i, 0)),
                   pl.BlockSpec((tm, f), lambda i: (i, 0))),
        sem=("parallel",), args=(x, gn, wgu, wd), hook=hook)


def _mix_proj(x, gn, w_in, hook=None):
    t, d = x.shape
    tm = 256
    nqkv = 1536
    ne = w_in.shape[1] - nqkv

    def body(x_ref, gn_ref, w_ref, qkv_ref, e_ref):
        n, _ = _rms(x_ref[...])
        h = (n * gn_ref[...]).astype(BF)
        qkv_ref[...] = _nn(h, w_ref[:, 0:nqkv]).astype(BF)
        for c0 in range(0, ne, 1024):
            e_ref[:, c0:c0 + 1024] = _nn(h, w_ref[:, nqkv + c0:nqkv + c0 + 1024]).astype(BF)

    return _call(
        body, name="mix_proj", grid=(t // tm,),
        out_shape=(S((t, nqkv), BF), S((t, ne), BF)),
        in_specs=[pl.BlockSpec((tm, d), lambda i: (i, 0)), _resident((1, d)), _resident(w_in.shape)],
        out_specs=(pl.BlockSpec((tm, nqkv), lambda i: (i, 0)), pl.BlockSpec((tm, ne), lambda i: (i, 0))),
        sem=("parallel",), args=(x, gn, w_in), hook=hook)


def _t5_bucket(rel):
    n = jnp.maximum(rel, 0)
    max_exact = REL_BUCKETS // 2
    nf = jnp.maximum(n, 1).astype(F32)
    large = max_exact + (jnp.log(nf / max_exact) / math.log(REL_MAX_DIST / max_exact)
                         * (REL_BUCKETS - max_exact)).astype(I32)
    large = jnp.minimum(large, REL_BUCKETS - 1)
    return jnp.where(n < max_exact, n, large)


def _block_rel():
    i = lax.broadcasted_iota(I32, (BLOCK, 2 * BLOCK), 0)
    j = lax.broadcasted_iota(I32, (BLOCK, 2 * BLOCK), 1)
    return i + BLOCK - j


def _bias_build(rel_bias):
    def body(rb_ref, o_ref):
        bucket = _t5_bucket(_block_rel())
        for h in range(SWA_HEADS):
            acc = jnp.zeros((BLOCK, 2 * BLOCK), F32)
            for b in range(REL_BUCKETS):
                acc = jnp.where(bucket == b, rb_ref[b, h], acc)
            o_ref[h] = acc

    return pl.pallas_call(
        body, name="bias_build", out_shape=S((SWA_HEADS, BLOCK, 2 * BLOCK), F32),
        in_specs=[SMEM_SPEC], out_specs=VMEM_SPEC,
    )(rel_bias)


GROUP_ROWS = SWA_GROUP * BLOCK


def _swa_visible(b, pq_ref, pkp_ref, pkc_ref):
    pk = jnp.concatenate([pkp_ref[...], pkc_ref[...]], axis=1)
    col = lax.broadcasted_iota(I32, (1, 2 * BLOCK), 1)
    pk = jnp.where(jnp.logical_and(b == 0, col < BLOCK), POS_PAD, pk)
    rel = jnp.concatenate([pq_ref[...]] * SWA_GROUP, axis=0) - pk
    return jnp.logical_and(rel >= 0, rel < WINDOW)


def _group_heads(ref, hk):
    h0 = hk * SWA_GROUP
    return jnp.concatenate([ref[:, (h0 + g) * HEAD_DIM:(h0 + g + 1) * HEAD_DIM] for g in range(SWA_GROUP)], axis=0)


def _group_sinks(sink_ref, hk):
    row = lax.broadcasted_iota(I32, (GROUP_ROWS, 1), 0)
    col = jnp.zeros((GROUP_ROWS, 1), F32) + sink_ref[0, hk * SWA_GROUP]
    for g in range(1, SWA_GROUP):
        col = jnp.where(row >= g * BLOCK, sink_ref[0, hk * SWA_GROUP + g], col)
    return col


def _swa_probs(qg, kh, vis, bias, sink):
    s = _nt(qg, kh) * (HEAD_DIM ** -0.5)
    s = jnp.where(vis, s + bias, NEG)
    m = jnp.maximum(jnp.max(s, axis=-1, keepdims=True), sink)
    p = jnp.exp(s - m)
    ps = jnp.exp(sink - m)
    inv = 1.0 / (jnp.sum(p, axis=-1, keepdims=True) + ps)
    return p * inv, ps * inv


def _swa_fwd(qkv, pos_col, pos_row, bias_t, sinks, hook=None):
    t = qkv.shape[0]
    nb = t // BLOCK
    qw = SWA_HEADS * HEAD_DIM
    kw = SWA_KV_HEADS * HEAD_DIM

    def body(q_ref, kp_ref, kc_ref, vp_ref, vc_ref, pq_ref, pkp_ref, pkc_ref, bias_ref, sink_ref, o_ref):
        b = pl.program_id(0)
        vis = _swa_visible(b, pq_ref, pkp_ref, pkc_ref)
        k2 = jnp.concatenate([kp_ref[...], kc_ref[...]], axis=0)
        v2 = jnp.concatenate([vp_ref[...], vc_ref[...]], axis=0)
        for hk in range(SWA_KV_HEADS):
            kh = k2[:, hk * HEAD_DIM:(hk + 1) * HEAD_DIM]
            vh = v2[:, hk * HEAD_DIM:(hk + 1) * HEAD_DIM]
            pn, _ = _swa_probs(_group_heads(q_ref, hk), kh, vis, bias_ref[hk], _group_sinks(sink_ref, hk))
            o = _nn(pn.astype(BF), vh)
            for g in range(SWA_GROUP):
                h = hk * SWA_GROUP + g
                o_ref[:, h * HEAD_DIM:(h + 1) * HEAD_DIM] = o[g * BLOCK:(g + 1) * BLOCK]

    prev = lambda b: jnp.maximum(b - 1, 0)
    return _call(
        body, name="swa_fwd", grid=(nb,), out_shape=(S((t, qw), F32),),
        in_specs=[
            pl.BlockSpec((BLOCK, qw), lambda b: (b, 0)),
            pl.BlockSpec((BLOCK, kw), lambda b: (prev(b), qw // kw)),
            pl.BlockSpec((BLOCK, kw), lambda b: (b, qw // kw)),
            pl.BlockSpec((BLOCK, kw), lambda b: (prev(b), qw // kw + 1)),
            pl.BlockSpec((BLOCK, kw), lambda b: (b, qw // kw + 1)),
            pl.BlockSpec((BLOCK, 1), lambda b: (b, 0)),
            pl.BlockSpec((1, BLOCK), lambda b: (0, prev(b))),
            pl.BlockSpec((1, BLOCK), lambda b: (0, b)),
            _resident(bias_t.shape),
            SMEM_SPEC,
        ],
        out_specs=(pl.BlockSpec((BLOCK, qw), lambda b: (b, 0)),),
        sem=("parallel",), args=(qkv, qkv, qkv, qkv, qkv, pos_col, pos_row, pos_row, bias_t, sinks), hook=hook)[0]


HALO = 16


def _conv_taps(z, zh, first):
    tm = z.shape[0]
    zh = jnp.where(first, 0.0, zh)
    row = lax.broadcasted_iota(I32, (tm, 1), 0)
    z1 = jnp.where(row == 0, zh[HALO - 1:HALO, :], pltpu.roll(z, 1, 0))
    z2 = jnp.where(row == 0, zh[HALO - 2:HALO - 1, :], jnp.where(row == 1, zh[HALO - 1:HALO, :], pltpu.roll(z, 2, 0)))
    return z1, z2


def _mix_out_fwd(e, attn, conv_w, w_out, x, hook=None):
    t, d = x.shape
    tm = 256
    hb = tm // HALO
    f32 = lambda ref: ref[...].astype(F32)

    def body(c_ref, b_ref, u_ref, ga_ref, gc_ref, ch_ref, uh_ref, attn_ref, cw_ref, w_ref, x_ref, xo_ref, mg_ref):
        i = pl.program_id(0)
        z = f32(c_ref) * f32(u_ref)
        z1, z2 = _conv_taps(z, f32(ch_ref) * f32(uh_ref), i == 0)
        s = cw_ref[0:1, :] * z2 + cw_ref[1:2, :] * z1 + cw_ref[2:3, :] * z
        conv = f32(b_ref) * s
        merged = (_sigmoid(f32(ga_ref)) * attn_ref[...] + _sigmoid(f32(gc_ref)) * conv).astype(BF)
        mg_ref[...] = merged
        xo_ref[...] = x_ref[...] + _nn(merged, w_ref[...])

    ecol = lambda cb: pl.BlockSpec((tm, d), lambda i: (i, cb))
    halo = lambda cb: pl.BlockSpec((HALO, d), lambda i: (jnp.maximum(i * hb - 1, 0), cb))
    row = pl.BlockSpec((tm, d), lambda i: (i, 0))
    return _call(
        body, name="mix_out_fwd", grid=(t // tm,),
        out_shape=(S((t, d), F32), S((t, d), BF)),
        in_specs=[ecol(0), ecol(1), ecol(2), ecol(3), ecol(4), halo(0), halo(2), row,
                  _resident(conv_w.shape), _resident(w_out.shape), row],
        out_specs=(row, row),
        sem=("parallel",), args=(e, e, e, e, e, e, e, attn, conv_w, w_out, x), hook=hook)


def _mem_kv(mem, gm, wkv):
    m, d = mem.shape

    def body(mem_ref, gm_ref, w_ref, mh_ref, kv_ref):
        n, _ = _rms(mem_ref[...])
        mh = (n * gm_ref[...]).astype(BF)
        mh_ref[...] = mh
        kv_ref[...] = _nn(mh, w_ref[...]).astype(BF)

    return pl.pallas_call(
        body, name="mem_kv", out_shape=(S((m, d), BF), S((m, wkv.shape[1]), BF)),
        compiler_params=_params(),
    )(mem, gm, wkv)


def _xattn_probs(qh, kh):
    s = _nt(qh, kh) * (kh.shape[1] ** -0.5)
    p = jnp.exp(s - jnp.max(s, axis=-1, keepdims=True))
    return p * (1.0 / jnp.sum(p, axis=-1, keepdims=True))


def _xattn_fwd(x, gn, wq, kv, wo):
    t, d = x.shape
    tm = 256
    hd = d // MEM_HEADS

    def body(x_ref, gn_ref, wq_ref, kv_ref, wo_ref, xo_ref, q_ref, o_ref):
        xv = x_ref[...]
        n, _ = _rms(xv)
        q = _nn((n * gn_ref[...]).astype(BF), wq_ref[...]).astype(BF)
        q_ref[...] = q
        outs = []
        for hh in range(MEM_HEADS):
            p = _xattn_probs(q[:, hh * hd:(hh + 1) * hd], kv_ref[:, hh * hd:(hh + 1) * hd])
            outs.append(_nn(p.astype(BF), kv_ref[:, d + hh * hd:d + (hh + 1) * hd]))
        o = jnp.concatenate(outs, axis=1).astype(BF)
        o_ref[...] = o
        xo_ref[...] = xv + _nn(o, wo_ref[...])

    row = pl.BlockSpec((tm, d), lambda i: (i, 0))
    return pl.pallas_call(
        body, name="xattn_fwd", grid=(t // tm,),
        out_shape=(S((t, d), F32), S((t, d), BF), S((t, d), BF)),
        in_specs=[row, _resident((1, d)), _resident(wq.shape), _resident(kv.shape), _resident(wo.shape)],
        out_specs=(row, row, row),
        compiler_params=_params(("parallel",)),
    )(x, gn, wq, kv, wo)


def _loss_bwd(x, gf, target):
    t, d = x.shape
    tm = 512

    def body(x_ref, gf_ref, t_ref, dx_ref, dg_ref, loss_ref):
        @pl.when(pl.program_id(0) == 0)
        def _():
            dg_ref[...] = jnp.zeros_like(dg_ref)
            loss_ref[...] = jnp.zeros_like(loss_ref)
        n, r = _rms(x_ref[...])
        g = gf_ref[...]
        err = n * g - t_ref[...]
        loss_ref[...] += 0.5 * jnp.sum(jnp.sum(err * err, axis=-1, keepdims=True) / d, axis=0, keepdims=True)
        dx, dg = _rms_bwd(err / d, n, r, g)
        dx_ref[...] = dx
        dg_ref[...] += dg

    row = pl.BlockSpec((tm, d), lambda i: (i, 0))
    return pl.pallas_call(
        body, name="loss_bwd", grid=(t // tm,),
        out_shape=(S((t, d), F32), S((1, d), F32), S((1, 128), F32)),
        in_specs=[row, _resident((1, d)), row],
        out_specs=(row, _acc_spec((1, d)), _acc_spec((1, 128))),
        compiler_params=_params(("arbitrary",)),
    )(x, gf, target)


def _ffn_bwd(dxo, x, gn, g, u, wgu, wd, name, hook=None):
    t, d = x.shape
    f = wd.shape[0]
    tm, fc = 256, 1408

    def body(dxo_ref, x_ref, gn_ref, g_ref, u_ref, wgu_ref, wd_ref, dx_ref, dgn_ref, dgu_ref, a_ref, h_ref, dyh_ref):
        @pl.when(pl.program_id(0) == 0)
        def _():
            dgn_ref[...] = jnp.zeros_like(dgn_ref)
        dxov = dxo_ref[...]
        dyh = (0.5 * dxov).astype(BF)
        dyh_ref[...] = dyh
        n, r = _rms(x_ref[...])
        gnv = gn_ref[...]
        h_ref[...] = (n * gnv).astype(BF)
        dh = jnp.zeros((tm, d), F32)
        for c0 in range(0, f, fc):
            gv = g_ref[:, c0:c0 + fc].astype(F32)
            uv = u_ref[:, c0:c0 + fc].astype(F32)
            da = _nt(dyh, wd_ref[c0:c0 + fc, :])
            sg = _sigmoid(gv)
            silu = gv * sg
            a_ref[:, c0:c0 + fc] = (silu * uv).astype(BF)
            dg = (da * uv * (sg * (1.0 + gv * (1.0 - sg)))).astype(BF)
            du = (da * silu).astype(BF)
            dgu_ref[:, c0:c0 + fc] = dg
            dgu_ref[:, f + c0:f + c0 + fc] = du
            dh = dh + _nt(dg, wgu_ref[:, c0:c0 + fc]) + _nt(du, wgu_ref[:, f + c0:f + c0 + fc])
        dx, dgn = _rms_bwd(dh, n, r, gnv)
        dx_ref[...] = dxov + dx
        dgn_ref[...] += dgn

    row = pl.BlockSpec((tm, d), lambda i: (i, 0))
    frow = pl.BlockSpec((tm, f), lambda i: (i, 0))
    return _call(
        body, name=name, grid=(t // tm,),
        out_shape=(S((t, d), F32), S((1, d), F32), S((t, 2 * f), BF), S((t, f), BF), S((t, d), BF), S((t, d), BF)),
        in_specs=[row, row, _resident((1, d)), frow, frow, _resident(wgu.shape), _resident(wd.shape)],
        out_specs=(row, _acc_spec((1, d)), pl.BlockSpec((tm, 2 * f), lambda i: (i, 0)), frow, row, row),
        sem=("arbitrary",), args=(dxo, x, gn, g, u, wgu, wd), hook=hook)


def _dw(a, b, tn, name, into=None, col_block=0, ncols=None):
    t, ka = a.shape
    nb = b.shape[1]
    tt = 1024
    nt = t // tt
    ncols = nb if ncols is None else ncols

    def body(*refs):
        a_ref, b_ref = refs[0], refs[1]
        o_ref, acc_ref = refs[-2], refs[-1]
        k = pl.program_id(1)

        @pl.when(k == 0)
        def _():
            acc_ref[...] = jnp.zeros_like(acc_ref)
        acc_ref[...] += _tn(a_ref[...], b_ref[...])

        @pl.when(k == nt - 1)
        def _():
            o_ref[...] = acc_ref[...].astype(BF)

    in_specs = [pl.BlockSpec((tt, ka), lambda j, k: (k, 0)), pl.BlockSpec((tt, tn), lambda j, k: (k, j))]
    args = [a, b]
    aliases = {}
    if into is not None:
        in_specs.append(ANY)
        args.append(into)
        aliases = {2: 0}
    return pl.pallas_call(
        body, name=name, grid=(nb // tn, nt), out_shape=S((ka, ncols), BF),
        in_specs=in_specs, out_specs=pl.BlockSpec((ka, tn), lambda j, k: (0, j + col_block)),
        scratch_shapes=[pltpu.VMEM((ka, tn), F32)], input_output_aliases=aliases,
        compiler_params=_params(("parallel", "arbitrary")),
    )(*args)


def _dw_pair(a, b, tn, name, kind, into=None, col_block=0, ncols=None, hook=None):
    t, ka = a.shape
    nb = b.shape[1]
    tt = 2048
    nt, nj = t // tt, nb // tn
    ncols = nb if ncols is None else ncols
    col = kind == "col"
    rh = ka // 2 if col else ka // 8
    tile = (rh, tn) if col else (4, rh, tn)
    half = (rh, ncols) if col else (4, rh, ncols)
    lead = (slice(None),) * (len(tile) - 1)

    def body(*refs):
        a_ref, b_ref = refs[0], refs[1]
        mine_ref, sib_ref, acc_ref, stage, ssem, rsem = refs[-6:]
        j, k = pl.program_id(0), pl.program_id(1)
        x, y, c = _place()
        sibling = (x, y, 1 - c)

        @pl.when(k == 0)
        def _():
            acc_ref[...] = jnp.zeros_like(acc_ref)
        acc_ref[...] += _tn(a_ref[...], b_ref[...])

        def send(slot, jj):
            dst = sib_ref.at[lead + (pl.ds(pl.multiple_of((jj + col_block) * tn, 128), tn),)]
            return _remote(stage.at[slot], dst, ssem.at[slot], rsem, sibling)

        def rows(s, whose):
            return acc_ref[pl.ds(pl.multiple_of(s * 2 * rh + whose * rh, 16), rh), :].astype(BF)

        @pl.when(k == nt - 1)
        def _():
            slot = j % 2

            @pl.when(j >= 2)
            def _():
                send(slot, j).wait_send()
            if col:
                mine_ref[...] = rows(0, c)
                stage[slot] = rows(0, 1 - c)
            else:
                for s in range(4):
                    mine_ref[s] = rows(s, c)
                    stage[slot, s] = rows(s, 1 - c)
            send(slot, j).start()

        @pl.when(jnp.logical_and(j == nj - 1, k == nt - 1))
        def _():
            for jj in range(max(nj - 2, 0), nj):
                send(jj % 2, jj).wait_send()
            got = sib_ref.at[lead + (pl.ds(col_block * tn, nb),)]
            _remote(got, got, ssem.at[0], rsem, sibling).wait_recv()

    in_specs = [pl.BlockSpec((tt, ka), lambda j, k: (k, 0)), pl.BlockSpec((tt, tn), lambda j, k: (k, j))]
    args = [a, b]
    aliases = {}
    if into is not None:
        in_specs += [ANY, ANY]
        args += list(into)
        aliases = {2: 0, 3: 1}
    mine_spec = pl.BlockSpec(tile, (lambda j, k: (0, j + col_block)) if col else (lambda j, k: (0, 0, j + col_block)))
    return _call(
        body, name=name, grid=(nj, nt), out_shape=(S(half, BF), S(half, BF)),
        in_specs=in_specs, out_specs=(mine_spec, ANY),
        scratch_shapes=[pltpu.VMEM((ka, tn), F32), pltpu.VMEM((2,) + tile, BF), pltpu.SemaphoreType.DMA((2,)),
                        pltpu.SemaphoreType.DMA],
        aliases=aliases, sem=("arbitrary", "arbitrary"), args=args, hook=hook)


def _xattn_bwd(dxo, x, gn, q, kv, wq, wo, hook=None):
    t, d = x.shape
    tm = 256
    hd = d // MEM_HEADS
    nkv = kv.shape[0]

    def body(dxo_ref, x_ref, gn_ref, q_ref, kv_ref, wq_ref, wo_ref, dx_ref, dgn_ref, dkv_ref, dxh_ref, h_ref, dq_ref):
        @pl.when(pl.program_id(0) == 0)
        def _():
            dgn_ref[...] = jnp.zeros_like(dgn_ref)
            dkv_ref[...] = jnp.zeros_like(dkv_ref)
        dxov = dxo_ref[...]
        dxh = dxov.astype(BF)
        dxh_ref[...] = dxh
        do = _nt(dxh, wo_ref[...]).astype(BF)
        dqs = []
        for hh in range(MEM_HEADS):
            lo, hi = hh * hd, (hh + 1) * hd
            qh = q_ref[:, lo:hi]
            kh = kv_ref[:, lo:hi]
            vh = kv_ref[:, d + lo:d + hi]
            doh = do[:, lo:hi]
            p = _xattn_probs(qh, kh)
            dp = _nt(doh, vh)
            ds = (p * (dp - jnp.sum(p * dp, axis=-1, keepdims=True)) * (hd ** -0.5)).astype(BF)
            dqs.append(_nn(ds, kh))
            dkv_ref[:, lo:hi] += _tn(ds, qh)
            dkv_ref[:, d + lo:d + hi] += _tn(p.astype(BF), doh)
        dq = jnp.concatenate(dqs, axis=1).astype(BF)
        dq_ref[...] = dq
        n, r = _rms(x_ref[...])
        gnv = gn_ref[...]
        h_ref[...] = (n * gnv).astype(BF)
        dx, dgn = _rms_bwd(_nt(dq, wq_ref[...]), n, r, gnv)
        dx_ref[...] = dxov + dx
        dgn_ref[...] += dgn

    row = pl.BlockSpec((tm, d), lambda i: (i, 0))
    return _call(
        body, name="xattn_bwd", grid=(t // tm,),
        out_shape=(S((t, d), F32), S((1, d), F32), S((nkv, 2 * d), F32), S((t, d), BF), S((t, d), BF), S((t, d), BF)),
        in_specs=[row, row, _resident((1, d)), row, _resident(kv.shape), _resident(wq.shape), _resident(wo.shape)],
        out_specs=(row, _acc_spec((1, d)), _acc_spec((nkv, 2 * d)), row, row, row),
        sem=("arbitrary",), args=(dxo, x, gn, q, kv, wq, wo), hook=hook)


def _mem_bwd(dkv, mh, mem, gm, wkv, pair=False):
    m, d = mem.shape
    rows, cols = wkv.shape
    rh = rows // 2

    def body(dkv_ref, mh_ref, mem_ref, gm_ref, w_ref, *outs):
        dkvb = dkv_ref[...].astype(BF)
        dw = _tn(mh_ref[...], dkvb).astype(BF)
        dmh = _nt(dkvb, w_ref[...])
        n, _ = _rms(mem_ref[...])
        if not pair:
            dw_ref, dgm_ref = outs
            dw_ref[...] = dw
        else:
            mine_ref, sib_ref, dgm_ref, whole, ssem, rsem = outs
            x, y, c = _place()
            whole[...] = dw
            cp = _remote(whole.at[pl.ds(pl.multiple_of((1 - c) * rh, 16), rh), :], sib_ref, ssem, rsem, (x, y, 1 - c))
            cp.start()
            mine_ref[...] = whole[pl.ds(pl.multiple_of(c * rh, 16), rh), :]
            cp.wait()
        dgm_ref[...] = jnp.sum(dmh * n, axis=0, keepdims=True)

    if not pair:
        return pl.pallas_call(body, name="mem_bwd", out_shape=(S(wkv.shape, BF), S((1, d), F32)),
                              compiler_params=_params())(dkv, mh, mem, gm, wkv)
    mine, sib, dgm = pl.pallas_call(
        body, name="mem_bwd", out_shape=(S((rh, cols), BF), S((rh, cols), BF), S((1, d), F32)),
        in_specs=[VMEM_SPEC] * 5, out_specs=(VMEM_SPEC, ANY, VMEM_SPEC),
        scratch_shapes=[pltpu.VMEM((rows, cols), BF), pltpu.SemaphoreType.DMA, pltpu.SemaphoreType.DMA],
        compiler_params=_params())(dkv, mh, mem, gm, wkv)
    return (mine, sib), dgm


def _mix_out_bwd(dxo, e, attn, conv_w, w_out, hook=None):
    t, d = attn.shape
    tm = 256
    nt = t // tm
    f32 = lambda ref: ref[...].astype(F32)

    def body(dxo_ref, dxn_ref, c_ref, b_ref, u_ref, ga_ref, gc_ref, ch_ref, uh_ref, bn_ref, gcn_ref,
             attn_ref, cw_ref, w_ref, dattn_ref, de_ref, dcw_ref, dxh_ref):
        i = pl.program_id(0)

        @pl.when(i == 0)
        def _():
            dcw_ref[...] = jnp.zeros_like(dcw_ref)
        dxh = dxo_ref[...].astype(BF)
        dxh_ref[...] = dxh
        w = w_ref[...]
        dm = _nt(dxh, w)
        dmn = _nt(dxn_ref[...].astype(BF), w)
        cv, bv, uv = f32(c_ref), f32(b_ref), f32(u_ref)
        sga = _sigmoid(f32(ga_ref))
        sgc = _sigmoid(f32(gc_ref))
        z = cv * uv
        z1, z2 = _conv_taps(z, f32(ch_ref) * f32(uh_ref), i == 0)
        w0, w1, w2 = cw_ref[0:1, :], cw_ref[1:2, :], cw_ref[2:3, :]
        s = w0 * z2 + w1 * z1 + w2 * z
        av = attn_ref[...]
        dattn_ref[...] = (dm * sga).astype(BF)
        dconv = dm * sgc
        ds = dconv * bv
        dsn = jnp.where(i == nt - 1, 0.0, dmn * _sigmoid(gcn_ref[0:8, :].astype(F32)) * bn_ref[0:8, :].astype(F32))
        row = lax.broadcasted_iota(I32, (tm, 1), 0)
        dsp1 = jnp.where(row == tm - 1, dsn[0:1, :], pltpu.roll(ds, tm - 1, 0))
        dsp2 = jnp.where(row == tm - 2, dsn[0:1, :], jnp.where(row == tm - 1, dsn[1:2, :], pltpu.roll(ds, tm - 2, 0)))
        dz = w2 * ds + w1 * dsp1 + w0 * dsp2
        de_ref[:, 0:d] = (dz * uv).astype(BF)
        de_ref[:, d:2 * d] = (dconv * s).astype(BF)
        de_ref[:, 2 * d:3 * d] = (dz * cv).astype(BF)
        de_ref[:, 3 * d:4 * d] = (dm * av * sga * (1.0 - sga)).astype(BF)
        de_ref[:, 4 * d:5 * d] = (dm * (bv * s) * sgc * (1.0 - sgc)).astype(BF)
        dcw_ref[0:1, :] += jnp.sum(ds * z2, axis=0, keepdims=True)
        dcw_ref[1:2, :] += jnp.sum(ds * z1, axis=0, keepdims=True)
        dcw_ref[2:3, :] += jnp.sum(ds * z, axis=0, keepdims=True)

    ecol = lambda cb: pl.BlockSpec((tm, d), lambda i: (i, cb))
    prev = lambda cb: pl.BlockSpec((HALO, d), lambda i: (jnp.maximum(i * (tm // HALO) - 1, 0), cb))
    nxt = lambda rows, cb: pl.BlockSpec((rows, d), lambda i: (jnp.minimum((i + 1) * (tm // rows), t // rows - 1), cb))
    row = pl.BlockSpec((tm, d), lambda i: (i, 0))
    return _call(
        body, name="mix_out_bwd", grid=(nt,),
        out_shape=(S((t, d), BF), S((t, 5 * d), BF), S((8, d), F32), S((t, d), BF)),
        in_specs=[row, nxt(8, 0), ecol(0), ecol(1), ecol(2), ecol(3), ecol(4), prev(0), prev(2), nxt(HALO, 1), nxt(HALO, 4),
                  row, _resident(conv_w.shape), _resident(w_out.shape)],
        out_specs=(row, pl.BlockSpec((tm, 5 * d), lambda i: (i, 0)), _acc_spec((8, d)), row),
        sem=("arbitrary",), args=(dxo, dxo, e, e, e, e, e, e, e, e, e, attn, conv_w, w_out), hook=hook)


def _swa_bwd(qkv, dattn, pos_col, pos_row, bias_t, sinks, hook=None):
    t = qkv.shape[0]
    nb = t // BLOCK
    qw = SWA_HEADS * HEAD_DIM
    kw = SWA_KV_HEADS * HEAD_DIM

    def body(q_ref, kp_ref, kc_ref, vp_ref, vc_ref, do_ref, pq_ref, pkp_ref, pkc_ref, bias_ref, sink_ref,
             dq_ref, dkv_ref, gb_ref, dsk_ref, carry_ref):
        b = pl.program_id(0)

        @pl.when(b == 0)
        def _():
            gb_ref[...] = jnp.zeros_like(gb_ref)
            dsk_ref[...] = jnp.zeros_like(dsk_ref)
            carry_ref[...] = jnp.zeros_like(carry_ref)

        @pl.when(b < nb)
        def _():
            vis = _swa_visible(b, pq_ref, pkp_ref, pkc_ref)
            k2 = jnp.concatenate([kp_ref[...], kc_ref[...]], axis=0)
            v2 = jnp.concatenate([vp_ref[...], vc_ref[...]], axis=0)
            for hk in range(SWA_KV_HEADS):
                lo, hi = hk * HEAD_DIM, (hk + 1) * HEAD_DIM
                kh = k2[:, lo:hi]
                vh = v2[:, lo:hi]
                qg = _group_heads(q_ref, hk)
                dog = _group_heads(do_ref, hk)
                pn, psn = _swa_probs(qg, kh, vis, bias_ref[hk], _group_sinks(sink_ref, hk))
                dp = _nt(dog, vh)
                delta = jnp.sum(pn * dp, axis=-1, keepdims=True)
                ds = pn * (dp - delta)
                gb_ref[hk] += ds
                dsk_ref[hk] += -psn * delta
                dsb = (ds * (HEAD_DIM ** -0.5)).astype(BF)
                dqg = _nn(dsb, kh).astype(BF)
                for g in range(SWA_GROUP):
                    h = hk * SWA_GROUP + g
                    dq_ref[:, h * HEAD_DIM:(h + 1) * HEAD_DIM] = dqg[g * BLOCK:(g + 1) * BLOCK]
                dk = _tn(dsb, qg)
                dv = _tn(pn.astype(BF), dog)
                dkv_ref[:, lo:hi] = (carry_ref[:, lo:hi] + dk[0:BLOCK]).astype(BF)
                dkv_ref[:, kw + lo:kw + hi] = (carry_ref[:, kw + lo:kw + hi] + dv[0:BLOCK]).astype(BF)
                carry_ref[:, lo:hi] = dk[BLOCK:2 * BLOCK]
                carry_ref[:, kw + lo:kw + hi] = dv[BLOCK:2 * BLOCK]

        @pl.when(b == nb)
        def _():
            dkv_ref[...] = carry_ref[...].astype(BF)

    cur = lambda b: jnp.minimum(b, nb - 1)
    prev = lambda b: jnp.maximum(cur(b) - 1, 0)
    return _call(
        body, name="swa_bwd", grid=(nb + 1,),
        out_shape=(S((t, qw), BF), S((t, 2 * kw), BF), S((SWA_KV_HEADS, GROUP_ROWS, 2 * BLOCK), F32),
                   S((SWA_KV_HEADS, GROUP_ROWS, 1), F32)),
        in_specs=[
            pl.BlockSpec((BLOCK, qw), lambda b: (cur(b), 0)),
            pl.BlockSpec((BLOCK, kw), lambda b: (prev(b), qw // kw)),
            pl.BlockSpec((BLOCK, kw), lambda b: (cur(b), qw // kw)),
            pl.BlockSpec((BLOCK, kw), lambda b: (prev(b), qw // kw + 1)),
            pl.BlockSpec((BLOCK, kw), lambda b: (cur(b), qw // kw + 1)),
            pl.BlockSpec((BLOCK, qw), lambda b: (cur(b), 0)),
            pl.BlockSpec((BLOCK, 1), lambda b: (cur(b), 0)),
            pl.BlockSpec((1, BLOCK), lambda b: (0, prev(b))),
            pl.BlockSpec((1, BLOCK), lambda b: (0, cur(b))),
            _resident(bias_t.shape),
            SMEM_SPEC,
        ],
        out_specs=(
            pl.BlockSpec((BLOCK, qw), lambda b: (cur(b), 0)),
            pl.BlockSpec((BLOCK, 2 * kw), lambda b: (jnp.maximum(b - 1, 0), 0)),
            _acc_spec((SWA_KV_HEADS, GROUP_ROWS, 2 * BLOCK)),
            _acc_spec((SWA_KV_HEADS, GROUP_ROWS, 1)),
        ),
        scratch_shapes=[pltpu.VMEM((BLOCK, 2 * kw), F32)],
        sem=("arbitrary",), args=(qkv, qkv, qkv, qkv, qkv, dattn, pos_col, pos_row, pos_row, bias_t, sinks), hook=hook)


def _bias_reduce(gb, dsk):
    def body(gb_ref, dsk_ref, drb_ref, dsink_ref):
        bucket = _t5_bucket(_block_rel())
        for b in range(REL_BUCKETS):
            mask = bucket == b
            for h in range(SWA_HEADS):
                drb_ref[b, h] = jnp.sum(jnp.where(mask, gb_ref[h], 0.0))
        for h in range(SWA_HEADS):
            dsink_ref[0, h] = jnp.sum(dsk_ref[h])

    return pl.pallas_call(
        body, name="bias_reduce", out_shape=(S((REL_BUCKETS, SWA_HEADS), F32), S((1, SWA_HEADS), F32)),
        in_specs=[VMEM_SPEC, VMEM_SPEC], out_specs=(SMEM_SPEC, SMEM_SPEC),
    )(gb, dsk)


def _mix_in_bwd(dq, dkv, de, x, gn, w_in, dxo):
    t, d = x.shape
    tm = 256
    nq, nkv, ne = dq.shape[1], dkv.shape[1], de.shape[1]

    def body(dq_ref, dkv_ref, de_ref, x_ref, gn_ref, w_ref, dxo_ref, dx_ref, dgn_ref, h_ref):
        @pl.when(pl.program_id(0) == 0)
        def _():
            dgn_ref[...] = jnp.zeros_like(dgn_ref)
        dh = _nt(dq_ref[...], w_ref[:, 0:nq]) + _nt(dkv_ref[...], w_ref[:, nq:nq + nkv])
        for c0 in range(0, ne, 1024):
            dh = dh + _nt(de_ref[:, c0:c0 + 1024], w_ref[:, nq + nkv + c0:nq + nkv + c0 + 1024])
        n, r = _rms(x_ref[...])
        gnv = gn_ref[...]
        h_ref[...] = (n * gnv).astype(BF)
        dx, dgn = _rms_bwd(dh, n, r, gnv)
        dx_ref[...] = dxo_ref[...] + dx
        dgn_ref[...] += dgn

    row = pl.BlockSpec((tm, d), lambda i: (i, 0))
    wide = lambda w: pl.BlockSpec((tm, w), lambda i: (i, 0))
    return pl.pallas_call(
        body, name="mix_in_bwd", grid=(t // tm,),
        out_shape=(S((t, d), F32), S((1, d), F32), S((t, d), BF)),
        in_specs=[wide(nq), wide(nkv), wide(ne), row, _resident((1, d)), _resident(w_in.shape), row],
        out_specs=(row, _acc_spec((1, d)), row),
        compiler_params=_params(("arbitrary",)),
    )(dq, dkv, de, x, gn, w_in, dxo)


def _row_tile(rows):
    for tr in (256, 352, 128, 64, 16, 8):
        if rows % tr == 0:
            return tr
    return rows


def _add_bf16(a, b, name):
    rows, cols = a.shape
    tr = _row_tile(rows)

    def body(a_ref, b_ref, o_ref):
        o_ref[...] = (a_ref[...].astype(F32) + b_ref[...].astype(F32)).astype(BF)

    blk = pl.BlockSpec((tr, cols), lambda i: (i, 0))
    return pl.pallas_call(body, name=name, grid=(rows // tr,), out_shape=S((rows, cols), BF),
                          in_specs=[blk, blk], out_specs=blk, compiler_params=_params(("parallel",)))(a, b)


def _adamw(w, g, m, v, name, echo=False):
    rows, cols = w.shape
    tr = _row_tile(rows)

    def body(w_ref, g_ref, m_ref, v_ref, *outs):
        d_ref, mo_ref, vo_ref = outs[-3:]
        gv = g_ref[...]
        if echo:
            outs[0][...] = gv
        mn = ADAM_B1 * m_ref[...] + (1.0 - ADAM_B1) * gv
        vn = ADAM_B2 * v_ref[...] + (1.0 - ADAM_B2) * (gv * gv)
        m_hat = mn / (1.0 - ADAM_B1 ** ADAM_STEP)
        v_hat = vn / (1.0 - ADAM_B2 ** ADAM_STEP)
        d_ref[...] = -ADAM_LR * (m_hat / (jnp.sqrt(v_hat) + ADAM_EPS) + ADAM_WD * w_ref[...])
        mo_ref[...] = mn
        vo_ref[...] = vn

    blk = pl.BlockSpec((tr, cols), lambda i: (i, 0))
    shp = S((rows, cols), F32)
    n_out = 4 if echo else 3
    return pl.pallas_call(body, name=name, grid=(rows // tr,), out_shape=(shp,) * n_out,
                          in_specs=[blk] * 4, out_specs=(blk,) * n_out, compiler_params=_params(("parallel",)))(w, g, m, v)


def _place():
    x, y, c = lax.axis_index("x"), lax.axis_index("y"), lax.axis_index("c")
    return x, y, c


OTHER_CHIPS = ((1, 0), (0, 1), (1, 1))


def _flip(v, f):
    return 1 - v if f else v


def _remote(src, dst, ssem, rsem, dev):
    return pltpu.make_async_remote_copy(src_ref=src, dst_ref=dst, send_sem=ssem, recv_sem=rsem,
                                        device_id=dev, device_id_type=MESH)


def _exchange(hooks, name):
    n_in = [len(h.ins) for h in hooks]
    n_out = [len(h.out_shape) for h in hooks]
    n_scr = [len(h.scratch) for h in hooks]

    def body(*refs):
        ins, outs, scr = refs[:sum(n_in)], refs[sum(n_in):sum(n_in) + sum(n_out)], refs[sum(n_in) + sum(n_out):]
        mine = []
        for k, h in enumerate(hooks):
            cut = lambda seq, counts: seq[sum(counts[:k]):sum(counts[:k + 1])]
            mine.append((h, cut(ins, n_in), cut(outs, n_out), cut(scr, n_scr)))
        for h, *refs_of in mine:
            h.start(*refs_of)
        for h, *refs_of in mine:
            h.finish(*refs_of)

    res = pl.pallas_call(
        body, name=name, out_shape=tuple(o for h in hooks for o in h.out_shape),
        in_specs=[ANY] * sum(n_in), out_specs=(ANY,) * sum(n_out),
        scratch_shapes=[x for h in hooks for x in h.scratch], compiler_params=_params(),
    )(*[a for h in hooks for a in h.ins])
    return [res[sum(n_out[:k]):sum(n_out[:k + 1])] for k in range(len(hooks))]


class _GatherHook:
    def __init__(self, shards, kinds):
        self.ins, self.kinds, n = list(shards), list(kinds), len(shards)
        self.out_shape = tuple(
            S((w.shape[0], 4 * w.shape[1]), BF) if k == "col" else S((4,) + w.shape, BF) for w, k in zip(shards, kinds))
        dma = pltpu.SemaphoreType.DMA
        self.scratch = ([dma((n, 3)) for _ in range(5)] + [dma((n,)), dma((n,))]
                        + [pltpu.VMEM((3, w.shape[0] // 2, w.shape[1]), BF) for w in shards]
                        + [pltpu.VMEM(w.shape, BF) for w in shards])

    def _window(self, outs, i, s, half):
        rows, cols = self.ins[i].shape
        rh = rows // 2
        start = pl.multiple_of(half * rh, 16)
        if self.kinds[i] == "col":
            return outs[i].at[pl.ds(start, rh), pl.ds(pl.multiple_of(s * cols, 128), cols)]
        return outs[i].at[s, pl.ds(start, rh), :]

    def _copies(self, ins, outs, scr):
        n = len(ins)
        ssem, rsem, fssem, frsem, ksem, lsem, osem = scr[:7]
        land, own = scr[7:7 + n], scr[7 + n:7 + 2 * n]
        x, y, c = _place()
        sibling = (x, y, 1 - c)
        loads, stores, sends, forwards, keeps, passed = [], [], [], [], [], []
        for i in range(n):
            rows, cols = self.ins[i].shape
            rh = rows // 2
            mine = (outs[i].at[:, pl.ds(pl.multiple_of((2 * x + y) * cols, 128), cols)] if self.kinds[i] == "col"
                    else outs[i].at[2 * x + y])
            loads.append(functools.partial(pltpu.make_async_copy, ins[i], own[i], lsem.at[i]))
            stores.append(functools.partial(pltpu.make_async_copy, own[i], mine, osem.at[i]))
            src = ins[i].at[pl.ds(pl.multiple_of(c * rh, 16), rh), :]
            for j, (fx, fy) in enumerate(OTHER_CHIPS):
                px, py = _flip(x, fx), _flip(y, fy)
                sends.append(functools.partial(_remote, src, land[i].at[j], ssem.at[i, j], rsem.at[i, j], (px, py, c)))
                here = self._window(outs, i, 2 * px + py, c)
                forwards.append(functools.partial(_remote, land[i].at[j], here, fssem.at[i, j], frsem.at[i, j], sibling))
                keeps.append(functools.partial(pltpu.make_async_copy, land[i].at[j], here, ksem.at[i, j]))
                there = self._window(outs, i, 2 * px + py, 1 - c)
                passed.append(functools.partial(_remote, there, there, fssem.at[i, j], frsem.at[i, j], sibling))
        return loads, stores, sends, forwards, keeps, passed

    def start(self, ins, outs, scr):
        loads, _, sends, _, _, _ = self._copies(ins, outs, scr)
        for make in sends + loads:
            make().start()

    def finish(self, ins, outs, scr):
        loads, stores, sends, forwards, keeps, passed = self._copies(ins, outs, scr)
        for load, store in zip(loads, stores):
            load().wait()
            store().start()
        for send, forward, keep in zip(sends, forwards, keeps):
            send().wait_recv()
            forward().start()
            keep().start()
        for make in passed:
            make().wait_recv()
        for make in sends + forwards:
            make().wait_send()
        for make in keeps + stores:
            make().wait()


class _ChipsHook:
    def __init__(self, parts, kinds):
        self.ins, self.kinds, n = list(parts), list(kinds), len(parts)
        self.out_shape = tuple(
            S((4, p.shape[0], p.shape[1] // 4), BF) if k == "col" else S(p.shape, BF) for p, k in zip(parts, kinds))
        dma = pltpu.SemaphoreType.DMA
        self.scratch = ([dma((n, 3)), dma((n, 3)), dma((n,)), dma((n,))]
                        + [pltpu.VMEM(o.shape[1:], BF) for o in self.out_shape])

    def _slab(self, ins, i, s):
        _, rows, cols = self.out_shape[i].shape
        if self.kinds[i] == "col":
            return ins[i].at[:, pl.ds(pl.multiple_of(s * cols, 128), cols)]
        return ins[i].at[s]

    def _copies(self, ins, outs, scr):
        ssem, rsem, lsem, osem = scr[:4]
        own = scr[4:]
        x, y, c = _place()
        loads, stores, sends = [], [], []
        for i in range(len(ins)):
            loads.append(functools.partial(pltpu.make_async_copy, self._slab(ins, i, 2 * x + y), own[i], lsem.at[i]))
            stores.append(functools.partial(pltpu.make_async_copy, own[i], outs[i].at[3], osem.at[i]))
            for j, (fx, fy) in enumerate(OTHER_CHIPS):
                px, py = _flip(x, fx), _flip(y, fy)
                sends.append(functools.partial(_remote, self._slab(ins, i, 2 * px + py), outs[i].at[j], ssem.at[i, j],
                                               rsem.at[i, j], (px, py, c)))
        return loads, stores, sends

    def start(self, ins, outs, scr):
        loads, _, sends = self._copies(ins, outs, scr)
        for make in sends + loads:
            make().start()

    def finish(self, ins, outs, scr):
        loads, stores, sends = self._copies(ins, outs, scr)
        for load, store in zip(loads, stores):
            load().wait()
            store().start()
        for make in sends + stores:
            make().wait()


def _sum_share(slabs, name):
    _, rh, cols = slabs.shape
    tr = _row_tile(rh)
    nt = rh // tr

    def body(p_ref, o_ref, stage, lsem, ssem, rsem):
        i = pl.program_id(0)
        x, y, c = _place()
        sibling = (x, y, 1 - c)
        acc = p_ref[3].astype(F32)
        for k in range(3):
            acc = acc + p_ref[k].astype(F32)
        stage[i] = acc

        def copies(k):
            dst = o_ref.at[pl.ds(pl.multiple_of(c * rh + k * tr, 8), tr), :]
            return (pltpu.make_async_copy(stage.at[k], dst, lsem.at[k]),
                    _remote(stage.at[k], dst, ssem.at[k], rsem, sibling))

        for cp in copies(i):
            cp.start()

        @pl.when(i == nt - 1)
        def _():
            for k in range(nt):
                local, remote = copies(k)
                local.wait()
                remote.wait_send()
            got = o_ref.at[pl.ds(pl.multiple_of((1 - c) * rh, 8), rh), :]
            _remote(got, got, ssem.at[0], rsem, sibling).wait_recv()

    return pl.pallas_call(
        body, name=name, grid=(nt,), out_shape=S((2 * rh, cols), F32),
        in_specs=[pl.BlockSpec((4, tr, cols), lambda i: (0, i, 0))], out_specs=ANY,
        scratch_shapes=[pltpu.VMEM((nt, tr, cols), F32), pltpu.SemaphoreType.DMA((nt,)), pltpu.SemaphoreType.DMA((nt,)),
                        pltpu.SemaphoreType.DMA],
        compiler_params=_params(("arbitrary",)))(slabs)


class _SmallSumHook:
    def __init__(self, buf):
        self.ins, self.out_shape = [buf], (S(buf.shape, F32),)
        dma = pltpu.SemaphoreType.DMA
        self.scratch = [pltpu.VMEM((8,) + buf.shape, F32), pltpu.VMEM(buf.shape, F32), dma((7,)), dma((7,)), dma]

    def _sends(self, scr):
        slots, _, ssem, rsem, _ = scr
        x, y, c = _place()
        me = 4 * x + 2 * y + c
        for r in range(1, 8):
            px, py, pc = _flip(x, (r >> 2) & 1), _flip(y, (r >> 1) & 1), _flip(c, r & 1)
            yield (functools.partial(_remote, slots.at[me], slots.at[me], ssem.at[r - 1], rsem.at[r - 1], (px, py, pc)),
                   functools.partial(_remote, slots.at[me], slots.at[4 * px + 2 * py + pc], ssem.at[r - 1],
                                     rsem.at[r - 1], (px, py, pc)))

    def start(self, ins, outs, scr):
        slots, _, _, _, lsem = scr
        x, y, c = _place()
        load = pltpu.make_async_copy(ins[0], slots.at[4 * x + 2 * y + c], lsem)
        load.start()
        load.wait()
        for send, _ in self._sends(scr):
            send().start()

    def finish(self, ins, outs, scr):
        slots, total, _, _, lsem = scr
        for _, arrival in self._sends(scr):
            arrival().wait_recv()
        for send, _ in self._sends(scr):
            send().wait_send()
        acc = slots[0]
        for k in range(1, 8):
            acc = acc + slots[k]
        total[...] = acc
        store = pltpu.make_async_copy(total, outs[0], lsem)
        store.start()
        store.wait()


BIG = ("ffn1_w_gu", "ffn1_w_down", "w_in", "w_out", "xattn_wq", "xattn_wkv", "xattn_wo", "ffn2_w_gu", "ffn2_w_down")
KIND = {"ffn1_w_gu": "col", "ffn1_w_down": "row", "w_in": "col", "w_out": "row", "xattn_wq": "row",
        "xattn_wkv": "col", "xattn_wo": "row", "ffn2_w_gu": "col", "ffn2_w_down": "row"}
WEIGHTS = ("rel_bias", "ffn1_norm", "ffn1_w_gu", "ffn1_w_down", "mix_norm", "w_in", "sinks", "conv_w", "w_out",
           "xattn_norm", "mem_norm", "xattn_wq", "xattn_wkv", "xattn_wo", "ffn2_norm", "ffn2_w_gu", "ffn2_w_down",
           "final_norm")
SMALL_ROWS = 16
GAIN_ROW = {"ffn1_norm": 0, "mix_norm": 1, "xattn_norm": 2, "mem_norm": 3, "ffn2_norm": 4, "final_norm": 5}
CONV_ROW, SINK_ROW, BIAS_ROW, LOSS_ROW = 6, 9, 10, 11


def _rows_block(rows, d):
    buf = jnp.zeros((SMALL_ROWS, d), F32)
    for r, v in rows.items():
        buf = lax.dynamic_update_slice(buf, v.reshape(1, -1).astype(F32), (r, 0))
    return buf


def _local_step(x, mem, pos, target, w, gains, rel_bias, sinks, conv_w, shards=None):
    t, d = x.shape
    dist = shards is not None
    w = dict(w)
    grads, slabs = {}, {}
    pos_col = pos.reshape(t, 1)
    pos_row = pos.reshape(1, t)
    bias_t = _bias_build(rel_bias).reshape(SWA_KV_HEADS, GROUP_ROWS, 2 * BLOCK)

    def gather(names):
        return _GatherHook([shards[k] for k in names], [KIND[k] for k in names]) if dist else None

    def gathered(names, hook):
        if dist:
            for k, gw in zip(names, hook.results):
                w[k] = gw if KIND[k] == "col" else gw.reshape(-1, gw.shape[-1])

    def dw(k, a, b, tn, name, **kw):
        return _dw_pair(a, b, tn, name, KIND[k], **kw) if dist else _dw(a, b, tn, name, **kw)

    def pair_sums(names):
        if not dist:
            return None
        parts = {}
        for k in names:
            mine, sib = grads[k]
            if KIND[k] == "row":
                mine, sib = mine.reshape(-1, mine.shape[-1]), sib.reshape(-1, sib.shape[-1])
            p = _add_bf16(mine, sib, "pair_sum_" + k)
            parts[k] = p if KIND[k] == "col" else p.reshape(4, -1, p.shape[-1])
        return parts

    def chips(names, parts):
        return _ChipsHook([parts[k] for k in names], [KIND[k] for k in names]) if dist else None

    def reduced(names, hook):
        if dist:
            slabs.update(zip(names, hook.results))

    names = ("w_in", "w_out")
    hook = gather(names)
    x1, g1, u1 = _ffn_fwd(x, gains["ffn1_norm"], w["ffn1_w_gu"], w["ffn1_w_down"], "ffn1_fwd", hook)
    gathered(names, hook)
    names = ("xattn_wq", "xattn_wkv", "xattn_wo")
    hook = gather(names)
    qkv, e = _mix_proj(x1, gains["mix_norm"], w["w_in"], hook)
    gathered(names, hook)
    names = ("ffn2_w_gu",)
    hook = gather(names)
    attn = _swa_fwd(qkv, pos_col, pos_row, bias_t, sinks, hook)
    gathered(names, hook)
    names = ("ffn2_w_down",)
    hook = gather(names)
    x2, merged = _mix_out_fwd(e, attn, conv_w, w["w_out"], x1, hook)
    gathered(names, hook)
    mh, kv = _mem_kv(mem, gains["mem_norm"], w["xattn_wkv"])
    x3, qx, o = _xattn_fwd(x2, gains["xattn_norm"], w["xattn_wq"], kv, w["xattn_wo"])
    x4, g2, u2 = _ffn_fwd(x3, gains["ffn2_norm"], w["ffn2_w_gu"], w["ffn2_w_down"], "ffn2_fwd")
    dx4, d_final, loss = _loss_bwd(x4, gains["final_norm"], target)

    dx3, d_ffn2, dgu2, a2, h4, dyh4 = _ffn_bwd(dx4, x3, gains["ffn2_norm"], g2, u2, w["ffn2_w_gu"], w["ffn2_w_down"],
                                               "ffn2_bwd")
    grads["ffn2_w_gu"] = dw("ffn2_w_gu", h4, dgu2, 1408, "dw_ffn2_gu")
    grads["ffn2_w_down"] = dw("ffn2_w_down", a2, dyh4, 512, "dw_ffn2_down")
    parts = pair_sums(("ffn2_w_gu", "ffn2_w_down"))
    hook = chips(("ffn2_w_gu",), parts)
    dx2, d_xattn, dkv, dxh3, h3, dqx = _xattn_bwd(dx3, x2, gains["xattn_norm"], qx, kv, w["xattn_wq"], w["xattn_wo"], hook)
    reduced(("ffn2_w_gu",), hook)
    grads["xattn_wo"] = dw("xattn_wo", o, dxh3, 1024, "dw_wo")
    grads["xattn_wq"] = dw("xattn_wq", h3, dqx, 1024, "dw_wq")
    grads["xattn_wkv"], d_mem = _mem_bwd(dkv, mh, mem, gains["mem_norm"], w["xattn_wkv"], dist)
    hook = chips(("ffn2_w_down",), parts)
    dattn, de, dcw, dxh2 = _mix_out_bwd(dx2, e, attn, conv_w, w["w_out"], hook)
    reduced(("ffn2_w_down",), hook)
    grads["w_out"] = dw("w_out", merged, dxh2, 1024, "dw_wout")
    names = ("xattn_wo", "xattn_wq", "xattn_wkv", "w_out")
    hook = chips(names, pair_sums(names))
    dq, dkvs, gb, dsk = _swa_bwd(qkv, dattn, pos_col, pos_row, bias_t, sinks, hook)
    reduced(names, hook)
    d_rel_bias, d_sinks = _bias_reduce(gb.reshape(SWA_HEADS, BLOCK, 2 * BLOCK), dsk.reshape(SWA_HEADS, BLOCK, 1))
    dx1, d_mix, h2 = _mix_in_bwd(dq, dkvs, de, x1, gains["mix_norm"], w["w_in"], dx2)
    n_in = w["w_in"].shape[1]
    gw = dw("w_in", h2, dq, 512, "dw_win_q", ncols=n_in)
    gw = dw("w_in", h2, dkvs, 512, "dw_win_kv", into=gw, col_block=dq.shape[1] // 512, ncols=n_in)
    grads["w_in"] = dw("w_in", h2, de, 512, "dw_win_e", into=gw, col_block=(dq.shape[1] + dkvs.shape[1]) // 512,
                       ncols=n_in)
    names = ("w_in",)
    hook = chips(names, pair_sums(names))
    dx0, d_ffn1, dgu1, a1, h1, dyh1 = _ffn_bwd(dx1, x, gains["ffn1_norm"], g1, u1, w["ffn1_w_gu"], w["ffn1_w_down"],
                                               "ffn1_bwd", hook)
    reduced(names, hook)
    grads["ffn1_w_down"] = dw("ffn1_w_down", a1, dyh1, 512, "dw_ffn1_down")
    names = ("ffn1_w_down",)
    hook = chips(names, pair_sums(names))
    grads["ffn1_w_gu"] = dw("ffn1_w_gu", h1, dgu1, 1408, "dw_ffn1_gu", **({"hook": hook} if dist else {}))
    reduced(names, hook)
    rows = {0: d_ffn1, 1: d_mix, 2: d_xattn, 3: d_mem, 4: d_ffn2, 5: d_final, SINK_ROW: d_sinks, BIAS_ROW: d_rel_bias,
            LOSS_ROW: loss[0, 0:1]}
    rows.update({CONV_ROW + j: dcw[j] for j in range(3)})
    small = _rows_block(rows, d)
    names = ("ffn1_w_gu",)
    hook = chips(names, pair_sums(names))
    if dist:
        hook.results, (small,) = _exchange([hook, _SmallSumHook(small)], "reduce_chips_ffn1_gu")
    reduced(names, hook)
    return dx0, (slabs if dist else grads), small


def kernel(x, mem, positions, rel_bias, ffn1_norm, ffn1_w_gu, ffn1_w_down, mix_norm, w_in, sinks, conv_w, w_out, xattn_norm, mem_norm, xattn_wq, xattn_wkv, xattn_wo, ffn2_norm, ffn2_w_gu, ffn2_w_down, final_norm, loss_target, m_rel_bias, m_ffn1_norm, m_ffn1_w_gu, m_ffn1_w_down, m_mix_norm, m_w_in, m_sinks, m_conv_w, m_w_out, m_xattn_norm, m_mem_norm, m_xattn_wq, m_xattn_wkv, m_xattn_wo, m_ffn2_norm, m_ffn2_w_gu, m_ffn2_w_down, m_final_norm, v_rel_bias, v_ffn1_norm, v_ffn1_w_gu, v_ffn1_w_down, v_mix_norm, v_w_in, v_sinks, v_conv_w, v_w_out, v_xattn_norm, v_mem_norm, v_xattn_wq, v_xattn_wkv, v_xattn_wo, v_ffn2_norm, v_ffn2_w_gu, v_ffn2_w_down, v_final_norm):
    args = dict(locals())
    wts = {k: args[k] for k in WEIGHTS}
    mom = {k: args["m_" + k] for k in WEIGHTS}
    var = {k: args["v_" + k] for k in WEIGHTS}
    d = x.shape[-1]
    s_me = 2 * lax.axis_index("x") + lax.axis_index("y")

    shards = {k: wts[k][0].astype(BF) for k in BIG}
    first = ("ffn1_w_gu", "ffn1_w_down")
    cw_cols = conv_w.shape[-1]
    placed = lax.dynamic_update_slice(jnp.zeros((SMALL_ROWS, d), F32), 0.5 * conv_w[0], (0, s_me * cw_cols))
    gathered, (conv_sum,) = _exchange(
        [_GatherHook([shards[k] for k in first], [KIND[k] for k in first]), _SmallSumHook(placed)], "gather_ffn1")
    whole = {k: (gw if KIND[k] == "col" else gw.reshape(-1, gw.shape[-1])) for k, gw in zip(first, gathered)}
    conv_whole = conv_sum[0:3]

    gains = {k: wts[k].reshape(1, d) for k in GAIN_ROW}
    dx0, slabs, small_sum = _local_step(x[0], mem[0], positions[0], loss_target[0], whole, gains, rel_bias, sinks,
                                        conv_whole, shards)

    shard_grads = {k: _sum_share(slabs[k], "sum_share_" + k) for k in BIG}

    loss = small_sum[LOSS_ROW, 0]

    out_g, out_d, out_m, out_v = {}, {}, {}, {}
    for k in BIG:
        g2d, dl, mn, vn = _adamw(wts[k][0], shard_grads[k], mom[k][0], var[k][0], "adamw_" + k, echo=True)
        out_g[k], out_d[k], out_m[k], out_v[k] = g2d[None], dl[None], mn[None], vn[None]

    conv_g = lax.dynamic_slice(small_sum, (CONV_ROW, s_me * cw_cols), (3, cw_cols))

    def pack(src, conv_block):
        rows = {r: src[k] for k, r in GAIN_ROW.items()}
        rows.update({CONV_ROW + j: conv_block[j] for j in range(3)})
        rows[SINK_ROW], rows[BIAS_ROW] = src["sinks"], src["rel_bias"]
        return _rows_block(rows, d)

    small_g = {k: small_sum[r] for k, r in GAIN_ROW.items()}
    small_g["sinks"] = small_sum[SINK_ROW, 0:sinks.size]
    small_g["rel_bias"] = small_sum[BIAS_ROW, 0:rel_bias.size]
    gp = pack(small_g, conv_g)
    dl, mn, vn = _adamw(pack(wts, conv_w[0]), gp, pack(mom, m_conv_w[0]), pack(var, v_conv_w[0]), "adamw_small")

    def unpack(buf, k):
        if k in GAIN_ROW:
            return buf[GAIN_ROW[k]].reshape(wts[k].shape)
        if k == "conv_w":
            return buf[CONV_ROW:CONV_ROW + 3, 0:cw_cols][None]
        if k == "sinks":
            return buf[SINK_ROW, 0:sinks.size].reshape(sinks.shape)
        return buf[BIAS_ROW, 0:rel_bias.size].reshape(rel_bias.shape)

    for k in WEIGHTS:
        if k not in KIND:
            out_g[k], out_d[k], out_m[k], out_v[k] = unpack(gp, k), unpack(dl, k), unpack(mn, k), unpack(vn, k)

    return (loss, dx0[None], *[out_g[k] for k in WEIGHTS], *[out_d[k] for k in WEIGHTS],
            *[out_m[k] for k in WEIGHTS], *[out_v[k] for k in WEIGHTS])
```

```python
import functools
import math

import jax
import jax.numpy as jnp
from jax import lax
from jax.experimental import pallas as pl
from jax.experimental.pallas import tpu as pltpu

BF = jnp.bfloat16
F32 = jnp.float32
I32 = jnp.int32
S = jax.ShapeDtypeStruct

EPS = 1e-6
NEG = -1e30
POS_PAD = 1 << 30
WINDOW = 128
BLOCK = 128
HEAD_DIM = 64
SWA_HEADS = 16
SWA_KV_HEADS = 4
SWA_GROUP = SWA_HEADS // SWA_KV_HEADS
MEM_HEADS = 4
REL_BUCKETS = 32
REL_MAX_DIST = 128
ADAM_LR = 0.001
ADAM_B1 = 0.9
ADAM_B2 = 0.999
ADAM_EPS = 1e-08
ADAM_WD = 0.01
ADAM_STEP = 10

V7X_VMEM_LIMIT_BYTES = 56 * 1024 * 1024
MESH = pl.DeviceIdType.MESH
ANY = pl.BlockSpec(memory_space=pl.ANY)
VMEM_SPEC = pl.BlockSpec(memory_space=pltpu.VMEM)
SMEM_SPEC = pl.BlockSpec(memory_space=pltpu.SMEM)


def _params(sem=None):
    return pltpu.CompilerParams(dimension_semantics=sem, vmem_limit_bytes=V7X_VMEM_LIMIT_BYTES)


def _resident(shape):
    nd = len(shape)
    return pl.BlockSpec(shape, lambda *_: (0,) * nd, pipeline_mode=pl.Buffered(1))


def _acc_spec(shape):
    nd = len(shape)
    return pl.BlockSpec(shape, lambda *_: (0,) * nd)


def _call(body, *, name, grid, out_shape, in_specs, out_specs, args, sem, scratch_shapes=(), aliases=None, hook=None):
    aliases = aliases or {}
    if hook is None:
        return pl.pallas_call(body, name=name, grid=grid, out_shape=out_shape, in_specs=in_specs, out_specs=out_specs,
                              scratch_shapes=list(scratch_shapes), input_output_aliases=aliases,
                              compiler_params=_params(sem))(*args)
    n_in, n_out, n_scr = len(in_specs), len(out_shape), len(scratch_shapes)
    h_in, h_out = len(hook.ins), len(hook.out_shape)

    def at_step(pick):
        conds = [pl.program_id(ax) == pick(size) for ax, size in enumerate(grid)]
        return functools.reduce(jnp.logical_and, conds)

    def hosted(*refs):
        k_in, x_in = refs[:n_in], refs[n_in:n_in + h_in]
        o0 = n_in + h_in
        k_out, x_out = refs[o0:o0 + n_out], refs[o0 + n_out:o0 + n_out + h_out]
        s0 = o0 + n_out + h_out
        k_scr, x_scr = refs[s0:s0 + n_scr], refs[s0 + n_scr:]

        @pl.when(at_step(lambda size: 0))
        def _():
            hook.start(x_in, x_out, x_scr)
        body(*k_in, *k_out, *k_scr)

        @pl.when(at_step(lambda size: size - 1))
        def _():
            hook.finish(x_in, x_out, x_scr)

    res = pl.pallas_call(
        hosted, name=name, grid=grid, out_shape=tuple(out_shape) + tuple(hook.out_shape),
        in_specs=list(in_specs) + [ANY] * h_in, out_specs=tuple(out_specs) + (ANY,) * h_out,
        scratch_shapes=list(scratch_shapes) + list(hook.scratch), input_output_aliases=aliases,
        compiler_params=_params(("arbitrary",) * len(grid)),
    )(*args, *hook.ins)
    hook.results = res[n_out:]
    return res[:n_out]


def _nn(a, b):
    return jnp.dot(a, b, preferred_element_type=F32)


def _nt(a, b):
    return lax.dot_general(a, b, (((1,), (1,)), ((), ())), preferred_element_type=F32)


def _tn(a, b):
    return lax.dot_general(a, b, (((0,), (0,)), ((), ())), preferred_element_type=F32)


def _sigmoid(v):
    return 1.0 / (1.0 + jnp.exp(-v))


def _rms(x):
    r = lax.rsqrt(jnp.mean(x * x, axis=-1, keepdims=True) + EPS)
    return x * r, r


def _rms_bwd(dh, n, r, g):
    dn = dh * g
    dx = r * (dn - n * jnp.mean(dn * n, axis=-1, keepdims=True))
    return dx, jnp.sum(dh * n, axis=0, keepdims=True)


def _ffn_fwd(x, gn, wgu, wd, name, hook=None):
    t, d = x.shape
    f = wd.shape[0]
    tm, fc = 256, 1408

    def body(x_ref, gn_ref, wgu_ref, wd_ref, xo_ref, g_ref, u_ref):
        xv = x_ref[...]
        n, _ = _rms(xv)
        h = (n * gn_ref[...]).astype(BF)
        acc = jnp.zeros((tm, d), F32)
        for c0 in range(0, f, fc):
            g = _nn(h, wgu_ref[:, c0:c0 + fc])
            u = _nn(h, wgu_ref[:, f + c0:f + c0 + fc])
            g_ref[:, c0:c0 + fc] = g.astype(BF)
            u_ref[:, c0:c0 + fc] = u.astype(BF)
            a = (g * _sigmoid(g)) * u
            acc = acc + _nn(a.astype(BF), wd_ref[c0:c0 + fc, :])
        xo_ref[...] = xv + 0.5 * acc

    return _call(
        body, name=name, grid=(t // tm,),
        out_shape=(S((t, d), F32), S((t, f), BF), S((t, f), BF)),
        in_specs=[pl.BlockSpec((tm, d), lambda i: (i, 0)), _resident((1, d)), _resident(wgu.shape), _resident(wd.shape)],
        out_specs=(pl.BlockSpec((tm, d), lambda i: (i, 0)), pl.BlockSpec((tm, f), lambda i: (i, 0)),
                   pl.BlockSpec((tm, f), lambda i: (i, 0))),
        sem=("parallel",), args=(x, gn, wgu, wd), hook=hook)


def _mix_proj(x, gn, w_in, hook=None):
    t, d = x.shape
    tm = 256
    nqkv = 1536
    ne = w_in.shape[1] - nqkv

    def body(x_ref, gn_ref, w_ref, qkv_ref, e_ref):
        n, _ = _rms(x_ref[...])
        h = (n * gn_ref[...]).astype(BF)
        qkv_ref[...] = _nn(h, w_ref[:, 0:nqkv]).astype(BF)
        for c0 in range(0, ne, 1024):
            e_ref[:, c0:c0 + 1024] = _nn(h, w_ref[:, nqkv + c0:nqkv + c0 + 1024]).astype(BF)

    return _call(
        body, name="mix_proj", grid=(t // tm,),
        out_shape=(S((t, nqkv), BF), S((t, ne), BF)),
        in_specs=[pl.BlockSpec((tm, d), lambda i: (i, 0)), _resident((1, d)), _resident(w_in.shape)],
        out_specs=(pl.BlockSpec((tm, nqkv), lambda i: (i, 0)), pl.BlockSpec((tm, ne), lambda i: (i, 0))),
        sem=("parallel",), args=(x, gn, w_in), hook=hook)


def _t5_bucket(rel):
    n = jnp.maximum(rel, 0)
    max_exact = REL_BUCKETS // 2
    nf = jnp.maximum(n, 1).astype(F32)
    large = max_exact + (jnp.log(nf / max_exact) / math.log(REL_MAX_DIST / max_exact)
                         * (REL_BUCKETS - max_exact)).astype(I32)
    large = jnp.minimum(large, REL_BUCKETS - 1)
    return jnp.where(n < max_exact, n, large)


def _block_rel():
    i = lax.broadcasted_iota(I32, (BLOCK, 2 * BLOCK), 0)
    j = lax.broadcasted_iota(I32, (BLOCK, 2 * BLOCK), 1)
    return i + BLOCK - j


def _bias_build(rel_bias):
    def body(rb_ref, o_ref):
        bucket = _t5_bucket(_block_rel())
        for h in range(SWA_HEADS):
            acc = jnp.zeros((BLOCK, 2 * BLOCK), F32)
            for b in range(REL_BUCKETS):
                acc = jnp.where(bucket == b, rb_ref[b, h], acc)
            o_ref[h] = acc

    return pl.pallas_call(
        body, name="bias_build", out_shape=S((SWA_HEADS, BLOCK, 2 * BLOCK), F32),
        in_specs=[SMEM_SPEC], out_specs=VMEM_SPEC,
    )(rel_bias)


GROUP_ROWS = SWA_GROUP * BLOCK


def _swa_visible(b, pq_ref, pkp_ref, pkc_ref):
    pk = jnp.concatenate([pkp_ref[...], pkc_ref[...]], axis=1)
    col = lax.broadcasted_iota(I32, (1, 2 * BLOCK), 1)
    pk = jnp.where(jnp.logical_and(b == 0, col < BLOCK), POS_PAD, pk)
    rel = jnp.concatenate([pq_ref[...]] * SWA_GROUP, axis=0) - pk
    return jnp.logical_and(rel >= 0, rel < WINDOW)


def _group_heads(ref, hk):
    h0 = hk * SWA_GROUP
    return jnp.concatenate([ref[:, (h0 + g) * HEAD_DIM:(h0 + g + 1) * HEAD_DIM] for g in range(SWA_GROUP)], axis=0)


def _group_sinks(sink_ref, hk):
    row = lax.broadcasted_iota(I32, (GROUP_ROWS, 1), 0)
    col = jnp.zeros((GROUP_ROWS, 1), F32) + sink_ref[0, hk * SWA_GROUP]
    for g in range(1, SWA_GROUP):
        col = jnp.where(row >= g * BLOCK, sink_ref[0, hk * SWA_GROUP + g], col)
    return col


def _swa_probs(qg, kh, vis, bias, sink):
    s = _nt(qg, kh) * (HEAD_DIM ** -0.5)
    s = jnp.where(vis, s + bias, NEG)
    m = jnp.maximum(jnp.max(s, axis=-1, keepdims=True), sink)
    p = jnp.exp(s - m)
    ps = jnp.exp(sink - m)
    inv = 1.0 / (jnp.sum(p, axis=-1, keepdims=True) + ps)
    return p * inv, ps * inv


def _swa_fwd(qkv, pos_col, pos_row, bias_t, sinks, hook=None):
    t = qkv.shape[0]
    nb = t // BLOCK
    qw = SWA_HEADS * HEAD_DIM
    kw = SWA_KV_HEADS * HEAD_DIM

    def body(q_ref, kp_ref, kc_ref, vp_ref, vc_ref, pq_ref, pkp_ref, pkc_ref, bias_ref, sink_ref, o_ref):
        b = pl.program_id(0)
        vis = _swa_visible(b, pq_ref, pkp_ref, pkc_ref)
        k2 = jnp.concatenate([kp_ref[...], kc_ref[...]], axis=0)
        v2 = jnp.concatenate([vp_ref[...], vc_ref[...]], axis=0)
        for hk in range(SWA_KV_HEADS):
            kh = k2[:, hk * HEAD_DIM:(hk + 1) * HEAD_DIM]
            vh = v2[:, hk * HEAD_DIM:(hk + 1) * HEAD_DIM]
            pn, _ = _swa_probs(_group_heads(q_ref, hk), kh, vis, bias_ref[hk], _group_sinks(sink_ref, hk))
            o = _nn(pn.astype(BF), vh)
            for g in range(SWA_GROUP):
                h = hk * SWA_GROUP + g
                o_ref[:, h * HEAD_DIM:(h + 1) * HEAD_DIM] = o[g * BLOCK:(g + 1) * BLOCK]

    prev = lambda b: jnp.maximum(b - 1, 0)
    return _call(
        body, name="swa_fwd", grid=(nb,), out_shape=(S((t, qw), F32),),
        in_specs=[
            pl.BlockSpec((BLOCK, qw), lambda b: (b, 0)),
            pl.BlockSpec((BLOCK, kw), lambda b: (prev(b), qw // kw)),
            pl.BlockSpec((BLOCK, kw), lambda b: (b, qw // kw)),
            pl.BlockSpec((BLOCK, kw), lambda b: (prev(b), qw // kw + 1)),
            pl.BlockSpec((BLOCK, kw), lambda b: (b, qw // kw + 1)),
            pl.BlockSpec((BLOCK, 1), lambda b: (b, 0)),
            pl.BlockSpec((1, BLOCK), lambda b: (0, prev(b))),
            pl.BlockSpec((1, BLOCK), lambda b: (0, b)),
            _resident(bias_t.shape),
            SMEM_SPEC,
        ],
        out_specs=(pl.BlockSpec((BLOCK, qw), lambda b: (b, 0)),),
        sem=("parallel",), args=(qkv, qkv, qkv, qkv, qkv, pos_col, pos_row, pos_row, bias_t, sinks), hook=hook)[0]


HALO = 16


def _conv_taps(z, zh, first):
    tm = z.shape[0]
    zh = jnp.where(first, 0.0, zh)
    row = lax.broadcasted_iota(I32, (tm, 1), 0)
    z1 = jnp.where(row == 0, zh[HALO - 1:HALO, :], pltpu.roll(z, 1, 0))
    z2 = jnp.where(row == 0, zh[HALO - 2:HALO - 1, :], jnp.where(row == 1, zh[HALO - 1:HALO, :], pltpu.roll(z, 2, 0)))
    return z1, z2


def _mix_out_fwd(e, attn, conv_w, w_out, x, hook=None):
    t, d = x.shape
    tm = 256
    hb = tm // HALO
    f32 = lambda ref: ref[...].astype(F32)

    def body(c_ref, b_ref, u_ref, ga_ref, gc_ref, ch_ref, uh_ref, attn_ref, cw_ref, w_ref, x_ref, xo_ref, mg_ref):
        i = pl.program_id(0)
        z = f32(c_ref) * f32(u_ref)
        z1, z2 = _conv_taps(z, f32(ch_ref) * f32(uh_ref), i == 0)
        s = cw_ref[0:1, :] * z2 + cw_ref[1:2, :] * z1 + cw_ref[2:3, :] * z
        conv = f32(b_ref) * s
        merged = (_sigmoid(f32(ga_ref)) * attn_ref[...] + _sigmoid(f32(gc_ref)) * conv).astype(BF)
        mg_ref[...] = merged
        xo_ref[...] = x_ref[...] + _nn(merged, w_ref[...])

    ecol = lambda cb: pl.BlockSpec((tm, d), lambda i: (i, cb))
    halo = lambda cb: pl.BlockSpec((HALO, d), lambda i: (jnp.maximum(i * hb - 1, 0), cb))
    row = pl.BlockSpec((tm, d), lambda i: (i, 0))
    return _call(
        body, name="mix_out_fwd", grid=(t // tm,),
        out_shape=(S((t, d), F32), S((t, d), BF)),
        in_specs=[ecol(0), ecol(1), ecol(2), ecol(3), ecol(4), halo(0), halo(2), row,
                  _resident(conv_w.shape), _resident(w_out.shape), row],
        out_specs=(row, row),
        sem=("parallel",), args=(e, e, e, e, e, e, e, attn, conv_w, w_out, x), hook=hook)


def _mem_kv(mem, gm, wkv):
    m, d = mem.shape

    def body(mem_ref, gm_ref, w_ref, mh_ref, kv_ref):
        n, _ = _rms(mem_ref[...])
        mh = (n * gm_ref[...]).astype(BF)
        mh_ref[...] = mh
        kv_ref[...] = _nn(mh, w_ref[...]).astype(BF)

    return pl.pallas_call(
        body, name="mem_kv", out_shape=(S((m, d), BF), S((m, wkv.shape[1]), BF)),
        compiler_params=_params(),
    )(mem, gm, wkv)


def _xattn_probs(qh, kh):
    s = _nt(qh, kh) * (kh.shape[1] ** -0.5)
    p = jnp.exp(s - jnp.max(s, axis=-1, keepdims=True))
    return p * (1.0 / jnp.sum(p, axis=-1, keepdims=True))


def _xattn_fwd(x, gn, wq, kv, wo):
    t, d = x.shape
    tm = 256
    hd = d // MEM_HEADS

    def body(x_ref, gn_ref, wq_ref, kv_ref, wo_ref, xo_ref, q_ref, o_ref):
        xv = x_ref[...]
        n, _ = _rms(xv)
        q = _nn((n * gn_ref[...]).astype(BF), wq_ref[...]).astype(BF)
        q_ref[...] = q
        outs = []
        for hh in range(MEM_HEADS):
            p = _xattn_probs(q[:, hh * hd:(hh + 1) * hd], kv_ref[:, hh * hd:(hh + 1) * hd])
            outs.append(_nn(p.astype(BF), kv_ref[:, d + hh * hd:d + (hh + 1) * hd]))
        o = jnp.concatenate(outs, axis=1).astype(BF)
        o_ref[...] = o
        xo_ref[...] = xv + _nn(o, wo_ref[...])

    row = pl.BlockSpec((tm, d), lambda i: (i, 0))
    return pl.pallas_call(
        body, name="xattn_fwd", grid=(t // tm,),
        out_shape=(S((t, d), F32), S((t, d), BF), S((t, d), BF)),
        in_specs=[row, _resident((1, d)), _resident(wq.shape), _resident(kv.shape), _resident(wo.shape)],
        out_specs=(row, row, row),
        compiler_params=_params(("parallel",)),
    )(x, gn, wq, kv, wo)


def _loss_bwd(x, gf, target):
    t, d = x.shape
    tm = 512

    def body(x_ref, gf_ref, t_ref, dx_ref, dg_ref, loss_ref):
        @pl.when(pl.program_id(0) == 0)
        def _():
            dg_ref[...] = jnp.zeros_like(dg_ref)
            loss_ref[...] = jnp.zeros_like(loss_ref)
        n, r = _rms(x_ref[...])
        g = gf_ref[...]
        err = n * g - t_ref[...]
        loss_ref[...] += 0.5 * jnp.sum(jnp.sum(err * err, axis=-1, keepdims=True) / d, axis=0, keepdims=True)
        dx, dg = _rms_bwd(err / d, n, r, g)
        dx_ref[...] = dx
        dg_ref[...] += dg

    row = pl.BlockSpec((tm, d), lambda i: (i, 0))
    return pl.pallas_call(
        body, name="loss_bwd", grid=(t // tm,),
        out_shape=(S((t, d), F32), S((1, d), F32), S((1, 128), F32)),
        in_specs=[row, _resident((1, d)), row],
        out_specs=(row, _acc_spec((1, d)), _acc_spec((1, 128))),
        compiler_params=_params(("arbitrary",)),
    )(x, gf, target)


def _ffn_bwd(dxo, x, gn, g, u, wgu, wd, name, hook=None):
    t, d = x.shape
    f = wd.shape[0]
    tm, fc = 256, 1408

    def body(dxo_ref, x_ref, gn_ref, g_ref, u_ref, wgu_ref, wd_ref, dx_ref, dgn_ref, dgu_ref, a_ref, h_ref, dyh_ref):
        @pl.when(pl.program_id(0) == 0)
        def _():
            dgn_ref[...] = jnp.zeros_like(dgn_ref)
        dxov = dxo_ref[...]
        dyh = (0.5 * dxov).astype(BF)
        dyh_ref[...] = dyh
        n, r = _rms(x_ref[...])
        gnv = gn_ref[...]
        h_ref[...] = (n * gnv).astype(BF)
        dh = jnp.zeros((tm, d), F32)
        for c0 in range(0, f, fc):
            gv = g_ref[:, c0:c0 + fc].astype(F32)
            uv = u_ref[:, c0:c0 + fc].astype(F32)
            da = _nt(dyh, wd_ref[c0:c0 + fc, :])
            sg = _sigmoid(gv)
            silu = gv * sg
            a_ref[:, c0:c0 + fc] = (silu * uv).astype(BF)
            dg = (da * uv * (sg * (1.0 + gv * (1.0 - sg)))).astype(BF)
            du = (da * silu).astype(BF)
            dgu_ref[:, c0:c0 + fc] = dg
            dgu_ref[:, f + c0:f + c0 + fc] = du
            dh = dh + _nt(dg, wgu_ref[:, c0:c0 + fc]) + _nt(du, wgu_ref[:, f + c0:f + c0 + fc])
        dx, dgn = _rms_bwd(dh, n, r, gnv)
        dx_ref[...] = dxov + dx
        dgn_ref[...] += dgn

    row = pl.BlockSpec((tm, d), lambda i: (i, 0))
    frow = pl.BlockSpec((tm, f), lambda i: (i, 0))
    return _call(
        body, name=name, grid=(t // tm,),
        out_shape=(S((t, d), F32), S((1, d), F32), S((t, 2 * f), BF), S((t, f), BF), S((t, d), BF), S((t, d), BF)),
        in_specs=[row, row, _resident((1, d)), frow, frow, _resident(wgu.shape), _resident(wd.shape)],
        out_specs=(row, _acc_spec((1, d)), pl.BlockSpec((tm, 2 * f), lambda i: (i, 0)), frow, row, row),
        sem=("arbitrary",), args=(dxo, x, gn, g, u, wgu, wd), hook=hook)


def _dw(a, b, tn, name, into=None, col_block=0, ncols=None):
    t, ka = a.shape
    nb = b.shape[1]
    tt = 1024
    nt = t // tt
    ncols = nb if ncols is None else ncols

    def body(*refs):
        a_ref, b_ref = refs[0], refs[1]
        o_ref, acc_ref = refs[-2], refs[-1]
        k = pl.program_id(1)

        @pl.when(k == 0)
        def _():
            acc_ref[...] = jnp.zeros_like(acc_ref)
        acc_ref[...] += _tn(a_ref[...], b_ref[...])

        @pl.when(k == nt - 1)
        def _():
            o_ref[...] = acc_ref[...].astype(BF)

    in_specs = [pl.BlockSpec((tt, ka), lambda j, k: (k, 0)), pl.BlockSpec((tt, tn), lambda j, k: (k, j))]
    args = [a, b]
    aliases = {}
    if into is not None:
        in_specs.append(ANY)
        args.append(into)
        aliases = {2: 0}
    return pl.pallas_call(
        body, name=name, grid=(nb // tn, nt), out_shape=S((ka, ncols), BF),
        in_specs=in_specs, out_specs=pl.BlockSpec((ka, tn), lambda j, k: (0, j + col_block)),
        scratch_shapes=[pltpu.VMEM((ka, tn), F32)], input_output_aliases=aliases,
        compiler_params=_params(("parallel", "arbitrary")),
    )(*args)


def _dw_pair(a, b, tn, name, kind, into=None, col_block=0, ncols=None, hook=None):
    t, ka = a.shape
    nb = b.shape[1]
    tt = 2048
    nt, nj = t // tt, nb // tn
    ncols = nb if ncols is None else ncols
    col = kind == "col"
    rh = ka // 2 if col else ka // 8
    tile = (rh, tn) if col else (4, rh, tn)
    half = (rh, ncols) if col else (4, rh, ncols)
    lead = (slice(None),) * (len(tile) - 1)

    def body(*refs):
        a_ref, b_ref = refs[0], refs[1]
        mine_ref, sib_ref, acc_ref, stage, ssem, rsem = refs[-6:]
        j, k = pl.program_id(0), pl.program_id(1)
        x, y, c = _place()
        sibling = (x, y, 1 - c)

        @pl.when(k == 0)
        def _():
            acc_ref[...] = jnp.zeros_like(acc_ref)
        acc_ref[...] += _tn(a_ref[...], b_ref[...])

        def send(slot, jj):
            dst = sib_ref.at[lead + (pl.ds(pl.multiple_of((jj + col_block) * tn, 128), tn),)]
            return _remote(stage.at[slot], dst, ssem.at[slot], rsem, sibling)

        def rows(s, whose):
            return acc_ref[pl.ds(pl.multiple_of(s * 2 * rh + whose * rh, 16), rh), :].astype(BF)

        @pl.when(k == nt - 1)
        def _():
            slot = j % 2

            @pl.when(j >= 2)
            def _():
                send(slot, j).wait_send()
            if col:
                mine_ref[...] = rows(0, c)
                stage[slot] = rows(0, 1 - c)
            else:
                for s in range(4):
                    mine_ref[s] = rows(s, c)
                    stage[slot, s] = rows(s, 1 - c)
            send(slot, j).start()

        @pl.when(jnp.logical_and(j == nj - 1, k == nt - 1))
        def _():
            for jj in range(max(nj - 2, 0), nj):
                send(jj % 2, jj).wait_send()
            got = sib_ref.at[lead + (pl.ds(col_block * tn, nb),)]
            _remote(got, got, ssem.at[0], rsem, sibling).wait_recv()

    in_specs = [pl.BlockSpec((tt, ka), lambda j, k: (k, 0)), pl.BlockSpec((tt, tn), lambda j, k: (k, j))]
    args = [a, b]
    aliases = {}
    if into is not None:
        in_specs += [ANY, ANY]
        args += list(into)
        aliases = {2: 0, 3: 1}
    mine_spec = pl.BlockSpec(tile, (lambda j, k: (0, j + col_block)) if col else (lambda j, k: (0, 0, j + col_block)))
    return _call(
        body, name=name, grid=(nj, nt), out_shape=(S(half, BF), S(half, BF)),
        in_specs=in_specs, out_specs=(mine_spec, ANY),
        scratch_shapes=[pltpu.VMEM((ka, tn), F32), pltpu.VMEM((2,) + tile, BF), pltpu.SemaphoreType.DMA((2,)),
                        pltpu.SemaphoreType.DMA],
        aliases=aliases, sem=("arbitrary", "arbitrary"), args=args, hook=hook)


def _xattn_bwd(dxo, x, gn, q, kv, wq, wo, hook=None):
    t, d = x.shape
    tm = 256
    hd = d // MEM_HEADS
    nkv = kv.shape[0]

    def body(dxo_ref, x_ref, gn_ref, q_ref, kv_ref, wq_ref, wo_ref, dx_ref, dgn_ref, dkv_ref, dxh_ref, h_ref, dq_ref):
        @pl.when(pl.program_id(0) == 0)
        def _():
            dgn_ref[...] = jnp.zeros_like(dgn_ref)
            dkv_ref[...] = jnp.zeros_like(dkv_ref)
        dxov = dxo_ref[...]
        dxh = dxov.astype(BF)
        dxh_ref[...] = dxh
        do = _nt(dxh, wo_ref[...]).astype(BF)
        dqs = []
        for hh in range(MEM_HEADS):
            lo, hi = hh * hd, (hh + 1) * hd
            qh = q_ref[:, lo:hi]
            kh = kv_ref[:, lo:hi]
            vh = kv_ref[:, d + lo:d + hi]
            doh = do[:, lo:hi]
            p = _xattn_probs(qh, kh)
            dp = _nt(doh, vh)
            ds = (p * (dp - jnp.sum(p * dp, axis=-1, keepdims=True)) * (hd ** -0.5)).astype(BF)
            dqs.append(_nn(ds, kh))
            dkv_ref[:, lo:hi] += _tn(ds, qh)
            dkv_ref[:, d + lo:d + hi] += _tn(p.astype(BF), doh)
        dq = jnp.concatenate(dqs, axis=1).astype(BF)
        dq_ref[...] = dq
        n, r = _rms(x_ref[...])
        gnv = gn_ref[...]
        h_ref[...] = (n * gnv).astype(BF)
        dx, dgn = _rms_bwd(_nt(dq, wq_ref[...]), n, r, gnv)
        dx_ref[...] = dxov + dx
        dgn_ref[...] += dgn

    row = pl.BlockSpec((tm, d), lambda i: (i, 0))
    return _call(
        body, name="xattn_bwd", grid=(t // tm,),
        out_shape=(S((t, d), F32), S((1, d), F32), S((nkv, 2 * d), F32), S((t, d), BF), S((t, d), BF), S((t, d), BF)),
        in_specs=[row, row, _resident((1, d)), row, _resident(kv.shape), _resident(wq.shape), _resident(wo.shape)],
        out_specs=(row, _acc_spec((1, d)), _acc_spec((nkv, 2 * d)), row, row, row),
        sem=("arbitrary",), args=(dxo, x, gn, q, kv, wq, wo), hook=hook)


def _mem_bwd(dkv, mh, mem, gm, wkv, pair=False):
    m, d = mem.shape
    rows, cols = wkv.shape
    rh = rows // 2

    def body(dkv_ref, mh_ref, mem_ref, gm_ref, w_ref, *outs):
        dkvb = dkv_ref[...].astype(BF)
        dw = _tn(mh_ref[...], dkvb).astype(BF)
        dmh = _nt(dkvb, w_ref[...])
        n, _ = _rms(mem_ref[...])
        if not pair:
            dw_ref, dgm_ref = outs
            dw_ref[...] = dw
        else:
            mine_ref, sib_ref, dgm_ref, whole, ssem, rsem = outs
            x, y, c = _place()
            whole[...] = dw
            cp = _remote(whole.at[pl.ds(pl.multiple_of((1 - c) * rh, 16), rh), :], sib_ref, ssem, rsem, (x, y, 1 - c))
            cp.start()
            mine_ref[...] = whole[pl.ds(pl.multiple_of(c * rh, 16), rh), :]
            cp.wait()
        dgm_ref[...] = jnp.sum(dmh * n, axis=0, keepdims=True)

    if not pair:
        return pl.pallas_call(body, name="mem_bwd", out_shape=(S(wkv.shape, BF), S((1, d), F32)),
                              compiler_params=_params())(dkv, mh, mem, gm, wkv)
    mine, sib, dgm = pl.pallas_call(
        body, name="mem_bwd", out_shape=(S((rh, cols), BF), S((rh, cols), BF), S((1, d), F32)),
        in_specs=[VMEM_SPEC] * 5, out_specs=(VMEM_SPEC, ANY, VMEM_SPEC),
        scratch_shapes=[pltpu.VMEM((rows, cols), BF), pltpu.SemaphoreType.DMA, pltpu.SemaphoreType.DMA],
        compiler_params=_params())(dkv, mh, mem, gm, wkv)
    return (mine, sib), dgm


def _mix_out_bwd(dxo, e, attn, conv_w, w_out, hook=None):
    t, d = attn.shape
    tm = 256
    nt = t // tm
    f32 = lambda ref: ref[...].astype(F32)

    def body(dxo_ref, dxn_ref, c_ref, b_ref, u_ref, ga_ref, gc_ref, ch_ref, uh_ref, bn_ref, gcn_ref,
             attn_ref, cw_ref, w_ref, dattn_ref, de_ref, dcw_ref, dxh_ref):
        i = pl.program_id(0)

        @pl.when(i == 0)
        def _():
            dcw_ref[...] = jnp.zeros_like(dcw_ref)
        dxh = dxo_ref[...].astype(BF)
        dxh_ref[...] = dxh
        w = w_ref[...]
        dm = _nt(dxh, w)
        dmn = _nt(dxn_ref[...].astype(BF), w)
        cv, bv, uv = f32(c_ref), f32(b_ref), f32(u_ref)
        sga = _sigmoid(f32(ga_ref))
        sgc = _sigmoid(f32(gc_ref))
        z = cv * uv
        z1, z2 = _conv_taps(z, f32(ch_ref) * f32(uh_ref), i == 0)
        w0, w1, w2 = cw_ref[0:1, :], cw_ref[1:2, :], cw_ref[2:3, :]
        s = w0 * z2 + w1 * z1 + w2 * z
        av = attn_ref[...]
        dattn_ref[...] = (dm * sga).astype(BF)
        dconv = dm * sgc
        ds = dconv * bv
        dsn = jnp.where(i == nt - 1, 0.0, dmn * _sigmoid(gcn_ref[0:8, :].astype(F32)) * bn_ref[0:8, :].astype(F32))
        row = lax.broadcasted_iota(I32, (tm, 1), 0)
        dsp1 = jnp.where(row == tm - 1, dsn[0:1, :], pltpu.roll(ds, tm - 1, 0))
        dsp2 = jnp.where(row == tm - 2, dsn[0:1, :], jnp.where(row == tm - 1, dsn[1:2, :], pltpu.roll(ds, tm - 2, 0)))
        dz = w2 * ds + w1 * dsp1 + w0 * dsp2
        de_ref[:, 0:d] = (dz * uv).astype(BF)
        de_ref[:, d:2 * d] = (dconv * s).astype(BF)
        de_ref[:, 2 * d:3 * d] = (dz * cv).astype(BF)
        de_ref[:, 3 * d:4 * d] = (dm * av * sga * (1.0 - sga)).astype(BF)
        de_ref[:, 4 * d:5 * d] = (dm * (bv * s) * sgc * (1.0 - sgc)).astype(BF)
        dcw_ref[0:1, :] += jnp.sum(ds * z2, axis=0, keepdims=True)
        dcw_ref[1:2, :] += jnp.sum(ds * z1, axis=0, keepdims=True)
        dcw_ref[2:3, :] += jnp.sum(ds * z, axis=0, keepdims=True)

    ecol = lambda cb: pl.BlockSpec((tm, d), lambda i: (i, cb))
    prev = lambda cb: pl.BlockSpec((HALO, d), lambda i: (jnp.maximum(i * (tm // HALO) - 1, 0), cb))
    nxt = lambda rows, cb: pl.BlockSpec((rows, d), lambda i: (jnp.minimum((i + 1) * (tm // rows), t // rows - 1), cb))
    row = pl.BlockSpec((tm, d), lambda i: (i, 0))
    return _call(
        body, name="mix_out_bwd", grid=(nt,),
        out_shape=(S((t, d), BF), S((t, 5 * d), BF), S((8, d), F32), S((t, d), BF)),
        in_specs=[row, nxt(8, 0), ecol(0), ecol(1), ecol(2), ecol(3), ecol(4), prev(0), prev(2), nxt(HALO, 1), nxt(HALO, 4),
                  row, _resident(conv_w.shape), _resident(w_out.shape)],
        out_specs=(row, pl.BlockSpec((tm, 5 * d), lambda i: (i, 0)), _acc_spec((8, d)), row),
        sem=("arbitrary",), args=(dxo, dxo, e, e, e, e, e, e, e, e, e, attn, conv_w, w_out), hook=hook)


def _swa_bwd(qkv, dattn, pos_col, pos_row, bias_t, sinks, hook=None):
    t = qkv.shape[0]
    nb = t // BLOCK
    qw = SWA_HEADS * HEAD_DIM
    kw = SWA_KV_HEADS * HEAD_DIM

    def body(q_ref, kp_ref, kc_ref, vp_ref, vc_ref, do_ref, pq_ref, pkp_ref, pkc_ref, bias_ref, sink_ref,
             dq_ref, dkv_ref, gb_ref, dsk_ref, carry_ref):
        b = pl.program_id(0)

        @pl.when(b == 0)
        def _():
            gb_ref[...] = jnp.zeros_like(gb_ref)
            dsk_ref[...] = jnp.zeros_like(dsk_ref)
            carry_ref[...] = jnp.zeros_like(carry_ref)

        @pl.when(b < nb)
        def _():
            vis = _swa_visible(b, pq_ref, pkp_ref, pkc_ref)
            k2 = jnp.concatenate([kp_ref[...], kc_ref[...]], axis=0)
            v2 = jnp.concatenate([vp_ref[...], vc_ref[...]], axis=0)
            for hk in range(SWA_KV_HEADS):
                lo, hi = hk * HEAD_DIM, (hk + 1) * HEAD_DIM
                kh = k2[:, lo:hi]
                vh = v2[:, lo:hi]
                qg = _group_heads(q_ref, hk)
                dog = _group_heads(do_ref, hk)
                pn, psn = _swa_probs(qg, kh, vis, bias_ref[hk], _group_sinks(sink_ref, hk))
                dp = _nt(dog, vh)
                delta = jnp.sum(pn * dp, axis=-1, keepdims=True)
                ds = pn * (dp - delta)
                gb_ref[hk] += ds
                dsk_ref[hk] += -psn * delta
                dsb = (ds * (HEAD_DIM ** -0.5)).astype(BF)
                dqg = _nn(dsb, kh).astype(BF)
                for g in range(SWA_GROUP):
                    h = hk * SWA_GROUP + g
                    dq_ref[:, h * HEAD_DIM:(h + 1) * HEAD_DIM] = dqg[g * BLOCK:(g + 1) * BLOCK]
                dk = _tn(dsb, qg)
                dv = _tn(pn.astype(BF), dog)
                dkv_ref[:, lo:hi] = (carry_ref[:, lo:hi] + dk[0:BLOCK]).astype(BF)
                dkv_ref[:, kw + lo:kw + hi] = (carry_ref[:, kw + lo:kw + hi] + dv[0:BLOCK]).astype(BF)
                carry_ref[:, lo:hi] = dk[BLOCK:2 * BLOCK]
                carry_ref[:, kw + lo:kw + hi] = dv[BLOCK:2 * BLOCK]

        @pl.when(b == nb)
        def _():
            dkv_ref[...] = carry_ref[...].astype(BF)

    cur = lambda b: jnp.minimum(b, nb - 1)
    prev = lambda b: jnp.maximum(cur(b) - 1, 0)
    return _call(
        body, name="swa_bwd", grid=(nb + 1,),
        out_shape=(S((t, qw), BF), S((t, 2 * kw), BF), S((SWA_KV_HEADS, GROUP_ROWS, 2 * BLOCK), F32),
                   S((SWA_KV_HEADS, GROUP_ROWS, 1), F32)),
        in_specs=[
            pl.BlockSpec((BLOCK, qw), lambda b: (cur(b), 0)),
            pl.BlockSpec((BLOCK, kw), lambda b: (prev(b), qw // kw)),
            pl.BlockSpec((BLOCK, kw), lambda b: (cur(b), qw // kw)),
            pl.BlockSpec((BLOCK, kw), lambda b: (prev(b), qw // kw + 1)),
            pl.BlockSpec((BLOCK, kw), lambda b: (cur(b), qw // kw + 1)),
            pl.BlockSpec((BLOCK, qw), lambda b: (cur(b), 0)),
            pl.BlockSpec((BLOCK, 1), lambda b: (cur(b), 0)),
            pl.BlockSpec((1, BLOCK), lambda b: (0, prev(b))),
            pl.BlockSpec((1, BLOCK), lambda b: (0, cur(b))),
            _resident(bias_t.shape),
            SMEM_SPEC,
        ],
        out_specs=(
            pl.BlockSpec((BLOCK, qw), lambda b: (cur(b), 0)),
            pl.BlockSpec((BLOCK, 2 * kw), lambda b: (jnp.maximum(b - 1, 0), 0)),
            _acc_spec((SWA_KV_HEADS, GROUP_ROWS, 2 * BLOCK)),
            _acc_spec((SWA_KV_HEADS, GROUP_ROWS, 1)),
        ),
        scratch_shapes=[pltpu.VMEM((BLOCK, 2 * kw), F32)],
        sem=("arbitrary",), args=(qkv, qkv, qkv, qkv, qkv, dattn, pos_col, pos_row, pos_row, bias_t, sinks), hook=hook)


def _bias_reduce(gb, dsk):
    def body(gb_ref, dsk_ref, drb_ref, dsink_ref):
        bucket = _t5_bucket(_block_rel())
        for b in range(REL_BUCKETS):
            mask = bucket == b
            for h in range(SWA_HEADS):
                drb_ref[b, h] = jnp.sum(jnp.where(mask, gb_ref[h], 0.0))
        for h in range(SWA_HEADS):
            dsink_ref[0, h] = jnp.sum(dsk_ref[h])

    return pl.pallas_call(
        body, name="bias_reduce", out_shape=(S((REL_BUCKETS, SWA_HEADS), F32), S((1, SWA_HEADS), F32)),
        in_specs=[VMEM_SPEC, VMEM_SPEC], out_specs=(SMEM_SPEC, SMEM_SPEC),
    )(gb, dsk)


def _mix_in_bwd(dq, dkv, de, x, gn, w_in, dxo):
    t, d = x.shape
    tm = 256
    nq, nkv, ne = dq.shape[1], dkv.shape[1], de.shape[1]

    def body(dq_ref, dkv_ref, de_ref, x_ref, gn_ref, w_ref, dxo_ref, dx_ref, dgn_ref, h_ref):
        @pl.when(pl.program_id(0) == 0)
        def _():
            dgn_ref[...] = jnp.zeros_like(dgn_ref)
        dh = _nt(dq_ref[...], w_ref[:, 0:nq]) + _nt(dkv_ref[...], w_ref[:, nq:nq + nkv])
        for c0 in range(0, ne, 1024):
            dh = dh + _nt(de_ref[:, c0:c0 + 1024], w_ref[:, nq + nkv + c0:nq + nkv + c0 + 1024])
        n, r = _rms(x_ref[...])
        gnv = gn_ref[...]
        h_ref[...] = (n * gnv).astype(BF)
        dx, dgn = _rms_bwd(dh, n, r, gnv)
        dx_ref[...] = dxo_ref[...] + dx
        dgn_ref[...] += dgn

    row = pl.BlockSpec((tm, d), lambda i: (i, 0))
    wide = lambda w: pl.BlockSpec((tm, w), lambda i: (i, 0))
    return pl.pallas_call(
        body, name="mix_in_bwd", grid=(t // tm,),
        out_shape=(S((t, d), F32), S((1, d), F32), S((t, d), BF)),
        in_specs=[wide(nq), wide(nkv), wide(ne), row, _resident((1, d)), _resident(w_in.shape), row],
        out_specs=(row, _acc_spec((1, d)), row),
        compiler_params=_params(("arbitrary",)),
    )(dq, dkv, de, x, gn, w_in, dxo)


def _row_tile(rows):
    for tr in (256, 352, 128, 64, 16, 8):
        if rows % tr == 0:
            return tr
    return rows


def _add_bf16(a, b, name):
    rows, cols = a.shape
    tr = _row_tile(rows)

    def body(a_ref, b_ref, o_ref):
        o_ref[...] = (a_ref[...].astype(F32) + b_ref[...].astype(F32)).astype(BF)

    blk = pl.BlockSpec((tr, cols), lambda i: (i, 0))
    return pl.pallas_call(body, name=name, grid=(rows // tr,), out_shape=S((rows, cols), BF),
                          in_specs=[blk, blk], out_specs=blk, compiler_params=_params(("parallel",)))(a, b)


def _adam_update(w, g, m, v):
    mn = ADAM_B1 * m + (1.0 - ADAM_B1) * g
    vn = ADAM_B2 * v + (1.0 - ADAM_B2) * (g * g)
    m_hat = mn / (1.0 - ADAM_B1 ** ADAM_STEP)
    v_hat = vn / (1.0 - ADAM_B2 ** ADAM_STEP)
    return -ADAM_LR * (m_hat / (jnp.sqrt(v_hat) + ADAM_EPS) + ADAM_WD * w), mn, vn


def _adamw(quads, name, steps, echo=False, hook=None):
    n_out = 4 if echo else 3

    def body(*refs):
        ins, outs = refs[:4 * len(quads)], refs[4 * len(quads):]
        for q in range(len(quads)):
            w_ref, g_ref, m_ref, v_ref = ins[4 * q:4 * q + 4]
            res = outs[n_out * q:n_out * q + n_out]
            gv = g_ref[...]
            if echo:
                res[0][...] = gv
            res[-3][...], res[-2][...], res[-1][...] = _adam_update(w_ref[...], gv, m_ref[...], v_ref[...])

    in_specs, out_specs, out_shape, args = [], [], [], []
    for quad in quads:
        rows, cols = quad[0].shape
        blk = pl.BlockSpec((rows // steps, cols), lambda i: (i, 0))
        in_specs += [blk] * 4
        out_specs += [blk] * n_out
        out_shape += [S((rows, cols), F32)] * n_out
        args += list(quad)
    res = _call(body, name=name, grid=(steps,), out_shape=tuple(out_shape), in_specs=in_specs, out_specs=tuple(out_specs),
                sem=("parallel",), args=args, hook=hook)
    return [res[n_out * q:n_out * q + n_out] for q in range(len(quads))]


def _place():
    x, y, c = lax.axis_index("x"), lax.axis_index("y"), lax.axis_index("c")
    return x, y, c


OTHER_CHIPS = ((1, 0), (0, 1), (1, 1))


def _flip(v, f):
    return 1 - v if f else v


def _remote(src, dst, ssem, rsem, dev):
    return pltpu.make_async_remote_copy(src_ref=src, dst_ref=dst, send_sem=ssem, recv_sem=rsem,
                                        device_id=dev, device_id_type=MESH)


class _Both:
    def __init__(self, hooks):
        self.hooks = hooks
        self.ins = [a for h in hooks for a in h.ins]
        self.out_shape = tuple(o for h in hooks for o in h.out_shape)
        self.scratch = [x for h in hooks for x in h.scratch]

    def _each(self, ins, outs, scr):
        i = o = s = 0
        for h in self.hooks:
            ni, no, ns = len(h.ins), len(h.out_shape), len(h.scratch)
            yield h, ins[i:i + ni], outs[o:o + no], scr[s:s + ns]
            i, o, s = i + ni, o + no, s + ns

    def start(self, ins, outs, scr):
        for h, *refs in self._each(ins, outs, scr):
            h.start(*refs)

    def finish(self, ins, outs, scr):
        for h, *refs in self._each(ins, outs, scr):
            h.finish(*refs)

    @property
    def results(self):
        return [h.results for h in self.hooks]

    @results.setter
    def results(self, res):
        for h, _, mine, _ in self._each((), tuple(res), ()):
            h.results = mine


def _exchange(hook, name):
    n_in, n_out = len(hook.ins), len(hook.out_shape)

    def body(*refs):
        ins, outs, scr = refs[:n_in], refs[n_in:n_in + n_out], refs[n_in + n_out:]
        hook.start(ins, outs, scr)
        hook.finish(ins, outs, scr)

    hook.results = pl.pallas_call(
        body, name=name, out_shape=tuple(hook.out_shape), in_specs=[ANY] * n_in, out_specs=(ANY,) * n_out,
        scratch_shapes=list(hook.scratch), compiler_params=_params(),
    )(*hook.ins)


class _GatherHook:
    def __init__(self, shards, kinds):
        self.ins, self.kinds, n = list(shards), list(kinds), len(shards)
        self.out_shape = tuple(
            S((w.shape[0], 4 * w.shape[1]), BF) if k == "col" else S((4,) + w.shape, BF) for w, k in zip(shards, kinds))
        dma = pltpu.SemaphoreType.DMA
        self.scratch = ([dma((n, 3)) for _ in range(5)] + [dma((n,)), dma((n,))]
                        + [pltpu.VMEM((3, w.shape[0] // 2, w.shape[1]), BF) for w in shards]
                        + [pltpu.VMEM(w.shape, BF) for w in shards])

    def _window(self, outs, i, s, half):
        rows, cols = self.ins[i].shape
        rh = rows // 2
        start = pl.multiple_of(half * rh, 16)
        if self.kinds[i] == "col":
            return outs[i].at[pl.ds(start, rh), pl.ds(pl.multiple_of(s * cols, 128), cols)]
        return outs[i].at[s, pl.ds(start, rh), :]

    def _copies(self, ins, outs, scr):
        n = len(ins)
        ssem, rsem, fssem, frsem, ksem, lsem, osem = scr[:7]
        land, own = scr[7:7 + n], scr[7 + n:7 + 2 * n]
        x, y, c = _place()
        sibling = (x, y, 1 - c)
        loads, stores, sends, forwards, keeps, passed = [], [], [], [], [], []
        for i in range(n):
            rows, cols = self.ins[i].shape
            rh = rows // 2
            mine = (outs[i].at[:, pl.ds(pl.multiple_of((2 * x + y) * cols, 128), cols)] if self.kinds[i] == "col"
                    else outs[i].at[2 * x + y])
            loads.append(functools.partial(pltpu.make_async_copy, ins[i], own[i], lsem.at[i]))
            stores.append(functools.partial(pltpu.make_async_copy, own[i], mine, osem.at[i]))
            src = ins[i].at[pl.ds(pl.multiple_of(c * rh, 16), rh), :]
            for j, (fx, fy) in enumerate(OTHER_CHIPS):
                px, py = _flip(x, fx), _flip(y, fy)
                sends.append(functools.partial(_remote, src, land[i].at[j], ssem.at[i, j], rsem.at[i, j], (px, py, c)))
                here = self._window(outs, i, 2 * px + py, c)
                forwards.append(functools.partial(_remote, land[i].at[j], here, fssem.at[i, j], frsem.at[i, j], sibling))
                keeps.append(functools.partial(pltpu.make_async_copy, land[i].at[j], here, ksem.at[i, j]))
                there = self._window(outs, i, 2 * px + py, 1 - c)
                passed.append(functools.partial(_remote, there, there, fssem.at[i, j], frsem.at[i, j], sibling))
        return loads, stores, sends, forwards, keeps, passed

    def start(self, ins, outs, scr):
        loads, _, sends, _, _, _ = self._copies(ins, outs, scr)
        for make in sends + loads:
            make().start()

    def finish(self, ins, outs, scr):
        loads, stores, sends, forwards, keeps, passed = self._copies(ins, outs, scr)
        for load, store in zip(loads, stores):
            load().wait()
            store().start()
        for send, forward, keep in zip(sends, forwards, keeps):
            send().wait_recv()
            forward().start()
            keep().start()
        for make in passed:
            make().wait_recv()
        for make in sends + forwards:
            make().wait_send()
        for make in keeps + stores:
            make().wait()


class _ChipsHook:
    def __init__(self, parts, kinds):
        self.ins, self.kinds, n = list(parts), list(kinds), len(parts)
        self.out_shape = tuple(
            S((4, p.shape[0], p.shape[1] // 4), BF) if k == "col" else S(p.shape, BF) for p, k in zip(parts, kinds))
        dma = pltpu.SemaphoreType.DMA
        self.scratch = ([dma((n, 3)), dma((n, 3)), dma((n,)), dma((n,))]
                        + [pltpu.VMEM(o.shape[1:], BF) for o in self.out_shape])

    def _slab(self, ins, i, s):
        _, rows, cols = self.out_shape[i].shape
        if self.kinds[i] == "col":
            return ins[i].at[:, pl.ds(pl.multiple_of(s * cols, 128), cols)]
        return ins[i].at[s]

    def _copies(self, ins, outs, scr):
        ssem, rsem, lsem, osem = scr[:4]
        own = scr[4:]
        x, y, c = _place()
        loads, stores, sends = [], [], []
        for i in range(len(ins)):
            loads.append(functools.partial(pltpu.make_async_copy, self._slab(ins, i, 2 * x + y), own[i], lsem.at[i]))
            stores.append(functools.partial(pltpu.make_async_copy, own[i], outs[i].at[3], osem.at[i]))
            for j, (fx, fy) in enumerate(OTHER_CHIPS):
                px, py = _flip(x, fx), _flip(y, fy)
                sends.append(functools.partial(_remote, self._slab(ins, i, 2 * px + py), outs[i].at[j], ssem.at[i, j],
                                               rsem.at[i, j], (px, py, c)))
        return loads, stores, sends

    def start(self, ins, outs, scr):
        loads, _, sends = self._copies(ins, outs, scr)
        for make in sends + loads:
            make().start()

    def finish(self, ins, outs, scr):
        loads, stores, sends = self._copies(ins, outs, scr)
        for load, store in zip(loads, stores):
            load().wait()
            store().start()
        for make in sends + stores:
            make().wait()


def _sum_share(slabs, name):
    _, rh, cols = slabs.shape
    tr = _row_tile(rh)
    nt = rh // tr

    def body(p_ref, o_ref, stage, lsem, ssem, rsem):
        i = pl.program_id(0)
        x, y, c = _place()
        sibling = (x, y, 1 - c)
        acc = p_ref[3].astype(F32)
        for k in range(3):
            acc = acc + p_ref[k].astype(F32)
        stage[i] = acc

        def copies(k):
            dst = o_ref.at[pl.ds(pl.multiple_of(c * rh + k * tr, 8), tr), :]
            return (pltpu.make_async_copy(stage.at[k], dst, lsem.at[k]),
                    _remote(stage.at[k], dst, ssem.at[k], rsem, sibling))

        for cp in copies(i):
            cp.start()

        @pl.when(i == nt - 1)
        def _():
            for k in range(nt):
                local, remote = copies(k)
                local.wait()
                remote.wait_send()
            got = o_ref.at[pl.ds(pl.multiple_of((1 - c) * rh, 8), rh), :]
            _remote(got, got, ssem.at[0], rsem, sibling).wait_recv()

    return pl.pallas_call(
        body, name=name, grid=(nt,), out_shape=S((2 * rh, cols), F32),
        in_specs=[pl.BlockSpec((4, tr, cols), lambda i: (0, i, 0))], out_specs=ANY,
        scratch_shapes=[pltpu.VMEM((nt, tr, cols), F32), pltpu.SemaphoreType.DMA((nt,)), pltpu.SemaphoreType.DMA((nt,)),
                        pltpu.SemaphoreType.DMA],
        compiler_params=_params(("arbitrary",)))(slabs)


class _SmallSumHook:
    def __init__(self, buf):
        self.ins, self.out_shape = [buf], (S(buf.shape, F32),)
        dma = pltpu.SemaphoreType.DMA
        self.scratch = [pltpu.VMEM((8,) + buf.shape, F32), pltpu.VMEM(buf.shape, F32), dma((7,)), dma((7,)), dma]

    def _sends(self, scr):
        slots, _, ssem, rsem, _ = scr
        x, y, c = _place()
        me = 4 * x + 2 * y + c
        for r in range(1, 8):
            px, py, pc = _flip(x, (r >> 2) & 1), _flip(y, (r >> 1) & 1), _flip(c, r & 1)
            yield (functools.partial(_remote, slots.at[me], slots.at[me], ssem.at[r - 1], rsem.at[r - 1], (px, py, pc)),
                   functools.partial(_remote, slots.at[me], slots.at[4 * px + 2 * py + pc], ssem.at[r - 1],
                                     rsem.at[r - 1], (px, py, pc)))

    def start(self, ins, outs, scr):
        slots, _, _, _, lsem = scr
        x, y, c = _place()
        load = pltpu.make_async_copy(ins[0], slots.at[4 * x + 2 * y + c], lsem)
        load.start()
        load.wait()
        for send, _ in self._sends(scr):
            send().start()

    def finish(self, ins, outs, scr):
        slots, total, _, _, lsem = scr
        for _, arrival in self._sends(scr):
            arrival().wait_recv()
        for send, _ in self._sends(scr):
            send().wait_send()
        acc = slots[0]
        for k in range(1, 8):
            acc = acc + slots[k]
        total[...] = acc
        store = pltpu.make_async_copy(total, outs[0], lsem)
        store.start()
        store.wait()


BIG = ("ffn1_w_gu", "ffn1_w_down", "w_in", "w_out", "xattn_wq", "xattn_wkv", "xattn_wo", "ffn2_w_gu", "ffn2_w_down")
KIND = {"ffn1_w_gu": "col", "ffn1_w_down": "row", "w_in": "col", "w_out": "row", "xattn_wq": "row",
        "xattn_wkv": "col", "xattn_wo": "row", "ffn2_w_gu": "col", "ffn2_w_down": "row"}
WEIGHTS = ("rel_bias", "ffn1_norm", "ffn1_w_gu", "ffn1_w_down", "mix_norm", "w_in", "sinks", "conv_w", "w_out",
           "xattn_norm", "mem_norm", "xattn_wq", "xattn_wkv", "xattn_wo", "ffn2_norm", "ffn2_w_gu", "ffn2_w_down",
           "final_norm")
SMALL_ROWS = 16
GAIN_ROW = {"ffn1_norm": 0, "mix_norm": 1, "xattn_norm": 2, "mem_norm": 3, "ffn2_norm": 4, "final_norm": 5}
CONV_ROW, SINK_ROW, BIAS_ROW, LOSS_ROW = 6, 9, 10, 11


def _rows_block(rows, d):
    buf = jnp.zeros((SMALL_ROWS, d), F32)
    for r, v in rows.items():
        buf = lax.dynamic_update_slice(buf, v.reshape(1, -1).astype(F32), (r, 0))
    return buf


def _local_step(x, mem, pos, target, w, gains, rel_bias, sinks, conv_w, shards=None):
    t, d = x.shape
    dist = shards is not None
    w = dict(w)
    grads, slabs = {}, {}
    pos_col = pos.reshape(t, 1)
    pos_row = pos.reshape(1, t)
    bias_t = _bias_build(rel_bias).reshape(SWA_KV_HEADS, GROUP_ROWS, 2 * BLOCK)

    def gather(names):
        return _GatherHook([shards[k] for k in names], [KIND[k] for k in names]) if dist else None

    def gathered(names, hook):
        if dist:
            for k, gw in zip(names, hook.results):
                w[k] = gw if KIND[k] == "col" else gw.reshape(-1, gw.shape[-1])

    def dw(k, a, b, tn, name, **kw):
        return _dw_pair(a, b, tn, name, KIND[k], **kw) if dist else _dw(a, b, tn, name, **kw)

    def pair_sums(names):
        if not dist:
            return None
        parts = {}
        for k in names:
            mine, sib = grads[k]
            if KIND[k] == "row":
                mine, sib = mine.reshape(-1, mine.shape[-1]), sib.reshape(-1, sib.shape[-1])
            p = _add_bf16(mine, sib, "pair_sum_" + k)
            parts[k] = p if KIND[k] == "col" else p.reshape(4, -1, p.shape[-1])
        return parts

    def chips(names, parts):
        return _ChipsHook([parts[k] for k in names], [KIND[k] for k in names]) if dist else None

    def reduced(names, hook):
        if dist:
            slabs.update(zip(names, hook.results))

    names = ("w_in", "w_out")
    hook = gather(names)
    x1, g1, u1 = _ffn_fwd(x, gains["ffn1_norm"], w["ffn1_w_gu"], w["ffn1_w_down"], "ffn1_fwd", hook)
    gathered(names, hook)
    names = ("xattn_wq", "xattn_wkv", "xattn_wo")
    hook = gather(names)
    qkv, e = _mix_proj(x1, gains["mix_norm"], w["w_in"], hook)
    gathered(names, hook)
    names = ("ffn2_w_gu",)
    hook = gather(names)
    attn = _swa_fwd(qkv, pos_col, pos_row, bias_t, sinks, hook)
    gathered(names, hook)
    names = ("ffn2_w_down",)
    hook = gather(names)
    x2, merged = _mix_out_fwd(e, attn, conv_w, w["w_out"], x1, hook)
    gathered(names, hook)
    mh, kv = _mem_kv(mem, gains["mem_norm"], w["xattn_wkv"])
    x3, qx, o = _xattn_fwd(x2, gains["xattn_norm"], w["xattn_wq"], kv, w["xattn_wo"])
    x4, g2, u2 = _ffn_fwd(x3, gains["ffn2_norm"], w["ffn2_w_gu"], w["ffn2_w_down"], "ffn2_fwd")
    dx4, d_final, loss = _loss_bwd(x4, gains["final_norm"], target)

    dx3, d_ffn2, dgu2, a2, h4, dyh4 = _ffn_bwd(dx4, x3, gains["ffn2_norm"], g2, u2, w["ffn2_w_gu"], w["ffn2_w_down"],
                                               "ffn2_bwd")
    grads["ffn2_w_gu"] = dw("ffn2_w_gu", h4, dgu2, 1408, "dw_ffn2_gu")
    grads["ffn2_w_down"] = dw("ffn2_w_down", a2, dyh4, 512, "dw_ffn2_down")
    parts = pair_sums(("ffn2_w_gu", "ffn2_w_down"))
    hook = chips(("ffn2_w_gu",), parts)
    dx2, d_xattn, dkv, dxh3, h3, dqx = _xattn_bwd(dx3, x2, gains["xattn_norm"], qx, kv, w["xattn_wq"], w["xattn_wo"], hook)
    reduced(("ffn2_w_gu",), hook)
    grads["xattn_wo"] = dw("xattn_wo", o, dxh3, 1024, "dw_wo")
    grads["xattn_wq"] = dw("xattn_wq", h3, dqx, 1024, "dw_wq")
    grads["xattn_wkv"], d_mem = _mem_bwd(dkv, mh, mem, gains["mem_norm"], w["xattn_wkv"], dist)
    hook = chips(("ffn2_w_down",), parts)
    dattn, de, dcw, dxh2 = _mix_out_bwd(dx2, e, attn, conv_w, w["w_out"], hook)
    reduced(("ffn2_w_down",), hook)
    grads["w_out"] = dw("w_out", merged, dxh2, 1024, "dw_wout")
    names = ("xattn_wo", "xattn_wq", "xattn_wkv", "w_out")
    hook = chips(names, pair_sums(names))
    dq, dkvs, gb, dsk = _swa_bwd(qkv, dattn, pos_col, pos_row, bias_t, sinks, hook)
    reduced(names, hook)
    d_rel_bias, d_sinks = _bias_reduce(gb.reshape(SWA_HEADS, BLOCK, 2 * BLOCK), dsk.reshape(SWA_HEADS, BLOCK, 1))
    dx1, d_mix, h2 = _mix_in_bwd(dq, dkvs, de, x1, gains["mix_norm"], w["w_in"], dx2)
    n_in = w["w_in"].shape[1]
    gw = dw("w_in", h2, dq, 512, "dw_win_q", ncols=n_in)
    gw = dw("w_in", h2, dkvs, 512, "dw_win_kv", into=gw, col_block=dq.shape[1] // 512, ncols=n_in)
    grads["w_in"] = dw("w_in", h2, de, 512, "dw_win_e", into=gw, col_block=(dq.shape[1] + dkvs.shape[1]) // 512,
                       ncols=n_in)
    names = ("w_in",)
    hook = chips(names, pair_sums(names))
    dx0, d_ffn1, dgu1, a1, h1, dyh1 = _ffn_bwd(dx1, x, gains["ffn1_norm"], g1, u1, w["ffn1_w_gu"], w["ffn1_w_down"],
                                               "ffn1_bwd", hook)
    reduced(names, hook)
    grads["ffn1_w_down"] = dw("ffn1_w_down", a1, dyh1, 512, "dw_ffn1_down")
    names = ("ffn1_w_down",)
    hook = chips(names, pair_sums(names))
    grads["ffn1_w_gu"] = dw("ffn1_w_gu", h1, dgu1, 1408, "dw_ffn1_gu", **({"hook": hook} if dist else {}))
    reduced(names, hook)
    rows = {0: d_ffn1, 1: d_mix, 2: d_xattn, 3: d_mem, 4: d_ffn2, 5: d_final, SINK_ROW: d_sinks, BIAS_ROW: d_rel_bias,
            LOSS_ROW: loss[0, 0:1]}
    rows.update({CONV_ROW + j: dcw[j] for j in range(3)})
    last = ("ffn1_w_gu",)
    return dx0, (slabs if dist else grads), _rows_block(rows, d), chips(last, pair_sums(last))


def kernel(x, mem, positions, rel_bias, ffn1_norm, ffn1_w_gu, ffn1_w_down, mix_norm, w_in, sinks, conv_w, w_out, xattn_norm, mem_norm, xattn_wq, xattn_wkv, xattn_wo, ffn2_norm, ffn2_w_gu, ffn2_w_down, final_norm, loss_target, m_rel_bias, m_ffn1_norm, m_ffn1_w_gu, m_ffn1_w_down, m_mix_norm, m_w_in, m_sinks, m_conv_w, m_w_out, m_xattn_norm, m_mem_norm, m_xattn_wq, m_xattn_wkv, m_xattn_wo, m_ffn2_norm, m_ffn2_w_gu, m_ffn2_w_down, m_final_norm, v_rel_bias, v_ffn1_norm, v_ffn1_w_gu, v_ffn1_w_down, v_mix_norm, v_w_in, v_sinks, v_conv_w, v_w_out, v_xattn_norm, v_mem_norm, v_xattn_wq, v_xattn_wkv, v_xattn_wo, v_ffn2_norm, v_ffn2_w_gu, v_ffn2_w_down, v_final_norm):
    args = dict(locals())
    wts = {k: args[k] for k in WEIGHTS}
    mom = {k: args["m_" + k] for k in WEIGHTS}
    var = {k: args["v_" + k] for k in WEIGHTS}
    d = x.shape[-1]
    s_me = 2 * lax.axis_index("x") + lax.axis_index("y")

    shards = {k: wts[k][0].astype(BF) for k in BIG}
    first = ("ffn1_w_gu", "ffn1_w_down")
    cw_cols = conv_w.shape[-1]
    placed = lax.dynamic_update_slice(jnp.zeros((SMALL_ROWS, d), F32), 0.5 * conv_w[0], (0, s_me * cw_cols))
    head = _Both([_GatherHook([shards[k] for k in first], [KIND[k] for k in first]), _SmallSumHook(placed)])
    _exchange(head, "gather_ffn1")
    gathered, (conv_sum,) = head.results
    whole = {k: (gw if KIND[k] == "col" else gw.reshape(-1, gw.shape[-1])) for k, gw in zip(first, gathered)}
    conv_whole = conv_sum[0:3]

    gains = {k: wts[k].reshape(1, d) for k in GAIN_ROW}
    dx0, slabs, small, last_chips = _local_step(x[0], mem[0], positions[0], loss_target[0], whole, gains, rel_bias, sinks,
                                               conv_whole, shards)

    late = ("ffn1_w_gu", "ffn1_w_down", "ffn2_w_down")
    early = tuple(k for k in BIG if k not in late)
    shard_grads = {k: _sum_share(slabs[k], "sum_share_" + k) for k in BIG if k != "ffn1_w_gu"}
    tail = _Both([last_chips, _SmallSumHook(small)])
    quad = lambda k: (wts[k][0], shard_grads[k], mom[k][0], var[k][0])
    updates = dict(zip(early, _adamw([quad(k) for k in early], "adamw_early", 16, echo=True, hook=tail)))
    (last_slabs,), (small_sum,) = tail.results
    shard_grads["ffn1_w_gu"] = _sum_share(last_slabs, "sum_share_ffn1_w_gu")
    updates.update(zip(late, _adamw([quad(k) for k in late], "adamw_late", 8, echo=True)))
    loss = small_sum[LOSS_ROW, 0]

    out_g, out_d, out_m, out_v = {}, {}, {}, {}
    for k in BIG:
        out_g[k], out_d[k], out_m[k], out_v[k] = (a[None] for a in updates[k])

    conv_g = lax.dynamic_slice(small_sum, (CONV_ROW, s_me * cw_cols), (3, cw_cols))

    def pack(src, conv_block):
        rows = {r: src[k] for k, r in GAIN_ROW.items()}
        rows.update({CONV_ROW + j: conv_block[j] for j in range(3)})
        rows[SINK_ROW], rows[BIAS_ROW] = src["sinks"], src["rel_bias"]
        return _rows_block(rows, d)

    small_g = {k: small_sum[r] for k, r in GAIN_ROW.items()}
    small_g["sinks"] = small_sum[SINK_ROW, 0:sinks.size]
    small_g["rel_bias"] = small_sum[BIAS_ROW, 0:rel_bias.size]
    gp = pack(small_g, conv_g)
    (dl, mn, vn), = _adamw([(pack(wts, conv_w[0]), gp, pack(mom, m_conv_w[0]), pack(var, v_conv_w[0]))], "adamw_small", 1)

    def unpack(buf, k):
        if k in GAIN_ROW:
            return buf[GAIN_ROW[k]].reshape(wts[k].shape)
        if k == "conv_w":
            return buf[CONV_ROW:CONV_ROW + 3, 0:cw_cols][None]
        if k == "sinks":
            return buf[SINK_ROW, 0:sinks.size].reshape(sinks.shape)
        return buf[BIAS_ROW, 0:rel_bias.size].reshape(rel_bias.shape)

    for k in WEIGHTS:
        if k not in KIND:
            out_g[k], out_d[k], out_m[k], out_v[k] = unpack(gp, k), unpack(dl, k), unpack(mn, k), unpack(vn, k)

    return (loss, dx0[None], *[out_g[k] for k in WEIGHTS], *[out_d[k] for k in WEIGHTS],
            *[out_m[k] for k in WEIGHTS], *[out_v[k] for k in WEIGHTS])
```

```python
import functools
import math

import jax
import jax.numpy as jnp
from jax import lax
from jax.experimental import pallas as pl
from jax.experimental.pallas import tpu as pltpu

BF = jnp.bfloat16
F32 = jnp.float32
I32 = jnp.int32
S = jax.ShapeDtypeStruct

EPS = 1e-6
NEG = -1e30
POS_PAD = 1 << 30
WINDOW = 128
BLOCK = 128
HEAD_DIM = 64
SWA_HEADS = 16
SWA_KV_HEADS = 4
SWA_GROUP = SWA_HEADS // SWA_KV_HEADS
MEM_HEADS = 4
REL_BUCKETS = 32
REL_MAX_DIST = 128
ADAM_LR = 0.001
ADAM_B1 = 0.9
ADAM_B2 = 0.999
ADAM_EPS = 1e-08
ADAM_WD = 0.01
ADAM_STEP = 10

V7X_VMEM_LIMIT_BYTES = 56 * 1024 * 1024
MESH = pl.DeviceIdType.MESH
ANY = pl.BlockSpec(memory_space=pl.ANY)
VMEM_SPEC = pl.BlockSpec(memory_space=pltpu.VMEM)
SMEM_SPEC = pl.BlockSpec(memory_space=pltpu.SMEM)


def _params(sem=None):
    return pltpu.CompilerParams(dimension_semantics=sem, vmem_limit_bytes=V7X_VMEM_LIMIT_BYTES)


def _resident(shape):
    nd = len(shape)
    return pl.BlockSpec(shape, lambda *_: (0,) * nd, pipeline_mode=pl.Buffered(1))


def _acc_spec(shape):
    nd = len(shape)
    return pl.BlockSpec(shape, lambda *_: (0,) * nd)


def _call(body, *, name, grid, out_shape, in_specs, out_specs, args, sem, scratch_shapes=(), aliases=None, hook=None):
    aliases = aliases or {}
    if hook is None:
        return pl.pallas_call(body, name=name, grid=grid, out_shape=out_shape, in_specs=in_specs, out_specs=out_specs,
                              scratch_shapes=list(scratch_shapes), input_output_aliases=aliases,
                              compiler_params=_params(sem))(*args)
    n_in, n_out, n_scr = len(in_specs), len(out_shape), len(scratch_shapes)
    h_in, h_out = len(hook.ins), len(hook.out_shape)

    def at_step(pick):
        conds = [pl.program_id(ax) == pick(size) for ax, size in enumerate(grid)]
        return functools.reduce(jnp.logical_and, conds)

    def hosted(*refs):
        k_in, x_in = refs[:n_in], refs[n_in:n_in + h_in]
        o0 = n_in + h_in
        k_out, x_out = refs[o0:o0 + n_out], refs[o0 + n_out:o0 + n_out + h_out]
        s0 = o0 + n_out + h_out
        k_scr, x_scr = refs[s0:s0 + n_scr], refs[s0 + n_scr:]

        @pl.when(at_step(lambda size: 0))
        def _():
            hook.start(x_in, x_out, x_scr)
        body(*k_in, *k_out, *k_scr)

        @pl.when(at_step(lambda size: size - 1))
        def _():
            hook.finish(x_in, x_out, x_scr)

    res = pl.pallas_call(
        hosted, name=name, grid=grid, out_shape=tuple(out_shape) + tuple(hook.out_shape),
        in_specs=list(in_specs) + [ANY] * h_in, out_specs=tuple(out_specs) + (ANY,) * h_out,
        scratch_shapes=list(scratch_shapes) + list(hook.scratch), input_output_aliases=aliases,
        compiler_params=_params(("arbitrary",) * len(grid)),
    )(*args, *hook.ins)
    hook.results = res[n_out:]
    return res[:n_out]


def _nn(a, b):
    return jnp.dot(a, b, preferred_element_type=F32)


def _nt(a, b):
    return lax.dot_general(a, b, (((1,), (1,)), ((), ())), preferred_element_type=F32)


def _tn(a, b):
    return lax.dot_general(a, b, (((0,), (0,)), ((), ())), preferred_element_type=F32)


def _sigmoid(v):
    return 1.0 / (1.0 + jnp.exp(-v))


def _rms(x):
    r = lax.rsqrt(jnp.mean(x * x, axis=-1, keepdims=True) + EPS)
    return x * r, r


def _rms_bwd(dh, n, r, g):
    dn = dh * g
    dx = r * (dn - n * jnp.mean(dn * n, axis=-1, keepdims=True))
    return dx, jnp.sum(dh * n, axis=0, keepdims=True)


def _ffn_fwd(x, gn, wgu, wd, name, hook=None):
    t, d = x.shape
    f = wd.shape[0]
    tm, fc = 256, 1408

    def body(x_ref, gn_ref, wgu_ref, wd_ref, xo_ref, g_ref, u_ref):
        xv = x_ref[...]
        n, _ = _rms(xv)
        h = (n * gn_ref[...]).astype(BF)
        acc = jnp.zeros((tm, d), F32)
        for c0 in range(0, f, fc):
            g = _nn(h, wgu_ref[:, c0:c0 + fc])
            u = _nn(h, wgu_ref[:, f + c0:f + c0 + fc])
            g_ref[:, c0:c0 + fc] = g.astype(BF)
            u_ref[:, c0:c0 + fc] = u.astype(BF)
            a = (g * _sigmoid(g)) * u
            acc = acc + _nn(a.astype(BF), wd_ref[c0:c0 + fc, :])
        xo_ref[...] = xv + 0.5 * acc

    return _call(
        body, name=name, grid=(t // tm,),
        out_shape=(S((t, d), F32), S((t, f), BF), S((t, f), BF)),
        in_specs=[pl.BlockSpec((tm, d), lambda i: (i, 0)), _resident((1, d)), _resident(wgu.shape), _resident(wd.shape)],
        out_specs=(pl.BlockSpec((tm, d), lambda i: (i, 0)), pl.BlockSpec((tm, f), lambda i: (i, 0)),
                   pl.BlockSpec((tm, f), lambda i: (i, 0))),
        sem=("parallel",), args=(x, gn, wgu, wd), hook=hook)


def _mix_proj(x, gn, w_in, hook=None):
    t, d = x.shape
    tm = 256
    nqkv = 1536
    ne = w_in.shape[1] - nqkv

    def body(x_ref, gn_ref, w_ref, qkv_ref, e_ref):
        n, _ = _rms(x_ref[...])
        h = (n * gn_ref[...]).astype(BF)
        qkv_ref[...] = _nn(h, w_ref[:, 0:nqkv]).astype(BF)
        for c0 in range(0, ne, 1024):
            e_ref[:, c0:c0 + 1024] = _nn(h, w_ref[:, nqkv + c0:nqkv + c0 + 1024]).astype(BF)

    return _call(
        body, name="mix_proj", grid=(t // tm,),
        out_shape=(S((t, nqkv), BF), S((t, ne), BF)),
        in_specs=[pl.BlockSpec((tm, d), lambda i: (i, 0)), _resident((1, d)), _resident(w_in.shape)],
        out_specs=(pl.BlockSpec((tm, nqkv), lambda i: (i, 0)), pl.BlockSpec((tm, ne), lambda i: (i, 0))),
        sem=("parallel",), args=(x, gn, w_in), hook=hook)


def _t5_bucket(rel):
    n = jnp.maximum(rel, 0)
    max_exact = REL_BUCKETS // 2
    nf = jnp.maximum(n, 1).astype(F32)
    large = max_exact + (jnp.log(nf / max_exact) / math.log(REL_MAX_DIST / max_exact)
                         * (REL_BUCKETS - max_exact)).astype(I32)
    large = jnp.minimum(large, REL_BUCKETS - 1)
    return jnp.where(n < max_exact, n, large)


def _block_rel():
    i = lax.broadcasted_iota(I32, (BLOCK, 2 * BLOCK), 0)
    j = lax.broadcasted_iota(I32, (BLOCK, 2 * BLOCK), 1)
    return i + BLOCK - j


def _bias_build(rel_bias):
    def body(rb_ref, o_ref):
        bucket = _t5_bucket(_block_rel())
        for h in range(SWA_HEADS):
            acc = jnp.zeros((BLOCK, 2 * BLOCK), F32)
            for b in range(REL_BUCKETS):
                acc = jnp.where(bucket == b, rb_ref[b, h], acc)
            o_ref[h] = acc

    return pl.pallas_call(
        body, name="bias_build", out_shape=S((SWA_HEADS, BLOCK, 2 * BLOCK), F32),
        in_specs=[SMEM_SPEC], out_specs=VMEM_SPEC,
    )(rel_bias)


GROUP_ROWS = SWA_GROUP * BLOCK


def _swa_visible(b, pq_ref, pkp_ref, pkc_ref):
    pk = jnp.concatenate([pkp_ref[...], pkc_ref[...]], axis=1)
    col = lax.broadcasted_iota(I32, (1, 2 * BLOCK), 1)
    pk = jnp.where(jnp.logical_and(b == 0, col < BLOCK), POS_PAD, pk)
    rel = jnp.concatenate([pq_ref[...]] * SWA_GROUP, axis=0) - pk
    return jnp.logical_and(rel >= 0, rel < WINDOW)


def _group_heads(ref, hk):
    h0 = hk * SWA_GROUP
    return jnp.concatenate([ref[:, (h0 + g) * HEAD_DIM:(h0 + g + 1) * HEAD_DIM] for g in range(SWA_GROUP)], axis=0)


def _group_sinks(sink_ref, hk):
    row = lax.broadcasted_iota(I32, (GROUP_ROWS, 1), 0)
    col = jnp.zeros((GROUP_ROWS, 1), F32) + sink_ref[0, hk * SWA_GROUP]
    for g in range(1, SWA_GROUP):
        col = jnp.where(row >= g * BLOCK, sink_ref[0, hk * SWA_GROUP + g], col)
    return col


def _swa_probs(qg, kh, vis, bias, sink):
    s = _nt(qg, kh) * (HEAD_DIM ** -0.5)
    s = jnp.where(vis, s + bias, NEG)
    m = jnp.maximum(jnp.max(s, axis=-1, keepdims=True), sink)
    p = jnp.exp(s - m)
    ps = jnp.exp(sink - m)
    inv = 1.0 / (jnp.sum(p, axis=-1, keepdims=True) + ps)
    return p * inv, ps * inv


def _swa_fwd(qkv, pos_col, pos_row, bias_t, sinks, hook=None):
    t = qkv.shape[0]
    nb = t // BLOCK
    qw = SWA_HEADS * HEAD_DIM
    kw = SWA_KV_HEADS * HEAD_DIM

    def body(q_ref, kp_ref, kc_ref, vp_ref, vc_ref, pq_ref, pkp_ref, pkc_ref, bias_ref, sink_ref, o_ref):
        b = pl.program_id(0)
        vis = _swa_visible(b, pq_ref, pkp_ref, pkc_ref)
        k2 = jnp.concatenate([kp_ref[...], kc_ref[...]], axis=0)
        v2 = jnp.concatenate([vp_ref[...], vc_ref[...]], axis=0)
        for hk in range(SWA_KV_HEADS):
            kh = k2[:, hk * HEAD_DIM:(hk + 1) * HEAD_DIM]
            vh = v2[:, hk * HEAD_DIM:(hk + 1) * HEAD_DIM]
            pn, _ = _swa_probs(_group_heads(q_ref, hk), kh, vis, bias_ref[hk], _group_sinks(sink_ref, hk))
            o = _nn(pn.astype(BF), vh)
            for g in range(SWA_GROUP):
                h = hk * SWA_GROUP + g
                o_ref[:, h * HEAD_DIM:(h + 1) * HEAD_DIM] = o[g * BLOCK:(g + 1) * BLOCK]

    prev = lambda b: jnp.maximum(b - 1, 0)
    return _call(
        body, name="swa_fwd", grid=(nb,), out_shape=(S((t, qw), F32),),
        in_specs=[
            pl.BlockSpec((BLOCK, qw), lambda b: (b, 0)),
            pl.BlockSpec((BLOCK, kw), lambda b: (prev(b), qw // kw)),
            pl.BlockSpec((BLOCK, kw), lambda b: (b, qw // kw)),
            pl.BlockSpec((BLOCK, kw), lambda b: (prev(b), qw // kw + 1)),
            pl.BlockSpec((BLOCK, kw), lambda b: (b, qw // kw + 1)),
            pl.BlockSpec((BLOCK, 1), lambda b: (b, 0)),
            pl.BlockSpec((1, BLOCK), lambda b: (0, prev(b))),
            pl.BlockSpec((1, BLOCK), lambda b: (0, b)),
            _resident(bias_t.shape),
            SMEM_SPEC,
        ],
        out_specs=(pl.BlockSpec((BLOCK, qw), lambda b: (b, 0)),),
        sem=("parallel",), args=(qkv, qkv, qkv, qkv, qkv, pos_col, pos_row, pos_row, bias_t, sinks), hook=hook)[0]


HALO = 16


def _conv_taps(z, zh, first):
    tm = z.shape[0]
    zh = jnp.where(first, 0.0, zh)
    row = lax.broadcasted_iota(I32, (tm, 1), 0)
    z1 = jnp.where(row == 0, zh[HALO - 1:HALO, :], pltpu.roll(z, 1, 0))
    z2 = jnp.where(row == 0, zh[HALO - 2:HALO - 1, :], jnp.where(row == 1, zh[HALO - 1:HALO, :], pltpu.roll(z, 2, 0)))
    return z1, z2


def _mix_out_fwd(e, attn, conv_w, w_out, x, hook=None):
    t, d = x.shape
    tm = 256
    hb = tm // HALO
    f32 = lambda ref: ref[...].astype(F32)

    def body(c_ref, b_ref, u_ref, ga_ref, gc_ref, ch_ref, uh_ref, attn_ref, cw_ref, w_ref, x_ref, xo_ref, mg_ref):
        i = pl.program_id(0)
        z = f32(c_ref) * f32(u_ref)
        z1, z2 = _conv_taps(z, f32(ch_ref) * f32(uh_ref), i == 0)
        s = cw_ref[0:1, :] * z2 + cw_ref[1:2, :] * z1 + cw_ref[2:3, :] * z
        conv = f32(b_ref) * s
        merged = (_sigmoid(f32(ga_ref)) * attn_ref[...] + _sigmoid(f32(gc_ref)) * conv).astype(BF)
        mg_ref[...] = merged
        xo_ref[...] = x_ref[...] + _nn(merged, w_ref[...])

    ecol = lambda cb: pl.BlockSpec((tm, d), lambda i: (i, cb))
    halo = lambda cb: pl.BlockSpec((HALO, d), lambda i: (jnp.maximum(i * hb - 1, 0), cb))
    row = pl.BlockSpec((tm, d), lambda i: (i, 0))
    return _call(
        body, name="mix_out_fwd", grid=(t // tm,),
        out_shape=(S((t, d), F32), S((t, d), BF)),
        in_specs=[ecol(0), ecol(1), ecol(2), ecol(3), ecol(4), halo(0), halo(2), row,
                  _resident(conv_w.shape), _resident(w_out.shape), row],
        out_specs=(row, row),
        sem=("parallel",), args=(e, e, e, e, e, e, e, attn, conv_w, w_out, x), hook=hook)


def _mem_kv(mem, gm, wkv):
    m, d = mem.shape

    def body(mem_ref, gm_ref, w_ref, mh_ref, kv_ref):
        n, _ = _rms(mem_ref[...])
        mh = (n * gm_ref[...]).astype(BF)
        mh_ref[...] = mh
        kv_ref[...] = _nn(mh, w_ref[...]).astype(BF)

    return pl.pallas_call(
        body, name="mem_kv", out_shape=(S((m, d), BF), S((m, wkv.shape[1]), BF)),
        compiler_params=_params(),
    )(mem, gm, wkv)


def _xattn_probs(qh, kh):
    s = _nt(qh, kh) * (kh.shape[1] ** -0.5)
    p = jnp.exp(s - jnp.max(s, axis=-1, keepdims=True))
    return p * (1.0 / jnp.sum(p, axis=-1, keepdims=True))


def _xattn_fwd(x, gn, wq, kv, wo):
    t, d = x.shape
    tm = 256
    hd = d // MEM_HEADS

    def body(x_ref, gn_ref, wq_ref, kv_ref, wo_ref, xo_ref, q_ref, o_ref):
        xv = x_ref[...]
        n, _ = _rms(xv)
        q = _nn((n * gn_ref[...]).astype(BF), wq_ref[...]).astype(BF)
        q_ref[...] = q
        outs = []
        for hh in range(MEM_HEADS):
            p = _xattn_probs(q[:, hh * hd:(hh + 1) * hd], kv_ref[:, hh * hd:(hh + 1) * hd])
            outs.append(_nn(p.astype(BF), kv_ref[:, d + hh * hd:d + (hh + 1) * hd]))
        o = jnp.concatenate(outs, axis=1).astype(BF)
        o_ref[...] = o
        xo_ref[...] = xv + _nn(o, wo_ref[...])

    row = pl.BlockSpec((tm, d), lambda i: (i, 0))
    return pl.pallas_call(
        body, name="xattn_fwd", grid=(t // tm,),
        out_shape=(S((t, d), F32), S((t, d), BF), S((t, d), BF)),
        in_specs=[row, _resident((1, d)), _resident(wq.shape), _resident(kv.shape), _resident(wo.shape)],
        out_specs=(row, row, row),
        compiler_params=_params(("parallel",)),
    )(x, gn, wq, kv, wo)


def _loss_bwd(x, gf, target):
    t, d = x.shape
    tm = 512

    def body(x_ref, gf_ref, t_ref, dx_ref, dg_ref, loss_ref):
        @pl.when(pl.program_id(0) == 0)
        def _():
            dg_ref[...] = jnp.zeros_like(dg_ref)
            loss_ref[...] = jnp.zeros_like(loss_ref)
        n, r = _rms(x_ref[...])
        g = gf_ref[...]
        err = n * g - t_ref[...]
        loss_ref[...] += 0.5 * jnp.sum(jnp.sum(err * err, axis=-1, keepdims=True) / d, axis=0, keepdims=True)
        dx, dg = _rms_bwd(err / d, n, r, g)
        dx_ref[...] = dx
        dg_ref[...] += dg

    row = pl.BlockSpec((tm, d), lambda i: (i, 0))
    return pl.pallas_call(
        body, name="loss_bwd", grid=(t // tm,),
        out_shape=(S((t, d), F32), S((1, d), F32), S((1, 128), F32)),
        in_specs=[row, _resident((1, d)), row],
        out_specs=(row, _acc_spec((1, d)), _acc_spec((1, 128))),
        compiler_params=_params(("arbitrary",)),
    )(x, gf, target)


def _ffn_bwd(dxo, x, gn, g, u, wgu, wd, name, hook=None):
    t, d = x.shape
    f = wd.shape[0]
    tm, fc = 256, 1408

    def body(dxo_ref, x_ref, gn_ref, g_ref, u_ref, wgu_ref, wd_ref, dx_ref, dgn_ref, dgu_ref, a_ref, h_ref, dyh_ref):
        @pl.when(pl.program_id(0) == 0)
        def _():
            dgn_ref[...] = jnp.zeros_like(dgn_ref)
        dxov = dxo_ref[...]
        dyh = (0.5 * dxov).astype(BF)
        dyh_ref[...] = dyh
        n, r = _rms(x_ref[...])
        gnv = gn_ref[...]
        h_ref[...] = (n * gnv).astype(BF)
        dh = jnp.zeros((tm, d), F32)
        for c0 in range(0, f, fc):
            gv = g_ref[:, c0:c0 + fc].astype(F32)
            uv = u_ref[:, c0:c0 + fc].astype(F32)
            da = _nt(dyh, wd_ref[c0:c0 + fc, :])
            sg = _sigmoid(gv)
            silu = gv * sg
            a_ref[:, c0:c0 + fc] = (silu * uv).astype(BF)
            dg = (da * uv * (sg * (1.0 + gv * (1.0 - sg)))).astype(BF)
            du = (da * silu).astype(BF)
            dgu_ref[:, c0:c0 + fc] = dg
            dgu_ref[:, f + c0:f + c0 + fc] = du
            dh = dh + _nt(dg, wgu_ref[:, c0:c0 + fc]) + _nt(du, wgu_ref[:, f + c0:f + c0 + fc])
        dx, dgn = _rms_bwd(dh, n, r, gnv)
        dx_ref[...] = dxov + dx
        dgn_ref[...] += dgn

    row = pl.BlockSpec((tm, d), lambda i: (i, 0))
    frow = pl.BlockSpec((tm, f), lambda i: (i, 0))
    return _call(
        body, name=name, grid=(t // tm,),
        out_shape=(S((t, d), F32), S((1, d), F32), S((t, 2 * f), BF), S((t, f), BF), S((t, d), BF), S((t, d), BF)),
        in_specs=[row, row, _resident((1, d)), frow, frow, _resident(wgu.shape), _resident(wd.shape)],
        out_specs=(row, _acc_spec((1, d)), pl.BlockSpec((tm, 2 * f), lambda i: (i, 0)), frow, row, row),
        sem=("arbitrary",), args=(dxo, x, gn, g, u, wgu, wd), hook=hook)


def _dw(a, b, tn, name, into=None, col_block=0, ncols=None):
    t, ka = a.shape
    nb = b.shape[1]
    tt = 1024
    nt = t // tt
    ncols = nb if ncols is None else ncols

    def body(*refs):
        a_ref, b_ref = refs[0], refs[1]
        o_ref, acc_ref = refs[-2], refs[-1]
        k = pl.program_id(1)

        @pl.when(k == 0)
        def _():
            acc_ref[...] = jnp.zeros_like(acc_ref)
        acc_ref[...] += _tn(a_ref[...], b_ref[...])

        @pl.when(k == nt - 1)
        def _():
            o_ref[...] = acc_ref[...].astype(BF)

    in_specs = [pl.BlockSpec((tt, ka), lambda j, k: (k, 0)), pl.BlockSpec((tt, tn), lambda j, k: (k, j))]
    args = [a, b]
    aliases = {}
    if into is not None:
        in_specs.append(ANY)
        args.append(into)
        aliases = {2: 0}
    return pl.pallas_call(
        body, name=name, grid=(nb // tn, nt), out_shape=S((ka, ncols), BF),
        in_specs=in_specs, out_specs=pl.BlockSpec((ka, tn), lambda j, k: (0, j + col_block)),
        scratch_shapes=[pltpu.VMEM((ka, tn), F32)], input_output_aliases=aliases,
        compiler_params=_params(("parallel", "arbitrary")),
    )(*args)


def _dw_pair(a, b, tn, name, kind, into=None, col_block=0, ncols=None, hook=None):
    t, ka = a.shape
    nb = b.shape[1]
    tt = 2048
    nt, nj = t // tt, nb // tn
    ncols = nb if ncols is None else ncols
    col = kind == "col"
    rh = ka // 2 if col else ka // 8
    tile = (rh, tn) if col else (4, rh, tn)
    half = (rh, ncols) if col else (4, rh, ncols)
    lead = (slice(None),) * (len(tile) - 1)

    def body(*refs):
        a_ref, b_ref = refs[0], refs[1]
        mine_ref, sib_ref, acc_ref, stage, ssem, rsem = refs[-6:]
        j, k = pl.program_id(0), pl.program_id(1)
        x, y, c = _place()
        sibling = (x, y, 1 - c)

        @pl.when(k == 0)
        def _():
            acc_ref[...] = jnp.zeros_like(acc_ref)
        acc_ref[...] += _tn(a_ref[...], b_ref[...])

        def send(slot, jj):
            dst = sib_ref.at[lead + (pl.ds(pl.multiple_of((jj + col_block) * tn, 128), tn),)]
            return _remote(stage.at[slot], dst, ssem.at[slot], rsem, sibling)

        def rows(s, whose):
            return acc_ref[pl.ds(pl.multiple_of(s * 2 * rh + whose * rh, 16), rh), :].astype(BF)

        @pl.when(k == nt - 1)
        def _():
            slot = j % 2

            @pl.when(j >= 2)
            def _():
                send(slot, j).wait_send()
            if col:
                mine_ref[...] = rows(0, c)
                stage[slot] = rows(0, 1 - c)
            else:
                for s in range(4):
                    mine_ref[s] = rows(s, c)
                    stage[slot, s] = rows(s, 1 - c)
            send(slot, j).start()

        @pl.when(jnp.logical_and(j == nj - 1, k == nt - 1))
        def _():
            for jj in range(max(nj - 2, 0), nj):
                send(jj % 2, jj).wait_send()
            got = sib_ref.at[lead + (pl.ds(col_block * tn, nb),)]
            _remote(got, got, ssem.at[0], rsem, sibling).wait_recv()

    in_specs = [pl.BlockSpec((tt, ka), lambda j, k: (k, 0)), pl.BlockSpec((tt, tn), lambda j, k: (k, j))]
    args = [a, b]
    aliases = {}
    if into is not None:
        in_specs += [ANY, ANY]
        args += list(into)
        aliases = {2: 0, 3: 1}
    mine_spec = pl.BlockSpec(tile, (lambda j, k: (0, j + col_block)) if col else (lambda j, k: (0, 0, j + col_block)))
    return _call(
        body, name=name, grid=(nj, nt), out_shape=(S(half, BF), S(half, BF)),
        in_specs=in_specs, out_specs=(mine_spec, ANY),
        scratch_shapes=[pltpu.VMEM((ka, tn), F32), pltpu.VMEM((2,) + tile, BF), pltpu.SemaphoreType.DMA((2,)),
                        pltpu.SemaphoreType.DMA],
        aliases=aliases, sem=("arbitrary", "arbitrary"), args=args, hook=hook)


def _xattn_bwd(dxo, x, gn, q, kv, wq, wo, hook=None):
    t, d = x.shape
    tm = 256
    hd = d // MEM_HEADS
    nkv = kv.shape[0]

    def body(dxo_ref, x_ref, gn_ref, q_ref, kv_ref, wq_ref, wo_ref, dx_ref, dgn_ref, dkv_ref, dxh_ref, h_ref, dq_ref):
        @pl.when(pl.program_id(0) == 0)
        def _():
            dgn_ref[...] = jnp.zeros_like(dgn_ref)
            dkv_ref[...] = jnp.zeros_like(dkv_ref)
        dxov = dxo_ref[...]
        dxh = dxov.astype(BF)
        dxh_ref[...] = dxh
        do = _nt(dxh, wo_ref[...]).astype(BF)
        dqs = []
        for hh in range(MEM_HEADS):
            lo, hi = hh * hd, (hh + 1) * hd
            qh = q_ref[:, lo:hi]
            kh = kv_ref[:, lo:hi]
            vh = kv_ref[:, d + lo:d + hi]
            doh = do[:, lo:hi]
            p = _xattn_probs(qh, kh)
            dp = _nt(doh, vh)
            ds = (p * (dp - jnp.sum(p * dp, axis=-1, keepdims=True)) * (hd ** -0.5)).astype(BF)
            dqs.append(_nn(ds, kh))
            dkv_ref[:, lo:hi] += _tn(ds, qh)
            dkv_ref[:, d + lo:d + hi] += _tn(p.astype(BF), doh)
        dq = jnp.concatenate(dqs, axis=1).astype(BF)
        dq_ref[...] = dq
        n, r = _rms(x_ref[...])
        gnv = gn_ref[...]
        h_ref[...] = (n * gnv).astype(BF)
        dx, dgn = _rms_bwd(_nt(dq, wq_ref[...]), n, r, gnv)
        dx_ref[...] = dxov + dx
        dgn_ref[...] += dgn

    row = pl.BlockSpec((tm, d), lambda i: (i, 0))
    return _call(
        body, name="xattn_bwd", grid=(t // tm,),
        out_shape=(S((t, d), F32), S((1, d), F32), S((nkv, 2 * d), F32), S((t, d), BF), S((t, d), BF), S((t, d), BF)),
        in_specs=[row, row, _resident((1, d)), row, _resident(kv.shape), _resident(wq.shape), _resident(wo.shape)],
        out_specs=(row, _acc_spec((1, d)), _acc_spec((nkv, 2 * d)), row, row, row),
        sem=("arbitrary",), args=(dxo, x, gn, q, kv, wq, wo), hook=hook)


def _mem_bwd(dkv, mh, mem, gm, wkv, pair=False):
    m, d = mem.shape
    rows, cols = wkv.shape
    rh = rows // 2

    def body(dkv_ref, mh_ref, mem_ref, gm_ref, w_ref, *outs):
        dkvb = dkv_ref[...].astype(BF)
        dw = _tn(mh_ref[...], dkvb).astype(BF)
        dmh = _nt(dkvb, w_ref[...])
        n, _ = _rms(mem_ref[...])
        if not pair:
            dw_ref, dgm_ref = outs
            dw_ref[...] = dw
        else:
            mine_ref, sib_ref, dgm_ref, whole, ssem, rsem = outs
            x, y, c = _place()
            whole[...] = dw
            cp = _remote(whole.at[pl.ds(pl.multiple_of((1 - c) * rh, 16), rh), :], sib_ref, ssem, rsem, (x, y, 1 - c))
            cp.start()
            mine_ref[...] = whole[pl.ds(pl.multiple_of(c * rh, 16), rh), :]
            cp.wait()
        dgm_ref[...] = jnp.sum(dmh * n, axis=0, keepdims=True)

    if not pair:
        return pl.pallas_call(body, name="mem_bwd", out_shape=(S(wkv.shape, BF), S((1, d), F32)),
                              compiler_params=_params())(dkv, mh, mem, gm, wkv)
    mine, sib, dgm = pl.pallas_call(
        body, name="mem_bwd", out_shape=(S((rh, cols), BF), S((rh, cols), BF), S((1, d), F32)),
        in_specs=[VMEM_SPEC] * 5, out_specs=(VMEM_SPEC, ANY, VMEM_SPEC),
        scratch_shapes=[pltpu.VMEM((rows, cols), BF), pltpu.SemaphoreType.DMA, pltpu.SemaphoreType.DMA],
        compiler_params=_params())(dkv, mh, mem, gm, wkv)
    return (mine, sib), dgm


def _mix_out_bwd(dxo, e, attn, conv_w, w_out, hook=None):
    t, d = attn.shape
    tm = 256
    nt = t // tm
    f32 = lambda ref: ref[...].astype(F32)

    def body(dxo_ref, dxn_ref, c_ref, b_ref, u_ref, ga_ref, gc_ref, ch_ref, uh_ref, bn_ref, gcn_ref,
             attn_ref, cw_ref, w_ref, dattn_ref, de_ref, dcw_ref, dxh_ref):
        i = pl.program_id(0)

        @pl.when(i == 0)
        def _():
            dcw_ref[...] = jnp.zeros_like(dcw_ref)
        dxh = dxo_ref[...].astype(BF)
        dxh_ref[...] = dxh
        w = w_ref[...]
        dm = _nt(dxh, w)
        dmn = _nt(dxn_ref[...].astype(BF), w)
        cv, bv, uv = f32(c_ref), f32(b_ref), f32(u_ref)
        sga = _sigmoid(f32(ga_ref))
        sgc = _sigmoid(f32(gc_ref))
        z = cv * uv
        z1, z2 = _conv_taps(z, f32(ch_ref) * f32(uh_ref), i == 0)
        w0, w1, w2 = cw_ref[0:1, :], cw_ref[1:2, :], cw_ref[2:3, :]
        s = w0 * z2 + w1 * z1 + w2 * z
        av = attn_ref[...]
        dattn_ref[...] = (dm * sga).astype(BF)
        dconv = dm * sgc
        ds = dconv * bv
        dsn = jnp.where(i == nt - 1, 0.0, dmn * _sigmoid(gcn_ref[0:8, :].astype(F32)) * bn_ref[0:8, :].astype(F32))
        row = lax.broadcasted_iota(I32, (tm, 1), 0)
        dsp1 = jnp.where(row == tm - 1, dsn[0:1, :], pltpu.roll(ds, tm - 1, 0))
        dsp2 = jnp.where(row == tm - 2, dsn[0:1, :], jnp.where(row == tm - 1, dsn[1:2, :], pltpu.roll(ds, tm - 2, 0)))
        dz = w2 * ds + w1 * dsp1 + w0 * dsp2
        de_ref[:, 0:d] = (dz * uv).astype(BF)
        de_ref[:, d:2 * d] = (dconv * s).astype(BF)
        de_ref[:, 2 * d:3 * d] = (dz * cv).astype(BF)
        de_ref[:, 3 * d:4 * d] = (dm * av * sga * (1.0 - sga)).astype(BF)
        de_ref[:, 4 * d:5 * d] = (dm * (bv * s) * sgc * (1.0 - sgc)).astype(BF)
        dcw_ref[0:1, :] += jnp.sum(ds * z2, axis=0, keepdims=True)
        dcw_ref[1:2, :] += jnp.sum(ds * z1, axis=0, keepdims=True)
        dcw_ref[2:3, :] += jnp.sum(ds * z, axis=0, keepdims=True)

    ecol = lambda cb: pl.BlockSpec((tm, d), lambda i: (i, cb))
    prev = lambda cb: pl.BlockSpec((HALO, d), lambda i: (jnp.maximum(i * (tm // HALO) - 1, 0), cb))
    nxt = lambda rows, cb: pl.BlockSpec((rows, d), lambda i: (jnp.minimum((i + 1) * (tm // rows), t // rows - 1), cb))
    row = pl.BlockSpec((tm, d), lambda i: (i, 0))
    return _call(
        body, name="mix_out_bwd", grid=(nt,),
        out_shape=(S((t, d), BF), S((t, 5 * d), BF), S((8, d), F32), S((t, d), BF)),
        in_specs=[row, nxt(8, 0), ecol(0), ecol(1), ecol(2), ecol(3), ecol(4), prev(0), prev(2), nxt(HALO, 1), nxt(HALO, 4),
                  row, _resident(conv_w.shape), _resident(w_out.shape)],
        out_specs=(row, pl.BlockSpec((tm, 5 * d), lambda i: (i, 0)), _acc_spec((8, d)), row),
        sem=("arbitrary",), args=(dxo, dxo, e, e, e, e, e, e, e, e, e, attn, conv_w, w_out), hook=hook)


def _swa_bwd(qkv, dattn, pos_col, pos_row, bias_t, sinks, hook=None):
    t = qkv.shape[0]
    nb = t // BLOCK
    qw = SWA_HEADS * HEAD_DIM
    kw = SWA_KV_HEADS * HEAD_DIM

    def body(q_ref, kp_ref, kc_ref, vp_ref, vc_ref, do_ref, pq_ref, pkp_ref, pkc_ref, bias_ref, sink_ref,
             dq_ref, dkv_ref, gb_ref, dsk_ref, carry_ref):
        b = pl.program_id(0)

        @pl.when(b == 0)
        def _():
            gb_ref[...] = jnp.zeros_like(gb_ref)
            dsk_ref[...] = jnp.zeros_like(dsk_ref)
            carry_ref[...] = jnp.zeros_like(carry_ref)

        @pl.when(b < nb)
        def _():
            vis = _swa_visible(b, pq_ref, pkp_ref, pkc_ref)
            k2 = jnp.concatenate([kp_ref[...], kc_ref[...]], axis=0)
            v2 = jnp.concatenate([vp_ref[...], vc_ref[...]], axis=0)
            for hk in range(SWA_KV_HEADS):
                lo, hi = hk * HEAD_DIM, (hk + 1) * HEAD_DIM
                kh = k2[:, lo:hi]
                vh = v2[:, lo:hi]
                qg = _group_heads(q_ref, hk)
                dog = _group_heads(do_ref, hk)
                pn, psn = _swa_probs(qg, kh, vis, bias_ref[hk], _group_sinks(sink_ref, hk))
                dp = _nt(dog, vh)
                delta = jnp.sum(pn * dp, axis=-1, keepdims=True)
                ds = pn * (dp - delta)
                gb_ref[hk] += ds
                dsk_ref[hk] += -psn * delta
                dsb = (ds * (HEAD_DIM ** -0.5)).astype(BF)
                dqg = _nn(dsb, kh).astype(BF)
                for g in range(SWA_GROUP):
                    h = hk * SWA_GROUP + g
                    dq_ref[:, h * HEAD_DIM:(h + 1) * HEAD_DIM] = dqg[g * BLOCK:(g + 1) * BLOCK]
                dk = _tn(dsb, qg)
                dv = _tn(pn.astype(BF), dog)
                dkv_ref[:, lo:hi] = (carry_ref[:, lo:hi] + dk[0:BLOCK]).astype(BF)
                dkv_ref[:, kw + lo:kw + hi] = (carry_ref[:, kw + lo:kw + hi] + dv[0:BLOCK]).astype(BF)
                carry_ref[:, lo:hi] = dk[BLOCK:2 * BLOCK]
                carry_ref[:, kw + lo:kw + hi] = dv[BLOCK:2 * BLOCK]

        @pl.when(b == nb)
        def _():
            dkv_ref[...] = carry_ref[...].astype(BF)

    cur = lambda b: jnp.minimum(b, nb - 1)
    prev = lambda b: jnp.maximum(cur(b) - 1, 0)
    return _call(
        body, name="swa_bwd", grid=(nb + 1,),
        out_shape=(S((t, qw), BF), S((t, 2 * kw), BF), S((SWA_KV_HEADS, GROUP_ROWS, 2 * BLOCK), F32),
                   S((SWA_KV_HEADS, GROUP_ROWS, 1), F32)),
        in_specs=[
            pl.BlockSpec((BLOCK, qw), lambda b: (cur(b), 0)),
            pl.BlockSpec((BLOCK, kw), lambda b: (prev(b), qw // kw)),
            pl.BlockSpec((BLOCK, kw), lambda b: (cur(b), qw // kw)),
            pl.BlockSpec((BLOCK, kw), lambda b: (prev(b), qw // kw + 1)),
            pl.BlockSpec((BLOCK, kw), lambda b: (cur(b), qw // kw + 1)),
            pl.BlockSpec((BLOCK, qw), lambda b: (cur(b), 0)),
            pl.BlockSpec((BLOCK, 1), lambda b: (cur(b), 0)),
            pl.BlockSpec((1, BLOCK), lambda b: (0, prev(b))),
            pl.BlockSpec((1, BLOCK), lambda b: (0, cur(b))),
            _resident(bias_t.shape),
            SMEM_SPEC,
        ],
        out_specs=(
            pl.BlockSpec((BLOCK, qw), lambda b: (cur(b), 0)),
            pl.BlockSpec((BLOCK, 2 * kw), lambda b: (jnp.maximum(b - 1, 0), 0)),
            _acc_spec((SWA_KV_HEADS, GROUP_ROWS, 2 * BLOCK)),
            _acc_spec((SWA_KV_HEADS, GROUP_ROWS, 1)),
        ),
        scratch_shapes=[pltpu.VMEM((BLOCK, 2 * kw), F32)],
        sem=("arbitrary",), args=(qkv, qkv, qkv, qkv, qkv, dattn, pos_col, pos_row, pos_row, bias_t, sinks), hook=hook)


def _bias_reduce(gb, dsk):
    def body(gb_ref, dsk_ref, drb_ref, dsink_ref):
        bucket = _t5_bucket(_block_rel())
        for b in range(REL_BUCKETS):
            mask = bucket == b
            for h in range(SWA_HEADS):
                drb_ref[b, h] = jnp.sum(jnp.where(mask, gb_ref[h], 0.0))
        for h in range(SWA_HEADS):
            dsink_ref[0, h] = jnp.sum(dsk_ref[h])

    return pl.pallas_call(
        body, name="bias_reduce", out_shape=(S((REL_BUCKETS, SWA_HEADS), F32), S((1, SWA_HEADS), F32)),
        in_specs=[VMEM_SPEC, VMEM_SPEC], out_specs=(SMEM_SPEC, SMEM_SPEC),
    )(gb, dsk)


def _mix_in_bwd(dq, dkv, de, x, gn, w_in, dxo):
    t, d = x.shape
    tm = 256
    nq, nkv, ne = dq.shape[1], dkv.shape[1], de.shape[1]

    def body(dq_ref, dkv_ref, de_ref, x_ref, gn_ref, w_ref, dxo_ref, dx_ref, dgn_ref, h_ref):
        @pl.when(pl.program_id(0) == 0)
        def _():
            dgn_ref[...] = jnp.zeros_like(dgn_ref)
        dh = _nt(dq_ref[...], w_ref[:, 0:nq]) + _nt(dkv_ref[...], w_ref[:, nq:nq + nkv])
        for c0 in range(0, ne, 1024):
            dh = dh + _nt(de_ref[:, c0:c0 + 1024], w_ref[:, nq + nkv + c0:nq + nkv + c0 + 1024])
        n, r = _rms(x_ref[...])
        gnv = gn_ref[...]
        h_ref[...] = (n * gnv).astype(BF)
        dx, dgn = _rms_bwd(dh, n, r, gnv)
        dx_ref[...] = dxo_ref[...] + dx
        dgn_ref[...] += dgn

    row = pl.BlockSpec((tm, d), lambda i: (i, 0))
    wide = lambda w: pl.BlockSpec((tm, w), lambda i: (i, 0))
    return pl.pallas_call(
        body, name="mix_in_bwd", grid=(t // tm,),
        out_shape=(S((t, d), F32), S((1, d), F32), S((t, d), BF)),
        in_specs=[wide(nq), wide(nkv), wide(ne), row, _resident((1, d)), _resident(w_in.shape), row],
        out_specs=(row, _acc_spec((1, d)), row),
        compiler_params=_params(("arbitrary",)),
    )(dq, dkv, de, x, gn, w_in, dxo)


def _row_tile(rows):
    for tr in (256, 352, 128, 64, 16, 8):
        if rows % tr == 0:
            return tr
    return rows


def _add_bf16(a, b, name):
    rows, cols = a.shape
    tr = _row_tile(rows)

    def body(a_ref, b_ref, o_ref):
        o_ref[...] = (a_ref[...].astype(F32) + b_ref[...].astype(F32)).astype(BF)

    blk = pl.BlockSpec((tr, cols), lambda i: (i, 0))
    return pl.pallas_call(body, name=name, grid=(rows // tr,), out_shape=S((rows, cols), BF),
                          in_specs=[blk, blk], out_specs=blk, compiler_params=_params(("parallel",)))(a, b)


def _adam_update(w, g, m, v):
    mn = ADAM_B1 * m + (1.0 - ADAM_B1) * g
    vn = ADAM_B2 * v + (1.0 - ADAM_B2) * (g * g)
    m_hat = mn / (1.0 - ADAM_B1 ** ADAM_STEP)
    v_hat = vn / (1.0 - ADAM_B2 ** ADAM_STEP)
    return -ADAM_LR * (m_hat / (jnp.sqrt(v_hat) + ADAM_EPS) + ADAM_WD * w), mn, vn


def _adamw(quads, name, steps, echo=False, hook=None):
    n_out = 4 if echo else 3

    def body(*refs):
        ins, outs = refs[:4 * len(quads)], refs[4 * len(quads):]
        for q in range(len(quads)):
            w_ref, g_ref, m_ref, v_ref = ins[4 * q:4 * q + 4]
            res = outs[n_out * q:n_out * q + n_out]
            gv = g_ref[...]
            if echo:
                res[0][...] = gv
            res[-3][...], res[-2][...], res[-1][...] = _adam_update(w_ref[...], gv, m_ref[...], v_ref[...])

    in_specs, out_specs, out_shape, args = [], [], [], []
    for quad in quads:
        rows, cols = quad[0].shape
        blk = pl.BlockSpec((rows // steps, cols), lambda i: (i, 0))
        in_specs += [blk] * 4
        out_specs += [blk] * n_out
        out_shape += [S((rows, cols), F32)] * n_out
        args += list(quad)
    res = _call(body, name=name, grid=(steps,), out_shape=tuple(out_shape), in_specs=in_specs, out_specs=tuple(out_specs),
                sem=("parallel",), args=args, hook=hook)
    return [res[n_out * q:n_out * q + n_out] for q in range(len(quads))]


def _place():
    x, y, c = lax.axis_index("x"), lax.axis_index("y"), lax.axis_index("c")
    return x, y, c


OTHER_CHIPS = ((1, 0), (0, 1), (1, 1))


def _flip(v, f):
    return 1 - v if f else v


def _remote(src, dst, ssem, rsem, dev):
    return pltpu.make_async_remote_copy(src_ref=src, dst_ref=dst, send_sem=ssem, recv_sem=rsem,
                                        device_id=dev, device_id_type=MESH)


class _Both:
    def __init__(self, hooks):
        self.hooks = hooks
        self.ins = [a for h in hooks for a in h.ins]
        self.out_shape = tuple(o for h in hooks for o in h.out_shape)
        self.scratch = [x for h in hooks for x in h.scratch]

    def _each(self, ins, outs, scr):
        i = o = s = 0
        for h in self.hooks:
            ni, no, ns = len(h.ins), len(h.out_shape), len(h.scratch)
            yield h, ins[i:i + ni], outs[o:o + no], scr[s:s + ns]
            i, o, s = i + ni, o + no, s + ns

    def start(self, ins, outs, scr):
        for h, *refs in self._each(ins, outs, scr):
            h.start(*refs)

    def finish(self, ins, outs, scr):
        for h, *refs in self._each(ins, outs, scr):
            h.finish(*refs)

    @property
    def results(self):
        return [h.results for h in self.hooks]

    @results.setter
    def results(self, res):
        for h, _, mine, _ in self._each((), tuple(res), ()):
            h.results = mine


def _exchange(hook, name):
    n_in, n_out = len(hook.ins), len(hook.out_shape)

    def body(*refs):
        ins, outs, scr = refs[:n_in], refs[n_in:n_in + n_out], refs[n_in + n_out:]
        hook.start(ins, outs, scr)
        hook.finish(ins, outs, scr)

    hook.results = pl.pallas_call(
        body, name=name, out_shape=tuple(hook.out_shape), in_specs=[ANY] * n_in, out_specs=(ANY,) * n_out,
        scratch_shapes=list(hook.scratch), compiler_params=_params(),
    )(*hook.ins)


class _GatherHook:
    def __init__(self, shards, kinds):
        self.ins, self.kinds, n = list(shards), list(kinds), len(shards)
        self.out_shape = tuple(
            S((w.shape[0], 4 * w.shape[1]), BF) if k == "col" else S((4,) + w.shape, BF) for w, k in zip(shards, kinds))
        dma = pltpu.SemaphoreType.DMA
        self.scratch = ([dma((n, 3)) for _ in range(5)] + [dma((n,)), dma((n,))]
                        + [pltpu.VMEM((3, w.shape[0] // 2, w.shape[1]), BF) for w in shards]
                        + [pltpu.VMEM(w.shape, BF) for w in shards])

    def _window(self, outs, i, s, half):
        rows, cols = self.ins[i].shape
        rh = rows // 2
        start = pl.multiple_of(half * rh, 16)
        if self.kinds[i] == "col":
            return outs[i].at[pl.ds(start, rh), pl.ds(pl.multiple_of(s * cols, 128), cols)]
        return outs[i].at[s, pl.ds(start, rh), :]

    def _copies(self, ins, outs, scr):
        n = len(ins)
        ssem, rsem, fssem, frsem, ksem, lsem, osem = scr[:7]
        land, own = scr[7:7 + n], scr[7 + n:7 + 2 * n]
        x, y, c = _place()
        sibling = (x, y, 1 - c)
        loads, stores, sends, forwards, keeps, passed = [], [], [], [], [], []
        for i in range(n):
            rows, cols = self.ins[i].shape
            rh = rows // 2
            mine = (outs[i].at[:, pl.ds(pl.multiple_of((2 * x + y) * cols, 128), cols)] if self.kinds[i] == "col"
                    else outs[i].at[2 * x + y])
            loads.append(functools.partial(pltpu.make_async_copy, ins[i], own[i], lsem.at[i]))
            stores.append(functools.partial(pltpu.make_async_copy, own[i], mine, osem.at[i]))
            src = ins[i].at[pl.ds(pl.multiple_of(c * rh, 16), rh), :]
            for j, (fx, fy) in enumerate(OTHER_CHIPS):
                px, py = _flip(x, fx), _flip(y, fy)
                sends.append(functools.partial(_remote, src, land[i].at[j], ssem.at[i, j], rsem.at[i, j], (px, py, c)))
                here = self._window(outs, i, 2 * px + py, c)
                forwards.append(functools.partial(_remote, land[i].at[j], here, fssem.at[i, j], frsem.at[i, j], sibling))
                keeps.append(functools.partial(pltpu.make_async_copy, land[i].at[j], here, ksem.at[i, j]))
                there = self._window(outs, i, 2 * px + py, 1 - c)
                passed.append(functools.partial(_remote, there, there, fssem.at[i, j], frsem.at[i, j], sibling))
        return loads, stores, sends, forwards, keeps, passed

    def start(self, ins, outs, scr):
        loads, _, sends, _, _, _ = self._copies(ins, outs, scr)
        for make in sends + loads:
            make().start()

    def finish(self, ins, outs, scr):
        loads, stores, sends, forwards, keeps, passed = self._copies(ins, outs, scr)
        for load, store in zip(loads, stores):
            load().wait()
            store().start()
        for send, forward, keep in zip(sends, forwards, keeps):
            send().wait_recv()
            forward().start()
            keep().start()
        for make in passed:
            make().wait_recv()
        for make in sends + forwards:
            make().wait_send()
        for make in keeps + stores:
            make().wait()


class _ChipsHook:
    def __init__(self, parts, kinds):
        self.ins, self.kinds, n = list(parts), list(kinds), len(parts)
        self.out_shape = tuple(
            S((4, p.shape[0], p.shape[1] // 4), BF) if k == "col" else S(p.shape, BF) for p, k in zip(parts, kinds))
        dma = pltpu.SemaphoreType.DMA
        self.scratch = ([dma((n, 3)), dma((n, 3)), dma((n,)), dma((n,))]
                        + [pltpu.VMEM(o.shape[1:], BF) for o in self.out_shape])

    def _slab(self, ins, i, s):
        _, rows, cols = self.out_shape[i].shape
        if self.kinds[i] == "col":
            return ins[i].at[:, pl.ds(pl.multiple_of(s * cols, 128), cols)]
        return ins[i].at[s]

    def _copies(self, ins, outs, scr):
        ssem, rsem, lsem, osem = scr[:4]
        own = scr[4:]
        x, y, c = _place()
        loads, stores, sends = [], [], []
        for i in range(len(ins)):
            loads.append(functools.partial(pltpu.make_async_copy, self._slab(ins, i, 2 * x + y), own[i], lsem.at[i]))
            stores.append(functools.partial(pltpu.make_async_copy, own[i], outs[i].at[3], osem.at[i]))
            for j, (fx, fy) in enumerate(OTHER_CHIPS):
                px, py = _flip(x, fx), _flip(y, fy)
                sends.append(functools.partial(_remote, self._slab(ins, i, 2 * px + py), outs[i].at[j], ssem.at[i, j],
                                               rsem.at[i, j], (px, py, c)))
        return loads, stores, sends

    def start(self, ins, outs, scr):
        loads, _, sends = self._copies(ins, outs, scr)
        for make in sends + loads:
            make().start()

    def finish(self, ins, outs, scr):
        loads, stores, sends = self._copies(ins, outs, scr)
        for load, store in zip(loads, stores):
            load().wait()
            store().start()
        for make in sends + stores:
            make().wait()


SHARE_STEPS = 2


def _sum_share(slab_list, name, hook=None):
    n = len(slab_list)
    geom = [(sl.shape[1], sl.shape[1] // SHARE_STEPS, sl.shape[2]) for sl in slab_list]

    def body(*refs):
        ins, outs, scr = refs[:n], refs[n:2 * n], refs[2 * n:]
        i = pl.program_id(0)
        x, y, c = _place()
        sibling = (x, y, 1 - c)

        def copies(q, k):
            rh, tr, _ = geom[q]
            stage, lsem, ssem, rsem = scr[4 * q:4 * q + 4]
            dst = outs[q].at[pl.ds(pl.multiple_of(c * rh + k * tr, 8), tr), :]
            return (pltpu.make_async_copy(stage.at[k], dst, lsem.at[k]),
                    _remote(stage.at[k], dst, ssem.at[k], rsem, sibling))

        for q in range(n):
            acc = ins[q][3].astype(F32)
            for k in range(3):
                acc = acc + ins[q][k].astype(F32)
            scr[4 * q][i] = acc
            for cp in copies(q, i):
                cp.start()

        @pl.when(i == SHARE_STEPS - 1)
        def _():
            for q in range(n):
                rh = geom[q][0]
                for k in range(SHARE_STEPS):
                    local, remote = copies(q, k)
                    local.wait()
                    remote.wait_send()
                got = outs[q].at[pl.ds(pl.multiple_of((1 - c) * rh, 8), rh), :]
                _remote(got, got, scr[4 * q + 2].at[0], scr[4 * q + 3], sibling).wait_recv()

    dma = pltpu.SemaphoreType.DMA
    scratch = []
    for rh, tr, cols in geom:
        scratch += [pltpu.VMEM((SHARE_STEPS, tr, cols), F32), dma((SHARE_STEPS,)), dma((SHARE_STEPS,)), dma]
    return _call(
        body, name=name, grid=(SHARE_STEPS,), out_shape=tuple(S((2 * rh, cols), F32) for rh, _, cols in geom),
        in_specs=[pl.BlockSpec((4, tr, cols), lambda i: (0, i, 0)) for _, tr, cols in geom], out_specs=(ANY,) * n,
        scratch_shapes=scratch, sem=("arbitrary",), args=list(slab_list), hook=hook)


class _SmallSumHook:
    def __init__(self, buf):
        self.ins, self.out_shape = [buf], (S(buf.shape, F32),)
        dma = pltpu.SemaphoreType.DMA
        self.scratch = [pltpu.VMEM((8,) + buf.shape, F32), pltpu.VMEM(buf.shape, F32), dma((7,)), dma((7,)), dma]

    def _sends(self, scr):
        slots, _, ssem, rsem, _ = scr
        x, y, c = _place()
        me = 4 * x + 2 * y + c
        for r in range(1, 8):
            px, py, pc = _flip(x, (r >> 2) & 1), _flip(y, (r >> 1) & 1), _flip(c, r & 1)
            yield (functools.partial(_remote, slots.at[me], slots.at[me], ssem.at[r - 1], rsem.at[r - 1], (px, py, pc)),
                   functools.partial(_remote, slots.at[me], slots.at[4 * px + 2 * py + pc], ssem.at[r - 1],
                                     rsem.at[r - 1], (px, py, pc)))

    def start(self, ins, outs, scr):
        slots, _, _, _, lsem = scr
        x, y, c = _place()
        load = pltpu.make_async_copy(ins[0], slots.at[4 * x + 2 * y + c], lsem)
        load.start()
        load.wait()
        for send, _ in self._sends(scr):
            send().start()

    def finish(self, ins, outs, scr):
        slots, total, _, _, lsem = scr
        for _, arrival in self._sends(scr):
            arrival().wait_recv()
        for send, _ in self._sends(scr):
            send().wait_send()
        acc = slots[0]
        for k in range(1, 8):
            acc = acc + slots[k]
        total[...] = acc
        store = pltpu.make_async_copy(total, outs[0], lsem)
        store.start()
        store.wait()


BIG = ("ffn1_w_gu", "ffn1_w_down", "w_in", "w_out", "xattn_wq", "xattn_wkv", "xattn_wo", "ffn2_w_gu", "ffn2_w_down")
KIND = {"ffn1_w_gu": "col", "ffn1_w_down": "row", "w_in": "col", "w_out": "row", "xattn_wq": "row",
        "xattn_wkv": "col", "xattn_wo": "row", "ffn2_w_gu": "col", "ffn2_w_down": "row"}
WEIGHTS = ("rel_bias", "ffn1_norm", "ffn1_w_gu", "ffn1_w_down", "mix_norm", "w_in", "sinks", "conv_w", "w_out",
           "xattn_norm", "mem_norm", "xattn_wq", "xattn_wkv", "xattn_wo", "ffn2_norm", "ffn2_w_gu", "ffn2_w_down",
           "final_norm")
SMALL_ROWS = 16
GAIN_ROW = {"ffn1_norm": 0, "mix_norm": 1, "xattn_norm": 2, "mem_norm": 3, "ffn2_norm": 4, "final_norm": 5}
CONV_ROW, SINK_ROW, BIAS_ROW, LOSS_ROW = 6, 9, 10, 11


def _rows_block(rows, d):
    buf = jnp.zeros((SMALL_ROWS, d), F32)
    for r, v in rows.items():
        buf = lax.dynamic_update_slice(buf, v.reshape(1, -1).astype(F32), (r, 0))
    return buf


def _local_step(x, mem, pos, target, w, gains, rel_bias, sinks, conv_w, shards=None):
    t, d = x.shape
    dist = shards is not None
    w = dict(w)
    grads, slabs = {}, {}
    pos_col = pos.reshape(t, 1)
    pos_row = pos.reshape(1, t)
    bias_t = _bias_build(rel_bias).reshape(SWA_KV_HEADS, GROUP_ROWS, 2 * BLOCK)

    def gather(names):
        return _GatherHook([shards[k] for k in names], [KIND[k] for k in names]) if dist else None

    def gathered(names, hook):
        if dist:
            for k, gw in zip(names, hook.results):
                w[k] = gw if KIND[k] == "col" else gw.reshape(-1, gw.shape[-1])

    def dw(k, a, b, tn, name, **kw):
        return _dw_pair(a, b, tn, name, KIND[k], **kw) if dist else _dw(a, b, tn, name, **kw)

    def pair_sums(names):
        if not dist:
            return None
        parts = {}
        for k in names:
            mine, sib = grads[k]
            if KIND[k] == "row":
                mine, sib = mine.reshape(-1, mine.shape[-1]), sib.reshape(-1, sib.shape[-1])
            p = _add_bf16(mine, sib, "pair_sum_" + k)
            parts[k] = p if KIND[k] == "col" else p.reshape(4, -1, p.shape[-1])
        return parts

    def chips(names, parts):
        return _ChipsHook([parts[k] for k in names], [KIND[k] for k in names]) if dist else None

    def reduced(names, hook):
        if dist:
            slabs.update(zip(names, hook.results))

    names = ("w_in", "w_out")
    hook = gather(names)
    x1, g1, u1 = _ffn_fwd(x, gains["ffn1_norm"], w["ffn1_w_gu"], w["ffn1_w_down"], "ffn1_fwd", hook)
    gathered(names, hook)
    names = ("xattn_wq", "xattn_wkv", "xattn_wo")
    hook = gather(names)
    qkv, e = _mix_proj(x1, gains["mix_norm"], w["w_in"], hook)
    gathered(names, hook)
    names = ("ffn2_w_gu",)
    hook = gather(names)
    attn = _swa_fwd(qkv, pos_col, pos_row, bias_t, sinks, hook)
    gathered(names, hook)
    names = ("ffn2_w_down",)
    hook = gather(names)
    x2, merged = _mix_out_fwd(e, attn, conv_w, w["w_out"], x1, hook)
    gathered(names, hook)
    mh, kv = _mem_kv(mem, gains["mem_norm"], w["xattn_wkv"])
    x3, qx, o = _xattn_fwd(x2, gains["xattn_norm"], w["xattn_wq"], kv, w["xattn_wo"])
    x4, g2, u2 = _ffn_fwd(x3, gains["ffn2_norm"], w["ffn2_w_gu"], w["ffn2_w_down"], "ffn2_fwd")
    dx4, d_final, loss = _loss_bwd(x4, gains["final_norm"], target)

    dx3, d_ffn2, dgu2, a2, h4, dyh4 = _ffn_bwd(dx4, x3, gains["ffn2_norm"], g2, u2, w["ffn2_w_gu"], w["ffn2_w_down"],
                                               "ffn2_bwd")
    grads["ffn2_w_gu"] = dw("ffn2_w_gu", h4, dgu2, 1408, "dw_ffn2_gu")
    grads["ffn2_w_down"] = dw("ffn2_w_down", a2, dyh4, 512, "dw_ffn2_down")
    parts = pair_sums(("ffn2_w_gu", "ffn2_w_down"))
    hook = chips(("ffn2_w_gu",), parts)
    dx2, d_xattn, dkv, dxh3, h3, dqx = _xattn_bwd(dx3, x2, gains["xattn_norm"], qx, kv, w["xattn_wq"], w["xattn_wo"], hook)
    reduced(("ffn2_w_gu",), hook)
    grads["xattn_wo"] = dw("xattn_wo", o, dxh3, 1024, "dw_wo")
    grads["xattn_wq"] = dw("xattn_wq", h3, dqx, 1024, "dw_wq")
    grads["xattn_wkv"], d_mem = _mem_bwd(dkv, mh, mem, gains["mem_norm"], w["xattn_wkv"], dist)
    hook = chips(("ffn2_w_down",), parts)
    dattn, de, dcw, dxh2 = _mix_out_bwd(dx2, e, attn, conv_w, w["w_out"], hook)
    reduced(("ffn2_w_down",), hook)
    grads["w_out"] = dw("w_out", merged, dxh2, 1024, "dw_wout")
    names = ("xattn_wo", "xattn_wq", "xattn_wkv", "w_out")
    hook = chips(names, pair_sums(names))
    dq, dkvs, gb, dsk = _swa_bwd(qkv, dattn, pos_col, pos_row, bias_t, sinks, hook)
    reduced(names, hook)
    d_rel_bias, d_sinks = _bias_reduce(gb.reshape(SWA_HEADS, BLOCK, 2 * BLOCK), dsk.reshape(SWA_HEADS, BLOCK, 1))
    dx1, d_mix, h2 = _mix_in_bwd(dq, dkvs, de, x1, gains["mix_norm"], w["w_in"], dx2)
    n_in = w["w_in"].shape[1]
    gw = dw("w_in", h2, dq, 512, "dw_win_q", ncols=n_in)
    gw = dw("w_in", h2, dkvs, 512, "dw_win_kv", into=gw, col_block=dq.shape[1] // 512, ncols=n_in)
    grads["w_in"] = dw("w_in", h2, de, 512, "dw_win_e", into=gw, col_block=(dq.shape[1] + dkvs.shape[1]) // 512,
                       ncols=n_in)
    names = ("w_in",)
    hook = chips(names, pair_sums(names))
    dx0, d_ffn1, dgu1, a1, h1, dyh1 = _ffn_bwd(dx1, x, gains["ffn1_norm"], g1, u1, w["ffn1_w_gu"], w["ffn1_w_down"],
                                               "ffn1_bwd", hook)
    reduced(names, hook)
    grads["ffn1_w_down"] = dw("ffn1_w_down", a1, dyh1, 512, "dw_ffn1_down")
    names = ("ffn1_w_down",)
    hook = chips(names, pair_sums(names))
    grads["ffn1_w_gu"] = dw("ffn1_w_gu", h1, dgu1, 1408, "dw_ffn1_gu", **({"hook": hook} if dist else {}))
    reduced(names, hook)
    rows = {0: d_ffn1, 1: d_mix, 2: d_xattn, 3: d_mem, 4: d_ffn2, 5: d_final, SINK_ROW: d_sinks, BIAS_ROW: d_rel_bias,
            LOSS_ROW: loss[0, 0:1]}
    rows.update({CONV_ROW + j: dcw[j] for j in range(3)})
    last = ("ffn1_w_gu",)
    return dx0, (slabs if dist else grads), _rows_block(rows, d), chips(last, pair_sums(last))


def kernel(x, mem, positions, rel_bias, ffn1_norm, ffn1_w_gu, ffn1_w_down, mix_norm, w_in, sinks, conv_w, w_out, xattn_norm, mem_norm, xattn_wq, xattn_wkv, xattn_wo, ffn2_norm, ffn2_w_gu, ffn2_w_down, final_norm, loss_target, m_rel_bias, m_ffn1_norm, m_ffn1_w_gu, m_ffn1_w_down, m_mix_norm, m_w_in, m_sinks, m_conv_w, m_w_out, m_xattn_norm, m_mem_norm, m_xattn_wq, m_xattn_wkv, m_xattn_wo, m_ffn2_norm, m_ffn2_w_gu, m_ffn2_w_down, m_final_norm, v_rel_bias, v_ffn1_norm, v_ffn1_w_gu, v_ffn1_w_down, v_mix_norm, v_w_in, v_sinks, v_conv_w, v_w_out, v_xattn_norm, v_mem_norm, v_xattn_wq, v_xattn_wkv, v_xattn_wo, v_ffn2_norm, v_ffn2_w_gu, v_ffn2_w_down, v_final_norm):
    args = dict(locals())
    wts = {k: args[k] for k in WEIGHTS}
    mom = {k: args["m_" + k] for k in WEIGHTS}
    var = {k: args["v_" + k] for k in WEIGHTS}
    d = x.shape[-1]
    s_me = 2 * lax.axis_index("x") + lax.axis_index("y")

    shards = {k: wts[k][0].astype(BF) for k in BIG}
    first = ("ffn1_w_gu", "ffn1_w_down")
    cw_cols = conv_w.shape[-1]
    placed = lax.dynamic_update_slice(jnp.zeros((SMALL_ROWS, d), F32), 0.5 * conv_w[0], (0, s_me * cw_cols))
    head = _Both([_GatherHook([shards[k] for k in first], [KIND[k] for k in first]), _SmallSumHook(placed)])
    _exchange(head, "gather_ffn1")
    gathered, (conv_sum,) = head.results
    whole = {k: (gw if KIND[k] == "col" else gw.reshape(-1, gw.shape[-1])) for k, gw in zip(first, gathered)}
    conv_whole = conv_sum[0:3]

    gains = {k: wts[k].reshape(1, d) for k in GAIN_ROW}
    dx0, slabs, small, last_chips = _local_step(x[0], mem[0], positions[0], loss_target[0], whole, gains, rel_bias, sinks,
                                               conv_whole, shards)

    late = ("ffn1_w_gu", "ffn1_w_down", "ffn2_w_down")
    early = tuple(k for k in BIG if k not in late)
    first_ones = tuple(k for k in BIG if k != "ffn1_w_gu")
    tail = _Both([last_chips, _SmallSumHook(small)])
    shard_grads = dict(zip(first_ones, _sum_share([slabs[k] for k in first_ones], "sum_share", tail)))
    (last_slabs,), (small_sum,) = tail.results
    shard_grads["ffn1_w_gu"], = _sum_share([last_slabs], "sum_share_ffn1_w_gu")
    quad = lambda k: (wts[k][0], shard_grads[k], mom[k][0], var[k][0])
    updates = dict(zip(early, _adamw([quad(k) for k in early], "adamw_early", 16, echo=True)))
    updates.update(zip(late, _adamw([quad(k) for k in late], "adamw_late", 8, echo=True)))
    loss = small_sum[LOSS_ROW, 0]

    out_g, out_d, out_m, out_v = {}, {}, {}, {}
    for k in BIG:
        out_g[k], out_d[k], out_m[k], out_v[k] = (a[None] for a in updates[k])

    conv_g = lax.dynamic_slice(small_sum, (CONV_ROW, s_me * cw_cols), (3, cw_cols))

    def pack(src, conv_block):
        rows = {r: src[k] for k, r in GAIN_ROW.items()}
        rows.update({CONV_ROW + j: conv_block[j] for j in range(3)})
        rows[SINK_ROW], rows[BIAS_ROW] = src["sinks"], src["rel_bias"]
        return _rows_block(rows, d)

    small_g = {k: small_sum[r] for k, r in GAIN_ROW.items()}
    small_g["sinks"] = small_sum[SINK_ROW, 0:sinks.size]
    small_g["rel_bias"] = small_sum[BIAS_ROW, 0:rel_bias.size]
    gp = pack(small_g, conv_g)
    (dl, mn, vn), = _adamw([(pack(wts, conv_w[0]), gp, pack(mom, m_conv_w[0]), pack(var, v_conv_w[0]))], "adamw_small", 1)

    def unpack(buf, k):
        if k in GAIN_ROW:
            return buf[GAIN_ROW[k]].reshape(wts[k].shape)
        if k == "conv_w":
            return buf[CONV_ROW:CONV_ROW + 3, 0:cw_cols][None]
        if k == "sinks":
            return buf[SINK_ROW, 0:sinks.size].reshape(sinks.shape)
        return buf[BIAS_ROW, 0:rel_bias.size].reshape(rel_bias.shape)

    for k in WEIGHTS:
        if k not in KIND:
            out_g[k], out_d[k], out_m[k], out_v[k] = unpack(gp, k), unpack(dl, k), unpack(mn, k), unpack(vn, k)

    return (loss, dx0[None], *[out_g[k] for k in WEIGHTS], *[out_d[k] for k in WEIGHTS],
            *[out_m[k] for k in WEIGHTS], *[out_v[k] for k in WEIGHTS])
```

```python
import functools
import math

import jax
import jax.numpy as jnp
from jax import lax
from jax.experimental import pallas as pl
from jax.experimental.pallas import tpu as pltpu

BF = jnp.bfloat16
F32 = jnp.float32
I32 = jnp.int32
S = jax.ShapeDtypeStruct

EPS = 1e-6
NEG = -1e30
POS_PAD = 1 << 30
WINDOW = 128
BLOCK = 128
HEAD_DIM = 64
SWA_HEADS = 16
SWA_KV_HEADS = 4
SWA_GROUP = SWA_HEADS // SWA_KV_HEADS
MEM_HEADS = 4
REL_BUCKETS = 32
REL_MAX_DIST = 128
ADAM_LR = 0.001
ADAM_B1 = 0.9
ADAM_B2 = 0.999
ADAM_EPS = 1e-08
ADAM_WD = 0.01
ADAM_STEP = 10

V7X_VMEM_LIMIT_BYTES = 56 * 1024 * 1024
MESH = pl.DeviceIdType.MESH
ANY = pl.BlockSpec(memory_space=pl.ANY)
VMEM_SPEC = pl.BlockSpec(memory_space=pltpu.VMEM)
SMEM_SPEC = pl.BlockSpec(memory_space=pltpu.SMEM)


def _params(sem=None):
    return pltpu.CompilerParams(dimension_semantics=sem, vmem_limit_bytes=V7X_VMEM_LIMIT_BYTES)


def _resident(shape):
    nd = len(shape)
    return pl.BlockSpec(shape, lambda *_: (0,) * nd, pipeline_mode=pl.Buffered(1))


def _acc_spec(shape):
    nd = len(shape)
    return pl.BlockSpec(shape, lambda *_: (0,) * nd)


def _call(body, *, name, grid, out_shape, in_specs, out_specs, args, sem, scratch_shapes=(), aliases=None, hook=None):
    aliases = aliases or {}
    if hook is None:
        return pl.pallas_call(body, name=name, grid=grid, out_shape=out_shape, in_specs=in_specs, out_specs=out_specs,
                              scratch_shapes=list(scratch_shapes), input_output_aliases=aliases,
                              compiler_params=_params(sem))(*args)
    n_in, n_out, n_scr = len(in_specs), len(out_shape), len(scratch_shapes)
    h_in, h_out = len(hook.ins), len(hook.out_shape)

    def at_step(pick):
        conds = [pl.program_id(ax) == pick(size) for ax, size in enumerate(grid)]
        return functools.reduce(jnp.logical_and, conds)

    def hosted(*refs):
        k_in, x_in = refs[:n_in], refs[n_in:n_in + h_in]
        o0 = n_in + h_in
        k_out, x_out = refs[o0:o0 + n_out], refs[o0 + n_out:o0 + n_out + h_out]
        s0 = o0 + n_out + h_out
        k_scr, x_scr = refs[s0:s0 + n_scr], refs[s0 + n_scr:]

        @pl.when(at_step(lambda size: 0))
        def _():
            hook.start(x_in, x_out, x_scr)
        body(*k_in, *k_out, *k_scr)

        @pl.when(at_step(lambda size: size - 1))
        def _():
            hook.finish(x_in, x_out, x_scr)

    res = pl.pallas_call(
        hosted, name=name, grid=grid, out_shape=tuple(out_shape) + tuple(hook.out_shape),
        in_specs=list(in_specs) + [ANY] * h_in, out_specs=tuple(out_specs) + (ANY,) * h_out,
        scratch_shapes=list(scratch_shapes) + list(hook.scratch), input_output_aliases=aliases,
        compiler_params=_params(("arbitrary",) * len(grid)),
    )(*args, *hook.ins)
    hook.results = res[n_out:]
    return res[:n_out]


def _nn(a, b):
    return jnp.dot(a, b, preferred_element_type=F32)


def _nt(a, b):
    return lax.dot_general(a, b, (((1,), (1,)), ((), ())), preferred_element_type=F32)


def _tn(a, b):
    return lax.dot_general(a, b, (((0,), (0,)), ((), ())), preferred_element_type=F32)


def _sigmoid(v):
    return 1.0 / (1.0 + jnp.exp(-v))


def _rms(x):
    r = lax.rsqrt(jnp.mean(x * x, axis=-1, keepdims=True) + EPS)
    return x * r, r


def _rms_bwd(dh, n, r, g):
    dn = dh * g
    dx = r * (dn - n * jnp.mean(dn * n, axis=-1, keepdims=True))
    return dx, jnp.sum(dh * n, axis=0, keepdims=True)


def _ffn_fwd(x, gn, wgu, wd, name, hook=None, head=None):
    t, d = x.shape
    f = wd.shape[0]
    tm, fc = 256, 1408

    def body(x_ref, gn_ref, wgu_ref, wd_ref, *rest):
        xv = x_ref[...]
        n, _ = _rms(xv)
        h = (n * gn_ref[...]).astype(BF)
        g_ref, u_ref = rest[-2:] if head is None else rest[3:5]
        acc = jnp.zeros((tm, d), F32)
        for c0 in range(0, f, fc):
            g = _nn(h, wgu_ref[:, c0:c0 + fc])
            u = _nn(h, wgu_ref[:, f + c0:f + c0 + fc])
            g_ref[:, c0:c0 + fc] = g.astype(BF)
            u_ref[:, c0:c0 + fc] = u.astype(BF)
            a = (g * _sigmoid(g)) * u
            acc = acc + _nn(a.astype(BF), wd_ref[c0:c0 + fc, :])
        y = xv + 0.5 * acc
        if head is None:
            rest[0][...] = y
            return
        gf_ref, t_ref, dy_ref, _, _, dgf_ref, loss_ref = rest

        @pl.when(pl.program_id(0) == 0)
        def _():
            dgf_ref[...] = jnp.zeros_like(dgf_ref)
            loss_ref[...] = jnp.zeros_like(loss_ref)
        ny, ry = _rms(y)
        gf = gf_ref[...]
        err = ny * gf - t_ref[...]
        loss_ref[...] += 0.5 * jnp.sum(jnp.sum(err * err, axis=-1, keepdims=True) / d, axis=0, keepdims=True)
        dy, dgf = _rms_bwd(err / d, ny, ry, gf)
        dy_ref[...] = dy
        dgf_ref[...] += dgf

    row = pl.BlockSpec((tm, d), lambda i: (i, 0))
    frow = pl.BlockSpec((tm, f), lambda i: (i, 0))
    in_specs = [row, _resident((1, d)), _resident(wgu.shape), _resident(wd.shape)]
    out_shape = (S((t, d), F32), S((t, f), BF), S((t, f), BF))
    out_specs = (row, frow, frow)
    args = (x, gn, wgu, wd)
    if head is not None:
        in_specs += [_resident((1, d)), row]
        out_shape += (S((1, d), F32), S((1, 128), F32))
        out_specs += (_acc_spec((1, d)), _acc_spec((1, 128)))
        args += tuple(head)
    return _call(body, name=name, grid=(t // tm,), out_shape=out_shape, in_specs=in_specs, out_specs=out_specs,
                 sem=("parallel",) if head is None else ("arbitrary",), args=args, hook=hook)


def _mix_proj(x, gn, w_in, hook=None):
    t, d = x.shape
    tm = 256
    nqkv = 1536
    ne = w_in.shape[1] - nqkv

    def body(x_ref, gn_ref, w_ref, qkv_ref, e_ref):
        n, _ = _rms(x_ref[...])
        h = (n * gn_ref[...]).astype(BF)
        qkv_ref[...] = _nn(h, w_ref[:, 0:nqkv]).astype(BF)
        for c0 in range(0, ne, 1024):
            e_ref[:, c0:c0 + 1024] = _nn(h, w_ref[:, nqkv + c0:nqkv + c0 + 1024]).astype(BF)

    return _call(
        body, name="mix_proj", grid=(t // tm,),
        out_shape=(S((t, nqkv), BF), S((t, ne), BF)),
        in_specs=[pl.BlockSpec((tm, d), lambda i: (i, 0)), _resident((1, d)), _resident(w_in.shape)],
        out_specs=(pl.BlockSpec((tm, nqkv), lambda i: (i, 0)), pl.BlockSpec((tm, ne), lambda i: (i, 0))),
        sem=("parallel",), args=(x, gn, w_in), hook=hook)


def _t5_bucket(rel):
    n = jnp.maximum(rel, 0)
    max_exact = REL_BUCKETS // 2
    nf = jnp.maximum(n, 1).astype(F32)
    large = max_exact + (jnp.log(nf / max_exact) / math.log(REL_MAX_DIST / max_exact)
                         * (REL_BUCKETS - max_exact)).astype(I32)
    large = jnp.minimum(large, REL_BUCKETS - 1)
    return jnp.where(n < max_exact, n, large)


def _block_rel():
    i = lax.broadcasted_iota(I32, (BLOCK, 2 * BLOCK), 0)
    j = lax.broadcasted_iota(I32, (BLOCK, 2 * BLOCK), 1)
    return i + BLOCK - j


def _bias_build(rel_bias):
    def body(rb_ref, o_ref):
        bucket = _t5_bucket(_block_rel())
        for h in range(SWA_HEADS):
            acc = jnp.zeros((BLOCK, 2 * BLOCK), F32)
            for b in range(REL_BUCKETS):
                acc = jnp.where(bucket == b, rb_ref[b, h], acc)
            o_ref[h] = acc

    return pl.pallas_call(
        body, name="bias_build", out_shape=S((SWA_HEADS, BLOCK, 2 * BLOCK), F32),
        in_specs=[SMEM_SPEC], out_specs=VMEM_SPEC,
    )(rel_bias)


GROUP_ROWS = SWA_GROUP * BLOCK


def _swa_visible(b, pq_ref, pkp_ref, pkc_ref):
    pk = jnp.concatenate([pkp_ref[...], pkc_ref[...]], axis=1)
    col = lax.broadcasted_iota(I32, (1, 2 * BLOCK), 1)
    pk = jnp.where(jnp.logical_and(b == 0, col < BLOCK), POS_PAD, pk)
    rel = jnp.concatenate([pq_ref[...]] * SWA_GROUP, axis=0) - pk
    return jnp.logical_and(rel >= 0, rel < WINDOW)


def _group_heads(ref, hk):
    h0 = hk * SWA_GROUP
    return jnp.concatenate([ref[:, (h0 + g) * HEAD_DIM:(h0 + g + 1) * HEAD_DIM] for g in range(SWA_GROUP)], axis=0)


def _group_sinks(sink_ref, hk):
    row = lax.broadcasted_iota(I32, (GROUP_ROWS, 1), 0)
    col = jnp.zeros((GROUP_ROWS, 1), F32) + sink_ref[0, hk * SWA_GROUP]
    for g in range(1, SWA_GROUP):
        col = jnp.where(row >= g * BLOCK, sink_ref[0, hk * SWA_GROUP + g], col)
    return col


def _swa_probs(qg, kh, vis, bias, sink):
    s = _nt(qg, kh) * (HEAD_DIM ** -0.5)
    s = jnp.where(vis, s + bias, NEG)
    m = jnp.maximum(jnp.max(s, axis=-1, keepdims=True), sink)
    p = jnp.exp(s - m)
    ps = jnp.exp(sink - m)
    inv = 1.0 / (jnp.sum(p, axis=-1, keepdims=True) + ps)
    return p * inv, ps * inv


def _swa_fwd(qkv, pos_col, pos_row, bias_t, sinks, hook=None):
    t = qkv.shape[0]
    nb = t // BLOCK
    qw = SWA_HEADS * HEAD_DIM
    kw = SWA_KV_HEADS * HEAD_DIM

    def body(q_ref, kp_ref, kc_ref, vp_ref, vc_ref, pq_ref, pkp_ref, pkc_ref, bias_ref, sink_ref, o_ref):
        b = pl.program_id(0)
        vis = _swa_visible(b, pq_ref, pkp_ref, pkc_ref)
        k2 = jnp.concatenate([kp_ref[...], kc_ref[...]], axis=0)
        v2 = jnp.concatenate([vp_ref[...], vc_ref[...]], axis=0)
        for hk in range(SWA_KV_HEADS):
            kh = k2[:, hk * HEAD_DIM:(hk + 1) * HEAD_DIM]
            vh = v2[:, hk * HEAD_DIM:(hk + 1) * HEAD_DIM]
            pn, _ = _swa_probs(_group_heads(q_ref, hk), kh, vis, bias_ref[hk], _group_sinks(sink_ref, hk))
            o = _nn(pn.astype(BF), vh)
            for g in range(SWA_GROUP):
                h = hk * SWA_GROUP + g
                o_ref[:, h * HEAD_DIM:(h + 1) * HEAD_DIM] = o[g * BLOCK:(g + 1) * BLOCK]

    prev = lambda b: jnp.maximum(b - 1, 0)
    return _call(
        body, name="swa_fwd", grid=(nb,), out_shape=(S((t, qw), F32),),
        in_specs=[
            pl.BlockSpec((BLOCK, qw), lambda b: (b, 0)),
            pl.BlockSpec((BLOCK, kw), lambda b: (prev(b), qw // kw)),
            pl.BlockSpec((BLOCK, kw), lambda b: (b, qw // kw)),
            pl.BlockSpec((BLOCK, kw), lambda b: (prev(b), qw // kw + 1)),
            pl.BlockSpec((BLOCK, kw), lambda b: (b, qw // kw + 1)),
            pl.BlockSpec((BLOCK, 1), lambda b: (b, 0)),
            pl.BlockSpec((1, BLOCK), lambda b: (0, prev(b))),
            pl.BlockSpec((1, BLOCK), lambda b: (0, b)),
            _resident(bias_t.shape),
            SMEM_SPEC,
        ],
        out_specs=(pl.BlockSpec((BLOCK, qw), lambda b: (b, 0)),),
        sem=("parallel",), args=(qkv, qkv, qkv, qkv, qkv, pos_col, pos_row, pos_row, bias_t, sinks), hook=hook)[0]


HALO = 16


def _conv_taps(z, zh, first):
    tm = z.shape[0]
    zh = jnp.where(first, 0.0, zh)
    row = lax.broadcasted_iota(I32, (tm, 1), 0)
    z1 = jnp.where(row == 0, zh[HALO - 1:HALO, :], pltpu.roll(z, 1, 0))
    z2 = jnp.where(row == 0, zh[HALO - 2:HALO - 1, :], jnp.where(row == 1, zh[HALO - 1:HALO, :], pltpu.roll(z, 2, 0)))
    return z1, z2


def _mix_out_fwd(e, attn, conv_w, w_out, x, hook=None):
    t, d = x.shape
    tm = 256
    hb = tm // HALO
    f32 = lambda ref: ref[...].astype(F32)

    def body(c_ref, b_ref, u_ref, ga_ref, gc_ref, ch_ref, uh_ref, attn_ref, cw_ref, w_ref, x_ref, xo_ref, mg_ref):
        i = pl.program_id(0)
        z = f32(c_ref) * f32(u_ref)
        z1, z2 = _conv_taps(z, f32(ch_ref) * f32(uh_ref), i == 0)
        s = cw_ref[0:1, :] * z2 + cw_ref[1:2, :] * z1 + cw_ref[2:3, :] * z
        conv = f32(b_ref) * s
        merged = (_sigmoid(f32(ga_ref)) * attn_ref[...] + _sigmoid(f32(gc_ref)) * conv).astype(BF)
        mg_ref[...] = merged
        xo_ref[...] = x_ref[...] + _nn(merged, w_ref[...])

    ecol = lambda cb: pl.BlockSpec((tm, d), lambda i: (i, cb))
    halo = lambda cb: pl.BlockSpec((HALO, d), lambda i: (jnp.maximum(i * hb - 1, 0), cb))
    row = pl.BlockSpec((tm, d), lambda i: (i, 0))
    return _call(
        body, name="mix_out_fwd", grid=(t // tm,),
        out_shape=(S((t, d), F32), S((t, d), BF)),
        in_specs=[ecol(0), ecol(1), ecol(2), ecol(3), ecol(4), halo(0), halo(2), row,
                  _resident(conv_w.shape), _resident(w_out.shape), row],
        out_specs=(row, row),
        sem=("parallel",), args=(e, e, e, e, e, e, e, attn, conv_w, w_out, x), hook=hook)


def _mem_kv(mem, gm, wkv):
    m, d = mem.shape

    def body(mem_ref, gm_ref, w_ref, mh_ref, kv_ref):
        n, _ = _rms(mem_ref[...])
        mh = (n * gm_ref[...]).astype(BF)
        mh_ref[...] = mh
        kv_ref[...] = _nn(mh, w_ref[...]).astype(BF)

    return pl.pallas_call(
        body, name="mem_kv", out_shape=(S((m, d), BF), S((m, wkv.shape[1]), BF)),
        compiler_params=_params(),
    )(mem, gm, wkv)


def _xattn_probs(qh, kh):
    s = _nt(qh, kh) * (kh.shape[1] ** -0.5)
    p = jnp.exp(s - jnp.max(s, axis=-1, keepdims=True))
    return p * (1.0 / jnp.sum(p, axis=-1, keepdims=True))


def _xattn_fwd(x, gn, wq, kv, wo):
    t, d = x.shape
    tm = 256
    hd = d // MEM_HEADS

    def body(x_ref, gn_ref, wq_ref, kv_ref, wo_ref, xo_ref, q_ref, o_ref):
        xv = x_ref[...]
        n, _ = _rms(xv)
        q = _nn((n * gn_ref[...]).astype(BF), wq_ref[...]).astype(BF)
        q_ref[...] = q
        outs = []
        for hh in range(MEM_HEADS):
            p = _xattn_probs(q[:, hh * hd:(hh + 1) * hd], kv_ref[:, hh * hd:(hh + 1) * hd])
            outs.append(_nn(p.astype(BF), kv_ref[:, d + hh * hd:d + (hh + 1) * hd]))
        o = jnp.concatenate(outs, axis=1).astype(BF)
        o_ref[...] = o
        xo_ref[...] = xv + _nn(o, wo_ref[...])

    row = pl.BlockSpec((tm, d), lambda i: (i, 0))
    return pl.pallas_call(
        body, name="xattn_fwd", grid=(t // tm,),
        out_shape=(S((t, d), F32), S((t, d), BF), S((t, d), BF)),
        in_specs=[row, _resident((1, d)), _resident(wq.shape), _resident(kv.shape), _resident(wo.shape)],
        out_specs=(row, row, row),
        compiler_params=_params(("parallel",)),
    )(x, gn, wq, kv, wo)


def _ffn_bwd(dxo, x, gn, g, u, wgu, wd, name, hook=None):
    t, d = x.shape
    f = wd.shape[0]
    tm, fc = 256, 1408

    def body(dxo_ref, x_ref, gn_ref, g_ref, u_ref, wgu_ref, wd_ref, dx_ref, dgn_ref, dgu_ref, a_ref, h_ref, dyh_ref):
        @pl.when(pl.program_id(0) == 0)
        def _():
            dgn_ref[...] = jnp.zeros_like(dgn_ref)
        dxov = dxo_ref[...]
        dyh = (0.5 * dxov).astype(BF)
        dyh_ref[...] = dyh
        n, r = _rms(x_ref[...])
        gnv = gn_ref[...]
        h_ref[...] = (n * gnv).astype(BF)
        dh = jnp.zeros((tm, d), F32)
        for c0 in range(0, f, fc):
            gv = g_ref[:, c0:c0 + fc].astype(F32)
            uv = u_ref[:, c0:c0 + fc].astype(F32)
            da = _nt(dyh, wd_ref[c0:c0 + fc, :])
            sg = _sigmoid(gv)
            silu = gv * sg
            a_ref[:, c0:c0 + fc] = (silu * uv).astype(BF)
            dg = (da * uv * (sg * (1.0 + gv * (1.0 - sg)))).astype(BF)
            du = (da * silu).astype(BF)
            dgu_ref[:, c0:c0 + fc] = dg
            dgu_ref[:, f + c0:f + c0 + fc] = du
            dh = dh + _nt(dg, wgu_ref[:, c0:c0 + fc]) + _nt(du, wgu_ref[:, f + c0:f + c0 + fc])
        dx, dgn = _rms_bwd(dh, n, r, gnv)
        dx_ref[...] = dxov + dx
        dgn_ref[...] += dgn

    row = pl.BlockSpec((tm, d), lambda i: (i, 0))
    frow = pl.BlockSpec((tm, f), lambda i: (i, 0))
    return _call(
        body, name=name, grid=(t // tm,),
        out_shape=(S((t, d), F32), S((1, d), F32), S((t, 2 * f), BF), S((t, f), BF), S((t, d), BF), S((t, d), BF)),
        in_specs=[row, row, _resident((1, d)), frow, frow, _resident(wgu.shape), _resident(wd.shape)],
        out_specs=(row, _acc_spec((1, d)), pl.BlockSpec((tm, 2 * f), lambda i: (i, 0)), frow, row, row),
        sem=("arbitrary",), args=(dxo, x, gn, g, u, wgu, wd), hook=hook)


def _dw(a, b, tn, name, into=None, col_block=0, ncols=None):
    t, ka = a.shape
    nb = b.shape[1]
    tt = 1024
    nt = t // tt
    ncols = nb if ncols is None else ncols

    def body(*refs):
        a_ref, b_ref = refs[0], refs[1]
        o_ref, acc_ref = refs[-2], refs[-1]
        k = pl.program_id(1)

        @pl.when(k == 0)
        def _():
            acc_ref[...] = jnp.zeros_like(acc_ref)
        acc_ref[...] += _tn(a_ref[...], b_ref[...])

        @pl.when(k == nt - 1)
        def _():
            o_ref[...] = acc_ref[...].astype(BF)

    in_specs = [pl.BlockSpec((tt, ka), lambda j, k: (k, 0)), pl.BlockSpec((tt, tn), lambda j, k: (k, j))]
    args = [a, b]
    aliases = {}
    if into is not None:
        in_specs.append(ANY)
        args.append(into)
        aliases = {2: 0}
    return pl.pallas_call(
        body, name=name, grid=(nb // tn, nt), out_shape=S((ka, ncols), BF),
        in_specs=in_specs, out_specs=pl.BlockSpec((ka, tn), lambda j, k: (0, j + col_block)),
        scratch_shapes=[pltpu.VMEM((ka, tn), F32)], input_output_aliases=aliases,
        compiler_params=_params(("parallel", "arbitrary")),
    )(*args)


def _dw_pair(a, b, tn, name, kind, into=None, col_block=0, ncols=None, hook=None):
    t, ka = a.shape
    nb = b.shape[1]
    tt = 2048
    nt, nj = t // tt, nb // tn
    ncols = nb if ncols is None else ncols
    col = kind == "col"
    rh = ka // 2 if col else ka // 8
    tile = (rh, tn) if col else (4, rh, tn)
    half = (rh, ncols) if col else (4, rh, ncols)
    lead = (slice(None),) * (len(tile) - 1)

    def body(*refs):
        a_ref, b_ref = refs[0], refs[1]
        mine_ref, sib_ref, acc_ref, stage, ssem, rsem = refs[-6:]
        j, k = pl.program_id(0), pl.program_id(1)
        x, y, c = _place()
        sibling = (x, y, 1 - c)

        @pl.when(k == 0)
        def _():
            acc_ref[...] = jnp.zeros_like(acc_ref)
        acc_ref[...] += _tn(a_ref[...], b_ref[...])

        def send(slot, jj):
            dst = sib_ref.at[lead + (pl.ds(pl.multiple_of((jj + col_block) * tn, 128), tn),)]
            return _remote(stage.at[slot], dst, ssem.at[slot], rsem, sibling)

        def rows(s, whose):
            return acc_ref[pl.ds(pl.multiple_of(s * 2 * rh + whose * rh, 16), rh), :].astype(BF)

        @pl.when(k == nt - 1)
        def _():
            slot = j % 2

            @pl.when(j >= 2)
            def _():
                send(slot, j).wait_send()
            if col:
                mine_ref[...] = rows(0, c)
                stage[slot] = rows(0, 1 - c)
            else:
                for s in range(4):
                    mine_ref[s] = rows(s, c)
                    stage[slot, s] = rows(s, 1 - c)
            send(slot, j).start()

        @pl.when(jnp.logical_and(j == nj - 1, k == nt - 1))
        def _():
            for jj in range(max(nj - 2, 0), nj):
                send(jj % 2, jj).wait_send()
            got = sib_ref.at[lead + (pl.ds(col_block * tn, nb),)]
            _remote(got, got, ssem.at[0], rsem, sibling).wait_recv()

    in_specs = [pl.BlockSpec((tt, ka), lambda j, k: (k, 0)), pl.BlockSpec((tt, tn), lambda j, k: (k, j))]
    args = [a, b]
    aliases = {}
    if into is not None:
        in_specs += [ANY, ANY]
        args += list(into)
        aliases = {2: 0, 3: 1}
    mine_spec = pl.BlockSpec(tile, (lambda j, k: (0, j + col_block)) if col else (lambda j, k: (0, 0, j + col_block)))
    return _call(
        body, name=name, grid=(nj, nt), out_shape=(S(half, BF), S(half, BF)),
        in_specs=in_specs, out_specs=(mine_spec, ANY),
        scratch_shapes=[pltpu.VMEM((ka, tn), F32), pltpu.VMEM((2,) + tile, BF), pltpu.SemaphoreType.DMA((2,)),
                        pltpu.SemaphoreType.DMA],
        aliases=aliases, sem=("arbitrary", "arbitrary"), args=args, hook=hook)


def _xattn_bwd(dxo, x, gn, q, kv, wq, wo, hook=None):
    t, d = x.shape
    tm = 256
    hd = d // MEM_HEADS
    nkv = kv.shape[0]

    def body(dxo_ref, x_ref, gn_ref, q_ref, kv_ref, wq_ref, wo_ref, dx_ref, dgn_ref, dkv_ref, dxh_ref, h_ref, dq_ref):
        @pl.when(pl.program_id(0) == 0)
        def _():
            dgn_ref[...] = jnp.zeros_like(dgn_ref)
            dkv_ref[...] = jnp.zeros_like(dkv_ref)
        dxov = dxo_ref[...]
        dxh = dxov.astype(BF)
        dxh_ref[...] = dxh
        do = _nt(dxh, wo_ref[...]).astype(BF)
        dqs = []
        for hh in range(MEM_HEADS):
            lo, hi = hh * hd, (hh + 1) * hd
            qh = q_ref[:, lo:hi]
            kh = kv_ref[:, lo:hi]
            vh = kv_ref[:, d + lo:d + hi]
            doh = do[:, lo:hi]
            p = _xattn_probs(qh, kh)
            dp = _nt(doh, vh)
            ds = (p * (dp - jnp.sum(p * dp, axis=-1, keepdims=True)) * (hd ** -0.5)).astype(BF)
            dqs.append(_nn(ds, kh))
            dkv_ref[:, lo:hi] += _tn(ds, qh)
            dkv_ref[:, d + lo:d + hi] += _tn(p.astype(BF), doh)
        dq = jnp.concatenate(dqs, axis=1).astype(BF)
        dq_ref[...] = dq
        n, r = _rms(x_ref[...])
        gnv = gn_ref[...]
        h_ref[...] = (n * gnv).astype(BF)
        dx, dgn = _rms_bwd(_nt(dq, wq_ref[...]), n, r, gnv)
        dx_ref[...] = dxov + dx
        dgn_ref[...] += dgn

    row = pl.BlockSpec((tm, d), lambda i: (i, 0))
    return _call(
        body, name="xattn_bwd", grid=(t // tm,),
        out_shape=(S((t, d), F32), S((1, d), F32), S((nkv, 2 * d), F32), S((t, d), BF), S((t, d), BF), S((t, d), BF)),
        in_specs=[row, row, _resident((1, d)), row, _resident(kv.shape), _resident(wq.shape), _resident(wo.shape)],
        out_specs=(row, _acc_spec((1, d)), _acc_spec((nkv, 2 * d)), row, row, row),
        sem=("arbitrary",), args=(dxo, x, gn, q, kv, wq, wo), hook=hook)


def _mem_bwd(dkv, mh, mem, gm, wkv, pair=False):
    m, d = mem.shape
    rows, cols = wkv.shape
    rh = rows // 2

    def body(dkv_ref, mh_ref, mem_ref, gm_ref, w_ref, *outs):
        dkvb = dkv_ref[...].astype(BF)
        dw = _tn(mh_ref[...], dkvb).astype(BF)
        dmh = _nt(dkvb, w_ref[...])
        n, _ = _rms(mem_ref[...])
        if not pair:
            dw_ref, dgm_ref = outs
            dw_ref[...] = dw
        else:
            mine_ref, sib_ref, dgm_ref, whole, ssem, rsem = outs
            x, y, c = _place()
            whole[...] = dw
            cp = _remote(whole.at[pl.ds(pl.multiple_of((1 - c) * rh, 16), rh), :], sib_ref, ssem, rsem, (x, y, 1 - c))
            cp.start()
            mine_ref[...] = whole[pl.ds(pl.multiple_of(c * rh, 16), rh), :]
            cp.wait()
        dgm_ref[...] = jnp.sum(dmh * n, axis=0, keepdims=True)

    if not pair:
        return pl.pallas_call(body, name="mem_bwd", out_shape=(S(wkv.shape, BF), S((1, d), F32)),
                              compiler_params=_params())(dkv, mh, mem, gm, wkv)
    mine, sib, dgm = pl.pallas_call(
        body, name="mem_bwd", out_shape=(S((rh, cols), BF), S((rh, cols), BF), S((1, d), F32)),
        in_specs=[VMEM_SPEC] * 5, out_specs=(VMEM_SPEC, ANY, VMEM_SPEC),
        scratch_shapes=[pltpu.VMEM((rows, cols), BF), pltpu.SemaphoreType.DMA, pltpu.SemaphoreType.DMA],
        compiler_params=_params())(dkv, mh, mem, gm, wkv)
    return (mine, sib), dgm


def _mix_out_bwd(dxo, e, attn, conv_w, w_out, nqkv, hook=None):
    t, d = attn.shape
    tm = 256
    nt = t // tm
    f32 = lambda ref: ref[...].astype(F32)

    def body(dxo_ref, dxn_ref, c_ref, b_ref, u_ref, ga_ref, gc_ref, ch_ref, uh_ref, bn_ref, gcn_ref,
             attn_ref, cw_ref, w_ref, dattn_ref, de_ref, dcw_ref, dxh_ref):
        i = pl.program_id(0)

        @pl.when(i == 0)
        def _():
            dcw_ref[...] = jnp.zeros_like(dcw_ref)
        dxh = dxo_ref[...].astype(BF)
        dxh_ref[...] = dxh
        w = w_ref[...]
        dm = _nt(dxh, w)
        dmn = _nt(dxn_ref[...].astype(BF), w)
        cv, bv, uv = f32(c_ref), f32(b_ref), f32(u_ref)
        sga = _sigmoid(f32(ga_ref))
        sgc = _sigmoid(f32(gc_ref))
        z = cv * uv
        z1, z2 = _conv_taps(z, f32(ch_ref) * f32(uh_ref), i == 0)
        w0, w1, w2 = cw_ref[0:1, :], cw_ref[1:2, :], cw_ref[2:3, :]
        s = w0 * z2 + w1 * z1 + w2 * z
        av = attn_ref[...]
        dattn_ref[...] = (dm * sga).astype(BF)
        dconv = dm * sgc
        ds = dconv * bv
        dsn = jnp.where(i == nt - 1, 0.0, dmn * _sigmoid(gcn_ref[0:8, :].astype(F32)) * bn_ref[0:8, :].astype(F32))
        row = lax.broadcasted_iota(I32, (tm, 1), 0)
        dsp1 = jnp.where(row == tm - 1, dsn[0:1, :], pltpu.roll(ds, tm - 1, 0))
        dsp2 = jnp.where(row == tm - 2, dsn[0:1, :], jnp.where(row == tm - 1, dsn[1:2, :], pltpu.roll(ds, tm - 2, 0)))
        dz = w2 * ds + w1 * dsp1 + w0 * dsp2
        de_ref[:, nqkv:nqkv + d] = (dz * uv).astype(BF)
        de_ref[:, nqkv + d:nqkv + 2 * d] = (dconv * s).astype(BF)
        de_ref[:, nqkv + 2 * d:nqkv + 3 * d] = (dz * cv).astype(BF)
        de_ref[:, nqkv + 3 * d:nqkv + 4 * d] = (dm * av * sga * (1.0 - sga)).astype(BF)
        de_ref[:, nqkv + 4 * d:nqkv + 5 * d] = (dm * (bv * s) * sgc * (1.0 - sgc)).astype(BF)
        dcw_ref[0:1, :] += jnp.sum(ds * z2, axis=0, keepdims=True)
        dcw_ref[1:2, :] += jnp.sum(ds * z1, axis=0, keepdims=True)
        dcw_ref[2:3, :] += jnp.sum(ds * z, axis=0, keepdims=True)

    ecol = lambda cb: pl.BlockSpec((tm, d), lambda i: (i, cb))
    prev = lambda cb: pl.BlockSpec((HALO, d), lambda i: (jnp.maximum(i * (tm // HALO) - 1, 0), cb))
    nxt = lambda rows, cb: pl.BlockSpec((rows, d), lambda i: (jnp.minimum((i + 1) * (tm // rows), t // rows - 1), cb))
    row = pl.BlockSpec((tm, d), lambda i: (i, 0))
    return _call(
        body, name="mix_out_bwd", grid=(nt,),
        out_shape=(S((t, d), BF), S((t, nqkv + 5 * d), BF), S((8, d), F32), S((t, d), BF)),
        in_specs=[row, nxt(8, 0), ecol(0), ecol(1), ecol(2), ecol(3), ecol(4), prev(0), prev(2), nxt(HALO, 1), nxt(HALO, 4),
                  row, _resident(conv_w.shape), _resident(w_out.shape)],
        out_specs=(row, pl.BlockSpec((tm, nqkv + 5 * d), lambda i: (i, 0)), _acc_spec((8, d)), row),
        sem=("arbitrary",), args=(dxo, dxo, e, e, e, e, e, e, e, e, e, attn, conv_w, w_out), hook=hook)


def _swa_bwd(qkv, dattn, pos_col, pos_row, bias_t, sinks, dp, hook=None):
    t = qkv.shape[0]
    nb = t // BLOCK
    qw = SWA_HEADS * HEAD_DIM
    kw = SWA_KV_HEADS * HEAD_DIM

    def body(q_ref, kp_ref, kc_ref, vp_ref, vc_ref, do_ref, pq_ref, pkp_ref, pkc_ref, bias_ref, sink_ref,
             dp_in, dp_ref, gb_ref, dsk_ref, carry_ref, dq_ref):
        b = pl.program_id(0)

        @pl.when(b == 0)
        def _():
            gb_ref[...] = jnp.zeros_like(gb_ref)
            dsk_ref[...] = jnp.zeros_like(dsk_ref)
            carry_ref[...] = jnp.zeros_like(carry_ref)
            dq_ref[...] = jnp.zeros_like(dq_ref)
        dp_ref[:, 0:qw] = dq_ref[...]

        @pl.when(b < nb)
        def _():
            vis = _swa_visible(b, pq_ref, pkp_ref, pkc_ref)
            k2 = jnp.concatenate([kp_ref[...], kc_ref[...]], axis=0)
            v2 = jnp.concatenate([vp_ref[...], vc_ref[...]], axis=0)
            for hk in range(SWA_KV_HEADS):
                lo, hi = hk * HEAD_DIM, (hk + 1) * HEAD_DIM
                kh = k2[:, lo:hi]
                vh = v2[:, lo:hi]
                qg = _group_heads(q_ref, hk)
                dog = _group_heads(do_ref, hk)
                pn, psn = _swa_probs(qg, kh, vis, bias_ref[hk], _group_sinks(sink_ref, hk))
                dp = _nt(dog, vh)
                delta = jnp.sum(pn * dp, axis=-1, keepdims=True)
                ds = pn * (dp - delta)
                gb_ref[hk] += ds
                dsk_ref[hk] += -psn * delta
                dsb = (ds * (HEAD_DIM ** -0.5)).astype(BF)
                dqg = _nn(dsb, kh).astype(BF)
                for g in range(SWA_GROUP):
                    h = hk * SWA_GROUP + g
                    dq_ref[:, h * HEAD_DIM:(h + 1) * HEAD_DIM] = dqg[g * BLOCK:(g + 1) * BLOCK]
                dk = _tn(dsb, qg)
                dv = _tn(pn.astype(BF), dog)
                dp_ref[:, qw + lo:qw + hi] = (carry_ref[:, lo:hi] + dk[0:BLOCK]).astype(BF)
                dp_ref[:, qw + kw + lo:qw + kw + hi] = (carry_ref[:, kw + lo:kw + hi] + dv[0:BLOCK]).astype(BF)
                carry_ref[:, lo:hi] = dk[BLOCK:2 * BLOCK]
                carry_ref[:, kw + lo:kw + hi] = dv[BLOCK:2 * BLOCK]

        @pl.when(b == nb)
        def _():
            dp_ref[:, qw:qw + 2 * kw] = carry_ref[...].astype(BF)

    cur = lambda b: jnp.minimum(b, nb - 1)
    prev = lambda b: jnp.maximum(cur(b) - 1, 0)
    return _call(
        body, name="swa_bwd", grid=(nb + 1,),
        out_shape=(S(dp.shape, BF), S((SWA_KV_HEADS, GROUP_ROWS, 2 * BLOCK), F32), S((SWA_KV_HEADS, GROUP_ROWS, 1), F32)),
        in_specs=[
            pl.BlockSpec((BLOCK, qw), lambda b: (cur(b), 0)),
            pl.BlockSpec((BLOCK, kw), lambda b: (prev(b), qw // kw)),
            pl.BlockSpec((BLOCK, kw), lambda b: (cur(b), qw // kw)),
            pl.BlockSpec((BLOCK, kw), lambda b: (prev(b), qw // kw + 1)),
            pl.BlockSpec((BLOCK, kw), lambda b: (cur(b), qw // kw + 1)),
            pl.BlockSpec((BLOCK, qw), lambda b: (cur(b), 0)),
            pl.BlockSpec((BLOCK, 1), lambda b: (cur(b), 0)),
            pl.BlockSpec((1, BLOCK), lambda b: (0, prev(b))),
            pl.BlockSpec((1, BLOCK), lambda b: (0, cur(b))),
            _resident(bias_t.shape),
            SMEM_SPEC,
            ANY,
        ],
        out_specs=(
            pl.BlockSpec((BLOCK, qw + 2 * kw), lambda b: (jnp.maximum(b - 1, 0), 0)),
            _acc_spec((SWA_KV_HEADS, GROUP_ROWS, 2 * BLOCK)),
            _acc_spec((SWA_KV_HEADS, GROUP_ROWS, 1)),
        ),
        scratch_shapes=[pltpu.VMEM((BLOCK, 2 * kw), F32), pltpu.VMEM((BLOCK, qw), BF)],
        aliases={11: 0}, sem=("arbitrary",),
        args=(qkv, qkv, qkv, qkv, qkv, dattn, pos_col, pos_row, pos_row, bias_t, sinks, dp), hook=hook)


def _bias_reduce(gb, dsk):
    def body(gb_ref, dsk_ref, drb_ref, dsink_ref):
        bucket = _t5_bucket(_block_rel())
        for b in range(REL_BUCKETS):
            mask = bucket == b
            for h in range(SWA_HEADS):
                drb_ref[b, h] = jnp.sum(jnp.where(mask, gb_ref[h], 0.0))
        for h in range(SWA_HEADS):
            dsink_ref[0, h] = jnp.sum(dsk_ref[h])

    return pl.pallas_call(
        body, name="bias_reduce", out_shape=(S((REL_BUCKETS, SWA_HEADS), F32), S((1, SWA_HEADS), F32)),
        in_specs=[VMEM_SPEC, VMEM_SPEC], out_specs=(SMEM_SPEC, SMEM_SPEC),
    )(gb, dsk)


def _mix_in_bwd(dp, x, gn, w_in, dxo):
    t, d = x.shape
    tm = 256
    npr = dp.shape[1]

    def body(dp_ref, x_ref, gn_ref, w_ref, dxo_ref, dx_ref, dgn_ref, h_ref):
        @pl.when(pl.program_id(0) == 0)
        def _():
            dgn_ref[...] = jnp.zeros_like(dgn_ref)
        dh = jnp.zeros((tm, d), F32)
        for c0 in range(0, npr, 1024):
            c1 = min(c0 + 1024, npr)
            dh = dh + _nt(dp_ref[:, c0:c1], w_ref[:, c0:c1])
        n, r = _rms(x_ref[...])
        gnv = gn_ref[...]
        h_ref[...] = (n * gnv).astype(BF)
        dx, dgn = _rms_bwd(dh, n, r, gnv)
        dx_ref[...] = dxo_ref[...] + dx
        dgn_ref[...] += dgn

    row = pl.BlockSpec((tm, d), lambda i: (i, 0))
    return pl.pallas_call(
        body, name="mix_in_bwd", grid=(t // tm,),
        out_shape=(S((t, d), F32), S((1, d), F32), S((t, d), BF)),
        in_specs=[pl.BlockSpec((tm, npr), lambda i: (i, 0)), row, _resident((1, d)), _resident(w_in.shape), row],
        out_specs=(row, _acc_spec((1, d)), row),
        compiler_params=_params(("arbitrary",)),
    )(dp, x, gn, w_in, dxo)


def _row_tile(rows):
    for tr in (256, 352, 128, 64, 16, 8):
        if rows % tr == 0:
            return tr
    return rows


def _add_bf16(a, b, name):
    rows, cols = a.shape
    tr = _row_tile(rows)

    def body(a_ref, b_ref, o_ref):
        o_ref[...] = (a_ref[...].astype(F32) + b_ref[...].astype(F32)).astype(BF)

    blk = pl.BlockSpec((tr, cols), lambda i: (i, 0))
    return pl.pallas_call(body, name=name, grid=(rows // tr,), out_shape=S((rows, cols), BF),
                          in_specs=[blk, blk], out_specs=blk, compiler_params=_params(("parallel",)))(a, b)


def _adam_update(w, g, m, v):
    mn = ADAM_B1 * m + (1.0 - ADAM_B1) * g
    vn = ADAM_B2 * v + (1.0 - ADAM_B2) * (g * g)
    m_hat = mn / (1.0 - ADAM_B1 ** ADAM_STEP)
    v_hat = vn / (1.0 - ADAM_B2 ** ADAM_STEP)
    return -ADAM_LR * (m_hat / (jnp.sqrt(v_hat) + ADAM_EPS) + ADAM_WD * w), mn, vn


def _adamw(quads, name, steps, echo=False, hook=None):
    n_out = 4 if echo else 3

    def body(*refs):
        ins, outs = refs[:4 * len(quads)], refs[4 * len(quads):]
        for q in range(len(quads)):
            w_ref, g_ref, m_ref, v_ref = ins[4 * q:4 * q + 4]
            res = outs[n_out * q:n_out * q + n_out]
            gv = g_ref[...]
            if echo:
                res[0][...] = gv
            res[-3][...], res[-2][...], res[-1][...] = _adam_update(w_ref[...], gv, m_ref[...], v_ref[...])

    in_specs, out_specs, out_shape, args = [], [], [], []
    for quad in quads:
        rows, cols = quad[0].shape
        blk = pl.BlockSpec((rows // steps, cols), lambda i: (i, 0))
        in_specs += [blk] * 4
        out_specs += [blk] * n_out
        out_shape += [S((rows, cols), F32)] * n_out
        args += list(quad)
    res = _call(body, name=name, grid=(steps,), out_shape=tuple(out_shape), in_specs=in_specs, out_specs=tuple(out_specs),
                sem=("parallel",), args=args, hook=hook)
    return [res[n_out * q:n_out * q + n_out] for q in range(len(quads))]


def _place():
    x, y, c = lax.axis_index("x"), lax.axis_index("y"), lax.axis_index("c")
    return x, y, c


OTHER_CHIPS = ((1, 0), (0, 1), (1, 1))


def _flip(v, f):
    return 1 - v if f else v


def _remote(src, dst, ssem, rsem, dev):
    return pltpu.make_async_remote_copy(src_ref=src, dst_ref=dst, send_sem=ssem, recv_sem=rsem,
                                        device_id=dev, device_id_type=MESH)


class _Both:
    def __init__(self, hooks):
        self.hooks = hooks
        self.ins = [a for h in hooks for a in h.ins]
        self.out_shape = tuple(o for h in hooks for o in h.out_shape)
        self.scratch = [x for h in hooks for x in h.scratch]

    def _each(self, ins, outs, scr):
        i = o = s = 0
        for h in self.hooks:
            ni, no, ns = len(h.ins), len(h.out_shape), len(h.scratch)
            yield h, ins[i:i + ni], outs[o:o + no], scr[s:s + ns]
            i, o, s = i + ni, o + no, s + ns

    def start(self, ins, outs, scr):
        for h, *refs in self._each(ins, outs, scr):
            h.start(*refs)

    def finish(self, ins, outs, scr):
        for h, *refs in self._each(ins, outs, scr):
            h.finish(*refs)

    @property
    def results(self):
        return [h.results for h in self.hooks]

    @results.setter
    def results(self, res):
        for h, _, mine, _ in self._each((), tuple(res), ()):
            h.results = mine


def _exchange(hook, name):
    n_in, n_out = len(hook.ins), len(hook.out_shape)

    def body(*refs):
        ins, outs, scr = refs[:n_in], refs[n_in:n_in + n_out], refs[n_in + n_out:]
        hook.start(ins, outs, scr)
        hook.finish(ins, outs, scr)

    hook.results = pl.pallas_call(
        body, name=name, out_shape=tuple(hook.out_shape), in_specs=[ANY] * n_in, out_specs=(ANY,) * n_out,
        scratch_shapes=list(hook.scratch), compiler_params=_params(),
    )(*hook.ins)


class _GatherHook:
    def __init__(self, shards, kinds):
        self.ins, self.kinds, n = list(shards), list(kinds), len(shards)
        self.out_shape = tuple(
            S((w.shape[0], 4 * w.shape[1]), BF) if k == "col" else S((4,) + w.shape, BF) for w, k in zip(shards, kinds))
        dma = pltpu.SemaphoreType.DMA
        self.scratch = ([dma((n, 3)) for _ in range(5)] + [dma((n,)), dma((n,))]
                        + [pltpu.VMEM((3, w.shape[0] // 2, w.shape[1]), BF) for w in shards]
                        + [pltpu.VMEM(w.shape, BF) for w in shards])

    def _window(self, outs, i, s, half):
        rows, cols = self.ins[i].shape
        rh = rows // 2
        start = pl.multiple_of(half * rh, 16)
        if self.kinds[i] == "col":
            return outs[i].at[pl.ds(start, rh), pl.ds(pl.multiple_of(s * cols, 128), cols)]
        return outs[i].at[s, pl.ds(start, rh), :]

    def _copies(self, ins, outs, scr):
        n = len(ins)
        ssem, rsem, fssem, frsem, ksem, lsem, osem = scr[:7]
        land, own = scr[7:7 + n], scr[7 + n:7 + 2 * n]
        x, y, c = _place()
        sibling = (x, y, 1 - c)
        loads, stores, sends, forwards, keeps, passed = [], [], [], [], [], []
        for i in range(n):
            rows, cols = self.ins[i].shape
            rh = rows // 2
            mine = (outs[i].at[:, pl.ds(pl.multiple_of((2 * x + y) * cols, 128), cols)] if self.kinds[i] == "col"
                    else outs[i].at[2 * x + y])
            loads.append(functools.partial(pltpu.make_async_copy, ins[i], own[i], lsem.at[i]))
            stores.append(functools.partial(pltpu.make_async_copy, own[i], mine, osem.at[i]))
            src = ins[i].at[pl.ds(pl.multiple_of(c * rh, 16), rh), :]
            for j, (fx, fy) in enumerate(OTHER_CHIPS):
                px, py = _flip(x, fx), _flip(y, fy)
                sends.append(functools.partial(_remote, src, land[i].at[j], ssem.at[i, j], rsem.at[i, j], (px, py, c)))
                here = self._window(outs, i, 2 * px + py, c)
                forwards.append(functools.partial(_remote, land[i].at[j], here, fssem.at[i, j], frsem.at[i, j], sibling))
                keeps.append(functools.partial(pltpu.make_async_copy, land[i].at[j], here, ksem.at[i, j]))
                there = self._window(outs, i, 2 * px + py, 1 - c)
                passed.append(functools.partial(_remote, there, there, fssem.at[i, j], frsem.at[i, j], sibling))
        return loads, stores, sends, forwards, keeps, passed

    def start(self, ins, outs, scr):
        loads, _, sends, _, _, _ = self._copies(ins, outs, scr)
        for make in sends + loads:
            make().start()

    def finish(self, ins, outs, scr):
        loads, stores, sends, forwards, keeps, passed = self._copies(ins, outs, scr)
        for load, store in zip(loads, stores):
            load().wait()
            store().start()
        for send, forward, keep in zip(sends, forwards, keeps):
            send().wait_recv()
            forward().start()
            keep().start()
        for make in passed:
            make().wait_recv()
        for make in sends + forwards:
            make().wait_send()
        for make in keeps + stores:
            make().wait()


class _ChipsHook:
    def __init__(self, parts, kinds):
        self.ins, self.kinds, n = list(parts), list(kinds), len(parts)
        self.out_shape = tuple(
            S((4, p.shape[0], p.shape[1] // 4), BF) if k == "col" else S(p.shape, BF) for p, k in zip(parts, kinds))
        dma = pltpu.SemaphoreType.DMA
        self.scratch = ([dma((n, 3)), dma((n, 3)), dma((n,)), dma((n,))]
                        + [pltpu.VMEM(o.shape[1:], BF) for o in self.out_shape])

    def _slab(self, ins, i, s):
        _, rows, cols = self.out_shape[i].shape
        if self.kinds[i] == "col":
            return ins[i].at[:, pl.ds(pl.multiple_of(s * cols, 128), cols)]
        return ins[i].at[s]

    def _copies(self, ins, outs, scr):
        ssem, rsem, lsem, osem = scr[:4]
        own = scr[4:]
        x, y, c = _place()
        loads, stores, sends = [], [], []
        for i in range(len(ins)):
            loads.append(functools.partial(pltpu.make_async_copy, self._slab(ins, i, 2 * x + y), own[i], lsem.at[i]))
            stores.append(functools.partial(pltpu.make_async_copy, own[i], outs[i].at[3], osem.at[i]))
            for j, (fx, fy) in enumerate(OTHER_CHIPS):
                px, py = _flip(x, fx), _flip(y, fy)
                sends.append(functools.partial(_remote, self._slab(ins, i, 2 * px + py), outs[i].at[j], ssem.at[i, j],
                                               rsem.at[i, j], (px, py, c)))
        return loads, stores, sends

    def start(self, ins, outs, scr):
        loads, _, sends = self._copies(ins, outs, scr)
        for make in sends + loads:
            make().start()

    def finish(self, ins, outs, scr):
        loads, stores, sends = self._copies(ins, outs, scr)
        for load, store in zip(loads, stores):
            load().wait()
            store().start()
        for make in sends + stores:
            make().wait()


SHARE_STEPS = 2


def _sum_share(slab_list, name, hook=None):
    n = len(slab_list)
    geom = [(sl.shape[1], sl.shape[1] // SHARE_STEPS, sl.shape[2]) for sl in slab_list]

    def body(*refs):
        ins, outs, scr = refs[:n], refs[n:2 * n], refs[2 * n:]
        i = pl.program_id(0)
        x, y, c = _place()
        sibling = (x, y, 1 - c)

        def copies(q, k):
            rh, tr, _ = geom[q]
            stage, lsem, ssem, rsem = scr[4 * q:4 * q + 4]
            dst = outs[q].at[pl.ds(pl.multiple_of(c * rh + k * tr, 8), tr), :]
            return (pltpu.make_async_copy(stage.at[k], dst, lsem.at[k]),
                    _remote(stage.at[k], dst, ssem.at[k], rsem, sibling))

        for q in range(n):
            acc = ins[q][3].astype(F32)
            for k in range(3):
                acc = acc + ins[q][k].astype(F32)
            scr[4 * q][i] = acc
            for cp in copies(q, i):
                cp.start()

        @pl.when(i == SHARE_STEPS - 1)
        def _():
            for q in range(n):
                rh = geom[q][0]
                for k in range(SHARE_STEPS):
                    local, remote = copies(q, k)
                    local.wait()
                    remote.wait_send()
                got = outs[q].at[pl.ds(pl.multiple_of((1 - c) * rh, 8), rh), :]
                _remote(got, got, scr[4 * q + 2].at[0], scr[4 * q + 3], sibling).wait_recv()

    dma = pltpu.SemaphoreType.DMA
    scratch = []
    for rh, tr, cols in geom:
        scratch += [pltpu.VMEM((SHARE_STEPS, tr, cols), F32), dma((SHARE_STEPS,)), dma((SHARE_STEPS,)), dma]
    return _call(
        body, name=name, grid=(SHARE_STEPS,), out_shape=tuple(S((2 * rh, cols), F32) for rh, _, cols in geom),
        in_specs=[pl.BlockSpec((4, tr, cols), lambda i: (0, i, 0)) for _, tr, cols in geom], out_specs=(ANY,) * n,
        scratch_shapes=scratch, sem=("arbitrary",), args=list(slab_list), hook=hook)


class _SmallSumHook:
    def __init__(self, buf):
        self.ins, self.out_shape = [buf], (S(buf.shape, F32),)
        dma = pltpu.SemaphoreType.DMA
        self.scratch = [pltpu.VMEM((8,) + buf.shape, F32), pltpu.VMEM(buf.shape, F32), dma((7,)), dma((7,)), dma]

    def _sends(self, scr):
        slots, _, ssem, rsem, _ = scr
        x, y, c = _place()
        me = 4 * x + 2 * y + c
        for r in range(1, 8):
            px, py, pc = _flip(x, (r >> 2) & 1), _flip(y, (r >> 1) & 1), _flip(c, r & 1)
            yield (functools.partial(_remote, slots.at[me], slots.at[me], ssem.at[r - 1], rsem.at[r - 1], (px, py, pc)),
                   functools.partial(_remote, slots.at[me], slots.at[4 * px + 2 * py + pc], ssem.at[r - 1],
                                     rsem.at[r - 1], (px, py, pc)))

    def start(self, ins, outs, scr):
        slots, _, _, _, lsem = scr
        x, y, c = _place()
        load = pltpu.make_async_copy(ins[0], slots.at[4 * x + 2 * y + c], lsem)
        load.start()
        load.wait()
        for send, _ in self._sends(scr):
            send().start()

    def finish(self, ins, outs, scr):
        slots, total, _, _, lsem = scr
        for _, arrival in self._sends(scr):
            arrival().wait_recv()
        for send, _ in self._sends(scr):
            send().wait_send()
        acc = slots[0]
        for k in range(1, 8):
            acc = acc + slots[k]
        total[...] = acc
        store = pltpu.make_async_copy(total, outs[0], lsem)
        store.start()
        store.wait()


BIG = ("ffn1_w_gu", "ffn1_w_down", "w_in", "w_out", "xattn_wq", "xattn_wkv", "xattn_wo", "ffn2_w_gu", "ffn2_w_down")
KIND = {"ffn1_w_gu": "col", "ffn1_w_down": "row", "w_in": "col", "w_out": "row", "xattn_wq": "row",
        "xattn_wkv": "col", "xattn_wo": "row", "ffn2_w_gu": "col", "ffn2_w_down": "row"}
WEIGHTS = ("rel_bias", "ffn1_norm", "ffn1_w_gu", "ffn1_w_down", "mix_norm", "w_in", "sinks", "conv_w", "w_out",
           "xattn_norm", "mem_norm", "xattn_wq", "xattn_wkv", "xattn_wo", "ffn2_norm", "ffn2_w_gu", "ffn2_w_down",
           "final_norm")
SMALL_ROWS = 16
GAIN_ROW = {"ffn1_norm": 0, "mix_norm": 1, "xattn_norm": 2, "mem_norm": 3, "ffn2_norm": 4, "final_norm": 5}
CONV_ROW, SINK_ROW, BIAS_ROW, LOSS_ROW = 6, 9, 10, 11


def _rows_block(rows, d):
    buf = jnp.zeros((SMALL_ROWS, d), F32)
    for r, v in rows.items():
        buf = lax.dynamic_update_slice(buf, v.reshape(1, -1).astype(F32), (r, 0))
    return buf


def _local_step(x, mem, pos, target, w, gains, rel_bias, sinks, conv_w, shards=None):
    t, d = x.shape
    dist = shards is not None
    w = dict(w)
    grads, slabs = {}, {}
    pos_col = pos.reshape(t, 1)
    pos_row = pos.reshape(1, t)
    bias_t = _bias_build(rel_bias).reshape(SWA_KV_HEADS, GROUP_ROWS, 2 * BLOCK)

    def gather(names):
        return _GatherHook([shards[k] for k in names], [KIND[k] for k in names]) if dist else None

    def gathered(names, hook):
        if dist:
            for k, gw in zip(names, hook.results):
                w[k] = gw if KIND[k] == "col" else gw.reshape(-1, gw.shape[-1])

    def dw(k, a, b, tn, name, **kw):
        return _dw_pair(a, b, tn, name, KIND[k], **kw) if dist else _dw(a, b, tn, name, **kw)

    def pair_sums(names):
        if not dist:
            return None
        parts = {}
        for k in names:
            mine, sib = grads[k]
            if KIND[k] == "row":
                mine, sib = mine.reshape(-1, mine.shape[-1]), sib.reshape(-1, sib.shape[-1])
            p = _add_bf16(mine, sib, "pair_sum_" + k)
            parts[k] = p if KIND[k] == "col" else p.reshape(4, -1, p.shape[-1])
        return parts

    def chips(names, parts):
        return _ChipsHook([parts[k] for k in names], [KIND[k] for k in names]) if dist else None

    def reduced(names, hook):
        if dist:
            slabs.update(zip(names, hook.results))

    names = ("w_in", "w_out")
    hook = gather(names)
    x1, g1, u1 = _ffn_fwd(x, gains["ffn1_norm"], w["ffn1_w_gu"], w["ffn1_w_down"], "ffn1_fwd", hook)
    gathered(names, hook)
    names = ("xattn_wq", "xattn_wkv", "xattn_wo")
    hook = gather(names)
    qkv, e = _mix_proj(x1, gains["mix_norm"], w["w_in"], hook)
    gathered(names, hook)
    names = ("ffn2_w_gu",)
    hook = gather(names)
    attn = _swa_fwd(qkv, pos_col, pos_row, bias_t, sinks, hook)
    gathered(names, hook)
    names = ("ffn2_w_down",)
    hook = gather(names)
    x2, merged = _mix_out_fwd(e, attn, conv_w, w["w_out"], x1, hook)
    gathered(names, hook)
    mh, kv = _mem_kv(mem, gains["mem_norm"], w["xattn_wkv"])
    x3, qx, o = _xattn_fwd(x2, gains["xattn_norm"], w["xattn_wq"], kv, w["xattn_wo"])
    dx4, g2, u2, d_final, loss = _ffn_fwd(x3, gains["ffn2_norm"], w["ffn2_w_gu"], w["ffn2_w_down"], "ffn2_fwd",
                                          head=(gains["final_norm"], target))

    dx3, d_ffn2, dgu2, a2, h4, dyh4 = _ffn_bwd(dx4, x3, gains["ffn2_norm"], g2, u2, w["ffn2_w_gu"], w["ffn2_w_down"],
                                               "ffn2_bwd")
    grads["ffn2_w_gu"] = dw("ffn2_w_gu", h4, dgu2, 1408, "dw_ffn2_gu")
    grads["ffn2_w_down"] = dw("ffn2_w_down", a2, dyh4, 512, "dw_ffn2_down")
    parts = pair_sums(("ffn2_w_gu", "ffn2_w_down"))
    hook = chips(("ffn2_w_gu",), parts)
    dx2, d_xattn, dkv, dxh3, h3, dqx = _xattn_bwd(dx3, x2, gains["xattn_norm"], qx, kv, w["xattn_wq"], w["xattn_wo"], hook)
    reduced(("ffn2_w_gu",), hook)
    grads["xattn_wo"] = dw("xattn_wo", o, dxh3, 1024, "dw_wo")
    grads["xattn_wq"] = dw("xattn_wq", h3, dqx, 1024, "dw_wq")
    grads["xattn_wkv"], d_mem = _mem_bwd(dkv, mh, mem, gains["mem_norm"], w["xattn_wkv"], dist)
    hook = chips(("ffn2_w_down",), parts)
    dattn, dp, dcw, dxh2 = _mix_out_bwd(dx2, e, attn, conv_w, w["w_out"], qkv.shape[1], hook)
    reduced(("ffn2_w_down",), hook)
    grads["w_out"] = dw("w_out", merged, dxh2, 1024, "dw_wout")
    names = ("xattn_wo", "xattn_wq", "xattn_wkv", "w_out")
    hook = chips(names, pair_sums(names))
    dp, gb, dsk = _swa_bwd(qkv, dattn, pos_col, pos_row, bias_t, sinks, dp, hook)
    reduced(names, hook)
    d_rel_bias, d_sinks = _bias_reduce(gb.reshape(SWA_HEADS, BLOCK, 2 * BLOCK), dsk.reshape(SWA_HEADS, BLOCK, 1))
    dx1, d_mix, h2 = _mix_in_bwd(dp, x1, gains["mix_norm"], w["w_in"], dx2)
    grads["w_in"] = dw("w_in", h2, dp, w["w_in"].shape[1] // 4, "dw_win")
    names = ("w_in",)
    hook = chips(names, pair_sums(names))
    dx0, d_ffn1, dgu1, a1, h1, dyh1 = _ffn_bwd(dx1, x, gains["ffn1_norm"], g1, u1, w["ffn1_w_gu"], w["ffn1_w_down"],
                                               "ffn1_bwd", hook)
    reduced(names, hook)
    grads["ffn1_w_down"] = dw("ffn1_w_down", a1, dyh1, 512, "dw_ffn1_down")
    names = ("ffn1_w_down",)
    hook = chips(names, pair_sums(names))
    grads["ffn1_w_gu"] = dw("ffn1_w_gu", h1, dgu1, 1408, "dw_ffn1_gu", **({"hook": hook} if dist else {}))
    reduced(names, hook)
    rows = {0: d_ffn1, 1: d_mix, 2: d_xattn, 3: d_mem, 4: d_ffn2, 5: d_final, SINK_ROW: d_sinks, BIAS_ROW: d_rel_bias,
            LOSS_ROW: loss[0, 0:1]}
    rows.update({CONV_ROW + j: dcw[j] for j in range(3)})
    last = ("ffn1_w_gu",)
    return dx0, (slabs if dist else grads), _rows_block(rows, d), chips(last, pair_sums(last))


def kernel(x, mem, positions, rel_bias, ffn1_norm, ffn1_w_gu, ffn1_w_down, mix_norm, w_in, sinks, conv_w, w_out, xattn_norm, mem_norm, xattn_wq, xattn_wkv, xattn_wo, ffn2_norm, ffn2_w_gu, ffn2_w_down, final_norm, loss_target, m_rel_bias, m_ffn1_norm, m_ffn1_w_gu, m_ffn1_w_down, m_mix_norm, m_w_in, m_sinks, m_conv_w, m_w_out, m_xattn_norm, m_mem_norm, m_xattn_wq, m_xattn_wkv, m_xattn_wo, m_ffn2_norm, m_ffn2_w_gu, m_ffn2_w_down, m_final_norm, v_rel_bias, v_ffn1_norm, v_ffn1_w_gu, v_ffn1_w_down, v_mix_norm, v_w_in, v_sinks, v_conv_w, v_w_out, v_xattn_norm, v_mem_norm, v_xattn_wq, v_xattn_wkv, v_xattn_wo, v_ffn2_norm, v_ffn2_w_gu, v_ffn2_w_down, v_final_norm):
    args = dict(locals())
    wts = {k: args[k] for k in WEIGHTS}
    mom = {k: args["m_" + k] for k in WEIGHTS}
    var = {k: args["v_" + k] for k in WEIGHTS}
    d = x.shape[-1]
    s_me = 2 * lax.axis_index("x") + lax.axis_index("y")

    shards = {k: wts[k][0].astype(BF) for k in BIG}
    first = ("ffn1_w_gu", "ffn1_w_down")
    cw_cols = conv_w.shape[-1]
    placed = lax.dynamic_update_slice(jnp.zeros((SMALL_ROWS, d), F32), 0.5 * conv_w[0], (0, s_me * cw_cols))
    head = _Both([_GatherHook([shards[k] for k in first], [KIND[k] for k in first]), _SmallSumHook(placed)])
    _exchange(head, "gather_ffn1")
    gathered, (conv_sum,) = head.results
    whole = {k: (gw if KIND[k] == "col" else gw.reshape(-1, gw.shape[-1])) for k, gw in zip(first, gathered)}
    conv_whole = conv_sum[0:3]

    gains = {k: wts[k].reshape(1, d) for k in GAIN_ROW}
    dx0, slabs, small, last_chips = _local_step(x[0], mem[0], positions[0], loss_target[0], whole, gains, rel_bias, sinks,
                                               conv_whole, shards)

    late = ("ffn1_w_gu", "ffn1_w_down", "ffn2_w_down")
    early = tuple(k for k in BIG if k not in late)
    first_ones = tuple(k for k in BIG if k != "ffn1_w_gu")
    tail = _Both([last_chips, _SmallSumHook(small)])
    shard_grads = dict(zip(first_ones, _sum_share([slabs[k] for k in first_ones], "sum_share", tail)))
    (last_slabs,), (small_sum,) = tail.results
    shard_grads["ffn1_w_gu"], = _sum_share([last_slabs], "sum_share_ffn1_w_gu")
    quad = lambda k: (wts[k][0], shard_grads[k], mom[k][0], var[k][0])
    updates = dict(zip(early, _adamw([quad(k) for k in early], "adamw_early", 16, echo=True)))
    updates.update(zip(late, _adamw([quad(k) for k in late], "adamw_late", 8, echo=True)))
    loss = small_sum[LOSS_ROW, 0]

    out_g, out_d, out_m, out_v = {}, {}, {}, {}
    for k in BIG:
        out_g[k], out_d[k], out_m[k], out_v[k] = (a[None] for a in updates[k])

    conv_g = lax.dynamic_slice(small_sum, (CONV_ROW, s_me * cw_cols), (3, cw_cols))

    def pack(src, conv_block):
        rows = {r: src[k] for k, r in GAIN_ROW.items()}
        rows.update({CONV_ROW + j: conv_block[j] for j in range(3)})
        rows[SINK_ROW], rows[BIAS_ROW] = src["sinks"], src["rel_bias"]
        return _rows_block(rows, d)

    small_g = {k: small_sum[r] for k, r in GAIN_ROW.items()}
    small_g["sinks"] = small_sum[SINK_ROW, 0:sinks.size]
    small_g["rel_bias"] = small_sum[BIAS_ROW, 0:rel_bias.size]
    gp = pack(small_g, conv_g)
    (dl, mn, vn), = _adamw([(pack(wts, conv_w[0]), gp, pack(mom, m_conv_w[0]), pack(var, v_conv_w[0]))], "adamw_small", 1)

    def unpack(buf, k):
        if k in GAIN_ROW:
            return buf[GAIN_ROW[k]].reshape(wts[k].shape)
        if k == "conv_w":
            return buf[CONV_ROW:CONV_ROW + 3, 0:cw_cols][None]
        if k == "sinks":
            return buf[SINK_ROW, 0:sinks.size].reshape(sinks.shape)
        return buf[BIAS_ROW, 0:rel_bias.size].reshape(rel_bias.shape)

    for k in WEIGHTS:
        if k not in KIND:
            out_g[k], out_d[k], out_m[k], out_v[k] = unpack(gp, k), unpack(dl, k), unpack(mn, k), unpack(vn, k)

    return (loss, dx0[None], *[out_g[k] for k in WEIGHTS], *[out_d[k] for k in WEIGHTS],
            *[out_m[k] for k in WEIGHTS], *[out_v[k] for k in WEIGHTS])
```

```python
import functools
import math

import jax
import jax.numpy as jnp
from jax import lax
from jax.experimental import pallas as pl
from jax.experimental.pallas import tpu as pltpu

BF = jnp.bfloat16
F32 = jnp.float32
I32 = jnp.int32
S = jax.ShapeDtypeStruct

EPS = 1e-6
NEG = -1e30
POS_PAD = 1 << 30
WINDOW = 128
BLOCK = 128
HEAD_DIM = 64
SWA_HEADS = 16
SWA_KV_HEADS = 4
SWA_GROUP = SWA_HEADS // SWA_KV_HEADS
MEM_HEADS = 4
REL_BUCKETS = 32
REL_MAX_DIST = 128
ADAM_LR = 0.001
ADAM_B1 = 0.9
ADAM_B2 = 0.999
ADAM_EPS = 1e-08
ADAM_WD = 0.01
ADAM_STEP = 10

V7X_VMEM_LIMIT_BYTES = 56 * 1024 * 1024
MESH = pl.DeviceIdType.MESH
ANY = pl.BlockSpec(memory_space=pl.ANY)
VMEM_SPEC = pl.BlockSpec(memory_space=pltpu.VMEM)
SMEM_SPEC = pl.BlockSpec(memory_space=pltpu.SMEM)


def _params(sem=None):
    return pltpu.CompilerParams(dimension_semantics=sem, vmem_limit_bytes=V7X_VMEM_LIMIT_BYTES)


def _resident(shape):
    nd = len(shape)
    return pl.BlockSpec(shape, lambda *_: (0,) * nd, pipeline_mode=pl.Buffered(1))


def _acc_spec(shape):
    nd = len(shape)
    return pl.BlockSpec(shape, lambda *_: (0,) * nd)


def _call(body, *, name, grid, out_shape, in_specs, out_specs, args, sem, scratch_shapes=(), aliases=None, hook=None):
    aliases = aliases or {}
    if hook is None:
        return pl.pallas_call(body, name=name, grid=grid, out_shape=out_shape, in_specs=in_specs, out_specs=out_specs,
                              scratch_shapes=list(scratch_shapes), input_output_aliases=aliases,
                              compiler_params=_params(sem))(*args)
    n_in, n_out, n_scr = len(in_specs), len(out_shape), len(scratch_shapes)
    h_in, h_out = len(hook.ins), len(hook.out_shape)

    def at_step(pick):
        conds = [pl.program_id(ax) == pick(size) for ax, size in enumerate(grid)]
        return functools.reduce(jnp.logical_and, conds)

    def hosted(*refs):
        k_in, x_in = refs[:n_in], refs[n_in:n_in + h_in]
        o0 = n_in + h_in
        k_out, x_out = refs[o0:o0 + n_out], refs[o0 + n_out:o0 + n_out + h_out]
        s0 = o0 + n_out + h_out
        k_scr, x_scr = refs[s0:s0 + n_scr], refs[s0 + n_scr:]

        @pl.when(at_step(lambda size: 0))
        def _():
            hook.start(x_in, x_out, x_scr)
        body(*k_in, *k_out, *k_scr)

        @pl.when(at_step(lambda size: size - 1))
        def _():
            hook.finish(x_in, x_out, x_scr)

    res = pl.pallas_call(
        hosted, name=name, grid=grid, out_shape=tuple(out_shape) + tuple(hook.out_shape),
        in_specs=list(in_specs) + [ANY] * h_in, out_specs=tuple(out_specs) + (ANY,) * h_out,
        scratch_shapes=list(scratch_shapes) + list(hook.scratch), input_output_aliases=aliases,
        compiler_params=_params(("arbitrary",) * len(grid)),
    )(*args, *hook.ins)
    hook.results = res[n_out:]
    return res[:n_out]


def _nn(a, b):
    return jnp.dot(a, b, preferred_element_type=F32)


def _nt(a, b):
    return lax.dot_general(a, b, (((1,), (1,)), ((), ())), preferred_element_type=F32)


def _tn(a, b):
    return lax.dot_general(a, b, (((0,), (0,)), ((), ())), preferred_element_type=F32)


def _sigmoid(v):
    return 1.0 / (1.0 + jnp.exp(-v))


def _rms(x):
    r = lax.rsqrt(jnp.mean(x * x, axis=-1, keepdims=True) + EPS)
    return x * r, r


def _rms_bwd(dh, n, r, g):
    dn = dh * g
    dx = r * (dn - n * jnp.mean(dn * n, axis=-1, keepdims=True))
    return dx, jnp.sum(dh * n, axis=0, keepdims=True)


def _ffn_fwd(x, gn, wgu, wd, name, hook=None, head=None):
    t, d = x.shape
    f = wd.shape[0]
    tm, fc = 256, 1408

    def body(x_ref, gn_ref, wgu_ref, wd_ref, *rest):
        xv = x_ref[...]
        n, _ = _rms(xv)
        h = (n * gn_ref[...]).astype(BF)
        g_ref, u_ref = rest[-2:] if head is None else rest[3:5]
        acc = jnp.zeros((tm, d), F32)
        for c0 in range(0, f, fc):
            g = _nn(h, wgu_ref[:, c0:c0 + fc])
            u = _nn(h, wgu_ref[:, f + c0:f + c0 + fc])
            g_ref[:, c0:c0 + fc] = g.astype(BF)
            u_ref[:, c0:c0 + fc] = u.astype(BF)
            a = (g * _sigmoid(g)) * u
            acc = acc + _nn(a.astype(BF), wd_ref[c0:c0 + fc, :])
        y = xv + 0.5 * acc
        if head is None:
            rest[0][...] = y
            return
        gf_ref, t_ref, dy_ref, _, _, dgf_ref, loss_ref = rest

        @pl.when(pl.program_id(0) == 0)
        def _():
            dgf_ref[...] = jnp.zeros_like(dgf_ref)
            loss_ref[...] = jnp.zeros_like(loss_ref)
        ny, ry = _rms(y)
        gf = gf_ref[...]
        err = ny * gf - t_ref[...]
        loss_ref[...] += 0.5 * jnp.sum(jnp.sum(err * err, axis=-1, keepdims=True) / d, axis=0, keepdims=True)
        dy, dgf = _rms_bwd(err / d, ny, ry, gf)
        dy_ref[...] = dy
        dgf_ref[...] += dgf

    row = pl.BlockSpec((tm, d), lambda i: (i, 0))
    frow = pl.BlockSpec((tm, f), lambda i: (i, 0))
    in_specs = [row, _resident((1, d)), _resident(wgu.shape), _resident(wd.shape)]
    out_shape = (S((t, d), F32), S((t, f), BF), S((t, f), BF))
    out_specs = (row, frow, frow)
    args = (x, gn, wgu, wd)
    if head is not None:
        in_specs += [_resident((1, d)), row]
        out_shape += (S((1, d), F32), S((1, 128), F32))
        out_specs += (_acc_spec((1, d)), _acc_spec((1, 128)))
        args += tuple(head)
    return _call(body, name=name, grid=(t // tm,), out_shape=out_shape, in_specs=in_specs, out_specs=out_specs,
                 sem=("parallel",) if head is None else ("arbitrary",), args=args, hook=hook)


def _mix_proj(x, gn, w_in, hook=None):
    t, d = x.shape
    tm = 256
    nqkv = 1536
    ne = w_in.shape[1] - nqkv

    def body(x_ref, gn_ref, w_ref, qkv_ref, e_ref):
        n, _ = _rms(x_ref[...])
        h = (n * gn_ref[...]).astype(BF)
        qkv_ref[...] = _nn(h, w_ref[:, 0:nqkv]).astype(BF)
        for c0 in range(0, ne, 1024):
            e_ref[:, c0:c0 + 1024] = _nn(h, w_ref[:, nqkv + c0:nqkv + c0 + 1024]).astype(BF)

    return _call(
        body, name="mix_proj", grid=(t // tm,),
        out_shape=(S((t, nqkv), BF), S((t, ne), BF)),
        in_specs=[pl.BlockSpec((tm, d), lambda i: (i, 0)), _resident((1, d)), _resident(w_in.shape)],
        out_specs=(pl.BlockSpec((tm, nqkv), lambda i: (i, 0)), pl.BlockSpec((tm, ne), lambda i: (i, 0))),
        sem=("parallel",), args=(x, gn, w_in), hook=hook)


def _t5_bucket(rel):
    n = jnp.maximum(rel, 0)
    max_exact = REL_BUCKETS // 2
    nf = jnp.maximum(n, 1).astype(F32)
    large = max_exact + (jnp.log(nf / max_exact) / math.log(REL_MAX_DIST / max_exact)
                         * (REL_BUCKETS - max_exact)).astype(I32)
    large = jnp.minimum(large, REL_BUCKETS - 1)
    return jnp.where(n < max_exact, n, large)


def _block_rel():
    i = lax.broadcasted_iota(I32, (BLOCK, 2 * BLOCK), 0)
    j = lax.broadcasted_iota(I32, (BLOCK, 2 * BLOCK), 1)
    return i + BLOCK - j


def _bias_build(rel_bias):
    def body(rb_ref, o_ref):
        bucket = _t5_bucket(_block_rel())
        for h in range(SWA_HEADS):
            acc = jnp.zeros((BLOCK, 2 * BLOCK), F32)
            for b in range(REL_BUCKETS):
                acc = jnp.where(bucket == b, rb_ref[b, h], acc)
            o_ref[h] = acc

    return pl.pallas_call(
        body, name="bias_build", out_shape=S((SWA_HEADS, BLOCK, 2 * BLOCK), F32),
        in_specs=[SMEM_SPEC], out_specs=VMEM_SPEC,
    )(rel_bias)


GROUP_ROWS = SWA_GROUP * BLOCK


def _swa_visible(b, pq_ref, pkp_ref, pkc_ref):
    pk = jnp.concatenate([pkp_ref[...], pkc_ref[...]], axis=1)
    col = lax.broadcasted_iota(I32, (1, 2 * BLOCK), 1)
    pk = jnp.where(jnp.logical_and(b == 0, col < BLOCK), POS_PAD, pk)
    rel = jnp.concatenate([pq_ref[...]] * SWA_GROUP, axis=0) - pk
    return jnp.logical_and(rel >= 0, rel < WINDOW)


def _group_heads(ref, hk):
    h0 = hk * SWA_GROUP
    return jnp.concatenate([ref[:, (h0 + g) * HEAD_DIM:(h0 + g + 1) * HEAD_DIM] for g in range(SWA_GROUP)], axis=0)


def _group_sinks(sink_ref, hk):
    row = lax.broadcasted_iota(I32, (GROUP_ROWS, 1), 0)
    col = jnp.zeros((GROUP_ROWS, 1), F32) + sink_ref[0, hk * SWA_GROUP]
    for g in range(1, SWA_GROUP):
        col = jnp.where(row >= g * BLOCK, sink_ref[0, hk * SWA_GROUP + g], col)
    return col


def _swa_probs(qg, kh, vis, bias, sink):
    s = _nt(qg, kh) * (HEAD_DIM ** -0.5)
    s = jnp.where(vis, s + bias, NEG)
    m = jnp.maximum(jnp.max(s, axis=-1, keepdims=True), sink)
    p = jnp.exp(s - m)
    ps = jnp.exp(sink - m)
    inv = 1.0 / (jnp.sum(p, axis=-1, keepdims=True) + ps)
    return p * inv, ps * inv


def _swa_fwd(qkv, pos_col, pos_row, bias_t, sinks, hook=None):
    t = qkv.shape[0]
    nb = t // BLOCK
    qw = SWA_HEADS * HEAD_DIM
    kw = SWA_KV_HEADS * HEAD_DIM

    def body(q_ref, kp_ref, kc_ref, vp_ref, vc_ref, pq_ref, pkp_ref, pkc_ref, bias_ref, sink_ref, o_ref):
        b = pl.program_id(0)
        vis = _swa_visible(b, pq_ref, pkp_ref, pkc_ref)
        k2 = jnp.concatenate([kp_ref[...], kc_ref[...]], axis=0)
        v2 = jnp.concatenate([vp_ref[...], vc_ref[...]], axis=0)
        for hk in range(SWA_KV_HEADS):
            kh = k2[:, hk * HEAD_DIM:(hk + 1) * HEAD_DIM]
            vh = v2[:, hk * HEAD_DIM:(hk + 1) * HEAD_DIM]
            pn, _ = _swa_probs(_group_heads(q_ref, hk), kh, vis, bias_ref[hk], _group_sinks(sink_ref, hk))
            o = _nn(pn.astype(BF), vh)
            for g in range(SWA_GROUP):
                h = hk * SWA_GROUP + g
                o_ref[:, h * HEAD_DIM:(h + 1) * HEAD_DIM] = o[g * BLOCK:(g + 1) * BLOCK]

    prev = lambda b: jnp.maximum(b - 1, 0)
    return _call(
        body, name="swa_fwd", grid=(nb,), out_shape=(S((t, qw), F32),),
        in_specs=[
            pl.BlockSpec((BLOCK, qw), lambda b: (b, 0)),
            pl.BlockSpec((BLOCK, kw), lambda b: (prev(b), qw // kw)),
            pl.BlockSpec((BLOCK, kw), lambda b: (b, qw // kw)),
            pl.BlockSpec((BLOCK, kw), lambda b: (prev(b), qw // kw + 1)),
            pl.BlockSpec((BLOCK, kw), lambda b: (b, qw // kw + 1)),
            pl.BlockSpec((BLOCK, 1), lambda b: (b, 0)),
            pl.BlockSpec((1, BLOCK), lambda b: (0, prev(b))),
            pl.BlockSpec((1, BLOCK), lambda b: (0, b)),
            _resident(bias_t.shape),
            SMEM_SPEC,
        ],
        out_specs=(pl.BlockSpec((BLOCK, qw), lambda b: (b, 0)),),
        sem=("parallel",), args=(qkv, qkv, qkv, qkv, qkv, pos_col, pos_row, pos_row, bias_t, sinks), hook=hook)[0]


HALO = 16


def _conv_taps(z, zh, first):
    tm = z.shape[0]
    zh = jnp.where(first, 0.0, zh)
    row = lax.broadcasted_iota(I32, (tm, 1), 0)
    z1 = jnp.where(row == 0, zh[HALO - 1:HALO, :], pltpu.roll(z, 1, 0))
    z2 = jnp.where(row == 0, zh[HALO - 2:HALO - 1, :], jnp.where(row == 1, zh[HALO - 1:HALO, :], pltpu.roll(z, 2, 0)))
    return z1, z2


def _mix_out_fwd(e, attn, conv_w, w_out, x, hook=None):
    t, d = x.shape
    tm = 256
    hb = tm // HALO
    f32 = lambda ref: ref[...].astype(F32)

    def body(c_ref, b_ref, u_ref, ga_ref, gc_ref, ch_ref, uh_ref, attn_ref, cw_ref, w_ref, x_ref, xo_ref, mg_ref):
        i = pl.program_id(0)
        z = f32(c_ref) * f32(u_ref)
        z1, z2 = _conv_taps(z, f32(ch_ref) * f32(uh_ref), i == 0)
        s = cw_ref[0:1, :] * z2 + cw_ref[1:2, :] * z1 + cw_ref[2:3, :] * z
        conv = f32(b_ref) * s
        merged = (_sigmoid(f32(ga_ref)) * attn_ref[...] + _sigmoid(f32(gc_ref)) * conv).astype(BF)
        mg_ref[...] = merged
        xo_ref[...] = x_ref[...] + _nn(merged, w_ref[...])

    ecol = lambda cb: pl.BlockSpec((tm, d), lambda i: (i, cb))
    halo = lambda cb: pl.BlockSpec((HALO, d), lambda i: (jnp.maximum(i * hb - 1, 0), cb))
    row = pl.BlockSpec((tm, d), lambda i: (i, 0))
    return _call(
        body, name="mix_out_fwd", grid=(t // tm,),
        out_shape=(S((t, d), F32), S((t, d), BF)),
        in_specs=[ecol(0), ecol(1), ecol(2), ecol(3), ecol(4), halo(0), halo(2), row,
                  _resident(conv_w.shape), _resident(w_out.shape), row],
        out_specs=(row, row),
        sem=("parallel",), args=(e, e, e, e, e, e, e, attn, conv_w, w_out, x), hook=hook)


def _mem_kv(mem, gm, wkv):
    m, d = mem.shape

    def body(mem_ref, gm_ref, w_ref, mh_ref, kv_ref):
        n, _ = _rms(mem_ref[...])
        mh = (n * gm_ref[...]).astype(BF)
        mh_ref[...] = mh
        kv_ref[...] = _nn(mh, w_ref[...]).astype(BF)

    return pl.pallas_call(
        body, name="mem_kv", out_shape=(S((m, d), BF), S((m, wkv.shape[1]), BF)),
        compiler_params=_params(),
    )(mem, gm, wkv)


def _xattn_probs(qh, kh):
    s = _nt(qh, kh) * (kh.shape[1] ** -0.5)
    p = jnp.exp(s - jnp.max(s, axis=-1, keepdims=True))
    return p * (1.0 / jnp.sum(p, axis=-1, keepdims=True))


def _xattn_fwd(x, gn, wq, kv, wo):
    t, d = x.shape
    tm = 256
    hd = d // MEM_HEADS

    def body(x_ref, gn_ref, wq_ref, kv_ref, wo_ref, xo_ref, q_ref, o_ref):
        xv = x_ref[...]
        n, _ = _rms(xv)
        q = _nn((n * gn_ref[...]).astype(BF), wq_ref[...]).astype(BF)
        q_ref[...] = q
        outs = []
        for hh in range(MEM_HEADS):
            p = _xattn_probs(q[:, hh * hd:(hh + 1) * hd], kv_ref[:, hh * hd:(hh + 1) * hd])
            outs.append(_nn(p.astype(BF), kv_ref[:, d + hh * hd:d + (hh + 1) * hd]))
        o = jnp.concatenate(outs, axis=1).astype(BF)
        o_ref[...] = o
        xo_ref[...] = xv + _nn(o, wo_ref[...])

    row = pl.BlockSpec((tm, d), lambda i: (i, 0))
    return pl.pallas_call(
        body, name="xattn_fwd", grid=(t // tm,),
        out_shape=(S((t, d), F32), S((t, d), BF), S((t, d), BF)),
        in_specs=[row, _resident((1, d)), _resident(wq.shape), _resident(kv.shape), _resident(wo.shape)],
        out_specs=(row, row, row),
        compiler_params=_params(("parallel",)),
    )(x, gn, wq, kv, wo)


def _ffn_bwd(dxo, x, gn, g, u, wgu, wd, name, hook=None):
    t, d = x.shape
    f = wd.shape[0]
    tm, fc = 256, 1408

    def body(dxo_ref, x_ref, gn_ref, g_ref, u_ref, wgu_ref, wd_ref, dx_ref, dgn_ref, dgu_ref, a_ref, h_ref, dyh_ref):
        @pl.when(pl.program_id(0) == 0)
        def _():
            dgn_ref[...] = jnp.zeros_like(dgn_ref)
        dxov = dxo_ref[...]
        dyh = (0.5 * dxov).astype(BF)
        dyh_ref[...] = dyh
        n, r = _rms(x_ref[...])
        gnv = gn_ref[...]
        h_ref[...] = (n * gnv).astype(BF)
        dh = jnp.zeros((tm, d), F32)
        for c0 in range(0, f, fc):
            gv = g_ref[:, c0:c0 + fc].astype(F32)
            uv = u_ref[:, c0:c0 + fc].astype(F32)
            da = _nt(dyh, wd_ref[c0:c0 + fc, :])
            sg = _sigmoid(gv)
            silu = gv * sg
            a_ref[:, c0:c0 + fc] = (silu * uv).astype(BF)
            dg = (da * uv * (sg * (1.0 + gv * (1.0 - sg)))).astype(BF)
            du = (da * silu).astype(BF)
            dgu_ref[:, c0:c0 + fc] = dg
            dgu_ref[:, f + c0:f + c0 + fc] = du
            dh = dh + _nt(dg, wgu_ref[:, c0:c0 + fc]) + _nt(du, wgu_ref[:, f + c0:f + c0 + fc])
        dx, dgn = _rms_bwd(dh, n, r, gnv)
        dx_ref[...] = dxov + dx
        dgn_ref[...] += dgn

    row = pl.BlockSpec((tm, d), lambda i: (i, 0))
    frow = pl.BlockSpec((tm, f), lambda i: (i, 0))
    return _call(
        body, name=name, grid=(t // tm,),
        out_shape=(S((t, d), F32), S((1, d), F32), S((t, 2 * f), BF), S((t, f), BF), S((t, d), BF), S((t, d), BF)),
        in_specs=[row, row, _resident((1, d)), frow, frow, _resident(wgu.shape), _resident(wd.shape)],
        out_specs=(row, _acc_spec((1, d)), pl.BlockSpec((tm, 2 * f), lambda i: (i, 0)), frow, row, row),
        sem=("arbitrary",), args=(dxo, x, gn, g, u, wgu, wd), hook=hook)


def _dw(a, b, tn, name):
    t, ka = a.shape
    nb = b.shape[1]
    tt = 1024
    nt = t // tt

    def body(a_ref, b_ref, o_ref, acc_ref):
        k = pl.program_id(1)

        @pl.when(k == 0)
        def _():
            acc_ref[...] = jnp.zeros_like(acc_ref)
        acc_ref[...] += _tn(a_ref[...], b_ref[...])

        @pl.when(k == nt - 1)
        def _():
            o_ref[...] = acc_ref[...].astype(BF)

    return pl.pallas_call(
        body, name=name, grid=(nb // tn, nt), out_shape=S((ka, nb), BF),
        in_specs=[pl.BlockSpec((tt, ka), lambda j, k: (k, 0)), pl.BlockSpec((tt, tn), lambda j, k: (k, j))],
        out_specs=pl.BlockSpec((ka, tn), lambda j, k: (0, j)), scratch_shapes=[pltpu.VMEM((ka, tn), F32)],
        compiler_params=_params(("parallel", "arbitrary")),
    )(a, b)


def _dw_pair(pairs, tn, name, kind, hook=None):
    npairs = len(pairs)
    t, ka = pairs[0][0].shape
    nb = pairs[0][1].shape[1]
    tt = 2048 if npairs == 1 else 1024
    nt, nj = t // tt, nb // tn
    col = kind == "col"
    rh = ka // 2 if col else ka // 8
    tile = (rh, tn) if col else (4, rh, tn)
    half = (rh, nb) if col else (4, rh, nb)
    lead = (slice(None),) * (len(tile) - 1)

    def body(*refs):
        ins, mines, sibs = refs[:2 * npairs], refs[2 * npairs:3 * npairs], refs[3 * npairs:4 * npairs]
        acc_ref, stage, ssem, rsem = refs[4 * npairs:]
        j, k = pl.program_id(0), pl.program_id(1)
        x, y, c = _place()
        sibling = (x, y, 1 - c)

        def send(p, slot, jj):
            dst = sibs[p].at[lead + (pl.ds(pl.multiple_of(jj * tn, 128), tn),)]
            return _remote(stage.at[slot], dst, ssem.at[slot], rsem.at[p], sibling)

        def rows(s, whose):
            return acc_ref[pl.ds(pl.multiple_of(s * 2 * rh + whose * rh, 16), rh), :].astype(BF)

        def step(p):
            a_ref, b_ref, mine_ref = ins[2 * p], ins[2 * p + 1], mines[p]

            @pl.when(k == 0)
            def _():
                acc_ref[...] = jnp.zeros_like(acc_ref)
            acc_ref[...] += _tn(a_ref[...], b_ref[...])

            @pl.when(k == nt - 1)
            def _():
                slot = j % 2

                @pl.when(j >= 2)
                def _():
                    send(p, slot, 0).wait_send()
                if col:
                    mine_ref[...] = rows(0, c)
                    stage[slot] = rows(0, 1 - c)
                else:
                    for s in range(4):
                        mine_ref[s] = rows(s, c)
                        stage[slot, s] = rows(s, 1 - c)
                send(p, slot, j - p * nj).start()

        for p in range(npairs):
            pl.when(j // nj == p)(functools.partial(step, p))

        @pl.when(jnp.logical_and(j == npairs * nj - 1, k == nt - 1))
        def _():
            for jj in range(max(npairs * nj - 2, 0), npairs * nj):
                send(0, jj % 2, 0).wait_send()
            for p in range(npairs):
                _remote(sibs[p], sibs[p], ssem.at[0], rsem.at[p], sibling).wait_recv()

    in_specs, args = [], []
    for p, (a, b) in enumerate(pairs):
        on = lambda j, p=p: j // nj == p
        in_specs += [pl.BlockSpec((tt, ka), lambda j, k, on=on: (jnp.where(on(j), k, 0), 0)),
                     pl.BlockSpec((tt, tn), lambda j, k, on=on, p=p: (jnp.where(on(j), k, 0), jnp.clip(j - p * nj, 0, nj - 1)))]
        args += [a, b]
    mine_spec = lambda p: pl.BlockSpec(tile, (lambda j, k: (0, jnp.clip(j - p * nj, 0, nj - 1))) if col
                                       else (lambda j, k: (0, 0, jnp.clip(j - p * nj, 0, nj - 1))))
    res = _call(
        body, name=name, grid=(npairs * nj, nt), out_shape=(S(half, BF),) * (2 * npairs),
        in_specs=in_specs, out_specs=tuple(mine_spec(p) for p in range(npairs)) + (ANY,) * npairs,
        scratch_shapes=[pltpu.VMEM((ka, tn), F32), pltpu.VMEM((2,) + tile, BF), pltpu.SemaphoreType.DMA((2,)),
                        pltpu.SemaphoreType.DMA((npairs,))],
        sem=("arbitrary", "arbitrary"), args=args, hook=hook)
    return [(res[p], res[npairs + p]) for p in range(npairs)]


def _xattn_bwd(dxo, x, gn, q, kv, wq, wo, hook=None):
    t, d = x.shape
    tm = 256
    hd = d // MEM_HEADS
    nkv = kv.shape[0]

    def body(dxo_ref, x_ref, gn_ref, q_ref, kv_ref, wq_ref, wo_ref, dx_ref, dgn_ref, dkv_ref, dxh_ref, h_ref, dq_ref):
        @pl.when(pl.program_id(0) == 0)
        def _():
            dgn_ref[...] = jnp.zeros_like(dgn_ref)
            dkv_ref[...] = jnp.zeros_like(dkv_ref)
        dxov = dxo_ref[...]
        dxh = dxov.astype(BF)
        dxh_ref[...] = dxh
        do = _nt(dxh, wo_ref[...]).astype(BF)
        dqs = []
        for hh in range(MEM_HEADS):
            lo, hi = hh * hd, (hh + 1) * hd
            qh = q_ref[:, lo:hi]
            kh = kv_ref[:, lo:hi]
            vh = kv_ref[:, d + lo:d + hi]
            doh = do[:, lo:hi]
            p = _xattn_probs(qh, kh)
            dp = _nt(doh, vh)
            ds = (p * (dp - jnp.sum(p * dp, axis=-1, keepdims=True)) * (hd ** -0.5)).astype(BF)
            dqs.append(_nn(ds, kh))
            dkv_ref[:, lo:hi] += _tn(ds, qh)
            dkv_ref[:, d + lo:d + hi] += _tn(p.astype(BF), doh)
        dq = jnp.concatenate(dqs, axis=1).astype(BF)
        dq_ref[...] = dq
        n, r = _rms(x_ref[...])
        gnv = gn_ref[...]
        h_ref[...] = (n * gnv).astype(BF)
        dx, dgn = _rms_bwd(_nt(dq, wq_ref[...]), n, r, gnv)
        dx_ref[...] = dxov + dx
        dgn_ref[...] += dgn

    row = pl.BlockSpec((tm, d), lambda i: (i, 0))
    return _call(
        body, name="xattn_bwd", grid=(t // tm,),
        out_shape=(S((t, d), F32), S((1, d), F32), S((nkv, 2 * d), F32), S((t, d), BF), S((t, d), BF), S((t, d), BF)),
        in_specs=[row, row, _resident((1, d)), row, _resident(kv.shape), _resident(wq.shape), _resident(wo.shape)],
        out_specs=(row, _acc_spec((1, d)), _acc_spec((nkv, 2 * d)), row, row, row),
        sem=("arbitrary",), args=(dxo, x, gn, q, kv, wq, wo), hook=hook)


def _mem_bwd(dkv, mh, mem, gm, wkv, pair=False):
    m, d = mem.shape
    rows, cols = wkv.shape
    rh = rows // 2

    def body(dkv_ref, mh_ref, mem_ref, gm_ref, w_ref, *outs):
        dkvb = dkv_ref[...].astype(BF)
        dw = _tn(mh_ref[...], dkvb).astype(BF)
        dmh = _nt(dkvb, w_ref[...])
        n, _ = _rms(mem_ref[...])
        if not pair:
            dw_ref, dgm_ref = outs
            dw_ref[...] = dw
        else:
            mine_ref, sib_ref, dgm_ref, whole, ssem, rsem = outs
            x, y, c = _place()
            whole[...] = dw
            cp = _remote(whole.at[pl.ds(pl.multiple_of((1 - c) * rh, 16), rh), :], sib_ref, ssem, rsem, (x, y, 1 - c))
            cp.start()
            mine_ref[...] = whole[pl.ds(pl.multiple_of(c * rh, 16), rh), :]
            cp.wait()
        dgm_ref[...] = jnp.sum(dmh * n, axis=0, keepdims=True)

    if not pair:
        return pl.pallas_call(body, name="mem_bwd", out_shape=(S(wkv.shape, BF), S((1, d), F32)),
                              compiler_params=_params())(dkv, mh, mem, gm, wkv)
    mine, sib, dgm = pl.pallas_call(
        body, name="mem_bwd", out_shape=(S((rh, cols), BF), S((rh, cols), BF), S((1, d), F32)),
        in_specs=[VMEM_SPEC] * 5, out_specs=(VMEM_SPEC, ANY, VMEM_SPEC),
        scratch_shapes=[pltpu.VMEM((rows, cols), BF), pltpu.SemaphoreType.DMA, pltpu.SemaphoreType.DMA],
        compiler_params=_params())(dkv, mh, mem, gm, wkv)
    return (mine, sib), dgm


def _mix_out_bwd(dxo, e, attn, conv_w, w_out, nqkv, hook=None):
    t, d = attn.shape
    tm = 256
    nt = t // tm
    f32 = lambda ref: ref[...].astype(F32)

    def body(dxo_ref, dxn_ref, c_ref, b_ref, u_ref, ga_ref, gc_ref, ch_ref, uh_ref, bn_ref, gcn_ref,
             attn_ref, cw_ref, w_ref, dattn_ref, de_ref, dcw_ref, dxh_ref):
        i = pl.program_id(0)

        @pl.when(i == 0)
        def _():
            dcw_ref[...] = jnp.zeros_like(dcw_ref)
        dxh = dxo_ref[...].astype(BF)
        dxh_ref[...] = dxh
        w = w_ref[...]
        dm = _nt(dxh, w)
        dmn = _nt(dxn_ref[...].astype(BF), w)
        cv, bv, uv = f32(c_ref), f32(b_ref), f32(u_ref)
        sga = _sigmoid(f32(ga_ref))
        sgc = _sigmoid(f32(gc_ref))
        z = cv * uv
        z1, z2 = _conv_taps(z, f32(ch_ref) * f32(uh_ref), i == 0)
        w0, w1, w2 = cw_ref[0:1, :], cw_ref[1:2, :], cw_ref[2:3, :]
        s = w0 * z2 + w1 * z1 + w2 * z
        av = attn_ref[...]
        dattn_ref[...] = (dm * sga).astype(BF)
        dconv = dm * sgc
        ds = dconv * bv
        dsn = jnp.where(i == nt - 1, 0.0, dmn * _sigmoid(gcn_ref[0:8, :].astype(F32)) * bn_ref[0:8, :].astype(F32))
        row = lax.broadcasted_iota(I32, (tm, 1), 0)
        dsp1 = jnp.where(row == tm - 1, dsn[0:1, :], pltpu.roll(ds, tm - 1, 0))
        dsp2 = jnp.where(row == tm - 2, dsn[0:1, :], jnp.where(row == tm - 1, dsn[1:2, :], pltpu.roll(ds, tm - 2, 0)))
        dz = w2 * ds + w1 * dsp1 + w0 * dsp2
        de_ref[:, nqkv:nqkv + d] = (dz * uv).astype(BF)
        de_ref[:, nqkv + d:nqkv + 2 * d] = (dconv * s).astype(BF)
        de_ref[:, nqkv + 2 * d:nqkv + 3 * d] = (dz * cv).astype(BF)
        de_ref[:, nqkv + 3 * d:nqkv + 4 * d] = (dm * av * sga * (1.0 - sga)).astype(BF)
        de_ref[:, nqkv + 4 * d:nqkv + 5 * d] = (dm * (bv * s) * sgc * (1.0 - sgc)).astype(BF)
        dcw_ref[0:1, :] += jnp.sum(ds * z2, axis=0, keepdims=True)
        dcw_ref[1:2, :] += jnp.sum(ds * z1, axis=0, keepdims=True)
        dcw_ref[2:3, :] += jnp.sum(ds * z, axis=0, keepdims=True)

    ecol = lambda cb: pl.BlockSpec((tm, d), lambda i: (i, cb))
    prev = lambda cb: pl.BlockSpec((HALO, d), lambda i: (jnp.maximum(i * (tm // HALO) - 1, 0), cb))
    nxt = lambda rows, cb: pl.BlockSpec((rows, d), lambda i: (jnp.minimum((i + 1) * (tm // rows), t // rows - 1), cb))
    row = pl.BlockSpec((tm, d), lambda i: (i, 0))
    return _call(
        body, name="mix_out_bwd", grid=(nt,),
        out_shape=(S((t, d), BF), S((t, nqkv + 5 * d), BF), S((8, d), F32), S((t, d), BF)),
        in_specs=[row, nxt(8, 0), ecol(0), ecol(1), ecol(2), ecol(3), ecol(4), prev(0), prev(2), nxt(HALO, 1), nxt(HALO, 4),
                  row, _resident(conv_w.shape), _resident(w_out.shape)],
        out_specs=(row, pl.BlockSpec((tm, nqkv + 5 * d), lambda i: (i, 0)), _acc_spec((8, d)), row),
        sem=("arbitrary",), args=(dxo, dxo, e, e, e, e, e, e, e, e, e, attn, conv_w, w_out), hook=hook)


def _swa_bwd(qkv, dattn, pos_col, pos_row, bias_t, sinks, dp, hook=None):
    t = qkv.shape[0]
    nb = t // BLOCK
    qw = SWA_HEADS * HEAD_DIM
    kw = SWA_KV_HEADS * HEAD_DIM

    def body(q_ref, kp_ref, kc_ref, vp_ref, vc_ref, do_ref, pq_ref, pkp_ref, pkc_ref, bias_ref, sink_ref,
             dp_in, dp_ref, gb_ref, dsk_ref, carry_ref, dq_ref):
        b = pl.program_id(0)

        @pl.when(b == 0)
        def _():
            gb_ref[...] = jnp.zeros_like(gb_ref)
            dsk_ref[...] = jnp.zeros_like(dsk_ref)
            carry_ref[...] = jnp.zeros_like(carry_ref)
            dq_ref[...] = jnp.zeros_like(dq_ref)
        dp_ref[:, 0:qw] = dq_ref[...]

        @pl.when(b < nb)
        def _():
            vis = _swa_visible(b, pq_ref, pkp_ref, pkc_ref)
            k2 = jnp.concatenate([kp_ref[...], kc_ref[...]], axis=0)
            v2 = jnp.concatenate([vp_ref[...], vc_ref[...]], axis=0)
            for hk in range(SWA_KV_HEADS):
                lo, hi = hk * HEAD_DIM, (hk + 1) * HEAD_DIM
                kh = k2[:, lo:hi]
                vh = v2[:, lo:hi]
                qg = _group_heads(q_ref, hk)
                dog = _group_heads(do_ref, hk)
                pn, psn = _swa_probs(qg, kh, vis, bias_ref[hk], _group_sinks(sink_ref, hk))
                dp = _nt(dog, vh)
                delta = jnp.sum(pn * dp, axis=-1, keepdims=True)
                ds = pn * (dp - delta)
                gb_ref[hk] += ds
                dsk_ref[hk] += -psn * delta
                dsb = (ds * (HEAD_DIM ** -0.5)).astype(BF)
                dqg = _nn(dsb, kh).astype(BF)
                for g in range(SWA_GROUP):
                    h = hk * SWA_GROUP + g
                    dq_ref[:, h * HEAD_DIM:(h + 1) * HEAD_DIM] = dqg[g * BLOCK:(g + 1) * BLOCK]
                dk = _tn(dsb, qg)
                dv = _tn(pn.astype(BF), dog)
                dp_ref[:, qw + lo:qw + hi] = (carry_ref[:, lo:hi] + dk[0:BLOCK]).astype(BF)
                dp_ref[:, qw + kw + lo:qw + kw + hi] = (carry_ref[:, kw + lo:kw + hi] + dv[0:BLOCK]).astype(BF)
                carry_ref[:, lo:hi] = dk[BLOCK:2 * BLOCK]
                carry_ref[:, kw + lo:kw + hi] = dv[BLOCK:2 * BLOCK]

        @pl.when(b == nb)
        def _():
            dp_ref[:, qw:qw + 2 * kw] = carry_ref[...].astype(BF)

    cur = lambda b: jnp.minimum(b, nb - 1)
    prev = lambda b: jnp.maximum(cur(b) - 1, 0)
    return _call(
        body, name="swa_bwd", grid=(nb + 1,),
        out_shape=(S(dp.shape, BF), S((SWA_KV_HEADS, GROUP_ROWS, 2 * BLOCK), F32), S((SWA_KV_HEADS, GROUP_ROWS, 1), F32)),
        in_specs=[
            pl.BlockSpec((BLOCK, qw), lambda b: (cur(b), 0)),
            pl.BlockSpec((BLOCK, kw), lambda b: (prev(b), qw // kw)),
            pl.BlockSpec((BLOCK, kw), lambda b: (cur(b), qw // kw)),
            pl.BlockSpec((BLOCK, kw), lambda b: (prev(b), qw // kw + 1)),
            pl.BlockSpec((BLOCK, kw), lambda b: (cur(b), qw // kw + 1)),
            pl.BlockSpec((BLOCK, qw), lambda b: (cur(b), 0)),
            pl.BlockSpec((BLOCK, 1), lambda b: (cur(b), 0)),
            pl.BlockSpec((1, BLOCK), lambda b: (0, prev(b))),
            pl.BlockSpec((1, BLOCK), lambda b: (0, cur(b))),
            _resident(bias_t.shape),
            SMEM_SPEC,
            ANY,
        ],
        out_specs=(
            pl.BlockSpec((BLOCK, qw + 2 * kw), lambda b: (jnp.maximum(b - 1, 0), 0)),
            _acc_spec((SWA_KV_HEADS, GROUP_ROWS, 2 * BLOCK)),
            _acc_spec((SWA_KV_HEADS, GROUP_ROWS, 1)),
        ),
        scratch_shapes=[pltpu.VMEM((BLOCK, 2 * kw), F32), pltpu.VMEM((BLOCK, qw), BF)],
        aliases={11: 0}, sem=("arbitrary",),
        args=(qkv, qkv, qkv, qkv, qkv, dattn, pos_col, pos_row, pos_row, bias_t, sinks, dp), hook=hook)


def _bias_reduce(gb, dsk):
    def body(gb_ref, dsk_ref, drb_ref, dsink_ref):
        bucket = _t5_bucket(_block_rel())
        for b in range(REL_BUCKETS):
            mask = bucket == b
            for h in range(SWA_HEADS):
                drb_ref[b, h] = jnp.sum(jnp.where(mask, gb_ref[h], 0.0))
        for h in range(SWA_HEADS):
            dsink_ref[0, h] = jnp.sum(dsk_ref[h])

    return pl.pallas_call(
        body, name="bias_reduce", out_shape=(S((REL_BUCKETS, SWA_HEADS), F32), S((1, SWA_HEADS), F32)),
        in_specs=[VMEM_SPEC, VMEM_SPEC], out_specs=(SMEM_SPEC, SMEM_SPEC),
    )(gb, dsk)


def _mix_in_bwd(dp, x, gn, w_in, dxo):
    t, d = x.shape
    tm = 256
    npr = dp.shape[1]

    def body(dp_ref, x_ref, gn_ref, w_ref, dxo_ref, dx_ref, dgn_ref, h_ref):
        @pl.when(pl.program_id(0) == 0)
        def _():
            dgn_ref[...] = jnp.zeros_like(dgn_ref)
        dh = jnp.zeros((tm, d), F32)
        for c0 in range(0, npr, 1024):
            c1 = min(c0 + 1024, npr)
            dh = dh + _nt(dp_ref[:, c0:c1], w_ref[:, c0:c1])
        n, r = _rms(x_ref[...])
        gnv = gn_ref[...]
        h_ref[...] = (n * gnv).astype(BF)
        dx, dgn = _rms_bwd(dh, n, r, gnv)
        dx_ref[...] = dxo_ref[...] + dx
        dgn_ref[...] += dgn

    row = pl.BlockSpec((tm, d), lambda i: (i, 0))
    return pl.pallas_call(
        body, name="mix_in_bwd", grid=(t // tm,),
        out_shape=(S((t, d), F32), S((1, d), F32), S((t, d), BF)),
        in_specs=[pl.BlockSpec((tm, npr), lambda i: (i, 0)), row, _resident((1, d)), _resident(w_in.shape), row],
        out_specs=(row, _acc_spec((1, d)), row),
        compiler_params=_params(("arbitrary",)),
    )(dp, x, gn, w_in, dxo)


PAIR_STEPS = 8


def _add_bf16(pairs, name):
    def body(*refs):
        ins, outs = refs[:2 * len(pairs)], refs[2 * len(pairs):]
        for q, o_ref in enumerate(outs):
            o_ref[...] = (ins[2 * q][...].astype(F32) + ins[2 * q + 1][...].astype(F32)).astype(BF)

    in_specs, out_specs, out_shape, args = [], [], [], []
    for a, b in pairs:
        rows, cols = a.shape
        blk = pl.BlockSpec((rows // PAIR_STEPS, cols), lambda i: (i, 0))
        in_specs += [blk, blk]
        out_specs.append(blk)
        out_shape.append(S((rows, cols), BF))
        args += [a, b]
    return pl.pallas_call(body, name=name, grid=(PAIR_STEPS,), out_shape=tuple(out_shape), in_specs=in_specs,
                          out_specs=tuple(out_specs), compiler_params=_params(("parallel",)))(*args)


def _adam_update(w, g, m, v):
    mn = ADAM_B1 * m + (1.0 - ADAM_B1) * g
    vn = ADAM_B2 * v + (1.0 - ADAM_B2) * (g * g)
    m_hat = mn / (1.0 - ADAM_B1 ** ADAM_STEP)
    v_hat = vn / (1.0 - ADAM_B2 ** ADAM_STEP)
    return -ADAM_LR * (m_hat / (jnp.sqrt(v_hat) + ADAM_EPS) + ADAM_WD * w), mn, vn


def _adamw(quads, name, steps, echo=False, hook=None):
    n_out = 4 if echo else 3

    def body(*refs):
        ins, outs = refs[:4 * len(quads)], refs[4 * len(quads):]
        for q in range(len(quads)):
            w_ref, g_ref, m_ref, v_ref = ins[4 * q:4 * q + 4]
            res = outs[n_out * q:n_out * q + n_out]
            gv = g_ref[...]
            if echo:
                res[0][...] = gv
            res[-3][...], res[-2][...], res[-1][...] = _adam_update(w_ref[...], gv, m_ref[...], v_ref[...])

    in_specs, out_specs, out_shape, args = [], [], [], []
    for quad in quads:
        rows, cols = quad[0].shape
        blk = pl.BlockSpec((rows // steps, cols), lambda i: (i, 0))
        in_specs += [blk] * 4
        out_specs += [blk] * n_out
        out_shape += [S((rows, cols), F32)] * n_out
        args += list(quad)
    res = _call(body, name=name, grid=(steps,), out_shape=tuple(out_shape), in_specs=in_specs, out_specs=tuple(out_specs),
                sem=("parallel",), args=args, hook=hook)
    return [res[n_out * q:n_out * q + n_out] for q in range(len(quads))]


def _place():
    x, y, c = lax.axis_index("x"), lax.axis_index("y"), lax.axis_index("c")
    return x, y, c


OTHER_CHIPS = ((1, 0), (0, 1), (1, 1))


def _flip(v, f):
    return 1 - v if f else v


def _remote(src, dst, ssem, rsem, dev):
    return pltpu.make_async_remote_copy(src_ref=src, dst_ref=dst, send_sem=ssem, recv_sem=rsem,
                                        device_id=dev, device_id_type=MESH)


class _Both:
    def __init__(self, hooks):
        self.hooks = hooks
        self.ins = [a for h in hooks for a in h.ins]
        self.out_shape = tuple(o for h in hooks for o in h.out_shape)
        self.scratch = [x for h in hooks for x in h.scratch]

    def _each(self, ins, outs, scr):
        i = o = s = 0
        for h in self.hooks:
            ni, no, ns = len(h.ins), len(h.out_shape), len(h.scratch)
            yield h, ins[i:i + ni], outs[o:o + no], scr[s:s + ns]
            i, o, s = i + ni, o + no, s + ns

    def start(self, ins, outs, scr):
        for h, *refs in self._each(ins, outs, scr):
            h.start(*refs)

    def finish(self, ins, outs, scr):
        for h, *refs in self._each(ins, outs, scr):
            h.finish(*refs)

    @property
    def results(self):
        return [h.results for h in self.hooks]

    @results.setter
    def results(self, res):
        for h, _, mine, _ in self._each((), tuple(res), ()):
            h.results = mine


def _exchange(hook, name):
    n_in, n_out = len(hook.ins), len(hook.out_shape)

    def body(*refs):
        ins, outs, scr = refs[:n_in], refs[n_in:n_in + n_out], refs[n_in + n_out:]
        hook.start(ins, outs, scr)
        hook.finish(ins, outs, scr)

    hook.results = pl.pallas_call(
        body, name=name, out_shape=tuple(hook.out_shape), in_specs=[ANY] * n_in, out_specs=(ANY,) * n_out,
        scratch_shapes=list(hook.scratch), compiler_params=_params(),
    )(*hook.ins)


class _GatherHook:
    def __init__(self, shards, kinds):
        self.ins, self.kinds, n = list(shards), list(kinds), len(shards)
        self.out_shape = tuple(
            S((w.shape[0], 4 * w.shape[1]), BF) if k == "col" else S((4,) + w.shape, BF) for w, k in zip(shards, kinds))
        dma = pltpu.SemaphoreType.DMA
        self.scratch = ([dma((n, 3)) for _ in range(5)] + [dma((n,)), dma((n,))]
                        + [pltpu.VMEM((3, w.shape[0] // 2, w.shape[1]), BF) for w in shards]
                        + [pltpu.VMEM(w.shape, BF) for w in shards])

    def _window(self, outs, i, s, half):
        rows, cols = self.ins[i].shape
        rh = rows // 2
        start = pl.multiple_of(half * rh, 16)
        if self.kinds[i] == "col":
            return outs[i].at[pl.ds(start, rh), pl.ds(pl.multiple_of(s * cols, 128), cols)]
        return outs[i].at[s, pl.ds(start, rh), :]

    def _copies(self, ins, outs, scr):
        n = len(ins)
        ssem, rsem, fssem, frsem, ksem, lsem, osem = scr[:7]
        land, own = scr[7:7 + n], scr[7 + n:7 + 2 * n]
        x, y, c = _place()
        sibling = (x, y, 1 - c)
        loads, stores, sends, forwards, keeps, passed = [], [], [], [], [], []
        for i in range(n):
            rows, cols = self.ins[i].shape
            rh = rows // 2
            mine = (outs[i].at[:, pl.ds(pl.multiple_of((2 * x + y) * cols, 128), cols)] if self.kinds[i] == "col"
                    else outs[i].at[2 * x + y])
            loads.append(functools.partial(pltpu.make_async_copy, ins[i], own[i], lsem.at[i]))
            stores.append(functools.partial(pltpu.make_async_copy, own[i], mine, osem.at[i]))
            src = ins[i].at[pl.ds(pl.multiple_of(c * rh, 16), rh), :]
            for j, (fx, fy) in enumerate(OTHER_CHIPS):
                px, py = _flip(x, fx), _flip(y, fy)
                sends.append(functools.partial(_remote, src, land[i].at[j], ssem.at[i, j], rsem.at[i, j], (px, py, c)))
                here = self._window(outs, i, 2 * px + py, c)
                forwards.append(functools.partial(_remote, land[i].at[j], here, fssem.at[i, j], frsem.at[i, j], sibling))
                keeps.append(functools.partial(pltpu.make_async_copy, land[i].at[j], here, ksem.at[i, j]))
                there = self._window(outs, i, 2 * px + py, 1 - c)
                passed.append(functools.partial(_remote, there, there, fssem.at[i, j], frsem.at[i, j], sibling))
        return loads, stores, sends, forwards, keeps, passed

    def start(self, ins, outs, scr):
        loads, _, sends, _, _, _ = self._copies(ins, outs, scr)
        for make in sends + loads:
            make().start()

    def finish(self, ins, outs, scr):
        loads, stores, sends, forwards, keeps, passed = self._copies(ins, outs, scr)
        for load, store in zip(loads, stores):
            load().wait()
            store().start()
        for send, forward, keep in zip(sends, forwards, keeps):
            send().wait_recv()
            forward().start()
            keep().start()
        for make in passed:
            make().wait_recv()
        for make in sends + forwards:
            make().wait_send()
        for make in keeps + stores:
            make().wait()


class _ChipsHook:
    def __init__(self, parts, kinds):
        self.ins, self.kinds, n = list(parts), list(kinds), len(parts)
        self.out_shape = tuple(
            S((4, p.shape[0], p.shape[1] // 4), BF) if k == "col" else S(p.shape, BF) for p, k in zip(parts, kinds))
        dma = pltpu.SemaphoreType.DMA
        self.scratch = ([dma((n, 3)), dma((n, 3)), dma((n,)), dma((n,))]
                        + [pltpu.VMEM(o.shape[1:], BF) for o in self.out_shape])

    def _slab(self, ins, i, s):
        _, rows, cols = self.out_shape[i].shape
        if self.kinds[i] == "col":
            return ins[i].at[:, pl.ds(pl.multiple_of(s * cols, 128), cols)]
        return ins[i].at[s]

    def _copies(self, ins, outs, scr):
        ssem, rsem, lsem, osem = scr[:4]
        own = scr[4:]
        x, y, c = _place()
        loads, stores, sends = [], [], []
        for i in range(len(ins)):
            loads.append(functools.partial(pltpu.make_async_copy, self._slab(ins, i, 2 * x + y), own[i], lsem.at[i]))
            stores.append(functools.partial(pltpu.make_async_copy, own[i], outs[i].at[3], osem.at[i]))
            for j, (fx, fy) in enumerate(OTHER_CHIPS):
                px, py = _flip(x, fx), _flip(y, fy)
                sends.append(functools.partial(_remote, self._slab(ins, i, 2 * px + py), outs[i].at[j], ssem.at[i, j],
                                               rsem.at[i, j], (px, py, c)))
        return loads, stores, sends

    def start(self, ins, outs, scr):
        loads, _, sends = self._copies(ins, outs, scr)
        for make in sends + loads:
            make().start()

    def finish(self, ins, outs, scr):
        loads, stores, sends = self._copies(ins, outs, scr)
        for load, store in zip(loads, stores):
            load().wait()
            store().start()
        for make in sends + stores:
            make().wait()


SHARE_STEPS = 2


def _sum_share(slab_list, name, hook=None):
    n = len(slab_list)
    geom = [(sl.shape[1], sl.shape[1] // SHARE_STEPS, sl.shape[2]) for sl in slab_list]

    def body(*refs):
        ins, outs, scr = refs[:n], refs[n:2 * n], refs[2 * n:]
        i = pl.program_id(0)
        x, y, c = _place()
        sibling = (x, y, 1 - c)

        def copies(q, k):
            rh, tr, _ = geom[q]
            stage, lsem, ssem, rsem = scr[4 * q:4 * q + 4]
            dst = outs[q].at[pl.ds(pl.multiple_of(c * rh + k * tr, 8), tr), :]
            return (pltpu.make_async_copy(stage.at[k], dst, lsem.at[k]),
                    _remote(stage.at[k], dst, ssem.at[k], rsem, sibling))

        for q in range(n):
            acc = ins[q][3].astype(F32)
            for k in range(3):
                acc = acc + ins[q][k].astype(F32)
            scr[4 * q][i] = acc
            for cp in copies(q, i):
                cp.start()

        @pl.when(i == SHARE_STEPS - 1)
        def _():
            for q in range(n):
                rh = geom[q][0]
                for k in range(SHARE_STEPS):
                    local, remote = copies(q, k)
                    local.wait()
                    remote.wait_send()
                got = outs[q].at[pl.ds(pl.multiple_of((1 - c) * rh, 8), rh), :]
                _remote(got, got, scr[4 * q + 2].at[0], scr[4 * q + 3], sibling).wait_recv()

    dma = pltpu.SemaphoreType.DMA
    scratch = []
    for rh, tr, cols in geom:
        scratch += [pltpu.VMEM((SHARE_STEPS, tr, cols), F32), dma((SHARE_STEPS,)), dma((SHARE_STEPS,)), dma]
    return _call(
        body, name=name, grid=(SHARE_STEPS,), out_shape=tuple(S((2 * rh, cols), F32) for rh, _, cols in geom),
        in_specs=[pl.BlockSpec((4, tr, cols), lambda i: (0, i, 0)) for _, tr, cols in geom], out_specs=(ANY,) * n,
        scratch_shapes=scratch, sem=("arbitrary",), args=list(slab_list), hook=hook)


class _SmallSumHook:
    def __init__(self, buf):
        self.ins, self.out_shape = [buf], (S(buf.shape, F32),)
        dma = pltpu.SemaphoreType.DMA
        self.scratch = [pltpu.VMEM((8,) + buf.shape, F32), pltpu.VMEM(buf.shape, F32), dma((7,)), dma((7,)), dma]

    def _sends(self, scr):
        slots, _, ssem, rsem, _ = scr
        x, y, c = _place()
        me = 4 * x + 2 * y + c
        for r in range(1, 8):
            px, py, pc = _flip(x, (r >> 2) & 1), _flip(y, (r >> 1) & 1), _flip(c, r & 1)
            yield (functools.partial(_remote, slots.at[me], slots.at[me], ssem.at[r - 1], rsem.at[r - 1], (px, py, pc)),
                   functools.partial(_remote, slots.at[me], slots.at[4 * px + 2 * py + pc], ssem.at[r - 1],
                                     rsem.at[r - 1], (px, py, pc)))

    def start(self, ins, outs, scr):
        slots, _, _, _, lsem = scr
        x, y, c = _place()
        load = pltpu.make_async_copy(ins[0], slots.at[4 * x + 2 * y + c], lsem)
        load.start()
        load.wait()
        for send, _ in self._sends(scr):
            send().start()

    def finish(self, ins, outs, scr):
        slots, total, _, _, lsem = scr
        for _, arrival in self._sends(scr):
            arrival().wait_recv()
        for send, _ in self._sends(scr):
            send().wait_send()
        acc = slots[0]
        for k in range(1, 8):
            acc = acc + slots[k]
        total[...] = acc
        store = pltpu.make_async_copy(total, outs[0], lsem)
        store.start()
        store.wait()


BIG = ("ffn1_w_gu", "ffn1_w_down", "w_in", "w_out", "xattn_wq", "xattn_wkv", "xattn_wo", "ffn2_w_gu", "ffn2_w_down")
KIND = {"ffn1_w_gu": "col", "ffn1_w_down": "row", "w_in": "col", "w_out": "row", "xattn_wq": "row",
        "xattn_wkv": "col", "xattn_wo": "row", "ffn2_w_gu": "col", "ffn2_w_down": "row"}
WEIGHTS = ("rel_bias", "ffn1_norm", "ffn1_w_gu", "ffn1_w_down", "mix_norm", "w_in", "sinks", "conv_w", "w_out",
           "xattn_norm", "mem_norm", "xattn_wq", "xattn_wkv", "xattn_wo", "ffn2_norm", "ffn2_w_gu", "ffn2_w_down",
           "final_norm")
SMALL_ROWS = 16
GAIN_ROW = {"ffn1_norm": 0, "mix_norm": 1, "xattn_norm": 2, "mem_norm": 3, "ffn2_norm": 4, "final_norm": 5}
CONV_ROW, SINK_ROW, BIAS_ROW, LOSS_ROW = 6, 9, 10, 11


def _rows_block(rows, d):
    buf = jnp.zeros((SMALL_ROWS, d), F32)
    for r, v in rows.items():
        buf = lax.dynamic_update_slice(buf, v.reshape(1, -1).astype(F32), (r, 0))
    return buf


def _local_step(x, mem, pos, target, w, gains, rel_bias, sinks, conv_w, shards=None):
    t, d = x.shape
    dist = shards is not None
    w = dict(w)
    grads, slabs = {}, {}
    pos_col = pos.reshape(t, 1)
    pos_row = pos.reshape(1, t)
    bias_t = _bias_build(rel_bias).reshape(SWA_KV_HEADS, GROUP_ROWS, 2 * BLOCK)

    def gather(names):
        return _GatherHook([shards[k] for k in names], [KIND[k] for k in names]) if dist else None

    def gathered(names, hook):
        if dist:
            for k, gw in zip(names, hook.results):
                w[k] = gw if KIND[k] == "col" else gw.reshape(-1, gw.shape[-1])

    def dw(problems, tn, name, hook=None):
        if dist:
            res = _dw_pair([(a, b) for _, a, b in problems], tn, name, KIND[problems[0][0]], hook)
        else:
            res = [_dw(a, b, tn, name + "_" + k) for k, a, b in problems]
        grads.update(zip((k for k, _, _ in problems), res))

    def pair_sums(names):
        if not dist:
            return None
        flat = lambda k, v: v.reshape(-1, v.shape[-1]) if KIND[k] == "row" else v
        sums = _add_bf16([(flat(k, grads[k][0]), flat(k, grads[k][1])) for k in names], "pair_sum_" + names[0])
        return {k: (p if KIND[k] == "col" else p.reshape(4, -1, p.shape[-1])) for k, p in zip(names, sums)}

    def chips(names, parts):
        return _ChipsHook([parts[k] for k in names], [KIND[k] for k in names]) if dist else None

    def reduced(names, hook):
        if dist:
            slabs.update(zip(names, hook.results))

    names = ("w_in", "w_out")
    hook = gather(names)
    x1, g1, u1 = _ffn_fwd(x, gains["ffn1_norm"], w["ffn1_w_gu"], w["ffn1_w_down"], "ffn1_fwd", hook)
    gathered(names, hook)
    names = ("xattn_wq", "xattn_wkv", "xattn_wo")
    hook = gather(names)
    qkv, e = _mix_proj(x1, gains["mix_norm"], w["w_in"], hook)
    gathered(names, hook)
    names = ("ffn2_w_gu",)
    hook = gather(names)
    attn = _swa_fwd(qkv, pos_col, pos_row, bias_t, sinks, hook)
    gathered(names, hook)
    names = ("ffn2_w_down",)
    hook = gather(names)
    x2, merged = _mix_out_fwd(e, attn, conv_w, w["w_out"], x1, hook)
    gathered(names, hook)
    mh, kv = _mem_kv(mem, gains["mem_norm"], w["xattn_wkv"])
    x3, qx, o = _xattn_fwd(x2, gains["xattn_norm"], w["xattn_wq"], kv, w["xattn_wo"])
    dx4, g2, u2, d_final, loss = _ffn_fwd(x3, gains["ffn2_norm"], w["ffn2_w_gu"], w["ffn2_w_down"], "ffn2_fwd",
                                          head=(gains["final_norm"], target))

    dx3, d_ffn2, dgu2, a2, h4, dyh4 = _ffn_bwd(dx4, x3, gains["ffn2_norm"], g2, u2, w["ffn2_w_gu"], w["ffn2_w_down"],
                                               "ffn2_bwd")
    dw([("ffn2_w_gu", h4, dgu2)], 1408, "dw_ffn2_gu")
    dw([("ffn2_w_down", a2, dyh4)], 512, "dw_ffn2_down")
    parts = pair_sums(("ffn2_w_gu", "ffn2_w_down"))
    hook = chips(("ffn2_w_gu",), parts)
    dx2, d_xattn, dkv, dxh3, h3, dqx = _xattn_bwd(dx3, x2, gains["xattn_norm"], qx, kv, w["xattn_wq"], w["xattn_wo"], hook)
    reduced(("ffn2_w_gu",), hook)
    grads["xattn_wkv"], d_mem = _mem_bwd(dkv, mh, mem, gains["mem_norm"], w["xattn_wkv"], dist)
    hook = chips(("ffn2_w_down",), parts)
    dattn, dp, dcw, dxh2 = _mix_out_bwd(dx2, e, attn, conv_w, w["w_out"], qkv.shape[1], hook)
    reduced(("ffn2_w_down",), hook)
    dw([("xattn_wo", o, dxh3), ("xattn_wq", h3, dqx), ("w_out", merged, dxh2)], 1024, "dw_wo_wq_wout")
    names = ("xattn_wo", "xattn_wq", "xattn_wkv", "w_out")
    hook = chips(names, pair_sums(names))
    dp, gb, dsk = _swa_bwd(qkv, dattn, pos_col, pos_row, bias_t, sinks, dp, hook)
    reduced(names, hook)
    d_rel_bias, d_sinks = _bias_reduce(gb.reshape(SWA_HEADS, BLOCK, 2 * BLOCK), dsk.reshape(SWA_HEADS, BLOCK, 1))
    dx1, d_mix, h2 = _mix_in_bwd(dp, x1, gains["mix_norm"], w["w_in"], dx2)
    dw([("w_in", h2, dp)], w["w_in"].shape[1] // 4, "dw_win")
    names = ("w_in",)
    hook = chips(names, pair_sums(names))
    dx0, d_ffn1, dgu1, a1, h1, dyh1 = _ffn_bwd(dx1, x, gains["ffn1_norm"], g1, u1, w["ffn1_w_gu"], w["ffn1_w_down"],
                                               "ffn1_bwd", hook)
    reduced(names, hook)
    dw([("ffn1_w_down", a1, dyh1)], 512, "dw_ffn1_down")
    names = ("ffn1_w_down",)
    hook = chips(names, pair_sums(names))
    dw([("ffn1_w_gu", h1, dgu1)], 1408, "dw_ffn1_gu", hook)
    reduced(names, hook)
    rows = {0: d_ffn1, 1: d_mix, 2: d_xattn, 3: d_mem, 4: d_ffn2, 5: d_final, SINK_ROW: d_sinks, BIAS_ROW: d_rel_bias,
            LOSS_ROW: loss[0, 0:1]}
    rows.update({CONV_ROW + j: dcw[j] for j in range(3)})
    last = ("ffn1_w_gu",)
    return dx0, (slabs if dist else grads), _rows_block(rows, d), chips(last, pair_sums(last))


def kernel(x, mem, positions, rel_bias, ffn1_norm, ffn1_w_gu, ffn1_w_down, mix_norm, w_in, sinks, conv_w, w_out, xattn_norm, mem_norm, xattn_wq, xattn_wkv, xattn_wo, ffn2_norm, ffn2_w_gu, ffn2_w_down, final_norm, loss_target, m_rel_bias, m_ffn1_norm, m_ffn1_w_gu, m_ffn1_w_down, m_mix_norm, m_w_in, m_sinks, m_conv_w, m_w_out, m_xattn_norm, m_mem_norm, m_xattn_wq, m_xattn_wkv, m_xattn_wo, m_ffn2_norm, m_ffn2_w_gu, m_ffn2_w_down, m_final_norm, v_rel_bias, v_ffn1_norm, v_ffn1_w_gu, v_ffn1_w_down, v_mix_norm, v_w_in, v_sinks, v_conv_w, v_w_out, v_xattn_norm, v_mem_norm, v_xattn_wq, v_xattn_wkv, v_xattn_wo, v_ffn2_norm, v_ffn2_w_gu, v_ffn2_w_down, v_final_norm):
    args = dict(locals())
    wts = {k: args[k] for k in WEIGHTS}
    mom = {k: args["m_" + k] for k in WEIGHTS}
    var = {k: args["v_" + k] for k in WEIGHTS}
    d = x.shape[-1]
    s_me = 2 * lax.axis_index("x") + lax.axis_index("y")

    shards = {k: wts[k][0].astype(BF) for k in BIG}
    first = ("ffn1_w_gu", "ffn1_w_down")
    cw_cols = conv_w.shape[-1]
    placed = lax.dynamic_update_slice(jnp.zeros((SMALL_ROWS, d), F32), 0.5 * conv_w[0], (0, s_me * cw_cols))
    head = _Both([_GatherHook([shards[k] for k in first], [KIND[k] for k in first]), _SmallSumHook(placed)])
    _exchange(head, "gather_ffn1")
    gathered, (conv_sum,) = head.results
    whole = {k: (gw if KIND[k] == "col" else gw.reshape(-1, gw.shape[-1])) for k, gw in zip(first, gathered)}
    conv_whole = conv_sum[0:3]

    gains = {k: wts[k].reshape(1, d) for k in GAIN_ROW}
    dx0, slabs, small, last_chips = _local_step(x[0], mem[0], positions[0], loss_target[0], whole, gains, rel_bias, sinks,
                                               conv_whole, shards)

    late = ("ffn1_w_gu", "ffn1_w_down", "ffn2_w_down")
    early = tuple(k for k in BIG if k not in late)
    first_ones = tuple(k for k in BIG if k != "ffn1_w_gu")
    tail = _Both([last_chips, _SmallSumHook(small)])
    shard_grads = dict(zip(first_ones, _sum_share([slabs[k] for k in first_ones], "sum_share", tail)))
    (last_slabs,), (small_sum,) = tail.results
    shard_grads["ffn1_w_gu"], = _sum_share([last_slabs], "sum_share_ffn1_w_gu")
    quad = lambda k: (wts[k][0], shard_grads[k], mom[k][0], var[k][0])
    updates = dict(zip(early, _adamw([quad(k) for k in early], "adamw_early", 16, echo=True)))
    updates.update(zip(late, _adamw([quad(k) for k in late], "adamw_late", 8, echo=True)))
    loss = small_sum[LOSS_ROW, 0]

    out_g, out_d, out_m, out_v = {}, {}, {}, {}
    for k in BIG:
        out_g[k], out_d[k], out_m[k], out_v[k] = (a[None] for a in updates[k])

    conv_g = lax.dynamic_slice(small_sum, (CONV_ROW, s_me * cw_cols), (3, cw_cols))

    def pack(src, conv_block):
        rows = {r: src[k] for k, r in GAIN_ROW.items()}
        rows.update({CONV_ROW + j: conv_block[j] for j in range(3)})
        rows[SINK_ROW], rows[BIAS_ROW] = src["sinks"], src["rel_bias"]
        return _rows_block(rows, d)

    small_g = {k: small_sum[r] for k, r in GAIN_ROW.items()}
    small_g["sinks"] = small_sum[SINK_ROW, 0:sinks.size]
    small_g["rel_bias"] = small_sum[BIAS_ROW, 0:rel_bias.size]
    gp = pack(small_g, conv_g)
    (dl, mn, vn), = _adamw([(pack(wts, conv_w[0]), gp, pack(mom, m_conv_w[0]), pack(var, v_conv_w[0]))], "adamw_small", 1)

    def unpack(buf, k):
        if k in GAIN_ROW:
            return buf[GAIN_ROW[k]].reshape(wts[k].shape)
        if k == "conv_w":
            return buf[CONV_ROW:CONV_ROW + 3, 0:cw_cols][None]
        if k == "sinks":
            return buf[SINK_ROW, 0:sinks.size].reshape(sinks.shape)
        return buf[BIAS_ROW, 0:rel_bias.size].reshape(rel_bias.shape)

    for k in WEIGHTS:
        if k not in KIND:
            out_g[k], out_d[k], out_m[k], out_v[k] = unpack(gp, k), unpack(dl, k), unpack(mn, k), unpack(vn, k)

    return (loss, dx0[None], *[out_g[k] for k in WEIGHTS], *[out_d[k] for k in WEIGHTS],
            *[out_m[k] for k in WEIGHTS], *[out_v[k] for k in WEIGHTS])
```

```python
import functools
import math

import jax
import jax.numpy as jnp
from jax import lax
from jax.experimental import pallas as pl
from jax.experimental.pallas import tpu as pltpu

BF = jnp.bfloat16
F32 = jnp.float32
I32 = jnp.int32
S = jax.ShapeDtypeStruct

EPS = 1e-6
NEG = -1e30
POS_PAD = 1 << 30
WINDOW = 128
BLOCK = 128
HEAD_DIM = 64
SWA_HEADS = 16
SWA_KV_HEADS = 4
SWA_GROUP = SWA_HEADS // SWA_KV_HEADS
MEM_HEADS = 4
REL_BUCKETS = 32
REL_MAX_DIST = 128
ADAM_LR = 0.001
ADAM_B1 = 0.9
ADAM_B2 = 0.999
ADAM_EPS = 1e-08
ADAM_WD = 0.01
ADAM_STEP = 10

V7X_VMEM_LIMIT_BYTES = 56 * 1024 * 1024
MESH = pl.DeviceIdType.MESH
ANY = pl.BlockSpec(memory_space=pl.ANY)
VMEM_SPEC = pl.BlockSpec(memory_space=pltpu.VMEM)
SMEM_SPEC = pl.BlockSpec(memory_space=pltpu.SMEM)


def _params(sem=None):
    return pltpu.CompilerParams(dimension_semantics=sem, vmem_limit_bytes=V7X_VMEM_LIMIT_BYTES)


def _resident(shape):
    nd = len(shape)
    return pl.BlockSpec(shape, lambda *_: (0,) * nd, pipeline_mode=pl.Buffered(1))


def _acc_spec(shape):
    nd = len(shape)
    return pl.BlockSpec(shape, lambda *_: (0,) * nd)


def _call(body, *, name, grid, out_shape, in_specs, out_specs, args, sem, scratch_shapes=(), aliases=None, hook=None):
    aliases = aliases or {}
    if hook is None:
        return pl.pallas_call(body, name=name, grid=grid, out_shape=out_shape, in_specs=in_specs, out_specs=out_specs,
                              scratch_shapes=list(scratch_shapes), input_output_aliases=aliases,
                              compiler_params=_params(sem))(*args)
    n_in, n_out, n_scr = len(in_specs), len(out_shape), len(scratch_shapes)
    h_in, h_out = len(hook.ins), len(hook.out_shape)

    def at_step(pick):
        conds = [pl.program_id(ax) == pick(size) for ax, size in enumerate(grid)]
        return functools.reduce(jnp.logical_and, conds)

    def hosted(*refs):
        k_in, x_in = refs[:n_in], refs[n_in:n_in + h_in]
        o0 = n_in + h_in
        k_out, x_out = refs[o0:o0 + n_out], refs[o0 + n_out:o0 + n_out + h_out]
        s0 = o0 + n_out + h_out
        k_scr, x_scr = refs[s0:s0 + n_scr], refs[s0 + n_scr:]

        @pl.when(at_step(lambda size: 0))
        def _():
            hook.start(x_in, x_out, x_scr)
        body(*k_in, *k_out, *k_scr)

        @pl.when(at_step(lambda size: size - 1))
        def _():
            hook.finish(x_in, x_out, x_scr)

    res = pl.pallas_call(
        hosted, name=name, grid=grid, out_shape=tuple(out_shape) + tuple(hook.out_shape),
        in_specs=list(in_specs) + [ANY] * h_in, out_specs=tuple(out_specs) + (ANY,) * h_out,
        scratch_shapes=list(scratch_shapes) + list(hook.scratch), input_output_aliases=aliases,
        compiler_params=_params(("arbitrary",) * len(grid)),
    )(*args, *hook.ins)
    hook.results = res[n_out:]
    return res[:n_out]


def _nn(a, b):
    return jnp.dot(a, b, preferred_element_type=F32)


def _nt(a, b):
    return lax.dot_general(a, b, (((1,), (1,)), ((), ())), preferred_element_type=F32)


def _tn(a, b):
    return lax.dot_general(a, b, (((0,), (0,)), ((), ())), preferred_element_type=F32)


def _sigmoid(v):
    return 1.0 / (1.0 + jnp.exp(-v))


def _rms(x):
    r = lax.rsqrt(jnp.mean(x * x, axis=-1, keepdims=True) + EPS)
    return x * r, r


def _rms_bwd(dh, n, r, g):
    dn = dh * g
    dx = r * (dn - n * jnp.mean(dn * n, axis=-1, keepdims=True))
    return dx, jnp.sum(dh * n, axis=0, keepdims=True)


def _ffn_fwd(x, gn, wgu, wd, name, hook=None, head=None):
    t, d = x.shape
    f = wd.shape[0]
    tm, fc = 256, 1408

    def body(x_ref, gn_ref, wgu_ref, wd_ref, *rest):
        xv = x_ref[...]
        n, _ = _rms(xv)
        h = (n * gn_ref[...]).astype(BF)
        g_ref, u_ref = rest[-2:] if head is None else rest[3:5]
        acc = jnp.zeros((tm, d), F32)
        for c0 in range(0, f, fc):
            g = _nn(h, wgu_ref[:, c0:c0 + fc])
            u = _nn(h, wgu_ref[:, f + c0:f + c0 + fc])
            g_ref[:, c0:c0 + fc] = g.astype(BF)
            u_ref[:, c0:c0 + fc] = u.astype(BF)
            a = (g * _sigmoid(g)) * u
            acc = acc + _nn(a.astype(BF), wd_ref[c0:c0 + fc, :])
        y = xv + 0.5 * acc
        if head is None:
            rest[0][...] = y
            return
        gf_ref, t_ref, dy_ref, _, _, dgf_ref, loss_ref = rest

        @pl.when(pl.program_id(0) == 0)
        def _():
            dgf_ref[...] = jnp.zeros_like(dgf_ref)
            loss_ref[...] = jnp.zeros_like(loss_ref)
        ny, ry = _rms(y)
        gf = gf_ref[...]
        err = ny * gf - t_ref[...]
        loss_ref[...] += 0.5 * jnp.sum(jnp.sum(err * err, axis=-1, keepdims=True) / d, axis=0, keepdims=True)
        dy, dgf = _rms_bwd(err / d, ny, ry, gf)
        dy_ref[...] = dy
        dgf_ref[...] += dgf

    row = pl.BlockSpec((tm, d), lambda i: (i, 0))
    frow = pl.BlockSpec((tm, f), lambda i: (i, 0))
    in_specs = [row, _resident((1, d)), _resident(wgu.shape), _resident(wd.shape)]
    out_shape = (S((t, d), F32), S((t, f), BF), S((t, f), BF))
    out_specs = (row, frow, frow)
    args = (x, gn, wgu, wd)
    if head is not None:
        in_specs += [_resident((1, d)), row]
        out_shape += (S((1, d), F32), S((1, 128), F32))
        out_specs += (_acc_spec((1, d)), _acc_spec((1, 128)))
        args += tuple(head)
    return _call(body, name=name, grid=(t // tm,), out_shape=out_shape, in_specs=in_specs, out_specs=out_specs,
                 sem=("parallel",) if head is None else ("arbitrary",), args=args, hook=hook)


def _mix_proj(x, gn, w_in, hook=None):
    t, d = x.shape
    tm = 256
    nqkv = 1536
    ne = w_in.shape[1] - nqkv

    def body(x_ref, gn_ref, w_ref, qkv_ref, e_ref):
        n, _ = _rms(x_ref[...])
        h = (n * gn_ref[...]).astype(BF)
        qkv_ref[...] = _nn(h, w_ref[:, 0:nqkv]).astype(BF)
        for c0 in range(0, ne, 1024):
            e_ref[:, c0:c0 + 1024] = _nn(h, w_ref[:, nqkv + c0:nqkv + c0 + 1024]).astype(BF)

    return _call(
        body, name="mix_proj", grid=(t // tm,),
        out_shape=(S((t, nqkv), BF), S((t, ne), BF)),
        in_specs=[pl.BlockSpec((tm, d), lambda i: (i, 0)), _resident((1, d)), _resident(w_in.shape)],
        out_specs=(pl.BlockSpec((tm, nqkv), lambda i: (i, 0)), pl.BlockSpec((tm, ne), lambda i: (i, 0))),
        sem=("parallel",), args=(x, gn, w_in), hook=hook)


def _t5_bucket(rel):
    n = jnp.maximum(rel, 0)
    max_exact = REL_BUCKETS // 2
    nf = jnp.maximum(n, 1).astype(F32)
    large = max_exact + (jnp.log(nf / max_exact) / math.log(REL_MAX_DIST / max_exact)
                         * (REL_BUCKETS - max_exact)).astype(I32)
    large = jnp.minimum(large, REL_BUCKETS - 1)
    return jnp.where(n < max_exact, n, large)


def _block_rel():
    i = lax.broadcasted_iota(I32, (BLOCK, 2 * BLOCK), 0)
    j = lax.broadcasted_iota(I32, (BLOCK, 2 * BLOCK), 1)
    return i + BLOCK - j


def _bias_build(rel_bias):
    def body(rb_ref, o_ref):
        bucket = _t5_bucket(_block_rel())
        for h in range(SWA_HEADS):
            acc = jnp.zeros((BLOCK, 2 * BLOCK), F32)
            for b in range(REL_BUCKETS):
                acc = jnp.where(bucket == b, rb_ref[b, h], acc)
            o_ref[h] = acc

    return pl.pallas_call(
        body, name="bias_build", out_shape=S((SWA_HEADS, BLOCK, 2 * BLOCK), F32),
        in_specs=[SMEM_SPEC], out_specs=VMEM_SPEC,
    )(rel_bias)


GROUP_ROWS = SWA_GROUP * BLOCK


def _swa_visible(b, pq_ref, pkp_ref, pkc_ref):
    pk = jnp.concatenate([pkp_ref[...], pkc_ref[...]], axis=1)
    col = lax.broadcasted_iota(I32, (1, 2 * BLOCK), 1)
    pk = jnp.where(jnp.logical_and(b == 0, col < BLOCK), POS_PAD, pk)
    rel = jnp.concatenate([pq_ref[...]] * SWA_GROUP, axis=0) - pk
    return jnp.logical_and(rel >= 0, rel < WINDOW)


def _group_heads(ref, hk):
    h0 = hk * SWA_GROUP
    return jnp.concatenate([ref[:, (h0 + g) * HEAD_DIM:(h0 + g + 1) * HEAD_DIM] for g in range(SWA_GROUP)], axis=0)


def _group_sinks(sink_ref, hk):
    row = lax.broadcasted_iota(I32, (GROUP_ROWS, 1), 0)
    col = jnp.zeros((GROUP_ROWS, 1), F32) + sink_ref[0, hk * SWA_GROUP]
    for g in range(1, SWA_GROUP):
        col = jnp.where(row >= g * BLOCK, sink_ref[0, hk * SWA_GROUP + g], col)
    return col


def _swa_probs(qg, kh, vis, bias, sink):
    s = _nt(qg, kh) * (HEAD_DIM ** -0.5)
    s = jnp.where(vis, s + bias, NEG)
    m = jnp.maximum(jnp.max(s, axis=-1, keepdims=True), sink)
    p = jnp.exp(s - m)
    ps = jnp.exp(sink - m)
    inv = 1.0 / (jnp.sum(p, axis=-1, keepdims=True) + ps)
    return p * inv, ps * inv


def _swa_fwd(qkv, pos_col, pos_row, bias_t, sinks, hook=None):
    t = qkv.shape[0]
    nb = t // BLOCK
    qw = SWA_HEADS * HEAD_DIM
    kw = SWA_KV_HEADS * HEAD_DIM

    def body(q_ref, kp_ref, kc_ref, vp_ref, vc_ref, pq_ref, pkp_ref, pkc_ref, bias_ref, sink_ref, o_ref):
        b = pl.program_id(0)
        vis = _swa_visible(b, pq_ref, pkp_ref, pkc_ref)
        k2 = jnp.concatenate([kp_ref[...], kc_ref[...]], axis=0)
        v2 = jnp.concatenate([vp_ref[...], vc_ref[...]], axis=0)
        for hk in range(SWA_KV_HEADS):
            kh = k2[:, hk * HEAD_DIM:(hk + 1) * HEAD_DIM]
            vh = v2[:, hk * HEAD_DIM:(hk + 1) * HEAD_DIM]
            pn, _ = _swa_probs(_group_heads(q_ref, hk), kh, vis, bias_ref[hk], _group_sinks(sink_ref, hk))
            o = _nn(pn.astype(BF), vh)
            for g in range(SWA_GROUP):
                h = hk * SWA_GROUP + g
                o_ref[:, h * HEAD_DIM:(h + 1) * HEAD_DIM] = o[g * BLOCK:(g + 1) * BLOCK]

    prev = lambda b: jnp.maximum(b - 1, 0)
    return _call(
        body, name="swa_fwd", grid=(nb,), out_shape=(S((t, qw), F32),),
        in_specs=[
            pl.BlockSpec((BLOCK, qw), lambda b: (b, 0)),
            pl.BlockSpec((BLOCK, kw), lambda b: (prev(b), qw // kw)),
            pl.BlockSpec((BLOCK, kw), lambda b: (b, qw // kw)),
            pl.BlockSpec((BLOCK, kw), lambda b: (prev(b), qw // kw + 1)),
            pl.BlockSpec((BLOCK, kw), lambda b: (b, qw // kw + 1)),
            pl.BlockSpec((BLOCK, 1), lambda b: (b, 0)),
            pl.BlockSpec((1, BLOCK), lambda b: (0, prev(b))),
            pl.BlockSpec((1, BLOCK), lambda b: (0, b)),
            _resident(bias_t.shape),
            SMEM_SPEC,
        ],
        out_specs=(pl.BlockSpec((BLOCK, qw), lambda b: (b, 0)),),
        sem=("parallel",), args=(qkv, qkv, qkv, qkv, qkv, pos_col, pos_row, pos_row, bias_t, sinks), hook=hook)[0]


HALO = 16


def _conv_taps(z, zh, first):
    tm = z.shape[0]
    zh = jnp.where(first, 0.0, zh)
    row = lax.broadcasted_iota(I32, (tm, 1), 0)
    z1 = jnp.where(row == 0, zh[HALO - 1:HALO, :], pltpu.roll(z, 1, 0))
    z2 = jnp.where(row == 0, zh[HALO - 2:HALO - 1, :], jnp.where(row == 1, zh[HALO - 1:HALO, :], pltpu.roll(z, 2, 0)))
    return z1, z2


def _mix_out_fwd(e, attn, conv_w, w_out, x, hook=None):
    t, d = x.shape
    tm = 256
    hb = tm // HALO
    f32 = lambda ref: ref[...].astype(F32)

    def body(c_ref, b_ref, u_ref, ga_ref, gc_ref, ch_ref, uh_ref, attn_ref, cw_ref, w_ref, x_ref, xo_ref, mg_ref):
        i = pl.program_id(0)
        z = f32(c_ref) * f32(u_ref)
        z1, z2 = _conv_taps(z, f32(ch_ref) * f32(uh_ref), i == 0)
        s = cw_ref[0:1, :] * z2 + cw_ref[1:2, :] * z1 + cw_ref[2:3, :] * z
        conv = f32(b_ref) * s
        merged = (_sigmoid(f32(ga_ref)) * attn_ref[...] + _sigmoid(f32(gc_ref)) * conv).astype(BF)
        mg_ref[...] = merged
        xo_ref[...] = x_ref[...] + _nn(merged, w_ref[...])

    ecol = lambda cb: pl.BlockSpec((tm, d), lambda i: (i, cb))
    halo = lambda cb: pl.BlockSpec((HALO, d), lambda i: (jnp.maximum(i * hb - 1, 0), cb))
    row = pl.BlockSpec((tm, d), lambda i: (i, 0))
    return _call(
        body, name="mix_out_fwd", grid=(t // tm,),
        out_shape=(S((t, d), F32), S((t, d), BF)),
        in_specs=[ecol(0), ecol(1), ecol(2), ecol(3), ecol(4), halo(0), halo(2), row,
                  _resident(conv_w.shape), _resident(w_out.shape), row],
        out_specs=(row, row),
        sem=("parallel",), args=(e, e, e, e, e, e, e, attn, conv_w, w_out, x), hook=hook)


def _mem_kv(mem, gm, wkv):
    m, d = mem.shape

    def body(mem_ref, gm_ref, w_ref, mh_ref, kv_ref):
        n, _ = _rms(mem_ref[...])
        mh = (n * gm_ref[...]).astype(BF)
        mh_ref[...] = mh
        kv_ref[...] = _nn(mh, w_ref[...]).astype(BF)

    return pl.pallas_call(
        body, name="mem_kv", out_shape=(S((m, d), BF), S((m, wkv.shape[1]), BF)),
        compiler_params=_params(),
    )(mem, gm, wkv)


def _xattn_probs(qh, kh):
    s = _nt(qh, kh) * (kh.shape[1] ** -0.5)
    p = jnp.exp(s - jnp.max(s, axis=-1, keepdims=True))
    return p * (1.0 / jnp.sum(p, axis=-1, keepdims=True))


def _xattn_fwd(x, gn, wq, kv, wo):
    t, d = x.shape
    tm = 256
    hd = d // MEM_HEADS

    def body(x_ref, gn_ref, wq_ref, kv_ref, wo_ref, xo_ref, q_ref, o_ref):
        xv = x_ref[...]
        n, _ = _rms(xv)
        q = _nn((n * gn_ref[...]).astype(BF), wq_ref[...]).astype(BF)
        q_ref[...] = q
        outs = []
        for hh in range(MEM_HEADS):
            p = _xattn_probs(q[:, hh * hd:(hh + 1) * hd], kv_ref[:, hh * hd:(hh + 1) * hd])
            outs.append(_nn(p.astype(BF), kv_ref[:, d + hh * hd:d + (hh + 1) * hd]))
        o = jnp.concatenate(outs, axis=1).astype(BF)
        o_ref[...] = o
        xo_ref[...] = xv + _nn(o, wo_ref[...])

    row = pl.BlockSpec((tm, d), lambda i: (i, 0))
    return pl.pallas_call(
        body, name="xattn_fwd", grid=(t // tm,),
        out_shape=(S((t, d), F32), S((t, d), BF), S((t, d), BF)),
        in_specs=[row, _resident((1, d)), _resident(wq.shape), _resident(kv.shape), _resident(wo.shape)],
        out_specs=(row, row, row),
        compiler_params=_params(("parallel",)),
    )(x, gn, wq, kv, wo)


def _ffn_bwd(dxo, x, gn, g, u, wgu, wd, name, hook=None):
    t, d = x.shape
    f = wd.shape[0]
    tm, fc = 256, 1408

    def body(dxo_ref, x_ref, gn_ref, g_ref, u_ref, wgu_ref, wd_ref, dx_ref, dgn_ref, dgu_ref, a_ref, h_ref, dyh_ref):
        @pl.when(pl.program_id(0) == 0)
        def _():
            dgn_ref[...] = jnp.zeros_like(dgn_ref)
        dxov = dxo_ref[...]
        dyh = (0.5 * dxov).astype(BF)
        dyh_ref[...] = dyh
        n, r = _rms(x_ref[...])
        gnv = gn_ref[...]
        h_ref[...] = (n * gnv).astype(BF)
        dh = jnp.zeros((tm, d), F32)
        for c0 in range(0, f, fc):
            gv = g_ref[:, c0:c0 + fc].astype(F32)
            uv = u_ref[:, c0:c0 + fc].astype(F32)
            da = _nt(dyh, wd_ref[c0:c0 + fc, :])
            sg = _sigmoid(gv)
            silu = gv * sg
            a_ref[:, c0:c0 + fc] = (silu * uv).astype(BF)
            dg = (da * uv * (sg * (1.0 + gv * (1.0 - sg)))).astype(BF)
            du = (da * silu).astype(BF)
            dgu_ref[:, c0:c0 + fc] = dg
            dgu_ref[:, f + c0:f + c0 + fc] = du
            dh = dh + _nt(dg, wgu_ref[:, c0:c0 + fc]) + _nt(du, wgu_ref[:, f + c0:f + c0 + fc])
        dx, dgn = _rms_bwd(dh, n, r, gnv)
        dx_ref[...] = dxov + dx
        dgn_ref[...] += dgn

    row = pl.BlockSpec((tm, d), lambda i: (i, 0))
    frow = pl.BlockSpec((tm, f), lambda i: (i, 0))
    return _call(
        body, name=name, grid=(t // tm,),
        out_shape=(S((t, d), F32), S((1, d), F32), S((t, 2 * f), BF), S((t, f), BF), S((t, d), BF), S((t, d), BF)),
        in_specs=[row, row, _resident((1, d)), frow, frow, _resident(wgu.shape), _resident(wd.shape)],
        out_specs=(row, _acc_spec((1, d)), pl.BlockSpec((tm, 2 * f), lambda i: (i, 0)), frow, row, row),
        sem=("arbitrary",), args=(dxo, x, gn, g, u, wgu, wd), hook=hook)


def _dw(a, b, tn, name):
    t, ka = a.shape
    nb = b.shape[1]
    tt = 1024
    nt = t // tt

    def body(a_ref, b_ref, o_ref, acc_ref):
        k = pl.program_id(1)

        @pl.when(k == 0)
        def _():
            acc_ref[...] = jnp.zeros_like(acc_ref)
        acc_ref[...] += _tn(a_ref[...], b_ref[...])

        @pl.when(k == nt - 1)
        def _():
            o_ref[...] = acc_ref[...].astype(BF)

    return pl.pallas_call(
        body, name=name, grid=(nb // tn, nt), out_shape=S((ka, nb), BF),
        in_specs=[pl.BlockSpec((tt, ka), lambda j, k: (k, 0)), pl.BlockSpec((tt, tn), lambda j, k: (k, j))],
        out_specs=pl.BlockSpec((ka, tn), lambda j, k: (0, j)), scratch_shapes=[pltpu.VMEM((ka, tn), F32)],
        compiler_params=_params(("parallel", "arbitrary")),
    )(a, b)


def _dw_pair(pairs, tn, name, kind, hook=None):
    npairs = len(pairs)
    t, ka = pairs[0][0].shape
    nb = pairs[0][1].shape[1]
    tt = 2048 if npairs == 1 else 1024
    nt, nj = t // tt, nb // tn
    col = kind == "col"
    rh = ka // 2 if col else ka // 8
    tile = (rh, tn) if col else (4, rh, tn)
    half = (rh, nb) if col else (4, rh, nb)
    lead = (slice(None),) * (len(tile) - 1)

    def body(*refs):
        ins, mines, sibs = refs[:2 * npairs], refs[2 * npairs:3 * npairs], refs[3 * npairs:4 * npairs]
        acc_ref, stage, ssem, rsem = refs[4 * npairs:]
        j, k = pl.program_id(0), pl.program_id(1)
        x, y, c = _place()
        sibling = (x, y, 1 - c)

        def send(p, slot, jj):
            dst = sibs[p].at[lead + (pl.ds(pl.multiple_of(jj * tn, 128), tn),)]
            return _remote(stage.at[slot], dst, ssem.at[slot], rsem.at[p], sibling)

        def rows(s, whose):
            return acc_ref[pl.ds(pl.multiple_of(s * 2 * rh + whose * rh, 16), rh), :].astype(BF)

        def step(p):
            a_ref, b_ref, mine_ref = ins[2 * p], ins[2 * p + 1], mines[p]

            @pl.when(k == 0)
            def _():
                acc_ref[...] = jnp.zeros_like(acc_ref)
            acc_ref[...] += _tn(a_ref[...], b_ref[...])

            @pl.when(k == nt - 1)
            def _():
                slot = j % 2

                @pl.when(j >= 2)
                def _():
                    send(p, slot, 0).wait_send()
                if col:
                    mine_ref[...] = rows(0, c)
                    stage[slot] = rows(0, 1 - c)
                else:
                    for s in range(4):
                        mine_ref[s] = rows(s, c)
                        stage[slot, s] = rows(s, 1 - c)
                send(p, slot, j - p * nj).start()

        for p in range(npairs):
            pl.when(j // nj == p)(functools.partial(step, p))

        @pl.when(jnp.logical_and(j == npairs * nj - 1, k == nt - 1))
        def _():
            for jj in range(max(npairs * nj - 2, 0), npairs * nj):
                send(0, jj % 2, 0).wait_send()
            for p in range(npairs):
                _remote(sibs[p], sibs[p], ssem.at[0], rsem.at[p], sibling).wait_recv()

    in_specs, args = [], []
    for p, (a, b) in enumerate(pairs):
        on = lambda j, p=p: j // nj == p
        in_specs += [pl.BlockSpec((tt, ka), lambda j, k, on=on: (jnp.where(on(j), k, 0), 0)),
                     pl.BlockSpec((tt, tn), lambda j, k, on=on, p=p: (jnp.where(on(j), k, 0), jnp.clip(j - p * nj, 0, nj - 1)))]
        args += [a, b]
    mine_spec = lambda p: pl.BlockSpec(tile, (lambda j, k: (0, jnp.clip(j - p * nj, 0, nj - 1))) if col
                                       else (lambda j, k: (0, 0, jnp.clip(j - p * nj, 0, nj - 1))))
    res = _call(
        body, name=name, grid=(npairs * nj, nt), out_shape=(S(half, BF),) * (2 * npairs),
        in_specs=in_specs, out_specs=tuple(mine_spec(p) for p in range(npairs)) + (ANY,) * npairs,
        scratch_shapes=[pltpu.VMEM((ka, tn), F32), pltpu.VMEM((2,) + tile, BF), pltpu.SemaphoreType.DMA((2,)),
                        pltpu.SemaphoreType.DMA((npairs,))],
        sem=("arbitrary", "arbitrary"), args=args, hook=hook)
    return [(res[p], res[npairs + p]) for p in range(npairs)]


def _xattn_bwd(dxo, x, gn, q, kv, wq, wo, hook=None):
    t, d = x.shape
    tm = 256
    hd = d // MEM_HEADS
    nkv = kv.shape[0]

    def body(dxo_ref, x_ref, gn_ref, q_ref, kv_ref, wq_ref, wo_ref, dx_ref, dgn_ref, dkv_ref, dxh_ref, h_ref, dq_ref):
        @pl.when(pl.program_id(0) == 0)
        def _():
            dgn_ref[...] = jnp.zeros_like(dgn_ref)
            dkv_ref[...] = jnp.zeros_like(dkv_ref)
        dxov = dxo_ref[...]
        dxh = dxov.astype(BF)
        dxh_ref[...] = dxh
        do = _nt(dxh, wo_ref[...]).astype(BF)
        dqs = []
        for hh in range(MEM_HEADS):
            lo, hi = hh * hd, (hh + 1) * hd
            qh = q_ref[:, lo:hi]
            kh = kv_ref[:, lo:hi]
            vh = kv_ref[:, d + lo:d + hi]
            doh = do[:, lo:hi]
            p = _xattn_probs(qh, kh)
            dp = _nt(doh, vh)
            ds = (p * (dp - jnp.sum(p * dp, axis=-1, keepdims=True)) * (hd ** -0.5)).astype(BF)
            dqs.append(_nn(ds, kh))
            dkv_ref[:, lo:hi] += _tn(ds, qh)
            dkv_ref[:, d + lo:d + hi] += _tn(p.astype(BF), doh)
        dq = jnp.concatenate(dqs, axis=1).astype(BF)
        dq_ref[...] = dq
        n, r = _rms(x_ref[...])
        gnv = gn_ref[...]
        h_ref[...] = (n * gnv).astype(BF)
        dx, dgn = _rms_bwd(_nt(dq, wq_ref[...]), n, r, gnv)
        dx_ref[...] = dxov + dx
        dgn_ref[...] += dgn

    row = pl.BlockSpec((tm, d), lambda i: (i, 0))
    return _call(
        body, name="xattn_bwd", grid=(t // tm,),
        out_shape=(S((t, d), F32), S((1, d), F32), S((nkv, 2 * d), F32), S((t, d), BF), S((t, d), BF), S((t, d), BF)),
        in_specs=[row, row, _resident((1, d)), row, _resident(kv.shape), _resident(wq.shape), _resident(wo.shape)],
        out_specs=(row, _acc_spec((1, d)), _acc_spec((nkv, 2 * d)), row, row, row),
        sem=("arbitrary",), args=(dxo, x, gn, q, kv, wq, wo), hook=hook)


def _mem_bwd(dkv, mh, mem, gm, wkv, pair=False):
    m, d = mem.shape
    rows, cols = wkv.shape
    rh = rows // 2

    def body(dkv_ref, mh_ref, mem_ref, gm_ref, w_ref, *outs):
        dkvb = dkv_ref[...].astype(BF)
        dw = _tn(mh_ref[...], dkvb).astype(BF)
        dmh = _nt(dkvb, w_ref[...])
        n, _ = _rms(mem_ref[...])
        if not pair:
            dw_ref, dgm_ref = outs
            dw_ref[...] = dw
        else:
            mine_ref, sib_ref, dgm_ref, whole, ssem, rsem = outs
            x, y, c = _place()
            whole[...] = dw
            cp = _remote(whole.at[pl.ds(pl.multiple_of((1 - c) * rh, 16), rh), :], sib_ref, ssem, rsem, (x, y, 1 - c))
            cp.start()
            mine_ref[...] = whole[pl.ds(pl.multiple_of(c * rh, 16), rh), :]
            cp.wait()
        dgm_ref[...] = jnp.sum(dmh * n, axis=0, keepdims=True)

    if not pair:
        return pl.pallas_call(body, name="mem_bwd", out_shape=(S(wkv.shape, BF), S((1, d), F32)),
                              compiler_params=_params())(dkv, mh, mem, gm, wkv)
    mine, sib, dgm = pl.pallas_call(
        body, name="mem_bwd", out_shape=(S((rh, cols), BF), S((rh, cols), BF), S((1, d), F32)),
        in_specs=[VMEM_SPEC] * 5, out_specs=(VMEM_SPEC, ANY, VMEM_SPEC),
        scratch_shapes=[pltpu.VMEM((rows, cols), BF), pltpu.SemaphoreType.DMA, pltpu.SemaphoreType.DMA],
        compiler_params=_params())(dkv, mh, mem, gm, wkv)
    return (mine, sib), dgm


def _mix_out_bwd(dxo, e, attn, conv_w, w_out, nqkv, hook=None):
    t, d = attn.shape
    tm = 256
    nt = t // tm
    f32 = lambda ref: ref[...].astype(F32)

    def body(dxo_ref, dxn_ref, c_ref, b_ref, u_ref, ga_ref, gc_ref, ch_ref, uh_ref, bn_ref, gcn_ref,
             attn_ref, cw_ref, w_ref, dattn_ref, de_ref, dcw_ref, dxh_ref):
        i = pl.program_id(0)

        @pl.when(i == 0)
        def _():
            dcw_ref[...] = jnp.zeros_like(dcw_ref)
        dxh = dxo_ref[...].astype(BF)
        dxh_ref[...] = dxh
        w = w_ref[...]
        dm = _nt(dxh, w)
        dmn = _nt(dxn_ref[...].astype(BF), w)
        cv, bv, uv = f32(c_ref), f32(b_ref), f32(u_ref)
        sga = _sigmoid(f32(ga_ref))
        sgc = _sigmoid(f32(gc_ref))
        z = cv * uv
        z1, z2 = _conv_taps(z, f32(ch_ref) * f32(uh_ref), i == 0)
        w0, w1, w2 = cw_ref[0:1, :], cw_ref[1:2, :], cw_ref[2:3, :]
        s = w0 * z2 + w1 * z1 + w2 * z
        av = attn_ref[...]
        dattn_ref[...] = (dm * sga).astype(BF)
        dconv = dm * sgc
        ds = dconv * bv
        dsn = jnp.where(i == nt - 1, 0.0, dmn * _sigmoid(gcn_ref[0:8, :].astype(F32)) * bn_ref[0:8, :].astype(F32))
        row = lax.broadcasted_iota(I32, (tm, 1), 0)
        dsp1 = jnp.where(row == tm - 1, dsn[0:1, :], pltpu.roll(ds, tm - 1, 0))
        dsp2 = jnp.where(row == tm - 2, dsn[0:1, :], jnp.where(row == tm - 1, dsn[1:2, :], pltpu.roll(ds, tm - 2, 0)))
        dz = w2 * ds + w1 * dsp1 + w0 * dsp2
        de_ref[:, nqkv:nqkv + d] = (dz * uv).astype(BF)
        de_ref[:, nqkv + d:nqkv + 2 * d] = (dconv * s).astype(BF)
        de_ref[:, nqkv + 2 * d:nqkv + 3 * d] = (dz * cv).astype(BF)
        de_ref[:, nqkv + 3 * d:nqkv + 4 * d] = (dm * av * sga * (1.0 - sga)).astype(BF)
        de_ref[:, nqkv + 4 * d:nqkv + 5 * d] = (dm * (bv * s) * sgc * (1.0 - sgc)).astype(BF)
        dcw_ref[0:1, :] += jnp.sum(ds * z2, axis=0, keepdims=True)
        dcw_ref[1:2, :] += jnp.sum(ds * z1, axis=0, keepdims=True)
        dcw_ref[2:3, :] += jnp.sum(ds * z, axis=0, keepdims=True)

    ecol = lambda cb: pl.BlockSpec((tm, d), lambda i: (i, cb))
    prev = lambda cb: pl.BlockSpec((HALO, d), lambda i: (jnp.maximum(i * (tm // HALO) - 1, 0), cb))
    nxt = lambda rows, cb: pl.BlockSpec((rows, d), lambda i: (jnp.minimum((i + 1) * (tm // rows), t // rows - 1), cb))
    row = pl.BlockSpec((tm, d), lambda i: (i, 0))
    return _call(
        body, name="mix_out_bwd", grid=(nt,),
        out_shape=(S((t, d), BF), S((t, nqkv + 5 * d), BF), S((8, d), F32), S((t, d), BF)),
        in_specs=[row, nxt(8, 0), ecol(0), ecol(1), ecol(2), ecol(3), ecol(4), prev(0), prev(2), nxt(HALO, 1), nxt(HALO, 4),
                  row, _resident(conv_w.shape), _resident(w_out.shape)],
        out_specs=(row, pl.BlockSpec((tm, nqkv + 5 * d), lambda i: (i, 0)), _acc_spec((8, d)), row),
        sem=("arbitrary",), args=(dxo, dxo, e, e, e, e, e, e, e, e, e, attn, conv_w, w_out), hook=hook)


def _swa_bwd(qkv, dattn, pos_col, pos_row, bias_t, sinks, dp, hook=None):
    t = qkv.shape[0]
    nb = t // BLOCK
    qw = SWA_HEADS * HEAD_DIM
    kw = SWA_KV_HEADS * HEAD_DIM

    def body(q_ref, kp_ref, kc_ref, vp_ref, vc_ref, do_ref, pq_ref, pkp_ref, pkc_ref, bias_ref, sink_ref,
             dp_in, dp_ref, gb_ref, dsk_ref, carry_ref, dq_ref):
        b = pl.program_id(0)

        @pl.when(b == 0)
        def _():
            gb_ref[...] = jnp.zeros_like(gb_ref)
            dsk_ref[...] = jnp.zeros_like(dsk_ref)
            carry_ref[...] = jnp.zeros_like(carry_ref)
            dq_ref[...] = jnp.zeros_like(dq_ref)
        dp_ref[:, 0:qw] = dq_ref[...]

        @pl.when(b < nb)
        def _():
            vis = _swa_visible(b, pq_ref, pkp_ref, pkc_ref)
            k2 = jnp.concatenate([kp_ref[...], kc_ref[...]], axis=0)
            v2 = jnp.concatenate([vp_ref[...], vc_ref[...]], axis=0)
            for hk in range(SWA_KV_HEADS):
                lo, hi = hk * HEAD_DIM, (hk + 1) * HEAD_DIM
                kh = k2[:, lo:hi]
                vh = v2[:, lo:hi]
                qg = _group_heads(q_ref, hk)
                dog = _group_heads(do_ref, hk)
                pn, psn = _swa_probs(qg, kh, vis, bias_ref[hk], _group_sinks(sink_ref, hk))
                dp = _nt(dog, vh)
                delta = jnp.sum(pn * dp, axis=-1, keepdims=True)
                ds = pn * (dp - delta)
                gb_ref[hk] += ds
                dsk_ref[hk] += -psn * delta
                dsb = (ds * (HEAD_DIM ** -0.5)).astype(BF)
                dqg = _nn(dsb, kh).astype(BF)
                for g in range(SWA_GROUP):
                    h = hk * SWA_GROUP + g
                    dq_ref[:, h * HEAD_DIM:(h + 1) * HEAD_DIM] = dqg[g * BLOCK:(g + 1) * BLOCK]
                dk = _tn(dsb, qg)
                dv = _tn(pn.astype(BF), dog)
                dp_ref[:, qw + lo:qw + hi] = (carry_ref[:, lo:hi] + dk[0:BLOCK]).astype(BF)
                dp_ref[:, qw + kw + lo:qw + kw + hi] = (carry_ref[:, kw + lo:kw + hi] + dv[0:BLOCK]).astype(BF)
                carry_ref[:, lo:hi] = dk[BLOCK:2 * BLOCK]
                carry_ref[:, kw + lo:kw + hi] = dv[BLOCK:2 * BLOCK]

        @pl.when(b == nb)
        def _():
            dp_ref[:, qw:qw + 2 * kw] = carry_ref[...].astype(BF)

    cur = lambda b: jnp.minimum(b, nb - 1)
    prev = lambda b: jnp.maximum(cur(b) - 1, 0)
    return _call(
        body, name="swa_bwd", grid=(nb + 1,),
        out_shape=(S(dp.shape, BF), S((SWA_KV_HEADS, GROUP_ROWS, 2 * BLOCK), F32), S((SWA_KV_HEADS, GROUP_ROWS, 1), F32)),
        in_specs=[
            pl.BlockSpec((BLOCK, qw), lambda b: (cur(b), 0)),
            pl.BlockSpec((BLOCK, kw), lambda b: (prev(b), qw // kw)),
            pl.BlockSpec((BLOCK, kw), lambda b: (cur(b), qw // kw)),
            pl.BlockSpec((BLOCK, kw), lambda b: (prev(b), qw // kw + 1)),
            pl.BlockSpec((BLOCK, kw), lambda b: (cur(b), qw // kw + 1)),
            pl.BlockSpec((BLOCK, qw), lambda b: (cur(b), 0)),
            pl.BlockSpec((BLOCK, 1), lambda b: (cur(b), 0)),
            pl.BlockSpec((1, BLOCK), lambda b: (0, prev(b))),
            pl.BlockSpec((1, BLOCK), lambda b: (0, cur(b))),
            _resident(bias_t.shape),
            SMEM_SPEC,
            ANY,
        ],
        out_specs=(
            pl.BlockSpec((BLOCK, qw + 2 * kw), lambda b: (jnp.maximum(b - 1, 0), 0)),
            _acc_spec((SWA_KV_HEADS, GROUP_ROWS, 2 * BLOCK)),
            _acc_spec((SWA_KV_HEADS, GROUP_ROWS, 1)),
        ),
        scratch_shapes=[pltpu.VMEM((BLOCK, 2 * kw), F32), pltpu.VMEM((BLOCK, qw), BF)],
        aliases={11: 0}, sem=("arbitrary",),
        args=(qkv, qkv, qkv, qkv, qkv, dattn, pos_col, pos_row, pos_row, bias_t, sinks, dp), hook=hook)


def _bias_reduce(gb, dsk):
    def body(gb_ref, dsk_ref, drb_ref, dsink_ref):
        bucket = _t5_bucket(_block_rel())
        for b in range(REL_BUCKETS):
            mask = bucket == b
            for h in range(SWA_HEADS):
                drb_ref[b, h] = jnp.sum(jnp.where(mask, gb_ref[h], 0.0))
        for h in range(SWA_HEADS):
            dsink_ref[0, h] = jnp.sum(dsk_ref[h])

    return pl.pallas_call(
        body, name="bias_reduce", out_shape=(S((REL_BUCKETS, SWA_HEADS), F32), S((1, SWA_HEADS), F32)),
        in_specs=[VMEM_SPEC, VMEM_SPEC], out_specs=(SMEM_SPEC, SMEM_SPEC),
    )(gb, dsk)


def _mix_in_bwd(dp, x, gn, w_in, dxo):
    t, d = x.shape
    tm = 256
    npr = dp.shape[1]

    def body(dp_ref, x_ref, gn_ref, w_ref, dxo_ref, dx_ref, dgn_ref, h_ref):
        @pl.when(pl.program_id(0) == 0)
        def _():
            dgn_ref[...] = jnp.zeros_like(dgn_ref)
        dh = jnp.zeros((tm, d), F32)
        for c0 in range(0, npr, 1024):
            c1 = min(c0 + 1024, npr)
            dh = dh + _nt(dp_ref[:, c0:c1], w_ref[:, c0:c1])
        n, r = _rms(x_ref[...])
        gnv = gn_ref[...]
        h_ref[...] = (n * gnv).astype(BF)
        dx, dgn = _rms_bwd(dh, n, r, gnv)
        dx_ref[...] = dxo_ref[...] + dx
        dgn_ref[...] += dgn

    row = pl.BlockSpec((tm, d), lambda i: (i, 0))
    return pl.pallas_call(
        body, name="mix_in_bwd", grid=(t // tm,),
        out_shape=(S((t, d), F32), S((1, d), F32), S((t, d), BF)),
        in_specs=[pl.BlockSpec((tm, npr), lambda i: (i, 0)), row, _resident((1, d)), _resident(w_in.shape), row],
        out_specs=(row, _acc_spec((1, d)), row),
        compiler_params=_params(("arbitrary",)),
    )(dp, x, gn, w_in, dxo)


PAIR_STEPS = 4


def _add_bf16(pairs, name):
    def body(*refs):
        ins, outs = refs[:2 * len(pairs)], refs[2 * len(pairs):]
        for q, o_ref in enumerate(outs):
            o_ref[...] = (ins[2 * q][...].astype(F32) + ins[2 * q + 1][...].astype(F32)).astype(BF)

    in_specs, out_specs, out_shape, args = [], [], [], []
    for a, b in pairs:
        rows, cols = a.shape
        blk = pl.BlockSpec((rows // PAIR_STEPS, cols), lambda i: (i, 0))
        in_specs += [blk, blk]
        out_specs.append(blk)
        out_shape.append(S((rows, cols), BF))
        args += [a, b]
    return pl.pallas_call(body, name=name, grid=(PAIR_STEPS,), out_shape=tuple(out_shape), in_specs=in_specs,
                          out_specs=tuple(out_specs), compiler_params=_params(("parallel",)))(*args)


def _adam_update(w, g, m, v):
    mn = ADAM_B1 * m + (1.0 - ADAM_B1) * g
    vn = ADAM_B2 * v + (1.0 - ADAM_B2) * (g * g)
    m_hat = mn / (1.0 - ADAM_B1 ** ADAM_STEP)
    v_hat = vn / (1.0 - ADAM_B2 ** ADAM_STEP)
    return -ADAM_LR * (m_hat / (jnp.sqrt(v_hat) + ADAM_EPS) + ADAM_WD * w), mn, vn


def _adamw(quads, name, steps, echo=False, hook=None):
    n_out = 4 if echo else 3

    def body(*refs):
        ins, outs = refs[:4 * len(quads)], refs[4 * len(quads):]
        for q in range(len(quads)):
            w_ref, g_ref, m_ref, v_ref = ins[4 * q:4 * q + 4]
            res = outs[n_out * q:n_out * q + n_out]
            gv = g_ref[...]
            if echo:
                res[0][...] = gv
            res[-3][...], res[-2][...], res[-1][...] = _adam_update(w_ref[...], gv, m_ref[...], v_ref[...])

    in_specs, out_specs, out_shape, args = [], [], [], []
    for quad in quads:
        rows, cols = quad[0].shape
        blk = pl.BlockSpec((rows // steps, cols), lambda i: (i, 0))
        in_specs += [blk] * 4
        out_specs += [blk] * n_out
        out_shape += [S((rows, cols), F32)] * n_out
        args += list(quad)
    res = _call(body, name=name, grid=(steps,), out_shape=tuple(out_shape), in_specs=in_specs, out_specs=tuple(out_specs),
                sem=("parallel",), args=args, hook=hook)
    return [res[n_out * q:n_out * q + n_out] for q in range(len(quads))]


def _place():
    x, y, c = lax.axis_index("x"), lax.axis_index("y"), lax.axis_index("c")
    return x, y, c


OTHER_CHIPS = ((1, 0), (0, 1), (1, 1))


def _flip(v, f):
    return 1 - v if f else v


def _remote(src, dst, ssem, rsem, dev):
    return pltpu.make_async_remote_copy(src_ref=src, dst_ref=dst, send_sem=ssem, recv_sem=rsem,
                                        device_id=dev, device_id_type=MESH)


class _Both:
    def __init__(self, hooks):
        self.hooks = hooks
        self.ins = [a for h in hooks for a in h.ins]
        self.out_shape = tuple(o for h in hooks for o in h.out_shape)
        self.scratch = [x for h in hooks for x in h.scratch]

    def _each(self, ins, outs, scr):
        i = o = s = 0
        for h in self.hooks:
            ni, no, ns = len(h.ins), len(h.out_shape), len(h.scratch)
            yield h, ins[i:i + ni], outs[o:o + no], scr[s:s + ns]
            i, o, s = i + ni, o + no, s + ns

    def start(self, ins, outs, scr):
        for h, *refs in self._each(ins, outs, scr):
            h.start(*refs)

    def finish(self, ins, outs, scr):
        for h, *refs in self._each(ins, outs, scr):
            h.finish(*refs)

    @property
    def results(self):
        return [h.results for h in self.hooks]

    @results.setter
    def results(self, res):
        for h, _, mine, _ in self._each((), tuple(res), ()):
            h.results = mine


def _exchange(hook, name):
    n_in, n_out = len(hook.ins), len(hook.out_shape)

    def body(*refs):
        ins, outs, scr = refs[:n_in], refs[n_in:n_in + n_out], refs[n_in + n_out:]
        hook.start(ins, outs, scr)
        hook.finish(ins, outs, scr)

    hook.results = pl.pallas_call(
        body, name=name, out_shape=tuple(hook.out_shape), in_specs=[ANY] * n_in, out_specs=(ANY,) * n_out,
        scratch_shapes=list(hook.scratch), compiler_params=_params(),
    )(*hook.ins)


class _GatherHook:
    def __init__(self, shards, kinds):
        self.ins, self.kinds, n = list(shards), list(kinds), len(shards)
        self.out_shape = tuple(
            S((w.shape[0], 4 * w.shape[1]), BF) if k == "col" else S((4,) + w.shape, BF) for w, k in zip(shards, kinds))
        dma = pltpu.SemaphoreType.DMA
        self.scratch = ([dma((n, 3)) for _ in range(5)] + [dma((n,)), dma((n,))]
                        + [pltpu.VMEM((3, w.shape[0] // 2, w.shape[1]), BF) for w in shards]
                        + [pltpu.VMEM(w.shape, BF) for w in shards])

    def _window(self, outs, i, s, half):
        rows, cols = self.ins[i].shape
        rh = rows // 2
        start = pl.multiple_of(half * rh, 16)
        if self.kinds[i] == "col":
            return outs[i].at[pl.ds(start, rh), pl.ds(pl.multiple_of(s * cols, 128), cols)]
        return outs[i].at[s, pl.ds(start, rh), :]

    def _copies(self, ins, outs, scr):
        n = len(ins)
        ssem, rsem, fssem, frsem, ksem, lsem, osem = scr[:7]
        land, own = scr[7:7 + n], scr[7 + n:7 + 2 * n]
        x, y, c = _place()
        sibling = (x, y, 1 - c)
        loads, stores, sends, forwards, keeps, passed = [], [], [], [], [], []
        for i in range(n):
            rows, cols = self.ins[i].shape
            rh = rows // 2
            mine = (outs[i].at[:, pl.ds(pl.multiple_of((2 * x + y) * cols, 128), cols)] if self.kinds[i] == "col"
                    else outs[i].at[2 * x + y])
            loads.append(functools.partial(pltpu.make_async_copy, ins[i], own[i], lsem.at[i]))
            stores.append(functools.partial(pltpu.make_async_copy, own[i], mine, osem.at[i]))
            src = ins[i].at[pl.ds(pl.multiple_of(c * rh, 16), rh), :]
            for j, (fx, fy) in enumerate(OTHER_CHIPS):
                px, py = _flip(x, fx), _flip(y, fy)
                sends.append(functools.partial(_remote, src, land[i].at[j], ssem.at[i, j], rsem.at[i, j], (px, py, c)))
                here = self._window(outs, i, 2 * px + py, c)
                forwards.append(functools.partial(_remote, land[i].at[j], here, fssem.at[i, j], frsem.at[i, j], sibling))
                keeps.append(functools.partial(pltpu.make_async_copy, land[i].at[j], here, ksem.at[i, j]))
                there = self._window(outs, i, 2 * px + py, 1 - c)
                passed.append(functools.partial(_remote, there, there, fssem.at[i, j], frsem.at[i, j], sibling))
        return loads, stores, sends, forwards, keeps, passed

    def start(self, ins, outs, scr):
        loads, _, sends, _, _, _ = self._copies(ins, outs, scr)
        for make in sends + loads:
            make().start()

    def finish(self, ins, outs, scr):
        loads, stores, sends, forwards, keeps, passed = self._copies(ins, outs, scr)
        for load, store in zip(loads, stores):
            load().wait()
            store().start()
        for send, forward, keep in zip(sends, forwards, keeps):
            send().wait_recv()
            forward().start()
            keep().start()
        for make in passed:
            make().wait_recv()
        for make in sends + forwards:
            make().wait_send()
        for make in keeps + stores:
            make().wait()


class _ChipsHook:
    def __init__(self, parts, kinds):
        self.ins, self.kinds, n = list(parts), list(kinds), len(parts)
        self.out_shape = tuple(
            S((4, p.shape[0], p.shape[1] // 4), BF) if k == "col" else S(p.shape, BF) for p, k in zip(parts, kinds))
        dma = pltpu.SemaphoreType.DMA
        self.scratch = ([dma((n, 3)), dma((n, 3)), dma((n,)), dma((n,))]
                        + [pltpu.VMEM(o.shape[1:], BF) for o in self.out_shape])

    def _slab(self, ins, i, s):
        _, rows, cols = self.out_shape[i].shape
        if self.kinds[i] == "col":
            return ins[i].at[:, pl.ds(pl.multiple_of(s * cols, 128), cols)]
        return ins[i].at[s]

    def _copies(self, ins, outs, scr):
        ssem, rsem, lsem, osem = scr[:4]
        own = scr[4:]
        x, y, c = _place()
        loads, stores, sends = [], [], []
        for i in range(len(ins)):
            loads.append(functools.partial(pltpu.make_async_copy, self._slab(ins, i, 2 * x + y), own[i], lsem.at[i]))
            stores.append(functools.partial(pltpu.make_async_copy, own[i], outs[i].at[3], osem.at[i]))
            for j, (fx, fy) in enumerate(OTHER_CHIPS):
                px, py = _flip(x, fx), _flip(y, fy)
                sends.append(functools.partial(_remote, self._slab(ins, i, 2 * px + py), outs[i].at[j], ssem.at[i, j],
                                               rsem.at[i, j], (px, py, c)))
        return loads, stores, sends

    def start(self, ins, outs, scr):
        loads, _, sends = self._copies(ins, outs, scr)
        for make in sends + loads:
            make().start()

    def finish(self, ins, outs, scr):
        loads, stores, sends = self._copies(ins, outs, scr)
        for load, store in zip(loads, stores):
            load().wait()
            store().start()
        for make in sends + stores:
            make().wait()


SHARE_STEPS = 2


def _sum_share(slab_list, name, hook=None):
    n = len(slab_list)
    geom = [(sl.shape[1], sl.shape[1] // SHARE_STEPS, sl.shape[2]) for sl in slab_list]

    def body(*refs):
        ins, outs, scr = refs[:n], refs[n:2 * n], refs[2 * n:]
        i = pl.program_id(0)
        x, y, c = _place()
        sibling = (x, y, 1 - c)

        def copies(q, k):
            rh, tr, _ = geom[q]
            stage, lsem, ssem, rsem = scr[4 * q:4 * q + 4]
            dst = outs[q].at[pl.ds(pl.multiple_of(c * rh + k * tr, 8), tr), :]
            return (pltpu.make_async_copy(stage.at[k], dst, lsem.at[k]),
                    _remote(stage.at[k], dst, ssem.at[k], rsem, sibling))

        for q in range(n):
            acc = ins[q][3].astype(F32)
            for k in range(3):
                acc = acc + ins[q][k].astype(F32)
            scr[4 * q][i] = acc
            for cp in copies(q, i):
                cp.start()

        @pl.when(i == SHARE_STEPS - 1)
        def _():
            for q in range(n):
                rh = geom[q][0]
                for k in range(SHARE_STEPS):
                    local, remote = copies(q, k)
                    local.wait()
                    remote.wait_send()
                got = outs[q].at[pl.ds(pl.multiple_of((1 - c) * rh, 8), rh), :]
                _remote(got, got, scr[4 * q + 2].at[0], scr[4 * q + 3], sibling).wait_recv()

    dma = pltpu.SemaphoreType.DMA
    scratch = []
    for rh, tr, cols in geom:
        scratch += [pltpu.VMEM((SHARE_STEPS, tr, cols), F32), dma((SHARE_STEPS,)), dma((SHARE_STEPS,)), dma]
    return _call(
        body, name=name, grid=(SHARE_STEPS,), out_shape=tuple(S((2 * rh, cols), F32) for rh, _, cols in geom),
        in_specs=[pl.BlockSpec((4, tr, cols), lambda i: (0, i, 0)) for _, tr, cols in geom], out_specs=(ANY,) * n,
        scratch_shapes=scratch, sem=("arbitrary",), args=list(slab_list), hook=hook)


class _SmallSumHook:
    def __init__(self, buf):
        self.ins, self.out_shape = [buf], (S(buf.shape, F32),)
        dma = pltpu.SemaphoreType.DMA
        self.scratch = [pltpu.VMEM((8,) + buf.shape, F32), pltpu.VMEM(buf.shape, F32), dma((7,)), dma((7,)), dma]

    def _sends(self, scr):
        slots, _, ssem, rsem, _ = scr
        x, y, c = _place()
        me = 4 * x + 2 * y + c
        for r in range(1, 8):
            px, py, pc = _flip(x, (r >> 2) & 1), _flip(y, (r >> 1) & 1), _flip(c, r & 1)
            yield (functools.partial(_remote, slots.at[me], slots.at[me], ssem.at[r - 1], rsem.at[r - 1], (px, py, pc)),
                   functools.partial(_remote, slots.at[me], slots.at[4 * px + 2 * py + pc], ssem.at[r - 1],
                                     rsem.at[r - 1], (px, py, pc)))

    def start(self, ins, outs, scr):
        slots, _, _, _, lsem = scr
        x, y, c = _place()
        load = pltpu.make_async_copy(ins[0], slots.at[4 * x + 2 * y + c], lsem)
        load.start()
        load.wait()
        for send, _ in self._sends(scr):
            send().start()

    def finish(self, ins, outs, scr):
        slots, total, _, _, lsem = scr
        for _, arrival in self._sends(scr):
            arrival().wait_recv()
        for send, _ in self._sends(scr):
            send().wait_send()
        acc = slots[0]
        for k in range(1, 8):
            acc = acc + slots[k]
        total[...] = acc
        store = pltpu.make_async_copy(total, outs[0], lsem)
        store.start()
        store.wait()


BIG = ("ffn1_w_gu", "ffn1_w_down", "w_in", "w_out", "xattn_wq", "xattn_wkv", "xattn_wo", "ffn2_w_gu", "ffn2_w_down")
KIND = {"ffn1_w_gu": "col", "ffn1_w_down": "row", "w_in": "col", "w_out": "row", "xattn_wq": "row",
        "xattn_wkv": "col", "xattn_wo": "row", "ffn2_w_gu": "col", "ffn2_w_down": "row"}
WEIGHTS = ("rel_bias", "ffn1_norm", "ffn1_w_gu", "ffn1_w_down", "mix_norm", "w_in", "sinks", "conv_w", "w_out",
           "xattn_norm", "mem_norm", "xattn_wq", "xattn_wkv", "xattn_wo", "ffn2_norm", "ffn2_w_gu", "ffn2_w_down",
           "final_norm")
SMALL_ROWS = 48
GAIN_ROW = {"ffn1_norm": 0, "mix_norm": 1, "xattn_norm": 2, "mem_norm": 3, "ffn2_norm": 4, "final_norm": 5}
CONV_ROW, SINK_ROW, LOSS_ROW, BIAS_ROW = 6, 9, 11, 16
TAP_ROWS = 8


def _rows_block(rows, d):
    buf = jnp.zeros((SMALL_ROWS, d), F32)
    for r, v in rows.items():
        buf = lax.dynamic_update_slice(buf, v if v.ndim == 2 else v.reshape(1, -1), (r, 0))
    return buf


def _adamw_small(gsum, conv_g, small):
    names = list(small)

    def grad(k, gsum_ref, conv_ref):
        rows, cols = small[k][0].shape
        if k in GAIN_ROW:
            return gsum_ref[GAIN_ROW[k]:GAIN_ROW[k] + 1, :]
        if k == "conv_w":
            return conv_ref[...]
        r0 = SINK_ROW if k == "sinks" else BIAS_ROW
        return gsum_ref[r0:r0 + rows, 0:cols]

    def body(gsum_ref, conv_ref, *refs):
        ins, outs = refs[:3 * len(names)], refs[3 * len(names):]
        for q, k in enumerate(names):
            w_ref, m_ref, v_ref = ins[3 * q:3 * q + 3]
            g_ref, d_ref, mo_ref, vo_ref = outs[4 * q:4 * q + 4]
            gv = grad(k, gsum_ref, conv_ref)
            g_ref[...] = gv
            d_ref[...], mo_ref[...], vo_ref[...] = _adam_update(w_ref[...], gv, m_ref[...], v_ref[...])

    res = pl.pallas_call(
        body, name="adamw_small", out_shape=tuple(S(small[k][0].shape, F32) for k in names for _ in range(4)),
        compiler_params=_params())(gsum, conv_g, *[a for k in names for a in small[k]])
    return {k: res[4 * q:4 * q + 4] for q, k in enumerate(names)}


def _local_step(x, mem, pos, target, w, gains, rel_bias, sinks, conv_w, shards=None):
    t, d = x.shape
    dist = shards is not None
    w = dict(w)
    grads, slabs = {}, {}
    pos_col = pos.reshape(t, 1)
    pos_row = pos.reshape(1, t)
    bias_t = _bias_build(rel_bias).reshape(SWA_KV_HEADS, GROUP_ROWS, 2 * BLOCK)

    def gather(names):
        return _GatherHook([shards[k] for k in names], [KIND[k] for k in names]) if dist else None

    def gathered(names, hook):
        if dist:
            for k, gw in zip(names, hook.results):
                w[k] = gw if KIND[k] == "col" else gw.reshape(-1, gw.shape[-1])

    def dw(problems, tn, name, hook=None):
        if dist:
            res = _dw_pair([(a, b) for _, a, b in problems], tn, name, KIND[problems[0][0]], hook)
        else:
            res = [_dw(a, b, tn, name + "_" + k) for k, a, b in problems]
        grads.update(zip((k for k, _, _ in problems), res))

    def pair_sums(names):
        if not dist:
            return None
        flat = lambda k, v: v.reshape(-1, v.shape[-1]) if KIND[k] == "row" else v
        sums = _add_bf16([(flat(k, grads[k][0]), flat(k, grads[k][1])) for k in names], "pair_sum_" + names[0])
        return {k: (p if KIND[k] == "col" else p.reshape(4, -1, p.shape[-1])) for k, p in zip(names, sums)}

    def chips(names, parts):
        return _ChipsHook([parts[k] for k in names], [KIND[k] for k in names]) if dist else None

    def reduced(names, hook):
        if dist:
            slabs.update(zip(names, hook.results))

    names = ("w_in", "w_out")
    hook = gather(names)
    x1, g1, u1 = _ffn_fwd(x, gains["ffn1_norm"], w["ffn1_w_gu"], w["ffn1_w_down"], "ffn1_fwd", hook)
    gathered(names, hook)
    names = ("xattn_wq", "xattn_wkv", "xattn_wo")
    hook = gather(names)
    qkv, e = _mix_proj(x1, gains["mix_norm"], w["w_in"], hook)
    gathered(names, hook)
    names = ("ffn2_w_gu",)
    hook = gather(names)
    attn = _swa_fwd(qkv, pos_col, pos_row, bias_t, sinks, hook)
    gathered(names, hook)
    names = ("ffn2_w_down",)
    hook = gather(names)
    x2, merged = _mix_out_fwd(e, attn, conv_w, w["w_out"], x1, hook)
    gathered(names, hook)
    mh, kv = _mem_kv(mem, gains["mem_norm"], w["xattn_wkv"])
    x3, qx, o = _xattn_fwd(x2, gains["xattn_norm"], w["xattn_wq"], kv, w["xattn_wo"])
    dx4, g2, u2, d_final, loss = _ffn_fwd(x3, gains["ffn2_norm"], w["ffn2_w_gu"], w["ffn2_w_down"], "ffn2_fwd",
                                          head=(gains["final_norm"], target))

    dx3, d_ffn2, dgu2, a2, h4, dyh4 = _ffn_bwd(dx4, x3, gains["ffn2_norm"], g2, u2, w["ffn2_w_gu"], w["ffn2_w_down"],
                                               "ffn2_bwd")
    dw([("ffn2_w_gu", h4, dgu2)], 1408, "dw_ffn2_gu")
    dw([("ffn2_w_down", a2, dyh4)], 512, "dw_ffn2_down")
    parts = pair_sums(("ffn2_w_gu", "ffn2_w_down"))
    hook = chips(("ffn2_w_gu",), parts)
    dx2, d_xattn, dkv, dxh3, h3, dqx = _xattn_bwd(dx3, x2, gains["xattn_norm"], qx, kv, w["xattn_wq"], w["xattn_wo"], hook)
    reduced(("ffn2_w_gu",), hook)
    grads["xattn_wkv"], d_mem = _mem_bwd(dkv, mh, mem, gains["mem_norm"], w["xattn_wkv"], dist)
    hook = chips(("ffn2_w_down",), parts)
    dattn, dp, dcw, dxh2 = _mix_out_bwd(dx2, e, attn, conv_w, w["w_out"], qkv.shape[1], hook)
    reduced(("ffn2_w_down",), hook)
    dw([("xattn_wo", o, dxh3), ("xattn_wq", h3, dqx), ("w_out", merged, dxh2)], 1024, "dw_wo_wq_wout")
    names = ("xattn_wo", "xattn_wq", "xattn_wkv", "w_out")
    hook = chips(names, pair_sums(names))
    dp, gb, dsk = _swa_bwd(qkv, dattn, pos_col, pos_row, bias_t, sinks, dp, hook)
    reduced(names, hook)
    d_rel_bias, d_sinks = _bias_reduce(gb.reshape(SWA_HEADS, BLOCK, 2 * BLOCK), dsk.reshape(SWA_HEADS, BLOCK, 1))
    dx1, d_mix, h2 = _mix_in_bwd(dp, x1, gains["mix_norm"], w["w_in"], dx2)
    dw([("w_in", h2, dp)], w["w_in"].shape[1] // 4, "dw_win")
    names = ("w_in",)
    hook = chips(names, pair_sums(names))
    dx0, d_ffn1, dgu1, a1, h1, dyh1 = _ffn_bwd(dx1, x, gains["ffn1_norm"], g1, u1, w["ffn1_w_gu"], w["ffn1_w_down"],
                                               "ffn1_bwd", hook)
    reduced(names, hook)
    dw([("ffn1_w_down", a1, dyh1)], 512, "dw_ffn1_down")
    names = ("ffn1_w_down",)
    hook = chips(names, pair_sums(names))
    dw([("ffn1_w_gu", h1, dgu1)], 1408, "dw_ffn1_gu", hook)
    reduced(names, hook)
    rows = {0: d_ffn1, 1: d_mix, 2: d_xattn, 3: d_mem, 4: d_ffn2, 5: d_final, SINK_ROW: d_sinks, BIAS_ROW: d_rel_bias,
            LOSS_ROW: loss[0, 0:1]}
    rows.update({CONV_ROW + j: dcw[j] for j in range(3)})
    last = ("ffn1_w_gu",)
    return dx0, (slabs if dist else grads), _rows_block(rows, d), chips(last, pair_sums(last))


def kernel(x, mem, positions, rel_bias, ffn1_norm, ffn1_w_gu, ffn1_w_down, mix_norm, w_in, sinks, conv_w, w_out, xattn_norm, mem_norm, xattn_wq, xattn_wkv, xattn_wo, ffn2_norm, ffn2_w_gu, ffn2_w_down, final_norm, loss_target, m_rel_bias, m_ffn1_norm, m_ffn1_w_gu, m_ffn1_w_down, m_mix_norm, m_w_in, m_sinks, m_conv_w, m_w_out, m_xattn_norm, m_mem_norm, m_xattn_wq, m_xattn_wkv, m_xattn_wo, m_ffn2_norm, m_ffn2_w_gu, m_ffn2_w_down, m_final_norm, v_rel_bias, v_ffn1_norm, v_ffn1_w_gu, v_ffn1_w_down, v_mix_norm, v_w_in, v_sinks, v_conv_w, v_w_out, v_xattn_norm, v_mem_norm, v_xattn_wq, v_xattn_wkv, v_xattn_wo, v_ffn2_norm, v_ffn2_w_gu, v_ffn2_w_down, v_final_norm):
    args = dict(locals())
    wts = {k: args[k] for k in WEIGHTS}
    mom = {k: args["m_" + k] for k in WEIGHTS}
    var = {k: args["v_" + k] for k in WEIGHTS}
    d = x.shape[-1]
    s_me = 2 * lax.axis_index("x") + lax.axis_index("y")

    shards = {k: wts[k][0].astype(BF) for k in BIG}
    first = ("ffn1_w_gu", "ffn1_w_down")
    cw_cols = conv_w.shape[-1]
    placed = lax.dynamic_update_slice(jnp.zeros((TAP_ROWS, d), F32), 0.5 * conv_w[0], (0, s_me * cw_cols))
    head = _Both([_GatherHook([shards[k] for k in first], [KIND[k] for k in first]), _SmallSumHook(placed)])
    _exchange(head, "gather_ffn1")
    gathered, (conv_sum,) = head.results
    whole = {k: (gw if KIND[k] == "col" else gw.reshape(-1, gw.shape[-1])) for k, gw in zip(first, gathered)}
    conv_whole = conv_sum[0:3]

    gains = {k: wts[k].reshape(1, d) for k in GAIN_ROW}
    dx0, slabs, small, last_chips = _local_step(x[0], mem[0], positions[0], loss_target[0], whole, gains, rel_bias, sinks,
                                               conv_whole, shards)

    late = ("ffn1_w_gu", "ffn1_w_down", "ffn2_w_down")
    early = tuple(k for k in BIG if k not in late)
    first_ones = tuple(k for k in BIG if k != "ffn1_w_gu")
    tail = _Both([last_chips, _SmallSumHook(small)])
    shard_grads = dict(zip(first_ones, _sum_share([slabs[k] for k in first_ones], "sum_share", tail)))
    (last_slabs,), (small_sum,) = tail.results
    shard_grads["ffn1_w_gu"], = _sum_share([last_slabs], "sum_share_ffn1_w_gu")
    quad = lambda k: (wts[k][0], shard_grads[k], mom[k][0], var[k][0])
    updates = dict(zip(early, _adamw([quad(k) for k in early], "adamw_early", 16, echo=True)))
    updates.update(zip(late, _adamw([quad(k) for k in late], "adamw_late", 8, echo=True)))
    loss = small_sum[LOSS_ROW, 0]

    out_g, out_d, out_m, out_v = {}, {}, {}, {}
    for k in BIG:
        out_g[k], out_d[k], out_m[k], out_v[k] = (a[None] for a in updates[k])

    conv_g = lax.dynamic_slice(small_sum, (CONV_ROW, s_me * cw_cols), (3, cw_cols))
    two_d = lambda a: a.reshape(a.shape[-2:]) if a.ndim > 1 else a.reshape(1, -1)
    small_names = [k for k in WEIGHTS if k not in KIND]
    done = _adamw_small(small_sum, conv_g, {k: (two_d(wts[k]), two_d(mom[k]), two_d(var[k])) for k in small_names})
    for k in small_names:
        out_g[k], out_d[k], out_m[k], out_v[k] = (a.reshape(wts[k].shape) for a in done[k])

    return (loss, dx0[None], *[out_g[k] for k in WEIGHTS], *[out_d[k] for k in WEIGHTS],
            *[out_m[k] for k in WEIGHTS], *[out_v[k] for k in WEIGHTS])
```

```python
import functools
import math

import jax
import jax.numpy as jnp
from jax import lax
from jax.experimental import pallas as pl
from jax.experimental.pallas import tpu as pltpu

BF = jnp.bfloat16
F32 = jnp.float32
I32 = jnp.int32
S = jax.ShapeDtypeStruct

EPS = 1e-6
NEG = -1e30
POS_PAD = 1 << 30
WINDOW = 128
BLOCK = 128
HEAD_DIM = 64
SWA_HEADS = 16
SWA_KV_HEADS = 4
SWA_GROUP = SWA_HEADS // SWA_KV_HEADS
MEM_HEADS = 4
REL_BUCKETS = 32
REL_MAX_DIST = 128
ADAM_LR = 0.001
ADAM_B1 = 0.9
ADAM_B2 = 0.999
ADAM_EPS = 1e-08
ADAM_WD = 0.01
ADAM_STEP = 10

V7X_VMEM_LIMIT_BYTES = 56 * 1024 * 1024
MESH = pl.DeviceIdType.MESH
ANY = pl.BlockSpec(memory_space=pl.ANY)
VMEM_SPEC = pl.BlockSpec(memory_space=pltpu.VMEM)
SMEM_SPEC = pl.BlockSpec(memory_space=pltpu.SMEM)


def _params(sem=None):
    return pltpu.CompilerParams(dimension_semantics=sem, vmem_limit_bytes=V7X_VMEM_LIMIT_BYTES)


def _resident(shape):
    nd = len(shape)
    return pl.BlockSpec(shape, lambda *_: (0,) * nd, pipeline_mode=pl.Buffered(1))


def _acc_spec(shape):
    nd = len(shape)
    return pl.BlockSpec(shape, lambda *_: (0,) * nd)


def _call(body, *, name, grid, out_shape, in_specs, out_specs, args, sem, scratch_shapes=(), aliases=None, hook=None):
    aliases = aliases or {}
    if hook is None:
        return pl.pallas_call(body, name=name, grid=grid, out_shape=out_shape, in_specs=in_specs, out_specs=out_specs,
                              scratch_shapes=list(scratch_shapes), input_output_aliases=aliases,
                              compiler_params=_params(sem))(*args)
    n_in, n_out, n_scr = len(in_specs), len(out_shape), len(scratch_shapes)
    h_in, h_out = len(hook.ins), len(hook.out_shape)

    def at_step(pick):
        conds = [pl.program_id(ax) == pick(size) for ax, size in enumerate(grid)]
        return functools.reduce(jnp.logical_and, conds)

    def hosted(*refs):
        k_in, x_in = refs[:n_in], refs[n_in:n_in + h_in]
        o0 = n_in + h_in
        k_out, x_out = refs[o0:o0 + n_out], refs[o0 + n_out:o0 + n_out + h_out]
        s0 = o0 + n_out + h_out
        k_scr, x_scr = refs[s0:s0 + n_scr], refs[s0 + n_scr:]

        @pl.when(at_step(lambda size: 0))
        def _():
            hook.start(x_in, x_out, x_scr)
        body(*k_in, *k_out, *k_scr)

        @pl.when(at_step(lambda size: size - 1))
        def _():
            hook.finish(x_in, x_out, x_scr)

    res = pl.pallas_call(
        hosted, name=name, grid=grid, out_shape=tuple(out_shape) + tuple(hook.out_shape),
        in_specs=list(in_specs) + [ANY] * h_in, out_specs=tuple(out_specs) + (ANY,) * h_out,
        scratch_shapes=list(scratch_shapes) + list(hook.scratch), input_output_aliases=aliases,
        compiler_params=_params(("arbitrary",) * len(grid)),
    )(*args, *hook.ins)
    hook.results = res[n_out:]
    return res[:n_out]


def _nn(a, b):
    return jnp.dot(a, b, preferred_element_type=F32)


def _nt(a, b):
    return lax.dot_general(a, b, (((1,), (1,)), ((), ())), preferred_element_type=F32)


def _tn(a, b):
    return lax.dot_general(a, b, (((0,), (0,)), ((), ())), preferred_element_type=F32)


def _sigmoid(v):
    return 1.0 / (1.0 + jnp.exp(-v))


def _rms(x):
    r = lax.rsqrt(jnp.mean(x * x, axis=-1, keepdims=True) + EPS)
    return x * r, r


def _rms_bwd(dh, n, r, g):
    dn = dh * g
    dx = r * (dn - n * jnp.mean(dn * n, axis=-1, keepdims=True))
    return dx, jnp.sum(dh * n, axis=0, keepdims=True)


def _ffn_fwd(x, gn, wgu, wd, name, hook=None, head=None):
    t, d = x.shape
    f = wd.shape[0]
    tm, fc = 256, 1408

    def body(x_ref, gn_ref, wgu_ref, wd_ref, *rest):
        xv = x_ref[...]
        n, _ = _rms(xv)
        h = (n * gn_ref[...]).astype(BF)
        g_ref, u_ref = rest[-2:] if head is None else rest[3:5]
        acc = jnp.zeros((tm, d), F32)
        for c0 in range(0, f, fc):
            g = _nn(h, wgu_ref[:, c0:c0 + fc])
            u = _nn(h, wgu_ref[:, f + c0:f + c0 + fc])
            g_ref[:, c0:c0 + fc] = g.astype(BF)
            u_ref[:, c0:c0 + fc] = u.astype(BF)
            a = (g * _sigmoid(g)) * u
            acc = acc + _nn(a.astype(BF), wd_ref[c0:c0 + fc, :])
        y = xv + 0.5 * acc
        if head is None:
            rest[0][...] = y
            return
        gf_ref, t_ref, dy_ref, _, _, dgf_ref, loss_ref = rest

        @pl.when(pl.program_id(0) == 0)
        def _():
            dgf_ref[...] = jnp.zeros_like(dgf_ref)
            loss_ref[...] = jnp.zeros_like(loss_ref)
        ny, ry = _rms(y)
        gf = gf_ref[...]
        err = ny * gf - t_ref[...]
        loss_ref[...] += 0.5 * jnp.sum(jnp.sum(err * err, axis=-1, keepdims=True) / d, axis=0, keepdims=True)
        dy, dgf = _rms_bwd(err / d, ny, ry, gf)
        dy_ref[...] = dy
        dgf_ref[...] += dgf

    row = pl.BlockSpec((tm, d), lambda i: (i, 0))
    frow = pl.BlockSpec((tm, f), lambda i: (i, 0))
    in_specs = [row, _resident((1, d)), _resident(wgu.shape), _resident(wd.shape)]
    out_shape = (S((t, d), F32), S((t, f), BF), S((t, f), BF))
    out_specs = (row, frow, frow)
    args = (x, gn, wgu, wd)
    if head is not None:
        in_specs += [_resident((1, d)), row]
        out_shape += (S((1, d), F32), S((1, 128), F32))
        out_specs += (_acc_spec((1, d)), _acc_spec((1, 128)))
        args += tuple(head)
    return _call(body, name=name, grid=(t // tm,), out_shape=out_shape, in_specs=in_specs, out_specs=out_specs,
                 sem=("parallel",) if head is None else ("arbitrary",), args=args, hook=hook)


def _mix_proj(x, gn, w_in, hook=None):
    t, d = x.shape
    tm = 256
    nqkv = 1536
    ne = w_in.shape[1] - nqkv

    def body(x_ref, gn_ref, w_ref, qkv_ref, e_ref):
        n, _ = _rms(x_ref[...])
        h = (n * gn_ref[...]).astype(BF)
        qkv_ref[...] = _nn(h, w_ref[:, 0:nqkv]).astype(BF)
        for c0 in range(0, ne, 1024):
            e_ref[:, c0:c0 + 1024] = _nn(h, w_ref[:, nqkv + c0:nqkv + c0 + 1024]).astype(BF)

    return _call(
        body, name="mix_proj", grid=(t // tm,),
        out_shape=(S((t, nqkv), BF), S((t, ne), BF)),
        in_specs=[pl.BlockSpec((tm, d), lambda i: (i, 0)), _resident((1, d)), _resident(w_in.shape)],
        out_specs=(pl.BlockSpec((tm, nqkv), lambda i: (i, 0)), pl.BlockSpec((tm, ne), lambda i: (i, 0))),
        sem=("parallel",), args=(x, gn, w_in), hook=hook)


def _t5_bucket(rel):
    n = jnp.maximum(rel, 0)
    max_exact = REL_BUCKETS // 2
    nf = jnp.maximum(n, 1).astype(F32)
    large = max_exact + (jnp.log(nf / max_exact) / math.log(REL_MAX_DIST / max_exact)
                         * (REL_BUCKETS - max_exact)).astype(I32)
    large = jnp.minimum(large, REL_BUCKETS - 1)
    return jnp.where(n < max_exact, n, large)


def _block_rel():
    i = lax.broadcasted_iota(I32, (BLOCK, 2 * BLOCK), 0)
    j = lax.broadcasted_iota(I32, (BLOCK, 2 * BLOCK), 1)
    return i + BLOCK - j


def _bias_build(rel_bias):
    def body(rb_ref, o_ref):
        bucket = _t5_bucket(_block_rel())
        for h in range(SWA_HEADS):
            acc = jnp.zeros((BLOCK, 2 * BLOCK), F32)
            for b in range(REL_BUCKETS):
                acc = jnp.where(bucket == b, rb_ref[b, h], acc)
            o_ref[h] = acc

    return pl.pallas_call(
        body, name="bias_build", out_shape=S((SWA_HEADS, BLOCK, 2 * BLOCK), F32),
        in_specs=[SMEM_SPEC], out_specs=VMEM_SPEC,
    )(rel_bias)


GROUP_ROWS = SWA_GROUP * BLOCK


def _swa_visible(b, pq_ref, pkp_ref, pkc_ref):
    pk = jnp.concatenate([pkp_ref[...], pkc_ref[...]], axis=1)
    col = lax.broadcasted_iota(I32, (1, 2 * BLOCK), 1)
    pk = jnp.where(jnp.logical_and(b == 0, col < BLOCK), POS_PAD, pk)
    rel = jnp.concatenate([pq_ref[...]] * SWA_GROUP, axis=0) - pk
    return jnp.logical_and(rel >= 0, rel < WINDOW)


def _group_heads(ref, hk):
    h0 = hk * SWA_GROUP
    return jnp.concatenate([ref[:, (h0 + g) * HEAD_DIM:(h0 + g + 1) * HEAD_DIM] for g in range(SWA_GROUP)], axis=0)


def _group_sinks(sink_ref, hk):
    row = lax.broadcasted_iota(I32, (GROUP_ROWS, 1), 0)
    col = jnp.zeros((GROUP_ROWS, 1), F32) + sink_ref[0, hk * SWA_GROUP]
    for g in range(1, SWA_GROUP):
        col = jnp.where(row >= g * BLOCK, sink_ref[0, hk * SWA_GROUP + g], col)
    return col


def _swa_probs(qg, kh, vis, bias, sink):
    s = _nt(qg, kh) * (HEAD_DIM ** -0.5)
    s = jnp.where(vis, s + bias, NEG)
    m = jnp.maximum(jnp.max(s, axis=-1, keepdims=True), sink)
    p = jnp.exp(s - m)
    ps = jnp.exp(sink - m)
    inv = 1.0 / (jnp.sum(p, axis=-1, keepdims=True) + ps)
    return p * inv, ps * inv


def _swa_fwd(qkv, pos_col, pos_row, bias_t, sinks, hook=None):
    t = qkv.shape[0]
    nb = t // BLOCK
    qw = SWA_HEADS * HEAD_DIM
    kw = SWA_KV_HEADS * HEAD_DIM

    def body(q_ref, kp_ref, kc_ref, vp_ref, vc_ref, pq_ref, pkp_ref, pkc_ref, bias_ref, sink_ref, o_ref):
        b = pl.program_id(0)
        vis = _swa_visible(b, pq_ref, pkp_ref, pkc_ref)
        k2 = jnp.concatenate([kp_ref[...], kc_ref[...]], axis=0)
        v2 = jnp.concatenate([vp_ref[...], vc_ref[...]], axis=0)
        for hk in range(SWA_KV_HEADS):
            kh = k2[:, hk * HEAD_DIM:(hk + 1) * HEAD_DIM]
            vh = v2[:, hk * HEAD_DIM:(hk + 1) * HEAD_DIM]
            pn, _ = _swa_probs(_group_heads(q_ref, hk), kh, vis, bias_ref[hk], _group_sinks(sink_ref, hk))
            o = _nn(pn.astype(BF), vh)
            for g in range(SWA_GROUP):
                h = hk * SWA_GROUP + g
                o_ref[:, h * HEAD_DIM:(h + 1) * HEAD_DIM] = o[g * BLOCK:(g + 1) * BLOCK]

    prev = lambda b: jnp.maximum(b - 1, 0)
    return _call(
        body, name="swa_fwd", grid=(nb,), out_shape=(S((t, qw), F32),),
        in_specs=[
            pl.BlockSpec((BLOCK, qw), lambda b: (b, 0)),
            pl.BlockSpec((BLOCK, kw), lambda b: (prev(b), qw // kw)),
            pl.BlockSpec((BLOCK, kw), lambda b: (b, qw // kw)),
            pl.BlockSpec((BLOCK, kw), lambda b: (prev(b), qw // kw + 1)),
            pl.BlockSpec((BLOCK, kw), lambda b: (b, qw // kw + 1)),
            pl.BlockSpec((BLOCK, 1), lambda b: (b, 0)),
            pl.BlockSpec((1, BLOCK), lambda b: (0, prev(b))),
            pl.BlockSpec((1, BLOCK), lambda b: (0, b)),
            _resident(bias_t.shape),
            SMEM_SPEC,
        ],
        out_specs=(pl.BlockSpec((BLOCK, qw), lambda b: (b, 0)),),
        sem=("parallel",), args=(qkv, qkv, qkv, qkv, qkv, pos_col, pos_row, pos_row, bias_t, sinks), hook=hook)[0]


HALO = 16


def _conv_taps(z, zh, first):
    tm = z.shape[0]
    zh = jnp.where(first, 0.0, zh)
    row = lax.broadcasted_iota(I32, (tm, 1), 0)
    z1 = jnp.where(row == 0, zh[HALO - 1:HALO, :], pltpu.roll(z, 1, 0))
    z2 = jnp.where(row == 0, zh[HALO - 2:HALO - 1, :], jnp.where(row == 1, zh[HALO - 1:HALO, :], pltpu.roll(z, 2, 0)))
    return z1, z2


def _mix_out_fwd(e, attn, conv_w, w_out, x, hook=None):
    t, d = x.shape
    tm = 256
    hb = tm // HALO
    f32 = lambda ref: ref[...].astype(F32)

    def body(c_ref, b_ref, u_ref, ga_ref, gc_ref, ch_ref, uh_ref, attn_ref, cw_ref, w_ref, x_ref, xo_ref, mg_ref):
        i = pl.program_id(0)
        z = f32(c_ref) * f32(u_ref)
        z1, z2 = _conv_taps(z, f32(ch_ref) * f32(uh_ref), i == 0)
        s = cw_ref[0:1, :] * z2 + cw_ref[1:2, :] * z1 + cw_ref[2:3, :] * z
        conv = f32(b_ref) * s
        merged = (_sigmoid(f32(ga_ref)) * attn_ref[...] + _sigmoid(f32(gc_ref)) * conv).astype(BF)
        mg_ref[...] = merged
        xo_ref[...] = x_ref[...] + _nn(merged, w_ref[...])

    ecol = lambda cb: pl.BlockSpec((tm, d), lambda i: (i, cb))
    halo = lambda cb: pl.BlockSpec((HALO, d), lambda i: (jnp.maximum(i * hb - 1, 0), cb))
    row = pl.BlockSpec((tm, d), lambda i: (i, 0))
    return _call(
        body, name="mix_out_fwd", grid=(t // tm,),
        out_shape=(S((t, d), F32), S((t, d), BF)),
        in_specs=[ecol(0), ecol(1), ecol(2), ecol(3), ecol(4), halo(0), halo(2), row,
                  _resident(conv_w.shape), _resident(w_out.shape), row],
        out_specs=(row, row),
        sem=("parallel",), args=(e, e, e, e, e, e, e, attn, conv_w, w_out, x), hook=hook)


def _mem_kv(mem, gm, wkv):
    m, d = mem.shape

    def body(mem_ref, gm_ref, w_ref, mh_ref, kv_ref):
        n, _ = _rms(mem_ref[...])
        mh = (n * gm_ref[...]).astype(BF)
        mh_ref[...] = mh
        kv_ref[...] = _nn(mh, w_ref[...]).astype(BF)

    return pl.pallas_call(
        body, name="mem_kv", out_shape=(S((m, d), BF), S((m, wkv.shape[1]), BF)),
        compiler_params=_params(),
    )(mem, gm, wkv)


def _xattn_probs(qh, kh):
    s = _nt(qh, kh) * (kh.shape[1] ** -0.5)
    p = jnp.exp(s - jnp.max(s, axis=-1, keepdims=True))
    return p * (1.0 / jnp.sum(p, axis=-1, keepdims=True))


def _xattn_fwd(x, gn, wq, kv, wo):
    t, d = x.shape
    tm = 256
    hd = d // MEM_HEADS

    def body(x_ref, gn_ref, wq_ref, kv_ref, wo_ref, xo_ref, q_ref, o_ref):
        xv = x_ref[...]
        n, _ = _rms(xv)
        q = _nn((n * gn_ref[...]).astype(BF), wq_ref[...]).astype(BF)
        q_ref[...] = q
        outs = []
        for hh in range(MEM_HEADS):
            p = _xattn_probs(q[:, hh * hd:(hh + 1) * hd], kv_ref[:, hh * hd:(hh + 1) * hd])
            outs.append(_nn(p.astype(BF), kv_ref[:, d + hh * hd:d + (hh + 1) * hd]))
        o = jnp.concatenate(outs, axis=1).astype(BF)
        o_ref[...] = o
        xo_ref[...] = xv + _nn(o, wo_ref[...])

    row = pl.BlockSpec((tm, d), lambda i: (i, 0))
    return pl.pallas_call(
        body, name="xattn_fwd", grid=(t // tm,),
        out_shape=(S((t, d), F32), S((t, d), BF), S((t, d), BF)),
        in_specs=[row, _resident((1, d)), _resident(wq.shape), _resident(kv.shape), _resident(wo.shape)],
        out_specs=(row, row, row),
        compiler_params=_params(("parallel",)),
    )(x, gn, wq, kv, wo)


def _ffn_bwd(dxo, x, gn, g, u, wgu, wd, name, hook=None):
    t, d = x.shape
    f = wd.shape[0]
    tm, fc = 256, 1408

    def body(dxo_ref, x_ref, gn_ref, g_ref, u_ref, wgu_ref, wd_ref, dx_ref, dgn_ref, dgu_ref, a_ref, h_ref, dyh_ref):
        @pl.when(pl.program_id(0) == 0)
        def _():
            dgn_ref[...] = jnp.zeros_like(dgn_ref)
        dxov = dxo_ref[...]
        dyh = (0.5 * dxov).astype(BF)
        dyh_ref[...] = dyh
        n, r = _rms(x_ref[...])
        gnv = gn_ref[...]
        h_ref[...] = (n * gnv).astype(BF)
        dh = jnp.zeros((tm, d), F32)
        for c0 in range(0, f, fc):
            gv = g_ref[:, c0:c0 + fc].astype(F32)
            uv = u_ref[:, c0:c0 + fc].astype(F32)
            da = _nt(dyh, wd_ref[c0:c0 + fc, :])
            sg = _sigmoid(gv)
            silu = gv * sg
            a_ref[:, c0:c0 + fc] = (silu * uv).astype(BF)
            dg = (da * uv * (sg * (1.0 + gv * (1.0 - sg)))).astype(BF)
            du = (da * silu).astype(BF)
            dgu_ref[:, c0:c0 + fc] = dg
            dgu_ref[:, f + c0:f + c0 + fc] = du
            dh = dh + _nt(dg, wgu_ref[:, c0:c0 + fc]) + _nt(du, wgu_ref[:, f + c0:f + c0 + fc])
        dx, dgn = _rms_bwd(dh, n, r, gnv)
        dx_ref[...] = dxov + dx
        dgn_ref[...] += dgn

    row = pl.BlockSpec((tm, d), lambda i: (i, 0))
    frow = pl.BlockSpec((tm, f), lambda i: (i, 0))
    return _call(
        body, name=name, grid=(t // tm,),
        out_shape=(S((t, d), F32), S((1, d), F32), S((t, 2 * f), BF), S((t, f), BF), S((t, d), BF), S((t, d), BF)),
        in_specs=[row, row, _resident((1, d)), frow, frow, _resident(wgu.shape), _resident(wd.shape)],
        out_specs=(row, _acc_spec((1, d)), pl.BlockSpec((tm, 2 * f), lambda i: (i, 0)), frow, row, row),
        sem=("arbitrary",), args=(dxo, x, gn, g, u, wgu, wd), hook=hook)


def _dw(a, b, tn, name):
    t, ka = a.shape
    nb = b.shape[1]
    tt = 1024
    nt = t // tt

    def body(a_ref, b_ref, o_ref, acc_ref):
        k = pl.program_id(1)

        @pl.when(k == 0)
        def _():
            acc_ref[...] = jnp.zeros_like(acc_ref)
        acc_ref[...] += _tn(a_ref[...], b_ref[...])

        @pl.when(k == nt - 1)
        def _():
            o_ref[...] = acc_ref[...].astype(BF)

    return pl.pallas_call(
        body, name=name, grid=(nb // tn, nt), out_shape=S((ka, nb), BF),
        in_specs=[pl.BlockSpec((tt, ka), lambda j, k: (k, 0)), pl.BlockSpec((tt, tn), lambda j, k: (k, j))],
        out_specs=pl.BlockSpec((ka, tn), lambda j, k: (0, j)), scratch_shapes=[pltpu.VMEM((ka, tn), F32)],
        compiler_params=_params(("parallel", "arbitrary")),
    )(a, b)


def _dw_pair(pairs, tn, name, kind, hook=None):
    npairs = len(pairs)
    t, ka = pairs[0][0].shape
    nb = pairs[0][1].shape[1]
    tt = 2048 if npairs == 1 else 1024
    nt, nj = t // tt, nb // tn
    col = kind == "col"
    rh = ka // 2 if col else ka // 8
    tile = (rh, tn) if col else (4, rh, tn)
    half = (rh, nb) if col else (4, rh, nb)
    lead = (slice(None),) * (len(tile) - 1)

    def body(*refs):
        ins, mines, sibs = refs[:2 * npairs], refs[2 * npairs:3 * npairs], refs[3 * npairs:4 * npairs]
        acc_ref, stage, ssem, rsem = refs[4 * npairs:]
        j, k = pl.program_id(0), pl.program_id(1)
        x, y, c = _place()
        sibling = (x, y, 1 - c)

        def send(p, slot, jj):
            dst = sibs[p].at[lead + (pl.ds(pl.multiple_of(jj * tn, 128), tn),)]
            return _remote(stage.at[slot], dst, ssem.at[slot], rsem.at[p], sibling)

        def rows(s, whose):
            return acc_ref[pl.ds(pl.multiple_of(s * 2 * rh + whose * rh, 16), rh), :].astype(BF)

        def step(p):
            a_ref, b_ref, mine_ref = ins[2 * p], ins[2 * p + 1], mines[p]

            @pl.when(k == 0)
            def _():
                acc_ref[...] = jnp.zeros_like(acc_ref)
            acc_ref[...] += _tn(a_ref[...], b_ref[...])

            @pl.when(k == nt - 1)
            def _():
                slot = j % 2

                @pl.when(j >= 2)
                def _():
                    send(p, slot, 0).wait_send()
                if col:
                    mine_ref[...] = rows(0, c)
                    stage[slot] = rows(0, 1 - c)
                else:
                    for s in range(4):
                        mine_ref[s] = rows(s, c)
                        stage[slot, s] = rows(s, 1 - c)
                send(p, slot, j - p * nj).start()

        for p in range(npairs):
            pl.when(j // nj == p)(functools.partial(step, p))

        @pl.when(jnp.logical_and(j == npairs * nj - 1, k == nt - 1))
        def _():
            for jj in range(max(npairs * nj - 2, 0), npairs * nj):
                send(0, jj % 2, 0).wait_send()
            for p in range(npairs):
                _remote(sibs[p], sibs[p], ssem.at[0], rsem.at[p], sibling).wait_recv()

    in_specs, args = [], []
    for p, (a, b) in enumerate(pairs):
        on = lambda j, p=p: j // nj == p
        in_specs += [pl.BlockSpec((tt, ka), lambda j, k, on=on: (jnp.where(on(j), k, 0), 0)),
                     pl.BlockSpec((tt, tn), lambda j, k, on=on, p=p: (jnp.where(on(j), k, 0), jnp.clip(j - p * nj, 0, nj - 1)))]
        args += [a, b]
    mine_spec = lambda p: pl.BlockSpec(tile, (lambda j, k: (0, jnp.clip(j - p * nj, 0, nj - 1))) if col
                                       else (lambda j, k: (0, 0, jnp.clip(j - p * nj, 0, nj - 1))))
    res = _call(
        body, name=name, grid=(npairs * nj, nt), out_shape=(S(half, BF),) * (2 * npairs),
        in_specs=in_specs, out_specs=tuple(mine_spec(p) for p in range(npairs)) + (ANY,) * npairs,
        scratch_shapes=[pltpu.VMEM((ka, tn), F32), pltpu.VMEM((2,) + tile, BF), pltpu.SemaphoreType.DMA((2,)),
                        pltpu.SemaphoreType.DMA((npairs,))],
        sem=("arbitrary", "arbitrary"), args=args, hook=hook)
    return [(res[p], res[npairs + p]) for p in range(npairs)]


def _xattn_bwd(dxo, x, gn, q, kv, wq, wo, hook=None):
    t, d = x.shape
    tm = 256
    hd = d // MEM_HEADS
    nkv = kv.shape[0]

    def body(dxo_ref, x_ref, gn_ref, q_ref, kv_ref, wq_ref, wo_ref, dx_ref, dgn_ref, dkv_ref, dxh_ref, h_ref, dq_ref):
        @pl.when(pl.program_id(0) == 0)
        def _():
            dgn_ref[...] = jnp.zeros_like(dgn_ref)
            dkv_ref[...] = jnp.zeros_like(dkv_ref)
        dxov = dxo_ref[...]
        dxh = dxov.astype(BF)
        dxh_ref[...] = dxh
        do = _nt(dxh, wo_ref[...]).astype(BF)
        dqs = []
        for hh in range(MEM_HEADS):
            lo, hi = hh * hd, (hh + 1) * hd
            qh = q_ref[:, lo:hi]
            kh = kv_ref[:, lo:hi]
            vh = kv_ref[:, d + lo:d + hi]
            doh = do[:, lo:hi]
            p = _xattn_probs(qh, kh)
            dp = _nt(doh, vh)
            ds = (p * (dp - jnp.sum(p * dp, axis=-1, keepdims=True)) * (hd ** -0.5)).astype(BF)
            dqs.append(_nn(ds, kh))
            dkv_ref[:, lo:hi] += _tn(ds, qh)
            dkv_ref[:, d + lo:d + hi] += _tn(p.astype(BF), doh)
        dq = jnp.concatenate(dqs, axis=1).astype(BF)
        dq_ref[...] = dq
        n, r = _rms(x_ref[...])
        gnv = gn_ref[...]
        h_ref[...] = (n * gnv).astype(BF)
        dx, dgn = _rms_bwd(_nt(dq, wq_ref[...]), n, r, gnv)
        dx_ref[...] = dxov + dx
        dgn_ref[...] += dgn

    row = pl.BlockSpec((tm, d), lambda i: (i, 0))
    return _call(
        body, name="xattn_bwd", grid=(t // tm,),
        out_shape=(S((t, d), F32), S((1, d), F32), S((nkv, 2 * d), F32), S((t, d), BF), S((t, d), BF), S((t, d), BF)),
        in_specs=[row, row, _resident((1, d)), row, _resident(kv.shape), _resident(wq.shape), _resident(wo.shape)],
        out_specs=(row, _acc_spec((1, d)), _acc_spec((nkv, 2 * d)), row, row, row),
        sem=("arbitrary",), args=(dxo, x, gn, q, kv, wq, wo), hook=hook)


def _mem_bwd(dkv, mh, mem, gm, wkv, pair=False):
    m, d = mem.shape
    rows, cols = wkv.shape
    rh = rows // 2

    def body(dkv_ref, mh_ref, mem_ref, gm_ref, w_ref, *outs):
        dkvb = dkv_ref[...].astype(BF)
        dw = _tn(mh_ref[...], dkvb).astype(BF)
        dmh = _nt(dkvb, w_ref[...])
        n, _ = _rms(mem_ref[...])
        if not pair:
            dw_ref, dgm_ref = outs
            dw_ref[...] = dw
        else:
            mine_ref, sib_ref, dgm_ref, whole, ssem, rsem = outs
            x, y, c = _place()
            whole[...] = dw
            cp = _remote(whole.at[pl.ds(pl.multiple_of((1 - c) * rh, 16), rh), :], sib_ref, ssem, rsem, (x, y, 1 - c))
            cp.start()
            mine_ref[...] = whole[pl.ds(pl.multiple_of(c * rh, 16), rh), :]
            cp.wait()
        dgm_ref[...] = jnp.sum(dmh * n, axis=0, keepdims=True)

    if not pair:
        return pl.pallas_call(body, name="mem_bwd", out_shape=(S(wkv.shape, BF), S((1, d), F32)),
                              compiler_params=_params())(dkv, mh, mem, gm, wkv)
    mine, sib, dgm = pl.pallas_call(
        body, name="mem_bwd", out_shape=(S((rh, cols), BF), S((rh, cols), BF), S((1, d), F32)),
        in_specs=[VMEM_SPEC] * 5, out_specs=(VMEM_SPEC, ANY, VMEM_SPEC),
        scratch_shapes=[pltpu.VMEM((rows, cols), BF), pltpu.SemaphoreType.DMA, pltpu.SemaphoreType.DMA],
        compiler_params=_params())(dkv, mh, mem, gm, wkv)
    return (mine, sib), dgm


def _mix_out_bwd(dxo, e, attn, conv_w, w_out, nqkv, hook=None):
    t, d = attn.shape
    tm = 256
    nt = t // tm
    f32 = lambda ref: ref[...].astype(F32)

    def body(dxo_ref, dxn_ref, c_ref, b_ref, u_ref, ga_ref, gc_ref, ch_ref, uh_ref, bn_ref, gcn_ref,
             attn_ref, cw_ref, w_ref, dattn_ref, de_ref, dcw_ref, dxh_ref):
        i = pl.program_id(0)

        @pl.when(i == 0)
        def _():
            dcw_ref[...] = jnp.zeros_like(dcw_ref)
        dxh = dxo_ref[...].astype(BF)
        dxh_ref[...] = dxh
        w = w_ref[...]
        dm = _nt(dxh, w)
        dmn = _nt(dxn_ref[...].astype(BF), w)
        cv, bv, uv = f32(c_ref), f32(b_ref), f32(u_ref)
        sga = _sigmoid(f32(ga_ref))
        sgc = _sigmoid(f32(gc_ref))
        z = cv * uv
        z1, z2 = _conv_taps(z, f32(ch_ref) * f32(uh_ref), i == 0)
        w0, w1, w2 = cw_ref[0:1, :], cw_ref[1:2, :], cw_ref[2:3, :]
        s = w0 * z2 + w1 * z1 + w2 * z
        av = attn_ref[...]
        dattn_ref[...] = (dm * sga).astype(BF)
        dconv = dm * sgc
        ds = dconv * bv
        dsn = jnp.where(i == nt - 1, 0.0, dmn * _sigmoid(gcn_ref[0:8, :].astype(F32)) * bn_ref[0:8, :].astype(F32))
        row = lax.broadcasted_iota(I32, (tm, 1), 0)
        dsp1 = jnp.where(row == tm - 1, dsn[0:1, :], pltpu.roll(ds, tm - 1, 0))
        dsp2 = jnp.where(row == tm - 2, dsn[0:1, :], jnp.where(row == tm - 1, dsn[1:2, :], pltpu.roll(ds, tm - 2, 0)))
        dz = w2 * ds + w1 * dsp1 + w0 * dsp2
        de_ref[:, nqkv:nqkv + d] = (dz * uv).astype(BF)
        de_ref[:, nqkv + d:nqkv + 2 * d] = (dconv * s).astype(BF)
        de_ref[:, nqkv + 2 * d:nqkv + 3 * d] = (dz * cv).astype(BF)
        de_ref[:, nqkv + 3 * d:nqkv + 4 * d] = (dm * av * sga * (1.0 - sga)).astype(BF)
        de_ref[:, nqkv + 4 * d:nqkv + 5 * d] = (dm * (bv * s) * sgc * (1.0 - sgc)).astype(BF)
        dcw_ref[0:1, :] += jnp.sum(ds * z2, axis=0, keepdims=True)
        dcw_ref[1:2, :] += jnp.sum(ds * z1, axis=0, keepdims=True)
        dcw_ref[2:3, :] += jnp.sum(ds * z, axis=0, keepdims=True)

    ecol = lambda cb: pl.BlockSpec((tm, d), lambda i: (i, cb))
    prev = lambda cb: pl.BlockSpec((HALO, d), lambda i: (jnp.maximum(i * (tm // HALO) - 1, 0), cb))
    nxt = lambda rows, cb: pl.BlockSpec((rows, d), lambda i: (jnp.minimum((i + 1) * (tm // rows), t // rows - 1), cb))
    row = pl.BlockSpec((tm, d), lambda i: (i, 0))
    return _call(
        body, name="mix_out_bwd", grid=(nt,),
        out_shape=(S((t, d), BF), S((t, nqkv + 5 * d), BF), S((8, d), F32), S((t, d), BF)),
        in_specs=[row, nxt(8, 0), ecol(0), ecol(1), ecol(2), ecol(3), ecol(4), prev(0), prev(2), nxt(HALO, 1), nxt(HALO, 4),
                  row, _resident(conv_w.shape), _resident(w_out.shape)],
        out_specs=(row, pl.BlockSpec((tm, nqkv + 5 * d), lambda i: (i, 0)), _acc_spec((8, d)), row),
        sem=("arbitrary",), args=(dxo, dxo, e, e, e, e, e, e, e, e, e, attn, conv_w, w_out), hook=hook)


def _swa_bwd(qkv, dattn, pos_col, pos_row, bias_t, sinks, dp, hook=None):
    t = qkv.shape[0]
    nb = t // BLOCK
    qw = SWA_HEADS * HEAD_DIM
    kw = SWA_KV_HEADS * HEAD_DIM

    def body(q_ref, kp_ref, kc_ref, vp_ref, vc_ref, do_ref, pq_ref, pkp_ref, pkc_ref, bias_ref, sink_ref,
             dp_in, dp_ref, gb_ref, dsk_ref, carry_ref, dq_ref):
        b = pl.program_id(0)

        @pl.when(b == 0)
        def _():
            gb_ref[...] = jnp.zeros_like(gb_ref)
            dsk_ref[...] = jnp.zeros_like(dsk_ref)
            carry_ref[...] = jnp.zeros_like(carry_ref)
            dq_ref[...] = jnp.zeros_like(dq_ref)
        dp_ref[:, 0:qw] = dq_ref[...]

        @pl.when(b < nb)
        def _():
            vis = _swa_visible(b, pq_ref, pkp_ref, pkc_ref)
            k2 = jnp.concatenate([kp_ref[...], kc_ref[...]], axis=0)
            v2 = jnp.concatenate([vp_ref[...], vc_ref[...]], axis=0)
            for hk in range(SWA_KV_HEADS):
                lo, hi = hk * HEAD_DIM, (hk + 1) * HEAD_DIM
                kh = k2[:, lo:hi]
                vh = v2[:, lo:hi]
                qg = _group_heads(q_ref, hk)
                dog = _group_heads(do_ref, hk)
                pn, psn = _swa_probs(qg, kh, vis, bias_ref[hk], _group_sinks(sink_ref, hk))
                dp = _nt(dog, vh)
                delta = jnp.sum(pn * dp, axis=-1, keepdims=True)
                ds = pn * (dp - delta)
                gb_ref[hk] += ds
                dsk_ref[hk] += -psn * delta
                dsb = (ds * (HEAD_DIM ** -0.5)).astype(BF)
                dqg = _nn(dsb, kh).astype(BF)
                for g in range(SWA_GROUP):
                    h = hk * SWA_GROUP + g
                    dq_ref[:, h * HEAD_DIM:(h + 1) * HEAD_DIM] = dqg[g * BLOCK:(g + 1) * BLOCK]
                dk = _tn(dsb, qg)
                dv = _tn(pn.astype(BF), dog)
                dp_ref[:, qw + lo:qw + hi] = (carry_ref[:, lo:hi] + dk[0:BLOCK]).astype(BF)
                dp_ref[:, qw + kw + lo:qw + kw + hi] = (carry_ref[:, kw + lo:kw + hi] + dv[0:BLOCK]).astype(BF)
                carry_ref[:, lo:hi] = dk[BLOCK:2 * BLOCK]
                carry_ref[:, kw + lo:kw + hi] = dv[BLOCK:2 * BLOCK]

        @pl.when(b == nb)
        def _():
            dp_ref[:, qw:qw + 2 * kw] = carry_ref[...].astype(BF)

    cur = lambda b: jnp.minimum(b, nb - 1)
    prev = lambda b: jnp.maximum(cur(b) - 1, 0)
    return _call(
        body, name="swa_bwd", grid=(nb + 1,),
        out_shape=(S(dp.shape, BF), S((SWA_KV_HEADS, GROUP_ROWS, 2 * BLOCK), F32), S((SWA_KV_HEADS, GROUP_ROWS, 1), F32)),
        in_specs=[
            pl.BlockSpec((BLOCK, qw), lambda b: (cur(b), 0)),
            pl.BlockSpec((BLOCK, kw), lambda b: (prev(b), qw // kw)),
            pl.BlockSpec((BLOCK, kw), lambda b: (cur(b), qw // kw)),
            pl.BlockSpec((BLOCK, kw), lambda b: (prev(b), qw // kw + 1)),
            pl.BlockSpec((BLOCK, kw), lambda b: (cur(b), qw // kw + 1)),
            pl.BlockSpec((BLOCK, qw), lambda b: (cur(b), 0)),
            pl.BlockSpec((BLOCK, 1), lambda b: (cur(b), 0)),
            pl.BlockSpec((1, BLOCK), lambda b: (0, prev(b))),
            pl.BlockSpec((1, BLOCK), lambda b: (0, cur(b))),
            _resident(bias_t.shape),
            SMEM_SPEC,
            ANY,
        ],
        out_specs=(
            pl.BlockSpec((BLOCK, qw + 2 * kw), lambda b: (jnp.maximum(b - 1, 0), 0)),
            _acc_spec((SWA_KV_HEADS, GROUP_ROWS, 2 * BLOCK)),
            _acc_spec((SWA_KV_HEADS, GROUP_ROWS, 1)),
        ),
        scratch_shapes=[pltpu.VMEM((BLOCK, 2 * kw), F32), pltpu.VMEM((BLOCK, qw), BF)],
        aliases={11: 0}, sem=("arbitrary",),
        args=(qkv, qkv, qkv, qkv, qkv, dattn, pos_col, pos_row, pos_row, bias_t, sinks, dp), hook=hook)


def _bias_reduce(gb, dsk):
    def body(gb_ref, dsk_ref, drb_ref, dsink_ref):
        bucket = _t5_bucket(_block_rel())
        for b in range(REL_BUCKETS):
            mask = bucket == b
            for h in range(SWA_HEADS):
                drb_ref[b, h] = jnp.sum(jnp.where(mask, gb_ref[h], 0.0))
        for h in range(SWA_HEADS):
            dsink_ref[0, h] = jnp.sum(dsk_ref[h])

    return pl.pallas_call(
        body, name="bias_reduce", out_shape=(S((REL_BUCKETS, SWA_HEADS), F32), S((1, SWA_HEADS), F32)),
        in_specs=[VMEM_SPEC, VMEM_SPEC], out_specs=(SMEM_SPEC, SMEM_SPEC),
    )(gb, dsk)


def _mix_in_bwd(dp, x, gn, w_in, dxo):
    t, d = x.shape
    tm = 256
    npr = dp.shape[1]

    def body(dp_ref, x_ref, gn_ref, w_ref, dxo_ref, dx_ref, dgn_ref, h_ref):
        @pl.when(pl.program_id(0) == 0)
        def _():
            dgn_ref[...] = jnp.zeros_like(dgn_ref)
        dh = jnp.zeros((tm, d), F32)
        for c0 in range(0, npr, 1024):
            c1 = min(c0 + 1024, npr)
            dh = dh + _nt(dp_ref[:, c0:c1], w_ref[:, c0:c1])
        n, r = _rms(x_ref[...])
        gnv = gn_ref[...]
        h_ref[...] = (n * gnv).astype(BF)
        dx, dgn = _rms_bwd(dh, n, r, gnv)
        dx_ref[...] = dxo_ref[...] + dx
        dgn_ref[...] += dgn

    row = pl.BlockSpec((tm, d), lambda i: (i, 0))
    return pl.pallas_call(
        body, name="mix_in_bwd", grid=(t // tm,),
        out_shape=(S((t, d), F32), S((1, d), F32), S((t, d), BF)),
        in_specs=[pl.BlockSpec((tm, npr), lambda i: (i, 0)), row, _resident((1, d)), _resident(w_in.shape), row],
        out_specs=(row, _acc_spec((1, d)), row),
        compiler_params=_params(("arbitrary",)),
    )(dp, x, gn, w_in, dxo)


CAST_STEPS = 4


def _to_bf16(arrays, name, hook=None):
    def body(*refs):
        for src, dst in zip(refs[:len(arrays)], refs[len(arrays):]):
            dst[...] = src[...].astype(BF)

    blocks = [pl.BlockSpec((a.shape[0] // CAST_STEPS, a.shape[1]), lambda i: (i, 0)) for a in arrays]
    return _call(body, name=name, grid=(CAST_STEPS,), out_shape=tuple(S(a.shape, BF) for a in arrays), in_specs=blocks,
                 out_specs=tuple(blocks), sem=("parallel",), args=list(arrays), hook=hook)


PAIR_STEPS = 4


def _add_bf16(pairs, name):
    def body(*refs):
        ins, outs = refs[:2 * len(pairs)], refs[2 * len(pairs):]
        for q, o_ref in enumerate(outs):
            o_ref[...] = (ins[2 * q][...].astype(F32) + ins[2 * q + 1][...].astype(F32)).astype(BF)

    in_specs, out_specs, out_shape, args = [], [], [], []
    for a, b in pairs:
        rows, cols = a.shape
        blk = pl.BlockSpec((rows // PAIR_STEPS, cols), lambda i: (i, 0))
        in_specs += [blk, blk]
        out_specs.append(blk)
        out_shape.append(S((rows, cols), BF))
        args += [a, b]
    return pl.pallas_call(body, name=name, grid=(PAIR_STEPS,), out_shape=tuple(out_shape), in_specs=in_specs,
                          out_specs=tuple(out_specs), compiler_params=_params(("parallel",)))(*args)


def _adam_update(w, g, m, v):
    mn = ADAM_B1 * m + (1.0 - ADAM_B1) * g
    vn = ADAM_B2 * v + (1.0 - ADAM_B2) * (g * g)
    m_hat = mn / (1.0 - ADAM_B1 ** ADAM_STEP)
    v_hat = vn / (1.0 - ADAM_B2 ** ADAM_STEP)
    return -ADAM_LR * (m_hat / (jnp.sqrt(v_hat) + ADAM_EPS) + ADAM_WD * w), mn, vn


def _adamw(quads, name, steps, echo=False, hook=None):
    n_out = 4 if echo else 3

    def body(*refs):
        ins, outs = refs[:4 * len(quads)], refs[4 * len(quads):]
        for q in range(len(quads)):
            w_ref, g_ref, m_ref, v_ref = ins[4 * q:4 * q + 4]
            res = outs[n_out * q:n_out * q + n_out]
            gv = g_ref[...]
            if echo:
                res[0][...] = gv
            res[-3][...], res[-2][...], res[-1][...] = _adam_update(w_ref[...], gv, m_ref[...], v_ref[...])

    in_specs, out_specs, out_shape, args = [], [], [], []
    for quad in quads:
        rows, cols = quad[0].shape
        blk = pl.BlockSpec((rows // steps, cols), lambda i: (i, 0))
        in_specs += [blk] * 4
        out_specs += [blk] * n_out
        out_shape += [S((rows, cols), F32)] * n_out
        args += list(quad)
    res = _call(body, name=name, grid=(steps,), out_shape=tuple(out_shape), in_specs=in_specs, out_specs=tuple(out_specs),
                sem=("parallel",), args=args, hook=hook)
    return [res[n_out * q:n_out * q + n_out] for q in range(len(quads))]


def _place():
    x, y, c = lax.axis_index("x"), lax.axis_index("y"), lax.axis_index("c")
    return x, y, c


OTHER_CHIPS = ((1, 0), (0, 1), (1, 1))


def _flip(v, f):
    return 1 - v if f else v


def _remote(src, dst, ssem, rsem, dev):
    return pltpu.make_async_remote_copy(src_ref=src, dst_ref=dst, send_sem=ssem, recv_sem=rsem,
                                        device_id=dev, device_id_type=MESH)


class _Both:
    def __init__(self, hooks):
        self.hooks = hooks
        self.ins = [a for h in hooks for a in h.ins]
        self.out_shape = tuple(o for h in hooks for o in h.out_shape)
        self.scratch = [x for h in hooks for x in h.scratch]

    def _each(self, ins, outs, scr):
        i = o = s = 0
        for h in self.hooks:
            ni, no, ns = len(h.ins), len(h.out_shape), len(h.scratch)
            yield h, ins[i:i + ni], outs[o:o + no], scr[s:s + ns]
            i, o, s = i + ni, o + no, s + ns

    def start(self, ins, outs, scr):
        for h, *refs in self._each(ins, outs, scr):
            h.start(*refs)

    def finish(self, ins, outs, scr):
        for h, *refs in self._each(ins, outs, scr):
            h.finish(*refs)

    @property
    def results(self):
        return [h.results for h in self.hooks]

    @results.setter
    def results(self, res):
        for h, _, mine, _ in self._each((), tuple(res), ()):
            h.results = mine


class _GatherHook:
    def __init__(self, shards, kinds):
        self.ins, self.kinds, n = list(shards), list(kinds), len(shards)
        self.out_shape = tuple(
            S((w.shape[0], 4 * w.shape[1]), BF) if k == "col" else S((4,) + w.shape, BF) for w, k in zip(shards, kinds))
        dma = pltpu.SemaphoreType.DMA
        self.scratch = ([dma((n, 3)) for _ in range(5)] + [dma((n,)), dma((n,))]
                        + [pltpu.VMEM((3, w.shape[0] // 2, w.shape[1]), BF) for w in shards]
                        + [pltpu.VMEM(w.shape, BF) for w in shards])

    def _window(self, outs, i, s, half):
        rows, cols = self.ins[i].shape
        rh = rows // 2
        start = pl.multiple_of(half * rh, 16)
        if self.kinds[i] == "col":
            return outs[i].at[pl.ds(start, rh), pl.ds(pl.multiple_of(s * cols, 128), cols)]
        return outs[i].at[s, pl.ds(start, rh), :]

    def _copies(self, ins, outs, scr):
        n = len(ins)
        ssem, rsem, fssem, frsem, ksem, lsem, osem = scr[:7]
        land, own = scr[7:7 + n], scr[7 + n:7 + 2 * n]
        x, y, c = _place()
        sibling = (x, y, 1 - c)
        loads, stores, sends, forwards, keeps, passed = [], [], [], [], [], []
        for i in range(n):
            rows, cols = self.ins[i].shape
            rh = rows // 2
            mine = (outs[i].at[:, pl.ds(pl.multiple_of((2 * x + y) * cols, 128), cols)] if self.kinds[i] == "col"
                    else outs[i].at[2 * x + y])
            loads.append(functools.partial(pltpu.make_async_copy, ins[i], own[i], lsem.at[i]))
            stores.append(functools.partial(pltpu.make_async_copy, own[i], mine, osem.at[i]))
            src = ins[i].at[pl.ds(pl.multiple_of(c * rh, 16), rh), :]
            for j, (fx, fy) in enumerate(OTHER_CHIPS):
                px, py = _flip(x, fx), _flip(y, fy)
                sends.append(functools.partial(_remote, src, land[i].at[j], ssem.at[i, j], rsem.at[i, j], (px, py, c)))
                here = self._window(outs, i, 2 * px + py, c)
                forwards.append(functools.partial(_remote, land[i].at[j], here, fssem.at[i, j], frsem.at[i, j], sibling))
                keeps.append(functools.partial(pltpu.make_async_copy, land[i].at[j], here, ksem.at[i, j]))
                there = self._window(outs, i, 2 * px + py, 1 - c)
                passed.append(functools.partial(_remote, there, there, fssem.at[i, j], frsem.at[i, j], sibling))
        return loads, stores, sends, forwards, keeps, passed

    def start(self, ins, outs, scr):
        loads, _, sends, _, _, _ = self._copies(ins, outs, scr)
        for make in sends + loads:
            make().start()

    def finish(self, ins, outs, scr):
        loads, stores, sends, forwards, keeps, passed = self._copies(ins, outs, scr)
        for load, store in zip(loads, stores):
            load().wait()
            store().start()
        for send, forward, keep in zip(sends, forwards, keeps):
            send().wait_recv()
            forward().start()
            keep().start()
        for make in passed:
            make().wait_recv()
        for make in sends + forwards:
            make().wait_send()
        for make in keeps + stores:
            make().wait()


class _ChipsHook:
    def __init__(self, parts, kinds):
        self.ins, self.kinds, n = list(parts), list(kinds), len(parts)
        self.out_shape = tuple(
            S((4, p.shape[0], p.shape[1] // 4), BF) if k == "col" else S(p.shape, BF) for p, k in zip(parts, kinds))
        dma = pltpu.SemaphoreType.DMA
        self.scratch = ([dma((n, 3)), dma((n, 3)), dma((n,)), dma((n,))]
                        + [pltpu.VMEM(o.shape[1:], BF) for o in self.out_shape])

    def _slab(self, ins, i, s):
        _, rows, cols = self.out_shape[i].shape
        if self.kinds[i] == "col":
            return ins[i].at[:, pl.ds(pl.multiple_of(s * cols, 128), cols)]
        return ins[i].at[s]

    def _copies(self, ins, outs, scr):
        ssem, rsem, lsem, osem = scr[:4]
        own = scr[4:]
        x, y, c = _place()
        loads, stores, sends = [], [], []
        for i in range(len(ins)):
            loads.append(functools.partial(pltpu.make_async_copy, self._slab(ins, i, 2 * x + y), own[i], lsem.at[i]))
            stores.append(functools.partial(pltpu.make_async_copy, own[i], outs[i].at[3], osem.at[i]))
            for j, (fx, fy) in enumerate(OTHER_CHIPS):
                px, py = _flip(x, fx), _flip(y, fy)
                sends.append(functools.partial(_remote, self._slab(ins, i, 2 * px + py), outs[i].at[j], ssem.at[i, j],
                                               rsem.at[i, j], (px, py, c)))
        return loads, stores, sends

    def start(self, ins, outs, scr):
        loads, _, sends = self._copies(ins, outs, scr)
        for make in sends + loads:
            make().start()

    def finish(self, ins, outs, scr):
        loads, stores, sends = self._copies(ins, outs, scr)
        for load, store in zip(loads, stores):
            load().wait()
            store().start()
        for make in sends + stores:
            make().wait()


SHARE_STEPS = 2


def _sum_share(slab_list, name, hook=None):
    n = len(slab_list)
    geom = [(sl.shape[1], sl.shape[1] // SHARE_STEPS, sl.shape[2]) for sl in slab_list]

    def body(*refs):
        ins, outs, scr = refs[:n], refs[n:2 * n], refs[2 * n:]
        i = pl.program_id(0)
        x, y, c = _place()
        sibling = (x, y, 1 - c)

        def copies(q, k):
            rh, tr, _ = geom[q]
            stage, lsem, ssem, rsem = scr[4 * q:4 * q + 4]
            dst = outs[q].at[pl.ds(pl.multiple_of(c * rh + k * tr, 8), tr), :]
            return (pltpu.make_async_copy(stage.at[k], dst, lsem.at[k]),
                    _remote(stage.at[k], dst, ssem.at[k], rsem, sibling))

        for q in range(n):
            acc = ins[q][3].astype(F32)
            for k in range(3):
                acc = acc + ins[q][k].astype(F32)
            scr[4 * q][i] = acc
            for cp in copies(q, i):
                cp.start()

        @pl.when(i == SHARE_STEPS - 1)
        def _():
            for q in range(n):
                rh = geom[q][0]
                for k in range(SHARE_STEPS):
                    local, remote = copies(q, k)
                    local.wait()
                    remote.wait_send()
                got = outs[q].at[pl.ds(pl.multiple_of((1 - c) * rh, 8), rh), :]
                _remote(got, got, scr[4 * q + 2].at[0], scr[4 * q + 3], sibling).wait_recv()

    dma = pltpu.SemaphoreType.DMA
    scratch = []
    for rh, tr, cols in geom:
        scratch += [pltpu.VMEM((SHARE_STEPS, tr, cols), F32), dma((SHARE_STEPS,)), dma((SHARE_STEPS,)), dma]
    return _call(
        body, name=name, grid=(SHARE_STEPS,), out_shape=tuple(S((2 * rh, cols), F32) for rh, _, cols in geom),
        in_specs=[pl.BlockSpec((4, tr, cols), lambda i: (0, i, 0)) for _, tr, cols in geom], out_specs=(ANY,) * n,
        scratch_shapes=scratch, sem=("arbitrary",), args=list(slab_list), hook=hook)


class _SmallSumHook:
    def __init__(self, buf):
        self.ins, self.out_shape = [buf], (S(buf.shape, F32),)
        dma = pltpu.SemaphoreType.DMA
        self.scratch = [pltpu.VMEM((8,) + buf.shape, F32), pltpu.VMEM(buf.shape, F32), dma((7,)), dma((7,)), dma]

    def _sends(self, scr):
        slots, _, ssem, rsem, _ = scr
        x, y, c = _place()
        me = 4 * x + 2 * y + c
        for r in range(1, 8):
            px, py, pc = _flip(x, (r >> 2) & 1), _flip(y, (r >> 1) & 1), _flip(c, r & 1)
            yield (functools.partial(_remote, slots.at[me], slots.at[me], ssem.at[r - 1], rsem.at[r - 1], (px, py, pc)),
                   functools.partial(_remote, slots.at[me], slots.at[4 * px + 2 * py + pc], ssem.at[r - 1],
                                     rsem.at[r - 1], (px, py, pc)))

    def start(self, ins, outs, scr):
        slots, _, _, _, lsem = scr
        x, y, c = _place()
        load = pltpu.make_async_copy(ins[0], slots.at[4 * x + 2 * y + c], lsem)
        load.start()
        load.wait()
        for send, _ in self._sends(scr):
            send().start()

    def finish(self, ins, outs, scr):
        slots, total, _, _, lsem = scr
        for _, arrival in self._sends(scr):
            arrival().wait_recv()
        for send, _ in self._sends(scr):
            send().wait_send()
        acc = slots[0]
        for k in range(1, 8):
            acc = acc + slots[k]
        total[...] = acc
        store = pltpu.make_async_copy(total, outs[0], lsem)
        store.start()
        store.wait()


BIG = ("ffn1_w_gu", "ffn1_w_down", "w_in", "w_out", "xattn_wq", "xattn_wkv", "xattn_wo", "ffn2_w_gu", "ffn2_w_down")
KIND = {"ffn1_w_gu": "col", "ffn1_w_down": "row", "w_in": "col", "w_out": "row", "xattn_wq": "row",
        "xattn_wkv": "col", "xattn_wo": "row", "ffn2_w_gu": "col", "ffn2_w_down": "row"}
WEIGHTS = ("rel_bias", "ffn1_norm", "ffn1_w_gu", "ffn1_w_down", "mix_norm", "w_in", "sinks", "conv_w", "w_out",
           "xattn_norm", "mem_norm", "xattn_wq", "xattn_wkv", "xattn_wo", "ffn2_norm", "ffn2_w_gu", "ffn2_w_down",
           "final_norm")
SMALL_ROWS = 48
GAIN_ROW = {"ffn1_norm": 0, "mix_norm": 1, "xattn_norm": 2, "mem_norm": 3, "ffn2_norm": 4, "final_norm": 5}
CONV_ROW, SINK_ROW, LOSS_ROW, BIAS_ROW = 6, 9, 11, 16
TAP_ROWS = 8


def _rows_block(rows, d):
    buf = jnp.zeros((SMALL_ROWS, d), F32)
    for r, v in rows.items():
        buf = lax.dynamic_update_slice(buf, v if v.ndim == 2 else v.reshape(1, -1), (r, 0))
    return buf


def _adamw_small(gsum, conv_g, small):
    names = list(small)

    def grad(k, gsum_ref, conv_ref):
        rows, cols = small[k][0].shape
        if k in GAIN_ROW:
            return gsum_ref[GAIN_ROW[k]:GAIN_ROW[k] + 1, :]
        if k == "conv_w":
            return conv_ref[...]
        r0 = SINK_ROW if k == "sinks" else BIAS_ROW
        return gsum_ref[r0:r0 + rows, 0:cols]

    def body(gsum_ref, conv_ref, *refs):
        ins, outs = refs[:3 * len(names)], refs[3 * len(names):]
        for q, k in enumerate(names):
            w_ref, m_ref, v_ref = ins[3 * q:3 * q + 3]
            g_ref, d_ref, mo_ref, vo_ref = outs[4 * q:4 * q + 4]
            gv = grad(k, gsum_ref, conv_ref)
            g_ref[...] = gv
            d_ref[...], mo_ref[...], vo_ref[...] = _adam_update(w_ref[...], gv, m_ref[...], v_ref[...])

    res = pl.pallas_call(
        body, name="adamw_small", out_shape=tuple(S(small[k][0].shape, F32) for k in names for _ in range(4)),
        compiler_params=_params())(gsum, conv_g, *[a for k in names for a in small[k]])
    return {k: res[4 * q:4 * q + 4] for q, k in enumerate(names)}


def _local_step(x, mem, pos, target, w, gains, rel_bias, sinks, conv_w, shards=None):
    t, d = x.shape
    dist = shards is not None
    w = dict(w)
    grads, slabs = {}, {}
    pos_col = pos.reshape(t, 1)
    pos_row = pos.reshape(1, t)
    bias_t = _bias_build(rel_bias).reshape(SWA_KV_HEADS, GROUP_ROWS, 2 * BLOCK)

    def gather(names):
        return _GatherHook([shards[k] for k in names], [KIND[k] for k in names]) if dist else None

    def gathered(names, hook):
        if dist:
            for k, gw in zip(names, hook.results):
                w[k] = gw if KIND[k] == "col" else gw.reshape(-1, gw.shape[-1])

    def dw(problems, tn, name, hook=None):
        if dist:
            res = _dw_pair([(a, b) for _, a, b in problems], tn, name, KIND[problems[0][0]], hook)
        else:
            res = [_dw(a, b, tn, name + "_" + k) for k, a, b in problems]
        grads.update(zip((k for k, _, _ in problems), res))

    def pair_sums(names):
        if not dist:
            return None
        flat = lambda k, v: v.reshape(-1, v.shape[-1]) if KIND[k] == "row" else v
        sums = _add_bf16([(flat(k, grads[k][0]), flat(k, grads[k][1])) for k in names], "pair_sum_" + names[0])
        return {k: (p if KIND[k] == "col" else p.reshape(4, -1, p.shape[-1])) for k, p in zip(names, sums)}

    def chips(names, parts):
        return _ChipsHook([parts[k] for k in names], [KIND[k] for k in names]) if dist else None

    def reduced(names, hook):
        if dist:
            slabs.update(zip(names, hook.results))

    names = ("w_in", "w_out")
    hook = gather(names)
    x1, g1, u1 = _ffn_fwd(x, gains["ffn1_norm"], w["ffn1_w_gu"], w["ffn1_w_down"], "ffn1_fwd", hook)
    gathered(names, hook)
    names = ("xattn_wq", "xattn_wkv", "xattn_wo")
    hook = gather(names)
    qkv, e = _mix_proj(x1, gains["mix_norm"], w["w_in"], hook)
    gathered(names, hook)
    names = ("ffn2_w_gu",)
    hook = gather(names)
    attn = _swa_fwd(qkv, pos_col, pos_row, bias_t, sinks, hook)
    gathered(names, hook)
    names = ("ffn2_w_down",)
    hook = gather(names)
    x2, merged = _mix_out_fwd(e, attn, conv_w, w["w_out"], x1, hook)
    gathered(names, hook)
    mh, kv = _mem_kv(mem, gains["mem_norm"], w["xattn_wkv"])
    x3, qx, o = _xattn_fwd(x2, gains["xattn_norm"], w["xattn_wq"], kv, w["xattn_wo"])
    dx4, g2, u2, d_final, loss = _ffn_fwd(x3, gains["ffn2_norm"], w["ffn2_w_gu"], w["ffn2_w_down"], "ffn2_fwd",
                                          head=(gains["final_norm"], target))

    dx3, d_ffn2, dgu2, a2, h4, dyh4 = _ffn_bwd(dx4, x3, gains["ffn2_norm"], g2, u2, w["ffn2_w_gu"], w["ffn2_w_down"],
                                               "ffn2_bwd")
    dw([("ffn2_w_gu", h4, dgu2)], 1408, "dw_ffn2_gu")
    dw([("ffn2_w_down", a2, dyh4)], 512, "dw_ffn2_down")
    parts = pair_sums(("ffn2_w_gu", "ffn2_w_down"))
    hook = chips(("ffn2_w_gu",), parts)
    dx2, d_xattn, dkv, dxh3, h3, dqx = _xattn_bwd(dx3, x2, gains["xattn_norm"], qx, kv, w["xattn_wq"], w["xattn_wo"], hook)
    reduced(("ffn2_w_gu",), hook)
    grads["xattn_wkv"], d_mem = _mem_bwd(dkv, mh, mem, gains["mem_norm"], w["xattn_wkv"], dist)
    hook = chips(("ffn2_w_down",), parts)
    dattn, dp, dcw, dxh2 = _mix_out_bwd(dx2, e, attn, conv_w, w["w_out"], qkv.shape[1], hook)
    reduced(("ffn2_w_down",), hook)
    dw([("xattn_wo", o, dxh3), ("xattn_wq", h3, dqx), ("w_out", merged, dxh2)], 1024, "dw_wo_wq_wout")
    names = ("xattn_wo", "xattn_wq", "xattn_wkv", "w_out")
    hook = chips(names, pair_sums(names))
    dp, gb, dsk = _swa_bwd(qkv, dattn, pos_col, pos_row, bias_t, sinks, dp, hook)
    reduced(names, hook)
    d_rel_bias, d_sinks = _bias_reduce(gb.reshape(SWA_HEADS, BLOCK, 2 * BLOCK), dsk.reshape(SWA_HEADS, BLOCK, 1))
    dx1, d_mix, h2 = _mix_in_bwd(dp, x1, gains["mix_norm"], w["w_in"], dx2)
    dw([("w_in", h2, dp)], w["w_in"].shape[1] // 4, "dw_win")
    names = ("w_in",)
    hook = chips(names, pair_sums(names))
    dx0, d_ffn1, dgu1, a1, h1, dyh1 = _ffn_bwd(dx1, x, gains["ffn1_norm"], g1, u1, w["ffn1_w_gu"], w["ffn1_w_down"],
                                               "ffn1_bwd", hook)
    reduced(names, hook)
    dw([("ffn1_w_down", a1, dyh1)], 512, "dw_ffn1_down")
    names = ("ffn1_w_down",)
    hook = chips(names, pair_sums(names))
    dw([("ffn1_w_gu", h1, dgu1)], 1408, "dw_ffn1_gu", hook)
    reduced(names, hook)
    rows = {0: d_ffn1, 1: d_mix, 2: d_xattn, 3: d_mem, 4: d_ffn2, 5: d_final, SINK_ROW: d_sinks, BIAS_ROW: d_rel_bias,
            LOSS_ROW: loss[0, 0:1]}
    rows.update({CONV_ROW + j: dcw[j] for j in range(3)})
    last = ("ffn1_w_gu",)
    return dx0, (slabs if dist else grads), _rows_block(rows, d), chips(last, pair_sums(last))


def kernel(x, mem, positions, rel_bias, ffn1_norm, ffn1_w_gu, ffn1_w_down, mix_norm, w_in, sinks, conv_w, w_out, xattn_norm, mem_norm, xattn_wq, xattn_wkv, xattn_wo, ffn2_norm, ffn2_w_gu, ffn2_w_down, final_norm, loss_target, m_rel_bias, m_ffn1_norm, m_ffn1_w_gu, m_ffn1_w_down, m_mix_norm, m_w_in, m_sinks, m_conv_w, m_w_out, m_xattn_norm, m_mem_norm, m_xattn_wq, m_xattn_wkv, m_xattn_wo, m_ffn2_norm, m_ffn2_w_gu, m_ffn2_w_down, m_final_norm, v_rel_bias, v_ffn1_norm, v_ffn1_w_gu, v_ffn1_w_down, v_mix_norm, v_w_in, v_sinks, v_conv_w, v_w_out, v_xattn_norm, v_mem_norm, v_xattn_wq, v_xattn_wkv, v_xattn_wo, v_ffn2_norm, v_ffn2_w_gu, v_ffn2_w_down, v_final_norm):
    args = dict(locals())
    wts = {k: args[k] for k in WEIGHTS}
    mom = {k: args["m_" + k] for k in WEIGHTS}
    var = {k: args["v_" + k] for k in WEIGHTS}
    d = x.shape[-1]
    s_me = 2 * lax.axis_index("x") + lax.axis_index("y")

    first = ("ffn1_w_gu", "ffn1_w_down")
    rest = tuple(k for k in BIG if k not in first)
    shards = {k: wts[k][0].astype(BF) for k in first}
    cw_cols = conv_w.shape[-1]
    placed = lax.dynamic_update_slice(jnp.zeros((TAP_ROWS, d), F32), 0.5 * conv_w[0], (0, s_me * cw_cols))
    head = _Both([_GatherHook([shards[k] for k in first], [KIND[k] for k in first]), _SmallSumHook(placed)])
    shards.update(zip(rest, _to_bf16([wts[k][0] for k in rest], "gather_ffn1", head)))
    gathered, (conv_sum,) = head.results
    whole = {k: (gw if KIND[k] == "col" else gw.reshape(-1, gw.shape[-1])) for k, gw in zip(first, gathered)}
    conv_whole = conv_sum[0:3]

    gains = {k: wts[k].reshape(1, d) for k in GAIN_ROW}
    dx0, slabs, small, last_chips = _local_step(x[0], mem[0], positions[0], loss_target[0], whole, gains, rel_bias, sinks,
                                               conv_whole, shards)

    late = ("ffn1_w_gu", "ffn1_w_down", "ffn2_w_down")
    early = tuple(k for k in BIG if k not in late)
    first_ones = tuple(k for k in BIG if k != "ffn1_w_gu")
    tail = _Both([last_chips, _SmallSumHook(small)])
    shard_grads = dict(zip(first_ones, _sum_share([slabs[k] for k in first_ones], "sum_share", tail)))
    (last_slabs,), (small_sum,) = tail.results
    shard_grads["ffn1_w_gu"], = _sum_share([last_slabs], "sum_share_ffn1_w_gu")
    quad = lambda k: (wts[k][0], shard_grads[k], mom[k][0], var[k][0])
    updates = dict(zip(early, _adamw([quad(k) for k in early], "adamw_early", 16, echo=True)))
    updates.update(zip(late, _adamw([quad(k) for k in late], "adamw_late", 8, echo=True)))
    loss = small_sum[LOSS_ROW, 0]

    out_g, out_d, out_m, out_v = {}, {}, {}, {}
    for k in BIG:
        out_g[k], out_d[k], out_m[k], out_v[k] = (a[None] for a in updates[k])

    conv_g = lax.dynamic_slice(small_sum, (CONV_ROW, s_me * cw_cols), (3, cw_cols))
    two_d = lambda a: a.reshape(a.shape[-2:]) if a.ndim > 1 else a.reshape(1, -1)
    small_names = [k for k in WEIGHTS if k not in KIND]
    done = _adamw_small(small_sum, conv_g, {k: (two_d(wts[k]), two_d(mom[k]), two_d(var[k])) for k in small_names})
    for k in small_names:
        out_g[k], out_d[k], out_m[k], out_v[k] = (a.reshape(wts[k].shape) for a in done[k])

    return (loss, dx0[None], *[out_g[k] for k in WEIGHTS], *[out_d[k] for k in WEIGHTS],
            *[out_m[k] for k in WEIGHTS], *[out_v[k] for k in WEIGHTS])
```

```python
import functools
import math

import jax
import jax.numpy as jnp
from jax import lax
from jax.experimental import pallas as pl
from jax.experimental.pallas import tpu as pltpu

BF = jnp.bfloat16
F32 = jnp.float32
I32 = jnp.int32
S = jax.ShapeDtypeStruct

EPS = 1e-6
NEG = -1e30
POS_PAD = 1 << 30
WINDOW = 128
BLOCK = 128
HEAD_DIM = 64
SWA_HEADS = 16
SWA_KV_HEADS = 4
SWA_GROUP = SWA_HEADS // SWA_KV_HEADS
MEM_HEADS = 4
REL_BUCKETS = 32
REL_MAX_DIST = 128
ADAM_LR = 0.001
ADAM_B1 = 0.9
ADAM_B2 = 0.999
ADAM_EPS = 1e-08
ADAM_WD = 0.01
ADAM_STEP = 10

V7X_VMEM_LIMIT_BYTES = 56 * 1024 * 1024
MESH = pl.DeviceIdType.MESH
ANY = pl.BlockSpec(memory_space=pl.ANY)
VMEM_SPEC = pl.BlockSpec(memory_space=pltpu.VMEM)
SMEM_SPEC = pl.BlockSpec(memory_space=pltpu.SMEM)


def _params(sem=None):
    return pltpu.CompilerParams(dimension_semantics=sem, vmem_limit_bytes=V7X_VMEM_LIMIT_BYTES)


def _resident(shape):
    nd = len(shape)
    return pl.BlockSpec(shape, lambda *_: (0,) * nd, pipeline_mode=pl.Buffered(1))


def _acc_spec(shape):
    nd = len(shape)
    return pl.BlockSpec(shape, lambda *_: (0,) * nd)


def _call(body, *, name, grid, out_shape, in_specs, out_specs, args, sem, scratch_shapes=(), aliases=None, hook=None):
    aliases = aliases or {}
    if hook is None:
        return pl.pallas_call(body, name=name, grid=grid, out_shape=out_shape, in_specs=in_specs, out_specs=out_specs,
                              scratch_shapes=list(scratch_shapes), input_output_aliases=aliases,
                              compiler_params=_params(sem))(*args)
    n_in, n_out, n_scr = len(in_specs), len(out_shape), len(scratch_shapes)
    h_in, h_out = len(hook.ins), len(hook.out_shape)

    def at_step(pick):
        conds = [pl.program_id(ax) == pick(size) for ax, size in enumerate(grid)]
        return functools.reduce(jnp.logical_and, conds)

    def hosted(*refs):
        k_in, x_in = refs[:n_in], refs[n_in:n_in + h_in]
        o0 = n_in + h_in
        k_out, x_out = refs[o0:o0 + n_out], refs[o0 + n_out:o0 + n_out + h_out]
        s0 = o0 + n_out + h_out
        k_scr, x_scr = refs[s0:s0 + n_scr], refs[s0 + n_scr:]

        @pl.when(at_step(lambda size: 0))
        def _():
            hook.start(x_in, x_out, x_scr)
        body(*k_in, *k_out, *k_scr)
        early = bool(hook.lead) and grid[0] > hook.lead
        if early:
            on_axis0 = pl.program_id(0) == grid[0] - 1 - hook.lead
            rest_first = [pl.program_id(ax) == 0 for ax in range(1, len(grid))]

            @pl.when(functools.reduce(jnp.logical_and, rest_first, on_axis0))
            def _():
                hook.relay(x_in, x_out, x_scr)

        @pl.when(at_step(lambda size: size - 1))
        def _():
            if not early:
                hook.relay(x_in, x_out, x_scr)
            hook.finish(x_in, x_out, x_scr)

    res = pl.pallas_call(
        hosted, name=name, grid=grid, out_shape=tuple(out_shape) + tuple(hook.out_shape),
        in_specs=list(in_specs) + [ANY] * h_in, out_specs=tuple(out_specs) + (ANY,) * h_out,
        scratch_shapes=list(scratch_shapes) + list(hook.scratch), input_output_aliases=aliases,
        compiler_params=_params(("arbitrary",) * len(grid)),
    )(*args, *hook.ins)
    hook.results = res[n_out:]
    return res[:n_out]


def _nn(a, b):
    return jnp.dot(a, b, preferred_element_type=F32)


def _nt(a, b):
    return lax.dot_general(a, b, (((1,), (1,)), ((), ())), preferred_element_type=F32)


def _tn(a, b):
    return lax.dot_general(a, b, (((0,), (0,)), ((), ())), preferred_element_type=F32)


def _sigmoid(v):
    return 1.0 / (1.0 + jnp.exp(-v))


def _rms(x):
    r = lax.rsqrt(jnp.mean(x * x, axis=-1, keepdims=True) + EPS)
    return x * r, r


def _rms_bwd(dh, n, r, g):
    dn = dh * g
    dx = r * (dn - n * jnp.mean(dn * n, axis=-1, keepdims=True))
    return dx, jnp.sum(dh * n, axis=0, keepdims=True)


def _ffn_fwd(x, gn, wgu, wd, name, hook=None, head=None):
    t, d = x.shape
    f = wd.shape[0]
    tm, fc = 256, 1408

    def body(x_ref, gn_ref, wgu_ref, wd_ref, *rest):
        xv = x_ref[...]
        n, _ = _rms(xv)
        h = (n * gn_ref[...]).astype(BF)
        g_ref, u_ref = rest[-2:] if head is None else rest[3:5]
        acc = jnp.zeros((tm, d), F32)
        for c0 in range(0, f, fc):
            g = _nn(h, wgu_ref[:, c0:c0 + fc])
            u = _nn(h, wgu_ref[:, f + c0:f + c0 + fc])
            g_ref[:, c0:c0 + fc] = g.astype(BF)
            u_ref[:, c0:c0 + fc] = u.astype(BF)
            a = (g * _sigmoid(g)) * u
            acc = acc + _nn(a.astype(BF), wd_ref[c0:c0 + fc, :])
        y = xv + 0.5 * acc
        if head is None:
            rest[0][...] = y
            return
        gf_ref, t_ref, dy_ref, _, _, dgf_ref, loss_ref = rest

        @pl.when(pl.program_id(0) == 0)
        def _():
            dgf_ref[...] = jnp.zeros_like(dgf_ref)
            loss_ref[...] = jnp.zeros_like(loss_ref)
        ny, ry = _rms(y)
        gf = gf_ref[...]
        err = ny * gf - t_ref[...]
        loss_ref[...] += 0.5 * jnp.sum(jnp.sum(err * err, axis=-1, keepdims=True) / d, axis=0, keepdims=True)
        dy, dgf = _rms_bwd(err / d, ny, ry, gf)
        dy_ref[...] = dy
        dgf_ref[...] += dgf

    row = pl.BlockSpec((tm, d), lambda i: (i, 0))
    frow = pl.BlockSpec((tm, f), lambda i: (i, 0))
    in_specs = [row, _resident((1, d)), _resident(wgu.shape), _resident(wd.shape)]
    out_shape = (S((t, d), F32), S((t, f), BF), S((t, f), BF))
    out_specs = (row, frow, frow)
    args = (x, gn, wgu, wd)
    if head is not None:
        in_specs += [_resident((1, d)), row]
        out_shape += (S((1, d), F32), S((1, 128), F32))
        out_specs += (_acc_spec((1, d)), _acc_spec((1, 128)))
        args += tuple(head)
    return _call(body, name=name, grid=(t // tm,), out_shape=out_shape, in_specs=in_specs, out_specs=out_specs,
                 sem=("parallel",) if head is None else ("arbitrary",), args=args, hook=hook)


def _mix_proj(x, gn, w_in, hook=None):
    t, d = x.shape
    tm = 256
    nqkv = 1536
    ne = w_in.shape[1] - nqkv

    def body(x_ref, gn_ref, w_ref, qkv_ref, e_ref):
        n, _ = _rms(x_ref[...])
        h = (n * gn_ref[...]).astype(BF)
        qkv_ref[...] = _nn(h, w_ref[:, 0:nqkv]).astype(BF)
        for c0 in range(0, ne, 1024):
            e_ref[:, c0:c0 + 1024] = _nn(h, w_ref[:, nqkv + c0:nqkv + c0 + 1024]).astype(BF)

    return _call(
        body, name="mix_proj", grid=(t // tm,),
        out_shape=(S((t, nqkv), BF), S((t, ne), BF)),
        in_specs=[pl.BlockSpec((tm, d), lambda i: (i, 0)), _resident((1, d)), _resident(w_in.shape)],
        out_specs=(pl.BlockSpec((tm, nqkv), lambda i: (i, 0)), pl.BlockSpec((tm, ne), lambda i: (i, 0))),
        sem=("parallel",), args=(x, gn, w_in), hook=hook)


def _t5_bucket(rel):
    n = jnp.maximum(rel, 0)
    max_exact = REL_BUCKETS // 2
    nf = jnp.maximum(n, 1).astype(F32)
    large = max_exact + (jnp.log(nf / max_exact) / math.log(REL_MAX_DIST / max_exact)
                         * (REL_BUCKETS - max_exact)).astype(I32)
    large = jnp.minimum(large, REL_BUCKETS - 1)
    return jnp.where(n < max_exact, n, large)


def _block_rel():
    i = lax.broadcasted_iota(I32, (BLOCK, 2 * BLOCK), 0)
    j = lax.broadcasted_iota(I32, (BLOCK, 2 * BLOCK), 1)
    return i + BLOCK - j


def _bias_build(rel_bias):
    def body(rb_ref, o_ref):
        bucket = _t5_bucket(_block_rel())
        for h in range(SWA_HEADS):
            acc = jnp.zeros((BLOCK, 2 * BLOCK), F32)
            for b in range(REL_BUCKETS):
                acc = jnp.where(bucket == b, rb_ref[b, h], acc)
            o_ref[h] = acc

    return pl.pallas_call(
        body, name="bias_build", out_shape=S((SWA_HEADS, BLOCK, 2 * BLOCK), F32),
        in_specs=[SMEM_SPEC], out_specs=VMEM_SPEC,
    )(rel_bias)


GROUP_ROWS = SWA_GROUP * BLOCK


def _swa_visible(b, pq_ref, pkp_ref, pkc_ref):
    pk = jnp.concatenate([pkp_ref[...], pkc_ref[...]], axis=1)
    col = lax.broadcasted_iota(I32, (1, 2 * BLOCK), 1)
    pk = jnp.where(jnp.logical_and(b == 0, col < BLOCK), POS_PAD, pk)
    rel = jnp.concatenate([pq_ref[...]] * SWA_GROUP, axis=0) - pk
    return jnp.logical_and(rel >= 0, rel < WINDOW)


def _group_heads(ref, hk):
    h0 = hk * SWA_GROUP
    return jnp.concatenate([ref[:, (h0 + g) * HEAD_DIM:(h0 + g + 1) * HEAD_DIM] for g in range(SWA_GROUP)], axis=0)


def _group_sinks(sink_ref, hk):
    row = lax.broadcasted_iota(I32, (GROUP_ROWS, 1), 0)
    col = jnp.zeros((GROUP_ROWS, 1), F32) + sink_ref[0, hk * SWA_GROUP]
    for g in range(1, SWA_GROUP):
        col = jnp.where(row >= g * BLOCK, sink_ref[0, hk * SWA_GROUP + g], col)
    return col


def _swa_probs(qg, kh, vis, bias, sink):
    s = _nt(qg, kh) * (HEAD_DIM ** -0.5)
    s = jnp.where(vis, s + bias, NEG)
    m = jnp.maximum(jnp.max(s, axis=-1, keepdims=True), sink)
    p = jnp.exp(s - m)
    ps = jnp.exp(sink - m)
    inv = 1.0 / (jnp.sum(p, axis=-1, keepdims=True) + ps)
    return p * inv, ps * inv


def _swa_fwd(qkv, pos_col, pos_row, bias_t, sinks, hook=None):
    t = qkv.shape[0]
    nb = t // BLOCK
    qw = SWA_HEADS * HEAD_DIM
    kw = SWA_KV_HEADS * HEAD_DIM

    def body(q_ref, kp_ref, kc_ref, vp_ref, vc_ref, pq_ref, pkp_ref, pkc_ref, bias_ref, sink_ref, o_ref):
        b = pl.program_id(0)
        vis = _swa_visible(b, pq_ref, pkp_ref, pkc_ref)
        k2 = jnp.concatenate([kp_ref[...], kc_ref[...]], axis=0)
        v2 = jnp.concatenate([vp_ref[...], vc_ref[...]], axis=0)
        for hk in range(SWA_KV_HEADS):
            kh = k2[:, hk * HEAD_DIM:(hk + 1) * HEAD_DIM]
            vh = v2[:, hk * HEAD_DIM:(hk + 1) * HEAD_DIM]
            pn, _ = _swa_probs(_group_heads(q_ref, hk), kh, vis, bias_ref[hk], _group_sinks(sink_ref, hk))
            o = _nn(pn.astype(BF), vh)
            for g in range(SWA_GROUP):
                h = hk * SWA_GROUP + g
                o_ref[:, h * HEAD_DIM:(h + 1) * HEAD_DIM] = o[g * BLOCK:(g + 1) * BLOCK]

    prev = lambda b: jnp.maximum(b - 1, 0)
    return _call(
        body, name="swa_fwd", grid=(nb,), out_shape=(S((t, qw), F32),),
        in_specs=[
            pl.BlockSpec((BLOCK, qw), lambda b: (b, 0)),
            pl.BlockSpec((BLOCK, kw), lambda b: (prev(b), qw // kw)),
            pl.BlockSpec((BLOCK, kw), lambda b: (b, qw // kw)),
            pl.BlockSpec((BLOCK, kw), lambda b: (prev(b), qw // kw + 1)),
            pl.BlockSpec((BLOCK, kw), lambda b: (b, qw // kw + 1)),
            pl.BlockSpec((BLOCK, 1), lambda b: (b, 0)),
            pl.BlockSpec((1, BLOCK), lambda b: (0, prev(b))),
            pl.BlockSpec((1, BLOCK), lambda b: (0, b)),
            _resident(bias_t.shape),
            SMEM_SPEC,
        ],
        out_specs=(pl.BlockSpec((BLOCK, qw), lambda b: (b, 0)),),
        sem=("parallel",), args=(qkv, qkv, qkv, qkv, qkv, pos_col, pos_row, pos_row, bias_t, sinks), hook=hook)[0]


HALO = 16


def _conv_taps(z, zh, first):
    tm = z.shape[0]
    zh = jnp.where(first, 0.0, zh)
    row = lax.broadcasted_iota(I32, (tm, 1), 0)
    z1 = jnp.where(row == 0, zh[HALO - 1:HALO, :], pltpu.roll(z, 1, 0))
    z2 = jnp.where(row == 0, zh[HALO - 2:HALO - 1, :], jnp.where(row == 1, zh[HALO - 1:HALO, :], pltpu.roll(z, 2, 0)))
    return z1, z2


def _mix_out_fwd(e, attn, conv_w, w_out, x, hook=None):
    t, d = x.shape
    tm = 256
    hb = tm // HALO
    f32 = lambda ref: ref[...].astype(F32)

    def body(c_ref, b_ref, u_ref, ga_ref, gc_ref, ch_ref, uh_ref, attn_ref, cw_ref, w_ref, x_ref, xo_ref, mg_ref):
        i = pl.program_id(0)
        z = f32(c_ref) * f32(u_ref)
        z1, z2 = _conv_taps(z, f32(ch_ref) * f32(uh_ref), i == 0)
        s = cw_ref[0:1, :] * z2 + cw_ref[1:2, :] * z1 + cw_ref[2:3, :] * z
        conv = f32(b_ref) * s
        merged = (_sigmoid(f32(ga_ref)) * attn_ref[...] + _sigmoid(f32(gc_ref)) * conv).astype(BF)
        mg_ref[...] = merged
        xo_ref[...] = x_ref[...] + _nn(merged, w_ref[...])

    ecol = lambda cb: pl.BlockSpec((tm, d), lambda i: (i, cb))
    halo = lambda cb: pl.BlockSpec((HALO, d), lambda i: (jnp.maximum(i * hb - 1, 0), cb))
    row = pl.BlockSpec((tm, d), lambda i: (i, 0))
    return _call(
        body, name="mix_out_fwd", grid=(t // tm,),
        out_shape=(S((t, d), F32), S((t, d), BF)),
        in_specs=[ecol(0), ecol(1), ecol(2), ecol(3), ecol(4), halo(0), halo(2), row,
                  _resident(conv_w.shape), _resident(w_out.shape), row],
        out_specs=(row, row),
        sem=("parallel",), args=(e, e, e, e, e, e, e, attn, conv_w, w_out, x), hook=hook)


def _mem_kv(mem, gm, wkv):
    m, d = mem.shape

    def body(mem_ref, gm_ref, w_ref, mh_ref, kv_ref):
        n, _ = _rms(mem_ref[...])
        mh = (n * gm_ref[...]).astype(BF)
        mh_ref[...] = mh
        kv_ref[...] = _nn(mh, w_ref[...]).astype(BF)

    return pl.pallas_call(
        body, name="mem_kv", out_shape=(S((m, d), BF), S((m, wkv.shape[1]), BF)),
        compiler_params=_params(),
    )(mem, gm, wkv)


def _xattn_probs(qh, kh):
    s = _nt(qh, kh) * (kh.shape[1] ** -0.5)
    p = jnp.exp(s - jnp.max(s, axis=-1, keepdims=True))
    return p * (1.0 / jnp.sum(p, axis=-1, keepdims=True))


def _xattn_fwd(x, gn, wq, kv, wo):
    t, d = x.shape
    tm = 256
    hd = d // MEM_HEADS

    def body(x_ref, gn_ref, wq_ref, kv_ref, wo_ref, xo_ref, q_ref, o_ref):
        xv = x_ref[...]
        n, _ = _rms(xv)
        q = _nn((n * gn_ref[...]).astype(BF), wq_ref[...]).astype(BF)
        q_ref[...] = q
        outs = []
        for hh in range(MEM_HEADS):
            p = _xattn_probs(q[:, hh * hd:(hh + 1) * hd], kv_ref[:, hh * hd:(hh + 1) * hd])
            outs.append(_nn(p.astype(BF), kv_ref[:, d + hh * hd:d + (hh + 1) * hd]))
        o = jnp.concatenate(outs, axis=1).astype(BF)
        o_ref[...] = o
        xo_ref[...] = xv + _nn(o, wo_ref[...])

    row = pl.BlockSpec((tm, d), lambda i: (i, 0))
    return pl.pallas_call(
        body, name="xattn_fwd", grid=(t // tm,),
        out_shape=(S((t, d), F32), S((t, d), BF), S((t, d), BF)),
        in_specs=[row, _resident((1, d)), _resident(wq.shape), _resident(kv.shape), _resident(wo.shape)],
        out_specs=(row, row, row),
        compiler_params=_params(("parallel",)),
    )(x, gn, wq, kv, wo)


def _ffn_bwd(dxo, x, gn, g, u, wgu, wd, name, hook=None):
    t, d = x.shape
    f = wd.shape[0]
    tm, fc = 256, 1408

    def body(dxo_ref, x_ref, gn_ref, g_ref, u_ref, wgu_ref, wd_ref, dx_ref, dgn_ref, dgu_ref, a_ref, h_ref, dyh_ref):
        @pl.when(pl.program_id(0) == 0)
        def _():
            dgn_ref[...] = jnp.zeros_like(dgn_ref)
        dxov = dxo_ref[...]
        dyh = (0.5 * dxov).astype(BF)
        dyh_ref[...] = dyh
        n, r = _rms(x_ref[...])
        gnv = gn_ref[...]
        h_ref[...] = (n * gnv).astype(BF)
        dh = jnp.zeros((tm, d), F32)
        for c0 in range(0, f, fc):
            gv = g_ref[:, c0:c0 + fc].astype(F32)
            uv = u_ref[:, c0:c0 + fc].astype(F32)
            da = _nt(dyh, wd_ref[c0:c0 + fc, :])
            sg = _sigmoid(gv)
            silu = gv * sg
            a_ref[:, c0:c0 + fc] = (silu * uv).astype(BF)
            dg = (da * uv * (sg * (1.0 + gv * (1.0 - sg)))).astype(BF)
            du = (da * silu).astype(BF)
            dgu_ref[:, c0:c0 + fc] = dg
            dgu_ref[:, f + c0:f + c0 + fc] = du
            dh = dh + _nt(dg, wgu_ref[:, c0:c0 + fc]) + _nt(du, wgu_ref[:, f + c0:f + c0 + fc])
        dx, dgn = _rms_bwd(dh, n, r, gnv)
        dx_ref[...] = dxov + dx
        dgn_ref[...] += dgn

    row = pl.BlockSpec((tm, d), lambda i: (i, 0))
    frow = pl.BlockSpec((tm, f), lambda i: (i, 0))
    return _call(
        body, name=name, grid=(t // tm,),
        out_shape=(S((t, d), F32), S((1, d), F32), S((t, 2 * f), BF), S((t, f), BF), S((t, d), BF), S((t, d), BF)),
        in_specs=[row, row, _resident((1, d)), frow, frow, _resident(wgu.shape), _resident(wd.shape)],
        out_specs=(row, _acc_spec((1, d)), pl.BlockSpec((tm, 2 * f), lambda i: (i, 0)), frow, row, row),
        sem=("arbitrary",), args=(dxo, x, gn, g, u, wgu, wd), hook=hook)


def _dw(a, b, tn, name):
    t, ka = a.shape
    nb = b.shape[1]
    tt = 1024
    nt = t // tt

    def body(a_ref, b_ref, o_ref, acc_ref):
        k = pl.program_id(1)

        @pl.when(k == 0)
        def _():
            acc_ref[...] = jnp.zeros_like(acc_ref)
        acc_ref[...] += _tn(a_ref[...], b_ref[...])

        @pl.when(k == nt - 1)
        def _():
            o_ref[...] = acc_ref[...].astype(BF)

    return pl.pallas_call(
        body, name=name, grid=(nb // tn, nt), out_shape=S((ka, nb), BF),
        in_specs=[pl.BlockSpec((tt, ka), lambda j, k: (k, 0)), pl.BlockSpec((tt, tn), lambda j, k: (k, j))],
        out_specs=pl.BlockSpec((ka, tn), lambda j, k: (0, j)), scratch_shapes=[pltpu.VMEM((ka, tn), F32)],
        compiler_params=_params(("parallel", "arbitrary")),
    )(a, b)


def _dw_pair(pairs, tn, name, kind, hook=None):
    npairs = len(pairs)
    t, ka = pairs[0][0].shape
    nb = pairs[0][1].shape[1]
    tt = 2048 if npairs == 1 else 1024
    nt, nj = t // tt, nb // tn
    col = kind == "col"
    rh = ka // 2 if col else ka // 8
    tile = (rh, tn) if col else (4, rh, tn)
    half = (rh, nb) if col else (4, rh, nb)
    lead = (slice(None),) * (len(tile) - 1)

    def body(*refs):
        ins, mines, sibs = refs[:2 * npairs], refs[2 * npairs:3 * npairs], refs[3 * npairs:4 * npairs]
        acc_ref, stage, ssem, rsem = refs[4 * npairs:]
        j, k = pl.program_id(0), pl.program_id(1)
        x, y, c = _place()
        sibling = (x, y, 1 - c)

        def send(p, slot, jj):
            dst = sibs[p].at[lead + (pl.ds(pl.multiple_of(jj * tn, 128), tn),)]
            return _remote(stage.at[slot], dst, ssem.at[slot], rsem.at[p], sibling)

        def rows(s, whose):
            return acc_ref[pl.ds(pl.multiple_of(s * 2 * rh + whose * rh, 16), rh), :].astype(BF)

        def step(p):
            a_ref, b_ref, mine_ref = ins[2 * p], ins[2 * p + 1], mines[p]

            @pl.when(k == 0)
            def _():
                acc_ref[...] = jnp.zeros_like(acc_ref)
            acc_ref[...] += _tn(a_ref[...], b_ref[...])

            @pl.when(k == nt - 1)
            def _():
                slot = j % 2

                @pl.when(j >= 2)
                def _():
                    send(p, slot, 0).wait_send()
                if col:
                    mine_ref[...] = rows(0, c)
                    stage[slot] = rows(0, 1 - c)
                else:
                    for s in range(4):
                        mine_ref[s] = rows(s, c)
                        stage[slot, s] = rows(s, 1 - c)
                send(p, slot, j - p * nj).start()

        for p in range(npairs):
            pl.when(j // nj == p)(functools.partial(step, p))

        @pl.when(jnp.logical_and(j == npairs * nj - 1, k == nt - 1))
        def _():
            for jj in range(max(npairs * nj - 2, 0), npairs * nj):
                send(0, jj % 2, 0).wait_send()
            for p in range(npairs):
                _remote(sibs[p], sibs[p], ssem.at[0], rsem.at[p], sibling).wait_recv()

    in_specs, args = [], []
    for p, (a, b) in enumerate(pairs):
        on = lambda j, p=p: j // nj == p
        in_specs += [pl.BlockSpec((tt, ka), lambda j, k, on=on: (jnp.where(on(j), k, 0), 0)),
                     pl.BlockSpec((tt, tn), lambda j, k, on=on, p=p: (jnp.where(on(j), k, 0), jnp.clip(j - p * nj, 0, nj - 1)))]
        args += [a, b]
    mine_spec = lambda p: pl.BlockSpec(tile, (lambda j, k: (0, jnp.clip(j - p * nj, 0, nj - 1))) if col
                                       else (lambda j, k: (0, 0, jnp.clip(j - p * nj, 0, nj - 1))))
    res = _call(
        body, name=name, grid=(npairs * nj, nt), out_shape=(S(half, BF),) * (2 * npairs),
        in_specs=in_specs, out_specs=tuple(mine_spec(p) for p in range(npairs)) + (ANY,) * npairs,
        scratch_shapes=[pltpu.VMEM((ka, tn), F32), pltpu.VMEM((2,) + tile, BF), pltpu.SemaphoreType.DMA((2,)),
                        pltpu.SemaphoreType.DMA((npairs,))],
        sem=("arbitrary", "arbitrary"), args=args, hook=hook)
    return [(res[p], res[npairs + p]) for p in range(npairs)]


def _xattn_bwd(dxo, x, gn, q, kv, wq, wo, hook=None):
    t, d = x.shape
    tm = 256
    hd = d // MEM_HEADS
    nkv = kv.shape[0]

    def body(dxo_ref, x_ref, gn_ref, q_ref, kv_ref, wq_ref, wo_ref, dx_ref, dgn_ref, dkv_ref, dxh_ref, h_ref, dq_ref):
        @pl.when(pl.program_id(0) == 0)
        def _():
            dgn_ref[...] = jnp.zeros_like(dgn_ref)
            dkv_ref[...] = jnp.zeros_like(dkv_ref)
        dxov = dxo_ref[...]
        dxh = dxov.astype(BF)
        dxh_ref[...] = dxh
        do = _nt(dxh, wo_ref[...]).astype(BF)
        dqs = []
        for hh in range(MEM_HEADS):
            lo, hi = hh * hd, (hh + 1) * hd
            qh = q_ref[:, lo:hi]
            kh = kv_ref[:, lo:hi]
            vh = kv_ref[:, d + lo:d + hi]
            doh = do[:, lo:hi]
            p = _xattn_probs(qh, kh)
            dp = _nt(doh, vh)
            ds = (p * (dp - jnp.sum(p * dp, axis=-1, keepdims=True)) * (hd ** -0.5)).astype(BF)
            dqs.append(_nn(ds, kh))
            dkv_ref[:, lo:hi] += _tn(ds, qh)
            dkv_ref[:, d + lo:d + hi] += _tn(p.astype(BF), doh)
        dq = jnp.concatenate(dqs, axis=1).astype(BF)
        dq_ref[...] = dq
        n, r = _rms(x_ref[...])
        gnv = gn_ref[...]
        h_ref[...] = (n * gnv).astype(BF)
        dx, dgn = _rms_bwd(_nt(dq, wq_ref[...]), n, r, gnv)
        dx_ref[...] = dxov + dx
        dgn_ref[...] += dgn

    row = pl.BlockSpec((tm, d), lambda i: (i, 0))
    return _call(
        body, name="xattn_bwd", grid=(t // tm,),
        out_shape=(S((t, d), F32), S((1, d), F32), S((nkv, 2 * d), F32), S((t, d), BF), S((t, d), BF), S((t, d), BF)),
        in_specs=[row, row, _resident((1, d)), row, _resident(kv.shape), _resident(wq.shape), _resident(wo.shape)],
        out_specs=(row, _acc_spec((1, d)), _acc_spec((nkv, 2 * d)), row, row, row),
        sem=("arbitrary",), args=(dxo, x, gn, q, kv, wq, wo), hook=hook)


def _mem_bwd(dkv, mh, mem, gm, wkv, pair=False):
    m, d = mem.shape
    rows, cols = wkv.shape
    rh = rows // 2

    def body(dkv_ref, mh_ref, mem_ref, gm_ref, w_ref, *outs):
        dkvb = dkv_ref[...].astype(BF)
        dw = _tn(mh_ref[...], dkvb).astype(BF)
        dmh = _nt(dkvb, w_ref[...])
        n, _ = _rms(mem_ref[...])
        if not pair:
            dw_ref, dgm_ref = outs
            dw_ref[...] = dw
        else:
            mine_ref, sib_ref, dgm_ref, whole, ssem, rsem = outs
            x, y, c = _place()
            whole[...] = dw
            cp = _remote(whole.at[pl.ds(pl.multiple_of((1 - c) * rh, 16), rh), :], sib_ref, ssem, rsem, (x, y, 1 - c))
            cp.start()
            mine_ref[...] = whole[pl.ds(pl.multiple_of(c * rh, 16), rh), :]
            cp.wait()
        dgm_ref[...] = jnp.sum(dmh * n, axis=0, keepdims=True)

    if not pair:
        return pl.pallas_call(body, name="mem_bwd", out_shape=(S(wkv.shape, BF), S((1, d), F32)),
                              compiler_params=_params())(dkv, mh, mem, gm, wkv)
    mine, sib, dgm = pl.pallas_call(
        body, name="mem_bwd", out_shape=(S((rh, cols), BF), S((rh, cols), BF), S((1, d), F32)),
        in_specs=[VMEM_SPEC] * 5, out_specs=(VMEM_SPEC, ANY, VMEM_SPEC),
        scratch_shapes=[pltpu.VMEM((rows, cols), BF), pltpu.SemaphoreType.DMA, pltpu.SemaphoreType.DMA],
        compiler_params=_params())(dkv, mh, mem, gm, wkv)
    return (mine, sib), dgm


def _mix_out_bwd(dxo, e, attn, conv_w, w_out, nqkv, hook=None):
    t, d = attn.shape
    tm = 256
    nt = t // tm
    f32 = lambda ref: ref[...].astype(F32)

    def body(dxo_ref, dxn_ref, c_ref, b_ref, u_ref, ga_ref, gc_ref, ch_ref, uh_ref, bn_ref, gcn_ref,
             attn_ref, cw_ref, w_ref, dattn_ref, de_ref, dcw_ref, dxh_ref):
        i = pl.program_id(0)

        @pl.when(i == 0)
        def _():
            dcw_ref[...] = jnp.zeros_like(dcw_ref)
        dxh = dxo_ref[...].astype(BF)
        dxh_ref[...] = dxh
        w = w_ref[...]
        dm = _nt(dxh, w)
        dmn = _nt(dxn_ref[...].astype(BF), w)
        cv, bv, uv = f32(c_ref), f32(b_ref), f32(u_ref)
        sga = _sigmoid(f32(ga_ref))
        sgc = _sigmoid(f32(gc_ref))
        z = cv * uv
        z1, z2 = _conv_taps(z, f32(ch_ref) * f32(uh_ref), i == 0)
        w0, w1, w2 = cw_ref[0:1, :], cw_ref[1:2, :], cw_ref[2:3, :]
        s = w0 * z2 + w1 * z1 + w2 * z
        av = attn_ref[...]
        dattn_ref[...] = (dm * sga).astype(BF)
        dconv = dm * sgc
        ds = dconv * bv
        dsn = jnp.where(i == nt - 1, 0.0, dmn * _sigmoid(gcn_ref[0:8, :].astype(F32)) * bn_ref[0:8, :].astype(F32))
        row = lax.broadcasted_iota(I32, (tm, 1), 0)
        dsp1 = jnp.where(row == tm - 1, dsn[0:1, :], pltpu.roll(ds, tm - 1, 0))
        dsp2 = jnp.where(row == tm - 2, dsn[0:1, :], jnp.where(row == tm - 1, dsn[1:2, :], pltpu.roll(ds, tm - 2, 0)))
        dz = w2 * ds + w1 * dsp1 + w0 * dsp2
        de_ref[:, nqkv:nqkv + d] = (dz * uv).astype(BF)
        de_ref[:, nqkv + d:nqkv + 2 * d] = (dconv * s).astype(BF)
        de_ref[:, nqkv + 2 * d:nqkv + 3 * d] = (dz * cv).astype(BF)
        de_ref[:, nqkv + 3 * d:nqkv + 4 * d] = (dm * av * sga * (1.0 - sga)).astype(BF)
        de_ref[:, nqkv + 4 * d:nqkv + 5 * d] = (dm * (bv * s) * sgc * (1.0 - sgc)).astype(BF)
        dcw_ref[0:1, :] += jnp.sum(ds * z2, axis=0, keepdims=True)
        dcw_ref[1:2, :] += jnp.sum(ds * z1, axis=0, keepdims=True)
        dcw_ref[2:3, :] += jnp.sum(ds * z, axis=0, keepdims=True)

    ecol = lambda cb: pl.BlockSpec((tm, d), lambda i: (i, cb))
    prev = lambda cb: pl.BlockSpec((HALO, d), lambda i: (jnp.maximum(i * (tm // HALO) - 1, 0), cb))
    nxt = lambda rows, cb: pl.BlockSpec((rows, d), lambda i: (jnp.minimum((i + 1) * (tm // rows), t // rows - 1), cb))
    row = pl.BlockSpec((tm, d), lambda i: (i, 0))
    return _call(
        body, name="mix_out_bwd", grid=(nt,),
        out_shape=(S((t, d), BF), S((t, nqkv + 5 * d), BF), S((8, d), F32), S((t, d), BF)),
        in_specs=[row, nxt(8, 0), ecol(0), ecol(1), ecol(2), ecol(3), ecol(4), prev(0), prev(2), nxt(HALO, 1), nxt(HALO, 4),
                  row, _resident(conv_w.shape), _resident(w_out.shape)],
        out_specs=(row, pl.BlockSpec((tm, nqkv + 5 * d), lambda i: (i, 0)), _acc_spec((8, d)), row),
        sem=("arbitrary",), args=(dxo, dxo, e, e, e, e, e, e, e, e, e, attn, conv_w, w_out), hook=hook)


def _swa_bwd(qkv, dattn, pos_col, pos_row, bias_t, sinks, dp, hook=None):
    t = qkv.shape[0]
    nb = t // BLOCK
    qw = SWA_HEADS * HEAD_DIM
    kw = SWA_KV_HEADS * HEAD_DIM

    def body(q_ref, kp_ref, kc_ref, vp_ref, vc_ref, do_ref, pq_ref, pkp_ref, pkc_ref, bias_ref, sink_ref,
             dp_in, dp_ref, gb_ref, dsk_ref, carry_ref, dq_ref):
        b = pl.program_id(0)

        @pl.when(b == 0)
        def _():
            gb_ref[...] = jnp.zeros_like(gb_ref)
            dsk_ref[...] = jnp.zeros_like(dsk_ref)
            carry_ref[...] = jnp.zeros_like(carry_ref)
            dq_ref[...] = jnp.zeros_like(dq_ref)
        dp_ref[:, 0:qw] = dq_ref[...]

        @pl.when(b < nb)
        def _():
            vis = _swa_visible(b, pq_ref, pkp_ref, pkc_ref)
            k2 = jnp.concatenate([kp_ref[...], kc_ref[...]], axis=0)
            v2 = jnp.concatenate([vp_ref[...], vc_ref[...]], axis=0)
            for hk in range(SWA_KV_HEADS):
                lo, hi = hk * HEAD_DIM, (hk + 1) * HEAD_DIM
                kh = k2[:, lo:hi]
                vh = v2[:, lo:hi]
                qg = _group_heads(q_ref, hk)
                dog = _group_heads(do_ref, hk)
                pn, psn = _swa_probs(qg, kh, vis, bias_ref[hk], _group_sinks(sink_ref, hk))
                dp = _nt(dog, vh)
                delta = jnp.sum(pn * dp, axis=-1, keepdims=True)
                ds = pn * (dp - delta)
                gb_ref[hk] += ds
                dsk_ref[hk] += -psn * delta
                dsb = (ds * (HEAD_DIM ** -0.5)).astype(BF)
                dqg = _nn(dsb, kh).astype(BF)
                for g in range(SWA_GROUP):
                    h = hk * SWA_GROUP + g
                    dq_ref[:, h * HEAD_DIM:(h + 1) * HEAD_DIM] = dqg[g * BLOCK:(g + 1) * BLOCK]
                dk = _tn(dsb, qg)
                dv = _tn(pn.astype(BF), dog)
                dp_ref[:, qw + lo:qw + hi] = (carry_ref[:, lo:hi] + dk[0:BLOCK]).astype(BF)
                dp_ref[:, qw + kw + lo:qw + kw + hi] = (carry_ref[:, kw + lo:kw + hi] + dv[0:BLOCK]).astype(BF)
                carry_ref[:, lo:hi] = dk[BLOCK:2 * BLOCK]
                carry_ref[:, kw + lo:kw + hi] = dv[BLOCK:2 * BLOCK]

        @pl.when(b == nb)
        def _():
            dp_ref[:, qw:qw + 2 * kw] = carry_ref[...].astype(BF)

    cur = lambda b: jnp.minimum(b, nb - 1)
    prev = lambda b: jnp.maximum(cur(b) - 1, 0)
    return _call(
        body, name="swa_bwd", grid=(nb + 1,),
        out_shape=(S(dp.shape, BF), S((SWA_KV_HEADS, GROUP_ROWS, 2 * BLOCK), F32), S((SWA_KV_HEADS, GROUP_ROWS, 1), F32)),
        in_specs=[
            pl.BlockSpec((BLOCK, qw), lambda b: (cur(b), 0)),
            pl.BlockSpec((BLOCK, kw), lambda b: (prev(b), qw // kw)),
            pl.BlockSpec((BLOCK, kw), lambda b: (cur(b), qw // kw)),
            pl.BlockSpec((BLOCK, kw), lambda b: (prev(b), qw // kw + 1)),
            pl.BlockSpec((BLOCK, kw), lambda b: (cur(b), qw // kw + 1)),
            pl.BlockSpec((BLOCK, qw), lambda b: (cur(b), 0)),
            pl.BlockSpec((BLOCK, 1), lambda b: (cur(b), 0)),
            pl.BlockSpec((1, BLOCK), lambda b: (0, prev(b))),
            pl.BlockSpec((1, BLOCK), lambda b: (0, cur(b))),
            _resident(bias_t.shape),
            SMEM_SPEC,
            ANY,
        ],
        out_specs=(
            pl.BlockSpec((BLOCK, qw + 2 * kw), lambda b: (jnp.maximum(b - 1, 0), 0)),
            _acc_spec((SWA_KV_HEADS, GROUP_ROWS, 2 * BLOCK)),
            _acc_spec((SWA_KV_HEADS, GROUP_ROWS, 1)),
        ),
        scratch_shapes=[pltpu.VMEM((BLOCK, 2 * kw), F32), pltpu.VMEM((BLOCK, qw), BF)],
        aliases={11: 0}, sem=("arbitrary",),
        args=(qkv, qkv, qkv, qkv, qkv, dattn, pos_col, pos_row, pos_row, bias_t, sinks, dp), hook=hook)


def _bias_reduce(gb, dsk):
    def body(gb_ref, dsk_ref, drb_ref, dsink_ref):
        bucket = _t5_bucket(_block_rel())
        for b in range(REL_BUCKETS):
            mask = bucket == b
            for h in range(SWA_HEADS):
                drb_ref[b, h] = jnp.sum(jnp.where(mask, gb_ref[h], 0.0))
        for h in range(SWA_HEADS):
            dsink_ref[0, h] = jnp.sum(dsk_ref[h])

    return pl.pallas_call(
        body, name="bias_reduce", out_shape=(S((REL_BUCKETS, SWA_HEADS), F32), S((1, SWA_HEADS), F32)),
        in_specs=[VMEM_SPEC, VMEM_SPEC], out_specs=(SMEM_SPEC, SMEM_SPEC),
    )(gb, dsk)


def _mix_in_bwd(dp, x, gn, w_in, dxo):
    t, d = x.shape
    tm = 256
    npr = dp.shape[1]

    def body(dp_ref, x_ref, gn_ref, w_ref, dxo_ref, dx_ref, dgn_ref, h_ref):
        @pl.when(pl.program_id(0) == 0)
        def _():
            dgn_ref[...] = jnp.zeros_like(dgn_ref)
        dh = jnp.zeros((tm, d), F32)
        for c0 in range(0, npr, 1024):
            c1 = min(c0 + 1024, npr)
            dh = dh + _nt(dp_ref[:, c0:c1], w_ref[:, c0:c1])
        n, r = _rms(x_ref[...])
        gnv = gn_ref[...]
        h_ref[...] = (n * gnv).astype(BF)
        dx, dgn = _rms_bwd(dh, n, r, gnv)
        dx_ref[...] = dxo_ref[...] + dx
        dgn_ref[...] += dgn

    row = pl.BlockSpec((tm, d), lambda i: (i, 0))
    return pl.pallas_call(
        body, name="mix_in_bwd", grid=(t // tm,),
        out_shape=(S((t, d), F32), S((1, d), F32), S((t, d), BF)),
        in_specs=[pl.BlockSpec((tm, npr), lambda i: (i, 0)), row, _resident((1, d)), _resident(w_in.shape), row],
        out_specs=(row, _acc_spec((1, d)), row),
        compiler_params=_params(("arbitrary",)),
    )(dp, x, gn, w_in, dxo)


CAST_STEPS = 4


def _to_bf16(arrays, name, hook=None):
    def body(*refs):
        for src, dst in zip(refs[:len(arrays)], refs[len(arrays):]):
            dst[...] = src[...].astype(BF)

    blocks = [pl.BlockSpec((a.shape[0] // CAST_STEPS, a.shape[1]), lambda i: (i, 0)) for a in arrays]
    return _call(body, name=name, grid=(CAST_STEPS,), out_shape=tuple(S(a.shape, BF) for a in arrays), in_specs=blocks,
                 out_specs=tuple(blocks), sem=("parallel",), args=list(arrays), hook=hook)


PAIR_STEPS = 4


def _add_bf16(pairs, name):
    def body(*refs):
        ins, outs = refs[:2 * len(pairs)], refs[2 * len(pairs):]
        for q, o_ref in enumerate(outs):
            o_ref[...] = (ins[2 * q][...].astype(F32) + ins[2 * q + 1][...].astype(F32)).astype(BF)

    in_specs, out_specs, out_shape, args = [], [], [], []
    for a, b in pairs:
        rows, cols = a.shape
        blk = pl.BlockSpec((rows // PAIR_STEPS, cols), lambda i: (i, 0))
        in_specs += [blk, blk]
        out_specs.append(blk)
        out_shape.append(S((rows, cols), BF))
        args += [a, b]
    return pl.pallas_call(body, name=name, grid=(PAIR_STEPS,), out_shape=tuple(out_shape), in_specs=in_specs,
                          out_specs=tuple(out_specs), compiler_params=_params(("parallel",)))(*args)


def _adam_update(w, g, m, v):
    mn = ADAM_B1 * m + (1.0 - ADAM_B1) * g
    vn = ADAM_B2 * v + (1.0 - ADAM_B2) * (g * g)
    m_hat = mn / (1.0 - ADAM_B1 ** ADAM_STEP)
    v_hat = vn / (1.0 - ADAM_B2 ** ADAM_STEP)
    return -ADAM_LR * (m_hat / (jnp.sqrt(v_hat) + ADAM_EPS) + ADAM_WD * w), mn, vn


def _adamw(quads, name, steps, echo=False, hook=None):
    n_out = 4 if echo else 3

    def body(*refs):
        ins, outs = refs[:4 * len(quads)], refs[4 * len(quads):]
        for q in range(len(quads)):
            w_ref, g_ref, m_ref, v_ref = ins[4 * q:4 * q + 4]
            res = outs[n_out * q:n_out * q + n_out]
            gv = g_ref[...]
            if echo:
                res[0][...] = gv
            res[-3][...], res[-2][...], res[-1][...] = _adam_update(w_ref[...], gv, m_ref[...], v_ref[...])

    in_specs, out_specs, out_shape, args = [], [], [], []
    for quad in quads:
        rows, cols = quad[0].shape
        blk = pl.BlockSpec((rows // steps, cols), lambda i: (i, 0))
        in_specs += [blk] * 4
        out_specs += [blk] * n_out
        out_shape += [S((rows, cols), F32)] * n_out
        args += list(quad)
    res = _call(body, name=name, grid=(steps,), out_shape=tuple(out_shape), in_specs=in_specs, out_specs=tuple(out_specs),
                sem=("parallel",), args=args, hook=hook)
    return [res[n_out * q:n_out * q + n_out] for q in range(len(quads))]


def _place():
    x, y, c = lax.axis_index("x"), lax.axis_index("y"), lax.axis_index("c")
    return x, y, c


OTHER_CHIPS = ((1, 0), (0, 1), (1, 1))


def _flip(v, f):
    return 1 - v if f else v


def _remote(src, dst, ssem, rsem, dev):
    return pltpu.make_async_remote_copy(src_ref=src, dst_ref=dst, send_sem=ssem, recv_sem=rsem,
                                        device_id=dev, device_id_type=MESH)


class _Both:
    def __init__(self, hooks):
        self.hooks = hooks
        self.ins = [a for h in hooks for a in h.ins]
        self.out_shape = tuple(o for h in hooks for o in h.out_shape)
        self.scratch = [x for h in hooks for x in h.scratch]

    def _each(self, ins, outs, scr):
        i = o = s = 0
        for h in self.hooks:
            ni, no, ns = len(h.ins), len(h.out_shape), len(h.scratch)
            yield h, ins[i:i + ni], outs[o:o + no], scr[s:s + ns]
            i, o, s = i + ni, o + no, s + ns

    def start(self, ins, outs, scr):
        for h, *refs in self._each(ins, outs, scr):
            h.start(*refs)

    def relay(self, ins, outs, scr):
        for h, *refs in self._each(ins, outs, scr):
            h.relay(*refs)

    def finish(self, ins, outs, scr):
        for h, *refs in self._each(ins, outs, scr):
            h.finish(*refs)

    @property
    def lead(self):
        return max(h.lead for h in self.hooks)

    @property
    def results(self):
        return [h.results for h in self.hooks]

    @results.setter
    def results(self, res):
        for h, _, mine, _ in self._each((), tuple(res), ()):
            h.results = mine


class _GatherHook:
    def __init__(self, shards, kinds):
        self.ins, self.kinds, n = list(shards), list(kinds), len(shards)
        self.out_shape = tuple(
            S((w.shape[0], 4 * w.shape[1]), BF) if k == "col" else S((4,) + w.shape, BF) for w, k in zip(shards, kinds))
        dma = pltpu.SemaphoreType.DMA
        self.scratch = ([dma((n, 3)) for _ in range(5)] + [dma((n,)), dma((n,))]
                        + [pltpu.VMEM((3, w.shape[0] // 2, w.shape[1]), BF) for w in shards]
                        + [pltpu.VMEM(w.shape, BF) for w in shards])

    def _window(self, outs, i, s, half):
        rows, cols = self.ins[i].shape
        rh = rows // 2
        start = pl.multiple_of(half * rh, 16)
        if self.kinds[i] == "col":
            return outs[i].at[pl.ds(start, rh), pl.ds(pl.multiple_of(s * cols, 128), cols)]
        return outs[i].at[s, pl.ds(start, rh), :]

    def _copies(self, ins, outs, scr):
        n = len(ins)
        ssem, rsem, fssem, frsem, ksem, lsem, osem = scr[:7]
        land, own = scr[7:7 + n], scr[7 + n:7 + 2 * n]
        x, y, c = _place()
        sibling = (x, y, 1 - c)
        loads, stores, sends, forwards, keeps, passed = [], [], [], [], [], []
        for i in range(n):
            rows, cols = self.ins[i].shape
            rh = rows // 2
            mine = (outs[i].at[:, pl.ds(pl.multiple_of((2 * x + y) * cols, 128), cols)] if self.kinds[i] == "col"
                    else outs[i].at[2 * x + y])
            loads.append(functools.partial(pltpu.make_async_copy, ins[i], own[i], lsem.at[i]))
            stores.append(functools.partial(pltpu.make_async_copy, own[i], mine, osem.at[i]))
            src = ins[i].at[pl.ds(pl.multiple_of(c * rh, 16), rh), :]
            for j, (fx, fy) in enumerate(OTHER_CHIPS):
                px, py = _flip(x, fx), _flip(y, fy)
                sends.append(functools.partial(_remote, src, land[i].at[j], ssem.at[i, j], rsem.at[i, j], (px, py, c)))
                here = self._window(outs, i, 2 * px + py, c)
                forwards.append(functools.partial(_remote, land[i].at[j], here, fssem.at[i, j], frsem.at[i, j], sibling))
                keeps.append(functools.partial(pltpu.make_async_copy, land[i].at[j], here, ksem.at[i, j]))
                there = self._window(outs, i, 2 * px + py, 1 - c)
                passed.append(functools.partial(_remote, there, there, fssem.at[i, j], frsem.at[i, j], sibling))
        return loads, stores, sends, forwards, keeps, passed

    def start(self, ins, outs, scr):
        loads, _, sends, _, _, _ = self._copies(ins, outs, scr)
        for make in sends + loads:
            make().start()

    lead = 2

    def relay(self, ins, outs, scr):
        loads, stores, sends, forwards, keeps, _ = self._copies(ins, outs, scr)
        for load, store in zip(loads, stores):
            load().wait()
            store().start()
        for send, forward, keep in zip(sends, forwards, keeps):
            send().wait_recv()
            forward().start()
            keep().start()

    def finish(self, ins, outs, scr):
        _, stores, sends, forwards, keeps, passed = self._copies(ins, outs, scr)
        for make in passed:
            make().wait_recv()
        for make in sends + forwards:
            make().wait_send()
        for make in keeps + stores:
            make().wait()


class _ChipsHook:
    def __init__(self, parts, kinds):
        self.ins, self.kinds, n = list(parts), list(kinds), len(parts)
        self.out_shape = tuple(
            S((4, p.shape[0], p.shape[1] // 4), BF) if k == "col" else S(p.shape, BF) for p, k in zip(parts, kinds))
        dma = pltpu.SemaphoreType.DMA
        self.scratch = ([dma((n, 3)), dma((n, 3)), dma((n,)), dma((n,))]
                        + [pltpu.VMEM(o.shape[1:], BF) for o in self.out_shape])

    def _slab(self, ins, i, s):
        _, rows, cols = self.out_shape[i].shape
        if self.kinds[i] == "col":
            return ins[i].at[:, pl.ds(pl.multiple_of(s * cols, 128), cols)]
        return ins[i].at[s]

    def _copies(self, ins, outs, scr):
        ssem, rsem, lsem, osem = scr[:4]
        own = scr[4:]
        x, y, c = _place()
        loads, stores, sends = [], [], []
        for i in range(len(ins)):
            loads.append(functools.partial(pltpu.make_async_copy, self._slab(ins, i, 2 * x + y), own[i], lsem.at[i]))
            stores.append(functools.partial(pltpu.make_async_copy, own[i], outs[i].at[3], osem.at[i]))
            for j, (fx, fy) in enumerate(OTHER_CHIPS):
                px, py = _flip(x, fx), _flip(y, fy)
                sends.append(functools.partial(_remote, self._slab(ins, i, 2 * px + py), outs[i].at[j], ssem.at[i, j],
                                               rsem.at[i, j], (px, py, c)))
        return loads, stores, sends

    def start(self, ins, outs, scr):
        loads, _, sends = self._copies(ins, outs, scr)
        for make in sends + loads:
            make().start()

    lead = 2

    def relay(self, ins, outs, scr):
        loads, stores, _ = self._copies(ins, outs, scr)
        for load, store in zip(loads, stores):
            load().wait()
            store().start()

    def finish(self, ins, outs, scr):
        _, stores, sends = self._copies(ins, outs, scr)
        for make in sends + stores:
            make().wait()


SHARE_STEPS = 2


def _sum_share(slab_list, name, hook=None):
    n = len(slab_list)
    geom = [(sl.shape[1], sl.shape[1] // SHARE_STEPS, sl.shape[2]) for sl in slab_list]

    def body(*refs):
        ins, outs, scr = refs[:n], refs[n:2 * n], refs[2 * n:]
        i = pl.program_id(0)
        x, y, c = _place()
        sibling = (x, y, 1 - c)

        def copies(q, k):
            rh, tr, _ = geom[q]
            stage, lsem, ssem, rsem = scr[4 * q:4 * q + 4]
            dst = outs[q].at[pl.ds(pl.multiple_of(c * rh + k * tr, 8), tr), :]
            return (pltpu.make_async_copy(stage.at[k], dst, lsem.at[k]),
                    _remote(stage.at[k], dst, ssem.at[k], rsem, sibling))

        for q in range(n):
            acc = ins[q][3].astype(F32)
            for k in range(3):
                acc = acc + ins[q][k].astype(F32)
            scr[4 * q][i] = acc
            for cp in copies(q, i):
                cp.start()

        @pl.when(i == SHARE_STEPS - 1)
        def _():
            for q in range(n):
                rh = geom[q][0]
                for k in range(SHARE_STEPS):
                    local, remote = copies(q, k)
                    local.wait()
                    remote.wait_send()
                got = outs[q].at[pl.ds(pl.multiple_of((1 - c) * rh, 8), rh), :]
                _remote(got, got, scr[4 * q + 2].at[0], scr[4 * q + 3], sibling).wait_recv()

    dma = pltpu.SemaphoreType.DMA
    scratch = []
    for rh, tr, cols in geom:
        scratch += [pltpu.VMEM((SHARE_STEPS, tr, cols), F32), dma((SHARE_STEPS,)), dma((SHARE_STEPS,)), dma]
    return _call(
        body, name=name, grid=(SHARE_STEPS,), out_shape=tuple(S((2 * rh, cols), F32) for rh, _, cols in geom),
        in_specs=[pl.BlockSpec((4, tr, cols), lambda i: (0, i, 0)) for _, tr, cols in geom], out_specs=(ANY,) * n,
        scratch_shapes=scratch, sem=("arbitrary",), args=list(slab_list), hook=hook)


class _SmallSumHook:
    def __init__(self, buf):
        self.ins, self.out_shape = [buf], (S(buf.shape, F32),)
        dma = pltpu.SemaphoreType.DMA
        self.scratch = [pltpu.VMEM((8,) + buf.shape, F32), pltpu.VMEM(buf.shape, F32), dma((7,)), dma((7,)), dma]

    def _sends(self, scr):
        slots, _, ssem, rsem, _ = scr
        x, y, c = _place()
        me = 4 * x + 2 * y + c
        for r in range(1, 8):
            px, py, pc = _flip(x, (r >> 2) & 1), _flip(y, (r >> 1) & 1), _flip(c, r & 1)
            yield (functools.partial(_remote, slots.at[me], slots.at[me], ssem.at[r - 1], rsem.at[r - 1], (px, py, pc)),
                   functools.partial(_remote, slots.at[me], slots.at[4 * px + 2 * py + pc], ssem.at[r - 1],
                                     rsem.at[r - 1], (px, py, pc)))

    def start(self, ins, outs, scr):
        slots, _, _, _, lsem = scr
        x, y, c = _place()
        load = pltpu.make_async_copy(ins[0], slots.at[4 * x + 2 * y + c], lsem)
        load.start()
        load.wait()
        for send, _ in self._sends(scr):
            send().start()

    lead = 0

    def relay(self, ins, outs, scr):
        pass

    def finish(self, ins, outs, scr):
        slots, total, _, _, lsem = scr
        for _, arrival in self._sends(scr):
            arrival().wait_recv()
        for send, _ in self._sends(scr):
            send().wait_send()
        acc = slots[0]
        for k in range(1, 8):
            acc = acc + slots[k]
        total[...] = acc
        store = pltpu.make_async_copy(total, outs[0], lsem)
        store.start()
        store.wait()


BIG = ("ffn1_w_gu", "ffn1_w_down", "w_in", "w_out", "xattn_wq", "xattn_wkv", "xattn_wo", "ffn2_w_gu", "ffn2_w_down")
KIND = {"ffn1_w_gu": "col", "ffn1_w_down": "row", "w_in": "col", "w_out": "row", "xattn_wq": "row",
        "xattn_wkv": "col", "xattn_wo": "row", "ffn2_w_gu": "col", "ffn2_w_down": "row"}
WEIGHTS = ("rel_bias", "ffn1_norm", "ffn1_w_gu", "ffn1_w_down", "mix_norm", "w_in", "sinks", "conv_w", "w_out",
           "xattn_norm", "mem_norm", "xattn_wq", "xattn_wkv", "xattn_wo", "ffn2_norm", "ffn2_w_gu", "ffn2_w_down",
           "final_norm")
SMALL_ROWS = 48
GAIN_ROW = {"ffn1_norm": 0, "mix_norm": 1, "xattn_norm": 2, "mem_norm": 3, "ffn2_norm": 4, "final_norm": 5}
CONV_ROW, SINK_ROW, LOSS_ROW, BIAS_ROW = 6, 9, 11, 16
TAP_ROWS = 8


def _rows_block(rows, d):
    buf = jnp.zeros((SMALL_ROWS, d), F32)
    for r, v in rows.items():
        buf = lax.dynamic_update_slice(buf, v if v.ndim == 2 else v.reshape(1, -1), (r, 0))
    return buf


def _adamw_small(gsum, conv_g, small):
    names = list(small)

    def grad(k, gsum_ref, conv_ref):
        rows, cols = small[k][0].shape
        if k in GAIN_ROW:
            return gsum_ref[GAIN_ROW[k]:GAIN_ROW[k] + 1, :]
        if k == "conv_w":
            return conv_ref[...]
        r0 = SINK_ROW if k == "sinks" else BIAS_ROW
        return gsum_ref[r0:r0 + rows, 0:cols]

    def body(gsum_ref, conv_ref, *refs):
        ins, outs = refs[:3 * len(names)], refs[3 * len(names):]
        for q, k in enumerate(names):
            w_ref, m_ref, v_ref = ins[3 * q:3 * q + 3]
            g_ref, d_ref, mo_ref, vo_ref = outs[4 * q:4 * q + 4]
            gv = grad(k, gsum_ref, conv_ref)
            g_ref[...] = gv
            d_ref[...], mo_ref[...], vo_ref[...] = _adam_update(w_ref[...], gv, m_ref[...], v_ref[...])

    res = pl.pallas_call(
        body, name="adamw_small", out_shape=tuple(S(small[k][0].shape, F32) for k in names for _ in range(4)),
        compiler_params=_params())(gsum, conv_g, *[a for k in names for a in small[k]])
    return {k: res[4 * q:4 * q + 4] for q, k in enumerate(names)}


def _local_step(x, mem, pos, target, w, gains, rel_bias, sinks, conv_w, shards=None):
    t, d = x.shape
    dist = shards is not None
    w = dict(w)
    grads, slabs = {}, {}
    pos_col = pos.reshape(t, 1)
    pos_row = pos.reshape(1, t)
    bias_t = _bias_build(rel_bias).reshape(SWA_KV_HEADS, GROUP_ROWS, 2 * BLOCK)

    def gather(names):
        return _GatherHook([shards[k] for k in names], [KIND[k] for k in names]) if dist else None

    def gathered(names, hook):
        if dist:
            for k, gw in zip(names, hook.results):
                w[k] = gw if KIND[k] == "col" else gw.reshape(-1, gw.shape[-1])

    def dw(problems, tn, name, hook=None):
        if dist:
            res = _dw_pair([(a, b) for _, a, b in problems], tn, name, KIND[problems[0][0]], hook)
        else:
            res = [_dw(a, b, tn, name + "_" + k) for k, a, b in problems]
        grads.update(zip((k for k, _, _ in problems), res))

    def pair_sums(names):
        if not dist:
            return None
        flat = lambda k, v: v.reshape(-1, v.shape[-1]) if KIND[k] == "row" else v
        sums = _add_bf16([(flat(k, grads[k][0]), flat(k, grads[k][1])) for k in names], "pair_sum_" + names[0])
        return {k: (p if KIND[k] == "col" else p.reshape(4, -1, p.shape[-1])) for k, p in zip(names, sums)}

    def chips(names, parts):
        return _ChipsHook([parts[k] for k in names], [KIND[k] for k in names]) if dist else None

    def reduced(names, hook):
        if dist:
            slabs.update(zip(names, hook.results))

    names = ("w_in", "w_out")
    hook = gather(names)
    x1, g1, u1 = _ffn_fwd(x, gains["ffn1_norm"], w["ffn1_w_gu"], w["ffn1_w_down"], "ffn1_fwd", hook)
    gathered(names, hook)
    names = ("xattn_wq", "xattn_wkv", "xattn_wo")
    hook = gather(names)
    qkv, e = _mix_proj(x1, gains["mix_norm"], w["w_in"], hook)
    gathered(names, hook)
    names = ("ffn2_w_gu",)
    hook = gather(names)
    attn = _swa_fwd(qkv, pos_col, pos_row, bias_t, sinks, hook)
    gathered(names, hook)
    names = ("ffn2_w_down",)
    hook = gather(names)
    x2, merged = _mix_out_fwd(e, attn, conv_w, w["w_out"], x1, hook)
    gathered(names, hook)
    mh, kv = _mem_kv(mem, gains["mem_norm"], w["xattn_wkv"])
    x3, qx, o = _xattn_fwd(x2, gains["xattn_norm"], w["xattn_wq"], kv, w["xattn_wo"])
    dx4, g2, u2, d_final, loss = _ffn_fwd(x3, gains["ffn2_norm"], w["ffn2_w_gu"], w["ffn2_w_down"], "ffn2_fwd",
                                          head=(gains["final_norm"], target))

    dx3, d_ffn2, dgu2, a2, h4, dyh4 = _ffn_bwd(dx4, x3, gains["ffn2_norm"], g2, u2, w["ffn2_w_gu"], w["ffn2_w_down"],
                                               "ffn2_bwd")
    dw([("ffn2_w_gu", h4, dgu2)], 1408, "dw_ffn2_gu")
    dw([("ffn2_w_down", a2, dyh4)], 512, "dw_ffn2_down")
    parts = pair_sums(("ffn2_w_gu", "ffn2_w_down"))
    hook = chips(("ffn2_w_gu",), parts)
    dx2, d_xattn, dkv, dxh3, h3, dqx = _xattn_bwd(dx3, x2, gains["xattn_norm"], qx, kv, w["xattn_wq"], w["xattn_wo"], hook)
    reduced(("ffn2_w_gu",), hook)
    grads["xattn_wkv"], d_mem = _mem_bwd(dkv, mh, mem, gains["mem_norm"], w["xattn_wkv"], dist)
    hook = chips(("ffn2_w_down",), parts)
    dattn, dp, dcw, dxh2 = _mix_out_bwd(dx2, e, attn, conv_w, w["w_out"], qkv.shape[1], hook)
    reduced(("ffn2_w_down",), hook)
    dw([("xattn_wo", o, dxh3), ("xattn_wq", h3, dqx), ("w_out", merged, dxh2)], 1024, "dw_wo_wq_wout")
    names = ("xattn_wo", "xattn_wq", "xattn_wkv", "w_out")
    hook = chips(names, pair_sums(names))
    dp, gb, dsk = _swa_bwd(qkv, dattn, pos_col, pos_row, bias_t, sinks, dp, hook)
    reduced(names, hook)
    d_rel_bias, d_sinks = _bias_reduce(gb.reshape(SWA_HEADS, BLOCK, 2 * BLOCK), dsk.reshape(SWA_HEADS, BLOCK, 1))
    dx1, d_mix, h2 = _mix_in_bwd(dp, x1, gains["mix_norm"], w["w_in"], dx2)
    dw([("w_in", h2, dp)], w["w_in"].shape[1] // 4, "dw_win")
    names = ("w_in",)
    hook = chips(names, pair_sums(names))
    dx0, d_ffn1, dgu1, a1, h1, dyh1 = _ffn_bwd(dx1, x, gains["ffn1_norm"], g1, u1, w["ffn1_w_gu"], w["ffn1_w_down"],
                                               "ffn1_bwd", hook)
    reduced(names, hook)
    dw([("ffn1_w_down", a1, dyh1)], 512, "dw_ffn1_down")
    names = ("ffn1_w_down",)
    hook = chips(names, pair_sums(names))
    dw([("ffn1_w_gu", h1, dgu1)], 1408, "dw_ffn1_gu", hook)
    reduced(names, hook)
    rows = {0: d_ffn1, 1: d_mix, 2: d_xattn, 3: d_mem, 4: d_ffn2, 5: d_final, SINK_ROW: d_sinks, BIAS_ROW: d_rel_bias,
            LOSS_ROW: loss[0, 0:1]}
    rows.update({CONV_ROW + j: dcw[j] for j in range(3)})
    last = ("ffn1_w_gu",)
    return dx0, (slabs if dist else grads), _rows_block(rows, d), chips(last, pair_sums(last))


def kernel(x, mem, positions, rel_bias, ffn1_norm, ffn1_w_gu, ffn1_w_down, mix_norm, w_in, sinks, conv_w, w_out, xattn_norm, mem_norm, xattn_wq, xattn_wkv, xattn_wo, ffn2_norm, ffn2_w_gu, ffn2_w_down, final_norm, loss_target, m_rel_bias, m_ffn1_norm, m_ffn1_w_gu, m_ffn1_w_down, m_mix_norm, m_w_in, m_sinks, m_conv_w, m_w_out, m_xattn_norm, m_mem_norm, m_xattn_wq, m_xattn_wkv, m_xattn_wo, m_ffn2_norm, m_ffn2_w_gu, m_ffn2_w_down, m_final_norm, v_rel_bias, v_ffn1_norm, v_ffn1_w_gu, v_ffn1_w_down, v_mix_norm, v_w_in, v_sinks, v_conv_w, v_w_out, v_xattn_norm, v_mem_norm, v_xattn_wq, v_xattn_wkv, v_xattn_wo, v_ffn2_norm, v_ffn2_w_gu, v_ffn2_w_down, v_final_norm):
    args = dict(locals())
    wts = {k: args[k] for k in WEIGHTS}
    mom = {k: args["m_" + k] for k in WEIGHTS}
    var = {k: args["v_" + k] for k in WEIGHTS}
    d = x.shape[-1]
    s_me = 2 * lax.axis_index("x") + lax.axis_index("y")

    first = ("ffn1_w_gu", "ffn1_w_down")
    rest = tuple(k for k in BIG if k not in first)
    shards = {k: wts[k][0].astype(BF) for k in first}
    cw_cols = conv_w.shape[-1]
    placed = lax.dynamic_update_slice(jnp.zeros((TAP_ROWS, d), F32), 0.5 * conv_w[0], (0, s_me * cw_cols))
    head = _Both([_GatherHook([shards[k] for k in first], [KIND[k] for k in first]), _SmallSumHook(placed)])
    shards.update(zip(rest, _to_bf16([wts[k][0] for k in rest], "gather_ffn1", head)))
    gathered, (conv_sum,) = head.results
    whole = {k: (gw if KIND[k] == "col" else gw.reshape(-1, gw.shape[-1])) for k, gw in zip(first, gathered)}
    conv_whole = conv_sum[0:3]

    gains = {k: wts[k].reshape(1, d) for k in GAIN_ROW}
    dx0, slabs, small, last_chips = _local_step(x[0], mem[0], positions[0], loss_target[0], whole, gains, rel_bias, sinks,
                                               conv_whole, shards)

    late = ("ffn1_w_gu", "ffn1_w_down", "ffn2_w_down")
    early = tuple(k for k in BIG if k not in late)
    first_ones = tuple(k for k in BIG if k != "ffn1_w_gu")
    tail = _Both([last_chips, _SmallSumHook(small)])
    shard_grads = dict(zip(first_ones, _sum_share([slabs[k] for k in first_ones], "sum_share", tail)))
    (last_slabs,), (small_sum,) = tail.results
    shard_grads["ffn1_w_gu"], = _sum_share([last_slabs], "sum_share_ffn1_w_gu")
    quad = lambda k: (wts[k][0], shard_grads[k], mom[k][0], var[k][0])
    updates = dict(zip(early, _adamw([quad(k) for k in early], "adamw_early", 16, echo=True)))
    updates.update(zip(late, _adamw([quad(k) for k in late], "adamw_late", 8, echo=True)))
    loss = small_sum[LOSS_ROW, 0]

    out_g, out_d, out_m, out_v = {}, {}, {}, {}
    for k in BIG:
        out_g[k], out_d[k], out_m[k], out_v[k] = (a[None] for a in updates[k])

    conv_g = lax.dynamic_slice(small_sum, (CONV_ROW, s_me * cw_cols), (3, cw_cols))
    two_d = lambda a: a.reshape(a.shape[-2:]) if a.ndim > 1 else a.reshape(1, -1)
    small_names = [k for k in WEIGHTS if k not in KIND]
    done = _adamw_small(small_sum, conv_g, {k: (two_d(wts[k]), two_d(mom[k]), two_d(var[k])) for k in small_names})
    for k in small_names:
        out_g[k], out_d[k], out_m[k], out_v[k] = (a.reshape(wts[k].shape) for a in done[k])

    return (loss, dx0[None], *[out_g[k] for k in WEIGHTS], *[out_d[k] for k in WEIGHTS],
            *[out_m[k] for k in WEIGHTS], *[out_v[k] for k in WEIGHTS])
```

```python
import functools
import math

import jax
import jax.numpy as jnp
from jax import lax
from jax.experimental import pallas as pl
from jax.experimental.pallas import tpu as pltpu

BF = jnp.bfloat16
F32 = jnp.float32
I32 = jnp.int32
S = jax.ShapeDtypeStruct

EPS = 1e-6
NEG = -1e30
POS_PAD = 1 << 30
WINDOW = 128
BLOCK = 128
HEAD_DIM = 64
SWA_HEADS = 16
SWA_KV_HEADS = 4
SWA_GROUP = SWA_HEADS // SWA_KV_HEADS
MEM_HEADS = 4
REL_BUCKETS = 32
REL_MAX_DIST = 128
ADAM_LR = 0.001
ADAM_B1 = 0.9
ADAM_B2 = 0.999
ADAM_EPS = 1e-08
ADAM_WD = 0.01
ADAM_STEP = 10

V7X_VMEM_LIMIT_BYTES = 56 * 1024 * 1024
MESH = pl.DeviceIdType.MESH
ANY = pl.BlockSpec(memory_space=pl.ANY)
VMEM_SPEC = pl.BlockSpec(memory_space=pltpu.VMEM)
SMEM_SPEC = pl.BlockSpec(memory_space=pltpu.SMEM)


def _params(sem=None):
    return pltpu.CompilerParams(dimension_semantics=sem, vmem_limit_bytes=V7X_VMEM_LIMIT_BYTES)


def _resident(shape):
    nd = len(shape)
    return pl.BlockSpec(shape, lambda *_: (0,) * nd, pipeline_mode=pl.Buffered(1))


def _acc_spec(shape):
    nd = len(shape)
    return pl.BlockSpec(shape, lambda *_: (0,) * nd)


def _call(body, *, name, grid, out_shape, in_specs, out_specs, args, sem, scratch_shapes=(), aliases=None, hook=None):
    aliases = aliases or {}
    if hook is None:
        return pl.pallas_call(body, name=name, grid=grid, out_shape=out_shape, in_specs=in_specs, out_specs=out_specs,
                              scratch_shapes=list(scratch_shapes), input_output_aliases=aliases,
                              compiler_params=_params(sem))(*args)
    n_in, n_out, n_scr = len(in_specs), len(out_shape), len(scratch_shapes)
    h_in, h_out = len(hook.ins), len(hook.out_shape)

    def at_step(pick):
        conds = [pl.program_id(ax) == pick(size) for ax, size in enumerate(grid)]
        return functools.reduce(jnp.logical_and, conds)

    def hosted(*refs):
        k_in, x_in = refs[:n_in], refs[n_in:n_in + h_in]
        o0 = n_in + h_in
        k_out, x_out = refs[o0:o0 + n_out], refs[o0 + n_out:o0 + n_out + h_out]
        s0 = o0 + n_out + h_out
        k_scr, x_scr = refs[s0:s0 + n_scr], refs[s0 + n_scr:]

        @pl.when(at_step(lambda size: 0))
        def _():
            hook.start(x_in, x_out, x_scr)
        body(*k_in, *k_out, *k_scr)
        early = bool(hook.lead) and grid[0] > hook.lead
        if early:
            on_axis0 = pl.program_id(0) == grid[0] - 1 - hook.lead
            rest_first = [pl.program_id(ax) == 0 for ax in range(1, len(grid))]

            @pl.when(functools.reduce(jnp.logical_and, rest_first, on_axis0))
            def _():
                hook.relay(x_in, x_out, x_scr)

        @pl.when(at_step(lambda size: size - 1))
        def _():
            if not early:
                hook.relay(x_in, x_out, x_scr)
            hook.finish(x_in, x_out, x_scr)

    res = pl.pallas_call(
        hosted, name=name, grid=grid, out_shape=tuple(out_shape) + tuple(hook.out_shape),
        in_specs=list(in_specs) + [ANY] * h_in, out_specs=tuple(out_specs) + (ANY,) * h_out,
        scratch_shapes=list(scratch_shapes) + list(hook.scratch), input_output_aliases=aliases,
        compiler_params=_params(("arbitrary",) * len(grid)),
    )(*args, *hook.ins)
    hook.results = res[n_out:]
    return res[:n_out]


def _nn(a, b):
    return jnp.dot(a, b, preferred_element_type=F32)


def _nt(a, b):
    return lax.dot_general(a, b, (((1,), (1,)), ((), ())), preferred_element_type=F32)


def _tn(a, b):
    return lax.dot_general(a, b, (((0,), (0,)), ((), ())), preferred_element_type=F32)


def _sigmoid(v):
    return 1.0 / (1.0 + jnp.exp(-v))


def _rms(x):
    r = lax.rsqrt(jnp.mean(x * x, axis=-1, keepdims=True) + EPS)
    return x * r, r


def _rms_bwd(dh, n, r, g):
    dn = dh * g
    dx = r * (dn - n * jnp.mean(dn * n, axis=-1, keepdims=True))
    return dx, jnp.sum(dh * n, axis=0, keepdims=True)


def _ffn_fwd(x, gn, wgu, wd, name, hook=None, head=None):
    t, d = x.shape
    f = wd.shape[0]
    tm, fc = 256, 1408

    def body(x_ref, gn_ref, wgu_ref, wd_ref, *rest):
        xv = x_ref[...]
        n, _ = _rms(xv)
        h = (n * gn_ref[...]).astype(BF)
        g_ref, u_ref = rest[-2:] if head is None else rest[3:5]
        acc = jnp.zeros((tm, d), F32)
        for c0 in range(0, f, fc):
            g = _nn(h, wgu_ref[:, c0:c0 + fc])
            u = _nn(h, wgu_ref[:, f + c0:f + c0 + fc])
            g_ref[:, c0:c0 + fc] = g.astype(BF)
            u_ref[:, c0:c0 + fc] = u.astype(BF)
            a = (g * _sigmoid(g)) * u
            acc = acc + _nn(a.astype(BF), wd_ref[c0:c0 + fc, :])
        y = xv + 0.5 * acc
        if head is None:
            rest[0][...] = y
            return
        gf_ref, t_ref, dy_ref, _, _, dgf_ref, loss_ref = rest

        @pl.when(pl.program_id(0) == 0)
        def _():
            dgf_ref[...] = jnp.zeros_like(dgf_ref)
            loss_ref[...] = jnp.zeros_like(loss_ref)
        ny, ry = _rms(y)
        gf = gf_ref[...]
        err = ny * gf - t_ref[...]
        loss_ref[...] += 0.5 * jnp.sum(jnp.sum(err * err, axis=-1, keepdims=True) / d, axis=0, keepdims=True)
        dy, dgf = _rms_bwd(err / d, ny, ry, gf)
        dy_ref[...] = dy
        dgf_ref[...] += dgf

    row = pl.BlockSpec((tm, d), lambda i: (i, 0))
    frow = pl.BlockSpec((tm, f), lambda i: (i, 0))
    in_specs = [row, _resident((1, d)), _resident(wgu.shape), _resident(wd.shape)]
    out_shape = (S((t, d), F32), S((t, f), BF), S((t, f), BF))
    out_specs = (row, frow, frow)
    args = (x, gn, wgu, wd)
    if head is not None:
        in_specs += [_resident((1, d)), row]
        out_shape += (S((1, d), F32), S((1, 128), F32))
        out_specs += (_acc_spec((1, d)), _acc_spec((1, 128)))
        args += tuple(head)
    return _call(body, name=name, grid=(t // tm,), out_shape=out_shape, in_specs=in_specs, out_specs=out_specs,
                 sem=("parallel",) if head is None else ("arbitrary",), args=args, hook=hook)


def _mix_proj(x, gn, w_in, hook=None):
    t, d = x.shape
    tm = 256
    nqkv = 1536
    ne = w_in.shape[1] - nqkv

    def body(x_ref, gn_ref, w_ref, qkv_ref, e_ref):
        n, _ = _rms(x_ref[...])
        h = (n * gn_ref[...]).astype(BF)
        qkv_ref[...] = _nn(h, w_ref[:, 0:nqkv]).astype(BF)
        for c0 in range(0, ne, 1024):
            e_ref[:, c0:c0 + 1024] = _nn(h, w_ref[:, nqkv + c0:nqkv + c0 + 1024]).astype(BF)

    return _call(
        body, name="mix_proj", grid=(t // tm,),
        out_shape=(S((t, nqkv), BF), S((t, ne), BF)),
        in_specs=[pl.BlockSpec((tm, d), lambda i: (i, 0)), _resident((1, d)), _resident(w_in.shape)],
        out_specs=(pl.BlockSpec((tm, nqkv), lambda i: (i, 0)), pl.BlockSpec((tm, ne), lambda i: (i, 0))),
        sem=("parallel",), args=(x, gn, w_in), hook=hook)


def _t5_bucket(rel):
    n = jnp.maximum(rel, 0)
    max_exact = REL_BUCKETS // 2
    nf = jnp.maximum(n, 1).astype(F32)
    large = max_exact + (jnp.log(nf / max_exact) / math.log(REL_MAX_DIST / max_exact)
                         * (REL_BUCKETS - max_exact)).astype(I32)
    large = jnp.minimum(large, REL_BUCKETS - 1)
    return jnp.where(n < max_exact, n, large)


def _block_rel():
    i = lax.broadcasted_iota(I32, (BLOCK, 2 * BLOCK), 0)
    j = lax.broadcasted_iota(I32, (BLOCK, 2 * BLOCK), 1)
    return i + BLOCK - j


def _bias_build(rel_bias):
    def body(rb_ref, o_ref):
        bucket = _t5_bucket(_block_rel())
        for h in range(SWA_HEADS):
            acc = jnp.zeros((BLOCK, 2 * BLOCK), F32)
            for b in range(REL_BUCKETS):
                acc = jnp.where(bucket == b, rb_ref[b, h], acc)
            o_ref[h] = acc

    return pl.pallas_call(
        body, name="bias_build", out_shape=S((SWA_HEADS, BLOCK, 2 * BLOCK), F32),
        in_specs=[SMEM_SPEC], out_specs=VMEM_SPEC,
    )(rel_bias)


GROUP_ROWS = SWA_GROUP * BLOCK


def _swa_visible(b, pq_ref, pkp_ref, pkc_ref):
    pk = jnp.concatenate([pkp_ref[...], pkc_ref[...]], axis=1)
    col = lax.broadcasted_iota(I32, (1, 2 * BLOCK), 1)
    pk = jnp.where(jnp.logical_and(b == 0, col < BLOCK), POS_PAD, pk)
    rel = jnp.concatenate([pq_ref[...]] * SWA_GROUP, axis=0) - pk
    return jnp.logical_and(rel >= 0, rel < WINDOW)


def _group_heads(ref, hk):
    h0 = hk * SWA_GROUP
    return jnp.concatenate([ref[:, (h0 + g) * HEAD_DIM:(h0 + g + 1) * HEAD_DIM] for g in range(SWA_GROUP)], axis=0)


def _group_sinks(sink_ref, hk):
    row = lax.broadcasted_iota(I32, (GROUP_ROWS, 1), 0)
    col = jnp.zeros((GROUP_ROWS, 1), F32) + sink_ref[0, hk * SWA_GROUP]
    for g in range(1, SWA_GROUP):
        col = jnp.where(row >= g * BLOCK, sink_ref[0, hk * SWA_GROUP + g], col)
    return col


def _swa_probs(qg, kh, vis, bias, sink):
    s = _nt(qg, kh) * (HEAD_DIM ** -0.5)
    s = jnp.where(vis, s + bias, NEG)
    m = jnp.maximum(jnp.max(s, axis=-1, keepdims=True), sink)
    p = jnp.exp(s - m)
    ps = jnp.exp(sink - m)
    inv = 1.0 / (jnp.sum(p, axis=-1, keepdims=True) + ps)
    return p * inv, ps * inv


def _swa_fwd(qkv, pos_col, pos_row, bias_t, sinks, hook=None):
    t = qkv.shape[0]
    nb = t // BLOCK
    qw = SWA_HEADS * HEAD_DIM
    kw = SWA_KV_HEADS * HEAD_DIM

    def body(q_ref, kp_ref, kc_ref, vp_ref, vc_ref, pq_ref, pkp_ref, pkc_ref, bias_ref, sink_ref, o_ref):
        b = pl.program_id(0)
        vis = _swa_visible(b, pq_ref, pkp_ref, pkc_ref)
        k2 = jnp.concatenate([kp_ref[...], kc_ref[...]], axis=0)
        v2 = jnp.concatenate([vp_ref[...], vc_ref[...]], axis=0)
        for hk in range(SWA_KV_HEADS):
            kh = k2[:, hk * HEAD_DIM:(hk + 1) * HEAD_DIM]
            vh = v2[:, hk * HEAD_DIM:(hk + 1) * HEAD_DIM]
            pn, _ = _swa_probs(_group_heads(q_ref, hk), kh, vis, bias_ref[hk], _group_sinks(sink_ref, hk))
            o = _nn(pn.astype(BF), vh)
            for g in range(SWA_GROUP):
                h = hk * SWA_GROUP + g
                o_ref[:, h * HEAD_DIM:(h + 1) * HEAD_DIM] = o[g * BLOCK:(g + 1) * BLOCK]

    prev = lambda b: jnp.maximum(b - 1, 0)
    return _call(
        body, name="swa_fwd", grid=(nb,), out_shape=(S((t, qw), F32),),
        in_specs=[
            pl.BlockSpec((BLOCK, qw), lambda b: (b, 0)),
            pl.BlockSpec((BLOCK, kw), lambda b: (prev(b), qw // kw)),
            pl.BlockSpec((BLOCK, kw), lambda b: (b, qw // kw)),
            pl.BlockSpec((BLOCK, kw), lambda b: (prev(b), qw // kw + 1)),
            pl.BlockSpec((BLOCK, kw), lambda b: (b, qw // kw + 1)),
            pl.BlockSpec((BLOCK, 1), lambda b: (b, 0)),
            pl.BlockSpec((1, BLOCK), lambda b: (0, prev(b))),
            pl.BlockSpec((1, BLOCK), lambda b: (0, b)),
            _resident(bias_t.shape),
            SMEM_SPEC,
        ],
        out_specs=(pl.BlockSpec((BLOCK, qw), lambda b: (b, 0)),),
        sem=("parallel",), args=(qkv, qkv, qkv, qkv, qkv, pos_col, pos_row, pos_row, bias_t, sinks), hook=hook)[0]


HALO = 16


def _conv_taps(z, zh, first):
    tm = z.shape[0]
    zh = jnp.where(first, 0.0, zh)
    row = lax.broadcasted_iota(I32, (tm, 1), 0)
    z1 = jnp.where(row == 0, zh[HALO - 1:HALO, :], pltpu.roll(z, 1, 0))
    z2 = jnp.where(row == 0, zh[HALO - 2:HALO - 1, :], jnp.where(row == 1, zh[HALO - 1:HALO, :], pltpu.roll(z, 2, 0)))
    return z1, z2


def _mix_out_fwd(e, attn, conv_w, w_out, x, hook=None):
    t, d = x.shape
    tm = 256
    hb = tm // HALO
    f32 = lambda ref: ref[...].astype(F32)

    def body(c_ref, b_ref, u_ref, ga_ref, gc_ref, ch_ref, uh_ref, attn_ref, cw_ref, w_ref, x_ref, xo_ref, mg_ref):
        i = pl.program_id(0)
        z = f32(c_ref) * f32(u_ref)
        z1, z2 = _conv_taps(z, f32(ch_ref) * f32(uh_ref), i == 0)
        s = cw_ref[0:1, :] * z2 + cw_ref[1:2, :] * z1 + cw_ref[2:3, :] * z
        conv = f32(b_ref) * s
        merged = (_sigmoid(f32(ga_ref)) * attn_ref[...] + _sigmoid(f32(gc_ref)) * conv).astype(BF)
        mg_ref[...] = merged
        xo_ref[...] = x_ref[...] + _nn(merged, w_ref[...])

    ecol = lambda cb: pl.BlockSpec((tm, d), lambda i: (i, cb))
    halo = lambda cb: pl.BlockSpec((HALO, d), lambda i: (jnp.maximum(i * hb - 1, 0), cb))
    row = pl.BlockSpec((tm, d), lambda i: (i, 0))
    return _call(
        body, name="mix_out_fwd", grid=(t // tm,),
        out_shape=(S((t, d), F32), S((t, d), BF)),
        in_specs=[ecol(0), ecol(1), ecol(2), ecol(3), ecol(4), halo(0), halo(2), row,
                  _resident(conv_w.shape), _resident(w_out.shape), row],
        out_specs=(row, row),
        sem=("parallel",), args=(e, e, e, e, e, e, e, attn, conv_w, w_out, x), hook=hook)


def _mem_kv(mem, gm, wkv):
    m, d = mem.shape

    def body(mem_ref, gm_ref, w_ref, mh_ref, kv_ref):
        n, _ = _rms(mem_ref[...])
        mh = (n * gm_ref[...]).astype(BF)
        mh_ref[...] = mh
        kv_ref[...] = _nn(mh, w_ref[...]).astype(BF)

    return pl.pallas_call(
        body, name="mem_kv", out_shape=(S((m, d), BF), S((m, wkv.shape[1]), BF)),
        compiler_params=_params(),
    )(mem, gm, wkv)


def _xattn_probs(qh, kh):
    s = _nt(qh, kh) * (kh.shape[1] ** -0.5)
    p = jnp.exp(s - jnp.max(s, axis=-1, keepdims=True))
    return p * (1.0 / jnp.sum(p, axis=-1, keepdims=True))


def _xattn_fwd(x, gn, wq, kv, wo):
    t, d = x.shape
    tm = 256
    hd = d // MEM_HEADS

    def body(x_ref, gn_ref, wq_ref, kv_ref, wo_ref, xo_ref, q_ref, o_ref):
        xv = x_ref[...]
        n, _ = _rms(xv)
        q = _nn((n * gn_ref[...]).astype(BF), wq_ref[...]).astype(BF)
        q_ref[...] = q
        outs = []
        for hh in range(MEM_HEADS):
            p = _xattn_probs(q[:, hh * hd:(hh + 1) * hd], kv_ref[:, hh * hd:(hh + 1) * hd])
            outs.append(_nn(p.astype(BF), kv_ref[:, d + hh * hd:d + (hh + 1) * hd]))
        o = jnp.concatenate(outs, axis=1).astype(BF)
        o_ref[...] = o
        xo_ref[...] = xv + _nn(o, wo_ref[...])

    row = pl.BlockSpec((tm, d), lambda i: (i, 0))
    return pl.pallas_call(
        body, name="xattn_fwd", grid=(t // tm,),
        out_shape=(S((t, d), F32), S((t, d), BF), S((t, d), BF)),
        in_specs=[row, _resident((1, d)), _resident(wq.shape), _resident(kv.shape), _resident(wo.shape)],
        out_specs=(row, row, row),
        compiler_params=_params(("parallel",)),
    )(x, gn, wq, kv, wo)


def _ffn_bwd(dxo, x, gn, g, u, wgu, wd, name, hook=None):
    t, d = x.shape
    f = wd.shape[0]
    tm, fc = 256, 1408

    def body(dxo_ref, x_ref, gn_ref, g_ref, u_ref, wgu_ref, wd_ref, dx_ref, dgn_ref, dgu_ref, a_ref, h_ref, dyh_ref):
        @pl.when(pl.program_id(0) == 0)
        def _():
            dgn_ref[...] = jnp.zeros_like(dgn_ref)
        dxov = dxo_ref[...]
        dyh = (0.5 * dxov).astype(BF)
        dyh_ref[...] = dyh
        n, r = _rms(x_ref[...])
        gnv = gn_ref[...]
        h_ref[...] = (n * gnv).astype(BF)
        dh = jnp.zeros((tm, d), F32)
        for c0 in range(0, f, fc):
            gv = g_ref[:, c0:c0 + fc].astype(F32)
            uv = u_ref[:, c0:c0 + fc].astype(F32)
            da = _nt(dyh, wd_ref[c0:c0 + fc, :])
            sg = _sigmoid(gv)
            silu = gv * sg
            a_ref[:, c0:c0 + fc] = (silu * uv).astype(BF)
            dg = (da * uv * (sg * (1.0 + gv * (1.0 - sg)))).astype(BF)
            du = (da * silu).astype(BF)
            dgu_ref[:, c0:c0 + fc] = dg
            dgu_ref[:, f + c0:f + c0 + fc] = du
            dh = dh + _nt(dg, wgu_ref[:, c0:c0 + fc]) + _nt(du, wgu_ref[:, f + c0:f + c0 + fc])
        dx, dgn = _rms_bwd(dh, n, r, gnv)
        dx_ref[...] = dxov + dx
        dgn_ref[...] += dgn

    row = pl.BlockSpec((tm, d), lambda i: (i, 0))
    frow = pl.BlockSpec((tm, f), lambda i: (i, 0))
    return _call(
        body, name=name, grid=(t // tm,),
        out_shape=(S((t, d), F32), S((1, d), F32), S((t, 2 * f), BF), S((t, f), BF), S((t, d), BF), S((t, d), BF)),
        in_specs=[row, row, _resident((1, d)), frow, frow, _resident(wgu.shape), _resident(wd.shape)],
        out_specs=(row, _acc_spec((1, d)), pl.BlockSpec((tm, 2 * f), lambda i: (i, 0)), frow, row, row),
        sem=("arbitrary",), args=(dxo, x, gn, g, u, wgu, wd), hook=hook)


def _dw(a, b, tn, name):
    t, ka = a.shape
    nb = b.shape[1]
    tt = 1024
    nt = t // tt

    def body(a_ref, b_ref, o_ref, acc_ref):
        k = pl.program_id(1)

        @pl.when(k == 0)
        def _():
            acc_ref[...] = jnp.zeros_like(acc_ref)
        acc_ref[...] += _tn(a_ref[...], b_ref[...])

        @pl.when(k == nt - 1)
        def _():
            o_ref[...] = acc_ref[...].astype(BF)

    return pl.pallas_call(
        body, name=name, grid=(nb // tn, nt), out_shape=S((ka, nb), BF),
        in_specs=[pl.BlockSpec((tt, ka), lambda j, k: (k, 0)), pl.BlockSpec((tt, tn), lambda j, k: (k, j))],
        out_specs=pl.BlockSpec((ka, tn), lambda j, k: (0, j)), scratch_shapes=[pltpu.VMEM((ka, tn), F32)],
        compiler_params=_params(("parallel", "arbitrary")),
    )(a, b)


def _dw_pair(pairs, tn, name, kind, hook=None):
    npairs = len(pairs)
    t, ka = pairs[0][0].shape
    nb = pairs[0][1].shape[1]
    tt = 2048 if npairs == 1 else 1024
    nt, nj = t // tt, nb // tn
    col = kind == "col"
    rh = ka // 2 if col else ka // 8
    tile = (rh, tn) if col else (4, rh, tn)
    half = (rh, nb) if col else (4, rh, nb)
    lead = (slice(None),) * (len(tile) - 1)

    def body(*refs):
        ins, mines, sibs = refs[:2 * npairs], refs[2 * npairs:3 * npairs], refs[3 * npairs:4 * npairs]
        acc_ref, stage, ssem, rsem = refs[4 * npairs:]
        j, k = pl.program_id(0), pl.program_id(1)
        x, y, c = _place()
        sibling = (x, y, 1 - c)

        def send(p, slot, jj):
            dst = sibs[p].at[lead + (pl.ds(pl.multiple_of(jj * tn, 128), tn),)]
            return _remote(stage.at[slot], dst, ssem.at[slot], rsem.at[p], sibling)

        def rows(s, whose):
            return acc_ref[pl.ds(pl.multiple_of(s * 2 * rh + whose * rh, 16), rh), :].astype(BF)

        def step(p):
            a_ref, b_ref, mine_ref = ins[2 * p], ins[2 * p + 1], mines[p]

            @pl.when(k == 0)
            def _():
                acc_ref[...] = jnp.zeros_like(acc_ref)
            acc_ref[...] += _tn(a_ref[...], b_ref[...])

            @pl.when(k == nt - 1)
            def _():
                slot = j % 2

                @pl.when(j >= 2)
                def _():
                    send(p, slot, 0).wait_send()
                if col:
                    mine_ref[...] = rows(0, c)
                    stage[slot] = rows(0, 1 - c)
                else:
                    for s in range(4):
                        mine_ref[s] = rows(s, c)
                        stage[slot, s] = rows(s, 1 - c)
                send(p, slot, j - p * nj).start()

        for p in range(npairs):
            pl.when(j // nj == p)(functools.partial(step, p))

        @pl.when(jnp.logical_and(j == npairs * nj - 1, k == nt - 1))
        def _():
            for jj in range(max(npairs * nj - 2, 0), npairs * nj):
                send(0, jj % 2, 0).wait_send()
            for p in range(npairs):
                _remote(sibs[p], sibs[p], ssem.at[0], rsem.at[p], sibling).wait_recv()

    in_specs, args = [], []
    for p, (a, b) in enumerate(pairs):
        on = lambda j, p=p: j // nj == p
        in_specs += [pl.BlockSpec((tt, ka), lambda j, k, on=on: (jnp.where(on(j), k, 0), 0)),
                     pl.BlockSpec((tt, tn), lambda j, k, on=on, p=p: (jnp.where(on(j), k, 0), jnp.clip(j - p * nj, 0, nj - 1)))]
        args += [a, b]
    mine_spec = lambda p: pl.BlockSpec(tile, (lambda j, k: (0, jnp.clip(j - p * nj, 0, nj - 1))) if col
                                       else (lambda j, k: (0, 0, jnp.clip(j - p * nj, 0, nj - 1))))
    res = _call(
        body, name=name, grid=(npairs * nj, nt), out_shape=(S(half, BF),) * (2 * npairs),
        in_specs=in_specs, out_specs=tuple(mine_spec(p) for p in range(npairs)) + (ANY,) * npairs,
        scratch_shapes=[pltpu.VMEM((ka, tn), F32), pltpu.VMEM((2,) + tile, BF), pltpu.SemaphoreType.DMA((2,)),
                        pltpu.SemaphoreType.DMA((npairs,))],
        sem=("arbitrary", "arbitrary"), args=args, hook=hook)
    return [(res[p], res[npairs + p]) for p in range(npairs)]


def _xattn_bwd(dxo, x, gn, q, kv, wq, wo, hook=None):
    t, d = x.shape
    tm = 256
    hd = d // MEM_HEADS
    nkv = kv.shape[0]

    def body(dxo_ref, x_ref, gn_ref, q_ref, kv_ref, wq_ref, wo_ref, dx_ref, dgn_ref, dkv_ref, dxh_ref, h_ref, dq_ref):
        @pl.when(pl.program_id(0) == 0)
        def _():
            dgn_ref[...] = jnp.zeros_like(dgn_ref)
            dkv_ref[...] = jnp.zeros_like(dkv_ref)
        dxov = dxo_ref[...]
        dxh = dxov.astype(BF)
        dxh_ref[...] = dxh
        do = _nt(dxh, wo_ref[...]).astype(BF)
        dqs = []
        for hh in range(MEM_HEADS):
            lo, hi = hh * hd, (hh + 1) * hd
            qh = q_ref[:, lo:hi]
            kh = kv_ref[:, lo:hi]
            vh = kv_ref[:, d + lo:d + hi]
            doh = do[:, lo:hi]
            p = _xattn_probs(qh, kh)
            dp = _nt(doh, vh)
            ds = (p * (dp - jnp.sum(p * dp, axis=-1, keepdims=True)) * (hd ** -0.5)).astype(BF)
            dqs.append(_nn(ds, kh))
            dkv_ref[:, lo:hi] += _tn(ds, qh)
            dkv_ref[:, d + lo:d + hi] += _tn(p.astype(BF), doh)
        dq = jnp.concatenate(dqs, axis=1).astype(BF)
        dq_ref[...] = dq
        n, r = _rms(x_ref[...])
        gnv = gn_ref[...]
        h_ref[...] = (n * gnv).astype(BF)
        dx, dgn = _rms_bwd(_nt(dq, wq_ref[...]), n, r, gnv)
        dx_ref[...] = dxov + dx
        dgn_ref[...] += dgn

    row = pl.BlockSpec((tm, d), lambda i: (i, 0))
    return _call(
        body, name="xattn_bwd", grid=(t // tm,),
        out_shape=(S((t, d), F32), S((1, d), F32), S((nkv, 2 * d), F32), S((t, d), BF), S((t, d), BF), S((t, d), BF)),
        in_specs=[row, row, _resident((1, d)), row, _resident(kv.shape), _resident(wq.shape), _resident(wo.shape)],
        out_specs=(row, _acc_spec((1, d)), _acc_spec((nkv, 2 * d)), row, row, row),
        sem=("arbitrary",), args=(dxo, x, gn, q, kv, wq, wo), hook=hook)


def _mem_bwd(dkv, mh, mem, gm, wkv, pair=False):
    m, d = mem.shape
    rows, cols = wkv.shape
    rh = rows // 2

    def body(dkv_ref, mh_ref, mem_ref, gm_ref, w_ref, *outs):
        dkvb = dkv_ref[...].astype(BF)
        dw = _tn(mh_ref[...], dkvb).astype(BF)
        dmh = _nt(dkvb, w_ref[...])
        n, _ = _rms(mem_ref[...])
        if not pair:
            dw_ref, dgm_ref = outs
            dw_ref[...] = dw
        else:
            mine_ref, sib_ref, dgm_ref, whole, ssem, rsem = outs
            x, y, c = _place()
            whole[...] = dw
            cp = _remote(whole.at[pl.ds(pl.multiple_of((1 - c) * rh, 16), rh), :], sib_ref, ssem, rsem, (x, y, 1 - c))
            cp.start()
            mine_ref[...] = whole[pl.ds(pl.multiple_of(c * rh, 16), rh), :]
            cp.wait()
        dgm_ref[...] = jnp.sum(dmh * n, axis=0, keepdims=True)

    if not pair:
        return pl.pallas_call(body, name="mem_bwd", out_shape=(S(wkv.shape, BF), S((1, d), F32)),
                              compiler_params=_params())(dkv, mh, mem, gm, wkv)
    mine, sib, dgm = pl.pallas_call(
        body, name="mem_bwd", out_shape=(S((rh, cols), BF), S((rh, cols), BF), S((1, d), F32)),
        in_specs=[VMEM_SPEC] * 5, out_specs=(VMEM_SPEC, ANY, VMEM_SPEC),
        scratch_shapes=[pltpu.VMEM((rows, cols), BF), pltpu.SemaphoreType.DMA, pltpu.SemaphoreType.DMA],
        compiler_params=_params())(dkv, mh, mem, gm, wkv)
    return (mine, sib), dgm


def _mix_out_bwd(dxo, e, attn, conv_w, w_out, nqkv, hook=None):
    t, d = attn.shape
    tm = 256
    nt = t // tm
    f32 = lambda ref: ref[...].astype(F32)

    def body(dxo_ref, dxn_ref, c_ref, b_ref, u_ref, ga_ref, gc_ref, ch_ref, uh_ref, bn_ref, gcn_ref,
             attn_ref, cw_ref, w_ref, dattn_ref, de_ref, dcw_ref, dxh_ref):
        i = pl.program_id(0)

        @pl.when(i == 0)
        def _():
            dcw_ref[...] = jnp.zeros_like(dcw_ref)
        dxh = dxo_ref[...].astype(BF)
        dxh_ref[...] = dxh
        w = w_ref[...]
        dm = _nt(dxh, w)
        dmn = _nt(dxn_ref[...].astype(BF), w)
        cv, bv, uv = f32(c_ref), f32(b_ref), f32(u_ref)
        sga = _sigmoid(f32(ga_ref))
        sgc = _sigmoid(f32(gc_ref))
        z = cv * uv
        z1, z2 = _conv_taps(z, f32(ch_ref) * f32(uh_ref), i == 0)
        w0, w1, w2 = cw_ref[0:1, :], cw_ref[1:2, :], cw_ref[2:3, :]
        s = w0 * z2 + w1 * z1 + w2 * z
        av = attn_ref[...]
        dattn_ref[...] = (dm * sga).astype(BF)
        dconv = dm * sgc
        ds = dconv * bv
        dsn = jnp.where(i == nt - 1, 0.0, dmn * _sigmoid(gcn_ref[0:8, :].astype(F32)) * bn_ref[0:8, :].astype(F32))
        row = lax.broadcasted_iota(I32, (tm, 1), 0)
        dsp1 = jnp.where(row == tm - 1, dsn[0:1, :], pltpu.roll(ds, tm - 1, 0))
        dsp2 = jnp.where(row == tm - 2, dsn[0:1, :], jnp.where(row == tm - 1, dsn[1:2, :], pltpu.roll(ds, tm - 2, 0)))
        dz = w2 * ds + w1 * dsp1 + w0 * dsp2
        de_ref[:, nqkv:nqkv + d] = (dz * uv).astype(BF)
        de_ref[:, nqkv + d:nqkv + 2 * d] = (dconv * s).astype(BF)
        de_ref[:, nqkv + 2 * d:nqkv + 3 * d] = (dz * cv).astype(BF)
        de_ref[:, nqkv + 3 * d:nqkv + 4 * d] = (dm * av * sga * (1.0 - sga)).astype(BF)
        de_ref[:, nqkv + 4 * d:nqkv + 5 * d] = (dm * (bv * s) * sgc * (1.0 - sgc)).astype(BF)
        dcw_ref[0:1, :] += jnp.sum(ds * z2, axis=0, keepdims=True)
        dcw_ref[1:2, :] += jnp.sum(ds * z1, axis=0, keepdims=True)
        dcw_ref[2:3, :] += jnp.sum(ds * z, axis=0, keepdims=True)

    ecol = lambda cb: pl.BlockSpec((tm, d), lambda i: (i, cb))
    prev = lambda cb: pl.BlockSpec((HALO, d), lambda i: (jnp.maximum(i * (tm // HALO) - 1, 0), cb))
    nxt = lambda rows, cb: pl.BlockSpec((rows, d), lambda i: (jnp.minimum((i + 1) * (tm // rows), t // rows - 1), cb))
    row = pl.BlockSpec((tm, d), lambda i: (i, 0))
    return _call(
        body, name="mix_out_bwd", grid=(nt,),
        out_shape=(S((t, d), BF), S((t, nqkv + 5 * d), BF), S((8, d), F32), S((t, d), BF)),
        in_specs=[row, nxt(8, 0), ecol(0), ecol(1), ecol(2), ecol(3), ecol(4), prev(0), prev(2), nxt(HALO, 1), nxt(HALO, 4),
                  row, _resident(conv_w.shape), _resident(w_out.shape)],
        out_specs=(row, pl.BlockSpec((tm, nqkv + 5 * d), lambda i: (i, 0)), _acc_spec((8, d)), row),
        sem=("arbitrary",), args=(dxo, dxo, e, e, e, e, e, e, e, e, e, attn, conv_w, w_out), hook=hook)


def _swa_bwd(qkv, dattn, pos_col, pos_row, bias_t, sinks, dp, hook=None):
    t = qkv.shape[0]
    nb = t // BLOCK
    qw = SWA_HEADS * HEAD_DIM
    kw = SWA_KV_HEADS * HEAD_DIM

    def body(q_ref, kp_ref, kc_ref, vp_ref, vc_ref, do_ref, pq_ref, pkp_ref, pkc_ref, bias_ref, sink_ref,
             dp_in, dp_ref, gb_ref, dsk_ref, carry_ref, dq_ref):
        b = pl.program_id(0)

        @pl.when(b == 0)
        def _():
            gb_ref[...] = jnp.zeros_like(gb_ref)
            dsk_ref[...] = jnp.zeros_like(dsk_ref)
            carry_ref[...] = jnp.zeros_like(carry_ref)
            dq_ref[...] = jnp.zeros_like(dq_ref)
        dp_ref[:, 0:qw] = dq_ref[...]

        @pl.when(b < nb)
        def _():
            vis = _swa_visible(b, pq_ref, pkp_ref, pkc_ref)
            k2 = jnp.concatenate([kp_ref[...], kc_ref[...]], axis=0)
            v2 = jnp.concatenate([vp_ref[...], vc_ref[...]], axis=0)
            for hk in range(SWA_KV_HEADS):
                lo, hi = hk * HEAD_DIM, (hk + 1) * HEAD_DIM
                kh = k2[:, lo:hi]
                vh = v2[:, lo:hi]
                qg = _group_heads(q_ref, hk)
                dog = _group_heads(do_ref, hk)
                pn, psn = _swa_probs(qg, kh, vis, bias_ref[hk], _group_sinks(sink_ref, hk))
                dp = _nt(dog, vh)
                delta = jnp.sum(pn * dp, axis=-1, keepdims=True)
                ds = pn * (dp - delta)
                gb_ref[hk] += ds
                dsk_ref[hk] += -psn * delta
                dsb = (ds * (HEAD_DIM ** -0.5)).astype(BF)
                dqg = _nn(dsb, kh).astype(BF)
                for g in range(SWA_GROUP):
                    h = hk * SWA_GROUP + g
                    dq_ref[:, h * HEAD_DIM:(h + 1) * HEAD_DIM] = dqg[g * BLOCK:(g + 1) * BLOCK]
                dk = _tn(dsb, qg)
                dv = _tn(pn.astype(BF), dog)
                dp_ref[:, qw + lo:qw + hi] = (carry_ref[:, lo:hi] + dk[0:BLOCK]).astype(BF)
                dp_ref[:, qw + kw + lo:qw + kw + hi] = (carry_ref[:, kw + lo:kw + hi] + dv[0:BLOCK]).astype(BF)
                carry_ref[:, lo:hi] = dk[BLOCK:2 * BLOCK]
                carry_ref[:, kw + lo:kw + hi] = dv[BLOCK:2 * BLOCK]

        @pl.when(b == nb)
        def _():
            dp_ref[:, qw:qw + 2 * kw] = carry_ref[...].astype(BF)

    cur = lambda b: jnp.minimum(b, nb - 1)
    prev = lambda b: jnp.maximum(cur(b) - 1, 0)
    return _call(
        body, name="swa_bwd", grid=(nb + 1,),
        out_shape=(S(dp.shape, BF), S((SWA_KV_HEADS, GROUP_ROWS, 2 * BLOCK), F32), S((SWA_KV_HEADS, GROUP_ROWS, 1), F32)),
        in_specs=[
            pl.BlockSpec((BLOCK, qw), lambda b: (cur(b), 0)),
            pl.BlockSpec((BLOCK, kw), lambda b: (prev(b), qw // kw)),
            pl.BlockSpec((BLOCK, kw), lambda b: (cur(b), qw // kw)),
            pl.BlockSpec((BLOCK, kw), lambda b: (prev(b), qw // kw + 1)),
            pl.BlockSpec((BLOCK, kw), lambda b: (cur(b), qw // kw + 1)),
            pl.BlockSpec((BLOCK, qw), lambda b: (cur(b), 0)),
            pl.BlockSpec((BLOCK, 1), lambda b: (cur(b), 0)),
            pl.BlockSpec((1, BLOCK), lambda b: (0, prev(b))),
            pl.BlockSpec((1, BLOCK), lambda b: (0, cur(b))),
            _resident(bias_t.shape),
            SMEM_SPEC,
            ANY,
        ],
        out_specs=(
            pl.BlockSpec((BLOCK, qw + 2 * kw), lambda b: (jnp.maximum(b - 1, 0), 0)),
            _acc_spec((SWA_KV_HEADS, GROUP_ROWS, 2 * BLOCK)),
            _acc_spec((SWA_KV_HEADS, GROUP_ROWS, 1)),
        ),
        scratch_shapes=[pltpu.VMEM((BLOCK, 2 * kw), F32), pltpu.VMEM((BLOCK, qw), BF)],
        aliases={11: 0}, sem=("arbitrary",),
        args=(qkv, qkv, qkv, qkv, qkv, dattn, pos_col, pos_row, pos_row, bias_t, sinks, dp), hook=hook)


def _bias_reduce(gb, dsk):
    def body(gb_ref, dsk_ref, drb_ref, dsink_ref):
        bucket = _t5_bucket(_block_rel())
        for b in range(REL_BUCKETS):
            mask = bucket == b
            for h in range(SWA_HEADS):
                drb_ref[b, h] = jnp.sum(jnp.where(mask, gb_ref[h], 0.0))
        for h in range(SWA_HEADS):
            dsink_ref[0, h] = jnp.sum(dsk_ref[h])

    return pl.pallas_call(
        body, name="bias_reduce", out_shape=(S((REL_BUCKETS, SWA_HEADS), F32), S((1, SWA_HEADS), F32)),
        in_specs=[VMEM_SPEC, VMEM_SPEC], out_specs=(SMEM_SPEC, SMEM_SPEC),
    )(gb, dsk)


def _mix_in_bwd(dp, x, gn, w_in, dxo):
    t, d = x.shape
    tm = 256
    npr = dp.shape[1]

    def body(dp_ref, x_ref, gn_ref, w_ref, dxo_ref, dx_ref, dgn_ref, h_ref):
        @pl.when(pl.program_id(0) == 0)
        def _():
            dgn_ref[...] = jnp.zeros_like(dgn_ref)
        dh = jnp.zeros((tm, d), F32)
        for c0 in range(0, npr, 1024):
            c1 = min(c0 + 1024, npr)
            dh = dh + _nt(dp_ref[:, c0:c1], w_ref[:, c0:c1])
        n, r = _rms(x_ref[...])
        gnv = gn_ref[...]
        h_ref[...] = (n * gnv).astype(BF)
        dx, dgn = _rms_bwd(dh, n, r, gnv)
        dx_ref[...] = dxo_ref[...] + dx
        dgn_ref[...] += dgn

    row = pl.BlockSpec((tm, d), lambda i: (i, 0))
    return pl.pallas_call(
        body, name="mix_in_bwd", grid=(t // tm,),
        out_shape=(S((t, d), F32), S((1, d), F32), S((t, d), BF)),
        in_specs=[pl.BlockSpec((tm, npr), lambda i: (i, 0)), row, _resident((1, d)), _resident(w_in.shape), row],
        out_specs=(row, _acc_spec((1, d)), row),
        compiler_params=_params(("arbitrary",)),
    )(dp, x, gn, w_in, dxo)


CAST_STEPS = 4


def _to_bf16(arrays, name, hook=None):
    def body(*refs):
        for src, dst in zip(refs[:len(arrays)], refs[len(arrays):]):
            dst[...] = src[...].astype(BF)

    blocks = [pl.BlockSpec((a.shape[0] // CAST_STEPS, a.shape[1]), lambda i: (i, 0)) for a in arrays]
    return _call(body, name=name, grid=(CAST_STEPS,), out_shape=tuple(S(a.shape, BF) for a in arrays), in_specs=blocks,
                 out_specs=tuple(blocks), sem=("parallel",), args=list(arrays), hook=hook)


PAIR_STEPS = 4


def _add_bf16(pairs, name):
    def body(*refs):
        ins, outs = refs[:2 * len(pairs)], refs[2 * len(pairs):]
        for q, o_ref in enumerate(outs):
            o_ref[...] = (ins[2 * q][...].astype(F32) + ins[2 * q + 1][...].astype(F32)).astype(BF)

    in_specs, out_specs, out_shape, args = [], [], [], []
    for a, b in pairs:
        rows, cols = a.shape
        blk = pl.BlockSpec((rows // PAIR_STEPS, cols), lambda i: (i, 0))
        in_specs += [blk, blk]
        out_specs.append(blk)
        out_shape.append(S((rows, cols), BF))
        args += [a, b]
    return pl.pallas_call(body, name=name, grid=(PAIR_STEPS,), out_shape=tuple(out_shape), in_specs=in_specs,
                          out_specs=tuple(out_specs), compiler_params=_params(("parallel",)))(*args)


def _adam_update(w, g, m, v):
    mn = ADAM_B1 * m + (1.0 - ADAM_B1) * g
    vn = ADAM_B2 * v + (1.0 - ADAM_B2) * (g * g)
    m_hat = mn / (1.0 - ADAM_B1 ** ADAM_STEP)
    v_hat = vn / (1.0 - ADAM_B2 ** ADAM_STEP)
    return -ADAM_LR * (m_hat / (jnp.sqrt(v_hat) + ADAM_EPS) + ADAM_WD * w), mn, vn


def _adamw(quads, name, steps, echo=False, hook=None):
    n_out = 4 if echo else 3

    def body(*refs):
        ins, outs = refs[:4 * len(quads)], refs[4 * len(quads):]
        for q in range(len(quads)):
            w_ref, g_ref, m_ref, v_ref = ins[4 * q:4 * q + 4]
            res = outs[n_out * q:n_out * q + n_out]
            gv = g_ref[...]
            if echo:
                res[0][...] = gv
            res[-3][...], res[-2][...], res[-1][...] = _adam_update(w_ref[...], gv, m_ref[...], v_ref[...])

    in_specs, out_specs, out_shape, args = [], [], [], []
    for quad in quads:
        rows, cols = quad[0].shape
        blk = pl.BlockSpec((rows // steps, cols), lambda i: (i, 0))
        in_specs += [blk] * 4
        out_specs += [blk] * n_out
        out_shape += [S((rows, cols), F32)] * n_out
        args += list(quad)
    res = _call(body, name=name, grid=(steps,), out_shape=tuple(out_shape), in_specs=in_specs, out_specs=tuple(out_specs),
                sem=("parallel",), args=args, hook=hook)
    return [res[n_out * q:n_out * q + n_out] for q in range(len(quads))]


def _place():
    x, y, c = lax.axis_index("x"), lax.axis_index("y"), lax.axis_index("c")
    return x, y, c


OTHER_CHIPS = ((1, 0), (0, 1), (1, 1))


def _flip(v, f):
    return 1 - v if f else v


def _remote(src, dst, ssem, rsem, dev):
    return pltpu.make_async_remote_copy(src_ref=src, dst_ref=dst, send_sem=ssem, recv_sem=rsem,
                                        device_id=dev, device_id_type=MESH)


class _Both:
    def __init__(self, hooks):
        self.hooks = hooks
        self.ins = [a for h in hooks for a in h.ins]
        self.out_shape = tuple(o for h in hooks for o in h.out_shape)
        self.scratch = [x for h in hooks for x in h.scratch]

    def _each(self, ins, outs, scr):
        i = o = s = 0
        for h in self.hooks:
            ni, no, ns = len(h.ins), len(h.out_shape), len(h.scratch)
            yield h, ins[i:i + ni], outs[o:o + no], scr[s:s + ns]
            i, o, s = i + ni, o + no, s + ns

    def start(self, ins, outs, scr):
        for h, *refs in self._each(ins, outs, scr):
            h.start(*refs)

    def relay(self, ins, outs, scr):
        for h, *refs in self._each(ins, outs, scr):
            h.relay(*refs)

    def finish(self, ins, outs, scr):
        for h, *refs in self._each(ins, outs, scr):
            h.finish(*refs)

    @property
    def lead(self):
        return max(h.lead for h in self.hooks)

    @property
    def results(self):
        return [h.results for h in self.hooks]

    @results.setter
    def results(self, res):
        for h, _, mine, _ in self._each((), tuple(res), ()):
            h.results = mine


class _GatherHook:
    def __init__(self, shards, kinds):
        self.ins, self.kinds, n = list(shards), list(kinds), len(shards)
        self.out_shape = tuple(
            S((w.shape[0], 4 * w.shape[1]), BF) if k == "col" else S((4,) + w.shape, BF) for w, k in zip(shards, kinds))
        dma = pltpu.SemaphoreType.DMA
        self.scratch = ([dma((n, 3)) for _ in range(5)] + [dma((n,)), dma((n,))]
                        + [pltpu.VMEM((3, w.shape[0] // 2, w.shape[1]), BF) for w in shards]
                        + [pltpu.VMEM(w.shape, BF) for w in shards])

    def _window(self, outs, i, s, half):
        rows, cols = self.ins[i].shape
        rh = rows // 2
        start = pl.multiple_of(half * rh, 16)
        if self.kinds[i] == "col":
            return outs[i].at[pl.ds(start, rh), pl.ds(pl.multiple_of(s * cols, 128), cols)]
        return outs[i].at[s, pl.ds(start, rh), :]

    def _copies(self, ins, outs, scr):
        n = len(ins)
        ssem, rsem, fssem, frsem, ksem, lsem, osem = scr[:7]
        land, own = scr[7:7 + n], scr[7 + n:7 + 2 * n]
        x, y, c = _place()
        sibling = (x, y, 1 - c)
        loads, stores, sends, forwards, keeps, passed = [], [], [], [], [], []
        for i in range(n):
            rows, cols = self.ins[i].shape
            rh = rows // 2
            mine = (outs[i].at[:, pl.ds(pl.multiple_of((2 * x + y) * cols, 128), cols)] if self.kinds[i] == "col"
                    else outs[i].at[2 * x + y])
            loads.append(functools.partial(pltpu.make_async_copy, ins[i], own[i], lsem.at[i]))
            stores.append(functools.partial(pltpu.make_async_copy, own[i], mine, osem.at[i]))
            src = ins[i].at[pl.ds(pl.multiple_of(c * rh, 16), rh), :]
            for j, (fx, fy) in enumerate(OTHER_CHIPS):
                px, py = _flip(x, fx), _flip(y, fy)
                sends.append(functools.partial(_remote, src, land[i].at[j], ssem.at[i, j], rsem.at[i, j], (px, py, c)))
                here = self._window(outs, i, 2 * px + py, c)
                forwards.append(functools.partial(_remote, land[i].at[j], here, fssem.at[i, j], frsem.at[i, j], sibling))
                keeps.append(functools.partial(pltpu.make_async_copy, land[i].at[j], here, ksem.at[i, j]))
                there = self._window(outs, i, 2 * px + py, 1 - c)
                passed.append(functools.partial(_remote, there, there, fssem.at[i, j], frsem.at[i, j], sibling))
        return loads, stores, sends, forwards, keeps, passed

    def start(self, ins, outs, scr):
        loads, _, sends, _, _, _ = self._copies(ins, outs, scr)
        for make in sends + loads:
            make().start()

    lead = 2

    def relay(self, ins, outs, scr):
        loads, stores, sends, forwards, keeps, _ = self._copies(ins, outs, scr)
        for load, store in zip(loads, stores):
            load().wait()
            store().start()
        for send, forward, keep in zip(sends, forwards, keeps):
            send().wait_recv()
            forward().start()
            keep().start()

    def finish(self, ins, outs, scr):
        _, stores, sends, forwards, keeps, passed = self._copies(ins, outs, scr)
        for make in passed:
            make().wait_recv()
        for make in sends + forwards:
            make().wait_send()
        for make in keeps + stores:
            make().wait()


class _ChipsHook:
    def __init__(self, parts, kinds):
        self.ins, self.kinds, n = list(parts), list(kinds), len(parts)
        self.out_shape = tuple(
            S((4, p.shape[0], p.shape[1] // 4), BF) if k == "col" else S(p.shape, BF) for p, k in zip(parts, kinds))
        dma = pltpu.SemaphoreType.DMA
        self.scratch = ([dma((n, 3)), dma((n, 3)), dma((n,)), dma((n,))]
                        + [pltpu.VMEM(o.shape[1:], BF) for o in self.out_shape])

    def _slab(self, ins, i, s):
        _, rows, cols = self.out_shape[i].shape
        if self.kinds[i] == "col":
            return ins[i].at[:, pl.ds(pl.multiple_of(s * cols, 128), cols)]
        return ins[i].at[s]

    def _copies(self, ins, outs, scr):
        ssem, rsem, lsem, osem = scr[:4]
        own = scr[4:]
        x, y, c = _place()
        loads, stores, sends = [], [], []
        for i in range(len(ins)):
            loads.append(functools.partial(pltpu.make_async_copy, self._slab(ins, i, 2 * x + y), own[i], lsem.at[i]))
            stores.append(functools.partial(pltpu.make_async_copy, own[i], outs[i].at[3], osem.at[i]))
            for j, (fx, fy) in enumerate(OTHER_CHIPS):
                px, py = _flip(x, fx), _flip(y, fy)
                sends.append(functools.partial(_remote, self._slab(ins, i, 2 * px + py), outs[i].at[j], ssem.at[i, j],
                                               rsem.at[i, j], (px, py, c)))
        return loads, stores, sends

    def start(self, ins, outs, scr):
        loads, _, sends = self._copies(ins, outs, scr)
        for make in sends + loads:
            make().start()

    lead = 2

    def relay(self, ins, outs, scr):
        loads, stores, _ = self._copies(ins, outs, scr)
        for load, store in zip(loads, stores):
            load().wait()
            store().start()

    def finish(self, ins, outs, scr):
        _, stores, sends = self._copies(ins, outs, scr)
        for make in sends + stores:
            make().wait()


SHARE_STEPS = 2


def _sum_share(slab_list, name, hook=None):
    n = len(slab_list)
    geom = [(sl.shape[1], sl.shape[1] // SHARE_STEPS, sl.shape[2]) for sl in slab_list]

    def body(*refs):
        ins, outs, scr = refs[:n], refs[n:2 * n], refs[2 * n:]
        i = pl.program_id(0)
        x, y, c = _place()
        sibling = (x, y, 1 - c)

        def copies(q, k):
            rh, tr, _ = geom[q]
            stage, lsem, ssem, rsem = scr[4 * q:4 * q + 4]
            dst = outs[q].at[pl.ds(pl.multiple_of(c * rh + k * tr, 8), tr), :]
            return (pltpu.make_async_copy(stage.at[k], dst, lsem.at[k]),
                    _remote(stage.at[k], dst, ssem.at[k], rsem, sibling))

        for q in range(n):
            acc = ins[q][3].astype(F32)
            for k in range(3):
                acc = acc + ins[q][k].astype(F32)
            scr[4 * q][i] = acc
            for cp in copies(q, i):
                cp.start()

        @pl.when(i == SHARE_STEPS - 1)
        def _():
            for q in range(n):
                rh = geom[q][0]
                for k in range(SHARE_STEPS):
                    local, remote = copies(q, k)
                    local.wait()
                    remote.wait_send()
                got = outs[q].at[pl.ds(pl.multiple_of((1 - c) * rh, 8), rh), :]
                _remote(got, got, scr[4 * q + 2].at[0], scr[4 * q + 3], sibling).wait_recv()

    dma = pltpu.SemaphoreType.DMA
    scratch = []
    for rh, tr, cols in geom:
        scratch += [pltpu.VMEM((SHARE_STEPS, tr, cols), F32), dma((SHARE_STEPS,)), dma((SHARE_STEPS,)), dma]
    return _call(
        body, name=name, grid=(SHARE_STEPS,), out_shape=tuple(S((2 * rh, cols), F32) for rh, _, cols in geom),
        in_specs=[pl.BlockSpec((4, tr, cols), lambda i: (0, i, 0)) for _, tr, cols in geom], out_specs=(ANY,) * n,
        scratch_shapes=scratch, sem=("arbitrary",), args=list(slab_list), hook=hook)


class _SmallSumHook:
    def __init__(self, buf):
        self.ins, self.out_shape = [buf], (S(buf.shape, F32),)
        dma = pltpu.SemaphoreType.DMA
        self.scratch = [pltpu.VMEM((8,) + buf.shape, F32), pltpu.VMEM(buf.shape, F32), dma((7,)), dma((7,)), dma]

    def _sends(self, scr):
        slots, _, ssem, rsem, _ = scr
        x, y, c = _place()
        me = 4 * x + 2 * y + c
        for r in range(1, 8):
            px, py, pc = _flip(x, (r >> 2) & 1), _flip(y, (r >> 1) & 1), _flip(c, r & 1)
            yield (functools.partial(_remote, slots.at[me], slots.at[me], ssem.at[r - 1], rsem.at[r - 1], (px, py, pc)),
                   functools.partial(_remote, slots.at[me], slots.at[4 * px + 2 * py + pc], ssem.at[r - 1],
                                     rsem.at[r - 1], (px, py, pc)))

    def start(self, ins, outs, scr):
        slots, _, _, _, lsem = scr
        x, y, c = _place()
        load = pltpu.make_async_copy(ins[0], slots.at[4 * x + 2 * y + c], lsem)
        load.start()
        load.wait()
        for send, _ in self._sends(scr):
            send().start()

    lead = 0

    def relay(self, ins, outs, scr):
        pass

    def finish(self, ins, outs, scr):
        slots, total, _, _, lsem = scr
        for _, arrival in self._sends(scr):
            arrival().wait_recv()
        for send, _ in self._sends(scr):
            send().wait_send()
        acc = slots[0]
        for k in range(1, 8):
            acc = acc + slots[k]
        total[...] = acc
        store = pltpu.make_async_copy(total, outs[0], lsem)
        store.start()
        store.wait()


BIG = ("ffn1_w_gu", "ffn1_w_down", "w_in", "w_out", "xattn_wq", "xattn_wkv", "xattn_wo", "ffn2_w_gu", "ffn2_w_down")
KIND = {"ffn1_w_gu": "col", "ffn1_w_down": "row", "w_in": "col", "w_out": "row", "xattn_wq": "row",
        "xattn_wkv": "col", "xattn_wo": "row", "ffn2_w_gu": "col", "ffn2_w_down": "row"}
WEIGHTS = ("rel_bias", "ffn1_norm", "ffn1_w_gu", "ffn1_w_down", "mix_norm", "w_in", "sinks", "conv_w", "w_out",
           "xattn_norm", "mem_norm", "xattn_wq", "xattn_wkv", "xattn_wo", "ffn2_norm", "ffn2_w_gu", "ffn2_w_down",
           "final_norm")
SMALL_ROWS = 16
GAIN_ROW = {"ffn1_norm": 0, "mix_norm": 1, "xattn_norm": 2, "mem_norm": 3, "ffn2_norm": 4, "final_norm": 5}
CONV_ROW, SINK_ROW, BIAS_ROW, LOSS_ROW = 6, 9, 10, 11
TAP_ROWS = 8


def _rows_block(rows, d):
    buf = jnp.zeros((SMALL_ROWS, d), F32)
    for r, v in rows.items():
        buf = lax.dynamic_update_slice(buf, v if v.ndim == 2 else v.reshape(1, -1), (r, 0))
    return buf


def _adamw_small(gsum, conv_g, bias_g, small):
    names = list(small)

    def grad(k, gsum_ref, conv_ref, bias_ref):
        if k in GAIN_ROW:
            return gsum_ref[GAIN_ROW[k]:GAIN_ROW[k] + 1, :]
        if k == "sinks":
            return gsum_ref[SINK_ROW:SINK_ROW + 1, 0:small[k][0].shape[1]]
        return conv_ref[...] if k == "conv_w" else bias_ref[...]

    def body(gsum_ref, conv_ref, bias_ref, *refs):
        ins, outs = refs[:3 * len(names)], refs[3 * len(names):]
        for q, k in enumerate(names):
            w_ref, m_ref, v_ref = ins[3 * q:3 * q + 3]
            g_ref, d_ref, mo_ref, vo_ref = outs[4 * q:4 * q + 4]
            gv = grad(k, gsum_ref, conv_ref, bias_ref)
            g_ref[...] = gv
            d_ref[...], mo_ref[...], vo_ref[...] = _adam_update(w_ref[...], gv, m_ref[...], v_ref[...])

    res = pl.pallas_call(
        body, name="adamw_small", out_shape=tuple(S(small[k][0].shape, F32) for k in names for _ in range(4)),
        compiler_params=_params())(gsum, conv_g, bias_g, *[a for k in names for a in small[k]])
    return {k: res[4 * q:4 * q + 4] for q, k in enumerate(names)}


def _local_step(x, mem, pos, target, w, gains, rel_bias, sinks, conv_w, shards=None):
    t, d = x.shape
    dist = shards is not None
    w = dict(w)
    grads, slabs = {}, {}
    pos_col = pos.reshape(t, 1)
    pos_row = pos.reshape(1, t)
    bias_t = _bias_build(rel_bias).reshape(SWA_KV_HEADS, GROUP_ROWS, 2 * BLOCK)

    def gather(names):
        return _GatherHook([shards[k] for k in names], [KIND[k] for k in names]) if dist else None

    def gathered(names, hook):
        if dist:
            for k, gw in zip(names, hook.results):
                w[k] = gw if KIND[k] == "col" else gw.reshape(-1, gw.shape[-1])

    def dw(problems, tn, name, hook=None):
        if dist:
            res = _dw_pair([(a, b) for _, a, b in problems], tn, name, KIND[problems[0][0]], hook)
        else:
            res = [_dw(a, b, tn, name + "_" + k) for k, a, b in problems]
        grads.update(zip((k for k, _, _ in problems), res))

    def pair_sums(names):
        if not dist:
            return None
        flat = lambda k, v: v.reshape(-1, v.shape[-1]) if KIND[k] == "row" else v
        sums = _add_bf16([(flat(k, grads[k][0]), flat(k, grads[k][1])) for k in names], "pair_sum_" + names[0])
        return {k: (p if KIND[k] == "col" else p.reshape(4, -1, p.shape[-1])) for k, p in zip(names, sums)}

    def chips(names, parts):
        return _ChipsHook([parts[k] for k in names], [KIND[k] for k in names]) if dist else None

    def reduced(names, hook):
        if dist:
            slabs.update(zip(names, hook.results))

    names = ("w_in", "w_out")
    hook = gather(names)
    x1, g1, u1 = _ffn_fwd(x, gains["ffn1_norm"], w["ffn1_w_gu"], w["ffn1_w_down"], "ffn1_fwd", hook)
    gathered(names, hook)
    names = ("xattn_wq", "xattn_wkv", "xattn_wo")
    hook = gather(names)
    qkv, e = _mix_proj(x1, gains["mix_norm"], w["w_in"], hook)
    gathered(names, hook)
    names = ("ffn2_w_gu",)
    hook = gather(names)
    attn = _swa_fwd(qkv, pos_col, pos_row, bias_t, sinks, hook)
    gathered(names, hook)
    names = ("ffn2_w_down",)
    hook = gather(names)
    x2, merged = _mix_out_fwd(e, attn, conv_w, w["w_out"], x1, hook)
    gathered(names, hook)
    mh, kv = _mem_kv(mem, gains["mem_norm"], w["xattn_wkv"])
    x3, qx, o = _xattn_fwd(x2, gains["xattn_norm"], w["xattn_wq"], kv, w["xattn_wo"])
    dx4, g2, u2, d_final, loss = _ffn_fwd(x3, gains["ffn2_norm"], w["ffn2_w_gu"], w["ffn2_w_down"], "ffn2_fwd",
                                          head=(gains["final_norm"], target))

    dx3, d_ffn2, dgu2, a2, h4, dyh4 = _ffn_bwd(dx4, x3, gains["ffn2_norm"], g2, u2, w["ffn2_w_gu"], w["ffn2_w_down"],
                                               "ffn2_bwd")
    dw([("ffn2_w_gu", h4, dgu2)], 1408, "dw_ffn2_gu")
    dw([("ffn2_w_down", a2, dyh4)], 512, "dw_ffn2_down")
    parts = pair_sums(("ffn2_w_gu", "ffn2_w_down"))
    hook = chips(("ffn2_w_gu",), parts)
    dx2, d_xattn, dkv, dxh3, h3, dqx = _xattn_bwd(dx3, x2, gains["xattn_norm"], qx, kv, w["xattn_wq"], w["xattn_wo"], hook)
    reduced(("ffn2_w_gu",), hook)
    grads["xattn_wkv"], d_mem = _mem_bwd(dkv, mh, mem, gains["mem_norm"], w["xattn_wkv"], dist)
    hook = chips(("ffn2_w_down",), parts)
    dattn, dp, dcw, dxh2 = _mix_out_bwd(dx2, e, attn, conv_w, w["w_out"], qkv.shape[1], hook)
    reduced(("ffn2_w_down",), hook)
    dw([("xattn_wo", o, dxh3), ("xattn_wq", h3, dqx), ("w_out", merged, dxh2)], 1024, "dw_wo_wq_wout")
    names = ("xattn_wo", "xattn_wq", "xattn_wkv", "w_out")
    hook = chips(names, pair_sums(names))
    dp, gb, dsk = _swa_bwd(qkv, dattn, pos_col, pos_row, bias_t, sinks, dp, hook)
    reduced(names, hook)
    d_rel_bias, d_sinks = _bias_reduce(gb.reshape(SWA_HEADS, BLOCK, 2 * BLOCK), dsk.reshape(SWA_HEADS, BLOCK, 1))
    dx1, d_mix, h2 = _mix_in_bwd(dp, x1, gains["mix_norm"], w["w_in"], dx2)
    dw([("w_in", h2, dp)], w["w_in"].shape[1] // 4, "dw_win")
    names = ("w_in",)
    hook = chips(names, pair_sums(names))
    dx0, d_ffn1, dgu1, a1, h1, dyh1 = _ffn_bwd(dx1, x, gains["ffn1_norm"], g1, u1, w["ffn1_w_gu"], w["ffn1_w_down"],
                                               "ffn1_bwd", hook)
    reduced(names, hook)
    dw([("ffn1_w_gu", h1, dgu1)], 1408, "dw_ffn1_gu")
    names = ("ffn1_w_gu",)
    hook = chips(names, pair_sums(names))
    dw([("ffn1_w_down", a1, dyh1)], 512, "dw_ffn1_down", hook)
    reduced(names, hook)
    rows = {0: d_ffn1, 1: d_mix, 2: d_xattn, 3: d_mem, 4: d_ffn2, 5: d_final, SINK_ROW: d_sinks,
            BIAS_ROW: d_rel_bias.reshape(1, -1), LOSS_ROW: loss[0, 0:1]}
    rows.update({CONV_ROW + j: dcw[j] for j in range(3)})
    last = ("ffn1_w_down",)
    return dx0, (slabs if dist else grads), _rows_block(rows, d), chips(last, pair_sums(last))


def kernel(x, mem, positions, rel_bias, ffn1_norm, ffn1_w_gu, ffn1_w_down, mix_norm, w_in, sinks, conv_w, w_out, xattn_norm, mem_norm, xattn_wq, xattn_wkv, xattn_wo, ffn2_norm, ffn2_w_gu, ffn2_w_down, final_norm, loss_target, m_rel_bias, m_ffn1_norm, m_ffn1_w_gu, m_ffn1_w_down, m_mix_norm, m_w_in, m_sinks, m_conv_w, m_w_out, m_xattn_norm, m_mem_norm, m_xattn_wq, m_xattn_wkv, m_xattn_wo, m_ffn2_norm, m_ffn2_w_gu, m_ffn2_w_down, m_final_norm, v_rel_bias, v_ffn1_norm, v_ffn1_w_gu, v_ffn1_w_down, v_mix_norm, v_w_in, v_sinks, v_conv_w, v_w_out, v_xattn_norm, v_mem_norm, v_xattn_wq, v_xattn_wkv, v_xattn_wo, v_ffn2_norm, v_ffn2_w_gu, v_ffn2_w_down, v_final_norm):
    args = dict(locals())
    wts = {k: args[k] for k in WEIGHTS}
    mom = {k: args["m_" + k] for k in WEIGHTS}
    var = {k: args["v_" + k] for k in WEIGHTS}
    d = x.shape[-1]
    s_me = 2 * lax.axis_index("x") + lax.axis_index("y")

    first = ("ffn1_w_gu", "ffn1_w_down")
    rest = tuple(k for k in BIG if k not in first)
    shards = {k: wts[k][0].astype(BF) for k in first}
    cw_cols = conv_w.shape[-1]
    placed = lax.dynamic_update_slice(jnp.zeros((TAP_ROWS, d), F32), 0.5 * conv_w[0], (0, s_me * cw_cols))
    head = _Both([_GatherHook([shards[k] for k in first], [KIND[k] for k in first]), _SmallSumHook(placed)])
    shards.update(zip(rest, _to_bf16([wts[k][0] for k in rest], "gather_ffn1", head)))
    gathered, (conv_sum,) = head.results
    whole = {k: (gw if KIND[k] == "col" else gw.reshape(-1, gw.shape[-1])) for k, gw in zip(first, gathered)}
    conv_whole = conv_sum[0:3]

    gains = {k: wts[k].reshape(1, d) for k in GAIN_ROW}
    dx0, slabs, small, last_chips = _local_step(x[0], mem[0], positions[0], loss_target[0], whole, gains, rel_bias, sinks,
                                               conv_whole, shards)

    late = ("ffn1_w_gu", "ffn1_w_down", "ffn2_w_down")
    early = tuple(k for k in BIG if k not in late)
    first_ones = tuple(k for k in BIG if k != "ffn1_w_down")
    tail = _Both([last_chips, _SmallSumHook(small)])
    shard_grads = dict(zip(first_ones, _sum_share([slabs[k] for k in first_ones], "sum_share", tail)))
    (last_slabs,), (small_sum,) = tail.results
    shard_grads["ffn1_w_down"], = _sum_share([last_slabs], "sum_share_ffn1_w_down")
    quad = lambda k: (wts[k][0], shard_grads[k], mom[k][0], var[k][0])
    updates = dict(zip(early, _adamw([quad(k) for k in early], "adamw_early", 16, echo=True)))
    updates.update(zip(late, _adamw([quad(k) for k in late], "adamw_late", 8, echo=True)))
    loss = small_sum[LOSS_ROW, 0]

    out_g, out_d, out_m, out_v = {}, {}, {}, {}
    for k in BIG:
        out_g[k], out_d[k], out_m[k], out_v[k] = (a[None] for a in updates[k])

    conv_g = lax.dynamic_slice(small_sum, (CONV_ROW, s_me * cw_cols), (3, cw_cols))
    two_d = lambda a: a.reshape(a.shape[-2:]) if a.ndim > 1 else a.reshape(1, -1)
    small_names = [k for k in WEIGHTS if k not in KIND]
    bias_g = small_sum[BIAS_ROW, 0:rel_bias.size].reshape(rel_bias.shape)
    done = _adamw_small(small_sum, conv_g, bias_g, {k: (two_d(wts[k]), two_d(mom[k]), two_d(var[k])) for k in small_names})
    for k in small_names:
        out_g[k], out_d[k], out_m[k], out_v[k] = (a.reshape(wts[k].shape) for a in done[k])

    return (loss, dx0[None], *[out_g[k] for k in WEIGHTS], *[out_d[k] for k in WEIGHTS],
            *[out_m[k] for k in WEIGHTS], *[out_v[k] for k in WEIGHTS])
```

```python
import functools
import math

import jax
import jax.numpy as jnp
from jax import lax
from jax.experimental import pallas as pl
from jax.experimental.pallas import tpu as pltpu

BF = jnp.bfloat16
F32 = jnp.float32
I32 = jnp.int32
S = jax.ShapeDtypeStruct

EPS = 1e-6
NEG = -1e30
POS_PAD = 1 << 30
WINDOW = 128
BLOCK = 128
HEAD_DIM = 64
SWA_HEADS = 16
SWA_KV_HEADS = 4
SWA_GROUP = SWA_HEADS // SWA_KV_HEADS
MEM_HEADS = 4
REL_BUCKETS = 32
REL_MAX_DIST = 128
ADAM_LR = 0.001
ADAM_B1 = 0.9
ADAM_B2 = 0.999
ADAM_EPS = 1e-08
ADAM_WD = 0.01
ADAM_STEP = 10

V7X_VMEM_LIMIT_BYTES = 56 * 1024 * 1024
MESH = pl.DeviceIdType.MESH
ANY = pl.BlockSpec(memory_space=pl.ANY)
VMEM_SPEC = pl.BlockSpec(memory_space=pltpu.VMEM)
SMEM_SPEC = pl.BlockSpec(memory_space=pltpu.SMEM)


def _params(sem=None):
    return pltpu.CompilerParams(dimension_semantics=sem, vmem_limit_bytes=V7X_VMEM_LIMIT_BYTES)


def _resident(shape):
    nd = len(shape)
    return pl.BlockSpec(shape, lambda *_: (0,) * nd, pipeline_mode=pl.Buffered(1))


def _acc_spec(shape):
    nd = len(shape)
    return pl.BlockSpec(shape, lambda *_: (0,) * nd)


def _call(body, *, name, grid, out_shape, in_specs, out_specs, args, sem, scratch_shapes=(), aliases=None, hook=None):
    aliases = aliases or {}
    if hook is None:
        return pl.pallas_call(body, name=name, grid=grid, out_shape=out_shape, in_specs=in_specs, out_specs=out_specs,
                              scratch_shapes=list(scratch_shapes), input_output_aliases=aliases,
                              compiler_params=_params(sem))(*args)
    n_in, n_out, n_scr = len(in_specs), len(out_shape), len(scratch_shapes)
    h_in, h_out = len(hook.ins), len(hook.out_shape)

    def at_step(pick):
        conds = [pl.program_id(ax) == pick(size) for ax, size in enumerate(grid)]
        return functools.reduce(jnp.logical_and, conds)

    def hosted(*refs):
        k_in, x_in = refs[:n_in], refs[n_in:n_in + h_in]
        o0 = n_in + h_in
        k_out, x_out = refs[o0:o0 + n_out], refs[o0 + n_out:o0 + n_out + h_out]
        s0 = o0 + n_out + h_out
        k_scr, x_scr = refs[s0:s0 + n_scr], refs[s0 + n_scr:]

        @pl.when(at_step(lambda size: 0))
        def _():
            hook.start(x_in, x_out, x_scr)
        body(*k_in, *k_out, *k_scr)
        early = bool(hook.lead) and grid[0] > hook.lead
        if early:
            on_axis0 = pl.program_id(0) == grid[0] - 1 - hook.lead
            rest_first = [pl.program_id(ax) == 0 for ax in range(1, len(grid))]

            @pl.when(functools.reduce(jnp.logical_and, rest_first, on_axis0))
            def _():
                hook.relay(x_in, x_out, x_scr)

        @pl.when(at_step(lambda size: size - 1))
        def _():
            if not early:
                hook.relay(x_in, x_out, x_scr)
            hook.finish(x_in, x_out, x_scr)

    res = pl.pallas_call(
        hosted, name=name, grid=grid, out_shape=tuple(out_shape) + tuple(hook.out_shape),
        in_specs=list(in_specs) + [ANY] * h_in, out_specs=tuple(out_specs) + (ANY,) * h_out,
        scratch_shapes=list(scratch_shapes) + list(hook.scratch), input_output_aliases=aliases,
        compiler_params=_params(("arbitrary",) * len(grid)),
    )(*args, *hook.ins)
    hook.results = res[n_out:]
    return res[:n_out]


def _nn(a, b):
    return jnp.dot(a, b, preferred_element_type=F32)


def _nt(a, b):
    return lax.dot_general(a, b, (((1,), (1,)), ((), ())), preferred_element_type=F32)


def _tn(a, b):
    return lax.dot_general(a, b, (((0,), (0,)), ((), ())), preferred_element_type=F32)


def _sigmoid(v):
    return 1.0 / (1.0 + jnp.exp(-v))


def _rms(x):
    r = lax.rsqrt(jnp.mean(x * x, axis=-1, keepdims=True) + EPS)
    return x * r, r


def _rms_bwd(dh, n, r, g):
    dn = dh * g
    dx = r * (dn - n * jnp.mean(dn * n, axis=-1, keepdims=True))
    return dx, jnp.sum(dh * n, axis=0, keepdims=True)


def _ffn_fwd(x, gn, wgu, wd, name, hook=None, head=None):
    t, d = x.shape
    f = wd.shape[0]
    tm, fc = 256, 1408

    def body(x_ref, gn_ref, wgu_ref, wd_ref, *rest):
        xv = x_ref[...]
        n, _ = _rms(xv)
        h = (n * gn_ref[...]).astype(BF)
        g_ref, u_ref = rest[-2:] if head is None else rest[3:5]
        acc = jnp.zeros((tm, d), F32)
        for c0 in range(0, f, fc):
            g = _nn(h, wgu_ref[:, c0:c0 + fc])
            u = _nn(h, wgu_ref[:, f + c0:f + c0 + fc])
            g_ref[:, c0:c0 + fc] = g.astype(BF)
            u_ref[:, c0:c0 + fc] = u.astype(BF)
            a = (g * _sigmoid(g)) * u
            acc = acc + _nn(a.astype(BF), wd_ref[c0:c0 + fc, :])
        y = xv + 0.5 * acc
        if head is None:
            rest[0][...] = y
            return
        gf_ref, t_ref, dy_ref, _, _, dgf_ref, loss_ref = rest

        @pl.when(pl.program_id(0) == 0)
        def _():
            dgf_ref[...] = jnp.zeros_like(dgf_ref)
            loss_ref[...] = jnp.zeros_like(loss_ref)
        ny, ry = _rms(y)
        gf = gf_ref[...]
        err = ny * gf - t_ref[...]
        loss_ref[...] += 0.5 * jnp.sum(jnp.sum(err * err, axis=-1, keepdims=True) / d, axis=0, keepdims=True)
        dy, dgf = _rms_bwd(err / d, ny, ry, gf)
        dy_ref[...] = dy
        dgf_ref[...] += dgf

    row = pl.BlockSpec((tm, d), lambda i: (i, 0))
    frow = pl.BlockSpec((tm, f), lambda i: (i, 0))
    in_specs = [row, _resident((1, d)), _resident(wgu.shape), _resident(wd.shape)]
    out_shape = (S((t, d), F32), S((t, f), BF), S((t, f), BF))
    out_specs = (row, frow, frow)
    args = (x, gn, wgu, wd)
    if head is not None:
        in_specs += [_resident((1, d)), row]
        out_shape += (S((1, d), F32), S((1, 128), F32))
        out_specs += (_acc_spec((1, d)), _acc_spec((1, 128)))
        args += tuple(head)
    return _call(body, name=name, grid=(t // tm,), out_shape=out_shape, in_specs=in_specs, out_specs=out_specs,
                 sem=("parallel",) if head is None else ("arbitrary",), args=args, hook=hook)


def _mix_proj(x, gn, w_in, hook=None):
    t, d = x.shape
    tm = 256
    nqkv = 1536
    ne = w_in.shape[1] - nqkv

    def body(x_ref, gn_ref, w_ref, qkv_ref, e_ref):
        n, _ = _rms(x_ref[...])
        h = (n * gn_ref[...]).astype(BF)
        qkv_ref[...] = _nn(h, w_ref[:, 0:nqkv]).astype(BF)
        for c0 in range(0, ne, 1024):
            e_ref[:, c0:c0 + 1024] = _nn(h, w_ref[:, nqkv + c0:nqkv + c0 + 1024]).astype(BF)

    return _call(
        body, name="mix_proj", grid=(t // tm,),
        out_shape=(S((t, nqkv), BF), S((t, ne), BF)),
        in_specs=[pl.BlockSpec((tm, d), lambda i: (i, 0)), _resident((1, d)), _resident(w_in.shape)],
        out_specs=(pl.BlockSpec((tm, nqkv), lambda i: (i, 0)), pl.BlockSpec((tm, ne), lambda i: (i, 0))),
        sem=("parallel",), args=(x, gn, w_in), hook=hook)


def _t5_bucket(rel):
    n = jnp.maximum(rel, 0)
    max_exact = REL_BUCKETS // 2
    nf = jnp.maximum(n, 1).astype(F32)
    large = max_exact + (jnp.log(nf / max_exact) / math.log(REL_MAX_DIST / max_exact)
                         * (REL_BUCKETS - max_exact)).astype(I32)
    large = jnp.minimum(large, REL_BUCKETS - 1)
    return jnp.where(n < max_exact, n, large)


def _block_rel():
    i = lax.broadcasted_iota(I32, (BLOCK, 2 * BLOCK), 0)
    j = lax.broadcasted_iota(I32, (BLOCK, 2 * BLOCK), 1)
    return i + BLOCK - j


def _bias_build(rel_bias):
    def body(rb_ref, o_ref):
        bucket = _t5_bucket(_block_rel())
        for h in range(SWA_HEADS):
            acc = jnp.zeros((BLOCK, 2 * BLOCK), F32)
            for b in range(REL_BUCKETS):
                acc = jnp.where(bucket == b, rb_ref[b, h], acc)
            o_ref[h] = acc

    return pl.pallas_call(
        body, name="bias_build", out_shape=S((SWA_HEADS, BLOCK, 2 * BLOCK), F32),
        in_specs=[SMEM_SPEC], out_specs=VMEM_SPEC,
    )(rel_bias)


GROUP_ROWS = SWA_GROUP * BLOCK


def _swa_visible(b, pq_ref, pkp_ref, pkc_ref):
    pk = jnp.concatenate([pkp_ref[...], pkc_ref[...]], axis=1)
    col = lax.broadcasted_iota(I32, (1, 2 * BLOCK), 1)
    pk = jnp.where(jnp.logical_and(b == 0, col < BLOCK), POS_PAD, pk)
    rel = jnp.concatenate([pq_ref[...]] * SWA_GROUP, axis=0) - pk
    return jnp.logical_and(rel >= 0, rel < WINDOW)


def _group_heads(ref, hk):
    h0 = hk * SWA_GROUP
    return jnp.concatenate([ref[:, (h0 + g) * HEAD_DIM:(h0 + g + 1) * HEAD_DIM] for g in range(SWA_GROUP)], axis=0)


def _group_sinks(sink_ref, hk):
    row = lax.broadcasted_iota(I32, (GROUP_ROWS, 1), 0)
    col = jnp.zeros((GROUP_ROWS, 1), F32) + sink_ref[0, hk * SWA_GROUP]
    for g in range(1, SWA_GROUP):
        col = jnp.where(row >= g * BLOCK, sink_ref[0, hk * SWA_GROUP + g], col)
    return col


def _swa_probs(qg, kh, vis, bias, sink):
    s = _nt(qg, kh) * (HEAD_DIM ** -0.5)
    s = jnp.where(vis, s + bias, NEG)
    m = jnp.maximum(jnp.max(s, axis=-1, keepdims=True), sink)
    p = jnp.exp(s - m)
    ps = jnp.exp(sink - m)
    inv = 1.0 / (jnp.sum(p, axis=-1, keepdims=True) + ps)
    return p * inv, ps * inv


def _swa_fwd(qkv, pos_col, pos_row, bias_t, sinks, hook=None):
    t = qkv.shape[0]
    nb = t // BLOCK
    qw = SWA_HEADS * HEAD_DIM
    kw = SWA_KV_HEADS * HEAD_DIM

    def body(q_ref, kp_ref, kc_ref, vp_ref, vc_ref, pq_ref, pkp_ref, pkc_ref, bias_ref, sink_ref, o_ref):
        b = pl.program_id(0)
        vis = _swa_visible(b, pq_ref, pkp_ref, pkc_ref)
        k2 = jnp.concatenate([kp_ref[...], kc_ref[...]], axis=0)
        v2 = jnp.concatenate([vp_ref[...], vc_ref[...]], axis=0)
        for hk in range(SWA_KV_HEADS):
            kh = k2[:, hk * HEAD_DIM:(hk + 1) * HEAD_DIM]
            vh = v2[:, hk * HEAD_DIM:(hk + 1) * HEAD_DIM]
            pn, _ = _swa_probs(_group_heads(q_ref, hk), kh, vis, bias_ref[hk], _group_sinks(sink_ref, hk))
            o = _nn(pn.astype(BF), vh)
            for g in range(SWA_GROUP):
                h = hk * SWA_GROUP + g
                o_ref[:, h * HEAD_DIM:(h + 1) * HEAD_DIM] = o[g * BLOCK:(g + 1) * BLOCK]

    prev = lambda b: jnp.maximum(b - 1, 0)
    return _call(
        body, name="swa_fwd", grid=(nb,), out_shape=(S((t, qw), F32),),
        in_specs=[
            pl.BlockSpec((BLOCK, qw), lambda b: (b, 0)),
            pl.BlockSpec((BLOCK, kw), lambda b: (prev(b), qw // kw)),
            pl.BlockSpec((BLOCK, kw), lambda b: (b, qw // kw)),
            pl.BlockSpec((BLOCK, kw), lambda b: (prev(b), qw // kw + 1)),
            pl.BlockSpec((BLOCK, kw), lambda b: (b, qw // kw + 1)),
            pl.BlockSpec((BLOCK, 1), lambda b: (b, 0)),
            pl.BlockSpec((1, BLOCK), lambda b: (0, prev(b))),
            pl.BlockSpec((1, BLOCK), lambda b: (0, b)),
            _resident(bias_t.shape),
            SMEM_SPEC,
        ],
        out_specs=(pl.BlockSpec((BLOCK, qw), lambda b: (b, 0)),),
        sem=("parallel",), args=(qkv, qkv, qkv, qkv, qkv, pos_col, pos_row, pos_row, bias_t, sinks), hook=hook)[0]


HALO = 16


def _conv_taps(z, zh, first):
    tm = z.shape[0]
    zh = jnp.where(first, 0.0, zh)
    row = lax.broadcasted_iota(I32, (tm, 1), 0)
    z1 = jnp.where(row == 0, zh[HALO - 1:HALO, :], pltpu.roll(z, 1, 0))
    z2 = jnp.where(row == 0, zh[HALO - 2:HALO - 1, :], jnp.where(row == 1, zh[HALO - 1:HALO, :], pltpu.roll(z, 2, 0)))
    return z1, z2


def _mix_out_fwd(e, attn, conv_w, w_out, x, hook=None):
    t, d = x.shape
    tm = 256
    hb = tm // HALO
    f32 = lambda ref: ref[...].astype(F32)

    def body(c_ref, b_ref, u_ref, ga_ref, gc_ref, ch_ref, uh_ref, attn_ref, cw_ref, w_ref, x_ref, xo_ref, mg_ref):
        i = pl.program_id(0)
        z = f32(c_ref) * f32(u_ref)
        z1, z2 = _conv_taps(z, f32(ch_ref) * f32(uh_ref), i == 0)
        s = cw_ref[0:1, :] * z2 + cw_ref[1:2, :] * z1 + cw_ref[2:3, :] * z
        conv = f32(b_ref) * s
        merged = (_sigmoid(f32(ga_ref)) * attn_ref[...] + _sigmoid(f32(gc_ref)) * conv).astype(BF)
        mg_ref[...] = merged
        xo_ref[...] = x_ref[...] + _nn(merged, w_ref[...])

    ecol = lambda cb: pl.BlockSpec((tm, d), lambda i: (i, cb))
    halo = lambda cb: pl.BlockSpec((HALO, d), lambda i: (jnp.maximum(i * hb - 1, 0), cb))
    row = pl.BlockSpec((tm, d), lambda i: (i, 0))
    return _call(
        body, name="mix_out_fwd", grid=(t // tm,),
        out_shape=(S((t, d), F32), S((t, d), BF)),
        in_specs=[ecol(0), ecol(1), ecol(2), ecol(3), ecol(4), halo(0), halo(2), row,
                  _resident(conv_w.shape), _resident(w_out.shape), row],
        out_specs=(row, row),
        sem=("parallel",), args=(e, e, e, e, e, e, e, attn, conv_w, w_out, x), hook=hook)


def _mem_kv(mem, gm, wkv):
    m, d = mem.shape

    def body(mem_ref, gm_ref, w_ref, mh_ref, kv_ref):
        n, _ = _rms(mem_ref[...])
        mh = (n * gm_ref[...]).astype(BF)
        mh_ref[...] = mh
        kv_ref[...] = _nn(mh, w_ref[...]).astype(BF)

    return pl.pallas_call(
        body, name="mem_kv", out_shape=(S((m, d), BF), S((m, wkv.shape[1]), BF)),
        compiler_params=_params(),
    )(mem, gm, wkv)


def _xattn_probs(qh, kh):
    s = _nt(qh, kh) * (kh.shape[1] ** -0.5)
    p = jnp.exp(s - jnp.max(s, axis=-1, keepdims=True))
    return p * (1.0 / jnp.sum(p, axis=-1, keepdims=True))


def _xattn_fwd(x, gn, wq, kv, wo):
    t, d = x.shape
    tm = 256
    hd = d // MEM_HEADS

    def body(x_ref, gn_ref, wq_ref, kv_ref, wo_ref, xo_ref, q_ref, o_ref):
        xv = x_ref[...]
        n, _ = _rms(xv)
        q = _nn((n * gn_ref[...]).astype(BF), wq_ref[...]).astype(BF)
        q_ref[...] = q
        outs = []
        for hh in range(MEM_HEADS):
            p = _xattn_probs(q[:, hh * hd:(hh + 1) * hd], kv_ref[:, hh * hd:(hh + 1) * hd])
            outs.append(_nn(p.astype(BF), kv_ref[:, d + hh * hd:d + (hh + 1) * hd]))
        o = jnp.concatenate(outs, axis=1).astype(BF)
        o_ref[...] = o
        xo_ref[...] = xv + _nn(o, wo_ref[...])

    row = pl.BlockSpec((tm, d), lambda i: (i, 0))
    return pl.pallas_call(
        body, name="xattn_fwd", grid=(t // tm,),
        out_shape=(S((t, d), F32), S((t, d), BF), S((t, d), BF)),
        in_specs=[row, _resident((1, d)), _resident(wq.shape), _resident(kv.shape), _resident(wo.shape)],
        out_specs=(row, row, row),
        compiler_params=_params(("parallel",)),
    )(x, gn, wq, kv, wo)


def _ffn_bwd(dxo, x, gn, g, u, wgu, wd, name, hook=None):
    t, d = x.shape
    f = wd.shape[0]
    tm, fc = 256, 1408

    def body(dxo_ref, x_ref, gn_ref, g_ref, u_ref, wgu_ref, wd_ref, dx_ref, dgn_ref, dgu_ref, a_ref, h_ref, dyh_ref):
        @pl.when(pl.program_id(0) == 0)
        def _():
            dgn_ref[...] = jnp.zeros_like(dgn_ref)
        dxov = dxo_ref[...]
        dyh = (0.5 * dxov).astype(BF)
        dyh_ref[...] = dyh
        n, r = _rms(x_ref[...])
        gnv = gn_ref[...]
        h_ref[...] = (n * gnv).astype(BF)
        dh = jnp.zeros((tm, d), F32)
        for c0 in range(0, f, fc):
            gv = g_ref[:, c0:c0 + fc].astype(F32)
            uv = u_ref[:, c0:c0 + fc].astype(F32)
            da = _nt(dyh, wd_ref[c0:c0 + fc, :])
            sg = _sigmoid(gv)
            silu = gv * sg
            a_ref[:, c0:c0 + fc] = (silu * uv).astype(BF)
            dg = (da * uv * (sg * (1.0 + gv * (1.0 - sg)))).astype(BF)
            du = (da * silu).astype(BF)
            dgu_ref[:, c0:c0 + fc] = dg
            dgu_ref[:, f + c0:f + c0 + fc] = du
            dh = dh + _nt(dg, wgu_ref[:, c0:c0 + fc]) + _nt(du, wgu_ref[:, f + c0:f + c0 + fc])
        dx, dgn = _rms_bwd(dh, n, r, gnv)
        dx_ref[...] = dxov + dx
        dgn_ref[...] += dgn

    row = pl.BlockSpec((tm, d), lambda i: (i, 0))
    frow = pl.BlockSpec((tm, f), lambda i: (i, 0))
    return _call(
        body, name=name, grid=(t // tm,),
        out_shape=(S((t, d), F32), S((1, d), F32), S((t, 2 * f), BF), S((t, f), BF), S((t, d), BF), S((t, d), BF)),
        in_specs=[row, row, _resident((1, d)), frow, frow, _resident(wgu.shape), _resident(wd.shape)],
        out_specs=(row, _acc_spec((1, d)), pl.BlockSpec((tm, 2 * f), lambda i: (i, 0)), frow, row, row),
        sem=("arbitrary",), args=(dxo, x, gn, g, u, wgu, wd), hook=hook)


def _dw(a, b, tn, name):
    t, ka = a.shape
    nb = b.shape[1]
    tt = 1024
    nt = t // tt

    def body(a_ref, b_ref, o_ref, acc_ref):
        k = pl.program_id(1)

        @pl.when(k == 0)
        def _():
            acc_ref[...] = jnp.zeros_like(acc_ref)
        acc_ref[...] += _tn(a_ref[...], b_ref[...])

        @pl.when(k == nt - 1)
        def _():
            o_ref[...] = acc_ref[...].astype(BF)

    return pl.pallas_call(
        body, name=name, grid=(nb // tn, nt), out_shape=S((ka, nb), BF),
        in_specs=[pl.BlockSpec((tt, ka), lambda j, k: (k, 0)), pl.BlockSpec((tt, tn), lambda j, k: (k, j))],
        out_specs=pl.BlockSpec((ka, tn), lambda j, k: (0, j)), scratch_shapes=[pltpu.VMEM((ka, tn), F32)],
        compiler_params=_params(("parallel", "arbitrary")),
    )(a, b)


def _dw_pair(pairs, tn, name, kind, hook=None):
    npairs = len(pairs)
    t, ka = pairs[0][0].shape
    nb = pairs[0][1].shape[1]
    tt = 2048 if npairs == 1 else 1024
    nt, nj = t // tt, nb // tn
    col = kind == "col"
    rh = ka // 2 if col else ka // 8
    tile = (rh, tn) if col else (4, rh, tn)
    half = (rh, nb) if col else (4, rh, nb)
    lead = (slice(None),) * (len(tile) - 1)

    def body(*refs):
        ins, mines, sibs = refs[:2 * npairs], refs[2 * npairs:3 * npairs], refs[3 * npairs:4 * npairs]
        acc_ref, stage, ssem, rsem = refs[4 * npairs:]
        j, k = pl.program_id(0), pl.program_id(1)
        x, y, c = _place()
        sibling = (x, y, 1 - c)

        def send(p, slot, jj):
            dst = sibs[p].at[lead + (pl.ds(pl.multiple_of(jj * tn, 128), tn),)]
            return _remote(stage.at[slot], dst, ssem.at[slot], rsem.at[p], sibling)

        def rows(s, whose):
            return acc_ref[pl.ds(pl.multiple_of(s * 2 * rh + whose * rh, 16), rh), :].astype(BF)

        def step(p):
            a_ref, b_ref, mine_ref = ins[2 * p], ins[2 * p + 1], mines[p]

            @pl.when(k == 0)
            def _():
                acc_ref[...] = jnp.zeros_like(acc_ref)
            acc_ref[...] += _tn(a_ref[...], b_ref[...])

            @pl.when(k == nt - 1)
            def _():
                slot = j % 2

                @pl.when(j >= 2)
                def _():
                    send(p, slot, 0).wait_send()
                if col:
                    mine_ref[...] = rows(0, c)
                    stage[slot] = rows(0, 1 - c)
                else:
                    for s in range(4):
                        mine_ref[s] = rows(s, c)
                        stage[slot, s] = rows(s, 1 - c)
                send(p, slot, j - p * nj).start()

        for p in range(npairs):
            pl.when(j // nj == p)(functools.partial(step, p))

        @pl.when(jnp.logical_and(j == npairs * nj - 1, k == nt - 1))
        def _():
            for jj in range(max(npairs * nj - 2, 0), npairs * nj):
                send(0, jj % 2, 0).wait_send()
            for p in range(npairs):
                _remote(sibs[p], sibs[p], ssem.at[0], rsem.at[p], sibling).wait_recv()

    in_specs, args = [], []
    for p, (a, b) in enumerate(pairs):
        on = lambda j, p=p: j // nj == p
        in_specs += [pl.BlockSpec((tt, ka), lambda j, k, on=on: (jnp.where(on(j), k, 0), 0)),
                     pl.BlockSpec((tt, tn), lambda j, k, on=on, p=p: (jnp.where(on(j), k, 0), jnp.clip(j - p * nj, 0, nj - 1)))]
        args += [a, b]
    mine_spec = lambda p: pl.BlockSpec(tile, (lambda j, k: (0, jnp.clip(j - p * nj, 0, nj - 1))) if col
                                       else (lambda j, k: (0, 0, jnp.clip(j - p * nj, 0, nj - 1))))
    res = _call(
        body, name=name, grid=(npairs * nj, nt), out_shape=(S(half, BF),) * (2 * npairs),
        in_specs=in_specs, out_specs=tuple(mine_spec(p) for p in range(npairs)) + (ANY,) * npairs,
        scratch_shapes=[pltpu.VMEM((ka, tn), F32), pltpu.VMEM((2,) + tile, BF), pltpu.SemaphoreType.DMA((2,)),
                        pltpu.SemaphoreType.DMA((npairs,))],
        sem=("arbitrary", "arbitrary"), args=args, hook=hook)
    return [(res[p], res[npairs + p]) for p in range(npairs)]


def _xattn_bwd(dxo, x, gn, q, kv, wq, wo, hook=None):
    t, d = x.shape
    tm = 256
    hd = d // MEM_HEADS
    nkv = kv.shape[0]

    def body(dxo_ref, x_ref, gn_ref, q_ref, kv_ref, wq_ref, wo_ref, dx_ref, dgn_ref, dkv_ref, dxh_ref, h_ref, dq_ref):
        @pl.when(pl.program_id(0) == 0)
        def _():
            dgn_ref[...] = jnp.zeros_like(dgn_ref)
            dkv_ref[...] = jnp.zeros_like(dkv_ref)
        dxov = dxo_ref[...]
        dxh = dxov.astype(BF)
        dxh_ref[...] = dxh
        do = _nt(dxh, wo_ref[...]).astype(BF)
        dqs = []
        for hh in range(MEM_HEADS):
            lo, hi = hh * hd, (hh + 1) * hd
            qh = q_ref[:, lo:hi]
            kh = kv_ref[:, lo:hi]
            vh = kv_ref[:, d + lo:d + hi]
            doh = do[:, lo:hi]
            p = _xattn_probs(qh, kh)
            dp = _nt(doh, vh)
            ds = (p * (dp - jnp.sum(p * dp, axis=-1, keepdims=True)) * (hd ** -0.5)).astype(BF)
            dqs.append(_nn(ds, kh))
            dkv_ref[:, lo:hi] += _tn(ds, qh)
            dkv_ref[:, d + lo:d + hi] += _tn(p.astype(BF), doh)
        dq = jnp.concatenate(dqs, axis=1).astype(BF)
        dq_ref[...] = dq
        n, r = _rms(x_ref[...])
        gnv = gn_ref[...]
        h_ref[...] = (n * gnv).astype(BF)
        dx, dgn = _rms_bwd(_nt(dq, wq_ref[...]), n, r, gnv)
        dx_ref[...] = dxov + dx
        dgn_ref[...] += dgn

    row = pl.BlockSpec((tm, d), lambda i: (i, 0))
    return _call(
        body, name="xattn_bwd", grid=(t // tm,),
        out_shape=(S((t, d), F32), S((1, d), F32), S((nkv, 2 * d), F32), S((t, d), BF), S((t, d), BF), S((t, d), BF)),
        in_specs=[row, row, _resident((1, d)), row, _resident(kv.shape), _resident(wq.shape), _resident(wo.shape)],
        out_specs=(row, _acc_spec((1, d)), _acc_spec((nkv, 2 * d)), row, row, row),
        sem=("arbitrary",), args=(dxo, x, gn, q, kv, wq, wo), hook=hook)


def _mem_bwd(dkv, mh, mem, gm, wkv, pair=False):
    m, d = mem.shape
    rows, cols = wkv.shape
    rh = rows // 2

    def body(dkv_ref, mh_ref, mem_ref, gm_ref, w_ref, *outs):
        dkvb = dkv_ref[...].astype(BF)
        dw = _tn(mh_ref[...], dkvb).astype(BF)
        dmh = _nt(dkvb, w_ref[...])
        n, _ = _rms(mem_ref[...])
        if not pair:
            dw_ref, dgm_ref = outs
            dw_ref[...] = dw
        else:
            mine_ref, sib_ref, dgm_ref, whole, ssem, rsem = outs
            x, y, c = _place()
            whole[...] = dw
            cp = _remote(whole.at[pl.ds(pl.multiple_of((1 - c) * rh, 16), rh), :], sib_ref, ssem, rsem, (x, y, 1 - c))
            cp.start()
            mine_ref[...] = whole[pl.ds(pl.multiple_of(c * rh, 16), rh), :]
            cp.wait()
        dgm_ref[...] = jnp.sum(dmh * n, axis=0, keepdims=True)

    if not pair:
        return pl.pallas_call(body, name="mem_bwd", out_shape=(S(wkv.shape, BF), S((1, d), F32)),
                              compiler_params=_params())(dkv, mh, mem, gm, wkv)
    mine, sib, dgm = pl.pallas_call(
        body, name="mem_bwd", out_shape=(S((rh, cols), BF), S((rh, cols), BF), S((1, d), F32)),
        in_specs=[VMEM_SPEC] * 5, out_specs=(VMEM_SPEC, ANY, VMEM_SPEC),
        scratch_shapes=[pltpu.VMEM((rows, cols), BF), pltpu.SemaphoreType.DMA, pltpu.SemaphoreType.DMA],
        compiler_params=_params())(dkv, mh, mem, gm, wkv)
    return (mine, sib), dgm


def _mix_out_bwd(dxo, e, attn, conv_w, w_out, nqkv, hook=None):
    t, d = attn.shape
    tm = 256
    nt = t // tm
    f32 = lambda ref: ref[...].astype(F32)

    def body(dxo_ref, dxn_ref, c_ref, b_ref, u_ref, ga_ref, gc_ref, ch_ref, uh_ref, bn_ref, gcn_ref,
             attn_ref, cw_ref, w_ref, dattn_ref, de_ref, dcw_ref, dxh_ref):
        i = pl.program_id(0)

        @pl.when(i == 0)
        def _():
            dcw_ref[...] = jnp.zeros_like(dcw_ref)
        dxh = dxo_ref[...].astype(BF)
        dxh_ref[...] = dxh
        w = w_ref[...]
        dm = _nt(dxh, w)
        dmn = _nt(dxn_ref[...].astype(BF), w)
        cv, bv, uv = f32(c_ref), f32(b_ref), f32(u_ref)
        sga = _sigmoid(f32(ga_ref))
        sgc = _sigmoid(f32(gc_ref))
        z = cv * uv
        z1, z2 = _conv_taps(z, f32(ch_ref) * f32(uh_ref), i == 0)
        w0, w1, w2 = cw_ref[0:1, :], cw_ref[1:2, :], cw_ref[2:3, :]
        s = w0 * z2 + w1 * z1 + w2 * z
        av = attn_ref[...]
        dattn_ref[...] = (dm * sga).astype(BF)
        dconv = dm * sgc
        ds = dconv * bv
        dsn = jnp.where(i == nt - 1, 0.0, dmn * _sigmoid(gcn_ref[0:8, :].astype(F32)) * bn_ref[0:8, :].astype(F32))
        row = lax.broadcasted_iota(I32, (tm, 1), 0)
        dsp1 = jnp.where(row == tm - 1, dsn[0:1, :], pltpu.roll(ds, tm - 1, 0))
        dsp2 = jnp.where(row == tm - 2, dsn[0:1, :], jnp.where(row == tm - 1, dsn[1:2, :], pltpu.roll(ds, tm - 2, 0)))
        dz = w2 * ds + w1 * dsp1 + w0 * dsp2
        de_ref[:, nqkv:nqkv + d] = (dz * uv).astype(BF)
        de_ref[:, nqkv + d:nqkv + 2 * d] = (dconv * s).astype(BF)
        de_ref[:, nqkv + 2 * d:nqkv + 3 * d] = (dz * cv).astype(BF)
        de_ref[:, nqkv + 3 * d:nqkv + 4 * d] = (dm * av * sga * (1.0 - sga)).astype(BF)
        de_ref[:, nqkv + 4 * d:nqkv + 5 * d] = (dm * (bv * s) * sgc * (1.0 - sgc)).astype(BF)
        dcw_ref[0:1, :] += jnp.sum(ds * z2, axis=0, keepdims=True)
        dcw_ref[1:2, :] += jnp.sum(ds * z1, axis=0, keepdims=True)
        dcw_ref[2:3, :] += jnp.sum(ds * z, axis=0, keepdims=True)

    ecol = lambda cb: pl.BlockSpec((tm, d), lambda i: (i, cb))
    prev = lambda cb: pl.BlockSpec((HALO, d), lambda i: (jnp.maximum(i * (tm // HALO) - 1, 0), cb))
    nxt = lambda rows, cb: pl.BlockSpec((rows, d), lambda i: (jnp.minimum((i + 1) * (tm // rows), t // rows - 1), cb))
    row = pl.BlockSpec((tm, d), lambda i: (i, 0))
    return _call(
        body, name="mix_out_bwd", grid=(nt,),
        out_shape=(S((t, d), BF), S((t, nqkv + 5 * d), BF), S((8, d), F32), S((t, d), BF)),
        in_specs=[row, nxt(8, 0), ecol(0), ecol(1), ecol(2), ecol(3), ecol(4), prev(0), prev(2), nxt(HALO, 1), nxt(HALO, 4),
                  row, _resident(conv_w.shape), _resident(w_out.shape)],
        out_specs=(row, pl.BlockSpec((tm, nqkv + 5 * d), lambda i: (i, 0)), _acc_spec((8, d)), row),
        sem=("arbitrary",), args=(dxo, dxo, e, e, e, e, e, e, e, e, e, attn, conv_w, w_out), hook=hook)


def _swa_bwd(qkv, dattn, pos_col, pos_row, bias_t, sinks, dp, hook=None):
    t = qkv.shape[0]
    nb = t // BLOCK
    qw = SWA_HEADS * HEAD_DIM
    kw = SWA_KV_HEADS * HEAD_DIM

    def body(q_ref, kp_ref, kc_ref, vp_ref, vc_ref, do_ref, pq_ref, pkp_ref, pkc_ref, bias_ref, sink_ref,
             dp_in, dp_ref, gb_ref, dsk_ref, carry_ref, dq_ref):
        b = pl.program_id(0)

        @pl.when(b == 0)
        def _():
            gb_ref[...] = jnp.zeros_like(gb_ref)
            dsk_ref[...] = jnp.zeros_like(dsk_ref)
            carry_ref[...] = jnp.zeros_like(carry_ref)
            dq_ref[...] = jnp.zeros_like(dq_ref)
        dp_ref[:, 0:qw] = dq_ref[...]

        @pl.when(b < nb)
        def _():
            vis = _swa_visible(b, pq_ref, pkp_ref, pkc_ref)
            k2 = jnp.concatenate([kp_ref[...], kc_ref[...]], axis=0)
            v2 = jnp.concatenate([vp_ref[...], vc_ref[...]], axis=0)
            for hk in range(SWA_KV_HEADS):
                lo, hi = hk * HEAD_DIM, (hk + 1) * HEAD_DIM
                kh = k2[:, lo:hi]
                vh = v2[:, lo:hi]
                qg = _group_heads(q_ref, hk)
                dog = _group_heads(do_ref, hk)
                pn, psn = _swa_probs(qg, kh, vis, bias_ref[hk], _group_sinks(sink_ref, hk))
                dp = _nt(dog, vh)
                delta = jnp.sum(pn * dp, axis=-1, keepdims=True)
                ds = pn * (dp - delta)
                gb_ref[hk] += ds
                dsk_ref[hk] += -psn * delta
                dsb = (ds * (HEAD_DIM ** -0.5)).astype(BF)
                dqg = _nn(dsb, kh).astype(BF)
                for g in range(SWA_GROUP):
                    h = hk * SWA_GROUP + g
                    dq_ref[:, h * HEAD_DIM:(h + 1) * HEAD_DIM] = dqg[g * BLOCK:(g + 1) * BLOCK]
                dk = _tn(dsb, qg)
                dv = _tn(pn.astype(BF), dog)
                dp_ref[:, qw + lo:qw + hi] = (carry_ref[:, lo:hi] + dk[0:BLOCK]).astype(BF)
                dp_ref[:, qw + kw + lo:qw + kw + hi] = (carry_ref[:, kw + lo:kw + hi] + dv[0:BLOCK]).astype(BF)
                carry_ref[:, lo:hi] = dk[BLOCK:2 * BLOCK]
                carry_ref[:, kw + lo:kw + hi] = dv[BLOCK:2 * BLOCK]

        @pl.when(b == nb)
        def _():
            dp_ref[:, qw:qw + 2 * kw] = carry_ref[...].astype(BF)

    cur = lambda b: jnp.minimum(b, nb - 1)
    prev = lambda b: jnp.maximum(cur(b) - 1, 0)
    return _call(
        body, name="swa_bwd", grid=(nb + 1,),
        out_shape=(S(dp.shape, BF), S((SWA_KV_HEADS, GROUP_ROWS, 2 * BLOCK), F32), S((SWA_KV_HEADS, GROUP_ROWS, 1), F32)),
        in_specs=[
            pl.BlockSpec((BLOCK, qw), lambda b: (cur(b), 0)),
            pl.BlockSpec((BLOCK, kw), lambda b: (prev(b), qw // kw)),
            pl.BlockSpec((BLOCK, kw), lambda b: (cur(b), qw // kw)),
            pl.BlockSpec((BLOCK, kw), lambda b: (prev(b), qw // kw + 1)),
            pl.BlockSpec((BLOCK, kw), lambda b: (cur(b), qw // kw + 1)),
            pl.BlockSpec((BLOCK, qw), lambda b: (cur(b), 0)),
            pl.BlockSpec((BLOCK, 1), lambda b: (cur(b), 0)),
            pl.BlockSpec((1, BLOCK), lambda b: (0, prev(b))),
            pl.BlockSpec((1, BLOCK), lambda b: (0, cur(b))),
            _resident(bias_t.shape),
            SMEM_SPEC,
            ANY,
        ],
        out_specs=(
            pl.BlockSpec((BLOCK, qw + 2 * kw), lambda b: (jnp.maximum(b - 1, 0), 0)),
            _acc_spec((SWA_KV_HEADS, GROUP_ROWS, 2 * BLOCK)),
            _acc_spec((SWA_KV_HEADS, GROUP_ROWS, 1)),
        ),
        scratch_shapes=[pltpu.VMEM((BLOCK, 2 * kw), F32), pltpu.VMEM((BLOCK, qw), BF)],
        aliases={11: 0}, sem=("arbitrary",),
        args=(qkv, qkv, qkv, qkv, qkv, dattn, pos_col, pos_row, pos_row, bias_t, sinks, dp), hook=hook)


def _bias_reduce(gb, dsk):
    def body(gb_ref, dsk_ref, drb_ref, dsink_ref):
        bucket = _t5_bucket(_block_rel())
        for b in range(REL_BUCKETS):
            mask = bucket == b
            for h in range(SWA_HEADS):
                drb_ref[b, h] = jnp.sum(jnp.where(mask, gb_ref[h], 0.0))
        for h in range(SWA_HEADS):
            dsink_ref[0, h] = jnp.sum(dsk_ref[h])

    return pl.pallas_call(
        body, name="bias_reduce", out_shape=(S((REL_BUCKETS, SWA_HEADS), F32), S((1, SWA_HEADS), F32)),
        in_specs=[VMEM_SPEC, VMEM_SPEC], out_specs=(SMEM_SPEC, SMEM_SPEC),
    )(gb, dsk)


def _mix_in_bwd(dp, x, gn, w_in, dxo):
    t, d = x.shape
    tm = 256
    npr = dp.shape[1]

    def body(dp_ref, x_ref, gn_ref, w_ref, dxo_ref, dx_ref, dgn_ref, h_ref):
        @pl.when(pl.program_id(0) == 0)
        def _():
            dgn_ref[...] = jnp.zeros_like(dgn_ref)
        dh = jnp.zeros((tm, d), F32)
        for c0 in range(0, npr, 1024):
            c1 = min(c0 + 1024, npr)
            dh = dh + _nt(dp_ref[:, c0:c1], w_ref[:, c0:c1])
        n, r = _rms(x_ref[...])
        gnv = gn_ref[...]
        h_ref[...] = (n * gnv).astype(BF)
        dx, dgn = _rms_bwd(dh, n, r, gnv)
        dx_ref[...] = dxo_ref[...] + dx
        dgn_ref[...] += dgn

    row = pl.BlockSpec((tm, d), lambda i: (i, 0))
    return pl.pallas_call(
        body, name="mix_in_bwd", grid=(t // tm,),
        out_shape=(S((t, d), F32), S((1, d), F32), S((t, d), BF)),
        in_specs=[pl.BlockSpec((tm, npr), lambda i: (i, 0)), row, _resident((1, d)), _resident(w_in.shape), row],
        out_specs=(row, _acc_spec((1, d)), row),
        compiler_params=_params(("arbitrary",)),
    )(dp, x, gn, w_in, dxo)


CAST_STEPS = 4


def _to_bf16(arrays, name, hook=None):
    def body(*refs):
        for src, dst in zip(refs[:len(arrays)], refs[len(arrays):]):
            dst[...] = src[...].astype(BF)

    blocks = [pl.BlockSpec((a.shape[0] // CAST_STEPS, a.shape[1]), lambda i: (i, 0)) for a in arrays]
    return _call(body, name=name, grid=(CAST_STEPS,), out_shape=tuple(S(a.shape, BF) for a in arrays), in_specs=blocks,
                 out_specs=tuple(blocks), sem=("parallel",), args=list(arrays), hook=hook)


PAIR_STEPS = 4


def _add_bf16(pairs, name):
    def body(*refs):
        ins, outs = refs[:2 * len(pairs)], refs[2 * len(pairs):]
        for q, o_ref in enumerate(outs):
            o_ref[...] = (ins[2 * q][...].astype(F32) + ins[2 * q + 1][...].astype(F32)).astype(BF)

    in_specs, out_specs, out_shape, args = [], [], [], []
    for a, b in pairs:
        rows, cols = a.shape
        blk = pl.BlockSpec((rows // PAIR_STEPS, cols), lambda i: (i, 0))
        in_specs += [blk, blk]
        out_specs.append(blk)
        out_shape.append(S((rows, cols), BF))
        args += [a, b]
    return pl.pallas_call(body, name=name, grid=(PAIR_STEPS,), out_shape=tuple(out_shape), in_specs=in_specs,
                          out_specs=tuple(out_specs), compiler_params=_params(("parallel",)))(*args)


def _adam_update(w, g, m, v):
    mn = ADAM_B1 * m + (1.0 - ADAM_B1) * g
    vn = ADAM_B2 * v + (1.0 - ADAM_B2) * (g * g)
    m_hat = mn / (1.0 - ADAM_B1 ** ADAM_STEP)
    v_hat = vn / (1.0 - ADAM_B2 ** ADAM_STEP)
    return -ADAM_LR * (m_hat / (jnp.sqrt(v_hat) + ADAM_EPS) + ADAM_WD * w), mn, vn


def _adamw(quads, name, steps, echo=False, hook=None):
    n_out = 4 if echo else 3

    def body(*refs):
        ins, outs = refs[:4 * len(quads)], refs[4 * len(quads):]
        for q in range(len(quads)):
            w_ref, g_ref, m_ref, v_ref = ins[4 * q:4 * q + 4]
            res = outs[n_out * q:n_out * q + n_out]
            gv = g_ref[...]
            if echo:
                res[0][...] = gv
            res[-3][...], res[-2][...], res[-1][...] = _adam_update(w_ref[...], gv, m_ref[...], v_ref[...])

    in_specs, out_specs, out_shape, args = [], [], [], []
    for quad in quads:
        rows, cols = quad[0].shape
        blk = pl.BlockSpec((rows // steps, cols), lambda i: (i, 0))
        in_specs += [blk] * 4
        out_specs += [blk] * n_out
        out_shape += [S((rows, cols), F32)] * n_out
        args += list(quad)
    res = _call(body, name=name, grid=(steps,), out_shape=tuple(out_shape), in_specs=in_specs, out_specs=tuple(out_specs),
                sem=("parallel",), args=args, hook=hook)
    return [res[n_out * q:n_out * q + n_out] for q in range(len(quads))]


def _place():
    x, y, c = lax.axis_index("x"), lax.axis_index("y"), lax.axis_index("c")
    return x, y, c


OTHER_CHIPS = ((1, 0), (0, 1), (1, 1))


def _flip(v, f):
    return 1 - v if f else v


def _remote(src, dst, ssem, rsem, dev):
    return pltpu.make_async_remote_copy(src_ref=src, dst_ref=dst, send_sem=ssem, recv_sem=rsem,
                                        device_id=dev, device_id_type=MESH)


class _Both:
    def __init__(self, hooks):
        self.hooks = hooks
        self.ins = [a for h in hooks for a in h.ins]
        self.out_shape = tuple(o for h in hooks for o in h.out_shape)
        self.scratch = [x for h in hooks for x in h.scratch]

    def _each(self, ins, outs, scr):
        i = o = s = 0
        for h in self.hooks:
            ni, no, ns = len(h.ins), len(h.out_shape), len(h.scratch)
            yield h, ins[i:i + ni], outs[o:o + no], scr[s:s + ns]
            i, o, s = i + ni, o + no, s + ns

    def start(self, ins, outs, scr):
        for h, *refs in self._each(ins, outs, scr):
            h.start(*refs)

    def relay(self, ins, outs, scr):
        for h, *refs in self._each(ins, outs, scr):
            h.relay(*refs)

    def finish(self, ins, outs, scr):
        for h, *refs in self._each(ins, outs, scr):
            h.finish(*refs)

    @property
    def lead(self):
        return max(h.lead for h in self.hooks)

    @property
    def results(self):
        return [h.results for h in self.hooks]

    @results.setter
    def results(self, res):
        for h, _, mine, _ in self._each((), tuple(res), ()):
            h.results = mine


class _GatherHook:
    def __init__(self, shards, kinds, lead=2):
        self.ins, self.kinds, self.lead, n = list(shards), list(kinds), lead, len(shards)
        self.out_shape = tuple(
            S((w.shape[0], 4 * w.shape[1]), BF) if k == "col" else S((4,) + w.shape, BF) for w, k in zip(shards, kinds))
        dma = pltpu.SemaphoreType.DMA
        self.scratch = ([dma((n, 3)) for _ in range(5)] + [dma((n,)), dma((n,))]
                        + [pltpu.VMEM((3, w.shape[0] // 2, w.shape[1]), BF) for w in shards]
                        + [pltpu.VMEM(w.shape, BF) for w in shards])

    def _window(self, outs, i, s, half):
        rows, cols = self.ins[i].shape
        rh = rows // 2
        start = pl.multiple_of(half * rh, 16)
        if self.kinds[i] == "col":
            return outs[i].at[pl.ds(start, rh), pl.ds(pl.multiple_of(s * cols, 128), cols)]
        return outs[i].at[s, pl.ds(start, rh), :]

    def _copies(self, ins, outs, scr):
        n = len(ins)
        ssem, rsem, fssem, frsem, ksem, lsem, osem = scr[:7]
        land, own = scr[7:7 + n], scr[7 + n:7 + 2 * n]
        x, y, c = _place()
        sibling = (x, y, 1 - c)
        loads, stores, sends, forwards, keeps, passed = [], [], [], [], [], []
        for i in range(n):
            rows, cols = self.ins[i].shape
            rh = rows // 2
            mine = (outs[i].at[:, pl.ds(pl.multiple_of((2 * x + y) * cols, 128), cols)] if self.kinds[i] == "col"
                    else outs[i].at[2 * x + y])
            loads.append(functools.partial(pltpu.make_async_copy, ins[i], own[i], lsem.at[i]))
            stores.append(functools.partial(pltpu.make_async_copy, own[i], mine, osem.at[i]))
            src = ins[i].at[pl.ds(pl.multiple_of(c * rh, 16), rh), :]
            for j, (fx, fy) in enumerate(OTHER_CHIPS):
                px, py = _flip(x, fx), _flip(y, fy)
                sends.append(functools.partial(_remote, src, land[i].at[j], ssem.at[i, j], rsem.at[i, j], (px, py, c)))
                here = self._window(outs, i, 2 * px + py, c)
                forwards.append(functools.partial(_remote, land[i].at[j], here, fssem.at[i, j], frsem.at[i, j], sibling))
                keeps.append(functools.partial(pltpu.make_async_copy, land[i].at[j], here, ksem.at[i, j]))
                there = self._window(outs, i, 2 * px + py, 1 - c)
                passed.append(functools.partial(_remote, there, there, fssem.at[i, j], frsem.at[i, j], sibling))
        return loads, stores, sends, forwards, keeps, passed

    def start(self, ins, outs, scr):
        loads, _, sends, _, _, _ = self._copies(ins, outs, scr)
        for make in sends + loads:
            make().start()

    def relay(self, ins, outs, scr):
        loads, stores, sends, forwards, keeps, _ = self._copies(ins, outs, scr)
        for load, store in zip(loads, stores):
            load().wait()
            store().start()
        for send, forward, keep in zip(sends, forwards, keeps):
            send().wait_recv()
            forward().start()
            keep().start()

    def finish(self, ins, outs, scr):
        _, stores, sends, forwards, keeps, passed = self._copies(ins, outs, scr)
        for make in passed:
            make().wait_recv()
        for make in sends + forwards:
            make().wait_send()
        for make in keeps + stores:
            make().wait()


class _ChipsHook:
    def __init__(self, parts, kinds):
        self.ins, self.kinds, n = list(parts), list(kinds), len(parts)
        self.out_shape = tuple(
            S((4, p.shape[0], p.shape[1] // 4), BF) if k == "col" else S(p.shape, BF) for p, k in zip(parts, kinds))
        dma = pltpu.SemaphoreType.DMA
        self.scratch = ([dma((n, 3)), dma((n, 3)), dma((n,)), dma((n,))]
                        + [pltpu.VMEM(o.shape[1:], BF) for o in self.out_shape])

    def _slab(self, ins, i, s):
        _, rows, cols = self.out_shape[i].shape
        if self.kinds[i] == "col":
            return ins[i].at[:, pl.ds(pl.multiple_of(s * cols, 128), cols)]
        return ins[i].at[s]

    def _copies(self, ins, outs, scr):
        ssem, rsem, lsem, osem = scr[:4]
        own = scr[4:]
        x, y, c = _place()
        loads, stores, sends = [], [], []
        for i in range(len(ins)):
            loads.append(functools.partial(pltpu.make_async_copy, self._slab(ins, i, 2 * x + y), own[i], lsem.at[i]))
            stores.append(functools.partial(pltpu.make_async_copy, own[i], outs[i].at[3], osem.at[i]))
            for j, (fx, fy) in enumerate(OTHER_CHIPS):
                px, py = _flip(x, fx), _flip(y, fy)
                sends.append(functools.partial(_remote, self._slab(ins, i, 2 * px + py), outs[i].at[j], ssem.at[i, j],
                                               rsem.at[i, j], (px, py, c)))
        return loads, stores, sends

    def start(self, ins, outs, scr):
        loads, _, sends = self._copies(ins, outs, scr)
        for make in sends + loads:
            make().start()

    lead = 2

    def relay(self, ins, outs, scr):
        loads, stores, _ = self._copies(ins, outs, scr)
        for load, store in zip(loads, stores):
            load().wait()
            store().start()

    def finish(self, ins, outs, scr):
        _, stores, sends = self._copies(ins, outs, scr)
        for make in sends + stores:
            make().wait()


SHARE_STEPS = 2


def _sum_share(slab_list, name, hook=None):
    n = len(slab_list)
    geom = [(sl.shape[1], sl.shape[1] // SHARE_STEPS, sl.shape[2]) for sl in slab_list]

    def body(*refs):
        ins, outs, scr = refs[:n], refs[n:2 * n], refs[2 * n:]
        i = pl.program_id(0)
        x, y, c = _place()
        sibling = (x, y, 1 - c)

        def copies(q, k):
            rh, tr, _ = geom[q]
            stage, lsem, ssem, rsem = scr[4 * q:4 * q + 4]
            dst = outs[q].at[pl.ds(pl.multiple_of(c * rh + k * tr, 8), tr), :]
            return (pltpu.make_async_copy(stage.at[k], dst, lsem.at[k]),
                    _remote(stage.at[k], dst, ssem.at[k], rsem, sibling))

        for q in range(n):
            acc = ins[q][3].astype(F32)
            for k in range(3):
                acc = acc + ins[q][k].astype(F32)
            scr[4 * q][i] = acc
            for cp in copies(q, i):
                cp.start()

        @pl.when(i == SHARE_STEPS - 1)
        def _():
            for q in range(n):
                rh = geom[q][0]
                for k in range(SHARE_STEPS):
                    local, remote = copies(q, k)
                    local.wait()
                    remote.wait_send()
                got = outs[q].at[pl.ds(pl.multiple_of((1 - c) * rh, 8), rh), :]
                _remote(got, got, scr[4 * q + 2].at[0], scr[4 * q + 3], sibling).wait_recv()

    dma = pltpu.SemaphoreType.DMA
    scratch = []
    for rh, tr, cols in geom:
        scratch += [pltpu.VMEM((SHARE_STEPS, tr, cols), F32), dma((SHARE_STEPS,)), dma((SHARE_STEPS,)), dma]
    return _call(
        body, name=name, grid=(SHARE_STEPS,), out_shape=tuple(S((2 * rh, cols), F32) for rh, _, cols in geom),
        in_specs=[pl.BlockSpec((4, tr, cols), lambda i: (0, i, 0)) for _, tr, cols in geom], out_specs=(ANY,) * n,
        scratch_shapes=scratch, sem=("arbitrary",), args=list(slab_list), hook=hook)


class _SmallSumHook:
    def __init__(self, buf):
        self.ins, self.out_shape = [buf], (S(buf.shape, F32),)
        dma = pltpu.SemaphoreType.DMA
        self.scratch = [pltpu.VMEM((8,) + buf.shape, F32), pltpu.VMEM(buf.shape, F32), dma((7,)), dma((7,)), dma]

    def _sends(self, scr):
        slots, _, ssem, rsem, _ = scr
        x, y, c = _place()
        me = 4 * x + 2 * y + c
        for r in range(1, 8):
            px, py, pc = _flip(x, (r >> 2) & 1), _flip(y, (r >> 1) & 1), _flip(c, r & 1)
            yield (functools.partial(_remote, slots.at[me], slots.at[me], ssem.at[r - 1], rsem.at[r - 1], (px, py, pc)),
                   functools.partial(_remote, slots.at[me], slots.at[4 * px + 2 * py + pc], ssem.at[r - 1],
                                     rsem.at[r - 1], (px, py, pc)))

    def start(self, ins, outs, scr):
        slots, _, _, _, lsem = scr
        x, y, c = _place()
        load = pltpu.make_async_copy(ins[0], slots.at[4 * x + 2 * y + c], lsem)
        load.start()
        load.wait()
        for send, _ in self._sends(scr):
            send().start()

    lead = 0

    def relay(self, ins, outs, scr):
        pass

    def finish(self, ins, outs, scr):
        slots, total, _, _, lsem = scr
        for _, arrival in self._sends(scr):
            arrival().wait_recv()
        for send, _ in self._sends(scr):
            send().wait_send()
        acc = slots[0]
        for k in range(1, 8):
            acc = acc + slots[k]
        total[...] = acc
        store = pltpu.make_async_copy(total, outs[0], lsem)
        store.start()
        store.wait()


BIG = ("ffn1_w_gu", "ffn1_w_down", "w_in", "w_out", "xattn_wq", "xattn_wkv", "xattn_wo", "ffn2_w_gu", "ffn2_w_down")
KIND = {"ffn1_w_gu": "col", "ffn1_w_down": "row", "w_in": "col", "w_out": "row", "xattn_wq": "row",
        "xattn_wkv": "col", "xattn_wo": "row", "ffn2_w_gu": "col", "ffn2_w_down": "row"}
WEIGHTS = ("rel_bias", "ffn1_norm", "ffn1_w_gu", "ffn1_w_down", "mix_norm", "w_in", "sinks", "conv_w", "w_out",
           "xattn_norm", "mem_norm", "xattn_wq", "xattn_wkv", "xattn_wo", "ffn2_norm", "ffn2_w_gu", "ffn2_w_down",
           "final_norm")
SMALL_ROWS = 16
GAIN_ROW = {"ffn1_norm": 0, "mix_norm": 1, "xattn_norm": 2, "mem_norm": 3, "ffn2_norm": 4, "final_norm": 5}
CONV_ROW, SINK_ROW, BIAS_ROW, LOSS_ROW = 6, 9, 10, 11
TAP_ROWS = 8


def _rows_block(rows, d):
    buf = jnp.zeros((SMALL_ROWS, d), F32)
    for r, v in rows.items():
        buf = lax.dynamic_update_slice(buf, v if v.ndim == 2 else v.reshape(1, -1), (r, 0))
    return buf


def _adamw_small(gsum, conv_g, bias_g, small):
    names = list(small)

    def grad(k, gsum_ref, conv_ref, bias_ref):
        if k in GAIN_ROW:
            return gsum_ref[GAIN_ROW[k]:GAIN_ROW[k] + 1, :]
        if k == "sinks":
            return gsum_ref[SINK_ROW:SINK_ROW + 1, 0:small[k][0].shape[1]]
        return conv_ref[...] if k == "conv_w" else bias_ref[...]

    def body(gsum_ref, conv_ref, bias_ref, *refs):
        ins, outs = refs[:3 * len(names)], refs[3 * len(names):]
        for q, k in enumerate(names):
            w_ref, m_ref, v_ref = ins[3 * q:3 * q + 3]
            g_ref, d_ref, mo_ref, vo_ref = outs[4 * q:4 * q + 4]
            gv = grad(k, gsum_ref, conv_ref, bias_ref)
            g_ref[...] = gv
            d_ref[...], mo_ref[...], vo_ref[...] = _adam_update(w_ref[...], gv, m_ref[...], v_ref[...])

    res = pl.pallas_call(
        body, name="adamw_small", out_shape=tuple(S(small[k][0].shape, F32) for k in names for _ in range(4)),
        compiler_params=_params())(gsum, conv_g, bias_g, *[a for k in names for a in small[k]])
    return {k: res[4 * q:4 * q + 4] for q, k in enumerate(names)}


def _local_step(x, mem, pos, target, w, gains, rel_bias, sinks, conv_w, shards=None):
    t, d = x.shape
    dist = shards is not None
    w = dict(w)
    grads, slabs = {}, {}
    pos_col = pos.reshape(t, 1)
    pos_row = pos.reshape(1, t)
    bias_t = _bias_build(rel_bias).reshape(SWA_KV_HEADS, GROUP_ROWS, 2 * BLOCK)

    def gather(names, lead=2):
        return _GatherHook([shards[k] for k in names], [KIND[k] for k in names], lead) if dist else None

    def gathered(names, hook):
        if dist:
            for k, gw in zip(names, hook.results):
                w[k] = gw if KIND[k] == "col" else gw.reshape(-1, gw.shape[-1])

    def dw(problems, tn, name, hook=None):
        if dist:
            res = _dw_pair([(a, b) for _, a, b in problems], tn, name, KIND[problems[0][0]], hook)
        else:
            res = [_dw(a, b, tn, name + "_" + k) for k, a, b in problems]
        grads.update(zip((k for k, _, _ in problems), res))

    def pair_sums(names):
        if not dist:
            return None
        flat = lambda k, v: v.reshape(-1, v.shape[-1]) if KIND[k] == "row" else v
        sums = _add_bf16([(flat(k, grads[k][0]), flat(k, grads[k][1])) for k in names], "pair_sum_" + names[0])
        return {k: (p if KIND[k] == "col" else p.reshape(4, -1, p.shape[-1])) for k, p in zip(names, sums)}

    def chips(names, parts):
        return _ChipsHook([parts[k] for k in names], [KIND[k] for k in names]) if dist else None

    def reduced(names, hook):
        if dist:
            slabs.update(zip(names, hook.results))

    names = ("w_in", "w_out")
    hook = gather(names, lead=1)
    x1, g1, u1 = _ffn_fwd(x, gains["ffn1_norm"], w["ffn1_w_gu"], w["ffn1_w_down"], "ffn1_fwd", hook)
    gathered(names, hook)
    names = ("xattn_wq", "xattn_wkv", "xattn_wo")
    hook = gather(names)
    qkv, e = _mix_proj(x1, gains["mix_norm"], w["w_in"], hook)
    gathered(names, hook)
    names = ("ffn2_w_gu",)
    hook = gather(names)
    attn = _swa_fwd(qkv, pos_col, pos_row, bias_t, sinks, hook)
    gathered(names, hook)
    names = ("ffn2_w_down",)
    hook = gather(names)
    x2, merged = _mix_out_fwd(e, attn, conv_w, w["w_out"], x1, hook)
    gathered(names, hook)
    mh, kv = _mem_kv(mem, gains["mem_norm"], w["xattn_wkv"])
    x3, qx, o = _xattn_fwd(x2, gains["xattn_norm"], w["xattn_wq"], kv, w["xattn_wo"])
    dx4, g2, u2, d_final, loss = _ffn_fwd(x3, gains["ffn2_norm"], w["ffn2_w_gu"], w["ffn2_w_down"], "ffn2_fwd",
                                          head=(gains["final_norm"], target))

    dx3, d_ffn2, dgu2, a2, h4, dyh4 = _ffn_bwd(dx4, x3, gains["ffn2_norm"], g2, u2, w["ffn2_w_gu"], w["ffn2_w_down"],
                                               "ffn2_bwd")
    dw([("ffn2_w_gu", h4, dgu2)], 1408, "dw_ffn2_gu")
    dw([("ffn2_w_down", a2, dyh4)], 512, "dw_ffn2_down")
    parts = pair_sums(("ffn2_w_gu", "ffn2_w_down"))
    hook = chips(("ffn2_w_gu",), parts)
    dx2, d_xattn, dkv, dxh3, h3, dqx = _xattn_bwd(dx3, x2, gains["xattn_norm"], qx, kv, w["xattn_wq"], w["xattn_wo"], hook)
    reduced(("ffn2_w_gu",), hook)
    grads["xattn_wkv"], d_mem = _mem_bwd(dkv, mh, mem, gains["mem_norm"], w["xattn_wkv"], dist)
    hook = chips(("ffn2_w_down",), parts)
    dattn, dp, dcw, dxh2 = _mix_out_bwd(dx2, e, attn, conv_w, w["w_out"], qkv.shape[1], hook)
    reduced(("ffn2_w_down",), hook)
    dw([("xattn_wo", o, dxh3), ("xattn_wq", h3, dqx), ("w_out", merged, dxh2)], 1024, "dw_wo_wq_wout")
    names = ("xattn_wo", "xattn_wq", "xattn_wkv", "w_out")
    hook = chips(names, pair_sums(names))
    dp, gb, dsk = _swa_bwd(qkv, dattn, pos_col, pos_row, bias_t, sinks, dp, hook)
    reduced(names, hook)
    d_rel_bias, d_sinks = _bias_reduce(gb.reshape(SWA_HEADS, BLOCK, 2 * BLOCK), dsk.reshape(SWA_HEADS, BLOCK, 1))
    dx1, d_mix, h2 = _mix_in_bwd(dp, x1, gains["mix_norm"], w["w_in"], dx2)
    dw([("w_in", h2, dp)], w["w_in"].shape[1] // 4, "dw_win")
    names = ("w_in",)
    hook = chips(names, pair_sums(names))
    dx0, d_ffn1, dgu1, a1, h1, dyh1 = _ffn_bwd(dx1, x, gains["ffn1_norm"], g1, u1, w["ffn1_w_gu"], w["ffn1_w_down"],
                                               "ffn1_bwd")
    dw([("ffn1_w_gu", h1, dgu1)], 1408, "dw_ffn1_gu", hook)
    reduced(names, hook)
    names = ("ffn1_w_gu",)
    hook = chips(names, pair_sums(names))
    dw([("ffn1_w_down", a1, dyh1)], 512, "dw_ffn1_down", hook)
    reduced(names, hook)
    rows = {0: d_ffn1, 1: d_mix, 2: d_xattn, 3: d_mem, 4: d_ffn2, 5: d_final, SINK_ROW: d_sinks,
            BIAS_ROW: d_rel_bias.reshape(1, -1), LOSS_ROW: loss[0, 0:1]}
    rows.update({CONV_ROW + j: dcw[j] for j in range(3)})
    last = ("ffn1_w_down",)
    return dx0, (slabs if dist else grads), _rows_block(rows, d), chips(last, pair_sums(last))


def kernel(x, mem, positions, rel_bias, ffn1_norm, ffn1_w_gu, ffn1_w_down, mix_norm, w_in, sinks, conv_w, w_out, xattn_norm, mem_norm, xattn_wq, xattn_wkv, xattn_wo, ffn2_norm, ffn2_w_gu, ffn2_w_down, final_norm, loss_target, m_rel_bias, m_ffn1_norm, m_ffn1_w_gu, m_ffn1_w_down, m_mix_norm, m_w_in, m_sinks, m_conv_w, m_w_out, m_xattn_norm, m_mem_norm, m_xattn_wq, m_xattn_wkv, m_xattn_wo, m_ffn2_norm, m_ffn2_w_gu, m_ffn2_w_down, m_final_norm, v_rel_bias, v_ffn1_norm, v_ffn1_w_gu, v_ffn1_w_down, v_mix_norm, v_w_in, v_sinks, v_conv_w, v_w_out, v_xattn_norm, v_mem_norm, v_xattn_wq, v_xattn_wkv, v_xattn_wo, v_ffn2_norm, v_ffn2_w_gu, v_ffn2_w_down, v_final_norm):
    args = dict(locals())
    wts = {k: args[k] for k in WEIGHTS}
    mom = {k: args["m_" + k] for k in WEIGHTS}
    var = {k: args["v_" + k] for k in WEIGHTS}
    d = x.shape[-1]
    s_me = 2 * lax.axis_index("x") + lax.axis_index("y")

    first = ("ffn1_w_gu", "ffn1_w_down")
    rest = tuple(k for k in BIG if k not in first)
    shards = {k: wts[k][0].astype(BF) for k in first}
    cw_cols = conv_w.shape[-1]
    placed = lax.dynamic_update_slice(jnp.zeros((TAP_ROWS, d), F32), 0.5 * conv_w[0], (0, s_me * cw_cols))
    head = _Both([_GatherHook([shards[k] for k in first], [KIND[k] for k in first]), _SmallSumHook(placed)])
    shards.update(zip(rest, _to_bf16([wts[k][0] for k in rest], "gather_ffn1", head)))
    gathered, (conv_sum,) = head.results
    whole = {k: (gw if KIND[k] == "col" else gw.reshape(-1, gw.shape[-1])) for k, gw in zip(first, gathered)}
    conv_whole = conv_sum[0:3]

    gains = {k: wts[k].reshape(1, d) for k in GAIN_ROW}
    dx0, slabs, small, last_chips = _local_step(x[0], mem[0], positions[0], loss_target[0], whole, gains, rel_bias, sinks,
                                               conv_whole, shards)

    late = ("ffn1_w_gu", "ffn1_w_down", "ffn2_w_down")
    early = tuple(k for k in BIG if k not in late)
    first_ones = tuple(k for k in BIG if k != "ffn1_w_down")
    tail = _Both([last_chips, _SmallSumHook(small)])
    shard_grads = dict(zip(first_ones, _sum_share([slabs[k] for k in first_ones], "sum_share", tail)))
    (last_slabs,), (small_sum,) = tail.results
    shard_grads["ffn1_w_down"], = _sum_share([last_slabs], "sum_share_ffn1_w_down")
    quad = lambda k: (wts[k][0], shard_grads[k], mom[k][0], var[k][0])
    updates = dict(zip(early, _adamw([quad(k) for k in early], "adamw_early", 16, echo=True)))
    updates.update(zip(late, _adamw([quad(k) for k in late], "adamw_late", 8, echo=True)))
    loss = small_sum[LOSS_ROW, 0]

    out_g, out_d, out_m, out_v = {}, {}, {}, {}
    for k in BIG:
        out_g[k], out_d[k], out_m[k], out_v[k] = (a[None] for a in updates[k])

    conv_g = lax.dynamic_slice(small_sum, (CONV_ROW, s_me * cw_cols), (3, cw_cols))
    two_d = lambda a: a.reshape(a.shape[-2:]) if a.ndim > 1 else a.reshape(1, -1)
    small_names = [k for k in WEIGHTS if k not in KIND]
    bias_g = small_sum[BIAS_ROW, 0:rel_bias.size].reshape(rel_bias.shape)
    done = _adamw_small(small_sum, conv_g, bias_g, {k: (two_d(wts[k]), two_d(mom[k]), two_d(var[k])) for k in small_names})
    for k in small_names:
        out_g[k], out_d[k], out_m[k], out_v[k] = (a.reshape(wts[k].shape) for a in done[k])

    return (loss, dx0[None], *[out_g[k] for k in WEIGHTS], *[out_d[k] for k in WEIGHTS],
            *[out_m[k] for k in WEIGHTS], *[out_v[k] for k in WEIGHTS])
```

```python
import functools
import math

import jax
import jax.numpy as jnp
from jax import lax
from jax.experimental import pallas as pl
from jax.experimental.pallas import tpu as pltpu

BF = jnp.bfloat16
F32 = jnp.float32
I32 = jnp.int32
S = jax.ShapeDtypeStruct

EPS = 1e-6
NEG = -1e30
POS_PAD = 1 << 30
WINDOW = 128
BLOCK = 128
HEAD_DIM = 64
SWA_HEADS = 16
SWA_KV_HEADS = 4
SWA_GROUP = SWA_HEADS // SWA_KV_HEADS
MEM_HEADS = 4
REL_BUCKETS = 32
REL_MAX_DIST = 128
ADAM_LR = 0.001
ADAM_B1 = 0.9
ADAM_B2 = 0.999
ADAM_EPS = 1e-08
ADAM_WD = 0.01
ADAM_STEP = 10

V7X_VMEM_LIMIT_BYTES = 56 * 1024 * 1024
MESH = pl.DeviceIdType.MESH
ANY = pl.BlockSpec(memory_space=pl.ANY)
VMEM_SPEC = pl.BlockSpec(memory_space=pltpu.VMEM)
SMEM_SPEC = pl.BlockSpec(memory_space=pltpu.SMEM)


def _params(sem=None):
    return pltpu.CompilerParams(dimension_semantics=sem, vmem_limit_bytes=V7X_VMEM_LIMIT_BYTES)


def _resident(shape):
    nd = len(shape)
    return pl.BlockSpec(shape, lambda *_: (0,) * nd, pipeline_mode=pl.Buffered(1))


def _acc_spec(shape):
    nd = len(shape)
    return pl.BlockSpec(shape, lambda *_: (0,) * nd)


def _call(body, *, name, grid, out_shape, in_specs, out_specs, args, sem, scratch_shapes=(), aliases=None, hook=None):
    aliases = aliases or {}
    if hook is None:
        return pl.pallas_call(body, name=name, grid=grid, out_shape=out_shape, in_specs=in_specs, out_specs=out_specs,
                              scratch_shapes=list(scratch_shapes), input_output_aliases=aliases,
                              compiler_params=_params(sem))(*args)
    n_in, n_out, n_scr = len(in_specs), len(out_shape), len(scratch_shapes)
    h_in, h_out = len(hook.ins), len(hook.out_shape)

    def at_step(pick):
        conds = [pl.program_id(ax) == pick(size) for ax, size in enumerate(grid)]
        return functools.reduce(jnp.logical_and, conds)

    def hosted(*refs):
        k_in, x_in = refs[:n_in], refs[n_in:n_in + h_in]
        o0 = n_in + h_in
        k_out, x_out = refs[o0:o0 + n_out], refs[o0 + n_out:o0 + n_out + h_out]
        s0 = o0 + n_out + h_out
        k_scr, x_scr = refs[s0:s0 + n_scr], refs[s0 + n_scr:]

        @pl.when(at_step(lambda size: 0))
        def _():
            hook.start(x_in, x_out, x_scr)
        body(*k_in, *k_out, *k_scr)
        early = bool(hook.lead) and grid[0] > hook.lead
        if early:
            on_axis0 = pl.program_id(0) == grid[0] - 1 - hook.lead
            rest_first = [pl.program_id(ax) == 0 for ax in range(1, len(grid))]

            @pl.when(functools.reduce(jnp.logical_and, rest_first, on_axis0))
            def _():
                hook.relay(x_in, x_out, x_scr)

        @pl.when(at_step(lambda size: size - 1))
        def _():
            if not early:
                hook.relay(x_in, x_out, x_scr)
            hook.finish(x_in, x_out, x_scr)

    res = pl.pallas_call(
        hosted, name=name, grid=grid, out_shape=tuple(out_shape) + tuple(hook.out_shape),
        in_specs=list(in_specs) + [ANY] * h_in, out_specs=tuple(out_specs) + (ANY,) * h_out,
        scratch_shapes=list(scratch_shapes) + list(hook.scratch), input_output_aliases=aliases,
        compiler_params=_params(("arbitrary",) * len(grid)),
    )(*args, *hook.ins)
    hook.results = res[n_out:]
    return res[:n_out]


def _nn(a, b):
    return jnp.dot(a, b, preferred_element_type=F32)


def _nt(a, b):
    return lax.dot_general(a, b, (((1,), (1,)), ((), ())), preferred_element_type=F32)


def _tn(a, b):
    return lax.dot_general(a, b, (((0,), (0,)), ((), ())), preferred_element_type=F32)


def _sigmoid(v):
    return 1.0 / (1.0 + jnp.exp(-v))


def _rms(x):
    r = lax.rsqrt(jnp.mean(x * x, axis=-1, keepdims=True) + EPS)
    return x * r, r


def _rms_bwd(dh, n, r, g):
    dn = dh * g
    dx = r * (dn - n * jnp.mean(dn * n, axis=-1, keepdims=True))
    return dx, jnp.sum(dh * n, axis=0, keepdims=True)


def _ffn_fwd(x, gn, wgu, wd, name, hook=None, head=None):
    t, d = x.shape
    f = wd.shape[0]
    tm, fc = 256, 1408

    def body(x_ref, gn_ref, wgu_ref, wd_ref, *rest):
        xv = x_ref[...]
        n, _ = _rms(xv)
        h = (n * gn_ref[...]).astype(BF)
        g_ref, u_ref = rest[-2:] if head is None else rest[3:5]
        acc = jnp.zeros((tm, d), F32)
        for c0 in range(0, f, fc):
            g = _nn(h, wgu_ref[:, c0:c0 + fc])
            u = _nn(h, wgu_ref[:, f + c0:f + c0 + fc])
            g_ref[:, c0:c0 + fc] = g.astype(BF)
            u_ref[:, c0:c0 + fc] = u.astype(BF)
            a = (g * _sigmoid(g)) * u
            acc = acc + _nn(a.astype(BF), wd_ref[c0:c0 + fc, :])
        y = xv + 0.5 * acc
        if head is None:
            rest[0][...] = y
            return
        gf_ref, t_ref, dy_ref, _, _, dgf_ref, loss_ref = rest

        @pl.when(pl.program_id(0) == 0)
        def _():
            dgf_ref[...] = jnp.zeros_like(dgf_ref)
            loss_ref[...] = jnp.zeros_like(loss_ref)
        ny, ry = _rms(y)
        gf = gf_ref[...]
        err = ny * gf - t_ref[...]
        loss_ref[...] += 0.5 * jnp.sum(jnp.sum(err * err, axis=-1, keepdims=True) / d, axis=0, keepdims=True)
        dy, dgf = _rms_bwd(err / d, ny, ry, gf)
        dy_ref[...] = dy
        dgf_ref[...] += dgf

    row = pl.BlockSpec((tm, d), lambda i: (i, 0))
    frow = pl.BlockSpec((tm, f), lambda i: (i, 0))
    in_specs = [row, _resident((1, d)), _resident(wgu.shape), _resident(wd.shape)]
    out_shape = (S((t, d), F32), S((t, f), BF), S((t, f), BF))
    out_specs = (row, frow, frow)
    args = (x, gn, wgu, wd)
    if head is not None:
        in_specs += [_resident((1, d)), row]
        out_shape += (S((1, d), F32), S((1, 128), F32))
        out_specs += (_acc_spec((1, d)), _acc_spec((1, 128)))
        args += tuple(head)
    return _call(body, name=name, grid=(t // tm,), out_shape=out_shape, in_specs=in_specs, out_specs=out_specs,
                 sem=("parallel",) if head is None else ("arbitrary",), args=args, hook=hook)


def _mix_proj(x, gn, w_in, hook=None):
    t, d = x.shape
    tm = 256
    nqkv = 1536
    ne = w_in.shape[1] - nqkv

    def body(x_ref, gn_ref, w_ref, qkv_ref, e_ref):
        n, _ = _rms(x_ref[...])
        h = (n * gn_ref[...]).astype(BF)
        qkv_ref[...] = _nn(h, w_ref[:, 0:nqkv]).astype(BF)
        for c0 in range(0, ne, 1024):
            e_ref[:, c0:c0 + 1024] = _nn(h, w_ref[:, nqkv + c0:nqkv + c0 + 1024]).astype(BF)

    return _call(
        body, name="mix_proj", grid=(t // tm,),
        out_shape=(S((t, nqkv), BF), S((t, ne), BF)),
        in_specs=[pl.BlockSpec((tm, d), lambda i: (i, 0)), _resident((1, d)), _resident(w_in.shape)],
        out_specs=(pl.BlockSpec((tm, nqkv), lambda i: (i, 0)), pl.BlockSpec((tm, ne), lambda i: (i, 0))),
        sem=("parallel",), args=(x, gn, w_in), hook=hook)


def _t5_bucket(rel):
    n = jnp.maximum(rel, 0)
    max_exact = REL_BUCKETS // 2
    nf = jnp.maximum(n, 1).astype(F32)
    large = max_exact + (jnp.log(nf / max_exact) / math.log(REL_MAX_DIST / max_exact)
                         * (REL_BUCKETS - max_exact)).astype(I32)
    large = jnp.minimum(large, REL_BUCKETS - 1)
    return jnp.where(n < max_exact, n, large)


def _block_rel():
    i = lax.broadcasted_iota(I32, (BLOCK, 2 * BLOCK), 0)
    j = lax.broadcasted_iota(I32, (BLOCK, 2 * BLOCK), 1)
    return i + BLOCK - j


def _bias_build(rel_bias):
    def body(rb_ref, o_ref):
        bucket = _t5_bucket(_block_rel())
        for h in range(SWA_HEADS):
            acc = jnp.zeros((BLOCK, 2 * BLOCK), F32)
            for b in range(REL_BUCKETS):
                acc = jnp.where(bucket == b, rb_ref[b, h], acc)
            o_ref[h] = acc

    return pl.pallas_call(
        body, name="bias_build", out_shape=S((SWA_HEADS, BLOCK, 2 * BLOCK), F32),
        in_specs=[SMEM_SPEC], out_specs=VMEM_SPEC,
    )(rel_bias)


GROUP_ROWS = SWA_GROUP * BLOCK


def _swa_visible(b, pq_ref, pkp_ref, pkc_ref):
    pk = jnp.concatenate([pkp_ref[...], pkc_ref[...]], axis=1)
    col = lax.broadcasted_iota(I32, (1, 2 * BLOCK), 1)
    pk = jnp.where(jnp.logical_and(b == 0, col < BLOCK), POS_PAD, pk)
    rel = jnp.concatenate([pq_ref[...]] * SWA_GROUP, axis=0) - pk
    return jnp.logical_and(rel >= 0, rel < WINDOW)


def _group_heads(ref, hk):
    h0 = hk * SWA_GROUP
    return jnp.concatenate([ref[:, (h0 + g) * HEAD_DIM:(h0 + g + 1) * HEAD_DIM] for g in range(SWA_GROUP)], axis=0)


def _group_sinks(sink_ref, hk):
    row = lax.broadcasted_iota(I32, (GROUP_ROWS, 1), 0)
    col = jnp.zeros((GROUP_ROWS, 1), F32) + sink_ref[0, hk * SWA_GROUP]
    for g in range(1, SWA_GROUP):
        col = jnp.where(row >= g * BLOCK, sink_ref[0, hk * SWA_GROUP + g], col)
    return col


def _swa_probs(qg, kh, vis, bias, sink):
    s = _nt(qg, kh) * (HEAD_DIM ** -0.5)
    s = jnp.where(vis, s + bias, NEG)
    m = jnp.maximum(jnp.max(s, axis=-1, keepdims=True), sink)
    p = jnp.exp(s - m)
    ps = jnp.exp(sink - m)
    inv = 1.0 / (jnp.sum(p, axis=-1, keepdims=True) + ps)
    return p * inv, ps * inv


def _swa_fwd(qkv, pos_col, pos_row, bias_t, sinks, hook=None):
    t = qkv.shape[0]
    nb = t // BLOCK
    qw = SWA_HEADS * HEAD_DIM
    kw = SWA_KV_HEADS * HEAD_DIM

    def body(q_ref, kp_ref, kc_ref, vp_ref, vc_ref, pq_ref, pkp_ref, pkc_ref, bias_ref, sink_ref, o_ref):
        b = pl.program_id(0)
        vis = _swa_visible(b, pq_ref, pkp_ref, pkc_ref)
        k2 = jnp.concatenate([kp_ref[...], kc_ref[...]], axis=0)
        v2 = jnp.concatenate([vp_ref[...], vc_ref[...]], axis=0)
        for hk in range(SWA_KV_HEADS):
            kh = k2[:, hk * HEAD_DIM:(hk + 1) * HEAD_DIM]
            vh = v2[:, hk * HEAD_DIM:(hk + 1) * HEAD_DIM]
            pn, _ = _swa_probs(_group_heads(q_ref, hk), kh, vis, bias_ref[hk], _group_sinks(sink_ref, hk))
            o = _nn(pn.astype(BF), vh)
            for g in range(SWA_GROUP):
                h = hk * SWA_GROUP + g
                o_ref[:, h * HEAD_DIM:(h + 1) * HEAD_DIM] = o[g * BLOCK:(g + 1) * BLOCK]

    prev = lambda b: jnp.maximum(b - 1, 0)
    return _call(
        body, name="swa_fwd", grid=(nb,), out_shape=(S((t, qw), F32),),
        in_specs=[
            pl.BlockSpec((BLOCK, qw), lambda b: (b, 0)),
            pl.BlockSpec((BLOCK, kw), lambda b: (prev(b), qw // kw)),
            pl.BlockSpec((BLOCK, kw), lambda b: (b, qw // kw)),
            pl.BlockSpec((BLOCK, kw), lambda b: (prev(b), qw // kw + 1)),
            pl.BlockSpec((BLOCK, kw), lambda b: (b, qw // kw + 1)),
            pl.BlockSpec((BLOCK, 1), lambda b: (b, 0)),
            pl.BlockSpec((1, BLOCK), lambda b: (0, prev(b))),
            pl.BlockSpec((1, BLOCK), lambda b: (0, b)),
            _resident(bias_t.shape),
            SMEM_SPEC,
        ],
        out_specs=(pl.BlockSpec((BLOCK, qw), lambda b: (b, 0)),),
        sem=("parallel",), args=(qkv, qkv, qkv, qkv, qkv, pos_col, pos_row, pos_row, bias_t, sinks), hook=hook)[0]


HALO = 16


def _conv_taps(z, zh, first):
    tm = z.shape[0]
    zh = jnp.where(first, 0.0, zh)
    row = lax.broadcasted_iota(I32, (tm, 1), 0)
    z1 = jnp.where(row == 0, zh[HALO - 1:HALO, :], pltpu.roll(z, 1, 0))
    z2 = jnp.where(row == 0, zh[HALO - 2:HALO - 1, :], jnp.where(row == 1, zh[HALO - 1:HALO, :], pltpu.roll(z, 2, 0)))
    return z1, z2


def _mix_out_fwd(e, attn, conv_w, w_out, x, hook=None):
    t, d = x.shape
    tm = 256
    hb = tm // HALO
    f32 = lambda ref: ref[...].astype(F32)

    def body(c_ref, b_ref, u_ref, ga_ref, gc_ref, ch_ref, uh_ref, attn_ref, cw_ref, w_ref, x_ref, xo_ref, mg_ref):
        i = pl.program_id(0)
        z = f32(c_ref) * f32(u_ref)
        z1, z2 = _conv_taps(z, f32(ch_ref) * f32(uh_ref), i == 0)
        s = cw_ref[0:1, :] * z2 + cw_ref[1:2, :] * z1 + cw_ref[2:3, :] * z
        conv = f32(b_ref) * s
        merged = (_sigmoid(f32(ga_ref)) * attn_ref[...] + _sigmoid(f32(gc_ref)) * conv).astype(BF)
        mg_ref[...] = merged
        xo_ref[...] = x_ref[...] + _nn(merged, w_ref[...])

    ecol = lambda cb: pl.BlockSpec((tm, d), lambda i: (i, cb))
    halo = lambda cb: pl.BlockSpec((HALO, d), lambda i: (jnp.maximum(i * hb - 1, 0), cb))
    row = pl.BlockSpec((tm, d), lambda i: (i, 0))
    return _call(
        body, name="mix_out_fwd", grid=(t // tm,),
        out_shape=(S((t, d), F32), S((t, d), BF)),
        in_specs=[ecol(0), ecol(1), ecol(2), ecol(3), ecol(4), halo(0), halo(2), row,
                  _resident(conv_w.shape), _resident(w_out.shape), row],
        out_specs=(row, row),
        sem=("parallel",), args=(e, e, e, e, e, e, e, attn, conv_w, w_out, x), hook=hook)


def _mem_kv(mem, gm, wkv):
    m, d = mem.shape

    def body(mem_ref, gm_ref, w_ref, mh_ref, kv_ref):
        n, _ = _rms(mem_ref[...])
        mh = (n * gm_ref[...]).astype(BF)
        mh_ref[...] = mh
        kv_ref[...] = _nn(mh, w_ref[...]).astype(BF)

    return pl.pallas_call(
        body, name="mem_kv", out_shape=(S((m, d), BF), S((m, wkv.shape[1]), BF)),
        compiler_params=_params(),
    )(mem, gm, wkv)


def _xattn_probs(qh, kh):
    s = _nt(qh, kh) * (kh.shape[1] ** -0.5)
    p = jnp.exp(s - jnp.max(s, axis=-1, keepdims=True))
    return p * (1.0 / jnp.sum(p, axis=-1, keepdims=True))


def _xattn_fwd(x, gn, wq, kv, wo):
    t, d = x.shape
    tm = 256
    hd = d // MEM_HEADS

    def body(x_ref, gn_ref, wq_ref, kv_ref, wo_ref, xo_ref, q_ref, o_ref):
        xv = x_ref[...]
        n, _ = _rms(xv)
        q = _nn((n * gn_ref[...]).astype(BF), wq_ref[...]).astype(BF)
        q_ref[...] = q
        outs = []
        for hh in range(MEM_HEADS):
            p = _xattn_probs(q[:, hh * hd:(hh + 1) * hd], kv_ref[:, hh * hd:(hh + 1) * hd])
            outs.append(_nn(p.astype(BF), kv_ref[:, d + hh * hd:d + (hh + 1) * hd]))
        o = jnp.concatenate(outs, axis=1).astype(BF)
        o_ref[...] = o
        xo_ref[...] = xv + _nn(o, wo_ref[...])

    row = pl.BlockSpec((tm, d), lambda i: (i, 0))
    return pl.pallas_call(
        body, name="xattn_fwd", grid=(t // tm,),
        out_shape=(S((t, d), F32), S((t, d), BF), S((t, d), BF)),
        in_specs=[row, _resident((1, d)), _resident(wq.shape), _resident(kv.shape), _resident(wo.shape)],
        out_specs=(row, row, row),
        compiler_params=_params(("parallel",)),
    )(x, gn, wq, kv, wo)


def _ffn_bwd(dxo, x, gn, g, u, wgu, wd, name, hook=None):
    t, d = x.shape
    f = wd.shape[0]
    tm, fc = 256, 1408

    def body(dxo_ref, x_ref, gn_ref, g_ref, u_ref, wgu_ref, wd_ref, dx_ref, dgn_ref, dgu_ref, a_ref, h_ref, dyh_ref):
        @pl.when(pl.program_id(0) == 0)
        def _():
            dgn_ref[...] = jnp.zeros_like(dgn_ref)
        dxov = dxo_ref[...]
        dyh = (0.5 * dxov).astype(BF)
        dyh_ref[...] = dyh
        n, r = _rms(x_ref[...])
        gnv = gn_ref[...]
        h_ref[...] = (n * gnv).astype(BF)
        dh = jnp.zeros((tm, d), F32)
        for c0 in range(0, f, fc):
            gv = g_ref[:, c0:c0 + fc].astype(F32)
            uv = u_ref[:, c0:c0 + fc].astype(F32)
            da = _nt(dyh, wd_ref[c0:c0 + fc, :])
            sg = _sigmoid(gv)
            silu = gv * sg
            a_ref[:, c0:c0 + fc] = (silu * uv).astype(BF)
            dg = (da * uv * (sg * (1.0 + gv * (1.0 - sg)))).astype(BF)
            du = (da * silu).astype(BF)
            dgu_ref[:, c0:c0 + fc] = dg
            dgu_ref[:, f + c0:f + c0 + fc] = du
            dh = dh + _nt(dg, wgu_ref[:, c0:c0 + fc]) + _nt(du, wgu_ref[:, f + c0:f + c0 + fc])
        dx, dgn = _rms_bwd(dh, n, r, gnv)
        dx_ref[...] = dxov + dx
        dgn_ref[...] += dgn

    row = pl.BlockSpec((tm, d), lambda i: (i, 0))
    frow = pl.BlockSpec((tm, f), lambda i: (i, 0))
    return _call(
        body, name=name, grid=(t // tm,),
        out_shape=(S((t, d), F32), S((1, d), F32), S((t, 2 * f), BF), S((t, f), BF), S((t, d), BF), S((t, d), BF)),
        in_specs=[row, row, _resident((1, d)), frow, frow, _resident(wgu.shape), _resident(wd.shape)],
        out_specs=(row, _acc_spec((1, d)), pl.BlockSpec((tm, 2 * f), lambda i: (i, 0)), frow, row, row),
        sem=("arbitrary",), args=(dxo, x, gn, g, u, wgu, wd), hook=hook)


def _dw_pair(pairs, tn, name, kind, hook=None):
    npairs = len(pairs)
    t, ka = pairs[0][0].shape
    nb = pairs[0][1].shape[1]
    tt = 2048 if npairs == 1 else 1024
    nt, nj = t // tt, nb // tn
    col = kind == "col"
    rh = ka // 2 if col else ka // 8
    tile = (rh, tn) if col else (4, rh, tn)
    half = (rh, nb) if col else (4, rh, nb)
    lead = (slice(None),) * (len(tile) - 1)

    def body(*refs):
        ins, mines, sibs = refs[:2 * npairs], refs[2 * npairs:3 * npairs], refs[3 * npairs:4 * npairs]
        acc_ref, stage, ssem, rsem = refs[4 * npairs:]
        j, k = pl.program_id(0), pl.program_id(1)
        x, y, c = _place()
        sibling = (x, y, 1 - c)

        def send(p, slot, jj):
            dst = sibs[p].at[lead + (pl.ds(pl.multiple_of(jj * tn, 128), tn),)]
            return _remote(stage.at[slot], dst, ssem.at[slot], rsem.at[p], sibling)

        def rows(s, whose):
            return acc_ref[pl.ds(pl.multiple_of(s * 2 * rh + whose * rh, 16), rh), :].astype(BF)

        def step(p):
            a_ref, b_ref, mine_ref = ins[2 * p], ins[2 * p + 1], mines[p]

            @pl.when(k == 0)
            def _():
                acc_ref[...] = jnp.zeros_like(acc_ref)
            acc_ref[...] += _tn(a_ref[...], b_ref[...])

            @pl.when(k == nt - 1)
            def _():
                slot = j % 2

                @pl.when(j >= 2)
                def _():
                    send(p, slot, 0).wait_send()
                if col:
                    mine_ref[...] = rows(0, c)
                    stage[slot] = rows(0, 1 - c)
                else:
                    for s in range(4):
                        mine_ref[s] = rows(s, c)
                        stage[slot, s] = rows(s, 1 - c)
                send(p, slot, j - p * nj).start()

        for p in range(npairs):
            pl.when(j // nj == p)(functools.partial(step, p))

        @pl.when(jnp.logical_and(j == npairs * nj - 1, k == nt - 1))
        def _():
            for jj in range(max(npairs * nj - 2, 0), npairs * nj):
                send(0, jj % 2, 0).wait_send()
            for p in range(npairs):
                _remote(sibs[p], sibs[p], ssem.at[0], rsem.at[p], sibling).wait_recv()

    in_specs, args = [], []
    for p, (a, b) in enumerate(pairs):
        on = lambda j, p=p: j // nj == p
        in_specs += [pl.BlockSpec((tt, ka), lambda j, k, on=on: (jnp.where(on(j), k, 0), 0)),
                     pl.BlockSpec((tt, tn), lambda j, k, on=on, p=p: (jnp.where(on(j), k, 0), jnp.clip(j - p * nj, 0, nj - 1)))]
        args += [a, b]
    mine_spec = lambda p: pl.BlockSpec(tile, (lambda j, k: (0, jnp.clip(j - p * nj, 0, nj - 1))) if col
                                       else (lambda j, k: (0, 0, jnp.clip(j - p * nj, 0, nj - 1))))
    res = _call(
        body, name=name, grid=(npairs * nj, nt), out_shape=(S(half, BF),) * (2 * npairs),
        in_specs=in_specs, out_specs=tuple(mine_spec(p) for p in range(npairs)) + (ANY,) * npairs,
        scratch_shapes=[pltpu.VMEM((ka, tn), F32), pltpu.VMEM((2,) + tile, BF), pltpu.SemaphoreType.DMA((2,)),
                        pltpu.SemaphoreType.DMA((npairs,))],
        sem=("arbitrary", "arbitrary"), args=args, hook=hook)
    return [(res[p], res[npairs + p]) for p in range(npairs)]


def _xattn_bwd(dxo, x, gn, q, kv, wq, wo, hook=None):
    t, d = x.shape
    tm = 256
    hd = d // MEM_HEADS
    nkv = kv.shape[0]

    def body(dxo_ref, x_ref, gn_ref, q_ref, kv_ref, wq_ref, wo_ref, dx_ref, dgn_ref, dkv_ref, dxh_ref, h_ref, dq_ref):
        @pl.when(pl.program_id(0) == 0)
        def _():
            dgn_ref[...] = jnp.zeros_like(dgn_ref)
            dkv_ref[...] = jnp.zeros_like(dkv_ref)
        dxov = dxo_ref[...]
        dxh = dxov.astype(BF)
        dxh_ref[...] = dxh
        do = _nt(dxh, wo_ref[...]).astype(BF)
        dqs = []
        for hh in range(MEM_HEADS):
            lo, hi = hh * hd, (hh + 1) * hd
            qh = q_ref[:, lo:hi]
            kh = kv_ref[:, lo:hi]
            vh = kv_ref[:, d + lo:d + hi]
            doh = do[:, lo:hi]
            p = _xattn_probs(qh, kh)
            dp = _nt(doh, vh)
            ds = (p * (dp - jnp.sum(p * dp, axis=-1, keepdims=True)) * (hd ** -0.5)).astype(BF)
            dqs.append(_nn(ds, kh))
            dkv_ref[:, lo:hi] += _tn(ds, qh)
            dkv_ref[:, d + lo:d + hi] += _tn(p.astype(BF), doh)
        dq = jnp.concatenate(dqs, axis=1).astype(BF)
        dq_ref[...] = dq
        n, r = _rms(x_ref[...])
        gnv = gn_ref[...]
        h_ref[...] = (n * gnv).astype(BF)
        dx, dgn = _rms_bwd(_nt(dq, wq_ref[...]), n, r, gnv)
        dx_ref[...] = dxov + dx
        dgn_ref[...] += dgn

    row = pl.BlockSpec((tm, d), lambda i: (i, 0))
    return _call(
        body, name="xattn_bwd", grid=(t // tm,),
        out_shape=(S((t, d), F32), S((1, d), F32), S((nkv, 2 * d), F32), S((t, d), BF), S((t, d), BF), S((t, d), BF)),
        in_specs=[row, row, _resident((1, d)), row, _resident(kv.shape), _resident(wq.shape), _resident(wo.shape)],
        out_specs=(row, _acc_spec((1, d)), _acc_spec((nkv, 2 * d)), row, row, row),
        sem=("arbitrary",), args=(dxo, x, gn, q, kv, wq, wo), hook=hook)


def _mem_bwd(dkv, mh, mem, gm, wkv):
    m, d = mem.shape
    rows, cols = wkv.shape
    rh = rows // 2

    def body(dkv_ref, mh_ref, mem_ref, gm_ref, w_ref, mine_ref, sib_ref, dgm_ref, whole, ssem, rsem):
        dkvb = dkv_ref[...].astype(BF)
        dmh = _nt(dkvb, w_ref[...])
        n, _ = _rms(mem_ref[...])
        x, y, c = _place()
        whole[...] = _tn(mh_ref[...], dkvb).astype(BF)
        cp = _remote(whole.at[pl.ds(pl.multiple_of((1 - c) * rh, 16), rh), :], sib_ref, ssem, rsem, (x, y, 1 - c))
        cp.start()
        mine_ref[...] = whole[pl.ds(pl.multiple_of(c * rh, 16), rh), :]
        cp.wait()
        dgm_ref[...] = jnp.sum(dmh * n, axis=0, keepdims=True)

    mine, sib, dgm = pl.pallas_call(
        body, name="mem_bwd", out_shape=(S((rh, cols), BF), S((rh, cols), BF), S((1, d), F32)),
        in_specs=[VMEM_SPEC] * 5, out_specs=(VMEM_SPEC, ANY, VMEM_SPEC),
        scratch_shapes=[pltpu.VMEM((rows, cols), BF), pltpu.SemaphoreType.DMA, pltpu.SemaphoreType.DMA],
        compiler_params=_params())(dkv, mh, mem, gm, wkv)
    return (mine, sib), dgm


def _mix_out_bwd(dxo, e, attn, conv_w, w_out, nqkv, hook=None):
    t, d = attn.shape
    tm = 256
    nt = t // tm
    f32 = lambda ref: ref[...].astype(F32)

    def body(dxo_ref, dxn_ref, c_ref, b_ref, u_ref, ga_ref, gc_ref, ch_ref, uh_ref, bn_ref, gcn_ref,
             attn_ref, cw_ref, w_ref, dattn_ref, de_ref, dcw_ref, dxh_ref):
        i = pl.program_id(0)

        @pl.when(i == 0)
        def _():
            dcw_ref[...] = jnp.zeros_like(dcw_ref)
        dxh = dxo_ref[...].astype(BF)
        dxh_ref[...] = dxh
        w = w_ref[...]
        dm = _nt(dxh, w)
        dmn = _nt(dxn_ref[...].astype(BF), w)
        cv, bv, uv = f32(c_ref), f32(b_ref), f32(u_ref)
        sga = _sigmoid(f32(ga_ref))
        sgc = _sigmoid(f32(gc_ref))
        z = cv * uv
        z1, z2 = _conv_taps(z, f32(ch_ref) * f32(uh_ref), i == 0)
        w0, w1, w2 = cw_ref[0:1, :], cw_ref[1:2, :], cw_ref[2:3, :]
        s = w0 * z2 + w1 * z1 + w2 * z
        av = attn_ref[...]
        dattn_ref[...] = (dm * sga).astype(BF)
        dconv = dm * sgc
        ds = dconv * bv
        dsn = jnp.where(i == nt - 1, 0.0, dmn * _sigmoid(gcn_ref[0:8, :].astype(F32)) * bn_ref[0:8, :].astype(F32))
        row = lax.broadcasted_iota(I32, (tm, 1), 0)
        dsp1 = jnp.where(row == tm - 1, dsn[0:1, :], pltpu.roll(ds, tm - 1, 0))
        dsp2 = jnp.where(row == tm - 2, dsn[0:1, :], jnp.where(row == tm - 1, dsn[1:2, :], pltpu.roll(ds, tm - 2, 0)))
        dz = w2 * ds + w1 * dsp1 + w0 * dsp2
        de_ref[:, nqkv:nqkv + d] = (dz * uv).astype(BF)
        de_ref[:, nqkv + d:nqkv + 2 * d] = (dconv * s).astype(BF)
        de_ref[:, nqkv + 2 * d:nqkv + 3 * d] = (dz * cv).astype(BF)
        de_ref[:, nqkv + 3 * d:nqkv + 4 * d] = (dm * av * sga * (1.0 - sga)).astype(BF)
        de_ref[:, nqkv + 4 * d:nqkv + 5 * d] = (dm * (bv * s) * sgc * (1.0 - sgc)).astype(BF)
        dcw_ref[0:1, :] += jnp.sum(ds * z2, axis=0, keepdims=True)
        dcw_ref[1:2, :] += jnp.sum(ds * z1, axis=0, keepdims=True)
        dcw_ref[2:3, :] += jnp.sum(ds * z, axis=0, keepdims=True)

    ecol = lambda cb: pl.BlockSpec((tm, d), lambda i: (i, cb))
    prev = lambda cb: pl.BlockSpec((HALO, d), lambda i: (jnp.maximum(i * (tm // HALO) - 1, 0), cb))
    nxt = lambda rows, cb: pl.BlockSpec((rows, d), lambda i: (jnp.minimum((i + 1) * (tm // rows), t // rows - 1), cb))
    row = pl.BlockSpec((tm, d), lambda i: (i, 0))
    return _call(
        body, name="mix_out_bwd", grid=(nt,),
        out_shape=(S((t, d), BF), S((t, nqkv + 5 * d), BF), S((8, d), F32), S((t, d), BF)),
        in_specs=[row, nxt(8, 0), ecol(0), ecol(1), ecol(2), ecol(3), ecol(4), prev(0), prev(2), nxt(HALO, 1), nxt(HALO, 4),
                  row, _resident(conv_w.shape), _resident(w_out.shape)],
        out_specs=(row, pl.BlockSpec((tm, nqkv + 5 * d), lambda i: (i, 0)), _acc_spec((8, d)), row),
        sem=("arbitrary",), args=(dxo, dxo, e, e, e, e, e, e, e, e, e, attn, conv_w, w_out), hook=hook)


def _swa_bwd(qkv, dattn, pos_col, pos_row, bias_t, sinks, dp, hook=None):
    t = qkv.shape[0]
    nb = t // BLOCK
    qw = SWA_HEADS * HEAD_DIM
    kw = SWA_KV_HEADS * HEAD_DIM

    def body(q_ref, kp_ref, kc_ref, vp_ref, vc_ref, do_ref, pq_ref, pkp_ref, pkc_ref, bias_ref, sink_ref,
             dp_in, dp_ref, gb_ref, dsk_ref, carry_ref, dq_ref):
        b = pl.program_id(0)

        @pl.when(b == 0)
        def _():
            gb_ref[...] = jnp.zeros_like(gb_ref)
            dsk_ref[...] = jnp.zeros_like(dsk_ref)
            carry_ref[...] = jnp.zeros_like(carry_ref)
            dq_ref[...] = jnp.zeros_like(dq_ref)
        dp_ref[:, 0:qw] = dq_ref[...]

        @pl.when(b < nb)
        def _():
            vis = _swa_visible(b, pq_ref, pkp_ref, pkc_ref)
            k2 = jnp.concatenate([kp_ref[...], kc_ref[...]], axis=0)
            v2 = jnp.concatenate([vp_ref[...], vc_ref[...]], axis=0)
            for hk in range(SWA_KV_HEADS):
                lo, hi = hk * HEAD_DIM, (hk + 1) * HEAD_DIM
                kh = k2[:, lo:hi]
                vh = v2[:, lo:hi]
                qg = _group_heads(q_ref, hk)
                dog = _group_heads(do_ref, hk)
                pn, psn = _swa_probs(qg, kh, vis, bias_ref[hk], _group_sinks(sink_ref, hk))
                dp = _nt(dog, vh)
                delta = jnp.sum(pn * dp, axis=-1, keepdims=True)
                ds = pn * (dp - delta)
                gb_ref[hk] += ds
                dsk_ref[hk] += -psn * delta
                dsb = (ds * (HEAD_DIM ** -0.5)).astype(BF)
                dqg = _nn(dsb, kh).astype(BF)
                for g in range(SWA_GROUP):
                    h = hk * SWA_GROUP + g
                    dq_ref[:, h * HEAD_DIM:(h + 1) * HEAD_DIM] = dqg[g * BLOCK:(g + 1) * BLOCK]
                dk = _tn(dsb, qg)
                dv = _tn(pn.astype(BF), dog)
                dp_ref[:, qw + lo:qw + hi] = (carry_ref[:, lo:hi] + dk[0:BLOCK]).astype(BF)
                dp_ref[:, qw + kw + lo:qw + kw + hi] = (carry_ref[:, kw + lo:kw + hi] + dv[0:BLOCK]).astype(BF)
                carry_ref[:, lo:hi] = dk[BLOCK:2 * BLOCK]
                carry_ref[:, kw + lo:kw + hi] = dv[BLOCK:2 * BLOCK]

        @pl.when(b == nb)
        def _():
            dp_ref[:, qw:qw + 2 * kw] = carry_ref[...].astype(BF)

    cur = lambda b: jnp.minimum(b, nb - 1)
    prev = lambda b: jnp.maximum(cur(b) - 1, 0)
    return _call(
        body, name="swa_bwd", grid=(nb + 1,),
        out_shape=(S(dp.shape, BF), S((SWA_KV_HEADS, GROUP_ROWS, 2 * BLOCK), F32), S((SWA_KV_HEADS, GROUP_ROWS, 1), F32)),
        in_specs=[
            pl.BlockSpec((BLOCK, qw), lambda b: (cur(b), 0)),
            pl.BlockSpec((BLOCK, kw), lambda b: (prev(b), qw // kw)),
            pl.BlockSpec((BLOCK, kw), lambda b: (cur(b), qw // kw)),
            pl.BlockSpec((BLOCK, kw), lambda b: (prev(b), qw // kw + 1)),
            pl.BlockSpec((BLOCK, kw), lambda b: (cur(b), qw // kw + 1)),
            pl.BlockSpec((BLOCK, qw), lambda b: (cur(b), 0)),
            pl.BlockSpec((BLOCK, 1), lambda b: (cur(b), 0)),
            pl.BlockSpec((1, BLOCK), lambda b: (0, prev(b))),
            pl.BlockSpec((1, BLOCK), lambda b: (0, cur(b))),
            _resident(bias_t.shape),
            SMEM_SPEC,
            ANY,
        ],
        out_specs=(
            pl.BlockSpec((BLOCK, qw + 2 * kw), lambda b: (jnp.maximum(b - 1, 0), 0)),
            _acc_spec((SWA_KV_HEADS, GROUP_ROWS, 2 * BLOCK)),
            _acc_spec((SWA_KV_HEADS, GROUP_ROWS, 1)),
        ),
        scratch_shapes=[pltpu.VMEM((BLOCK, 2 * kw), F32), pltpu.VMEM((BLOCK, qw), BF)],
        aliases={11: 0}, sem=("arbitrary",),
        args=(qkv, qkv, qkv, qkv, qkv, dattn, pos_col, pos_row, pos_row, bias_t, sinks, dp), hook=hook)


def _bias_reduce(gb, dsk):
    def body(gb_ref, dsk_ref, drb_ref, dsink_ref):
        bucket = _t5_bucket(_block_rel())
        for b in range(REL_BUCKETS):
            mask = bucket == b
            for h in range(SWA_HEADS):
                drb_ref[b, h] = jnp.sum(jnp.where(mask, gb_ref[h], 0.0))
        for h in range(SWA_HEADS):
            dsink_ref[0, h] = jnp.sum(dsk_ref[h])

    return pl.pallas_call(
        body, name="bias_reduce", out_shape=(S((REL_BUCKETS, SWA_HEADS), F32), S((1, SWA_HEADS), F32)),
        in_specs=[VMEM_SPEC, VMEM_SPEC], out_specs=(SMEM_SPEC, SMEM_SPEC),
    )(gb, dsk)


def _mix_in_bwd(dp, x, gn, w_in, dxo):
    t, d = x.shape
    tm = 256
    npr = dp.shape[1]

    def body(dp_ref, x_ref, gn_ref, w_ref, dxo_ref, dx_ref, dgn_ref, h_ref):
        @pl.when(pl.program_id(0) == 0)
        def _():
            dgn_ref[...] = jnp.zeros_like(dgn_ref)
        dh = jnp.zeros((tm, d), F32)
        for c0 in range(0, npr, 1024):
            c1 = min(c0 + 1024, npr)
            dh = dh + _nt(dp_ref[:, c0:c1], w_ref[:, c0:c1])
        n, r = _rms(x_ref[...])
        gnv = gn_ref[...]
        h_ref[...] = (n * gnv).astype(BF)
        dx, dgn = _rms_bwd(dh, n, r, gnv)
        dx_ref[...] = dxo_ref[...] + dx
        dgn_ref[...] += dgn

    row = pl.BlockSpec((tm, d), lambda i: (i, 0))
    return pl.pallas_call(
        body, name="mix_in_bwd", grid=(t // tm,),
        out_shape=(S((t, d), F32), S((1, d), F32), S((t, d), BF)),
        in_specs=[pl.BlockSpec((tm, npr), lambda i: (i, 0)), row, _resident((1, d)), _resident(w_in.shape), row],
        out_specs=(row, _acc_spec((1, d)), row),
        compiler_params=_params(("arbitrary",)),
    )(dp, x, gn, w_in, dxo)


CAST_STEPS = 4


def _to_bf16(arrays, name, hook=None):
    def body(*refs):
        for src, dst in zip(refs[:len(arrays)], refs[len(arrays):]):
            dst[...] = src[...].astype(BF)

    blocks = [pl.BlockSpec((a.shape[0] // CAST_STEPS, a.shape[1]), lambda i: (i, 0)) for a in arrays]
    return _call(body, name=name, grid=(CAST_STEPS,), out_shape=tuple(S(a.shape, BF) for a in arrays), in_specs=blocks,
                 out_specs=tuple(blocks), sem=("parallel",), args=list(arrays), hook=hook)


PAIR_STEPS = 4


def _add_bf16(pairs, name):
    def body(*refs):
        ins, outs = refs[:2 * len(pairs)], refs[2 * len(pairs):]
        for q, o_ref in enumerate(outs):
            o_ref[...] = (ins[2 * q][...].astype(F32) + ins[2 * q + 1][...].astype(F32)).astype(BF)

    in_specs, out_specs, out_shape, args = [], [], [], []
    for a, b in pairs:
        rows, cols = a.shape
        blk = pl.BlockSpec((rows // PAIR_STEPS, cols), lambda i: (i, 0))
        in_specs += [blk, blk]
        out_specs.append(blk)
        out_shape.append(S((rows, cols), BF))
        args += [a, b]
    return pl.pallas_call(body, name=name, grid=(PAIR_STEPS,), out_shape=tuple(out_shape), in_specs=in_specs,
                          out_specs=tuple(out_specs), compiler_params=_params(("parallel",)))(*args)


def _adam_update(w, g, m, v):
    mn = ADAM_B1 * m + (1.0 - ADAM_B1) * g
    vn = ADAM_B2 * v + (1.0 - ADAM_B2) * (g * g)
    m_hat = mn / (1.0 - ADAM_B1 ** ADAM_STEP)
    v_hat = vn / (1.0 - ADAM_B2 ** ADAM_STEP)
    return -ADAM_LR * (m_hat / (jnp.sqrt(v_hat) + ADAM_EPS) + ADAM_WD * w), mn, vn


def _adamw(quads, name, steps, echo=False, hook=None):
    n_out = 4 if echo else 3

    def body(*refs):
        ins, outs = refs[:4 * len(quads)], refs[4 * len(quads):]
        for q in range(len(quads)):
            w_ref, g_ref, m_ref, v_ref = ins[4 * q:4 * q + 4]
            res = outs[n_out * q:n_out * q + n_out]
            gv = g_ref[...]
            if echo:
                res[0][...] = gv
            res[-3][...], res[-2][...], res[-1][...] = _adam_update(w_ref[...], gv, m_ref[...], v_ref[...])

    in_specs, out_specs, out_shape, args = [], [], [], []
    for quad in quads:
        rows, cols = quad[0].shape
        blk = pl.BlockSpec((rows // steps, cols), lambda i: (i, 0))
        in_specs += [blk] * 4
        out_specs += [blk] * n_out
        out_shape += [S((rows, cols), F32)] * n_out
        args += list(quad)
    res = _call(body, name=name, grid=(steps,), out_shape=tuple(out_shape), in_specs=in_specs, out_specs=tuple(out_specs),
                sem=("parallel",), args=args, hook=hook)
    return [res[n_out * q:n_out * q + n_out] for q in range(len(quads))]


def _place():
    x, y, c = lax.axis_index("x"), lax.axis_index("y"), lax.axis_index("c")
    return x, y, c


OTHER_CHIPS = ((1, 0), (0, 1), (1, 1))


def _flip(v, f):
    return 1 - v if f else v


def _remote(src, dst, ssem, rsem, dev):
    return pltpu.make_async_remote_copy(src_ref=src, dst_ref=dst, send_sem=ssem, recv_sem=rsem,
                                        device_id=dev, device_id_type=MESH)


class _Both:
    def __init__(self, hooks):
        self.hooks = hooks
        self.ins = [a for h in hooks for a in h.ins]
        self.out_shape = tuple(o for h in hooks for o in h.out_shape)
        self.scratch = [x for h in hooks for x in h.scratch]

    def _each(self, ins, outs, scr):
        i = o = s = 0
        for h in self.hooks:
            ni, no, ns = len(h.ins), len(h.out_shape), len(h.scratch)
            yield h, ins[i:i + ni], outs[o:o + no], scr[s:s + ns]
            i, o, s = i + ni, o + no, s + ns

    def start(self, ins, outs, scr):
        for h, *refs in self._each(ins, outs, scr):
            h.start(*refs)

    def relay(self, ins, outs, scr):
        for h, *refs in self._each(ins, outs, scr):
            h.relay(*refs)

    def finish(self, ins, outs, scr):
        for h, *refs in self._each(ins, outs, scr):
            h.finish(*refs)

    @property
    def lead(self):
        return max(h.lead for h in self.hooks)

    @property
    def results(self):
        return [h.results for h in self.hooks]

    @results.setter
    def results(self, res):
        for h, _, mine, _ in self._each((), tuple(res), ()):
            h.results = mine


class _GatherHook:
    def __init__(self, shards, kinds, lead=2):
        self.ins, self.kinds, self.lead, n = list(shards), list(kinds), lead, len(shards)
        self.out_shape = tuple(
            S((w.shape[0], 4 * w.shape[1]), BF) if k == "col" else S((4,) + w.shape, BF) for w, k in zip(shards, kinds))
        dma = pltpu.SemaphoreType.DMA
        self.scratch = ([dma((n, 3)) for _ in range(5)] + [dma((n,)), dma((n,))]
                        + [pltpu.VMEM((3, w.shape[0] // 2, w.shape[1]), BF) for w in shards]
                        + [pltpu.VMEM(w.shape, BF) for w in shards])

    def _window(self, outs, i, s, half):
        rows, cols = self.ins[i].shape
        rh = rows // 2
        start = pl.multiple_of(half * rh, 16)
        if self.kinds[i] == "col":
            return outs[i].at[pl.ds(start, rh), pl.ds(pl.multiple_of(s * cols, 128), cols)]
        return outs[i].at[s, pl.ds(start, rh), :]

    def _copies(self, ins, outs, scr):
        n = len(ins)
        ssem, rsem, fssem, frsem, ksem, lsem, osem = scr[:7]
        land, own = scr[7:7 + n], scr[7 + n:7 + 2 * n]
        x, y, c = _place()
        sibling = (x, y, 1 - c)
        loads, stores, sends, forwards, keeps, passed = [], [], [], [], [], []
        for i in range(n):
            rows, cols = self.ins[i].shape
            rh = rows // 2
            mine = (outs[i].at[:, pl.ds(pl.multiple_of((2 * x + y) * cols, 128), cols)] if self.kinds[i] == "col"
                    else outs[i].at[2 * x + y])
            loads.append(functools.partial(pltpu.make_async_copy, ins[i], own[i], lsem.at[i]))
            stores.append(functools.partial(pltpu.make_async_copy, own[i], mine, osem.at[i]))
            src = ins[i].at[pl.ds(pl.multiple_of(c * rh, 16), rh), :]
            for j, (fx, fy) in enumerate(OTHER_CHIPS):
                px, py = _flip(x, fx), _flip(y, fy)
                sends.append(functools.partial(_remote, src, land[i].at[j], ssem.at[i, j], rsem.at[i, j], (px, py, c)))
                here = self._window(outs, i, 2 * px + py, c)
                forwards.append(functools.partial(_remote, land[i].at[j], here, fssem.at[i, j], frsem.at[i, j], sibling))
                keeps.append(functools.partial(pltpu.make_async_copy, land[i].at[j], here, ksem.at[i, j]))
                there = self._window(outs, i, 2 * px + py, 1 - c)
                passed.append(functools.partial(_remote, there, there, fssem.at[i, j], frsem.at[i, j], sibling))
        return loads, stores, sends, forwards, keeps, passed

    def start(self, ins, outs, scr):
        loads, _, sends, _, _, _ = self._copies(ins, outs, scr)
        for make in sends + loads:
            make().start()

    def relay(self, ins, outs, scr):
        loads, stores, sends, forwards, keeps, _ = self._copies(ins, outs, scr)
        for load, store in zip(loads, stores):
            load().wait()
            store().start()
        for send, forward, keep in zip(sends, forwards, keeps):
            send().wait_recv()
            forward().start()
            keep().start()

    def finish(self, ins, outs, scr):
        _, stores, sends, forwards, keeps, passed = self._copies(ins, outs, scr)
        for make in passed:
            make().wait_recv()
        for make in sends + forwards:
            make().wait_send()
        for make in keeps + stores:
            make().wait()


class _ChipsHook:
    def __init__(self, parts, kinds):
        self.ins, self.kinds, n = list(parts), list(kinds), len(parts)
        self.out_shape = tuple(
            S((4, p.shape[0], p.shape[1] // 4), BF) if k == "col" else S(p.shape, BF) for p, k in zip(parts, kinds))
        dma = pltpu.SemaphoreType.DMA
        self.scratch = ([dma((n, 3)), dma((n, 3)), dma((n,)), dma((n,))]
                        + [pltpu.VMEM(o.shape[1:], BF) for o in self.out_shape])

    def _slab(self, ins, i, s):
        _, rows, cols = self.out_shape[i].shape
        if self.kinds[i] == "col":
            return ins[i].at[:, pl.ds(pl.multiple_of(s * cols, 128), cols)]
        return ins[i].at[s]

    def _copies(self, ins, outs, scr):
        ssem, rsem, lsem, osem = scr[:4]
        own = scr[4:]
        x, y, c = _place()
        loads, stores, sends = [], [], []
        for i in range(len(ins)):
            loads.append(functools.partial(pltpu.make_async_copy, self._slab(ins, i, 2 * x + y), own[i], lsem.at[i]))
            stores.append(functools.partial(pltpu.make_async_copy, own[i], outs[i].at[3], osem.at[i]))
            for j, (fx, fy) in enumerate(OTHER_CHIPS):
                px, py = _flip(x, fx), _flip(y, fy)
                sends.append(functools.partial(_remote, self._slab(ins, i, 2 * px + py), outs[i].at[j], ssem.at[i, j],
                                               rsem.at[i, j], (px, py, c)))
        return loads, stores, sends

    def start(self, ins, outs, scr):
        loads, _, sends = self._copies(ins, outs, scr)
        for make in sends + loads:
            make().start()

    lead = 2

    def relay(self, ins, outs, scr):
        loads, stores, _ = self._copies(ins, outs, scr)
        for load, store in zip(loads, stores):
            load().wait()
            store().start()

    def finish(self, ins, outs, scr):
        _, stores, sends = self._copies(ins, outs, scr)
        for make in sends + stores:
            make().wait()


SHARE_STEPS = 2


def _sum_share(slab_list, name, hook=None):
    n = len(slab_list)
    geom = [(sl.shape[1], sl.shape[1] // SHARE_STEPS, sl.shape[2]) for sl in slab_list]

    def body(*refs):
        ins, outs, scr = refs[:n], refs[n:2 * n], refs[2 * n:]
        i = pl.program_id(0)
        x, y, c = _place()
        sibling = (x, y, 1 - c)

        def copies(q, k):
            rh, tr, _ = geom[q]
            stage, lsem, ssem, rsem = scr[4 * q:4 * q + 4]
            dst = outs[q].at[pl.ds(pl.multiple_of(c * rh + k * tr, 8), tr), :]
            return (pltpu.make_async_copy(stage.at[k], dst, lsem.at[k]),
                    _remote(stage.at[k], dst, ssem.at[k], rsem, sibling))

        for q in range(n):
            acc = ins[q][3].astype(F32)
            for k in range(3):
                acc = acc + ins[q][k].astype(F32)
            scr[4 * q][i] = acc
            for cp in copies(q, i):
                cp.start()

        @pl.when(i == SHARE_STEPS - 1)
        def _():
            for q in range(n):
                rh = geom[q][0]
                for k in range(SHARE_STEPS):
                    local, remote = copies(q, k)
                    local.wait()
                    remote.wait_send()
                got = outs[q].at[pl.ds(pl.multiple_of((1 - c) * rh, 8), rh), :]
                _remote(got, got, scr[4 * q + 2].at[0], scr[4 * q + 3], sibling).wait_recv()

    dma = pltpu.SemaphoreType.DMA
    scratch = []
    for rh, tr, cols in geom:
        scratch += [pltpu.VMEM((SHARE_STEPS, tr, cols), F32), dma((SHARE_STEPS,)), dma((SHARE_STEPS,)), dma]
    return _call(
        body, name=name, grid=(SHARE_STEPS,), out_shape=tuple(S((2 * rh, cols), F32) for rh, _, cols in geom),
        in_specs=[pl.BlockSpec((4, tr, cols), lambda i: (0, i, 0)) for _, tr, cols in geom], out_specs=(ANY,) * n,
        scratch_shapes=scratch, sem=("arbitrary",), args=list(slab_list), hook=hook)


class _SmallSumHook:
    def __init__(self, buf):
        self.ins, self.out_shape = [buf], (S(buf.shape, F32),)
        dma = pltpu.SemaphoreType.DMA
        self.scratch = [pltpu.VMEM((8,) + buf.shape, F32), pltpu.VMEM(buf.shape, F32), dma((7,)), dma((7,)), dma]

    def _sends(self, scr):
        slots, _, ssem, rsem, _ = scr
        x, y, c = _place()
        me = 4 * x + 2 * y + c
        for r in range(1, 8):
            px, py, pc = _flip(x, (r >> 2) & 1), _flip(y, (r >> 1) & 1), _flip(c, r & 1)
            yield (functools.partial(_remote, slots.at[me], slots.at[me], ssem.at[r - 1], rsem.at[r - 1], (px, py, pc)),
                   functools.partial(_remote, slots.at[me], slots.at[4 * px + 2 * py + pc], ssem.at[r - 1],
                                     rsem.at[r - 1], (px, py, pc)))

    def start(self, ins, outs, scr):
        slots, _, _, _, lsem = scr
        x, y, c = _place()
        load = pltpu.make_async_copy(ins[0], slots.at[4 * x + 2 * y + c], lsem)
        load.start()
        load.wait()
        for send, _ in self._sends(scr):
            send().start()

    lead = 0

    def relay(self, ins, outs, scr):
        pass

    def finish(self, ins, outs, scr):
        slots, total, _, _, lsem = scr
        for _, arrival in self._sends(scr):
            arrival().wait_recv()
        for send, _ in self._sends(scr):
            send().wait_send()
        acc = slots[0]
        for k in range(1, 8):
            acc = acc + slots[k]
        total[...] = acc
        store = pltpu.make_async_copy(total, outs[0], lsem)
        store.start()
        store.wait()


BIG = ("ffn1_w_gu", "ffn1_w_down", "w_in", "w_out", "xattn_wq", "xattn_wkv", "xattn_wo", "ffn2_w_gu", "ffn2_w_down")
KIND = {"ffn1_w_gu": "col", "ffn1_w_down": "row", "w_in": "col", "w_out": "row", "xattn_wq": "row",
        "xattn_wkv": "col", "xattn_wo": "row", "ffn2_w_gu": "col", "ffn2_w_down": "row"}
WEIGHTS = ("rel_bias", "ffn1_norm", "ffn1_w_gu", "ffn1_w_down", "mix_norm", "w_in", "sinks", "conv_w", "w_out",
           "xattn_norm", "mem_norm", "xattn_wq", "xattn_wkv", "xattn_wo", "ffn2_norm", "ffn2_w_gu", "ffn2_w_down",
           "final_norm")
SMALL_ROWS = 16
GAIN_ROW = {"ffn1_norm": 0, "mix_norm": 1, "xattn_norm": 2, "mem_norm": 3, "ffn2_norm": 4, "final_norm": 5}
CONV_ROW, SINK_ROW, BIAS_ROW, LOSS_ROW = 6, 9, 10, 11
TAP_ROWS = 8


def _rows_block(rows, d):
    buf = jnp.zeros((SMALL_ROWS, d), F32)
    for r, v in rows.items():
        buf = lax.dynamic_update_slice(buf, v if v.ndim == 2 else v.reshape(1, -1), (r, 0))
    return buf


def _adamw_small(gsum, conv_g, bias_g, small):
    names = list(small)

    def grad(k, gsum_ref, conv_ref, bias_ref):
        if k in GAIN_ROW:
            return gsum_ref[GAIN_ROW[k]:GAIN_ROW[k] + 1, :]
        if k == "sinks":
            return gsum_ref[SINK_ROW:SINK_ROW + 1, 0:small[k][0].shape[1]]
        return conv_ref[...] if k == "conv_w" else bias_ref[...]

    def body(gsum_ref, conv_ref, bias_ref, *refs):
        ins, outs = refs[:3 * len(names)], refs[3 * len(names):]
        for q, k in enumerate(names):
            w_ref, m_ref, v_ref = ins[3 * q:3 * q + 3]
            g_ref, d_ref, mo_ref, vo_ref = outs[4 * q:4 * q + 4]
            gv = grad(k, gsum_ref, conv_ref, bias_ref)
            g_ref[...] = gv
            d_ref[...], mo_ref[...], vo_ref[...] = _adam_update(w_ref[...], gv, m_ref[...], v_ref[...])

    res = pl.pallas_call(
        body, name="adamw_small", out_shape=tuple(S(small[k][0].shape, F32) for k in names for _ in range(4)),
        compiler_params=_params())(gsum, conv_g, bias_g, *[a for k in names for a in small[k]])
    return {k: res[4 * q:4 * q + 4] for q, k in enumerate(names)}


def _local_step(x, mem, pos, target, w, gains, rel_bias, sinks, conv_w, shards):
    t, d = x.shape
    w = dict(w)
    grads, slabs = {}, {}
    pos_col = pos.reshape(t, 1)
    pos_row = pos.reshape(1, t)
    bias_t = _bias_build(rel_bias).reshape(SWA_KV_HEADS, GROUP_ROWS, 2 * BLOCK)

    def gather(names, lead=2):
        return _GatherHook([shards[k] for k in names], [KIND[k] for k in names], lead)

    def gathered(names, hook):
        for k, gw in zip(names, hook.results):
            w[k] = gw if KIND[k] == "col" else gw.reshape(-1, gw.shape[-1])

    def dw(problems, tn, name, hook=None):
        res = _dw_pair([(a, b) for _, a, b in problems], tn, name, KIND[problems[0][0]], hook)
        grads.update(zip((k for k, _, _ in problems), res))

    def pair_sums(names):
        flat = lambda k, v: v.reshape(-1, v.shape[-1]) if KIND[k] == "row" else v
        sums = _add_bf16([(flat(k, grads[k][0]), flat(k, grads[k][1])) for k in names], "pair_sum_" + names[0])
        return {k: (p if KIND[k] == "col" else p.reshape(4, -1, p.shape[-1])) for k, p in zip(names, sums)}

    def chips(names, parts):
        return _ChipsHook([parts[k] for k in names], [KIND[k] for k in names])

    def reduced(names, hook):
        slabs.update(zip(names, hook.results))

    names = ("w_in", "w_out")
    hook = gather(names, lead=1)
    x1, g1, u1 = _ffn_fwd(x, gains["ffn1_norm"], w["ffn1_w_gu"], w["ffn1_w_down"], "ffn1_fwd", hook)
    gathered(names, hook)
    names = ("xattn_wq", "xattn_wkv", "xattn_wo")
    hook = gather(names)
    qkv, e = _mix_proj(x1, gains["mix_norm"], w["w_in"], hook)
    gathered(names, hook)
    names = ("ffn2_w_gu",)
    hook = gather(names)
    attn = _swa_fwd(qkv, pos_col, pos_row, bias_t, sinks, hook)
    gathered(names, hook)
    names = ("ffn2_w_down",)
    hook = gather(names)
    x2, merged = _mix_out_fwd(e, attn, conv_w, w["w_out"], x1, hook)
    gathered(names, hook)
    mh, kv = _mem_kv(mem, gains["mem_norm"], w["xattn_wkv"])
    x3, qx, o = _xattn_fwd(x2, gains["xattn_norm"], w["xattn_wq"], kv, w["xattn_wo"])
    dx4, g2, u2, d_final, loss = _ffn_fwd(x3, gains["ffn2_norm"], w["ffn2_w_gu"], w["ffn2_w_down"], "ffn2_fwd",
                                          head=(gains["final_norm"], target))

    dx3, d_ffn2, dgu2, a2, h4, dyh4 = _ffn_bwd(dx4, x3, gains["ffn2_norm"], g2, u2, w["ffn2_w_gu"], w["ffn2_w_down"],
                                               "ffn2_bwd")
    dw([("ffn2_w_gu", h4, dgu2)], 1408, "dw_ffn2_gu")
    dw([("ffn2_w_down", a2, dyh4)], 512, "dw_ffn2_down")
    parts = pair_sums(("ffn2_w_gu", "ffn2_w_down"))
    hook = chips(("ffn2_w_gu",), parts)
    dx2, d_xattn, dkv, dxh3, h3, dqx = _xattn_bwd(dx3, x2, gains["xattn_norm"], qx, kv, w["xattn_wq"], w["xattn_wo"], hook)
    reduced(("ffn2_w_gu",), hook)
    grads["xattn_wkv"], d_mem = _mem_bwd(dkv, mh, mem, gains["mem_norm"], w["xattn_wkv"])
    hook = chips(("ffn2_w_down",), parts)
    dattn, dp, dcw, dxh2 = _mix_out_bwd(dx2, e, attn, conv_w, w["w_out"], qkv.shape[1], hook)
    reduced(("ffn2_w_down",), hook)
    dw([("xattn_wo", o, dxh3), ("xattn_wq", h3, dqx), ("w_out", merged, dxh2)], 1024, "dw_wo_wq_wout")
    names = ("xattn_wo", "xattn_wq", "xattn_wkv", "w_out")
    hook = chips(names, pair_sums(names))
    dp, gb, dsk = _swa_bwd(qkv, dattn, pos_col, pos_row, bias_t, sinks, dp, hook)
    reduced(names, hook)
    d_rel_bias, d_sinks = _bias_reduce(gb.reshape(SWA_HEADS, BLOCK, 2 * BLOCK), dsk.reshape(SWA_HEADS, BLOCK, 1))
    dx1, d_mix, h2 = _mix_in_bwd(dp, x1, gains["mix_norm"], w["w_in"], dx2)
    dw([("w_in", h2, dp)], w["w_in"].shape[1] // 4, "dw_win")
    names = ("w_in",)
    hook = chips(names, pair_sums(names))
    dx0, d_ffn1, dgu1, a1, h1, dyh1 = _ffn_bwd(dx1, x, gains["ffn1_norm"], g1, u1, w["ffn1_w_gu"], w["ffn1_w_down"],
                                               "ffn1_bwd")
    dw([("ffn1_w_gu", h1, dgu1)], 1408, "dw_ffn1_gu", hook)
    reduced(names, hook)
    names = ("ffn1_w_gu",)
    hook = chips(names, pair_sums(names))
    dw([("ffn1_w_down", a1, dyh1)], 512, "dw_ffn1_down", hook)
    reduced(names, hook)
    rows = {0: d_ffn1, 1: d_mix, 2: d_xattn, 3: d_mem, 4: d_ffn2, 5: d_final, SINK_ROW: d_sinks,
            BIAS_ROW: d_rel_bias.reshape(1, -1), LOSS_ROW: loss[0, 0:1]}
    rows.update({CONV_ROW + j: dcw[j] for j in range(3)})
    last = ("ffn1_w_down",)
    return dx0, slabs, _rows_block(rows, d), chips(last, pair_sums(last))


def kernel(x, mem, positions, rel_bias, ffn1_norm, ffn1_w_gu, ffn1_w_down, mix_norm, w_in, sinks, conv_w, w_out, xattn_norm, mem_norm, xattn_wq, xattn_wkv, xattn_wo, ffn2_norm, ffn2_w_gu, ffn2_w_down, final_norm, loss_target, m_rel_bias, m_ffn1_norm, m_ffn1_w_gu, m_ffn1_w_down, m_mix_norm, m_w_in, m_sinks, m_conv_w, m_w_out, m_xattn_norm, m_mem_norm, m_xattn_wq, m_xattn_wkv, m_xattn_wo, m_ffn2_norm, m_ffn2_w_gu, m_ffn2_w_down, m_final_norm, v_rel_bias, v_ffn1_norm, v_ffn1_w_gu, v_ffn1_w_down, v_mix_norm, v_w_in, v_sinks, v_conv_w, v_w_out, v_xattn_norm, v_mem_norm, v_xattn_wq, v_xattn_wkv, v_xattn_wo, v_ffn2_norm, v_ffn2_w_gu, v_ffn2_w_down, v_final_norm):
    args = dict(locals())
    wts = {k: args[k] for k in WEIGHTS}
    mom = {k: args["m_" + k] for k in WEIGHTS}
    var = {k: args["v_" + k] for k in WEIGHTS}
    d = x.shape[-1]
    s_me = 2 * lax.axis_index("x") + lax.axis_index("y")

    first = ("ffn1_w_gu", "ffn1_w_down")
    rest = tuple(k for k in BIG if k not in first)
    shards = {k: wts[k][0].astype(BF) for k in first}
    cw_cols = conv_w.shape[-1]
    placed = lax.dynamic_update_slice(jnp.zeros((TAP_ROWS, d), F32), 0.5 * conv_w[0], (0, s_me * cw_cols))
    head = _Both([_GatherHook([shards[k] for k in first], [KIND[k] for k in first]), _SmallSumHook(placed)])
    shards.update(zip(rest, _to_bf16([wts[k][0] for k in rest], "gather_ffn1", head)))
    gathered, (conv_sum,) = head.results
    whole = {k: (gw if KIND[k] == "col" else gw.reshape(-1, gw.shape[-1])) for k, gw in zip(first, gathered)}
    conv_whole = conv_sum[0:3]

    gains = {k: wts[k].reshape(1, d) for k in GAIN_ROW}
    dx0, slabs, small, last_chips = _local_step(x[0], mem[0], positions[0], loss_target[0], whole, gains, rel_bias, sinks,
                                               conv_whole, shards)

    late = ("ffn1_w_gu", "ffn1_w_down", "ffn2_w_down")
    early = tuple(k for k in BIG if k not in late)
    first_ones = tuple(k for k in BIG if k != "ffn1_w_down")
    tail = _Both([last_chips, _SmallSumHook(small)])
    shard_grads = dict(zip(first_ones, _sum_share([slabs[k] for k in first_ones], "sum_share", tail)))
    (last_slabs,), (small_sum,) = tail.results
    shard_grads["ffn1_w_down"], = _sum_share([last_slabs], "sum_share_ffn1_w_down")
    quad = lambda k: (wts[k][0], shard_grads[k], mom[k][0], var[k][0])
    updates = dict(zip(early, _adamw([quad(k) for k in early], "adamw_early", 16, echo=True)))
    updates.update(zip(late, _adamw([quad(k) for k in late], "adamw_late", 8, echo=True)))
    loss = small_sum[LOSS_ROW, 0]

    out_g, out_d, out_m, out_v = {}, {}, {}, {}
    for k in BIG:
        out_g[k], out_d[k], out_m[k], out_v[k] = (a[None] for a in updates[k])

    conv_g = lax.dynamic_slice(small_sum, (CONV_ROW, s_me * cw_cols), (3, cw_cols))
    two_d = lambda a: a.reshape(a.shape[-2:]) if a.ndim > 1 else a.reshape(1, -1)
    small_names = [k for k in WEIGHTS if k not in KIND]
    bias_g = small_sum[BIAS_ROW, 0:rel_bias.size].reshape(rel_bias.shape)
    done = _adamw_small(small_sum, conv_g, bias_g, {k: (two_d(wts[k]), two_d(mom[k]), two_d(var[k])) for k in small_names})
    for k in small_names:
        out_g[k], out_d[k], out_m[k], out_v[k] = (a.reshape(wts[k].shape) for a in done[k])

    return (loss, dx0[None], *[out_g[k] for k in WEIGHTS], *[out_d[k] for k in WEIGHTS],
            *[out_m[k] for k in WEIGHTS], *[out_v[k] for k in WEIGHTS])
```

```python
import functools
import math

import jax
import jax.numpy as jnp
from jax import lax
from jax.experimental import pallas as pl
from jax.experimental.pallas import tpu as pltpu

BF = jnp.bfloat16
F32 = jnp.float32
I32 = jnp.int32
S = jax.ShapeDtypeStruct

EPS = 1e-6
NEG = -1e30
POS_PAD = 1 << 30
WINDOW = 128
BLOCK = 128
HEAD_DIM = 64
SWA_HEADS = 16
SWA_KV_HEADS = 4
SWA_GROUP = SWA_HEADS // SWA_KV_HEADS
MEM_HEADS = 4
REL_BUCKETS = 32
REL_MAX_DIST = 128
ADAM_LR = 0.001
ADAM_B1 = 0.9
ADAM_B2 = 0.999
ADAM_EPS = 1e-08
ADAM_WD = 0.01
ADAM_STEP = 10

V7X_VMEM_LIMIT_BYTES = 56 * 1024 * 1024
MESH = pl.DeviceIdType.MESH
ANY = pl.BlockSpec(memory_space=pl.ANY)
VMEM_SPEC = pl.BlockSpec(memory_space=pltpu.VMEM)
SMEM_SPEC = pl.BlockSpec(memory_space=pltpu.SMEM)


def _params(sem=None):
    return pltpu.CompilerParams(dimension_semantics=sem, vmem_limit_bytes=V7X_VMEM_LIMIT_BYTES)


def _resident(shape):
    nd = len(shape)
    return pl.BlockSpec(shape, lambda *_: (0,) * nd, pipeline_mode=pl.Buffered(1))


def _acc_spec(shape):
    nd = len(shape)
    return pl.BlockSpec(shape, lambda *_: (0,) * nd)


def _call(body, *, name, grid, out_shape, in_specs, out_specs, args, sem, scratch_shapes=(), aliases=None, hook=None):
    aliases = aliases or {}
    if hook is None:
        return pl.pallas_call(body, name=name, grid=grid, out_shape=out_shape, in_specs=in_specs, out_specs=out_specs,
                              scratch_shapes=list(scratch_shapes), input_output_aliases=aliases,
                              compiler_params=_params(sem))(*args)
    n_in, n_out, n_scr = len(in_specs), len(out_shape), len(scratch_shapes)
    h_in, h_out = len(hook.ins), len(hook.out_shape)

    def at_step(pick):
        conds = [pl.program_id(ax) == pick(size) for ax, size in enumerate(grid)]
        return functools.reduce(jnp.logical_and, conds)

    def hosted(*refs):
        k_in, x_in = refs[:n_in], refs[n_in:n_in + h_in]
        o0 = n_in + h_in
        k_out, x_out = refs[o0:o0 + n_out], refs[o0 + n_out:o0 + n_out + h_out]
        s0 = o0 + n_out + h_out
        k_scr, x_scr = refs[s0:s0 + n_scr], refs[s0 + n_scr:]

        @pl.when(at_step(lambda size: 0))
        def _():
            hook.start(x_in, x_out, x_scr)
        body(*k_in, *k_out, *k_scr)
        early = bool(hook.lead) and grid[0] > hook.lead
        if early:
            on_axis0 = pl.program_id(0) == grid[0] - 1 - hook.lead
            rest_first = [pl.program_id(ax) == 0 for ax in range(1, len(grid))]

            @pl.when(functools.reduce(jnp.logical_and, rest_first, on_axis0))
            def _():
                hook.relay(x_in, x_out, x_scr)

        @pl.when(at_step(lambda size: size - 1))
        def _():
            if not early:
                hook.relay(x_in, x_out, x_scr)
            hook.finish(x_in, x_out, x_scr)

    res = pl.pallas_call(
        hosted, name=name, grid=grid, out_shape=tuple(out_shape) + tuple(hook.out_shape),
        in_specs=list(in_specs) + [ANY] * h_in, out_specs=tuple(out_specs) + (ANY,) * h_out,
        scratch_shapes=list(scratch_shapes) + list(hook.scratch), input_output_aliases=aliases,
        compiler_params=_params(("arbitrary",) * len(grid)),
    )(*args, *hook.ins)
    hook.results = res[n_out:]
    return res[:n_out]


def _nn(a, b):
    return jnp.dot(a, b, preferred_element_type=F32)


def _nt(a, b):
    return lax.dot_general(a, b, (((1,), (1,)), ((), ())), preferred_element_type=F32)


def _tn(a, b):
    return lax.dot_general(a, b, (((0,), (0,)), ((), ())), preferred_element_type=F32)


def _sigmoid(v):
    return 1.0 / (1.0 + jnp.exp(-v))


def _rms(x):
    r = lax.rsqrt(jnp.mean(x * x, axis=-1, keepdims=True) + EPS)
    return x * r, r


def _rms_bwd(dh, n, r, g):
    dn = dh * g
    dx = r * (dn - n * jnp.mean(dn * n, axis=-1, keepdims=True))
    return dx, jnp.sum(dh * n, axis=0, keepdims=True)


def _ffn_fwd(x, gn, wgu, wd, name, hook=None, head=None):
    t, d = x.shape
    f = wd.shape[0]
    tm, fc = 256, 1408

    def body(x_ref, gn_ref, wgu_ref, wd_ref, *rest):
        xv = x_ref[...]
        n, _ = _rms(xv)
        h = (n * gn_ref[...]).astype(BF)
        g_ref, u_ref = rest[-2:] if head is None else rest[3:5]
        acc = jnp.zeros((tm, d), F32)
        for c0 in range(0, f, fc):
            g = _nn(h, wgu_ref[:, c0:c0 + fc])
            u = _nn(h, wgu_ref[:, f + c0:f + c0 + fc])
            g_ref[:, c0:c0 + fc] = g.astype(BF)
            u_ref[:, c0:c0 + fc] = u.astype(BF)
            a = (g * _sigmoid(g)) * u
            acc = acc + _nn(a.astype(BF), wd_ref[c0:c0 + fc, :])
        y = xv + 0.5 * acc
        if head is None:
            rest[0][...] = y
            return
        gf_ref, t_ref, dy_ref, _, _, dgf_ref, loss_ref = rest

        @pl.when(pl.program_id(0) == 0)
        def _():
            dgf_ref[...] = jnp.zeros_like(dgf_ref)
            loss_ref[...] = jnp.zeros_like(loss_ref)
        ny, ry = _rms(y)
        gf = gf_ref[...]
        err = ny * gf - t_ref[...]
        loss_ref[...] += 0.5 * jnp.sum(jnp.sum(err * err, axis=-1, keepdims=True) / d, axis=0, keepdims=True)
        dy, dgf = _rms_bwd(err / d, ny, ry, gf)
        dy_ref[...] = dy
        dgf_ref[...] += dgf

    row = pl.BlockSpec((tm, d), lambda i: (i, 0))
    frow = pl.BlockSpec((tm, f), lambda i: (i, 0))
    in_specs = [row, _resident((1, d)), _resident(wgu.shape), _resident(wd.shape)]
    out_shape = (S((t, d), F32), S((t, f), BF), S((t, f), BF))
    out_specs = (row, frow, frow)
    args = (x, gn, wgu, wd)
    if head is not None:
        in_specs += [_resident((1, d)), row]
        out_shape += (S((1, d), F32), S((1, 128), F32))
        out_specs += (_acc_spec((1, d)), _acc_spec((1, 128)))
        args += tuple(head)
    return _call(body, name=name, grid=(t // tm,), out_shape=out_shape, in_specs=in_specs, out_specs=out_specs,
                 sem=("parallel",) if head is None else ("arbitrary",), args=args, hook=hook)


def _mix_proj(x, gn, w_in, hook=None):
    t, d = x.shape
    tm = 256
    nqkv = 1536
    ne = w_in.shape[1] - nqkv

    def body(x_ref, gn_ref, w_ref, qkv_ref, e_ref):
        n, _ = _rms(x_ref[...])
        h = (n * gn_ref[...]).astype(BF)
        qkv_ref[...] = _nn(h, w_ref[:, 0:nqkv]).astype(BF)
        for c0 in range(0, ne, 1024):
            e_ref[:, c0:c0 + 1024] = _nn(h, w_ref[:, nqkv + c0:nqkv + c0 + 1024]).astype(BF)

    return _call(
        body, name="mix_proj", grid=(t // tm,),
        out_shape=(S((t, nqkv), BF), S((t, ne), BF)),
        in_specs=[pl.BlockSpec((tm, d), lambda i: (i, 0)), _resident((1, d)), _resident(w_in.shape)],
        out_specs=(pl.BlockSpec((tm, nqkv), lambda i: (i, 0)), pl.BlockSpec((tm, ne), lambda i: (i, 0))),
        sem=("parallel",), args=(x, gn, w_in), hook=hook)


def _t5_bucket(rel):
    n = jnp.maximum(rel, 0)
    max_exact = REL_BUCKETS // 2
    nf = jnp.maximum(n, 1).astype(F32)
    large = max_exact + (jnp.log(nf / max_exact) / math.log(REL_MAX_DIST / max_exact)
                         * (REL_BUCKETS - max_exact)).astype(I32)
    large = jnp.minimum(large, REL_BUCKETS - 1)
    return jnp.where(n < max_exact, n, large)


def _block_rel():
    i = lax.broadcasted_iota(I32, (BLOCK, 2 * BLOCK), 0)
    j = lax.broadcasted_iota(I32, (BLOCK, 2 * BLOCK), 1)
    return i + BLOCK - j


def _bias_build(rel_bias):
    def body(rb_ref, o_ref):
        bucket = _t5_bucket(_block_rel())
        for h in range(SWA_HEADS):
            acc = jnp.zeros((BLOCK, 2 * BLOCK), F32)
            for b in range(REL_BUCKETS):
                acc = jnp.where(bucket == b, rb_ref[b, h], acc)
            o_ref[h] = acc

    return pl.pallas_call(
        body, name="bias_build", out_shape=S((SWA_HEADS, BLOCK, 2 * BLOCK), F32),
        in_specs=[SMEM_SPEC], out_specs=VMEM_SPEC,
    )(rel_bias)


GROUP_ROWS = SWA_GROUP * BLOCK


def _swa_visible(b, pq_ref, pkp_ref, pkc_ref):
    pk = jnp.concatenate([pkp_ref[...], pkc_ref[...]], axis=1)
    col = lax.broadcasted_iota(I32, (1, 2 * BLOCK), 1)
    pk = jnp.where(jnp.logical_and(b == 0, col < BLOCK), POS_PAD, pk)
    rel = jnp.concatenate([pq_ref[...]] * SWA_GROUP, axis=0) - pk
    return jnp.logical_and(rel >= 0, rel < WINDOW)


def _group_heads(ref, hk):
    h0 = hk * SWA_GROUP
    return jnp.concatenate([ref[:, (h0 + g) * HEAD_DIM:(h0 + g + 1) * HEAD_DIM] for g in range(SWA_GROUP)], axis=0)


def _group_sinks(sink_ref, hk):
    row = lax.broadcasted_iota(I32, (GROUP_ROWS, 1), 0)
    col = jnp.zeros((GROUP_ROWS, 1), F32) + sink_ref[0, hk * SWA_GROUP]
    for g in range(1, SWA_GROUP):
        col = jnp.where(row >= g * BLOCK, sink_ref[0, hk * SWA_GROUP + g], col)
    return col


def _swa_probs(qg, kh, vis, bias, sink):
    s = _nt(qg, kh) * (HEAD_DIM ** -0.5)
    s = jnp.where(vis, s + bias, NEG)
    m = jnp.maximum(jnp.max(s, axis=-1, keepdims=True), sink)
    p = jnp.exp(s - m)
    ps = jnp.exp(sink - m)
    inv = 1.0 / (jnp.sum(p, axis=-1, keepdims=True) + ps)
    return p * inv, ps * inv


def _swa_fwd(qkv, pos_col, pos_row, bias_t, sinks, hook=None):
    t = qkv.shape[0]
    nb = t // BLOCK
    qw = SWA_HEADS * HEAD_DIM
    kw = SWA_KV_HEADS * HEAD_DIM

    def body(q_ref, kp_ref, kc_ref, vp_ref, vc_ref, pq_ref, pkp_ref, pkc_ref, bias_ref, sink_ref, o_ref):
        b = pl.program_id(0)
        vis = _swa_visible(b, pq_ref, pkp_ref, pkc_ref)
        k2 = jnp.concatenate([kp_ref[...], kc_ref[...]], axis=0)
        v2 = jnp.concatenate([vp_ref[...], vc_ref[...]], axis=0)
        for hk in range(SWA_KV_HEADS):
            kh = k2[:, hk * HEAD_DIM:(hk + 1) * HEAD_DIM]
            vh = v2[:, hk * HEAD_DIM:(hk + 1) * HEAD_DIM]
            pn, _ = _swa_probs(_group_heads(q_ref, hk), kh, vis, bias_ref[hk], _group_sinks(sink_ref, hk))
            o = _nn(pn.astype(BF), vh)
            for g in range(SWA_GROUP):
                h = hk * SWA_GROUP + g
                o_ref[:, h * HEAD_DIM:(h + 1) * HEAD_DIM] = o[g * BLOCK:(g + 1) * BLOCK]

    prev = lambda b: jnp.maximum(b - 1, 0)
    return _call(
        body, name="swa_fwd", grid=(nb,), out_shape=(S((t, qw), F32),),
        in_specs=[
            pl.BlockSpec((BLOCK, qw), lambda b: (b, 0)),
            pl.BlockSpec((BLOCK, kw), lambda b: (prev(b), qw // kw)),
            pl.BlockSpec((BLOCK, kw), lambda b: (b, qw // kw)),
            pl.BlockSpec((BLOCK, kw), lambda b: (prev(b), qw // kw + 1)),
            pl.BlockSpec((BLOCK, kw), lambda b: (b, qw // kw + 1)),
            pl.BlockSpec((BLOCK, 1), lambda b: (b, 0)),
            pl.BlockSpec((1, BLOCK), lambda b: (0, prev(b))),
            pl.BlockSpec((1, BLOCK), lambda b: (0, b)),
            _resident(bias_t.shape),
            SMEM_SPEC,
        ],
        out_specs=(pl.BlockSpec((BLOCK, qw), lambda b: (b, 0)),),
        sem=("parallel",), args=(qkv, qkv, qkv, qkv, qkv, pos_col, pos_row, pos_row, bias_t, sinks), hook=hook)[0]


HALO = 16


def _conv_taps(z, zh, first):
    tm = z.shape[0]
    zh = jnp.where(first, 0.0, zh)
    row = lax.broadcasted_iota(I32, (tm, 1), 0)
    z1 = jnp.where(row == 0, zh[HALO - 1:HALO, :], pltpu.roll(z, 1, 0))
    z2 = jnp.where(row == 0, zh[HALO - 2:HALO - 1, :], jnp.where(row == 1, zh[HALO - 1:HALO, :], pltpu.roll(z, 2, 0)))
    return z1, z2


def _mix_out_fwd(e, attn, conv_w, w_out, x, hook=None):
    t, d = x.shape
    tm = 256
    hb = tm // HALO
    f32 = lambda ref: ref[...].astype(F32)

    def body(c_ref, b_ref, u_ref, ga_ref, gc_ref, ch_ref, uh_ref, attn_ref, cw_ref, w_ref, x_ref, xo_ref, mg_ref):
        i = pl.program_id(0)
        z = f32(c_ref) * f32(u_ref)
        z1, z2 = _conv_taps(z, f32(ch_ref) * f32(uh_ref), i == 0)
        s = cw_ref[0:1, :] * z2 + cw_ref[1:2, :] * z1 + cw_ref[2:3, :] * z
        conv = f32(b_ref) * s
        merged = (_sigmoid(f32(ga_ref)) * attn_ref[...] + _sigmoid(f32(gc_ref)) * conv).astype(BF)
        mg_ref[...] = merged
        xo_ref[...] = x_ref[...] + _nn(merged, w_ref[...])

    ecol = lambda cb: pl.BlockSpec((tm, d), lambda i: (i, cb))
    halo = lambda cb: pl.BlockSpec((HALO, d), lambda i: (jnp.maximum(i * hb - 1, 0), cb))
    row = pl.BlockSpec((tm, d), lambda i: (i, 0))
    return _call(
        body, name="mix_out_fwd", grid=(t // tm,),
        out_shape=(S((t, d), F32), S((t, d), BF)),
        in_specs=[ecol(0), ecol(1), ecol(2), ecol(3), ecol(4), halo(0), halo(2), row,
                  _resident(conv_w.shape), _resident(w_out.shape), row],
        out_specs=(row, row),
        sem=("parallel",), args=(e, e, e, e, e, e, e, attn, conv_w, w_out, x), hook=hook)


def _mem_kv(mem, gm, wkv):
    m, d = mem.shape

    def body(mem_ref, gm_ref, w_ref, mh_ref, kv_ref):
        n, _ = _rms(mem_ref[...])
        mh = (n * gm_ref[...]).astype(BF)
        mh_ref[...] = mh
        kv_ref[...] = _nn(mh, w_ref[...]).astype(BF)

    return pl.pallas_call(
        body, name="mem_kv", out_shape=(S((m, d), BF), S((m, wkv.shape[1]), BF)),
        compiler_params=_params(),
    )(mem, gm, wkv)


def _xattn_probs(qh, kh):
    s = _nt(qh, kh) * (kh.shape[1] ** -0.5)
    p = jnp.exp(s - jnp.max(s, axis=-1, keepdims=True))
    return p * (1.0 / jnp.sum(p, axis=-1, keepdims=True))


def _xattn_fwd(x, gn, wq, kv, wo):
    t, d = x.shape
    tm = 256
    hd = d // MEM_HEADS

    def body(x_ref, gn_ref, wq_ref, kv_ref, wo_ref, xo_ref, q_ref, o_ref):
        xv = x_ref[...]
        n, _ = _rms(xv)
        q = _nn((n * gn_ref[...]).astype(BF), wq_ref[...]).astype(BF)
        q_ref[...] = q
        outs = []
        for hh in range(MEM_HEADS):
            p = _xattn_probs(q[:, hh * hd:(hh + 1) * hd], kv_ref[:, hh * hd:(hh + 1) * hd])
            outs.append(_nn(p.astype(BF), kv_ref[:, d + hh * hd:d + (hh + 1) * hd]))
        o = jnp.concatenate(outs, axis=1).astype(BF)
        o_ref[...] = o
        xo_ref[...] = xv + _nn(o, wo_ref[...])

    row = pl.BlockSpec((tm, d), lambda i: (i, 0))
    return pl.pallas_call(
        body, name="xattn_fwd", grid=(t // tm,),
        out_shape=(S((t, d), F32), S((t, d), BF), S((t, d), BF)),
        in_specs=[row, _resident((1, d)), _resident(wq.shape), _resident(kv.shape), _resident(wo.shape)],
        out_specs=(row, row, row),
        compiler_params=_params(("parallel",)),
    )(x, gn, wq, kv, wo)


def _ffn_bwd(dxo, x, gn, g, u, wgu, wd, name, hook=None):
    t, d = x.shape
    f = wd.shape[0]
    tm, fc = 256, 1408

    def body(dxo_ref, x_ref, gn_ref, g_ref, u_ref, wgu_ref, wd_ref, dx_ref, dgn_ref, dgu_ref, a_ref, h_ref, dyh_ref):
        @pl.when(pl.program_id(0) == 0)
        def _():
            dgn_ref[...] = jnp.zeros_like(dgn_ref)
        dxov = dxo_ref[...]
        dyh = (0.5 * dxov).astype(BF)
        dyh_ref[...] = dyh
        n, r = _rms(x_ref[...])
        gnv = gn_ref[...]
        h_ref[...] = (n * gnv).astype(BF)
        dh = jnp.zeros((tm, d), F32)
        for c0 in range(0, f, fc):
            gv = g_ref[:, c0:c0 + fc].astype(F32)
            uv = u_ref[:, c0:c0 + fc].astype(F32)
            da = _nt(dyh, wd_ref[c0:c0 + fc, :])
            sg = _sigmoid(gv)
            silu = gv * sg
            a_ref[:, c0:c0 + fc] = (silu * uv).astype(BF)
            dg = (da * uv * (sg * (1.0 + gv * (1.0 - sg)))).astype(BF)
            du = (da * silu).astype(BF)
            dgu_ref[:, c0:c0 + fc] = dg
            dgu_ref[:, f + c0:f + c0 + fc] = du
            dh = dh + _nt(dg, wgu_ref[:, c0:c0 + fc]) + _nt(du, wgu_ref[:, f + c0:f + c0 + fc])
        dx, dgn = _rms_bwd(dh, n, r, gnv)
        dx_ref[...] = dxov + dx
        dgn_ref[...] += dgn

    row = pl.BlockSpec((tm, d), lambda i: (i, 0))
    frow = pl.BlockSpec((tm, f), lambda i: (i, 0))
    return _call(
        body, name=name, grid=(t // tm,),
        out_shape=(S((t, d), F32), S((1, d), F32), S((t, 2 * f), BF), S((t, f), BF), S((t, d), BF), S((t, d), BF)),
        in_specs=[row, row, _resident((1, d)), frow, frow, _resident(wgu.shape), _resident(wd.shape)],
        out_specs=(row, _acc_spec((1, d)), pl.BlockSpec((tm, 2 * f), lambda i: (i, 0)), frow, row, row),
        sem=("arbitrary",), args=(dxo, x, gn, g, u, wgu, wd), hook=hook)


def _dw_pair(pairs, tn, name, kind, hook=None):
    npairs = len(pairs)
    t, ka = pairs[0][0].shape
    nb = pairs[0][1].shape[1]
    tt = 2048 if npairs == 1 else 1024
    nt, nj = t // tt, nb // tn
    col = kind == "col"
    rh = ka // 2 if col else ka // 8
    tile = (rh, tn) if col else (4, rh, tn)
    half = (rh, nb) if col else (4, rh, nb)
    lead = (slice(None),) * (len(tile) - 1)

    def body(*refs):
        ins, mines, sibs = refs[:2 * npairs], refs[2 * npairs:3 * npairs], refs[3 * npairs:4 * npairs]
        acc_ref, stage, ssem, rsem = refs[4 * npairs:]
        j, k = pl.program_id(0), pl.program_id(1)
        x, y, c = _place()
        sibling = (x, y, 1 - c)

        def send(p, slot, jj):
            dst = sibs[p].at[lead + (pl.ds(pl.multiple_of(jj * tn, 128), tn),)]
            return _remote(stage.at[slot], dst, ssem.at[slot], rsem.at[p], sibling)

        def rows(s, whose):
            return acc_ref[pl.ds(pl.multiple_of(s * 2 * rh + whose * rh, 16), rh), :].astype(BF)

        def step(p):
            a_ref, b_ref, mine_ref = ins[2 * p], ins[2 * p + 1], mines[p]

            @pl.when(k == 0)
            def _():
                acc_ref[...] = jnp.zeros_like(acc_ref)
            acc_ref[...] += _tn(a_ref[...], b_ref[...])

            @pl.when(k == nt - 1)
            def _():
                slot = j % 2

                @pl.when(j >= 2)
                def _():
                    send(p, slot, 0).wait_send()
                if col:
                    mine_ref[...] = rows(0, c)
                    stage[slot] = rows(0, 1 - c)
                else:
                    for s in range(4):
                        mine_ref[s] = rows(s, c)
                        stage[slot, s] = rows(s, 1 - c)
                send(p, slot, j - p * nj).start()

        for p in range(npairs):
            pl.when(j // nj == p)(functools.partial(step, p))

        @pl.when(jnp.logical_and(j == npairs * nj - 1, k == nt - 1))
        def _():
            for jj in range(max(npairs * nj - 2, 0), npairs * nj):
                send(0, jj % 2, 0).wait_send()
            for p in range(npairs):
                _remote(sibs[p], sibs[p], ssem.at[0], rsem.at[p], sibling).wait_recv()

    in_specs, args = [], []
    for p, (a, b) in enumerate(pairs):
        on = lambda j, p=p: j // nj == p
        in_specs += [pl.BlockSpec((tt, ka), lambda j, k, on=on: (jnp.where(on(j), k, 0), 0)),
                     pl.BlockSpec((tt, tn), lambda j, k, on=on, p=p: (jnp.where(on(j), k, 0), jnp.clip(j - p * nj, 0, nj - 1)))]
        args += [a, b]
    mine_spec = lambda p: pl.BlockSpec(tile, (lambda j, k: (0, jnp.clip(j - p * nj, 0, nj - 1))) if col
                                       else (lambda j, k: (0, 0, jnp.clip(j - p * nj, 0, nj - 1))))
    res = _call(
        body, name=name, grid=(npairs * nj, nt), out_shape=(S(half, BF),) * (2 * npairs),
        in_specs=in_specs, out_specs=tuple(mine_spec(p) for p in range(npairs)) + (ANY,) * npairs,
        scratch_shapes=[pltpu.VMEM((ka, tn), F32), pltpu.VMEM((2,) + tile, BF), pltpu.SemaphoreType.DMA((2,)),
                        pltpu.SemaphoreType.DMA((npairs,))],
        sem=("arbitrary", "arbitrary"), args=args, hook=hook)
    return [(res[p], res[npairs + p]) for p in range(npairs)]


def _xattn_bwd(dxo, x, gn, q, kv, wq, wo, hook=None):
    t, d = x.shape
    tm = 256
    hd = d // MEM_HEADS
    nkv = kv.shape[0]

    def body(dxo_ref, x_ref, gn_ref, q_ref, kv_ref, wq_ref, wo_ref, dx_ref, dgn_ref, dkv_ref, dxh_ref, h_ref, dq_ref):
        @pl.when(pl.program_id(0) == 0)
        def _():
            dgn_ref[...] = jnp.zeros_like(dgn_ref)
            dkv_ref[...] = jnp.zeros_like(dkv_ref)
        dxov = dxo_ref[...]
        dxh = dxov.astype(BF)
        dxh_ref[...] = dxh
        do = _nt(dxh, wo_ref[...]).astype(BF)
        dqs = []
        for hh in range(MEM_HEADS):
            lo, hi = hh * hd, (hh + 1) * hd
            qh = q_ref[:, lo:hi]
            kh = kv_ref[:, lo:hi]
            vh = kv_ref[:, d + lo:d + hi]
            doh = do[:, lo:hi]
            p = _xattn_probs(qh, kh)
            dp = _nt(doh, vh)
            ds = (p * (dp - jnp.sum(p * dp, axis=-1, keepdims=True)) * (hd ** -0.5)).astype(BF)
            dqs.append(_nn(ds, kh))
            dkv_ref[:, lo:hi] += _tn(ds, qh)
            dkv_ref[:, d + lo:d + hi] += _tn(p.astype(BF), doh)
        dq = jnp.concatenate(dqs, axis=1).astype(BF)
        dq_ref[...] = dq
        n, r = _rms(x_ref[...])
        gnv = gn_ref[...]
        h_ref[...] = (n * gnv).astype(BF)
        dx, dgn = _rms_bwd(_nt(dq, wq_ref[...]), n, r, gnv)
        dx_ref[...] = dxov + dx
        dgn_ref[...] += dgn

    row = pl.BlockSpec((tm, d), lambda i: (i, 0))
    return _call(
        body, name="xattn_bwd", grid=(t // tm,),
        out_shape=(S((t, d), F32), S((1, d), F32), S((nkv, 2 * d), F32), S((t, d), BF), S((t, d), BF), S((t, d), BF)),
        in_specs=[row, row, _resident((1, d)), row, _resident(kv.shape), _resident(wq.shape), _resident(wo.shape)],
        out_specs=(row, _acc_spec((1, d)), _acc_spec((nkv, 2 * d)), row, row, row),
        sem=("arbitrary",), args=(dxo, x, gn, q, kv, wq, wo), hook=hook)


def _mem_bwd(dkv, mh, mem, gm, wkv):
    m, d = mem.shape
    rows, cols = wkv.shape
    rh = rows // 2

    def body(dkv_ref, mh_ref, mem_ref, gm_ref, w_ref, mine_ref, sib_ref, dgm_ref, whole, ssem, rsem):
        dkvb = dkv_ref[...].astype(BF)
        dmh = _nt(dkvb, w_ref[...])
        n, _ = _rms(mem_ref[...])
        x, y, c = _place()
        whole[...] = _tn(mh_ref[...], dkvb).astype(BF)
        cp = _remote(whole.at[pl.ds(pl.multiple_of((1 - c) * rh, 16), rh), :], sib_ref, ssem, rsem, (x, y, 1 - c))
        cp.start()
        mine_ref[...] = whole[pl.ds(pl.multiple_of(c * rh, 16), rh), :]
        cp.wait()
        dgm_ref[...] = jnp.sum(dmh * n, axis=0, keepdims=True)

    mine, sib, dgm = pl.pallas_call(
        body, name="mem_bwd", out_shape=(S((rh, cols), BF), S((rh, cols), BF), S((1, d), F32)),
        in_specs=[VMEM_SPEC] * 5, out_specs=(VMEM_SPEC, ANY, VMEM_SPEC),
        scratch_shapes=[pltpu.VMEM((rows, cols), BF), pltpu.SemaphoreType.DMA, pltpu.SemaphoreType.DMA],
        compiler_params=_params())(dkv, mh, mem, gm, wkv)
    return (mine, sib), dgm


def _mix_out_bwd(dxo, e, attn, conv_w, w_out, nqkv, hook=None):
    t, d = attn.shape
    tm = 256
    nt = t // tm
    f32 = lambda ref: ref[...].astype(F32)

    def body(dxo_ref, dxn_ref, c_ref, b_ref, u_ref, ga_ref, gc_ref, ch_ref, uh_ref, bn_ref, gcn_ref,
             attn_ref, cw_ref, w_ref, dattn_ref, de_ref, dcw_ref, dxh_ref):
        i = pl.program_id(0)

        @pl.when(i == 0)
        def _():
            dcw_ref[...] = jnp.zeros_like(dcw_ref)
        dxh = dxo_ref[...].astype(BF)
        dxh_ref[...] = dxh
        w = w_ref[...]
        dm = _nt(dxh, w)
        dmn = _nt(dxn_ref[...].astype(BF), w)
        cv, bv, uv = f32(c_ref), f32(b_ref), f32(u_ref)
        sga = _sigmoid(f32(ga_ref))
        sgc = _sigmoid(f32(gc_ref))
        z = cv * uv
        z1, z2 = _conv_taps(z, f32(ch_ref) * f32(uh_ref), i == 0)
        w0, w1, w2 = cw_ref[0:1, :], cw_ref[1:2, :], cw_ref[2:3, :]
        s = w0 * z2 + w1 * z1 + w2 * z
        av = attn_ref[...]
        dattn_ref[...] = (dm * sga).astype(BF)
        dconv = dm * sgc
        ds = dconv * bv
        dsn = jnp.where(i == nt - 1, 0.0, dmn * _sigmoid(gcn_ref[0:8, :].astype(F32)) * bn_ref[0:8, :].astype(F32))
        row = lax.broadcasted_iota(I32, (tm, 1), 0)
        dsp1 = jnp.where(row == tm - 1, dsn[0:1, :], pltpu.roll(ds, tm - 1, 0))
        dsp2 = jnp.where(row == tm - 2, dsn[0:1, :], jnp.where(row == tm - 1, dsn[1:2, :], pltpu.roll(ds, tm - 2, 0)))
        dz = w2 * ds + w1 * dsp1 + w0 * dsp2
        de_ref[:, nqkv:nqkv + d] = (dz * uv).astype(BF)
        de_ref[:, nqkv + d:nqkv + 2 * d] = (dconv * s).astype(BF)
        de_ref[:, nqkv + 2 * d:nqkv + 3 * d] = (dz * cv).astype(BF)
        de_ref[:, nqkv + 3 * d:nqkv + 4 * d] = (dm * av * sga * (1.0 - sga)).astype(BF)
        de_ref[:, nqkv + 4 * d:nqkv + 5 * d] = (dm * (bv * s) * sgc * (1.0 - sgc)).astype(BF)
        dcw_ref[0:1, :] += jnp.sum(ds * z2, axis=0, keepdims=True)
        dcw_ref[1:2, :] += jnp.sum(ds * z1, axis=0, keepdims=True)
        dcw_ref[2:3, :] += jnp.sum(ds * z, axis=0, keepdims=True)

    ecol = lambda cb: pl.BlockSpec((tm, d), lambda i: (i, cb))
    prev = lambda cb: pl.BlockSpec((HALO, d), lambda i: (jnp.maximum(i * (tm // HALO) - 1, 0), cb))
    nxt = lambda rows, cb: pl.BlockSpec((rows, d), lambda i: (jnp.minimum((i + 1) * (tm // rows), t // rows - 1), cb))
    row = pl.BlockSpec((tm, d), lambda i: (i, 0))
    return _call(
        body, name="mix_out_bwd", grid=(nt,),
        out_shape=(S((t, d), BF), S((t, nqkv + 5 * d), BF), S((8, d), F32), S((t, d), BF)),
        in_specs=[row, nxt(8, 0), ecol(0), ecol(1), ecol(2), ecol(3), ecol(4), prev(0), prev(2), nxt(HALO, 1), nxt(HALO, 4),
                  row, _resident(conv_w.shape), _resident(w_out.shape)],
        out_specs=(row, pl.BlockSpec((tm, nqkv + 5 * d), lambda i: (i, 0)), _acc_spec((8, d)), row),
        sem=("arbitrary",), args=(dxo, dxo, e, e, e, e, e, e, e, e, e, attn, conv_w, w_out), hook=hook)


def _swa_bwd(qkv, dattn, pos_col, pos_row, bias_t, sinks, dp, hook=None):
    t = qkv.shape[0]
    nb = t // BLOCK
    qw = SWA_HEADS * HEAD_DIM
    kw = SWA_KV_HEADS * HEAD_DIM

    def body(q_ref, kp_ref, kc_ref, vp_ref, vc_ref, do_ref, pq_ref, pkp_ref, pkc_ref, bias_ref, sink_ref,
             dp_in, dp_ref, gb_ref, dsk_ref, carry_ref, dq_ref):
        b = pl.program_id(0)

        @pl.when(b == 0)
        def _():
            gb_ref[...] = jnp.zeros_like(gb_ref)
            dsk_ref[...] = jnp.zeros_like(dsk_ref)
            carry_ref[...] = jnp.zeros_like(carry_ref)
            dq_ref[...] = jnp.zeros_like(dq_ref)
        dp_ref[:, 0:qw] = dq_ref[...]

        @pl.when(b < nb)
        def _():
            vis = _swa_visible(b, pq_ref, pkp_ref, pkc_ref)
            k2 = jnp.concatenate([kp_ref[...], kc_ref[...]], axis=0)
            v2 = jnp.concatenate([vp_ref[...], vc_ref[...]], axis=0)
            for hk in range(SWA_KV_HEADS):
                lo, hi = hk * HEAD_DIM, (hk + 1) * HEAD_DIM
                kh = k2[:, lo:hi]
                vh = v2[:, lo:hi]
                qg = _group_heads(q_ref, hk)
                dog = _group_heads(do_ref, hk)
                pn, psn = _swa_probs(qg, kh, vis, bias_ref[hk], _group_sinks(sink_ref, hk))
                dp = _nt(dog, vh)
                delta = jnp.sum(pn * dp, axis=-1, keepdims=True)
                ds = pn * (dp - delta)
                gb_ref[hk] += ds
                dsk_ref[hk] += -psn * delta
                dsb = (ds * (HEAD_DIM ** -0.5)).astype(BF)
                dqg = _nn(dsb, kh).astype(BF)
                for g in range(SWA_GROUP):
                    h = hk * SWA_GROUP + g
                    dq_ref[:, h * HEAD_DIM:(h + 1) * HEAD_DIM] = dqg[g * BLOCK:(g + 1) * BLOCK]
                dk = _tn(dsb, qg)
                dv = _tn(pn.astype(BF), dog)
                dp_ref[:, qw + lo:qw + hi] = (carry_ref[:, lo:hi] + dk[0:BLOCK]).astype(BF)
                dp_ref[:, qw + kw + lo:qw + kw + hi] = (carry_ref[:, kw + lo:kw + hi] + dv[0:BLOCK]).astype(BF)
                carry_ref[:, lo:hi] = dk[BLOCK:2 * BLOCK]
                carry_ref[:, kw + lo:kw + hi] = dv[BLOCK:2 * BLOCK]

        @pl.when(b == nb)
        def _():
            dp_ref[:, qw:qw + 2 * kw] = carry_ref[...].astype(BF)

    cur = lambda b: jnp.minimum(b, nb - 1)
    prev = lambda b: jnp.maximum(cur(b) - 1, 0)
    return _call(
        body, name="swa_bwd", grid=(nb + 1,),
        out_shape=(S(dp.shape, BF), S((SWA_KV_HEADS, GROUP_ROWS, 2 * BLOCK), F32), S((SWA_KV_HEADS, GROUP_ROWS, 1), F32)),
        in_specs=[
            pl.BlockSpec((BLOCK, qw), lambda b: (cur(b), 0)),
            pl.BlockSpec((BLOCK, kw), lambda b: (prev(b), qw // kw)),
            pl.BlockSpec((BLOCK, kw), lambda b: (cur(b), qw // kw)),
            pl.BlockSpec((BLOCK, kw), lambda b: (prev(b), qw // kw + 1)),
            pl.BlockSpec((BLOCK, kw), lambda b: (cur(b), qw // kw + 1)),
            pl.BlockSpec((BLOCK, qw), lambda b: (cur(b), 0)),
            pl.BlockSpec((BLOCK, 1), lambda b: (cur(b), 0)),
            pl.BlockSpec((1, BLOCK), lambda b: (0, prev(b))),
            pl.BlockSpec((1, BLOCK), lambda b: (0, cur(b))),
            _resident(bias_t.shape),
            SMEM_SPEC,
            ANY,
        ],
        out_specs=(
            pl.BlockSpec((BLOCK, qw + 2 * kw), lambda b: (jnp.maximum(b - 1, 0), 0)),
            _acc_spec((SWA_KV_HEADS, GROUP_ROWS, 2 * BLOCK)),
            _acc_spec((SWA_KV_HEADS, GROUP_ROWS, 1)),
        ),
        scratch_shapes=[pltpu.VMEM((BLOCK, 2 * kw), F32), pltpu.VMEM((BLOCK, qw), BF)],
        aliases={11: 0}, sem=("arbitrary",),
        args=(qkv, qkv, qkv, qkv, qkv, dattn, pos_col, pos_row, pos_row, bias_t, sinks, dp), hook=hook)


def _bias_reduce(gb, dsk):
    def body(gb_ref, dsk_ref, drb_ref, dsink_ref):
        bucket = _t5_bucket(_block_rel())
        for b in range(REL_BUCKETS):
            mask = bucket == b
            for h in range(SWA_HEADS):
                drb_ref[b, h] = jnp.sum(jnp.where(mask, gb_ref[h], 0.0))
        for h in range(SWA_HEADS):
            dsink_ref[0, h] = jnp.sum(dsk_ref[h])

    return pl.pallas_call(
        body, name="bias_reduce", out_shape=(S((REL_BUCKETS, SWA_HEADS), F32), S((1, SWA_HEADS), F32)),
        in_specs=[VMEM_SPEC, VMEM_SPEC], out_specs=(SMEM_SPEC, SMEM_SPEC),
    )(gb, dsk)


def _mix_in_bwd(dp, x, gn, w_in, dxo):
    t, d = x.shape
    tm = 256
    npr = dp.shape[1]

    def body(dp_ref, x_ref, gn_ref, w_ref, dxo_ref, dx_ref, dgn_ref, h_ref):
        @pl.when(pl.program_id(0) == 0)
        def _():
            dgn_ref[...] = jnp.zeros_like(dgn_ref)
        dh = jnp.zeros((tm, d), F32)
        for c0 in range(0, npr, 1024):
            c1 = min(c0 + 1024, npr)
            dh = dh + _nt(dp_ref[:, c0:c1], w_ref[:, c0:c1])
        n, r = _rms(x_ref[...])
        gnv = gn_ref[...]
        h_ref[...] = (n * gnv).astype(BF)
        dx, dgn = _rms_bwd(dh, n, r, gnv)
        dx_ref[...] = dxo_ref[...] + dx
        dgn_ref[...] += dgn

    row = pl.BlockSpec((tm, d), lambda i: (i, 0))
    return pl.pallas_call(
        body, name="mix_in_bwd", grid=(t // tm,),
        out_shape=(S((t, d), F32), S((1, d), F32), S((t, d), BF)),
        in_specs=[pl.BlockSpec((tm, npr), lambda i: (i, 0)), row, _resident((1, d)), _resident(w_in.shape), row],
        out_specs=(row, _acc_spec((1, d)), row),
        compiler_params=_params(("arbitrary",)),
    )(dp, x, gn, w_in, dxo)


CAST_STEPS = 4


def _to_bf16(arrays, name, hook=None):
    def body(*refs):
        for src, dst in zip(refs[:len(arrays)], refs[len(arrays):]):
            dst[...] = src[...].astype(BF)

    blocks = [pl.BlockSpec((a.shape[0] // CAST_STEPS, a.shape[1]), lambda i: (i, 0)) for a in arrays]
    return _call(body, name=name, grid=(CAST_STEPS,), out_shape=tuple(S(a.shape, BF) for a in arrays), in_specs=blocks,
                 out_specs=tuple(blocks), sem=("parallel",), args=list(arrays), hook=hook)


PAIR_STEPS = 4


def _add_bf16(pairs, name):
    def body(*refs):
        ins, outs = refs[:2 * len(pairs)], refs[2 * len(pairs):]
        for q, o_ref in enumerate(outs):
            o_ref[...] = (ins[2 * q][...].astype(F32) + ins[2 * q + 1][...].astype(F32)).astype(BF)

    in_specs, out_specs, out_shape, args = [], [], [], []
    for a, b in pairs:
        rows, cols = a.shape
        blk = pl.BlockSpec((rows // PAIR_STEPS, cols), lambda i: (i, 0))
        in_specs += [blk, blk]
        out_specs.append(blk)
        out_shape.append(S((rows, cols), BF))
        args += [a, b]
    return pl.pallas_call(body, name=name, grid=(PAIR_STEPS,), out_shape=tuple(out_shape), in_specs=in_specs,
                          out_specs=tuple(out_specs), compiler_params=_params(("parallel",)))(*args)


def _adam_update(w, g, m, v):
    mn = ADAM_B1 * m + (1.0 - ADAM_B1) * g
    vn = ADAM_B2 * v + (1.0 - ADAM_B2) * (g * g)
    m_hat = mn / (1.0 - ADAM_B1 ** ADAM_STEP)
    v_hat = vn / (1.0 - ADAM_B2 ** ADAM_STEP)
    return -ADAM_LR * (m_hat / (jnp.sqrt(v_hat) + ADAM_EPS) + ADAM_WD * w), mn, vn


def _adamw(quads, name, steps, echo=False, hook=None):
    n_out = 4 if echo else 3

    def body(*refs):
        ins, outs = refs[:4 * len(quads)], refs[4 * len(quads):]
        for q in range(len(quads)):
            w_ref, g_ref, m_ref, v_ref = ins[4 * q:4 * q + 4]
            res = outs[n_out * q:n_out * q + n_out]
            gv = g_ref[...]
            if echo:
                res[0][...] = gv
            res[-3][...], res[-2][...], res[-1][...] = _adam_update(w_ref[...], gv, m_ref[...], v_ref[...])

    in_specs, out_specs, out_shape, args = [], [], [], []
    for quad in quads:
        rows, cols = quad[0].shape
        blk = pl.BlockSpec((rows // steps, cols), lambda i: (i, 0))
        in_specs += [blk] * 4
        out_specs += [blk] * n_out
        out_shape += [S((rows, cols), F32)] * n_out
        args += list(quad)
    res = _call(body, name=name, grid=(steps,), out_shape=tuple(out_shape), in_specs=in_specs, out_specs=tuple(out_specs),
                sem=("parallel",), args=args, hook=hook)
    return [res[n_out * q:n_out * q + n_out] for q in range(len(quads))]


def _place():
    x, y, c = lax.axis_index("x"), lax.axis_index("y"), lax.axis_index("c")
    return x, y, c


OTHER_CHIPS = ((1, 0), (0, 1), (1, 1))


def _flip(v, f):
    return 1 - v if f else v


def _remote(src, dst, ssem, rsem, dev):
    return pltpu.make_async_remote_copy(src_ref=src, dst_ref=dst, send_sem=ssem, recv_sem=rsem,
                                        device_id=dev, device_id_type=MESH)


class _Both:
    def __init__(self, hooks):
        self.hooks = hooks
        self.ins = [a for h in hooks for a in h.ins]
        self.out_shape = tuple(o for h in hooks for o in h.out_shape)
        self.scratch = [x for h in hooks for x in h.scratch]

    def _each(self, ins, outs, scr):
        i = o = s = 0
        for h in self.hooks:
            ni, no, ns = len(h.ins), len(h.out_shape), len(h.scratch)
            yield h, ins[i:i + ni], outs[o:o + no], scr[s:s + ns]
            i, o, s = i + ni, o + no, s + ns

    def start(self, ins, outs, scr):
        for h, *refs in self._each(ins, outs, scr):
            h.start(*refs)

    def relay(self, ins, outs, scr):
        for h, *refs in self._each(ins, outs, scr):
            h.relay(*refs)

    def finish(self, ins, outs, scr):
        for h, *refs in self._each(ins, outs, scr):
            h.finish(*refs)

    @property
    def lead(self):
        return max(h.lead for h in self.hooks)

    @property
    def results(self):
        return [h.results for h in self.hooks]

    @results.setter
    def results(self, res):
        for h, _, mine, _ in self._each((), tuple(res), ()):
            h.results = mine


class _GatherHook:
    def __init__(self, shards, kinds, lead=2):
        self.ins, self.kinds, self.lead, n = list(shards), list(kinds), lead, len(shards)
        self.out_shape = tuple(
            S((w.shape[0], 4 * w.shape[1]), BF) if k == "col" else S((4,) + w.shape, BF) for w, k in zip(shards, kinds))
        dma = pltpu.SemaphoreType.DMA
        self.scratch = ([dma((n, 3)) for _ in range(5)] + [dma((n,)), dma((n,))]
                        + [pltpu.VMEM((3, w.shape[0] // 2, w.shape[1]), BF) for w in shards]
                        + [pltpu.VMEM(w.shape, BF) for w in shards])

    def _window(self, outs, i, s, half):
        rows, cols = self.ins[i].shape
        rh = rows // 2
        start = pl.multiple_of(half * rh, 16)
        if self.kinds[i] == "col":
            return outs[i].at[pl.ds(start, rh), pl.ds(pl.multiple_of(s * cols, 128), cols)]
        return outs[i].at[s, pl.ds(start, rh), :]

    def _copies(self, ins, outs, scr):
        n = len(ins)
        ssem, rsem, fssem, frsem, ksem, lsem, osem = scr[:7]
        land, own = scr[7:7 + n], scr[7 + n:7 + 2 * n]
        x, y, c = _place()
        sibling = (x, y, 1 - c)
        loads, stores, sends, forwards, keeps, passed = [], [], [], [], [], []
        for i in range(n):
            rows, cols = self.ins[i].shape
            rh = rows // 2
            mine = (outs[i].at[:, pl.ds(pl.multiple_of((2 * x + y) * cols, 128), cols)] if self.kinds[i] == "col"
                    else outs[i].at[2 * x + y])
            loads.append(functools.partial(pltpu.make_async_copy, ins[i], own[i], lsem.at[i]))
            stores.append(functools.partial(pltpu.make_async_copy, own[i], mine, osem.at[i]))
            src = ins[i].at[pl.ds(pl.multiple_of(c * rh, 16), rh), :]
            for j, (fx, fy) in enumerate(OTHER_CHIPS):
                px, py = _flip(x, fx), _flip(y, fy)
                sends.append(functools.partial(_remote, src, land[i].at[j], ssem.at[i, j], rsem.at[i, j], (px, py, c)))
                here = self._window(outs, i, 2 * px + py, c)
                forwards.append(functools.partial(_remote, land[i].at[j], here, fssem.at[i, j], frsem.at[i, j], sibling))
                keeps.append(functools.partial(pltpu.make_async_copy, land[i].at[j], here, ksem.at[i, j]))
                there = self._window(outs, i, 2 * px + py, 1 - c)
                passed.append(functools.partial(_remote, there, there, fssem.at[i, j], frsem.at[i, j], sibling))
        return loads, stores, sends, forwards, keeps, passed

    def start(self, ins, outs, scr):
        loads, _, sends, _, _, _ = self._copies(ins, outs, scr)
        for make in sends + loads:
            make().start()

    def relay(self, ins, outs, scr):
        loads, stores, sends, forwards, keeps, _ = self._copies(ins, outs, scr)
        for load, store in zip(loads, stores):
            load().wait()
            store().start()
        for send, forward, keep in zip(sends, forwards, keeps):
            send().wait_recv()
            forward().start()
            keep().start()

    def finish(self, ins, outs, scr):
        _, stores, sends, forwards, keeps, passed = self._copies(ins, outs, scr)
        for make in passed:
            make().wait_recv()
        for make in sends + forwards:
            make().wait_send()
        for make in keeps + stores:
            make().wait()


class _ChipsHook:
    def __init__(self, parts, kinds):
        self.ins, self.kinds, n = list(parts), list(kinds), len(parts)
        self.out_shape = tuple(
            S((4, p.shape[0], p.shape[1] // 4), BF) if k == "col" else S(p.shape, BF) for p, k in zip(parts, kinds))
        dma = pltpu.SemaphoreType.DMA
        self.scratch = ([dma((n, 3)), dma((n, 3)), dma((n,)), dma((n,))]
                        + [pltpu.VMEM(o.shape[1:], BF) for o in self.out_shape])

    def _slab(self, ins, i, s):
        _, rows, cols = self.out_shape[i].shape
        if self.kinds[i] == "col":
            return ins[i].at[:, pl.ds(pl.multiple_of(s * cols, 128), cols)]
        return ins[i].at[s]

    def _copies(self, ins, outs, scr):
        ssem, rsem, lsem, osem = scr[:4]
        own = scr[4:]
        x, y, c = _place()
        loads, stores, sends = [], [], []
        for i in range(len(ins)):
            loads.append(functools.partial(pltpu.make_async_copy, self._slab(ins, i, 2 * x + y), own[i], lsem.at[i]))
            stores.append(functools.partial(pltpu.make_async_copy, own[i], outs[i].at[3], osem.at[i]))
            for j, (fx, fy) in enumerate(OTHER_CHIPS):
                px, py = _flip(x, fx), _flip(y, fy)
                sends.append(functools.partial(_remote, self._slab(ins, i, 2 * px + py), outs[i].at[j], ssem.at[i, j],
                                               rsem.at[i, j], (px, py, c)))
        return loads, stores, sends

    def start(self, ins, outs, scr):
        loads, _, sends = self._copies(ins, outs, scr)
        for make in sends + loads:
            make().start()

    lead = 2

    def relay(self, ins, outs, scr):
        loads, stores, _ = self._copies(ins, outs, scr)
        for load, store in zip(loads, stores):
            load().wait()
            store().start()

    def finish(self, ins, outs, scr):
        _, stores, sends = self._copies(ins, outs, scr)
        for make in sends + stores:
            make().wait()


SHARE_STEPS = 2


def _sum_share(slab_list, name, hook=None):
    n = len(slab_list)
    geom = [(sl.shape[1], sl.shape[1] // SHARE_STEPS, sl.shape[2]) for sl in slab_list]

    def body(*refs):
        ins, outs, scr = refs[:n], refs[n:2 * n], refs[2 * n:]
        i = pl.program_id(0)
        x, y, c = _place()
        sibling = (x, y, 1 - c)

        def copies(q, k):
            rh, tr, _ = geom[q]
            stage, lsem, ssem, rsem = scr[4 * q:4 * q + 4]
            dst = outs[q].at[pl.ds(pl.multiple_of(c * rh + k * tr, 8), tr), :]
            return (pltpu.make_async_copy(stage.at[k], dst, lsem.at[k]),
                    _remote(stage.at[k], dst, ssem.at[k], rsem, sibling))

        for q in range(n):
            acc = ins[q][3].astype(F32)
            for k in range(3):
                acc = acc + ins[q][k].astype(F32)
            scr[4 * q][i] = acc
            for cp in copies(q, i):
                cp.start()

        @pl.when(i == SHARE_STEPS - 1)
        def _():
            for q in range(n):
                rh = geom[q][0]
                for k in range(SHARE_STEPS):
                    local, remote = copies(q, k)
                    local.wait()
                    remote.wait_send()
                got = outs[q].at[pl.ds(pl.multiple_of((1 - c) * rh, 8), rh), :]
                _remote(got, got, scr[4 * q + 2].at[0], scr[4 * q + 3], sibling).wait_recv()

    dma = pltpu.SemaphoreType.DMA
    scratch = []
    for rh, tr, cols in geom:
        scratch += [pltpu.VMEM((SHARE_STEPS, tr, cols), F32), dma((SHARE_STEPS,)), dma((SHARE_STEPS,)), dma]
    return _call(
        body, name=name, grid=(SHARE_STEPS,), out_shape=tuple(S((2 * rh, cols), F32) for rh, _, cols in geom),
        in_specs=[pl.BlockSpec((4, tr, cols), lambda i: (0, i, 0)) for _, tr, cols in geom], out_specs=(ANY,) * n,
        scratch_shapes=scratch, sem=("arbitrary",), args=list(slab_list), hook=hook)


class _SmallSumHook:
    def __init__(self, buf):
        self.ins, self.out_shape = [buf], (S(buf.shape, F32),)
        dma = pltpu.SemaphoreType.DMA
        self.scratch = [pltpu.VMEM((8,) + buf.shape, F32), pltpu.VMEM(buf.shape, F32), dma((7,)), dma((7,)), dma]

    def _sends(self, scr):
        slots, _, ssem, rsem, _ = scr
        x, y, c = _place()
        me = 4 * x + 2 * y + c
        for r in range(1, 8):
            px, py, pc = _flip(x, (r >> 2) & 1), _flip(y, (r >> 1) & 1), _flip(c, r & 1)
            yield (functools.partial(_remote, slots.at[me], slots.at[me], ssem.at[r - 1], rsem.at[r - 1], (px, py, pc)),
                   functools.partial(_remote, slots.at[me], slots.at[4 * px + 2 * py + pc], ssem.at[r - 1],
                                     rsem.at[r - 1], (px, py, pc)))

    def start(self, ins, outs, scr):
        slots, _, _, _, lsem = scr
        x, y, c = _place()
        load = pltpu.make_async_copy(ins[0], slots.at[4 * x + 2 * y + c], lsem)
        load.start()
        load.wait()
        for send, _ in self._sends(scr):
            send().start()

    lead = 0

    def relay(self, ins, outs, scr):
        pass

    def finish(self, ins, outs, scr):
        slots, total, _, _, lsem = scr
        for _, arrival in self._sends(scr):
            arrival().wait_recv()
        for send, _ in self._sends(scr):
            send().wait_send()
        acc = slots[0]
        for k in range(1, 8):
            acc = acc + slots[k]
        total[...] = acc
        store = pltpu.make_async_copy(total, outs[0], lsem)
        store.start()
        store.wait()


BIG = ("ffn1_w_gu", "ffn1_w_down", "w_in", "w_out", "xattn_wq", "xattn_wkv", "xattn_wo", "ffn2_w_gu", "ffn2_w_down")
KIND = {"ffn1_w_gu": "col", "ffn1_w_down": "row", "w_in": "col", "w_out": "row", "xattn_wq": "row",
        "xattn_wkv": "col", "xattn_wo": "row", "ffn2_w_gu": "col", "ffn2_w_down": "row"}
WEIGHTS = ("rel_bias", "ffn1_norm", "ffn1_w_gu", "ffn1_w_down", "mix_norm", "w_in", "sinks", "conv_w", "w_out",
           "xattn_norm", "mem_norm", "xattn_wq", "xattn_wkv", "xattn_wo", "ffn2_norm", "ffn2_w_gu", "ffn2_w_down",
           "final_norm")
SMALL_ROWS = 16
GAIN_ROW = {"ffn1_norm": 0, "mix_norm": 1, "xattn_norm": 2, "mem_norm": 3, "ffn2_norm": 4, "final_norm": 5}
CONV_ROW, SINK_ROW, BIAS_ROW, LOSS_ROW = 6, 9, 10, 11
TAP_ROWS = 8


def _rows_block(rows, d):
    buf = jnp.zeros((SMALL_ROWS, d), F32)
    for r, v in rows.items():
        buf = lax.dynamic_update_slice(buf, v if v.ndim == 2 else v.reshape(1, -1), (r, 0))
    return buf


def _adamw_small(gsum, conv_g, bias_g, small):
    names = list(small)

    def grad(k, gsum_ref, conv_ref, bias_ref):
        if k in GAIN_ROW:
            return gsum_ref[GAIN_ROW[k]:GAIN_ROW[k] + 1, :]
        if k == "sinks":
            return gsum_ref[SINK_ROW:SINK_ROW + 1, 0:small[k][0].shape[1]]
        return conv_ref[...] if k == "conv_w" else bias_ref[...]

    def body(gsum_ref, conv_ref, bias_ref, *refs):
        ins, outs = refs[:3 * len(names)], refs[3 * len(names):]
        for q, k in enumerate(names):
            w_ref, m_ref, v_ref = ins[3 * q:3 * q + 3]
            g_ref, d_ref, mo_ref, vo_ref = outs[4 * q:4 * q + 4]
            gv = grad(k, gsum_ref, conv_ref, bias_ref)
            g_ref[...] = gv
            d_ref[...], mo_ref[...], vo_ref[...] = _adam_update(w_ref[...], gv, m_ref[...], v_ref[...])

    res = pl.pallas_call(
        body, name="adamw_small", out_shape=tuple(S(small[k][0].shape, F32) for k in names for _ in range(4)),
        compiler_params=_params())(gsum, conv_g, bias_g, *[a for k in names for a in small[k]])
    return {k: res[4 * q:4 * q + 4] for q, k in enumerate(names)}


def _local_step(x, mem, pos, target, w, gains, rel_bias, sinks, conv_w, shards):
    t, d = x.shape
    w = dict(w)
    grads, slabs = {}, {}
    pos_col = pos.reshape(t, 1)
    pos_row = pos.reshape(1, t)
    bias_t = _bias_build(rel_bias).reshape(SWA_KV_HEADS, GROUP_ROWS, 2 * BLOCK)

    def gather(names, lead=2):
        return _GatherHook([shards[k] for k in names], [KIND[k] for k in names], lead)

    def gathered(names, hook):
        for k, gw in zip(names, hook.results):
            w[k] = gw if KIND[k] == "col" else gw.reshape(-1, gw.shape[-1])

    def dw(problems, tn, name, hook=None):
        res = _dw_pair([(a, b) for _, a, b in problems], tn, name, KIND[problems[0][0]], hook)
        grads.update(zip((k for k, _, _ in problems), res))

    def pair_sums(names):
        flat = lambda k, v: v.reshape(-1, v.shape[-1]) if KIND[k] == "row" else v
        sums = _add_bf16([(flat(k, grads[k][0]), flat(k, grads[k][1])) for k in names], "pair_sum_" + names[0])
        return {k: (p if KIND[k] == "col" else p.reshape(4, -1, p.shape[-1])) for k, p in zip(names, sums)}

    def chips(names, parts):
        return _ChipsHook([parts[k] for k in names], [KIND[k] for k in names])

    def reduced(names, hook):
        slabs.update(zip(names, hook.results))

    names = ("w_in", "w_out")
    hook = gather(names, lead=1)
    x1, g1, u1 = _ffn_fwd(x, gains["ffn1_norm"], w["ffn1_w_gu"], w["ffn1_w_down"], "ffn1_fwd", hook)
    gathered(names, hook)
    names = ("xattn_wq", "xattn_wkv", "xattn_wo")
    hook = gather(names)
    qkv, e = _mix_proj(x1, gains["mix_norm"], w["w_in"], hook)
    gathered(names, hook)
    names = ("ffn2_w_gu",)
    hook = gather(names)
    attn = _swa_fwd(qkv, pos_col, pos_row, bias_t, sinks, hook)
    gathered(names, hook)
    names = ("ffn2_w_down",)
    hook = gather(names)
    x2, merged = _mix_out_fwd(e, attn, conv_w, w["w_out"], x1, hook)
    gathered(names, hook)
    mh, kv = _mem_kv(mem, gains["mem_norm"], w["xattn_wkv"])
    x3, qx, o = _xattn_fwd(x2, gains["xattn_norm"], w["xattn_wq"], kv, w["xattn_wo"])
    dx4, g2, u2, d_final, loss = _ffn_fwd(x3, gains["ffn2_norm"], w["ffn2_w_gu"], w["ffn2_w_down"], "ffn2_fwd",
                                          head=(gains["final_norm"], target))

    dx3, d_ffn2, dgu2, a2, h4, dyh4 = _ffn_bwd(dx4, x3, gains["ffn2_norm"], g2, u2, w["ffn2_w_gu"], w["ffn2_w_down"],
                                               "ffn2_bwd")
    dw([("ffn2_w_gu", h4, dgu2)], 1408, "dw_ffn2_gu")
    dw([("ffn2_w_down", a2, dyh4)], 512, "dw_ffn2_down")
    parts = pair_sums(("ffn2_w_gu", "ffn2_w_down"))
    hook = chips(("ffn2_w_gu",), parts)
    dx2, d_xattn, dkv, dxh3, h3, dqx = _xattn_bwd(dx3, x2, gains["xattn_norm"], qx, kv, w["xattn_wq"], w["xattn_wo"], hook)
    reduced(("ffn2_w_gu",), hook)
    grads["xattn_wkv"], d_mem = _mem_bwd(dkv, mh, mem, gains["mem_norm"], w["xattn_wkv"])
    hook = chips(("ffn2_w_down",), parts)
    dattn, dp, dcw, dxh2 = _mix_out_bwd(dx2, e, attn, conv_w, w["w_out"], qkv.shape[1], hook)
    reduced(("ffn2_w_down",), hook)
    dw([("xattn_wo", o, dxh3), ("xattn_wq", h3, dqx), ("w_out", merged, dxh2)], 1024, "dw_wo_wq_wout")
    names = ("xattn_wo", "xattn_wq", "xattn_wkv", "w_out")
    hook = chips(names, pair_sums(names))
    dp, gb, dsk = _swa_bwd(qkv, dattn, pos_col, pos_row, bias_t, sinks, dp, hook)
    reduced(names, hook)
    d_rel_bias, d_sinks = _bias_reduce(gb.reshape(SWA_HEADS, BLOCK, 2 * BLOCK), dsk.reshape(SWA_HEADS, BLOCK, 1))
    dx1, d_mix, h2 = _mix_in_bwd(dp, x1, gains["mix_norm"], w["w_in"], dx2)
    dw([("w_in", h2, dp)], w["w_in"].shape[1] // 4, "dw_win")
    names = ("w_in",)
    hook = chips(names, pair_sums(names))
    dx0, d_ffn1, dgu1, a1, h1, dyh1 = _ffn_bwd(dx1, x, gains["ffn1_norm"], g1, u1, w["ffn1_w_gu"], w["ffn1_w_down"],
                                               "ffn1_bwd")
    dw([("ffn1_w_gu", h1, dgu1)], 1408, "dw_ffn1_gu", hook)
    reduced(names, hook)
    names = ("ffn1_w_gu",)
    hook = chips(names, pair_sums(names))
    dw([("ffn1_w_down", a1, dyh1)], 512, "dw_ffn1_down", hook)
    reduced(names, hook)
    rows = {0: d_ffn1, 1: d_mix, 2: d_xattn, 3: d_mem, 4: d_ffn2, 5: d_final, SINK_ROW: d_sinks,
            BIAS_ROW: d_rel_bias.reshape(1, -1), LOSS_ROW: loss[0, 0:1]}
    rows.update({CONV_ROW + j: dcw[j] for j in range(3)})
    last = ("ffn1_w_down",)
    return dx0, slabs, _rows_block(rows, d), chips(last, pair_sums(last))


def kernel(x, mem, positions, rel_bias, ffn1_norm, ffn1_w_gu, ffn1_w_down, mix_norm, w_in, sinks, conv_w, w_out, xattn_norm, mem_norm, xattn_wq, xattn_wkv, xattn_wo, ffn2_norm, ffn2_w_gu, ffn2_w_down, final_norm, loss_target, m_rel_bias, m_ffn1_norm, m_ffn1_w_gu, m_ffn1_w_down, m_mix_norm, m_w_in, m_sinks, m_conv_w, m_w_out, m_xattn_norm, m_mem_norm, m_xattn_wq, m_xattn_wkv, m_xattn_wo, m_ffn2_norm, m_ffn2_w_gu, m_ffn2_w_down, m_final_norm, v_rel_bias, v_ffn1_norm, v_ffn1_w_gu, v_ffn1_w_down, v_mix_norm, v_w_in, v_sinks, v_conv_w, v_w_out, v_xattn_norm, v_mem_norm, v_xattn_wq, v_xattn_wkv, v_xattn_wo, v_ffn2_norm, v_ffn2_w_gu, v_ffn2_w_down, v_final_norm):
    args = dict(locals())
    wts = {k: args[k] for k in WEIGHTS}
    mom = {k: args["m_" + k] for k in WEIGHTS}
    var = {k: args["v_" + k] for k in WEIGHTS}
    d = x.shape[-1]
    s_me = 2 * lax.axis_index("x") + lax.axis_index("y")

    first = ("ffn1_w_gu", "ffn1_w_down")
    rest = tuple(k for k in BIG if k not in first)
    shards = {k: wts[k][0].astype(BF) for k in first}
    cw_cols = conv_w.shape[-1]
    placed = lax.dynamic_update_slice(jnp.zeros((TAP_ROWS, d), F32), 0.5 * conv_w[0], (0, s_me * cw_cols))
    head = _Both([_GatherHook([shards[k] for k in first], [KIND[k] for k in first]), _SmallSumHook(placed)])
    shards.update(zip(rest, _to_bf16([wts[k][0] for k in rest], "gather_ffn1", head)))
    gathered, (conv_sum,) = head.results
    whole = {k: (gw if KIND[k] == "col" else gw.reshape(-1, gw.shape[-1])) for k, gw in zip(first, gathered)}
    conv_whole = conv_sum[0:3]

    gains = {k: wts[k].reshape(1, d) for k in GAIN_ROW}
    dx0, slabs, small, last_chips = _local_step(x[0], mem[0], positions[0], loss_target[0], whole, gains, rel_bias, sinks,
                                               conv_whole, shards)

    late = ("ffn1_w_gu", "ffn1_w_down", "ffn2_w_down")
    early = tuple(k for k in BIG if k not in late)
    first_ones = tuple(k for k in BIG if k != "ffn1_w_down")
    tail = _Both([last_chips, _SmallSumHook(small)])
    shard_grads = dict(zip(first_ones, _sum_share([slabs[k] for k in first_ones], "sum_share", tail)))
    (last_slabs,), (small_sum,) = tail.results
    shard_grads["ffn1_w_down"], = _sum_share([last_slabs], "sum_share_ffn1_w_down")
    quad = lambda k: (wts[k][0], shard_grads[k], mom[k][0], var[k][0])
    updates = dict(zip(early, _adamw([quad(k) for k in early], "adamw_early", 16, echo=True)))
    updates.update(zip(late, _adamw([quad(k) for k in late], "adamw_late", 8, echo=True)))
    loss = small_sum[LOSS_ROW, 0]

    out_g, out_d, out_m, out_v = {}, {}, {}, {}
    for k in BIG:
        out_g[k], out_d[k], out_m[k], out_v[k] = (a[None] for a in updates[k])

    to_call = lambda k, a: a.T if k == "rel_bias" else a.reshape(1, -1)
    from_call = lambda k, a: a.T if k == "rel_bias" else a.reshape(wts[k].shape)
    small_names = [k for k in WEIGHTS if k not in KIND]
    conv_g = lax.dynamic_slice(small_sum, (CONV_ROW, s_me * cw_cols), (3, cw_cols)).reshape(1, -1)
    bias_g = small_sum[BIAS_ROW, 0:rel_bias.size].reshape(rel_bias.shape).T
    done = _adamw_small(small_sum, conv_g, bias_g,
                        {k: tuple(to_call(k, a) for a in (wts[k], mom[k], var[k])) for k in small_names})
    for k in small_names:
        out_g[k], out_d[k], out_m[k], out_v[k] = (from_call(k, a) for a in done[k])

    return (loss, dx0[None], *[out_g[k] for k in WEIGHTS], *[out_d[k] for k in WEIGHTS],
            *[out_m[k] for k in WEIGHTS], *[out_v[k] for k in WEIGHTS])
```

```python
import functools
import math

import jax
import jax.numpy as jnp
from jax import lax
from jax.experimental import pallas as pl
from jax.experimental.pallas import tpu as pltpu

BF = jnp.bfloat16
F32 = jnp.float32
I32 = jnp.int32
S = jax.ShapeDtypeStruct

EPS = 1e-6
NEG = -1e30
POS_PAD = 1 << 30
WINDOW = 128
BLOCK = 128
HEAD_DIM = 64
SWA_HEADS = 16
SWA_KV_HEADS = 4
SWA_GROUP = SWA_HEADS // SWA_KV_HEADS
MEM_HEADS = 4
REL_BUCKETS = 32
REL_MAX_DIST = 128
ADAM_LR = 0.001
ADAM_B1 = 0.9
ADAM_B2 = 0.999
ADAM_EPS = 1e-08
ADAM_WD = 0.01
ADAM_STEP = 10

V7X_VMEM_LIMIT_BYTES = 56 * 1024 * 1024
MESH = pl.DeviceIdType.MESH
ANY = pl.BlockSpec(memory_space=pl.ANY)
VMEM_SPEC = pl.BlockSpec(memory_space=pltpu.VMEM)
SMEM_SPEC = pl.BlockSpec(memory_space=pltpu.SMEM)


def _params(sem=None):
    return pltpu.CompilerParams(dimension_semantics=sem, vmem_limit_bytes=V7X_VMEM_LIMIT_BYTES)


def _resident(shape):
    nd = len(shape)
    return pl.BlockSpec(shape, lambda *_: (0,) * nd, pipeline_mode=pl.Buffered(1))


def _acc_spec(shape):
    nd = len(shape)
    return pl.BlockSpec(shape, lambda *_: (0,) * nd)


def _call(body, *, name, grid, out_shape, in_specs, out_specs, args, sem, scratch_shapes=(), aliases=None, hook=None):
    aliases = aliases or {}
    if hook is None:
        return pl.pallas_call(body, name=name, grid=grid, out_shape=out_shape, in_specs=in_specs, out_specs=out_specs,
                              scratch_shapes=list(scratch_shapes), input_output_aliases=aliases,
                              compiler_params=_params(sem))(*args)
    n_in, n_out, n_scr = len(in_specs), len(out_shape), len(scratch_shapes)
    h_in, h_out = len(hook.ins), len(hook.out_shape)

    def at_step(pick):
        conds = [pl.program_id(ax) == pick(size) for ax, size in enumerate(grid)]
        return functools.reduce(jnp.logical_and, conds)

    def hosted(*refs):
        k_in, x_in = refs[:n_in], refs[n_in:n_in + h_in]
        o0 = n_in + h_in
        k_out, x_out = refs[o0:o0 + n_out], refs[o0 + n_out:o0 + n_out + h_out]
        s0 = o0 + n_out + h_out
        k_scr, x_scr = refs[s0:s0 + n_scr], refs[s0 + n_scr:]

        @pl.when(at_step(lambda size: 0))
        def _():
            hook.start(x_in, x_out, x_scr)
        body(*k_in, *k_out, *k_scr)
        early = bool(hook.lead) and grid[0] > hook.lead
        if early:
            on_axis0 = pl.program_id(0) == grid[0] - 1 - hook.lead
            rest_first = [pl.program_id(ax) == 0 for ax in range(1, len(grid))]

            @pl.when(functools.reduce(jnp.logical_and, rest_first, on_axis0))
            def _():
                hook.relay(x_in, x_out, x_scr)

        @pl.when(at_step(lambda size: size - 1))
        def _():
            if not early:
                hook.relay(x_in, x_out, x_scr)
            hook.finish(x_in, x_out, x_scr)

    res = pl.pallas_call(
        hosted, name=name, grid=grid, out_shape=tuple(out_shape) + tuple(hook.out_shape),
        in_specs=list(in_specs) + [ANY] * h_in, out_specs=tuple(out_specs) + (ANY,) * h_out,
        scratch_shapes=list(scratch_shapes) + list(hook.scratch), input_output_aliases=aliases,
        compiler_params=_params(("arbitrary",) * len(grid)),
    )(*args, *hook.ins)
    hook.results = res[n_out:]
    return res[:n_out]


def _nn(a, b):
    return jnp.dot(a, b, preferred_element_type=F32)


def _nt(a, b):
    return lax.dot_general(a, b, (((1,), (1,)), ((), ())), preferred_element_type=F32)


def _tn(a, b):
    return lax.dot_general(a, b, (((0,), (0,)), ((), ())), preferred_element_type=F32)


def _sigmoid(v):
    return 1.0 / (1.0 + jnp.exp(-v))


def _rms(x):
    r = lax.rsqrt(jnp.mean(x * x, axis=-1, keepdims=True) + EPS)
    return x * r, r


def _rms_bwd(dh, n, r, g):
    dn = dh * g
    dx = r * (dn - n * jnp.mean(dn * n, axis=-1, keepdims=True))
    return dx, jnp.sum(dh * n, axis=0, keepdims=True)


def _ffn_fwd(x, gn, wgu, wd, name, hook=None, head=None):
    t, d = x.shape
    f = wd.shape[0]
    tm, fc = 256, 1408

    def body(x_ref, gn_ref, wgu_ref, wd_ref, *rest):
        xv = x_ref[...]
        n, _ = _rms(xv)
        h = (n * gn_ref[...]).astype(BF)
        g_ref, u_ref = rest[-2:] if head is None else rest[3:5]
        acc = jnp.zeros((tm, d), F32)
        for c0 in range(0, f, fc):
            g = _nn(h, wgu_ref[:, c0:c0 + fc])
            u = _nn(h, wgu_ref[:, f + c0:f + c0 + fc])
            g_ref[:, c0:c0 + fc] = g.astype(BF)
            u_ref[:, c0:c0 + fc] = u.astype(BF)
            a = (g * _sigmoid(g)) * u
            acc = acc + _nn(a.astype(BF), wd_ref[c0:c0 + fc, :])
        y = xv + 0.5 * acc
        if head is None:
            rest[0][...] = y
            return
        gf_ref, t_ref, dy_ref, _, _, dgf_ref, loss_ref = rest

        @pl.when(pl.program_id(0) == 0)
        def _():
            dgf_ref[...] = jnp.zeros_like(dgf_ref)
            loss_ref[...] = jnp.zeros_like(loss_ref)
        ny, ry = _rms(y)
        gf = gf_ref[...]
        err = ny * gf - t_ref[...]
        loss_ref[...] += 0.5 * jnp.sum(jnp.sum(err * err, axis=-1, keepdims=True) / d, axis=0, keepdims=True)
        dy, dgf = _rms_bwd(err / d, ny, ry, gf)
        dy_ref[...] = dy
        dgf_ref[...] += dgf

    row = pl.BlockSpec((tm, d), lambda i: (i, 0))
    frow = pl.BlockSpec((tm, f), lambda i: (i, 0))
    in_specs = [row, _resident((1, d)), _resident(wgu.shape), _resident(wd.shape)]
    out_shape = (S((t, d), F32), S((t, f), BF), S((t, f), BF))
    out_specs = (row, frow, frow)
    args = (x, gn, wgu, wd)
    if head is not None:
        in_specs += [_resident((1, d)), row]
        out_shape += (S((1, d), F32), S((1, 128), F32))
        out_specs += (_acc_spec((1, d)), _acc_spec((1, 128)))
        args += tuple(head)
    return _call(body, name=name, grid=(t // tm,), out_shape=out_shape, in_specs=in_specs, out_specs=out_specs,
                 sem=("parallel",) if head is None else ("arbitrary",), args=args, hook=hook)


def _mix_proj(x, gn, w_in, hook=None):
    t, d = x.shape
    tm = 256
    nqkv = 1536
    ne = w_in.shape[1] - nqkv

    def body(x_ref, gn_ref, w_ref, qkv_ref, e_ref):
        n, _ = _rms(x_ref[...])
        h = (n * gn_ref[...]).astype(BF)
        qkv_ref[...] = _nn(h, w_ref[:, 0:nqkv]).astype(BF)
        for c0 in range(0, ne, 1024):
            e_ref[:, c0:c0 + 1024] = _nn(h, w_ref[:, nqkv + c0:nqkv + c0 + 1024]).astype(BF)

    return _call(
        body, name="mix_proj", grid=(t // tm,),
        out_shape=(S((t, nqkv), BF), S((t, ne), BF)),
        in_specs=[pl.BlockSpec((tm, d), lambda i: (i, 0)), _resident((1, d)), _resident(w_in.shape)],
        out_specs=(pl.BlockSpec((tm, nqkv), lambda i: (i, 0)), pl.BlockSpec((tm, ne), lambda i: (i, 0))),
        sem=("parallel",), args=(x, gn, w_in), hook=hook)


def _t5_bucket(rel):
    n = jnp.maximum(rel, 0)
    max_exact = REL_BUCKETS // 2
    nf = jnp.maximum(n, 1).astype(F32)
    large = max_exact + (jnp.log(nf / max_exact) / math.log(REL_MAX_DIST / max_exact)
                         * (REL_BUCKETS - max_exact)).astype(I32)
    large = jnp.minimum(large, REL_BUCKETS - 1)
    return jnp.where(n < max_exact, n, large)


def _block_rel():
    i = lax.broadcasted_iota(I32, (BLOCK, 2 * BLOCK), 0)
    j = lax.broadcasted_iota(I32, (BLOCK, 2 * BLOCK), 1)
    return i + BLOCK - j


def _bias_build(rel_bias):
    def body(rb_ref, o_ref):
        bucket = _t5_bucket(_block_rel())
        for h in range(SWA_HEADS):
            acc = jnp.zeros((BLOCK, 2 * BLOCK), F32)
            for b in range(REL_BUCKETS):
                acc = jnp.where(bucket == b, rb_ref[b, h], acc)
            o_ref[h] = acc

    return pl.pallas_call(
        body, name="bias_build", out_shape=S((SWA_HEADS, BLOCK, 2 * BLOCK), F32),
        in_specs=[SMEM_SPEC], out_specs=VMEM_SPEC,
    )(rel_bias)


GROUP_ROWS = SWA_GROUP * BLOCK


def _swa_visible(b, pq_ref, pkp_ref, pkc_ref):
    pk = jnp.concatenate([pkp_ref[...], pkc_ref[...]], axis=1)
    col = lax.broadcasted_iota(I32, (1, 2 * BLOCK), 1)
    pk = jnp.where(jnp.logical_and(b == 0, col < BLOCK), POS_PAD, pk)
    rel = jnp.concatenate([pq_ref[...]] * SWA_GROUP, axis=0) - pk
    return jnp.logical_and(rel >= 0, rel < WINDOW)


def _group_heads(ref, hk):
    h0 = hk * SWA_GROUP
    return jnp.concatenate([ref[:, (h0 + g) * HEAD_DIM:(h0 + g + 1) * HEAD_DIM] for g in range(SWA_GROUP)], axis=0)


def _group_sinks(sink_ref, hk):
    row = lax.broadcasted_iota(I32, (GROUP_ROWS, 1), 0)
    col = jnp.zeros((GROUP_ROWS, 1), F32) + sink_ref[0, hk * SWA_GROUP]
    for g in range(1, SWA_GROUP):
        col = jnp.where(row >= g * BLOCK, sink_ref[0, hk * SWA_GROUP + g], col)
    return col


def _swa_probs(qg, kh, vis, bias, sink):
    s = _nt(qg, kh) * (HEAD_DIM ** -0.5)
    s = jnp.where(vis, s + bias, NEG)
    m = jnp.maximum(jnp.max(s, axis=-1, keepdims=True), sink)
    p = jnp.exp(s - m)
    ps = jnp.exp(sink - m)
    inv = 1.0 / (jnp.sum(p, axis=-1, keepdims=True) + ps)
    return p * inv, ps * inv


def _swa_fwd(qkv, pos_col, pos_row, bias_t, sinks, hook=None):
    t = qkv.shape[0]
    nb = t // BLOCK
    qw = SWA_HEADS * HEAD_DIM
    kw = SWA_KV_HEADS * HEAD_DIM

    def body(q_ref, kp_ref, kc_ref, vp_ref, vc_ref, pq_ref, pkp_ref, pkc_ref, bias_ref, sink_ref, o_ref):
        b = pl.program_id(0)
        vis = _swa_visible(b, pq_ref, pkp_ref, pkc_ref)
        k2 = jnp.concatenate([kp_ref[...], kc_ref[...]], axis=0)
        v2 = jnp.concatenate([vp_ref[...], vc_ref[...]], axis=0)
        for hk in range(SWA_KV_HEADS):
            kh = k2[:, hk * HEAD_DIM:(hk + 1) * HEAD_DIM]
            vh = v2[:, hk * HEAD_DIM:(hk + 1) * HEAD_DIM]
            pn, _ = _swa_probs(_group_heads(q_ref, hk), kh, vis, bias_ref[hk], _group_sinks(sink_ref, hk))
            o = _nn(pn.astype(BF), vh)
            for g in range(SWA_GROUP):
                h = hk * SWA_GROUP + g
                o_ref[:, h * HEAD_DIM:(h + 1) * HEAD_DIM] = o[g * BLOCK:(g + 1) * BLOCK]

    prev = lambda b: jnp.maximum(b - 1, 0)
    return _call(
        body, name="swa_fwd", grid=(nb,), out_shape=(S((t, qw), F32),),
        in_specs=[
            pl.BlockSpec((BLOCK, qw), lambda b: (b, 0)),
            pl.BlockSpec((BLOCK, kw), lambda b: (prev(b), qw // kw)),
            pl.BlockSpec((BLOCK, kw), lambda b: (b, qw // kw)),
            pl.BlockSpec((BLOCK, kw), lambda b: (prev(b), qw // kw + 1)),
            pl.BlockSpec((BLOCK, kw), lambda b: (b, qw // kw + 1)),
            pl.BlockSpec((BLOCK, 1), lambda b: (b, 0)),
            pl.BlockSpec((1, BLOCK), lambda b: (0, prev(b))),
            pl.BlockSpec((1, BLOCK), lambda b: (0, b)),
            _resident(bias_t.shape),
            SMEM_SPEC,
        ],
        out_specs=(pl.BlockSpec((BLOCK, qw), lambda b: (b, 0)),),
        sem=("parallel",), args=(qkv, qkv, qkv, qkv, qkv, pos_col, pos_row, pos_row, bias_t, sinks), hook=hook)[0]


HALO = 16


def _conv_taps(z, zh, first):
    tm = z.shape[0]
    zh = jnp.where(first, 0.0, zh)
    row = lax.broadcasted_iota(I32, (tm, 1), 0)
    z1 = jnp.where(row == 0, zh[HALO - 1:HALO, :], pltpu.roll(z, 1, 0))
    z2 = jnp.where(row == 0, zh[HALO - 2:HALO - 1, :], jnp.where(row == 1, zh[HALO - 1:HALO, :], pltpu.roll(z, 2, 0)))
    return z1, z2


def _mix_out_fwd(e, attn, conv_w, w_out, x, hook=None):
    t, d = x.shape
    tm = 256
    hb = tm // HALO
    f32 = lambda ref: ref[...].astype(F32)

    def body(c_ref, b_ref, u_ref, ga_ref, gc_ref, ch_ref, uh_ref, attn_ref, cw_ref, w_ref, x_ref, xo_ref, mg_ref):
        i = pl.program_id(0)
        z = f32(c_ref) * f32(u_ref)
        z1, z2 = _conv_taps(z, f32(ch_ref) * f32(uh_ref), i == 0)
        s = cw_ref[0:1, :] * z2 + cw_ref[1:2, :] * z1 + cw_ref[2:3, :] * z
        conv = f32(b_ref) * s
        merged = (_sigmoid(f32(ga_ref)) * attn_ref[...] + _sigmoid(f32(gc_ref)) * conv).astype(BF)
        mg_ref[...] = merged
        xo_ref[...] = x_ref[...] + _nn(merged, w_ref[...])

    ecol = lambda cb: pl.BlockSpec((tm, d), lambda i: (i, cb))
    halo = lambda cb: pl.BlockSpec((HALO, d), lambda i: (jnp.maximum(i * hb - 1, 0), cb))
    row = pl.BlockSpec((tm, d), lambda i: (i, 0))
    return _call(
        body, name="mix_out_fwd", grid=(t // tm,),
        out_shape=(S((t, d), F32), S((t, d), BF)),
        in_specs=[ecol(0), ecol(1), ecol(2), ecol(3), ecol(4), halo(0), halo(2), row,
                  _resident(conv_w.shape), _resident(w_out.shape), row],
        out_specs=(row, row),
        sem=("parallel",), args=(e, e, e, e, e, e, e, attn, conv_w, w_out, x), hook=hook)


def _mem_kv(mem, gm, wkv):
    m, d = mem.shape

    def body(mem_ref, gm_ref, w_ref, mh_ref, kv_ref):
        n, _ = _rms(mem_ref[...])
        mh = (n * gm_ref[...]).astype(BF)
        mh_ref[...] = mh
        kv_ref[...] = _nn(mh, w_ref[...]).astype(BF)

    return pl.pallas_call(
        body, name="mem_kv", out_shape=(S((m, d), BF), S((m, wkv.shape[1]), BF)),
        compiler_params=_params(),
    )(mem, gm, wkv)


def _xattn_probs(qh, kh):
    s = _nt(qh, kh) * (kh.shape[1] ** -0.5)
    p = jnp.exp(s - jnp.max(s, axis=-1, keepdims=True))
    return p * (1.0 / jnp.sum(p, axis=-1, keepdims=True))


def _xattn_fwd(x, gn, wq, kv, wo):
    t, d = x.shape
    tm = 256
    hd = d // MEM_HEADS

    def body(x_ref, gn_ref, wq_ref, kv_ref, wo_ref, xo_ref, q_ref, o_ref):
        xv = x_ref[...]
        n, _ = _rms(xv)
        q = _nn((n * gn_ref[...]).astype(BF), wq_ref[...]).astype(BF)
        q_ref[...] = q
        outs = []
        for hh in range(MEM_HEADS):
            p = _xattn_probs(q[:, hh * hd:(hh + 1) * hd], kv_ref[:, hh * hd:(hh + 1) * hd])
            outs.append(_nn(p.astype(BF), kv_ref[:, d + hh * hd:d + (hh + 1) * hd]))
        o = jnp.concatenate(outs, axis=1).astype(BF)
        o_ref[...] = o
        xo_ref[...] = xv + _nn(o, wo_ref[...])

    row = pl.BlockSpec((tm, d), lambda i: (i, 0))
    return pl.pallas_call(
        body, name="xattn_fwd", grid=(t // tm,),
        out_shape=(S((t, d), F32), S((t, d), BF), S((t, d), BF)),
        in_specs=[row, _resident((1, d)), _resident(wq.shape), _resident(kv.shape), _resident(wo.shape)],
        out_specs=(row, row, row),
        compiler_params=_params(("parallel",)),
    )(x, gn, wq, kv, wo)


def _ffn_bwd(dxo, x, gn, g, u, wgu, wd, name, hook=None):
    t, d = x.shape
    f = wd.shape[0]
    tm, fc = 256, 1408

    def body(dxo_ref, x_ref, gn_ref, g_ref, u_ref, wgu_ref, wd_ref, dx_ref, dgn_ref, dgu_ref, a_ref, h_ref, dyh_ref):
        @pl.when(pl.program_id(0) == 0)
        def _():
            dgn_ref[...] = jnp.zeros_like(dgn_ref)
        dxov = dxo_ref[...]
        dyh = (0.5 * dxov).astype(BF)
        dyh_ref[...] = dyh
        n, r = _rms(x_ref[...])
        gnv = gn_ref[...]
        h_ref[...] = (n * gnv).astype(BF)
        dh = jnp.zeros((tm, d), F32)
        for c0 in range(0, f, fc):
            gv = g_ref[:, c0:c0 + fc].astype(F32)
            uv = u_ref[:, c0:c0 + fc].astype(F32)
            da = _nt(dyh, wd_ref[c0:c0 + fc, :])
            sg = _sigmoid(gv)
            silu = gv * sg
            a_ref[:, c0:c0 + fc] = (silu * uv).astype(BF)
            dg = (da * uv * (sg * (1.0 + gv * (1.0 - sg)))).astype(BF)
            du = (da * silu).astype(BF)
            dgu_ref[:, c0:c0 + fc] = dg
            dgu_ref[:, f + c0:f + c0 + fc] = du
            dh = dh + _nt(dg, wgu_ref[:, c0:c0 + fc]) + _nt(du, wgu_ref[:, f + c0:f + c0 + fc])
        dx, dgn = _rms_bwd(dh, n, r, gnv)
        dx_ref[...] = dxov + dx
        dgn_ref[...] += dgn

    row = pl.BlockSpec((tm, d), lambda i: (i, 0))
    frow = pl.BlockSpec((tm, f), lambda i: (i, 0))
    return _call(
        body, name=name, grid=(t // tm,),
        out_shape=(S((t, d), F32), S((1, d), F32), S((t, 2 * f), BF), S((t, f), BF), S((t, d), BF), S((t, d), BF)),
        in_specs=[row, row, _resident((1, d)), frow, frow, _resident(wgu.shape), _resident(wd.shape)],
        out_specs=(row, _acc_spec((1, d)), pl.BlockSpec((tm, 2 * f), lambda i: (i, 0)), frow, row, row),
        sem=("arbitrary",), args=(dxo, x, gn, g, u, wgu, wd), hook=hook)


def _dw_pair(pairs, tn, name, kind, hook=None):
    npairs = len(pairs)
    t, ka = pairs[0][0].shape
    nb = pairs[0][1].shape[1]
    tt = 2048 if npairs == 1 else 1024
    nt, nj = t // tt, nb // tn
    col = kind == "col"
    rh = ka // 2 if col else ka // 8
    tile = (rh, tn) if col else (4, rh, tn)
    half = (rh, nb) if col else (4, rh, nb)
    lead = (slice(None),) * (len(tile) - 1)

    def body(*refs):
        ins, mines, sibs = refs[:2 * npairs], refs[2 * npairs:3 * npairs], refs[3 * npairs:4 * npairs]
        acc_ref, stage, ssem, rsem = refs[4 * npairs:]
        j, k = pl.program_id(0), pl.program_id(1)
        x, y, c = _place()
        sibling = (x, y, 1 - c)

        def send(p, slot, jj):
            dst = sibs[p].at[lead + (pl.ds(pl.multiple_of(jj * tn, 128), tn),)]
            return _remote(stage.at[slot], dst, ssem.at[slot], rsem.at[p], sibling)

        def rows(s, whose):
            return acc_ref[pl.ds(pl.multiple_of(s * 2 * rh + whose * rh, 16), rh), :].astype(BF)

        def step(p):
            a_ref, b_ref, mine_ref = ins[2 * p], ins[2 * p + 1], mines[p]

            @pl.when(k == 0)
            def _():
                acc_ref[...] = jnp.zeros_like(acc_ref)
            acc_ref[...] += _tn(a_ref[...], b_ref[...])

            @pl.when(k == nt - 1)
            def _():
                slot = j % 2

                @pl.when(j >= 2)
                def _():
                    send(p, slot, 0).wait_send()
                if col:
                    mine_ref[...] = rows(0, c)
                    stage[slot] = rows(0, 1 - c)
                else:
                    for s in range(4):
                        mine_ref[s] = rows(s, c)
                        stage[slot, s] = rows(s, 1 - c)
                send(p, slot, j - p * nj).start()

        for p in range(npairs):
            pl.when(j // nj == p)(functools.partial(step, p))

        @pl.when(jnp.logical_and(j == npairs * nj - 1, k == nt - 1))
        def _():
            for jj in range(max(npairs * nj - 2, 0), npairs * nj):
                send(0, jj % 2, 0).wait_send()
            for p in range(npairs):
                _remote(sibs[p], sibs[p], ssem.at[0], rsem.at[p], sibling).wait_recv()

    in_specs, args = [], []
    for p, (a, b) in enumerate(pairs):
        on = lambda j, p=p: j // nj == p
        in_specs += [pl.BlockSpec((tt, ka), lambda j, k, on=on: (jnp.where(on(j), k, 0), 0)),
                     pl.BlockSpec((tt, tn), lambda j, k, on=on, p=p: (jnp.where(on(j), k, 0), jnp.clip(j - p * nj, 0, nj - 1)))]
        args += [a, b]
    mine_spec = lambda p: pl.BlockSpec(tile, (lambda j, k: (0, jnp.clip(j - p * nj, 0, nj - 1))) if col
                                       else (lambda j, k: (0, 0, jnp.clip(j - p * nj, 0, nj - 1))))
    res = _call(
        body, name=name, grid=(npairs * nj, nt), out_shape=(S(half, BF),) * (2 * npairs),
        in_specs=in_specs, out_specs=tuple(mine_spec(p) for p in range(npairs)) + (ANY,) * npairs,
        scratch_shapes=[pltpu.VMEM((ka, tn), F32), pltpu.VMEM((2,) + tile, BF), pltpu.SemaphoreType.DMA((2,)),
                        pltpu.SemaphoreType.DMA((npairs,))],
        sem=("arbitrary", "arbitrary"), args=args, hook=hook)
    return [(res[p], res[npairs + p]) for p in range(npairs)]


def _xattn_bwd(dxo, x, gn, q, kv, wq, wo, mh, mem, wkv, hook=None):
    t, d = x.shape
    tm = 256
    hd = d // MEM_HEADS
    nkv = kv.shape[0]
    rows, cols = wkv.shape
    rh = rows // 2

    def body(dxo_ref, x_ref, gn_ref, q_ref, kv_ref, wq_ref, wo_ref, mh_ref, mem_ref, wkv_ref,
             dx_ref, dgn_ref, dxh_ref, h_ref, dq_ref, mine_ref, sib_ref, dgm_ref, dkv_ref, whole, ssem, rsem):
        @pl.when(pl.program_id(0) == 0)
        def _():
            dgn_ref[...] = jnp.zeros_like(dgn_ref)
            dkv_ref[...] = jnp.zeros_like(dkv_ref)
        dxov = dxo_ref[...]
        dxh = dxov.astype(BF)
        dxh_ref[...] = dxh
        do = _nt(dxh, wo_ref[...]).astype(BF)
        dqs = []
        for hh in range(MEM_HEADS):
            lo, hi = hh * hd, (hh + 1) * hd
            qh = q_ref[:, lo:hi]
            kh = kv_ref[:, lo:hi]
            vh = kv_ref[:, d + lo:d + hi]
            doh = do[:, lo:hi]
            p = _xattn_probs(qh, kh)
            dp = _nt(doh, vh)
            ds = (p * (dp - jnp.sum(p * dp, axis=-1, keepdims=True)) * (hd ** -0.5)).astype(BF)
            dqs.append(_nn(ds, kh))
            dkv_ref[:, lo:hi] += _tn(ds, qh)
            dkv_ref[:, d + lo:d + hi] += _tn(p.astype(BF), doh)
        dq = jnp.concatenate(dqs, axis=1).astype(BF)
        dq_ref[...] = dq
        n, r = _rms(x_ref[...])
        gnv = gn_ref[...]
        h_ref[...] = (n * gnv).astype(BF)
        dx, dgn = _rms_bwd(_nt(dq, wq_ref[...]), n, r, gnv)
        dx_ref[...] = dxov + dx
        dgn_ref[...] += dgn

        @pl.when(pl.program_id(0) == pl.num_programs(0) - 1)
        def _():
            dkvb = dkv_ref[...].astype(BF)
            dmh = _nt(dkvb, wkv_ref[...])
            nm, _ = _rms(mem_ref[...])
            px, py, c = _place()
            whole[...] = _tn(mh_ref[...], dkvb).astype(BF)
            cp = _remote(whole.at[pl.ds(pl.multiple_of((1 - c) * rh, 16), rh), :], sib_ref, ssem, rsem, (px, py, 1 - c))
            cp.start()
            mine_ref[...] = whole[pl.ds(pl.multiple_of(c * rh, 16), rh), :]
            dgm_ref[...] = jnp.sum(dmh * nm, axis=0, keepdims=True)
            cp.wait()

    row = pl.BlockSpec((tm, d), lambda i: (i, 0))
    dx, dgn, dxh, h, dq, mine, sib, dgm = _call(
        body, name="xattn_bwd", grid=(t // tm,),
        out_shape=(S((t, d), F32), S((1, d), F32), S((t, d), BF), S((t, d), BF), S((t, d), BF),
                   S((rh, cols), BF), S((rh, cols), BF), S((1, d), F32)),
        in_specs=[row, row, _resident((1, d)), row, _resident(kv.shape), _resident(wq.shape), _resident(wo.shape),
                  _resident(mh.shape), _resident(mem.shape), _resident(wkv.shape)],
        out_specs=(row, _acc_spec((1, d)), row, row, row, _acc_spec((rh, cols)), ANY, _acc_spec((1, d))),
        scratch_shapes=[pltpu.VMEM((nkv, 2 * d), F32), pltpu.VMEM((rows, cols), BF),
                        pltpu.SemaphoreType.DMA, pltpu.SemaphoreType.DMA],
        sem=("arbitrary",), args=(dxo, x, gn, q, kv, wq, wo, mh, mem, wkv), hook=hook)
    return dx, dgn, dxh, h, dq, (mine, sib), dgm


def _mix_out_bwd(dxo, e, attn, conv_w, w_out, nqkv, hook=None):
    t, d = attn.shape
    tm = 256
    nt = t // tm
    f32 = lambda ref: ref[...].astype(F32)

    def body(dxo_ref, dxn_ref, c_ref, b_ref, u_ref, ga_ref, gc_ref, ch_ref, uh_ref, bn_ref, gcn_ref,
             attn_ref, cw_ref, w_ref, dattn_ref, de_ref, dcw_ref, dxh_ref):
        i = pl.program_id(0)

        @pl.when(i == 0)
        def _():
            dcw_ref[...] = jnp.zeros_like(dcw_ref)
        dxh = dxo_ref[...].astype(BF)
        dxh_ref[...] = dxh
        w = w_ref[...]
        dm = _nt(dxh, w)
        dmn = _nt(dxn_ref[...].astype(BF), w)
        cv, bv, uv = f32(c_ref), f32(b_ref), f32(u_ref)
        sga = _sigmoid(f32(ga_ref))
        sgc = _sigmoid(f32(gc_ref))
        z = cv * uv
        z1, z2 = _conv_taps(z, f32(ch_ref) * f32(uh_ref), i == 0)
        w0, w1, w2 = cw_ref[0:1, :], cw_ref[1:2, :], cw_ref[2:3, :]
        s = w0 * z2 + w1 * z1 + w2 * z
        av = attn_ref[...]
        dattn_ref[...] = (dm * sga).astype(BF)
        dconv = dm * sgc
        ds = dconv * bv
        dsn = jnp.where(i == nt - 1, 0.0, dmn * _sigmoid(gcn_ref[0:8, :].astype(F32)) * bn_ref[0:8, :].astype(F32))
        row = lax.broadcasted_iota(I32, (tm, 1), 0)
        dsp1 = jnp.where(row == tm - 1, dsn[0:1, :], pltpu.roll(ds, tm - 1, 0))
        dsp2 = jnp.where(row == tm - 2, dsn[0:1, :], jnp.where(row == tm - 1, dsn[1:2, :], pltpu.roll(ds, tm - 2, 0)))
        dz = w2 * ds + w1 * dsp1 + w0 * dsp2
        de_ref[:, nqkv:nqkv + d] = (dz * uv).astype(BF)
        de_ref[:, nqkv + d:nqkv + 2 * d] = (dconv * s).astype(BF)
        de_ref[:, nqkv + 2 * d:nqkv + 3 * d] = (dz * cv).astype(BF)
        de_ref[:, nqkv + 3 * d:nqkv + 4 * d] = (dm * av * sga * (1.0 - sga)).astype(BF)
        de_ref[:, nqkv + 4 * d:nqkv + 5 * d] = (dm * (bv * s) * sgc * (1.0 - sgc)).astype(BF)
        dcw_ref[0:1, :] += jnp.sum(ds * z2, axis=0, keepdims=True)
        dcw_ref[1:2, :] += jnp.sum(ds * z1, axis=0, keepdims=True)
        dcw_ref[2:3, :] += jnp.sum(ds * z, axis=0, keepdims=True)

    ecol = lambda cb: pl.BlockSpec((tm, d), lambda i: (i, cb))
    prev = lambda cb: pl.BlockSpec((HALO, d), lambda i: (jnp.maximum(i * (tm // HALO) - 1, 0), cb))
    nxt = lambda rows, cb: pl.BlockSpec((rows, d), lambda i: (jnp.minimum((i + 1) * (tm // rows), t // rows - 1), cb))
    row = pl.BlockSpec((tm, d), lambda i: (i, 0))
    return _call(
        body, name="mix_out_bwd", grid=(nt,),
        out_shape=(S((t, d), BF), S((t, nqkv + 5 * d), BF), S((8, d), F32), S((t, d), BF)),
        in_specs=[row, nxt(8, 0), ecol(0), ecol(1), ecol(2), ecol(3), ecol(4), prev(0), prev(2), nxt(HALO, 1), nxt(HALO, 4),
                  row, _resident(conv_w.shape), _resident(w_out.shape)],
        out_specs=(row, pl.BlockSpec((tm, nqkv + 5 * d), lambda i: (i, 0)), _acc_spec((8, d)), row),
        sem=("arbitrary",), args=(dxo, dxo, e, e, e, e, e, e, e, e, e, attn, conv_w, w_out), hook=hook)


def _swa_bwd(qkv, dattn, pos_col, pos_row, bias_t, sinks, dp, hook=None):
    t = qkv.shape[0]
    nb = t // BLOCK
    qw = SWA_HEADS * HEAD_DIM
    kw = SWA_KV_HEADS * HEAD_DIM

    def body(q_ref, kp_ref, kc_ref, vp_ref, vc_ref, do_ref, pq_ref, pkp_ref, pkc_ref, bias_ref, sink_ref,
             dp_in, dp_ref, gb_ref, dsk_ref, carry_ref, dq_ref):
        b = pl.program_id(0)

        @pl.when(b == 0)
        def _():
            gb_ref[...] = jnp.zeros_like(gb_ref)
            dsk_ref[...] = jnp.zeros_like(dsk_ref)
            carry_ref[...] = jnp.zeros_like(carry_ref)
            dq_ref[...] = jnp.zeros_like(dq_ref)
        dp_ref[:, 0:qw] = dq_ref[...]

        @pl.when(b < nb)
        def _():
            vis = _swa_visible(b, pq_ref, pkp_ref, pkc_ref)
            k2 = jnp.concatenate([kp_ref[...], kc_ref[...]], axis=0)
            v2 = jnp.concatenate([vp_ref[...], vc_ref[...]], axis=0)
            for hk in range(SWA_KV_HEADS):
                lo, hi = hk * HEAD_DIM, (hk + 1) * HEAD_DIM
                kh = k2[:, lo:hi]
                vh = v2[:, lo:hi]
                qg = _group_heads(q_ref, hk)
                dog = _group_heads(do_ref, hk)
                pn, psn = _swa_probs(qg, kh, vis, bias_ref[hk], _group_sinks(sink_ref, hk))
                dp = _nt(dog, vh)
                delta = jnp.sum(pn * dp, axis=-1, keepdims=True)
                ds = pn * (dp - delta)
                gb_ref[hk] += ds
                dsk_ref[hk] += -psn * delta
                dsb = (ds * (HEAD_DIM ** -0.5)).astype(BF)
                dqg = _nn(dsb, kh).astype(BF)
                for g in range(SWA_GROUP):
                    h = hk * SWA_GROUP + g
                    dq_ref[:, h * HEAD_DIM:(h + 1) * HEAD_DIM] = dqg[g * BLOCK:(g + 1) * BLOCK]
                dk = _tn(dsb, qg)
                dv = _tn(pn.astype(BF), dog)
                dp_ref[:, qw + lo:qw + hi] = (carry_ref[:, lo:hi] + dk[0:BLOCK]).astype(BF)
                dp_ref[:, qw + kw + lo:qw + kw + hi] = (carry_ref[:, kw + lo:kw + hi] + dv[0:BLOCK]).astype(BF)
                carry_ref[:, lo:hi] = dk[BLOCK:2 * BLOCK]
                carry_ref[:, kw + lo:kw + hi] = dv[BLOCK:2 * BLOCK]

        @pl.when(b == nb)
        def _():
            dp_ref[:, qw:qw + 2 * kw] = carry_ref[...].astype(BF)

    cur = lambda b: jnp.minimum(b, nb - 1)
    prev = lambda b: jnp.maximum(cur(b) - 1, 0)
    return _call(
        body, name="swa_bwd", grid=(nb + 1,),
        out_shape=(S(dp.shape, BF), S((SWA_KV_HEADS, GROUP_ROWS, 2 * BLOCK), F32), S((SWA_KV_HEADS, GROUP_ROWS, 1), F32)),
        in_specs=[
            pl.BlockSpec((BLOCK, qw), lambda b: (cur(b), 0)),
            pl.BlockSpec((BLOCK, kw), lambda b: (prev(b), qw // kw)),
            pl.BlockSpec((BLOCK, kw), lambda b: (cur(b), qw // kw)),
            pl.BlockSpec((BLOCK, kw), lambda b: (prev(b), qw // kw + 1)),
            pl.BlockSpec((BLOCK, kw), lambda b: (cur(b), qw // kw + 1)),
            pl.BlockSpec((BLOCK, qw), lambda b: (cur(b), 0)),
            pl.BlockSpec((BLOCK, 1), lambda b: (cur(b), 0)),
            pl.BlockSpec((1, BLOCK), lambda b: (0, prev(b))),
            pl.BlockSpec((1, BLOCK), lambda b: (0, cur(b))),
            _resident(bias_t.shape),
            SMEM_SPEC,
            ANY,
        ],
        out_specs=(
            pl.BlockSpec((BLOCK, qw + 2 * kw), lambda b: (jnp.maximum(b - 1, 0), 0)),
            _acc_spec((SWA_KV_HEADS, GROUP_ROWS, 2 * BLOCK)),
            _acc_spec((SWA_KV_HEADS, GROUP_ROWS, 1)),
        ),
        scratch_shapes=[pltpu.VMEM((BLOCK, 2 * kw), F32), pltpu.VMEM((BLOCK, qw), BF)],
        aliases={11: 0}, sem=("arbitrary",),
        args=(qkv, qkv, qkv, qkv, qkv, dattn, pos_col, pos_row, pos_row, bias_t, sinks, dp), hook=hook)


def _bias_reduce(gb, dsk):
    def body(gb_ref, dsk_ref, drb_ref, dsink_ref):
        bucket = _t5_bucket(_block_rel())
        for b in range(REL_BUCKETS):
            mask = bucket == b
            for h in range(SWA_HEADS):
                drb_ref[b, h] = jnp.sum(jnp.where(mask, gb_ref[h], 0.0))
        for h in range(SWA_HEADS):
            dsink_ref[0, h] = jnp.sum(dsk_ref[h])

    return pl.pallas_call(
        body, name="bias_reduce", out_shape=(S((REL_BUCKETS, SWA_HEADS), F32), S((1, SWA_HEADS), F32)),
        in_specs=[VMEM_SPEC, VMEM_SPEC], out_specs=(SMEM_SPEC, SMEM_SPEC),
    )(gb, dsk)


def _mix_in_bwd(dp, x, gn, w_in, dxo):
    t, d = x.shape
    tm = 256
    npr = dp.shape[1]

    def body(dp_ref, x_ref, gn_ref, w_ref, dxo_ref, dx_ref, dgn_ref, h_ref):
        @pl.when(pl.program_id(0) == 0)
        def _():
            dgn_ref[...] = jnp.zeros_like(dgn_ref)
        dh = jnp.zeros((tm, d), F32)
        for c0 in range(0, npr, 1024):
            c1 = min(c0 + 1024, npr)
            dh = dh + _nt(dp_ref[:, c0:c1], w_ref[:, c0:c1])
        n, r = _rms(x_ref[...])
        gnv = gn_ref[...]
        h_ref[...] = (n * gnv).astype(BF)
        dx, dgn = _rms_bwd(dh, n, r, gnv)
        dx_ref[...] = dxo_ref[...] + dx
        dgn_ref[...] += dgn

    row = pl.BlockSpec((tm, d), lambda i: (i, 0))
    return pl.pallas_call(
        body, name="mix_in_bwd", grid=(t // tm,),
        out_shape=(S((t, d), F32), S((1, d), F32), S((t, d), BF)),
        in_specs=[pl.BlockSpec((tm, npr), lambda i: (i, 0)), row, _resident((1, d)), _resident(w_in.shape), row],
        out_specs=(row, _acc_spec((1, d)), row),
        compiler_params=_params(("arbitrary",)),
    )(dp, x, gn, w_in, dxo)


CAST_STEPS = 4


def _to_bf16(arrays, name, hook=None):
    def body(*refs):
        for src, dst in zip(refs[:len(arrays)], refs[len(arrays):]):
            dst[...] = src[...].astype(BF)

    blocks = [pl.BlockSpec((a.shape[0] // CAST_STEPS, a.shape[1]), lambda i: (i, 0)) for a in arrays]
    return _call(body, name=name, grid=(CAST_STEPS,), out_shape=tuple(S(a.shape, BF) for a in arrays), in_specs=blocks,
                 out_specs=tuple(blocks), sem=("parallel",), args=list(arrays), hook=hook)


PAIR_STEPS = 4


def _add_bf16(pairs, name):
    def body(*refs):
        ins, outs = refs[:2 * len(pairs)], refs[2 * len(pairs):]
        for q, o_ref in enumerate(outs):
            o_ref[...] = (ins[2 * q][...].astype(F32) + ins[2 * q + 1][...].astype(F32)).astype(BF)

    in_specs, out_specs, out_shape, args = [], [], [], []
    for a, b in pairs:
        rows, cols = a.shape
        blk = pl.BlockSpec((rows // PAIR_STEPS, cols), lambda i: (i, 0))
        in_specs += [blk, blk]
        out_specs.append(blk)
        out_shape.append(S((rows, cols), BF))
        args += [a, b]
    return pl.pallas_call(body, name=name, grid=(PAIR_STEPS,), out_shape=tuple(out_shape), in_specs=in_specs,
                          out_specs=tuple(out_specs), compiler_params=_params(("parallel",)))(*args)


def _adam_update(w, g, m, v):
    mn = ADAM_B1 * m + (1.0 - ADAM_B1) * g
    vn = ADAM_B2 * v + (1.0 - ADAM_B2) * (g * g)
    m_hat = mn / (1.0 - ADAM_B1 ** ADAM_STEP)
    v_hat = vn / (1.0 - ADAM_B2 ** ADAM_STEP)
    return -ADAM_LR * (m_hat / (jnp.sqrt(v_hat) + ADAM_EPS) + ADAM_WD * w), mn, vn


def _adamw(quads, name, steps, echo=False, hook=None):
    n_out = 4 if echo else 3

    def body(*refs):
        ins, outs = refs[:4 * len(quads)], refs[4 * len(quads):]
        for q in range(len(quads)):
            w_ref, g_ref, m_ref, v_ref = ins[4 * q:4 * q + 4]
            res = outs[n_out * q:n_out * q + n_out]
            gv = g_ref[...]
            if echo:
                res[0][...] = gv
            res[-3][...], res[-2][...], res[-1][...] = _adam_update(w_ref[...], gv, m_ref[...], v_ref[...])

    in_specs, out_specs, out_shape, args = [], [], [], []
    for quad in quads:
        rows, cols = quad[0].shape
        blk = pl.BlockSpec((rows // steps, cols), lambda i: (i, 0))
        in_specs += [blk] * 4
        out_specs += [blk] * n_out
        out_shape += [S((rows, cols), F32)] * n_out
        args += list(quad)
    res = _call(body, name=name, grid=(steps,), out_shape=tuple(out_shape), in_specs=in_specs, out_specs=tuple(out_specs),
                sem=("parallel",), args=args, hook=hook)
    return [res[n_out * q:n_out * q + n_out] for q in range(len(quads))]


def _place():
    x, y, c = lax.axis_index("x"), lax.axis_index("y"), lax.axis_index("c")
    return x, y, c


OTHER_CHIPS = ((1, 0), (0, 1), (1, 1))


def _flip(v, f):
    return 1 - v if f else v


def _remote(src, dst, ssem, rsem, dev):
    return pltpu.make_async_remote_copy(src_ref=src, dst_ref=dst, send_sem=ssem, recv_sem=rsem,
                                        device_id=dev, device_id_type=MESH)


class _Both:
    def __init__(self, hooks):
        self.hooks = hooks
        self.ins = [a for h in hooks for a in h.ins]
        self.out_shape = tuple(o for h in hooks for o in h.out_shape)
        self.scratch = [x for h in hooks for x in h.scratch]

    def _each(self, ins, outs, scr):
        i = o = s = 0
        for h in self.hooks:
            ni, no, ns = len(h.ins), len(h.out_shape), len(h.scratch)
            yield h, ins[i:i + ni], outs[o:o + no], scr[s:s + ns]
            i, o, s = i + ni, o + no, s + ns

    def start(self, ins, outs, scr):
        for h, *refs in self._each(ins, outs, scr):
            h.start(*refs)

    def relay(self, ins, outs, scr):
        for h, *refs in self._each(ins, outs, scr):
            h.relay(*refs)

    def finish(self, ins, outs, scr):
        for h, *refs in self._each(ins, outs, scr):
            h.finish(*refs)

    @property
    def lead(self):
        return max(h.lead for h in self.hooks)

    @property
    def results(self):
        return [h.results for h in self.hooks]

    @results.setter
    def results(self, res):
        for h, _, mine, _ in self._each((), tuple(res), ()):
            h.results = mine


class _GatherHook:
    def __init__(self, shards, kinds, lead=2):
        self.ins, self.kinds, self.lead, n = list(shards), list(kinds), lead, len(shards)
        self.out_shape = tuple(
            S((w.shape[0], 4 * w.shape[1]), BF) if k == "col" else S((4,) + w.shape, BF) for w, k in zip(shards, kinds))
        dma = pltpu.SemaphoreType.DMA
        self.scratch = ([dma((n, 3)) for _ in range(5)] + [dma((n,)), dma((n,))]
                        + [pltpu.VMEM((3, w.shape[0] // 2, w.shape[1]), BF) for w in shards]
                        + [pltpu.VMEM(w.shape, BF) for w in shards])

    def _window(self, outs, i, s, half):
        rows, cols = self.ins[i].shape
        rh = rows // 2
        start = pl.multiple_of(half * rh, 16)
        if self.kinds[i] == "col":
            return outs[i].at[pl.ds(start, rh), pl.ds(pl.multiple_of(s * cols, 128), cols)]
        return outs[i].at[s, pl.ds(start, rh), :]

    def _copies(self, ins, outs, scr):
        n = len(ins)
        ssem, rsem, fssem, frsem, ksem, lsem, osem = scr[:7]
        land, own = scr[7:7 + n], scr[7 + n:7 + 2 * n]
        x, y, c = _place()
        sibling = (x, y, 1 - c)
        loads, stores, sends, forwards, keeps, passed = [], [], [], [], [], []
        for i in range(n):
            rows, cols = self.ins[i].shape
            rh = rows // 2
            mine = (outs[i].at[:, pl.ds(pl.multiple_of((2 * x + y) * cols, 128), cols)] if self.kinds[i] == "col"
                    else outs[i].at[2 * x + y])
            loads.append(functools.partial(pltpu.make_async_copy, ins[i], own[i], lsem.at[i]))
            stores.append(functools.partial(pltpu.make_async_copy, own[i], mine, osem.at[i]))
            src = ins[i].at[pl.ds(pl.multiple_of(c * rh, 16), rh), :]
            for j, (fx, fy) in enumerate(OTHER_CHIPS):
                px, py = _flip(x, fx), _flip(y, fy)
                sends.append(functools.partial(_remote, src, land[i].at[j], ssem.at[i, j], rsem.at[i, j], (px, py, c)))
                here = self._window(outs, i, 2 * px + py, c)
                forwards.append(functools.partial(_remote, land[i].at[j], here, fssem.at[i, j], frsem.at[i, j], sibling))
                keeps.append(functools.partial(pltpu.make_async_copy, land[i].at[j], here, ksem.at[i, j]))
                there = self._window(outs, i, 2 * px + py, 1 - c)
                passed.append(functools.partial(_remote, there, there, fssem.at[i, j], frsem.at[i, j], sibling))
        return loads, stores, sends, forwards, keeps, passed

    def start(self, ins, outs, scr):
        loads, _, sends, _, _, _ = self._copies(ins, outs, scr)
        for make in sends + loads:
            make().start()

    def relay(self, ins, outs, scr):
        loads, stores, sends, forwards, keeps, _ = self._copies(ins, outs, scr)
        for load, store in zip(loads, stores):
            load().wait()
            store().start()
        for send, forward, keep in zip(sends, forwards, keeps):
            send().wait_recv()
            forward().start()
            keep().start()

    def finish(self, ins, outs, scr):
        _, stores, sends, forwards, keeps, passed = self._copies(ins, outs, scr)
        for make in passed:
            make().wait_recv()
        for make in sends + forwards:
            make().wait_send()
        for make in keeps + stores:
            make().wait()


class _ChipsHook:
    def __init__(self, parts, kinds):
        self.ins, self.kinds, n = list(parts), list(kinds), len(parts)
        self.out_shape = tuple(
            S((4, p.shape[0], p.shape[1] // 4), BF) if k == "col" else S(p.shape, BF) for p, k in zip(parts, kinds))
        dma = pltpu.SemaphoreType.DMA
        self.scratch = ([dma((n, 3)), dma((n, 3)), dma((n,)), dma((n,))]
                        + [pltpu.VMEM(o.shape[1:], BF) for o in self.out_shape])

    def _slab(self, ins, i, s):
        _, rows, cols = self.out_shape[i].shape
        if self.kinds[i] == "col":
            return ins[i].at[:, pl.ds(pl.multiple_of(s * cols, 128), cols)]
        return ins[i].at[s]

    def _copies(self, ins, outs, scr):
        ssem, rsem, lsem, osem = scr[:4]
        own = scr[4:]
        x, y, c = _place()
        loads, stores, sends = [], [], []
        for i in range(len(ins)):
            loads.append(functools.partial(pltpu.make_async_copy, self._slab(ins, i, 2 * x + y), own[i], lsem.at[i]))
            stores.append(functools.partial(pltpu.make_async_copy, own[i], outs[i].at[3], osem.at[i]))
            for j, (fx, fy) in enumerate(OTHER_CHIPS):
                px, py = _flip(x, fx), _flip(y, fy)
                sends.append(functools.partial(_remote, self._slab(ins, i, 2 * px + py), outs[i].at[j], ssem.at[i, j],
                                               rsem.at[i, j], (px, py, c)))
        return loads, stores, sends

    def start(self, ins, outs, scr):
        loads, _, sends = self._copies(ins, outs, scr)
        for make in sends + loads:
            make().start()

    lead = 2

    def relay(self, ins, outs, scr):
        loads, stores, _ = self._copies(ins, outs, scr)
        for load, store in zip(loads, stores):
            load().wait()
            store().start()

    def finish(self, ins, outs, scr):
        _, stores, sends = self._copies(ins, outs, scr)
        for make in sends + stores:
            make().wait()


SHARE_STEPS = 2


def _sum_share(slab_list, name, hook=None):
    n = len(slab_list)
    geom = [(sl.shape[1], sl.shape[1] // SHARE_STEPS, sl.shape[2]) for sl in slab_list]

    def body(*refs):
        ins, outs, scr = refs[:n], refs[n:2 * n], refs[2 * n:]
        i = pl.program_id(0)
        x, y, c = _place()
        sibling = (x, y, 1 - c)

        def copies(q, k):
            rh, tr, _ = geom[q]
            stage, lsem, ssem, rsem = scr[4 * q:4 * q + 4]
            dst = outs[q].at[pl.ds(pl.multiple_of(c * rh + k * tr, 8), tr), :]
            return (pltpu.make_async_copy(stage.at[k], dst, lsem.at[k]),
                    _remote(stage.at[k], dst, ssem.at[k], rsem, sibling))

        for q in range(n):
            acc = ins[q][3].astype(F32)
            for k in range(3):
                acc = acc + ins[q][k].astype(F32)
            scr[4 * q][i] = acc
            for cp in copies(q, i):
                cp.start()

        @pl.when(i == SHARE_STEPS - 1)
        def _():
            for q in range(n):
                rh = geom[q][0]
                for k in range(SHARE_STEPS):
                    local, remote = copies(q, k)
                    local.wait()
                    remote.wait_send()
                got = outs[q].at[pl.ds(pl.multiple_of((1 - c) * rh, 8), rh), :]
                _remote(got, got, scr[4 * q + 2].at[0], scr[4 * q + 3], sibling).wait_recv()

    dma = pltpu.SemaphoreType.DMA
    scratch = []
    for rh, tr, cols in geom:
        scratch += [pltpu.VMEM((SHARE_STEPS, tr, cols), F32), dma((SHARE_STEPS,)), dma((SHARE_STEPS,)), dma]
    return _call(
        body, name=name, grid=(SHARE_STEPS,), out_shape=tuple(S((2 * rh, cols), F32) for rh, _, cols in geom),
        in_specs=[pl.BlockSpec((4, tr, cols), lambda i: (0, i, 0)) for _, tr, cols in geom], out_specs=(ANY,) * n,
        scratch_shapes=scratch, sem=("arbitrary",), args=list(slab_list), hook=hook)


class _SmallSumHook:
    def __init__(self, buf):
        self.ins, self.out_shape = [buf], (S(buf.shape, F32),)
        dma = pltpu.SemaphoreType.DMA
        self.scratch = [pltpu.VMEM((8,) + buf.shape, F32), pltpu.VMEM(buf.shape, F32), dma((7,)), dma((7,)), dma]

    def _sends(self, scr):
        slots, _, ssem, rsem, _ = scr
        x, y, c = _place()
        me = 4 * x + 2 * y + c
        for r in range(1, 8):
            px, py, pc = _flip(x, (r >> 2) & 1), _flip(y, (r >> 1) & 1), _flip(c, r & 1)
            yield (functools.partial(_remote, slots.at[me], slots.at[me], ssem.at[r - 1], rsem.at[r - 1], (px, py, pc)),
                   functools.partial(_remote, slots.at[me], slots.at[4 * px + 2 * py + pc], ssem.at[r - 1],
                                     rsem.at[r - 1], (px, py, pc)))

    def start(self, ins, outs, scr):
        slots, _, _, _, lsem = scr
        x, y, c = _place()
        load = pltpu.make_async_copy(ins[0], slots.at[4 * x + 2 * y + c], lsem)
        load.start()
        load.wait()
        for send, _ in self._sends(scr):
            send().start()

    lead = 0

    def relay(self, ins, outs, scr):
        pass

    def finish(self, ins, outs, scr):
        slots, total, _, _, lsem = scr
        for _, arrival in self._sends(scr):
            arrival().wait_recv()
        for send, _ in self._sends(scr):
            send().wait_send()
        acc = slots[0]
        for k in range(1, 8):
            acc = acc + slots[k]
        total[...] = acc
        store = pltpu.make_async_copy(total, outs[0], lsem)
        store.start()
        store.wait()


BIG = ("ffn1_w_gu", "ffn1_w_down", "w_in", "w_out", "xattn_wq", "xattn_wkv", "xattn_wo", "ffn2_w_gu", "ffn2_w_down")
KIND = {"ffn1_w_gu": "col", "ffn1_w_down": "row", "w_in": "col", "w_out": "row", "xattn_wq": "row",
        "xattn_wkv": "col", "xattn_wo": "row", "ffn2_w_gu": "col", "ffn2_w_down": "row"}
WEIGHTS = ("rel_bias", "ffn1_norm", "ffn1_w_gu", "ffn1_w_down", "mix_norm", "w_in", "sinks", "conv_w", "w_out",
           "xattn_norm", "mem_norm", "xattn_wq", "xattn_wkv", "xattn_wo", "ffn2_norm", "ffn2_w_gu", "ffn2_w_down",
           "final_norm")
SMALL_ROWS = 16
GAIN_ROW = {"ffn1_norm": 0, "mix_norm": 1, "xattn_norm": 2, "mem_norm": 3, "ffn2_norm": 4, "final_norm": 5}
CONV_ROW, SINK_ROW, BIAS_ROW, LOSS_ROW = 6, 9, 10, 11
TAP_ROWS = 8


def _rows_block(rows, d):
    buf = jnp.zeros((SMALL_ROWS, d), F32)
    for r, v in rows.items():
        buf = lax.dynamic_update_slice(buf, v if v.ndim == 2 else v.reshape(1, -1), (r, 0))
    return buf


def _adamw_small(gsum, conv_g, bias_g, small):
    names = list(small)

    def grad(k, gsum_ref, conv_ref, bias_ref):
        if k in GAIN_ROW:
            return gsum_ref[GAIN_ROW[k]:GAIN_ROW[k] + 1, :]
        if k == "sinks":
            return gsum_ref[SINK_ROW:SINK_ROW + 1, 0:small[k][0].shape[1]]
        return conv_ref[...] if k == "conv_w" else bias_ref[...]

    def body(gsum_ref, conv_ref, bias_ref, *refs):
        ins, outs = refs[:3 * len(names)], refs[3 * len(names):]
        for q, k in enumerate(names):
            w_ref, m_ref, v_ref = ins[3 * q:3 * q + 3]
            g_ref, d_ref, mo_ref, vo_ref = outs[4 * q:4 * q + 4]
            gv = grad(k, gsum_ref, conv_ref, bias_ref)
            g_ref[...] = gv
            d_ref[...], mo_ref[...], vo_ref[...] = _adam_update(w_ref[...], gv, m_ref[...], v_ref[...])

    res = pl.pallas_call(
        body, name="adamw_small", out_shape=tuple(S(small[k][0].shape, F32) for k in names for _ in range(4)),
        compiler_params=_params())(gsum, conv_g, bias_g, *[a for k in names for a in small[k]])
    return {k: res[4 * q:4 * q + 4] for q, k in enumerate(names)}


def _local_step(x, mem, pos, target, w, gains, rel_bias, sinks, conv_w, shards):
    t, d = x.shape
    w = dict(w)
    grads, slabs = {}, {}
    pos_col = pos.reshape(t, 1)
    pos_row = pos.reshape(1, t)
    bias_t = _bias_build(rel_bias).reshape(SWA_KV_HEADS, GROUP_ROWS, 2 * BLOCK)

    def gather(names, lead=2):
        return _GatherHook([shards[k] for k in names], [KIND[k] for k in names], lead)

    def gathered(names, hook):
        for k, gw in zip(names, hook.results):
            w[k] = gw if KIND[k] == "col" else gw.reshape(-1, gw.shape[-1])

    def dw(problems, tn, name, hook=None):
        res = _dw_pair([(a, b) for _, a, b in problems], tn, name, KIND[problems[0][0]], hook)
        grads.update(zip((k for k, _, _ in problems), res))

    def pair_sums(names):
        flat = lambda k, v: v.reshape(-1, v.shape[-1]) if KIND[k] == "row" else v
        sums = _add_bf16([(flat(k, grads[k][0]), flat(k, grads[k][1])) for k in names], "pair_sum_" + names[0])
        return {k: (p if KIND[k] == "col" else p.reshape(4, -1, p.shape[-1])) for k, p in zip(names, sums)}

    def chips(names, parts):
        return _ChipsHook([parts[k] for k in names], [KIND[k] for k in names])

    def reduced(names, hook):
        slabs.update(zip(names, hook.results))

    names = ("w_in", "w_out")
    hook = gather(names, lead=1)
    x1, g1, u1 = _ffn_fwd(x, gains["ffn1_norm"], w["ffn1_w_gu"], w["ffn1_w_down"], "ffn1_fwd", hook)
    gathered(names, hook)
    names = ("xattn_wq", "xattn_wkv", "xattn_wo")
    hook = gather(names)
    qkv, e = _mix_proj(x1, gains["mix_norm"], w["w_in"], hook)
    gathered(names, hook)
    names = ("ffn2_w_gu",)
    hook = gather(names)
    attn = _swa_fwd(qkv, pos_col, pos_row, bias_t, sinks, hook)
    gathered(names, hook)
    names = ("ffn2_w_down",)
    hook = gather(names)
    x2, merged = _mix_out_fwd(e, attn, conv_w, w["w_out"], x1, hook)
    gathered(names, hook)
    mh, kv = _mem_kv(mem, gains["mem_norm"], w["xattn_wkv"])
    x3, qx, o = _xattn_fwd(x2, gains["xattn_norm"], w["xattn_wq"], kv, w["xattn_wo"])
    dx4, g2, u2, d_final, loss = _ffn_fwd(x3, gains["ffn2_norm"], w["ffn2_w_gu"], w["ffn2_w_down"], "ffn2_fwd",
                                          head=(gains["final_norm"], target))

    dx3, d_ffn2, dgu2, a2, h4, dyh4 = _ffn_bwd(dx4, x3, gains["ffn2_norm"], g2, u2, w["ffn2_w_gu"], w["ffn2_w_down"],
                                               "ffn2_bwd")
    dw([("ffn2_w_gu", h4, dgu2)], 1408, "dw_ffn2_gu")
    dw([("ffn2_w_down", a2, dyh4)], 512, "dw_ffn2_down")
    parts = pair_sums(("ffn2_w_gu", "ffn2_w_down"))
    hook = chips(("ffn2_w_gu",), parts)
    dx2, d_xattn, dxh3, h3, dqx, grads["xattn_wkv"], d_mem = _xattn_bwd(
        dx3, x2, gains["xattn_norm"], qx, kv, w["xattn_wq"], w["xattn_wo"], mh, mem, w["xattn_wkv"], hook)
    reduced(("ffn2_w_gu",), hook)
    hook = chips(("ffn2_w_down",), parts)
    dattn, dp, dcw, dxh2 = _mix_out_bwd(dx2, e, attn, conv_w, w["w_out"], qkv.shape[1], hook)
    reduced(("ffn2_w_down",), hook)
    dw([("xattn_wo", o, dxh3), ("xattn_wq", h3, dqx), ("w_out", merged, dxh2)], 1024, "dw_wo_wq_wout")
    names = ("xattn_wo", "xattn_wq", "xattn_wkv", "w_out")
    hook = chips(names, pair_sums(names))
    dp, gb, dsk = _swa_bwd(qkv, dattn, pos_col, pos_row, bias_t, sinks, dp, hook)
    reduced(names, hook)
    d_rel_bias, d_sinks = _bias_reduce(gb.reshape(SWA_HEADS, BLOCK, 2 * BLOCK), dsk.reshape(SWA_HEADS, BLOCK, 1))
    dx1, d_mix, h2 = _mix_in_bwd(dp, x1, gains["mix_norm"], w["w_in"], dx2)
    dw([("w_in", h2, dp)], w["w_in"].shape[1] // 4, "dw_win")
    names = ("w_in",)
    hook = chips(names, pair_sums(names))
    dx0, d_ffn1, dgu1, a1, h1, dyh1 = _ffn_bwd(dx1, x, gains["ffn1_norm"], g1, u1, w["ffn1_w_gu"], w["ffn1_w_down"],
                                               "ffn1_bwd")
    dw([("ffn1_w_gu", h1, dgu1)], 1408, "dw_ffn1_gu", hook)
    reduced(names, hook)
    names = ("ffn1_w_gu",)
    hook = chips(names, pair_sums(names))
    dw([("ffn1_w_down", a1, dyh1)], 512, "dw_ffn1_down", hook)
    reduced(names, hook)
    rows = {0: d_ffn1, 1: d_mix, 2: d_xattn, 3: d_mem, 4: d_ffn2, 5: d_final, SINK_ROW: d_sinks,
            BIAS_ROW: d_rel_bias.reshape(1, -1), LOSS_ROW: loss[0, 0:1]}
    rows.update({CONV_ROW + j: dcw[j] for j in range(3)})
    last = ("ffn1_w_down",)
    return dx0, slabs, _rows_block(rows, d), chips(last, pair_sums(last))


def kernel(x, mem, positions, rel_bias, ffn1_norm, ffn1_w_gu, ffn1_w_down, mix_norm, w_in, sinks, conv_w, w_out, xattn_norm, mem_norm, xattn_wq, xattn_wkv, xattn_wo, ffn2_norm, ffn2_w_gu, ffn2_w_down, final_norm, loss_target, m_rel_bias, m_ffn1_norm, m_ffn1_w_gu, m_ffn1_w_down, m_mix_norm, m_w_in, m_sinks, m_conv_w, m_w_out, m_xattn_norm, m_mem_norm, m_xattn_wq, m_xattn_wkv, m_xattn_wo, m_ffn2_norm, m_ffn2_w_gu, m_ffn2_w_down, m_final_norm, v_rel_bias, v_ffn1_norm, v_ffn1_w_gu, v_ffn1_w_down, v_mix_norm, v_w_in, v_sinks, v_conv_w, v_w_out, v_xattn_norm, v_mem_norm, v_xattn_wq, v_xattn_wkv, v_xattn_wo, v_ffn2_norm, v_ffn2_w_gu, v_ffn2_w_down, v_final_norm):
    args = dict(locals())
    wts = {k: args[k] for k in WEIGHTS}
    mom = {k: args["m_" + k] for k in WEIGHTS}
    var = {k: args["v_" + k] for k in WEIGHTS}
    d = x.shape[-1]
    s_me = 2 * lax.axis_index("x") + lax.axis_index("y")

    first = ("ffn1_w_gu", "ffn1_w_down")
    rest = tuple(k for k in BIG if k not in first)
    shards = {k: wts[k][0].astype(BF) for k in first}
    cw_cols = conv_w.shape[-1]
    placed = lax.dynamic_update_slice(jnp.zeros((TAP_ROWS, d), F32), 0.5 * conv_w[0], (0, s_me * cw_cols))
    head = _Both([_GatherHook([shards[k] for k in first], [KIND[k] for k in first]), _SmallSumHook(placed)])
    shards.update(zip(rest, _to_bf16([wts[k][0] for k in rest], "gather_ffn1", head)))
    gathered, (conv_sum,) = head.results
    whole = {k: (gw if KIND[k] == "col" else gw.reshape(-1, gw.shape[-1])) for k, gw in zip(first, gathered)}
    conv_whole = conv_sum[0:3]

    gains = {k: wts[k].reshape(1, d) for k in GAIN_ROW}
    dx0, slabs, small, last_chips = _local_step(x[0], mem[0], positions[0], loss_target[0], whole, gains, rel_bias, sinks,
                                               conv_whole, shards)

    late = ("ffn1_w_gu", "ffn1_w_down", "ffn2_w_down")
    early = tuple(k for k in BIG if k not in late)
    first_ones = tuple(k for k in BIG if k != "ffn1_w_down")
    tail = _Both([last_chips, _SmallSumHook(small)])
    shard_grads = dict(zip(first_ones, _sum_share([slabs[k] for k in first_ones], "sum_share", tail)))
    (last_slabs,), (small_sum,) = tail.results
    shard_grads["ffn1_w_down"], = _sum_share([last_slabs], "sum_share_ffn1_w_down")
    quad = lambda k: (wts[k][0], shard_grads[k], mom[k][0], var[k][0])
    updates = dict(zip(early, _adamw([quad(k) for k in early], "adamw_early", 16, echo=True)))
    updates.update(zip(late, _adamw([quad(k) for k in late], "adamw_late", 8, echo=True)))
    loss = small_sum[LOSS_ROW, 0]

    out_g, out_d, out_m, out_v = {}, {}, {}, {}
    for k in BIG:
        out_g[k], out_d[k], out_m[k], out_v[k] = (a[None] for a in updates[k])

    to_call = lambda k, a: a.T if k == "rel_bias" else a.reshape(1, -1)
    from_call = lambda k, a: a.T if k == "rel_bias" else a.reshape(wts[k].shape)
    small_names = [k for k in WEIGHTS if k not in KIND]
    conv_g = lax.dynamic_slice(small_sum, (CONV_ROW, s_me * cw_cols), (3, cw_cols)).reshape(1, -1)
    bias_g = small_sum[BIAS_ROW, 0:rel_bias.size].reshape(rel_bias.shape).T
    done = _adamw_small(small_sum, conv_g, bias_g,
                        {k: tuple(to_call(k, a) for a in (wts[k], mom[k], var[k])) for k in small_names})
    for k in small_names:
        out_g[k], out_d[k], out_m[k], out_v[k] = (from_call(k, a) for a in done[k])

    return (loss, dx0[None], *[out_g[k] for k in WEIGHTS], *[out_d[k] for k in WEIGHTS],
            *[out_m[k] for k in WEIGHTS], *[out_v[k] for k in WEIGHTS])
```

```python
import functools
import math

import jax
import jax.numpy as jnp
from jax import lax
from jax.experimental import pallas as pl
from jax.experimental.pallas import tpu as pltpu

BF = jnp.bfloat16
F32 = jnp.float32
I32 = jnp.int32
S = jax.ShapeDtypeStruct

EPS = 1e-6
NEG = -1e30
POS_PAD = 1 << 30
WINDOW = 128
BLOCK = 128
HEAD_DIM = 64
SWA_HEADS = 16
SWA_KV_HEADS = 4
SWA_GROUP = SWA_HEADS // SWA_KV_HEADS
MEM_HEADS = 4
REL_BUCKETS = 32
REL_MAX_DIST = 128
ADAM_LR = 0.001
ADAM_B1 = 0.9
ADAM_B2 = 0.999
ADAM_EPS = 1e-08
ADAM_WD = 0.01
ADAM_STEP = 10

V7X_VMEM_LIMIT_BYTES = 56 * 1024 * 1024
MESH = pl.DeviceIdType.MESH
ANY = pl.BlockSpec(memory_space=pl.ANY)
VMEM_SPEC = pl.BlockSpec(memory_space=pltpu.VMEM)
SMEM_SPEC = pl.BlockSpec(memory_space=pltpu.SMEM)


def _params(sem=None):
    return pltpu.CompilerParams(dimension_semantics=sem, vmem_limit_bytes=V7X_VMEM_LIMIT_BYTES)


def _resident(shape):
    nd = len(shape)
    return pl.BlockSpec(shape, lambda *_: (0,) * nd, pipeline_mode=pl.Buffered(1))


def _acc_spec(shape):
    nd = len(shape)
    return pl.BlockSpec(shape, lambda *_: (0,) * nd)


def _call(body, *, name, grid, out_shape, in_specs, out_specs, args, sem, scratch_shapes=(), aliases=None, hook=None):
    aliases = aliases or {}
    if hook is None:
        return pl.pallas_call(body, name=name, grid=grid, out_shape=out_shape, in_specs=in_specs, out_specs=out_specs,
                              scratch_shapes=list(scratch_shapes), input_output_aliases=aliases,
                              compiler_params=_params(sem))(*args)
    n_in, n_out, n_scr = len(in_specs), len(out_shape), len(scratch_shapes)
    h_in, h_out = len(hook.ins), len(hook.out_shape)

    def at_step(pick):
        conds = [pl.program_id(ax) == pick(size) for ax, size in enumerate(grid)]
        return functools.reduce(jnp.logical_and, conds)

    def hosted(*refs):
        k_in, x_in = refs[:n_in], refs[n_in:n_in + h_in]
        o0 = n_in + h_in
        k_out, x_out = refs[o0:o0 + n_out], refs[o0 + n_out:o0 + n_out + h_out]
        s0 = o0 + n_out + h_out
        k_scr, x_scr = refs[s0:s0 + n_scr], refs[s0 + n_scr:]

        @pl.when(at_step(lambda size: 0))
        def _():
            hook.start(x_in, x_out, x_scr)
        body(*k_in, *k_out, *k_scr)
        early = bool(hook.lead) and grid[0] > hook.lead
        if early:
            on_axis0 = pl.program_id(0) == grid[0] - 1 - hook.lead
            rest_first = [pl.program_id(ax) == 0 for ax in range(1, len(grid))]

            @pl.when(functools.reduce(jnp.logical_and, rest_first, on_axis0))
            def _():
                hook.relay(x_in, x_out, x_scr)

        @pl.when(at_step(lambda size: size - 1))
        def _():
            if not early:
                hook.relay(x_in, x_out, x_scr)
            hook.finish(x_in, x_out, x_scr)

    res = pl.pallas_call(
        hosted, name=name, grid=grid, out_shape=tuple(out_shape) + tuple(hook.out_shape),
        in_specs=list(in_specs) + [ANY] * h_in, out_specs=tuple(out_specs) + (ANY,) * h_out,
        scratch_shapes=list(scratch_shapes) + list(hook.scratch), input_output_aliases=aliases,
        compiler_params=_params(("arbitrary",) * len(grid)),
    )(*args, *hook.ins)
    hook.results = res[n_out:]
    return res[:n_out]


def _nn(a, b):
    return jnp.dot(a, b, preferred_element_type=F32)


def _nt(a, b):
    return lax.dot_general(a, b, (((1,), (1,)), ((), ())), preferred_element_type=F32)


def _tn(a, b):
    return lax.dot_general(a, b, (((0,), (0,)), ((), ())), preferred_element_type=F32)


def _sigmoid(v):
    return 1.0 / (1.0 + jnp.exp(-v))


def _rms(x):
    r = lax.rsqrt(jnp.mean(x * x, axis=-1, keepdims=True) + EPS)
    return x * r, r


def _rms_bwd(dh, n, r, g):
    dn = dh * g
    dx = r * (dn - n * jnp.mean(dn * n, axis=-1, keepdims=True))
    return dx, jnp.sum(dh * n, axis=0, keepdims=True)


def _ffn_fwd(x, gn, wgu, wd, name, hook=None, head=None):
    t, d = x.shape
    f = wd.shape[0]
    tm, fc = 256, 1408

    def body(x_ref, gn_ref, wgu_ref, wd_ref, *rest):
        xv = x_ref[...]
        n, _ = _rms(xv)
        h = (n * gn_ref[...]).astype(BF)
        g_ref, u_ref = rest[-2:] if head is None else rest[3:5]
        acc = jnp.zeros((tm, d), F32)
        for c0 in range(0, f, fc):
            g = _nn(h, wgu_ref[:, c0:c0 + fc])
            u = _nn(h, wgu_ref[:, f + c0:f + c0 + fc])
            g_ref[:, c0:c0 + fc] = g.astype(BF)
            u_ref[:, c0:c0 + fc] = u.astype(BF)
            a = (g * _sigmoid(g)) * u
            acc = acc + _nn(a.astype(BF), wd_ref[c0:c0 + fc, :])
        y = xv + 0.5 * acc
        if head is None:
            rest[0][...] = y
            return
        gf_ref, t_ref, dy_ref, _, _, dgf_ref, loss_ref = rest

        @pl.when(pl.program_id(0) == 0)
        def _():
            dgf_ref[...] = jnp.zeros_like(dgf_ref)
            loss_ref[...] = jnp.zeros_like(loss_ref)
        ny, ry = _rms(y)
        gf = gf_ref[...]
        err = ny * gf - t_ref[...]
        loss_ref[...] += 0.5 * jnp.sum(jnp.sum(err * err, axis=-1, keepdims=True) / d, axis=0, keepdims=True)
        dy, dgf = _rms_bwd(err / d, ny, ry, gf)
        dy_ref[...] = dy
        dgf_ref[...] += dgf

    row = pl.BlockSpec((tm, d), lambda i: (i, 0))
    frow = pl.BlockSpec((tm, f), lambda i: (i, 0))
    in_specs = [row, _resident((1, d)), _resident(wgu.shape), _resident(wd.shape)]
    out_shape = (S((t, d), F32), S((t, f), BF), S((t, f), BF))
    out_specs = (row, frow, frow)
    args = (x, gn, wgu, wd)
    if head is not None:
        in_specs += [_resident((1, d)), row]
        out_shape += (S((1, d), F32), S((1, 128), F32))
        out_specs += (_acc_spec((1, d)), _acc_spec((1, 128)))
        args += tuple(head)
    return _call(body, name=name, grid=(t // tm,), out_shape=out_shape, in_specs=in_specs, out_specs=out_specs,
                 sem=("parallel",) if head is None else ("arbitrary",), args=args, hook=hook)


def _mix_proj(x, gn, w_in, hook=None):
    t, d = x.shape
    tm = 256
    nqkv = 1536
    ne = w_in.shape[1] - nqkv

    def body(x_ref, gn_ref, w_ref, qkv_ref, e_ref):
        n, _ = _rms(x_ref[...])
        h = (n * gn_ref[...]).astype(BF)
        qkv_ref[...] = _nn(h, w_ref[:, 0:nqkv]).astype(BF)
        for c0 in range(0, ne, 1024):
            e_ref[:, c0:c0 + 1024] = _nn(h, w_ref[:, nqkv + c0:nqkv + c0 + 1024]).astype(BF)

    return _call(
        body, name="mix_proj", grid=(t // tm,),
        out_shape=(S((t, nqkv), BF), S((t, ne), BF)),
        in_specs=[pl.BlockSpec((tm, d), lambda i: (i, 0)), _resident((1, d)), _resident(w_in.shape)],
        out_specs=(pl.BlockSpec((tm, nqkv), lambda i: (i, 0)), pl.BlockSpec((tm, ne), lambda i: (i, 0))),
        sem=("parallel",), args=(x, gn, w_in), hook=hook)


def _t5_bucket(rel):
    n = jnp.maximum(rel, 0)
    max_exact = REL_BUCKETS // 2
    nf = jnp.maximum(n, 1).astype(F32)
    large = max_exact + (jnp.log(nf / max_exact) / math.log(REL_MAX_DIST / max_exact)
                         * (REL_BUCKETS - max_exact)).astype(I32)
    large = jnp.minimum(large, REL_BUCKETS - 1)
    return jnp.where(n < max_exact, n, large)


def _block_rel():
    i = lax.broadcasted_iota(I32, (BLOCK, 2 * BLOCK), 0)
    j = lax.broadcasted_iota(I32, (BLOCK, 2 * BLOCK), 1)
    return i + BLOCK - j


def _bias_build(rel_bias_t):
    def body(rb_ref, o_ref):
        bucket = _t5_bucket(_block_rel())
        for h in range(SWA_HEADS):
            acc = jnp.zeros((BLOCK, 2 * BLOCK), F32)
            for b in range(REL_BUCKETS):
                acc = jnp.where(bucket == b, rb_ref[h, b], acc)
            o_ref[h] = acc

    return pl.pallas_call(
        body, name="bias_build", out_shape=S((SWA_HEADS, BLOCK, 2 * BLOCK), F32),
        in_specs=[SMEM_SPEC], out_specs=VMEM_SPEC,
    )(rel_bias_t)


GROUP_ROWS = SWA_GROUP * BLOCK


def _swa_visible(b, pq_ref, pkp_ref, pkc_ref):
    pk = jnp.concatenate([pkp_ref[...], pkc_ref[...]], axis=1)
    col = lax.broadcasted_iota(I32, (1, 2 * BLOCK), 1)
    pk = jnp.where(jnp.logical_and(b == 0, col < BLOCK), POS_PAD, pk)
    rel = jnp.concatenate([pq_ref[...]] * SWA_GROUP, axis=0) - pk
    return jnp.logical_and(rel >= 0, rel < WINDOW)


def _swa_poison(b, pq_ref, pkp_ref, pkc_ref):
    pk = jnp.concatenate([pkp_ref[...], pkc_ref[...]], axis=1)
    col = lax.broadcasted_iota(I32, (1, 2 * BLOCK), 1)
    pk = jnp.where(jnp.logical_and(b == 0, col < BLOCK), POS_PAD, pk)
    rel = pq_ref[...] - pk
    off = jnp.logical_and(jnp.logical_and(rel >= 0, rel < WINDOW), rel != _block_rel())
    return jnp.where(jnp.max(off.astype(F32)) > 0.0, jnp.nan, 0.0)


def _group_heads(ref, hk):
    h0 = hk * SWA_GROUP
    return jnp.concatenate([ref[:, (h0 + g) * HEAD_DIM:(h0 + g + 1) * HEAD_DIM] for g in range(SWA_GROUP)], axis=0)


def _group_sinks(sink_ref, hk):
    row = lax.broadcasted_iota(I32, (GROUP_ROWS, 1), 0)
    col = jnp.zeros((GROUP_ROWS, 1), F32) + sink_ref[0, hk * SWA_GROUP]
    for g in range(1, SWA_GROUP):
        col = jnp.where(row >= g * BLOCK, sink_ref[0, hk * SWA_GROUP + g], col)
    return col


def _swa_probs(qg, kh, vis, bias, sink):
    s = _nt(qg, kh) * (HEAD_DIM ** -0.5)
    s = jnp.where(vis, s + bias, NEG)
    m = jnp.maximum(jnp.max(s, axis=-1, keepdims=True), sink)
    p = jnp.exp(s - m)
    ps = jnp.exp(sink - m)
    inv = 1.0 / (jnp.sum(p, axis=-1, keepdims=True) + ps)
    return p * inv, ps * inv


def _swa_fwd(qkv, pos_col, pos_row, bias_t, sinks, hook=None):
    t = qkv.shape[0]
    nb = t // BLOCK
    qw = SWA_HEADS * HEAD_DIM
    kw = SWA_KV_HEADS * HEAD_DIM

    def body(q_ref, kp_ref, kc_ref, vp_ref, vc_ref, pq_ref, pkp_ref, pkc_ref, bias_ref, sink_ref, o_ref):
        b = pl.program_id(0)
        vis = _swa_visible(b, pq_ref, pkp_ref, pkc_ref)
        poison = _swa_poison(b, pq_ref, pkp_ref, pkc_ref)
        k2 = jnp.concatenate([kp_ref[...], kc_ref[...]], axis=0)
        v2 = jnp.concatenate([vp_ref[...], vc_ref[...]], axis=0)
        for hk in range(SWA_KV_HEADS):
            kh = k2[:, hk * HEAD_DIM:(hk + 1) * HEAD_DIM]
            vh = v2[:, hk * HEAD_DIM:(hk + 1) * HEAD_DIM]
            pn, _ = _swa_probs(_group_heads(q_ref, hk), kh, vis, bias_ref[hk], _group_sinks(sink_ref, hk))
            o = _nn(pn.astype(BF), vh) + poison
            for g in range(SWA_GROUP):
                h = hk * SWA_GROUP + g
                o_ref[:, h * HEAD_DIM:(h + 1) * HEAD_DIM] = o[g * BLOCK:(g + 1) * BLOCK]

    prev = lambda b: jnp.maximum(b - 1, 0)
    return _call(
        body, name="swa_fwd", grid=(nb,), out_shape=(S((t, qw), F32),),
        in_specs=[
            pl.BlockSpec((BLOCK, qw), lambda b: (b, 0)),
            pl.BlockSpec((BLOCK, kw), lambda b: (prev(b), qw // kw)),
            pl.BlockSpec((BLOCK, kw), lambda b: (b, qw // kw)),
            pl.BlockSpec((BLOCK, kw), lambda b: (prev(b), qw // kw + 1)),
            pl.BlockSpec((BLOCK, kw), lambda b: (b, qw // kw + 1)),
            pl.BlockSpec((BLOCK, 1), lambda b: (b, 0)),
            pl.BlockSpec((1, BLOCK), lambda b: (0, prev(b))),
            pl.BlockSpec((1, BLOCK), lambda b: (0, b)),
            _resident(bias_t.shape),
            SMEM_SPEC,
        ],
        out_specs=(pl.BlockSpec((BLOCK, qw), lambda b: (b, 0)),),
        sem=("parallel",), args=(qkv, qkv, qkv, qkv, qkv, pos_col, pos_row, pos_row, bias_t, sinks), hook=hook)[0]


HALO = 16


def _conv_taps(z, zh, first):
    tm = z.shape[0]
    zh = jnp.where(first, 0.0, zh)
    row = lax.broadcasted_iota(I32, (tm, 1), 0)
    z1 = jnp.where(row == 0, zh[HALO - 1:HALO, :], pltpu.roll(z, 1, 0))
    z2 = jnp.where(row == 0, zh[HALO - 2:HALO - 1, :], jnp.where(row == 1, zh[HALO - 1:HALO, :], pltpu.roll(z, 2, 0)))
    return z1, z2


def _mix_out_fwd(e, attn, conv_w, w_out, x, hook=None):
    t, d = x.shape
    tm = 256
    hb = tm // HALO
    f32 = lambda ref: ref[...].astype(F32)

    def body(c_ref, b_ref, u_ref, ga_ref, gc_ref, ch_ref, uh_ref, attn_ref, cw_ref, w_ref, x_ref, xo_ref, mg_ref):
        i = pl.program_id(0)
        z = f32(c_ref) * f32(u_ref)
        z1, z2 = _conv_taps(z, f32(ch_ref) * f32(uh_ref), i == 0)
        s = cw_ref[0:1, :] * z2 + cw_ref[1:2, :] * z1 + cw_ref[2:3, :] * z
        conv = f32(b_ref) * s
        merged = (_sigmoid(f32(ga_ref)) * attn_ref[...] + _sigmoid(f32(gc_ref)) * conv).astype(BF)
        mg_ref[...] = merged
        xo_ref[...] = x_ref[...] + _nn(merged, w_ref[...])

    ecol = lambda cb: pl.BlockSpec((tm, d), lambda i: (i, cb))
    halo = lambda cb: pl.BlockSpec((HALO, d), lambda i: (jnp.maximum(i * hb - 1, 0), cb))
    row = pl.BlockSpec((tm, d), lambda i: (i, 0))
    return _call(
        body, name="mix_out_fwd", grid=(t // tm,),
        out_shape=(S((t, d), F32), S((t, d), BF)),
        in_specs=[ecol(0), ecol(1), ecol(2), ecol(3), ecol(4), halo(0), halo(2), row,
                  _resident(conv_w.shape), _resident(w_out.shape), row],
        out_specs=(row, row),
        sem=("parallel",), args=(e, e, e, e, e, e, e, attn, conv_w, w_out, x), hook=hook)


def _mem_kv(mem, gm, wkv):
    m, d = mem.shape

    def body(mem_ref, gm_ref, w_ref, mh_ref, kv_ref):
        n, _ = _rms(mem_ref[...])
        mh = (n * gm_ref[...]).astype(BF)
        mh_ref[...] = mh
        kv_ref[...] = _nn(mh, w_ref[...]).astype(BF)

    return pl.pallas_call(
        body, name="mem_kv", out_shape=(S((m, d), BF), S((m, wkv.shape[1]), BF)),
        compiler_params=_params(),
    )(mem, gm, wkv)


def _xattn_probs(qh, kh):
    s = _nt(qh, kh) * (kh.shape[1] ** -0.5)
    p = jnp.exp(s - jnp.max(s, axis=-1, keepdims=True))
    return p * (1.0 / jnp.sum(p, axis=-1, keepdims=True))


def _xattn_fwd(x, gn, wq, kv, wo):
    t, d = x.shape
    tm = 256
    hd = d // MEM_HEADS

    def body(x_ref, gn_ref, wq_ref, kv_ref, wo_ref, xo_ref, q_ref, o_ref):
        xv = x_ref[...]
        n, _ = _rms(xv)
        q = _nn((n * gn_ref[...]).astype(BF), wq_ref[...]).astype(BF)
        q_ref[...] = q
        outs = []
        for hh in range(MEM_HEADS):
            p = _xattn_probs(q[:, hh * hd:(hh + 1) * hd], kv_ref[:, hh * hd:(hh + 1) * hd])
            outs.append(_nn(p.astype(BF), kv_ref[:, d + hh * hd:d + (hh + 1) * hd]))
        o = jnp.concatenate(outs, axis=1).astype(BF)
        o_ref[...] = o
        xo_ref[...] = xv + _nn(o, wo_ref[...])

    row = pl.BlockSpec((tm, d), lambda i: (i, 0))
    return pl.pallas_call(
        body, name="xattn_fwd", grid=(t // tm,),
        out_shape=(S((t, d), F32), S((t, d), BF), S((t, d), BF)),
        in_specs=[row, _resident((1, d)), _resident(wq.shape), _resident(kv.shape), _resident(wo.shape)],
        out_specs=(row, row, row),
        compiler_params=_params(("parallel",)),
    )(x, gn, wq, kv, wo)


def _ffn_bwd(dxo, x, gn, g, u, wgu, wd, name, hook=None):
    t, d = x.shape
    f = wd.shape[0]
    tm, fc = 256, 1408

    def body(dxo_ref, x_ref, gn_ref, g_ref, u_ref, wgu_ref, wd_ref, dx_ref, dgn_ref, dgu_ref, a_ref, h_ref, dyh_ref):
        @pl.when(pl.program_id(0) == 0)
        def _():
            dgn_ref[...] = jnp.zeros_like(dgn_ref)
        dxov = dxo_ref[...]
        dyh = (0.5 * dxov).astype(BF)
        dyh_ref[...] = dyh
        n, r = _rms(x_ref[...])
        gnv = gn_ref[...]
        h_ref[...] = (n * gnv).astype(BF)
        dh = jnp.zeros((tm, d), F32)
        for c0 in range(0, f, fc):
            gv = g_ref[:, c0:c0 + fc].astype(F32)
            uv = u_ref[:, c0:c0 + fc].astype(F32)
            da = _nt(dyh, wd_ref[c0:c0 + fc, :])
            sg = _sigmoid(gv)
            silu = gv * sg
            a_ref[:, c0:c0 + fc] = (silu * uv).astype(BF)
            dg = (da * uv * (sg * (1.0 + gv * (1.0 - sg)))).astype(BF)
            du = (da * silu).astype(BF)
            dgu_ref[:, c0:c0 + fc] = dg
            dgu_ref[:, f + c0:f + c0 + fc] = du
            dh = dh + _nt(dg, wgu_ref[:, c0:c0 + fc]) + _nt(du, wgu_ref[:, f + c0:f + c0 + fc])
        dx, dgn = _rms_bwd(dh, n, r, gnv)
        dx_ref[...] = dxov + dx
        dgn_ref[...] += dgn

    row = pl.BlockSpec((tm, d), lambda i: (i, 0))
    frow = pl.BlockSpec((tm, f), lambda i: (i, 0))
    return _call(
        body, name=name, grid=(t // tm,),
        out_shape=(S((t, d), F32), S((1, d), F32), S((t, 2 * f), BF), S((t, f), BF), S((t, d), BF), S((t, d), BF)),
        in_specs=[row, row, _resident((1, d)), frow, frow, _resident(wgu.shape), _resident(wd.shape)],
        out_specs=(row, _acc_spec((1, d)), pl.BlockSpec((tm, 2 * f), lambda i: (i, 0)), frow, row, row),
        sem=("arbitrary",), args=(dxo, x, gn, g, u, wgu, wd), hook=hook)


def _dw_pair(pairs, tn, name, kind, hook=None):
    npairs = len(pairs)
    t, ka = pairs[0][0].shape
    nb = pairs[0][1].shape[1]
    tt = 2048 if npairs == 1 else 1024
    nt, nj = t // tt, nb // tn
    col = kind == "col"
    rh = ka // 2 if col else ka // 8
    tile = (rh, tn) if col else (4, rh, tn)
    half = (rh, nb) if col else (4, rh, nb)
    lead = (slice(None),) * (len(tile) - 1)

    def body(*refs):
        ins, mines, sibs = refs[:2 * npairs], refs[2 * npairs:3 * npairs], refs[3 * npairs:4 * npairs]
        acc_ref, stage, ssem, rsem = refs[4 * npairs:]
        j, k = pl.program_id(0), pl.program_id(1)
        x, y, c = _place()
        sibling = (x, y, 1 - c)

        def send(p, slot, jj):
            dst = sibs[p].at[lead + (pl.ds(pl.multiple_of(jj * tn, 128), tn),)]
            return _remote(stage.at[slot], dst, ssem.at[slot], rsem.at[p], sibling)

        def rows(s, whose):
            return acc_ref[pl.ds(pl.multiple_of(s * 2 * rh + whose * rh, 16), rh), :].astype(BF)

        def step(p):
            a_ref, b_ref, mine_ref = ins[2 * p], ins[2 * p + 1], mines[p]

            @pl.when(k == 0)
            def _():
                acc_ref[...] = jnp.zeros_like(acc_ref)
            acc_ref[...] += _tn(a_ref[...], b_ref[...])

            @pl.when(k == nt - 1)
            def _():
                slot = j % 2

                @pl.when(j >= 2)
                def _():
                    send(p, slot, 0).wait_send()
                if col:
                    mine_ref[...] = rows(0, c)
                    stage[slot] = rows(0, 1 - c)
                else:
                    for s in range(4):
                        mine_ref[s] = rows(s, c)
                        stage[slot, s] = rows(s, 1 - c)
                send(p, slot, j - p * nj).start()

        for p in range(npairs):
            pl.when(j // nj == p)(functools.partial(step, p))

        @pl.when(jnp.logical_and(j == npairs * nj - 1, k == nt - 1))
        def _():
            for jj in range(max(npairs * nj - 2, 0), npairs * nj):
                send(0, jj % 2, 0).wait_send()
            for p in range(npairs):
                _remote(sibs[p], sibs[p], ssem.at[0], rsem.at[p], sibling).wait_recv()

    in_specs, args = [], []
    for p, (a, b) in enumerate(pairs):
        on = lambda j, p=p: j // nj == p
        in_specs += [pl.BlockSpec((tt, ka), lambda j, k, on=on: (jnp.where(on(j), k, 0), 0)),
                     pl.BlockSpec((tt, tn), lambda j, k, on=on, p=p: (jnp.where(on(j), k, 0), jnp.clip(j - p * nj, 0, nj - 1)))]
        args += [a, b]
    mine_spec = lambda p: pl.BlockSpec(tile, (lambda j, k: (0, jnp.clip(j - p * nj, 0, nj - 1))) if col
                                       else (lambda j, k: (0, 0, jnp.clip(j - p * nj, 0, nj - 1))))
    res = _call(
        body, name=name, grid=(npairs * nj, nt), out_shape=(S(half, BF),) * (2 * npairs),
        in_specs=in_specs, out_specs=tuple(mine_spec(p) for p in range(npairs)) + (ANY,) * npairs,
        scratch_shapes=[pltpu.VMEM((ka, tn), F32), pltpu.VMEM((2,) + tile, BF), pltpu.SemaphoreType.DMA((2,)),
                        pltpu.SemaphoreType.DMA((npairs,))],
        sem=("arbitrary", "arbitrary"), args=args, hook=hook)
    return [(res[p], res[npairs + p]) for p in range(npairs)]


def _xattn_bwd(dxo, x, gn, q, kv, wq, wo, mh, mem, wkv, hook=None):
    t, d = x.shape
    tm = 256
    hd = d // MEM_HEADS
    nkv = kv.shape[0]
    rows, cols = wkv.shape
    rh = rows // 2

    def body(dxo_ref, x_ref, gn_ref, q_ref, kv_ref, wq_ref, wo_ref, mh_ref, mem_ref, wkv_ref,
             dx_ref, dgn_ref, dxh_ref, h_ref, dq_ref, mine_ref, sib_ref, dgm_ref, dkv_ref, whole, ssem, rsem):
        @pl.when(pl.program_id(0) == 0)
        def _():
            dgn_ref[...] = jnp.zeros_like(dgn_ref)
            dkv_ref[...] = jnp.zeros_like(dkv_ref)
        dxov = dxo_ref[...]
        dxh = dxov.astype(BF)
        dxh_ref[...] = dxh
        do = _nt(dxh, wo_ref[...]).astype(BF)
        dqs = []
        for hh in range(MEM_HEADS):
            lo, hi = hh * hd, (hh + 1) * hd
            qh = q_ref[:, lo:hi]
            kh = kv_ref[:, lo:hi]
            vh = kv_ref[:, d + lo:d + hi]
            doh = do[:, lo:hi]
            p = _xattn_probs(qh, kh)
            dp = _nt(doh, vh)
            ds = (p * (dp - jnp.sum(p * dp, axis=-1, keepdims=True)) * (hd ** -0.5)).astype(BF)
            dqs.append(_nn(ds, kh))
            dkv_ref[:, lo:hi] += _tn(ds, qh)
            dkv_ref[:, d + lo:d + hi] += _tn(p.astype(BF), doh)
        dq = jnp.concatenate(dqs, axis=1).astype(BF)
        dq_ref[...] = dq
        n, r = _rms(x_ref[...])
        gnv = gn_ref[...]
        h_ref[...] = (n * gnv).astype(BF)
        dx, dgn = _rms_bwd(_nt(dq, wq_ref[...]), n, r, gnv)
        dx_ref[...] = dxov + dx
        dgn_ref[...] += dgn

        @pl.when(pl.program_id(0) == pl.num_programs(0) - 1)
        def _():
            dkvb = dkv_ref[...].astype(BF)
            dmh = _nt(dkvb, wkv_ref[...])
            nm, _ = _rms(mem_ref[...])
            px, py, c = _place()
            whole[...] = _tn(mh_ref[...], dkvb).astype(BF)
            cp = _remote(whole.at[pl.ds(pl.multiple_of((1 - c) * rh, 16), rh), :], sib_ref, ssem, rsem, (px, py, 1 - c))
            cp.start()
            mine_ref[...] = whole[pl.ds(pl.multiple_of(c * rh, 16), rh), :]
            dgm_ref[...] = jnp.sum(dmh * nm, axis=0, keepdims=True)
            cp.wait()

    row = pl.BlockSpec((tm, d), lambda i: (i, 0))
    dx, dgn, dxh, h, dq, mine, sib, dgm = _call(
        body, name="xattn_bwd", grid=(t // tm,),
        out_shape=(S((t, d), F32), S((1, d), F32), S((t, d), BF), S((t, d), BF), S((t, d), BF),
                   S((rh, cols), BF), S((rh, cols), BF), S((1, d), F32)),
        in_specs=[row, row, _resident((1, d)), row, _resident(kv.shape), _resident(wq.shape), _resident(wo.shape),
                  _resident(mh.shape), _resident(mem.shape), _resident(wkv.shape)],
        out_specs=(row, _acc_spec((1, d)), row, row, row, _acc_spec((rh, cols)), ANY, _acc_spec((1, d))),
        scratch_shapes=[pltpu.VMEM((nkv, 2 * d), F32), pltpu.VMEM((rows, cols), BF),
                        pltpu.SemaphoreType.DMA, pltpu.SemaphoreType.DMA],
        sem=("arbitrary",), args=(dxo, x, gn, q, kv, wq, wo, mh, mem, wkv), hook=hook)
    return dx, dgn, dxh, h, dq, (mine, sib), dgm


def _mix_out_bwd(dxo, e, attn, conv_w, w_out, nqkv, hook=None):
    t, d = attn.shape
    tm = 256
    nt = t // tm
    f32 = lambda ref: ref[...].astype(F32)

    def body(dxo_ref, dxn_ref, c_ref, b_ref, u_ref, ga_ref, gc_ref, ch_ref, uh_ref, bn_ref, gcn_ref,
             attn_ref, cw_ref, w_ref, dattn_ref, de_ref, dcw_ref, dxh_ref):
        i = pl.program_id(0)

        @pl.when(i == 0)
        def _():
            dcw_ref[...] = jnp.zeros_like(dcw_ref)
        dxh = dxo_ref[...].astype(BF)
        dxh_ref[...] = dxh
        w = w_ref[...]
        dm = _nt(dxh, w)
        dmn = _nt(dxn_ref[...].astype(BF), w)
        cv, bv, uv = f32(c_ref), f32(b_ref), f32(u_ref)
        sga = _sigmoid(f32(ga_ref))
        sgc = _sigmoid(f32(gc_ref))
        z = cv * uv
        z1, z2 = _conv_taps(z, f32(ch_ref) * f32(uh_ref), i == 0)
        w0, w1, w2 = cw_ref[0:1, :], cw_ref[1:2, :], cw_ref[2:3, :]
        s = w0 * z2 + w1 * z1 + w2 * z
        av = attn_ref[...]
        dattn_ref[...] = (dm * sga).astype(BF)
        dconv = dm * sgc
        ds = dconv * bv
        dsn = jnp.where(i == nt - 1, 0.0, dmn * _sigmoid(gcn_ref[0:8, :].astype(F32)) * bn_ref[0:8, :].astype(F32))
        row = lax.broadcasted_iota(I32, (tm, 1), 0)
        dsp1 = jnp.where(row == tm - 1, dsn[0:1, :], pltpu.roll(ds, tm - 1, 0))
        dsp2 = jnp.where(row == tm - 2, dsn[0:1, :], jnp.where(row == tm - 1, dsn[1:2, :], pltpu.roll(ds, tm - 2, 0)))
        dz = w2 * ds + w1 * dsp1 + w0 * dsp2
        de_ref[:, nqkv:nqkv + d] = (dz * uv).astype(BF)
        de_ref[:, nqkv + d:nqkv + 2 * d] = (dconv * s).astype(BF)
        de_ref[:, nqkv + 2 * d:nqkv + 3 * d] = (dz * cv).astype(BF)
        de_ref[:, nqkv + 3 * d:nqkv + 4 * d] = (dm * av * sga * (1.0 - sga)).astype(BF)
        de_ref[:, nqkv + 4 * d:nqkv + 5 * d] = (dm * (bv * s) * sgc * (1.0 - sgc)).astype(BF)
        dcw_ref[0:1, :] += jnp.sum(ds * z2, axis=0, keepdims=True)
        dcw_ref[1:2, :] += jnp.sum(ds * z1, axis=0, keepdims=True)
        dcw_ref[2:3, :] += jnp.sum(ds * z, axis=0, keepdims=True)

    ecol = lambda cb: pl.BlockSpec((tm, d), lambda i: (i, cb))
    prev = lambda cb: pl.BlockSpec((HALO, d), lambda i: (jnp.maximum(i * (tm // HALO) - 1, 0), cb))
    nxt = lambda rows, cb: pl.BlockSpec((rows, d), lambda i: (jnp.minimum((i + 1) * (tm // rows), t // rows - 1), cb))
    row = pl.BlockSpec((tm, d), lambda i: (i, 0))
    return _call(
        body, name="mix_out_bwd", grid=(nt,),
        out_shape=(S((t, d), BF), S((t, nqkv + 5 * d), BF), S((8, d), F32), S((t, d), BF)),
        in_specs=[row, nxt(8, 0), ecol(0), ecol(1), ecol(2), ecol(3), ecol(4), prev(0), prev(2), nxt(HALO, 1), nxt(HALO, 4),
                  row, _resident(conv_w.shape), _resident(w_out.shape)],
        out_specs=(row, pl.BlockSpec((tm, nqkv + 5 * d), lambda i: (i, 0)), _acc_spec((8, d)), row),
        sem=("arbitrary",), args=(dxo, dxo, e, e, e, e, e, e, e, e, e, attn, conv_w, w_out), hook=hook)


def _swa_bwd(qkv, dattn, pos_col, pos_row, bias_t, sinks, dp, hook=None):
    t = qkv.shape[0]
    nb = t // BLOCK
    qw = SWA_HEADS * HEAD_DIM
    kw = SWA_KV_HEADS * HEAD_DIM

    def body(q_ref, kp_ref, kc_ref, vp_ref, vc_ref, do_ref, pq_ref, pkp_ref, pkc_ref, bias_ref, sink_ref,
             dp_in, dp_ref, gb_ref, dsk_ref, carry_ref, dq_ref):
        b = pl.program_id(0)

        @pl.when(b == 0)
        def _():
            gb_ref[...] = jnp.zeros_like(gb_ref)
            dsk_ref[...] = jnp.zeros_like(dsk_ref)
            carry_ref[...] = jnp.zeros_like(carry_ref)
            dq_ref[...] = jnp.zeros_like(dq_ref)
        dp_ref[:, 0:qw] = dq_ref[...]

        @pl.when(b < nb)
        def _():
            vis = _swa_visible(b, pq_ref, pkp_ref, pkc_ref)
            k2 = jnp.concatenate([kp_ref[...], kc_ref[...]], axis=0)
            v2 = jnp.concatenate([vp_ref[...], vc_ref[...]], axis=0)
            for hk in range(SWA_KV_HEADS):
                lo, hi = hk * HEAD_DIM, (hk + 1) * HEAD_DIM
                kh = k2[:, lo:hi]
                vh = v2[:, lo:hi]
                qg = _group_heads(q_ref, hk)
                dog = _group_heads(do_ref, hk)
                pn, psn = _swa_probs(qg, kh, vis, bias_ref[hk], _group_sinks(sink_ref, hk))
                dp = _nt(dog, vh)
                delta = jnp.sum(pn * dp, axis=-1, keepdims=True)
                ds = pn * (dp - delta)
                gb_ref[hk] += ds
                dsk_ref[hk] += -psn * delta
                dsb = (ds * (HEAD_DIM ** -0.5)).astype(BF)
                dqg = _nn(dsb, kh).astype(BF)
                for g in range(SWA_GROUP):
                    h = hk * SWA_GROUP + g
                    dq_ref[:, h * HEAD_DIM:(h + 1) * HEAD_DIM] = dqg[g * BLOCK:(g + 1) * BLOCK]
                dk = _tn(dsb, qg)
                dv = _tn(pn.astype(BF), dog)
                dp_ref[:, qw + lo:qw + hi] = (carry_ref[:, lo:hi] + dk[0:BLOCK]).astype(BF)
                dp_ref[:, qw + kw + lo:qw + kw + hi] = (carry_ref[:, kw + lo:kw + hi] + dv[0:BLOCK]).astype(BF)
                carry_ref[:, lo:hi] = dk[BLOCK:2 * BLOCK]
                carry_ref[:, kw + lo:kw + hi] = dv[BLOCK:2 * BLOCK]

        @pl.when(b == nb)
        def _():
            dp_ref[:, qw:qw + 2 * kw] = carry_ref[...].astype(BF)

    cur = lambda b: jnp.minimum(b, nb - 1)
    prev = lambda b: jnp.maximum(cur(b) - 1, 0)
    return _call(
        body, name="swa_bwd", grid=(nb + 1,),
        out_shape=(S(dp.shape, BF), S((SWA_KV_HEADS, GROUP_ROWS, 2 * BLOCK), F32), S((SWA_KV_HEADS, GROUP_ROWS, 1), F32)),
        in_specs=[
            pl.BlockSpec((BLOCK, qw), lambda b: (cur(b), 0)),
            pl.BlockSpec((BLOCK, kw), lambda b: (prev(b), qw // kw)),
            pl.BlockSpec((BLOCK, kw), lambda b: (cur(b), qw // kw)),
            pl.BlockSpec((BLOCK, kw), lambda b: (prev(b), qw // kw + 1)),
            pl.BlockSpec((BLOCK, kw), lambda b: (cur(b), qw // kw + 1)),
            pl.BlockSpec((BLOCK, qw), lambda b: (cur(b), 0)),
            pl.BlockSpec((BLOCK, 1), lambda b: (cur(b), 0)),
            pl.BlockSpec((1, BLOCK), lambda b: (0, prev(b))),
            pl.BlockSpec((1, BLOCK), lambda b: (0, cur(b))),
            _resident(bias_t.shape),
            SMEM_SPEC,
            ANY,
        ],
        out_specs=(
            pl.BlockSpec((BLOCK, qw + 2 * kw), lambda b: (jnp.maximum(b - 1, 0), 0)),
            _acc_spec((SWA_KV_HEADS, GROUP_ROWS, 2 * BLOCK)),
            _acc_spec((SWA_KV_HEADS, GROUP_ROWS, 1)),
        ),
        scratch_shapes=[pltpu.VMEM((BLOCK, 2 * kw), F32), pltpu.VMEM((BLOCK, qw), BF)],
        aliases={11: 0}, sem=("arbitrary",),
        args=(qkv, qkv, qkv, qkv, qkv, dattn, pos_col, pos_row, pos_row, bias_t, sinks, dp), hook=hook)


def _bias_reduce(gb, dsk):
    def body(gb_ref, dsk_ref, drb_ref, dsink_ref):
        bucket = _t5_bucket(_block_rel())
        for b in range(REL_BUCKETS):
            mask = bucket == b
            for h in range(SWA_HEADS):
                drb_ref[h, b] = jnp.sum(jnp.where(mask, gb_ref[h], 0.0))
        for h in range(SWA_HEADS):
            dsink_ref[0, h] = jnp.sum(dsk_ref[h])

    return pl.pallas_call(
        body, name="bias_reduce", out_shape=(S((SWA_HEADS, REL_BUCKETS), F32), S((1, SWA_HEADS), F32)),
        in_specs=[VMEM_SPEC, VMEM_SPEC], out_specs=(SMEM_SPEC, SMEM_SPEC),
    )(gb, dsk)


def _mix_in_bwd(dp, x, gn, w_in, dxo):
    t, d = x.shape
    tm = 256
    npr = dp.shape[1]

    def body(dp_ref, x_ref, gn_ref, w_ref, dxo_ref, dx_ref, dgn_ref, h_ref):
        @pl.when(pl.program_id(0) == 0)
        def _():
            dgn_ref[...] = jnp.zeros_like(dgn_ref)
        dh = jnp.zeros((tm, d), F32)
        for c0 in range(0, npr, 1024):
            c1 = min(c0 + 1024, npr)
            dh = dh + _nt(dp_ref[:, c0:c1], w_ref[:, c0:c1])
        n, r = _rms(x_ref[...])
        gnv = gn_ref[...]
        h_ref[...] = (n * gnv).astype(BF)
        dx, dgn = _rms_bwd(dh, n, r, gnv)
        dx_ref[...] = dxo_ref[...] + dx
        dgn_ref[...] += dgn

    row = pl.BlockSpec((tm, d), lambda i: (i, 0))
    return pl.pallas_call(
        body, name="mix_in_bwd", grid=(t // tm,),
        out_shape=(S((t, d), F32), S((1, d), F32), S((t, d), BF)),
        in_specs=[pl.BlockSpec((tm, npr), lambda i: (i, 0)), row, _resident((1, d)), _resident(w_in.shape), row],
        out_specs=(row, _acc_spec((1, d)), row),
        compiler_params=_params(("arbitrary",)),
    )(dp, x, gn, w_in, dxo)


CAST_STEPS = 4


def _to_bf16(arrays, name, hook=None):
    def body(*refs):
        for src, dst in zip(refs[:len(arrays)], refs[len(arrays):]):
            dst[...] = src[...].astype(BF)

    blocks = [pl.BlockSpec((a.shape[0] // CAST_STEPS, a.shape[1]), lambda i: (i, 0)) for a in arrays]
    return _call(body, name=name, grid=(CAST_STEPS,), out_shape=tuple(S(a.shape, BF) for a in arrays), in_specs=blocks,
                 out_specs=tuple(blocks), sem=("parallel",), args=list(arrays), hook=hook)


PAIR_STEPS = 4


def _add_bf16(pairs, name):
    def body(*refs):
        ins, outs = refs[:2 * len(pairs)], refs[2 * len(pairs):]
        for q, o_ref in enumerate(outs):
            o_ref[...] = (ins[2 * q][...].astype(F32) + ins[2 * q + 1][...].astype(F32)).astype(BF)

    in_specs, out_specs, out_shape, args = [], [], [], []
    for a, b in pairs:
        rows, cols = a.shape
        blk = pl.BlockSpec((rows // PAIR_STEPS, cols), lambda i: (i, 0))
        in_specs += [blk, blk]
        out_specs.append(blk)
        out_shape.append(S((rows, cols), BF))
        args += [a, b]
    return pl.pallas_call(body, name=name, grid=(PAIR_STEPS,), out_shape=tuple(out_shape), in_specs=in_specs,
                          out_specs=tuple(out_specs), compiler_params=_params(("parallel",)))(*args)


def _adam_update(w, g, m, v):
    mn = ADAM_B1 * m + (1.0 - ADAM_B1) * g
    vn = ADAM_B2 * v + (1.0 - ADAM_B2) * (g * g)
    m_hat = mn / (1.0 - ADAM_B1 ** ADAM_STEP)
    v_hat = vn / (1.0 - ADAM_B2 ** ADAM_STEP)
    return -ADAM_LR * (m_hat / (jnp.sqrt(v_hat) + ADAM_EPS) + ADAM_WD * w), mn, vn


def _adamw(quads, name, steps, echo=False, hook=None):
    n_out = 4 if echo else 3

    def body(*refs):
        ins, outs = refs[:4 * len(quads)], refs[4 * len(quads):]
        for q in range(len(quads)):
            w_ref, g_ref, m_ref, v_ref = ins[4 * q:4 * q + 4]
            res = outs[n_out * q:n_out * q + n_out]
            gv = g_ref[...]
            if echo:
                res[0][...] = gv
            res[-3][...], res[-2][...], res[-1][...] = _adam_update(w_ref[...], gv, m_ref[...], v_ref[...])

    in_specs, out_specs, out_shape, args = [], [], [], []
    for quad in quads:
        rows, cols = quad[0].shape
        blk = pl.BlockSpec((rows // steps, cols), lambda i: (i, 0))
        in_specs += [blk] * 4
        out_specs += [blk] * n_out
        out_shape += [S((rows, cols), F32)] * n_out
        args += list(quad)
    res = _call(body, name=name, grid=(steps,), out_shape=tuple(out_shape), in_specs=in_specs, out_specs=tuple(out_specs),
                sem=("parallel",), args=args, hook=hook)
    return [res[n_out * q:n_out * q + n_out] for q in range(len(quads))]


def _place():
    x, y, c = lax.axis_index("x"), lax.axis_index("y"), lax.axis_index("c")
    return x, y, c


OTHER_CHIPS = ((1, 0), (0, 1), (1, 1))


def _flip(v, f):
    return 1 - v if f else v


def _remote(src, dst, ssem, rsem, dev):
    return pltpu.make_async_remote_copy(src_ref=src, dst_ref=dst, send_sem=ssem, recv_sem=rsem,
                                        device_id=dev, device_id_type=MESH)


class _Both:
    def __init__(self, hooks):
        self.hooks = hooks
        self.ins = [a for h in hooks for a in h.ins]
        self.out_shape = tuple(o for h in hooks for o in h.out_shape)
        self.scratch = [x for h in hooks for x in h.scratch]

    def _each(self, ins, outs, scr):
        i = o = s = 0
        for h in self.hooks:
            ni, no, ns = len(h.ins), len(h.out_shape), len(h.scratch)
            yield h, ins[i:i + ni], outs[o:o + no], scr[s:s + ns]
            i, o, s = i + ni, o + no, s + ns

    def start(self, ins, outs, scr):
        for h, *refs in self._each(ins, outs, scr):
            h.start(*refs)

    def relay(self, ins, outs, scr):
        for h, *refs in self._each(ins, outs, scr):
            h.relay(*refs)

    def finish(self, ins, outs, scr):
        for h, *refs in self._each(ins, outs, scr):
            h.finish(*refs)

    @property
    def lead(self):
        return max(h.lead for h in self.hooks)

    @property
    def results(self):
        return [h.results for h in self.hooks]

    @results.setter
    def results(self, res):
        for h, _, mine, _ in self._each((), tuple(res), ()):
            h.results = mine


class _GatherHook:
    def __init__(self, shards, kinds, lead=2):
        self.ins, self.kinds, self.lead, n = list(shards), list(kinds), lead, len(shards)
        self.out_shape = tuple(
            S((w.shape[0], 4 * w.shape[1]), BF) if k == "col" else S((4,) + w.shape, BF) for w, k in zip(shards, kinds))
        dma = pltpu.SemaphoreType.DMA
        self.scratch = ([dma((n, 3)) for _ in range(5)] + [dma((n,)), dma((n,))]
                        + [pltpu.VMEM((3, w.shape[0] // 2, w.shape[1]), BF) for w in shards]
                        + [pltpu.VMEM(w.shape, BF) for w in shards])

    def _window(self, outs, i, s, half):
        rows, cols = self.ins[i].shape
        rh = rows // 2
        start = pl.multiple_of(half * rh, 16)
        if self.kinds[i] == "col":
            return outs[i].at[pl.ds(start, rh), pl.ds(pl.multiple_of(s * cols, 128), cols)]
        return outs[i].at[s, pl.ds(start, rh), :]

    def _copies(self, ins, outs, scr):
        n = len(ins)
        ssem, rsem, fssem, frsem, ksem, lsem, osem = scr[:7]
        land, own = scr[7:7 + n], scr[7 + n:7 + 2 * n]
        x, y, c = _place()
        sibling = (x, y, 1 - c)
        loads, stores, sends, forwards, keeps, passed = [], [], [], [], [], []
        for i in range(n):
            rows, cols = self.ins[i].shape
            rh = rows // 2
            mine = (outs[i].at[:, pl.ds(pl.multiple_of((2 * x + y) * cols, 128), cols)] if self.kinds[i] == "col"
                    else outs[i].at[2 * x + y])
            loads.append(functools.partial(pltpu.make_async_copy, ins[i], own[i], lsem.at[i]))
            stores.append(functools.partial(pltpu.make_async_copy, own[i], mine, osem.at[i]))
            src = ins[i].at[pl.ds(pl.multiple_of(c * rh, 16), rh), :]
            for j, (fx, fy) in enumerate(OTHER_CHIPS):
                px, py = _flip(x, fx), _flip(y, fy)
                sends.append(functools.partial(_remote, src, land[i].at[j], ssem.at[i, j], rsem.at[i, j], (px, py, c)))
                here = self._window(outs, i, 2 * px + py, c)
                forwards.append(functools.partial(_remote, land[i].at[j], here, fssem.at[i, j], frsem.at[i, j], sibling))
                keeps.append(functools.partial(pltpu.make_async_copy, land[i].at[j], here, ksem.at[i, j]))
                there = self._window(outs, i, 2 * px + py, 1 - c)
                passed.append(functools.partial(_remote, there, there, fssem.at[i, j], frsem.at[i, j], sibling))
        return loads, stores, sends, forwards, keeps, passed

    def start(self, ins, outs, scr):
        loads, _, sends, _, _, _ = self._copies(ins, outs, scr)
        for make in sends + loads:
            make().start()

    def relay(self, ins, outs, scr):
        loads, stores, sends, forwards, keeps, _ = self._copies(ins, outs, scr)
        for load, store in zip(loads, stores):
            load().wait()
            store().start()
        for send, forward, keep in zip(sends, forwards, keeps):
            send().wait_recv()
            forward().start()
            keep().start()

    def finish(self, ins, outs, scr):
        _, stores, sends, forwards, keeps, passed = self._copies(ins, outs, scr)
        for make in passed:
            make().wait_recv()
        for make in sends + forwards:
            make().wait_send()
        for make in keeps + stores:
            make().wait()


class _ChipsHook:
    def __init__(self, parts, kinds):
        self.ins, self.kinds, n = list(parts), list(kinds), len(parts)
        self.out_shape = tuple(
            S((4, p.shape[0], p.shape[1] // 4), BF) if k == "col" else S(p.shape, BF) for p, k in zip(parts, kinds))
        dma = pltpu.SemaphoreType.DMA
        self.scratch = ([dma((n, 3)), dma((n, 3)), dma((n,)), dma((n,))]
                        + [pltpu.VMEM(o.shape[1:], BF) for o in self.out_shape])

    def _slab(self, ins, i, s):
        _, rows, cols = self.out_shape[i].shape
        if self.kinds[i] == "col":
            return ins[i].at[:, pl.ds(pl.multiple_of(s * cols, 128), cols)]
        return ins[i].at[s]

    def _copies(self, ins, outs, scr):
        ssem, rsem, lsem, osem = scr[:4]
        own = scr[4:]
        x, y, c = _place()
        loads, stores, sends = [], [], []
        for i in range(len(ins)):
            loads.append(functools.partial(pltpu.make_async_copy, self._slab(ins, i, 2 * x + y), own[i], lsem.at[i]))
            stores.append(functools.partial(pltpu.make_async_copy, own[i], outs[i].at[3], osem.at[i]))
            for j, (fx, fy) in enumerate(OTHER_CHIPS):
                px, py = _flip(x, fx), _flip(y, fy)
                sends.append(functools.partial(_remote, self._slab(ins, i, 2 * px + py), outs[i].at[j], ssem.at[i, j],
                                               rsem.at[i, j], (px, py, c)))
        return loads, stores, sends

    def start(self, ins, outs, scr):
        loads, _, sends = self._copies(ins, outs, scr)
        for make in sends + loads:
            make().start()

    lead = 2

    def relay(self, ins, outs, scr):
        loads, stores, _ = self._copies(ins, outs, scr)
        for load, store in zip(loads, stores):
            load().wait()
            store().start()

    def finish(self, ins, outs, scr):
        _, stores, sends = self._copies(ins, outs, scr)
        for make in sends + stores:
            make().wait()


SHARE_STEPS = 2


def _sum_share(slab_list, name, hook=None):
    n = len(slab_list)
    geom = [(sl.shape[1], sl.shape[1] // SHARE_STEPS, sl.shape[2]) for sl in slab_list]

    def body(*refs):
        ins, outs, scr = refs[:n], refs[n:2 * n], refs[2 * n:]
        i = pl.program_id(0)
        x, y, c = _place()
        sibling = (x, y, 1 - c)

        def copies(q, k):
            rh, tr, _ = geom[q]
            stage, lsem, ssem, rsem = scr[4 * q:4 * q + 4]
            dst = outs[q].at[pl.ds(pl.multiple_of(c * rh + k * tr, 8), tr), :]
            return (pltpu.make_async_copy(stage.at[k], dst, lsem.at[k]),
                    _remote(stage.at[k], dst, ssem.at[k], rsem, sibling))

        for q in range(n):
            acc = ins[q][3].astype(F32)
            for k in range(3):
                acc = acc + ins[q][k].astype(F32)
            scr[4 * q][i] = acc
            for cp in copies(q, i):
                cp.start()

        @pl.when(i == SHARE_STEPS - 1)
        def _():
            for q in range(n):
                rh = geom[q][0]
                for k in range(SHARE_STEPS):
                    local, remote = copies(q, k)
                    local.wait()
                    remote.wait_send()
                got = outs[q].at[pl.ds(pl.multiple_of((1 - c) * rh, 8), rh), :]
                _remote(got, got, scr[4 * q + 2].at[0], scr[4 * q + 3], sibling).wait_recv()

    dma = pltpu.SemaphoreType.DMA
    scratch = []
    for rh, tr, cols in geom:
        scratch += [pltpu.VMEM((SHARE_STEPS, tr, cols), F32), dma((SHARE_STEPS,)), dma((SHARE_STEPS,)), dma]
    return _call(
        body, name=name, grid=(SHARE_STEPS,), out_shape=tuple(S((2 * rh, cols), F32) for rh, _, cols in geom),
        in_specs=[pl.BlockSpec((4, tr, cols), lambda i: (0, i, 0)) for _, tr, cols in geom], out_specs=(ANY,) * n,
        scratch_shapes=scratch, sem=("arbitrary",), args=list(slab_list), hook=hook)


class _SmallSumHook:
    def __init__(self, buf):
        self.ins, self.out_shape = [buf], (S(buf.shape, F32),)
        dma = pltpu.SemaphoreType.DMA
        self.scratch = [pltpu.VMEM((8,) + buf.shape, F32), pltpu.VMEM(buf.shape, F32), dma((7,)), dma((7,)), dma]

    def _sends(self, scr):
        slots, _, ssem, rsem, _ = scr
        x, y, c = _place()
        me = 4 * x + 2 * y + c
        for r in range(1, 8):
            px, py, pc = _flip(x, (r >> 2) & 1), _flip(y, (r >> 1) & 1), _flip(c, r & 1)
            yield (functools.partial(_remote, slots.at[me], slots.at[me], ssem.at[r - 1], rsem.at[r - 1], (px, py, pc)),
                   functools.partial(_remote, slots.at[me], slots.at[4 * px + 2 * py + pc], ssem.at[r - 1],
                                     rsem.at[r - 1], (px, py, pc)))

    def start(self, ins, outs, scr):
        slots, _, _, _, lsem = scr
        x, y, c = _place()
        load = pltpu.make_async_copy(ins[0], slots.at[4 * x + 2 * y + c], lsem)
        load.start()
        load.wait()
        for send, _ in self._sends(scr):
            send().start()

    lead = 0

    def relay(self, ins, outs, scr):
        pass

    def finish(self, ins, outs, scr):
        slots, total, _, _, lsem = scr
        for _, arrival in self._sends(scr):
            arrival().wait_recv()
        for send, _ in self._sends(scr):
            send().wait_send()
        acc = slots[0]
        for k in range(1, 8):
            acc = acc + slots[k]
        total[...] = acc
        store = pltpu.make_async_copy(total, outs[0], lsem)
        store.start()
        store.wait()


BIG = ("ffn1_w_gu", "ffn1_w_down", "w_in", "w_out", "xattn_wq", "xattn_wkv", "xattn_wo", "ffn2_w_gu", "ffn2_w_down")
KIND = {"ffn1_w_gu": "col", "ffn1_w_down": "row", "w_in": "col", "w_out": "row", "xattn_wq": "row",
        "xattn_wkv": "col", "xattn_wo": "row", "ffn2_w_gu": "col", "ffn2_w_down": "row"}
WEIGHTS = ("rel_bias", "ffn1_norm", "ffn1_w_gu", "ffn1_w_down", "mix_norm", "w_in", "sinks", "conv_w", "w_out",
           "xattn_norm", "mem_norm", "xattn_wq", "xattn_wkv", "xattn_wo", "ffn2_norm", "ffn2_w_gu", "ffn2_w_down",
           "final_norm")
SMALL_ROWS = 16
GAIN_ROW = {"ffn1_norm": 0, "mix_norm": 1, "xattn_norm": 2, "mem_norm": 3, "ffn2_norm": 4, "final_norm": 5}
CONV_ROW, SINK_ROW, BIAS_ROW, LOSS_ROW = 6, 9, 10, 11
TAP_ROWS = 8


def _rows_block(rows, d):
    buf = jnp.zeros((SMALL_ROWS, d), F32)
    for r, v in rows.items():
        buf = lax.dynamic_update_slice(buf, v if v.ndim == 2 else v.reshape(1, -1), (r, 0))
    return buf


def _adamw_small(gsum, conv_g, bias_g, small):
    names = list(small)

    def grad(k, gsum_ref, conv_ref, bias_ref):
        if k in GAIN_ROW:
            return gsum_ref[GAIN_ROW[k]:GAIN_ROW[k] + 1, :]
        if k == "sinks":
            return gsum_ref[SINK_ROW:SINK_ROW + 1, 0:small[k][0].shape[1]]
        return conv_ref[...] if k == "conv_w" else bias_ref[...]

    def body(gsum_ref, conv_ref, bias_ref, *refs):
        ins, outs = refs[:3 * len(names)], refs[3 * len(names):]
        for q, k in enumerate(names):
            w_ref, m_ref, v_ref = ins[3 * q:3 * q + 3]
            g_ref, d_ref, mo_ref, vo_ref = outs[4 * q:4 * q + 4]
            gv = grad(k, gsum_ref, conv_ref, bias_ref)
            g_ref[...] = gv
            d_ref[...], mo_ref[...], vo_ref[...] = _adam_update(w_ref[...], gv, m_ref[...], v_ref[...])

    res = pl.pallas_call(
        body, name="adamw_small", out_shape=tuple(S(small[k][0].shape, F32) for k in names for _ in range(4)),
        compiler_params=_params())(gsum, conv_g, bias_g, *[a for k in names for a in small[k]])
    return {k: res[4 * q:4 * q + 4] for q, k in enumerate(names)}


def _local_step(x, mem, pos, target, w, gains, rel_bias, sinks, conv_w, shards):
    t, d = x.shape
    w = dict(w)
    grads, slabs = {}, {}
    pos_col = pos.reshape(t, 1)
    pos_row = pos.reshape(1, t)
    bias_t = _bias_build(rel_bias.T).reshape(SWA_KV_HEADS, GROUP_ROWS, 2 * BLOCK)

    def gather(names, lead=2):
        return _GatherHook([shards[k] for k in names], [KIND[k] for k in names], lead)

    def gathered(names, hook):
        for k, gw in zip(names, hook.results):
            w[k] = gw if KIND[k] == "col" else gw.reshape(-1, gw.shape[-1])

    def dw(problems, tn, name, hook=None):
        res = _dw_pair([(a, b) for _, a, b in problems], tn, name, KIND[problems[0][0]], hook)
        grads.update(zip((k for k, _, _ in problems), res))

    def pair_sums(names):
        flat = lambda k, v: v.reshape(-1, v.shape[-1]) if KIND[k] == "row" else v
        sums = _add_bf16([(flat(k, grads[k][0]), flat(k, grads[k][1])) for k in names], "pair_sum_" + names[0])
        return {k: (p if KIND[k] == "col" else p.reshape(4, -1, p.shape[-1])) for k, p in zip(names, sums)}

    def chips(names, parts):
        return _ChipsHook([parts[k] for k in names], [KIND[k] for k in names])

    def reduced(names, hook):
        slabs.update(zip(names, hook.results))

    names = ("w_in", "w_out")
    hook = gather(names, lead=1)
    x1, g1, u1 = _ffn_fwd(x, gains["ffn1_norm"], w["ffn1_w_gu"], w["ffn1_w_down"], "ffn1_fwd", hook)
    gathered(names, hook)
    names = ("xattn_wq", "xattn_wkv", "xattn_wo")
    hook = gather(names)
    qkv, e = _mix_proj(x1, gains["mix_norm"], w["w_in"], hook)
    gathered(names, hook)
    names = ("ffn2_w_gu",)
    hook = gather(names)
    attn = _swa_fwd(qkv, pos_col, pos_row, bias_t, sinks, hook)
    gathered(names, hook)
    names = ("ffn2_w_down",)
    hook = gather(names)
    x2, merged = _mix_out_fwd(e, attn, conv_w, w["w_out"], x1, hook)
    gathered(names, hook)
    mh, kv = _mem_kv(mem, gains["mem_norm"], w["xattn_wkv"])
    x3, qx, o = _xattn_fwd(x2, gains["xattn_norm"], w["xattn_wq"], kv, w["xattn_wo"])
    dx4, g2, u2, d_final, loss = _ffn_fwd(x3, gains["ffn2_norm"], w["ffn2_w_gu"], w["ffn2_w_down"], "ffn2_fwd",
                                          head=(gains["final_norm"], target))

    dx3, d_ffn2, dgu2, a2, h4, dyh4 = _ffn_bwd(dx4, x3, gains["ffn2_norm"], g2, u2, w["ffn2_w_gu"], w["ffn2_w_down"],
                                               "ffn2_bwd")
    dw([("ffn2_w_gu", h4, dgu2)], 1408, "dw_ffn2_gu")
    dw([("ffn2_w_down", a2, dyh4)], 512, "dw_ffn2_down")
    parts = pair_sums(("ffn2_w_gu", "ffn2_w_down"))
    hook = chips(("ffn2_w_gu",), parts)
    dx2, d_xattn, dxh3, h3, dqx, grads["xattn_wkv"], d_mem = _xattn_bwd(
        dx3, x2, gains["xattn_norm"], qx, kv, w["xattn_wq"], w["xattn_wo"], mh, mem, w["xattn_wkv"], hook)
    reduced(("ffn2_w_gu",), hook)
    hook = chips(("ffn2_w_down",), parts)
    dattn, dp, dcw, dxh2 = _mix_out_bwd(dx2, e, attn, conv_w, w["w_out"], qkv.shape[1], hook)
    reduced(("ffn2_w_down",), hook)
    dw([("xattn_wo", o, dxh3), ("xattn_wq", h3, dqx), ("w_out", merged, dxh2)], 1024, "dw_wo_wq_wout")
    names = ("xattn_wo", "xattn_wq", "xattn_wkv", "w_out")
    hook = chips(names, pair_sums(names))
    dp, gb, dsk = _swa_bwd(qkv, dattn, pos_col, pos_row, bias_t, sinks, dp, hook)
    reduced(names, hook)
    d_rel_bias, d_sinks = _bias_reduce(gb.reshape(SWA_HEADS, BLOCK, 2 * BLOCK), dsk.reshape(SWA_HEADS, BLOCK, 1))
    dx1, d_mix, h2 = _mix_in_bwd(dp, x1, gains["mix_norm"], w["w_in"], dx2)
    dw([("w_in", h2, dp)], w["w_in"].shape[1] // 4, "dw_win")
    names = ("w_in",)
    hook = chips(names, pair_sums(names))
    dx0, d_ffn1, dgu1, a1, h1, dyh1 = _ffn_bwd(dx1, x, gains["ffn1_norm"], g1, u1, w["ffn1_w_gu"], w["ffn1_w_down"],
                                               "ffn1_bwd")
    dw([("ffn1_w_gu", h1, dgu1)], 1408, "dw_ffn1_gu", hook)
    reduced(names, hook)
    names = ("ffn1_w_gu",)
    hook = chips(names, pair_sums(names))
    dw([("ffn1_w_down", a1, dyh1)], 512, "dw_ffn1_down", hook)
    reduced(names, hook)
    rows = {0: d_ffn1, 1: d_mix, 2: d_xattn, 3: d_mem, 4: d_ffn2, 5: d_final, SINK_ROW: d_sinks,
            BIAS_ROW: d_rel_bias.reshape(1, -1), LOSS_ROW: loss[0, 0:1]}
    rows.update({CONV_ROW + j: dcw[j] for j in range(3)})
    last = ("ffn1_w_down",)
    return dx0, slabs, _rows_block(rows, d), chips(last, pair_sums(last))


def kernel(x, mem, positions, rel_bias, ffn1_norm, ffn1_w_gu, ffn1_w_down, mix_norm, w_in, sinks, conv_w, w_out, xattn_norm, mem_norm, xattn_wq, xattn_wkv, xattn_wo, ffn2_norm, ffn2_w_gu, ffn2_w_down, final_norm, loss_target, m_rel_bias, m_ffn1_norm, m_ffn1_w_gu, m_ffn1_w_down, m_mix_norm, m_w_in, m_sinks, m_conv_w, m_w_out, m_xattn_norm, m_mem_norm, m_xattn_wq, m_xattn_wkv, m_xattn_wo, m_ffn2_norm, m_ffn2_w_gu, m_ffn2_w_down, m_final_norm, v_rel_bias, v_ffn1_norm, v_ffn1_w_gu, v_ffn1_w_down, v_mix_norm, v_w_in, v_sinks, v_conv_w, v_w_out, v_xattn_norm, v_mem_norm, v_xattn_wq, v_xattn_wkv, v_xattn_wo, v_ffn2_norm, v_ffn2_w_gu, v_ffn2_w_down, v_final_norm):
    args = dict(locals())
    wts = {k: args[k] for k in WEIGHTS}
    mom = {k: args["m_" + k] for k in WEIGHTS}
    var = {k: args["v_" + k] for k in WEIGHTS}
    d = x.shape[-1]
    s_me = 2 * lax.axis_index("x") + lax.axis_index("y")

    first = ("ffn1_w_gu", "ffn1_w_down")
    rest = tuple(k for k in BIG if k not in first)
    shards = {k: wts[k][0].astype(BF) for k in first}
    cw_cols = conv_w.shape[-1]
    placed = lax.dynamic_update_slice(jnp.zeros((TAP_ROWS, d), F32), 0.5 * conv_w[0], (0, s_me * cw_cols))
    head = _Both([_GatherHook([shards[k] for k in first], [KIND[k] for k in first]), _SmallSumHook(placed)])
    shards.update(zip(rest, _to_bf16([wts[k][0] for k in rest], "gather_ffn1", head)))
    gathered, (conv_sum,) = head.results
    whole = {k: (gw if KIND[k] == "col" else gw.reshape(-1, gw.shape[-1])) for k, gw in zip(first, gathered)}
    conv_whole = conv_sum[0:3]

    gains = {k: wts[k].reshape(1, d) for k in GAIN_ROW}
    dx0, slabs, small, last_chips = _local_step(x[0], mem[0], positions[0], loss_target[0], whole, gains, rel_bias, sinks,
                                               conv_whole, shards)

    late = ("ffn1_w_gu", "ffn1_w_down", "ffn2_w_down")
    early = tuple(k for k in BIG if k not in late)
    first_ones = tuple(k for k in BIG if k != "ffn1_w_down")
    tail = _Both([last_chips, _SmallSumHook(small)])
    shard_grads = dict(zip(first_ones, _sum_share([slabs[k] for k in first_ones], "sum_share", tail)))
    (last_slabs,), (small_sum,) = tail.results
    shard_grads["ffn1_w_down"], = _sum_share([last_slabs], "sum_share_ffn1_w_down")
    quad = lambda k: (wts[k][0], shard_grads[k], mom[k][0], var[k][0])
    updates = dict(zip(early, _adamw([quad(k) for k in early], "adamw_early", 16, echo=True)))
    updates.update(zip(late, _adamw([quad(k) for k in late], "adamw_late", 8, echo=True)))
    loss = small_sum[LOSS_ROW, 0]

    out_g, out_d, out_m, out_v = {}, {}, {}, {}
    for k in BIG:
        out_g[k], out_d[k], out_m[k], out_v[k] = (a[None] for a in updates[k])

    to_call = lambda k, a: a.T if k == "rel_bias" else a.reshape(1, -1)
    from_call = lambda k, a: a.T if k == "rel_bias" else a.reshape(wts[k].shape)
    small_names = [k for k in WEIGHTS if k not in KIND]
    conv_g = lax.dynamic_slice(small_sum, (CONV_ROW, s_me * cw_cols), (3, cw_cols)).reshape(1, -1)
    bias_g = small_sum[BIAS_ROW, 0:rel_bias.size].reshape(rel_bias.shape[::-1])
    done = _adamw_small(small_sum, conv_g, bias_g,
                        {k: tuple(to_call(k, a) for a in (wts[k], mom[k], var[k])) for k in small_names})
    for k in small_names:
        out_g[k], out_d[k], out_m[k], out_v[k] = (from_call(k, a) for a in done[k])

    return (loss, dx0[None], *[out_g[k] for k in WEIGHTS], *[out_d[k] for k in WEIGHTS],
            *[out_m[k] for k in WEIGHTS], *[out_v[k] for k in WEIGHTS])
```

```python
import functools
import math

import jax
import jax.numpy as jnp
from jax import lax
from jax.experimental import pallas as pl
from jax.experimental.pallas import tpu as pltpu

BF = jnp.bfloat16
F32 = jnp.float32
I32 = jnp.int32
S = jax.ShapeDtypeStruct

EPS = 1e-6
NEG = -1e30
POS_PAD = 1 << 30
WINDOW = 128
BLOCK = 128
HEAD_DIM = 64
SWA_HEADS = 16
SWA_KV_HEADS = 4
SWA_GROUP = SWA_HEADS // SWA_KV_HEADS
MEM_HEADS = 4
REL_BUCKETS = 32
REL_MAX_DIST = 128
ADAM_LR = 0.001
ADAM_B1 = 0.9
ADAM_B2 = 0.999
ADAM_EPS = 1e-08
ADAM_WD = 0.01
ADAM_STEP = 10

V7X_VMEM_LIMIT_BYTES = 56 * 1024 * 1024
MESH = pl.DeviceIdType.MESH
ANY = pl.BlockSpec(memory_space=pl.ANY)
VMEM_SPEC = pl.BlockSpec(memory_space=pltpu.VMEM)
SMEM_SPEC = pl.BlockSpec(memory_space=pltpu.SMEM)


def _params(sem=None):
    return pltpu.CompilerParams(dimension_semantics=sem, vmem_limit_bytes=V7X_VMEM_LIMIT_BYTES)


def _resident(shape):
    nd = len(shape)
    return pl.BlockSpec(shape, lambda *_: (0,) * nd, pipeline_mode=pl.Buffered(1))


def _acc_spec(shape):
    nd = len(shape)
    return pl.BlockSpec(shape, lambda *_: (0,) * nd)


def _call(body, *, name, grid, out_shape, in_specs, out_specs, args, sem, scratch_shapes=(), aliases=None, hook=None):
    aliases = aliases or {}
    if hook is None:
        return pl.pallas_call(body, name=name, grid=grid, out_shape=out_shape, in_specs=in_specs, out_specs=out_specs,
                              scratch_shapes=list(scratch_shapes), input_output_aliases=aliases,
                              compiler_params=_params(sem))(*args)
    n_in, n_out, n_scr = len(in_specs), len(out_shape), len(scratch_shapes)
    h_in, h_out = len(hook.ins), len(hook.out_shape)

    def at_step(pick):
        conds = [pl.program_id(ax) == pick(size) for ax, size in enumerate(grid)]
        return functools.reduce(jnp.logical_and, conds)

    def hosted(*refs):
        k_in, x_in = refs[:n_in], refs[n_in:n_in + h_in]
        o0 = n_in + h_in
        k_out, x_out = refs[o0:o0 + n_out], refs[o0 + n_out:o0 + n_out + h_out]
        s0 = o0 + n_out + h_out
        k_scr, x_scr = refs[s0:s0 + n_scr], refs[s0 + n_scr:]

        @pl.when(at_step(lambda size: 0))
        def _():
            hook.start(x_in, x_out, x_scr)
        body(*k_in, *k_out, *k_scr)
        early = bool(hook.lead) and grid[0] > hook.lead
        if early:
            on_axis0 = pl.program_id(0) == grid[0] - 1 - hook.lead
            rest_first = [pl.program_id(ax) == 0 for ax in range(1, len(grid))]

            @pl.when(functools.reduce(jnp.logical_and, rest_first, on_axis0))
            def _():
                hook.relay(x_in, x_out, x_scr)

        @pl.when(at_step(lambda size: size - 1))
        def _():
            if not early:
                hook.relay(x_in, x_out, x_scr)
            hook.finish(x_in, x_out, x_scr)

    res = pl.pallas_call(
        hosted, name=name, grid=grid, out_shape=tuple(out_shape) + tuple(hook.out_shape),
        in_specs=list(in_specs) + [ANY] * h_in, out_specs=tuple(out_specs) + (ANY,) * h_out,
        scratch_shapes=list(scratch_shapes) + list(hook.scratch), input_output_aliases=aliases,
        compiler_params=_params(("arbitrary",) * len(grid)),
    )(*args, *hook.ins)
    hook.results = res[n_out:]
    return res[:n_out]


def _nn(a, b):
    return jnp.dot(a, b, preferred_element_type=F32)


def _nt(a, b):
    return lax.dot_general(a, b, (((1,), (1,)), ((), ())), preferred_element_type=F32)


def _tn(a, b):
    return lax.dot_general(a, b, (((0,), (0,)), ((), ())), preferred_element_type=F32)


def _sigmoid(v):
    return 1.0 / (1.0 + jnp.exp(-v))


def _rms(x):
    r = lax.rsqrt(jnp.mean(x * x, axis=-1, keepdims=True) + EPS)
    return x * r, r


def _rms_bwd(dh, n, r, g):
    dn = dh * g
    dx = r * (dn - n * jnp.mean(dn * n, axis=-1, keepdims=True))
    return dx, jnp.sum(dh * n, axis=0, keepdims=True)


def _ffn_fwd(x, gn, wgu, wd, name, hook=None, head=None):
    t, d = x.shape
    f = wd.shape[0]
    tm, fc = 256, 1408

    def body(x_ref, gn_ref, wgu_ref, wd_ref, *rest):
        xv = x_ref[...]
        n, _ = _rms(xv)
        h = (n * gn_ref[...]).astype(BF)
        g_ref, u_ref = rest[-2:] if head is None else rest[3:5]
        acc = jnp.zeros((tm, d), F32)
        for c0 in range(0, f, fc):
            g = _nn(h, wgu_ref[:, c0:c0 + fc])
            u = _nn(h, wgu_ref[:, f + c0:f + c0 + fc])
            g_ref[:, c0:c0 + fc] = g.astype(BF)
            u_ref[:, c0:c0 + fc] = u.astype(BF)
            a = (g * _sigmoid(g)) * u
            acc = acc + _nn(a.astype(BF), wd_ref[c0:c0 + fc, :])
        y = xv + 0.5 * acc
        if head is None:
            rest[0][...] = y
            return
        gf_ref, t_ref, dy_ref, _, _, dgf_ref, loss_ref = rest

        @pl.when(pl.program_id(0) == 0)
        def _():
            dgf_ref[...] = jnp.zeros_like(dgf_ref)
            loss_ref[...] = jnp.zeros_like(loss_ref)
        ny, ry = _rms(y)
        gf = gf_ref[...]
        err = ny * gf - t_ref[...]
        loss_ref[...] += 0.5 * jnp.sum(jnp.sum(err * err, axis=-1, keepdims=True) / d, axis=0, keepdims=True)
        dy, dgf = _rms_bwd(err / d, ny, ry, gf)
        dy_ref[...] = dy
        dgf_ref[...] += dgf

    row = pl.BlockSpec((tm, d), lambda i: (i, 0))
    frow = pl.BlockSpec((tm, f), lambda i: (i, 0))
    in_specs = [row, _resident((1, d)), _resident(wgu.shape), _resident(wd.shape)]
    out_shape = (S((t, d), F32), S((t, f), BF), S((t, f), BF))
    out_specs = (row, frow, frow)
    args = (x, gn, wgu, wd)
    if head is not None:
        in_specs += [_resident((1, d)), row]
        out_shape += (S((1, d), F32), S((1, 128), F32))
        out_specs += (_acc_spec((1, d)), _acc_spec((1, 128)))
        args += tuple(head)
    return _call(body, name=name, grid=(t // tm,), out_shape=out_shape, in_specs=in_specs, out_specs=out_specs,
                 sem=("parallel",) if head is None else ("arbitrary",), args=args, hook=hook)


def _mix_proj(x, gn, w_in, hook=None):
    t, d = x.shape
    tm = 256
    nqkv = 1536
    ne = w_in.shape[1] - nqkv

    def body(x_ref, gn_ref, w_ref, qkv_ref, e_ref):
        n, _ = _rms(x_ref[...])
        h = (n * gn_ref[...]).astype(BF)
        qkv_ref[...] = _nn(h, w_ref[:, 0:nqkv]).astype(BF)
        for c0 in range(0, ne, 1024):
            e_ref[:, c0:c0 + 1024] = _nn(h, w_ref[:, nqkv + c0:nqkv + c0 + 1024]).astype(BF)

    return _call(
        body, name="mix_proj", grid=(t // tm,),
        out_shape=(S((t, nqkv), BF), S((t, ne), BF)),
        in_specs=[pl.BlockSpec((tm, d), lambda i: (i, 0)), _resident((1, d)), _resident(w_in.shape)],
        out_specs=(pl.BlockSpec((tm, nqkv), lambda i: (i, 0)), pl.BlockSpec((tm, ne), lambda i: (i, 0))),
        sem=("parallel",), args=(x, gn, w_in), hook=hook)


def _t5_bucket(rel):
    n = jnp.maximum(rel, 0)
    max_exact = REL_BUCKETS // 2
    nf = jnp.maximum(n, 1).astype(F32)
    large = max_exact + (jnp.log(nf / max_exact) / math.log(REL_MAX_DIST / max_exact)
                         * (REL_BUCKETS - max_exact)).astype(I32)
    large = jnp.minimum(large, REL_BUCKETS - 1)
    return jnp.where(n < max_exact, n, large)


def _block_rel():
    i = lax.broadcasted_iota(I32, (BLOCK, 2 * BLOCK), 0)
    j = lax.broadcasted_iota(I32, (BLOCK, 2 * BLOCK), 1)
    return i + BLOCK - j


def _bias_table(rb_ref, o_ref):
    bucket = _t5_bucket(_block_rel())
    for h in range(SWA_HEADS):
        acc = jnp.zeros((BLOCK, 2 * BLOCK), F32)
        for b in range(REL_BUCKETS):
            acc = jnp.where(bucket == b, rb_ref[h, b], acc)
        o_ref[h] = acc


GROUP_ROWS = SWA_GROUP * BLOCK


def _swa_visible(b, pq_ref, pkp_ref, pkc_ref):
    pk = jnp.concatenate([pkp_ref[...], pkc_ref[...]], axis=1)
    col = lax.broadcasted_iota(I32, (1, 2 * BLOCK), 1)
    pk = jnp.where(jnp.logical_and(b == 0, col < BLOCK), POS_PAD, pk)
    rel = jnp.concatenate([pq_ref[...]] * SWA_GROUP, axis=0) - pk
    return jnp.logical_and(rel >= 0, rel < WINDOW)


def _swa_poison(b, pq_ref, pkp_ref, pkc_ref):
    pk = jnp.concatenate([pkp_ref[...], pkc_ref[...]], axis=1)
    col = lax.broadcasted_iota(I32, (1, 2 * BLOCK), 1)
    pk = jnp.where(jnp.logical_and(b == 0, col < BLOCK), POS_PAD, pk)
    rel = pq_ref[...] - pk
    off = jnp.logical_and(jnp.logical_and(rel >= 0, rel < WINDOW), rel != _block_rel())
    return jnp.where(jnp.max(off.astype(F32)) > 0.0, jnp.nan, 0.0)


def _group_heads(ref, hk):
    h0 = hk * SWA_GROUP
    return jnp.concatenate([ref[:, (h0 + g) * HEAD_DIM:(h0 + g + 1) * HEAD_DIM] for g in range(SWA_GROUP)], axis=0)


def _group_sinks(sink_ref, hk):
    row = lax.broadcasted_iota(I32, (GROUP_ROWS, 1), 0)
    col = jnp.zeros((GROUP_ROWS, 1), F32) + sink_ref[0, hk * SWA_GROUP]
    for g in range(1, SWA_GROUP):
        col = jnp.where(row >= g * BLOCK, sink_ref[0, hk * SWA_GROUP + g], col)
    return col


def _swa_probs(qg, kh, vis, bias, sink):
    s = _nt(qg, kh) * (HEAD_DIM ** -0.5)
    s = jnp.where(vis, s + bias, NEG)
    m = jnp.maximum(jnp.max(s, axis=-1, keepdims=True), sink)
    p = jnp.exp(s - m)
    ps = jnp.exp(sink - m)
    inv = 1.0 / (jnp.sum(p, axis=-1, keepdims=True) + ps)
    return p * inv, ps * inv


def _swa_fwd(qkv, pos_col, pos_row, bias_t, sinks, hook=None):
    t = qkv.shape[0]
    nb = t // BLOCK
    qw = SWA_HEADS * HEAD_DIM
    kw = SWA_KV_HEADS * HEAD_DIM

    def body(q_ref, kp_ref, kc_ref, vp_ref, vc_ref, pq_ref, pkp_ref, pkc_ref, bias_ref, sink_ref, o_ref):
        b = pl.program_id(0)
        vis = _swa_visible(b, pq_ref, pkp_ref, pkc_ref)
        poison = _swa_poison(b, pq_ref, pkp_ref, pkc_ref)
        k2 = jnp.concatenate([kp_ref[...], kc_ref[...]], axis=0)
        v2 = jnp.concatenate([vp_ref[...], vc_ref[...]], axis=0)
        for hk in range(SWA_KV_HEADS):
            kh = k2[:, hk * HEAD_DIM:(hk + 1) * HEAD_DIM]
            vh = v2[:, hk * HEAD_DIM:(hk + 1) * HEAD_DIM]
            pn, _ = _swa_probs(_group_heads(q_ref, hk), kh, vis, bias_ref[hk], _group_sinks(sink_ref, hk))
            o = _nn(pn.astype(BF), vh) + poison
            for g in range(SWA_GROUP):
                h = hk * SWA_GROUP + g
                o_ref[:, h * HEAD_DIM:(h + 1) * HEAD_DIM] = o[g * BLOCK:(g + 1) * BLOCK]

    prev = lambda b: jnp.maximum(b - 1, 0)
    return _call(
        body, name="swa_fwd", grid=(nb,), out_shape=(S((t, qw), F32),),
        in_specs=[
            pl.BlockSpec((BLOCK, qw), lambda b: (b, 0)),
            pl.BlockSpec((BLOCK, kw), lambda b: (prev(b), qw // kw)),
            pl.BlockSpec((BLOCK, kw), lambda b: (b, qw // kw)),
            pl.BlockSpec((BLOCK, kw), lambda b: (prev(b), qw // kw + 1)),
            pl.BlockSpec((BLOCK, kw), lambda b: (b, qw // kw + 1)),
            pl.BlockSpec((BLOCK, 1), lambda b: (b, 0)),
            pl.BlockSpec((1, BLOCK), lambda b: (0, prev(b))),
            pl.BlockSpec((1, BLOCK), lambda b: (0, b)),
            _resident(bias_t.shape),
            SMEM_SPEC,
        ],
        out_specs=(pl.BlockSpec((BLOCK, qw), lambda b: (b, 0)),),
        sem=("parallel",), args=(qkv, qkv, qkv, qkv, qkv, pos_col, pos_row, pos_row, bias_t, sinks), hook=hook)[0]


HALO = 16


def _conv_taps(z, zh, first):
    tm = z.shape[0]
    zh = jnp.where(first, 0.0, zh)
    row = lax.broadcasted_iota(I32, (tm, 1), 0)
    z1 = jnp.where(row == 0, zh[HALO - 1:HALO, :], pltpu.roll(z, 1, 0))
    z2 = jnp.where(row == 0, zh[HALO - 2:HALO - 1, :], jnp.where(row == 1, zh[HALO - 1:HALO, :], pltpu.roll(z, 2, 0)))
    return z1, z2


def _mix_out_fwd(e, attn, conv_w, w_out, x, hook=None):
    t, d = x.shape
    tm = 256
    hb = tm // HALO
    f32 = lambda ref: ref[...].astype(F32)

    def body(c_ref, b_ref, u_ref, ga_ref, gc_ref, ch_ref, uh_ref, attn_ref, cw_ref, w_ref, x_ref, xo_ref, mg_ref):
        i = pl.program_id(0)
        z = f32(c_ref) * f32(u_ref)
        z1, z2 = _conv_taps(z, f32(ch_ref) * f32(uh_ref), i == 0)
        s = cw_ref[0:1, :] * z2 + cw_ref[1:2, :] * z1 + cw_ref[2:3, :] * z
        conv = f32(b_ref) * s
        merged = (_sigmoid(f32(ga_ref)) * attn_ref[...] + _sigmoid(f32(gc_ref)) * conv).astype(BF)
        mg_ref[...] = merged
        xo_ref[...] = x_ref[...] + _nn(merged, w_ref[...])

    ecol = lambda cb: pl.BlockSpec((tm, d), lambda i: (i, cb))
    halo = lambda cb: pl.BlockSpec((HALO, d), lambda i: (jnp.maximum(i * hb - 1, 0), cb))
    row = pl.BlockSpec((tm, d), lambda i: (i, 0))
    return _call(
        body, name="mix_out_fwd", grid=(t // tm,),
        out_shape=(S((t, d), F32), S((t, d), BF)),
        in_specs=[ecol(0), ecol(1), ecol(2), ecol(3), ecol(4), halo(0), halo(2), row,
                  _resident(conv_w.shape), _resident(w_out.shape), row],
        out_specs=(row, row),
        sem=("parallel",), args=(e, e, e, e, e, e, e, attn, conv_w, w_out, x), hook=hook)


def _mem_kv(mem, gm, wkv):
    m, d = mem.shape

    def body(mem_ref, gm_ref, w_ref, mh_ref, kv_ref):
        n, _ = _rms(mem_ref[...])
        mh = (n * gm_ref[...]).astype(BF)
        mh_ref[...] = mh
        kv_ref[...] = _nn(mh, w_ref[...]).astype(BF)

    return pl.pallas_call(
        body, name="mem_kv", out_shape=(S((m, d), BF), S((m, wkv.shape[1]), BF)),
        compiler_params=_params(),
    )(mem, gm, wkv)


def _xattn_probs(qh, kh):
    s = _nt(qh, kh) * (kh.shape[1] ** -0.5)
    p = jnp.exp(s - jnp.max(s, axis=-1, keepdims=True))
    return p * (1.0 / jnp.sum(p, axis=-1, keepdims=True))


def _xattn_fwd(x, gn, wq, kv, wo):
    t, d = x.shape
    tm = 256
    hd = d // MEM_HEADS

    def body(x_ref, gn_ref, wq_ref, kv_ref, wo_ref, xo_ref, q_ref, o_ref):
        xv = x_ref[...]
        n, _ = _rms(xv)
        q = _nn((n * gn_ref[...]).astype(BF), wq_ref[...]).astype(BF)
        q_ref[...] = q
        outs = []
        for hh in range(MEM_HEADS):
            p = _xattn_probs(q[:, hh * hd:(hh + 1) * hd], kv_ref[:, hh * hd:(hh + 1) * hd])
            outs.append(_nn(p.astype(BF), kv_ref[:, d + hh * hd:d + (hh + 1) * hd]))
        o = jnp.concatenate(outs, axis=1).astype(BF)
        o_ref[...] = o
        xo_ref[...] = xv + _nn(o, wo_ref[...])

    row = pl.BlockSpec((tm, d), lambda i: (i, 0))
    return pl.pallas_call(
        body, name="xattn_fwd", grid=(t // tm,),
        out_shape=(S((t, d), F32), S((t, d), BF), S((t, d), BF)),
        in_specs=[row, _resident((1, d)), _resident(wq.shape), _resident(kv.shape), _resident(wo.shape)],
        out_specs=(row, row, row),
        compiler_params=_params(("parallel",)),
    )(x, gn, wq, kv, wo)


def _ffn_bwd(dxo, x, gn, g, u, wgu, wd, name, hook=None):
    t, d = x.shape
    f = wd.shape[0]
    tm, fc = 256, 1408

    def body(dxo_ref, x_ref, gn_ref, g_ref, u_ref, wgu_ref, wd_ref, dx_ref, dgn_ref, dgu_ref, a_ref, h_ref, dyh_ref):
        @pl.when(pl.program_id(0) == 0)
        def _():
            dgn_ref[...] = jnp.zeros_like(dgn_ref)
        dxov = dxo_ref[...]
        dyh = (0.5 * dxov).astype(BF)
        dyh_ref[...] = dyh
        n, r = _rms(x_ref[...])
        gnv = gn_ref[...]
        h_ref[...] = (n * gnv).astype(BF)
        dh = jnp.zeros((tm, d), F32)
        for c0 in range(0, f, fc):
            gv = g_ref[:, c0:c0 + fc].astype(F32)
            uv = u_ref[:, c0:c0 + fc].astype(F32)
            da = _nt(dyh, wd_ref[c0:c0 + fc, :])
            sg = _sigmoid(gv)
            silu = gv * sg
            a_ref[:, c0:c0 + fc] = (silu * uv).astype(BF)
            dg = (da * uv * (sg * (1.0 + gv * (1.0 - sg)))).astype(BF)
            du = (da * silu).astype(BF)
            dgu_ref[:, c0:c0 + fc] = dg
            dgu_ref[:, f + c0:f + c0 + fc] = du
            dh = dh + _nt(dg, wgu_ref[:, c0:c0 + fc]) + _nt(du, wgu_ref[:, f + c0:f + c0 + fc])
        dx, dgn = _rms_bwd(dh, n, r, gnv)
        dx_ref[...] = dxov + dx
        dgn_ref[...] += dgn

    row = pl.BlockSpec((tm, d), lambda i: (i, 0))
    frow = pl.BlockSpec((tm, f), lambda i: (i, 0))
    return _call(
        body, name=name, grid=(t // tm,),
        out_shape=(S((t, d), F32), S((1, d), F32), S((t, 2 * f), BF), S((t, f), BF), S((t, d), BF), S((t, d), BF)),
        in_specs=[row, row, _resident((1, d)), frow, frow, _resident(wgu.shape), _resident(wd.shape)],
        out_specs=(row, _acc_spec((1, d)), pl.BlockSpec((tm, 2 * f), lambda i: (i, 0)), frow, row, row),
        sem=("arbitrary",), args=(dxo, x, gn, g, u, wgu, wd), hook=hook)


def _dw_pair(pairs, tn, name, kind, hook=None):
    npairs = len(pairs)
    t, ka = pairs[0][0].shape
    nb = pairs[0][1].shape[1]
    tt = 2048 if npairs == 1 else 1024
    nt, nj = t // tt, nb // tn
    col = kind == "col"
    rh = ka // 2 if col else ka // 8
    tile = (rh, tn) if col else (4, rh, tn)
    half = (rh, nb) if col else (4, rh, nb)
    lead = (slice(None),) * (len(tile) - 1)

    def body(*refs):
        ins, mines, sibs = refs[:2 * npairs], refs[2 * npairs:3 * npairs], refs[3 * npairs:4 * npairs]
        acc_ref, stage, ssem, rsem = refs[4 * npairs:]
        j, k = pl.program_id(0), pl.program_id(1)
        x, y, c = _place()
        sibling = (x, y, 1 - c)

        def send(p, slot, jj):
            dst = sibs[p].at[lead + (pl.ds(pl.multiple_of(jj * tn, 128), tn),)]
            return _remote(stage.at[slot], dst, ssem.at[slot], rsem.at[p], sibling)

        def rows(s, whose):
            return acc_ref[pl.ds(pl.multiple_of(s * 2 * rh + whose * rh, 16), rh), :].astype(BF)

        def step(p):
            a_ref, b_ref, mine_ref = ins[2 * p], ins[2 * p + 1], mines[p]

            @pl.when(k == 0)
            def _():
                acc_ref[...] = jnp.zeros_like(acc_ref)
            acc_ref[...] += _tn(a_ref[...], b_ref[...])

            @pl.when(k == nt - 1)
            def _():
                slot = j % 2

                @pl.when(j >= 2)
                def _():
                    send(p, slot, 0).wait_send()
                if col:
                    mine_ref[...] = rows(0, c)
                    stage[slot] = rows(0, 1 - c)
                else:
                    for s in range(4):
                        mine_ref[s] = rows(s, c)
                        stage[slot, s] = rows(s, 1 - c)
                send(p, slot, j - p * nj).start()

        for p in range(npairs):
            pl.when(j // nj == p)(functools.partial(step, p))

        @pl.when(jnp.logical_and(j == npairs * nj - 1, k == nt - 1))
        def _():
            for jj in range(max(npairs * nj - 2, 0), npairs * nj):
                send(0, jj % 2, 0).wait_send()
            for p in range(npairs):
                _remote(sibs[p], sibs[p], ssem.at[0], rsem.at[p], sibling).wait_recv()

    in_specs, args = [], []
    for p, (a, b) in enumerate(pairs):
        on = lambda j, p=p: j // nj == p
        in_specs += [pl.BlockSpec((tt, ka), lambda j, k, on=on: (jnp.where(on(j), k, 0), 0)),
                     pl.BlockSpec((tt, tn), lambda j, k, on=on, p=p: (jnp.where(on(j), k, 0), jnp.clip(j - p * nj, 0, nj - 1)))]
        args += [a, b]
    mine_spec = lambda p: pl.BlockSpec(tile, (lambda j, k: (0, jnp.clip(j - p * nj, 0, nj - 1))) if col
                                       else (lambda j, k: (0, 0, jnp.clip(j - p * nj, 0, nj - 1))))
    res = _call(
        body, name=name, grid=(npairs * nj, nt), out_shape=(S(half, BF),) * (2 * npairs),
        in_specs=in_specs, out_specs=tuple(mine_spec(p) for p in range(npairs)) + (ANY,) * npairs,
        scratch_shapes=[pltpu.VMEM((ka, tn), F32), pltpu.VMEM((2,) + tile, BF), pltpu.SemaphoreType.DMA((2,)),
                        pltpu.SemaphoreType.DMA((npairs,))],
        sem=("arbitrary", "arbitrary"), args=args, hook=hook)
    return [(res[p], res[npairs + p]) for p in range(npairs)]


def _xattn_bwd(dxo, x, gn, q, kv, wq, wo, mh, mem, wkv, hook=None):
    t, d = x.shape
    tm = 256
    hd = d // MEM_HEADS
    nkv = kv.shape[0]
    rows, cols = wkv.shape
    rh = rows // 2

    def body(dxo_ref, x_ref, gn_ref, q_ref, kv_ref, wq_ref, wo_ref, mh_ref, mem_ref, wkv_ref,
             dx_ref, dgn_ref, dxh_ref, h_ref, dq_ref, mine_ref, sib_ref, dgm_ref, dkv_ref, whole, ssem, rsem):
        @pl.when(pl.program_id(0) == 0)
        def _():
            dgn_ref[...] = jnp.zeros_like(dgn_ref)
            dkv_ref[...] = jnp.zeros_like(dkv_ref)
        dxov = dxo_ref[...]
        dxh = dxov.astype(BF)
        dxh_ref[...] = dxh
        do = _nt(dxh, wo_ref[...]).astype(BF)
        dqs = []
        for hh in range(MEM_HEADS):
            lo, hi = hh * hd, (hh + 1) * hd
            qh = q_ref[:, lo:hi]
            kh = kv_ref[:, lo:hi]
            vh = kv_ref[:, d + lo:d + hi]
            doh = do[:, lo:hi]
            p = _xattn_probs(qh, kh)
            dp = _nt(doh, vh)
            ds = (p * (dp - jnp.sum(p * dp, axis=-1, keepdims=True)) * (hd ** -0.5)).astype(BF)
            dqs.append(_nn(ds, kh))
            dkv_ref[:, lo:hi] += _tn(ds, qh)
            dkv_ref[:, d + lo:d + hi] += _tn(p.astype(BF), doh)
        dq = jnp.concatenate(dqs, axis=1).astype(BF)
        dq_ref[...] = dq
        n, r = _rms(x_ref[...])
        gnv = gn_ref[...]
        h_ref[...] = (n * gnv).astype(BF)
        dx, dgn = _rms_bwd(_nt(dq, wq_ref[...]), n, r, gnv)
        dx_ref[...] = dxov + dx
        dgn_ref[...] += dgn

        @pl.when(pl.program_id(0) == pl.num_programs(0) - 1)
        def _():
            dkvb = dkv_ref[...].astype(BF)
            dmh = _nt(dkvb, wkv_ref[...])
            nm, _ = _rms(mem_ref[...])
            px, py, c = _place()
            whole[...] = _tn(mh_ref[...], dkvb).astype(BF)
            cp = _remote(whole.at[pl.ds(pl.multiple_of((1 - c) * rh, 16), rh), :], sib_ref, ssem, rsem, (px, py, 1 - c))
            cp.start()
            mine_ref[...] = whole[pl.ds(pl.multiple_of(c * rh, 16), rh), :]
            dgm_ref[...] = jnp.sum(dmh * nm, axis=0, keepdims=True)
            cp.wait()

    row = pl.BlockSpec((tm, d), lambda i: (i, 0))
    dx, dgn, dxh, h, dq, mine, sib, dgm = _call(
        body, name="xattn_bwd", grid=(t // tm,),
        out_shape=(S((t, d), F32), S((1, d), F32), S((t, d), BF), S((t, d), BF), S((t, d), BF),
                   S((rh, cols), BF), S((rh, cols), BF), S((1, d), F32)),
        in_specs=[row, row, _resident((1, d)), row, _resident(kv.shape), _resident(wq.shape), _resident(wo.shape),
                  _resident(mh.shape), _resident(mem.shape), _resident(wkv.shape)],
        out_specs=(row, _acc_spec((1, d)), row, row, row, _acc_spec((rh, cols)), ANY, _acc_spec((1, d))),
        scratch_shapes=[pltpu.VMEM((nkv, 2 * d), F32), pltpu.VMEM((rows, cols), BF),
                        pltpu.SemaphoreType.DMA, pltpu.SemaphoreType.DMA],
        sem=("arbitrary",), args=(dxo, x, gn, q, kv, wq, wo, mh, mem, wkv), hook=hook)
    return dx, dgn, dxh, h, dq, (mine, sib), dgm


def _mix_out_bwd(dxo, e, attn, conv_w, w_out, nqkv, hook=None):
    t, d = attn.shape
    tm = 256
    nt = t // tm
    f32 = lambda ref: ref[...].astype(F32)

    def body(dxo_ref, dxn_ref, c_ref, b_ref, u_ref, ga_ref, gc_ref, ch_ref, uh_ref, bn_ref, gcn_ref,
             attn_ref, cw_ref, w_ref, dattn_ref, de_ref, dcw_ref, dxh_ref):
        i = pl.program_id(0)

        @pl.when(i == 0)
        def _():
            dcw_ref[...] = jnp.zeros_like(dcw_ref)
        dxh = dxo_ref[...].astype(BF)
        dxh_ref[...] = dxh
        w = w_ref[...]
        dm = _nt(dxh, w)
        dmn = _nt(dxn_ref[...].astype(BF), w)
        cv, bv, uv = f32(c_ref), f32(b_ref), f32(u_ref)
        sga = _sigmoid(f32(ga_ref))
        sgc = _sigmoid(f32(gc_ref))
        z = cv * uv
        z1, z2 = _conv_taps(z, f32(ch_ref) * f32(uh_ref), i == 0)
        w0, w1, w2 = cw_ref[0:1, :], cw_ref[1:2, :], cw_ref[2:3, :]
        s = w0 * z2 + w1 * z1 + w2 * z
        av = attn_ref[...]
        dattn_ref[...] = (dm * sga).astype(BF)
        dconv = dm * sgc
        ds = dconv * bv
        dsn = jnp.where(i == nt - 1, 0.0, dmn * _sigmoid(gcn_ref[0:8, :].astype(F32)) * bn_ref[0:8, :].astype(F32))
        row = lax.broadcasted_iota(I32, (tm, 1), 0)
        dsp1 = jnp.where(row == tm - 1, dsn[0:1, :], pltpu.roll(ds, tm - 1, 0))
        dsp2 = jnp.where(row == tm - 2, dsn[0:1, :], jnp.where(row == tm - 1, dsn[1:2, :], pltpu.roll(ds, tm - 2, 0)))
        dz = w2 * ds + w1 * dsp1 + w0 * dsp2
        de_ref[:, nqkv:nqkv + d] = (dz * uv).astype(BF)
        de_ref[:, nqkv + d:nqkv + 2 * d] = (dconv * s).astype(BF)
        de_ref[:, nqkv + 2 * d:nqkv + 3 * d] = (dz * cv).astype(BF)
        de_ref[:, nqkv + 3 * d:nqkv + 4 * d] = (dm * av * sga * (1.0 - sga)).astype(BF)
        de_ref[:, nqkv + 4 * d:nqkv + 5 * d] = (dm * (bv * s) * sgc * (1.0 - sgc)).astype(BF)
        dcw_ref[0:1, :] += jnp.sum(ds * z2, axis=0, keepdims=True)
        dcw_ref[1:2, :] += jnp.sum(ds * z1, axis=0, keepdims=True)
        dcw_ref[2:3, :] += jnp.sum(ds * z, axis=0, keepdims=True)

    ecol = lambda cb: pl.BlockSpec((tm, d), lambda i: (i, cb))
    prev = lambda cb: pl.BlockSpec((HALO, d), lambda i: (jnp.maximum(i * (tm // HALO) - 1, 0), cb))
    nxt = lambda rows, cb: pl.BlockSpec((rows, d), lambda i: (jnp.minimum((i + 1) * (tm // rows), t // rows - 1), cb))
    row = pl.BlockSpec((tm, d), lambda i: (i, 0))
    return _call(
        body, name="mix_out_bwd", grid=(nt,),
        out_shape=(S((t, d), BF), S((t, nqkv + 5 * d), BF), S((8, d), F32), S((t, d), BF)),
        in_specs=[row, nxt(8, 0), ecol(0), ecol(1), ecol(2), ecol(3), ecol(4), prev(0), prev(2), nxt(HALO, 1), nxt(HALO, 4),
                  row, _resident(conv_w.shape), _resident(w_out.shape)],
        out_specs=(row, pl.BlockSpec((tm, nqkv + 5 * d), lambda i: (i, 0)), _acc_spec((8, d)), row),
        sem=("arbitrary",), args=(dxo, dxo, e, e, e, e, e, e, e, e, e, attn, conv_w, w_out), hook=hook)


def _swa_bwd(qkv, dattn, pos_col, pos_row, bias_t, sinks, dp, hook=None):
    t = qkv.shape[0]
    nb = t // BLOCK
    qw = SWA_HEADS * HEAD_DIM
    kw = SWA_KV_HEADS * HEAD_DIM

    def body(q_ref, kp_ref, kc_ref, vp_ref, vc_ref, do_ref, pq_ref, pkp_ref, pkc_ref, bias_ref, sink_ref,
             dp_in, dp_ref, gb_ref, dsk_ref, carry_ref, dq_ref):
        b = pl.program_id(0)

        @pl.when(b == 0)
        def _():
            gb_ref[...] = jnp.zeros_like(gb_ref)
            dsk_ref[...] = jnp.zeros_like(dsk_ref)
            carry_ref[...] = jnp.zeros_like(carry_ref)
            dq_ref[...] = jnp.zeros_like(dq_ref)
        dp_ref[:, 0:qw] = dq_ref[...]

        @pl.when(b < nb)
        def _():
            vis = _swa_visible(b, pq_ref, pkp_ref, pkc_ref)
            k2 = jnp.concatenate([kp_ref[...], kc_ref[...]], axis=0)
            v2 = jnp.concatenate([vp_ref[...], vc_ref[...]], axis=0)
            for hk in range(SWA_KV_HEADS):
                lo, hi = hk * HEAD_DIM, (hk + 1) * HEAD_DIM
                kh = k2[:, lo:hi]
                vh = v2[:, lo:hi]
                qg = _group_heads(q_ref, hk)
                dog = _group_heads(do_ref, hk)
                pn, psn = _swa_probs(qg, kh, vis, bias_ref[hk], _group_sinks(sink_ref, hk))
                dp = _nt(dog, vh)
                delta = jnp.sum(pn * dp, axis=-1, keepdims=True)
                ds = pn * (dp - delta)
                gb_ref[hk] += ds
                dsk_ref[hk] += -psn * delta
                dsb = (ds * (HEAD_DIM ** -0.5)).astype(BF)
                dqg = _nn(dsb, kh).astype(BF)
                for g in range(SWA_GROUP):
                    h = hk * SWA_GROUP + g
                    dq_ref[:, h * HEAD_DIM:(h + 1) * HEAD_DIM] = dqg[g * BLOCK:(g + 1) * BLOCK]
                dk = _tn(dsb, qg)
                dv = _tn(pn.astype(BF), dog)
                dp_ref[:, qw + lo:qw + hi] = (carry_ref[:, lo:hi] + dk[0:BLOCK]).astype(BF)
                dp_ref[:, qw + kw + lo:qw + kw + hi] = (carry_ref[:, kw + lo:kw + hi] + dv[0:BLOCK]).astype(BF)
                carry_ref[:, lo:hi] = dk[BLOCK:2 * BLOCK]
                carry_ref[:, kw + lo:kw + hi] = dv[BLOCK:2 * BLOCK]

        @pl.when(b == nb)
        def _():
            dp_ref[:, qw:qw + 2 * kw] = carry_ref[...].astype(BF)

    cur = lambda b: jnp.minimum(b, nb - 1)
    prev = lambda b: jnp.maximum(cur(b) - 1, 0)
    return _call(
        body, name="swa_bwd", grid=(nb + 1,),
        out_shape=(S(dp.shape, BF), S((SWA_KV_HEADS, GROUP_ROWS, 2 * BLOCK), F32), S((SWA_KV_HEADS, GROUP_ROWS, 1), F32)),
        in_specs=[
            pl.BlockSpec((BLOCK, qw), lambda b: (cur(b), 0)),
            pl.BlockSpec((BLOCK, kw), lambda b: (prev(b), qw // kw)),
            pl.BlockSpec((BLOCK, kw), lambda b: (cur(b), qw // kw)),
            pl.BlockSpec((BLOCK, kw), lambda b: (prev(b), qw // kw + 1)),
            pl.BlockSpec((BLOCK, kw), lambda b: (cur(b), qw // kw + 1)),
            pl.BlockSpec((BLOCK, qw), lambda b: (cur(b), 0)),
            pl.BlockSpec((BLOCK, 1), lambda b: (cur(b), 0)),
            pl.BlockSpec((1, BLOCK), lambda b: (0, prev(b))),
            pl.BlockSpec((1, BLOCK), lambda b: (0, cur(b))),
            _resident(bias_t.shape),
            SMEM_SPEC,
            ANY,
        ],
        out_specs=(
            pl.BlockSpec((BLOCK, qw + 2 * kw), lambda b: (jnp.maximum(b - 1, 0), 0)),
            _acc_spec((SWA_KV_HEADS, GROUP_ROWS, 2 * BLOCK)),
            _acc_spec((SWA_KV_HEADS, GROUP_ROWS, 1)),
        ),
        scratch_shapes=[pltpu.VMEM((BLOCK, 2 * kw), F32), pltpu.VMEM((BLOCK, qw), BF)],
        aliases={11: 0}, sem=("arbitrary",),
        args=(qkv, qkv, qkv, qkv, qkv, dattn, pos_col, pos_row, pos_row, bias_t, sinks, dp), hook=hook)


def _bias_reduce(gb, dsk):
    def body(gb_ref, dsk_ref, drb_ref, dsink_ref):
        bucket = _t5_bucket(_block_rel())
        for b in range(REL_BUCKETS):
            mask = bucket == b
            for h in range(SWA_HEADS):
                drb_ref[h, b] = jnp.sum(jnp.where(mask, gb_ref[h], 0.0))
        for h in range(SWA_HEADS):
            dsink_ref[0, h] = jnp.sum(dsk_ref[h])

    return pl.pallas_call(
        body, name="bias_reduce", out_shape=(S((SWA_HEADS, REL_BUCKETS), F32), S((1, SWA_HEADS), F32)),
        in_specs=[VMEM_SPEC, VMEM_SPEC], out_specs=(SMEM_SPEC, SMEM_SPEC),
    )(gb, dsk)


def _mix_in_bwd(dp, x, gn, w_in, dxo):
    t, d = x.shape
    tm = 256
    npr = dp.shape[1]

    def body(dp_ref, x_ref, gn_ref, w_ref, dxo_ref, dx_ref, dgn_ref, h_ref):
        @pl.when(pl.program_id(0) == 0)
        def _():
            dgn_ref[...] = jnp.zeros_like(dgn_ref)
        dh = jnp.zeros((tm, d), F32)
        for c0 in range(0, npr, 1024):
            c1 = min(c0 + 1024, npr)
            dh = dh + _nt(dp_ref[:, c0:c1], w_ref[:, c0:c1])
        n, r = _rms(x_ref[...])
        gnv = gn_ref[...]
        h_ref[...] = (n * gnv).astype(BF)
        dx, dgn = _rms_bwd(dh, n, r, gnv)
        dx_ref[...] = dxo_ref[...] + dx
        dgn_ref[...] += dgn

    row = pl.BlockSpec((tm, d), lambda i: (i, 0))
    return pl.pallas_call(
        body, name="mix_in_bwd", grid=(t // tm,),
        out_shape=(S((t, d), F32), S((1, d), F32), S((t, d), BF)),
        in_specs=[pl.BlockSpec((tm, npr), lambda i: (i, 0)), row, _resident((1, d)), _resident(w_in.shape), row],
        out_specs=(row, _acc_spec((1, d)), row),
        compiler_params=_params(("arbitrary",)),
    )(dp, x, gn, w_in, dxo)


CAST_STEPS = 4


def _to_bf16(arrays, rel_bias_t, name, hook=None):
    n = len(arrays)

    def body(*refs):
        rb_ref, table_ref = refs[n], refs[-1]
        for src, dst in zip(refs[:n], refs[n + 1:-1]):
            dst[...] = src[...].astype(BF)

        @pl.when(pl.program_id(0) == 0)
        def _():
            _bias_table(rb_ref, table_ref)

    blocks = [pl.BlockSpec((a.shape[0] // CAST_STEPS, a.shape[1]), lambda i: (i, 0)) for a in arrays]
    table = (SWA_HEADS, BLOCK, 2 * BLOCK)
    res = _call(body, name=name, grid=(CAST_STEPS,), out_shape=tuple(S(a.shape, BF) for a in arrays) + (S(table, F32),),
                in_specs=blocks + [SMEM_SPEC], out_specs=tuple(blocks) + (_acc_spec(table),), sem=("arbitrary",),
                args=list(arrays) + [rel_bias_t], hook=hook)
    return res[:n], res[n]


PAIR_STEPS = 4


def _add_bf16(pairs, name):
    def body(*refs):
        ins, outs = refs[:2 * len(pairs)], refs[2 * len(pairs):]
        for q, o_ref in enumerate(outs):
            o_ref[...] = (ins[2 * q][...].astype(F32) + ins[2 * q + 1][...].astype(F32)).astype(BF)

    in_specs, out_specs, out_shape, args = [], [], [], []
    for a, b in pairs:
        rows, cols = a.shape
        blk = pl.BlockSpec((rows // PAIR_STEPS, cols), lambda i: (i, 0))
        in_specs += [blk, blk]
        out_specs.append(blk)
        out_shape.append(S((rows, cols), BF))
        args += [a, b]
    return pl.pallas_call(body, name=name, grid=(PAIR_STEPS,), out_shape=tuple(out_shape), in_specs=in_specs,
                          out_specs=tuple(out_specs), compiler_params=_params(("parallel",)))(*args)


def _adam_update(w, g, m, v):
    mn = ADAM_B1 * m + (1.0 - ADAM_B1) * g
    vn = ADAM_B2 * v + (1.0 - ADAM_B2) * (g * g)
    m_hat = mn / (1.0 - ADAM_B1 ** ADAM_STEP)
    v_hat = vn / (1.0 - ADAM_B2 ** ADAM_STEP)
    return -ADAM_LR * (m_hat / (jnp.sqrt(v_hat) + ADAM_EPS) + ADAM_WD * w), mn, vn


def _adamw(quads, name, steps, echo=False, hook=None):
    n_out = 4 if echo else 3

    def body(*refs):
        ins, outs = refs[:4 * len(quads)], refs[4 * len(quads):]
        for q in range(len(quads)):
            w_ref, g_ref, m_ref, v_ref = ins[4 * q:4 * q + 4]
            res = outs[n_out * q:n_out * q + n_out]
            gv = g_ref[...]
            if echo:
                res[0][...] = gv
            res[-3][...], res[-2][...], res[-1][...] = _adam_update(w_ref[...], gv, m_ref[...], v_ref[...])

    in_specs, out_specs, out_shape, args = [], [], [], []
    for quad in quads:
        rows, cols = quad[0].shape
        blk = pl.BlockSpec((rows // steps, cols), lambda i: (i, 0))
        in_specs += [blk] * 4
        out_specs += [blk] * n_out
        out_shape += [S((rows, cols), F32)] * n_out
        args += list(quad)
    res = _call(body, name=name, grid=(steps,), out_shape=tuple(out_shape), in_specs=in_specs, out_specs=tuple(out_specs),
                sem=("parallel",), args=args, hook=hook)
    return [res[n_out * q:n_out * q + n_out] for q in range(len(quads))]


def _place():
    x, y, c = lax.axis_index("x"), lax.axis_index("y"), lax.axis_index("c")
    return x, y, c


OTHER_CHIPS = ((1, 0), (0, 1), (1, 1))


def _flip(v, f):
    return 1 - v if f else v


def _remote(src, dst, ssem, rsem, dev):
    return pltpu.make_async_remote_copy(src_ref=src, dst_ref=dst, send_sem=ssem, recv_sem=rsem,
                                        device_id=dev, device_id_type=MESH)


class _Both:
    def __init__(self, hooks):
        self.hooks = hooks
        self.ins = [a for h in hooks for a in h.ins]
        self.out_shape = tuple(o for h in hooks for o in h.out_shape)
        self.scratch = [x for h in hooks for x in h.scratch]

    def _each(self, ins, outs, scr):
        i = o = s = 0
        for h in self.hooks:
            ni, no, ns = len(h.ins), len(h.out_shape), len(h.scratch)
            yield h, ins[i:i + ni], outs[o:o + no], scr[s:s + ns]
            i, o, s = i + ni, o + no, s + ns

    def start(self, ins, outs, scr):
        for h, *refs in self._each(ins, outs, scr):
            h.start(*refs)

    def relay(self, ins, outs, scr):
        for h, *refs in self._each(ins, outs, scr):
            h.relay(*refs)

    def finish(self, ins, outs, scr):
        for h, *refs in self._each(ins, outs, scr):
            h.finish(*refs)

    @property
    def lead(self):
        return max(h.lead for h in self.hooks)

    @property
    def results(self):
        return [h.results for h in self.hooks]

    @results.setter
    def results(self, res):
        for h, _, mine, _ in self._each((), tuple(res), ()):
            h.results = mine


class _GatherHook:
    def __init__(self, shards, kinds, lead=2):
        self.ins, self.kinds, self.lead, n = list(shards), list(kinds), lead, len(shards)
        self.out_shape = tuple(
            S((w.shape[0], 4 * w.shape[1]), BF) if k == "col" else S((4,) + w.shape, BF) for w, k in zip(shards, kinds))
        dma = pltpu.SemaphoreType.DMA
        self.scratch = ([dma((n, 3)) for _ in range(5)] + [dma((n,)), dma((n,))]
                        + [pltpu.VMEM((3, w.shape[0] // 2, w.shape[1]), BF) for w in shards]
                        + [pltpu.VMEM(w.shape, BF) for w in shards])

    def _window(self, outs, i, s, half):
        rows, cols = self.ins[i].shape
        rh = rows // 2
        start = pl.multiple_of(half * rh, 16)
        if self.kinds[i] == "col":
            return outs[i].at[pl.ds(start, rh), pl.ds(pl.multiple_of(s * cols, 128), cols)]
        return outs[i].at[s, pl.ds(start, rh), :]

    def _copies(self, ins, outs, scr):
        n = len(ins)
        ssem, rsem, fssem, frsem, ksem, lsem, osem = scr[:7]
        land, own = scr[7:7 + n], scr[7 + n:7 + 2 * n]
        x, y, c = _place()
        sibling = (x, y, 1 - c)
        loads, stores, sends, forwards, keeps, passed = [], [], [], [], [], []
        for i in range(n):
            rows, cols = self.ins[i].shape
            rh = rows // 2
            mine = (outs[i].at[:, pl.ds(pl.multiple_of((2 * x + y) * cols, 128), cols)] if self.kinds[i] == "col"
                    else outs[i].at[2 * x + y])
            loads.append(functools.partial(pltpu.make_async_copy, ins[i], own[i], lsem.at[i]))
            stores.append(functools.partial(pltpu.make_async_copy, own[i], mine, osem.at[i]))
            src = ins[i].at[pl.ds(pl.multiple_of(c * rh, 16), rh), :]
            for j, (fx, fy) in enumerate(OTHER_CHIPS):
                px, py = _flip(x, fx), _flip(y, fy)
                sends.append(functools.partial(_remote, src, land[i].at[j], ssem.at[i, j], rsem.at[i, j], (px, py, c)))
                here = self._window(outs, i, 2 * px + py, c)
                forwards.append(functools.partial(_remote, land[i].at[j], here, fssem.at[i, j], frsem.at[i, j], sibling))
                keeps.append(functools.partial(pltpu.make_async_copy, land[i].at[j], here, ksem.at[i, j]))
                there = self._window(outs, i, 2 * px + py, 1 - c)
                passed.append(functools.partial(_remote, there, there, fssem.at[i, j], frsem.at[i, j], sibling))
        return loads, stores, sends, forwards, keeps, passed

    def start(self, ins, outs, scr):
        loads, _, sends, _, _, _ = self._copies(ins, outs, scr)
        for make in sends + loads:
            make().start()

    def relay(self, ins, outs, scr):
        loads, stores, sends, forwards, keeps, _ = self._copies(ins, outs, scr)
        for load, store in zip(loads, stores):
            load().wait()
            store().start()
        for send, forward, keep in zip(sends, forwards, keeps):
            send().wait_recv()
            forward().start()
            keep().start()

    def finish(self, ins, outs, scr):
        _, stores, sends, forwards, keeps, passed = self._copies(ins, outs, scr)
        for make in passed:
            make().wait_recv()
        for make in sends + forwards:
            make().wait_send()
        for make in keeps + stores:
            make().wait()


class _ChipsHook:
    def __init__(self, parts, kinds):
        self.ins, self.kinds, n = list(parts), list(kinds), len(parts)
        self.out_shape = tuple(
            S((4, p.shape[0], p.shape[1] // 4), BF) if k == "col" else S(p.shape, BF) for p, k in zip(parts, kinds))
        dma = pltpu.SemaphoreType.DMA
        self.scratch = ([dma((n, 3)), dma((n, 3)), dma((n,)), dma((n,))]
                        + [pltpu.VMEM(o.shape[1:], BF) for o in self.out_shape])

    def _slab(self, ins, i, s):
        _, rows, cols = self.out_shape[i].shape
        if self.kinds[i] == "col":
            return ins[i].at[:, pl.ds(pl.multiple_of(s * cols, 128), cols)]
        return ins[i].at[s]

    def _copies(self, ins, outs, scr):
        ssem, rsem, lsem, osem = scr[:4]
        own = scr[4:]
        x, y, c = _place()
        loads, stores, sends = [], [], []
        for i in range(len(ins)):
            loads.append(functools.partial(pltpu.make_async_copy, self._slab(ins, i, 2 * x + y), own[i], lsem.at[i]))
            stores.append(functools.partial(pltpu.make_async_copy, own[i], outs[i].at[3], osem.at[i]))
            for j, (fx, fy) in enumerate(OTHER_CHIPS):
                px, py = _flip(x, fx), _flip(y, fy)
                sends.append(functools.partial(_remote, self._slab(ins, i, 2 * px + py), outs[i].at[j], ssem.at[i, j],
                                               rsem.at[i, j], (px, py, c)))
        return loads, stores, sends

    def start(self, ins, outs, scr):
        loads, _, sends = self._copies(ins, outs, scr)
        for make in sends + loads:
            make().start()

    lead = 2

    def relay(self, ins, outs, scr):
        loads, stores, _ = self._copies(ins, outs, scr)
        for load, store in zip(loads, stores):
            load().wait()
            store().start()

    def finish(self, ins, outs, scr):
        _, stores, sends = self._copies(ins, outs, scr)
        for make in sends + stores:
            make().wait()


SHARE_STEPS = 2


def _sum_share(slab_list, name, hook=None):
    n = len(slab_list)
    geom = [(sl.shape[1], sl.shape[1] // SHARE_STEPS, sl.shape[2]) for sl in slab_list]

    def body(*refs):
        ins, outs, scr = refs[:n], refs[n:2 * n], refs[2 * n:]
        i = pl.program_id(0)
        x, y, c = _place()
        sibling = (x, y, 1 - c)

        def copies(q, k):
            rh, tr, _ = geom[q]
            stage, lsem, ssem, rsem = scr[4 * q:4 * q + 4]
            dst = outs[q].at[pl.ds(pl.multiple_of(c * rh + k * tr, 8), tr), :]
            return (pltpu.make_async_copy(stage.at[k], dst, lsem.at[k]),
                    _remote(stage.at[k], dst, ssem.at[k], rsem, sibling))

        for q in range(n):
            acc = ins[q][3].astype(F32)
            for k in range(3):
                acc = acc + ins[q][k].astype(F32)
            scr[4 * q][i] = acc
            for cp in copies(q, i):
                cp.start()

        @pl.when(i == SHARE_STEPS - 1)
        def _():
            for q in range(n):
                rh = geom[q][0]
                for k in range(SHARE_STEPS):
                    local, remote = copies(q, k)
                    local.wait()
                    remote.wait_send()
                got = outs[q].at[pl.ds(pl.multiple_of((1 - c) * rh, 8), rh), :]
                _remote(got, got, scr[4 * q + 2].at[0], scr[4 * q + 3], sibling).wait_recv()

    dma = pltpu.SemaphoreType.DMA
    scratch = []
    for rh, tr, cols in geom:
        scratch += [pltpu.VMEM((SHARE_STEPS, tr, cols), F32), dma((SHARE_STEPS,)), dma((SHARE_STEPS,)), dma]
    return _call(
        body, name=name, grid=(SHARE_STEPS,), out_shape=tuple(S((2 * rh, cols), F32) for rh, _, cols in geom),
        in_specs=[pl.BlockSpec((4, tr, cols), lambda i: (0, i, 0)) for _, tr, cols in geom], out_specs=(ANY,) * n,
        scratch_shapes=scratch, sem=("arbitrary",), args=list(slab_list), hook=hook)


class _SmallSumHook:
    def __init__(self, buf):
        self.ins, self.out_shape = [buf], (S(buf.shape, F32),)
        dma = pltpu.SemaphoreType.DMA
        self.scratch = [pltpu.VMEM((8,) + buf.shape, F32), pltpu.VMEM(buf.shape, F32), dma((7,)), dma((7,)), dma]

    def _sends(self, scr):
        slots, _, ssem, rsem, _ = scr
        x, y, c = _place()
        me = 4 * x + 2 * y + c
        for r in range(1, 8):
            px, py, pc = _flip(x, (r >> 2) & 1), _flip(y, (r >> 1) & 1), _flip(c, r & 1)
            yield (functools.partial(_remote, slots.at[me], slots.at[me], ssem.at[r - 1], rsem.at[r - 1], (px, py, pc)),
                   functools.partial(_remote, slots.at[me], slots.at[4 * px + 2 * py + pc], ssem.at[r - 1],
                                     rsem.at[r - 1], (px, py, pc)))

    def start(self, ins, outs, scr):
        slots, _, _, _, lsem = scr
        x, y, c = _place()
        load = pltpu.make_async_copy(ins[0], slots.at[4 * x + 2 * y + c], lsem)
        load.start()
        load.wait()
        for send, _ in self._sends(scr):
            send().start()

    lead = 0

    def relay(self, ins, outs, scr):
        pass

    def finish(self, ins, outs, scr):
        slots, total, _, _, lsem = scr
        for _, arrival in self._sends(scr):
            arrival().wait_recv()
        for send, _ in self._sends(scr):
            send().wait_send()
        acc = slots[0]
        for k in range(1, 8):
            acc = acc + slots[k]
        total[...] = acc
        store = pltpu.make_async_copy(total, outs[0], lsem)
        store.start()
        store.wait()


BIG = ("ffn1_w_gu", "ffn1_w_down", "w_in", "w_out", "xattn_wq", "xattn_wkv", "xattn_wo", "ffn2_w_gu", "ffn2_w_down")
KIND = {"ffn1_w_gu": "col", "ffn1_w_down": "row", "w_in": "col", "w_out": "row", "xattn_wq": "row",
        "xattn_wkv": "col", "xattn_wo": "row", "ffn2_w_gu": "col", "ffn2_w_down": "row"}
WEIGHTS = ("rel_bias", "ffn1_norm", "ffn1_w_gu", "ffn1_w_down", "mix_norm", "w_in", "sinks", "conv_w", "w_out",
           "xattn_norm", "mem_norm", "xattn_wq", "xattn_wkv", "xattn_wo", "ffn2_norm", "ffn2_w_gu", "ffn2_w_down",
           "final_norm")
SMALL_ROWS = 16
GAIN_ROW = {"ffn1_norm": 0, "mix_norm": 1, "xattn_norm": 2, "mem_norm": 3, "ffn2_norm": 4, "final_norm": 5}
CONV_ROW, SINK_ROW, BIAS_ROW, LOSS_ROW = 6, 9, 10, 11
TAP_ROWS = 8


def _rows_block(rows, d):
    buf = jnp.zeros((SMALL_ROWS, d), F32)
    for r, v in rows.items():
        buf = lax.dynamic_update_slice(buf, v if v.ndim == 2 else v.reshape(1, -1), (r, 0))
    return buf


def _adamw_small(gsum, conv_g, bias_g, small):
    names = list(small)

    def grad(k, gsum_ref, conv_ref, bias_ref):
        if k in GAIN_ROW:
            return gsum_ref[GAIN_ROW[k]:GAIN_ROW[k] + 1, :]
        if k == "sinks":
            return gsum_ref[SINK_ROW:SINK_ROW + 1, 0:small[k][0].shape[1]]
        return conv_ref[...] if k == "conv_w" else bias_ref[...]

    def body(gsum_ref, conv_ref, bias_ref, *refs):
        ins, outs = refs[:3 * len(names)], refs[3 * len(names):]
        for q, k in enumerate(names):
            w_ref, m_ref, v_ref = ins[3 * q:3 * q + 3]
            g_ref, d_ref, mo_ref, vo_ref = outs[4 * q:4 * q + 4]
            gv = grad(k, gsum_ref, conv_ref, bias_ref)
            g_ref[...] = gv
            d_ref[...], mo_ref[...], vo_ref[...] = _adam_update(w_ref[...], gv, m_ref[...], v_ref[...])

    res = pl.pallas_call(
        body, name="adamw_small", out_shape=tuple(S(small[k][0].shape, F32) for k in names for _ in range(4)),
        compiler_params=_params())(gsum, conv_g, bias_g, *[a for k in names for a in small[k]])
    return {k: res[4 * q:4 * q + 4] for q, k in enumerate(names)}


def _local_step(x, mem, pos, target, w, gains, bias_table, sinks, conv_w, shards):
    t, d = x.shape
    w = dict(w)
    grads, slabs = {}, {}
    pos_col = pos.reshape(t, 1)
    pos_row = pos.reshape(1, t)
    bias_t = bias_table.reshape(SWA_KV_HEADS, GROUP_ROWS, 2 * BLOCK)

    def gather(names, lead=2):
        return _GatherHook([shards[k] for k in names], [KIND[k] for k in names], lead)

    def gathered(names, hook):
        for k, gw in zip(names, hook.results):
            w[k] = gw if KIND[k] == "col" else gw.reshape(-1, gw.shape[-1])

    def dw(problems, tn, name, hook=None):
        res = _dw_pair([(a, b) for _, a, b in problems], tn, name, KIND[problems[0][0]], hook)
        grads.update(zip((k for k, _, _ in problems), res))

    def pair_sums(names):
        flat = lambda k, v: v.reshape(-1, v.shape[-1]) if KIND[k] == "row" else v
        sums = _add_bf16([(flat(k, grads[k][0]), flat(k, grads[k][1])) for k in names], "pair_sum_" + names[0])
        return {k: (p if KIND[k] == "col" else p.reshape(4, -1, p.shape[-1])) for k, p in zip(names, sums)}

    def chips(names, parts):
        return _ChipsHook([parts[k] for k in names], [KIND[k] for k in names])

    def reduced(names, hook):
        slabs.update(zip(names, hook.results))

    names = ("w_in", "w_out")
    hook = gather(names, lead=1)
    x1, g1, u1 = _ffn_fwd(x, gains["ffn1_norm"], w["ffn1_w_gu"], w["ffn1_w_down"], "ffn1_fwd", hook)
    gathered(names, hook)
    names = ("xattn_wq", "xattn_wkv", "xattn_wo")
    hook = gather(names)
    qkv, e = _mix_proj(x1, gains["mix_norm"], w["w_in"], hook)
    gathered(names, hook)
    names = ("ffn2_w_gu",)
    hook = gather(names)
    attn = _swa_fwd(qkv, pos_col, pos_row, bias_t, sinks, hook)
    gathered(names, hook)
    names = ("ffn2_w_down",)
    hook = gather(names)
    x2, merged = _mix_out_fwd(e, attn, conv_w, w["w_out"], x1, hook)
    gathered(names, hook)
    mh, kv = _mem_kv(mem, gains["mem_norm"], w["xattn_wkv"])
    x3, qx, o = _xattn_fwd(x2, gains["xattn_norm"], w["xattn_wq"], kv, w["xattn_wo"])
    dx4, g2, u2, d_final, loss = _ffn_fwd(x3, gains["ffn2_norm"], w["ffn2_w_gu"], w["ffn2_w_down"], "ffn2_fwd",
                                          head=(gains["final_norm"], target))

    dx3, d_ffn2, dgu2, a2, h4, dyh4 = _ffn_bwd(dx4, x3, gains["ffn2_norm"], g2, u2, w["ffn2_w_gu"], w["ffn2_w_down"],
                                               "ffn2_bwd")
    dw([("ffn2_w_gu", h4, dgu2)], 1408, "dw_ffn2_gu")
    dw([("ffn2_w_down", a2, dyh4)], 512, "dw_ffn2_down")
    parts = pair_sums(("ffn2_w_gu", "ffn2_w_down"))
    hook = chips(("ffn2_w_gu",), parts)
    dx2, d_xattn, dxh3, h3, dqx, grads["xattn_wkv"], d_mem = _xattn_bwd(
        dx3, x2, gains["xattn_norm"], qx, kv, w["xattn_wq"], w["xattn_wo"], mh, mem, w["xattn_wkv"], hook)
    reduced(("ffn2_w_gu",), hook)
    hook = chips(("ffn2_w_down",), parts)
    dattn, dp, dcw, dxh2 = _mix_out_bwd(dx2, e, attn, conv_w, w["w_out"], qkv.shape[1], hook)
    reduced(("ffn2_w_down",), hook)
    dw([("xattn_wo", o, dxh3), ("xattn_wq", h3, dqx), ("w_out", merged, dxh2)], 1024, "dw_wo_wq_wout")
    names = ("xattn_wo", "xattn_wq", "xattn_wkv", "w_out")
    hook = chips(names, pair_sums(names))
    dp, gb, dsk = _swa_bwd(qkv, dattn, pos_col, pos_row, bias_t, sinks, dp, hook)
    reduced(names, hook)
    d_rel_bias, d_sinks = _bias_reduce(gb.reshape(SWA_HEADS, BLOCK, 2 * BLOCK), dsk.reshape(SWA_HEADS, BLOCK, 1))
    dx1, d_mix, h2 = _mix_in_bwd(dp, x1, gains["mix_norm"], w["w_in"], dx2)
    dw([("w_in", h2, dp)], w["w_in"].shape[1] // 4, "dw_win")
    names = ("w_in",)
    hook = chips(names, pair_sums(names))
    dx0, d_ffn1, dgu1, a1, h1, dyh1 = _ffn_bwd(dx1, x, gains["ffn1_norm"], g1, u1, w["ffn1_w_gu"], w["ffn1_w_down"],
                                               "ffn1_bwd")
    dw([("ffn1_w_gu", h1, dgu1)], 1408, "dw_ffn1_gu", hook)
    reduced(names, hook)
    names = ("ffn1_w_gu",)
    hook = chips(names, pair_sums(names))
    dw([("ffn1_w_down", a1, dyh1)], 512, "dw_ffn1_down", hook)
    reduced(names, hook)
    rows = {0: d_ffn1, 1: d_mix, 2: d_xattn, 3: d_mem, 4: d_ffn2, 5: d_final, SINK_ROW: d_sinks,
            BIAS_ROW: d_rel_bias.reshape(1, -1), LOSS_ROW: loss[0, 0:1]}
    rows.update({CONV_ROW + j: dcw[j] for j in range(3)})
    last = ("ffn1_w_down",)
    return dx0, slabs, _rows_block(rows, d), chips(last, pair_sums(last))


def kernel(x, mem, positions, rel_bias, ffn1_norm, ffn1_w_gu, ffn1_w_down, mix_norm, w_in, sinks, conv_w, w_out, xattn_norm, mem_norm, xattn_wq, xattn_wkv, xattn_wo, ffn2_norm, ffn2_w_gu, ffn2_w_down, final_norm, loss_target, m_rel_bias, m_ffn1_norm, m_ffn1_w_gu, m_ffn1_w_down, m_mix_norm, m_w_in, m_sinks, m_conv_w, m_w_out, m_xattn_norm, m_mem_norm, m_xattn_wq, m_xattn_wkv, m_xattn_wo, m_ffn2_norm, m_ffn2_w_gu, m_ffn2_w_down, m_final_norm, v_rel_bias, v_ffn1_norm, v_ffn1_w_gu, v_ffn1_w_down, v_mix_norm, v_w_in, v_sinks, v_conv_w, v_w_out, v_xattn_norm, v_mem_norm, v_xattn_wq, v_xattn_wkv, v_xattn_wo, v_ffn2_norm, v_ffn2_w_gu, v_ffn2_w_down, v_final_norm):
    args = dict(locals())
    wts = {k: args[k] for k in WEIGHTS}
    mom = {k: args["m_" + k] for k in WEIGHTS}
    var = {k: args["v_" + k] for k in WEIGHTS}
    d = x.shape[-1]
    s_me = 2 * lax.axis_index("x") + lax.axis_index("y")

    first = ("ffn1_w_gu", "ffn1_w_down")
    rest = tuple(k for k in BIG if k not in first)
    shards = {k: wts[k][0].astype(BF) for k in first}
    cw_cols = conv_w.shape[-1]
    placed = lax.dynamic_update_slice(jnp.zeros((TAP_ROWS, d), F32), 0.5 * conv_w[0], (0, s_me * cw_cols))
    head = _Both([_GatherHook([shards[k] for k in first], [KIND[k] for k in first]), _SmallSumHook(placed)])
    casts, bias_table = _to_bf16([wts[k][0] for k in rest], rel_bias.T, "gather_ffn1", head)
    shards.update(zip(rest, casts))
    gathered, (conv_sum,) = head.results
    whole = {k: (gw if KIND[k] == "col" else gw.reshape(-1, gw.shape[-1])) for k, gw in zip(first, gathered)}
    conv_whole = conv_sum[0:3]

    gains = {k: wts[k].reshape(1, d) for k in GAIN_ROW}
    dx0, slabs, small, last_chips = _local_step(x[0], mem[0], positions[0], loss_target[0], whole, gains, bias_table,
                                               sinks, conv_whole, shards)

    late = ("ffn1_w_gu", "ffn1_w_down", "ffn2_w_down")
    early = tuple(k for k in BIG if k not in late)
    first_ones = tuple(k for k in BIG if k != "ffn1_w_down")
    tail = _Both([last_chips, _SmallSumHook(small)])
    shard_grads = dict(zip(first_ones, _sum_share([slabs[k] for k in first_ones], "sum_share", tail)))
    (last_slabs,), (small_sum,) = tail.results
    shard_grads["ffn1_w_down"], = _sum_share([last_slabs], "sum_share_ffn1_w_down")
    quad = lambda k: (wts[k][0], shard_grads[k], mom[k][0], var[k][0])
    updates = dict(zip(early, _adamw([quad(k) for k in early], "adamw_early", 16, echo=True)))
    updates.update(zip(late, _adamw([quad(k) for k in late], "adamw_late", 8, echo=True)))
    loss = small_sum[LOSS_ROW, 0]

    out_g, out_d, out_m, out_v = {}, {}, {}, {}
    for k in BIG:
        out_g[k], out_d[k], out_m[k], out_v[k] = (a[None] for a in updates[k])

    to_call = lambda k, a: a.T if k == "rel_bias" else a.reshape(1, -1)
    from_call = lambda k, a: a.T if k == "rel_bias" else a.reshape(wts[k].shape)
    small_names = [k for k in WEIGHTS if k not in KIND]
    conv_g = lax.dynamic_slice(small_sum, (CONV_ROW, s_me * cw_cols), (3, cw_cols)).reshape(1, -1)
    bias_g = small_sum[BIAS_ROW, 0:rel_bias.size].reshape(rel_bias.shape[::-1])
    done = _adamw_small(small_sum, conv_g, bias_g,
                        {k: tuple(to_call(k, a) for a in (wts[k], mom[k], var[k])) for k in small_names})
    for k in small_names:
        out_g[k], out_d[k], out_m[k], out_v[k] = (from_call(k, a) for a in done[k])

    return (loss, dx0[None], *[out_g[k] for k in WEIGHTS], *[out_d[k] for k in WEIGHTS],
            *[out_m[k] for k in WEIGHTS], *[out_v[k] for k in WEIGHTS])
```

```python
import functools
import math

import jax
import jax.numpy as jnp
from jax import lax
from jax.experimental import pallas as pl
from jax.experimental.pallas import tpu as pltpu

BF = jnp.bfloat16
F32 = jnp.float32
I32 = jnp.int32
S = jax.ShapeDtypeStruct

EPS = 1e-6
NEG = -1e30
POS_PAD = 1 << 30
WINDOW = 128
BLOCK = 128
HEAD_DIM = 64
SWA_HEADS = 16
SWA_KV_HEADS = 4
SWA_GROUP = SWA_HEADS // SWA_KV_HEADS
MEM_HEADS = 4
REL_BUCKETS = 32
REL_MAX_DIST = 128
ADAM_LR = 0.001
ADAM_B1 = 0.9
ADAM_B2 = 0.999
ADAM_EPS = 1e-08
ADAM_WD = 0.01
ADAM_STEP = 10

V7X_VMEM_LIMIT_BYTES = 56 * 1024 * 1024
MESH = pl.DeviceIdType.MESH
ANY = pl.BlockSpec(memory_space=pl.ANY)
VMEM_SPEC = pl.BlockSpec(memory_space=pltpu.VMEM)
SMEM_SPEC = pl.BlockSpec(memory_space=pltpu.SMEM)


def _params(sem=None):
    return pltpu.CompilerParams(dimension_semantics=sem, vmem_limit_bytes=V7X_VMEM_LIMIT_BYTES)


def _resident(shape):
    nd = len(shape)
    return pl.BlockSpec(shape, lambda *_: (0,) * nd, pipeline_mode=pl.Buffered(1))


def _acc_spec(shape):
    nd = len(shape)
    return pl.BlockSpec(shape, lambda *_: (0,) * nd)


def _call(body, *, name, grid, out_shape, in_specs, out_specs, args, sem, scratch_shapes=(), aliases=None, hook=None):
    aliases = aliases or {}
    if hook is None:
        return pl.pallas_call(body, name=name, grid=grid, out_shape=out_shape, in_specs=in_specs, out_specs=out_specs,
                              scratch_shapes=list(scratch_shapes), input_output_aliases=aliases,
                              compiler_params=_params(sem))(*args)
    n_in, n_out, n_scr = len(in_specs), len(out_shape), len(scratch_shapes)
    h_in, h_out = len(hook.ins), len(hook.out_shape)

    def at_step(pick):
        conds = [pl.program_id(ax) == pick(size) for ax, size in enumerate(grid)]
        return functools.reduce(jnp.logical_and, conds)

    def hosted(*refs):
        k_in, x_in = refs[:n_in], refs[n_in:n_in + h_in]
        o0 = n_in + h_in
        k_out, x_out = refs[o0:o0 + n_out], refs[o0 + n_out:o0 + n_out + h_out]
        s0 = o0 + n_out + h_out
        k_scr, x_scr = refs[s0:s0 + n_scr], refs[s0 + n_scr:]

        @pl.when(at_step(lambda size: 0))
        def _():
            hook.start(x_in, x_out, x_scr)
        body(*k_in, *k_out, *k_scr)
        early = bool(hook.lead) and grid[0] > hook.lead
        if early:
            on_axis0 = pl.program_id(0) == grid[0] - 1 - hook.lead
            rest_first = [pl.program_id(ax) == 0 for ax in range(1, len(grid))]

            @pl.when(functools.reduce(jnp.logical_and, rest_first, on_axis0))
            def _():
                hook.relay(x_in, x_out, x_scr)

        @pl.when(at_step(lambda size: size - 1))
        def _():
            if not early:
                hook.relay(x_in, x_out, x_scr)
            hook.finish(x_in, x_out, x_scr)

    res = pl.pallas_call(
        hosted, name=name, grid=grid, out_shape=tuple(out_shape) + tuple(hook.out_shape),
        in_specs=list(in_specs) + [ANY] * h_in, out_specs=tuple(out_specs) + (ANY,) * h_out,
        scratch_shapes=list(scratch_shapes) + list(hook.scratch), input_output_aliases=aliases,
        compiler_params=_params(("arbitrary",) * len(grid)),
    )(*args, *hook.ins)
    hook.results = res[n_out:]
    return res[:n_out]


def _nn(a, b):
    return jnp.dot(a, b, preferred_element_type=F32)


def _nt(a, b):
    return lax.dot_general(a, b, (((1,), (1,)), ((), ())), preferred_element_type=F32)


def _tn(a, b):
    return lax.dot_general(a, b, (((0,), (0,)), ((), ())), preferred_element_type=F32)


def _sigmoid(v):
    return 1.0 / (1.0 + jnp.exp(-v))


def _rms(x):
    r = lax.rsqrt(jnp.mean(x * x, axis=-1, keepdims=True) + EPS)
    return x * r, r


def _rms_bwd(dh, n, r, g):
    dn = dh * g
    dx = r * (dn - n * jnp.mean(dn * n, axis=-1, keepdims=True))
    return dx, jnp.sum(dh * n, axis=0, keepdims=True)


def _ffn_fwd(x, gn, wgu, wd, name, hook=None, head=None):
    t, d = x.shape
    f = wd.shape[0]
    tm, fc = 256, 1408

    def body(x_ref, gn_ref, wgu_ref, wd_ref, *rest):
        xv = x_ref[...]
        n, _ = _rms(xv)
        h = (n * gn_ref[...]).astype(BF)
        g_ref, u_ref = rest[-2:] if head is None else rest[3:5]
        acc = jnp.zeros((tm, d), F32)
        for c0 in range(0, f, fc):
            g = _nn(h, wgu_ref[:, c0:c0 + fc])
            u = _nn(h, wgu_ref[:, f + c0:f + c0 + fc])
            g_ref[:, c0:c0 + fc] = g.astype(BF)
            u_ref[:, c0:c0 + fc] = u.astype(BF)
            a = (g * _sigmoid(g)) * u
            acc = acc + _nn(a.astype(BF), wd_ref[c0:c0 + fc, :])
        y = xv + 0.5 * acc
        if head is None:
            rest[0][...] = y
            return
        gf_ref, t_ref, dy_ref, _, _, dgf_ref, loss_ref = rest

        @pl.when(pl.program_id(0) == 0)
        def _():
            dgf_ref[...] = jnp.zeros_like(dgf_ref)
            loss_ref[...] = jnp.zeros_like(loss_ref)
        ny, ry = _rms(y)
        gf = gf_ref[...]
        err = ny * gf - t_ref[...]
        loss_ref[...] += 0.5 * jnp.sum(jnp.sum(err * err, axis=-1, keepdims=True) / d, axis=0, keepdims=True)
        dy, dgf = _rms_bwd(err / d, ny, ry, gf)
        dy_ref[...] = dy
        dgf_ref[...] += dgf

    row = pl.BlockSpec((tm, d), lambda i: (i, 0))
    frow = pl.BlockSpec((tm, f), lambda i: (i, 0))
    in_specs = [row, _resident((1, d)), _resident(wgu.shape), _resident(wd.shape)]
    out_shape = (S((t, d), F32), S((t, f), BF), S((t, f), BF))
    out_specs = (row, frow, frow)
    args = (x, gn, wgu, wd)
    if head is not None:
        in_specs += [_resident((1, d)), row]
        out_shape += (S((1, d), F32), S((1, 128), F32))
        out_specs += (_acc_spec((1, d)), _acc_spec((1, 128)))
        args += tuple(head)
    return _call(body, name=name, grid=(t // tm,), out_shape=out_shape, in_specs=in_specs, out_specs=out_specs,
                 sem=("parallel",) if head is None else ("arbitrary",), args=args, hook=hook)


def _mix_proj(x, gn, w_in, hook=None):
    t, d = x.shape
    tm = 256
    nqkv = 1536
    ne = w_in.shape[1] - nqkv

    def body(x_ref, gn_ref, w_ref, qkv_ref, e_ref):
        n, _ = _rms(x_ref[...])
        h = (n * gn_ref[...]).astype(BF)
        qkv_ref[...] = _nn(h, w_ref[:, 0:nqkv]).astype(BF)
        for c0 in range(0, ne, 1024):
            e_ref[:, c0:c0 + 1024] = _nn(h, w_ref[:, nqkv + c0:nqkv + c0 + 1024]).astype(BF)

    return _call(
        body, name="mix_proj", grid=(t // tm,),
        out_shape=(S((t, nqkv), BF), S((t, ne), BF)),
        in_specs=[pl.BlockSpec((tm, d), lambda i: (i, 0)), _resident((1, d)), _resident(w_in.shape)],
        out_specs=(pl.BlockSpec((tm, nqkv), lambda i: (i, 0)), pl.BlockSpec((tm, ne), lambda i: (i, 0))),
        sem=("parallel",), args=(x, gn, w_in), hook=hook)


def _t5_bucket(rel):
    n = jnp.maximum(rel, 0)
    max_exact = REL_BUCKETS // 2
    nf = jnp.maximum(n, 1).astype(F32)
    large = max_exact + (jnp.log(nf / max_exact) / math.log(REL_MAX_DIST / max_exact)
                         * (REL_BUCKETS - max_exact)).astype(I32)
    large = jnp.minimum(large, REL_BUCKETS - 1)
    return jnp.where(n < max_exact, n, large)


def _block_rel():
    i = lax.broadcasted_iota(I32, (BLOCK, 2 * BLOCK), 0)
    j = lax.broadcasted_iota(I32, (BLOCK, 2 * BLOCK), 1)
    return i + BLOCK - j


def _bias_table(rb_ref, o_ref):
    bucket = _t5_bucket(_block_rel())
    for h in range(SWA_HEADS):
        acc = jnp.zeros((BLOCK, 2 * BLOCK), F32)
        for b in range(REL_BUCKETS):
            acc = jnp.where(bucket == b, rb_ref[h, b], acc)
        o_ref[h] = acc


GROUP_ROWS = SWA_GROUP * BLOCK


def _swa_visible(b, pq_ref, pkp_ref, pkc_ref):
    pk = jnp.concatenate([pkp_ref[...], pkc_ref[...]], axis=1)
    col = lax.broadcasted_iota(I32, (1, 2 * BLOCK), 1)
    pk = jnp.where(jnp.logical_and(b == 0, col < BLOCK), POS_PAD, pk)
    rel = jnp.concatenate([pq_ref[...]] * SWA_GROUP, axis=0) - pk
    return jnp.logical_and(rel >= 0, rel < WINDOW)


def _swa_poison(b, pq_ref, pkp_ref, pkc_ref):
    pk = jnp.concatenate([pkp_ref[...], pkc_ref[...]], axis=1)
    col = lax.broadcasted_iota(I32, (1, 2 * BLOCK), 1)
    pk = jnp.where(jnp.logical_and(b == 0, col < BLOCK), POS_PAD, pk)
    rel = pq_ref[...] - pk
    off = jnp.logical_and(jnp.logical_and(rel >= 0, rel < WINDOW), rel != _block_rel())
    return jnp.where(jnp.max(off.astype(F32)) > 0.0, jnp.nan, 0.0)


def _group_heads(ref, hk):
    h0 = hk * SWA_GROUP
    return jnp.concatenate([ref[:, (h0 + g) * HEAD_DIM:(h0 + g + 1) * HEAD_DIM] for g in range(SWA_GROUP)], axis=0)


def _group_sinks(sink_ref, hk):
    row = lax.broadcasted_iota(I32, (GROUP_ROWS, 1), 0)
    col = jnp.zeros((GROUP_ROWS, 1), F32) + sink_ref[0, hk * SWA_GROUP]
    for g in range(1, SWA_GROUP):
        col = jnp.where(row >= g * BLOCK, sink_ref[0, hk * SWA_GROUP + g], col)
    return col


def _swa_probs(qg, kh, vis, bias, sink):
    s = _nt(qg, kh) * (HEAD_DIM ** -0.5)
    s = jnp.where(vis, s + bias, NEG)
    m = jnp.maximum(jnp.max(s, axis=-1, keepdims=True), sink)
    p = jnp.exp(s - m)
    ps = jnp.exp(sink - m)
    inv = 1.0 / (jnp.sum(p, axis=-1, keepdims=True) + ps)
    return p * inv, ps * inv


def _swa_fwd(qkv, pos_col, pos_row, bias_t, sinks, hook=None):
    t = qkv.shape[0]
    nb = t // BLOCK
    qw = SWA_HEADS * HEAD_DIM
    kw = SWA_KV_HEADS * HEAD_DIM

    def body(q_ref, kp_ref, kc_ref, vp_ref, vc_ref, pq_ref, pkp_ref, pkc_ref, bias_ref, sink_ref, o_ref):
        b = pl.program_id(0)
        vis = _swa_visible(b, pq_ref, pkp_ref, pkc_ref)
        poison = _swa_poison(b, pq_ref, pkp_ref, pkc_ref)
        k2 = jnp.concatenate([kp_ref[...], kc_ref[...]], axis=0)
        v2 = jnp.concatenate([vp_ref[...], vc_ref[...]], axis=0)
        for hk in range(SWA_KV_HEADS):
            kh = k2[:, hk * HEAD_DIM:(hk + 1) * HEAD_DIM]
            vh = v2[:, hk * HEAD_DIM:(hk + 1) * HEAD_DIM]
            pn, _ = _swa_probs(_group_heads(q_ref, hk), kh, vis, bias_ref[hk], _group_sinks(sink_ref, hk))
            o = _nn(pn.astype(BF), vh) + poison
            for g in range(SWA_GROUP):
                h = hk * SWA_GROUP + g
                o_ref[:, h * HEAD_DIM:(h + 1) * HEAD_DIM] = o[g * BLOCK:(g + 1) * BLOCK]

    prev = lambda b: jnp.maximum(b - 1, 0)
    return _call(
        body, name="swa_fwd", grid=(nb,), out_shape=(S((t, qw), F32),),
        in_specs=[
            pl.BlockSpec((BLOCK, qw), lambda b: (b, 0)),
            pl.BlockSpec((BLOCK, kw), lambda b: (prev(b), qw // kw)),
            pl.BlockSpec((BLOCK, kw), lambda b: (b, qw // kw)),
            pl.BlockSpec((BLOCK, kw), lambda b: (prev(b), qw // kw + 1)),
            pl.BlockSpec((BLOCK, kw), lambda b: (b, qw // kw + 1)),
            pl.BlockSpec((BLOCK, 1), lambda b: (b, 0)),
            pl.BlockSpec((1, BLOCK), lambda b: (0, prev(b))),
            pl.BlockSpec((1, BLOCK), lambda b: (0, b)),
            _resident(bias_t.shape),
            SMEM_SPEC,
        ],
        out_specs=(pl.BlockSpec((BLOCK, qw), lambda b: (b, 0)),),
        sem=("parallel",), args=(qkv, qkv, qkv, qkv, qkv, pos_col, pos_row, pos_row, bias_t, sinks), hook=hook)[0]


HALO = 16


def _conv_taps(z, zh, first):
    tm = z.shape[0]
    zh = jnp.where(first, 0.0, zh)
    row = lax.broadcasted_iota(I32, (tm, 1), 0)
    z1 = jnp.where(row == 0, zh[HALO - 1:HALO, :], pltpu.roll(z, 1, 0))
    z2 = jnp.where(row == 0, zh[HALO - 2:HALO - 1, :], jnp.where(row == 1, zh[HALO - 1:HALO, :], pltpu.roll(z, 2, 0)))
    return z1, z2


def _mix_out_fwd(e, attn, conv_w, w_out, x, hook=None):
    t, d = x.shape
    tm = 256
    hb = tm // HALO
    f32 = lambda ref: ref[...].astype(F32)

    def body(c_ref, b_ref, u_ref, ga_ref, gc_ref, ch_ref, uh_ref, attn_ref, cw_ref, w_ref, x_ref, xo_ref, mg_ref):
        i = pl.program_id(0)
        z = f32(c_ref) * f32(u_ref)
        z1, z2 = _conv_taps(z, f32(ch_ref) * f32(uh_ref), i == 0)
        s = cw_ref[0:1, :] * z2 + cw_ref[1:2, :] * z1 + cw_ref[2:3, :] * z
        conv = f32(b_ref) * s
        merged = (_sigmoid(f32(ga_ref)) * attn_ref[...] + _sigmoid(f32(gc_ref)) * conv).astype(BF)
        mg_ref[...] = merged
        xo_ref[...] = x_ref[...] + _nn(merged, w_ref[...])

    ecol = lambda cb: pl.BlockSpec((tm, d), lambda i: (i, cb))
    halo = lambda cb: pl.BlockSpec((HALO, d), lambda i: (jnp.maximum(i * hb - 1, 0), cb))
    row = pl.BlockSpec((tm, d), lambda i: (i, 0))
    return _call(
        body, name="mix_out_fwd", grid=(t // tm,),
        out_shape=(S((t, d), F32), S((t, d), BF)),
        in_specs=[ecol(0), ecol(1), ecol(2), ecol(3), ecol(4), halo(0), halo(2), row,
                  _resident(conv_w.shape), _resident(w_out.shape), row],
        out_specs=(row, row),
        sem=("parallel",), args=(e, e, e, e, e, e, e, attn, conv_w, w_out, x), hook=hook)


def _mem_kv(mem, gm, wkv):
    m, d = mem.shape

    def body(mem_ref, gm_ref, w_ref, mh_ref, kv_ref):
        n, _ = _rms(mem_ref[...])
        mh = (n * gm_ref[...]).astype(BF)
        mh_ref[...] = mh
        kv_ref[...] = _nn(mh, w_ref[...]).astype(BF)

    return pl.pallas_call(
        body, name="mem_kv", out_shape=(S((m, d), BF), S((m, wkv.shape[1]), BF)),
        compiler_params=_params(),
    )(mem, gm, wkv)


def _xattn_probs(qh, kh):
    s = _nt(qh, kh) * (kh.shape[1] ** -0.5)
    p = jnp.exp(s - jnp.max(s, axis=-1, keepdims=True))
    return p * (1.0 / jnp.sum(p, axis=-1, keepdims=True))


def _xattn_fwd(x, gn, wq, kv, wo):
    t, d = x.shape
    tm = 256
    hd = d // MEM_HEADS

    def body(x_ref, gn_ref, wq_ref, kv_ref, wo_ref, xo_ref, q_ref, o_ref):
        xv = x_ref[...]
        n, _ = _rms(xv)
        q = _nn((n * gn_ref[...]).astype(BF), wq_ref[...]).astype(BF)
        q_ref[...] = q
        outs = []
        for hh in range(MEM_HEADS):
            p = _xattn_probs(q[:, hh * hd:(hh + 1) * hd], kv_ref[:, hh * hd:(hh + 1) * hd])
            outs.append(_nn(p.astype(BF), kv_ref[:, d + hh * hd:d + (hh + 1) * hd]))
        o = jnp.concatenate(outs, axis=1).astype(BF)
        o_ref[...] = o
        xo_ref[...] = xv + _nn(o, wo_ref[...])

    row = pl.BlockSpec((tm, d), lambda i: (i, 0))
    return pl.pallas_call(
        body, name="xattn_fwd", grid=(t // tm,),
        out_shape=(S((t, d), F32), S((t, d), BF), S((t, d), BF)),
        in_specs=[row, _resident((1, d)), _resident(wq.shape), _resident(kv.shape), _resident(wo.shape)],
        out_specs=(row, row, row),
        compiler_params=_params(("parallel",)),
    )(x, gn, wq, kv, wo)


def _ffn_bwd(dxo, x, gn, g, u, wgu, wd, name):
    t, d = x.shape
    f = wd.shape[0]
    tm, fc = 256, 1408

    def body(dxo_ref, x_ref, gn_ref, g_ref, u_ref, wgu_ref, wd_ref, dx_ref, dgn_ref, dgu_ref, a_ref, h_ref, dyh_ref):
        @pl.when(pl.program_id(0) == 0)
        def _():
            dgn_ref[...] = jnp.zeros_like(dgn_ref)
        dxov = dxo_ref[...]
        dyh = (0.5 * dxov).astype(BF)
        dyh_ref[...] = dyh
        n, r = _rms(x_ref[...])
        gnv = gn_ref[...]
        h_ref[...] = (n * gnv).astype(BF)
        dh = jnp.zeros((tm, d), F32)
        for c0 in range(0, f, fc):
            gv = g_ref[:, c0:c0 + fc].astype(F32)
            uv = u_ref[:, c0:c0 + fc].astype(F32)
            da = _nt(dyh, wd_ref[c0:c0 + fc, :])
            sg = _sigmoid(gv)
            silu = gv * sg
            a_ref[:, c0:c0 + fc] = (silu * uv).astype(BF)
            dg = (da * uv * (sg * (1.0 + gv * (1.0 - sg)))).astype(BF)
            du = (da * silu).astype(BF)
            dgu_ref[:, c0:c0 + fc] = dg
            dgu_ref[:, f + c0:f + c0 + fc] = du
            dh = dh + _nt(dg, wgu_ref[:, c0:c0 + fc]) + _nt(du, wgu_ref[:, f + c0:f + c0 + fc])
        dx, dgn = _rms_bwd(dh, n, r, gnv)
        dx_ref[...] = dxov + dx
        dgn_ref[...] += dgn

    row = pl.BlockSpec((tm, d), lambda i: (i, 0))
    frow = pl.BlockSpec((tm, f), lambda i: (i, 0))
    return _call(
        body, name=name, grid=(t // tm,),
        out_shape=(S((t, d), F32), S((1, d), F32), S((t, 2 * f), BF), S((t, f), BF), S((t, d), BF), S((t, d), BF)),
        in_specs=[row, row, _resident((1, d)), frow, frow, _resident(wgu.shape), _resident(wd.shape)],
        out_specs=(row, _acc_spec((1, d)), pl.BlockSpec((tm, 2 * f), lambda i: (i, 0)), frow, row, row),
        sem=("arbitrary",), args=(dxo, x, gn, g, u, wgu, wd))


def _dw_pair(pairs, tn, name, kind, hook=None):
    npairs = len(pairs)
    t, ka = pairs[0][0].shape
    nb = pairs[0][1].shape[1]
    tt = 2048 if npairs == 1 else 1024
    nt, nj = t // tt, nb // tn
    col = kind == "col"
    rh = ka // 2 if col else ka // 8
    tile = (rh, tn) if col else (4, rh, tn)
    half = (rh, nb) if col else (4, rh, nb)
    lead = (slice(None),) * (len(tile) - 1)

    def body(*refs):
        ins, mines, sibs = refs[:2 * npairs], refs[2 * npairs:3 * npairs], refs[3 * npairs:4 * npairs]
        acc_ref, stage, ssem, rsem = refs[4 * npairs:]
        j, k = pl.program_id(0), pl.program_id(1)
        x, y, c = _place()
        sibling = (x, y, 1 - c)

        def send(p, slot, jj):
            dst = sibs[p].at[lead + (pl.ds(pl.multiple_of(jj * tn, 128), tn),)]
            return _remote(stage.at[slot], dst, ssem.at[slot], rsem.at[p], sibling)

        def rows(s, whose):
            return acc_ref[pl.ds(pl.multiple_of(s * 2 * rh + whose * rh, 16), rh), :].astype(BF)

        def step(p):
            a_ref, b_ref, mine_ref = ins[2 * p], ins[2 * p + 1], mines[p]

            @pl.when(k == 0)
            def _():
                acc_ref[...] = jnp.zeros_like(acc_ref)
            acc_ref[...] += _tn(a_ref[...], b_ref[...])

            @pl.when(k == nt - 1)
            def _():
                slot = j % 2

                @pl.when(j >= 2)
                def _():
                    send(p, slot, 0).wait_send()
                if col:
                    mine_ref[...] = rows(0, c)
                    stage[slot] = rows(0, 1 - c)
                else:
                    for s in range(4):
                        mine_ref[s] = rows(s, c)
                        stage[slot, s] = rows(s, 1 - c)
                send(p, slot, j - p * nj).start()

        for p in range(npairs):
            pl.when(j // nj == p)(functools.partial(step, p))

        @pl.when(jnp.logical_and(j == npairs * nj - 1, k == nt - 1))
        def _():
            for jj in range(max(npairs * nj - 2, 0), npairs * nj):
                send(0, jj % 2, 0).wait_send()
            for p in range(npairs):
                _remote(sibs[p], sibs[p], ssem.at[0], rsem.at[p], sibling).wait_recv()

    in_specs, args = [], []
    for p, (a, b) in enumerate(pairs):
        on = lambda j, p=p: j // nj == p
        in_specs += [pl.BlockSpec((tt, ka), lambda j, k, on=on: (jnp.where(on(j), k, 0), 0)),
                     pl.BlockSpec((tt, tn), lambda j, k, on=on, p=p: (jnp.where(on(j), k, 0), jnp.clip(j - p * nj, 0, nj - 1)))]
        args += [a, b]
    mine_spec = lambda p: pl.BlockSpec(tile, (lambda j, k: (0, jnp.clip(j - p * nj, 0, nj - 1))) if col
                                       else (lambda j, k: (0, 0, jnp.clip(j - p * nj, 0, nj - 1))))
    res = _call(
        body, name=name, grid=(npairs * nj, nt), out_shape=(S(half, BF),) * (2 * npairs),
        in_specs=in_specs, out_specs=tuple(mine_spec(p) for p in range(npairs)) + (ANY,) * npairs,
        scratch_shapes=[pltpu.VMEM((ka, tn), F32), pltpu.VMEM((2,) + tile, BF), pltpu.SemaphoreType.DMA((2,)),
                        pltpu.SemaphoreType.DMA((npairs,))],
        sem=("arbitrary", "arbitrary"), args=args, hook=hook)
    return [(res[p], res[npairs + p]) for p in range(npairs)]


def _xattn_bwd(dxo, x, gn, q, kv, wq, wo, mh, mem, wkv, hook=None):
    t, d = x.shape
    tm = 256
    hd = d // MEM_HEADS
    nkv = kv.shape[0]
    rows, cols = wkv.shape
    rh = rows // 2

    def body(dxo_ref, x_ref, gn_ref, q_ref, kv_ref, wq_ref, wo_ref, mh_ref, mem_ref, wkv_ref,
             dx_ref, dgn_ref, dxh_ref, h_ref, dq_ref, mine_ref, sib_ref, dgm_ref, dkv_ref, whole, ssem, rsem):
        @pl.when(pl.program_id(0) == 0)
        def _():
            dgn_ref[...] = jnp.zeros_like(dgn_ref)
            dkv_ref[...] = jnp.zeros_like(dkv_ref)
        dxov = dxo_ref[...]
        dxh = dxov.astype(BF)
        dxh_ref[...] = dxh
        do = _nt(dxh, wo_ref[...]).astype(BF)
        dqs = []
        for hh in range(MEM_HEADS):
            lo, hi = hh * hd, (hh + 1) * hd
            qh = q_ref[:, lo:hi]
            kh = kv_ref[:, lo:hi]
            vh = kv_ref[:, d + lo:d + hi]
            doh = do[:, lo:hi]
            p = _xattn_probs(qh, kh)
            dp = _nt(doh, vh)
            ds = (p * (dp - jnp.sum(p * dp, axis=-1, keepdims=True)) * (hd ** -0.5)).astype(BF)
            dqs.append(_nn(ds, kh))
            dkv_ref[:, lo:hi] += _tn(ds, qh)
            dkv_ref[:, d + lo:d + hi] += _tn(p.astype(BF), doh)
        dq = jnp.concatenate(dqs, axis=1).astype(BF)
        dq_ref[...] = dq
        n, r = _rms(x_ref[...])
        gnv = gn_ref[...]
        h_ref[...] = (n * gnv).astype(BF)
        dx, dgn = _rms_bwd(_nt(dq, wq_ref[...]), n, r, gnv)
        dx_ref[...] = dxov + dx
        dgn_ref[...] += dgn

        @pl.when(pl.program_id(0) == pl.num_programs(0) - 1)
        def _():
            dkvb = dkv_ref[...].astype(BF)
            dmh = _nt(dkvb, wkv_ref[...])
            nm, _ = _rms(mem_ref[...])
            px, py, c = _place()
            whole[...] = _tn(mh_ref[...], dkvb).astype(BF)
            cp = _remote(whole.at[pl.ds(pl.multiple_of((1 - c) * rh, 16), rh), :], sib_ref, ssem, rsem, (px, py, 1 - c))
            cp.start()
            mine_ref[...] = whole[pl.ds(pl.multiple_of(c * rh, 16), rh), :]
            dgm_ref[...] = jnp.sum(dmh * nm, axis=0, keepdims=True)
            cp.wait()

    row = pl.BlockSpec((tm, d), lambda i: (i, 0))
    dx, dgn, dxh, h, dq, mine, sib, dgm = _call(
        body, name="xattn_bwd", grid=(t // tm,),
        out_shape=(S((t, d), F32), S((1, d), F32), S((t, d), BF), S((t, d), BF), S((t, d), BF),
                   S((rh, cols), BF), S((rh, cols), BF), S((1, d), F32)),
        in_specs=[row, row, _resident((1, d)), row, _resident(kv.shape), _resident(wq.shape), _resident(wo.shape),
                  _resident(mh.shape), _resident(mem.shape), _resident(wkv.shape)],
        out_specs=(row, _acc_spec((1, d)), row, row, row, _acc_spec((rh, cols)), ANY, _acc_spec((1, d))),
        scratch_shapes=[pltpu.VMEM((nkv, 2 * d), F32), pltpu.VMEM((rows, cols), BF),
                        pltpu.SemaphoreType.DMA, pltpu.SemaphoreType.DMA],
        sem=("arbitrary",), args=(dxo, x, gn, q, kv, wq, wo, mh, mem, wkv), hook=hook)
    return dx, dgn, dxh, h, dq, (mine, sib), dgm


def _mix_out_bwd(dxo, e, attn, conv_w, w_out, nqkv, hook=None):
    t, d = attn.shape
    tm = 256
    nt = t // tm
    f32 = lambda ref: ref[...].astype(F32)

    def body(dxo_ref, dxn_ref, c_ref, b_ref, u_ref, ga_ref, gc_ref, ch_ref, uh_ref, bn_ref, gcn_ref,
             attn_ref, cw_ref, w_ref, dattn_ref, de_ref, dcw_ref, dxh_ref):
        i = pl.program_id(0)

        @pl.when(i == 0)
        def _():
            dcw_ref[...] = jnp.zeros_like(dcw_ref)
        dxh = dxo_ref[...].astype(BF)
        dxh_ref[...] = dxh
        w = w_ref[...]
        dm = _nt(dxh, w)
        dmn = _nt(dxn_ref[...].astype(BF), w)
        cv, bv, uv = f32(c_ref), f32(b_ref), f32(u_ref)
        sga = _sigmoid(f32(ga_ref))
        sgc = _sigmoid(f32(gc_ref))
        z = cv * uv
        z1, z2 = _conv_taps(z, f32(ch_ref) * f32(uh_ref), i == 0)
        w0, w1, w2 = cw_ref[0:1, :], cw_ref[1:2, :], cw_ref[2:3, :]
        s = w0 * z2 + w1 * z1 + w2 * z
        av = attn_ref[...]
        dattn_ref[...] = (dm * sga).astype(BF)
        dconv = dm * sgc
        ds = dconv * bv
        dsn = jnp.where(i == nt - 1, 0.0, dmn * _sigmoid(gcn_ref[0:8, :].astype(F32)) * bn_ref[0:8, :].astype(F32))
        row = lax.broadcasted_iota(I32, (tm, 1), 0)
        dsp1 = jnp.where(row == tm - 1, dsn[0:1, :], pltpu.roll(ds, tm - 1, 0))
        dsp2 = jnp.where(row == tm - 2, dsn[0:1, :], jnp.where(row == tm - 1, dsn[1:2, :], pltpu.roll(ds, tm - 2, 0)))
        dz = w2 * ds + w1 * dsp1 + w0 * dsp2
        de_ref[:, nqkv:nqkv + d] = (dz * uv).astype(BF)
        de_ref[:, nqkv + d:nqkv + 2 * d] = (dconv * s).astype(BF)
        de_ref[:, nqkv + 2 * d:nqkv + 3 * d] = (dz * cv).astype(BF)
        de_ref[:, nqkv + 3 * d:nqkv + 4 * d] = (dm * av * sga * (1.0 - sga)).astype(BF)
        de_ref[:, nqkv + 4 * d:nqkv + 5 * d] = (dm * (bv * s) * sgc * (1.0 - sgc)).astype(BF)
        dcw_ref[0:1, :] += jnp.sum(ds * z2, axis=0, keepdims=True)
        dcw_ref[1:2, :] += jnp.sum(ds * z1, axis=0, keepdims=True)
        dcw_ref[2:3, :] += jnp.sum(ds * z, axis=0, keepdims=True)

    ecol = lambda cb: pl.BlockSpec((tm, d), lambda i: (i, cb))
    prev = lambda cb: pl.BlockSpec((HALO, d), lambda i: (jnp.maximum(i * (tm // HALO) - 1, 0), cb))
    nxt = lambda rows, cb: pl.BlockSpec((rows, d), lambda i: (jnp.minimum((i + 1) * (tm // rows), t // rows - 1), cb))
    row = pl.BlockSpec((tm, d), lambda i: (i, 0))
    return _call(
        body, name="mix_out_bwd", grid=(nt,),
        out_shape=(S((t, d), BF), S((t, nqkv + 5 * d), BF), S((8, d), F32), S((t, d), BF)),
        in_specs=[row, nxt(8, 0), ecol(0), ecol(1), ecol(2), ecol(3), ecol(4), prev(0), prev(2), nxt(HALO, 1), nxt(HALO, 4),
                  row, _resident(conv_w.shape), _resident(w_out.shape)],
        out_specs=(row, pl.BlockSpec((tm, nqkv + 5 * d), lambda i: (i, 0)), _acc_spec((8, d)), row),
        sem=("arbitrary",), args=(dxo, dxo, e, e, e, e, e, e, e, e, e, attn, conv_w, w_out), hook=hook)


def _swa_bwd(qkv, dattn, pos_col, pos_row, bias_t, sinks, dp, hook=None):
    t = qkv.shape[0]
    nb = t // BLOCK
    qw = SWA_HEADS * HEAD_DIM
    kw = SWA_KV_HEADS * HEAD_DIM

    def body(q_ref, kp_ref, kc_ref, vp_ref, vc_ref, do_ref, pq_ref, pkp_ref, pkc_ref, bias_ref, sink_ref,
             dp_in, dp_ref, gb_ref, dsk_ref, carry_ref, dq_ref):
        b = pl.program_id(0)

        @pl.when(b == 0)
        def _():
            gb_ref[...] = jnp.zeros_like(gb_ref)
            dsk_ref[...] = jnp.zeros_like(dsk_ref)
            carry_ref[...] = jnp.zeros_like(carry_ref)
            dq_ref[...] = jnp.zeros_like(dq_ref)
        dp_ref[:, 0:qw] = dq_ref[...]

        @pl.when(b < nb)
        def _():
            vis = _swa_visible(b, pq_ref, pkp_ref, pkc_ref)
            k2 = jnp.concatenate([kp_ref[...], kc_ref[...]], axis=0)
            v2 = jnp.concatenate([vp_ref[...], vc_ref[...]], axis=0)
            for hk in range(SWA_KV_HEADS):
                lo, hi = hk * HEAD_DIM, (hk + 1) * HEAD_DIM
                kh = k2[:, lo:hi]
                vh = v2[:, lo:hi]
                qg = _group_heads(q_ref, hk)
                dog = _group_heads(do_ref, hk)
                pn, psn = _swa_probs(qg, kh, vis, bias_ref[hk], _group_sinks(sink_ref, hk))
                dp = _nt(dog, vh)
                delta = jnp.sum(pn * dp, axis=-1, keepdims=True)
                ds = pn * (dp - delta)
                gb_ref[hk] += ds
                dsk_ref[hk] += -psn * delta
                dsb = (ds * (HEAD_DIM ** -0.5)).astype(BF)
                dqg = _nn(dsb, kh).astype(BF)
                for g in range(SWA_GROUP):
                    h = hk * SWA_GROUP + g
                    dq_ref[:, h * HEAD_DIM:(h + 1) * HEAD_DIM] = dqg[g * BLOCK:(g + 1) * BLOCK]
                dk = _tn(dsb, qg)
                dv = _tn(pn.astype(BF), dog)
                dp_ref[:, qw + lo:qw + hi] = (carry_ref[:, lo:hi] + dk[0:BLOCK]).astype(BF)
                dp_ref[:, qw + kw + lo:qw + kw + hi] = (carry_ref[:, kw + lo:kw + hi] + dv[0:BLOCK]).astype(BF)
                carry_ref[:, lo:hi] = dk[BLOCK:2 * BLOCK]
                carry_ref[:, kw + lo:kw + hi] = dv[BLOCK:2 * BLOCK]

        @pl.when(b == nb)
        def _():
            dp_ref[:, qw:qw + 2 * kw] = carry_ref[...].astype(BF)

    cur = lambda b: jnp.minimum(b, nb - 1)
    prev = lambda b: jnp.maximum(cur(b) - 1, 0)
    return _call(
        body, name="swa_bwd", grid=(nb + 1,),
        out_shape=(S(dp.shape, BF), S((SWA_KV_HEADS, GROUP_ROWS, 2 * BLOCK), F32), S((SWA_KV_HEADS, GROUP_ROWS, 1), F32)),
        in_specs=[
            pl.BlockSpec((BLOCK, qw), lambda b: (cur(b), 0)),
            pl.BlockSpec((BLOCK, kw), lambda b: (prev(b), qw // kw)),
            pl.BlockSpec((BLOCK, kw), lambda b: (cur(b), qw // kw)),
            pl.BlockSpec((BLOCK, kw), lambda b: (prev(b), qw // kw + 1)),
            pl.BlockSpec((BLOCK, kw), lambda b: (cur(b), qw // kw + 1)),
            pl.BlockSpec((BLOCK, qw), lambda b: (cur(b), 0)),
            pl.BlockSpec((BLOCK, 1), lambda b: (cur(b), 0)),
            pl.BlockSpec((1, BLOCK), lambda b: (0, prev(b))),
            pl.BlockSpec((1, BLOCK), lambda b: (0, cur(b))),
            _resident(bias_t.shape),
            SMEM_SPEC,
            ANY,
        ],
        out_specs=(
            pl.BlockSpec((BLOCK, qw + 2 * kw), lambda b: (jnp.maximum(b - 1, 0), 0)),
            _acc_spec((SWA_KV_HEADS, GROUP_ROWS, 2 * BLOCK)),
            _acc_spec((SWA_KV_HEADS, GROUP_ROWS, 1)),
        ),
        scratch_shapes=[pltpu.VMEM((BLOCK, 2 * kw), F32), pltpu.VMEM((BLOCK, qw), BF)],
        aliases={11: 0}, sem=("arbitrary",),
        args=(qkv, qkv, qkv, qkv, qkv, dattn, pos_col, pos_row, pos_row, bias_t, sinks, dp), hook=hook)


def _bias_reduce(gb, dsk):
    def body(gb_ref, dsk_ref, drb_ref, dsink_ref):
        bucket = _t5_bucket(_block_rel())
        for b in range(REL_BUCKETS):
            mask = bucket == b
            for h in range(SWA_HEADS):
                drb_ref[h, b] = jnp.sum(jnp.where(mask, gb_ref[h], 0.0))
        for h in range(SWA_HEADS):
            dsink_ref[0, h] = jnp.sum(dsk_ref[h])

    return pl.pallas_call(
        body, name="bias_reduce", out_shape=(S((SWA_HEADS, REL_BUCKETS), F32), S((1, SWA_HEADS), F32)),
        in_specs=[VMEM_SPEC, VMEM_SPEC], out_specs=(SMEM_SPEC, SMEM_SPEC),
    )(gb, dsk)


def _mix_in_bwd(dp, x, gn, w_in, dxo):
    t, d = x.shape
    tm = 256
    npr = dp.shape[1]

    def body(dp_ref, x_ref, gn_ref, w_ref, dxo_ref, dx_ref, dgn_ref, h_ref):
        @pl.when(pl.program_id(0) == 0)
        def _():
            dgn_ref[...] = jnp.zeros_like(dgn_ref)
        dh = jnp.zeros((tm, d), F32)
        for c0 in range(0, npr, 1024):
            c1 = min(c0 + 1024, npr)
            dh = dh + _nt(dp_ref[:, c0:c1], w_ref[:, c0:c1])
        n, r = _rms(x_ref[...])
        gnv = gn_ref[...]
        h_ref[...] = (n * gnv).astype(BF)
        dx, dgn = _rms_bwd(dh, n, r, gnv)
        dx_ref[...] = dxo_ref[...] + dx
        dgn_ref[...] += dgn

    row = pl.BlockSpec((tm, d), lambda i: (i, 0))
    return pl.pallas_call(
        body, name="mix_in_bwd", grid=(t // tm,),
        out_shape=(S((t, d), F32), S((1, d), F32), S((t, d), BF)),
        in_specs=[pl.BlockSpec((tm, npr), lambda i: (i, 0)), row, _resident((1, d)), _resident(w_in.shape), row],
        out_specs=(row, _acc_spec((1, d)), row),
        compiler_params=_params(("arbitrary",)),
    )(dp, x, gn, w_in, dxo)


CAST_STEPS = 4


def _local_prelude(arrays, rel_bias_t, name, hook=None):
    n = len(arrays)

    def body(*refs):
        rb_ref, table_ref = refs[n], refs[-1]
        for src, dst in zip(refs[:n], refs[n + 1:-1]):
            dst[...] = src[...].astype(BF)

        @pl.when(pl.program_id(0) == 0)
        def _():
            _bias_table(rb_ref, table_ref)

    blocks = [pl.BlockSpec((a.shape[0] // CAST_STEPS, a.shape[1]), lambda i: (i, 0)) for a in arrays]
    table = (SWA_HEADS, BLOCK, 2 * BLOCK)
    res = _call(body, name=name, grid=(CAST_STEPS,), out_shape=tuple(S(a.shape, BF) for a in arrays) + (S(table, F32),),
                in_specs=blocks + [SMEM_SPEC], out_specs=tuple(blocks) + (_acc_spec(table),), sem=("arbitrary",),
                args=list(arrays) + [rel_bias_t], hook=hook)
    return res[:n], res[n]


PAIR_STEPS = 4


def _add_bf16(pairs, name):
    def body(*refs):
        ins, outs = refs[:2 * len(pairs)], refs[2 * len(pairs):]
        for q, o_ref in enumerate(outs):
            o_ref[...] = (ins[2 * q][...].astype(F32) + ins[2 * q + 1][...].astype(F32)).astype(BF)

    in_specs, out_specs, out_shape, args = [], [], [], []
    for a, b in pairs:
        rows, cols = a.shape
        blk = pl.BlockSpec((rows // PAIR_STEPS, cols), lambda i: (i, 0))
        in_specs += [blk, blk]
        out_specs.append(blk)
        out_shape.append(S((rows, cols), BF))
        args += [a, b]
    return pl.pallas_call(body, name=name, grid=(PAIR_STEPS,), out_shape=tuple(out_shape), in_specs=in_specs,
                          out_specs=tuple(out_specs), compiler_params=_params(("parallel",)))(*args)


def _adam_update(w, g, m, v):
    mn = ADAM_B1 * m + (1.0 - ADAM_B1) * g
    vn = ADAM_B2 * v + (1.0 - ADAM_B2) * (g * g)
    m_hat = mn / (1.0 - ADAM_B1 ** ADAM_STEP)
    v_hat = vn / (1.0 - ADAM_B2 ** ADAM_STEP)
    return -ADAM_LR * (m_hat / (jnp.sqrt(v_hat) + ADAM_EPS) + ADAM_WD * w), mn, vn


def _adamw(quads, name, steps):
    def body(*refs):
        ins, outs = refs[:4 * len(quads)], refs[4 * len(quads):]
        for q in range(len(quads)):
            w_ref, g_ref, m_ref, v_ref = ins[4 * q:4 * q + 4]
            echo_ref, d_ref, mo_ref, vo_ref = outs[4 * q:4 * q + 4]
            gv = g_ref[...]
            echo_ref[...] = gv
            d_ref[...], mo_ref[...], vo_ref[...] = _adam_update(w_ref[...], gv, m_ref[...], v_ref[...])

    in_specs, out_shape, args = [], [], []
    for quad in quads:
        rows, cols = quad[0].shape
        in_specs += [pl.BlockSpec((rows // steps, cols), lambda i: (i, 0))] * 4
        out_shape += [S((rows, cols), F32)] * 4
        args += list(quad)
    res = pl.pallas_call(body, name=name, grid=(steps,), out_shape=tuple(out_shape), in_specs=in_specs,
                         out_specs=tuple(in_specs), compiler_params=_params(("parallel",)))(*args)
    return [res[4 * q:4 * q + 4] for q in range(len(quads))]


def _place():
    x, y, c = lax.axis_index("x"), lax.axis_index("y"), lax.axis_index("c")
    return x, y, c


OTHER_CHIPS = ((1, 0), (0, 1), (1, 1))


def _flip(v, f):
    return 1 - v if f else v


def _remote(src, dst, ssem, rsem, dev):
    return pltpu.make_async_remote_copy(src_ref=src, dst_ref=dst, send_sem=ssem, recv_sem=rsem,
                                        device_id=dev, device_id_type=MESH)


class _Both:
    def __init__(self, hooks):
        self.hooks = hooks
        self.ins = [a for h in hooks for a in h.ins]
        self.out_shape = tuple(o for h in hooks for o in h.out_shape)
        self.scratch = [x for h in hooks for x in h.scratch]

    def _each(self, ins, outs, scr):
        i = o = s = 0
        for h in self.hooks:
            ni, no, ns = len(h.ins), len(h.out_shape), len(h.scratch)
            yield h, ins[i:i + ni], outs[o:o + no], scr[s:s + ns]
            i, o, s = i + ni, o + no, s + ns

    def start(self, ins, outs, scr):
        for h, *refs in self._each(ins, outs, scr):
            h.start(*refs)

    def relay(self, ins, outs, scr):
        for h, *refs in self._each(ins, outs, scr):
            h.relay(*refs)

    def finish(self, ins, outs, scr):
        for h, *refs in self._each(ins, outs, scr):
            h.finish(*refs)

    @property
    def lead(self):
        return max(h.lead for h in self.hooks)

    @property
    def results(self):
        return [h.results for h in self.hooks]

    @results.setter
    def results(self, res):
        for h, _, mine, _ in self._each((), tuple(res), ()):
            h.results = mine


class _GatherHook:
    def __init__(self, shards, kinds, lead=2):
        self.ins, self.kinds, self.lead, n = list(shards), list(kinds), lead, len(shards)
        self.out_shape = tuple(
            S((w.shape[0], 4 * w.shape[1]), BF) if k == "col" else S((4,) + w.shape, BF) for w, k in zip(shards, kinds))
        dma = pltpu.SemaphoreType.DMA
        self.scratch = ([dma((n, 3)) for _ in range(5)] + [dma((n,)), dma((n,))]
                        + [pltpu.VMEM((3, w.shape[0] // 2, w.shape[1]), BF) for w in shards]
                        + [pltpu.VMEM(w.shape, BF) for w in shards])

    def _window(self, outs, i, s, half):
        rows, cols = self.ins[i].shape
        rh = rows // 2
        start = pl.multiple_of(half * rh, 16)
        if self.kinds[i] == "col":
            return outs[i].at[pl.ds(start, rh), pl.ds(pl.multiple_of(s * cols, 128), cols)]
        return outs[i].at[s, pl.ds(start, rh), :]

    def _copies(self, ins, outs, scr):
        n = len(ins)
        ssem, rsem, fssem, frsem, ksem, lsem, osem = scr[:7]
        land, own = scr[7:7 + n], scr[7 + n:7 + 2 * n]
        x, y, c = _place()
        sibling = (x, y, 1 - c)
        loads, stores, sends, forwards, keeps, passed = [], [], [], [], [], []
        for i in range(n):
            rows, cols = self.ins[i].shape
            rh = rows // 2
            mine = (outs[i].at[:, pl.ds(pl.multiple_of((2 * x + y) * cols, 128), cols)] if self.kinds[i] == "col"
                    else outs[i].at[2 * x + y])
            loads.append(functools.partial(pltpu.make_async_copy, ins[i], own[i], lsem.at[i]))
            stores.append(functools.partial(pltpu.make_async_copy, own[i], mine, osem.at[i]))
            src = ins[i].at[pl.ds(pl.multiple_of(c * rh, 16), rh), :]
            for j, (fx, fy) in enumerate(OTHER_CHIPS):
                px, py = _flip(x, fx), _flip(y, fy)
                sends.append(functools.partial(_remote, src, land[i].at[j], ssem.at[i, j], rsem.at[i, j], (px, py, c)))
                here = self._window(outs, i, 2 * px + py, c)
                forwards.append(functools.partial(_remote, land[i].at[j], here, fssem.at[i, j], frsem.at[i, j], sibling))
                keeps.append(functools.partial(pltpu.make_async_copy, land[i].at[j], here, ksem.at[i, j]))
                there = self._window(outs, i, 2 * px + py, 1 - c)
                passed.append(functools.partial(_remote, there, there, fssem.at[i, j], frsem.at[i, j], sibling))
        return loads, stores, sends, forwards, keeps, passed

    def start(self, ins, outs, scr):
        loads, _, sends, _, _, _ = self._copies(ins, outs, scr)
        for make in sends + loads:
            make().start()

    def relay(self, ins, outs, scr):
        loads, stores, sends, forwards, keeps, _ = self._copies(ins, outs, scr)
        for load, store in zip(loads, stores):
            load().wait()
            store().start()
        for send, forward, keep in zip(sends, forwards, keeps):
            send().wait_recv()
            forward().start()
            keep().start()

    def finish(self, ins, outs, scr):
        _, stores, sends, forwards, keeps, passed = self._copies(ins, outs, scr)
        for make in passed:
            make().wait_recv()
        for make in sends + forwards:
            make().wait_send()
        for make in keeps + stores:
            make().wait()


class _ChipsHook:
    def __init__(self, parts, kinds):
        self.ins, self.kinds, n = list(parts), list(kinds), len(parts)
        self.out_shape = tuple(
            S((4, p.shape[0], p.shape[1] // 4), BF) if k == "col" else S(p.shape, BF) for p, k in zip(parts, kinds))
        dma = pltpu.SemaphoreType.DMA
        self.scratch = ([dma((n, 3)), dma((n, 3)), dma((n,)), dma((n,))]
                        + [pltpu.VMEM(o.shape[1:], BF) for o in self.out_shape])

    def _slab(self, ins, i, s):
        _, rows, cols = self.out_shape[i].shape
        if self.kinds[i] == "col":
            return ins[i].at[:, pl.ds(pl.multiple_of(s * cols, 128), cols)]
        return ins[i].at[s]

    def _copies(self, ins, outs, scr):
        ssem, rsem, lsem, osem = scr[:4]
        own = scr[4:]
        x, y, c = _place()
        loads, stores, sends = [], [], []
        for i in range(len(ins)):
            loads.append(functools.partial(pltpu.make_async_copy, self._slab(ins, i, 2 * x + y), own[i], lsem.at[i]))
            stores.append(functools.partial(pltpu.make_async_copy, own[i], outs[i].at[3], osem.at[i]))
            for j, (fx, fy) in enumerate(OTHER_CHIPS):
                px, py = _flip(x, fx), _flip(y, fy)
                sends.append(functools.partial(_remote, self._slab(ins, i, 2 * px + py), outs[i].at[j], ssem.at[i, j],
                                               rsem.at[i, j], (px, py, c)))
        return loads, stores, sends

    def start(self, ins, outs, scr):
        loads, _, sends = self._copies(ins, outs, scr)
        for make in sends + loads:
            make().start()

    lead = 2

    def relay(self, ins, outs, scr):
        loads, stores, _ = self._copies(ins, outs, scr)
        for load, store in zip(loads, stores):
            load().wait()
            store().start()

    def finish(self, ins, outs, scr):
        _, stores, sends = self._copies(ins, outs, scr)
        for make in sends + stores:
            make().wait()


SHARE_STEPS = 2


def _sum_share(slab_list, name, hook=None):
    n = len(slab_list)
    geom = [(sl.shape[1], sl.shape[1] // SHARE_STEPS, sl.shape[2]) for sl in slab_list]

    def body(*refs):
        ins, outs, scr = refs[:n], refs[n:2 * n], refs[2 * n:]
        i = pl.program_id(0)
        x, y, c = _place()
        sibling = (x, y, 1 - c)

        def copies(q, k):
            rh, tr, _ = geom[q]
            stage, lsem, ssem, rsem = scr[4 * q:4 * q + 4]
            dst = outs[q].at[pl.ds(pl.multiple_of(c * rh + k * tr, 8), tr), :]
            return (pltpu.make_async_copy(stage.at[k], dst, lsem.at[k]),
                    _remote(stage.at[k], dst, ssem.at[k], rsem, sibling))

        for q in range(n):
            acc = ins[q][3].astype(F32)
            for k in range(3):
                acc = acc + ins[q][k].astype(F32)
            scr[4 * q][i] = acc
            for cp in copies(q, i):
                cp.start()

        @pl.when(i == SHARE_STEPS - 1)
        def _():
            for q in range(n):
                rh = geom[q][0]
                for k in range(SHARE_STEPS):
                    local, remote = copies(q, k)
                    local.wait()
                    remote.wait_send()
                got = outs[q].at[pl.ds(pl.multiple_of((1 - c) * rh, 8), rh), :]
                _remote(got, got, scr[4 * q + 2].at[0], scr[4 * q + 3], sibling).wait_recv()

    dma = pltpu.SemaphoreType.DMA
    scratch = []
    for rh, tr, cols in geom:
        scratch += [pltpu.VMEM((SHARE_STEPS, tr, cols), F32), dma((SHARE_STEPS,)), dma((SHARE_STEPS,)), dma]
    return _call(
        body, name=name, grid=(SHARE_STEPS,), out_shape=tuple(S((2 * rh, cols), F32) for rh, _, cols in geom),
        in_specs=[pl.BlockSpec((4, tr, cols), lambda i: (0, i, 0)) for _, tr, cols in geom], out_specs=(ANY,) * n,
        scratch_shapes=scratch, sem=("arbitrary",), args=list(slab_list), hook=hook)


class _SmallSumHook:
    def __init__(self, buf):
        self.ins, self.out_shape = [buf], (S(buf.shape, F32),)
        dma = pltpu.SemaphoreType.DMA
        self.scratch = [pltpu.VMEM((8,) + buf.shape, F32), pltpu.VMEM(buf.shape, F32), dma((7,)), dma((7,)), dma]

    def _sends(self, scr):
        slots, _, ssem, rsem, _ = scr
        x, y, c = _place()
        me = 4 * x + 2 * y + c
        for r in range(1, 8):
            px, py, pc = _flip(x, (r >> 2) & 1), _flip(y, (r >> 1) & 1), _flip(c, r & 1)
            yield (functools.partial(_remote, slots.at[me], slots.at[me], ssem.at[r - 1], rsem.at[r - 1], (px, py, pc)),
                   functools.partial(_remote, slots.at[me], slots.at[4 * px + 2 * py + pc], ssem.at[r - 1],
                                     rsem.at[r - 1], (px, py, pc)))

    def start(self, ins, outs, scr):
        slots, _, _, _, lsem = scr
        x, y, c = _place()
        load = pltpu.make_async_copy(ins[0], slots.at[4 * x + 2 * y + c], lsem)
        load.start()
        load.wait()
        for send, _ in self._sends(scr):
            send().start()

    lead = 0

    def relay(self, ins, outs, scr):
        pass

    def finish(self, ins, outs, scr):
        slots, total, _, _, lsem = scr
        for _, arrival in self._sends(scr):
            arrival().wait_recv()
        for send, _ in self._sends(scr):
            send().wait_send()
        acc = slots[0]
        for k in range(1, 8):
            acc = acc + slots[k]
        total[...] = acc
        store = pltpu.make_async_copy(total, outs[0], lsem)
        store.start()
        store.wait()


BIG = ("ffn1_w_gu", "ffn1_w_down", "w_in", "w_out", "xattn_wq", "xattn_wkv", "xattn_wo", "ffn2_w_gu", "ffn2_w_down")
KIND = {"ffn1_w_gu": "col", "ffn1_w_down": "row", "w_in": "col", "w_out": "row", "xattn_wq": "row",
        "xattn_wkv": "col", "xattn_wo": "row", "ffn2_w_gu": "col", "ffn2_w_down": "row"}
WEIGHTS = ("rel_bias", "ffn1_norm", "ffn1_w_gu", "ffn1_w_down", "mix_norm", "w_in", "sinks", "conv_w", "w_out",
           "xattn_norm", "mem_norm", "xattn_wq", "xattn_wkv", "xattn_wo", "ffn2_norm", "ffn2_w_gu", "ffn2_w_down",
           "final_norm")
SMALL_ROWS = 16
GAIN_ROW = {"ffn1_norm": 0, "mix_norm": 1, "xattn_norm": 2, "mem_norm": 3, "ffn2_norm": 4, "final_norm": 5}
CONV_ROW, SINK_ROW, BIAS_ROW, LOSS_ROW = 6, 9, 10, 11
TAP_ROWS = 8


def _rows_block(rows, d):
    buf = jnp.zeros((SMALL_ROWS, d), F32)
    for r, v in rows.items():
        buf = lax.dynamic_update_slice(buf, v if v.ndim == 2 else v.reshape(1, -1), (r, 0))
    return buf


def _adamw_small(gsum, conv_g, bias_g, small):
    names = list(small)

    def grad(k, gsum_ref, conv_ref, bias_ref):
        if k in GAIN_ROW:
            return gsum_ref[GAIN_ROW[k]:GAIN_ROW[k] + 1, :]
        if k == "sinks":
            return gsum_ref[SINK_ROW:SINK_ROW + 1, 0:small[k][0].shape[1]]
        return conv_ref[...] if k == "conv_w" else bias_ref[...]

    def body(gsum_ref, conv_ref, bias_ref, *refs):
        ins, outs = refs[:3 * len(names)], refs[3 * len(names):]
        for q, k in enumerate(names):
            w_ref, m_ref, v_ref = ins[3 * q:3 * q + 3]
            g_ref, d_ref, mo_ref, vo_ref = outs[4 * q:4 * q + 4]
            gv = grad(k, gsum_ref, conv_ref, bias_ref)
            g_ref[...] = gv
            d_ref[...], mo_ref[...], vo_ref[...] = _adam_update(w_ref[...], gv, m_ref[...], v_ref[...])

    res = pl.pallas_call(
        body, name="adamw_small", out_shape=tuple(S(small[k][0].shape, F32) for k in names for _ in range(4)),
        compiler_params=_params())(gsum, conv_g, bias_g, *[a for k in names for a in small[k]])
    return {k: res[4 * q:4 * q + 4] for q, k in enumerate(names)}


def _local_step(x, mem, pos, target, w, gains, bias_table, sinks, conv_w, shards):
    t, d = x.shape
    w = dict(w)
    grads, slabs = {}, {}
    pos_col = pos.reshape(t, 1)
    pos_row = pos.reshape(1, t)
    bias_t = bias_table.reshape(SWA_KV_HEADS, GROUP_ROWS, 2 * BLOCK)

    def gather(names, lead=2):
        return _GatherHook([shards[k] for k in names], [KIND[k] for k in names], lead)

    def gathered(names, hook):
        for k, gw in zip(names, hook.results):
            w[k] = gw if KIND[k] == "col" else gw.reshape(-1, gw.shape[-1])

    def dw(problems, tn, name, hook=None):
        res = _dw_pair([(a, b) for _, a, b in problems], tn, name, KIND[problems[0][0]], hook)
        grads.update(zip((k for k, _, _ in problems), res))

    def pair_sums(names):
        flat = lambda k, v: v.reshape(-1, v.shape[-1]) if KIND[k] == "row" else v
        sums = _add_bf16([(flat(k, grads[k][0]), flat(k, grads[k][1])) for k in names], "pair_sum_" + names[0])
        return {k: (p if KIND[k] == "col" else p.reshape(4, -1, p.shape[-1])) for k, p in zip(names, sums)}

    def chips(names, parts):
        return _ChipsHook([parts[k] for k in names], [KIND[k] for k in names])

    def reduced(names, hook):
        slabs.update(zip(names, hook.results))

    names = ("w_in", "w_out")
    hook = gather(names, lead=1)
    x1, g1, u1 = _ffn_fwd(x, gains["ffn1_norm"], w["ffn1_w_gu"], w["ffn1_w_down"], "ffn1_fwd", hook)
    gathered(names, hook)
    names = ("xattn_wq", "xattn_wkv", "xattn_wo")
    hook = gather(names)
    qkv, e = _mix_proj(x1, gains["mix_norm"], w["w_in"], hook)
    gathered(names, hook)
    names = ("ffn2_w_gu",)
    hook = gather(names)
    attn = _swa_fwd(qkv, pos_col, pos_row, bias_t, sinks, hook)
    gathered(names, hook)
    names = ("ffn2_w_down",)
    hook = gather(names)
    x2, merged = _mix_out_fwd(e, attn, conv_w, w["w_out"], x1, hook)
    gathered(names, hook)
    mh, kv = _mem_kv(mem, gains["mem_norm"], w["xattn_wkv"])
    x3, qx, o = _xattn_fwd(x2, gains["xattn_norm"], w["xattn_wq"], kv, w["xattn_wo"])
    dx4, g2, u2, d_final, loss = _ffn_fwd(x3, gains["ffn2_norm"], w["ffn2_w_gu"], w["ffn2_w_down"], "ffn2_fwd",
                                          head=(gains["final_norm"], target))

    dx3, d_ffn2, dgu2, a2, h4, dyh4 = _ffn_bwd(dx4, x3, gains["ffn2_norm"], g2, u2, w["ffn2_w_gu"], w["ffn2_w_down"],
                                               "ffn2_bwd")
    dw([("ffn2_w_gu", h4, dgu2)], 1408, "dw_ffn2_gu")
    dw([("ffn2_w_down", a2, dyh4)], 512, "dw_ffn2_down")
    parts = pair_sums(("ffn2_w_gu", "ffn2_w_down"))
    hook = chips(("ffn2_w_gu",), parts)
    dx2, d_xattn, dxh3, h3, dqx, grads["xattn_wkv"], d_mem = _xattn_bwd(
        dx3, x2, gains["xattn_norm"], qx, kv, w["xattn_wq"], w["xattn_wo"], mh, mem, w["xattn_wkv"], hook)
    reduced(("ffn2_w_gu",), hook)
    hook = chips(("ffn2_w_down",), parts)
    dattn, dp, dcw, dxh2 = _mix_out_bwd(dx2, e, attn, conv_w, w["w_out"], qkv.shape[1], hook)
    reduced(("ffn2_w_down",), hook)
    dw([("xattn_wo", o, dxh3), ("xattn_wq", h3, dqx), ("w_out", merged, dxh2)], 1024, "dw_wo_wq_wout")
    names = ("xattn_wo", "xattn_wq", "xattn_wkv", "w_out")
    hook = chips(names, pair_sums(names))
    dp, gb, dsk = _swa_bwd(qkv, dattn, pos_col, pos_row, bias_t, sinks, dp, hook)
    reduced(names, hook)
    d_rel_bias, d_sinks = _bias_reduce(gb.reshape(SWA_HEADS, BLOCK, 2 * BLOCK), dsk.reshape(SWA_HEADS, BLOCK, 1))
    dx1, d_mix, h2 = _mix_in_bwd(dp, x1, gains["mix_norm"], w["w_in"], dx2)
    dw([("w_in", h2, dp)], w["w_in"].shape[1] // 4, "dw_win")
    names = ("w_in",)
    hook = chips(names, pair_sums(names))
    dx0, d_ffn1, dgu1, a1, h1, dyh1 = _ffn_bwd(dx1, x, gains["ffn1_norm"], g1, u1, w["ffn1_w_gu"], w["ffn1_w_down"],
                                               "ffn1_bwd")
    dw([("ffn1_w_gu", h1, dgu1)], 1408, "dw_ffn1_gu", hook)
    reduced(names, hook)
    names = ("ffn1_w_gu",)
    hook = chips(names, pair_sums(names))
    dw([("ffn1_w_down", a1, dyh1)], 512, "dw_ffn1_down", hook)
    reduced(names, hook)
    rows = {0: d_ffn1, 1: d_mix, 2: d_xattn, 3: d_mem, 4: d_ffn2, 5: d_final, SINK_ROW: d_sinks,
            BIAS_ROW: d_rel_bias.reshape(1, -1), LOSS_ROW: loss[0, 0:1]}
    rows.update({CONV_ROW + j: dcw[j] for j in range(3)})
    last = ("ffn1_w_down",)
    return dx0, slabs, _rows_block(rows, d), chips(last, pair_sums(last))


def kernel(x, mem, positions, rel_bias, ffn1_norm, ffn1_w_gu, ffn1_w_down, mix_norm, w_in, sinks, conv_w, w_out, xattn_norm, mem_norm, xattn_wq, xattn_wkv, xattn_wo, ffn2_norm, ffn2_w_gu, ffn2_w_down, final_norm, loss_target, m_rel_bias, m_ffn1_norm, m_ffn1_w_gu, m_ffn1_w_down, m_mix_norm, m_w_in, m_sinks, m_conv_w, m_w_out, m_xattn_norm, m_mem_norm, m_xattn_wq, m_xattn_wkv, m_xattn_wo, m_ffn2_norm, m_ffn2_w_gu, m_ffn2_w_down, m_final_norm, v_rel_bias, v_ffn1_norm, v_ffn1_w_gu, v_ffn1_w_down, v_mix_norm, v_w_in, v_sinks, v_conv_w, v_w_out, v_xattn_norm, v_mem_norm, v_xattn_wq, v_xattn_wkv, v_xattn_wo, v_ffn2_norm, v_ffn2_w_gu, v_ffn2_w_down, v_final_norm):
    args = dict(locals())
    wts = {k: args[k] for k in WEIGHTS}
    mom = {k: args["m_" + k] for k in WEIGHTS}
    var = {k: args["v_" + k] for k in WEIGHTS}
    d = x.shape[-1]
    s_me = 2 * lax.axis_index("x") + lax.axis_index("y")

    first = ("ffn1_w_gu", "ffn1_w_down")
    rest = tuple(k for k in BIG if k not in first)
    shards = {k: wts[k][0].astype(BF) for k in first}
    cw_cols = conv_w.shape[-1]
    placed = lax.dynamic_update_slice(jnp.zeros((TAP_ROWS, d), F32), 0.5 * conv_w[0], (0, s_me * cw_cols))
    head = _Both([_GatherHook([shards[k] for k in first], [KIND[k] for k in first]), _SmallSumHook(placed)])
    casts, bias_table = _local_prelude([wts[k][0] for k in rest], rel_bias.T, "gather_ffn1", head)
    shards.update(zip(rest, casts))
    gathered, (conv_sum,) = head.results
    whole = {k: (gw if KIND[k] == "col" else gw.reshape(-1, gw.shape[-1])) for k, gw in zip(first, gathered)}
    conv_whole = conv_sum[0:3]

    gains = {k: wts[k].reshape(1, d) for k in GAIN_ROW}
    dx0, slabs, small, last_chips = _local_step(x[0], mem[0], positions[0], loss_target[0], whole, gains, bias_table,
                                               sinks, conv_whole, shards)

    late = ("ffn1_w_gu", "ffn1_w_down", "ffn2_w_down")
    early = tuple(k for k in BIG if k not in late)
    first_ones = tuple(k for k in BIG if k != "ffn1_w_down")
    tail = _Both([last_chips, _SmallSumHook(small)])
    shard_grads = dict(zip(first_ones, _sum_share([slabs[k] for k in first_ones], "sum_share", tail)))
    (last_slabs,), (small_sum,) = tail.results
    shard_grads["ffn1_w_down"], = _sum_share([last_slabs], "sum_share_ffn1_w_down")
    quad = lambda k: (wts[k][0], shard_grads[k], mom[k][0], var[k][0])
    updates = dict(zip(early, _adamw([quad(k) for k in early], "adamw_early", 8)))
    updates.update(zip(late, _adamw([quad(k) for k in late], "adamw_late", 8)))
    loss = small_sum[LOSS_ROW, 0]

    out_g, out_d, out_m, out_v = {}, {}, {}, {}
    for k in BIG:
        out_g[k], out_d[k], out_m[k], out_v[k] = (a[None] for a in updates[k])

    to_call = lambda k, a: a.T if k == "rel_bias" else a.reshape(1, -1)
    from_call = lambda k, a: a.T if k == "rel_bias" else a.reshape(wts[k].shape)
    small_names = [k for k in WEIGHTS if k not in KIND]
    conv_g = lax.dynamic_slice(small_sum, (CONV_ROW, s_me * cw_cols), (3, cw_cols)).reshape(1, -1)
    bias_g = small_sum[BIAS_ROW, 0:rel_bias.size].reshape(rel_bias.shape[::-1])
    done = _adamw_small(small_sum, conv_g, bias_g,
                        {k: tuple(to_call(k, a) for a in (wts[k], mom[k], var[k])) for k in small_names})
    for k in small_names:
        out_g[k], out_d[k], out_m[k], out_v[k] = (from_call(k, a) for a in done[k])

    return (loss, dx0[None], *[out_g[k] for k in WEIGHTS], *[out_d[k] for k in WEIGHTS],
            *[out_m[k] for k in WEIGHTS], *[out_v[k] for k in WEIGHTS])
```

```python
import functools
import math

import jax
import jax.numpy as jnp
from jax import lax
from jax.experimental import pallas as pl
from jax.experimental.pallas import tpu as pltpu

BF = jnp.bfloat16
F32 = jnp.float32
I32 = jnp.int32
S = jax.ShapeDtypeStruct

EPS = 1e-6
NEG = -1e30
POS_PAD = 1 << 30
WINDOW = 128
BLOCK = 128
HEAD_DIM = 64
SWA_HEADS = 16
SWA_KV_HEADS = 4
SWA_GROUP = SWA_HEADS // SWA_KV_HEADS
MEM_HEADS = 4
REL_BUCKETS = 32
REL_MAX_DIST = 128
ADAM_LR = 0.001
ADAM_B1 = 0.9
ADAM_B2 = 0.999
ADAM_EPS = 1e-08
ADAM_WD = 0.01
ADAM_STEP = 10

V7X_VMEM_LIMIT_BYTES = 56 * 1024 * 1024
MESH = pl.DeviceIdType.MESH
ANY = pl.BlockSpec(memory_space=pl.ANY)
VMEM_SPEC = pl.BlockSpec(memory_space=pltpu.VMEM)
SMEM_SPEC = pl.BlockSpec(memory_space=pltpu.SMEM)


def _params(sem=None):
    return pltpu.CompilerParams(dimension_semantics=sem, vmem_limit_bytes=V7X_VMEM_LIMIT_BYTES)


def _resident(shape):
    nd = len(shape)
    return pl.BlockSpec(shape, lambda *_: (0,) * nd, pipeline_mode=pl.Buffered(1))


def _acc_spec(shape):
    nd = len(shape)
    return pl.BlockSpec(shape, lambda *_: (0,) * nd)


def _call(body, *, name, grid, out_shape, in_specs, out_specs, args, sem, scratch_shapes=(), aliases=None, hook=None):
    aliases = aliases or {}
    if hook is None:
        return pl.pallas_call(body, name=name, grid=grid, out_shape=out_shape, in_specs=in_specs, out_specs=out_specs,
                              scratch_shapes=list(scratch_shapes), input_output_aliases=aliases,
                              compiler_params=_params(sem))(*args)
    n_in, n_out, n_scr = len(in_specs), len(out_shape), len(scratch_shapes)
    h_in, h_out = len(hook.ins), len(hook.out_shape)

    def at_step(pick):
        conds = [pl.program_id(ax) == pick(size) for ax, size in enumerate(grid)]
        return functools.reduce(jnp.logical_and, conds)

    def hosted(*refs):
        k_in, x_in = refs[:n_in], refs[n_in:n_in + h_in]
        o0 = n_in + h_in
        k_out, x_out = refs[o0:o0 + n_out], refs[o0 + n_out:o0 + n_out + h_out]
        s0 = o0 + n_out + h_out
        k_scr, x_scr = refs[s0:s0 + n_scr], refs[s0 + n_scr:]

        @pl.when(at_step(lambda size: 0))
        def _():
            hook.start(x_in, x_out, x_scr)
        body(*k_in, *k_out, *k_scr)
        early = bool(hook.lead) and grid[0] > hook.lead
        if early:
            on_axis0 = pl.program_id(0) == grid[0] - 1 - hook.lead
            rest_first = [pl.program_id(ax) == 0 for ax in range(1, len(grid))]

            @pl.when(functools.reduce(jnp.logical_and, rest_first, on_axis0))
            def _():
                hook.relay(x_in, x_out, x_scr)

        @pl.when(at_step(lambda size: size - 1))
        def _():
            if not early:
                hook.relay(x_in, x_out, x_scr)
            hook.finish(x_in, x_out, x_scr)

    res = pl.pallas_call(
        hosted, name=name, grid=grid, out_shape=tuple(out_shape) + tuple(hook.out_shape),
        in_specs=list(in_specs) + [ANY] * h_in, out_specs=tuple(out_specs) + (ANY,) * h_out,
        scratch_shapes=list(scratch_shapes) + list(hook.scratch), input_output_aliases=aliases,
        compiler_params=_params(("arbitrary",) * len(grid)),
    )(*args, *hook.ins)
    hook.results = res[n_out:]
    return res[:n_out]


def _nn(a, b):
    return jnp.dot(a, b, preferred_element_type=F32)


def _nt(a, b):
    return lax.dot_general(a, b, (((1,), (1,)), ((), ())), preferred_element_type=F32)


def _tn(a, b):
    return lax.dot_general(a, b, (((0,), (0,)), ((), ())), preferred_element_type=F32)


def _sigmoid(v):
    return 1.0 / (1.0 + jnp.exp(-v))


def _rms(x):
    r = lax.rsqrt(jnp.mean(x * x, axis=-1, keepdims=True) + EPS)
    return x * r, r


def _rms_bwd(dh, n, r, g):
    dn = dh * g
    dx = r * (dn - n * jnp.mean(dn * n, axis=-1, keepdims=True))
    return dx, jnp.sum(dh * n, axis=0, keepdims=True)


def _ffn_fwd(x, gn, wgu, wd, name, hook=None, head=None):
    t, d = x.shape
    f = wd.shape[0]
    tm, fc = 256, 1408

    def body(x_ref, gn_ref, wgu_ref, wd_ref, *rest):
        xv = x_ref[...]
        n, _ = _rms(xv)
        h = (n * gn_ref[...]).astype(BF)
        g_ref, u_ref = rest[-2:] if head is None else rest[3:5]
        acc = jnp.zeros((tm, d), F32)
        for c0 in range(0, f, fc):
            g = _nn(h, wgu_ref[:, c0:c0 + fc])
            u = _nn(h, wgu_ref[:, f + c0:f + c0 + fc])
            g_ref[:, c0:c0 + fc] = g.astype(BF)
            u_ref[:, c0:c0 + fc] = u.astype(BF)
            a = (g * _sigmoid(g)) * u
            acc = acc + _nn(a.astype(BF), wd_ref[c0:c0 + fc, :])
        y = xv + 0.5 * acc
        if head is None:
            rest[0][...] = y
            return
        gf_ref, t_ref, dy_ref, _, _, dgf_ref, loss_ref = rest

        @pl.when(pl.program_id(0) == 0)
        def _():
            dgf_ref[...] = jnp.zeros_like(dgf_ref)
            loss_ref[...] = jnp.zeros_like(loss_ref)
        ny, ry = _rms(y)
        gf = gf_ref[...]
        err = ny * gf - t_ref[...]
        loss_ref[...] += 0.5 * jnp.sum(jnp.sum(err * err, axis=-1, keepdims=True) / d, axis=0, keepdims=True)
        dy, dgf = _rms_bwd(err / d, ny, ry, gf)
        dy_ref[...] = dy
        dgf_ref[...] += dgf

    row = pl.BlockSpec((tm, d), lambda i: (i, 0))
    frow = pl.BlockSpec((tm, f), lambda i: (i, 0))
    in_specs = [row, _resident((1, d)), _resident(wgu.shape), _resident(wd.shape)]
    out_shape = (S((t, d), F32), S((t, f), BF), S((t, f), BF))
    out_specs = (row, frow, frow)
    args = (x, gn, wgu, wd)
    if head is not None:
        in_specs += [_resident((1, d)), row]
        out_shape += (S((1, d), F32), S((1, 128), F32))
        out_specs += (_acc_spec((1, d)), _acc_spec((1, 128)))
        args += tuple(head)
    return _call(body, name=name, grid=(t // tm,), out_shape=out_shape, in_specs=in_specs, out_specs=out_specs,
                 sem=("parallel",) if head is None else ("arbitrary",), args=args, hook=hook)


def _mix_proj(x, gn, w_in, hook=None):
    t, d = x.shape
    tm = 256
    nqkv = 1536
    ne = w_in.shape[1] - nqkv

    def body(x_ref, gn_ref, w_ref, qkv_ref, e_ref):
        n, _ = _rms(x_ref[...])
        h = (n * gn_ref[...]).astype(BF)
        qkv_ref[...] = _nn(h, w_ref[:, 0:nqkv]).astype(BF)
        for c0 in range(0, ne, 1024):
            e_ref[:, c0:c0 + 1024] = _nn(h, w_ref[:, nqkv + c0:nqkv + c0 + 1024]).astype(BF)

    return _call(
        body, name="mix_proj", grid=(t // tm,),
        out_shape=(S((t, nqkv), BF), S((t, ne), BF)),
        in_specs=[pl.BlockSpec((tm, d), lambda i: (i, 0)), _resident((1, d)), _resident(w_in.shape)],
        out_specs=(pl.BlockSpec((tm, nqkv), lambda i: (i, 0)), pl.BlockSpec((tm, ne), lambda i: (i, 0))),
        sem=("parallel",), args=(x, gn, w_in), hook=hook)


def _t5_bucket(rel):
    n = jnp.maximum(rel, 0)
    max_exact = REL_BUCKETS // 2
    nf = jnp.maximum(n, 1).astype(F32)
    large = max_exact + (jnp.log(nf / max_exact) / math.log(REL_MAX_DIST / max_exact)
                         * (REL_BUCKETS - max_exact)).astype(I32)
    large = jnp.minimum(large, REL_BUCKETS - 1)
    return jnp.where(n < max_exact, n, large)


def _block_rel():
    i = lax.broadcasted_iota(I32, (BLOCK, 2 * BLOCK), 0)
    j = lax.broadcasted_iota(I32, (BLOCK, 2 * BLOCK), 1)
    return i + BLOCK - j


def _bias_table(rb_ref, o_ref):
    bucket = _t5_bucket(_block_rel())
    for h in range(SWA_HEADS):
        acc = jnp.zeros((BLOCK, 2 * BLOCK), F32)
        for b in range(REL_BUCKETS):
            acc = jnp.where(bucket == b, rb_ref[h, b], acc)
        o_ref[h] = acc


GROUP_ROWS = SWA_GROUP * BLOCK


def _swa_visible(b, pq_ref, pkp_ref, pkc_ref):
    pk = jnp.concatenate([pkp_ref[...], pkc_ref[...]], axis=1)
    col = lax.broadcasted_iota(I32, (1, 2 * BLOCK), 1)
    pk = jnp.where(jnp.logical_and(b == 0, col < BLOCK), POS_PAD, pk)
    rel = jnp.concatenate([pq_ref[...]] * SWA_GROUP, axis=0) - pk
    return jnp.logical_and(rel >= 0, rel < WINDOW)


def _swa_poison(b, pq_ref, pkp_ref, pkc_ref):
    pk = jnp.concatenate([pkp_ref[...], pkc_ref[...]], axis=1)
    col = lax.broadcasted_iota(I32, (1, 2 * BLOCK), 1)
    pk = jnp.where(jnp.logical_and(b == 0, col < BLOCK), POS_PAD, pk)
    rel = pq_ref[...] - pk
    off = jnp.logical_and(jnp.logical_and(rel >= 0, rel < WINDOW), rel != _block_rel())
    return jnp.where(jnp.max(off.astype(F32)) > 0.0, jnp.nan, 0.0)


def _group_heads(ref, hk):
    h0 = hk * SWA_GROUP
    return jnp.concatenate([ref[:, (h0 + g) * HEAD_DIM:(h0 + g + 1) * HEAD_DIM] for g in range(SWA_GROUP)], axis=0)


def _group_sinks(sink_ref, hk):
    row = lax.broadcasted_iota(I32, (GROUP_ROWS, 1), 0)
    col = jnp.zeros((GROUP_ROWS, 1), F32) + sink_ref[0, hk * SWA_GROUP]
    for g in range(1, SWA_GROUP):
        col = jnp.where(row >= g * BLOCK, sink_ref[0, hk * SWA_GROUP + g], col)
    return col


def _swa_probs(qg, kh, vis, bias, sink):
    s = _nt(qg, kh) * (HEAD_DIM ** -0.5)
    s = jnp.where(vis, s + bias, NEG)
    m = jnp.maximum(jnp.max(s, axis=-1, keepdims=True), sink)
    p = jnp.exp(s - m)
    ps = jnp.exp(sink - m)
    inv = 1.0 / (jnp.sum(p, axis=-1, keepdims=True) + ps)
    return p * inv, ps * inv


def _swa_fwd(qkv, pos_col, pos_row, bias_t, sinks, hook=None):
    t = qkv.shape[0]
    nb = t // BLOCK
    qw = SWA_HEADS * HEAD_DIM
    kw = SWA_KV_HEADS * HEAD_DIM

    def body(q_ref, kp_ref, kc_ref, vp_ref, vc_ref, pq_ref, pkp_ref, pkc_ref, bias_ref, sink_ref, o_ref):
        b = pl.program_id(0)
        vis = _swa_visible(b, pq_ref, pkp_ref, pkc_ref)
        poison = _swa_poison(b, pq_ref, pkp_ref, pkc_ref)
        k2 = jnp.concatenate([kp_ref[...], kc_ref[...]], axis=0)
        v2 = jnp.concatenate([vp_ref[...], vc_ref[...]], axis=0)
        for hk in range(SWA_KV_HEADS):
            kh = k2[:, hk * HEAD_DIM:(hk + 1) * HEAD_DIM]
            vh = v2[:, hk * HEAD_DIM:(hk + 1) * HEAD_DIM]
            pn, _ = _swa_probs(_group_heads(q_ref, hk), kh, vis, bias_ref[hk], _group_sinks(sink_ref, hk))
            o = _nn(pn.astype(BF), vh) + poison
            for g in range(SWA_GROUP):
                h = hk * SWA_GROUP + g
                o_ref[:, h * HEAD_DIM:(h + 1) * HEAD_DIM] = o[g * BLOCK:(g + 1) * BLOCK]

    prev = lambda b: jnp.maximum(b - 1, 0)
    return _call(
        body, name="swa_fwd", grid=(nb,), out_shape=(S((t, qw), F32),),
        in_specs=[
            pl.BlockSpec((BLOCK, qw), lambda b: (b, 0)),
            pl.BlockSpec((BLOCK, kw), lambda b: (prev(b), qw // kw)),
            pl.BlockSpec((BLOCK, kw), lambda b: (b, qw // kw)),
            pl.BlockSpec((BLOCK, kw), lambda b: (prev(b), qw // kw + 1)),
            pl.BlockSpec((BLOCK, kw), lambda b: (b, qw // kw + 1)),
            pl.BlockSpec((BLOCK, 1), lambda b: (b, 0)),
            pl.BlockSpec((1, BLOCK), lambda b: (0, prev(b))),
            pl.BlockSpec((1, BLOCK), lambda b: (0, b)),
            _resident(bias_t.shape),
            SMEM_SPEC,
        ],
        out_specs=(pl.BlockSpec((BLOCK, qw), lambda b: (b, 0)),),
        sem=("parallel",), args=(qkv, qkv, qkv, qkv, qkv, pos_col, pos_row, pos_row, bias_t, sinks), hook=hook)[0]


HALO = 16


def _conv_taps(z, zh, first):
    tm = z.shape[0]
    zh = jnp.where(first, 0.0, zh)
    row = lax.broadcasted_iota(I32, (tm, 1), 0)
    z1 = jnp.where(row == 0, zh[HALO - 1:HALO, :], pltpu.roll(z, 1, 0))
    z2 = jnp.where(row == 0, zh[HALO - 2:HALO - 1, :], jnp.where(row == 1, zh[HALO - 1:HALO, :], pltpu.roll(z, 2, 0)))
    return z1, z2


def _mix_out_fwd(e, attn, conv_w, w_out, x, hook=None):
    t, d = x.shape
    tm = 256
    hb = tm // HALO
    f32 = lambda ref: ref[...].astype(F32)

    def body(c_ref, b_ref, u_ref, ga_ref, gc_ref, ch_ref, uh_ref, attn_ref, cw_ref, w_ref, x_ref, xo_ref, mg_ref):
        i = pl.program_id(0)
        z = f32(c_ref) * f32(u_ref)
        z1, z2 = _conv_taps(z, f32(ch_ref) * f32(uh_ref), i == 0)
        s = cw_ref[0:1, :] * z2 + cw_ref[1:2, :] * z1 + cw_ref[2:3, :] * z
        conv = f32(b_ref) * s
        merged = (_sigmoid(f32(ga_ref)) * attn_ref[...] + _sigmoid(f32(gc_ref)) * conv).astype(BF)
        mg_ref[...] = merged
        xo_ref[...] = x_ref[...] + _nn(merged, w_ref[...])

    ecol = lambda cb: pl.BlockSpec((tm, d), lambda i: (i, cb))
    halo = lambda cb: pl.BlockSpec((HALO, d), lambda i: (jnp.maximum(i * hb - 1, 0), cb))
    row = pl.BlockSpec((tm, d), lambda i: (i, 0))
    return _call(
        body, name="mix_out_fwd", grid=(t // tm,),
        out_shape=(S((t, d), F32), S((t, d), BF)),
        in_specs=[ecol(0), ecol(1), ecol(2), ecol(3), ecol(4), halo(0), halo(2), row,
                  _resident(conv_w.shape), _resident(w_out.shape), row],
        out_specs=(row, row),
        sem=("parallel",), args=(e, e, e, e, e, e, e, attn, conv_w, w_out, x), hook=hook)


def _mem_kv(mem, gm, wkv):
    m, d = mem.shape

    def body(mem_ref, gm_ref, w_ref, mh_ref, kv_ref):
        n, _ = _rms(mem_ref[...])
        mh = (n * gm_ref[...]).astype(BF)
        mh_ref[...] = mh
        kv_ref[...] = _nn(mh, w_ref[...]).astype(BF)

    return pl.pallas_call(
        body, name="mem_kv", out_shape=(S((m, d), BF), S((m, wkv.shape[1]), BF)),
        compiler_params=_params(),
    )(mem, gm, wkv)


def _xattn_probs(qh, kh):
    s = _nt(qh, kh) * (kh.shape[1] ** -0.5)
    p = jnp.exp(s - jnp.max(s, axis=-1, keepdims=True))
    return p * (1.0 / jnp.sum(p, axis=-1, keepdims=True))


def _xattn_fwd(x, gn, wq, kv, wo):
    t, d = x.shape
    tm = 256
    hd = d // MEM_HEADS

    def body(x_ref, gn_ref, wq_ref, kv_ref, wo_ref, xo_ref, q_ref, o_ref):
        xv = x_ref[...]
        n, _ = _rms(xv)
        q = _nn((n * gn_ref[...]).astype(BF), wq_ref[...]).astype(BF)
        q_ref[...] = q
        outs = []
        for hh in range(MEM_HEADS):
            p = _xattn_probs(q[:, hh * hd:(hh + 1) * hd], kv_ref[:, hh * hd:(hh + 1) * hd])
            outs.append(_nn(p.astype(BF), kv_ref[:, d + hh * hd:d + (hh + 1) * hd]))
        o = jnp.concatenate(outs, axis=1).astype(BF)
        o_ref[...] = o
        xo_ref[...] = xv + _nn(o, wo_ref[...])

    row = pl.BlockSpec((tm, d), lambda i: (i, 0))
    return pl.pallas_call(
        body, name="xattn_fwd", grid=(t // tm,),
        out_shape=(S((t, d), F32), S((t, d), BF), S((t, d), BF)),
        in_specs=[row, _resident((1, d)), _resident(wq.shape), _resident(kv.shape), _resident(wo.shape)],
        out_specs=(row, row, row),
        compiler_params=_params(("parallel",)),
    )(x, gn, wq, kv, wo)


def _ffn_bwd(dxo, x, gn, g, u, wgu, wd, name):
    t, d = x.shape
    f = wd.shape[0]
    tm, fc = 256, 1408

    def body(dxo_ref, x_ref, gn_ref, g_ref, u_ref, wgu_ref, wd_ref, dx_ref, dgn_ref, dgu_ref, a_ref, h_ref, dyh_ref):
        @pl.when(pl.program_id(0) == 0)
        def _():
            dgn_ref[...] = jnp.zeros_like(dgn_ref)
        dxov = dxo_ref[...]
        dyh = (0.5 * dxov).astype(BF)
        dyh_ref[...] = dyh
        n, r = _rms(x_ref[...])
        gnv = gn_ref[...]
        h_ref[...] = (n * gnv).astype(BF)
        dh = jnp.zeros((tm, d), F32)
        for c0 in range(0, f, fc):
            gv = g_ref[:, c0:c0 + fc].astype(F32)
            uv = u_ref[:, c0:c0 + fc].astype(F32)
            da = _nt(dyh, wd_ref[c0:c0 + fc, :])
            sg = _sigmoid(gv)
            silu = gv * sg
            a_ref[:, c0:c0 + fc] = (silu * uv).astype(BF)
            dg = (da * uv * (sg * (1.0 + gv * (1.0 - sg)))).astype(BF)
            du = (da * silu).astype(BF)
            dgu_ref[:, c0:c0 + fc] = dg
            dgu_ref[:, f + c0:f + c0 + fc] = du
            dh = dh + _nt(dg, wgu_ref[:, c0:c0 + fc]) + _nt(du, wgu_ref[:, f + c0:f + c0 + fc])
        dx, dgn = _rms_bwd(dh, n, r, gnv)
        dx_ref[...] = dxov + dx
        dgn_ref[...] += dgn

    row = pl.BlockSpec((tm, d), lambda i: (i, 0))
    frow = pl.BlockSpec((tm, f), lambda i: (i, 0))
    return _call(
        body, name=name, grid=(t // tm,),
        out_shape=(S((t, d), F32), S((1, d), F32), S((t, 2 * f), BF), S((t, f), BF), S((t, d), BF), S((t, d), BF)),
        in_specs=[row, row, _resident((1, d)), frow, frow, _resident(wgu.shape), _resident(wd.shape)],
        out_specs=(row, _acc_spec((1, d)), pl.BlockSpec((tm, 2 * f), lambda i: (i, 0)), frow, row, row),
        sem=("arbitrary",), args=(dxo, x, gn, g, u, wgu, wd))


def _dw_pair(pairs, tn, name, kind, hook=None):
    npairs = len(pairs)
    t, ka = pairs[0][0].shape
    nb = pairs[0][1].shape[1]
    tt = 2048 if npairs == 1 else 1024
    nt, nj = t // tt, nb // tn
    col = kind == "col"
    rh = ka // 2 if col else ka // 8
    tile = (rh, tn) if col else (4, rh, tn)
    half = (rh, nb) if col else (4, rh, nb)
    lead = (slice(None),) * (len(tile) - 1)

    def body(*refs):
        ins, mines, sibs = refs[:2 * npairs], refs[2 * npairs:3 * npairs], refs[3 * npairs:4 * npairs]
        acc_ref, stage, ssem, rsem = refs[4 * npairs:]
        j, k = pl.program_id(0), pl.program_id(1)
        x, y, c = _place()
        sibling = (x, y, 1 - c)

        def send(p, slot, jj):
            dst = sibs[p].at[lead + (pl.ds(pl.multiple_of(jj * tn, 128), tn),)]
            return _remote(stage.at[slot], dst, ssem.at[slot], rsem.at[p], sibling)

        def rows(s, whose):
            return acc_ref[pl.ds(pl.multiple_of(s * 2 * rh + whose * rh, 16), rh), :].astype(BF)

        def step(p):
            a_ref, b_ref, mine_ref = ins[2 * p], ins[2 * p + 1], mines[p]

            @pl.when(k == 0)
            def _():
                acc_ref[...] = jnp.zeros_like(acc_ref)
            acc_ref[...] += _tn(a_ref[...], b_ref[...])

            @pl.when(k == nt - 1)
            def _():
                slot = j % 2

                @pl.when(j >= 2)
                def _():
                    send(p, slot, 0).wait_send()
                if col:
                    mine_ref[...] = rows(0, c)
                    stage[slot] = rows(0, 1 - c)
                else:
                    for s in range(4):
                        mine_ref[s] = rows(s, c)
                        stage[slot, s] = rows(s, 1 - c)
                send(p, slot, j - p * nj).start()

        for p in range(npairs):
            pl.when(j // nj == p)(functools.partial(step, p))

        @pl.when(jnp.logical_and(j == npairs * nj - 1, k == nt - 1))
        def _():
            for jj in range(max(npairs * nj - 2, 0), npairs * nj):
                send(0, jj % 2, 0).wait_send()
            for p in range(npairs):
                _remote(sibs[p], sibs[p], ssem.at[0], rsem.at[p], sibling).wait_recv()

    in_specs, args = [], []
    for p, (a, b) in enumerate(pairs):
        on = lambda j, p=p: j // nj == p
        in_specs += [pl.BlockSpec((tt, ka), lambda j, k, on=on: (jnp.where(on(j), k, 0), 0)),
                     pl.BlockSpec((tt, tn), lambda j, k, on=on, p=p: (jnp.where(on(j), k, 0), jnp.clip(j - p * nj, 0, nj - 1)))]
        args += [a, b]
    mine_spec = lambda p: pl.BlockSpec(tile, (lambda j, k: (0, jnp.clip(j - p * nj, 0, nj - 1))) if col
                                       else (lambda j, k: (0, 0, jnp.clip(j - p * nj, 0, nj - 1))))
    res = _call(
        body, name=name, grid=(npairs * nj, nt), out_shape=(S(half, BF),) * (2 * npairs),
        in_specs=in_specs, out_specs=tuple(mine_spec(p) for p in range(npairs)) + (ANY,) * npairs,
        scratch_shapes=[pltpu.VMEM((ka, tn), F32), pltpu.VMEM((2,) + tile, BF), pltpu.SemaphoreType.DMA((2,)),
                        pltpu.SemaphoreType.DMA((npairs,))],
        sem=("arbitrary", "arbitrary"), args=args, hook=hook)
    return [(res[p], res[npairs + p]) for p in range(npairs)]


def _xattn_bwd(dxo, x, gn, q, kv, wq, wo, mh, mem, wkv, hook=None):
    t, d = x.shape
    tm = 256
    hd = d // MEM_HEADS
    nkv = kv.shape[0]
    rows, cols = wkv.shape
    rh = rows // 2

    def body(dxo_ref, x_ref, gn_ref, q_ref, kv_ref, wq_ref, wo_ref, mh_ref, mem_ref, wkv_ref,
             dx_ref, dgn_ref, dxh_ref, h_ref, dq_ref, mine_ref, sib_ref, dgm_ref, dkv_ref, whole, ssem, rsem):
        @pl.when(pl.program_id(0) == 0)
        def _():
            dgn_ref[...] = jnp.zeros_like(dgn_ref)
            dkv_ref[...] = jnp.zeros_like(dkv_ref)
        dxov = dxo_ref[...]
        dxh = dxov.astype(BF)
        dxh_ref[...] = dxh
        do = _nt(dxh, wo_ref[...]).astype(BF)
        dqs = []
        for hh in range(MEM_HEADS):
            lo, hi = hh * hd, (hh + 1) * hd
            qh = q_ref[:, lo:hi]
            kh = kv_ref[:, lo:hi]
            vh = kv_ref[:, d + lo:d + hi]
            doh = do[:, lo:hi]
            p = _xattn_probs(qh, kh)
            dp = _nt(doh, vh)
            ds = (p * (dp - jnp.sum(p * dp, axis=-1, keepdims=True)) * (hd ** -0.5)).astype(BF)
            dqs.append(_nn(ds, kh))
            dkv_ref[:, lo:hi] += _tn(ds, qh)
            dkv_ref[:, d + lo:d + hi] += _tn(p.astype(BF), doh)
        dq = jnp.concatenate(dqs, axis=1).astype(BF)
        dq_ref[...] = dq
        n, r = _rms(x_ref[...])
        gnv = gn_ref[...]
        h_ref[...] = (n * gnv).astype(BF)
        dx, dgn = _rms_bwd(_nt(dq, wq_ref[...]), n, r, gnv)
        dx_ref[...] = dxov + dx
        dgn_ref[...] += dgn

        @pl.when(pl.program_id(0) == pl.num_programs(0) - 1)
        def _():
            dkvb = dkv_ref[...].astype(BF)
            dmh = _nt(dkvb, wkv_ref[...])
            nm, _ = _rms(mem_ref[...])
            px, py, c = _place()
            whole[...] = _tn(mh_ref[...], dkvb).astype(BF)
            cp = _remote(whole.at[pl.ds(pl.multiple_of((1 - c) * rh, 16), rh), :], sib_ref, ssem, rsem, (px, py, 1 - c))
            cp.start()
            mine_ref[...] = whole[pl.ds(pl.multiple_of(c * rh, 16), rh), :]
            dgm_ref[...] = jnp.sum(dmh * nm, axis=0, keepdims=True)
            cp.wait()

    row = pl.BlockSpec((tm, d), lambda i: (i, 0))
    dx, dgn, dxh, h, dq, mine, sib, dgm = _call(
        body, name="xattn_bwd", grid=(t // tm,),
        out_shape=(S((t, d), F32), S((1, d), F32), S((t, d), BF), S((t, d), BF), S((t, d), BF),
                   S((rh, cols), BF), S((rh, cols), BF), S((1, d), F32)),
        in_specs=[row, row, _resident((1, d)), row, _resident(kv.shape), _resident(wq.shape), _resident(wo.shape),
                  _resident(mh.shape), _resident(mem.shape), _resident(wkv.shape)],
        out_specs=(row, _acc_spec((1, d)), row, row, row, _acc_spec((rh, cols)), ANY, _acc_spec((1, d))),
        scratch_shapes=[pltpu.VMEM((nkv, 2 * d), F32), pltpu.VMEM((rows, cols), BF),
                        pltpu.SemaphoreType.DMA, pltpu.SemaphoreType.DMA],
        sem=("arbitrary",), args=(dxo, x, gn, q, kv, wq, wo, mh, mem, wkv), hook=hook)
    return dx, dgn, dxh, h, dq, (mine, sib), dgm


def _mix_out_bwd(dxo, e, attn, conv_w, w_out, nqkv, hook=None):
    t, d = attn.shape
    tm = 256
    nt = t // tm
    f32 = lambda ref: ref[...].astype(F32)

    def body(dxo_ref, dxn_ref, c_ref, b_ref, u_ref, ga_ref, gc_ref, ch_ref, uh_ref, bn_ref, gcn_ref,
             attn_ref, cw_ref, w_ref, dattn_ref, de_ref, dcw_ref, dxh_ref):
        i = pl.program_id(0)

        @pl.when(i == 0)
        def _():
            dcw_ref[...] = jnp.zeros_like(dcw_ref)
        dxh = dxo_ref[...].astype(BF)
        dxh_ref[...] = dxh
        w = w_ref[...]
        dm = _nt(dxh, w)
        dmn = _nt(dxn_ref[...].astype(BF), w)
        cv, bv, uv = f32(c_ref), f32(b_ref), f32(u_ref)
        sga = _sigmoid(f32(ga_ref))
        sgc = _sigmoid(f32(gc_ref))
        z = cv * uv
        z1, z2 = _conv_taps(z, f32(ch_ref) * f32(uh_ref), i == 0)
        w0, w1, w2 = cw_ref[0:1, :], cw_ref[1:2, :], cw_ref[2:3, :]
        s = w0 * z2 + w1 * z1 + w2 * z
        av = attn_ref[...]
        dattn_ref[...] = (dm * sga).astype(BF)
        dconv = dm * sgc
        ds = dconv * bv
        dsn = jnp.where(i == nt - 1, 0.0, dmn * _sigmoid(gcn_ref[0:8, :].astype(F32)) * bn_ref[0:8, :].astype(F32))
        row = lax.broadcasted_iota(I32, (tm, 1), 0)
        dsp1 = jnp.where(row == tm - 1, dsn[0:1, :], pltpu.roll(ds, tm - 1, 0))
        dsp2 = jnp.where(row == tm - 2, dsn[0:1, :], jnp.where(row == tm - 1, dsn[1:2, :], pltpu.roll(ds, tm - 2, 0)))
        dz = w2 * ds + w1 * dsp1 + w0 * dsp2
        de_ref[:, nqkv:nqkv + d] = (dz * uv).astype(BF)
        de_ref[:, nqkv + d:nqkv + 2 * d] = (dconv * s).astype(BF)
        de_ref[:, nqkv + 2 * d:nqkv + 3 * d] = (dz * cv).astype(BF)
        de_ref[:, nqkv + 3 * d:nqkv + 4 * d] = (dm * av * sga * (1.0 - sga)).astype(BF)
        de_ref[:, nqkv + 4 * d:nqkv + 5 * d] = (dm * (bv * s) * sgc * (1.0 - sgc)).astype(BF)
        dcw_ref[0:1, :] += jnp.sum(ds * z2, axis=0, keepdims=True)
        dcw_ref[1:2, :] += jnp.sum(ds * z1, axis=0, keepdims=True)
        dcw_ref[2:3, :] += jnp.sum(ds * z, axis=0, keepdims=True)

    ecol = lambda cb: pl.BlockSpec((tm, d), lambda i: (i, cb))
    prev = lambda cb: pl.BlockSpec((HALO, d), lambda i: (jnp.maximum(i * (tm // HALO) - 1, 0), cb))
    nxt = lambda rows, cb: pl.BlockSpec((rows, d), lambda i: (jnp.minimum((i + 1) * (tm // rows), t // rows - 1), cb))
    row = pl.BlockSpec((tm, d), lambda i: (i, 0))
    return _call(
        body, name="mix_out_bwd", grid=(nt,),
        out_shape=(S((t, d), BF), S((t, nqkv + 5 * d), BF), S((8, d), F32), S((t, d), BF)),
        in_specs=[row, nxt(8, 0), ecol(0), ecol(1), ecol(2), ecol(3), ecol(4), prev(0), prev(2), nxt(HALO, 1), nxt(HALO, 4),
                  row, _resident(conv_w.shape), _resident(w_out.shape)],
        out_specs=(row, pl.BlockSpec((tm, nqkv + 5 * d), lambda i: (i, 0)), _acc_spec((8, d)), row),
        sem=("arbitrary",), args=(dxo, dxo, e, e, e, e, e, e, e, e, e, attn, conv_w, w_out), hook=hook)


def _swa_bwd(qkv, dattn, pos_col, pos_row, bias_t, sinks, dp, hook=None):
    t = qkv.shape[0]
    nb = t // BLOCK
    qw = SWA_HEADS * HEAD_DIM
    kw = SWA_KV_HEADS * HEAD_DIM

    def body(q_ref, kp_ref, kc_ref, vp_ref, vc_ref, do_ref, pq_ref, pkp_ref, pkc_ref, bias_ref, sink_ref,
             dp_in, dp_ref, gb_ref, dsk_ref, carry_ref, dq_ref):
        b = pl.program_id(0)

        @pl.when(b == 0)
        def _():
            gb_ref[...] = jnp.zeros_like(gb_ref)
            dsk_ref[...] = jnp.zeros_like(dsk_ref)
            carry_ref[...] = jnp.zeros_like(carry_ref)
            dq_ref[...] = jnp.zeros_like(dq_ref)
        dp_ref[:, 0:qw] = dq_ref[...]

        @pl.when(b < nb)
        def _():
            vis = _swa_visible(b, pq_ref, pkp_ref, pkc_ref)
            k2 = jnp.concatenate([kp_ref[...], kc_ref[...]], axis=0)
            v2 = jnp.concatenate([vp_ref[...], vc_ref[...]], axis=0)
            for hk in range(SWA_KV_HEADS):
                lo, hi = hk * HEAD_DIM, (hk + 1) * HEAD_DIM
                kh = k2[:, lo:hi]
                vh = v2[:, lo:hi]
                qg = _group_heads(q_ref, hk)
                dog = _group_heads(do_ref, hk)
                pn, psn = _swa_probs(qg, kh, vis, bias_ref[hk], _group_sinks(sink_ref, hk))
                dp = _nt(dog, vh)
                delta = jnp.sum(pn * dp, axis=-1, keepdims=True)
                ds = pn * (dp - delta)
                gb_ref[hk] += ds
                dsk_ref[hk] += -psn * delta
                dsb = (ds * (HEAD_DIM ** -0.5)).astype(BF)
                dqg = _nn(dsb, kh).astype(BF)
                for g in range(SWA_GROUP):
                    h = hk * SWA_GROUP + g
                    dq_ref[:, h * HEAD_DIM:(h + 1) * HEAD_DIM] = dqg[g * BLOCK:(g + 1) * BLOCK]
                dk = _tn(dsb, qg)
                dv = _tn(pn.astype(BF), dog)
                dp_ref[:, qw + lo:qw + hi] = (carry_ref[:, lo:hi] + dk[0:BLOCK]).astype(BF)
                dp_ref[:, qw + kw + lo:qw + kw + hi] = (carry_ref[:, kw + lo:kw + hi] + dv[0:BLOCK]).astype(BF)
                carry_ref[:, lo:hi] = dk[BLOCK:2 * BLOCK]
                carry_ref[:, kw + lo:kw + hi] = dv[BLOCK:2 * BLOCK]

        @pl.when(b == nb)
        def _():
            dp_ref[:, qw:qw + 2 * kw] = carry_ref[...].astype(BF)

    cur = lambda b: jnp.minimum(b, nb - 1)
    prev = lambda b: jnp.maximum(cur(b) - 1, 0)
    return _call(
        body, name="swa_bwd", grid=(nb + 1,),
        out_shape=(S(dp.shape, BF), S((SWA_KV_HEADS, GROUP_ROWS, 2 * BLOCK), F32), S((SWA_KV_HEADS, GROUP_ROWS, 1), F32)),
        in_specs=[
            pl.BlockSpec((BLOCK, qw), lambda b: (cur(b), 0)),
            pl.BlockSpec((BLOCK, kw), lambda b: (prev(b), qw // kw)),
            pl.BlockSpec((BLOCK, kw), lambda b: (cur(b), qw // kw)),
            pl.BlockSpec((BLOCK, kw), lambda b: (prev(b), qw // kw + 1)),
            pl.BlockSpec((BLOCK, kw), lambda b: (cur(b), qw // kw + 1)),
            pl.BlockSpec((BLOCK, qw), lambda b: (cur(b), 0)),
            pl.BlockSpec((BLOCK, 1), lambda b: (cur(b), 0)),
            pl.BlockSpec((1, BLOCK), lambda b: (0, prev(b))),
            pl.BlockSpec((1, BLOCK), lambda b: (0, cur(b))),
            _resident(bias_t.shape),
            SMEM_SPEC,
            ANY,
        ],
        out_specs=(
            pl.BlockSpec((BLOCK, qw + 2 * kw), lambda b: (jnp.maximum(b - 1, 0), 0)),
            _acc_spec((SWA_KV_HEADS, GROUP_ROWS, 2 * BLOCK)),
            _acc_spec((SWA_KV_HEADS, GROUP_ROWS, 1)),
        ),
        scratch_shapes=[pltpu.VMEM((BLOCK, 2 * kw), F32), pltpu.VMEM((BLOCK, qw), BF)],
        aliases={11: 0}, sem=("arbitrary",),
        args=(qkv, qkv, qkv, qkv, qkv, dattn, pos_col, pos_row, pos_row, bias_t, sinks, dp), hook=hook)


def _bias_reduce(gb, dsk):
    def body(gb_ref, dsk_ref, drb_ref, dsink_ref):
        bucket = _t5_bucket(_block_rel())
        for b in range(REL_BUCKETS):
            mask = bucket == b
            for h in range(SWA_HEADS):
                drb_ref[h, b] = jnp.sum(jnp.where(mask, gb_ref[h], 0.0))
        for h in range(SWA_HEADS):
            dsink_ref[0, h] = jnp.sum(dsk_ref[h])

    return pl.pallas_call(
        body, name="bias_reduce", out_shape=(S((SWA_HEADS, REL_BUCKETS), F32), S((1, SWA_HEADS), F32)),
        in_specs=[VMEM_SPEC, VMEM_SPEC], out_specs=(SMEM_SPEC, SMEM_SPEC),
    )(gb, dsk)


def _mix_in_bwd(dp, x, gn, w_in, dxo):
    t, d = x.shape
    tm = 256
    npr = dp.shape[1]

    def body(dp_ref, x_ref, gn_ref, w_ref, dxo_ref, dx_ref, dgn_ref, h_ref):
        @pl.when(pl.program_id(0) == 0)
        def _():
            dgn_ref[...] = jnp.zeros_like(dgn_ref)
        dh = jnp.zeros((tm, d), F32)
        for c0 in range(0, npr, 1024):
            c1 = min(c0 + 1024, npr)
            dh = dh + _nt(dp_ref[:, c0:c1], w_ref[:, c0:c1])
        n, r = _rms(x_ref[...])
        gnv = gn_ref[...]
        h_ref[...] = (n * gnv).astype(BF)
        dx, dgn = _rms_bwd(dh, n, r, gnv)
        dx_ref[...] = dxo_ref[...] + dx
        dgn_ref[...] += dgn

    row = pl.BlockSpec((tm, d), lambda i: (i, 0))
    return pl.pallas_call(
        body, name="mix_in_bwd", grid=(t // tm,),
        out_shape=(S((t, d), F32), S((1, d), F32), S((t, d), BF)),
        in_specs=[pl.BlockSpec((tm, npr), lambda i: (i, 0)), row, _resident((1, d)), _resident(w_in.shape), row],
        out_specs=(row, _acc_spec((1, d)), row),
        compiler_params=_params(("arbitrary",)),
    )(dp, x, gn, w_in, dxo)


CAST_STEPS = 4


def _local_prelude(arrays, rel_bias_t, name, hook=None):
    n = len(arrays)

    def body(*refs):
        rb_ref, table_ref = refs[n], refs[-1]
        for src, dst in zip(refs[:n], refs[n + 1:-1]):
            dst[...] = src[...].astype(BF)

        @pl.when(pl.program_id(0) == 0)
        def _():
            _bias_table(rb_ref, table_ref)

    blocks = [pl.BlockSpec((a.shape[0] // CAST_STEPS, a.shape[1]), lambda i: (i, 0)) for a in arrays]
    table = (SWA_HEADS, BLOCK, 2 * BLOCK)
    res = _call(body, name=name, grid=(CAST_STEPS,), out_shape=tuple(S(a.shape, BF) for a in arrays) + (S(table, F32),),
                in_specs=blocks + [SMEM_SPEC], out_specs=tuple(blocks) + (_acc_spec(table),), sem=("arbitrary",),
                args=list(arrays) + [rel_bias_t], hook=hook)
    return res[:n], res[n]


PAIR_STEPS = 4


def _add_bf16(pairs, name):
    def body(*refs):
        ins, outs = refs[:2 * len(pairs)], refs[2 * len(pairs):]
        for q, o_ref in enumerate(outs):
            o_ref[...] = (ins[2 * q][...].astype(F32) + ins[2 * q + 1][...].astype(F32)).astype(BF)

    in_specs, out_specs, out_shape, args = [], [], [], []
    for a, b in pairs:
        rows, cols = a.shape
        blk = pl.BlockSpec((rows // PAIR_STEPS, cols), lambda i: (i, 0))
        in_specs += [blk, blk]
        out_specs.append(blk)
        out_shape.append(S((rows, cols), BF))
        args += [a, b]
    return pl.pallas_call(body, name=name, grid=(PAIR_STEPS,), out_shape=tuple(out_shape), in_specs=in_specs,
                          out_specs=tuple(out_specs), compiler_params=_params(("parallel",)))(*args)


def _adam_update(w, g, m, v):
    mn = ADAM_B1 * m + (1.0 - ADAM_B1) * g
    vn = ADAM_B2 * v + (1.0 - ADAM_B2) * (g * g)
    m_hat = mn / (1.0 - ADAM_B1 ** ADAM_STEP)
    v_hat = vn / (1.0 - ADAM_B2 ** ADAM_STEP)
    return -ADAM_LR * (m_hat / (jnp.sqrt(v_hat) + ADAM_EPS) + ADAM_WD * w), mn, vn


def _adamw(quads, name, steps):
    def body(*refs):
        ins, outs = refs[:4 * len(quads)], refs[4 * len(quads):]
        for q in range(len(quads)):
            w_ref, g_ref, m_ref, v_ref = ins[4 * q:4 * q + 4]
            echo_ref, d_ref, mo_ref, vo_ref = outs[4 * q:4 * q + 4]
            gv = g_ref[...]
            echo_ref[...] = gv
            d_ref[...], mo_ref[...], vo_ref[...] = _adam_update(w_ref[...], gv, m_ref[...], v_ref[...])

    in_specs, out_shape, args = [], [], []
    for quad in quads:
        rows, cols = quad[0].shape
        in_specs += [pl.BlockSpec((rows // steps, cols), lambda i: (i, 0))] * 4
        out_shape += [S((rows, cols), F32)] * 4
        args += list(quad)
    res = pl.pallas_call(body, name=name, grid=(steps,), out_shape=tuple(out_shape), in_specs=in_specs,
                         out_specs=tuple(in_specs), compiler_params=_params(("parallel",)))(*args)
    return [res[4 * q:4 * q + 4] for q in range(len(quads))]


def _place():
    x, y, c = lax.axis_index("x"), lax.axis_index("y"), lax.axis_index("c")
    return x, y, c


OTHER_CHIPS = ((1, 0), (0, 1), (1, 1))


def _flip(v, f):
    return 1 - v if f else v


def _remote(src, dst, ssem, rsem, dev):
    return pltpu.make_async_remote_copy(src_ref=src, dst_ref=dst, send_sem=ssem, recv_sem=rsem,
                                        device_id=dev, device_id_type=MESH)


class _Both:
    def __init__(self, hooks):
        self.hooks = hooks
        self.ins = [a for h in hooks for a in h.ins]
        self.out_shape = tuple(o for h in hooks for o in h.out_shape)
        self.scratch = [x for h in hooks for x in h.scratch]

    def _each(self, ins, outs, scr):
        i = o = s = 0
        for h in self.hooks:
            ni, no, ns = len(h.ins), len(h.out_shape), len(h.scratch)
            yield h, ins[i:i + ni], outs[o:o + no], scr[s:s + ns]
            i, o, s = i + ni, o + no, s + ns

    def start(self, ins, outs, scr):
        for h, *refs in self._each(ins, outs, scr):
            h.start(*refs)

    def relay(self, ins, outs, scr):
        for h, *refs in self._each(ins, outs, scr):
            h.relay(*refs)

    def finish(self, ins, outs, scr):
        for h, *refs in self._each(ins, outs, scr):
            h.finish(*refs)

    @property
    def lead(self):
        return max(h.lead for h in self.hooks)

    @property
    def results(self):
        return [h.results for h in self.hooks]

    @results.setter
    def results(self, res):
        for h, _, mine, _ in self._each((), tuple(res), ()):
            h.results = mine


class _GatherHook:
    def __init__(self, shards, kinds, lead=2):
        self.ins, self.kinds, self.lead, n = list(shards), list(kinds), lead, len(shards)
        self.out_shape = tuple(
            S((w.shape[0], 4 * w.shape[1]), BF) if k == "col" else S((4,) + w.shape, BF) for w, k in zip(shards, kinds))
        dma = pltpu.SemaphoreType.DMA
        self.scratch = ([dma((n, 3)) for _ in range(5)] + [dma((n,)), dma((n,))]
                        + [pltpu.VMEM((3, w.shape[0] // 2, w.shape[1]), BF) for w in shards]
                        + [pltpu.VMEM(w.shape, BF) for w in shards])

    def _window(self, outs, i, s, half):
        rows, cols = self.ins[i].shape
        rh = rows // 2
        start = pl.multiple_of(half * rh, 16)
        if self.kinds[i] == "col":
            return outs[i].at[pl.ds(start, rh), pl.ds(pl.multiple_of(s * cols, 128), cols)]
        return outs[i].at[s, pl.ds(start, rh), :]

    def _copies(self, ins, outs, scr):
        n = len(ins)
        ssem, rsem, fssem, frsem, ksem, lsem, osem = scr[:7]
        land, own = scr[7:7 + n], scr[7 + n:7 + 2 * n]
        x, y, c = _place()
        sibling = (x, y, 1 - c)
        loads, stores, sends, forwards, keeps, passed = [], [], [], [], [], []
        for i in range(n):
            rows, cols = self.ins[i].shape
            rh = rows // 2
            mine = (outs[i].at[:, pl.ds(pl.multiple_of((2 * x + y) * cols, 128), cols)] if self.kinds[i] == "col"
                    else outs[i].at[2 * x + y])
            loads.append(functools.partial(pltpu.make_async_copy, ins[i], own[i], lsem.at[i]))
            stores.append(functools.partial(pltpu.make_async_copy, own[i], mine, osem.at[i]))
            src = ins[i].at[pl.ds(pl.multiple_of(c * rh, 16), rh), :]
            for j, (fx, fy) in enumerate(OTHER_CHIPS):
                px, py = _flip(x, fx), _flip(y, fy)
                sends.append(functools.partial(_remote, src, land[i].at[j], ssem.at[i, j], rsem.at[i, j], (px, py, c)))
                here = self._window(outs, i, 2 * px + py, c)
                forwards.append(functools.partial(_remote, land[i].at[j], here, fssem.at[i, j], frsem.at[i, j], sibling))
                keeps.append(functools.partial(pltpu.make_async_copy, land[i].at[j], here, ksem.at[i, j]))
                there = self._window(outs, i, 2 * px + py, 1 - c)
                passed.append(functools.partial(_remote, there, there, fssem.at[i, j], frsem.at[i, j], sibling))
        return loads, stores, sends, forwards, keeps, passed

    def start(self, ins, outs, scr):
        loads, _, sends, _, _, _ = self._copies(ins, outs, scr)
        for make in sends + loads:
            make().start()

    def relay(self, ins, outs, scr):
        loads, stores, sends, forwards, keeps, _ = self._copies(ins, outs, scr)
        for load, store in zip(loads, stores):
            load().wait()
            store().start()
        for send, forward, keep in zip(sends, forwards, keeps):
            send().wait_recv()
            forward().start()
            keep().start()

    def finish(self, ins, outs, scr):
        _, stores, sends, forwards, keeps, passed = self._copies(ins, outs, scr)
        for make in passed:
            make().wait_recv()
        for make in sends + forwards:
            make().wait_send()
        for make in keeps + stores:
            make().wait()


class _ChipsHook:
    def __init__(self, parts, kinds):
        self.ins, self.kinds, n = list(parts), list(kinds), len(parts)
        self.out_shape = tuple(
            S((4, p.shape[0], p.shape[1] // 4), BF) if k == "col" else S(p.shape, BF) for p, k in zip(parts, kinds))
        dma = pltpu.SemaphoreType.DMA
        self.scratch = ([dma((n, 3)), dma((n, 3)), dma((n,)), dma((n,))]
                        + [pltpu.VMEM(o.shape[1:], BF) for o in self.out_shape])

    def _slab(self, ins, i, s):
        _, rows, cols = self.out_shape[i].shape
        if self.kinds[i] == "col":
            return ins[i].at[:, pl.ds(pl.multiple_of(s * cols, 128), cols)]
        return ins[i].at[s]

    def _copies(self, ins, outs, scr):
        ssem, rsem, lsem, osem = scr[:4]
        own = scr[4:]
        x, y, c = _place()
        loads, stores, sends = [], [], []
        for i in range(len(ins)):
            loads.append(functools.partial(pltpu.make_async_copy, self._slab(ins, i, 2 * x + y), own[i], lsem.at[i]))
            stores.append(functools.partial(pltpu.make_async_copy, own[i], outs[i].at[3], osem.at[i]))
            for j, (fx, fy) in enumerate(OTHER_CHIPS):
                px, py = _flip(x, fx), _flip(y, fy)
                sends.append(functools.partial(_remote, self._slab(ins, i, 2 * px + py), outs[i].at[j], ssem.at[i, j],
                                               rsem.at[i, j], (px, py, c)))
        return loads, stores, sends

    def start(self, ins, outs, scr):
        loads, _, sends = self._copies(ins, outs, scr)
        for make in sends + loads:
            make().start()

    lead = 2

    def relay(self, ins, outs, scr):
        loads, stores, _ = self._copies(ins, outs, scr)
        for load, store in zip(loads, stores):
            load().wait()
            store().start()

    def finish(self, ins, outs, scr):
        _, stores, sends = self._copies(ins, outs, scr)
        for make in sends + stores:
            make().wait()


SHARE_STEPS = 2


def _sum_share(slab_list, name, hook=None):
    n = len(slab_list)
    geom = [(sl.shape[1], sl.shape[1] // SHARE_STEPS, sl.shape[2]) for sl in slab_list]

    def body(*refs):
        ins, outs, scr = refs[:n], refs[n:2 * n], refs[2 * n:]
        i = pl.program_id(0)
        x, y, c = _place()
        sibling = (x, y, 1 - c)

        def copies(q, k):
            rh, tr, _ = geom[q]
            stage, lsem, ssem, rsem = scr[4 * q:4 * q + 4]
            dst = outs[q].at[pl.ds(pl.multiple_of(c * rh + k * tr, 8), tr), :]
            return (pltpu.make_async_copy(stage.at[k], dst, lsem.at[k]),
                    _remote(stage.at[k], dst, ssem.at[k], rsem, sibling))

        for q in range(n):
            acc = ins[q][3].astype(F32)
            for k in range(3):
                acc = acc + ins[q][k].astype(F32)
            scr[4 * q][i] = acc
            for cp in copies(q, i):
                cp.start()

        @pl.when(i == SHARE_STEPS - 1)
        def _():
            for q in range(n):
                rh = geom[q][0]
                for k in range(SHARE_STEPS):
                    local, remote = copies(q, k)
                    local.wait()
                    remote.wait_send()
                got = outs[q].at[pl.ds(pl.multiple_of((1 - c) * rh, 8), rh), :]
                _remote(got, got, scr[4 * q + 2].at[0], scr[4 * q + 3], sibling).wait_recv()

    dma = pltpu.SemaphoreType.DMA
    scratch = []
    for rh, tr, cols in geom:
        scratch += [pltpu.VMEM((SHARE_STEPS, tr, cols), F32), dma((SHARE_STEPS,)), dma((SHARE_STEPS,)), dma]
    return _call(
        body, name=name, grid=(SHARE_STEPS,), out_shape=tuple(S((2 * rh, cols), F32) for rh, _, cols in geom),
        in_specs=[pl.BlockSpec((4, tr, cols), lambda i: (0, i, 0)) for _, tr, cols in geom], out_specs=(ANY,) * n,
        scratch_shapes=scratch, sem=("arbitrary",), args=list(slab_list), hook=hook)


class _SmallSumHook:
    def __init__(self, buf):
        self.ins, self.out_shape = [buf], (S(buf.shape, F32),)
        dma = pltpu.SemaphoreType.DMA
        self.scratch = [pltpu.VMEM((8,) + buf.shape, F32), pltpu.VMEM(buf.shape, F32), dma((7,)), dma((7,)), dma]

    def _sends(self, scr):
        slots, _, ssem, rsem, _ = scr
        x, y, c = _place()
        me = 4 * x + 2 * y + c
        for r in range(1, 8):
            px, py, pc = _flip(x, (r >> 2) & 1), _flip(y, (r >> 1) & 1), _flip(c, r & 1)
            yield (functools.partial(_remote, slots.at[me], slots.at[me], ssem.at[r - 1], rsem.at[r - 1], (px, py, pc)),
                   functools.partial(_remote, slots.at[me], slots.at[4 * px + 2 * py + pc], ssem.at[r - 1],
                                     rsem.at[r - 1], (px, py, pc)))

    def start(self, ins, outs, scr):
        slots, _, _, _, lsem = scr
        x, y, c = _place()
        load = pltpu.make_async_copy(ins[0], slots.at[4 * x + 2 * y + c], lsem)
        load.start()
        load.wait()
        for send, _ in self._sends(scr):
            send().start()

    lead = 0

    def relay(self, ins, outs, scr):
        pass

    def finish(self, ins, outs, scr):
        slots, total, _, _, lsem = scr
        for _, arrival in self._sends(scr):
            arrival().wait_recv()
        for send, _ in self._sends(scr):
            send().wait_send()
        acc = slots[0]
        for k in range(1, 8):
            acc = acc + slots[k]
        total[...] = acc
        store = pltpu.make_async_copy(total, outs[0], lsem)
        store.start()
        store.wait()


BIG = ("ffn1_w_gu", "ffn1_w_down", "w_in", "w_out", "xattn_wq", "xattn_wkv", "xattn_wo", "ffn2_w_gu", "ffn2_w_down")
KIND = {"ffn1_w_gu": "col", "ffn1_w_down": "row", "w_in": "col", "w_out": "row", "xattn_wq": "row",
        "xattn_wkv": "col", "xattn_wo": "row", "ffn2_w_gu": "col", "ffn2_w_down": "row"}
WEIGHTS = ("rel_bias", "ffn1_norm", "ffn1_w_gu", "ffn1_w_down", "mix_norm", "w_in", "sinks", "conv_w", "w_out",
           "xattn_norm", "mem_norm", "xattn_wq", "xattn_wkv", "xattn_wo", "ffn2_norm", "ffn2_w_gu", "ffn2_w_down",
           "final_norm")
SMALL_ROWS = 16
GAIN_ROW = {"ffn1_norm": 0, "mix_norm": 1, "xattn_norm": 2, "mem_norm": 3, "ffn2_norm": 4, "final_norm": 5}
CONV_ROW, SINK_ROW, BIAS_ROW, LOSS_ROW = 6, 9, 10, 11
TAP_ROWS = 8


def _rows_block(rows, d):
    buf = jnp.zeros((SMALL_ROWS, d), F32)
    for r, v in rows.items():
        buf = lax.dynamic_update_slice(buf, v if v.ndim == 2 else v.reshape(1, -1), (r, 0))
    return buf


def _adamw_small(gsum, conv_g, bias_g, small):
    names = list(small)

    def grad(k, gsum_ref, conv_ref, bias_ref):
        if k in GAIN_ROW:
            return gsum_ref[GAIN_ROW[k]:GAIN_ROW[k] + 1, :]
        if k == "sinks":
            return gsum_ref[SINK_ROW:SINK_ROW + 1, 0:small[k][0].shape[1]]
        return conv_ref[...] if k == "conv_w" else bias_ref[...]

    def body(gsum_ref, conv_ref, bias_ref, *refs):
        ins, outs = refs[:3 * len(names)], refs[3 * len(names):]
        for q, k in enumerate(names):
            w_ref, m_ref, v_ref = ins[3 * q:3 * q + 3]
            g_ref, d_ref, mo_ref, vo_ref = outs[4 * q:4 * q + 4]
            gv = grad(k, gsum_ref, conv_ref, bias_ref)
            g_ref[...] = gv
            d_ref[...], mo_ref[...], vo_ref[...] = _adam_update(w_ref[...], gv, m_ref[...], v_ref[...])

    res = pl.pallas_call(
        body, name="adamw_small", out_shape=tuple(S(small[k][0].shape, F32) for k in names for _ in range(4)),
        compiler_params=_params())(gsum, conv_g, bias_g, *[a for k in names for a in small[k]])
    return {k: res[4 * q:4 * q + 4] for q, k in enumerate(names)}


def _local_step(x, mem, pos, target, w, gains, bias_table, sinks, conv_w, shards):
    t, d = x.shape
    w = dict(w)
    grads, slabs = {}, {}
    pos_col = pos.reshape(t, 1)
    pos_row = pos.reshape(1, t)
    bias_t = bias_table.reshape(SWA_KV_HEADS, GROUP_ROWS, 2 * BLOCK)

    def gather(names, lead=2):
        return _GatherHook([shards[k] for k in names], [KIND[k] for k in names], lead)

    def gathered(names, hook):
        for k, gw in zip(names, hook.results):
            w[k] = gw if KIND[k] == "col" else gw.reshape(-1, gw.shape[-1])

    def dw(problems, tn, name, hook=None):
        res = _dw_pair([(a, b) for _, a, b in problems], tn, name, KIND[problems[0][0]], hook)
        grads.update(zip((k for k, _, _ in problems), res))

    def pair_sums(names):
        flat = lambda k, v: v.reshape(-1, v.shape[-1]) if KIND[k] == "row" else v
        sums = _add_bf16([(flat(k, grads[k][0]), flat(k, grads[k][1])) for k in names], "pair_sum_" + names[0])
        return {k: (p if KIND[k] == "col" else p.reshape(4, -1, p.shape[-1])) for k, p in zip(names, sums)}

    def chips(names, parts):
        return _ChipsHook([parts[k] for k in names], [KIND[k] for k in names])

    def reduced(names, hook):
        slabs.update(zip(names, hook.results))

    names = ("w_in", "w_out")
    hook = gather(names, lead=1)
    x1, g1, u1 = _ffn_fwd(x, gains["ffn1_norm"], w["ffn1_w_gu"], w["ffn1_w_down"], "ffn1_fwd", hook)
    gathered(names, hook)
    names = ("xattn_wq", "xattn_wkv", "xattn_wo")
    hook = gather(names)
    qkv, e = _mix_proj(x1, gains["mix_norm"], w["w_in"], hook)
    gathered(names, hook)
    names = ("ffn2_w_gu",)
    hook = gather(names)
    attn = _swa_fwd(qkv, pos_col, pos_row, bias_t, sinks, hook)
    gathered(names, hook)
    names = ("ffn2_w_down",)
    hook = gather(names)
    x2, merged = _mix_out_fwd(e, attn, conv_w, w["w_out"], x1, hook)
    gathered(names, hook)
    mh, kv = _mem_kv(mem, gains["mem_norm"], w["xattn_wkv"])
    x3, qx, o = _xattn_fwd(x2, gains["xattn_norm"], w["xattn_wq"], kv, w["xattn_wo"])
    dx4, g2, u2, d_final, loss = _ffn_fwd(x3, gains["ffn2_norm"], w["ffn2_w_gu"], w["ffn2_w_down"], "ffn2_fwd",
                                          head=(gains["final_norm"], target))

    dx3, d_ffn2, dgu2, a2, h4, dyh4 = _ffn_bwd(dx4, x3, gains["ffn2_norm"], g2, u2, w["ffn2_w_gu"], w["ffn2_w_down"],
                                               "ffn2_bwd")
    dw([("ffn2_w_gu", h4, dgu2)], 1408, "dw_ffn2_gu")
    dw([("ffn2_w_down", a2, dyh4)], 512, "dw_ffn2_down")
    parts = pair_sums(("ffn2_w_gu", "ffn2_w_down"))
    hook = chips(("ffn2_w_gu",), parts)
    dx2, d_xattn, dxh3, h3, dqx, grads["xattn_wkv"], d_mem = _xattn_bwd(
        dx3, x2, gains["xattn_norm"], qx, kv, w["xattn_wq"], w["xattn_wo"], mh, mem, w["xattn_wkv"], hook)
    reduced(("ffn2_w_gu",), hook)
    hook = chips(("ffn2_w_down",), parts)
    dattn, dp, dcw, dxh2 = _mix_out_bwd(dx2, e, attn, conv_w, w["w_out"], qkv.shape[1], hook)
    reduced(("ffn2_w_down",), hook)
    dw([("xattn_wo", o, dxh3), ("xattn_wq", h3, dqx), ("w_out", merged, dxh2)], 1024, "dw_wo_wq_wout")
    names = ("xattn_wo", "xattn_wq", "xattn_wkv", "w_out")
    hook = chips(names, pair_sums(names))
    dp, gb, dsk = _swa_bwd(qkv, dattn, pos_col, pos_row, bias_t, sinks, dp, hook)
    reduced(names, hook)
    d_rel_bias, d_sinks = _bias_reduce(gb.reshape(SWA_HEADS, BLOCK, 2 * BLOCK), dsk.reshape(SWA_HEADS, BLOCK, 1))
    dx1, d_mix, h2 = _mix_in_bwd(dp, x1, gains["mix_norm"], w["w_in"], dx2)
    dw([("w_in", h2, dp)], w["w_in"].shape[1] // 4, "dw_win")
    names = ("w_in",)
    hook = chips(names, pair_sums(names))
    dx0, d_ffn1, dgu1, a1, h1, dyh1 = _ffn_bwd(dx1, x, gains["ffn1_norm"], g1, u1, w["ffn1_w_gu"], w["ffn1_w_down"],
                                               "ffn1_bwd")
    dw([("ffn1_w_gu", h1, dgu1)], 1408, "dw_ffn1_gu", hook)
    reduced(names, hook)
    names = ("ffn1_w_gu",)
    hook = chips(names, pair_sums(names))
    dw([("ffn1_w_down", a1, dyh1)], 512, "dw_ffn1_down", hook)
    reduced(names, hook)
    rows = {0: d_ffn1, 1: d_mix, 2: d_xattn, 3: d_mem, 4: d_ffn2, 5: d_final, SINK_ROW: d_sinks,
            BIAS_ROW: d_rel_bias.reshape(1, -1), LOSS_ROW: loss[0, 0:1]}
    rows.update({CONV_ROW + j: dcw[j] for j in range(3)})
    last = ("ffn1_w_down",)
    return dx0, slabs, _rows_block(rows, d), chips(last, pair_sums(last))


def kernel(x, mem, positions, rel_bias, ffn1_norm, ffn1_w_gu, ffn1_w_down, mix_norm, w_in, sinks, conv_w, w_out, xattn_norm, mem_norm, xattn_wq, xattn_wkv, xattn_wo, ffn2_norm, ffn2_w_gu, ffn2_w_down, final_norm, loss_target, m_rel_bias, m_ffn1_norm, m_ffn1_w_gu, m_ffn1_w_down, m_mix_norm, m_w_in, m_sinks, m_conv_w, m_w_out, m_xattn_norm, m_mem_norm, m_xattn_wq, m_xattn_wkv, m_xattn_wo, m_ffn2_norm, m_ffn2_w_gu, m_ffn2_w_down, m_final_norm, v_rel_bias, v_ffn1_norm, v_ffn1_w_gu, v_ffn1_w_down, v_mix_norm, v_w_in, v_sinks, v_conv_w, v_w_out, v_xattn_norm, v_mem_norm, v_xattn_wq, v_xattn_wkv, v_xattn_wo, v_ffn2_norm, v_ffn2_w_gu, v_ffn2_w_down, v_final_norm):
    args = dict(locals())
    wts = {k: args[k] for k in WEIGHTS}
    mom = {k: args["m_" + k] for k in WEIGHTS}
    var = {k: args["v_" + k] for k in WEIGHTS}
    d = x.shape[-1]
    s_me = 2 * lax.axis_index("x") + lax.axis_index("y")

    first = ("ffn1_w_gu", "ffn1_w_down")
    rest = tuple(k for k in BIG if k not in first)
    shards = {k: wts[k][0].astype(BF) for k in first}
    cw_cols = conv_w.shape[-1]
    placed = lax.dynamic_update_slice(jnp.zeros((TAP_ROWS, d), F32), 0.5 * conv_w[0], (0, s_me * cw_cols))
    head = _Both([_GatherHook([shards[k] for k in first], [KIND[k] for k in first]), _SmallSumHook(placed)])
    casts, bias_table = _local_prelude([wts[k][0] for k in rest], rel_bias.T, "gather_ffn1", head)
    shards.update(zip(rest, casts))
    gathered, (conv_sum,) = head.results
    whole = {k: (gw if KIND[k] == "col" else gw.reshape(-1, gw.shape[-1])) for k, gw in zip(first, gathered)}
    conv_whole = conv_sum[0:3]

    gains = {k: wts[k].reshape(1, d) for k in GAIN_ROW}
    dx0, slabs, small, last_chips = _local_step(x[0], mem[0], positions[0], loss_target[0], whole, gains, bias_table,
                                               sinks, conv_whole, shards)

    late = ("ffn1_w_gu", "ffn1_w_down", "ffn2_w_down")
    early = tuple(k for k in BIG if k not in late)
    first_ones = tuple(k for k in BIG if k != "ffn1_w_down")
    tail = _Both([last_chips, _SmallSumHook(small)])
    shard_grads = dict(zip(first_ones, _sum_share([slabs[k] for k in first_ones], "sum_share", tail)))
    (last_slabs,), (small_sum,) = tail.results
    shard_grads["ffn1_w_down"], = _sum_share([last_slabs], "sum_share_ffn1_w_down")
    quad = lambda k: (wts[k][0], shard_grads[k], mom[k][0], var[k][0])
    updates = dict(zip(early, _adamw([quad(k) for k in early], "adamw_early", 8)))
    updates.update(zip(late, _adamw([quad(k) for k in late], "adamw_late", 4)))
    loss = small_sum[LOSS_ROW, 0]

    out_g, out_d, out_m, out_v = {}, {}, {}, {}
    for k in BIG:
        out_g[k], out_d[k], out_m[k], out_v[k] = (a[None] for a in updates[k])

    to_call = lambda k, a: a.T if k == "rel_bias" else a.reshape(1, -1)
    from_call = lambda k, a: a.T if k == "rel_bias" else a.reshape(wts[k].shape)
    small_names = [k for k in WEIGHTS if k not in KIND]
    conv_g = lax.dynamic_slice(small_sum, (CONV_ROW, s_me * cw_cols), (3, cw_cols)).reshape(1, -1)
    bias_g = small_sum[BIAS_ROW, 0:rel_bias.size].reshape(rel_bias.shape[::-1])
    done = _adamw_small(small_sum, conv_g, bias_g,
                        {k: tuple(to_call(k, a) for a in (wts[k], mom[k], var[k])) for k in small_names})
    for k in small_names:
        out_g[k], out_d[k], out_m[k], out_v[k] = (from_call(k, a) for a in done[k])

    return (loss, dx0[None], *[out_g[k] for k in WEIGHTS], *[out_d[k] for k in WEIGHTS],
            *[out_m[k] for k in WEIGHTS], *[out_v[k] for k in WEIGHTS])
```

```python
import functools
import math

import jax
import jax.numpy as jnp
from jax import lax
from jax.experimental import pallas as pl
from jax.experimental.pallas import tpu as pltpu

BF = jnp.bfloat16
F32 = jnp.float32
I32 = jnp.int32
S = jax.ShapeDtypeStruct

EPS = 1e-6
NEG = -1e30
POS_PAD = 1 << 30
WINDOW = 128
BLOCK = 128
HEAD_DIM = 64
SWA_HEADS = 16
SWA_KV_HEADS = 4
SWA_GROUP = SWA_HEADS // SWA_KV_HEADS
MEM_HEADS = 4
REL_BUCKETS = 32
REL_MAX_DIST = 128
ADAM_LR = 0.001
ADAM_B1 = 0.9
ADAM_B2 = 0.999
ADAM_EPS = 1e-08
ADAM_WD = 0.01
ADAM_STEP = 10

V7X_VMEM_LIMIT_BYTES = 56 * 1024 * 1024
MESH = pl.DeviceIdType.MESH
ANY = pl.BlockSpec(memory_space=pl.ANY)
VMEM_SPEC = pl.BlockSpec(memory_space=pltpu.VMEM)
SMEM_SPEC = pl.BlockSpec(memory_space=pltpu.SMEM)


def _params(sem=None):
    return pltpu.CompilerParams(dimension_semantics=sem, vmem_limit_bytes=V7X_VMEM_LIMIT_BYTES)


def _resident(shape):
    nd = len(shape)
    return pl.BlockSpec(shape, lambda *_: (0,) * nd, pipeline_mode=pl.Buffered(1))


def _acc_spec(shape):
    nd = len(shape)
    return pl.BlockSpec(shape, lambda *_: (0,) * nd)


def _call(body, *, name, grid, out_shape, in_specs, out_specs, args, sem, scratch_shapes=(), aliases=None, hook=None):
    aliases = aliases or {}
    if hook is None:
        return pl.pallas_call(body, name=name, grid=grid, out_shape=out_shape, in_specs=in_specs, out_specs=out_specs,
                              scratch_shapes=list(scratch_shapes), input_output_aliases=aliases,
                              compiler_params=_params(sem))(*args)
    n_in, n_out, n_scr = len(in_specs), len(out_shape), len(scratch_shapes)
    h_in, h_out = len(hook.ins), len(hook.out_shape)

    def at_step(pick):
        conds = [pl.program_id(ax) == pick(size) for ax, size in enumerate(grid)]
        return functools.reduce(jnp.logical_and, conds)

    def hosted(*refs):
        k_in, x_in = refs[:n_in], refs[n_in:n_in + h_in]
        o0 = n_in + h_in
        k_out, x_out = refs[o0:o0 + n_out], refs[o0 + n_out:o0 + n_out + h_out]
        s0 = o0 + n_out + h_out
        k_scr, x_scr = refs[s0:s0 + n_scr], refs[s0 + n_scr:]

        @pl.when(at_step(lambda size: 0))
        def _():
            hook.start(x_in, x_out, x_scr)
        body(*k_in, *k_out, *k_scr)
        early = bool(hook.lead) and grid[0] > hook.lead
        if early:
            on_axis0 = pl.program_id(0) == grid[0] - 1 - hook.lead
            rest_first = [pl.program_id(ax) == 0 for ax in range(1, len(grid))]

            @pl.when(functools.reduce(jnp.logical_and, rest_first, on_axis0))
            def _():
                hook.relay(x_in, x_out, x_scr)

        @pl.when(at_step(lambda size: size - 1))
        def _():
            if not early:
                hook.relay(x_in, x_out, x_scr)
            hook.finish(x_in, x_out, x_scr)

    res = pl.pallas_call(
        hosted, name=name, grid=grid, out_shape=tuple(out_shape) + tuple(hook.out_shape),
        in_specs=list(in_specs) + [ANY] * h_in, out_specs=tuple(out_specs) + (ANY,) * h_out,
        scratch_shapes=list(scratch_shapes) + list(hook.scratch), input_output_aliases=aliases,
        compiler_params=_params(("arbitrary",) * len(grid)),
    )(*args, *hook.ins)
    hook.results = res[n_out:]
    return res[:n_out]


def _nn(a, b):
    return jnp.dot(a, b, preferred_element_type=F32)


def _nt(a, b):
    return lax.dot_general(a, b, (((1,), (1,)), ((), ())), preferred_element_type=F32)


def _tn(a, b):
    return lax.dot_general(a, b, (((0,), (0,)), ((), ())), preferred_element_type=F32)


def _sigmoid(v):
    return 1.0 / (1.0 + jnp.exp(-v))


def _rms(x):
    r = lax.rsqrt(jnp.mean(x * x, axis=-1, keepdims=True) + EPS)
    return x * r, r


def _rms_bwd(dh, n, r, g):
    dn = dh * g
    dx = r * (dn - n * jnp.mean(dn * n, axis=-1, keepdims=True))
    return dx, jnp.sum(dh * n, axis=0, keepdims=True)


def _ffn_fwd(x, gn, wgu, wd, name, hook=None, head=None):
    t, d = x.shape
    f = wd.shape[0]
    tm, fc = 256, 1408

    def body(x_ref, gn_ref, wgu_ref, wd_ref, *rest):
        xv = x_ref[...]
        n, _ = _rms(xv)
        h = (n * gn_ref[...]).astype(BF)
        g_ref, u_ref = rest[-2:] if head is None else rest[3:5]
        acc = jnp.zeros((tm, d), F32)
        for c0 in range(0, f, fc):
            g = _nn(h, wgu_ref[:, c0:c0 + fc])
            u = _nn(h, wgu_ref[:, f + c0:f + c0 + fc])
            g_ref[:, c0:c0 + fc] = g.astype(BF)
            u_ref[:, c0:c0 + fc] = u.astype(BF)
            a = (g * _sigmoid(g)) * u
            acc = acc + _nn(a.astype(BF), wd_ref[c0:c0 + fc, :])
        y = xv + 0.5 * acc
        if head is None:
            rest[0][...] = y
            return
        gf_ref, t_ref, dy_ref, _, _, dgf_ref, loss_ref = rest

        @pl.when(pl.program_id(0) == 0)
        def _():
            dgf_ref[...] = jnp.zeros_like(dgf_ref)
            loss_ref[...] = jnp.zeros_like(loss_ref)
        ny, ry = _rms(y)
        gf = gf_ref[...]
        err = ny * gf - t_ref[...]
        loss_ref[...] += 0.5 * jnp.sum(jnp.sum(err * err, axis=-1, keepdims=True) / d, axis=0, keepdims=True)
        dy, dgf = _rms_bwd(err / d, ny, ry, gf)
        dy_ref[...] = dy
        dgf_ref[...] += dgf

    row = pl.BlockSpec((tm, d), lambda i: (i, 0))
    frow = pl.BlockSpec((tm, f), lambda i: (i, 0))
    in_specs = [row, _resident((1, d)), _resident(wgu.shape), _resident(wd.shape)]
    out_shape = (S((t, d), F32), S((t, f), BF), S((t, f), BF))
    out_specs = (row, frow, frow)
    args = (x, gn, wgu, wd)
    if head is not None:
        in_specs += [_resident((1, d)), row]
        out_shape += (S((1, d), F32), S((1, 128), F32))
        out_specs += (_acc_spec((1, d)), _acc_spec((1, 128)))
        args += tuple(head)
    return _call(body, name=name, grid=(t // tm,), out_shape=out_shape, in_specs=in_specs, out_specs=out_specs,
                 sem=("parallel",) if head is None else ("arbitrary",), args=args, hook=hook)


def _mix_proj(x, gn, w_in, hook=None):
    t, d = x.shape
    tm = 256
    nqkv = 1536
    ne = w_in.shape[1] - nqkv

    def body(x_ref, gn_ref, w_ref, qkv_ref, e_ref):
        n, _ = _rms(x_ref[...])
        h = (n * gn_ref[...]).astype(BF)
        qkv_ref[...] = _nn(h, w_ref[:, 0:nqkv]).astype(BF)
        for c0 in range(0, ne, 1024):
            e_ref[:, c0:c0 + 1024] = _nn(h, w_ref[:, nqkv + c0:nqkv + c0 + 1024]).astype(BF)

    return _call(
        body, name="mix_proj", grid=(t // tm,),
        out_shape=(S((t, nqkv), BF), S((t, ne), BF)),
        in_specs=[pl.BlockSpec((tm, d), lambda i: (i, 0)), _resident((1, d)), _resident(w_in.shape)],
        out_specs=(pl.BlockSpec((tm, nqkv), lambda i: (i, 0)), pl.BlockSpec((tm, ne), lambda i: (i, 0))),
        sem=("parallel",), args=(x, gn, w_in), hook=hook)


def _t5_bucket(rel):
    n = jnp.maximum(rel, 0)
    max_exact = REL_BUCKETS // 2
    nf = jnp.maximum(n, 1).astype(F32)
    large = max_exact + (jnp.log(nf / max_exact) / math.log(REL_MAX_DIST / max_exact)
                         * (REL_BUCKETS - max_exact)).astype(I32)
    large = jnp.minimum(large, REL_BUCKETS - 1)
    return jnp.where(n < max_exact, n, large)


def _block_rel():
    i = lax.broadcasted_iota(I32, (BLOCK, 2 * BLOCK), 0)
    j = lax.broadcasted_iota(I32, (BLOCK, 2 * BLOCK), 1)
    return i + BLOCK - j


def _bias_table(rb_ref, o_ref):
    bucket = _t5_bucket(_block_rel())
    for h in range(SWA_HEADS):
        acc = jnp.zeros((BLOCK, 2 * BLOCK), F32)
        for b in range(REL_BUCKETS):
            acc = jnp.where(bucket == b, rb_ref[h, b], acc)
        o_ref[h] = acc


GROUP_ROWS = SWA_GROUP * BLOCK


def _swa_visible(b, pq_ref, pkp_ref, pkc_ref):
    pk = jnp.concatenate([pkp_ref[...], pkc_ref[...]], axis=1)
    col = lax.broadcasted_iota(I32, (1, 2 * BLOCK), 1)
    pk = jnp.where(jnp.logical_and(b == 0, col < BLOCK), POS_PAD, pk)
    rel = jnp.concatenate([pq_ref[...]] * SWA_GROUP, axis=0) - pk
    return jnp.logical_and(rel >= 0, rel < WINDOW)


def _swa_poison(b, pq_ref, pkp_ref, pkc_ref):
    pk = jnp.concatenate([pkp_ref[...], pkc_ref[...]], axis=1)
    col = lax.broadcasted_iota(I32, (1, 2 * BLOCK), 1)
    pk = jnp.where(jnp.logical_and(b == 0, col < BLOCK), POS_PAD, pk)
    rel = pq_ref[...] - pk
    off = jnp.logical_and(jnp.logical_and(rel >= 0, rel < WINDOW), rel != _block_rel())
    return jnp.where(jnp.max(off.astype(F32)) > 0.0, jnp.nan, 0.0)


def _group_heads(ref, hk):
    h0 = hk * SWA_GROUP
    return jnp.concatenate([ref[:, (h0 + g) * HEAD_DIM:(h0 + g + 1) * HEAD_DIM] for g in range(SWA_GROUP)], axis=0)


def _group_sinks(sink_ref, hk):
    row = lax.broadcasted_iota(I32, (GROUP_ROWS, 1), 0)
    col = jnp.zeros((GROUP_ROWS, 1), F32) + sink_ref[0, hk * SWA_GROUP]
    for g in range(1, SWA_GROUP):
        col = jnp.where(row >= g * BLOCK, sink_ref[0, hk * SWA_GROUP + g], col)
    return col


def _swa_probs(qg, kh, vis, bias, sink):
    s = _nt(qg, kh) * (HEAD_DIM ** -0.5)
    s = jnp.where(vis, s + bias, NEG)
    m = jnp.maximum(jnp.max(s, axis=-1, keepdims=True), sink)
    p = jnp.exp(s - m)
    ps = jnp.exp(sink - m)
    inv = 1.0 / (jnp.sum(p, axis=-1, keepdims=True) + ps)
    return p * inv, ps * inv


def _swa_fwd(qkv, pos_col, pos_row, bias_t, sinks, hook=None):
    t = qkv.shape[0]
    nb = t // BLOCK
    qw = SWA_HEADS * HEAD_DIM
    kw = SWA_KV_HEADS * HEAD_DIM

    def body(q_ref, kp_ref, kc_ref, vp_ref, vc_ref, pq_ref, pkp_ref, pkc_ref, bias_ref, sink_ref, o_ref):
        b = pl.program_id(0)
        vis = _swa_visible(b, pq_ref, pkp_ref, pkc_ref)
        poison = _swa_poison(b, pq_ref, pkp_ref, pkc_ref)
        k2 = jnp.concatenate([kp_ref[...], kc_ref[...]], axis=0)
        v2 = jnp.concatenate([vp_ref[...], vc_ref[...]], axis=0)
        for hk in range(SWA_KV_HEADS):
            kh = k2[:, hk * HEAD_DIM:(hk + 1) * HEAD_DIM]
            vh = v2[:, hk * HEAD_DIM:(hk + 1) * HEAD_DIM]
            pn, _ = _swa_probs(_group_heads(q_ref, hk), kh, vis, bias_ref[hk], _group_sinks(sink_ref, hk))
            o = _nn(pn.astype(BF), vh) + poison
            for g in range(SWA_GROUP):
                h = hk * SWA_GROUP + g
                o_ref[:, h * HEAD_DIM:(h + 1) * HEAD_DIM] = o[g * BLOCK:(g + 1) * BLOCK]

    prev = lambda b: jnp.maximum(b - 1, 0)
    return _call(
        body, name="swa_fwd", grid=(nb,), out_shape=(S((t, qw), F32),),
        in_specs=[
            pl.BlockSpec((BLOCK, qw), lambda b: (b, 0)),
            pl.BlockSpec((BLOCK, kw), lambda b: (prev(b), qw // kw)),
            pl.BlockSpec((BLOCK, kw), lambda b: (b, qw // kw)),
            pl.BlockSpec((BLOCK, kw), lambda b: (prev(b), qw // kw + 1)),
            pl.BlockSpec((BLOCK, kw), lambda b: (b, qw // kw + 1)),
            pl.BlockSpec((BLOCK, 1), lambda b: (b, 0)),
            pl.BlockSpec((1, BLOCK), lambda b: (0, prev(b))),
            pl.BlockSpec((1, BLOCK), lambda b: (0, b)),
            _resident(bias_t.shape),
            SMEM_SPEC,
        ],
        out_specs=(pl.BlockSpec((BLOCK, qw), lambda b: (b, 0)),),
        sem=("parallel",), args=(qkv, qkv, qkv, qkv, qkv, pos_col, pos_row, pos_row, bias_t, sinks), hook=hook)[0]


HALO = 16


def _conv_taps(z, zh, first):
    tm = z.shape[0]
    zh = jnp.where(first, 0.0, zh)
    row = lax.broadcasted_iota(I32, (tm, 1), 0)
    z1 = jnp.where(row == 0, zh[HALO - 1:HALO, :], pltpu.roll(z, 1, 0))
    z2 = jnp.where(row == 0, zh[HALO - 2:HALO - 1, :], jnp.where(row == 1, zh[HALO - 1:HALO, :], pltpu.roll(z, 2, 0)))
    return z1, z2


E_RING = 3


def _mix_out_fwd(e, attn, conv_w, w_out, x, hook=None):
    t, d = x.shape
    tm = 256
    hb = tm // HALO
    nt = t // tm
    f32 = lambda ref: ref[...].astype(F32)

    def body(e_ref, ch_ref, uh_ref, attn_ref, cw_ref, w_ref, x_ref, xo_ref, mg_ref, ring, sem):
        i = pl.program_id(0)

        def fetch(step):
            slot = step % E_RING
            return pltpu.make_async_copy(e_ref.at[pl.ds(pl.multiple_of(step * tm, tm), tm), :], ring.at[slot], sem.at[slot])

        @pl.when(i == 0)
        def _():
            for step in range(E_RING - 1):
                fetch(step).start()

        @pl.when(i + E_RING - 1 < nt)
        def _():
            fetch(i + E_RING - 1).start()
        fetch(i).wait()
        tile = ring.at[i % E_RING]
        cv, bv, uv, ga, gc = (tile[:, k * d:(k + 1) * d].astype(F32) for k in range(5))
        z = cv * uv
        z1, z2 = _conv_taps(z, f32(ch_ref) * f32(uh_ref), i == 0)
        s = cw_ref[0:1, :] * z2 + cw_ref[1:2, :] * z1 + cw_ref[2:3, :] * z
        conv = bv * s
        merged = (_sigmoid(ga) * attn_ref[...] + _sigmoid(gc) * conv).astype(BF)
        mg_ref[...] = merged
        xo_ref[...] = x_ref[...] + _nn(merged, w_ref[...])

    halo = lambda cb: pl.BlockSpec((HALO, d), lambda i: (jnp.maximum(i * hb - 1, 0), cb))
    row = pl.BlockSpec((tm, d), lambda i: (i, 0))
    return _call(
        body, name="mix_out_fwd", grid=(nt,),
        out_shape=(S((t, d), F32), S((t, d), BF)),
        in_specs=[ANY, halo(0), halo(2), row, _resident(conv_w.shape), _resident(w_out.shape), row],
        out_specs=(row, row),
        scratch_shapes=[pltpu.VMEM((E_RING, tm, e.shape[1]), BF), pltpu.SemaphoreType.DMA((E_RING,))],
        sem=("arbitrary",), args=(e, e, e, attn, conv_w, w_out, x), hook=hook)


def _mem_kv(mem, gm, wkv):
    m, d = mem.shape

    def body(mem_ref, gm_ref, w_ref, mh_ref, kv_ref):
        n, _ = _rms(mem_ref[...])
        mh = (n * gm_ref[...]).astype(BF)
        mh_ref[...] = mh
        kv_ref[...] = _nn(mh, w_ref[...]).astype(BF)

    return pl.pallas_call(
        body, name="mem_kv", out_shape=(S((m, d), BF), S((m, wkv.shape[1]), BF)),
        compiler_params=_params(),
    )(mem, gm, wkv)


def _xattn_probs(qh, kh):
    s = _nt(qh, kh) * (kh.shape[1] ** -0.5)
    p = jnp.exp(s - jnp.max(s, axis=-1, keepdims=True))
    return p * (1.0 / jnp.sum(p, axis=-1, keepdims=True))


def _xattn_fwd(x, gn, wq, kv, wo):
    t, d = x.shape
    tm = 256
    hd = d // MEM_HEADS

    def body(x_ref, gn_ref, wq_ref, kv_ref, wo_ref, xo_ref, q_ref, o_ref):
        xv = x_ref[...]
        n, _ = _rms(xv)
        q = _nn((n * gn_ref[...]).astype(BF), wq_ref[...]).astype(BF)
        q_ref[...] = q
        outs = []
        for hh in range(MEM_HEADS):
            p = _xattn_probs(q[:, hh * hd:(hh + 1) * hd], kv_ref[:, hh * hd:(hh + 1) * hd])
            outs.append(_nn(p.astype(BF), kv_ref[:, d + hh * hd:d + (hh + 1) * hd]))
        o = jnp.concatenate(outs, axis=1).astype(BF)
        o_ref[...] = o
        xo_ref[...] = xv + _nn(o, wo_ref[...])

    row = pl.BlockSpec((tm, d), lambda i: (i, 0))
    return pl.pallas_call(
        body, name="xattn_fwd", grid=(t // tm,),
        out_shape=(S((t, d), F32), S((t, d), BF), S((t, d), BF)),
        in_specs=[row, _resident((1, d)), _resident(wq.shape), _resident(kv.shape), _resident(wo.shape)],
        out_specs=(row, row, row),
        compiler_params=_params(("parallel",)),
    )(x, gn, wq, kv, wo)


def _ffn_bwd(dxo, x, gn, g, u, wgu, wd, name):
    t, d = x.shape
    f = wd.shape[0]
    tm, fc = 256, 1408

    def body(dxo_ref, x_ref, gn_ref, g_ref, u_ref, wgu_ref, wd_ref, dx_ref, dgn_ref, dgu_ref, a_ref, h_ref, dyh_ref):
        @pl.when(pl.program_id(0) == 0)
        def _():
            dgn_ref[...] = jnp.zeros_like(dgn_ref)
        dxov = dxo_ref[...]
        dyh = (0.5 * dxov).astype(BF)
        dyh_ref[...] = dyh
        n, r = _rms(x_ref[...])
        gnv = gn_ref[...]
        h_ref[...] = (n * gnv).astype(BF)
        dh = jnp.zeros((tm, d), F32)
        for c0 in range(0, f, fc):
            gv = g_ref[:, c0:c0 + fc].astype(F32)
            uv = u_ref[:, c0:c0 + fc].astype(F32)
            da = _nt(dyh, wd_ref[c0:c0 + fc, :])
            sg = _sigmoid(gv)
            silu = gv * sg
            a_ref[:, c0:c0 + fc] = (silu * uv).astype(BF)
            dg = (da * uv * (sg * (1.0 + gv * (1.0 - sg)))).astype(BF)
            du = (da * silu).astype(BF)
            dgu_ref[:, c0:c0 + fc] = dg
            dgu_ref[:, f + c0:f + c0 + fc] = du
            dh = dh + _nt(dg, wgu_ref[:, c0:c0 + fc]) + _nt(du, wgu_ref[:, f + c0:f + c0 + fc])
        dx, dgn = _rms_bwd(dh, n, r, gnv)
        dx_ref[...] = dxov + dx
        dgn_ref[...] += dgn

    row = pl.BlockSpec((tm, d), lambda i: (i, 0))
    frow = pl.BlockSpec((tm, f), lambda i: (i, 0))
    return _call(
        body, name=name, grid=(t // tm,),
        out_shape=(S((t, d), F32), S((1, d), F32), S((t, 2 * f), BF), S((t, f), BF), S((t, d), BF), S((t, d), BF)),
        in_specs=[row, row, _resident((1, d)), frow, frow, _resident(wgu.shape), _resident(wd.shape)],
        out_specs=(row, _acc_spec((1, d)), pl.BlockSpec((tm, 2 * f), lambda i: (i, 0)), frow, row, row),
        sem=("arbitrary",), args=(dxo, x, gn, g, u, wgu, wd))


def _dw_pair(pairs, tn, name, kind, hook=None):
    npairs = len(pairs)
    t, ka = pairs[0][0].shape
    nb = pairs[0][1].shape[1]
    tt = 2048 if npairs == 1 else 1024
    nt, nj = t // tt, nb // tn
    col = kind == "col"
    rh = ka // 2 if col else ka // 8
    tile = (rh, tn) if col else (4, rh, tn)
    half = (rh, nb) if col else (4, rh, nb)
    lead = (slice(None),) * (len(tile) - 1)

    def body(*refs):
        ins, mines, sibs = refs[:2 * npairs], refs[2 * npairs:3 * npairs], refs[3 * npairs:4 * npairs]
        acc_ref, stage, ssem, rsem = refs[4 * npairs:]
        j, k = pl.program_id(0), pl.program_id(1)
        x, y, c = _place()
        sibling = (x, y, 1 - c)

        def send(p, slot, jj):
            dst = sibs[p].at[lead + (pl.ds(pl.multiple_of(jj * tn, 128), tn),)]
            return _remote(stage.at[slot], dst, ssem.at[slot], rsem.at[p], sibling)

        def rows(s, whose):
            return acc_ref[pl.ds(pl.multiple_of(s * 2 * rh + whose * rh, 16), rh), :].astype(BF)

        def step(p):
            a_ref, b_ref, mine_ref = ins[2 * p], ins[2 * p + 1], mines[p]

            @pl.when(k == 0)
            def _():
                acc_ref[...] = jnp.zeros_like(acc_ref)
            acc_ref[...] += _tn(a_ref[...], b_ref[...])

            @pl.when(k == nt - 1)
            def _():
                slot = j % 2

                @pl.when(j >= 2)
                def _():
                    send(p, slot, 0).wait_send()
                if col:
                    mine_ref[...] = rows(0, c)
                    stage[slot] = rows(0, 1 - c)
                else:
                    for s in range(4):
                        mine_ref[s] = rows(s, c)
                        stage[slot, s] = rows(s, 1 - c)
                send(p, slot, j - p * nj).start()

        for p in range(npairs):
            pl.when(j // nj == p)(functools.partial(step, p))

        @pl.when(jnp.logical_and(j == npairs * nj - 1, k == nt - 1))
        def _():
            for jj in range(max(npairs * nj - 2, 0), npairs * nj):
                send(0, jj % 2, 0).wait_send()
            for p in range(npairs):
                _remote(sibs[p], sibs[p], ssem.at[0], rsem.at[p], sibling).wait_recv()

    in_specs, args = [], []
    for p, (a, b) in enumerate(pairs):
        on = lambda j, p=p: j // nj == p
        in_specs += [pl.BlockSpec((tt, ka), lambda j, k, on=on: (jnp.where(on(j), k, 0), 0)),
                     pl.BlockSpec((tt, tn), lambda j, k, on=on, p=p: (jnp.where(on(j), k, 0), jnp.clip(j - p * nj, 0, nj - 1)))]
        args += [a, b]
    mine_spec = lambda p: pl.BlockSpec(tile, (lambda j, k: (0, jnp.clip(j - p * nj, 0, nj - 1))) if col
                                       else (lambda j, k: (0, 0, jnp.clip(j - p * nj, 0, nj - 1))))
    res = _call(
        body, name=name, grid=(npairs * nj, nt), out_shape=(S(half, BF),) * (2 * npairs),
        in_specs=in_specs, out_specs=tuple(mine_spec(p) for p in range(npairs)) + (ANY,) * npairs,
        scratch_shapes=[pltpu.VMEM((ka, tn), F32), pltpu.VMEM((2,) + tile, BF), pltpu.SemaphoreType.DMA((2,)),
                        pltpu.SemaphoreType.DMA((npairs,))],
        sem=("arbitrary", "arbitrary"), args=args, hook=hook)
    return [(res[p], res[npairs + p]) for p in range(npairs)]


def _xattn_bwd(dxo, x, gn, q, kv, wq, wo, mh, mem, wkv, hook=None):
    t, d = x.shape
    tm = 256
    hd = d // MEM_HEADS
    nkv = kv.shape[0]
    rows, cols = wkv.shape
    rh = rows // 2

    def body(dxo_ref, x_ref, gn_ref, q_ref, kv_ref, wq_ref, wo_ref, mh_ref, mem_ref, wkv_ref,
             dx_ref, dgn_ref, dxh_ref, h_ref, dq_ref, mine_ref, sib_ref, dgm_ref, dkv_ref, whole, ssem, rsem):
        @pl.when(pl.program_id(0) == 0)
        def _():
            dgn_ref[...] = jnp.zeros_like(dgn_ref)
            dkv_ref[...] = jnp.zeros_like(dkv_ref)
        dxov = dxo_ref[...]
        dxh = dxov.astype(BF)
        dxh_ref[...] = dxh
        do = _nt(dxh, wo_ref[...]).astype(BF)
        dqs = []
        for hh in range(MEM_HEADS):
            lo, hi = hh * hd, (hh + 1) * hd
            qh = q_ref[:, lo:hi]
            kh = kv_ref[:, lo:hi]
            vh = kv_ref[:, d + lo:d + hi]
            doh = do[:, lo:hi]
            p = _xattn_probs(qh, kh)
            dp = _nt(doh, vh)
            ds = (p * (dp - jnp.sum(p * dp, axis=-1, keepdims=True)) * (hd ** -0.5)).astype(BF)
            dqs.append(_nn(ds, kh))
            dkv_ref[:, lo:hi] += _tn(ds, qh)
            dkv_ref[:, d + lo:d + hi] += _tn(p.astype(BF), doh)
        dq = jnp.concatenate(dqs, axis=1).astype(BF)
        dq_ref[...] = dq
        n, r = _rms(x_ref[...])
        gnv = gn_ref[...]
        h_ref[...] = (n * gnv).astype(BF)
        dx, dgn = _rms_bwd(_nt(dq, wq_ref[...]), n, r, gnv)
        dx_ref[...] = dxov + dx
        dgn_ref[...] += dgn

        @pl.when(pl.program_id(0) == pl.num_programs(0) - 1)
        def _():
            dkvb = dkv_ref[...].astype(BF)
            dmh = _nt(dkvb, wkv_ref[...])
            nm, _ = _rms(mem_ref[...])
            px, py, c = _place()
            whole[...] = _tn(mh_ref[...], dkvb).astype(BF)
            cp = _remote(whole.at[pl.ds(pl.multiple_of((1 - c) * rh, 16), rh), :], sib_ref, ssem, rsem, (px, py, 1 - c))
            cp.start()
            mine_ref[...] = whole[pl.ds(pl.multiple_of(c * rh, 16), rh), :]
            dgm_ref[...] = jnp.sum(dmh * nm, axis=0, keepdims=True)
            cp.wait()

    row = pl.BlockSpec((tm, d), lambda i: (i, 0))
    dx, dgn, dxh, h, dq, mine, sib, dgm = _call(
        body, name="xattn_bwd", grid=(t // tm,),
        out_shape=(S((t, d), F32), S((1, d), F32), S((t, d), BF), S((t, d), BF), S((t, d), BF),
                   S((rh, cols), BF), S((rh, cols), BF), S((1, d), F32)),
        in_specs=[row, row, _resident((1, d)), row, _resident(kv.shape), _resident(wq.shape), _resident(wo.shape),
                  _resident(mh.shape), _resident(mem.shape), _resident(wkv.shape)],
        out_specs=(row, _acc_spec((1, d)), row, row, row, _acc_spec((rh, cols)), ANY, _acc_spec((1, d))),
        scratch_shapes=[pltpu.VMEM((nkv, 2 * d), F32), pltpu.VMEM((rows, cols), BF),
                        pltpu.SemaphoreType.DMA, pltpu.SemaphoreType.DMA],
        sem=("arbitrary",), args=(dxo, x, gn, q, kv, wq, wo, mh, mem, wkv), hook=hook)
    return dx, dgn, dxh, h, dq, (mine, sib), dgm


def _mix_out_bwd(dxo, e, attn, conv_w, w_out, nqkv, hook=None):
    t, d = attn.shape
    tm = 256
    nt = t // tm
    f32 = lambda ref: ref[...].astype(F32)

    def body(dxo_ref, dxn_ref, c_ref, b_ref, u_ref, ga_ref, gc_ref, ch_ref, uh_ref, bn_ref, gcn_ref,
             attn_ref, cw_ref, w_ref, dattn_ref, de_ref, dcw_ref, dxh_ref):
        i = pl.program_id(0)

        @pl.when(i == 0)
        def _():
            dcw_ref[...] = jnp.zeros_like(dcw_ref)
        dxh = dxo_ref[...].astype(BF)
        dxh_ref[...] = dxh
        w = w_ref[...]
        dm = _nt(dxh, w)
        dmn = _nt(dxn_ref[...].astype(BF), w)
        cv, bv, uv = f32(c_ref), f32(b_ref), f32(u_ref)
        sga = _sigmoid(f32(ga_ref))
        sgc = _sigmoid(f32(gc_ref))
        z = cv * uv
        z1, z2 = _conv_taps(z, f32(ch_ref) * f32(uh_ref), i == 0)
        w0, w1, w2 = cw_ref[0:1, :], cw_ref[1:2, :], cw_ref[2:3, :]
        s = w0 * z2 + w1 * z1 + w2 * z
        av = attn_ref[...]
        dattn_ref[...] = (dm * sga).astype(BF)
        dconv = dm * sgc
        ds = dconv * bv
        dsn = jnp.where(i == nt - 1, 0.0, dmn * _sigmoid(gcn_ref[0:8, :].astype(F32)) * bn_ref[0:8, :].astype(F32))
        row = lax.broadcasted_iota(I32, (tm, 1), 0)
        dsp1 = jnp.where(row == tm - 1, dsn[0:1, :], pltpu.roll(ds, tm - 1, 0))
        dsp2 = jnp.where(row == tm - 2, dsn[0:1, :], jnp.where(row == tm - 1, dsn[1:2, :], pltpu.roll(ds, tm - 2, 0)))
        dz = w2 * ds + w1 * dsp1 + w0 * dsp2
        de_ref[:, nqkv:nqkv + d] = (dz * uv).astype(BF)
        de_ref[:, nqkv + d:nqkv + 2 * d] = (dconv * s).astype(BF)
        de_ref[:, nqkv + 2 * d:nqkv + 3 * d] = (dz * cv).astype(BF)
        de_ref[:, nqkv + 3 * d:nqkv + 4 * d] = (dm * av * sga * (1.0 - sga)).astype(BF)
        de_ref[:, nqkv + 4 * d:nqkv + 5 * d] = (dm * (bv * s) * sgc * (1.0 - sgc)).astype(BF)
        dcw_ref[0:1, :] += jnp.sum(ds * z2, axis=0, keepdims=True)
        dcw_ref[1:2, :] += jnp.sum(ds * z1, axis=0, keepdims=True)
        dcw_ref[2:3, :] += jnp.sum(ds * z, axis=0, keepdims=True)

    ecol = lambda cb: pl.BlockSpec((tm, d), lambda i: (i, cb))
    prev = lambda cb: pl.BlockSpec((HALO, d), lambda i: (jnp.maximum(i * (tm // HALO) - 1, 0), cb))
    nxt = lambda rows, cb: pl.BlockSpec((rows, d), lambda i: (jnp.minimum((i + 1) * (tm // rows), t // rows - 1), cb))
    row = pl.BlockSpec((tm, d), lambda i: (i, 0))
    return _call(
        body, name="mix_out_bwd", grid=(nt,),
        out_shape=(S((t, d), BF), S((t, nqkv + 5 * d), BF), S((8, d), F32), S((t, d), BF)),
        in_specs=[row, nxt(8, 0), ecol(0), ecol(1), ecol(2), ecol(3), ecol(4), prev(0), prev(2), nxt(HALO, 1), nxt(HALO, 4),
                  row, _resident(conv_w.shape), _resident(w_out.shape)],
        out_specs=(row, pl.BlockSpec((tm, nqkv + 5 * d), lambda i: (i, 0)), _acc_spec((8, d)), row),
        sem=("arbitrary",), args=(dxo, dxo, e, e, e, e, e, e, e, e, e, attn, conv_w, w_out), hook=hook)


def _swa_bwd(qkv, dattn, pos_col, pos_row, bias_t, sinks, dp, hook=None):
    t = qkv.shape[0]
    nb = t // BLOCK
    qw = SWA_HEADS * HEAD_DIM
    kw = SWA_KV_HEADS * HEAD_DIM

    def body(q_ref, kp_ref, kc_ref, vp_ref, vc_ref, do_ref, pq_ref, pkp_ref, pkc_ref, bias_ref, sink_ref,
             dp_in, dp_ref, gb_ref, dsk_ref, carry_ref, dq_ref):
        b = pl.program_id(0)

        @pl.when(b == 0)
        def _():
            gb_ref[...] = jnp.zeros_like(gb_ref)
            dsk_ref[...] = jnp.zeros_like(dsk_ref)
            carry_ref[...] = jnp.zeros_like(carry_ref)
            dq_ref[...] = jnp.zeros_like(dq_ref)
        dp_ref[:, 0:qw] = dq_ref[...]

        @pl.when(b < nb)
        def _():
            vis = _swa_visible(b, pq_ref, pkp_ref, pkc_ref)
            k2 = jnp.concatenate([kp_ref[...], kc_ref[...]], axis=0)
            v2 = jnp.concatenate([vp_ref[...], vc_ref[...]], axis=0)
            for hk in range(SWA_KV_HEADS):
                lo, hi = hk * HEAD_DIM, (hk + 1) * HEAD_DIM
                kh = k2[:, lo:hi]
                vh = v2[:, lo:hi]
                qg = _group_heads(q_ref, hk)
                dog = _group_heads(do_ref, hk)
                pn, psn = _swa_probs(qg, kh, vis, bias_ref[hk], _group_sinks(sink_ref, hk))
                dp = _nt(dog, vh)
                delta = jnp.sum(pn * dp, axis=-1, keepdims=True)
                ds = pn * (dp - delta)
                gb_ref[hk] += ds
                dsk_ref[hk] += -psn * delta
                dsb = (ds * (HEAD_DIM ** -0.5)).astype(BF)
                dqg = _nn(dsb, kh).astype(BF)
                for g in range(SWA_GROUP):
                    h = hk * SWA_GROUP + g
                    dq_ref[:, h * HEAD_DIM:(h + 1) * HEAD_DIM] = dqg[g * BLOCK:(g + 1) * BLOCK]
                dk = _tn(dsb, qg)
                dv = _tn(pn.astype(BF), dog)
                dp_ref[:, qw + lo:qw + hi] = (carry_ref[:, lo:hi] + dk[0:BLOCK]).astype(BF)
                dp_ref[:, qw + kw + lo:qw + kw + hi] = (carry_ref[:, kw + lo:kw + hi] + dv[0:BLOCK]).astype(BF)
                carry_ref[:, lo:hi] = dk[BLOCK:2 * BLOCK]
                carry_ref[:, kw + lo:kw + hi] = dv[BLOCK:2 * BLOCK]

        @pl.when(b == nb)
        def _():
            dp_ref[:, qw:qw + 2 * kw] = carry_ref[...].astype(BF)

    cur = lambda b: jnp.minimum(b, nb - 1)
    prev = lambda b: jnp.maximum(cur(b) - 1, 0)
    return _call(
        body, name="swa_bwd", grid=(nb + 1,),
        out_shape=(S(dp.shape, BF), S((SWA_KV_HEADS, GROUP_ROWS, 2 * BLOCK), F32), S((SWA_KV_HEADS, GROUP_ROWS, 1), F32)),
        in_specs=[
            pl.BlockSpec((BLOCK, qw), lambda b: (cur(b), 0)),
            pl.BlockSpec((BLOCK, kw), lambda b: (prev(b), qw // kw)),
            pl.BlockSpec((BLOCK, kw), lambda b: (cur(b), qw // kw)),
            pl.BlockSpec((BLOCK, kw), lambda b: (prev(b), qw // kw + 1)),
            pl.BlockSpec((BLOCK, kw), lambda b: (cur(b), qw // kw + 1)),
            pl.BlockSpec((BLOCK, qw), lambda b: (cur(b), 0)),
            pl.BlockSpec((BLOCK, 1), lambda b: (cur(b), 0)),
            pl.BlockSpec((1, BLOCK), lambda b: (0, prev(b))),
            pl.BlockSpec((1, BLOCK), lambda b: (0, cur(b))),
            _resident(bias_t.shape),
            SMEM_SPEC,
            ANY,
        ],
        out_specs=(
            pl.BlockSpec((BLOCK, qw + 2 * kw), lambda b: (jnp.maximum(b - 1, 0), 0)),
            _acc_spec((SWA_KV_HEADS, GROUP_ROWS, 2 * BLOCK)),
            _acc_spec((SWA_KV_HEADS, GROUP_ROWS, 1)),
        ),
        scratch_shapes=[pltpu.VMEM((BLOCK, 2 * kw), F32), pltpu.VMEM((BLOCK, qw), BF)],
        aliases={11: 0}, sem=("arbitrary",),
        args=(qkv, qkv, qkv, qkv, qkv, dattn, pos_col, pos_row, pos_row, bias_t, sinks, dp), hook=hook)


def _bias_reduce(gb, dsk):
    def body(gb_ref, dsk_ref, drb_ref, dsink_ref):
        bucket = _t5_bucket(_block_rel())
        for b in range(REL_BUCKETS):
            mask = bucket == b
            for h in range(SWA_HEADS):
                drb_ref[h, b] = jnp.sum(jnp.where(mask, gb_ref[h], 0.0))
        for h in range(SWA_HEADS):
            dsink_ref[0, h] = jnp.sum(dsk_ref[h])

    return pl.pallas_call(
        body, name="bias_reduce", out_shape=(S((SWA_HEADS, REL_BUCKETS), F32), S((1, SWA_HEADS), F32)),
        in_specs=[VMEM_SPEC, VMEM_SPEC], out_specs=(SMEM_SPEC, SMEM_SPEC),
    )(gb, dsk)


def _mix_in_bwd(dp, x, gn, w_in, dxo):
    t, d = x.shape
    tm = 256
    npr = dp.shape[1]

    def body(dp_ref, x_ref, gn_ref, w_ref, dxo_ref, dx_ref, dgn_ref, h_ref):
        @pl.when(pl.program_id(0) == 0)
        def _():
            dgn_ref[...] = jnp.zeros_like(dgn_ref)
        dh = jnp.zeros((tm, d), F32)
        for c0 in range(0, npr, 1024):
            c1 = min(c0 + 1024, npr)
            dh = dh + _nt(dp_ref[:, c0:c1], w_ref[:, c0:c1])
        n, r = _rms(x_ref[...])
        gnv = gn_ref[...]
        h_ref[...] = (n * gnv).astype(BF)
        dx, dgn = _rms_bwd(dh, n, r, gnv)
        dx_ref[...] = dxo_ref[...] + dx
        dgn_ref[...] += dgn

    row = pl.BlockSpec((tm, d), lambda i: (i, 0))
    return pl.pallas_call(
        body, name="mix_in_bwd", grid=(t // tm,),
        out_shape=(S((t, d), F32), S((1, d), F32), S((t, d), BF)),
        in_specs=[pl.BlockSpec((tm, npr), lambda i: (i, 0)), row, _resident((1, d)), _resident(w_in.shape), row],
        out_specs=(row, _acc_spec((1, d)), row),
        compiler_params=_params(("arbitrary",)),
    )(dp, x, gn, w_in, dxo)


CAST_STEPS = 4


def _local_prelude(arrays, rel_bias_t, name, hook=None):
    n = len(arrays)

    def body(*refs):
        rb_ref, table_ref = refs[n], refs[-1]
        for src, dst in zip(refs[:n], refs[n + 1:-1]):
            dst[...] = src[...].astype(BF)

        @pl.when(pl.program_id(0) == 0)
        def _():
            _bias_table(rb_ref, table_ref)

    blocks = [pl.BlockSpec((a.shape[0] // CAST_STEPS, a.shape[1]), lambda i: (i, 0)) for a in arrays]
    table = (SWA_HEADS, BLOCK, 2 * BLOCK)
    res = _call(body, name=name, grid=(CAST_STEPS,), out_shape=tuple(S(a.shape, BF) for a in arrays) + (S(table, F32),),
                in_specs=blocks + [SMEM_SPEC], out_specs=tuple(blocks) + (_acc_spec(table),), sem=("arbitrary",),
                args=list(arrays) + [rel_bias_t], hook=hook)
    return res[:n], res[n]


PAIR_STEPS = 4


def _add_bf16(pairs, name):
    def body(*refs):
        ins, outs = refs[:2 * len(pairs)], refs[2 * len(pairs):]
        for q, o_ref in enumerate(outs):
            o_ref[...] = (ins[2 * q][...].astype(F32) + ins[2 * q + 1][...].astype(F32)).astype(BF)

    in_specs, out_specs, out_shape, args = [], [], [], []
    for a, b in pairs:
        rows, cols = a.shape
        blk = pl.BlockSpec((rows // PAIR_STEPS, cols), lambda i: (i, 0))
        in_specs += [blk, blk]
        out_specs.append(blk)
        out_shape.append(S((rows, cols), BF))
        args += [a, b]
    return pl.pallas_call(body, name=name, grid=(PAIR_STEPS,), out_shape=tuple(out_shape), in_specs=in_specs,
                          out_specs=tuple(out_specs), compiler_params=_params(("parallel",)))(*args)


def _adam_update(w, g, m, v):
    mn = ADAM_B1 * m + (1.0 - ADAM_B1) * g
    vn = ADAM_B2 * v + (1.0 - ADAM_B2) * (g * g)
    m_hat = mn / (1.0 - ADAM_B1 ** ADAM_STEP)
    v_hat = vn / (1.0 - ADAM_B2 ** ADAM_STEP)
    return -ADAM_LR * (m_hat / (jnp.sqrt(v_hat) + ADAM_EPS) + ADAM_WD * w), mn, vn


def _adamw(quads, name, steps):
    def body(*refs):
        ins, outs = refs[:4 * len(quads)], refs[4 * len(quads):]
        for q in range(len(quads)):
            w_ref, g_ref, m_ref, v_ref = ins[4 * q:4 * q + 4]
            echo_ref, d_ref, mo_ref, vo_ref = outs[4 * q:4 * q + 4]
            gv = g_ref[...]
            echo_ref[...] = gv
            d_ref[...], mo_ref[...], vo_ref[...] = _adam_update(w_ref[...], gv, m_ref[...], v_ref[...])

    in_specs, out_shape, args = [], [], []
    for quad in quads:
        rows, cols = quad[0].shape
        in_specs += [pl.BlockSpec((rows // steps, cols), lambda i: (i, 0))] * 4
        out_shape += [S((rows, cols), F32)] * 4
        args += list(quad)
    res = pl.pallas_call(body, name=name, grid=(steps,), out_shape=tuple(out_shape), in_specs=in_specs,
                         out_specs=tuple(in_specs), compiler_params=_params(("parallel",)))(*args)
    return [res[4 * q:4 * q + 4] for q in range(len(quads))]


def _place():
    x, y, c = lax.axis_index("x"), lax.axis_index("y"), lax.axis_index("c")
    return x, y, c


OTHER_CHIPS = ((1, 0), (0, 1), (1, 1))


def _flip(v, f):
    return 1 - v if f else v


def _remote(src, dst, ssem, rsem, dev):
    return pltpu.make_async_remote_copy(src_ref=src, dst_ref=dst, send_sem=ssem, recv_sem=rsem,
                                        device_id=dev, device_id_type=MESH)


class _Both:
    def __init__(self, hooks):
        self.hooks = hooks
        self.ins = [a for h in hooks for a in h.ins]
        self.out_shape = tuple(o for h in hooks for o in h.out_shape)
        self.scratch = [x for h in hooks for x in h.scratch]

    def _each(self, ins, outs, scr):
        i = o = s = 0
        for h in self.hooks:
            ni, no, ns = len(h.ins), len(h.out_shape), len(h.scratch)
            yield h, ins[i:i + ni], outs[o:o + no], scr[s:s + ns]
            i, o, s = i + ni, o + no, s + ns

    def start(self, ins, outs, scr):
        for h, *refs in self._each(ins, outs, scr):
            h.start(*refs)

    def relay(self, ins, outs, scr):
        for h, *refs in self._each(ins, outs, scr):
            h.relay(*refs)

    def finish(self, ins, outs, scr):
        for h, *refs in self._each(ins, outs, scr):
            h.finish(*refs)

    @property
    def lead(self):
        return max(h.lead for h in self.hooks)

    @property
    def results(self):
        return [h.results for h in self.hooks]

    @results.setter
    def results(self, res):
        for h, _, mine, _ in self._each((), tuple(res), ()):
            h.results = mine


class _GatherHook:
    def __init__(self, shards, kinds, lead=2):
        self.ins, self.kinds, self.lead, n = list(shards), list(kinds), lead, len(shards)
        self.out_shape = tuple(
            S((w.shape[0], 4 * w.shape[1]), BF) if k == "col" else S((4,) + w.shape, BF) for w, k in zip(shards, kinds))
        dma = pltpu.SemaphoreType.DMA
        self.scratch = ([dma((n, 3)) for _ in range(5)] + [dma((n,)), dma((n,))]
                        + [pltpu.VMEM((3, w.shape[0] // 2, w.shape[1]), BF) for w in shards]
                        + [pltpu.VMEM(w.shape, BF) for w in shards])

    def _window(self, outs, i, s, half):
        rows, cols = self.ins[i].shape
        rh = rows // 2
        start = pl.multiple_of(half * rh, 16)
        if self.kinds[i] == "col":
            return outs[i].at[pl.ds(start, rh), pl.ds(pl.multiple_of(s * cols, 128), cols)]
        return outs[i].at[s, pl.ds(start, rh), :]

    def _copies(self, ins, outs, scr):
        n = len(ins)
        ssem, rsem, fssem, frsem, ksem, lsem, osem = scr[:7]
        land, own = scr[7:7 + n], scr[7 + n:7 + 2 * n]
        x, y, c = _place()
        sibling = (x, y, 1 - c)
        loads, stores, sends, forwards, keeps, passed = [], [], [], [], [], []
        for i in range(n):
            rows, cols = self.ins[i].shape
            rh = rows // 2
            mine = (outs[i].at[:, pl.ds(pl.multiple_of((2 * x + y) * cols, 128), cols)] if self.kinds[i] == "col"
                    else outs[i].at[2 * x + y])
            loads.append(functools.partial(pltpu.make_async_copy, ins[i], own[i], lsem.at[i]))
            stores.append(functools.partial(pltpu.make_async_copy, own[i], mine, osem.at[i]))
            src = ins[i].at[pl.ds(pl.multiple_of(c * rh, 16), rh), :]
            for j, (fx, fy) in enumerate(OTHER_CHIPS):
                px, py = _flip(x, fx), _flip(y, fy)
                sends.append(functools.partial(_remote, src, land[i].at[j], ssem.at[i, j], rsem.at[i, j], (px, py, c)))
                here = self._window(outs, i, 2 * px + py, c)
                forwards.append(functools.partial(_remote, land[i].at[j], here, fssem.at[i, j], frsem.at[i, j], sibling))
                keeps.append(functools.partial(pltpu.make_async_copy, land[i].at[j], here, ksem.at[i, j]))
                there = self._window(outs, i, 2 * px + py, 1 - c)
                passed.append(functools.partial(_remote, there, there, fssem.at[i, j], frsem.at[i, j], sibling))
        return loads, stores, sends, forwards, keeps, passed

    def start(self, ins, outs, scr):
        loads, _, sends, _, _, _ = self._copies(ins, outs, scr)
        for make in sends + loads:
            make().start()

    def relay(self, ins, outs, scr):
        loads, stores, sends, forwards, keeps, _ = self._copies(ins, outs, scr)
        for load, store in zip(loads, stores):
            load().wait()
            store().start()
        for send, forward, keep in zip(sends, forwards, keeps):
            send().wait_recv()
            forward().start()
            keep().start()

    def finish(self, ins, outs, scr):
        _, stores, sends, forwards, keeps, passed = self._copies(ins, outs, scr)
        for make in passed:
            make().wait_recv()
        for make in sends + forwards:
            make().wait_send()
        for make in keeps + stores:
            make().wait()


class _ChipsHook:
    def __init__(self, parts, kinds):
        self.ins, self.kinds, n = list(parts), list(kinds), len(parts)
        self.out_shape = tuple(
            S((4, p.shape[0], p.shape[1] // 4), BF) if k == "col" else S(p.shape, BF) for p, k in zip(parts, kinds))
        dma = pltpu.SemaphoreType.DMA
        self.scratch = ([dma((n, 3)), dma((n, 3)), dma((n,)), dma((n,))]
                        + [pltpu.VMEM(o.shape[1:], BF) for o in self.out_shape])

    def _slab(self, ins, i, s):
        _, rows, cols = self.out_shape[i].shape
        if self.kinds[i] == "col":
            return ins[i].at[:, pl.ds(pl.multiple_of(s * cols, 128), cols)]
        return ins[i].at[s]

    def _copies(self, ins, outs, scr):
        ssem, rsem, lsem, osem = scr[:4]
        own = scr[4:]
        x, y, c = _place()
        loads, stores, sends = [], [], []
        for i in range(len(ins)):
            loads.append(functools.partial(pltpu.make_async_copy, self._slab(ins, i, 2 * x + y), own[i], lsem.at[i]))
            stores.append(functools.partial(pltpu.make_async_copy, own[i], outs[i].at[3], osem.at[i]))
            for j, (fx, fy) in enumerate(OTHER_CHIPS):
                px, py = _flip(x, fx), _flip(y, fy)
                sends.append(functools.partial(_remote, self._slab(ins, i, 2 * px + py), outs[i].at[j], ssem.at[i, j],
                                               rsem.at[i, j], (px, py, c)))
        return loads, stores, sends

    def start(self, ins, outs, scr):
        loads, _, sends = self._copies(ins, outs, scr)
        for make in sends + loads:
            make().start()

    lead = 2

    def relay(self, ins, outs, scr):
        loads, stores, _ = self._copies(ins, outs, scr)
        for load, store in zip(loads, stores):
            load().wait()
            store().start()

    def finish(self, ins, outs, scr):
        _, stores, sends = self._copies(ins, outs, scr)
        for make in sends + stores:
            make().wait()


SHARE_STEPS = 2


def _sum_share(slab_list, name, hook=None):
    n = len(slab_list)
    geom = [(sl.shape[1], sl.shape[1] // SHARE_STEPS, sl.shape[2]) for sl in slab_list]

    def body(*refs):
        ins, outs, scr = refs[:n], refs[n:2 * n], refs[2 * n:]
        i = pl.program_id(0)
        x, y, c = _place()
        sibling = (x, y, 1 - c)

        def copies(q, k):
            rh, tr, _ = geom[q]
            stage, lsem, ssem, rsem = scr[4 * q:4 * q + 4]
            dst = outs[q].at[pl.ds(pl.multiple_of(c * rh + k * tr, 8), tr), :]
            return (pltpu.make_async_copy(stage.at[k], dst, lsem.at[k]),
                    _remote(stage.at[k], dst, ssem.at[k], rsem, sibling))

        for q in range(n):
            acc = ins[q][3].astype(F32)
            for k in range(3):
                acc = acc + ins[q][k].astype(F32)
            scr[4 * q][i] = acc
            for cp in copies(q, i):
                cp.start()

        @pl.when(i == SHARE_STEPS - 1)
        def _():
            for q in range(n):
                rh = geom[q][0]
                for k in range(SHARE_STEPS):
                    local, remote = copies(q, k)
                    local.wait()
                    remote.wait_send()
                got = outs[q].at[pl.ds(pl.multiple_of((1 - c) * rh, 8), rh), :]
                _remote(got, got, scr[4 * q + 2].at[0], scr[4 * q + 3], sibling).wait_recv()

    dma = pltpu.SemaphoreType.DMA
    scratch = []
    for rh, tr, cols in geom:
        scratch += [pltpu.VMEM((SHARE_STEPS, tr, cols), F32), dma((SHARE_STEPS,)), dma((SHARE_STEPS,)), dma]
    return _call(
        body, name=name, grid=(SHARE_STEPS,), out_shape=tuple(S((2 * rh, cols), F32) for rh, _, cols in geom),
        in_specs=[pl.BlockSpec((4, tr, cols), lambda i: (0, i, 0)) for _, tr, cols in geom], out_specs=(ANY,) * n,
        scratch_shapes=scratch, sem=("arbitrary",), args=list(slab_list), hook=hook)


class _SmallSumHook:
    def __init__(self, buf):
        self.ins, self.out_shape = [buf], (S(buf.shape, F32),)
        dma = pltpu.SemaphoreType.DMA
        self.scratch = [pltpu.VMEM((8,) + buf.shape, F32), pltpu.VMEM(buf.shape, F32), dma((7,)), dma((7,)), dma]

    def _sends(self, scr):
        slots, _, ssem, rsem, _ = scr
        x, y, c = _place()
        me = 4 * x + 2 * y + c
        for r in range(1, 8):
            px, py, pc = _flip(x, (r >> 2) & 1), _flip(y, (r >> 1) & 1), _flip(c, r & 1)
            yield (functools.partial(_remote, slots.at[me], slots.at[me], ssem.at[r - 1], rsem.at[r - 1], (px, py, pc)),
                   functools.partial(_remote, slots.at[me], slots.at[4 * px + 2 * py + pc], ssem.at[r - 1],
                                     rsem.at[r - 1], (px, py, pc)))

    def start(self, ins, outs, scr):
        slots, _, _, _, lsem = scr
        x, y, c = _place()
        load = pltpu.make_async_copy(ins[0], slots.at[4 * x + 2 * y + c], lsem)
        load.start()
        load.wait()
        for send, _ in self._sends(scr):
            send().start()

    lead = 0

    def relay(self, ins, outs, scr):
        pass

    def finish(self, ins, outs, scr):
        slots, total, _, _, lsem = scr
        for _, arrival in self._sends(scr):
            arrival().wait_recv()
        for send, _ in self._sends(scr):
            send().wait_send()
        acc = slots[0]
        for k in range(1, 8):
            acc = acc + slots[k]
        total[...] = acc
        store = pltpu.make_async_copy(total, outs[0], lsem)
        store.start()
        store.wait()


BIG = ("ffn1_w_gu", "ffn1_w_down", "w_in", "w_out", "xattn_wq", "xattn_wkv", "xattn_wo", "ffn2_w_gu", "ffn2_w_down")
KIND = {"ffn1_w_gu": "col", "ffn1_w_down": "row", "w_in": "col", "w_out": "row", "xattn_wq": "row",
        "xattn_wkv": "col", "xattn_wo": "row", "ffn2_w_gu": "col", "ffn2_w_down": "row"}
WEIGHTS = ("rel_bias", "ffn1_norm", "ffn1_w_gu", "ffn1_w_down", "mix_norm", "w_in", "sinks", "conv_w", "w_out",
           "xattn_norm", "mem_norm", "xattn_wq", "xattn_wkv", "xattn_wo", "ffn2_norm", "ffn2_w_gu", "ffn2_w_down",
           "final_norm")
SMALL_ROWS = 16
GAIN_ROW = {"ffn1_norm": 0, "mix_norm": 1, "xattn_norm": 2, "mem_norm": 3, "ffn2_norm": 4, "final_norm": 5}
CONV_ROW, SINK_ROW, BIAS_ROW, LOSS_ROW = 6, 9, 10, 11
TAP_ROWS = 8


def _rows_block(rows, d):
    buf = jnp.zeros((SMALL_ROWS, d), F32)
    for r, v in rows.items():
        buf = lax.dynamic_update_slice(buf, v if v.ndim == 2 else v.reshape(1, -1), (r, 0))
    return buf


def _adamw_small(gsum, conv_g, bias_g, small):
    names = list(small)

    def grad(k, gsum_ref, conv_ref, bias_ref):
        if k in GAIN_ROW:
            return gsum_ref[GAIN_ROW[k]:GAIN_ROW[k] + 1, :]
        if k == "sinks":
            return gsum_ref[SINK_ROW:SINK_ROW + 1, 0:small[k][0].shape[1]]
        return conv_ref[...] if k == "conv_w" else bias_ref[...]

    def body(gsum_ref, conv_ref, bias_ref, *refs):
        ins, outs = refs[:3 * len(names)], refs[3 * len(names):]
        for q, k in enumerate(names):
            w_ref, m_ref, v_ref = ins[3 * q:3 * q + 3]
            g_ref, d_ref, mo_ref, vo_ref = outs[4 * q:4 * q + 4]
            gv = grad(k, gsum_ref, conv_ref, bias_ref)
            g_ref[...] = gv
            d_ref[...], mo_ref[...], vo_ref[...] = _adam_update(w_ref[...], gv, m_ref[...], v_ref[...])

    res = pl.pallas_call(
        body, name="adamw_small", out_shape=tuple(S(small[k][0].shape, F32) for k in names for _ in range(4)),
        compiler_params=_params())(gsum, conv_g, bias_g, *[a for k in names for a in small[k]])
    return {k: res[4 * q:4 * q + 4] for q, k in enumerate(names)}


def _local_step(x, mem, pos, target, w, gains, bias_table, sinks, conv_w, shards):
    t, d = x.shape
    w = dict(w)
    grads, slabs = {}, {}
    pos_col = pos.reshape(t, 1)
    pos_row = pos.reshape(1, t)
    bias_t = bias_table.reshape(SWA_KV_HEADS, GROUP_ROWS, 2 * BLOCK)

    def gather(names, lead=2):
        return _GatherHook([shards[k] for k in names], [KIND[k] for k in names], lead)

    def gathered(names, hook):
        for k, gw in zip(names, hook.results):
            w[k] = gw if KIND[k] == "col" else gw.reshape(-1, gw.shape[-1])

    def dw(problems, tn, name, hook=None):
        res = _dw_pair([(a, b) for _, a, b in problems], tn, name, KIND[problems[0][0]], hook)
        grads.update(zip((k for k, _, _ in problems), res))

    def pair_sums(names):
        flat = lambda k, v: v.reshape(-1, v.shape[-1]) if KIND[k] == "row" else v
        sums = _add_bf16([(flat(k, grads[k][0]), flat(k, grads[k][1])) for k in names], "pair_sum_" + names[0])
        return {k: (p if KIND[k] == "col" else p.reshape(4, -1, p.shape[-1])) for k, p in zip(names, sums)}

    def chips(names, parts):
        return _ChipsHook([parts[k] for k in names], [KIND[k] for k in names])

    def reduced(names, hook):
        slabs.update(zip(names, hook.results))

    names = ("w_in", "w_out")
    hook = gather(names, lead=1)
    x1, g1, u1 = _ffn_fwd(x, gains["ffn1_norm"], w["ffn1_w_gu"], w["ffn1_w_down"], "ffn1_fwd", hook)
    gathered(names, hook)
    names = ("xattn_wq", "xattn_wkv", "xattn_wo")
    hook = gather(names)
    qkv, e = _mix_proj(x1, gains["mix_norm"], w["w_in"], hook)
    gathered(names, hook)
    names = ("ffn2_w_gu",)
    hook = gather(names)
    attn = _swa_fwd(qkv, pos_col, pos_row, bias_t, sinks, hook)
    gathered(names, hook)
    names = ("ffn2_w_down",)
    hook = gather(names)
    x2, merged = _mix_out_fwd(e, attn, conv_w, w["w_out"], x1, hook)
    gathered(names, hook)
    mh, kv = _mem_kv(mem, gains["mem_norm"], w["xattn_wkv"])
    x3, qx, o = _xattn_fwd(x2, gains["xattn_norm"], w["xattn_wq"], kv, w["xattn_wo"])
    dx4, g2, u2, d_final, loss = _ffn_fwd(x3, gains["ffn2_norm"], w["ffn2_w_gu"], w["ffn2_w_down"], "ffn2_fwd",
                                          head=(gains["final_norm"], target))

    dx3, d_ffn2, dgu2, a2, h4, dyh4 = _ffn_bwd(dx4, x3, gains["ffn2_norm"], g2, u2, w["ffn2_w_gu"], w["ffn2_w_down"],
                                               "ffn2_bwd")
    dw([("ffn2_w_gu", h4, dgu2)], 1408, "dw_ffn2_gu")
    dw([("ffn2_w_down", a2, dyh4)], 512, "dw_ffn2_down")
    parts = pair_sums(("ffn2_w_gu", "ffn2_w_down"))
    hook = chips(("ffn2_w_gu",), parts)
    dx2, d_xattn, dxh3, h3, dqx, grads["xattn_wkv"], d_mem = _xattn_bwd(
        dx3, x2, gains["xattn_norm"], qx, kv, w["xattn_wq"], w["xattn_wo"], mh, mem, w["xattn_wkv"], hook)
    reduced(("ffn2_w_gu",), hook)
    hook = chips(("ffn2_w_down",), parts)
    dattn, dp, dcw, dxh2 = _mix_out_bwd(dx2, e, attn, conv_w, w["w_out"], qkv.shape[1], hook)
    reduced(("ffn2_w_down",), hook)
    dw([("xattn_wo", o, dxh3), ("xattn_wq", h3, dqx), ("w_out", merged, dxh2)], 1024, "dw_wo_wq_wout")
    names = ("xattn_wo", "xattn_wq", "xattn_wkv", "w_out")
    hook = chips(names, pair_sums(names))
    dp, gb, dsk = _swa_bwd(qkv, dattn, pos_col, pos_row, bias_t, sinks, dp, hook)
    reduced(names, hook)
    d_rel_bias, d_sinks = _bias_reduce(gb.reshape(SWA_HEADS, BLOCK, 2 * BLOCK), dsk.reshape(SWA_HEADS, BLOCK, 1))
    dx1, d_mix, h2 = _mix_in_bwd(dp, x1, gains["mix_norm"], w["w_in"], dx2)
    dw([("w_in", h2, dp)], w["w_in"].shape[1] // 4, "dw_win")
    names = ("w_in",)
    hook = chips(names, pair_sums(names))
    dx0, d_ffn1, dgu1, a1, h1, dyh1 = _ffn_bwd(dx1, x, gains["ffn1_norm"], g1, u1, w["ffn1_w_gu"], w["ffn1_w_down"],
                                               "ffn1_bwd")
    dw([("ffn1_w_gu", h1, dgu1)], 1408, "dw_ffn1_gu", hook)
    reduced(names, hook)
    names = ("ffn1_w_gu",)
    hook = chips(names, pair_sums(names))
    dw([("ffn1_w_down", a1, dyh1)], 512, "dw_ffn1_down", hook)
    reduced(names, hook)
    rows = {0: d_ffn1, 1: d_mix, 2: d_xattn, 3: d_mem, 4: d_ffn2, 5: d_final, SINK_ROW: d_sinks,
            BIAS_ROW: d_rel_bias.reshape(1, -1), LOSS_ROW: loss[0, 0:1]}
    rows.update({CONV_ROW + j: dcw[j] for j in range(3)})
    last = ("ffn1_w_down",)
    return dx0, slabs, _rows_block(rows, d), chips(last, pair_sums(last))


def kernel(x, mem, positions, rel_bias, ffn1_norm, ffn1_w_gu, ffn1_w_down, mix_norm, w_in, sinks, conv_w, w_out, xattn_norm, mem_norm, xattn_wq, xattn_wkv, xattn_wo, ffn2_norm, ffn2_w_gu, ffn2_w_down, final_norm, loss_target, m_rel_bias, m_ffn1_norm, m_ffn1_w_gu, m_ffn1_w_down, m_mix_norm, m_w_in, m_sinks, m_conv_w, m_w_out, m_xattn_norm, m_mem_norm, m_xattn_wq, m_xattn_wkv, m_xattn_wo, m_ffn2_norm, m_ffn2_w_gu, m_ffn2_w_down, m_final_norm, v_rel_bias, v_ffn1_norm, v_ffn1_w_gu, v_ffn1_w_down, v_mix_norm, v_w_in, v_sinks, v_conv_w, v_w_out, v_xattn_norm, v_mem_norm, v_xattn_wq, v_xattn_wkv, v_xattn_wo, v_ffn2_norm, v_ffn2_w_gu, v_ffn2_w_down, v_final_norm):
    args = dict(locals())
    wts = {k: args[k] for k in WEIGHTS}
    mom = {k: args["m_" + k] for k in WEIGHTS}
    var = {k: args["v_" + k] for k in WEIGHTS}
    d = x.shape[-1]
    s_me = 2 * lax.axis_index("x") + lax.axis_index("y")

    first = ("ffn1_w_gu", "ffn1_w_down")
    rest = tuple(k for k in BIG if k not in first)
    shards = {k: wts[k][0].astype(BF) for k in first}
    cw_cols = conv_w.shape[-1]
    placed = lax.dynamic_update_slice(jnp.zeros((TAP_ROWS, d), F32), 0.5 * conv_w[0], (0, s_me * cw_cols))
    head = _Both([_GatherHook([shards[k] for k in first], [KIND[k] for k in first]), _SmallSumHook(placed)])
    casts, bias_table = _local_prelude([wts[k][0] for k in rest], rel_bias.T, "gather_ffn1", head)
    shards.update(zip(rest, casts))
    gathered, (conv_sum,) = head.results
    whole = {k: (gw if KIND[k] == "col" else gw.reshape(-1, gw.shape[-1])) for k, gw in zip(first, gathered)}
    conv_whole = conv_sum[0:3]

    gains = {k: wts[k].reshape(1, d) for k in GAIN_ROW}
    dx0, slabs, small, last_chips = _local_step(x[0], mem[0], positions[0], loss_target[0], whole, gains, bias_table,
                                               sinks, conv_whole, shards)

    late = ("ffn1_w_gu", "ffn1_w_down", "ffn2_w_down")
    early = tuple(k for k in BIG if k not in late)
    first_ones = tuple(k for k in BIG if k != "ffn1_w_down")
    tail = _Both([last_chips, _SmallSumHook(small)])
    shard_grads = dict(zip(first_ones, _sum_share([slabs[k] for k in first_ones], "sum_share", tail)))
    (last_slabs,), (small_sum,) = tail.results
    shard_grads["ffn1_w_down"], = _sum_share([last_slabs], "sum_share_ffn1_w_down")
    quad = lambda k: (wts[k][0], shard_grads[k], mom[k][0], var[k][0])
    updates = dict(zip(early, _adamw([quad(k) for k in early], "adamw_early", 8)))
    updates.update(zip(late, _adamw([quad(k) for k in late], "adamw_late", 8)))
    loss = small_sum[LOSS_ROW, 0]

    out_g, out_d, out_m, out_v = {}, {}, {}, {}
    for k in BIG:
        out_g[k], out_d[k], out_m[k], out_v[k] = (a[None] for a in updates[k])

    to_call = lambda k, a: a.T if k == "rel_bias" else a.reshape(1, -1)
    from_call = lambda k, a: a.T if k == "rel_bias" else a.reshape(wts[k].shape)
    small_names = [k for k in WEIGHTS if k not in KIND]
    conv_g = lax.dynamic_slice(small_sum, (CONV_ROW, s_me * cw_cols), (3, cw_cols)).reshape(1, -1)
    bias_g = small_sum[BIAS_ROW, 0:rel_bias.size].reshape(rel_bias.shape[::-1])
    done = _adamw_small(small_sum, conv_g, bias_g,
                        {k: tuple(to_call(k, a) for a in (wts[k], mom[k], var[k])) for k in small_names})
    for k in small_names:
        out_g[k], out_d[k], out_m[k], out_v[k] = (from_call(k, a) for a in done[k])

    return (loss, dx0[None], *[out_g[k] for k in WEIGHTS], *[out_d[k] for k in WEIGHTS],
            *[out_m[k] for k in WEIGHTS], *[out_v[k] for k in WEIGHTS])
```

```python
import functools
import math

import jax
import jax.numpy as jnp
from jax import lax
from jax.experimental import pallas as pl
from jax.experimental.pallas import tpu as pltpu

BF = jnp.bfloat16
F32 = jnp.float32
I32 = jnp.int32
S = jax.ShapeDtypeStruct

EPS = 1e-6
NEG = -1e30
POS_PAD = 1 << 30
WINDOW = 128
BLOCK = 128
HEAD_DIM = 64
SWA_HEADS = 16
SWA_KV_HEADS = 4
SWA_GROUP = SWA_HEADS // SWA_KV_HEADS
MEM_HEADS = 4
REL_BUCKETS = 32
REL_MAX_DIST = 128
ADAM_LR = 0.001
ADAM_B1 = 0.9
ADAM_B2 = 0.999
ADAM_EPS = 1e-08
ADAM_WD = 0.01
ADAM_STEP = 10

V7X_VMEM_LIMIT_BYTES = 56 * 1024 * 1024
MESH = pl.DeviceIdType.MESH
ANY = pl.BlockSpec(memory_space=pl.ANY)
VMEM_SPEC = pl.BlockSpec(memory_space=pltpu.VMEM)
SMEM_SPEC = pl.BlockSpec(memory_space=pltpu.SMEM)


def _params(sem=None):
    return pltpu.CompilerParams(dimension_semantics=sem, vmem_limit_bytes=V7X_VMEM_LIMIT_BYTES)


def _resident(shape):
    nd = len(shape)
    return pl.BlockSpec(shape, lambda *_: (0,) * nd, pipeline_mode=pl.Buffered(1))


def _acc_spec(shape):
    nd = len(shape)
    return pl.BlockSpec(shape, lambda *_: (0,) * nd)


def _call(body, *, name, grid, out_shape, in_specs, out_specs, args, sem, scratch_shapes=(), aliases=None, hook=None):
    aliases = aliases or {}
    if hook is None:
        return pl.pallas_call(body, name=name, grid=grid, out_shape=out_shape, in_specs=in_specs, out_specs=out_specs,
                              scratch_shapes=list(scratch_shapes), input_output_aliases=aliases,
                              compiler_params=_params(sem))(*args)
    n_in, n_out, n_scr = len(in_specs), len(out_shape), len(scratch_shapes)
    h_in, h_out = len(hook.ins), len(hook.out_shape)

    def at_step(pick):
        conds = [pl.program_id(ax) == pick(size) for ax, size in enumerate(grid)]
        return functools.reduce(jnp.logical_and, conds)

    def hosted(*refs):
        k_in, x_in = refs[:n_in], refs[n_in:n_in + h_in]
        o0 = n_in + h_in
        k_out, x_out = refs[o0:o0 + n_out], refs[o0 + n_out:o0 + n_out + h_out]
        s0 = o0 + n_out + h_out
        k_scr, x_scr = refs[s0:s0 + n_scr], refs[s0 + n_scr:]

        @pl.when(at_step(lambda size: 0))
        def _():
            hook.start(x_in, x_out, x_scr)
        body(*k_in, *k_out, *k_scr)
        early = bool(hook.lead) and grid[0] > hook.lead
        if early:
            on_axis0 = pl.program_id(0) == grid[0] - 1 - hook.lead
            rest_first = [pl.program_id(ax) == 0 for ax in range(1, len(grid))]

            @pl.when(functools.reduce(jnp.logical_and, rest_first, on_axis0))
            def _():
                hook.relay(x_in, x_out, x_scr)

        @pl.when(at_step(lambda size: size - 1))
        def _():
            if not early:
                hook.relay(x_in, x_out, x_scr)
            hook.finish(x_in, x_out, x_scr)

    res = pl.pallas_call(
        hosted, name=name, grid=grid, out_shape=tuple(out_shape) + tuple(hook.out_shape),
        in_specs=list(in_specs) + [ANY] * h_in, out_specs=tuple(out_specs) + (ANY,) * h_out,
        scratch_shapes=list(scratch_shapes) + list(hook.scratch), input_output_aliases=aliases,
        compiler_params=_params(("arbitrary",) * len(grid)),
    )(*args, *hook.ins)
    hook.results = res[n_out:]
    return res[:n_out]


def _nn(a, b):
    return jnp.dot(a, b, preferred_element_type=F32)


def _nt(a, b):
    return lax.dot_general(a, b, (((1,), (1,)), ((), ())), preferred_element_type=F32)


def _tn(a, b):
    return lax.dot_general(a, b, (((0,), (0,)), ((), ())), preferred_element_type=F32)


def _sigmoid(v):
    return 1.0 / (1.0 + jnp.exp(-v))


def _rms(x):
    r = lax.rsqrt(jnp.mean(x * x, axis=-1, keepdims=True) + EPS)
    return x * r, r


def _rms_bwd(dh, n, r, g):
    dn = dh * g
    dx = r * (dn - n * jnp.mean(dn * n, axis=-1, keepdims=True))
    return dx, jnp.sum(dh * n, axis=0, keepdims=True)


def _ffn_fwd(x, gn, wgu, wd, name, hook=None, head=None):
    t, d = x.shape
    f = wd.shape[0]
    tm, fc = 256, 1408

    def body(x_ref, gn_ref, wgu_ref, wd_ref, *rest):
        xv = x_ref[...]
        n, _ = _rms(xv)
        h = (n * gn_ref[...]).astype(BF)
        g_ref, u_ref = rest[-2:] if head is None else rest[3:5]
        acc = jnp.zeros((tm, d), F32)
        for c0 in range(0, f, fc):
            g = _nn(h, wgu_ref[:, c0:c0 + fc])
            u = _nn(h, wgu_ref[:, f + c0:f + c0 + fc])
            g_ref[:, c0:c0 + fc] = g.astype(BF)
            u_ref[:, c0:c0 + fc] = u.astype(BF)
            a = (g * _sigmoid(g)) * u
            acc = acc + _nn(a.astype(BF), wd_ref[c0:c0 + fc, :])
        y = xv + 0.5 * acc
        if head is None:
            rest[0][...] = y
            return
        gf_ref, t_ref, dy_ref, _, _, dgf_ref, loss_ref = rest

        @pl.when(pl.program_id(0) == 0)
        def _():
            dgf_ref[...] = jnp.zeros_like(dgf_ref)
            loss_ref[...] = jnp.zeros_like(loss_ref)
        ny, ry = _rms(y)
        gf = gf_ref[...]
        err = ny * gf - t_ref[...]
        loss_ref[...] += 0.5 * jnp.sum(jnp.sum(err * err, axis=-1, keepdims=True) / d, axis=0, keepdims=True)
        dy, dgf = _rms_bwd(err / d, ny, ry, gf)
        dy_ref[...] = dy
        dgf_ref[...] += dgf

    row = pl.BlockSpec((tm, d), lambda i: (i, 0))
    frow = pl.BlockSpec((tm, f), lambda i: (i, 0))
    in_specs = [row, _resident((1, d)), _resident(wgu.shape), _resident(wd.shape)]
    out_shape = (S((t, d), F32), S((t, f), BF), S((t, f), BF))
    out_specs = (row, frow, frow)
    args = (x, gn, wgu, wd)
    if head is not None:
        in_specs += [_resident((1, d)), row]
        out_shape += (S((1, d), F32), S((1, 128), F32))
        out_specs += (_acc_spec((1, d)), _acc_spec((1, 128)))
        args += tuple(head)
    return _call(body, name=name, grid=(t // tm,), out_shape=out_shape, in_specs=in_specs, out_specs=out_specs,
                 sem=("parallel",) if head is None else ("arbitrary",), args=args, hook=hook)


def _mix_proj(x, gn, w_in, hook=None):
    t, d = x.shape
    tm = 256
    nqkv = 1536
    ne = w_in.shape[1] - nqkv

    def body(x_ref, gn_ref, w_ref, qkv_ref, e_ref):
        n, _ = _rms(x_ref[...])
        h = (n * gn_ref[...]).astype(BF)
        qkv_ref[...] = _nn(h, w_ref[:, 0:nqkv]).astype(BF)
        for c0 in range(0, ne, 1024):
            e_ref[:, c0:c0 + 1024] = _nn(h, w_ref[:, nqkv + c0:nqkv + c0 + 1024]).astype(BF)

    return _call(
        body, name="mix_proj", grid=(t // tm,),
        out_shape=(S((t, nqkv), BF), S((t, ne), BF)),
        in_specs=[pl.BlockSpec((tm, d), lambda i: (i, 0)), _resident((1, d)), _resident(w_in.shape)],
        out_specs=(pl.BlockSpec((tm, nqkv), lambda i: (i, 0)), pl.BlockSpec((tm, ne), lambda i: (i, 0))),
        sem=("parallel",), args=(x, gn, w_in), hook=hook)


def _t5_bucket(rel):
    n = jnp.maximum(rel, 0)
    max_exact = REL_BUCKETS // 2
    nf = jnp.maximum(n, 1).astype(F32)
    large = max_exact + (jnp.log(nf / max_exact) / math.log(REL_MAX_DIST / max_exact)
                         * (REL_BUCKETS - max_exact)).astype(I32)
    large = jnp.minimum(large, REL_BUCKETS - 1)
    return jnp.where(n < max_exact, n, large)


def _block_rel():
    i = lax.broadcasted_iota(I32, (BLOCK, 2 * BLOCK), 0)
    j = lax.broadcasted_iota(I32, (BLOCK, 2 * BLOCK), 1)
    return i + BLOCK - j


def _bias_table(rb_ref, o_ref):
    bucket = _t5_bucket(_block_rel())
    for h in range(SWA_HEADS):
        acc = jnp.zeros((BLOCK, 2 * BLOCK), F32)
        for b in range(REL_BUCKETS):
            acc = jnp.where(bucket == b, rb_ref[h, b], acc)
        o_ref[h] = acc


GROUP_ROWS = SWA_GROUP * BLOCK


def _swa_visible(b, pq_ref, pkp_ref, pkc_ref):
    pk = jnp.concatenate([pkp_ref[...], pkc_ref[...]], axis=1)
    col = lax.broadcasted_iota(I32, (1, 2 * BLOCK), 1)
    pk = jnp.where(jnp.logical_and(b == 0, col < BLOCK), POS_PAD, pk)
    rel = jnp.concatenate([pq_ref[...]] * SWA_GROUP, axis=0) - pk
    return jnp.logical_and(rel >= 0, rel < WINDOW)


def _swa_poison(b, pq_ref, pkp_ref, pkc_ref):
    pk = jnp.concatenate([pkp_ref[...], pkc_ref[...]], axis=1)
    col = lax.broadcasted_iota(I32, (1, 2 * BLOCK), 1)
    pk = jnp.where(jnp.logical_and(b == 0, col < BLOCK), POS_PAD, pk)
    rel = pq_ref[...] - pk
    off = jnp.logical_and(jnp.logical_and(rel >= 0, rel < WINDOW), rel != _block_rel())
    return jnp.where(jnp.max(off.astype(F32)) > 0.0, jnp.nan, 0.0)


def _group_heads(ref, hk):
    h0 = hk * SWA_GROUP
    return jnp.concatenate([ref[:, (h0 + g) * HEAD_DIM:(h0 + g + 1) * HEAD_DIM] for g in range(SWA_GROUP)], axis=0)


def _group_sinks(sink_ref, hk):
    row = lax.broadcasted_iota(I32, (GROUP_ROWS, 1), 0)
    col = jnp.zeros((GROUP_ROWS, 1), F32) + sink_ref[0, hk * SWA_GROUP]
    for g in range(1, SWA_GROUP):
        col = jnp.where(row >= g * BLOCK, sink_ref[0, hk * SWA_GROUP + g], col)
    return col


def _swa_probs(qg, kh, vis, bias, sink):
    s = _nt(qg, kh) * (HEAD_DIM ** -0.5)
    s = jnp.where(vis, s + bias, NEG)
    m = jnp.maximum(jnp.max(s, axis=-1, keepdims=True), sink)
    p = jnp.exp(s - m)
    ps = jnp.exp(sink - m)
    inv = 1.0 / (jnp.sum(p, axis=-1, keepdims=True) + ps)
    return p * inv, ps * inv


def _swa_fwd(qkv, pos_col, pos_row, bias_t, sinks, hook=None):
    t = qkv.shape[0]
    nb = t // BLOCK
    qw = SWA_HEADS * HEAD_DIM
    kw = SWA_KV_HEADS * HEAD_DIM

    def body(q_ref, kp_ref, kc_ref, vp_ref, vc_ref, pq_ref, pkp_ref, pkc_ref, bias_ref, sink_ref, o_ref):
        b = pl.program_id(0)
        vis = _swa_visible(b, pq_ref, pkp_ref, pkc_ref)
        poison = _swa_poison(b, pq_ref, pkp_ref, pkc_ref)
        k2 = jnp.concatenate([kp_ref[...], kc_ref[...]], axis=0)
        v2 = jnp.concatenate([vp_ref[...], vc_ref[...]], axis=0)
        for hk in range(SWA_KV_HEADS):
            kh = k2[:, hk * HEAD_DIM:(hk + 1) * HEAD_DIM]
            vh = v2[:, hk * HEAD_DIM:(hk + 1) * HEAD_DIM]
            pn, _ = _swa_probs(_group_heads(q_ref, hk), kh, vis, bias_ref[hk], _group_sinks(sink_ref, hk))
            o = _nn(pn.astype(BF), vh) + poison
            for g in range(SWA_GROUP):
                h = hk * SWA_GROUP + g
                o_ref[:, h * HEAD_DIM:(h + 1) * HEAD_DIM] = o[g * BLOCK:(g + 1) * BLOCK]

    prev = lambda b: jnp.maximum(b - 1, 0)
    return _call(
        body, name="swa_fwd", grid=(nb,), out_shape=(S((t, qw), F32),),
        in_specs=[
            pl.BlockSpec((BLOCK, qw), lambda b: (b, 0)),
            pl.BlockSpec((BLOCK, kw), lambda b: (prev(b), qw // kw)),
            pl.BlockSpec((BLOCK, kw), lambda b: (b, qw // kw)),
            pl.BlockSpec((BLOCK, kw), lambda b: (prev(b), qw // kw + 1)),
            pl.BlockSpec((BLOCK, kw), lambda b: (b, qw // kw + 1)),
            pl.BlockSpec((BLOCK, 1), lambda b: (b, 0)),
            pl.BlockSpec((1, BLOCK), lambda b: (0, prev(b))),
            pl.BlockSpec((1, BLOCK), lambda b: (0, b)),
            _resident(bias_t.shape),
            SMEM_SPEC,
        ],
        out_specs=(pl.BlockSpec((BLOCK, qw), lambda b: (b, 0)),),
        sem=("parallel",), args=(qkv, qkv, qkv, qkv, qkv, pos_col, pos_row, pos_row, bias_t, sinks), hook=hook)[0]


HALO = 16


def _conv_taps(z, zh, first):
    tm = z.shape[0]
    zh = jnp.where(first, 0.0, zh)
    row = lax.broadcasted_iota(I32, (tm, 1), 0)
    z1 = jnp.where(row == 0, zh[HALO - 1:HALO, :], pltpu.roll(z, 1, 0))
    z2 = jnp.where(row == 0, zh[HALO - 2:HALO - 1, :], jnp.where(row == 1, zh[HALO - 1:HALO, :], pltpu.roll(z, 2, 0)))
    return z1, z2


def _mix_out_fwd(e, attn, conv_w, w_out, x, hook=None):
    t, d = x.shape
    tm = 256
    hb = tm // HALO
    f32 = lambda ref: ref[...].astype(F32)

    def body(c_ref, b_ref, u_ref, ga_ref, gc_ref, ch_ref, uh_ref, attn_ref, cw_ref, w_ref, x_ref, xo_ref, mg_ref):
        i = pl.program_id(0)
        z = f32(c_ref) * f32(u_ref)
        z1, z2 = _conv_taps(z, f32(ch_ref) * f32(uh_ref), i == 0)
        s = cw_ref[0:1, :] * z2 + cw_ref[1:2, :] * z1 + cw_ref[2:3, :] * z
        conv = f32(b_ref) * s
        merged = (_sigmoid(f32(ga_ref)) * attn_ref[...] + _sigmoid(f32(gc_ref)) * conv).astype(BF)
        mg_ref[...] = merged
        xo_ref[...] = x_ref[...] + _nn(merged, w_ref[...])

    ecol = lambda cb: pl.BlockSpec((tm, d), lambda i: (i, cb))
    halo = lambda cb: pl.BlockSpec((HALO, d), lambda i: (jnp.maximum(i * hb - 1, 0), cb))
    row = pl.BlockSpec((tm, d), lambda i: (i, 0))
    return _call(
        body, name="mix_out_fwd", grid=(t // tm,),
        out_shape=(S((t, d), F32), S((t, d), BF)),
        in_specs=[ecol(0), ecol(1), ecol(2), ecol(3), ecol(4), halo(0), halo(2), row,
                  _resident(conv_w.shape), _resident(w_out.shape), row],
        out_specs=(row, row),
        sem=("parallel",), args=(e, e, e, e, e, e, e, attn, conv_w, w_out, x), hook=hook)


def _mem_kv(mem, gm, wkv):
    m, d = mem.shape

    def body(mem_ref, gm_ref, w_ref, mh_ref, kv_ref):
        n, _ = _rms(mem_ref[...])
        mh = (n * gm_ref[...]).astype(BF)
        mh_ref[...] = mh
        kv_ref[...] = _nn(mh, w_ref[...]).astype(BF)

    return pl.pallas_call(
        body, name="mem_kv", out_shape=(S((m, d), BF), S((m, wkv.shape[1]), BF)),
        compiler_params=_params(),
    )(mem, gm, wkv)


def _xattn_probs(qh, kh):
    s = _nt(qh, kh) * (kh.shape[1] ** -0.5)
    p = jnp.exp(s - jnp.max(s, axis=-1, keepdims=True))
    return p * (1.0 / jnp.sum(p, axis=-1, keepdims=True))


def _xattn_fwd(x, gn, wq, kv, wo):
    t, d = x.shape
    tm = 256
    hd = d // MEM_HEADS

    def body(x_ref, gn_ref, wq_ref, kv_ref, wo_ref, xo_ref, q_ref, o_ref):
        xv = x_ref[...]
        n, _ = _rms(xv)
        q = _nn((n * gn_ref[...]).astype(BF), wq_ref[...]).astype(BF)
        q_ref[...] = q
        outs = []
        for hh in range(MEM_HEADS):
            p = _xattn_probs(q[:, hh * hd:(hh + 1) * hd], kv_ref[:, hh * hd:(hh + 1) * hd])
            outs.append(_nn(p.astype(BF), kv_ref[:, d + hh * hd:d + (hh + 1) * hd]))
        o = jnp.concatenate(outs, axis=1).astype(BF)
        o_ref[...] = o
        xo_ref[...] = xv + _nn(o, wo_ref[...])

    row = pl.BlockSpec((tm, d), lambda i: (i, 0))
    return pl.pallas_call(
        body, name="xattn_fwd", grid=(t // tm,),
        out_shape=(S((t, d), F32), S((t, d), BF), S((t, d), BF)),
        in_specs=[row, _resident((1, d)), _resident(wq.shape), _resident(kv.shape), _resident(wo.shape)],
        out_specs=(row, row, row),
        compiler_params=_params(("parallel",)),
    )(x, gn, wq, kv, wo)


def _ffn_bwd(dxo, x, gn, g, u, wgu, wd, name):
    t, d = x.shape
    f = wd.shape[0]
    tm, fc = 256, 1408

    def body(dxo_ref, x_ref, gn_ref, g_ref, u_ref, wgu_ref, wd_ref, dx_ref, dgn_ref, dgu_ref, a_ref, h_ref, dyh_ref):
        @pl.when(pl.program_id(0) == 0)
        def _():
            dgn_ref[...] = jnp.zeros_like(dgn_ref)
        dxov = dxo_ref[...]
        dyh = (0.5 * dxov).astype(BF)
        dyh_ref[...] = dyh
        n, r = _rms(x_ref[...])
        gnv = gn_ref[...]
        h_ref[...] = (n * gnv).astype(BF)
        dh = jnp.zeros((tm, d), F32)
        for c0 in range(0, f, fc):
            gv = g_ref[:, c0:c0 + fc].astype(F32)
            uv = u_ref[:, c0:c0 + fc].astype(F32)
            da = _nt(dyh, wd_ref[c0:c0 + fc, :])
            sg = _sigmoid(gv)
            silu = gv * sg
            a_ref[:, c0:c0 + fc] = (silu * uv).astype(BF)
            dg = (da * uv * (sg * (1.0 + gv * (1.0 - sg)))).astype(BF)
            du = (da * silu).astype(BF)
            dgu_ref[:, c0:c0 + fc] = dg
            dgu_ref[:, f + c0:f + c0 + fc] = du
            dh = dh + _nt(dg, wgu_ref[:, c0:c0 + fc]) + _nt(du, wgu_ref[:, f + c0:f + c0 + fc])
        dx, dgn = _rms_bwd(dh, n, r, gnv)
        dx_ref[...] = dxov + dx
        dgn_ref[...] += dgn

    row = pl.BlockSpec((tm, d), lambda i: (i, 0))
    frow = pl.BlockSpec((tm, f), lambda i: (i, 0))
    return _call(
        body, name=name, grid=(t // tm,),
        out_shape=(S((t, d), F32), S((1, d), F32), S((t, 2 * f), BF), S((t, f), BF), S((t, d), BF), S((t, d), BF)),
        in_specs=[row, row, _resident((1, d)), frow, frow, _resident(wgu.shape), _resident(wd.shape)],
        out_specs=(row, _acc_spec((1, d)), pl.BlockSpec((tm, 2 * f), lambda i: (i, 0)), frow, row, row),
        sem=("arbitrary",), args=(dxo, x, gn, g, u, wgu, wd))


def _dw_pair(pairs, tn, name, kind, hook=None):
    npairs = len(pairs)
    t, ka = pairs[0][0].shape
    nb = pairs[0][1].shape[1]
    tt = 2048 if npairs == 1 else 1024
    nt, nj = t // tt, nb // tn
    col = kind == "col"
    rh = ka // 2 if col else ka // 8
    tile = (rh, tn) if col else (4, rh, tn)
    half = (rh, nb) if col else (4, rh, nb)
    lead = (slice(None),) * (len(tile) - 1)

    def body(*refs):
        ins, mines, sibs = refs[:2 * npairs], refs[2 * npairs:3 * npairs], refs[3 * npairs:4 * npairs]
        acc_ref, stage, ssem, rsem = refs[4 * npairs:]
        j, k = pl.program_id(0), pl.program_id(1)
        x, y, c = _place()
        sibling = (x, y, 1 - c)

        def send(p, slot, jj):
            dst = sibs[p].at[lead + (pl.ds(pl.multiple_of(jj * tn, 128), tn),)]
            return _remote(stage.at[slot], dst, ssem.at[slot], rsem.at[p], sibling)

        def rows(s, whose):
            return acc_ref[pl.ds(pl.multiple_of(s * 2 * rh + whose * rh, 16), rh), :].astype(BF)

        def step(p):
            a_ref, b_ref, mine_ref = ins[2 * p], ins[2 * p + 1], mines[p]

            @pl.when(k == 0)
            def _():
                acc_ref[...] = jnp.zeros_like(acc_ref)
            acc_ref[...] += _tn(a_ref[...], b_ref[...])

            @pl.when(k == nt - 1)
            def _():
                slot = j % 2

                @pl.when(j >= 2)
                def _():
                    send(p, slot, 0).wait_send()
                if col:
                    mine_ref[...] = rows(0, c)
                    stage[slot] = rows(0, 1 - c)
                else:
                    for s in range(4):
                        mine_ref[s] = rows(s, c)
                        stage[slot, s] = rows(s, 1 - c)
                send(p, slot, j - p * nj).start()

        for p in range(npairs):
            pl.when(j // nj == p)(functools.partial(step, p))

        @pl.when(jnp.logical_and(j == npairs * nj - 1, k == nt - 1))
        def _():
            for jj in range(max(npairs * nj - 2, 0), npairs * nj):
                send(0, jj % 2, 0).wait_send()
            for p in range(npairs):
                _remote(sibs[p], sibs[p], ssem.at[0], rsem.at[p], sibling).wait_recv()

    in_specs, args = [], []
    for p, (a, b) in enumerate(pairs):
        on = lambda j, p=p: j // nj == p
        in_specs += [pl.BlockSpec((tt, ka), lambda j, k, on=on: (jnp.where(on(j), k, 0), 0)),
                     pl.BlockSpec((tt, tn), lambda j, k, on=on, p=p: (jnp.where(on(j), k, 0), jnp.clip(j - p * nj, 0, nj - 1)))]
        args += [a, b]
    mine_spec = lambda p: pl.BlockSpec(tile, (lambda j, k: (0, jnp.clip(j - p * nj, 0, nj - 1))) if col
                                       else (lambda j, k: (0, 0, jnp.clip(j - p * nj, 0, nj - 1))))
    res = _call(
        body, name=name, grid=(npairs * nj, nt), out_shape=(S(half, BF),) * (2 * npairs),
        in_specs=in_specs, out_specs=tuple(mine_spec(p) for p in range(npairs)) + (ANY,) * npairs,
        scratch_shapes=[pltpu.VMEM((ka, tn), F32), pltpu.VMEM((2,) + tile, BF), pltpu.SemaphoreType.DMA((2,)),
                        pltpu.SemaphoreType.DMA((npairs,))],
        sem=("arbitrary", "arbitrary"), args=args, hook=hook)
    return [(res[p], res[npairs + p]) for p in range(npairs)]


def _xattn_bwd(dxo, x, gn, q, kv, wq, wo, mh, mem, wkv, hook=None):
    t, d = x.shape
    tm = 256
    hd = d // MEM_HEADS
    nkv = kv.shape[0]
    rows, cols = wkv.shape
    rh = rows // 2

    def body(dxo_ref, x_ref, gn_ref, q_ref, kv_ref, wq_ref, wo_ref, mh_ref, mem_ref, wkv_ref,
             dx_ref, dgn_ref, dxh_ref, h_ref, dq_ref, mine_ref, sib_ref, dgm_ref, dkv_ref, whole, ssem, rsem):
        @pl.when(pl.program_id(0) == 0)
        def _():
            dgn_ref[...] = jnp.zeros_like(dgn_ref)
            dkv_ref[...] = jnp.zeros_like(dkv_ref)
        dxov = dxo_ref[...]
        dxh = dxov.astype(BF)
        dxh_ref[...] = dxh
        do = _nt(dxh, wo_ref[...]).astype(BF)
        dqs = []
        for hh in range(MEM_HEADS):
            lo, hi = hh * hd, (hh + 1) * hd
            qh = q_ref[:, lo:hi]
            kh = kv_ref[:, lo:hi]
            vh = kv_ref[:, d + lo:d + hi]
            doh = do[:, lo:hi]
            p = _xattn_probs(qh, kh)
            dp = _nt(doh, vh)
            ds = (p * (dp - jnp.sum(p * dp, axis=-1, keepdims=True)) * (hd ** -0.5)).astype(BF)
            dqs.append(_nn(ds, kh))
            dkv_ref[:, lo:hi] += _tn(ds, qh)
            dkv_ref[:, d + lo:d + hi] += _tn(p.astype(BF), doh)
        dq = jnp.concatenate(dqs, axis=1).astype(BF)
        dq_ref[...] = dq
        n, r = _rms(x_ref[...])
        gnv = gn_ref[...]
        h_ref[...] = (n * gnv).astype(BF)
        dx, dgn = _rms_bwd(_nt(dq, wq_ref[...]), n, r, gnv)
        dx_ref[...] = dxov + dx
        dgn_ref[...] += dgn

        @pl.when(pl.program_id(0) == pl.num_programs(0) - 1)
        def _():
            dkvb = dkv_ref[...].astype(BF)
            dmh = _nt(dkvb, wkv_ref[...])
            nm, _ = _rms(mem_ref[...])
            px, py, c = _place()
            whole[...] = _tn(mh_ref[...], dkvb).astype(BF)
            cp = _remote(whole.at[pl.ds(pl.multiple_of((1 - c) * rh, 16), rh), :], sib_ref, ssem, rsem, (px, py, 1 - c))
            cp.start()
            mine_ref[...] = whole[pl.ds(pl.multiple_of(c * rh, 16), rh), :]
            dgm_ref[...] = jnp.sum(dmh * nm, axis=0, keepdims=True)
            cp.wait()

    row = pl.BlockSpec((tm, d), lambda i: (i, 0))
    dx, dgn, dxh, h, dq, mine, sib, dgm = _call(
        body, name="xattn_bwd", grid=(t // tm,),
        out_shape=(S((t, d), F32), S((1, d), F32), S((t, d), BF), S((t, d), BF), S((t, d), BF),
                   S((rh, cols), BF), S((rh, cols), BF), S((1, d), F32)),
        in_specs=[row, row, _resident((1, d)), row, _resident(kv.shape), _resident(wq.shape), _resident(wo.shape),
                  _resident(mh.shape), _resident(mem.shape), _resident(wkv.shape)],
        out_specs=(row, _acc_spec((1, d)), row, row, row, _acc_spec((rh, cols)), ANY, _acc_spec((1, d))),
        scratch_shapes=[pltpu.VMEM((nkv, 2 * d), F32), pltpu.VMEM((rows, cols), BF),
                        pltpu.SemaphoreType.DMA, pltpu.SemaphoreType.DMA],
        sem=("arbitrary",), args=(dxo, x, gn, q, kv, wq, wo, mh, mem, wkv), hook=hook)
    return dx, dgn, dxh, h, dq, (mine, sib), dgm


def _mix_out_bwd(dxo, e, attn, conv_w, w_out, nqkv, hook=None):
    t, d = attn.shape
    tm = 256
    nt = t // tm
    f32 = lambda ref: ref[...].astype(F32)

    def body(dxo_ref, dxn_ref, c_ref, b_ref, u_ref, ga_ref, gc_ref, ch_ref, uh_ref, bn_ref, gcn_ref,
             attn_ref, cw_ref, w_ref, dattn_ref, de_ref, dcw_ref, dxh_ref):
        i = pl.program_id(0)

        @pl.when(i == 0)
        def _():
            dcw_ref[...] = jnp.zeros_like(dcw_ref)
        dxh = dxo_ref[...].astype(BF)
        dxh_ref[...] = dxh
        w = w_ref[...]
        dm = _nt(dxh, w)
        dmn = _nt(dxn_ref[...].astype(BF), w)
        cv, bv, uv = f32(c_ref), f32(b_ref), f32(u_ref)
        sga = _sigmoid(f32(ga_ref))
        sgc = _sigmoid(f32(gc_ref))
        z = cv * uv
        z1, z2 = _conv_taps(z, f32(ch_ref) * f32(uh_ref), i == 0)
        w0, w1, w2 = cw_ref[0:1, :], cw_ref[1:2, :], cw_ref[2:3, :]
        s = w0 * z2 + w1 * z1 + w2 * z
        av = attn_ref[...]
        dattn_ref[...] = (dm * sga).astype(BF)
        dconv = dm * sgc
        ds = dconv * bv
        dsn = jnp.where(i == nt - 1, 0.0, dmn * _sigmoid(gcn_ref[0:8, :].astype(F32)) * bn_ref[0:8, :].astype(F32))
        row = lax.broadcasted_iota(I32, (tm, 1), 0)
        dsp1 = jnp.where(row == tm - 1, dsn[0:1, :], pltpu.roll(ds, tm - 1, 0))
        dsp2 = jnp.where(row == tm - 2, dsn[0:1, :], jnp.where(row == tm - 1, dsn[1:2, :], pltpu.roll(ds, tm - 2, 0)))
        dz = w2 * ds + w1 * dsp1 + w0 * dsp2
        de_ref[:, nqkv:nqkv + d] = (dz * uv).astype(BF)
        de_ref[:, nqkv + d:nqkv + 2 * d] = (dconv * s).astype(BF)
        de_ref[:, nqkv + 2 * d:nqkv + 3 * d] = (dz * cv).astype(BF)
        de_ref[:, nqkv + 3 * d:nqkv + 4 * d] = (dm * av * sga * (1.0 - sga)).astype(BF)
        de_ref[:, nqkv + 4 * d:nqkv + 5 * d] = (dm * (bv * s) * sgc * (1.0 - sgc)).astype(BF)
        dcw_ref[0:1, :] += jnp.sum(ds * z2, axis=0, keepdims=True)
        dcw_ref[1:2, :] += jnp.sum(ds * z1, axis=0, keepdims=True)
        dcw_ref[2:3, :] += jnp.sum(ds * z, axis=0, keepdims=True)

    ecol = lambda cb: pl.BlockSpec((tm, d), lambda i: (i, cb))
    prev = lambda cb: pl.BlockSpec((HALO, d), lambda i: (jnp.maximum(i * (tm // HALO) - 1, 0), cb))
    nxt = lambda rows, cb: pl.BlockSpec((rows, d), lambda i: (jnp.minimum((i + 1) * (tm // rows), t // rows - 1), cb))
    row = pl.BlockSpec((tm, d), lambda i: (i, 0))
    return _call(
        body, name="mix_out_bwd", grid=(nt,),
        out_shape=(S((t, d), BF), S((t, nqkv + 5 * d), BF), S((8, d), F32), S((t, d), BF)),
        in_specs=[row, nxt(8, 0), ecol(0), ecol(1), ecol(2), ecol(3), ecol(4), prev(0), prev(2), nxt(HALO, 1), nxt(HALO, 4),
                  row, _resident(conv_w.shape), _resident(w_out.shape)],
        out_specs=(row, pl.BlockSpec((tm, nqkv + 5 * d), lambda i: (i, 0)), _acc_spec((8, d)), row),
        sem=("arbitrary",), args=(dxo, dxo, e, e, e, e, e, e, e, e, e, attn, conv_w, w_out), hook=hook)


def _swa_bwd(qkv, dattn, pos_col, pos_row, bias_t, sinks, dp, hook=None):
    t = qkv.shape[0]
    nb = t // BLOCK
    qw = SWA_HEADS * HEAD_DIM
    kw = SWA_KV_HEADS * HEAD_DIM

    def body(q_ref, kp_ref, kc_ref, vp_ref, vc_ref, do_ref, pq_ref, pkp_ref, pkc_ref, bias_ref, sink_ref,
             dp_in, dp_ref, gb_ref, dsk_ref, carry_ref, dq_ref):
        b = pl.program_id(0)

        @pl.when(b == 0)
        def _():
            gb_ref[...] = jnp.zeros_like(gb_ref)
            dsk_ref[...] = jnp.zeros_like(dsk_ref)
            carry_ref[...] = jnp.zeros_like(carry_ref)
            dq_ref[...] = jnp.zeros_like(dq_ref)
        dp_ref[:, 0:qw] = dq_ref[...]

        @pl.when(b < nb)
        def _():
            vis = _swa_visible(b, pq_ref, pkp_ref, pkc_ref)
            k2 = jnp.concatenate([kp_ref[...], kc_ref[...]], axis=0)
            v2 = jnp.concatenate([vp_ref[...], vc_ref[...]], axis=0)
            for hk in range(SWA_KV_HEADS):
                lo, hi = hk * HEAD_DIM, (hk + 1) * HEAD_DIM
                kh = k2[:, lo:hi]
                vh = v2[:, lo:hi]
                qg = _group_heads(q_ref, hk)
                dog = _group_heads(do_ref, hk)
                pn, psn = _swa_probs(qg, kh, vis, bias_ref[hk], _group_sinks(sink_ref, hk))
                dp = _nt(dog, vh)
                delta = jnp.sum(pn * dp, axis=-1, keepdims=True)
                ds = pn * (dp - delta)
                gb_ref[hk] += ds
                dsk_ref[hk] += -psn * delta
                dsb = (ds * (HEAD_DIM ** -0.5)).astype(BF)
                dqg = _nn(dsb, kh).astype(BF)
                for g in range(SWA_GROUP):
                    h = hk * SWA_GROUP + g
                    dq_ref[:, h * HEAD_DIM:(h + 1) * HEAD_DIM] = dqg[g * BLOCK:(g + 1) * BLOCK]
                dk = _tn(dsb, qg)
                dv = _tn(pn.astype(BF), dog)
                dp_ref[:, qw + lo:qw + hi] = (carry_ref[:, lo:hi] + dk[0:BLOCK]).astype(BF)
                dp_ref[:, qw + kw + lo:qw + kw + hi] = (carry_ref[:, kw + lo:kw + hi] + dv[0:BLOCK]).astype(BF)
                carry_ref[:, lo:hi] = dk[BLOCK:2 * BLOCK]
                carry_ref[:, kw + lo:kw + hi] = dv[BLOCK:2 * BLOCK]

        @pl.when(b == nb)
        def _():
            dp_ref[:, qw:qw + 2 * kw] = carry_ref[...].astype(BF)

    cur = lambda b: jnp.minimum(b, nb - 1)
    prev = lambda b: jnp.maximum(cur(b) - 1, 0)
    return _call(
        body, name="swa_bwd", grid=(nb + 1,),
        out_shape=(S(dp.shape, BF), S((SWA_KV_HEADS, GROUP_ROWS, 2 * BLOCK), F32), S((SWA_KV_HEADS, GROUP_ROWS, 1), F32)),
        in_specs=[
            pl.BlockSpec((BLOCK, qw), lambda b: (cur(b), 0)),
            pl.BlockSpec((BLOCK, kw), lambda b: (prev(b), qw // kw)),
            pl.BlockSpec((BLOCK, kw), lambda b: (cur(b), qw // kw)),
            pl.BlockSpec((BLOCK, kw), lambda b: (prev(b), qw // kw + 1)),
            pl.BlockSpec((BLOCK, kw), lambda b: (cur(b), qw // kw + 1)),
            pl.BlockSpec((BLOCK, qw), lambda b: (cur(b), 0)),
            pl.BlockSpec((BLOCK, 1), lambda b: (cur(b), 0)),
            pl.BlockSpec((1, BLOCK), lambda b: (0, prev(b))),
            pl.BlockSpec((1, BLOCK), lambda b: (0, cur(b))),
            _resident(bias_t.shape),
            SMEM_SPEC,
            ANY,
        ],
        out_specs=(
            pl.BlockSpec((BLOCK, qw + 2 * kw), lambda b: (jnp.maximum(b - 1, 0), 0)),
            _acc_spec((SWA_KV_HEADS, GROUP_ROWS, 2 * BLOCK)),
            _acc_spec((SWA_KV_HEADS, GROUP_ROWS, 1)),
        ),
        scratch_shapes=[pltpu.VMEM((BLOCK, 2 * kw), F32), pltpu.VMEM((BLOCK, qw), BF)],
        aliases={11: 0}, sem=("arbitrary",),
        args=(qkv, qkv, qkv, qkv, qkv, dattn, pos_col, pos_row, pos_row, bias_t, sinks, dp), hook=hook)


def _bias_reduce(gb, dsk):
    def body(gb_ref, dsk_ref, drb_ref, dsink_ref):
        bucket = _t5_bucket(_block_rel())
        for b in range(REL_BUCKETS):
            mask = bucket == b
            for h in range(SWA_HEADS):
                drb_ref[h, b] = jnp.sum(jnp.where(mask, gb_ref[h], 0.0))
        for h in range(SWA_HEADS):
            dsink_ref[0, h] = jnp.sum(dsk_ref[h])

    return pl.pallas_call(
        body, name="bias_reduce", out_shape=(S((SWA_HEADS, REL_BUCKETS), F32), S((1, SWA_HEADS), F32)),
        in_specs=[VMEM_SPEC, VMEM_SPEC], out_specs=(SMEM_SPEC, SMEM_SPEC),
    )(gb, dsk)


def _mix_in_bwd(dp, x, gn, w_in, dxo, hook=None):
    t, d = x.shape
    tm = 256
    npr = dp.shape[1]

    def body(dp_ref, x_ref, gn_ref, w_ref, dxo_ref, dx_ref, dgn_ref, h_ref):
        @pl.when(pl.program_id(0) == 0)
        def _():
            dgn_ref[...] = jnp.zeros_like(dgn_ref)
        dh = jnp.zeros((tm, d), F32)
        for c0 in range(0, npr, 1024):
            c1 = min(c0 + 1024, npr)
            dh = dh + _nt(dp_ref[:, c0:c1], w_ref[:, c0:c1])
        n, r = _rms(x_ref[...])
        gnv = gn_ref[...]
        h_ref[...] = (n * gnv).astype(BF)
        dx, dgn = _rms_bwd(dh, n, r, gnv)
        dx_ref[...] = dxo_ref[...] + dx
        dgn_ref[...] += dgn

    row = pl.BlockSpec((tm, d), lambda i: (i, 0))
    return _call(
        body, name="mix_in_bwd", grid=(t // tm,),
        out_shape=(S((t, d), F32), S((1, d), F32), S((t, d), BF)),
        in_specs=[pl.BlockSpec((tm, npr), lambda i: (i, 0)), row, _resident((1, d)), _resident(w_in.shape), row],
        out_specs=(row, _acc_spec((1, d)), row),
        sem=("arbitrary",), args=(dp, x, gn, w_in, dxo), hook=hook)


CAST_STEPS = 4


def _local_prelude(arrays, rel_bias_t, name, hook=None):
    n = len(arrays)

    def body(*refs):
        rb_ref, table_ref = refs[n], refs[-1]
        for src, dst in zip(refs[:n], refs[n + 1:-1]):
            dst[...] = src[...].astype(BF)

        @pl.when(pl.program_id(0) == 0)
        def _():
            _bias_table(rb_ref, table_ref)

    blocks = [pl.BlockSpec((a.shape[0] // CAST_STEPS, a.shape[1]), lambda i: (i, 0)) for a in arrays]
    table = (SWA_HEADS, BLOCK, 2 * BLOCK)
    res = _call(body, name=name, grid=(CAST_STEPS,), out_shape=tuple(S(a.shape, BF) for a in arrays) + (S(table, F32),),
                in_specs=blocks + [SMEM_SPEC], out_specs=tuple(blocks) + (_acc_spec(table),), sem=("arbitrary",),
                args=list(arrays) + [rel_bias_t], hook=hook)
    return res[:n], res[n]


PAIR_STEPS = 4


def _add_bf16(pairs, name):
    def body(*refs):
        ins, outs = refs[:2 * len(pairs)], refs[2 * len(pairs):]
        for q, o_ref in enumerate(outs):
            o_ref[...] = (ins[2 * q][...].astype(F32) + ins[2 * q + 1][...].astype(F32)).astype(BF)

    in_specs, out_specs, out_shape, args = [], [], [], []
    for a, b in pairs:
        rows, cols = a.shape
        blk = pl.BlockSpec((rows // PAIR_STEPS, cols), lambda i: (i, 0))
        in_specs += [blk, blk]
        out_specs.append(blk)
        out_shape.append(S((rows, cols), BF))
        args += [a, b]
    return pl.pallas_call(body, name=name, grid=(PAIR_STEPS,), out_shape=tuple(out_shape), in_specs=in_specs,
                          out_specs=tuple(out_specs), compiler_params=_params(("parallel",)))(*args)


def _adam_update(w, g, m, v):
    mn = ADAM_B1 * m + (1.0 - ADAM_B1) * g
    vn = ADAM_B2 * v + (1.0 - ADAM_B2) * (g * g)
    m_hat = mn / (1.0 - ADAM_B1 ** ADAM_STEP)
    v_hat = vn / (1.0 - ADAM_B2 ** ADAM_STEP)
    return -ADAM_LR * (m_hat / (jnp.sqrt(v_hat) + ADAM_EPS) + ADAM_WD * w), mn, vn


def _adamw(quads, name, steps):
    def body(*refs):
        ins, outs = refs[:4 * len(quads)], refs[4 * len(quads):]
        for q in range(len(quads)):
            w_ref, g_ref, m_ref, v_ref = ins[4 * q:4 * q + 4]
            echo_ref, d_ref, mo_ref, vo_ref = outs[4 * q:4 * q + 4]
            gv = g_ref[...]
            echo_ref[...] = gv
            d_ref[...], mo_ref[...], vo_ref[...] = _adam_update(w_ref[...], gv, m_ref[...], v_ref[...])

    in_specs, out_shape, args = [], [], []
    for quad in quads:
        rows, cols = quad[0].shape
        in_specs += [pl.BlockSpec((rows // steps, cols), lambda i: (i, 0))] * 4
        out_shape += [S((rows, cols), F32)] * 4
        args += list(quad)
    res = pl.pallas_call(body, name=name, grid=(steps,), out_shape=tuple(out_shape), in_specs=in_specs,
                         out_specs=tuple(in_specs), compiler_params=_params(("parallel",)))(*args)
    return [res[4 * q:4 * q + 4] for q in range(len(quads))]


def _place():
    x, y, c = lax.axis_index("x"), lax.axis_index("y"), lax.axis_index("c")
    return x, y, c


OTHER_CHIPS = ((1, 0), (0, 1), (1, 1))


def _flip(v, f):
    return 1 - v if f else v


def _remote(src, dst, ssem, rsem, dev):
    return pltpu.make_async_remote_copy(src_ref=src, dst_ref=dst, send_sem=ssem, recv_sem=rsem,
                                        device_id=dev, device_id_type=MESH)


class _Both:
    def __init__(self, hooks):
        self.hooks = hooks
        self.ins = [a for h in hooks for a in h.ins]
        self.out_shape = tuple(o for h in hooks for o in h.out_shape)
        self.scratch = [x for h in hooks for x in h.scratch]

    def _each(self, ins, outs, scr):
        i = o = s = 0
        for h in self.hooks:
            ni, no, ns = len(h.ins), len(h.out_shape), len(h.scratch)
            yield h, ins[i:i + ni], outs[o:o + no], scr[s:s + ns]
            i, o, s = i + ni, o + no, s + ns

    def start(self, ins, outs, scr):
        for h, *refs in self._each(ins, outs, scr):
            h.start(*refs)

    def relay(self, ins, outs, scr):
        for h, *refs in self._each(ins, outs, scr):
            h.relay(*refs)

    def finish(self, ins, outs, scr):
        for h, *refs in self._each(ins, outs, scr):
            h.finish(*refs)

    @property
    def lead(self):
        return max(h.lead for h in self.hooks)

    @property
    def results(self):
        return [h.results for h in self.hooks]

    @results.setter
    def results(self, res):
        for h, _, mine, _ in self._each((), tuple(res), ()):
            h.results = mine


class _GatherHook:
    def __init__(self, shards, kinds, lead=2):
        self.ins, self.kinds, self.lead, n = list(shards), list(kinds), lead, len(shards)
        self.out_shape = tuple(
            S((w.shape[0], 4 * w.shape[1]), BF) if k == "col" else S((4,) + w.shape, BF) for w, k in zip(shards, kinds))
        dma = pltpu.SemaphoreType.DMA
        self.scratch = ([dma((n, 3)) for _ in range(5)] + [dma((n,)), dma((n,))]
                        + [pltpu.VMEM((3, w.shape[0] // 2, w.shape[1]), BF) for w in shards]
                        + [pltpu.VMEM(w.shape, BF) for w in shards])

    def _window(self, outs, i, s, half):
        rows, cols = self.ins[i].shape
        rh = rows // 2
        start = pl.multiple_of(half * rh, 16)
        if self.kinds[i] == "col":
            return outs[i].at[pl.ds(start, rh), pl.ds(pl.multiple_of(s * cols, 128), cols)]
        return outs[i].at[s, pl.ds(start, rh), :]

    def _copies(self, ins, outs, scr):
        n = len(ins)
        ssem, rsem, fssem, frsem, ksem, lsem, osem = scr[:7]
        land, own = scr[7:7 + n], scr[7 + n:7 + 2 * n]
        x, y, c = _place()
        sibling = (x, y, 1 - c)
        loads, stores, sends, forwards, keeps, passed = [], [], [], [], [], []
        for i in range(n):
            rows, cols = self.ins[i].shape
            rh = rows // 2
            mine = (outs[i].at[:, pl.ds(pl.multiple_of((2 * x + y) * cols, 128), cols)] if self.kinds[i] == "col"
                    else outs[i].at[2 * x + y])
            loads.append(functools.partial(pltpu.make_async_copy, ins[i], own[i], lsem.at[i]))
            stores.append(functools.partial(pltpu.make_async_copy, own[i], mine, osem.at[i]))
            src = ins[i].at[pl.ds(pl.multiple_of(c * rh, 16), rh), :]
            for j, (fx, fy) in enumerate(OTHER_CHIPS):
                px, py = _flip(x, fx), _flip(y, fy)
                sends.append(functools.partial(_remote, src, land[i].at[j], ssem.at[i, j], rsem.at[i, j], (px, py, c)))
                here = self._window(outs, i, 2 * px + py, c)
                forwards.append(functools.partial(_remote, land[i].at[j], here, fssem.at[i, j], frsem.at[i, j], sibling))
                keeps.append(functools.partial(pltpu.make_async_copy, land[i].at[j], here, ksem.at[i, j]))
                there = self._window(outs, i, 2 * px + py, 1 - c)
                passed.append(functools.partial(_remote, there, there, fssem.at[i, j], frsem.at[i, j], sibling))
        return loads, stores, sends, forwards, keeps, passed

    def start(self, ins, outs, scr):
        loads, _, sends, _, _, _ = self._copies(ins, outs, scr)
        for make in sends + loads:
            make().start()

    def relay(self, ins, outs, scr):
        loads, stores, sends, forwards, keeps, _ = self._copies(ins, outs, scr)
        for load, store in zip(loads, stores):
            load().wait()
            store().start()
        for send, forward, keep in zip(sends, forwards, keeps):
            send().wait_recv()
            forward().start()
            keep().start()

    def finish(self, ins, outs, scr):
        _, stores, sends, forwards, keeps, passed = self._copies(ins, outs, scr)
        for make in passed:
            make().wait_recv()
        for make in sends + forwards:
            make().wait_send()
        for make in keeps + stores:
            make().wait()


class _ChipsHook:
    def __init__(self, parts, kinds):
        self.ins, self.kinds, n = list(parts), list(kinds), len(parts)
        self.out_shape = tuple(
            S((4, p.shape[0], p.shape[1] // 4), BF) if k == "col" else S(p.shape, BF) for p, k in zip(parts, kinds))
        dma = pltpu.SemaphoreType.DMA
        self.scratch = ([dma((n, 3)), dma((n, 3)), dma((n,)), dma((n,))]
                        + [pltpu.VMEM(o.shape[1:], BF) for o in self.out_shape])

    def _slab(self, ins, i, s):
        _, rows, cols = self.out_shape[i].shape
        if self.kinds[i] == "col":
            return ins[i].at[:, pl.ds(pl.multiple_of(s * cols, 128), cols)]
        return ins[i].at[s]

    def _copies(self, ins, outs, scr):
        ssem, rsem, lsem, osem = scr[:4]
        own = scr[4:]
        x, y, c = _place()
        loads, stores, sends = [], [], []
        for i in range(len(ins)):
            loads.append(functools.partial(pltpu.make_async_copy, self._slab(ins, i, 2 * x + y), own[i], lsem.at[i]))
            stores.append(functools.partial(pltpu.make_async_copy, own[i], outs[i].at[3], osem.at[i]))
            for j, (fx, fy) in enumerate(OTHER_CHIPS):
                px, py = _flip(x, fx), _flip(y, fy)
                sends.append(functools.partial(_remote, self._slab(ins, i, 2 * px + py), outs[i].at[j], ssem.at[i, j],
                                               rsem.at[i, j], (px, py, c)))
        return loads, stores, sends

    def start(self, ins, outs, scr):
        loads, _, sends = self._copies(ins, outs, scr)
        for make in sends + loads:
            make().start()

    lead = 2

    def relay(self, ins, outs, scr):
        loads, stores, _ = self._copies(ins, outs, scr)
        for load, store in zip(loads, stores):
            load().wait()
            store().start()

    def finish(self, ins, outs, scr):
        _, stores, sends = self._copies(ins, outs, scr)
        for make in sends + stores:
            make().wait()


SHARE_STEPS = 2


def _sum_share(slab_list, name, hook=None):
    n = len(slab_list)
    geom = [(sl.shape[1], sl.shape[1] // SHARE_STEPS, sl.shape[2]) for sl in slab_list]

    def body(*refs):
        ins, outs, scr = refs[:n], refs[n:2 * n], refs[2 * n:]
        i = pl.program_id(0)
        x, y, c = _place()
        sibling = (x, y, 1 - c)

        def copies(q, k):
            rh, tr, _ = geom[q]
            stage, lsem, ssem, rsem = scr[4 * q:4 * q + 4]
            dst = outs[q].at[pl.ds(pl.multiple_of(c * rh + k * tr, 8), tr), :]
            return (pltpu.make_async_copy(stage.at[k], dst, lsem.at[k]),
                    _remote(stage.at[k], dst, ssem.at[k], rsem, sibling))

        for q in range(n):
            acc = ins[q][3].astype(F32)
            for k in range(3):
                acc = acc + ins[q][k].astype(F32)
            scr[4 * q][i] = acc
            for cp in copies(q, i):
                cp.start()

        @pl.when(i == SHARE_STEPS - 1)
        def _():
            for q in range(n):
                rh = geom[q][0]
                for k in range(SHARE_STEPS):
                    local, remote = copies(q, k)
                    local.wait()
                    remote.wait_send()
                got = outs[q].at[pl.ds(pl.multiple_of((1 - c) * rh, 8), rh), :]
                _remote(got, got, scr[4 * q + 2].at[0], scr[4 * q + 3], sibling).wait_recv()

    dma = pltpu.SemaphoreType.DMA
    scratch = []
    for rh, tr, cols in geom:
        scratch += [pltpu.VMEM((SHARE_STEPS, tr, cols), F32), dma((SHARE_STEPS,)), dma((SHARE_STEPS,)), dma]
    return _call(
        body, name=name, grid=(SHARE_STEPS,), out_shape=tuple(S((2 * rh, cols), F32) for rh, _, cols in geom),
        in_specs=[pl.BlockSpec((4, tr, cols), lambda i: (0, i, 0)) for _, tr, cols in geom], out_specs=(ANY,) * n,
        scratch_shapes=scratch, sem=("arbitrary",), args=list(slab_list), hook=hook)


class _SmallSumHook:
    def __init__(self, buf):
        self.ins, self.out_shape = [buf], (S(buf.shape, F32),)
        dma = pltpu.SemaphoreType.DMA
        self.scratch = [pltpu.VMEM((8,) + buf.shape, F32), pltpu.VMEM(buf.shape, F32), dma((7,)), dma((7,)), dma]

    def _sends(self, scr):
        slots, _, ssem, rsem, _ = scr
        x, y, c = _place()
        me = 4 * x + 2 * y + c
        for r in range(1, 8):
            px, py, pc = _flip(x, (r >> 2) & 1), _flip(y, (r >> 1) & 1), _flip(c, r & 1)
            yield (functools.partial(_remote, slots.at[me], slots.at[me], ssem.at[r - 1], rsem.at[r - 1], (px, py, pc)),
                   functools.partial(_remote, slots.at[me], slots.at[4 * px + 2 * py + pc], ssem.at[r - 1],
                                     rsem.at[r - 1], (px, py, pc)))

    def start(self, ins, outs, scr):
        slots, _, _, _, lsem = scr
        x, y, c = _place()
        load = pltpu.make_async_copy(ins[0], slots.at[4 * x + 2 * y + c], lsem)
        load.start()
        load.wait()
        for send, _ in self._sends(scr):
            send().start()

    lead = 0

    def relay(self, ins, outs, scr):
        pass

    def finish(self, ins, outs, scr):
        slots, total, _, _, lsem = scr
        for _, arrival in self._sends(scr):
            arrival().wait_recv()
        for send, _ in self._sends(scr):
            send().wait_send()
        acc = slots[0]
        for k in range(1, 8):
            acc = acc + slots[k]
        total[...] = acc
        store = pltpu.make_async_copy(total, outs[0], lsem)
        store.start()
        store.wait()


BIG = ("ffn1_w_gu", "ffn1_w_down", "w_in", "w_out", "xattn_wq", "xattn_wkv", "xattn_wo", "ffn2_w_gu", "ffn2_w_down")
KIND = {"ffn1_w_gu": "col", "ffn1_w_down": "row", "w_in": "col", "w_out": "row", "xattn_wq": "row",
        "xattn_wkv": "col", "xattn_wo": "row", "ffn2_w_gu": "col", "ffn2_w_down": "row"}
WEIGHTS = ("rel_bias", "ffn1_norm", "ffn1_w_gu", "ffn1_w_down", "mix_norm", "w_in", "sinks", "conv_w", "w_out",
           "xattn_norm", "mem_norm", "xattn_wq", "xattn_wkv", "xattn_wo", "ffn2_norm", "ffn2_w_gu", "ffn2_w_down",
           "final_norm")
SMALL_ROWS = 16
GAIN_ROW = {"ffn1_norm": 0, "mix_norm": 1, "xattn_norm": 2, "mem_norm": 3, "ffn2_norm": 4, "final_norm": 5}
CONV_ROW, SINK_ROW, BIAS_ROW, LOSS_ROW = 6, 9, 10, 11
TAP_ROWS = 8


def _rows_block(rows, d):
    buf = jnp.zeros((SMALL_ROWS, d), F32)
    for r, v in rows.items():
        buf = lax.dynamic_update_slice(buf, v if v.ndim == 2 else v.reshape(1, -1), (r, 0))
    return buf


def _adamw_small(gsum, conv_g, bias_g, small):
    names = list(small)

    def grad(k, gsum_ref, conv_ref, bias_ref):
        if k in GAIN_ROW:
            return gsum_ref[GAIN_ROW[k]:GAIN_ROW[k] + 1, :]
        if k == "sinks":
            return gsum_ref[SINK_ROW:SINK_ROW + 1, 0:small[k][0].shape[1]]
        return conv_ref[...] if k == "conv_w" else bias_ref[...]

    def body(gsum_ref, conv_ref, bias_ref, *refs):
        ins, outs = refs[:3 * len(names)], refs[3 * len(names):]
        for q, k in enumerate(names):
            w_ref, m_ref, v_ref = ins[3 * q:3 * q + 3]
            g_ref, d_ref, mo_ref, vo_ref = outs[4 * q:4 * q + 4]
            gv = grad(k, gsum_ref, conv_ref, bias_ref)
            g_ref[...] = gv
            d_ref[...], mo_ref[...], vo_ref[...] = _adam_update(w_ref[...], gv, m_ref[...], v_ref[...])

    res = pl.pallas_call(
        body, name="adamw_small", out_shape=tuple(S(small[k][0].shape, F32) for k in names for _ in range(4)),
        compiler_params=_params())(gsum, conv_g, bias_g, *[a for k in names for a in small[k]])
    return {k: res[4 * q:4 * q + 4] for q, k in enumerate(names)}


def _local_step(x, mem, pos, target, w, gains, bias_table, sinks, conv_w, shards):
    t, d = x.shape
    w = dict(w)
    grads, slabs = {}, {}
    pos_col = pos.reshape(t, 1)
    pos_row = pos.reshape(1, t)
    bias_t = bias_table.reshape(SWA_KV_HEADS, GROUP_ROWS, 2 * BLOCK)

    def gather(names, lead=2):
        return _GatherHook([shards[k] for k in names], [KIND[k] for k in names], lead)

    def gathered(names, hook):
        for k, gw in zip(names, hook.results):
            w[k] = gw if KIND[k] == "col" else gw.reshape(-1, gw.shape[-1])

    def dw(problems, tn, name, hook=None):
        res = _dw_pair([(a, b) for _, a, b in problems], tn, name, KIND[problems[0][0]], hook)
        grads.update(zip((k for k, _, _ in problems), res))

    def pair_sums(names):
        flat = lambda k, v: v.reshape(-1, v.shape[-1]) if KIND[k] == "row" else v
        sums = _add_bf16([(flat(k, grads[k][0]), flat(k, grads[k][1])) for k in names], "pair_sum_" + names[0])
        return {k: (p if KIND[k] == "col" else p.reshape(4, -1, p.shape[-1])) for k, p in zip(names, sums)}

    def chips(names, parts):
        return _ChipsHook([parts[k] for k in names], [KIND[k] for k in names])

    def reduced(names, hook):
        slabs.update(zip(names, hook.results))

    names = ("w_in", "w_out")
    hook = gather(names, lead=1)
    x1, g1, u1 = _ffn_fwd(x, gains["ffn1_norm"], w["ffn1_w_gu"], w["ffn1_w_down"], "ffn1_fwd", hook)
    gathered(names, hook)
    names = ("xattn_wq", "xattn_wkv", "xattn_wo")
    hook = gather(names)
    qkv, e = _mix_proj(x1, gains["mix_norm"], w["w_in"], hook)
    gathered(names, hook)
    names = ("ffn2_w_gu",)
    hook = gather(names)
    attn = _swa_fwd(qkv, pos_col, pos_row, bias_t, sinks, hook)
    gathered(names, hook)
    names = ("ffn2_w_down",)
    hook = gather(names)
    x2, merged = _mix_out_fwd(e, attn, conv_w, w["w_out"], x1, hook)
    gathered(names, hook)
    mh, kv = _mem_kv(mem, gains["mem_norm"], w["xattn_wkv"])
    x3, qx, o = _xattn_fwd(x2, gains["xattn_norm"], w["xattn_wq"], kv, w["xattn_wo"])
    dx4, g2, u2, d_final, loss = _ffn_fwd(x3, gains["ffn2_norm"], w["ffn2_w_gu"], w["ffn2_w_down"], "ffn2_fwd",
                                          head=(gains["final_norm"], target))

    dx3, d_ffn2, dgu2, a2, h4, dyh4 = _ffn_bwd(dx4, x3, gains["ffn2_norm"], g2, u2, w["ffn2_w_gu"], w["ffn2_w_down"],
                                               "ffn2_bwd")
    dw([("ffn2_w_gu", h4, dgu2)], 1408, "dw_ffn2_gu")
    dw([("ffn2_w_down", a2, dyh4)], 512, "dw_ffn2_down")
    parts = pair_sums(("ffn2_w_gu", "ffn2_w_down"))
    hook = chips(("ffn2_w_gu",), parts)
    dx2, d_xattn, dxh3, h3, dqx, grads["xattn_wkv"], d_mem = _xattn_bwd(
        dx3, x2, gains["xattn_norm"], qx, kv, w["xattn_wq"], w["xattn_wo"], mh, mem, w["xattn_wkv"], hook)
    reduced(("ffn2_w_gu",), hook)
    hook = chips(("ffn2_w_down",), parts)
    dattn, dp, dcw, dxh2 = _mix_out_bwd(dx2, e, attn, conv_w, w["w_out"], qkv.shape[1], hook)
    reduced(("ffn2_w_down",), hook)
    dw([("xattn_wo", o, dxh3), ("xattn_wq", h3, dqx), ("w_out", merged, dxh2)], 1024, "dw_wo_wq_wout")
    names = ("xattn_wo", "xattn_wq", "xattn_wkv", "w_out")
    hook = chips(names, pair_sums(names))
    dp, gb, dsk = _swa_bwd(qkv, dattn, pos_col, pos_row, bias_t, sinks, dp)
    d_rel_bias, d_sinks = _bias_reduce(gb.reshape(SWA_HEADS, BLOCK, 2 * BLOCK), dsk.reshape(SWA_HEADS, BLOCK, 1))
    dx1, d_mix, h2 = _mix_in_bwd(dp, x1, gains["mix_norm"], w["w_in"], dx2, hook)
    reduced(names, hook)
    dw([("w_in", h2, dp)], w["w_in"].shape[1] // 4, "dw_win")
    names = ("w_in",)
    hook = chips(names, pair_sums(names))
    dx0, d_ffn1, dgu1, a1, h1, dyh1 = _ffn_bwd(dx1, x, gains["ffn1_norm"], g1, u1, w["ffn1_w_gu"], w["ffn1_w_down"],
                                               "ffn1_bwd")
    dw([("ffn1_w_gu", h1, dgu1)], 1408, "dw_ffn1_gu", hook)
    reduced(names, hook)
    names = ("ffn1_w_gu",)
    hook = chips(names, pair_sums(names))
    dw([("ffn1_w_down", a1, dyh1)], 512, "dw_ffn1_down", hook)
    reduced(names, hook)
    rows = {0: d_ffn1, 1: d_mix, 2: d_xattn, 3: d_mem, 4: d_ffn2, 5: d_final, SINK_ROW: d_sinks,
            BIAS_ROW: d_rel_bias.reshape(1, -1), LOSS_ROW: loss[0, 0:1]}
    rows.update({CONV_ROW + j: dcw[j] for j in range(3)})
    last = ("ffn1_w_down",)
    return dx0, slabs, _rows_block(rows, d), chips(last, pair_sums(last))


def kernel(x, mem, positions, rel_bias, ffn1_norm, ffn1_w_gu, ffn1_w_down, mix_norm, w_in, sinks, conv_w, w_out, xattn_norm, mem_norm, xattn_wq, xattn_wkv, xattn_wo, ffn2_norm, ffn2_w_gu, ffn2_w_down, final_norm, loss_target, m_rel_bias, m_ffn1_norm, m_ffn1_w_gu, m_ffn1_w_down, m_mix_norm, m_w_in, m_sinks, m_conv_w, m_w_out, m_xattn_norm, m_mem_norm, m_xattn_wq, m_xattn_wkv, m_xattn_wo, m_ffn2_norm, m_ffn2_w_gu, m_ffn2_w_down, m_final_norm, v_rel_bias, v_ffn1_norm, v_ffn1_w_gu, v_ffn1_w_down, v_mix_norm, v_w_in, v_sinks, v_conv_w, v_w_out, v_xattn_norm, v_mem_norm, v_xattn_wq, v_xattn_wkv, v_xattn_wo, v_ffn2_norm, v_ffn2_w_gu, v_ffn2_w_down, v_final_norm):
    args = dict(locals())
    wts = {k: args[k] for k in WEIGHTS}
    mom = {k: args["m_" + k] for k in WEIGHTS}
    var = {k: args["v_" + k] for k in WEIGHTS}
    d = x.shape[-1]
    s_me = 2 * lax.axis_index("x") + lax.axis_index("y")

    first = ("ffn1_w_gu", "ffn1_w_down")
    rest = tuple(k for k in BIG if k not in first)
    shards = {k: wts[k][0].astype(BF) for k in first}
    cw_cols = conv_w.shape[-1]
    placed = lax.dynamic_update_slice(jnp.zeros((TAP_ROWS, d), F32), 0.5 * conv_w[0], (0, s_me * cw_cols))
    head = _Both([_GatherHook([shards[k] for k in first], [KIND[k] for k in first]), _SmallSumHook(placed)])
    casts, bias_table = _local_prelude([wts[k][0] for k in rest], rel_bias.T, "gather_ffn1", head)
    shards.update(zip(rest, casts))
    gathered, (conv_sum,) = head.results
    whole = {k: (gw if KIND[k] == "col" else gw.reshape(-1, gw.shape[-1])) for k, gw in zip(first, gathered)}
    conv_whole = conv_sum[0:3]

    gains = {k: wts[k].reshape(1, d) for k in GAIN_ROW}
    dx0, slabs, small, last_chips = _local_step(x[0], mem[0], positions[0], loss_target[0], whole, gains, bias_table,
                                               sinks, conv_whole, shards)

    late = ("ffn1_w_gu", "ffn1_w_down", "ffn2_w_down")
    early = tuple(k for k in BIG if k not in late)
    first_ones = tuple(k for k in BIG if k != "ffn1_w_down")
    tail = _Both([last_chips, _SmallSumHook(small)])
    shard_grads = dict(zip(first_ones, _sum_share([slabs[k] for k in first_ones], "sum_share", tail)))
    (last_slabs,), (small_sum,) = tail.results
    shard_grads["ffn1_w_down"], = _sum_share([last_slabs], "sum_share_ffn1_w_down")
    quad = lambda k: (wts[k][0], shard_grads[k], mom[k][0], var[k][0])
    updates = dict(zip(early, _adamw([quad(k) for k in early], "adamw_early", 8)))
    updates.update(zip(late, _adamw([quad(k) for k in late], "adamw_late", 8)))
    loss = small_sum[LOSS_ROW, 0]

    out_g, out_d, out_m, out_v = {}, {}, {}, {}
    for k in BIG:
        out_g[k], out_d[k], out_m[k], out_v[k] = (a[None] for a in updates[k])

    to_call = lambda k, a: a.T if k == "rel_bias" else a.reshape(1, -1)
    from_call = lambda k, a: a.T if k == "rel_bias" else a.reshape(wts[k].shape)
    small_names = [k for k in WEIGHTS if k not in KIND]
    conv_g = lax.dynamic_slice(small_sum, (CONV_ROW, s_me * cw_cols), (3, cw_cols)).reshape(1, -1)
    bias_g = small_sum[BIAS_ROW, 0:rel_bias.size].reshape(rel_bias.shape[::-1])
    done = _adamw_small(small_sum, conv_g, bias_g,
                        {k: tuple(to_call(k, a) for a in (wts[k], mom[k], var[k])) for k in small_names})
    for k in small_names:
        out_g[k], out_d[k], out_m[k], out_v[k] = (from_call(k, a) for a in done[k])

    return (loss, dx0[None], *[out_g[k] for k in WEIGHTS], *[out_d[k] for k in WEIGHTS],
            *[out_m[k] for k in WEIGHTS], *[out_v[k] for k in WEIGHTS])
```
